```python
import math
import jax, jax.numpy as jnp
from jax import lax
import numpy as np

D_MODEL = 2048
BATCH = 8
SEQ = 2048
DEPTH = 1

GRID_W = 64
CTX_LEN = 256
HEAD_DIM = 128
N_HEADS = 8
N_KV_HEADS = 2
GROUP = N_HEADS // N_KV_HEADS
WINDOW = 128
BLOCK = 128
NBAND = -(-WINDOW // BLOCK)
BAND = (2 * NBAND + 1) * BLOCK
ROT_HALF = HEAD_DIM // 2
ROPE_BASE = 10000.0
SSM_WIDTH = D_MODEL // 4
SSM_GROUP = 16
SSM_GROUPS = SSM_WIDTH // SSM_GROUP
SSM_STATE = 64
D_FF = 4 * D_MODEL
Q_W = N_HEADS * HEAD_DIM
KV_W = N_KV_HEADS * HEAD_DIM
IN_COLS = Q_W + 2 * KV_W + SSM_WIDTH + 2 * D_MODEL
SPLIT_AT = (Q_W, Q_W + KV_W, Q_W + 2 * KV_W, Q_W + 2 * KV_W + SSM_WIDTH,
            Q_W + 2 * KV_W + SSM_WIDTH + D_MODEL)
ALPHA = (2.0 * DEPTH) ** 0.25
BETA = (8.0 * DEPTH) ** -0.25
LN_EPS = 1e-6
NEG_INF = -1e30

kernel_name = 'hybrid_dit_swa_s5_block'


def layer_norm(x):
    xf = x.astype(jnp.float32)
    mu = jnp.mean(xf, -1, keepdims=True)
    var = jnp.mean(jnp.square(xf - mu), -1, keepdims=True)
    return ((xf - mu) * lax.rsqrt(var + LN_EPS)).astype(x.dtype)


def post_norm(x, g, b):
    return layer_norm(x) * g + b


def axial_rope_tables(n_tokens):
    rows = n_tokens // GRID_W
    row = jnp.repeat(jnp.arange(rows), GRID_W)
    col = jnp.tile(jnp.arange(GRID_W), rows)
    n_freq = ROT_HALF // 2
    freqs = ROPE_BASE ** (-jnp.arange(n_freq, dtype=jnp.float32) / n_freq)
    ang_r = row.astype(jnp.float32)[:, None] * freqs
    ang_c = col.astype(jnp.float32)[:, None] * freqs
    ang = jnp.concatenate([ang_r, ang_r, ang_c, ang_c], -1)
    return jnp.cos(ang), jnp.sin(ang)


def rotate_axial(x):
    h = ROT_HALF // 2
    a, b = x[..., :ROT_HALF], x[..., ROT_HALF:]
    return jnp.concatenate([-a[..., h:], a[..., :h], -b[..., h:], b[..., :h]], -1)


def apply_rope(x, cos, sin):
    cos = cos[:, None, :].astype(x.dtype)
    sin = sin[:, None, :].astype(x.dtype)
    return x * cos + rotate_axial(x) * sin


def windowed_gqa_latent(q, k, v, k_ctx, v_ctx, sink):
    bsz, n_tok = q.shape[0], q.shape[1]
    n_ctx = k_ctx.shape[1]
    nb = n_tok // BLOCK
    scale = HEAD_DIM ** -0.5
    qb = q.reshape(bsz, nb, BLOCK, N_KV_HEADS, GROUP, HEAD_DIM)

    def band(t):
        tp = jnp.pad(t, ((0, 0), (NBAND * BLOCK, NBAND * BLOCK), (0, 0), (0, 0)))
        tp = tp.reshape(bsz, nb + 2 * NBAND, BLOCK, N_KV_HEADS, HEAD_DIM)
        return jnp.concatenate([tp[:, j:j + nb] for j in range(2 * NBAND + 1)], axis=2)

    kb, vb = band(k), band(v)
    q_pos = jnp.arange(n_tok).reshape(nb, BLOCK)
    k_pos = (jnp.arange(nb)[:, None] - NBAND) * BLOCK + jnp.arange(BAND)[None, :]
    valid = ((jnp.abs(q_pos[:, :, None] - k_pos[:, None, :]) <= WINDOW)
             & (k_pos >= 0)[:, None, :] & (k_pos < n_tok)[:, None, :])
    s_loc = jnp.einsum('bnqkgd,bnjkd->bnkgqj', qb, kb, preferred_element_type=jnp.float32) * scale
    s_loc = jnp.where(valid[None, :, None, None], s_loc, NEG_INF)
    s_ctx = jnp.einsum('bnqkgd,bckd->bnkgqc', qb, k_ctx, preferred_element_type=jnp.float32) * scale
    s_sink = jnp.broadcast_to(sink.astype(jnp.float32).reshape(1, 1, N_KV_HEADS, GROUP, 1, 1),
                              s_loc.shape[:-1] + (1,))
    p = jax.nn.softmax(jnp.concatenate([s_loc, s_ctx, s_sink], -1), axis=-1).astype(v.dtype)
    out = (jnp.einsum('bnkgqj,bnjkd->bnqkgd', p[..., :BAND], vb)
           + jnp.einsum('bnkgqc,bckd->bnqkgd', p[..., BAND:BAND + n_ctx], v_ctx))
    return out.reshape(bsz, n_tok, Q_W)


def context_attention(q, k, v, sink):
    bsz, n_ctx = q.shape[0], q.shape[1]
    scale = HEAD_DIM ** -0.5
    qg = q.reshape(bsz, n_ctx, N_KV_HEADS, GROUP, HEAD_DIM)
    s = jnp.einsum('bqkgd,bckd->bkgqc', qg, k, preferred_element_type=jnp.float32) * scale
    s_sink = jnp.broadcast_to(sink.astype(jnp.float32).reshape(1, N_KV_HEADS, GROUP, 1, 1),
                              s.shape[:-1] + (1,))
    p = jax.nn.softmax(jnp.concatenate([s, s_sink], -1), axis=-1).astype(v.dtype)
    out = jnp.einsum('bkgqc,bckd->bqkgd', p[..., :n_ctx], v)
    return out.reshape(bsz, n_ctx, Q_W)


def s5_discretise(a_re, a_im, log_dt, b_re, b_im):
    lam = lax.complex(a_re.astype(jnp.float32), a_im.astype(jnp.float32))
    dt = jnp.exp(log_dt.astype(jnp.float32))[:, None]
    lam_bar = jnp.exp(lam * dt)
    b = lax.complex(b_re.astype(jnp.float32), b_im.astype(jnp.float32))
    b_bar = ((lam_bar - 1.0) / lam)[..., None] * b
    return lam_bar, b_bar


def _linear_recurrence(e1, e2):
    a1, b1 = e1
    a2, b2 = e2
    return a1 * a2, a2 * b1 + b2


def s5_scan(u, lam_bar, b_bar, reverse, s0=None):
    bu = jnp.einsum('blgh,gph->blgp', u, b_bar)
    if s0 is not None:
        edge = u.shape[1] - 1 if reverse else 0
        bu = bu.at[:, edge].add(lam_bar * s0)
    a = jnp.broadcast_to(lam_bar, bu.shape)
    _, s = lax.associative_scan(_linear_recurrence, (a, bu), reverse=reverse, axis=1)
    return s


def s5_readout(s, c_re, c_im):
    cmat = lax.complex(c_re.astype(jnp.float32), c_im.astype(jnp.float32))
    return jnp.real(jnp.einsum('blgp,ghp->blgh', s, cmat))


def s5_bidirectional(u_lat, u_ctx, a_re, a_im, log_dt, b_re, b_im, c_re, c_im, d_skip, need_ctx):
    ul = u_lat.astype(jnp.float32).reshape(u_lat.shape[0], u_lat.shape[1], SSM_GROUPS, SSM_GROUP)
    uc = u_ctx.astype(jnp.float32).reshape(u_ctx.shape[0], u_ctx.shape[1], SSM_GROUPS, SSM_GROUP)
    dsk = d_skip.astype(jnp.float32)
    y_lat = dsk * ul
    y_ctx = dsk * uc
    for direction, reverse in enumerate((False, True)):
        lam_bar, b_bar = s5_discretise(a_re[direction], a_im[direction], log_dt[direction],
                                       b_re[direction], b_im[direction])
        s_ctx = s5_scan(uc, lam_bar, b_bar, reverse)
        s_init = s_ctx[:, 0] if reverse else s_ctx[:, -1]
        s_lat = s5_scan(ul, lam_bar, b_bar, reverse, s_init)
        y_lat = y_lat + s5_readout(s_lat, c_re[direction], c_im[direction])
        if need_ctx:
            y_ctx = y_ctx + s5_readout(s_ctx, c_re[direction], c_im[direction])
    y_lat = y_lat.reshape(u_lat.shape).astype(u_lat.dtype)
    y_ctx = y_ctx.reshape(u_ctx.shape).astype(u_ctx.dtype) if need_ctx else None
    return y_lat, y_ctx


def gelu_glu(y, w_glu):
    z = jax.nn.gelu(y) @ w_glu
    return z[..., :SSM_WIDTH] * jax.nn.sigmoid(z[..., SSM_WIDTH:])


def hybrid_mixer(h_lat, h_ctx, w_in, attn_sink, a_re, a_im, log_dt, b_re, b_im, c_re, c_im, d_skip,
                 w_glu, w_attn_up, w_ssm_up, w_out, need_ctx):
    q_l, k_l, v_l, u_l, ga_l, gs_l = jnp.split(h_lat @ w_in, SPLIT_AT, axis=-1)
    q_c, k_c, v_c, u_c, ga_c, gs_c = jnp.split(h_ctx @ w_in, SPLIT_AT, axis=-1)

    def heads(t, n):
        return t.reshape(t.shape[0], t.shape[1], n, HEAD_DIM)

    cos, sin = axial_rope_tables(h_lat.shape[1])
    q_l = apply_rope(heads(q_l, N_HEADS), cos, sin)
    k_l = apply_rope(heads(k_l, N_KV_HEADS), cos, sin)
    k_c, v_c = heads(k_c, N_KV_HEADS), heads(v_c, N_KV_HEADS)
    attn_l = windowed_gqa_latent(q_l, k_l, heads(v_l, N_KV_HEADS), k_c, v_c, attn_sink)
    ssm_l, ssm_c = s5_bidirectional(u_l, u_c, a_re, a_im, log_dt, b_re, b_im, c_re, c_im, d_skip, need_ctx)

    def merge(attn, ssm, ga, gs):
        attn_d = attn @ w_attn_up
        ssm_d = gelu_glu(ssm, w_glu) @ w_ssm_up
        return (jax.nn.sigmoid(ga) * attn_d + jax.nn.sigmoid(gs) * ssm_d) @ w_out

    y_lat = merge(attn_l, ssm_l, ga_l, gs_l)
    y_ctx = None
    if need_ctx:
        attn_c = context_attention(heads(q_c, N_HEADS), k_c, v_c, attn_sink)
        y_ctx = merge(attn_c, ssm_c, ga_c, gs_c)
    return y_lat, y_ctx


def squared_relu_mlp(h, w1, b1, w2, b2):
    return jnp.square(jax.nn.relu(h @ w1 + b1)) @ w2 + b2


def _fwd_setup_inputs(seed: int = 0) -> dict:
    key = jax.random.key(seed)
    ks = jax.random.split(key, 28)
    f32 = jnp.float32

    def nrm(k, shape, s):
        return jax.random.normal(k, shape, f32) * s

    G, P, HG = SSM_GROUPS, SSM_STATE, SSM_GROUP
    return {
        'x': nrm(ks[0], (BATCH, SEQ, D_MODEL), 1.0),
        'c': nrm(ks[1], (BATCH, D_MODEL), 1.0),
        'ctx': nrm(ks[2], (BATCH, CTX_LEN, D_MODEL), 1.0),
        'c_ctx': nrm(ks[3], (D_MODEL,), 1.0),
        'w_ada': nrm(ks[4], (DEPTH, D_MODEL, 6 * D_MODEL), D_MODEL ** -0.5),
        'b_ada': nrm(ks[5], (DEPTH, 6 * D_MODEL), 0.01),
        'w_in': nrm(ks[6], (DEPTH, D_MODEL, IN_COLS), D_MODEL ** -0.5),
        'attn_sink': nrm(ks[7], (DEPTH, N_HEADS), 0.5),
        'ssm_a_re': -0.5 + nrm(ks[8], (DEPTH, 2, G, P), 0.01),
        'ssm_a_im': jnp.pi * jnp.arange(P, dtype=f32) + nrm(ks[9], (DEPTH, 2, G, P), 0.01),
        'ssm_log_dt': jax.random.uniform(ks[10], (DEPTH, 2, G), f32, math.log(1e-3), math.log(1e-1)),
        'ssm_b_re': nrm(ks[11], (DEPTH, 2, G, P, HG), (2 * HG) ** -0.5),
        'ssm_b_im': nrm(ks[12], (DEPTH, 2, G, P, HG), (2 * HG) ** -0.5),
        'ssm_c_re': nrm(ks[13], (DEPTH, 2, G, HG, P), P ** -0.5),
        'ssm_c_im': nrm(ks[14], (DEPTH, 2, G, HG, P), P ** -0.5),
        'ssm_d': nrm(ks[15], (DEPTH, G, HG), 0.5),
        'w_glu': nrm(ks[16], (DEPTH, SSM_WIDTH, 2 * SSM_WIDTH), SSM_WIDTH ** -0.5),
        'w_attn_up': nrm(ks[17], (DEPTH, Q_W, D_MODEL), Q_W ** -0.5),
        'w_ssm_up': nrm(ks[18], (DEPTH, SSM_WIDTH, D_MODEL), SSM_WIDTH ** -0.5),
        'w_out': nrm(ks[19], (DEPTH, D_MODEL, D_MODEL), BETA * D_MODEL ** -0.5),
        'ln_mix_g': 1.0 + nrm(ks[20], (DEPTH, D_MODEL), 0.01),
        'ln_mix_b': nrm(ks[21], (DEPTH, D_MODEL), 0.01),
        'w_mlp1': nrm(ks[22], (DEPTH, D_MODEL, D_FF), D_MODEL ** -0.5),
        'b_mlp1': nrm(ks[23], (DEPTH, D_FF), 0.01),
        'w_mlp2': nrm(ks[24], (DEPTH, D_FF, D_MODEL), BETA * D_FF ** -0.5),
        'b_mlp2': nrm(ks[25], (DEPTH, D_MODEL), 0.01),
        'ln_mlp_g': 1.0 + nrm(ks[26], (DEPTH, D_MODEL), 0.01),
        'ln_mlp_b': nrm(ks[27], (DEPTH, D_MODEL), 0.01),
    }


def _fwd_reference(x, c, ctx, c_ctx, w_ada, b_ada, w_in, attn_sink, ssm_a_re, ssm_a_im, ssm_log_dt,
              ssm_b_re, ssm_b_im, ssm_c_re, ssm_c_im, ssm_d, w_glu, w_attn_up, w_ssm_up, w_out,
              ln_mix_g, ln_mix_b, w_mlp1, b_mlp1, w_mlp2, b_mlp2, ln_mlp_g, ln_mlp_b):
    ctx_s = ctx
    for layer in range(DEPTH):
        last = layer == DEPTH - 1
        mod_lat = jax.nn.silu(c) @ w_ada[layer] + b_ada[layer]
        mod_ctx = jax.nn.silu(c_ctx) @ w_ada[layer] + b_ada[layer]
        sh1, sc1, g1, sh2, sc2, g2 = jnp.split(mod_lat[:, None, :], 6, axis=-1)
        csh1, csc1, cg1, csh2, csc2, cg2 = jnp.split(mod_ctx, 6, axis=-1)

        h_lat = layer_norm(x) * (1.0 + sc1) + sh1
        h_ctx = layer_norm(ctx_s) * (1.0 + csc1) + csh1
        y_lat, y_ctx = hybrid_mixer(h_lat, h_ctx, w_in[layer], attn_sink[layer],
                                    ssm_a_re[layer], ssm_a_im[layer], ssm_log_dt[layer],
                                    ssm_b_re[layer], ssm_b_im[layer], ssm_c_re[layer], ssm_c_im[layer],
                                    ssm_d[layer], w_glu[layer], w_attn_up[layer], w_ssm_up[layer],
                                    w_out[layer], not last)
        x = post_norm(ALPHA * x + g1 * y_lat, ln_mix_g[layer], ln_mix_b[layer])
        h2 = layer_norm(x) * (1.0 + sc2) + sh2
        x = post_norm(ALPHA * x + g2 * squared_relu_mlp(h2, w_mlp1[layer], b_mlp1[layer],
                                                        w_mlp2[layer], b_mlp2[layer]),
                      ln_mlp_g[layer], ln_mlp_b[layer])
        if not last:
            ctx_s = post_norm(ALPHA * ctx_s + cg1 * y_ctx, ln_mix_g[layer], ln_mix_b[layer])
            hc2 = layer_norm(ctx_s) * (1.0 + csc2) + csh2
            ctx_s = post_norm(ALPHA * ctx_s + cg2 * squared_relu_mlp(hc2, w_mlp1[layer], b_mlp1[layer],
                                                                   w_mlp2[layer], b_mlp2[layer]),
                              ln_mlp_g[layer], ln_mlp_b[layer])
    return x


import jax as _jax
import jax.numpy as _jnp

TWIN_FORMAT = 'train_step'
FWD_PARAMS = ['x', 'c', 'ctx', 'c_ctx', 'w_ada', 'b_ada', 'w_in', 'attn_sink', 'ssm_a_re', 'ssm_a_im', 'ssm_log_dt', 'ssm_b_re', 'ssm_b_im', 'ssm_c_re', 'ssm_c_im', 'ssm_d', 'w_glu', 'w_attn_up', 'w_ssm_up', 'w_out', 'ln_mix_g', 'ln_mix_b', 'w_mlp1', 'b_mlp1', 'w_mlp2', 'b_mlp2', 'ln_mlp_g', 'ln_mlp_b']
TWIN_WEIGHTS = ['c_ctx', 'w_ada', 'b_ada', 'w_in', 'attn_sink', 'ssm_a_re', 'ssm_a_im', 'ssm_log_dt', 'ssm_b_re', 'ssm_b_im', 'ssm_c_re', 'ssm_c_im', 'ssm_d', 'w_glu', 'w_attn_up', 'w_ssm_up', 'w_out', 'ln_mix_g', 'ln_mix_b', 'w_mlp1', 'b_mlp1', 'w_mlp2', 'b_mlp2', 'ln_mlp_g', 'ln_mlp_b']
TWIN_DIFF_INPUT = 'x'
TWIN_INPUTS = ['x', 'c', 'ctx', 'c_ctx', 'w_ada', 'b_ada', 'w_in', 'attn_sink', 'ssm_a_re', 'ssm_a_im', 'ssm_log_dt', 'ssm_b_re', 'ssm_b_im', 'ssm_c_re', 'ssm_c_im', 'ssm_d', 'w_glu', 'w_attn_up', 'w_ssm_up', 'w_out', 'ln_mix_g', 'ln_mix_b', 'w_mlp1', 'b_mlp1', 'w_mlp2', 'b_mlp2', 'ln_mlp_g', 'ln_mlp_b', 'loss_target', 'm_c_ctx', 'm_w_ada', 'm_b_ada', 'm_w_in', 'm_attn_sink', 'm_ssm_a_re', 'm_ssm_a_im', 'm_ssm_log_dt', 'm_ssm_b_re', 'm_ssm_b_im', 'm_ssm_c_re', 'm_ssm_c_im', 'm_ssm_d', 'm_w_glu', 'm_w_attn_up', 'm_w_ssm_up', 'm_w_out', 'm_ln_mix_g', 'm_ln_mix_b', 'm_w_mlp1', 'm_b_mlp1', 'm_w_mlp2', 'm_b_mlp2', 'm_ln_mlp_g', 'm_ln_mlp_b', 'v_c_ctx', 'v_w_ada', 'v_b_ada', 'v_w_in', 'v_attn_sink', 'v_ssm_a_re', 'v_ssm_a_im', 'v_ssm_log_dt', 'v_ssm_b_re', 'v_ssm_b_im', 'v_ssm_c_re', 'v_ssm_c_im', 'v_ssm_d', 'v_w_glu', 'v_w_attn_up', 'v_w_ssm_up', 'v_w_out', 'v_ln_mix_g', 'v_ln_mix_b', 'v_w_mlp1', 'v_b_mlp1', 'v_w_mlp2', 'v_b_mlp2', 'v_ln_mlp_g', 'v_ln_mlp_b']
TWIN_OUTPUTS = ['loss', 'grad_x', 'grad_c_ctx', 'grad_w_ada', 'grad_b_ada', 'grad_w_in', 'grad_attn_sink', 'grad_ssm_a_re', 'grad_ssm_a_im', 'grad_ssm_log_dt', 'grad_ssm_b_re', 'grad_ssm_b_im', 'grad_ssm_c_re', 'grad_ssm_c_im', 'grad_ssm_d', 'grad_w_glu', 'grad_w_attn_up', 'grad_w_ssm_up', 'grad_w_out', 'grad_ln_mix_g', 'grad_ln_mix_b', 'grad_w_mlp1', 'grad_b_mlp1', 'grad_w_mlp2', 'grad_b_mlp2', 'grad_ln_mlp_g', 'grad_ln_mlp_b', 'delta_c_ctx', 'delta_w_ada', 'delta_b_ada', 'delta_w_in', 'delta_attn_sink', 'delta_ssm_a_re', 'delta_ssm_a_im', 'delta_ssm_log_dt', 'delta_ssm_b_re', 'delta_ssm_b_im', 'delta_ssm_c_re', 'delta_ssm_c_im', 'delta_ssm_d', 'delta_w_glu', 'delta_w_attn_up', 'delta_w_ssm_up', 'delta_w_out', 'delta_ln_mix_g', 'delta_ln_mix_b', 'delta_w_mlp1', 'delta_b_mlp1', 'delta_w_mlp2', 'delta_b_mlp2', 'delta_ln_mlp_g', 'delta_ln_mlp_b', 'new_m_c_ctx', 'new_m_w_ada', 'new_m_b_ada', 'new_m_w_in', 'new_m_attn_sink', 'new_m_ssm_a_re', 'new_m_ssm_a_im', 'new_m_ssm_log_dt', 'new_m_ssm_b_re', 'new_m_ssm_b_im', 'new_m_ssm_c_re', 'new_m_ssm_c_im', 'new_m_ssm_d', 'new_m_w_glu', 'new_m_w_attn_up', 'new_m_w_ssm_up', 'new_m_w_out', 'new_m_ln_mix_g', 'new_m_ln_mix_b', 'new_m_w_mlp1', 'new_m_b_mlp1', 'new_m_w_mlp2', 'new_m_b_mlp2', 'new_m_ln_mlp_g', 'new_m_ln_mlp_b', 'new_v_c_ctx', 'new_v_w_ada', 'new_v_b_ada', 'new_v_w_in', 'new_v_attn_sink', 'new_v_ssm_a_re', 'new_v_ssm_a_im', 'new_v_ssm_log_dt', 'new_v_ssm_b_re', 'new_v_ssm_b_im', 'new_v_ssm_c_re', 'new_v_ssm_c_im', 'new_v_ssm_d', 'new_v_w_glu', 'new_v_w_attn_up', 'new_v_w_ssm_up', 'new_v_w_out', 'new_v_ln_mix_g', 'new_v_ln_mix_b', 'new_v_w_mlp1', 'new_v_b_mlp1', 'new_v_w_mlp2', 'new_v_b_mlp2', 'new_v_ln_mlp_g', 'new_v_ln_mlp_b']
TWIN_LEAF_KINDS = {'loss': 'loss', 'grad_x': 'grad_x', 'grad_c_ctx': 'grad_w', 'grad_w_ada': 'grad_w', 'grad_b_ada': 'grad_w', 'grad_w_in': 'grad_w', 'grad_attn_sink': 'grad_w', 'grad_ssm_a_re': 'grad_w', 'grad_ssm_a_im': 'grad_w', 'grad_ssm_log_dt': 'grad_w', 'grad_ssm_b_re': 'grad_w', 'grad_ssm_b_im': 'grad_w', 'grad_ssm_c_re': 'grad_w', 'grad_ssm_c_im': 'grad_w', 'grad_ssm_d': 'grad_w', 'grad_w_glu': 'grad_w', 'grad_w_attn_up': 'grad_w', 'grad_w_ssm_up': 'grad_w', 'grad_w_out': 'grad_w', 'grad_ln_mix_g': 'grad_w', 'grad_ln_mix_b': 'grad_w', 'grad_w_mlp1': 'grad_w', 'grad_b_mlp1': 'grad_w', 'grad_w_mlp2': 'grad_w', 'grad_b_mlp2': 'grad_w', 'grad_ln_mlp_g': 'grad_w', 'grad_ln_mlp_b': 'grad_w', 'delta_c_ctx': 'delta_w', 'delta_w_ada': 'delta_w', 'delta_b_ada': 'delta_w', 'delta_w_in': 'delta_w', 'delta_attn_sink': 'delta_w', 'delta_ssm_a_re': 'delta_w', 'delta_ssm_a_im': 'delta_w', 'delta_ssm_log_dt': 'delta_w', 'delta_ssm_b_re': 'delta_w', 'delta_ssm_b_im': 'delta_w', 'delta_ssm_c_re': 'delta_w', 'delta_ssm_c_im': 'delta_w', 'delta_ssm_d': 'delta_w', 'delta_w_glu': 'delta_w', 'delta_w_attn_up': 'delta_w', 'delta_w_ssm_up': 'delta_w', 'delta_w_out': 'delta_w', 'delta_ln_mix_g': 'delta_w', 'delta_ln_mix_b': 'delta_w', 'delta_w_mlp1': 'delta_w', 'delta_b_mlp1': 'delta_w', 'delta_w_mlp2': 'delta_w', 'delta_b_mlp2': 'delta_w', 'delta_ln_mlp_g': 'delta_w', 'delta_ln_mlp_b': 'delta_w', 'new_m_c_ctx': 'new_m', 'new_m_w_ada': 'new_m', 'new_m_b_ada': 'new_m', 'new_m_w_in': 'new_m', 'new_m_attn_sink': 'new_m', 'new_m_ssm_a_re': 'new_m', 'new_m_ssm_a_im': 'new_m', 'new_m_ssm_log_dt': 'new_m', 'new_m_ssm_b_re': 'new_m', 'new_m_ssm_b_im': 'new_m', 'new_m_ssm_c_re': 'new_m', 'new_m_ssm_c_im': 'new_m', 'new_m_ssm_d': 'new_m', 'new_m_w_glu': 'new_m', 'new_m_w_attn_up': 'new_m', 'new_m_w_ssm_up': 'new_m', 'new_m_w_out': 'new_m', 'new_m_ln_mix_g': 'new_m', 'new_m_ln_mix_b': 'new_m', 'new_m_w_mlp1': 'new_m', 'new_m_b_mlp1': 'new_m', 'new_m_w_mlp2': 'new_m', 'new_m_b_mlp2': 'new_m', 'new_m_ln_mlp_g': 'new_m', 'new_m_ln_mlp_b': 'new_m', 'new_v_c_ctx': 'new_v', 'new_v_w_ada': 'new_v', 'new_v_b_ada': 'new_v', 'new_v_w_in': 'new_v', 'new_v_attn_sink': 'new_v', 'new_v_ssm_a_re': 'new_v', 'new_v_ssm_a_im': 'new_v', 'new_v_ssm_log_dt': 'new_v', 'new_v_ssm_b_re': 'new_v', 'new_v_ssm_b_im': 'new_v', 'new_v_ssm_c_re': 'new_v', 'new_v_ssm_c_im': 'new_v', 'new_v_ssm_d': 'new_v', 'new_v_w_glu': 'new_v', 'new_v_w_attn_up': 'new_v', 'new_v_w_ssm_up': 'new_v', 'new_v_w_out': 'new_v', 'new_v_ln_mix_g': 'new_v', 'new_v_ln_mix_b': 'new_v', 'new_v_w_mlp1': 'new_v', 'new_v_b_mlp1': 'new_v', 'new_v_w_mlp2': 'new_v', 'new_v_b_mlp2': 'new_v', 'new_v_ln_mlp_g': 'new_v', 'new_v_ln_mlp_b': 'new_v'}


def _forward(args):
    return _fwd_reference(*[args[k] for k in FWD_PARAMS])


def _output_shape():
    out = _jax.eval_shape(lambda: _forward(_fwd_setup_inputs(0)))
    return out.shape, out.dtype

N_MICROBATCH = 1
ADAM_LR = 0.001
ADAM_B1 = 0.9
ADAM_B2 = 0.999
ADAM_EPS = 1e-08
ADAM_WD = 0.01
ADAM_STEP = 10
PER_EXAMPLE_BATCH_AXIS = {'x': 0, 'c': 0, 'ctx': 0, 'loss_target': 0}
SHARED_INPUTS = []
_WEIGHT_DTYPES = {'c_ctx': _jnp.float32, 'w_ada': _jnp.float32, 'b_ada': _jnp.float32, 'w_in': _jnp.float32, 'attn_sink': _jnp.float32, 'ssm_a_re': _jnp.float32, 'ssm_a_im': _jnp.float32, 'ssm_log_dt': _jnp.float32, 'ssm_b_re': _jnp.float32, 'ssm_b_im': _jnp.float32, 'ssm_c_re': _jnp.float32, 'ssm_c_im': _jnp.float32, 'ssm_d': _jnp.float32, 'w_glu': _jnp.float32, 'w_attn_up': _jnp.float32, 'w_ssm_up': _jnp.float32, 'w_out': _jnp.float32, 'ln_mix_g': _jnp.float32, 'ln_mix_b': _jnp.float32, 'w_mlp1': _jnp.float32, 'b_mlp1': _jnp.float32, 'w_mlp2': _jnp.float32, 'b_mlp2': _jnp.float32, 'ln_mlp_g': _jnp.float32, 'ln_mlp_b': _jnp.float32}
MOMENT_SCALE = {'c_ctx': 4.037372e-03, 'w_ada': 1.947572e-02, 'b_ada': 3.520768e-02, 'w_in': 4.523121e-03, 'attn_sink': 5.873525e-05, 'ssm_a_re': 1.155201e-03, 'ssm_a_im': 9.508382e-04, 'ssm_log_dt': 4.202369e-01, 'ssm_b_re': 5.471117e-04, 'ssm_b_im': 6.842684e-04, 'ssm_c_re': 8.164426e-04, 'ssm_c_im': 8.454170e-04, 'ssm_d': 1.016100e-02, 'w_glu': 3.790128e-03, 'w_attn_up': 5.436647e-03, 'w_ssm_up': 2.571535e-03, 'w_out': 1.019808e-02, 'ln_mix_g': 1.034333e-01, 'ln_mix_b': 8.176131e-02, 'w_mlp1': 1.967118e-02, 'b_mlp1': 1.611795e-02, 'w_mlp2': 6.034829e-02, 'b_mlp2': 3.468405e-02, 'ln_mlp_g': 8.206476e+00, 'ln_mlp_b': 1.292532e+00}


def _to_microbatches(a, axis):
    t = _jnp.moveaxis(a, axis, 0)
    t = t.reshape((N_MICROBATCH, t.shape[0] // N_MICROBATCH) + t.shape[1:])
    return _jnp.moveaxis(t, 1, axis + 1)


def setup_inputs(seed: int = 0) -> dict:
    inp = _fwd_setup_inputs(seed)
    key = _jax.random.fold_in(_jax.random.key(seed), 7919)
    shape, _ = _output_shape()
    out = dict(inp)
    out["loss_target"] = _jax.random.normal(_jax.random.fold_in(key, 0), shape, _jnp.float32)
    for i, name in enumerate(TWIN_WEIGHTS):
        w = inp[name].astype(_jnp.float32)
        if MOMENT_SCALE is None:
            s = _jnp.sqrt(_jnp.mean(_jnp.square(w)) + 1e-30)
        else:
            s = MOMENT_SCALE[name]
        km, kv = _jax.random.split(_jax.random.fold_in(key, i + 1))
        out[name] = w
        out["m_" + name] = s * _jax.random.normal(km, w.shape, _jnp.float32)
        out["v_" + name] = (s * s) * _jax.random.uniform(kv, w.shape, _jnp.float32, 0.5, 1.5)
    if N_MICROBATCH > 1:
        for name, axis in PER_EXAMPLE_BATCH_AXIS.items():
            out[name] = _to_microbatches(out[name], axis)
    return {'x': out['x'], 'c': out['c'], 'ctx': out['ctx'], 'c_ctx': out['c_ctx'], 'w_ada': out['w_ada'], 'b_ada': out['b_ada'], 'w_in': out['w_in'], 'attn_sink': out['attn_sink'], 'ssm_a_re': out['ssm_a_re'], 'ssm_a_im': out['ssm_a_im'], 'ssm_log_dt': out['ssm_log_dt'], 'ssm_b_re': out['ssm_b_re'], 'ssm_b_im': out['ssm_b_im'], 'ssm_c_re': out['ssm_c_re'], 'ssm_c_im': out['ssm_c_im'], 'ssm_d': out['ssm_d'], 'w_glu': out['w_glu'], 'w_attn_up': out['w_attn_up'], 'w_ssm_up': out['w_ssm_up'], 'w_out': out['w_out'], 'ln_mix_g': out['ln_mix_g'], 'ln_mix_b': out['ln_mix_b'], 'w_mlp1': out['w_mlp1'], 'b_mlp1': out['b_mlp1'], 'w_mlp2': out['w_mlp2'], 'b_mlp2': out['b_mlp2'], 'ln_mlp_g': out['ln_mlp_g'], 'ln_mlp_b': out['ln_mlp_b'], 'loss_target': out['loss_target'], 'm_c_ctx': out['m_c_ctx'], 'm_w_ada': out['m_w_ada'], 'm_b_ada': out['m_b_ada'], 'm_w_in': out['m_w_in'], 'm_attn_sink': out['m_attn_sink'], 'm_ssm_a_re': out['m_ssm_a_re'], 'm_ssm_a_im': out['m_ssm_a_im'], 'm_ssm_log_dt': out['m_ssm_log_dt'], 'm_ssm_b_re': out['m_ssm_b_re'], 'm_ssm_b_im': out['m_ssm_b_im'], 'm_ssm_c_re': out['m_ssm_c_re'], 'm_ssm_c_im': out['m_ssm_c_im'], 'm_ssm_d': out['m_ssm_d'], 'm_w_glu': out['m_w_glu'], 'm_w_attn_up': out['m_w_attn_up'], 'm_w_ssm_up': out['m_w_ssm_up'], 'm_w_out': out['m_w_out'], 'm_ln_mix_g': out['m_ln_mix_g'], 'm_ln_mix_b': out['m_ln_mix_b'], 'm_w_mlp1': out['m_w_mlp1'], 'm_b_mlp1': out['m_b_mlp1'], 'm_w_mlp2': out['m_w_mlp2'], 'm_b_mlp2': out['m_b_mlp2'], 'm_ln_mlp_g': out['m_ln_mlp_g'], 'm_ln_mlp_b': out['m_ln_mlp_b'], 'v_c_ctx': out['v_c_ctx'], 'v_w_ada': out['v_w_ada'], 'v_b_ada': out['v_b_ada'], 'v_w_in': out['v_w_in'], 'v_attn_sink': out['v_attn_sink'], 'v_ssm_a_re': out['v_ssm_a_re'], 'v_ssm_a_im': out['v_ssm_a_im'], 'v_ssm_log_dt': out['v_ssm_log_dt'], 'v_ssm_b_re': out['v_ssm_b_re'], 'v_ssm_b_im': out['v_ssm_b_im'], 'v_ssm_c_re': out['v_ssm_c_re'], 'v_ssm_c_im': out['v_ssm_c_im'], 'v_ssm_d': out['v_ssm_d'], 'v_w_glu': out['v_w_glu'], 'v_w_attn_up': out['v_w_attn_up'], 'v_w_ssm_up': out['v_w_ssm_up'], 'v_w_out': out['v_w_out'], 'v_ln_mix_g': out['v_ln_mix_g'], 'v_ln_mix_b': out['v_ln_mix_b'], 'v_w_mlp1': out['v_w_mlp1'], 'v_b_mlp1': out['v_b_mlp1'], 'v_w_mlp2': out['v_w_mlp2'], 'v_b_mlp2': out['v_b_mlp2'], 'v_ln_mlp_g': out['v_ln_mlp_g'], 'v_ln_mlp_b': out['v_ln_mlp_b']}


def _loss(weights, diff, rest, loss_target):
    with _jax.named_scope("forward"):
        args = {**rest, TWIN_DIFF_INPUT: diff, **{k: w.astype(_WEIGHT_DTYPES[k]) for k, w in weights.items()}}
        y = _forward(args)
    with _jax.named_scope("loss_head"):
        err = _jnp.square(y.astype(_jnp.float32) - loss_target)
        return 0.5 * _jnp.sum(_jnp.mean(err, axis=-1)) if err.ndim else 0.5 * err


def _adamw(w, g, m, v):
    m = ADAM_B1 * m + (1.0 - ADAM_B1) * g
    v = ADAM_B2 * v + (1.0 - ADAM_B2) * _jnp.square(g)
    m_hat = m / (1.0 - ADAM_B1 ** ADAM_STEP)
    v_hat = v / (1.0 - ADAM_B2 ** ADAM_STEP)
    delta = -ADAM_LR * (m_hat / (_jnp.sqrt(v_hat) + ADAM_EPS) + ADAM_WD * w)
    return delta, m, v


def reference(x, c, ctx, c_ctx, w_ada, b_ada, w_in, attn_sink, ssm_a_re, ssm_a_im, ssm_log_dt, ssm_b_re, ssm_b_im, ssm_c_re, ssm_c_im, ssm_d, w_glu, w_attn_up, w_ssm_up, w_out, ln_mix_g, ln_mix_b, w_mlp1, b_mlp1, w_mlp2, b_mlp2, ln_mlp_g, ln_mlp_b, loss_target, m_c_ctx, m_w_ada, m_b_ada, m_w_in, m_attn_sink, m_ssm_a_re, m_ssm_a_im, m_ssm_log_dt, m_ssm_b_re, m_ssm_b_im, m_ssm_c_re, m_ssm_c_im, m_ssm_d, m_w_glu, m_w_attn_up, m_w_ssm_up, m_w_out, m_ln_mix_g, m_ln_mix_b, m_w_mlp1, m_b_mlp1, m_w_mlp2, m_b_mlp2, m_ln_mlp_g, m_ln_mlp_b, v_c_ctx, v_w_ada, v_b_ada, v_w_in, v_attn_sink, v_ssm_a_re, v_ssm_a_im, v_ssm_log_dt, v_ssm_b_re, v_ssm_b_im, v_ssm_c_re, v_ssm_c_im, v_ssm_d, v_w_glu, v_w_attn_up, v_w_ssm_up, v_w_out, v_ln_mix_g, v_ln_mix_b, v_w_mlp1, v_b_mlp1, v_w_mlp2, v_b_mlp2, v_ln_mlp_g, v_ln_mlp_b):
    given = dict(x=x, c=c, ctx=ctx, c_ctx=c_ctx, w_ada=w_ada, b_ada=b_ada, w_in=w_in, attn_sink=attn_sink, ssm_a_re=ssm_a_re, ssm_a_im=ssm_a_im, ssm_log_dt=ssm_log_dt, ssm_b_re=ssm_b_re, ssm_b_im=ssm_b_im, ssm_c_re=ssm_c_re, ssm_c_im=ssm_c_im, ssm_d=ssm_d, w_glu=w_glu, w_attn_up=w_attn_up, w_ssm_up=w_ssm_up, w_out=w_out, ln_mix_g=ln_mix_g, ln_mix_b=ln_mix_b, w_mlp1=w_mlp1, b_mlp1=b_mlp1, w_mlp2=w_mlp2, b_mlp2=b_mlp2, ln_mlp_g=ln_mlp_g, ln_mlp_b=ln_mlp_b, loss_target=loss_target, m_c_ctx=m_c_ctx, m_w_ada=m_w_ada, m_b_ada=m_b_ada, m_w_in=m_w_in, m_attn_sink=m_attn_sink, m_ssm_a_re=m_ssm_a_re, m_ssm_a_im=m_ssm_a_im, m_ssm_log_dt=m_ssm_log_dt, m_ssm_b_re=m_ssm_b_re, m_ssm_b_im=m_ssm_b_im, m_ssm_c_re=m_ssm_c_re, m_ssm_c_im=m_ssm_c_im, m_ssm_d=m_ssm_d, m_w_glu=m_w_glu, m_w_attn_up=m_w_attn_up, m_w_ssm_up=m_w_ssm_up, m_w_out=m_w_out, m_ln_mix_g=m_ln_mix_g, m_ln_mix_b=m_ln_mix_b, m_w_mlp1=m_w_mlp1, m_b_mlp1=m_b_mlp1, m_w_mlp2=m_w_mlp2, m_b_mlp2=m_b_mlp2, m_ln_mlp_g=m_ln_mlp_g, m_ln_mlp_b=m_ln_mlp_b, v_c_ctx=v_c_ctx, v_w_ada=v_w_ada, v_b_ada=v_b_ada, v_w_in=v_w_in, v_attn_sink=v_attn_sink, v_ssm_a_re=v_ssm_a_re, v_ssm_a_im=v_ssm_a_im, v_ssm_log_dt=v_ssm_log_dt, v_ssm_b_re=v_ssm_b_re, v_ssm_b_im=v_ssm_b_im, v_ssm_c_re=v_ssm_c_re, v_ssm_c_im=v_ssm_c_im, v_ssm_d=v_ssm_d, v_w_glu=v_w_glu, v_w_attn_up=v_w_attn_up, v_w_ssm_up=v_w_ssm_up, v_w_out=v_w_out, v_ln_mix_g=v_ln_mix_g, v_ln_mix_b=v_ln_mix_b, v_w_mlp1=v_w_mlp1, v_b_mlp1=v_b_mlp1, v_w_mlp2=v_w_mlp2, v_b_mlp2=v_b_mlp2, v_ln_mlp_g=v_ln_mlp_g, v_ln_mlp_b=v_ln_mlp_b)
    weights = {n: given[n] for n in TWIN_WEIGHTS}
    shared = {n: given[n] for n in SHARED_INPUTS}
    per_example = {n: given[n] for n in ['x', 'c', 'ctx']}
    grad_fn = _jax.value_and_grad(_loss, argnums=(0, 1))

    def one_microbatch(ex, loss_target):
        ex = dict(ex)
        diff = ex.pop(TWIN_DIFF_INPUT)
        return grad_fn(weights, diff, {**shared, **ex}, loss_target)

    if N_MICROBATCH == 1:
        loss, (grad_w, grad_x) = one_microbatch(per_example, given["loss_target"])
    else:
        def body(carry, xs):
            loss_sum, grad_sum = carry
            l_k, (gw_k, gx_k) = one_microbatch(xs[0], xs[1])
            with _jax.named_scope("update"):
                return (loss_sum + l_k, _jax.tree.map(_jnp.add, grad_sum, gw_k)), gx_k

        init = (_jnp.zeros((), _jnp.float32), _jax.tree.map(_jnp.zeros_like, weights))
        (loss, grad_w), grad_x = _jax.lax.scan(body, init, (per_example, given["loss_target"]))
    with _jax.named_scope("update"):
        delta_w, new_m, new_v = {}, {}, {}
        for n in TWIN_WEIGHTS:
            delta_w[n], new_m[n], new_v[n] = _adamw(weights[n], grad_w[n], given["m_" + n], given["v_" + n])
    return (loss, grad_x, *[grad_w[n] for n in TWIN_WEIGHTS], *[delta_w[n] for n in TWIN_WEIGHTS],
            *[new_m[n] for n in TWIN_WEIGHTS], *[new_v[n] for n in TWIN_WEIGHTS])
```

```python
import functools
import math

import jax
import jax.numpy as jnp
from jax import lax
from jax.experimental import pallas as pl
from jax.experimental.pallas import tpu as pltpu

F32 = jnp.float32
BF16 = jnp.bfloat16

N_DEV = 8
D = 2048
T = 2048
C = 256
TA = T + C
GRID_W = 64
HD = 128
NH = 8
NKV = 2
GROUP = NH // NKV
WINDOW = 128
QW = NH * HD
KVW = NKV * HD
SW = D // 4
SG = 16
NG = SW // SG
SP = 64
DFF = 4 * D
IN_COLS = QW + 2 * KVW + SW + 2 * D
ALPHA = 2.0 ** 0.25
LN_EPS = 1e-6
NEG_INF = -1e30
ROPE_BASE = 10000.0
ATT_SCALE = HD ** -0.5

NSEG = 8
SEGLEN = TA // NSEG
GBLK = 8
NBLK = NG // GBLK
BW = GBLK * SP
UW = GBLK * SG

ADAM_LR = 0.001
ADAM_B1 = 0.9
ADAM_B2 = 0.999
ADAM_EPS = 1e-08
ADAM_WD = 0.01
ADAM_STEP = 10

VMEM_LIMIT_BYTES = 56 * 1024 * 1024
MESH = pl.DeviceIdType.MESH


def _cparams(sem=None):
    return pltpu.CompilerParams(dimension_semantics=sem, vmem_limit_bytes=VMEM_LIMIT_BYTES)


def _matmul(a, b, *, mode, name, out_dtypes=(F32,), tm=512, tn=512, tk=None, bias=None, extras=(), epilogue=None):
    if mode == "nn":
        (M, K), (K2, N) = a.shape, b.shape
    elif mode == "nt":
        (M, K), (N, K2) = a.shape, b.shape
    else:
        (K, M), (K2, N) = a.shape, b.shape
    assert K == K2, (name, a.shape, b.shape)
    tm, tn, tk = min(tm, M), min(tn, N), min(tk or K, K)
    assert M % tm == 0 and N % tn == 0 and K % tk == 0, (name, M, N, K, tm, tn, tk)
    nk = K // tk
    if mode == "tn":
        a_spec = pl.BlockSpec((tk, tm), lambda i, j, k: (k, i))
    else:
        a_spec = pl.BlockSpec((tm, tk), lambda i, j, k: (i, k))
    if mode == "nt":
        b_spec = pl.BlockSpec((tn, tk), lambda i, j, k: (j, k))
    else:
        b_spec = pl.BlockSpec((tk, tn), lambda i, j, k: (k, j))
    dims = {"nn": (((1,), (0,)), ((), ())), "nt": (((1,), (1,)), ((), ())), "tn": (((0,), (0,)), ((), ()))}[mode]
    in_specs = [a_spec, b_spec]
    operands = [a, b]
    if bias is not None:
        in_specs.append(pl.BlockSpec((1, tn), lambda i, j, k: (0, j)))
        operands.append(bias)
    for e in extras:
        in_specs.append(pl.BlockSpec((tm, tn), lambda i, j, k: (i, j)))
        operands.append(e)
    n_ex = len(extras)
    n_out = len(out_dtypes)
    has_bias = bias is not None

    def kern(*refs):
        a_ref, b_ref = refs[0], refs[1]
        pos = 2
        bias_ref = None
        if has_bias:
            bias_ref = refs[pos]
            pos += 1
        ex_refs = refs[pos:pos + n_ex]
        pos += n_ex
        out_refs = refs[pos:pos + n_out]
        acc_ref = refs[pos + n_out] if nk > 1 else None

        def finish(r):
            if has_bias:
                r = r + bias_ref[...]
            outs = epilogue(r, *[e[...] for e in ex_refs]) if epilogue is not None else (r,)
            for o_ref, o in zip(out_refs, outs):
                o_ref[...] = o.astype(o_ref.dtype)

        part = lax.dot_general(a_ref[...].astype(BF16), b_ref[...].astype(BF16), dims, preferred_element_type=F32)
        if nk == 1:
            finish(part)
        else:
            k = pl.program_id(2)

            @pl.when(k == 0)
            def _():
                acc_ref[...] = part

            @pl.when(k > 0)
            def _():
                acc_ref[...] += part

            @pl.when(k == nk - 1)
            def _():
                finish(acc_ref[...])

    outs = pl.pallas_call(
        kern,
        name=name,
        grid=(M // tm, N // tn, nk),
        in_specs=in_specs,
        out_specs=[pl.BlockSpec((tm, tn), lambda i, j, k: (i, j)) for _ in out_dtypes],
        out_shape=[jax.ShapeDtypeStruct((M, N), dt) for dt in out_dtypes],
        scratch_shapes=[pltpu.VMEM((tm, tn), F32)] if nk > 1 else [],
        compiler_params=_cparams(("parallel", "parallel", "arbitrary")),
    )(*operands)
    return outs[0] if n_out == 1 else tuple(outs)


def _rowwise(fn, rows, vecs, outs, vec_outs, *, nrows, tr, name):
    n_rows, n_vecs, n_outs = len(rows), len(vecs), len(outs)
    in_specs = [pl.BlockSpec((tr, w), lambda i, cb=cb, ro=ro: (i + ro, cb)) for (_, w, cb, ro) in rows]
    in_specs += [pl.BlockSpec(v.shape, lambda i: (0, 0)) for v in vecs]
    out_specs = [pl.BlockSpec((tr, w), lambda i: (i, 0)) for (w, _) in outs]
    out_specs += [pl.BlockSpec(s, lambda i: (0, 0)) for s in vec_outs]
    out_shape = [jax.ShapeDtypeStruct((nrows, w), dt) for (w, dt) in outs]
    out_shape += [jax.ShapeDtypeStruct(s, F32) for s in vec_outs]

    def kern(*refs):
        rvals = [r[...] for r in refs[:n_rows]]
        vvals = [r[...] for r in refs[n_rows:n_rows + n_vecs]]
        o_refs = refs[n_rows + n_vecs:n_rows + n_vecs + n_outs]
        v_refs = refs[n_rows + n_vecs + n_outs:]
        ro, vo = fn(rvals, vvals)
        for r, val in zip(o_refs, ro):
            r[...] = val.astype(r.dtype)
        i = pl.program_id(0)
        for r, val in zip(v_refs, vo):
            @pl.when(i == 0)
            def _(r=r, val=val):
                r[...] = val.astype(F32)

            @pl.when(i > 0)
            def _(r=r, val=val):
                r[...] += val.astype(F32)

    res = pl.pallas_call(
        kern,
        name=name,
        grid=(nrows // tr,),
        in_specs=in_specs,
        out_specs=out_specs,
        out_shape=out_shape,
        compiler_params=_cparams(("arbitrary",)),
    )(*[r[0] for r in rows], *vecs)
    return list(res)


def _ln(x):
    mu = jnp.mean(x, axis=-1, keepdims=True)
    xc = x - mu
    var = jnp.mean(xc * xc, axis=-1, keepdims=True)
    return xc * lax.rsqrt(var + LN_EPS)


def _sigmoid(x):
    return 1.0 / (1.0 + jnp.exp(-x))


def _gelu(x):
    return 0.5 * x * (1.0 + jnp.tanh(math.sqrt(2.0 / math.pi) * (x + 0.044715 * (x * x * x))))


def _silu(x):
    return x * _sigmoid(x)


def _f_ln_mod(x, sc, sh):
    return _ln(x) * (1.0 + sc) + sh


def _f_ssm_pre(u, yf, yb, dskip):
    return dskip * u + yf + yb


def _f_glu(z):
    return z[:, :SW] * _sigmoid(z[:, SW:])


def _f_mix(ga, gs, attn_d, ssm_d):
    return _sigmoid(ga) * attn_d + _sigmoid(gs) * ssm_d


def _f_post1(x, y, g1, lg, lb, sc2, sh2):
    r1 = ALPHA * x + g1 * y
    x1 = _ln(r1) * lg + lb
    h2 = _ln(x1) * (1.0 + sc2) + sh2
    return x1, h2


def _f_loss(x1, mlp, tgt, g2, lg, lb, b2z):
    r2 = ALPHA * x1 + g2 * (mlp + b2z)
    out = _ln(r2) * lg + lb
    err = out - tgt
    return 0.5 * jnp.sum(err * err) * (1.0 / D)


def _rope_tables():
    rows = T // GRID_W
    row = jnp.repeat(jnp.arange(rows), GRID_W)
    col = jnp.tile(jnp.arange(GRID_W), rows)
    n_freq = HD // 4
    freqs = ROPE_BASE ** (-jnp.arange(n_freq, dtype=F32) / n_freq)
    ang_r = row.astype(F32)[:, None] * freqs
    ang_c = col.astype(F32)[:, None] * freqs
    ang = jnp.concatenate([ang_r, ang_r, ang_c, ang_c], -1)
    cos, sin = jnp.cos(ang), jnp.sin(ang)
    lo = (jnp.arange(HD) % (HD // 2)) < (HD // 4)
    sin_a = jnp.where(lo[None, :], -sin, 0.0)
    sin_b = jnp.where(lo[None, :], 0.0, sin)
    return cos, sin_a, sin_b


def _rope(x, cos, sa, sb):
    return x * cos + pltpu.roll(x, 96, 1) * sa + pltpu.roll(x, 32, 1) * sb


def _rope_t(dy, cos, sa, sb):
    return dy * cos + pltpu.roll(dy * sa, 32, 1) + pltpu.roll(dy * sb, 96, 1)


BAND = 3 * WINDOW
KPAD = T + 2 * WINDOW


def _attn_fill_kv(k_ref, v_ref, cos_ref, sa_ref, sb_ref, kp, vp, kc, vc):
    zeros = jnp.zeros((WINDOW, KVW), BF16)
    kp[0:WINDOW, :] = zeros
    kp[WINDOW + T:KPAD, :] = zeros
    vp[0:WINDOW, :] = zeros
    vp[WINDOW + T:KPAD, :] = zeros
    for hh in range(NKV):
        cs = slice(hh * HD, (hh + 1) * HD)
        for r0 in range(0, T, 512):
            rs = slice(r0, r0 + 512)
            kr = _rope(k_ref[rs, cs], cos_ref[rs, :], sa_ref[rs, :], sb_ref[rs, :])
            kp[WINDOW + r0:WINDOW + r0 + 512, cs] = kr.astype(BF16)
    vp[WINDOW:WINDOW + T, :] = v_ref[0:T, :].astype(BF16)
    kc[...] = k_ref[T:TA, :].astype(BF16)
    vc[...] = v_ref[T:TA, :].astype(BF16)


def _attn_scores(n, h, q_ref, cos_ref, sa_ref, sb_ref, sink_ref, kp, kc):
    kvh = h // GROUP
    r0 = pl.multiple_of(n * WINDOW, WINDOW)
    cos = cos_ref[pl.ds(r0, WINDOW), :]
    sa = sa_ref[pl.ds(r0, WINDOW), :]
    sb = sb_ref[pl.ds(r0, WINDOW), :]
    q_h = _rope(q_ref[:, h * HD:(h + 1) * HD], cos, sa, sb).astype(BF16)
    kb = kp[pl.ds(r0, BAND), kvh * HD:(kvh + 1) * HD]
    kcb = kc[:, kvh * HD:(kvh + 1) * HD]
    nt = (((1,), (1,)), ((), ()))
    s_loc = lax.dot_general(q_h, kb, nt, preferred_element_type=F32) * ATT_SCALE
    s_ctx = lax.dot_general(q_h, kcb, nt, preferred_element_type=F32) * ATT_SCALE
    row = lax.broadcasted_iota(jnp.int32, (WINDOW, BAND), 0)
    col = lax.broadcasted_iota(jnp.int32, (WINDOW, BAND), 1)
    rel = col - WINDOW - row
    kpos = r0 - WINDOW + col
    valid = (jnp.abs(rel) <= WINDOW) & (kpos >= 0) & (kpos < T)
    s_loc = jnp.where(valid, s_loc, NEG_INF)
    sk = sink_ref[0:1, h:h + 1]
    m = jnp.maximum(jnp.maximum(jnp.max(s_loc, -1, keepdims=True), jnp.max(s_ctx, -1, keepdims=True)), sk)
    e_loc = jnp.exp(s_loc - m)
    e_ctx = jnp.exp(s_ctx - m)
    e_sink = jnp.exp(sk - m)
    inv = 1.0 / (jnp.sum(e_loc, -1, keepdims=True) + jnp.sum(e_ctx, -1, keepdims=True) + e_sink)
    return q_h, r0, e_loc * inv, e_ctx * inv, e_sink * inv


def _attn_fwd(proj, sink, tabs):
    cos, sa, sb = tabs

    def kern(q_ref, k_ref, v_ref, cos_ref, sa_ref, sb_ref, sink_ref, o_ref, kp, vp, kc, vc):
        n = pl.program_id(0)

        @pl.when(n == 0)
        def _():
            _attn_fill_kv(k_ref, v_ref, cos_ref, sa_ref, sb_ref, kp, vp, kc, vc)

        for h in range(NH):
            kvh = h // GROUP
            _, r0, p_loc, p_ctx, _ = _attn_scores(n, h, q_ref, cos_ref, sa_ref, sb_ref, sink_ref, kp, kc)
            vb = vp[pl.ds(r0, BAND), kvh * HD:(kvh + 1) * HD]
            vcb = vc[:, kvh * HD:(kvh + 1) * HD]
            o = jnp.dot(p_loc.astype(BF16), vb, preferred_element_type=F32)
            o = o + jnp.dot(p_ctx.astype(BF16), vcb, preferred_element_type=F32)
            o_ref[:, h * HD:(h + 1) * HD] = o.astype(o_ref.dtype)

    full = lambda shape: pl.BlockSpec(shape, lambda n: (0, 0))
    return pl.pallas_call(
        kern,
        name="attn_fwd",
        grid=(T // WINDOW,),
        in_specs=[
            pl.BlockSpec((WINDOW, QW), lambda n: (n, 0)),
            pl.BlockSpec((TA, KVW), lambda n: (0, QW // KVW)),
            pl.BlockSpec((TA, KVW), lambda n: (0, QW // KVW + 1)),
            full((T, HD)), full((T, HD)), full((T, HD)), full((1, NH)),
        ],
        out_specs=pl.BlockSpec((WINDOW, QW), lambda n: (n, 0)),
        out_shape=jax.ShapeDtypeStruct((T, QW), BF16),
        scratch_shapes=[pltpu.VMEM((KPAD, KVW), BF16), pltpu.VMEM((KPAD, KVW), BF16),
                        pltpu.VMEM((C, KVW), BF16), pltpu.VMEM((C, KVW), BF16)],
        compiler_params=_cparams(("arbitrary",)),
    )(proj, proj, proj, cos, sa, sb, sink)


def _attn_bwd(proj, d_attn, sink, tabs):
    cos, sa, sb = tabs
    n_blocks = T // WINDOW

    def kern(q_ref, k_ref, v_ref, do_ref, cos_ref, sa_ref, sb_ref, sink_ref,
             dq_ref, dk_ref, dv_ref, dsink_ref, kp, vp, kc, vc, dkp, dvp, dkc, dvc):
        n = pl.program_id(0)

        @pl.when(n == 0)
        def _():
            _attn_fill_kv(k_ref, v_ref, cos_ref, sa_ref, sb_ref, kp, vp, kc, vc)
            dkp[...] = jnp.zeros_like(dkp)
            dvp[...] = jnp.zeros_like(dvp)
            dkc[...] = jnp.zeros_like(dkc)
            dvc[...] = jnp.zeros_like(dvc)
            dsink_ref[...] = jnp.zeros_like(dsink_ref)

        nt = (((1,), (1,)), ((), ()))
        tn = (((0,), (0,)), ((), ()))
        for h in range(NH):
            kvh = h // GROUP
            cs = slice(kvh * HD, (kvh + 1) * HD)
            q_h, r0, p_loc, p_ctx, p_sink = _attn_scores(n, h, q_ref, cos_ref, sa_ref, sb_ref, sink_ref, kp, kc)
            kb = kp[pl.ds(r0, BAND), cs]
            vb = vp[pl.ds(r0, BAND), cs]
            kcb = kc[:, cs]
            vcb = vc[:, cs]
            do_h = do_ref[:, h * HD:(h + 1) * HD]
            dp_loc = lax.dot_general(do_h, vb, nt, preferred_element_type=F32)
            dp_ctx = lax.dot_general(do_h, vcb, nt, preferred_element_type=F32)
            delta = jnp.sum(p_loc * dp_loc, -1, keepdims=True) + jnp.sum(p_ctx * dp_ctx, -1, keepdims=True)
            ds_loc = (p_loc * (dp_loc - delta) * ATT_SCALE).astype(BF16)
            ds_ctx = (p_ctx * (dp_ctx - delta) * ATT_SCALE).astype(BF16)
            dq = jnp.dot(ds_loc, kb, preferred_element_type=F32) + jnp.dot(ds_ctx, kcb, preferred_element_type=F32)
            cos = cos_ref[pl.ds(r0, WINDOW), :]
            sa_ = sa_ref[pl.ds(r0, WINDOW), :]
            sb_ = sb_ref[pl.ds(r0, WINDOW), :]
            dq_ref[:, h * HD:(h + 1) * HD] = _rope_t(dq, cos, sa_, sb_).astype(dq_ref.dtype)
            dkp[pl.ds(r0, BAND), cs] += lax.dot_general(ds_loc, q_h, tn, preferred_element_type=F32)
            dkc[:, cs] += lax.dot_general(ds_ctx, q_h, tn, preferred_element_type=F32)
            dvp[pl.ds(r0, BAND), cs] += lax.dot_general(p_loc.astype(BF16), do_h, tn, preferred_element_type=F32)
            dvc[:, cs] += lax.dot_general(p_ctx.astype(BF16), do_h, tn, preferred_element_type=F32)
            dsk = -jnp.sum(p_sink * delta, axis=0, keepdims=True)
            dsink_ref[h:h + 1, :] += jnp.broadcast_to(dsk, (1, HD))

        @pl.when(n == n_blocks - 1)
        def _():
            for hh in range(NKV):
                cs = slice(hh * HD, (hh + 1) * HD)
                for r0 in range(0, T, 512):
                    rs = slice(r0, r0 + 512)
                    g = dkp[WINDOW + r0:WINDOW + r0 + 512, cs]
                    dk_ref[rs, cs] = _rope_t(g, cos_ref[rs, :], sa_ref[rs, :], sb_ref[rs, :]).astype(dk_ref.dtype)
            dk_ref[T:TA, :] = dkc[...].astype(dk_ref.dtype)
            dv_ref[0:T, :] = dvp[WINDOW:WINDOW + T, :].astype(dv_ref.dtype)
            dv_ref[T:TA, :] = dvc[...].astype(dv_ref.dtype)

    full = lambda shape: pl.BlockSpec(shape, lambda n: (0, 0))
    return pl.pallas_call(
        kern,
        name="attn_bwd",
        grid=(n_blocks,),
        in_specs=[
            pl.BlockSpec((WINDOW, QW), lambda n: (n, 0)),
            pl.BlockSpec((TA, KVW), lambda n: (0, QW // KVW)),
            pl.BlockSpec((TA, KVW), lambda n: (0, QW // KVW + 1)),
            pl.BlockSpec((WINDOW, QW), lambda n: (n, 0)),
            full((T, HD)), full((T, HD)), full((T, HD)), full((1, NH)),
        ],
        out_specs=[pl.BlockSpec((WINDOW, QW), lambda n: (n, 0)), full((TA, KVW)), full((TA, KVW)), full((NH, HD))],
        out_shape=[jax.ShapeDtypeStruct((T, QW), BF16), jax.ShapeDtypeStruct((TA, KVW), BF16),
                   jax.ShapeDtypeStruct((TA, KVW), BF16), jax.ShapeDtypeStruct((NH, HD), F32)],
        scratch_shapes=[pltpu.VMEM((KPAD, KVW), BF16), pltpu.VMEM((KPAD, KVW), BF16),
                        pltpu.VMEM((C, KVW), BF16), pltpu.VMEM((C, KVW), BF16),
                        pltpu.VMEM((KPAD, KVW), F32), pltpu.VMEM((KPAD, KVW), F32),
                        pltpu.VMEM((C, KVW), F32), pltpu.VMEM((C, KVW), F32)],
        compiler_params=_cparams(("arbitrary",)),
    )(proj, proj, proj, d_attn, cos, sa, sb, sink)


def _s5_prep(a_re, a_im, log_dt, b_re, b_im, c_re, c_im):
    lam = lax.complex(a_re, a_im)
    dt = jnp.exp(log_dt)[..., None]
    lam_bar = jnp.exp(lam * dt)
    b_bar = ((lam_bar - 1.0) / lam)[..., None] * lax.complex(b_re, b_im)
    eye = jnp.eye(GBLK, dtype=F32)

    def lam_rows(v):
        return v.reshape(2, NBLK, 1, BW)

    lam_l = jnp.concatenate([lam_rows(jnp.real(lam_bar)), lam_rows(jnp.imag(lam_bar))], -1)
    lam_l = jnp.broadcast_to(lam_l, (2, NBLK, 8, 2 * BW))

    def b_blocks(v):
        v = v.reshape(2, NBLK, GBLK, SP, SG).transpose(0, 1, 2, 4, 3)
        return (v[:, :, :, :, None, :] * eye[None, None, :, None, :, None]).reshape(2, NBLK, UW, BW)

    bmat = jnp.concatenate([b_blocks(jnp.real(b_bar)), b_blocks(jnp.imag(b_bar))], -1)

    def c_blocks(v):
        v = v.reshape(2, NBLK, GBLK, SG, SP).transpose(0, 1, 2, 4, 3)
        return (v[:, :, :, :, None, :] * eye[None, None, :, None, :, None]).reshape(2, NBLK, BW, UW)

    cmat = jnp.concatenate([c_blocks(c_re), -c_blocks(c_im)], 2)
    return lam_l, bmat, cmat


def _to_seq(lat, ctx):
    w = lat.shape[-1]
    f = jnp.concatenate([ctx, lat], 0)
    b = jnp.concatenate([ctx[::-1], lat[::-1]], 0)
    s = jnp.stack([f, b])
    return s.reshape(2, NSEG, SEGLEN, w).transpose(0, 2, 1, 3).reshape(2, TA, w)


def _from_seq(s):
    w = s.shape[-1]
    s = s.reshape(2, SEGLEN, NSEG, w).transpose(0, 2, 1, 3).reshape(2, TA, w)
    return s[0, C:], s[0, :C], s[1, C:][::-1], s[1, :C][::-1]


def _cmul(ar, ai, br, bi):
    return ar * br - ai * bi, ar * bi + ai * br


def _shift_rows(x, up):
    r = lax.broadcasted_iota(jnp.int32, x.shape, 0)
    if up:
        return jnp.where(r == NSEG - 1, 0.0, pltpu.roll(x, NSEG - 1, 0))
    return jnp.where(r == 0, 0.0, pltpu.roll(x, 1, 0))


RCH = 256


def _s5_fwd(u_seq, lam, bmat, cmat):
    def kern(u_ref, lam_ref, b_ref, c_ref, s_ref, y_ref):
        bm = b_ref[0, 0].astype(BF16)
        for r0 in range(0, TA, RCH):
            s_ref[0, 0, r0:r0 + RCH, :] = jnp.dot(u_ref[0, r0:r0 + RCH, :].astype(BF16), bm, preferred_element_type=F32)
        lr = lam_ref[0, 0, :, 0:BW]
        li = lam_ref[0, 0, :, BW:2 * BW]
        zero = jnp.zeros((NSEG, BW), F32)

        def scan1(j, carry):
            sr, si, pr, pi = carry
            row = pl.multiple_of(j * NSEG, NSEG)
            tr, ti = _cmul(lr, li, sr, si)
            sr = tr + s_ref[0, 0, pl.ds(row, NSEG), 0:BW]
            si = ti + s_ref[0, 0, pl.ds(row, NSEG), BW:2 * BW]
            s_ref[0, 0, pl.ds(row, NSEG), 0:BW] = sr
            s_ref[0, 0, pl.ds(row, NSEG), BW:2 * BW] = si
            pr, pi = _cmul(lr, li, pr, pi)
            return sr, si, pr, pi

        er, ei, lpr, lpi = lax.fori_loop(0, SEGLEN, scan1, (zero, zero, zero + 1.0, zero))
        cr, ci = zero, zero
        for _ in range(NSEG - 1):
            tr, ti = _cmul(lpr, lpi, cr, ci)
            cr, ci = _shift_rows(er + tr, False), _shift_rows(ei + ti, False)

        def scan2(j, carry):
            pr, pi = carry
            row = pl.multiple_of(j * NSEG, NSEG)
            tr, ti = _cmul(pr, pi, cr, ci)
            s_ref[0, 0, pl.ds(row, NSEG), 0:BW] += tr
            s_ref[0, 0, pl.ds(row, NSEG), BW:2 * BW] += ti
            return _cmul(lr, li, pr, pi)

        lax.fori_loop(0, SEGLEN, scan2, (lr, li))
        cm = c_ref[0, 0].astype(BF16)
        for r0 in range(0, TA, RCH):
            y_ref[0, r0:r0 + RCH, :] = jnp.dot(s_ref[0, 0, r0:r0 + RCH, :].astype(BF16), cm, preferred_element_type=F32)

    return pl.pallas_call(
        kern,
        name="s5_fwd",
        grid=(2, NBLK),
        in_specs=[
            pl.BlockSpec((1, TA, UW), lambda d, b: (d, 0, b)),
            pl.BlockSpec((1, 1, 8, 2 * BW), lambda d, b: (d, b, 0, 0)),
            pl.BlockSpec((1, 1, UW, 2 * BW), lambda d, b: (d, b, 0, 0)),
            pl.BlockSpec((1, 1, 2 * BW, UW), lambda d, b: (d, b, 0, 0)),
        ],
        out_specs=[pl.BlockSpec((1, 1, TA, 2 * BW), lambda d, b: (d, b, 0, 0)),
                   pl.BlockSpec((1, TA, UW), lambda d, b: (d, 0, b))],
        out_shape=[jax.ShapeDtypeStruct((2, NBLK, TA, 2 * BW), F32), jax.ShapeDtypeStruct((2, TA, SW), F32)],
        compiler_params=_cparams(("parallel", "parallel")),
    )(u_seq, lam, bmat, cmat)


def _s5_bwd(dy_seq, states, u_seq, lam, bmat, cmat):
    nt = (((1,), (1,)), ((), ()))
    tn = (((0,), (0,)), ((), ()))

    def kern(dy_ref, s_ref, u_ref, lam_ref, b_ref, c_ref, du_ref, dlam_ref, db_ref, dc_ref, g_ref):
        cm = c_ref[0, 0].astype(BF16)
        for r0 in range(0, TA, RCH):
            g_ref[r0:r0 + RCH, :] = lax.dot_general(dy_ref[0, r0:r0 + RCH, :].astype(BF16), cm, nt, preferred_element_type=F32)
        lr = lam_ref[0, 0, :, 0:BW]
        li = -lam_ref[0, 0, :, BW:2 * BW]
        zero = jnp.zeros((NSEG, BW), F32)

        def scan1(jj, carry):
            gr, gi, pr, pi = carry
            row = pl.multiple_of((SEGLEN - 1 - jj) * NSEG, NSEG)
            tr, ti = _cmul(lr, li, gr, gi)
            gr = tr + g_ref[pl.ds(row, NSEG), 0:BW]
            gi = ti + g_ref[pl.ds(row, NSEG), BW:2 * BW]
            g_ref[pl.ds(row, NSEG), 0:BW] = gr
            g_ref[pl.ds(row, NSEG), BW:2 * BW] = gi
            pr, pi = _cmul(lr, li, pr, pi)
            return gr, gi, pr, pi

        br, bi, lpr, lpi = lax.fori_loop(0, SEGLEN, scan1, (zero, zero, zero + 1.0, zero))
        cr, ci = zero, zero
        for _ in range(NSEG - 1):
            tr, ti = _cmul(lpr, lpi, cr, ci)
            cr, ci = _shift_rows(br + tr, True), _shift_rows(bi + ti, True)

        def dlam_terms(gr, gi, sr, si):
            return gr * sr + gi * si, gi * sr - gr * si

        def scan2(jj, carry):
            pr, pi, ar, ai = carry
            j = SEGLEN - 1 - jj
            row = pl.multiple_of(j * NSEG, NSEG)
            prev = pl.multiple_of((j - 1) * NSEG, NSEG)
            tr, ti = _cmul(pr, pi, cr, ci)
            gr = g_ref[pl.ds(row, NSEG), 0:BW] + tr
            gi = g_ref[pl.ds(row, NSEG), BW:2 * BW] + ti
            g_ref[pl.ds(row, NSEG), 0:BW] = gr
            g_ref[pl.ds(row, NSEG), BW:2 * BW] = gi
            dr, di = dlam_terms(gr, gi, s_ref[0, 0, pl.ds(prev, NSEG), 0:BW], s_ref[0, 0, pl.ds(prev, NSEG), BW:2 * BW])
            pr, pi = _cmul(lr, li, pr, pi)
            return pr, pi, ar + dr, ai + di

        pr, pi, ar, ai = lax.fori_loop(0, SEGLEN - 1, scan2, (lr, li, zero, zero))
        tr, ti = _cmul(pr, pi, cr, ci)
        gr = g_ref[0:NSEG, 0:BW] + tr
        gi = g_ref[0:NSEG, BW:2 * BW] + ti
        g_ref[0:NSEG, 0:BW] = gr
        g_ref[0:NSEG, BW:2 * BW] = gi
        last = (SEGLEN - 1) * NSEG
        dr, di = dlam_terms(gr, gi, _shift_rows(s_ref[0, 0, last:last + NSEG, 0:BW], False),
                            _shift_rows(s_ref[0, 0, last:last + NSEG, BW:2 * BW], False))
        dlam_ref[0, 0, :, 0:BW] = ar + dr
        dlam_ref[0, 0, :, BW:2 * BW] = ai + di

        bm = b_ref[0, 0].astype(BF16)
        db = jnp.zeros((UW, 2 * BW), F32)
        dc = jnp.zeros((2 * BW, UW), F32)
        for r0 in range(0, TA, RCH):
            g = g_ref[r0:r0 + RCH, :].astype(BF16)
            du_ref[0, r0:r0 + RCH, :] = lax.dot_general(g, bm, nt, preferred_element_type=F32)
            db = db + lax.dot_general(u_ref[0, r0:r0 + RCH, :].astype(BF16), g, tn, preferred_element_type=F32)
            dc = dc + lax.dot_general(s_ref[0, 0, r0:r0 + RCH, :].astype(BF16), dy_ref[0, r0:r0 + RCH, :].astype(BF16), tn,
                                      preferred_element_type=F32)
        db_ref[0, 0] = db
        dc_ref[0, 0] = dc

    blk4 = lambda shape: pl.BlockSpec((1, 1) + shape, lambda d, b: (d, b, 0, 0))
    cols = pl.BlockSpec((1, TA, UW), lambda d, b: (d, 0, b))
    return pl.pallas_call(
        kern,
        name="s5_bwd",
        grid=(2, NBLK),
        in_specs=[cols, blk4((TA, 2 * BW)), cols, blk4((8, 2 * BW)), blk4((UW, 2 * BW)), blk4((2 * BW, UW))],
        out_specs=[cols, blk4((8, 2 * BW)), blk4((UW, 2 * BW)), blk4((2 * BW, UW))],
        out_shape=[jax.ShapeDtypeStruct((2, TA, SW), F32), jax.ShapeDtypeStruct((2, NBLK, 8, 2 * BW), F32),
                   jax.ShapeDtypeStruct((2, NBLK, UW, 2 * BW), F32), jax.ShapeDtypeStruct((2, NBLK, 2 * BW, UW), F32)],
        scratch_shapes=[pltpu.VMEM((TA, 2 * BW), F32)],
        compiler_params=_cparams(("parallel", "parallel")),
    )(dy_seq, states, u_seq, lam, bmat, cmat)


TR = 256


def _vjp_rows(f, primals, cots, n_row):
    _, pull = jax.vjp(f, *primals)
    g = pull(cots)
    return list(g[:n_row]), list(g[n_row:])


def _local_step(x, ctx, tgt, mod_lat, mod_ctx, wb, sp):
    sh1, sc1, g1, sh2, sc2, g2 = [mod_lat[:, i * D:(i + 1) * D] for i in range(6)]
    csh1, csc1 = mod_ctx[:, 0:D], mod_ctx[:, D:2 * D]
    tabs = _rope_tables()
    sink = sp["attn_sink"].reshape(1, NH)
    dskip = sp["ssm_d"].reshape(1, SW)
    lg_mix, lb_mix = sp["ln_mix_g"].reshape(1, D), sp["ln_mix_b"].reshape(1, D)
    lg_mlp, lb_mlp = sp["ln_mlp_g"].reshape(1, D), sp["ln_mlp_b"].reshape(1, D)
    b1, b2 = sp["b_mlp1"].reshape(1, DFF), sp["b_mlp2"].reshape(1, D)
    s5_names = ("ssm_a_re", "ssm_a_im", "ssm_log_dt", "ssm_b_re", "ssm_b_im", "ssm_c_re", "ssm_c_im")
    (lam, bmat, cmat), s5_pull = jax.vjp(_s5_prep, *[sp[n] for n in s5_names])

    def ln_mod(rv, vv):
        return [_f_ln_mod(rv[0], vv[0], vv[1])], []

    h_lat, = _rowwise(ln_mod, [(x, D, 0, 0)], [sc1, sh1], [(D, BF16)], [], nrows=T, tr=TR, name="ln1_lat")
    h_ctx, = _rowwise(ln_mod, [(ctx, D, 0, 0)], [csc1, csh1], [(D, BF16)], [], nrows=C, tr=TR, name="ln1_ctx")
    h1 = jnp.concatenate([h_lat, h_ctx], 0)
    proj = _matmul(h1, wb["w_in"], mode="nn", name="proj", tm=768, tn=512)
    attn = _attn_fwd(proj, sink, tabs)
    u_all = proj[:, QW + 2 * KVW:QW + 2 * KVW + SW]
    u_lat, u_ctx = u_all[:T], u_all[T:]
    u_seq = _to_seq(u_lat, u_ctx)
    states, y_seq = _s5_fwd(u_seq, lam, bmat, cmat)
    y_f, _, y_b, _ = _from_seq(y_seq)

    def ssm_pre(rv, vv):
        s = _f_ssm_pre(rv[0], rv[1], rv[2], vv[0])
        return [s, _gelu(s)], []

    ssm, ge = _rowwise(ssm_pre, [(u_lat, SW, 0, 0), (y_f, SW, 0, 0), (y_b, SW, 0, 0)], [dskip],
                       [(SW, F32), (SW, BF16)], [], nrows=T, tr=TR, name="ssm_pre")
    z = _matmul(ge, wb["w_glu"], mode="nn", name="glu_mm", tm=1024, tn=1024)

    def glu_act(rv, vv):
        return [_f_glu(rv[0])], []

    glu, = _rowwise(glu_act, [(z, 2 * SW, 0, 0)], [], [(SW, BF16)], [], nrows=T, tr=TR, name="glu_act")
    attn_d = _matmul(attn, wb["w_attn_up"], mode="nn", name="attn_up", tm=1024, tn=512)
    ssm_d = _matmul(glu, wb["w_ssm_up"], mode="nn", name="ssm_up", tm=1024, tn=512)
    ga_cb, gs_cb = (QW + 2 * KVW + SW) // D, (QW + 2 * KVW + SW) // D + 1

    def mix(rv, vv):
        return [_f_mix(*rv)], []

    mixv, = _rowwise(mix, [(proj, D, ga_cb, 0), (proj, D, gs_cb, 0), (attn_d, D, 0, 0), (ssm_d, D, 0, 0)], [],
                     [(D, BF16)], [], nrows=T, tr=TR, name="mix")
    y = _matmul(mixv, wb["w_out"], mode="nn", name="out_proj", tm=1024, tn=512)

    def post1(rv, vv):
        x1, h2 = _f_post1(rv[0], rv[1], *vv)
        return [x1, h2], []

    x1, h2 = _rowwise(post1, [(x, D, 0, 0), (y, D, 0, 0)], [g1, lg_mix, lb_mix, sc2, sh2],
                      [(D, F32), (D, BF16)], [], nrows=T, tr=TR, name="post1")

    def relu_sq(acc):
        r = jnp.maximum(acc, 0.0)
        return r, r * r

    r_act, act = _matmul(h2, wb["w_mlp1"], mode="nn", name="mlp1", tm=1024, tn=512, bias=b1,
                         out_dtypes=(BF16, BF16), epilogue=relu_sq)
    mlp = _matmul(act, wb["w_mlp2"], mode="nn", name="mlp2", tm=1024, tn=512, tk=2048)

    def loss_fb(rv, vv):
        x1_t, mlp_t, tgt_t = rv
        g2_v, lg_v, lb_v, b2_v = vv
        f = lambda a, m, g, p, q, b: _f_loss(a, m, tgt_t, g, p, q, b)
        val, grads = jax.value_and_grad(f, argnums=(0, 1, 2, 3, 4, 5))(x1_t, mlp_t, g2_v, lg_v, lb_v, b2_v)
        dx1, dmlp, dg2, dlg, dlb, db2 = grads
        return [dx1, dmlp], [jnp.reshape(val, (1, 1)), dg2, dlg, dlb, db2]

    dx1_a, d_mlp, loss_p, d_g2, d_lg_mlp, d_lb_mlp, d_b2 = _rowwise(
        loss_fb, [(x1, D, 0, 0), (mlp, D, 0, 0), (tgt, D, 0, 0)], [g2, lg_mlp, lb_mlp, b2],
        [(D, F32), (D, BF16)], [(1, 1), (1, D), (1, D), (1, D), (1, D)], nrows=T, tr=TR, name="loss_fb")

    gw = {}
    gw["w_mlp2"] = _matmul(act, d_mlp, mode="tn", name="dw_mlp2", out_dtypes=(BF16,), tm=512, tn=1024, tk=1024)
    da, = (_matmul(d_mlp, wb["w_mlp2"], mode="nt", name="d_act", out_dtypes=(BF16,), tm=1024, tn=512,
                   extras=(r_act,), epilogue=lambda acc, r: (acc * (2.0 * r.astype(F32)),)),)
    ones = jnp.ones((8, T), BF16)
    d_b1 = _matmul(ones, da, mode="nn", name="db_mlp1", tm=8, tn=2048)[0:1]
    gw["w_mlp1"] = _matmul(h2, da, mode="tn", name="dw_mlp1", out_dtypes=(BF16,), tm=512, tn=1024, tk=1024)
    dh2 = _matmul(da, wb["w_mlp1"], mode="nt", name="d_h2", tm=1024, tn=512, tk=2048)

    def post1_b(rv, vv):
        x_t, y_t, dx1_t, dh2_t = rv
        gr, gv = _vjp_rows(_f_post1, (x_t, y_t, *vv), (dx1_t, dh2_t), 2)
        return [gr[0], gr[1]], gv

    dx_a, dy, d_g1, d_lg_mix, d_lb_mix, d_sc2, d_sh2 = _rowwise(
        post1_b, [(x, D, 0, 0), (y, D, 0, 0), (dx1_a, D, 0, 0), (dh2, D, 0, 0)], [g1, lg_mix, lb_mix, sc2, sh2],
        [(D, F32), (D, BF16)], [(1, D)] * 5, nrows=T, tr=TR, name="post1_bwd")
    gw["w_out"] = _matmul(mixv, dy, mode="tn", name="dw_out", out_dtypes=(BF16,), tm=512, tn=1024, tk=1024)
    dmix = _matmul(dy, wb["w_out"], mode="nt", name="d_mix", tm=1024, tn=512)

    def mix_b(rv, vv):
        gr, _ = _vjp_rows(_f_mix, tuple(rv[:4]), rv[4], 4)
        return gr, []

    d_ga, d_gs, d_attn_d, d_ssm_d = _rowwise(
        mix_b, [(proj, D, ga_cb, 0), (proj, D, gs_cb, 0), (attn_d, D, 0, 0), (ssm_d, D, 0, 0), (dmix, D, 0, 0)], [],
        [(D, BF16)] * 4, [], nrows=T, tr=TR, name="mix_bwd")
    gw["w_attn_up"] = _matmul(attn, d_attn_d, mode="tn", name="dw_attn_up", out_dtypes=(BF16,), tm=512, tn=1024, tk=1024)
    d_attn = _matmul(d_attn_d, wb["w_attn_up"], mode="nt", name="d_attn", out_dtypes=(BF16,), tm=1024, tn=512)
    gw["w_ssm_up"] = _matmul(glu, d_ssm_d, mode="tn", name="dw_ssm_up", out_dtypes=(BF16,), tm=512, tn=1024, tk=1024)
    d_glu = _matmul(d_ssm_d, wb["w_ssm_up"], mode="nt", name="d_glu", tm=1024, tn=512)

    def glu_b(rv, vv):
        gr, _ = _vjp_rows(_f_glu, (rv[0],), rv[1], 1)
        return gr, []

    dz, = _rowwise(glu_b, [(z, 2 * SW, 0, 0), (d_glu, SW, 0, 0)], [], [(2 * SW, BF16)], [], nrows=T, tr=TR, name="glu_bwd")
    gw["w_glu"] = _matmul(ge, dz, mode="tn", name="dw_glu", out_dtypes=(BF16,), tm=512, tn=1024, tk=1024)
    d_ge = _matmul(dz, wb["w_glu"], mode="nt", name="d_ge", tm=1024, tn=512)

    def ssm_pre_b(rv, vv):
        u_t, yf_t, yb_t, dge_t = rv
        f = lambda u, yf, yb, dk: _gelu(_f_ssm_pre(u, yf, yb, dk))
        gr, gv = _vjp_rows(f, (u_t, yf_t, yb_t, vv[0]), dge_t, 3)
        return [gr[0], gr[1]], gv

    du_dir, d_ssm, d_dskip = _rowwise(
        ssm_pre_b, [(u_lat, SW, 0, 0), (y_f, SW, 0, 0), (y_b, SW, 0, 0), (d_ge, SW, 0, 0)], [dskip],
        [(SW, F32), (SW, F32)], [(1, SW)], nrows=T, tr=TR, name="ssm_pre_bwd")
    dy_seq = _to_seq(d_ssm, jnp.zeros((C, SW), F32))
    du_seq, dlam, dbmat, dcmat = _s5_bwd(dy_seq, states, u_seq, lam, bmat, cmat)
    du_f, duc_f, du_b, duc_b = _from_seq(du_seq)
    du_all = jnp.concatenate([du_dir + du_f + du_b, duc_f + duc_b], 0).astype(BF16)
    s5_grads = s5_pull((dlam, dbmat, dcmat))

    dq, dk, dv, dsink = _attn_bwd(proj, d_attn, sink, tabs)
    zc = lambda w: jnp.zeros((C, w), BF16)
    dproj = jnp.concatenate([
        jnp.concatenate([dq, zc(QW)], 0), dk, dv, du_all,
        jnp.concatenate([d_ga, zc(D)], 0), jnp.concatenate([d_gs, zc(D)], 0)], 1)
    gw["w_in"] = _matmul(h1, dproj, mode="tn", name="dw_in", out_dtypes=(BF16,), tm=512, tn=1536, tk=768)
    dh1 = _matmul(dproj, wb["w_in"], mode="nt", name="d_h1", tm=768, tn=512, tk=2048)

    def ln1_b(rv, vv):
        x_t, dh_t, dxa_t = rv
        gr, gv = _vjp_rows(_f_ln_mod, (x_t, vv[0], vv[1]), dh_t, 1)
        return [gr[0] + dxa_t], gv

    grad_x, d_sc1, d_sh1 = _rowwise(ln1_b, [(x, D, 0, 0), (dh1, D, 0, 0), (dx_a, D, 0, 0)], [sc1, sh1],
                                    [(D, F32)], [(1, D), (1, D)], nrows=T, tr=TR, name="ln1_lat_bwd")

    def ln1c_b(rv, vv):
        _, gv = _vjp_rows(_f_ln_mod, (rv[0], vv[0], vv[1]), rv[1], 1)
        return [], gv

    d_csc1, d_csh1 = _rowwise(ln1c_b, [(ctx, D, 0, 0), (dh1, D, 0, T // TR)], [csc1, csh1],
                              [], [(1, D), (1, D)], nrows=C, tr=TR, name="ln1_ctx_bwd")

    d_mod_lat = jnp.concatenate([d_sh1, d_sc1, d_g1, d_sh2, d_sc2, d_g2], 1)
    zv = jnp.zeros((1, D), F32)
    d_mod_ctx = jnp.concatenate([d_csh1, d_csc1, zv, zv, zv, zv], 1)
    gs = {n: g for n, g in zip(s5_names, s5_grads)}
    gs["attn_sink"] = dsink[:, 0]
    gs["ssm_d"] = d_dskip
    gs["ln_mix_g"], gs["ln_mix_b"] = d_lg_mix, d_lb_mix
    gs["ln_mlp_g"], gs["ln_mlp_b"] = d_lg_mlp, d_lb_mlp
    gs["b_mlp1"], gs["b_mlp2"] = d_b1, d_b2
    return loss_p, grad_x, d_mod_lat, d_mod_ctx, gw, gs


def _my_pos():
    return lax.axis_index("x"), lax.axis_index("y"), lax.axis_index("c")


def _flip(p, bit):
    return 1 - p if bit else p


def _peer(pos, k):
    x, y, c = pos
    return (_flip(x, (k >> 2) & 1), _flip(y, (k >> 1) & 1), _flip(c, k & 1))


def _lin(pos):
    return 4 * pos[0] + 2 * pos[1] + pos[2]


def _allgather_small(v, name):
    r, w = v.shape

    def body(v_ref, out_ref, send_sems, recv_sems, local_sem):
        me = _my_pos()
        mine = pltpu.make_async_copy(v_ref, out_ref.at[_lin(me)], local_sem)
        mine.start()
        sends = []
        for k in range(1, N_DEV):
            cp = pltpu.make_async_remote_copy(src_ref=v_ref, dst_ref=out_ref.at[_lin(me)], send_sem=send_sems.at[k - 1],
                                              recv_sem=recv_sems.at[k - 1], device_id=_peer(me, k), device_id_type=MESH)
            cp.start()
            sends.append(cp)
        for k in range(1, N_DEV):
            peer = _peer(me, k)
            pltpu.make_async_remote_copy(src_ref=v_ref, dst_ref=out_ref.at[_lin(peer)], send_sem=send_sems.at[k - 1],
                                         recv_sem=recv_sems.at[k - 1], device_id=peer, device_id_type=MESH).wait_recv()
        for cp in sends:
            cp.wait_send()
        mine.wait()

    return pl.pallas_call(
        body,
        name=name,
        out_shape=jax.ShapeDtypeStruct((N_DEV, r, w), v.dtype),
        in_specs=[pl.BlockSpec(memory_space=pltpu.VMEM)],
        out_specs=pl.BlockSpec(memory_space=pltpu.VMEM),
        scratch_shapes=[pltpu.SemaphoreType.DMA((N_DEV - 1,)), pltpu.SemaphoreType.DMA((N_DEV - 1,)), pltpu.SemaphoreType.DMA],
        compiler_params=pltpu.CompilerParams(vmem_limit_bytes=VMEM_LIMIT_BYTES),
    )(v)


def _block_of(ref, kind, idx, n):
    start = pl.multiple_of(idx * n, 128)
    if kind == "col":
        return ref.at[:, pl.ds(start, n)]
    return ref.at[pl.ds(start, n), :]


def _allgather_weights(shards, kinds):
    nt = len(shards)
    out_shape = []
    for s, kind in zip(shards, kinds):
        k, n = s.shape
        out_shape.append(jax.ShapeDtypeStruct((k, n * N_DEV) if kind == "col" else (k * N_DEV, n), s.dtype))

    def body(*refs):
        ins, outs = refs[:nt], refs[nt:2 * nt]
        send_sems, recv_sems, local_sems = refs[2 * nt:]
        x, y, c = _my_pos()
        me, sibling = (x, y, c), (x, y, 1 - c)
        chips = [(1 - x, y), (x, 1 - y), (1 - x, 1 - y)]

        def blk(t, pos):
            n = shards[t].shape[1] if kinds[t] == "col" else shards[t].shape[0]
            return _block_of(outs[t], kinds[t], _lin(pos), n)

        def copy(t, k, block, to, src=None):
            return pltpu.make_async_remote_copy(src_ref=blk(t, block) if src is None else src, dst_ref=blk(t, block),
                                                send_sem=send_sems.at[t, k], recv_sem=recv_sems.at[t, k],
                                                device_id=to, device_id_type=MESH)

        local, sends = [], []
        for t in range(nt):
            mine = pltpu.make_async_copy(ins[t], blk(t, me), local_sems.at[t])
            mine.start()
            local.append(mine)
            first = [copy(t, 0, me, sibling, src=ins[t])]
            first += [copy(t, 1 + j, me, (*chip, c), src=ins[t]) for j, chip in enumerate(chips)]
            for cp in first:
                cp.start()
            sends += first
        for t in range(nt):
            for j, chip in enumerate(chips):
                copy(t, 1 + j, (*chip, c), me).wait_recv()
                fwd = copy(t, 4 + j, (*chip, c), sibling)
                fwd.start()
                sends.append(fwd)
        for t in range(nt):
            copy(t, 0, sibling, me).wait_recv()
            for j, chip in enumerate(chips):
                copy(t, 4 + j, (*chip, 1 - c), me).wait_recv()
        for cp in sends:
            cp.wait_send()
        for cp in local:
            cp.wait()

    any_spec = pl.BlockSpec(memory_space=pl.ANY)
    return pl.pallas_call(
        body,
        name="allgather_weights",
        out_shape=out_shape,
        in_specs=[any_spec] * nt,
        out_specs=[any_spec] * nt,
        scratch_shapes=[pltpu.SemaphoreType.DMA((nt, N_DEV - 1)), pltpu.SemaphoreType.DMA((nt, N_DEV - 1)),
                        pltpu.SemaphoreType.DMA((nt,))],
    )(*shards)


def _scatter_grads(grads, kinds):
    nt = len(grads)
    shard_shapes = []
    for g, kind in zip(grads, kinds):
        k, n = g.shape
        shard_shapes.append((k, n // N_DEV) if kind == "col" else (k // N_DEV, n))

    def body(*refs):
        ins, outs = refs[:nt], refs[nt:2 * nt]
        send_sems, recv_sems, local_sems = refs[2 * nt:]
        me = _my_pos()

        def blk(t, pos):
            n = shard_shapes[t][1] if kinds[t] == "col" else shard_shapes[t][0]
            return _block_of(ins[t], kinds[t], _lin(pos), n)

        local, sends = [], []
        for t in range(nt):
            cp = pltpu.make_async_copy(blk(t, me), outs[t].at[_lin(me)], local_sems.at[t])
            cp.start()
            local.append(cp)
            for k in range(1, N_DEV):
                peer = _peer(me, k)
                cp = pltpu.make_async_remote_copy(src_ref=blk(t, peer), dst_ref=outs[t].at[_lin(me)], send_sem=send_sems.at[t, k - 1],
                                                  recv_sem=recv_sems.at[t, k - 1], device_id=peer, device_id_type=MESH)
                cp.start()
                sends.append(cp)
        for t in range(nt):
            for k in range(1, N_DEV):
                peer = _peer(me, k)
                pltpu.make_async_remote_copy(src_ref=blk(t, me), dst_ref=outs[t].at[_lin(peer)], send_sem=send_sems.at[t, k - 1],
                                             recv_sem=recv_sems.at[t, k - 1], device_id=peer, device_id_type=MESH).wait_recv()
        for cp in sends:
            cp.wait_send()
        for cp in local:
            cp.wait()

    any_spec = pl.BlockSpec(memory_space=pl.ANY)
    return pl.pallas_call(
        body,
        name="scatter_grads",
        out_shape=[jax.ShapeDtypeStruct((N_DEV,) + s, g.dtype) for s, g in zip(shard_shapes, grads)],
        in_specs=[any_spec] * nt,
        out_specs=[any_spec] * nt,
        scratch_shapes=[pltpu.SemaphoreType.DMA((nt, N_DEV - 1)), pltpu.SemaphoreType.DMA((nt, N_DEV - 1)),
                        pltpu.SemaphoreType.DMA((nt,))],
    )(*grads)


def _adam(g_slots, w, m, v, *, tr, name):
    ns, r, wd = g_slots.shape
    tr = min(tr, r)
    assert r % tr == 0, (name, r, tr)
    c1 = 1.0 - ADAM_B1 ** ADAM_STEP
    c2 = 1.0 - ADAM_B2 ** ADAM_STEP

    def kern(g_ref, w_ref, m_ref, v_ref, go_ref, d_ref, mo_ref, vo_ref):
        g = g_ref[0].astype(F32)
        for s in range(1, ns):
            g = g + g_ref[s].astype(F32)
        m_new = ADAM_B1 * m_ref[...] + (1.0 - ADAM_B1) * g
        v_new = ADAM_B2 * v_ref[...] + (1.0 - ADAM_B2) * (g * g)
        m_hat = m_new / c1
        v_hat = v_new / c2
        go_ref[...] = g
        d_ref[...] = -ADAM_LR * (m_hat / (jnp.sqrt(v_hat) + ADAM_EPS) + ADAM_WD * w_ref[...])
        mo_ref[...] = m_new
        vo_ref[...] = v_new

    tile = pl.BlockSpec((tr, wd), lambda i: (i, 0))
    return pl.pallas_call(
        kern,
        name=name,
        grid=(r // tr,),
        in_specs=[pl.BlockSpec((ns, tr, wd), lambda i: (0, i, 0)), tile, tile, tile],
        out_specs=[tile] * 4,
        out_shape=[jax.ShapeDtypeStruct((r, wd), F32)] * 4,
        compiler_params=_cparams(("parallel",)),
    )(g_slots, w, m, v)


SMALL = ("c_ctx", "b_ada", "attn_sink", "ssm_a_re", "ssm_a_im", "ssm_log_dt", "ssm_b_re", "ssm_b_im", "ssm_c_re", "ssm_c_im",
         "ssm_d", "ln_mix_g", "ln_mix_b", "b_mlp1", "b_mlp2", "ln_mlp_g", "ln_mlp_b")
BIG = ("w_in", "w_glu", "w_attn_up", "w_ssm_up", "w_out", "w_mlp1", "w_mlp2")
BIG_KIND = ("col", "col", "col", "col", "row", "col", "row")
LANES = 128


def _pack(parts):
    rows = []
    for p in parts:
        flat = p.reshape(-1).astype(F32)
        pad = (-flat.shape[0]) % LANES
        rows.append(jnp.pad(flat, (0, pad)).reshape(-1, LANES))
    packed = jnp.concatenate(rows, 0)
    return jnp.pad(packed, ((0, (-packed.shape[0]) % 8), (0, 0)))


def _unpack(packed, shapes):
    out, r0 = [], 0
    for s in shapes:
        n = math.prod(s)
        nr = -(-n // LANES)
        out.append(packed[r0:r0 + nr].reshape(-1)[:n].reshape(s))
        r0 += nr
    return out


WEIGHTS = ("c_ctx", "w_ada", "b_ada", "w_in", "attn_sink", "ssm_a_re", "ssm_a_im", "ssm_log_dt", "ssm_b_re", "ssm_b_im",
           "ssm_c_re", "ssm_c_im", "ssm_d", "w_glu", "w_attn_up", "w_ssm_up", "w_out", "ln_mix_g", "ln_mix_b", "w_mlp1",
           "b_mlp1", "w_mlp2", "b_mlp2", "ln_mlp_g", "ln_mlp_b")
ADA_COLS = 6 * D // N_DEV


def _step(x, c, ctx, loss_target, p, m, v):
    me = _lin(_my_pos())
    x2, ctx2, tgt2 = x[0], ctx[0], loss_target[0]

    full = _allgather_weights([p[n][0].astype(BF16) for n in BIG], BIG_KIND)
    wb = dict(zip(BIG, full))

    c_all = _allgather_small(jnp.broadcast_to(c, (8, D)), "gather_c")[:, 0, :]
    cc = p["c_ctx"].reshape(1, D)
    s_in = jnp.concatenate([c_all, cc, jnp.zeros((7, D), F32)], 0)
    s_act, = _rowwise(lambda rv, vv: ([_silu(rv[0])], []), [(s_in, D, 0, 0)], [], [(D, F32)], [], nrows=16, tr=16, name="silu_c")
    b_mine = lax.dynamic_slice_in_dim(p["b_ada"], me * ADA_COLS, ADA_COLS, axis=1)
    mod_part = _matmul(s_act, p["w_ada"][0], mode="nn", name="ada_fwd", tm=16, tn=512, bias=b_mine)
    mod_all = _allgather_small(mod_part, "gather_mod")
    mod_lat = lax.dynamic_index_in_dim(mod_all, me, axis=1, keepdims=False).reshape(1, 6 * D)
    mod_ctx = mod_all[:, 8, :].reshape(1, 6 * D)

    sp = {n: p[n][0] for n in SMALL if n not in ("c_ctx", "b_ada")}
    loss_p, grad_x, d_mod_lat, d_mod_ctx, gw, gs = _local_step(x2, ctx2, tgt2, mod_lat, mod_ctx, wb, sp)

    dm = jnp.concatenate([d_mod_lat, d_mod_ctx, jnp.zeros((6, 6 * D), F32)], 0)
    dm_all = _allgather_small(dm, "gather_dmod")
    dm2 = jnp.concatenate([dm_all[:, 0, :], dm_all[:, 1, :]], 0)
    dm2_mine = lax.dynamic_slice_in_dim(dm2, me * ADA_COLS, ADA_COLS, axis=1)
    s2 = jnp.concatenate([s_act[0:8], jnp.broadcast_to(s_act[8:9], (8, D))], 0)
    g_w_ada = _matmul(s2, dm2_mine, mode="tn", name="dw_ada", tm=512, tn=ADA_COLS)
    dsc_part = _matmul(dm2_mine[8:16], p["w_ada"][0], mode="nt", name="d_silu_cctx", tm=8, tn=512)

    def cctx_b(rv, vv):
        _, pull = jax.vjp(_silu, vv[0])
        return [], [pull(jnp.sum(rv[0], axis=0, keepdims=True))[0]]

    g_cctx, = _rowwise(cctx_b, [(dsc_part, D, 0, 0)], [cc], [], [(1, D)], nrows=8, tr=8, name="cctx_bwd")
    gs["c_ctx"] = g_cctx
    gs["b_ada"] = d_mod_lat + d_mod_ctx

    recv = _scatter_grads([gw[n] for n in BIG], BIG_KIND)
    res = {}
    for n, slots in zip(BIG, recv):
        res[n] = _adam(slots, p[n][0], m[n][0], v[n][0], tr=256, name="adam_" + n)
    res["w_ada"] = _adam(g_w_ada[None], p["w_ada"][0], m["w_ada"][0], v["w_ada"][0], tr=256, name="adam_w_ada")

    g_pack = _allgather_small(_pack([gs[n] for n in SMALL]), "gather_small_grads")
    sm = _adam(g_pack, _pack([p[n] for n in SMALL]), _pack([m[n] for n in SMALL]), _pack([v[n] for n in SMALL]),
               tr=g_pack.shape[1], name="adam_small")
    shapes = [p[n].shape for n in SMALL]
    for j, outs in enumerate(zip(*[_unpack(a, shapes) for a in sm])):
        res[SMALL[j]] = outs

    loss = lax.psum(loss_p[0, 0], ("x", "y", "c"))
    outs = [loss, grad_x[None]]
    for j in range(4):
        outs += [res[n][j].reshape(p[n].shape) for n in WEIGHTS]
    return tuple(outs)


def kernel(x, c, ctx, c_ctx, w_ada, b_ada, w_in, attn_sink, ssm_a_re, ssm_a_im, ssm_log_dt, ssm_b_re, ssm_b_im, ssm_c_re, ssm_c_im, ssm_d, w_glu, w_attn_up, w_ssm_up, w_out, ln_mix_g, ln_mix_b, w_mlp1, b_mlp1, w_mlp2, b_mlp2, ln_mlp_g, ln_mlp_b, loss_target, m_c_ctx, m_w_ada, m_b_ada, m_w_in, m_attn_sink, m_ssm_a_re, m_ssm_a_im, m_ssm_log_dt, m_ssm_b_re, m_ssm_b_im, m_ssm_c_re, m_ssm_c_im, m_ssm_d, m_w_glu, m_w_attn_up, m_w_ssm_up, m_w_out, m_ln_mix_g, m_ln_mix_b, m_w_mlp1, m_b_mlp1, m_w_mlp2, m_b_mlp2, m_ln_mlp_g, m_ln_mlp_b, v_c_ctx, v_w_ada, v_b_ada, v_w_in, v_attn_sink, v_ssm_a_re, v_ssm_a_im, v_ssm_log_dt, v_ssm_b_re, v_ssm_b_im, v_ssm_c_re, v_ssm_c_im, v_ssm_d, v_w_glu, v_w_attn_up, v_w_ssm_up, v_w_out, v_ln_mix_g, v_ln_mix_b, v_w_mlp1, v_b_mlp1, v_w_mlp2, v_b_mlp2, v_ln_mlp_g, v_ln_mlp_b):
    given = dict(locals())
    p = {n: given[n] for n in WEIGHTS}
    m = {n: given["m_" + n] for n in WEIGHTS}
    v = {n: given["v_" + n] for n in WEIGHTS}
    return _step(x, c, ctx, loss_target, p, m, v)
```

```python
import functools
import math

import jax
import jax.numpy as jnp
from jax import lax
from jax.experimental import pallas as pl
from jax.experimental.pallas import tpu as pltpu
from jax.experimental.pallas import tpu_sc as plsc

F32 = jnp.float32
BF16 = jnp.bfloat16

N_DEV = 8
D = 2048
T = 2048
C = 256
TA = T + C
GRID_W = 64
HD = 128
NH = 8
NKV = 2
GROUP = NH // NKV
WINDOW = 128
QW = NH * HD
KVW = NKV * HD
SW = D // 4
SG = 16
NG = SW // SG
SP = 64
DFF = 4 * D
IN_COLS = QW + 2 * KVW + SW + 2 * D
ALPHA = 2.0 ** 0.25
LN_EPS = 1e-6
NEG_INF = -1e30
ROPE_BASE = 10000.0
ATT_SCALE = HD ** -0.5

NSEG = 8
SEGLEN = TA // NSEG
GBLK = 8
NBLK = NG // GBLK
BW = GBLK * SP
UW = GBLK * SG

ADAM_LR = 0.001
ADAM_B1 = 0.9
ADAM_B2 = 0.999
ADAM_EPS = 1e-08
ADAM_WD = 0.01
ADAM_STEP = 10

VMEM_LIMIT_BYTES = 56 * 1024 * 1024
MESH = pl.DeviceIdType.MESH


def _cparams(sem=None):
    return pltpu.CompilerParams(dimension_semantics=sem, vmem_limit_bytes=VMEM_LIMIT_BYTES)


def _matmul(a, b, *, mode, name, out_dtypes=(F32,), tm=512, tn=512, tk=None, bias=None, extras=(), epilogue=None):
    if mode == "nn":
        (M, K), (K2, N) = a.shape, b.shape
    elif mode == "nt":
        (M, K), (N, K2) = a.shape, b.shape
    else:
        (K, M), (K2, N) = a.shape, b.shape
    assert K == K2, (name, a.shape, b.shape)
    tm, tn, tk = min(tm, M), min(tn, N), min(tk or K, K)
    assert M % tm == 0 and N % tn == 0 and K % tk == 0, (name, M, N, K, tm, tn, tk)
    nk = K // tk
    if mode == "tn":
        a_spec = pl.BlockSpec((tk, tm), lambda i, j, k: (k, i))
    else:
        a_spec = pl.BlockSpec((tm, tk), lambda i, j, k: (i, k))
    if mode == "nt":
        b_spec = pl.BlockSpec((tn, tk), lambda i, j, k: (j, k))
    else:
        b_spec = pl.BlockSpec((tk, tn), lambda i, j, k: (k, j))
    dims = {"nn": (((1,), (0,)), ((), ())), "nt": (((1,), (1,)), ((), ())), "tn": (((0,), (0,)), ((), ()))}[mode]
    in_specs = [a_spec, b_spec]
    operands = [a, b]
    if bias is not None:
        in_specs.append(pl.BlockSpec((1, tn), lambda i, j, k: (0, j)))
        operands.append(bias)
    for e in extras:
        in_specs.append(pl.BlockSpec((tm, tn), lambda i, j, k: (i, j)))
        operands.append(e)
    n_ex = len(extras)
    n_out = len(out_dtypes)
    has_bias = bias is not None

    def kern(*refs):
        a_ref, b_ref = refs[0], refs[1]
        pos = 2
        bias_ref = None
        if has_bias:
            bias_ref = refs[pos]
            pos += 1
        ex_refs = refs[pos:pos + n_ex]
        pos += n_ex
        out_refs = refs[pos:pos + n_out]
        acc_ref = refs[pos + n_out] if nk > 1 else None

        def finish(r):
            if has_bias:
                r = r + bias_ref[...]
            outs = epilogue(r, *[e[...] for e in ex_refs]) if epilogue is not None else (r,)
            for o_ref, o in zip(out_refs, outs):
                o_ref[...] = o.astype(o_ref.dtype)

        part = lax.dot_general(a_ref[...].astype(BF16), b_ref[...].astype(BF16), dims, preferred_element_type=F32)
        if nk == 1:
            finish(part)
        else:
            k = pl.program_id(2)

            @pl.when(k == 0)
            def _():
                acc_ref[...] = part

            @pl.when(k > 0)
            def _():
                acc_ref[...] += part

            @pl.when(k == nk - 1)
            def _():
                finish(acc_ref[...])

    outs = pl.pallas_call(
        kern,
        name=name,
        grid=(M // tm, N // tn, nk),
        in_specs=in_specs,
        out_specs=[pl.BlockSpec((tm, tn), lambda i, j, k: (i, j)) for _ in out_dtypes],
        out_shape=[jax.ShapeDtypeStruct((M, N), dt) for dt in out_dtypes],
        scratch_shapes=[pltpu.VMEM((tm, tn), F32)] if nk > 1 else [],
        compiler_params=_cparams(("parallel", "parallel", "arbitrary")),
    )(*operands)
    return outs[0] if n_out == 1 else tuple(outs)


def _rowwise(fn, rows, vecs, outs, vec_outs, *, nrows, tr, name):
    n_rows, n_vecs, n_outs = len(rows), len(vecs), len(outs)
    in_specs = [pl.BlockSpec((tr, w), lambda i, cb=cb, ro=ro: (i + ro, cb)) for (_, w, cb, ro) in rows]
    in_specs += [pl.BlockSpec(v.shape, lambda i: (0, 0)) for v in vecs]
    out_specs = [pl.BlockSpec((tr, w), lambda i: (i, 0)) for (w, _) in outs]
    out_specs += [pl.BlockSpec(s, lambda i: (0, 0)) for s in vec_outs]
    out_shape = [jax.ShapeDtypeStruct((nrows, w), dt) for (w, dt) in outs]
    out_shape += [jax.ShapeDtypeStruct(s, F32) for s in vec_outs]

    def kern(*refs):
        rvals = [r[...] for r in refs[:n_rows]]
        vvals = [r[...] for r in refs[n_rows:n_rows + n_vecs]]
        o_refs = refs[n_rows + n_vecs:n_rows + n_vecs + n_outs]
        v_refs = refs[n_rows + n_vecs + n_outs:]
        ro, vo = fn(rvals, vvals)
        for r, val in zip(o_refs, ro):
            r[...] = val.astype(r.dtype)
        i = pl.program_id(0)
        for r, val in zip(v_refs, vo):
            @pl.when(i == 0)
            def _(r=r, val=val):
                r[...] = val.astype(F32)

            @pl.when(i > 0)
            def _(r=r, val=val):
                r[...] += val.astype(F32)

    res = pl.pallas_call(
        kern,
        name=name,
        grid=(nrows // tr,),
        in_specs=in_specs,
        out_specs=out_specs,
        out_shape=out_shape,
        compiler_params=_cparams(("arbitrary",)),
    )(*[r[0] for r in rows], *vecs)
    return list(res)


def _ln(x):
    mu = jnp.mean(x, axis=-1, keepdims=True)
    xc = x - mu
    var = jnp.mean(xc * xc, axis=-1, keepdims=True)
    return xc * lax.rsqrt(var + LN_EPS)


def _sigmoid(x):
    return 1.0 / (1.0 + jnp.exp(-x))


def _gelu(x):
    return 0.5 * x * (1.0 + jnp.tanh(math.sqrt(2.0 / math.pi) * (x + 0.044715 * (x * x * x))))


def _silu(x):
    return x * _sigmoid(x)


def _f_ln_mod(x, sc, sh):
    return _ln(x) * (1.0 + sc) + sh


def _f_ssm_pre(u, yf, yb, dskip):
    return dskip * u + yf + yb


def _f_glu(z):
    return z[:, :SW] * _sigmoid(z[:, SW:])


def _f_mix(ga, gs, attn_d, ssm_d):
    return _sigmoid(ga) * attn_d + _sigmoid(gs) * ssm_d


def _f_post1(x, y, g1, lg, lb, sc2, sh2):
    r1 = ALPHA * x + g1 * y
    x1 = _ln(r1) * lg + lb
    h2 = _ln(x1) * (1.0 + sc2) + sh2
    return x1, h2


def _f_loss(x1, mlp, tgt, g2, lg, lb, b2z):
    r2 = ALPHA * x1 + g2 * (mlp + b2z)
    out = _ln(r2) * lg + lb
    err = out - tgt
    return 0.5 * jnp.sum(err * err) * (1.0 / D)


def _rope_tables():
    rows = T // GRID_W
    row = jnp.repeat(jnp.arange(rows), GRID_W)
    col = jnp.tile(jnp.arange(GRID_W), rows)
    n_freq = HD // 4
    freqs = ROPE_BASE ** (-jnp.arange(n_freq, dtype=F32) / n_freq)
    ang_r = row.astype(F32)[:, None] * freqs
    ang_c = col.astype(F32)[:, None] * freqs
    ang = jnp.concatenate([ang_r, ang_r, ang_c, ang_c], -1)
    cos, sin = jnp.cos(ang), jnp.sin(ang)
    lo = (jnp.arange(HD) % (HD // 2)) < (HD // 4)
    sin_a = jnp.where(lo[None, :], -sin, 0.0)
    sin_b = jnp.where(lo[None, :], 0.0, sin)
    return cos, sin_a, sin_b


def _rope(x, cos, sa, sb):
    return x * cos + pltpu.roll(x, 96, 1) * sa + pltpu.roll(x, 32, 1) * sb


def _rope_t(dy, cos, sa, sb):
    return dy * cos + pltpu.roll(dy * sa, 32, 1) + pltpu.roll(dy * sb, 96, 1)


BAND = 3 * WINDOW
KPAD = T + 2 * WINDOW


def _attn_fill_kv(k_ref, v_ref, cos_ref, sa_ref, sb_ref, kp, vp, kc, vc):
    zeros = jnp.zeros((WINDOW, KVW), BF16)
    kp[0:WINDOW, :] = zeros
    kp[WINDOW + T:KPAD, :] = zeros
    vp[0:WINDOW, :] = zeros
    vp[WINDOW + T:KPAD, :] = zeros
    for hh in range(NKV):
        cs = slice(hh * HD, (hh + 1) * HD)
        for r0 in range(0, T, 512):
            rs = slice(r0, r0 + 512)
            kr = _rope(k_ref[rs, cs], cos_ref[rs, :], sa_ref[rs, :], sb_ref[rs, :])
            kp[WINDOW + r0:WINDOW + r0 + 512, cs] = kr.astype(BF16)
    vp[WINDOW:WINDOW + T, :] = v_ref[0:T, :].astype(BF16)
    kc[...] = k_ref[T:TA, :].astype(BF16)
    vc[...] = v_ref[T:TA, :].astype(BF16)


def _attn_scores(n, h, q_ref, cos_ref, sa_ref, sb_ref, sink_ref, kp, kc):
    kvh = h // GROUP
    r0 = pl.multiple_of(n * WINDOW, WINDOW)
    cos = cos_ref[pl.ds(r0, WINDOW), :]
    sa = sa_ref[pl.ds(r0, WINDOW), :]
    sb = sb_ref[pl.ds(r0, WINDOW), :]
    q_h = _rope(q_ref[:, h * HD:(h + 1) * HD], cos, sa, sb).astype(BF16)
    kb = kp[pl.ds(r0, BAND), kvh * HD:(kvh + 1) * HD]
    kcb = kc[:, kvh * HD:(kvh + 1) * HD]
    nt = (((1,), (1,)), ((), ()))
    s_loc = lax.dot_general(q_h, kb, nt, preferred_element_type=F32) * ATT_SCALE
    s_ctx = lax.dot_general(q_h, kcb, nt, preferred_element_type=F32) * ATT_SCALE
    row = lax.broadcasted_iota(jnp.int32, (WINDOW, BAND), 0)
    col = lax.broadcasted_iota(jnp.int32, (WINDOW, BAND), 1)
    rel = col - WINDOW - row
    kpos = r0 - WINDOW + col
    valid = (jnp.abs(rel) <= WINDOW) & (kpos >= 0) & (kpos < T)
    s_loc = jnp.where(valid, s_loc, NEG_INF)
    sk = sink_ref[0:1, h:h + 1]
    m = jnp.maximum(jnp.maximum(jnp.max(s_loc, -1, keepdims=True), jnp.max(s_ctx, -1, keepdims=True)), sk)
    e_loc = jnp.exp(s_loc - m)
    e_ctx = jnp.exp(s_ctx - m)
    e_sink = jnp.exp(sk - m)
    inv = 1.0 / (jnp.sum(e_loc, -1, keepdims=True) + jnp.sum(e_ctx, -1, keepdims=True) + e_sink)
    return q_h, r0, e_loc * inv, e_ctx * inv, e_sink * inv


def _attn_fwd(proj, sink, tabs):
    cos, sa, sb = tabs

    def kern(q_ref, k_ref, v_ref, cos_ref, sa_ref, sb_ref, sink_ref, o_ref, kp, vp, kc, vc):
        n = pl.program_id(0)

        @pl.when(n == 0)
        def _():
            _attn_fill_kv(k_ref, v_ref, cos_ref, sa_ref, sb_ref, kp, vp, kc, vc)

        for h in range(NH):
            kvh = h // GROUP
            _, r0, p_loc, p_ctx, _ = _attn_scores(n, h, q_ref, cos_ref, sa_ref, sb_ref, sink_ref, kp, kc)
            vb = vp[pl.ds(r0, BAND), kvh * HD:(kvh + 1) * HD]
            vcb = vc[:, kvh * HD:(kvh + 1) * HD]
            o = jnp.dot(p_loc.astype(BF16), vb, preferred_element_type=F32)
            o = o + jnp.dot(p_ctx.astype(BF16), vcb, preferred_element_type=F32)
            o_ref[:, h * HD:(h + 1) * HD] = o.astype(o_ref.dtype)

    full = lambda shape: pl.BlockSpec(shape, lambda n: (0, 0))
    return pl.pallas_call(
        kern,
        name="attn_fwd",
        grid=(T // WINDOW,),
        in_specs=[
            pl.BlockSpec((WINDOW, QW), lambda n: (n, 0)),
            pl.BlockSpec((TA, KVW), lambda n: (0, QW // KVW)),
            pl.BlockSpec((TA, KVW), lambda n: (0, QW // KVW + 1)),
            full((T, HD)), full((T, HD)), full((T, HD)), full((1, NH)),
        ],
        out_specs=pl.BlockSpec((WINDOW, QW), lambda n: (n, 0)),
        out_shape=jax.ShapeDtypeStruct((T, QW), BF16),
        scratch_shapes=[pltpu.VMEM((KPAD, KVW), BF16), pltpu.VMEM((KPAD, KVW), BF16),
                        pltpu.VMEM((C, KVW), BF16), pltpu.VMEM((C, KVW), BF16)],
        compiler_params=_cparams(("arbitrary",)),
    )(proj, proj, proj, cos, sa, sb, sink)


def _attn_bwd(proj, d_attn, sink, tabs):
    cos, sa, sb = tabs
    n_blocks = T // WINDOW

    def kern(q_ref, k_ref, v_ref, do_ref, cos_ref, sa_ref, sb_ref, sink_ref,
             dq_ref, dk_ref, dv_ref, dsink_ref, kp, vp, kc, vc, dkp, dvp, dkc, dvc):
        n = pl.program_id(0)

        @pl.when(n == 0)
        def _():
            _attn_fill_kv(k_ref, v_ref, cos_ref, sa_ref, sb_ref, kp, vp, kc, vc)
            dkp[...] = jnp.zeros_like(dkp)
            dvp[...] = jnp.zeros_like(dvp)
            dkc[...] = jnp.zeros_like(dkc)
            dvc[...] = jnp.zeros_like(dvc)
            dsink_ref[...] = jnp.zeros_like(dsink_ref)

        nt = (((1,), (1,)), ((), ()))
        tn = (((0,), (0,)), ((), ()))
        for h in range(NH):
            kvh = h // GROUP
            cs = slice(kvh * HD, (kvh + 1) * HD)
            q_h, r0, p_loc, p_ctx, p_sink = _attn_scores(n, h, q_ref, cos_ref, sa_ref, sb_ref, sink_ref, kp, kc)
            kb = kp[pl.ds(r0, BAND), cs]
            vb = vp[pl.ds(r0, BAND), cs]
            kcb = kc[:, cs]
            vcb = vc[:, cs]
            do_h = do_ref[:, h * HD:(h + 1) * HD]
            dp_loc = lax.dot_general(do_h, vb, nt, preferred_element_type=F32)
            dp_ctx = lax.dot_general(do_h, vcb, nt, preferred_element_type=F32)
            delta = jnp.sum(p_loc * dp_loc, -1, keepdims=True) + jnp.sum(p_ctx * dp_ctx, -1, keepdims=True)
            ds_loc = (p_loc * (dp_loc - delta) * ATT_SCALE).astype(BF16)
            ds_ctx = (p_ctx * (dp_ctx - delta) * ATT_SCALE).astype(BF16)
            dq = jnp.dot(ds_loc, kb, preferred_element_type=F32) + jnp.dot(ds_ctx, kcb, preferred_element_type=F32)
            cos = cos_ref[pl.ds(r0, WINDOW), :]
            sa_ = sa_ref[pl.ds(r0, WINDOW), :]
            sb_ = sb_ref[pl.ds(r0, WINDOW), :]
            dq_ref[:, h * HD:(h + 1) * HD] = _rope_t(dq, cos, sa_, sb_).astype(dq_ref.dtype)
            dkp[pl.ds(r0, BAND), cs] += lax.dot_general(ds_loc, q_h, tn, preferred_element_type=F32)
            dkc[:, cs] += lax.dot_general(ds_ctx, q_h, tn, preferred_element_type=F32)
            dvp[pl.ds(r0, BAND), cs] += lax.dot_general(p_loc.astype(BF16), do_h, tn, preferred_element_type=F32)
            dvc[:, cs] += lax.dot_general(p_ctx.astype(BF16), do_h, tn, preferred_element_type=F32)
            dsk = -jnp.sum(p_sink * delta, axis=0, keepdims=True)
            dsink_ref[h:h + 1, :] += jnp.broadcast_to(dsk, (1, HD))

        @pl.when(n == n_blocks - 1)
        def _():
            for hh in range(NKV):
                cs = slice(hh * HD, (hh + 1) * HD)
                for r0 in range(0, T, 512):
                    rs = slice(r0, r0 + 512)
                    g = dkp[WINDOW + r0:WINDOW + r0 + 512, cs]
                    dk_ref[rs, cs] = _rope_t(g, cos_ref[rs, :], sa_ref[rs, :], sb_ref[rs, :]).astype(dk_ref.dtype)
            dk_ref[T:TA, :] = dkc[...].astype(dk_ref.dtype)
            dv_ref[0:T, :] = dvp[WINDOW:WINDOW + T, :].astype(dv_ref.dtype)
            dv_ref[T:TA, :] = dvc[...].astype(dv_ref.dtype)

    full = lambda shape: pl.BlockSpec(shape, lambda n: (0, 0))
    return pl.pallas_call(
        kern,
        name="attn_bwd",
        grid=(n_blocks,),
        in_specs=[
            pl.BlockSpec((WINDOW, QW), lambda n: (n, 0)),
            pl.BlockSpec((TA, KVW), lambda n: (0, QW // KVW)),
            pl.BlockSpec((TA, KVW), lambda n: (0, QW // KVW + 1)),
            pl.BlockSpec((WINDOW, QW), lambda n: (n, 0)),
            full((T, HD)), full((T, HD)), full((T, HD)), full((1, NH)),
        ],
        out_specs=[pl.BlockSpec((WINDOW, QW), lambda n: (n, 0)), full((TA, KVW)), full((TA, KVW)), full((NH, HD))],
        out_shape=[jax.ShapeDtypeStruct((T, QW), BF16), jax.ShapeDtypeStruct((TA, KVW), BF16),
                   jax.ShapeDtypeStruct((TA, KVW), BF16), jax.ShapeDtypeStruct((NH, HD), F32)],
        scratch_shapes=[pltpu.VMEM((KPAD, KVW), BF16), pltpu.VMEM((KPAD, KVW), BF16),
                        pltpu.VMEM((C, KVW), BF16), pltpu.VMEM((C, KVW), BF16),
                        pltpu.VMEM((KPAD, KVW), F32), pltpu.VMEM((KPAD, KVW), F32),
                        pltpu.VMEM((C, KVW), F32), pltpu.VMEM((C, KVW), F32)],
        compiler_params=_cparams(("arbitrary",)),
    )(proj, proj, proj, d_attn, cos, sa, sb, sink)


def _s5_prep(a_re, a_im, log_dt, b_re, b_im, c_re, c_im):
    lam = lax.complex(a_re, a_im)
    dt = jnp.exp(log_dt)[..., None]
    lam_bar = jnp.exp(lam * dt)
    b_bar = ((lam_bar - 1.0) / lam)[..., None] * lax.complex(b_re, b_im)
    eye = jnp.eye(GBLK, dtype=F32)

    def lam_rows(v):
        return v.reshape(2, NBLK, 1, BW)

    lam_l = jnp.concatenate([lam_rows(jnp.real(lam_bar)), lam_rows(jnp.imag(lam_bar))], -1)
    lam_l = jnp.broadcast_to(lam_l, (2, NBLK, 8, 2 * BW))

    def b_blocks(v):
        v = v.reshape(2, NBLK, GBLK, SP, SG).transpose(0, 1, 2, 4, 3)
        return (v[:, :, :, :, None, :] * eye[None, None, :, None, :, None]).reshape(2, NBLK, UW, BW)

    bmat = jnp.concatenate([b_blocks(jnp.real(b_bar)), b_blocks(jnp.imag(b_bar))], -1)

    def c_blocks(v):
        v = v.reshape(2, NBLK, GBLK, SG, SP).transpose(0, 1, 2, 4, 3)
        return (v[:, :, :, :, None, :] * eye[None, None, :, None, :, None]).reshape(2, NBLK, BW, UW)

    cmat = jnp.concatenate([c_blocks(c_re), -c_blocks(c_im)], 2)
    return lam_l, bmat, cmat


def _to_seq(lat, ctx):
    w = lat.shape[-1]
    f = jnp.concatenate([ctx, lat], 0)
    b = jnp.concatenate([ctx[::-1], lat[::-1]], 0)
    s = jnp.stack([f, b])
    return s.reshape(2, NSEG, SEGLEN, w).transpose(0, 2, 1, 3).reshape(2, TA, w)


def _from_seq(s):
    w = s.shape[-1]
    s = s.reshape(2, SEGLEN, NSEG, w).transpose(0, 2, 1, 3).reshape(2, TA, w)
    return s[0, C:], s[0, :C], s[1, C:][::-1], s[1, :C][::-1]


def _cmul(ar, ai, br, bi):
    return ar * br - ai * bi, ar * bi + ai * br


def _shift_rows(x, up):
    r = lax.broadcasted_iota(jnp.int32, x.shape, 0)
    if up:
        return jnp.where(r == NSEG - 1, 0.0, pltpu.roll(x, NSEG - 1, 0))
    return jnp.where(r == 0, 0.0, pltpu.roll(x, 1, 0))


RCH = 256


def _s5_fwd(u_seq, lam, bmat, cmat):
    def kern(u_ref, lam_ref, b_ref, c_ref, s_ref, y_ref):
        bm = b_ref[0, 0].astype(BF16)
        for r0 in range(0, TA, RCH):
            s_ref[0, 0, r0:r0 + RCH, :] = jnp.dot(u_ref[0, r0:r0 + RCH, :].astype(BF16), bm, preferred_element_type=F32)
        lr = lam_ref[0, 0, :, 0:BW]
        li = lam_ref[0, 0, :, BW:2 * BW]
        zero = jnp.zeros((NSEG, BW), F32)

        def scan1(j, carry):
            sr, si, pr, pi = carry
            row = pl.multiple_of(j * NSEG, NSEG)
            tr, ti = _cmul(lr, li, sr, si)
            sr = tr + s_ref[0, 0, pl.ds(row, NSEG), 0:BW]
            si = ti + s_ref[0, 0, pl.ds(row, NSEG), BW:2 * BW]
            s_ref[0, 0, pl.ds(row, NSEG), 0:BW] = sr
            s_ref[0, 0, pl.ds(row, NSEG), BW:2 * BW] = si
            pr, pi = _cmul(lr, li, pr, pi)
            return sr, si, pr, pi

        er, ei, lpr, lpi = lax.fori_loop(0, SEGLEN, scan1, (zero, zero, zero + 1.0, zero))
        cr, ci = zero, zero
        for _ in range(NSEG - 1):
            tr, ti = _cmul(lpr, lpi, cr, ci)
            cr, ci = _shift_rows(er + tr, False), _shift_rows(ei + ti, False)

        def scan2(j, carry):
            pr, pi = carry
            row = pl.multiple_of(j * NSEG, NSEG)
            tr, ti = _cmul(pr, pi, cr, ci)
            s_ref[0, 0, pl.ds(row, NSEG), 0:BW] += tr
            s_ref[0, 0, pl.ds(row, NSEG), BW:2 * BW] += ti
            return _cmul(lr, li, pr, pi)

        lax.fori_loop(0, SEGLEN, scan2, (lr, li))
        cm = c_ref[0, 0].astype(BF16)
        for r0 in range(0, TA, RCH):
            y_ref[0, r0:r0 + RCH, :] = jnp.dot(s_ref[0, 0, r0:r0 + RCH, :].astype(BF16), cm, preferred_element_type=F32)

    return pl.pallas_call(
        kern,
        name="s5_fwd",
        grid=(2, NBLK),
        in_specs=[
            pl.BlockSpec((1, TA, UW), lambda d, b: (d, 0, b)),
            pl.BlockSpec((1, 1, 8, 2 * BW), lambda d, b: (d, b, 0, 0)),
            pl.BlockSpec((1, 1, UW, 2 * BW), lambda d, b: (d, b, 0, 0)),
            pl.BlockSpec((1, 1, 2 * BW, UW), lambda d, b: (d, b, 0, 0)),
        ],
        out_specs=[pl.BlockSpec((1, 1, TA, 2 * BW), lambda d, b: (d, b, 0, 0)),
                   pl.BlockSpec((1, TA, UW), lambda d, b: (d, 0, b))],
        out_shape=[jax.ShapeDtypeStruct((2, NBLK, TA, 2 * BW), F32), jax.ShapeDtypeStruct((2, TA, SW), F32)],
        compiler_params=_cparams(("parallel", "parallel")),
    )(u_seq, lam, bmat, cmat)


def _s5_bwd(dy_seq, states, u_seq, lam, bmat, cmat):
    nt = (((1,), (1,)), ((), ()))
    tn = (((0,), (0,)), ((), ()))

    def kern(dy_ref, s_ref, u_ref, lam_ref, b_ref, c_ref, du_ref, dlam_ref, db_ref, dc_ref, g_ref):
        cm = c_ref[0, 0].astype(BF16)
        for r0 in range(0, TA, RCH):
            g_ref[r0:r0 + RCH, :] = lax.dot_general(dy_ref[0, r0:r0 + RCH, :].astype(BF16), cm, nt, preferred_element_type=F32)
        lr = lam_ref[0, 0, :, 0:BW]
        li = -lam_ref[0, 0, :, BW:2 * BW]
        zero = jnp.zeros((NSEG, BW), F32)

        def scan1(jj, carry):
            gr, gi, pr, pi = carry
            row = pl.multiple_of((SEGLEN - 1 - jj) * NSEG, NSEG)
            tr, ti = _cmul(lr, li, gr, gi)
            gr = tr + g_ref[pl.ds(row, NSEG), 0:BW]
            gi = ti + g_ref[pl.ds(row, NSEG), BW:2 * BW]
            g_ref[pl.ds(row, NSEG), 0:BW] = gr
            g_ref[pl.ds(row, NSEG), BW:2 * BW] = gi
            pr, pi = _cmul(lr, li, pr, pi)
            return gr, gi, pr, pi

        br, bi, lpr, lpi = lax.fori_loop(0, SEGLEN, scan1, (zero, zero, zero + 1.0, zero))
        cr, ci = zero, zero
        for _ in range(NSEG - 1):
            tr, ti = _cmul(lpr, lpi, cr, ci)
            cr, ci = _shift_rows(br + tr, True), _shift_rows(bi + ti, True)

        def dlam_terms(gr, gi, sr, si):
            return gr * sr + gi * si, gi * sr - gr * si

        def scan2(jj, carry):
            pr, pi, ar, ai = carry
            j = SEGLEN - 1 - jj
            row = pl.multiple_of(j * NSEG, NSEG)
            prev = pl.multiple_of((j - 1) * NSEG, NSEG)
            tr, ti = _cmul(pr, pi, cr, ci)
            gr = g_ref[pl.ds(row, NSEG), 0:BW] + tr
            gi = g_ref[pl.ds(row, NSEG), BW:2 * BW] + ti
            g_ref[pl.ds(row, NSEG), 0:BW] = gr
            g_ref[pl.ds(row, NSEG), BW:2 * BW] = gi
            dr, di = dlam_terms(gr, gi, s_ref[0, 0, pl.ds(prev, NSEG), 0:BW], s_ref[0, 0, pl.ds(prev, NSEG), BW:2 * BW])
            pr, pi = _cmul(lr, li, pr, pi)
            return pr, pi, ar + dr, ai + di

        pr, pi, ar, ai = lax.fori_loop(0, SEGLEN - 1, scan2, (lr, li, zero, zero))
        tr, ti = _cmul(pr, pi, cr, ci)
        gr = g_ref[0:NSEG, 0:BW] + tr
        gi = g_ref[0:NSEG, BW:2 * BW] + ti
        g_ref[0:NSEG, 0:BW] = gr
        g_ref[0:NSEG, BW:2 * BW] = gi
        last = (SEGLEN - 1) * NSEG
        dr, di = dlam_terms(gr, gi, _shift_rows(s_ref[0, 0, last:last + NSEG, 0:BW], False),
                            _shift_rows(s_ref[0, 0, last:last + NSEG, BW:2 * BW], False))
        dlam_ref[0, 0, :, 0:BW] = ar + dr
        dlam_ref[0, 0, :, BW:2 * BW] = ai + di

        bm = b_ref[0, 0].astype(BF16)
        db = jnp.zeros((UW, 2 * BW), F32)
        dc = jnp.zeros((2 * BW, UW), F32)
        for r0 in range(0, TA, RCH):
            g = g_ref[r0:r0 + RCH, :].astype(BF16)
            du_ref[0, r0:r0 + RCH, :] = lax.dot_general(g, bm, nt, preferred_element_type=F32)
            db = db + lax.dot_general(u_ref[0, r0:r0 + RCH, :].astype(BF16), g, tn, preferred_element_type=F32)
            dc = dc + lax.dot_general(s_ref[0, 0, r0:r0 + RCH, :].astype(BF16), dy_ref[0, r0:r0 + RCH, :].astype(BF16), tn,
                                      preferred_element_type=F32)
        db_ref[0, 0] = db
        dc_ref[0, 0] = dc

    blk4 = lambda shape: pl.BlockSpec((1, 1) + shape, lambda d, b: (d, b, 0, 0))
    cols = pl.BlockSpec((1, TA, UW), lambda d, b: (d, 0, b))
    return pl.pallas_call(
        kern,
        name="s5_bwd",
        grid=(2, NBLK),
        in_specs=[cols, blk4((TA, 2 * BW)), cols, blk4((8, 2 * BW)), blk4((UW, 2 * BW)), blk4((2 * BW, UW))],
        out_specs=[cols, blk4((8, 2 * BW)), blk4((UW, 2 * BW)), blk4((2 * BW, UW))],
        out_shape=[jax.ShapeDtypeStruct((2, TA, SW), F32), jax.ShapeDtypeStruct((2, NBLK, 8, 2 * BW), F32),
                   jax.ShapeDtypeStruct((2, NBLK, UW, 2 * BW), F32), jax.ShapeDtypeStruct((2, NBLK, 2 * BW, UW), F32)],
        scratch_shapes=[pltpu.VMEM((TA, 2 * BW), F32)],
        compiler_params=_cparams(("parallel", "parallel")),
    )(dy_seq, states, u_seq, lam, bmat, cmat)


TR = 256


def _vjp_rows(f, primals, cots, n_row):
    _, pull = jax.vjp(f, *primals)
    g = pull(cots)
    return list(g[:n_row]), list(g[n_row:])


class _GradDict(dict):
    def __init__(self, on_set=None):
        super().__init__()
        self._on_set = on_set

    def __setitem__(self, key, value):
        super().__setitem__(key, value)
        if self._on_set is not None:
            self._on_set(self)


def _local_step(x, ctx, tgt, mod_lat, mod_ctx, wb, sp, on_grad=None):
    sh1, sc1, g1, sh2, sc2, g2 = [mod_lat[:, i * D:(i + 1) * D] for i in range(6)]
    csh1, csc1 = mod_ctx[:, 0:D], mod_ctx[:, D:2 * D]
    tabs = _rope_tables()
    sink = sp["attn_sink"].reshape(1, NH)
    dskip = sp["ssm_d"].reshape(1, SW)
    lg_mix, lb_mix = sp["ln_mix_g"].reshape(1, D), sp["ln_mix_b"].reshape(1, D)
    lg_mlp, lb_mlp = sp["ln_mlp_g"].reshape(1, D), sp["ln_mlp_b"].reshape(1, D)
    b1, b2 = sp["b_mlp1"].reshape(1, DFF), sp["b_mlp2"].reshape(1, D)
    s5_names = ("ssm_a_re", "ssm_a_im", "ssm_log_dt", "ssm_b_re", "ssm_b_im", "ssm_c_re", "ssm_c_im")
    (lam, bmat, cmat), s5_pull = jax.vjp(_s5_prep, *[sp[n] for n in s5_names])

    def ln_mod(rv, vv):
        return [_f_ln_mod(rv[0], vv[0], vv[1])], []

    h_lat, = _rowwise(ln_mod, [(x, D, 0, 0)], [sc1, sh1], [(D, BF16)], [], nrows=T, tr=TR, name="ln1_lat")
    h_ctx, = _rowwise(ln_mod, [(ctx, D, 0, 0)], [csc1, csh1], [(D, BF16)], [], nrows=C, tr=TR, name="ln1_ctx")
    h1 = jnp.concatenate([h_lat, h_ctx], 0)
    proj = _matmul(h1, wb["w_in"], mode="nn", name="proj", tm=768, tn=512)
    attn = _attn_fwd(proj, sink, tabs)
    u_all = proj[:, QW + 2 * KVW:QW + 2 * KVW + SW]
    u_lat, u_ctx = u_all[:T], u_all[T:]
    u_seq = _to_seq(u_lat, u_ctx)
    states, y_seq = _s5_fwd(u_seq, lam, bmat, cmat)
    y_f, _, y_b, _ = _from_seq(y_seq)

    def ssm_pre(rv, vv):
        s = _f_ssm_pre(rv[0], rv[1], rv[2], vv[0])
        return [s, _gelu(s)], []

    ssm, ge = _rowwise(ssm_pre, [(u_lat, SW, 0, 0), (y_f, SW, 0, 0), (y_b, SW, 0, 0)], [dskip],
                       [(SW, F32), (SW, BF16)], [], nrows=T, tr=TR, name="ssm_pre")
    z = _matmul(ge, wb["w_glu"], mode="nn", name="glu_mm", tm=1024, tn=1024)

    def glu_act(rv, vv):
        return [_f_glu(rv[0])], []

    glu, = _rowwise(glu_act, [(z, 2 * SW, 0, 0)], [], [(SW, BF16)], [], nrows=T, tr=TR, name="glu_act")
    attn_d = _matmul(attn, wb["w_attn_up"], mode="nn", name="attn_up", tm=1024, tn=512)
    ssm_d = _matmul(glu, wb["w_ssm_up"], mode="nn", name="ssm_up", tm=1024, tn=512)
    ga_cb, gs_cb = (QW + 2 * KVW + SW) // D, (QW + 2 * KVW + SW) // D + 1

    def mix(rv, vv):
        return [_f_mix(*rv)], []

    mixv, = _rowwise(mix, [(proj, D, ga_cb, 0), (proj, D, gs_cb, 0), (attn_d, D, 0, 0), (ssm_d, D, 0, 0)], [],
                     [(D, BF16)], [], nrows=T, tr=TR, name="mix")
    y = _matmul(mixv, wb["w_out"], mode="nn", name="out_proj", tm=1024, tn=512)

    def post1(rv, vv):
        x1, h2 = _f_post1(rv[0], rv[1], *vv)
        return [x1, h2], []

    x1, h2 = _rowwise(post1, [(x, D, 0, 0), (y, D, 0, 0)], [g1, lg_mix, lb_mix, sc2, sh2],
                      [(D, F32), (D, BF16)], [], nrows=T, tr=TR, name="post1")

    def relu_sq(acc):
        r = jnp.maximum(acc, 0.0)
        return r, r * r

    r_act, act = _matmul(h2, wb["w_mlp1"], mode="nn", name="mlp1", tm=1024, tn=512, bias=b1,
                         out_dtypes=(BF16, BF16), epilogue=relu_sq)
    mlp = _matmul(act, wb["w_mlp2"], mode="nn", name="mlp2", tm=1024, tn=512, tk=2048)

    def loss_fb(rv, vv):
        x1_t, mlp_t, tgt_t = rv
        g2_v, lg_v, lb_v, b2_v = vv
        f = lambda a, m, g, p, q, b: _f_loss(a, m, tgt_t, g, p, q, b)
        val, grads = jax.value_and_grad(f, argnums=(0, 1, 2, 3, 4, 5))(x1_t, mlp_t, g2_v, lg_v, lb_v, b2_v)
        dx1, dmlp, dg2, dlg, dlb, db2 = grads
        return [dx1, dmlp], [jnp.reshape(val, (1, 1)), dg2, dlg, dlb, db2]

    dx1_a, d_mlp, loss_p, d_g2, d_lg_mlp, d_lb_mlp, d_b2 = _rowwise(
        loss_fb, [(x1, D, 0, 0), (mlp, D, 0, 0), (tgt, D, 0, 0)], [g2, lg_mlp, lb_mlp, b2],
        [(D, F32), (D, BF16)], [(1, 1), (1, D), (1, D), (1, D), (1, D)], nrows=T, tr=TR, name="loss_fb")

    gw = _GradDict(on_grad)
    gw["w_mlp2"] = _matmul(act, d_mlp, mode="tn", name="dw_mlp2", out_dtypes=(BF16,), tm=512, tn=1024, tk=1024)
    da, = (_matmul(d_mlp, wb["w_mlp2"], mode="nt", name="d_act", out_dtypes=(BF16,), tm=1024, tn=512,
                   extras=(r_act,), epilogue=lambda acc, r: (acc * (2.0 * r.astype(F32)),)),)
    ones = jnp.ones((8, T), BF16)
    d_b1 = _matmul(ones, da, mode="nn", name="db_mlp1", tm=8, tn=2048)[0:1]
    gw["w_mlp1"] = _matmul(h2, da, mode="tn", name="dw_mlp1", out_dtypes=(BF16,), tm=512, tn=1024, tk=1024)
    dh2 = _matmul(da, wb["w_mlp1"], mode="nt", name="d_h2", tm=1024, tn=512, tk=2048)

    def post1_b(rv, vv):
        x_t, y_t, dx1_t, dh2_t = rv
        gr, gv = _vjp_rows(_f_post1, (x_t, y_t, *vv), (dx1_t, dh2_t), 2)
        return [gr[0], gr[1]], gv

    dx_a, dy, d_g1, d_lg_mix, d_lb_mix, d_sc2, d_sh2 = _rowwise(
        post1_b, [(x, D, 0, 0), (y, D, 0, 0), (dx1_a, D, 0, 0), (dh2, D, 0, 0)], [g1, lg_mix, lb_mix, sc2, sh2],
        [(D, F32), (D, BF16)], [(1, D)] * 5, nrows=T, tr=TR, name="post1_bwd")
    gw["w_out"] = _matmul(mixv, dy, mode="tn", name="dw_out", out_dtypes=(BF16,), tm=512, tn=1024, tk=1024)
    dmix = _matmul(dy, wb["w_out"], mode="nt", name="d_mix", tm=1024, tn=512)

    def mix_b(rv, vv):
        gr, _ = _vjp_rows(_f_mix, tuple(rv[:4]), rv[4], 4)
        return gr, []

    d_ga, d_gs, d_attn_d, d_ssm_d = _rowwise(
        mix_b, [(proj, D, ga_cb, 0), (proj, D, gs_cb, 0), (attn_d, D, 0, 0), (ssm_d, D, 0, 0), (dmix, D, 0, 0)], [],
        [(D, BF16)] * 4, [], nrows=T, tr=TR, name="mix_bwd")
    gw["w_attn_up"] = _matmul(attn, d_attn_d, mode="tn", name="dw_attn_up", out_dtypes=(BF16,), tm=512, tn=1024, tk=1024)
    d_attn = _matmul(d_attn_d, wb["w_attn_up"], mode="nt", name="d_attn", out_dtypes=(BF16,), tm=1024, tn=512)
    gw["w_ssm_up"] = _matmul(glu, d_ssm_d, mode="tn", name="dw_ssm_up", out_dtypes=(BF16,), tm=512, tn=1024, tk=1024)
    d_glu = _matmul(d_ssm_d, wb["w_ssm_up"], mode="nt", name="d_glu", tm=1024, tn=512)

    def glu_b(rv, vv):
        gr, _ = _vjp_rows(_f_glu, (rv[0],), rv[1], 1)
        return gr, []

    dz, = _rowwise(glu_b, [(z, 2 * SW, 0, 0), (d_glu, SW, 0, 0)], [], [(2 * SW, BF16)], [], nrows=T, tr=TR, name="glu_bwd")
    gw["w_glu"] = _matmul(ge, dz, mode="tn", name="dw_glu", out_dtypes=(BF16,), tm=512, tn=1024, tk=1024)
    d_ge = _matmul(dz, wb["w_glu"], mode="nt", name="d_ge", tm=1024, tn=512)

    def ssm_pre_b(rv, vv):
        u_t, yf_t, yb_t, dge_t = rv
        f = lambda u, yf, yb, dk: _gelu(_f_ssm_pre(u, yf, yb, dk))
        gr, gv = _vjp_rows(f, (u_t, yf_t, yb_t, vv[0]), dge_t, 3)
        return [gr[0], gr[1]], gv

    du_dir, d_ssm, d_dskip = _rowwise(
        ssm_pre_b, [(u_lat, SW, 0, 0), (y_f, SW, 0, 0), (y_b, SW, 0, 0), (d_ge, SW, 0, 0)], [dskip],
        [(SW, F32), (SW, F32)], [(1, SW)], nrows=T, tr=TR, name="ssm_pre_bwd")
    dy_seq = _to_seq(d_ssm, jnp.zeros((C, SW), F32))
    du_seq, dlam, dbmat, dcmat = _s5_bwd(dy_seq, states, u_seq, lam, bmat, cmat)
    du_f, duc_f, du_b, duc_b = _from_seq(du_seq)
    du_all = jnp.concatenate([du_dir + du_f + du_b, duc_f + duc_b], 0).astype(BF16)
    s5_grads = s5_pull((dlam, dbmat, dcmat))

    dq, dk, dv, dsink = _attn_bwd(proj, d_attn, sink, tabs)
    zc = lambda w: jnp.zeros((C, w), BF16)
    dproj = jnp.concatenate([
        jnp.concatenate([dq, zc(QW)], 0), dk, dv, du_all,
        jnp.concatenate([d_ga, zc(D)], 0), jnp.concatenate([d_gs, zc(D)], 0)], 1)
    gw["w_in"] = _matmul(h1, dproj, mode="tn", name="dw_in", out_dtypes=(BF16,), tm=512, tn=1536, tk=768)
    dh1 = _matmul(dproj, wb["w_in"], mode="nt", name="d_h1", tm=768, tn=512, tk=2048)

    def ln1_b(rv, vv):
        x_t, dh_t, dxa_t = rv
        gr, gv = _vjp_rows(_f_ln_mod, (x_t, vv[0], vv[1]), dh_t, 1)
        return [gr[0] + dxa_t], gv

    grad_x, d_sc1, d_sh1 = _rowwise(ln1_b, [(x, D, 0, 0), (dh1, D, 0, 0), (dx_a, D, 0, 0)], [sc1, sh1],
                                    [(D, F32)], [(1, D), (1, D)], nrows=T, tr=TR, name="ln1_lat_bwd")

    def ln1c_b(rv, vv):
        _, gv = _vjp_rows(_f_ln_mod, (rv[0], vv[0], vv[1]), rv[1], 1)
        return [], gv

    d_csc1, d_csh1 = _rowwise(ln1c_b, [(ctx, D, 0, 0), (dh1, D, 0, T // TR)], [csc1, csh1],
                              [], [(1, D), (1, D)], nrows=C, tr=TR, name="ln1_ctx_bwd")

    d_mod_lat = jnp.concatenate([d_sh1, d_sc1, d_g1, d_sh2, d_sc2, d_g2], 1)
    zv = jnp.zeros((1, D), F32)
    d_mod_ctx = jnp.concatenate([d_csh1, d_csc1, zv, zv, zv, zv], 1)
    gs = {n: g for n, g in zip(s5_names, s5_grads)}
    gs["attn_sink"] = dsink[:, 0]
    gs["ssm_d"] = d_dskip
    gs["ln_mix_g"], gs["ln_mix_b"] = d_lg_mix, d_lb_mix
    gs["ln_mlp_g"], gs["ln_mlp_b"] = d_lg_mlp, d_lb_mlp
    gs["b_mlp1"], gs["b_mlp2"] = d_b1, d_b2
    return loss_p, grad_x, d_mod_lat, d_mod_ctx, gw, gs


def _my_pos():
    return lax.axis_index("x"), lax.axis_index("y"), lax.axis_index("c")


def _flip(p, bit):
    return 1 - p if bit else p


def _peer(pos, k):
    x, y, c = pos
    return (_flip(x, (k >> 2) & 1), _flip(y, (k >> 1) & 1), _flip(c, k & 1))


def _lin(pos):
    return 4 * pos[0] + 2 * pos[1] + pos[2]


def _allgather_small(v, name):
    r, w = v.shape

    def body(v_ref, out_ref, send_sems, recv_sems, local_sem):
        me = _my_pos()
        mine = pltpu.make_async_copy(v_ref, out_ref.at[_lin(me)], local_sem)
        mine.start()
        sends = []
        for k in range(1, N_DEV):
            cp = pltpu.make_async_remote_copy(src_ref=v_ref, dst_ref=out_ref.at[_lin(me)], send_sem=send_sems.at[k - 1],
                                              recv_sem=recv_sems.at[k - 1], device_id=_peer(me, k), device_id_type=MESH)
            cp.start()
            sends.append(cp)
        for k in range(1, N_DEV):
            peer = _peer(me, k)
            pltpu.make_async_remote_copy(src_ref=v_ref, dst_ref=out_ref.at[_lin(peer)], send_sem=send_sems.at[k - 1],
                                         recv_sem=recv_sems.at[k - 1], device_id=peer, device_id_type=MESH).wait_recv()
        for cp in sends:
            cp.wait_send()
        mine.wait()

    return pl.pallas_call(
        body,
        name=name,
        out_shape=jax.ShapeDtypeStruct((N_DEV, r, w), v.dtype),
        in_specs=[pl.BlockSpec(memory_space=pltpu.VMEM)],
        out_specs=pl.BlockSpec(memory_space=pltpu.VMEM),
        scratch_shapes=[pltpu.SemaphoreType.DMA((N_DEV - 1,)), pltpu.SemaphoreType.DMA((N_DEV - 1,)), pltpu.SemaphoreType.DMA],
        compiler_params=pltpu.CompilerParams(vmem_limit_bytes=VMEM_LIMIT_BYTES),
    )(v)


def _block_of(ref, kind, idx, n):
    start = pl.multiple_of(idx * n, 128)
    if kind == "col":
        return ref.at[:, pl.ds(start, n)]
    return ref.at[pl.ds(start, n), :]


def _allgather_weights(shards, kinds):
    nt = len(shards)
    out_shape = []
    for s, kind in zip(shards, kinds):
        k, n = s.shape
        out_shape.append(jax.ShapeDtypeStruct((k, n * N_DEV) if kind == "col" else (k * N_DEV, n), s.dtype))

    def body(*refs):
        ins, outs = refs[:nt], refs[nt:2 * nt]
        send_sems, recv_sems, local_sems = refs[2 * nt:]
        x, y, c = _my_pos()
        me, sibling = (x, y, c), (x, y, 1 - c)
        chips = [(1 - x, y), (x, 1 - y), (1 - x, 1 - y)]

        def blk(t, pos):
            n = shards[t].shape[1] if kinds[t] == "col" else shards[t].shape[0]
            return _block_of(outs[t], kinds[t], _lin(pos), n)

        def copy(t, k, block, to, src=None):
            return pltpu.make_async_remote_copy(src_ref=blk(t, block) if src is None else src, dst_ref=blk(t, block),
                                                send_sem=send_sems.at[t, k], recv_sem=recv_sems.at[t, k],
                                                device_id=to, device_id_type=MESH)

        local, sends = [], []
        for t in range(nt):
            mine = pltpu.make_async_copy(ins[t], blk(t, me), local_sems.at[t])
            mine.start()
            local.append(mine)
            first = [copy(t, 0, me, sibling, src=ins[t])]
            first += [copy(t, 1 + j, me, (*chip, c), src=ins[t]) for j, chip in enumerate(chips)]
            for cp in first:
                cp.start()
            sends += first
        for t in range(nt):
            for j, chip in enumerate(chips):
                copy(t, 1 + j, (*chip, c), me).wait_recv()
                fwd = copy(t, 4 + j, (*chip, c), sibling)
                fwd.start()
                sends.append(fwd)
        for t in range(nt):
            copy(t, 0, sibling, me).wait_recv()
            for j, chip in enumerate(chips):
                copy(t, 4 + j, (*chip, 1 - c), me).wait_recv()
        for cp in sends:
            cp.wait_send()
        for cp in local:
            cp.wait()

    any_spec = pl.BlockSpec(memory_space=pl.ANY)
    return pl.pallas_call(
        body,
        name="allgather_weights",
        out_shape=out_shape,
        in_specs=[any_spec] * nt,
        out_specs=[any_spec] * nt,
        scratch_shapes=[pltpu.SemaphoreType.DMA((nt, N_DEV - 1)), pltpu.SemaphoreType.DMA((nt, N_DEV - 1)),
                        pltpu.SemaphoreType.DMA((nt,))],
    )(*shards)


def _handshake(peers):
    barrier = pltpu.get_barrier_semaphore()
    for peer in peers:
        pl.semaphore_signal(barrier, inc=1, device_id=peer, device_id_type=MESH)
    pl.semaphore_wait(barrier, len(peers))


def _allgather_weights_seq(shards, kinds, name, collective_id):
    nt = len(shards)
    hbm = pltpu.MemorySpace.HBM
    ins = [jax.new_ref(s, memory_space=hbm) for s in shards]
    outs = []
    for s, kind in zip(shards, kinds):
        k, n = s.shape
        shape = (k, n * N_DEV) if kind == "col" else (k * N_DEV, n)
        outs.append(jax.empty_ref(jax.ShapeDtypeStruct(shape, s.dtype), memory_space=hbm))

    @functools.partial(
        pl.kernel, mesh=plsc.ScalarSubcoreMesh(axis_name="seq", num_cores=1), name=name,
        scratch_types=(pltpu.SemaphoreType.DMA((nt, N_DEV - 1)), pltpu.SemaphoreType.DMA((nt, N_DEV - 1)),
                       pltpu.SemaphoreType.DMA((nt,))),
        compiler_params=pltpu.CompilerParams(collective_id=collective_id))
    def launch(send_sems, recv_sems, local_sems):
        x, y, c = _my_pos()
        me, sibling = (x, y, c), (x, y, 1 - c)
        chips = [(1 - x, y), (x, 1 - y), (1 - x, 1 - y)]
        _handshake([sibling] + [(*chip, c) for chip in chips])

        def blk(t, pos):
            n = shards[t].shape[1] if kinds[t] == "col" else shards[t].shape[0]
            return _block_of(outs[t], kinds[t], _lin(pos), n)

        def copy(t, k, block, to, src=None):
            return pltpu.make_async_remote_copy(src_ref=blk(t, block) if src is None else src, dst_ref=blk(t, block),
                                                send_sem=send_sems.at[t, k], recv_sem=recv_sems.at[t, k],
                                                device_id=to, device_id_type=MESH)

        local, sends = [], []
        for t in range(nt):
            mine = pltpu.make_async_copy(ins[t], blk(t, me), local_sems.at[t])
            mine.start()
            local.append(mine)
            first = [copy(t, 0, me, sibling, src=ins[t])]
            first += [copy(t, 1 + j, me, (*chip, c), src=ins[t]) for j, chip in enumerate(chips)]
            for cp in first:
                cp.start()
            sends += first
        for t in range(nt):
            for j, chip in enumerate(chips):
                copy(t, 1 + j, (*chip, c), me).wait_recv()
                fwd = copy(t, 4 + j, (*chip, c), sibling)
                fwd.start()
                sends.append(fwd)
        for t in range(nt):
            copy(t, 0, sibling, me).wait_recv()
            for j, chip in enumerate(chips):
                copy(t, 4 + j, (*chip, 1 - c), me).wait_recv()
        for cp in sends:
            cp.wait_send()
        for cp in local:
            cp.wait()

    launch()
    return [o[...] for o in outs]


def _scatter_grads_seq(grads, kinds, name, collective_id):
    nt = len(grads)
    hbm = pltpu.MemorySpace.HBM
    shard_shapes = []
    for g, kind in zip(grads, kinds):
        k, n = g.shape
        shard_shapes.append((k, n // N_DEV) if kind == "col" else (k // N_DEV, n))
    ins = [jax.new_ref(g, memory_space=hbm) for g in grads]
    outs = [jax.empty_ref(jax.ShapeDtypeStruct((N_DEV,) + s, g.dtype), memory_space=hbm) for s, g in zip(shard_shapes, grads)]

    @functools.partial(
        pl.kernel, mesh=plsc.ScalarSubcoreMesh(axis_name="seq", num_cores=1), name=name,
        scratch_types=(pltpu.SemaphoreType.DMA((nt, N_DEV - 1)), pltpu.SemaphoreType.DMA((nt, N_DEV - 1)),
                       pltpu.SemaphoreType.DMA((nt,))),
        compiler_params=pltpu.CompilerParams(collective_id=collective_id))
    def launch(send_sems, recv_sems, local_sems):
        me = _my_pos()
        _handshake([_peer(me, k) for k in range(1, N_DEV)])

        def blk(t, pos):
            n = shard_shapes[t][1] if kinds[t] == "col" else shard_shapes[t][0]
            return _block_of(ins[t], kinds[t], _lin(pos), n)

        local, sends = [], []
        for t in range(nt):
            cp = pltpu.make_async_copy(blk(t, me), outs[t].at[_lin(me)], local_sems.at[t])
            cp.start()
            local.append(cp)
            for k in range(1, N_DEV):
                peer = _peer(me, k)
                cp = pltpu.make_async_remote_copy(src_ref=blk(t, peer), dst_ref=outs[t].at[_lin(me)], send_sem=send_sems.at[t, k - 1],
                                                  recv_sem=recv_sems.at[t, k - 1], device_id=peer, device_id_type=MESH)
                cp.start()
                sends.append(cp)
        for t in range(nt):
            for k in range(1, N_DEV):
                peer = _peer(me, k)
                pltpu.make_async_remote_copy(src_ref=blk(t, me), dst_ref=outs[t].at[_lin(peer)], send_sem=send_sems.at[t, k - 1],
                                             recv_sem=recv_sems.at[t, k - 1], device_id=peer, device_id_type=MESH).wait_recv()
        for cp in sends:
            cp.wait_send()
        for cp in local:
            cp.wait()

    launch()
    return [o[...] for o in outs]


def _scatter_grads(grads, kinds):
    nt = len(grads)
    shard_shapes = []
    for g, kind in zip(grads, kinds):
        k, n = g.shape
        shard_shapes.append((k, n // N_DEV) if kind == "col" else (k // N_DEV, n))

    def body(*refs):
        ins, outs = refs[:nt], refs[nt:2 * nt]
        send_sems, recv_sems, local_sems = refs[2 * nt:]
        me = _my_pos()

        def blk(t, pos):
            n = shard_shapes[t][1] if kinds[t] == "col" else shard_shapes[t][0]
            return _block_of(ins[t], kinds[t], _lin(pos), n)

        local, sends = [], []
        for t in range(nt):
            cp = pltpu.make_async_copy(blk(t, me), outs[t].at[_lin(me)], local_sems.at[t])
            cp.start()
            local.append(cp)
            for k in range(1, N_DEV):
                peer = _peer(me, k)
                cp = pltpu.make_async_remote_copy(src_ref=blk(t, peer), dst_ref=outs[t].at[_lin(me)], send_sem=send_sems.at[t, k - 1],
                                                  recv_sem=recv_sems.at[t, k - 1], device_id=peer, device_id_type=MESH)
                cp.start()
                sends.append(cp)
        for t in range(nt):
            for k in range(1, N_DEV):
                peer = _peer(me, k)
                pltpu.make_async_remote_copy(src_ref=blk(t, me), dst_ref=outs[t].at[_lin(peer)], send_sem=send_sems.at[t, k - 1],
                                             recv_sem=recv_sems.at[t, k - 1], device_id=peer, device_id_type=MESH).wait_recv()
        for cp in sends:
            cp.wait_send()
        for cp in local:
            cp.wait()

    any_spec = pl.BlockSpec(memory_space=pl.ANY)
    return pl.pallas_call(
        body,
        name="scatter_grads",
        out_shape=[jax.ShapeDtypeStruct((N_DEV,) + s, g.dtype) for s, g in zip(shard_shapes, grads)],
        in_specs=[any_spec] * nt,
        out_specs=[any_spec] * nt,
        scratch_shapes=[pltpu.SemaphoreType.DMA((nt, N_DEV - 1)), pltpu.SemaphoreType.DMA((nt, N_DEV - 1)),
                        pltpu.SemaphoreType.DMA((nt,))],
    )(*grads)


def _adam(g_slots, w, m, v, *, tr, name):
    ns, r, wd = g_slots.shape
    tr = min(tr, r)
    assert r % tr == 0, (name, r, tr)
    c1 = 1.0 - ADAM_B1 ** ADAM_STEP
    c2 = 1.0 - ADAM_B2 ** ADAM_STEP

    def kern(g_ref, w_ref, m_ref, v_ref, go_ref, d_ref, mo_ref, vo_ref):
        g = g_ref[0].astype(F32)
        for s in range(1, ns):
            g = g + g_ref[s].astype(F32)
        m_new = ADAM_B1 * m_ref[...] + (1.0 - ADAM_B1) * g
        v_new = ADAM_B2 * v_ref[...] + (1.0 - ADAM_B2) * (g * g)
        m_hat = m_new / c1
        v_hat = v_new / c2
        go_ref[...] = g
        d_ref[...] = -ADAM_LR * (m_hat / (jnp.sqrt(v_hat) + ADAM_EPS) + ADAM_WD * w_ref[...])
        mo_ref[...] = m_new
        vo_ref[...] = v_new

    tile = pl.BlockSpec((tr, wd), lambda i: (i, 0))
    return pl.pallas_call(
        kern,
        name=name,
        grid=(r // tr,),
        in_specs=[pl.BlockSpec((ns, tr, wd), lambda i: (0, i, 0)), tile, tile, tile],
        out_specs=[tile] * 4,
        out_shape=[jax.ShapeDtypeStruct((r, wd), F32)] * 4,
        compiler_params=_cparams(("parallel",)),
    )(g_slots, w, m, v)


SMALL = ("c_ctx", "b_ada", "attn_sink", "ssm_a_re", "ssm_a_im", "ssm_log_dt", "ssm_b_re", "ssm_b_im", "ssm_c_re", "ssm_c_im",
         "ssm_d", "ln_mix_g", "ln_mix_b", "b_mlp1", "b_mlp2", "ln_mlp_g", "ln_mlp_b")
BIG = ("w_in", "w_glu", "w_attn_up", "w_ssm_up", "w_out", "w_mlp1", "w_mlp2")
BIG_KIND = ("col", "col", "col", "col", "row", "col", "row")
AG_GROUPS = (("w_in",), ("w_glu", "w_attn_up", "w_ssm_up", "w_out"), ("w_mlp1",), ("w_mlp2",))
AG_COLLECTIVE_ID0 = 1
RS_GROUPS = (("w_mlp2",), ("w_mlp1",), ("w_out", "w_attn_up", "w_ssm_up", "w_glu"), ("w_in",))
RS_COLLECTIVE_ID0 = AG_COLLECTIVE_ID0 + len(AG_GROUPS)
LANES = 128


def _pack(parts):
    rows = []
    for p in parts:
        flat = p.reshape(-1).astype(F32)
        pad = (-flat.shape[0]) % LANES
        rows.append(jnp.pad(flat, (0, pad)).reshape(-1, LANES))
    packed = jnp.concatenate(rows, 0)
    return jnp.pad(packed, ((0, (-packed.shape[0]) % 8), (0, 0)))


def _unpack(packed, shapes):
    out, r0 = [], 0
    for s in shapes:
        n = math.prod(s)
        nr = -(-n // LANES)
        out.append(packed[r0:r0 + nr].reshape(-1)[:n].reshape(s))
        r0 += nr
    return out


WEIGHTS = ("c_ctx", "w_ada", "b_ada", "w_in", "attn_sink", "ssm_a_re", "ssm_a_im", "ssm_log_dt", "ssm_b_re", "ssm_b_im",
           "ssm_c_re", "ssm_c_im", "ssm_d", "w_glu", "w_attn_up", "w_ssm_up", "w_out", "ln_mix_g", "ln_mix_b", "w_mlp1",
           "b_mlp1", "w_mlp2", "b_mlp2", "ln_mlp_g", "ln_mlp_b")
ADA_COLS = 6 * D // N_DEV


def _step(x, c, ctx, loss_target, p, m, v):
    me = _lin(_my_pos())
    x2, ctx2, tgt2 = x[0], ctx[0], loss_target[0]

    wb = {}
    for gi, group in enumerate(AG_GROUPS):
        full = _allgather_weights_seq([p[n][0].astype(BF16) for n in group], [BIG_KIND[BIG.index(n)] for n in group],
                                      "allgather_seq%d" % gi, AG_COLLECTIVE_ID0 + gi)
        wb.update(zip(group, full))

    c_all = _allgather_small(jnp.broadcast_to(c, (8, D)), "gather_c")[:, 0, :]
    cc = p["c_ctx"].reshape(1, D)
    s_in = jnp.concatenate([c_all, cc, jnp.zeros((7, D), F32)], 0)
    s_act, = _rowwise(lambda rv, vv: ([_silu(rv[0])], []), [(s_in, D, 0, 0)], [], [(D, F32)], [], nrows=16, tr=16, name="silu_c")
    b_mine = lax.dynamic_slice_in_dim(p["b_ada"], me * ADA_COLS, ADA_COLS, axis=1)
    mod_part = _matmul(s_act, p["w_ada"][0], mode="nn", name="ada_fwd", tm=16, tn=512, bias=b_mine)
    mod_all = _allgather_small(mod_part, "gather_mod")
    mod_lat = lax.dynamic_index_in_dim(mod_all, me, axis=1, keepdims=False).reshape(1, 6 * D)
    mod_ctx = mod_all[:, 8, :].reshape(1, 6 * D)

    sp = {n: p[n][0] for n in SMALL if n not in ("c_ctx", "b_ada")}
    recv = {}

    def on_grad(gw):
        for gi, group in enumerate(RS_GROUPS):
            if group[0] not in recv and all(n in gw for n in group):
                slots = _scatter_grads_seq([gw[n] for n in group], [BIG_KIND[BIG.index(n)] for n in group],
                                           "scatter_seq%d" % gi, RS_COLLECTIVE_ID0 + gi)
                recv.update(zip(group, slots))

    loss_p, grad_x, d_mod_lat, d_mod_ctx, gw, gs = _local_step(x2, ctx2, tgt2, mod_lat, mod_ctx, wb, sp, on_grad)

    dm = jnp.concatenate([d_mod_lat, d_mod_ctx, jnp.zeros((6, 6 * D), F32)], 0)
    dm_all = _allgather_small(dm, "gather_dmod")
    dm2 = jnp.concatenate([dm_all[:, 0, :], dm_all[:, 1, :]], 0)
    dm2_mine = lax.dynamic_slice_in_dim(dm2, me * ADA_COLS, ADA_COLS, axis=1)
    s2 = jnp.concatenate([s_act[0:8], jnp.broadcast_to(s_act[8:9], (8, D))], 0)
    g_w_ada = _matmul(s2, dm2_mine, mode="tn", name="dw_ada", tm=512, tn=ADA_COLS)
    dsc_part = _matmul(dm2_mine[8:16], p["w_ada"][0], mode="nt", name="d_silu_cctx", tm=8, tn=512)

    def cctx_b(rv, vv):
        _, pull = jax.vjp(_silu, vv[0])
        return [], [pull(jnp.sum(rv[0], axis=0, keepdims=True))[0]]

    g_cctx, = _rowwise(cctx_b, [(dsc_part, D, 0, 0)], [cc], [], [(1, D)], nrows=8, tr=8, name="cctx_bwd")
    gs["c_ctx"] = g_cctx
    gs["b_ada"] = d_mod_lat + d_mod_ctx

    res = {}
    for n in BIG:
        res[n] = _adam(recv[n], p[n][0], m[n][0], v[n][0], tr=256, name="adam_" + n)
    res["w_ada"] = _adam(g_w_ada[None], p["w_ada"][0], m["w_ada"][0], v["w_ada"][0], tr=256, name="adam_w_ada")

    g_pack = _allgather_small(_pack([gs[n] for n in SMALL]), "gather_small_grads")
    sm = _adam(g_pack, _pack([p[n] for n in SMALL]), _pack([m[n] for n in SMALL]), _pack([v[n] for n in SMALL]),
               tr=g_pack.shape[1], name="adam_small")
    shapes = [p[n].shape for n in SMALL]
    for j, outs in enumerate(zip(*[_unpack(a, shapes) for a in sm])):
        res[SMALL[j]] = outs

    loss = lax.psum(loss_p[0, 0], ("x", "y", "c"))
    outs = [loss, grad_x[None]]
    for j in range(4):
        outs += [res[n][j].reshape(p[n].shape) for n in WEIGHTS]
    return tuple(outs)


def kernel(x, c, ctx, c_ctx, w_ada, b_ada, w_in, attn_sink, ssm_a_re, ssm_a_im, ssm_log_dt, ssm_b_re, ssm_b_im, ssm_c_re, ssm_c_im, ssm_d, w_glu, w_attn_up, w_ssm_up, w_out, ln_mix_g, ln_mix_b, w_mlp1, b_mlp1, w_mlp2, b_mlp2, ln_mlp_g, ln_mlp_b, loss_target, m_c_ctx, m_w_ada, m_b_ada, m_w_in, m_attn_sink, m_ssm_a_re, m_ssm_a_im, m_ssm_log_dt, m_ssm_b_re, m_ssm_b_im, m_ssm_c_re, m_ssm_c_im, m_ssm_d, m_w_glu, m_w_attn_up, m_w_ssm_up, m_w_out, m_ln_mix_g, m_ln_mix_b, m_w_mlp1, m_b_mlp1, m_w_mlp2, m_b_mlp2, m_ln_mlp_g, m_ln_mlp_b, v_c_ctx, v_w_ada, v_b_ada, v_w_in, v_attn_sink, v_ssm_a_re, v_ssm_a_im, v_ssm_log_dt, v_ssm_b_re, v_ssm_b_im, v_ssm_c_re, v_ssm_c_im, v_ssm_d, v_w_glu, v_w_attn_up, v_w_ssm_up, v_w_out, v_ln_mix_g, v_ln_mix_b, v_w_mlp1, v_b_mlp1, v_w_mlp2, v_b_mlp2, v_ln_mlp_g, v_ln_mlp_b):
    given = dict(locals())
    p = {n: given[n] for n in WEIGHTS}
    m = {n: given["m_" + n] for n in WEIGHTS}
    v = {n: given["v_" + n] for n in WEIGHTS}
    return _step(x, c, ctx, loss_target, p, m, v)
```

```python
import functools
import math

import jax
import jax.numpy as jnp
from jax import lax
from jax.experimental import pallas as pl
from jax.experimental.pallas import tpu as pltpu
from jax.experimental.pallas import tpu_sc as plsc

F32 = jnp.float32
BF16 = jnp.bfloat16

N_DEV = 8
D = 2048
T = 2048
C = 256
TA = T + C
GRID_W = 64
HD = 128
NH = 8
NKV = 2
GROUP = NH // NKV
WINDOW = 128
QW = NH * HD
KVW = NKV * HD
SW = D // 4
SG = 16
NG = SW // SG
SP = 64
DFF = 4 * D
IN_COLS = QW + 2 * KVW + SW + 2 * D
ALPHA = 2.0 ** 0.25
LN_EPS = 1e-6
NEG_INF = -1e30
ROPE_BASE = 10000.0
ATT_SCALE = HD ** -0.5

NSEG = 8
SEGLEN = TA // NSEG
GBLK = 8
NBLK = NG // GBLK
BW = GBLK * SP
UW = GBLK * SG

ADAM_LR = 0.001
ADAM_B1 = 0.9
ADAM_B2 = 0.999
ADAM_EPS = 1e-08
ADAM_WD = 0.01
ADAM_STEP = 10

VMEM_LIMIT_BYTES = 56 * 1024 * 1024
MESH = pl.DeviceIdType.MESH


def _cparams(sem=None):
    return pltpu.CompilerParams(dimension_semantics=sem, vmem_limit_bytes=VMEM_LIMIT_BYTES)


def _matmul(a, b, *, mode, name, out_dtypes=(F32,), tm=512, tn=512, tk=None, bias=None, extras=(), epilogue=None, after=()):
    if mode == "nn":
        (M, K), (K2, N) = a.shape, b.shape
    elif mode == "nt":
        (M, K), (N, K2) = a.shape, b.shape
    else:
        (K, M), (K2, N) = a.shape, b.shape
    assert K == K2, (name, a.shape, b.shape)
    tm, tn, tk = min(tm, M), min(tn, N), min(tk or K, K)
    assert M % tm == 0 and N % tn == 0 and K % tk == 0, (name, M, N, K, tm, tn, tk)
    nk = K // tk
    if mode == "tn":
        a_spec = pl.BlockSpec((tk, tm), lambda i, j, k: (k, i))
    else:
        a_spec = pl.BlockSpec((tm, tk), lambda i, j, k: (i, k))
    if mode == "nt":
        b_spec = pl.BlockSpec((tn, tk), lambda i, j, k: (j, k))
    else:
        b_spec = pl.BlockSpec((tk, tn), lambda i, j, k: (k, j))
    dims = {"nn": (((1,), (0,)), ((), ())), "nt": (((1,), (1,)), ((), ())), "tn": (((0,), (0,)), ((), ()))}[mode]
    in_specs = [a_spec, b_spec]
    operands = [a, b]
    if bias is not None:
        in_specs.append(pl.BlockSpec((1, tn), lambda i, j, k: (0, j)))
        operands.append(bias)
    for e in extras:
        in_specs.append(pl.BlockSpec((tm, tn), lambda i, j, k: (i, j)))
        operands.append(e)
    n_ex = len(extras)
    for t in after:
        in_specs.append(pl.BlockSpec(memory_space=pl.ANY))
        operands.append(t)
    n_after = len(after)
    n_out = len(out_dtypes)
    has_bias = bias is not None

    def kern(*refs):
        a_ref, b_ref = refs[0], refs[1]
        pos = 2
        bias_ref = None
        if has_bias:
            bias_ref = refs[pos]
            pos += 1
        ex_refs = refs[pos:pos + n_ex]
        pos += n_ex + n_after
        out_refs = refs[pos:pos + n_out]
        acc_ref = refs[pos + n_out] if nk > 1 else None

        def finish(r):
            if has_bias:
                r = r + bias_ref[...]
            outs = epilogue(r, *[e[...] for e in ex_refs]) if epilogue is not None else (r,)
            for o_ref, o in zip(out_refs, outs):
                o_ref[...] = o.astype(o_ref.dtype)

        part = lax.dot_general(a_ref[...].astype(BF16), b_ref[...].astype(BF16), dims, preferred_element_type=F32)
        if nk == 1:
            finish(part)
        else:
            k = pl.program_id(2)

            @pl.when(k == 0)
            def _():
                acc_ref[...] = part

            @pl.when(k > 0)
            def _():
                acc_ref[...] += part

            @pl.when(k == nk - 1)
            def _():
                finish(acc_ref[...])

    outs = pl.pallas_call(
        kern,
        name=name,
        grid=(M // tm, N // tn, nk),
        in_specs=in_specs,
        out_specs=[pl.BlockSpec((tm, tn), lambda i, j, k: (i, j)) for _ in out_dtypes],
        out_shape=[jax.ShapeDtypeStruct((M, N), dt) for dt in out_dtypes],
        scratch_shapes=[pltpu.VMEM((tm, tn), F32)] if nk > 1 else [],
        compiler_params=_cparams(("parallel", "parallel", "arbitrary")),
    )(*operands)
    return outs[0] if n_out == 1 else tuple(outs)


def _rowwise(fn, rows, vecs, outs, vec_outs, *, nrows, tr, name):
    n_rows, n_vecs, n_outs = len(rows), len(vecs), len(outs)
    in_specs = [pl.BlockSpec((tr, w), lambda i, cb=cb, ro=ro: (i + ro, cb)) for (_, w, cb, ro) in rows]
    in_specs += [pl.BlockSpec(v.shape, lambda i: (0, 0)) for v in vecs]
    out_specs = [pl.BlockSpec((tr, w), lambda i: (i, 0)) for (w, _) in outs]
    out_specs += [pl.BlockSpec(s, lambda i: (0, 0)) for s in vec_outs]
    out_shape = [jax.ShapeDtypeStruct((nrows, w), dt) for (w, dt) in outs]
    out_shape += [jax.ShapeDtypeStruct(s, F32) for s in vec_outs]

    def kern(*refs):
        rvals = [r[...] for r in refs[:n_rows]]
        vvals = [r[...] for r in refs[n_rows:n_rows + n_vecs]]
        o_refs = refs[n_rows + n_vecs:n_rows + n_vecs + n_outs]
        v_refs = refs[n_rows + n_vecs + n_outs:]
        ro, vo = fn(rvals, vvals)
        for r, val in zip(o_refs, ro):
            r[...] = val.astype(r.dtype)
        i = pl.program_id(0)
        for r, val in zip(v_refs, vo):
            @pl.when(i == 0)
            def _(r=r, val=val):
                r[...] = val.astype(F32)

            @pl.when(i > 0)
            def _(r=r, val=val):
                r[...] += val.astype(F32)

    res = pl.pallas_call(
        kern,
        name=name,
        grid=(nrows // tr,),
        in_specs=in_specs,
        out_specs=out_specs,
        out_shape=out_shape,
        compiler_params=_cparams(("arbitrary",)),
    )(*[r[0] for r in rows], *vecs)
    return list(res)


def _ln(x):
    mu = jnp.mean(x, axis=-1, keepdims=True)
    xc = x - mu
    var = jnp.mean(xc * xc, axis=-1, keepdims=True)
    return xc * lax.rsqrt(var + LN_EPS)


def _sigmoid(x):
    return 1.0 / (1.0 + jnp.exp(-x))


def _gelu(x):
    return 0.5 * x * (1.0 + jnp.tanh(math.sqrt(2.0 / math.pi) * (x + 0.044715 * (x * x * x))))


def _silu(x):
    return x * _sigmoid(x)


def _f_ln_mod(x, sc, sh):
    return _ln(x) * (1.0 + sc) + sh


def _f_ssm_pre(u, yf, yb, dskip):
    return dskip * u + yf + yb


def _f_glu(z):
    return z[:, :SW] * _sigmoid(z[:, SW:])


def _f_mix(ga, gs, attn_d, ssm_d):
    return _sigmoid(ga) * attn_d + _sigmoid(gs) * ssm_d


def _f_post1(x, y, g1, lg, lb, sc2, sh2):
    r1 = ALPHA * x + g1 * y
    x1 = _ln(r1) * lg + lb
    h2 = _ln(x1) * (1.0 + sc2) + sh2
    return x1, h2


def _f_loss(x1, mlp, tgt, g2, lg, lb, b2z):
    r2 = ALPHA * x1 + g2 * (mlp + b2z)
    out = _ln(r2) * lg + lb
    err = out - tgt
    return 0.5 * jnp.sum(err * err) * (1.0 / D)


def _rope_tables():
    rows = T // GRID_W
    row = jnp.repeat(jnp.arange(rows), GRID_W)
    col = jnp.tile(jnp.arange(GRID_W), rows)
    n_freq = HD // 4
    freqs = ROPE_BASE ** (-jnp.arange(n_freq, dtype=F32) / n_freq)
    ang_r = row.astype(F32)[:, None] * freqs
    ang_c = col.astype(F32)[:, None] * freqs
    ang = jnp.concatenate([ang_r, ang_r, ang_c, ang_c], -1)
    cos, sin = jnp.cos(ang), jnp.sin(ang)
    lo = (jnp.arange(HD) % (HD // 2)) < (HD // 4)
    sin_a = jnp.where(lo[None, :], -sin, 0.0)
    sin_b = jnp.where(lo[None, :], 0.0, sin)
    return cos, sin_a, sin_b


def _rope(x, cos, sa, sb):
    return x * cos + pltpu.roll(x, 96, 1) * sa + pltpu.roll(x, 32, 1) * sb


def _rope_t(dy, cos, sa, sb):
    return dy * cos + pltpu.roll(dy * sa, 32, 1) + pltpu.roll(dy * sb, 96, 1)


BAND = 3 * WINDOW
KPAD = T + 2 * WINDOW


def _attn_fill_kv(k_ref, v_ref, cos_ref, sa_ref, sb_ref, kp, vp, kc, vc):
    zeros = jnp.zeros((WINDOW, KVW), BF16)
    kp[0:WINDOW, :] = zeros
    kp[WINDOW + T:KPAD, :] = zeros
    vp[0:WINDOW, :] = zeros
    vp[WINDOW + T:KPAD, :] = zeros
    for hh in range(NKV):
        cs = slice(hh * HD, (hh + 1) * HD)
        for r0 in range(0, T, 512):
            rs = slice(r0, r0 + 512)
            kr = _rope(k_ref[rs, cs], cos_ref[rs, :], sa_ref[rs, :], sb_ref[rs, :])
            kp[WINDOW + r0:WINDOW + r0 + 512, cs] = kr.astype(BF16)
    vp[WINDOW:WINDOW + T, :] = v_ref[0:T, :].astype(BF16)
    kc[...] = k_ref[T:TA, :].astype(BF16)
    vc[...] = v_ref[T:TA, :].astype(BF16)


def _attn_scores(n, h, q_ref, cos_ref, sa_ref, sb_ref, sink_ref, kp, kc):
    kvh = h // GROUP
    r0 = pl.multiple_of(n * WINDOW, WINDOW)
    cos = cos_ref[pl.ds(r0, WINDOW), :]
    sa = sa_ref[pl.ds(r0, WINDOW), :]
    sb = sb_ref[pl.ds(r0, WINDOW), :]
    q_h = _rope(q_ref[:, h * HD:(h + 1) * HD], cos, sa, sb).astype(BF16)
    kb = kp[pl.ds(r0, BAND), kvh * HD:(kvh + 1) * HD]
    kcb = kc[:, kvh * HD:(kvh + 1) * HD]
    nt = (((1,), (1,)), ((), ()))
    s_loc = lax.dot_general(q_h, kb, nt, preferred_element_type=F32) * ATT_SCALE
    s_ctx = lax.dot_general(q_h, kcb, nt, preferred_element_type=F32) * ATT_SCALE
    row = lax.broadcasted_iota(jnp.int32, (WINDOW, BAND), 0)
    col = lax.broadcasted_iota(jnp.int32, (WINDOW, BAND), 1)
    rel = col - WINDOW - row
    kpos = r0 - WINDOW + col
    valid = (jnp.abs(rel) <= WINDOW) & (kpos >= 0) & (kpos < T)
    s_loc = jnp.where(valid, s_loc, NEG_INF)
    sk = sink_ref[0:1, h:h + 1]
    m = jnp.maximum(jnp.maximum(jnp.max(s_loc, -1, keepdims=True), jnp.max(s_ctx, -1, keepdims=True)), sk)
    e_loc = jnp.exp(s_loc - m)
    e_ctx = jnp.exp(s_ctx - m)
    e_sink = jnp.exp(sk - m)
    inv = 1.0 / (jnp.sum(e_loc, -1, keepdims=True) + jnp.sum(e_ctx, -1, keepdims=True) + e_sink)
    return q_h, r0, e_loc * inv, e_ctx * inv, e_sink * inv


def _attn_fwd(proj, sink, tabs):
    cos, sa, sb = tabs

    def kern(q_ref, k_ref, v_ref, cos_ref, sa_ref, sb_ref, sink_ref, o_ref, kp, vp, kc, vc):
        n = pl.program_id(0)

        @pl.when(n == 0)
        def _():
            _attn_fill_kv(k_ref, v_ref, cos_ref, sa_ref, sb_ref, kp, vp, kc, vc)

        for h in range(NH):
            kvh = h // GROUP
            _, r0, p_loc, p_ctx, _ = _attn_scores(n, h, q_ref, cos_ref, sa_ref, sb_ref, sink_ref, kp, kc)
            vb = vp[pl.ds(r0, BAND), kvh * HD:(kvh + 1) * HD]
            vcb = vc[:, kvh * HD:(kvh + 1) * HD]
            o = jnp.dot(p_loc.astype(BF16), vb, preferred_element_type=F32)
            o = o + jnp.dot(p_ctx.astype(BF16), vcb, preferred_element_type=F32)
            o_ref[:, h * HD:(h + 1) * HD] = o.astype(o_ref.dtype)

    full = lambda shape: pl.BlockSpec(shape, lambda n: (0, 0))
    return pl.pallas_call(
        kern,
        name="attn_fwd",
        grid=(T // WINDOW,),
        in_specs=[
            pl.BlockSpec((WINDOW, QW), lambda n: (n, 0)),
            pl.BlockSpec((TA, KVW), lambda n: (0, QW // KVW)),
            pl.BlockSpec((TA, KVW), lambda n: (0, QW // KVW + 1)),
            full((T, HD)), full((T, HD)), full((T, HD)), full((1, NH)),
        ],
        out_specs=pl.BlockSpec((WINDOW, QW), lambda n: (n, 0)),
        out_shape=jax.ShapeDtypeStruct((T, QW), BF16),
        scratch_shapes=[pltpu.VMEM((KPAD, KVW), BF16), pltpu.VMEM((KPAD, KVW), BF16),
                        pltpu.VMEM((C, KVW), BF16), pltpu.VMEM((C, KVW), BF16)],
        compiler_params=_cparams(("arbitrary",)),
    )(proj, proj, proj, cos, sa, sb, sink)


def _attn_bwd(proj, d_attn, sink, tabs):
    cos, sa, sb = tabs
    n_blocks = T // WINDOW

    def kern(q_ref, k_ref, v_ref, do_ref, cos_ref, sa_ref, sb_ref, sink_ref,
             dq_ref, dk_ref, dv_ref, dsink_ref, kp, vp, kc, vc, dkp, dvp, dkc, dvc):
        n = pl.program_id(0)

        @pl.when(n == 0)
        def _():
            _attn_fill_kv(k_ref, v_ref, cos_ref, sa_ref, sb_ref, kp, vp, kc, vc)
            dkp[...] = jnp.zeros_like(dkp)
            dvp[...] = jnp.zeros_like(dvp)
            dkc[...] = jnp.zeros_like(dkc)
            dvc[...] = jnp.zeros_like(dvc)
            dsink_ref[...] = jnp.zeros_like(dsink_ref)

        nt = (((1,), (1,)), ((), ()))
        tn = (((0,), (0,)), ((), ()))
        for h in range(NH):
            kvh = h // GROUP
            cs = slice(kvh * HD, (kvh + 1) * HD)
            q_h, r0, p_loc, p_ctx, p_sink = _attn_scores(n, h, q_ref, cos_ref, sa_ref, sb_ref, sink_ref, kp, kc)
            kb = kp[pl.ds(r0, BAND), cs]
            vb = vp[pl.ds(r0, BAND), cs]
            kcb = kc[:, cs]
            vcb = vc[:, cs]
            do_h = do_ref[:, h * HD:(h + 1) * HD]
            dp_loc = lax.dot_general(do_h, vb, nt, preferred_element_type=F32)
            dp_ctx = lax.dot_general(do_h, vcb, nt, preferred_element_type=F32)
            delta = jnp.sum(p_loc * dp_loc, -1, keepdims=True) + jnp.sum(p_ctx * dp_ctx, -1, keepdims=True)
            ds_loc = (p_loc * (dp_loc - delta) * ATT_SCALE).astype(BF16)
            ds_ctx = (p_ctx * (dp_ctx - delta) * ATT_SCALE).astype(BF16)
            dq = jnp.dot(ds_loc, kb, preferred_element_type=F32) + jnp.dot(ds_ctx, kcb, preferred_element_type=F32)
            cos = cos_ref[pl.ds(r0, WINDOW), :]
            sa_ = sa_ref[pl.ds(r0, WINDOW), :]
            sb_ = sb_ref[pl.ds(r0, WINDOW), :]
            dq_ref[:, h * HD:(h + 1) * HD] = _rope_t(dq, cos, sa_, sb_).astype(dq_ref.dtype)
            dkp[pl.ds(r0, BAND), cs] += lax.dot_general(ds_loc, q_h, tn, preferred_element_type=F32)
            dkc[:, cs] += lax.dot_general(ds_ctx, q_h, tn, preferred_element_type=F32)
            dvp[pl.ds(r0, BAND), cs] += lax.dot_general(p_loc.astype(BF16), do_h, tn, preferred_element_type=F32)
            dvc[:, cs] += lax.dot_general(p_ctx.astype(BF16), do_h, tn, preferred_element_type=F32)
            dsk = -jnp.sum(p_sink * delta, axis=0, keepdims=True)
            dsink_ref[h:h + 1, :] += jnp.broadcast_to(dsk, (1, HD))

        @pl.when(n == n_blocks - 1)
        def _():
            for hh in range(NKV):
                cs = slice(hh * HD, (hh + 1) * HD)
                for r0 in range(0, T, 512):
                    rs = slice(r0, r0 + 512)
                    g = dkp[WINDOW + r0:WINDOW + r0 + 512, cs]
                    dk_ref[rs, cs] = _rope_t(g, cos_ref[rs, :], sa_ref[rs, :], sb_ref[rs, :]).astype(dk_ref.dtype)
            dk_ref[T:TA, :] = dkc[...].astype(dk_ref.dtype)
            dv_ref[0:T, :] = dvp[WINDOW:WINDOW + T, :].astype(dv_ref.dtype)
            dv_ref[T:TA, :] = dvc[...].astype(dv_ref.dtype)

    full = lambda shape: pl.BlockSpec(shape, lambda n: (0, 0))
    return pl.pallas_call(
        kern,
        name="attn_bwd",
        grid=(n_blocks,),
        in_specs=[
            pl.BlockSpec((WINDOW, QW), lambda n: (n, 0)),
            pl.BlockSpec((TA, KVW), lambda n: (0, QW // KVW)),
            pl.BlockSpec((TA, KVW), lambda n: (0, QW // KVW + 1)),
            pl.BlockSpec((WINDOW, QW), lambda n: (n, 0)),
            full((T, HD)), full((T, HD)), full((T, HD)), full((1, NH)),
        ],
        out_specs=[pl.BlockSpec((WINDOW, QW), lambda n: (n, 0)), full((TA, KVW)), full((TA, KVW)), full((NH, HD))],
        out_shape=[jax.ShapeDtypeStruct((T, QW), BF16), jax.ShapeDtypeStruct((TA, KVW), BF16),
                   jax.ShapeDtypeStruct((TA, KVW), BF16), jax.ShapeDtypeStruct((NH, HD), F32)],
        scratch_shapes=[pltpu.VMEM((KPAD, KVW), BF16), pltpu.VMEM((KPAD, KVW), BF16),
                        pltpu.VMEM((C, KVW), BF16), pltpu.VMEM((C, KVW), BF16),
                        pltpu.VMEM((KPAD, KVW), F32), pltpu.VMEM((KPAD, KVW), F32),
                        pltpu.VMEM((C, KVW), F32), pltpu.VMEM((C, KVW), F32)],
        compiler_params=_cparams(("arbitrary",)),
    )(proj, proj, proj, d_attn, cos, sa, sb, sink)


def _s5_prep(a_re, a_im, log_dt, b_re, b_im, c_re, c_im):
    lam = lax.complex(a_re, a_im)
    dt = jnp.exp(log_dt)[..., None]
    lam_bar = jnp.exp(lam * dt)
    b_bar = ((lam_bar - 1.0) / lam)[..., None] * lax.complex(b_re, b_im)
    eye = jnp.eye(GBLK, dtype=F32)

    def lam_rows(v):
        return v.reshape(2, NBLK, 1, BW)

    lam_l = jnp.concatenate([lam_rows(jnp.real(lam_bar)), lam_rows(jnp.imag(lam_bar))], -1)
    lam_l = jnp.broadcast_to(lam_l, (2, NBLK, 8, 2 * BW))

    def b_blocks(v):
        v = v.reshape(2, NBLK, GBLK, SP, SG).transpose(0, 1, 2, 4, 3)
        return (v[:, :, :, :, None, :] * eye[None, None, :, None, :, None]).reshape(2, NBLK, UW, BW)

    bmat = jnp.concatenate([b_blocks(jnp.real(b_bar)), b_blocks(jnp.imag(b_bar))], -1)

    def c_blocks(v):
        v = v.reshape(2, NBLK, GBLK, SG, SP).transpose(0, 1, 2, 4, 3)
        return (v[:, :, :, :, None, :] * eye[None, None, :, None, :, None]).reshape(2, NBLK, BW, UW)

    cmat = jnp.concatenate([c_blocks(c_re), -c_blocks(c_im)], 2)
    return lam_l, bmat, cmat


def _to_seq(lat, ctx):
    w = lat.shape[-1]
    f = jnp.concatenate([ctx, lat], 0)
    b = jnp.concatenate([ctx[::-1], lat[::-1]], 0)
    s = jnp.stack([f, b])
    return s.reshape(2, NSEG, SEGLEN, w).transpose(0, 2, 1, 3).reshape(2, TA, w)


def _from_seq(s):
    w = s.shape[-1]
    s = s.reshape(2, SEGLEN, NSEG, w).transpose(0, 2, 1, 3).reshape(2, TA, w)
    return s[0, C:], s[0, :C], s[1, C:][::-1], s[1, :C][::-1]


def _cmul(ar, ai, br, bi):
    return ar * br - ai * bi, ar * bi + ai * br


def _shift_rows(x, up):
    r = lax.broadcasted_iota(jnp.int32, x.shape, 0)
    if up:
        return jnp.where(r == NSEG - 1, 0.0, pltpu.roll(x, NSEG - 1, 0))
    return jnp.where(r == 0, 0.0, pltpu.roll(x, 1, 0))


RCH = 256


def _s5_fwd(u_seq, lam, bmat, cmat):
    def kern(u_ref, lam_ref, b_ref, c_ref, s_ref, y_ref):
        bm = b_ref[0, 0].astype(BF16)
        for r0 in range(0, TA, RCH):
            s_ref[0, 0, r0:r0 + RCH, :] = jnp.dot(u_ref[0, r0:r0 + RCH, :].astype(BF16), bm, preferred_element_type=F32)
        lr = lam_ref[0, 0, :, 0:BW]
        li = lam_ref[0, 0, :, BW:2 * BW]
        zero = jnp.zeros((NSEG, BW), F32)

        def scan1(j, carry):
            sr, si, pr, pi = carry
            row = pl.multiple_of(j * NSEG, NSEG)
            tr, ti = _cmul(lr, li, sr, si)
            sr = tr + s_ref[0, 0, pl.ds(row, NSEG), 0:BW]
            si = ti + s_ref[0, 0, pl.ds(row, NSEG), BW:2 * BW]
            s_ref[0, 0, pl.ds(row, NSEG), 0:BW] = sr
            s_ref[0, 0, pl.ds(row, NSEG), BW:2 * BW] = si
            pr, pi = _cmul(lr, li, pr, pi)
            return sr, si, pr, pi

        er, ei, lpr, lpi = lax.fori_loop(0, SEGLEN, scan1, (zero, zero, zero + 1.0, zero))
        cr, ci = zero, zero
        for _ in range(NSEG - 1):
            tr, ti = _cmul(lpr, lpi, cr, ci)
            cr, ci = _shift_rows(er + tr, False), _shift_rows(ei + ti, False)

        def scan2(j, carry):
            pr, pi = carry
            row = pl.multiple_of(j * NSEG, NSEG)
            tr, ti = _cmul(pr, pi, cr, ci)
            s_ref[0, 0, pl.ds(row, NSEG), 0:BW] += tr
            s_ref[0, 0, pl.ds(row, NSEG), BW:2 * BW] += ti
            return _cmul(lr, li, pr, pi)

        lax.fori_loop(0, SEGLEN, scan2, (lr, li))
        cm = c_ref[0, 0].astype(BF16)
        for r0 in range(0, TA, RCH):
            y_ref[0, r0:r0 + RCH, :] = jnp.dot(s_ref[0, 0, r0:r0 + RCH, :].astype(BF16), cm, preferred_element_type=F32)

    return pl.pallas_call(
        kern,
        name="s5_fwd",
        grid=(2, NBLK),
        in_specs=[
            pl.BlockSpec((1, TA, UW), lambda d, b: (d, 0, b)),
            pl.BlockSpec((1, 1, 8, 2 * BW), lambda d, b: (d, b, 0, 0)),
            pl.BlockSpec((1, 1, UW, 2 * BW), lambda d, b: (d, b, 0, 0)),
            pl.BlockSpec((1, 1, 2 * BW, UW), lambda d, b: (d, b, 0, 0)),
        ],
        out_specs=[pl.BlockSpec((1, 1, TA, 2 * BW), lambda d, b: (d, b, 0, 0)),
                   pl.BlockSpec((1, TA, UW), lambda d, b: (d, 0, b))],
        out_shape=[jax.ShapeDtypeStruct((2, NBLK, TA, 2 * BW), F32), jax.ShapeDtypeStruct((2, TA, SW), F32)],
        compiler_params=_cparams(("parallel", "parallel")),
    )(u_seq, lam, bmat, cmat)


def _s5_bwd(dy_seq, states, u_seq, lam, bmat, cmat):
    nt = (((1,), (1,)), ((), ()))
    tn = (((0,), (0,)), ((), ()))

    def kern(dy_ref, s_ref, u_ref, lam_ref, b_ref, c_ref, du_ref, dlam_ref, db_ref, dc_ref, g_ref):
        cm = c_ref[0, 0].astype(BF16)
        for r0 in range(0, TA, RCH):
            g_ref[r0:r0 + RCH, :] = lax.dot_general(dy_ref[0, r0:r0 + RCH, :].astype(BF16), cm, nt, preferred_element_type=F32)
        lr = lam_ref[0, 0, :, 0:BW]
        li = -lam_ref[0, 0, :, BW:2 * BW]
        zero = jnp.zeros((NSEG, BW), F32)

        def scan1(jj, carry):
            gr, gi, pr, pi = carry
            row = pl.multiple_of((SEGLEN - 1 - jj) * NSEG, NSEG)
            tr, ti = _cmul(lr, li, gr, gi)
            gr = tr + g_ref[pl.ds(row, NSEG), 0:BW]
            gi = ti + g_ref[pl.ds(row, NSEG), BW:2 * BW]
            g_ref[pl.ds(row, NSEG), 0:BW] = gr
            g_ref[pl.ds(row, NSEG), BW:2 * BW] = gi
            pr, pi = _cmul(lr, li, pr, pi)
            return gr, gi, pr, pi

        br, bi, lpr, lpi = lax.fori_loop(0, SEGLEN, scan1, (zero, zero, zero + 1.0, zero))
        cr, ci = zero, zero
        for _ in range(NSEG - 1):
            tr, ti = _cmul(lpr, lpi, cr, ci)
            cr, ci = _shift_rows(br + tr, True), _shift_rows(bi + ti, True)

        def dlam_terms(gr, gi, sr, si):
            return gr * sr + gi * si, gi * sr - gr * si

        def scan2(jj, carry):
            pr, pi, ar, ai = carry
            j = SEGLEN - 1 - jj
            row = pl.multiple_of(j * NSEG, NSEG)
            prev = pl.multiple_of((j - 1) * NSEG, NSEG)
            tr, ti = _cmul(pr, pi, cr, ci)
            gr = g_ref[pl.ds(row, NSEG), 0:BW] + tr
            gi = g_ref[pl.ds(row, NSEG), BW:2 * BW] + ti
            g_ref[pl.ds(row, NSEG), 0:BW] = gr
            g_ref[pl.ds(row, NSEG), BW:2 * BW] = gi
            dr, di = dlam_terms(gr, gi, s_ref[0, 0, pl.ds(prev, NSEG), 0:BW], s_ref[0, 0, pl.ds(prev, NSEG), BW:2 * BW])
            pr, pi = _cmul(lr, li, pr, pi)
            return pr, pi, ar + dr, ai + di

        pr, pi, ar, ai = lax.fori_loop(0, SEGLEN - 1, scan2, (lr, li, zero, zero))
        tr, ti = _cmul(pr, pi, cr, ci)
        gr = g_ref[0:NSEG, 0:BW] + tr
        gi = g_ref[0:NSEG, BW:2 * BW] + ti
        g_ref[0:NSEG, 0:BW] = gr
        g_ref[0:NSEG, BW:2 * BW] = gi
        last = (SEGLEN - 1) * NSEG
        dr, di = dlam_terms(gr, gi, _shift_rows(s_ref[0, 0, last:last + NSEG, 0:BW], False),
                            _shift_rows(s_ref[0, 0, last:last + NSEG, BW:2 * BW], False))
        dlam_ref[0, 0, :, 0:BW] = ar + dr
        dlam_ref[0, 0, :, BW:2 * BW] = ai + di

        bm = b_ref[0, 0].astype(BF16)
        db = jnp.zeros((UW, 2 * BW), F32)
        dc = jnp.zeros((2 * BW, UW), F32)
        for r0 in range(0, TA, RCH):
            g = g_ref[r0:r0 + RCH, :].astype(BF16)
            du_ref[0, r0:r0 + RCH, :] = lax.dot_general(g, bm, nt, preferred_element_type=F32)
            db = db + lax.dot_general(u_ref[0, r0:r0 + RCH, :].astype(BF16), g, tn, preferred_element_type=F32)
            dc = dc + lax.dot_general(s_ref[0, 0, r0:r0 + RCH, :].astype(BF16), dy_ref[0, r0:r0 + RCH, :].astype(BF16), tn,
                                      preferred_element_type=F32)
        db_ref[0, 0] = db
        dc_ref[0, 0] = dc

    blk4 = lambda shape: pl.BlockSpec((1, 1) + shape, lambda d, b: (d, b, 0, 0))
    cols = pl.BlockSpec((1, TA, UW), lambda d, b: (d, 0, b))
    return pl.pallas_call(
        kern,
        name="s5_bwd",
        grid=(2, NBLK),
        in_specs=[cols, blk4((TA, 2 * BW)), cols, blk4((8, 2 * BW)), blk4((UW, 2 * BW)), blk4((2 * BW, UW))],
        out_specs=[cols, blk4((8, 2 * BW)), blk4((UW, 2 * BW)), blk4((2 * BW, UW))],
        out_shape=[jax.ShapeDtypeStruct((2, TA, SW), F32), jax.ShapeDtypeStruct((2, NBLK, 8, 2 * BW), F32),
                   jax.ShapeDtypeStruct((2, NBLK, UW, 2 * BW), F32), jax.ShapeDtypeStruct((2, NBLK, 2 * BW, UW), F32)],
        scratch_shapes=[pltpu.VMEM((TA, 2 * BW), F32)],
        compiler_params=_cparams(("parallel", "parallel")),
    )(dy_seq, states, u_seq, lam, bmat, cmat)


TR = 256


def _vjp_rows(f, primals, cots, n_row):
    _, pull = jax.vjp(f, *primals)
    g = pull(cots)
    return list(g[:n_row]), list(g[n_row:])


class _GradDict(dict):
    def __init__(self, on_set=None):
        super().__init__()
        self._on_set = on_set

    def __setitem__(self, key, value):
        super().__setitem__(key, value)
        if self._on_set is not None:
            self._on_set(self)


def _local_step(x, ctx, tgt, mod_lat, mod_ctx, wb, sp, on_grad=None):
    sh1, sc1, g1, sh2, sc2, g2 = [mod_lat[:, i * D:(i + 1) * D] for i in range(6)]
    csh1, csc1 = mod_ctx[:, 0:D], mod_ctx[:, D:2 * D]
    tabs = _rope_tables()
    sink = sp["attn_sink"].reshape(1, NH)
    dskip = sp["ssm_d"].reshape(1, SW)
    lg_mix, lb_mix = sp["ln_mix_g"].reshape(1, D), sp["ln_mix_b"].reshape(1, D)
    lg_mlp, lb_mlp = sp["ln_mlp_g"].reshape(1, D), sp["ln_mlp_b"].reshape(1, D)
    b1, b2 = sp["b_mlp1"].reshape(1, DFF), sp["b_mlp2"].reshape(1, D)
    s5_names = ("ssm_a_re", "ssm_a_im", "ssm_log_dt", "ssm_b_re", "ssm_b_im", "ssm_c_re", "ssm_c_im")
    (lam, bmat, cmat), s5_pull = jax.vjp(_s5_prep, *[sp[n] for n in s5_names])

    def ln_mod(rv, vv):
        return [_f_ln_mod(rv[0], vv[0], vv[1])], []

    h_lat, = _rowwise(ln_mod, [(x, D, 0, 0)], [sc1, sh1], [(D, BF16)], [], nrows=T, tr=TR, name="ln1_lat")
    h_ctx, = _rowwise(ln_mod, [(ctx, D, 0, 0)], [csc1, csh1], [(D, BF16)], [], nrows=C, tr=TR, name="ln1_ctx")
    h1 = jnp.concatenate([h_lat, h_ctx], 0)
    proj = _matmul(h1, wb["w_in"], mode="nn", name="proj", tm=768, tn=512)
    attn = _attn_fwd(proj, sink, tabs)
    u_all = proj[:, QW + 2 * KVW:QW + 2 * KVW + SW]
    u_lat, u_ctx = u_all[:T], u_all[T:]
    u_seq = _to_seq(u_lat, u_ctx)
    states, y_seq = _s5_fwd(u_seq, lam, bmat, cmat)
    y_f, _, y_b, _ = _from_seq(y_seq)

    def ssm_pre(rv, vv):
        s = _f_ssm_pre(rv[0], rv[1], rv[2], vv[0])
        return [s, _gelu(s)], []

    ssm, ge = _rowwise(ssm_pre, [(u_lat, SW, 0, 0), (y_f, SW, 0, 0), (y_b, SW, 0, 0)], [dskip],
                       [(SW, F32), (SW, BF16)], [], nrows=T, tr=TR, name="ssm_pre")
    z = _matmul(ge, wb["w_glu"], mode="nn", name="glu_mm", tm=1024, tn=1024)

    def glu_act(rv, vv):
        return [_f_glu(rv[0])], []

    glu, = _rowwise(glu_act, [(z, 2 * SW, 0, 0)], [], [(SW, BF16)], [], nrows=T, tr=TR, name="glu_act")
    attn_d = _matmul(attn, wb["w_attn_up"], mode="nn", name="attn_up", tm=1024, tn=512)
    ssm_d = _matmul(glu, wb["w_ssm_up"], mode="nn", name="ssm_up", tm=1024, tn=512)
    ga_cb, gs_cb = (QW + 2 * KVW + SW) // D, (QW + 2 * KVW + SW) // D + 1

    def mix(rv, vv):
        return [_f_mix(*rv)], []

    mixv, = _rowwise(mix, [(proj, D, ga_cb, 0), (proj, D, gs_cb, 0), (attn_d, D, 0, 0), (ssm_d, D, 0, 0)], [],
                     [(D, BF16)], [], nrows=T, tr=TR, name="mix")
    y = _matmul(mixv, wb["w_out"], mode="nn", name="out_proj", tm=1024, tn=512)

    def post1(rv, vv):
        x1, h2 = _f_post1(rv[0], rv[1], *vv)
        return [x1, h2], []

    x1, h2 = _rowwise(post1, [(x, D, 0, 0), (y, D, 0, 0)], [g1, lg_mix, lb_mix, sc2, sh2],
                      [(D, F32), (D, BF16)], [], nrows=T, tr=TR, name="post1")

    def relu_sq(acc):
        r = jnp.maximum(acc, 0.0)
        return r, r * r

    r_act, act = _matmul(h2, wb["w_mlp1"], mode="nn", name="mlp1", tm=1024, tn=512, bias=b1,
                         out_dtypes=(BF16, BF16), epilogue=relu_sq)
    mlp = _matmul(act, wb["w_mlp2"], mode="nn", name="mlp2", tm=1024, tn=512, tk=2048)

    def loss_fb(rv, vv):
        x1_t, mlp_t, tgt_t = rv
        g2_v, lg_v, lb_v, b2_v = vv
        f = lambda a, m, g, p, q, b: _f_loss(a, m, tgt_t, g, p, q, b)
        val, grads = jax.value_and_grad(f, argnums=(0, 1, 2, 3, 4, 5))(x1_t, mlp_t, g2_v, lg_v, lb_v, b2_v)
        dx1, dmlp, dg2, dlg, dlb, db2 = grads
        return [dx1, dmlp], [jnp.reshape(val, (1, 1)), dg2, dlg, dlb, db2]

    dx1_a, d_mlp, loss_p, d_g2, d_lg_mlp, d_lb_mlp, d_b2 = _rowwise(
        loss_fb, [(x1, D, 0, 0), (mlp, D, 0, 0), (tgt, D, 0, 0)], [g2, lg_mlp, lb_mlp, b2],
        [(D, F32), (D, BF16)], [(1, 1), (1, D), (1, D), (1, D), (1, D)], nrows=T, tr=TR, name="loss_fb")

    gw = _GradDict(on_grad)
    gw["w_mlp2"] = _matmul(act, d_mlp, mode="tn", name="dw_mlp2", out_dtypes=(BF16,), tm=512, tn=1024, tk=1024)
    da, = (_matmul(d_mlp, wb["w_mlp2"], mode="nt", name="d_act", out_dtypes=(BF16,), tm=1024, tn=512,
                   extras=(r_act,), epilogue=lambda acc, r: (acc * (2.0 * r.astype(F32)),), after=(gw["w_mlp2"],)),)
    ones = jnp.ones((8, T), BF16)
    d_b1 = _matmul(ones, da, mode="nn", name="db_mlp1", tm=8, tn=2048)[0:1]
    gw["w_mlp1"] = _matmul(h2, da, mode="tn", name="dw_mlp1", out_dtypes=(BF16,), tm=512, tn=1024, tk=1024)
    dh2 = _matmul(da, wb["w_mlp1"], mode="nt", name="d_h2", tm=1024, tn=512, tk=2048, after=(gw["w_mlp1"],))

    def post1_b(rv, vv):
        x_t, y_t, dx1_t, dh2_t = rv
        gr, gv = _vjp_rows(_f_post1, (x_t, y_t, *vv), (dx1_t, dh2_t), 2)
        return [gr[0], gr[1]], gv

    dx_a, dy, d_g1, d_lg_mix, d_lb_mix, d_sc2, d_sh2 = _rowwise(
        post1_b, [(x, D, 0, 0), (y, D, 0, 0), (dx1_a, D, 0, 0), (dh2, D, 0, 0)], [g1, lg_mix, lb_mix, sc2, sh2],
        [(D, F32), (D, BF16)], [(1, D)] * 5, nrows=T, tr=TR, name="post1_bwd")
    gw["w_out"] = _matmul(mixv, dy, mode="tn", name="dw_out", out_dtypes=(BF16,), tm=512, tn=1024, tk=1024)
    dmix = _matmul(dy, wb["w_out"], mode="nt", name="d_mix", tm=1024, tn=512, after=(gw["w_out"],))

    def mix_b(rv, vv):
        gr, _ = _vjp_rows(_f_mix, tuple(rv[:4]), rv[4], 4)
        return gr, []

    d_ga, d_gs, d_attn_d, d_ssm_d = _rowwise(
        mix_b, [(proj, D, ga_cb, 0), (proj, D, gs_cb, 0), (attn_d, D, 0, 0), (ssm_d, D, 0, 0), (dmix, D, 0, 0)], [],
        [(D, BF16)] * 4, [], nrows=T, tr=TR, name="mix_bwd")
    gw["w_attn_up"] = _matmul(attn, d_attn_d, mode="tn", name="dw_attn_up", out_dtypes=(BF16,), tm=512, tn=1024, tk=1024)
    d_attn = _matmul(d_attn_d, wb["w_attn_up"], mode="nt", name="d_attn", out_dtypes=(BF16,), tm=1024, tn=512)
    gw["w_ssm_up"] = _matmul(glu, d_ssm_d, mode="tn", name="dw_ssm_up", out_dtypes=(BF16,), tm=512, tn=1024, tk=1024)
    d_glu = _matmul(d_ssm_d, wb["w_ssm_up"], mode="nt", name="d_glu", tm=1024, tn=512, after=(gw["w_attn_up"], gw["w_ssm_up"]))

    def glu_b(rv, vv):
        gr, _ = _vjp_rows(_f_glu, (rv[0],), rv[1], 1)
        return gr, []

    dz, = _rowwise(glu_b, [(z, 2 * SW, 0, 0), (d_glu, SW, 0, 0)], [], [(2 * SW, BF16)], [], nrows=T, tr=TR, name="glu_bwd")
    gw["w_glu"] = _matmul(ge, dz, mode="tn", name="dw_glu", out_dtypes=(BF16,), tm=512, tn=1024, tk=1024)
    d_ge = _matmul(dz, wb["w_glu"], mode="nt", name="d_ge", tm=1024, tn=512, after=(gw["w_glu"],))

    def ssm_pre_b(rv, vv):
        u_t, yf_t, yb_t, dge_t = rv
        f = lambda u, yf, yb, dk: _gelu(_f_ssm_pre(u, yf, yb, dk))
        gr, gv = _vjp_rows(f, (u_t, yf_t, yb_t, vv[0]), dge_t, 3)
        return [gr[0], gr[1]], gv

    du_dir, d_ssm, d_dskip = _rowwise(
        ssm_pre_b, [(u_lat, SW, 0, 0), (y_f, SW, 0, 0), (y_b, SW, 0, 0), (d_ge, SW, 0, 0)], [dskip],
        [(SW, F32), (SW, F32)], [(1, SW)], nrows=T, tr=TR, name="ssm_pre_bwd")
    dy_seq = _to_seq(d_ssm, jnp.zeros((C, SW), F32))
    du_seq, dlam, dbmat, dcmat = _s5_bwd(dy_seq, states, u_seq, lam, bmat, cmat)
    du_f, duc_f, du_b, duc_b = _from_seq(du_seq)
    du_all = jnp.concatenate([du_dir + du_f + du_b, duc_f + duc_b], 0).astype(BF16)
    s5_grads = s5_pull((dlam, dbmat, dcmat))

    dq, dk, dv, dsink = _attn_bwd(proj, d_attn, sink, tabs)
    zc = lambda w: jnp.zeros((C, w), BF16)
    dproj = jnp.concatenate([
        jnp.concatenate([dq, zc(QW)], 0), dk, dv, du_all,
        jnp.concatenate([d_ga, zc(D)], 0), jnp.concatenate([d_gs, zc(D)], 0)], 1)
    gw["w_in"] = _matmul(h1, dproj, mode="tn", name="dw_in", out_dtypes=(BF16,), tm=512, tn=1536, tk=768)
    dh1 = _matmul(dproj, wb["w_in"], mode="nt", name="d_h1", tm=768, tn=512, tk=2048, after=(gw["w_in"],))

    def ln1_b(rv, vv):
        x_t, dh_t, dxa_t = rv
        gr, gv = _vjp_rows(_f_ln_mod, (x_t, vv[0], vv[1]), dh_t, 1)
        return [gr[0] + dxa_t], gv

    grad_x, d_sc1, d_sh1 = _rowwise(ln1_b, [(x, D, 0, 0), (dh1, D, 0, 0), (dx_a, D, 0, 0)], [sc1, sh1],
                                    [(D, F32)], [(1, D), (1, D)], nrows=T, tr=TR, name="ln1_lat_bwd")

    def ln1c_b(rv, vv):
        _, gv = _vjp_rows(_f_ln_mod, (rv[0], vv[0], vv[1]), rv[1], 1)
        return [], gv

    d_csc1, d_csh1 = _rowwise(ln1c_b, [(ctx, D, 0, 0), (dh1, D, 0, T // TR)], [csc1, csh1],
                              [], [(1, D), (1, D)], nrows=C, tr=TR, name="ln1_ctx_bwd")

    d_mod_lat = jnp.concatenate([d_sh1, d_sc1, d_g1, d_sh2, d_sc2, d_g2], 1)
    zv = jnp.zeros((1, D), F32)
    d_mod_ctx = jnp.concatenate([d_csh1, d_csc1, zv, zv, zv, zv], 1)
    gs = {n: g for n, g in zip(s5_names, s5_grads)}
    gs["attn_sink"] = dsink[:, 0]
    gs["ssm_d"] = d_dskip
    gs["ln_mix_g"], gs["ln_mix_b"] = d_lg_mix, d_lb_mix
    gs["ln_mlp_g"], gs["ln_mlp_b"] = d_lg_mlp, d_lb_mlp
    gs["b_mlp1"], gs["b_mlp2"] = d_b1, d_b2
    return loss_p, grad_x, d_mod_lat, d_mod_ctx, gw, gs


def _my_pos():
    return lax.axis_index("x"), lax.axis_index("y"), lax.axis_index("c")


def _flip(p, bit):
    return 1 - p if bit else p


def _peer(pos, k):
    x, y, c = pos
    return (_flip(x, (k >> 2) & 1), _flip(y, (k >> 1) & 1), _flip(c, k & 1))


def _lin(pos):
    return 4 * pos[0] + 2 * pos[1] + pos[2]


def _allgather_small(v, name):
    r, w = v.shape

    def body(v_ref, out_ref, send_sems, recv_sems, local_sem):
        me = _my_pos()
        mine = pltpu.make_async_copy(v_ref, out_ref.at[_lin(me)], local_sem)
        mine.start()
        sends = []
        for k in range(1, N_DEV):
            cp = pltpu.make_async_remote_copy(src_ref=v_ref, dst_ref=out_ref.at[_lin(me)], send_sem=send_sems.at[k - 1],
                                              recv_sem=recv_sems.at[k - 1], device_id=_peer(me, k), device_id_type=MESH)
            cp.start()
            sends.append(cp)
        for k in range(1, N_DEV):
            peer = _peer(me, k)
            pltpu.make_async_remote_copy(src_ref=v_ref, dst_ref=out_ref.at[_lin(peer)], send_sem=send_sems.at[k - 1],
                                         recv_sem=recv_sems.at[k - 1], device_id=peer, device_id_type=MESH).wait_recv()
        for cp in sends:
            cp.wait_send()
        mine.wait()

    return pl.pallas_call(
        body,
        name=name,
        out_shape=jax.ShapeDtypeStruct((N_DEV, r, w), v.dtype),
        in_specs=[pl.BlockSpec(memory_space=pltpu.VMEM)],
        out_specs=pl.BlockSpec(memory_space=pltpu.VMEM),
        scratch_shapes=[pltpu.SemaphoreType.DMA((N_DEV - 1,)), pltpu.SemaphoreType.DMA((N_DEV - 1,)), pltpu.SemaphoreType.DMA],
        compiler_params=pltpu.CompilerParams(vmem_limit_bytes=VMEM_LIMIT_BYTES),
    )(v)


def _block_of(ref, kind, idx, n):
    start = pl.multiple_of(idx * n, 128)
    if kind == "col":
        return ref.at[:, pl.ds(start, n)]
    return ref.at[pl.ds(start, n), :]


def _allgather_weights(shards, kinds):
    nt = len(shards)
    out_shape = []
    for s, kind in zip(shards, kinds):
        k, n = s.shape
        out_shape.append(jax.ShapeDtypeStruct((k, n * N_DEV) if kind == "col" else (k * N_DEV, n), s.dtype))

    def body(*refs):
        ins, outs = refs[:nt], refs[nt:2 * nt]
        send_sems, recv_sems, local_sems = refs[2 * nt:]
        x, y, c = _my_pos()
        me, sibling = (x, y, c), (x, y, 1 - c)
        chips = [(1 - x, y), (x, 1 - y), (1 - x, 1 - y)]

        def blk(t, pos):
            n = shards[t].shape[1] if kinds[t] == "col" else shards[t].shape[0]
            return _block_of(outs[t], kinds[t], _lin(pos), n)

        def copy(t, k, block, to, src=None):
            return pltpu.make_async_remote_copy(src_ref=blk(t, block) if src is None else src, dst_ref=blk(t, block),
                                                send_sem=send_sems.at[t, k], recv_sem=recv_sems.at[t, k],
                                                device_id=to, device_id_type=MESH)

        local, sends = [], []
        for t in range(nt):
            mine = pltpu.make_async_copy(ins[t], blk(t, me), local_sems.at[t])
            mine.start()
            local.append(mine)
            first = [copy(t, 0, me, sibling, src=ins[t])]
            first += [copy(t, 1 + j, me, (*chip, c), src=ins[t]) for j, chip in enumerate(chips)]
            for cp in first:
                cp.start()
            sends += first
        for t in range(nt):
            for j, chip in enumerate(chips):
                copy(t, 1 + j, (*chip, c), me).wait_recv()
                fwd = copy(t, 4 + j, (*chip, c), sibling)
                fwd.start()
                sends.append(fwd)
        for t in range(nt):
            copy(t, 0, sibling, me).wait_recv()
            for j, chip in enumerate(chips):
                copy(t, 4 + j, (*chip, 1 - c), me).wait_recv()
        for cp in sends:
            cp.wait_send()
        for cp in local:
            cp.wait()

    any_spec = pl.BlockSpec(memory_space=pl.ANY)
    return pl.pallas_call(
        body,
        name="allgather_weights",
        out_shape=out_shape,
        in_specs=[any_spec] * nt,
        out_specs=[any_spec] * nt,
        scratch_shapes=[pltpu.SemaphoreType.DMA((nt, N_DEV - 1)), pltpu.SemaphoreType.DMA((nt, N_DEV - 1)),
                        pltpu.SemaphoreType.DMA((nt,))],
    )(*shards)


def _handshake(peers):
    barrier = pltpu.get_barrier_semaphore()
    for peer in peers:
        pl.semaphore_signal(barrier, inc=1, device_id=peer, device_id_type=MESH)
    pl.semaphore_wait(barrier, len(peers))


def _allgather_weights_seq(shards, kinds, name, collective_id):
    nt = len(shards)
    hbm = pltpu.MemorySpace.HBM
    ins = [jax.new_ref(s, memory_space=hbm) for s in shards]
    outs = []
    for s, kind in zip(shards, kinds):
        k, n = s.shape
        shape = (k, n * N_DEV) if kind == "col" else (k * N_DEV, n)
        outs.append(jax.empty_ref(jax.ShapeDtypeStruct(shape, s.dtype), memory_space=hbm))

    @functools.partial(
        pl.kernel, mesh=plsc.ScalarSubcoreMesh(axis_name="seq", num_cores=1), name=name,
        scratch_types=(pltpu.SemaphoreType.DMA((nt, N_DEV - 1)), pltpu.SemaphoreType.DMA((nt, N_DEV - 1)),
                       pltpu.SemaphoreType.DMA((nt,))),
        compiler_params=pltpu.CompilerParams(collective_id=collective_id))
    def launch(send_sems, recv_sems, local_sems):
        x, y, c = _my_pos()
        me, sibling = (x, y, c), (x, y, 1 - c)
        chips = [(1 - x, y), (x, 1 - y), (1 - x, 1 - y)]
        _handshake([sibling] + [(*chip, c) for chip in chips])

        def blk(t, pos):
            n = shards[t].shape[1] if kinds[t] == "col" else shards[t].shape[0]
            return _block_of(outs[t], kinds[t], _lin(pos), n)

        def copy(t, k, block, to, src=None):
            return pltpu.make_async_remote_copy(src_ref=blk(t, block) if src is None else src, dst_ref=blk(t, block),
                                                send_sem=send_sems.at[t, k], recv_sem=recv_sems.at[t, k],
                                                device_id=to, device_id_type=MESH)

        local, sends = [], []
        for t in range(nt):
            mine = pltpu.make_async_copy(ins[t], blk(t, me), local_sems.at[t])
            mine.start()
            local.append(mine)
            first = [copy(t, 0, me, sibling, src=ins[t])]
            first += [copy(t, 1 + j, me, (*chip, c), src=ins[t]) for j, chip in enumerate(chips)]
            for cp in first:
                cp.start()
            sends += first
        for t in range(nt):
            for j, chip in enumerate(chips):
                copy(t, 1 + j, (*chip, c), me).wait_recv()
                fwd = copy(t, 4 + j, (*chip, c), sibling)
                fwd.start()
                sends.append(fwd)
        for t in range(nt):
            copy(t, 0, sibling, me).wait_recv()
            for j, chip in enumerate(chips):
                copy(t, 4 + j, (*chip, 1 - c), me).wait_recv()
        for cp in sends:
            cp.wait_send()
        for cp in local:
            cp.wait()

    launch()
    return [o[...] for o in outs]


def _allgather_small_seq(v, name, collective_id):
    hbm = pltpu.MemorySpace.HBM
    src = jax.new_ref(v, memory_space=hbm)
    out = jax.empty_ref(jax.ShapeDtypeStruct((N_DEV,) + v.shape, v.dtype), memory_space=hbm)

    @functools.partial(
        pl.kernel, mesh=plsc.ScalarSubcoreMesh(axis_name="seq", num_cores=1), name=name,
        scratch_types=(pltpu.SemaphoreType.DMA((N_DEV - 1,)), pltpu.SemaphoreType.DMA((N_DEV - 1,)), pltpu.SemaphoreType.DMA),
        compiler_params=pltpu.CompilerParams(collective_id=collective_id))
    def launch(send_sems, recv_sems, local_sem):
        me = _my_pos()
        _handshake([_peer(me, k) for k in range(1, N_DEV)])
        mine = pltpu.make_async_copy(src, out.at[_lin(me)], local_sem)
        mine.start()
        sends = []
        for k in range(1, N_DEV):
            cp = pltpu.make_async_remote_copy(src_ref=src, dst_ref=out.at[_lin(me)], send_sem=send_sems.at[k - 1],
                                              recv_sem=recv_sems.at[k - 1], device_id=_peer(me, k), device_id_type=MESH)
            cp.start()
            sends.append(cp)
        for k in range(1, N_DEV):
            peer = _peer(me, k)
            pltpu.make_async_remote_copy(src_ref=src, dst_ref=out.at[_lin(peer)], send_sem=send_sems.at[k - 1],
                                         recv_sem=recv_sems.at[k - 1], device_id=peer, device_id_type=MESH).wait_recv()
        for cp in sends:
            cp.wait_send()
        mine.wait()

    launch()
    return out[...]


def _scatter_grads_seq(grads, kinds, name, collective_id):
    nt = len(grads)
    hbm = pltpu.MemorySpace.HBM
    shard_shapes = []
    for g, kind in zip(grads, kinds):
        k, n = g.shape
        shard_shapes.append((k, n // N_DEV) if kind == "col" else (k // N_DEV, n))
    ins = [jax.new_ref(g, memory_space=hbm) for g in grads]
    outs = [jax.empty_ref(jax.ShapeDtypeStruct((N_DEV,) + s, g.dtype), memory_space=hbm) for s, g in zip(shard_shapes, grads)]

    @functools.partial(
        pl.kernel, mesh=plsc.ScalarSubcoreMesh(axis_name="seq", num_cores=1), name=name,
        scratch_types=(pltpu.SemaphoreType.DMA((nt, N_DEV - 1)), pltpu.SemaphoreType.DMA((nt, N_DEV - 1)),
                       pltpu.SemaphoreType.DMA((nt,))),
        compiler_params=pltpu.CompilerParams(collective_id=collective_id))
    def launch(send_sems, recv_sems, local_sems):
        me = _my_pos()
        _handshake([_peer(me, k) for k in range(1, N_DEV)])

        def blk(t, pos):
            n = shard_shapes[t][1] if kinds[t] == "col" else shard_shapes[t][0]
            return _block_of(ins[t], kinds[t], _lin(pos), n)

        local, sends = [], []
        for t in range(nt):
            cp = pltpu.make_async_copy(blk(t, me), outs[t].at[_lin(me)], local_sems.at[t])
            cp.start()
            local.append(cp)
            for k in range(1, N_DEV):
                peer = _peer(me, k)
                cp = pltpu.make_async_remote_copy(src_ref=blk(t, peer), dst_ref=outs[t].at[_lin(me)], send_sem=send_sems.at[t, k - 1],
                                                  recv_sem=recv_sems.at[t, k - 1], device_id=peer, device_id_type=MESH)
                cp.start()
                sends.append(cp)
        for t in range(nt):
            for k in range(1, N_DEV):
                peer = _peer(me, k)
                pltpu.make_async_remote_copy(src_ref=blk(t, me), dst_ref=outs[t].at[_lin(peer)], send_sem=send_sems.at[t, k - 1],
                                             recv_sem=recv_sems.at[t, k - 1], device_id=peer, device_id_type=MESH).wait_recv()
        for cp in sends:
            cp.wait_send()
        for cp in local:
            cp.wait()

    launch()
    return [o[...] for o in outs]


def _scatter_grads(grads, kinds):
    nt = len(grads)
    shard_shapes = []
    for g, kind in zip(grads, kinds):
        k, n = g.shape
        shard_shapes.append((k, n // N_DEV) if kind == "col" else (k // N_DEV, n))

    def body(*refs):
        ins, outs = refs[:nt], refs[nt:2 * nt]
        send_sems, recv_sems, local_sems = refs[2 * nt:]
        me = _my_pos()

        def blk(t, pos):
            n = shard_shapes[t][1] if kinds[t] == "col" else shard_shapes[t][0]
            return _block_of(ins[t], kinds[t], _lin(pos), n)

        local, sends = [], []
        for t in range(nt):
            cp = pltpu.make_async_copy(blk(t, me), outs[t].at[_lin(me)], local_sems.at[t])
            cp.start()
            local.append(cp)
            for k in range(1, N_DEV):
                peer = _peer(me, k)
                cp = pltpu.make_async_remote_copy(src_ref=blk(t, peer), dst_ref=outs[t].at[_lin(me)], send_sem=send_sems.at[t, k - 1],
                                                  recv_sem=recv_sems.at[t, k - 1], device_id=peer, device_id_type=MESH)
                cp.start()
                sends.append(cp)
        for t in range(nt):
            for k in range(1, N_DEV):
                peer = _peer(me, k)
                pltpu.make_async_remote_copy(src_ref=blk(t, me), dst_ref=outs[t].at[_lin(peer)], send_sem=send_sems.at[t, k - 1],
                                             recv_sem=recv_sems.at[t, k - 1], device_id=peer, device_id_type=MESH).wait_recv()
        for cp in sends:
            cp.wait_send()
        for cp in local:
            cp.wait()

    any_spec = pl.BlockSpec(memory_space=pl.ANY)
    return pl.pallas_call(
        body,
        name="scatter_grads",
        out_shape=[jax.ShapeDtypeStruct((N_DEV,) + s, g.dtype) for s, g in zip(shard_shapes, grads)],
        in_specs=[any_spec] * nt,
        out_specs=[any_spec] * nt,
        scratch_shapes=[pltpu.SemaphoreType.DMA((nt, N_DEV - 1)), pltpu.SemaphoreType.DMA((nt, N_DEV - 1)),
                        pltpu.SemaphoreType.DMA((nt,))],
    )(*grads)


def _adam(g_slots, w, m, v, *, tr, name):
    ns, r, wd = g_slots.shape
    tr = min(tr, r)
    assert r % tr == 0, (name, r, tr)
    c1 = 1.0 - ADAM_B1 ** ADAM_STEP
    c2 = 1.0 - ADAM_B2 ** ADAM_STEP

    def kern(g_ref, w_ref, m_ref, v_ref, go_ref, d_ref, mo_ref, vo_ref):
        g = g_ref[0].astype(F32)
        for s in range(1, ns):
            g = g + g_ref[s].astype(F32)
        m_new = ADAM_B1 * m_ref[...] + (1.0 - ADAM_B1) * g
        v_new = ADAM_B2 * v_ref[...] + (1.0 - ADAM_B2) * (g * g)
        m_hat = m_new / c1
        v_hat = v_new / c2
        go_ref[...] = g
        d_ref[...] = -ADAM_LR * (m_hat / (jnp.sqrt(v_hat) + ADAM_EPS) + ADAM_WD * w_ref[...])
        mo_ref[...] = m_new
        vo_ref[...] = v_new

    tile = pl.BlockSpec((tr, wd), lambda i: (i, 0))
    return pl.pallas_call(
        kern,
        name=name,
        grid=(r // tr,),
        in_specs=[pl.BlockSpec((ns, tr, wd), lambda i: (0, i, 0)), tile, tile, tile],
        out_specs=[tile] * 4,
        out_shape=[jax.ShapeDtypeStruct((r, wd), F32)] * 4,
        compiler_params=_cparams(("parallel",)),
    )(g_slots, w, m, v)


SMALL = ("c_ctx", "b_ada", "attn_sink", "ssm_a_re", "ssm_a_im", "ssm_log_dt", "ssm_b_re", "ssm_b_im", "ssm_c_re", "ssm_c_im",
         "ssm_d", "ln_mix_g", "ln_mix_b", "b_mlp1", "b_mlp2", "ln_mlp_g", "ln_mlp_b")
BIG = ("w_in", "w_glu", "w_attn_up", "w_ssm_up", "w_out", "w_mlp1", "w_mlp2")
BIG_KIND = ("col", "col", "col", "col", "row", "col", "row")
AG_GROUPS = (("w_in",), ("w_glu", "w_attn_up", "w_ssm_up", "w_out"), ("w_mlp1",), ("w_mlp2",))
AG_COLLECTIVE_ID0 = 1
RS_GROUPS = (("w_mlp2",), ("w_mlp1",), ("w_out", "w_attn_up", "w_ssm_up", "w_glu"), ("w_in",))
RS_COLLECTIVE_ID0 = AG_COLLECTIVE_ID0 + len(AG_GROUPS)
SMALL_EARLY = ("ssm_a_re", "ssm_a_im", "ssm_log_dt", "ssm_b_re", "ssm_b_im", "ssm_c_re", "ssm_c_im", "ssm_d")
SMALL_LATE = tuple(n for n in SMALL if n not in SMALL_EARLY)
SMALL_COLLECTIVE_ID0 = RS_COLLECTIVE_ID0 + len(RS_GROUPS)
LANES = 128


def _pack(parts):
    rows = []
    for p in parts:
        flat = p.reshape(-1).astype(F32)
        pad = (-flat.shape[0]) % LANES
        rows.append(jnp.pad(flat, (0, pad)).reshape(-1, LANES))
    packed = jnp.concatenate(rows, 0)
    return jnp.pad(packed, ((0, (-packed.shape[0]) % 8), (0, 0)))


def _unpack(packed, shapes):
    out, r0 = [], 0
    for s in shapes:
        n = math.prod(s)
        nr = -(-n // LANES)
        out.append(packed[r0:r0 + nr].reshape(-1)[:n].reshape(s))
        r0 += nr
    return out


WEIGHTS = ("c_ctx", "w_ada", "b_ada", "w_in", "attn_sink", "ssm_a_re", "ssm_a_im", "ssm_log_dt", "ssm_b_re", "ssm_b_im",
           "ssm_c_re", "ssm_c_im", "ssm_d", "w_glu", "w_attn_up", "w_ssm_up", "w_out", "ln_mix_g", "ln_mix_b", "w_mlp1",
           "b_mlp1", "w_mlp2", "b_mlp2", "ln_mlp_g", "ln_mlp_b")
ADA_COLS = 6 * D // N_DEV


def _step(x, c, ctx, loss_target, p, m, v):
    me = _lin(_my_pos())
    x2, ctx2, tgt2 = x[0], ctx[0], loss_target[0]

    wb = {}
    for gi, group in enumerate(AG_GROUPS):
        full = _allgather_weights_seq([p[n][0].astype(BF16) for n in group], [BIG_KIND[BIG.index(n)] for n in group],
                                      "allgather_seq%d" % gi, AG_COLLECTIVE_ID0 + gi)
        wb.update(zip(group, full))

    c_all = _allgather_small(jnp.broadcast_to(c, (8, D)), "gather_c")[:, 0, :]
    cc = p["c_ctx"].reshape(1, D)
    s_in = jnp.concatenate([c_all, cc, jnp.zeros((7, D), F32)], 0)
    s_act, = _rowwise(lambda rv, vv: ([_silu(rv[0])], []), [(s_in, D, 0, 0)], [], [(D, F32)], [], nrows=16, tr=16, name="silu_c")
    b_mine = lax.dynamic_slice_in_dim(p["b_ada"], me * ADA_COLS, ADA_COLS, axis=1)
    mod_part = _matmul(s_act, p["w_ada"][0], mode="nn", name="ada_fwd", tm=16, tn=512, bias=b_mine)
    mod_all = _allgather_small(mod_part, "gather_mod")
    mod_lat = lax.dynamic_index_in_dim(mod_all, me, axis=1, keepdims=False).reshape(1, 6 * D)
    mod_ctx = mod_all[:, 8, :].reshape(1, 6 * D)

    sp = {n: p[n][0] for n in SMALL if n not in ("c_ctx", "b_ada")}
    recv = {}

    def on_grad(gw):
        for gi, group in enumerate(RS_GROUPS):
            if group[0] not in recv and all(n in gw for n in group):
                slots = _scatter_grads_seq([gw[n] for n in group], [BIG_KIND[BIG.index(n)] for n in group],
                                           "scatter_seq%d" % gi, RS_COLLECTIVE_ID0 + gi)
                recv.update(zip(group, slots))

    loss_p, grad_x, d_mod_lat, d_mod_ctx, gw, gs = _local_step(x2, ctx2, tgt2, mod_lat, mod_ctx, wb, sp, on_grad)

    g_early = _allgather_small_seq(_pack([gs[n] for n in SMALL_EARLY]), "gather_small_early", SMALL_COLLECTIVE_ID0)

    dm = jnp.concatenate([d_mod_lat, d_mod_ctx, jnp.zeros((6, 6 * D), F32)], 0)
    dm_all = _allgather_small_seq(dm, "gather_dmod", SMALL_COLLECTIVE_ID0 + 1)
    dm2 = jnp.concatenate([dm_all[:, 0, :], dm_all[:, 1, :]], 0)
    dm2_mine = lax.dynamic_slice_in_dim(dm2, me * ADA_COLS, ADA_COLS, axis=1)
    s2 = jnp.concatenate([s_act[0:8], jnp.broadcast_to(s_act[8:9], (8, D))], 0)
    g_w_ada = _matmul(s2, dm2_mine, mode="tn", name="dw_ada", tm=512, tn=ADA_COLS)
    dsc_part = _matmul(dm2_mine[8:16], p["w_ada"][0], mode="nt", name="d_silu_cctx", tm=8, tn=512)

    def cctx_b(rv, vv):
        _, pull = jax.vjp(_silu, vv[0])
        return [], [pull(jnp.sum(rv[0], axis=0, keepdims=True))[0]]

    g_cctx, = _rowwise(cctx_b, [(dsc_part, D, 0, 0)], [cc], [], [(1, D)], nrows=8, tr=8, name="cctx_bwd")
    gs["c_ctx"] = g_cctx
    gs["b_ada"] = d_mod_lat + d_mod_ctx

    res = {}
    for n in BIG:
        res[n] = _adam(recv[n], p[n][0], m[n][0], v[n][0], tr=256, name="adam_" + n)
    res["w_ada"] = _adam(g_w_ada[None], p["w_ada"][0], m["w_ada"][0], v["w_ada"][0], tr=256, name="adam_w_ada")

    g_late = _allgather_small_seq(_pack([gs[n] for n in SMALL_LATE]), "gather_small_late", SMALL_COLLECTIVE_ID0 + 2)
    for names, g_pack, tag in ((SMALL_EARLY, g_early, "early"), (SMALL_LATE, g_late, "late")):
        sm = _adam(g_pack, _pack([p[n] for n in names]), _pack([m[n] for n in names]), _pack([v[n] for n in names]),
                   tr=g_pack.shape[1], name="adam_small_" + tag)
        shapes = [p[n].shape for n in names]
        for j, outs in enumerate(zip(*[_unpack(a, shapes) for a in sm])):
            res[names[j]] = outs

    loss = lax.psum(loss_p[0, 0], ("x", "y", "c"))
    outs = [loss, grad_x[None]]
    for j in range(4):
        outs += [res[n][j].reshape(p[n].shape) for n in WEIGHTS]
    return tuple(outs)


def kernel(x, c, ctx, c_ctx, w_ada, b_ada, w_in, attn_sink, ssm_a_re, ssm_a_im, ssm_log_dt, ssm_b_re, ssm_b_im, ssm_c_re, ssm_c_im, ssm_d, w_glu, w_attn_up, w_ssm_up, w_out, ln_mix_g, ln_mix_b, w_mlp1, b_mlp1, w_mlp2, b_mlp2, ln_mlp_g, ln_mlp_b, loss_target, m_c_ctx, m_w_ada, m_b_ada, m_w_in, m_attn_sink, m_ssm_a_re, m_ssm_a_im, m_ssm_log_dt, m_ssm_b_re, m_ssm_b_im, m_ssm_c_re, m_ssm_c_im, m_ssm_d, m_w_glu, m_w_attn_up, m_w_ssm_up, m_w_out, m_ln_mix_g, m_ln_mix_b, m_w_mlp1, m_b_mlp1, m_w_mlp2, m_b_mlp2, m_ln_mlp_g, m_ln_mlp_b, v_c_ctx, v_w_ada, v_b_ada, v_w_in, v_attn_sink, v_ssm_a_re, v_ssm_a_im, v_ssm_log_dt, v_ssm_b_re, v_ssm_b_im, v_ssm_c_re, v_ssm_c_im, v_ssm_d, v_w_glu, v_w_attn_up, v_w_ssm_up, v_w_out, v_ln_mix_g, v_ln_mix_b, v_w_mlp1, v_b_mlp1, v_w_mlp2, v_b_mlp2, v_ln_mlp_g, v_ln_mlp_b):
    given = dict(locals())
    p = {n: given[n] for n in WEIGHTS}
    m = {n: given["m_" + n] for n in WEIGHTS}
    v = {n: given["v_" + n] for n in WEIGHTS}
    return _step(x, c, ctx, loss_target, p, m, v)
```

```python
import functools
import math

import jax
import jax.numpy as jnp
from jax import lax
from jax.experimental import pallas as pl
from jax.experimental.pallas import tpu as pltpu
from jax.experimental.pallas import tpu_sc as plsc

F32 = jnp.float32
BF16 = jnp.bfloat16

N_DEV = 8
D = 2048
T = 2048
C = 256
TA = T + C
GRID_W = 64
HD = 128
NH = 8
NKV = 2
GROUP = NH // NKV
WINDOW = 128
QW = NH * HD
KVW = NKV * HD
SW = D // 4
SG = 16
NG = SW // SG
SP = 64
DFF = 4 * D
IN_COLS = QW + 2 * KVW + SW + 2 * D
ALPHA = 2.0 ** 0.25
LN_EPS = 1e-6
NEG_INF = -1e30
ROPE_BASE = 10000.0
ATT_SCALE = HD ** -0.5

NSEG = 8
SEGLEN = TA // NSEG
GBLK = 8
NBLK = NG // GBLK
BW = GBLK * SP
UW = GBLK * SG

ADAM_LR = 0.001
ADAM_B1 = 0.9
ADAM_B2 = 0.999
ADAM_EPS = 1e-08
ADAM_WD = 0.01
ADAM_STEP = 10

VMEM_LIMIT_BYTES = 56 * 1024 * 1024
MESH = pl.DeviceIdType.MESH


def _cparams(sem=None):
    return pltpu.CompilerParams(dimension_semantics=sem, vmem_limit_bytes=VMEM_LIMIT_BYTES)


def _matmul(a, b, *, mode, name, out_dtypes=(F32,), tm=512, tn=512, tk=None, bias=None, extras=(), epilogue=None, after=()):
    if mode == "nn":
        (M, K), (K2, N) = a.shape, b.shape
    elif mode == "nt":
        (M, K), (N, K2) = a.shape, b.shape
    else:
        (K, M), (K2, N) = a.shape, b.shape
    assert K == K2, (name, a.shape, b.shape)
    tm, tn, tk = min(tm, M), min(tn, N), min(tk or K, K)
    assert M % tm == 0 and N % tn == 0 and K % tk == 0, (name, M, N, K, tm, tn, tk)
    nk = K // tk
    if mode == "tn":
        a_spec = pl.BlockSpec((tk, tm), lambda i, j, k: (k, i))
    else:
        a_spec = pl.BlockSpec((tm, tk), lambda i, j, k: (i, k))
    if mode == "nt":
        b_spec = pl.BlockSpec((tn, tk), lambda i, j, k: (j, k))
    else:
        b_spec = pl.BlockSpec((tk, tn), lambda i, j, k: (k, j))
    dims = {"nn": (((1,), (0,)), ((), ())), "nt": (((1,), (1,)), ((), ())), "tn": (((0,), (0,)), ((), ()))}[mode]
    in_specs = [a_spec, b_spec]
    operands = [a, b]
    if bias is not None:
        in_specs.append(pl.BlockSpec((1, tn), lambda i, j, k: (0, j)))
        operands.append(bias)
    for e in extras:
        in_specs.append(pl.BlockSpec((tm, tn), lambda i, j, k: (i, j)))
        operands.append(e)
    n_ex = len(extras)
    for t in after:
        in_specs.append(pl.BlockSpec(memory_space=pl.ANY))
        operands.append(t)
    n_after = len(after)
    n_out = len(out_dtypes)
    has_bias = bias is not None

    def kern(*refs):
        a_ref, b_ref = refs[0], refs[1]
        pos = 2
        bias_ref = None
        if has_bias:
            bias_ref = refs[pos]
            pos += 1
        ex_refs = refs[pos:pos + n_ex]
        pos += n_ex + n_after
        out_refs = refs[pos:pos + n_out]
        acc_ref = refs[pos + n_out] if nk > 1 else None

        def finish(r):
            if has_bias:
                r = r + bias_ref[...]
            outs = epilogue(r, *[e[...] for e in ex_refs]) if epilogue is not None else (r,)
            for o_ref, o in zip(out_refs, outs):
                o_ref[...] = o.astype(o_ref.dtype)

        part = lax.dot_general(a_ref[...].astype(BF16), b_ref[...].astype(BF16), dims, preferred_element_type=F32)
        if nk == 1:
            finish(part)
        else:
            k = pl.program_id(2)

            @pl.when(k == 0)
            def _():
                acc_ref[...] = part

            @pl.when(k > 0)
            def _():
                acc_ref[...] += part

            @pl.when(k == nk - 1)
            def _():
                finish(acc_ref[...])

    outs = pl.pallas_call(
        kern,
        name=name,
        grid=(M // tm, N // tn, nk),
        in_specs=in_specs,
        out_specs=[pl.BlockSpec((tm, tn), lambda i, j, k: (i, j)) for _ in out_dtypes],
        out_shape=[jax.ShapeDtypeStruct((M, N), dt) for dt in out_dtypes],
        scratch_shapes=[pltpu.VMEM((tm, tn), F32)] if nk > 1 else [],
        compiler_params=_cparams(("parallel", "parallel", "arbitrary")),
    )(*operands)
    return outs[0] if n_out == 1 else tuple(outs)


def _rowwise(fn, rows, vecs, outs, vec_outs, *, nrows, tr, name):
    n_rows, n_vecs, n_outs = len(rows), len(vecs), len(outs)
    in_specs = [pl.BlockSpec((tr, w), lambda i, cb=cb, ro=ro: (i + ro, cb)) for (_, w, cb, ro) in rows]
    in_specs += [pl.BlockSpec(v.shape, lambda i: (0, 0)) for v in vecs]
    out_specs = [pl.BlockSpec((tr, w), lambda i: (i, 0)) for (w, _) in outs]
    out_specs += [pl.BlockSpec(s, lambda i: (0, 0)) for s in vec_outs]
    out_shape = [jax.ShapeDtypeStruct((nrows, w), dt) for (w, dt) in outs]
    out_shape += [jax.ShapeDtypeStruct(s, F32) for s in vec_outs]

    def kern(*refs):
        rvals = [r[...] for r in refs[:n_rows]]
        vvals = [r[...] for r in refs[n_rows:n_rows + n_vecs]]
        o_refs = refs[n_rows + n_vecs:n_rows + n_vecs + n_outs]
        v_refs = refs[n_rows + n_vecs + n_outs:]
        ro, vo = fn(rvals, vvals)
        for r, val in zip(o_refs, ro):
            r[...] = val.astype(r.dtype)
        i = pl.program_id(0)
        for r, val in zip(v_refs, vo):
            @pl.when(i == 0)
            def _(r=r, val=val):
                r[...] = val.astype(F32)

            @pl.when(i > 0)
            def _(r=r, val=val):
                r[...] += val.astype(F32)

    res = pl.pallas_call(
        kern,
        name=name,
        grid=(nrows // tr,),
        in_specs=in_specs,
        out_specs=out_specs,
        out_shape=out_shape,
        compiler_params=_cparams(("arbitrary",)),
    )(*[r[0] for r in rows], *vecs)
    return list(res)


def _ln(x):
    mu = jnp.mean(x, axis=-1, keepdims=True)
    xc = x - mu
    var = jnp.mean(xc * xc, axis=-1, keepdims=True)
    return xc * lax.rsqrt(var + LN_EPS)


def _sigmoid(x):
    return 1.0 / (1.0 + jnp.exp(-x))


def _gelu(x):
    return 0.5 * x * (1.0 + jnp.tanh(math.sqrt(2.0 / math.pi) * (x + 0.044715 * (x * x * x))))


def _silu(x):
    return x * _sigmoid(x)


def _f_ln_mod(x, sc, sh):
    return _ln(x) * (1.0 + sc) + sh


def _f_ssm_pre(u, yf, yb, dskip):
    return dskip * u + yf + yb


def _f_glu(z):
    return z[:, :SW] * _sigmoid(z[:, SW:])


def _f_mix(ga, gs, attn_d, ssm_d):
    return _sigmoid(ga) * attn_d + _sigmoid(gs) * ssm_d


def _f_post1(x, y, g1, lg, lb, sc2, sh2):
    r1 = ALPHA * x + g1 * y
    x1 = _ln(r1) * lg + lb
    h2 = _ln(x1) * (1.0 + sc2) + sh2
    return x1, h2


def _f_loss(x1, mlp, tgt, g2, lg, lb, b2z):
    r2 = ALPHA * x1 + g2 * (mlp + b2z)
    out = _ln(r2) * lg + lb
    err = out - tgt
    return 0.5 * jnp.sum(err * err) * (1.0 / D)


def _rope_tables():
    rows = T // GRID_W
    row = jnp.repeat(jnp.arange(rows), GRID_W)
    col = jnp.tile(jnp.arange(GRID_W), rows)
    n_freq = HD // 4
    freqs = ROPE_BASE ** (-jnp.arange(n_freq, dtype=F32) / n_freq)
    ang_r = row.astype(F32)[:, None] * freqs
    ang_c = col.astype(F32)[:, None] * freqs
    ang = jnp.concatenate([ang_r, ang_r, ang_c, ang_c], -1)
    cos, sin = jnp.cos(ang), jnp.sin(ang)
    lo = (jnp.arange(HD) % (HD // 2)) < (HD // 4)
    sin_a = jnp.where(lo[None, :], -sin, 0.0)
    sin_b = jnp.where(lo[None, :], 0.0, sin)
    return cos, sin_a, sin_b


def _rope(x, cos, sa, sb):
    return x * cos + pltpu.roll(x, 96, 1) * sa + pltpu.roll(x, 32, 1) * sb


def _rope_t(dy, cos, sa, sb):
    return dy * cos + pltpu.roll(dy * sa, 32, 1) + pltpu.roll(dy * sb, 96, 1)


BAND = 3 * WINDOW
KPAD = T + 2 * WINDOW


def _attn_fill_kv(k_ref, v_ref, cos_ref, sa_ref, sb_ref, kp, vp, kc, vc):
    zeros = jnp.zeros((WINDOW, KVW), BF16)
    kp[0:WINDOW, :] = zeros
    kp[WINDOW + T:KPAD, :] = zeros
    vp[0:WINDOW, :] = zeros
    vp[WINDOW + T:KPAD, :] = zeros
    for hh in range(NKV):
        cs = slice(hh * HD, (hh + 1) * HD)
        for r0 in range(0, T, 512):
            rs = slice(r0, r0 + 512)
            kr = _rope(k_ref[rs, cs], cos_ref[rs, :], sa_ref[rs, :], sb_ref[rs, :])
            kp[WINDOW + r0:WINDOW + r0 + 512, cs] = kr.astype(BF16)
    vp[WINDOW:WINDOW + T, :] = v_ref[0:T, :].astype(BF16)
    kc[...] = k_ref[T:TA, :].astype(BF16)
    vc[...] = v_ref[T:TA, :].astype(BF16)


def _attn_scores(n, h, q_ref, cos_ref, sa_ref, sb_ref, sink_ref, kp, kc):
    kvh = h // GROUP
    r0 = pl.multiple_of(n * WINDOW, WINDOW)
    cos = cos_ref[pl.ds(r0, WINDOW), :]
    sa = sa_ref[pl.ds(r0, WINDOW), :]
    sb = sb_ref[pl.ds(r0, WINDOW), :]
    q_h = _rope(q_ref[:, h * HD:(h + 1) * HD], cos, sa, sb).astype(BF16)
    kb = kp[pl.ds(r0, BAND), kvh * HD:(kvh + 1) * HD]
    kcb = kc[:, kvh * HD:(kvh + 1) * HD]
    nt = (((1,), (1,)), ((), ()))
    s_loc = lax.dot_general(q_h, kb, nt, preferred_element_type=F32) * ATT_SCALE
    s_ctx = lax.dot_general(q_h, kcb, nt, preferred_element_type=F32) * ATT_SCALE
    row = lax.broadcasted_iota(jnp.int32, (WINDOW, BAND), 0)
    col = lax.broadcasted_iota(jnp.int32, (WINDOW, BAND), 1)
    rel = col - WINDOW - row
    kpos = r0 - WINDOW + col
    valid = (jnp.abs(rel) <= WINDOW) & (kpos >= 0) & (kpos < T)
    s_loc = jnp.where(valid, s_loc, NEG_INF)
    sk = sink_ref[0:1, h:h + 1]
    m = jnp.maximum(jnp.maximum(jnp.max(s_loc, -1, keepdims=True), jnp.max(s_ctx, -1, keepdims=True)), sk)
    e_loc = jnp.exp(s_loc - m)
    e_ctx = jnp.exp(s_ctx - m)
    e_sink = jnp.exp(sk - m)
    inv = 1.0 / (jnp.sum(e_loc, -1, keepdims=True) + jnp.sum(e_ctx, -1, keepdims=True) + e_sink)
    return q_h, r0, e_loc * inv, e_ctx * inv, e_sink * inv


def _attn_fwd(proj, sink, tabs):
    cos, sa, sb = tabs

    def kern(q_ref, k_ref, v_ref, cos_ref, sa_ref, sb_ref, sink_ref, o_ref, kp, vp, kc, vc):
        n = pl.program_id(0)

        @pl.when(n == 0)
        def _():
            _attn_fill_kv(k_ref, v_ref, cos_ref, sa_ref, sb_ref, kp, vp, kc, vc)

        for h in range(NH):
            kvh = h // GROUP
            _, r0, p_loc, p_ctx, _ = _attn_scores(n, h, q_ref, cos_ref, sa_ref, sb_ref, sink_ref, kp, kc)
            vb = vp[pl.ds(r0, BAND), kvh * HD:(kvh + 1) * HD]
            vcb = vc[:, kvh * HD:(kvh + 1) * HD]
            o = jnp.dot(p_loc.astype(BF16), vb, preferred_element_type=F32)
            o = o + jnp.dot(p_ctx.astype(BF16), vcb, preferred_element_type=F32)
            o_ref[:, h * HD:(h + 1) * HD] = o.astype(o_ref.dtype)

    full = lambda shape: pl.BlockSpec(shape, lambda n: (0, 0))
    return pl.pallas_call(
        kern,
        name="attn_fwd",
        grid=(T // WINDOW,),
        in_specs=[
            pl.BlockSpec((WINDOW, QW), lambda n: (n, 0)),
            pl.BlockSpec((TA, KVW), lambda n: (0, QW // KVW)),
            pl.BlockSpec((TA, KVW), lambda n: (0, QW // KVW + 1)),
            full((T, HD)), full((T, HD)), full((T, HD)), full((1, NH)),
        ],
        out_specs=pl.BlockSpec((WINDOW, QW), lambda n: (n, 0)),
        out_shape=jax.ShapeDtypeStruct((T, QW), BF16),
        scratch_shapes=[pltpu.VMEM((KPAD, KVW), BF16), pltpu.VMEM((KPAD, KVW), BF16),
                        pltpu.VMEM((C, KVW), BF16), pltpu.VMEM((C, KVW), BF16)],
        compiler_params=_cparams(("arbitrary",)),
    )(proj, proj, proj, cos, sa, sb, sink)


def _attn_bwd(proj, d_attn, sink, tabs):
    cos, sa, sb = tabs
    n_blocks = T // WINDOW

    def kern(q_ref, k_ref, v_ref, do_ref, cos_ref, sa_ref, sb_ref, sink_ref,
             dq_ref, dk_ref, dv_ref, dsink_ref, kp, vp, kc, vc, dkp, dvp, dkc, dvc):
        n = pl.program_id(0)

        @pl.when(n == 0)
        def _():
            _attn_fill_kv(k_ref, v_ref, cos_ref, sa_ref, sb_ref, kp, vp, kc, vc)
            dkp[...] = jnp.zeros_like(dkp)
            dvp[...] = jnp.zeros_like(dvp)
            dkc[...] = jnp.zeros_like(dkc)
            dvc[...] = jnp.zeros_like(dvc)
            dsink_ref[...] = jnp.zeros_like(dsink_ref)

        nt = (((1,), (1,)), ((), ()))
        tn = (((0,), (0,)), ((), ()))
        for h in range(NH):
            kvh = h // GROUP
            cs = slice(kvh * HD, (kvh + 1) * HD)
            q_h, r0, p_loc, p_ctx, p_sink = _attn_scores(n, h, q_ref, cos_ref, sa_ref, sb_ref, sink_ref, kp, kc)
            kb = kp[pl.ds(r0, BAND), cs]
            vb = vp[pl.ds(r0, BAND), cs]
            kcb = kc[:, cs]
            vcb = vc[:, cs]
            do_h = do_ref[:, h * HD:(h + 1) * HD]
            dp_loc = lax.dot_general(do_h, vb, nt, preferred_element_type=F32)
            dp_ctx = lax.dot_general(do_h, vcb, nt, preferred_element_type=F32)
            delta = jnp.sum(p_loc * dp_loc, -1, keepdims=True) + jnp.sum(p_ctx * dp_ctx, -1, keepdims=True)
            ds_loc = (p_loc * (dp_loc - delta) * ATT_SCALE).astype(BF16)
            ds_ctx = (p_ctx * (dp_ctx - delta) * ATT_SCALE).astype(BF16)
            dq = jnp.dot(ds_loc, kb, preferred_element_type=F32) + jnp.dot(ds_ctx, kcb, preferred_element_type=F32)
            cos = cos_ref[pl.ds(r0, WINDOW), :]
            sa_ = sa_ref[pl.ds(r0, WINDOW), :]
            sb_ = sb_ref[pl.ds(r0, WINDOW), :]
            dq_ref[:, h * HD:(h + 1) * HD] = _rope_t(dq, cos, sa_, sb_).astype(dq_ref.dtype)
            dkp[pl.ds(r0, BAND), cs] += lax.dot_general(ds_loc, q_h, tn, preferred_element_type=F32)
            dkc[:, cs] += lax.dot_general(ds_ctx, q_h, tn, preferred_element_type=F32)
            dvp[pl.ds(r0, BAND), cs] += lax.dot_general(p_loc.astype(BF16), do_h, tn, preferred_element_type=F32)
            dvc[:, cs] += lax.dot_general(p_ctx.astype(BF16), do_h, tn, preferred_element_type=F32)
            dsk = -jnp.sum(p_sink * delta, axis=0, keepdims=True)
            dsink_ref[h:h + 1, :] += jnp.broadcast_to(dsk, (1, HD))

        @pl.when(n == n_blocks - 1)
        def _():
            for hh in range(NKV):
                cs = slice(hh * HD, (hh + 1) * HD)
                for r0 in range(0, T, 512):
                    rs = slice(r0, r0 + 512)
                    g = dkp[WINDOW + r0:WINDOW + r0 + 512, cs]
                    dk_ref[rs, cs] = _rope_t(g, cos_ref[rs, :], sa_ref[rs, :], sb_ref[rs, :]).astype(dk_ref.dtype)
            dk_ref[T:TA, :] = dkc[...].astype(dk_ref.dtype)
            dv_ref[0:T, :] = dvp[WINDOW:WINDOW + T, :].astype(dv_ref.dtype)
            dv_ref[T:TA, :] = dvc[...].astype(dv_ref.dtype)

    full = lambda shape: pl.BlockSpec(shape, lambda n: (0, 0))
    return pl.pallas_call(
        kern,
        name="attn_bwd",
        grid=(n_blocks,),
        in_specs=[
            pl.BlockSpec((WINDOW, QW), lambda n: (n, 0)),
            pl.BlockSpec((TA, KVW), lambda n: (0, QW // KVW)),
            pl.BlockSpec((TA, KVW), lambda n: (0, QW // KVW + 1)),
            pl.BlockSpec((WINDOW, QW), lambda n: (n, 0)),
            full((T, HD)), full((T, HD)), full((T, HD)), full((1, NH)),
        ],
        out_specs=[pl.BlockSpec((WINDOW, QW), lambda n: (n, 0)), full((TA, KVW)), full((TA, KVW)), full((NH, HD))],
        out_shape=[jax.ShapeDtypeStruct((T, QW), BF16), jax.ShapeDtypeStruct((TA, KVW), BF16),
                   jax.ShapeDtypeStruct((TA, KVW), BF16), jax.ShapeDtypeStruct((NH, HD), F32)],
        scratch_shapes=[pltpu.VMEM((KPAD, KVW), BF16), pltpu.VMEM((KPAD, KVW), BF16),
                        pltpu.VMEM((C, KVW), BF16), pltpu.VMEM((C, KVW), BF16),
                        pltpu.VMEM((KPAD, KVW), F32), pltpu.VMEM((KPAD, KVW), F32),
                        pltpu.VMEM((C, KVW), F32), pltpu.VMEM((C, KVW), F32)],
        compiler_params=_cparams(("arbitrary",)),
    )(proj, proj, proj, d_attn, cos, sa, sb, sink)


def _s5_prep(a_re, a_im, log_dt, b_re, b_im, c_re, c_im):
    lam = lax.complex(a_re, a_im)
    dt = jnp.exp(log_dt)[..., None]
    lam_bar = jnp.exp(lam * dt)
    b_bar = ((lam_bar - 1.0) / lam)[..., None] * lax.complex(b_re, b_im)
    eye = jnp.eye(GBLK, dtype=F32)

    def lam_rows(v):
        return v.reshape(2, NBLK, 1, BW)

    lam_l = jnp.concatenate([lam_rows(jnp.real(lam_bar)), lam_rows(jnp.imag(lam_bar))], -1)
    lam_l = jnp.broadcast_to(lam_l, (2, NBLK, 8, 2 * BW))

    def b_blocks(v):
        v = v.reshape(2, NBLK, GBLK, SP, SG).transpose(0, 1, 2, 4, 3)
        return (v[:, :, :, :, None, :] * eye[None, None, :, None, :, None]).reshape(2, NBLK, UW, BW)

    bmat = jnp.concatenate([b_blocks(jnp.real(b_bar)), b_blocks(jnp.imag(b_bar))], -1)

    def c_blocks(v):
        v = v.reshape(2, NBLK, GBLK, SG, SP).transpose(0, 1, 2, 4, 3)
        return (v[:, :, :, :, None, :] * eye[None, None, :, None, :, None]).reshape(2, NBLK, BW, UW)

    cmat = jnp.concatenate([c_blocks(c_re), -c_blocks(c_im)], 2)
    return lam_l, bmat, cmat


def _to_seq(lat, ctx):
    w = lat.shape[-1]
    f = jnp.concatenate([ctx, lat], 0)
    b = jnp.concatenate([ctx[::-1], lat[::-1]], 0)
    s = jnp.stack([f, b])
    return s.reshape(2, NSEG, SEGLEN, w).transpose(0, 2, 1, 3).reshape(2, TA, w)


def _from_seq(s):
    w = s.shape[-1]
    s = s.reshape(2, SEGLEN, NSEG, w).transpose(0, 2, 1, 3).reshape(2, TA, w)
    return s[0, C:], s[0, :C], s[1, C:][::-1], s[1, :C][::-1]


def _cmul(ar, ai, br, bi):
    return ar * br - ai * bi, ar * bi + ai * br


def _shift_rows(x, up):
    r = lax.broadcasted_iota(jnp.int32, x.shape, 0)
    if up:
        return jnp.where(r == NSEG - 1, 0.0, pltpu.roll(x, NSEG - 1, 0))
    return jnp.where(r == 0, 0.0, pltpu.roll(x, 1, 0))


RCH = 256


def _s5_fwd(u_seq, lam, bmat, cmat):
    def kern(u_ref, lam_ref, b_ref, c_ref, s_ref, y_ref):
        bm = b_ref[0, 0].astype(BF16)
        for r0 in range(0, TA, RCH):
            s_ref[0, 0, r0:r0 + RCH, :] = jnp.dot(u_ref[0, r0:r0 + RCH, :].astype(BF16), bm, preferred_element_type=F32)
        lr = lam_ref[0, 0, :, 0:BW]
        li = lam_ref[0, 0, :, BW:2 * BW]
        zero = jnp.zeros((NSEG, BW), F32)

        def scan1(j, carry):
            sr, si, pr, pi = carry
            row = pl.multiple_of(j * NSEG, NSEG)
            tr, ti = _cmul(lr, li, sr, si)
            sr = tr + s_ref[0, 0, pl.ds(row, NSEG), 0:BW]
            si = ti + s_ref[0, 0, pl.ds(row, NSEG), BW:2 * BW]
            s_ref[0, 0, pl.ds(row, NSEG), 0:BW] = sr
            s_ref[0, 0, pl.ds(row, NSEG), BW:2 * BW] = si
            pr, pi = _cmul(lr, li, pr, pi)
            return sr, si, pr, pi

        er, ei, lpr, lpi = lax.fori_loop(0, SEGLEN, scan1, (zero, zero, zero + 1.0, zero))
        cr, ci = zero, zero
        for _ in range(NSEG - 1):
            tr, ti = _cmul(lpr, lpi, cr, ci)
            cr, ci = _shift_rows(er + tr, False), _shift_rows(ei + ti, False)

        def scan2(j, carry):
            pr, pi = carry
            row = pl.multiple_of(j * NSEG, NSEG)
            tr, ti = _cmul(pr, pi, cr, ci)
            s_ref[0, 0, pl.ds(row, NSEG), 0:BW] += tr
            s_ref[0, 0, pl.ds(row, NSEG), BW:2 * BW] += ti
            return _cmul(lr, li, pr, pi)

        lax.fori_loop(0, SEGLEN, scan2, (lr, li))
        cm = c_ref[0, 0].astype(BF16)
        for r0 in range(0, TA, RCH):
            y_ref[0, r0:r0 + RCH, :] = jnp.dot(s_ref[0, 0, r0:r0 + RCH, :].astype(BF16), cm, preferred_element_type=F32)

    return pl.pallas_call(
        kern,
        name="s5_fwd",
        grid=(2, NBLK),
        in_specs=[
            pl.BlockSpec((1, TA, UW), lambda d, b: (d, 0, b)),
            pl.BlockSpec((1, 1, 8, 2 * BW), lambda d, b: (d, b, 0, 0)),
            pl.BlockSpec((1, 1, UW, 2 * BW), lambda d, b: (d, b, 0, 0)),
            pl.BlockSpec((1, 1, 2 * BW, UW), lambda d, b: (d, b, 0, 0)),
        ],
        out_specs=[pl.BlockSpec((1, 1, TA, 2 * BW), lambda d, b: (d, b, 0, 0)),
                   pl.BlockSpec((1, TA, UW), lambda d, b: (d, 0, b))],
        out_shape=[jax.ShapeDtypeStruct((2, NBLK, TA, 2 * BW), F32), jax.ShapeDtypeStruct((2, TA, SW), F32)],
        compiler_params=_cparams(("parallel", "parallel")),
    )(u_seq, lam, bmat, cmat)


def _s5_bwd(dy_seq, states, u_seq, lam, bmat, cmat):
    nt = (((1,), (1,)), ((), ()))
    tn = (((0,), (0,)), ((), ()))

    def kern(dy_ref, s_ref, u_ref, lam_ref, b_ref, c_ref, du_ref, dlam_ref, db_ref, dc_ref, g_ref):
        cm = c_ref[0, 0].astype(BF16)
        for r0 in range(0, TA, RCH):
            g_ref[r0:r0 + RCH, :] = lax.dot_general(dy_ref[0, r0:r0 + RCH, :].astype(BF16), cm, nt, preferred_element_type=F32)
        lr = lam_ref[0, 0, :, 0:BW]
        li = -lam_ref[0, 0, :, BW:2 * BW]
        zero = jnp.zeros((NSEG, BW), F32)

        def scan1(jj, carry):
            gr, gi, pr, pi = carry
            row = pl.multiple_of((SEGLEN - 1 - jj) * NSEG, NSEG)
            tr, ti = _cmul(lr, li, gr, gi)
            gr = tr + g_ref[pl.ds(row, NSEG), 0:BW]
            gi = ti + g_ref[pl.ds(row, NSEG), BW:2 * BW]
            g_ref[pl.ds(row, NSEG), 0:BW] = gr
            g_ref[pl.ds(row, NSEG), BW:2 * BW] = gi
            pr, pi = _cmul(lr, li, pr, pi)
            return gr, gi, pr, pi

        br, bi, lpr, lpi = lax.fori_loop(0, SEGLEN, scan1, (zero, zero, zero + 1.0, zero))
        cr, ci = zero, zero
        for _ in range(NSEG - 1):
            tr, ti = _cmul(lpr, lpi, cr, ci)
            cr, ci = _shift_rows(br + tr, True), _shift_rows(bi + ti, True)

        def dlam_terms(gr, gi, sr, si):
            return gr * sr + gi * si, gi * sr - gr * si

        def scan2(jj, carry):
            pr, pi, ar, ai = carry
            j = SEGLEN - 1 - jj
            row = pl.multiple_of(j * NSEG, NSEG)
            prev = pl.multiple_of((j - 1) * NSEG, NSEG)
            tr, ti = _cmul(pr, pi, cr, ci)
            gr = g_ref[pl.ds(row, NSEG), 0:BW] + tr
            gi = g_ref[pl.ds(row, NSEG), BW:2 * BW] + ti
            g_ref[pl.ds(row, NSEG), 0:BW] = gr
            g_ref[pl.ds(row, NSEG), BW:2 * BW] = gi
            dr, di = dlam_terms(gr, gi, s_ref[0, 0, pl.ds(prev, NSEG), 0:BW], s_ref[0, 0, pl.ds(prev, NSEG), BW:2 * BW])
            pr, pi = _cmul(lr, li, pr, pi)
            return pr, pi, ar + dr, ai + di

        pr, pi, ar, ai = lax.fori_loop(0, SEGLEN - 1, scan2, (lr, li, zero, zero))
        tr, ti = _cmul(pr, pi, cr, ci)
        gr = g_ref[0:NSEG, 0:BW] + tr
        gi = g_ref[0:NSEG, BW:2 * BW] + ti
        g_ref[0:NSEG, 0:BW] = gr
        g_ref[0:NSEG, BW:2 * BW] = gi
        last = (SEGLEN - 1) * NSEG
        dr, di = dlam_terms(gr, gi, _shift_rows(s_ref[0, 0, last:last + NSEG, 0:BW], False),
                            _shift_rows(s_ref[0, 0, last:last + NSEG, BW:2 * BW], False))
        dlam_ref[0, 0, :, 0:BW] = ar + dr
        dlam_ref[0, 0, :, BW:2 * BW] = ai + di

        bm = b_ref[0, 0].astype(BF16)
        db = jnp.zeros((UW, 2 * BW), F32)
        dc = jnp.zeros((2 * BW, UW), F32)
        for r0 in range(0, TA, RCH):
            g = g_ref[r0:r0 + RCH, :].astype(BF16)
            du_ref[0, r0:r0 + RCH, :] = lax.dot_general(g, bm, nt, preferred_element_type=F32)
            db = db + lax.dot_general(u_ref[0, r0:r0 + RCH, :].astype(BF16), g, tn, preferred_element_type=F32)
            dc = dc + lax.dot_general(s_ref[0, 0, r0:r0 + RCH, :].astype(BF16), dy_ref[0, r0:r0 + RCH, :].astype(BF16), tn,
                                      preferred_element_type=F32)
        db_ref[0, 0] = db
        dc_ref[0, 0] = dc

    blk4 = lambda shape: pl.BlockSpec((1, 1) + shape, lambda d, b: (d, b, 0, 0))
    cols = pl.BlockSpec((1, TA, UW), lambda d, b: (d, 0, b))
    return pl.pallas_call(
        kern,
        name="s5_bwd",
        grid=(2, NBLK),
        in_specs=[cols, blk4((TA, 2 * BW)), cols, blk4((8, 2 * BW)), blk4((UW, 2 * BW)), blk4((2 * BW, UW))],
        out_specs=[cols, blk4((8, 2 * BW)), blk4((UW, 2 * BW)), blk4((2 * BW, UW))],
        out_shape=[jax.ShapeDtypeStruct((2, TA, SW), F32), jax.ShapeDtypeStruct((2, NBLK, 8, 2 * BW), F32),
                   jax.ShapeDtypeStruct((2, NBLK, UW, 2 * BW), F32), jax.ShapeDtypeStruct((2, NBLK, 2 * BW, UW), F32)],
        scratch_shapes=[pltpu.VMEM((TA, 2 * BW), F32)],
        compiler_params=_cparams(("parallel", "parallel")),
    )(dy_seq, states, u_seq, lam, bmat, cmat)


TR = 256


def _vjp_rows(f, primals, cots, n_row):
    _, pull = jax.vjp(f, *primals)
    g = pull(cots)
    return list(g[:n_row]), list(g[n_row:])


class _GradDict(dict):
    def __init__(self, on_set=None):
        super().__init__()
        self._on_set = on_set
        self.tokens = {}

    def __setitem__(self, key, value):
        super().__setitem__(key, value)
        if self._on_set is not None:
            self._on_set(self)

    def order(self, key):
        return self.tokens.get(key, self.get(key))


def _local_step(x, ctx, tgt, mod_lat, mod_ctx, wb, sp, on_grad=None):
    sh1, sc1, g1, sh2, sc2, g2 = [mod_lat[:, i * D:(i + 1) * D] for i in range(6)]
    csh1, csc1 = mod_ctx[:, 0:D], mod_ctx[:, D:2 * D]
    tabs = _rope_tables()
    sink = sp["attn_sink"].reshape(1, NH)
    dskip = sp["ssm_d"].reshape(1, SW)
    lg_mix, lb_mix = sp["ln_mix_g"].reshape(1, D), sp["ln_mix_b"].reshape(1, D)
    lg_mlp, lb_mlp = sp["ln_mlp_g"].reshape(1, D), sp["ln_mlp_b"].reshape(1, D)
    b1, b2 = sp["b_mlp1"].reshape(1, DFF), sp["b_mlp2"].reshape(1, D)
    s5_names = ("ssm_a_re", "ssm_a_im", "ssm_log_dt", "ssm_b_re", "ssm_b_im", "ssm_c_re", "ssm_c_im")
    (lam, bmat, cmat), s5_pull = jax.vjp(_s5_prep, *[sp[n] for n in s5_names])

    def ln_mod(rv, vv):
        return [_f_ln_mod(rv[0], vv[0], vv[1])], []

    h_lat, = _rowwise(ln_mod, [(x, D, 0, 0)], [sc1, sh1], [(D, BF16)], [], nrows=T, tr=TR, name="ln1_lat")
    h_ctx, = _rowwise(ln_mod, [(ctx, D, 0, 0)], [csc1, csh1], [(D, BF16)], [], nrows=C, tr=TR, name="ln1_ctx")
    h1 = jnp.concatenate([h_lat, h_ctx], 0)
    proj = _matmul(h1, wb["w_in"], mode="nn", name="proj", tm=768, tn=512)
    attn = _attn_fwd(proj, sink, tabs)
    u_all = proj[:, QW + 2 * KVW:QW + 2 * KVW + SW]
    u_lat, u_ctx = u_all[:T], u_all[T:]
    u_seq = _to_seq(u_lat, u_ctx)
    states, y_seq = _s5_fwd(u_seq, lam, bmat, cmat)
    y_f, _, y_b, _ = _from_seq(y_seq)

    def ssm_pre(rv, vv):
        s = _f_ssm_pre(rv[0], rv[1], rv[2], vv[0])
        return [s, _gelu(s)], []

    ssm, ge = _rowwise(ssm_pre, [(u_lat, SW, 0, 0), (y_f, SW, 0, 0), (y_b, SW, 0, 0)], [dskip],
                       [(SW, F32), (SW, BF16)], [], nrows=T, tr=TR, name="ssm_pre")
    z = _matmul(ge, wb["w_glu"], mode="nn", name="glu_mm", tm=1024, tn=1024)

    def glu_act(rv, vv):
        return [_f_glu(rv[0])], []

    glu, = _rowwise(glu_act, [(z, 2 * SW, 0, 0)], [], [(SW, BF16)], [], nrows=T, tr=TR, name="glu_act")
    attn_d = _matmul(attn, wb["w_attn_up"], mode="nn", name="attn_up", tm=1024, tn=512)
    ssm_d = _matmul(glu, wb["w_ssm_up"], mode="nn", name="ssm_up", tm=1024, tn=512)
    ga_cb, gs_cb = (QW + 2 * KVW + SW) // D, (QW + 2 * KVW + SW) // D + 1

    def mix(rv, vv):
        return [_f_mix(*rv)], []

    mixv, = _rowwise(mix, [(proj, D, ga_cb, 0), (proj, D, gs_cb, 0), (attn_d, D, 0, 0), (ssm_d, D, 0, 0)], [],
                     [(D, BF16)], [], nrows=T, tr=TR, name="mix")
    y = _matmul(mixv, wb["w_out"], mode="nn", name="out_proj", tm=1024, tn=512)

    def post1(rv, vv):
        x1, h2 = _f_post1(rv[0], rv[1], *vv)
        return [x1, h2], []

    x1, h2 = _rowwise(post1, [(x, D, 0, 0), (y, D, 0, 0)], [g1, lg_mix, lb_mix, sc2, sh2],
                      [(D, F32), (D, BF16)], [], nrows=T, tr=TR, name="post1")

    def relu_sq(acc):
        r = jnp.maximum(acc, 0.0)
        return r, r * r

    r_act, act = _matmul(h2, wb["w_mlp1"], mode="nn", name="mlp1", tm=1024, tn=512, bias=b1,
                         out_dtypes=(BF16, BF16), epilogue=relu_sq)
    mlp = _matmul(act, wb["w_mlp2"], mode="nn", name="mlp2", tm=1024, tn=512, tk=2048)

    def loss_fb(rv, vv):
        x1_t, mlp_t, tgt_t = rv
        g2_v, lg_v, lb_v, b2_v = vv
        f = lambda a, m, g, p, q, b: _f_loss(a, m, tgt_t, g, p, q, b)
        val, grads = jax.value_and_grad(f, argnums=(0, 1, 2, 3, 4, 5))(x1_t, mlp_t, g2_v, lg_v, lb_v, b2_v)
        dx1, dmlp, dg2, dlg, dlb, db2 = grads
        return [dx1, dmlp], [jnp.reshape(val, (1, 1)), dg2, dlg, dlb, db2]

    dx1_a, d_mlp, loss_p, d_g2, d_lg_mlp, d_lb_mlp, d_b2 = _rowwise(
        loss_fb, [(x1, D, 0, 0), (mlp, D, 0, 0), (tgt, D, 0, 0)], [g2, lg_mlp, lb_mlp, b2],
        [(D, F32), (D, BF16)], [(1, 1), (1, D), (1, D), (1, D), (1, D)], nrows=T, tr=TR, name="loss_fb")

    gw = _GradDict(on_grad)
    gw["w_mlp2"] = _matmul(act, d_mlp, mode="tn", name="dw_mlp2", out_dtypes=(BF16,), tm=512, tn=1024, tk=1024)
    da, = (_matmul(d_mlp, wb["w_mlp2"], mode="nt", name="d_act", out_dtypes=(BF16,), tm=1024, tn=512,
                   extras=(r_act,), epilogue=lambda acc, r: (acc * (2.0 * r.astype(F32)),), after=(gw.order("w_mlp2"),)),)
    ones = jnp.ones((8, T), BF16)
    d_b1 = _matmul(ones, da, mode="nn", name="db_mlp1", tm=8, tn=2048)[0:1]
    gw["w_mlp1"] = _matmul(h2, da, mode="tn", name="dw_mlp1", out_dtypes=(BF16,), tm=512, tn=1024, tk=1024)
    dh2 = _matmul(da, wb["w_mlp1"], mode="nt", name="d_h2", tm=1024, tn=512, tk=2048, after=(gw.order("w_mlp1"),))

    def post1_b(rv, vv):
        x_t, y_t, dx1_t, dh2_t = rv
        gr, gv = _vjp_rows(_f_post1, (x_t, y_t, *vv), (dx1_t, dh2_t), 2)
        return [gr[0], gr[1]], gv

    dx_a, dy, d_g1, d_lg_mix, d_lb_mix, d_sc2, d_sh2 = _rowwise(
        post1_b, [(x, D, 0, 0), (y, D, 0, 0), (dx1_a, D, 0, 0), (dh2, D, 0, 0)], [g1, lg_mix, lb_mix, sc2, sh2],
        [(D, F32), (D, BF16)], [(1, D)] * 5, nrows=T, tr=TR, name="post1_bwd")
    gw["w_out"] = _matmul(mixv, dy, mode="tn", name="dw_out", out_dtypes=(BF16,), tm=512, tn=1024, tk=1024)
    dmix = _matmul(dy, wb["w_out"], mode="nt", name="d_mix", tm=1024, tn=512, after=(gw.order("w_out"),))

    def mix_b(rv, vv):
        gr, _ = _vjp_rows(_f_mix, tuple(rv[:4]), rv[4], 4)
        return gr, []

    d_ga, d_gs, d_attn_d, d_ssm_d = _rowwise(
        mix_b, [(proj, D, ga_cb, 0), (proj, D, gs_cb, 0), (attn_d, D, 0, 0), (ssm_d, D, 0, 0), (dmix, D, 0, 0)], [],
        [(D, BF16)] * 4, [], nrows=T, tr=TR, name="mix_bwd")
    gw["w_attn_up"] = _matmul(attn, d_attn_d, mode="tn", name="dw_attn_up", out_dtypes=(BF16,), tm=512, tn=1024, tk=1024)
    d_attn = _matmul(d_attn_d, wb["w_attn_up"], mode="nt", name="d_attn", out_dtypes=(BF16,), tm=1024, tn=512)
    gw["w_ssm_up"] = _matmul(glu, d_ssm_d, mode="tn", name="dw_ssm_up", out_dtypes=(BF16,), tm=512, tn=1024, tk=1024)
    d_glu = _matmul(d_ssm_d, wb["w_ssm_up"], mode="nt", name="d_glu", tm=1024, tn=512, after=(gw.order("w_attn_up"), gw.order("w_ssm_up")))

    def glu_b(rv, vv):
        gr, _ = _vjp_rows(_f_glu, (rv[0],), rv[1], 1)
        return gr, []

    dz, = _rowwise(glu_b, [(z, 2 * SW, 0, 0), (d_glu, SW, 0, 0)], [], [(2 * SW, BF16)], [], nrows=T, tr=TR, name="glu_bwd")
    gw["w_glu"] = _matmul(ge, dz, mode="tn", name="dw_glu", out_dtypes=(BF16,), tm=512, tn=1024, tk=1024)
    d_ge = _matmul(dz, wb["w_glu"], mode="nt", name="d_ge", tm=1024, tn=512, after=(gw.order("w_glu"),))

    def ssm_pre_b(rv, vv):
        u_t, yf_t, yb_t, dge_t = rv
        f = lambda u, yf, yb, dk: _gelu(_f_ssm_pre(u, yf, yb, dk))
        gr, gv = _vjp_rows(f, (u_t, yf_t, yb_t, vv[0]), dge_t, 3)
        return [gr[0], gr[1]], gv

    du_dir, d_ssm, d_dskip = _rowwise(
        ssm_pre_b, [(u_lat, SW, 0, 0), (y_f, SW, 0, 0), (y_b, SW, 0, 0), (d_ge, SW, 0, 0)], [dskip],
        [(SW, F32), (SW, F32)], [(1, SW)], nrows=T, tr=TR, name="ssm_pre_bwd")
    dy_seq = _to_seq(d_ssm, jnp.zeros((C, SW), F32))
    du_seq, dlam, dbmat, dcmat = _s5_bwd(dy_seq, states, u_seq, lam, bmat, cmat)
    du_f, duc_f, du_b, duc_b = _from_seq(du_seq)
    du_all = jnp.concatenate([du_dir + du_f + du_b, duc_f + duc_b], 0).astype(BF16)
    s5_grads = s5_pull((dlam, dbmat, dcmat))

    dq, dk, dv, dsink = _attn_bwd(proj, d_attn, sink, tabs)
    zc = lambda w: jnp.zeros((C, w), BF16)
    dproj = jnp.concatenate([
        jnp.concatenate([dq, zc(QW)], 0), dk, dv, du_all,
        jnp.concatenate([d_ga, zc(D)], 0), jnp.concatenate([d_gs, zc(D)], 0)], 1)
    gw["w_in"] = _matmul(h1, dproj, mode="tn", name="dw_in", out_dtypes=(BF16,), tm=512, tn=1536, tk=768)
    dh1 = _matmul(dproj, wb["w_in"], mode="nt", name="d_h1", tm=768, tn=512, tk=2048, after=(gw.order("w_in"),))

    def ln1_b(rv, vv):
        x_t, dh_t, dxa_t = rv
        gr, gv = _vjp_rows(_f_ln_mod, (x_t, vv[0], vv[1]), dh_t, 1)
        return [gr[0] + dxa_t], gv

    grad_x, d_sc1, d_sh1 = _rowwise(ln1_b, [(x, D, 0, 0), (dh1, D, 0, 0), (dx_a, D, 0, 0)], [sc1, sh1],
                                    [(D, F32)], [(1, D), (1, D)], nrows=T, tr=TR, name="ln1_lat_bwd")

    def ln1c_b(rv, vv):
        _, gv = _vjp_rows(_f_ln_mod, (rv[0], vv[0], vv[1]), rv[1], 1)
        return [], gv

    d_csc1, d_csh1 = _rowwise(ln1c_b, [(ctx, D, 0, 0), (dh1, D, 0, T // TR)], [csc1, csh1],
                              [], [(1, D), (1, D)], nrows=C, tr=TR, name="ln1_ctx_bwd")

    d_mod_lat = jnp.concatenate([d_sh1, d_sc1, d_g1, d_sh2, d_sc2, d_g2], 1)
    zv = jnp.zeros((1, D), F32)
    d_mod_ctx = jnp.concatenate([d_csh1, d_csc1, zv, zv, zv, zv], 1)
    gs = {n: g for n, g in zip(s5_names, s5_grads)}
    gs["attn_sink"] = dsink[:, 0]
    gs["ssm_d"] = d_dskip
    gs["ln_mix_g"], gs["ln_mix_b"] = d_lg_mix, d_lb_mix
    gs["ln_mlp_g"], gs["ln_mlp_b"] = d_lg_mlp, d_lb_mlp
    gs["b_mlp1"], gs["b_mlp2"] = d_b1, d_b2
    return loss_p, grad_x, d_mod_lat, d_mod_ctx, gw, gs


def _my_pos():
    return lax.axis_index("x"), lax.axis_index("y"), lax.axis_index("c")


def _flip(p, bit):
    return 1 - p if bit else p


def _peer(pos, k):
    x, y, c = pos
    return (_flip(x, (k >> 2) & 1), _flip(y, (k >> 1) & 1), _flip(c, k & 1))


def _lin(pos):
    return 4 * pos[0] + 2 * pos[1] + pos[2]


def _allgather_small(v, name):
    r, w = v.shape

    def body(v_ref, out_ref, send_sems, recv_sems, local_sem):
        me = _my_pos()
        mine = pltpu.make_async_copy(v_ref, out_ref.at[_lin(me)], local_sem)
        mine.start()
        sends = []
        for k in range(1, N_DEV):
            cp = pltpu.make_async_remote_copy(src_ref=v_ref, dst_ref=out_ref.at[_lin(me)], send_sem=send_sems.at[k - 1],
                                              recv_sem=recv_sems.at[k - 1], device_id=_peer(me, k), device_id_type=MESH)
            cp.start()
            sends.append(cp)
        for k in range(1, N_DEV):
            peer = _peer(me, k)
            pltpu.make_async_remote_copy(src_ref=v_ref, dst_ref=out_ref.at[_lin(peer)], send_sem=send_sems.at[k - 1],
                                         recv_sem=recv_sems.at[k - 1], device_id=peer, device_id_type=MESH).wait_recv()
        for cp in sends:
            cp.wait_send()
        mine.wait()

    return pl.pallas_call(
        body,
        name=name,
        out_shape=jax.ShapeDtypeStruct((N_DEV, r, w), v.dtype),
        in_specs=[pl.BlockSpec(memory_space=pltpu.VMEM)],
        out_specs=pl.BlockSpec(memory_space=pltpu.VMEM),
        scratch_shapes=[pltpu.SemaphoreType.DMA((N_DEV - 1,)), pltpu.SemaphoreType.DMA((N_DEV - 1,)), pltpu.SemaphoreType.DMA],
        compiler_params=pltpu.CompilerParams(vmem_limit_bytes=VMEM_LIMIT_BYTES),
    )(v)


def _block_of(ref, kind, idx, n):
    start = pl.multiple_of(idx * n, 128)
    if kind == "col":
        return ref.at[:, pl.ds(start, n)]
    return ref.at[pl.ds(start, n), :]


def _allgather_weights(shards, kinds):
    nt = len(shards)
    out_shape = []
    for s, kind in zip(shards, kinds):
        k, n = s.shape
        out_shape.append(jax.ShapeDtypeStruct((k, n * N_DEV) if kind == "col" else (k * N_DEV, n), s.dtype))

    def body(*refs):
        ins, outs = refs[:nt], refs[nt:2 * nt]
        send_sems, recv_sems, local_sems = refs[2 * nt:]
        x, y, c = _my_pos()
        me, sibling = (x, y, c), (x, y, 1 - c)
        chips = [(1 - x, y), (x, 1 - y), (1 - x, 1 - y)]

        def blk(t, pos):
            n = shards[t].shape[1] if kinds[t] == "col" else shards[t].shape[0]
            return _block_of(outs[t], kinds[t], _lin(pos), n)

        def copy(t, k, block, to, src=None):
            return pltpu.make_async_remote_copy(src_ref=blk(t, block) if src is None else src, dst_ref=blk(t, block),
                                                send_sem=send_sems.at[t, k], recv_sem=recv_sems.at[t, k],
                                                device_id=to, device_id_type=MESH)

        local, sends = [], []
        for t in range(nt):
            mine = pltpu.make_async_copy(ins[t], blk(t, me), local_sems.at[t])
            mine.start()
            local.append(mine)
            first = [copy(t, 0, me, sibling, src=ins[t])]
            first += [copy(t, 1 + j, me, (*chip, c), src=ins[t]) for j, chip in enumerate(chips)]
            for cp in first:
                cp.start()
            sends += first
        for t in range(nt):
            for j, chip in enumerate(chips):
                copy(t, 1 + j, (*chip, c), me).wait_recv()
                fwd = copy(t, 4 + j, (*chip, c), sibling)
                fwd.start()
                sends.append(fwd)
        for t in range(nt):
            copy(t, 0, sibling, me).wait_recv()
            for j, chip in enumerate(chips):
                copy(t, 4 + j, (*chip, 1 - c), me).wait_recv()
        for cp in sends:
            cp.wait_send()
        for cp in local:
            cp.wait()

    any_spec = pl.BlockSpec(memory_space=pl.ANY)
    return pl.pallas_call(
        body,
        name="allgather_weights",
        out_shape=out_shape,
        in_specs=[any_spec] * nt,
        out_specs=[any_spec] * nt,
        scratch_shapes=[pltpu.SemaphoreType.DMA((nt, N_DEV - 1)), pltpu.SemaphoreType.DMA((nt, N_DEV - 1)),
                        pltpu.SemaphoreType.DMA((nt,))],
    )(*shards)


def _handshake(peers):
    barrier = pltpu.get_barrier_semaphore()
    for peer in peers:
        pl.semaphore_signal(barrier, inc=1, device_id=peer, device_id_type=MESH)
    pl.semaphore_wait(barrier, len(peers))


def _allgather_weights_seq(shards, kinds, name, collective_id):
    nt = len(shards)
    hbm = pltpu.MemorySpace.HBM
    ins = [jax.new_ref(s, memory_space=hbm) for s in shards]
    outs = []
    for s, kind in zip(shards, kinds):
        k, n = s.shape
        shape = (k, n * N_DEV) if kind == "col" else (k * N_DEV, n)
        outs.append(jax.empty_ref(jax.ShapeDtypeStruct(shape, s.dtype), memory_space=hbm))

    @functools.partial(
        pl.kernel, mesh=plsc.ScalarSubcoreMesh(axis_name="seq", num_cores=1), name=name,
        scratch_types=(pltpu.SemaphoreType.DMA((nt, N_DEV - 1)), pltpu.SemaphoreType.DMA((nt, N_DEV - 1)),
                       pltpu.SemaphoreType.DMA((nt,))),
        compiler_params=pltpu.CompilerParams(collective_id=collective_id))
    def launch(send_sems, recv_sems, local_sems):
        x, y, c = _my_pos()
        me, sibling = (x, y, c), (x, y, 1 - c)
        chips = [(1 - x, y), (x, 1 - y), (1 - x, 1 - y)]
        _handshake([sibling] + [(*chip, c) for chip in chips])

        def blk(t, pos):
            n = shards[t].shape[1] if kinds[t] == "col" else shards[t].shape[0]
            return _block_of(outs[t], kinds[t], _lin(pos), n)

        def copy(t, k, block, to, src=None):
            return pltpu.make_async_remote_copy(src_ref=blk(t, block) if src is None else src, dst_ref=blk(t, block),
                                                send_sem=send_sems.at[t, k], recv_sem=recv_sems.at[t, k],
                                                device_id=to, device_id_type=MESH)

        local, sends = [], []
        for t in range(nt):
            mine = pltpu.make_async_copy(ins[t], blk(t, me), local_sems.at[t])
            mine.start()
            local.append(mine)
            first = [copy(t, 0, me, sibling, src=ins[t])]
            first += [copy(t, 1 + j, me, (*chip, c), src=ins[t]) for j, chip in enumerate(chips)]
            for cp in first:
                cp.start()
            sends += first
        for t in range(nt):
            for j, chip in enumerate(chips):
                copy(t, 1 + j, (*chip, c), me).wait_recv()
                fwd = copy(t, 4 + j, (*chip, c), sibling)
                fwd.start()
                sends.append(fwd)
        for t in range(nt):
            copy(t, 0, sibling, me).wait_recv()
            for j, chip in enumerate(chips):
                copy(t, 4 + j, (*chip, 1 - c), me).wait_recv()
        for cp in sends:
            cp.wait_send()
        for cp in local:
            cp.wait()

    launch()
    return [o[...] for o in outs]


def _allgather_small_seq(v, name, collective_id):
    hbm = pltpu.MemorySpace.HBM
    src = jax.new_ref(v, memory_space=hbm)
    out = jax.empty_ref(jax.ShapeDtypeStruct((N_DEV,) + v.shape, v.dtype), memory_space=hbm)

    @functools.partial(
        pl.kernel, mesh=plsc.ScalarSubcoreMesh(axis_name="seq", num_cores=1), name=name,
        scratch_types=(pltpu.SemaphoreType.DMA((N_DEV - 1,)), pltpu.SemaphoreType.DMA((N_DEV - 1,)), pltpu.SemaphoreType.DMA),
        compiler_params=pltpu.CompilerParams(collective_id=collective_id))
    def launch(send_sems, recv_sems, local_sem):
        me = _my_pos()
        _handshake([_peer(me, k) for k in range(1, N_DEV)])
        mine = pltpu.make_async_copy(src, out.at[_lin(me)], local_sem)
        mine.start()
        sends = []
        for k in range(1, N_DEV):
            cp = pltpu.make_async_remote_copy(src_ref=src, dst_ref=out.at[_lin(me)], send_sem=send_sems.at[k - 1],
                                              recv_sem=recv_sems.at[k - 1], device_id=_peer(me, k), device_id_type=MESH)
            cp.start()
            sends.append(cp)
        for k in range(1, N_DEV):
            peer = _peer(me, k)
            pltpu.make_async_remote_copy(src_ref=src, dst_ref=out.at[_lin(peer)], send_sem=send_sems.at[k - 1],
                                         recv_sem=recv_sems.at[k - 1], device_id=peer, device_id_type=MESH).wait_recv()
        for cp in sends:
            cp.wait_send()
        mine.wait()

    launch()
    return out[...]


def _scatter_grads_seq(grads, kinds, name, collective_id):
    nt = len(grads)
    hbm = pltpu.MemorySpace.HBM
    shard_shapes = []
    for g, kind in zip(grads, kinds):
        k, n = g.shape
        shard_shapes.append((k, n // N_DEV) if kind == "col" else (k // N_DEV, n))
    ins = [jax.new_ref(g, memory_space=hbm) for g in grads]
    outs = [jax.empty_ref(jax.ShapeDtypeStruct((N_DEV,) + s, g.dtype), memory_space=hbm) for s, g in zip(shard_shapes, grads)]

    @functools.partial(
        pl.kernel, mesh=plsc.ScalarSubcoreMesh(axis_name="seq", num_cores=1), name=name,
        scratch_types=(pltpu.SemaphoreType.DMA((nt, N_DEV - 1)), pltpu.SemaphoreType.DMA((nt, N_DEV - 1)),
                       pltpu.SemaphoreType.DMA((nt,))),
        compiler_params=pltpu.CompilerParams(collective_id=collective_id))
    def launch(send_sems, recv_sems, local_sems):
        me = _my_pos()
        _handshake([_peer(me, k) for k in range(1, N_DEV)])

        def blk(t, pos):
            n = shard_shapes[t][1] if kinds[t] == "col" else shard_shapes[t][0]
            return _block_of(ins[t], kinds[t], _lin(pos), n)

        local, sends = [], []
        for t in range(nt):
            cp = pltpu.make_async_copy(blk(t, me), outs[t].at[_lin(me)], local_sems.at[t])
            cp.start()
            local.append(cp)
            for k in range(1, N_DEV):
                peer = _peer(me, k)
                cp = pltpu.make_async_remote_copy(src_ref=blk(t, peer), dst_ref=outs[t].at[_lin(me)], send_sem=send_sems.at[t, k - 1],
                                                  recv_sem=recv_sems.at[t, k - 1], device_id=peer, device_id_type=MESH)
                cp.start()
                sends.append(cp)
        for t in range(nt):
            for k in range(1, N_DEV):
                peer = _peer(me, k)
                pltpu.make_async_remote_copy(src_ref=blk(t, me), dst_ref=outs[t].at[_lin(peer)], send_sem=send_sems.at[t, k - 1],
                                             recv_sem=recv_sems.at[t, k - 1], device_id=peer, device_id_type=MESH).wait_recv()
        for cp in sends:
            cp.wait_send()
        for cp in local:
            cp.wait()

    launch()
    return [o[...] for o in outs]


_HBM_SPEC = pl.BlockSpec(memory_space=pltpu.HBM)
_SEM_SPEC = pl.BlockSpec(memory_space=pltpu.SEMAPHORE)
_EFFECT = pltpu.SideEffectType.DATAFLOW_SIDE_EFFECTING


def _shard_shapes(grads, kinds):
    return [(g.shape[0], g.shape[1] // N_DEV) if kind == "col" else (g.shape[0] // N_DEV, g.shape[1]) for g, kind in zip(grads, kinds)]


def _scatter_copies(g_refs, land_refs, send_sems, recv_sems, kinds, shard_shapes):
    me = _my_pos()
    copies = []
    for t in range(len(g_refs)):
        n = shard_shapes[t][1] if kinds[t] == "col" else shard_shapes[t][0]
        for k in range(1, N_DEV):
            peer = _peer(me, k)
            copies.append(pltpu.make_async_remote_copy(
                src_ref=_block_of(g_refs[t], kinds[t], _lin(peer), n), dst_ref=land_refs[t].at[_lin(me)],
                send_sem=send_sems.at[t * (N_DEV - 1) + k - 1], recv_sem=recv_sems.at[t * (N_DEV - 1) + k - 1],
                device_id=peer, device_id_type=MESH))
    return copies


def _scatter_start(grads, kinds, name):
    nt = len(grads)
    shard_shapes = _shard_shapes(grads, kinds)

    def body(*refs):
        g_refs, land_refs = refs[:nt], refs[nt:2 * nt]
        send_sems, recv_sems = refs[2 * nt], refs[2 * nt + 1]
        token = refs[2 * nt + 2 + 2 * nt]
        local_sems = refs[-1]
        me = _my_pos()
        for cp in _scatter_copies(g_refs, land_refs, send_sems, recv_sems, kinds, shard_shapes):
            cp.start()
        local = []
        for t in range(nt):
            n = shard_shapes[t][1] if kinds[t] == "col" else shard_shapes[t][0]
            cp = pltpu.make_async_copy(_block_of(g_refs[t], kinds[t], _lin(me), n), land_refs[t].at[_lin(me)], local_sems.at[t])
            cp.start()
            local.append(cp)
        token[...] = jnp.zeros_like(token)
        for cp in local:
            cp.wait()

    lands = [pltpu.with_memory_space_constraint(lax.empty((N_DEV,) + s, g.dtype), pltpu.HBM) for s, g in zip(shard_shapes, grads)]
    sem_shape = pltpu.SemaphoreType.DMA((nt * (N_DEV - 1),))
    out = pl.pallas_call(
        body,
        name=name,
        out_shape=(sem_shape, sem_shape, *[pltpu.HBM(g.shape, g.dtype) for g in grads],
                   *[pltpu.HBM(l.shape, l.dtype) for l in lands], jax.ShapeDtypeStruct((8, 128), F32)),
        in_specs=[_HBM_SPEC] * (2 * nt),
        out_specs=(_SEM_SPEC, _SEM_SPEC, *[_HBM_SPEC] * (2 * nt), pl.BlockSpec(memory_space=pltpu.VMEM)),
        input_output_aliases={i: 2 + i for i in range(2 * nt)},
        scratch_shapes=[pltpu.SemaphoreType.DMA((nt,))],
        compiler_params=pltpu.CompilerParams(has_side_effects=_EFFECT),
    )(*[pltpu.with_memory_space_constraint(g, pltpu.HBM) for g in grads], *lands)
    return out[0], out[1], list(out[2:2 + nt]), list(out[2 + nt:2 + 2 * nt]), out[-1]


def _scatter_wait(send_sems, recv_sems, g_thru, land_thru, kinds, after, name):
    nt = len(g_thru)
    shard_shapes = _shard_shapes(g_thru, kinds)

    def body(*refs):
        g_refs, land_refs = refs[:nt], refs[nt:2 * nt]
        send_sems, recv_sems = refs[2 * nt], refs[2 * nt + 1]
        for cp in _scatter_copies(g_refs, land_refs, send_sems, recv_sems, kinds, shard_shapes):
            cp.wait_send()
            cp.wait_recv()

    out = pl.pallas_call(
        body,
        name=name,
        out_shape=tuple(pltpu.HBM(a.shape, a.dtype) for a in (*g_thru, *land_thru)),
        in_specs=[*[_HBM_SPEC] * (2 * nt), _SEM_SPEC, _SEM_SPEC, pl.BlockSpec(memory_space=pl.ANY)],
        out_specs=tuple([_HBM_SPEC] * (2 * nt)),
        input_output_aliases={i: i for i in range(2 * nt)},
        compiler_params=pltpu.CompilerParams(has_side_effects=_EFFECT),
    )(*g_thru, *land_thru, send_sems, recv_sems, after)
    return list(out[nt:])


def _scatter_grads(grads, kinds):
    nt = len(grads)
    shard_shapes = []
    for g, kind in zip(grads, kinds):
        k, n = g.shape
        shard_shapes.append((k, n // N_DEV) if kind == "col" else (k // N_DEV, n))

    def body(*refs):
        ins, outs = refs[:nt], refs[nt:2 * nt]
        send_sems, recv_sems, local_sems = refs[2 * nt:]
        me = _my_pos()

        def blk(t, pos):
            n = shard_shapes[t][1] if kinds[t] == "col" else shard_shapes[t][0]
            return _block_of(ins[t], kinds[t], _lin(pos), n)

        local, sends = [], []
        for t in range(nt):
            cp = pltpu.make_async_copy(blk(t, me), outs[t].at[_lin(me)], local_sems.at[t])
            cp.start()
            local.append(cp)
            for k in range(1, N_DEV):
                peer = _peer(me, k)
                cp = pltpu.make_async_remote_copy(src_ref=blk(t, peer), dst_ref=outs[t].at[_lin(me)], send_sem=send_sems.at[t, k - 1],
                                                  recv_sem=recv_sems.at[t, k - 1], device_id=peer, device_id_type=MESH)
                cp.start()
                sends.append(cp)
        for t in range(nt):
            for k in range(1, N_DEV):
                peer = _peer(me, k)
                pltpu.make_async_remote_copy(src_ref=blk(t, me), dst_ref=outs[t].at[_lin(peer)], send_sem=send_sems.at[t, k - 1],
                                             recv_sem=recv_sems.at[t, k - 1], device_id=peer, device_id_type=MESH).wait_recv()
        for cp in sends:
            cp.wait_send()
        for cp in local:
            cp.wait()

    any_spec = pl.BlockSpec(memory_space=pl.ANY)
    return pl.pallas_call(
        body,
        name="scatter_grads",
        out_shape=[jax.ShapeDtypeStruct((N_DEV,) + s, g.dtype) for s, g in zip(shard_shapes, grads)],
        in_specs=[any_spec] * nt,
        out_specs=[any_spec] * nt,
        scratch_shapes=[pltpu.SemaphoreType.DMA((nt, N_DEV - 1)), pltpu.SemaphoreType.DMA((nt, N_DEV - 1)),
                        pltpu.SemaphoreType.DMA((nt,))],
    )(*grads)


def _adam(g_slots, w, m, v, *, tr, name):
    ns, r, wd = g_slots.shape
    tr = min(tr, r)
    assert r % tr == 0, (name, r, tr)
    c1 = 1.0 - ADAM_B1 ** ADAM_STEP
    c2 = 1.0 - ADAM_B2 ** ADAM_STEP

    def kern(g_ref, w_ref, m_ref, v_ref, go_ref, d_ref, mo_ref, vo_ref):
        g = g_ref[0].astype(F32)
        for s in range(1, ns):
            g = g + g_ref[s].astype(F32)
        m_new = ADAM_B1 * m_ref[...] + (1.0 - ADAM_B1) * g
        v_new = ADAM_B2 * v_ref[...] + (1.0 - ADAM_B2) * (g * g)
        m_hat = m_new / c1
        v_hat = v_new / c2
        go_ref[...] = g
        d_ref[...] = -ADAM_LR * (m_hat / (jnp.sqrt(v_hat) + ADAM_EPS) + ADAM_WD * w_ref[...])
        mo_ref[...] = m_new
        vo_ref[...] = v_new

    tile = pl.BlockSpec((tr, wd), lambda i: (i, 0))
    return pl.pallas_call(
        kern,
        name=name,
        grid=(r // tr,),
        in_specs=[pl.BlockSpec((ns, tr, wd), lambda i: (0, i, 0)), tile, tile, tile],
        out_specs=[tile] * 4,
        out_shape=[jax.ShapeDtypeStruct((r, wd), F32)] * 4,
        compiler_params=_cparams(("parallel",)),
    )(g_slots, w, m, v)


SMALL = ("c_ctx", "b_ada", "attn_sink", "ssm_a_re", "ssm_a_im", "ssm_log_dt", "ssm_b_re", "ssm_b_im", "ssm_c_re", "ssm_c_im",
         "ssm_d", "ln_mix_g", "ln_mix_b", "b_mlp1", "b_mlp2", "ln_mlp_g", "ln_mlp_b")
BIG = ("w_in", "w_glu", "w_attn_up", "w_ssm_up", "w_out", "w_mlp1", "w_mlp2")
BIG_KIND = ("col", "col", "col", "col", "row", "col", "row")
AG_GROUPS = (("w_in",), ("w_glu", "w_attn_up", "w_ssm_up", "w_out"), ("w_mlp1",), ("w_mlp2",))
AG_COLLECTIVE_ID0 = 1
RS_GROUPS = (("w_mlp2",), ("w_mlp1",), ("w_out", "w_attn_up", "w_ssm_up", "w_glu"), ("w_in",))
RS_COLLECTIVE_ID0 = AG_COLLECTIVE_ID0 + len(AG_GROUPS)
SMALL_EARLY = ("ssm_a_re", "ssm_a_im", "ssm_log_dt", "ssm_b_re", "ssm_b_im", "ssm_c_re", "ssm_c_im", "ssm_d")
SMALL_LATE = tuple(n for n in SMALL if n not in SMALL_EARLY)
SMALL_COLLECTIVE_ID0 = RS_COLLECTIVE_ID0 + len(RS_GROUPS)
LANES = 128


def _pack(parts):
    rows = []
    for p in parts:
        flat = p.reshape(-1).astype(F32)
        pad = (-flat.shape[0]) % LANES
        rows.append(jnp.pad(flat, (0, pad)).reshape(-1, LANES))
    packed = jnp.concatenate(rows, 0)
    return jnp.pad(packed, ((0, (-packed.shape[0]) % 8), (0, 0)))


def _unpack(packed, shapes):
    out, r0 = [], 0
    for s in shapes:
        n = math.prod(s)
        nr = -(-n // LANES)
        out.append(packed[r0:r0 + nr].reshape(-1)[:n].reshape(s))
        r0 += nr
    return out


WEIGHTS = ("c_ctx", "w_ada", "b_ada", "w_in", "attn_sink", "ssm_a_re", "ssm_a_im", "ssm_log_dt", "ssm_b_re", "ssm_b_im",
           "ssm_c_re", "ssm_c_im", "ssm_d", "w_glu", "w_attn_up", "w_ssm_up", "w_out", "ln_mix_g", "ln_mix_b", "w_mlp1",
           "b_mlp1", "w_mlp2", "b_mlp2", "ln_mlp_g", "ln_mlp_b")
ADA_COLS = 6 * D // N_DEV


def _step(x, c, ctx, loss_target, p, m, v):
    me = _lin(_my_pos())
    x2, ctx2, tgt2 = x[0], ctx[0], loss_target[0]

    wb = {}
    for gi, group in enumerate(AG_GROUPS):
        full = _allgather_weights_seq([p[n][0].astype(BF16) for n in group], [BIG_KIND[BIG.index(n)] for n in group],
                                      "allgather_seq%d" % gi, AG_COLLECTIVE_ID0 + gi)
        wb.update(zip(group, full))

    c_all = _allgather_small(jnp.broadcast_to(c, (8, D)), "gather_c")[:, 0, :]
    cc = p["c_ctx"].reshape(1, D)
    s_in = jnp.concatenate([c_all, cc, jnp.zeros((7, D), F32)], 0)
    s_act, = _rowwise(lambda rv, vv: ([_silu(rv[0])], []), [(s_in, D, 0, 0)], [], [(D, F32)], [], nrows=16, tr=16, name="silu_c")
    b_mine = lax.dynamic_slice_in_dim(p["b_ada"], me * ADA_COLS, ADA_COLS, axis=1)
    mod_part = _matmul(s_act, p["w_ada"][0], mode="nn", name="ada_fwd", tm=16, tn=512, bias=b_mine)
    mod_all = _allgather_small(mod_part, "gather_mod")
    mod_lat = lax.dynamic_index_in_dim(mod_all, me, axis=1, keepdims=False).reshape(1, 6 * D)
    mod_ctx = mod_all[:, 8, :].reshape(1, 6 * D)

    sp = {n: p[n][0] for n in SMALL if n not in ("c_ctx", "b_ada")}
    started = {}

    def on_grad(gw):
        for gi, group in enumerate(RS_GROUPS):
            if gi not in started and all(n in gw for n in group):
                kinds = [BIG_KIND[BIG.index(n)] for n in group]
                send_sems, recv_sems, g_thru, land_thru, token = _scatter_start([gw[n] for n in group], kinds, "scatter_start%d" % gi)
                started[gi] = (send_sems, recv_sems, g_thru, land_thru, kinds)
                for n in group:
                    gw.tokens[n] = token

    loss_p, grad_x, d_mod_lat, d_mod_ctx, gw, gs = _local_step(x2, ctx2, tgt2, mod_lat, mod_ctx, wb, sp, on_grad)
    recv = {}
    for gi, group in enumerate(RS_GROUPS):
        send_sems, recv_sems, g_thru, land_thru, kinds = started[gi]
        recv.update(zip(group, _scatter_wait(send_sems, recv_sems, g_thru, land_thru, kinds, grad_x, "scatter_wait%d" % gi)))

    g_early = _allgather_small_seq(_pack([gs[n] for n in SMALL_EARLY]), "gather_small_early", SMALL_COLLECTIVE_ID0)

    dm = jnp.concatenate([d_mod_lat, d_mod_ctx, jnp.zeros((6, 6 * D), F32)], 0)
    dm_all = _allgather_small_seq(dm, "gather_dmod", SMALL_COLLECTIVE_ID0 + 1)
    dm2 = jnp.concatenate([dm_all[:, 0, :], dm_all[:, 1, :]], 0)
    dm2_mine = lax.dynamic_slice_in_dim(dm2, me * ADA_COLS, ADA_COLS, axis=1)
    s2 = jnp.concatenate([s_act[0:8], jnp.broadcast_to(s_act[8:9], (8, D))], 0)
    g_w_ada = _matmul(s2, dm2_mine, mode="tn", name="dw_ada", tm=512, tn=ADA_COLS)
    dsc_part = _matmul(dm2_mine[8:16], p["w_ada"][0], mode="nt", name="d_silu_cctx", tm=8, tn=512)

    def cctx_b(rv, vv):
        _, pull = jax.vjp(_silu, vv[0])
        return [], [pull(jnp.sum(rv[0], axis=0, keepdims=True))[0]]

    g_cctx, = _rowwise(cctx_b, [(dsc_part, D, 0, 0)], [cc], [], [(1, D)], nrows=8, tr=8, name="cctx_bwd")
    gs["c_ctx"] = g_cctx
    gs["b_ada"] = d_mod_lat + d_mod_ctx

    res = {}
    for n in BIG:
        res[n] = _adam(recv[n], p[n][0], m[n][0], v[n][0], tr=256, name="adam_" + n)
    res["w_ada"] = _adam(g_w_ada[None], p["w_ada"][0], m["w_ada"][0], v["w_ada"][0], tr=256, name="adam_w_ada")

    g_late = _allgather_small_seq(_pack([gs[n] for n in SMALL_LATE]), "gather_small_late", SMALL_COLLECTIVE_ID0 + 2)
    for names, g_pack, tag in ((SMALL_EARLY, g_early, "early"), (SMALL_LATE, g_late, "late")):
        sm = _adam(g_pack, _pack([p[n] for n in names]), _pack([m[n] for n in names]), _pack([v[n] for n in names]),
                   tr=g_pack.shape[1], name="adam_small_" + tag)
        shapes = [p[n].shape for n in names]
        for j, outs in enumerate(zip(*[_unpack(a, shapes) for a in sm])):
            res[names[j]] = outs

    loss = lax.psum(loss_p[0, 0], ("x", "y", "c"))
    outs = [loss, grad_x[None]]
    for j in range(4):
        outs += [res[n][j].reshape(p[n].shape) for n in WEIGHTS]
    return tuple(outs)


def kernel(x, c, ctx, c_ctx, w_ada, b_ada, w_in, attn_sink, ssm_a_re, ssm_a_im, ssm_log_dt, ssm_b_re, ssm_b_im, ssm_c_re, ssm_c_im, ssm_d, w_glu, w_attn_up, w_ssm_up, w_out, ln_mix_g, ln_mix_b, w_mlp1, b_mlp1, w_mlp2, b_mlp2, ln_mlp_g, ln_mlp_b, loss_target, m_c_ctx, m_w_ada, m_b_ada, m_w_in, m_attn_sink, m_ssm_a_re, m_ssm_a_im, m_ssm_log_dt, m_ssm_b_re, m_ssm_b_im, m_ssm_c_re, m_ssm_c_im, m_ssm_d, m_w_glu, m_w_attn_up, m_w_ssm_up, m_w_out, m_ln_mix_g, m_ln_mix_b, m_w_mlp1, m_b_mlp1, m_w_mlp2, m_b_mlp2, m_ln_mlp_g, m_ln_mlp_b, v_c_ctx, v_w_ada, v_b_ada, v_w_in, v_attn_sink, v_ssm_a_re, v_ssm_a_im, v_ssm_log_dt, v_ssm_b_re, v_ssm_b_im, v_ssm_c_re, v_ssm_c_im, v_ssm_d, v_w_glu, v_w_attn_up, v_w_ssm_up, v_w_out, v_ln_mix_g, v_ln_mix_b, v_w_mlp1, v_b_mlp1, v_w_mlp2, v_b_mlp2, v_ln_mlp_g, v_ln_mlp_b):
    given = dict(locals())
    p = {n: given[n] for n in WEIGHTS}
    m = {n: given["m_" + n] for n in WEIGHTS}
    v = {n: given["v_" + n] for n in WEIGHTS}
    return _step(x, c, ctx, loss_target, p, m, v)
```

```python
import functools
import math

import jax
import jax.numpy as jnp
from jax import lax
from jax.experimental import pallas as pl
from jax.experimental.pallas import tpu as pltpu
from jax.experimental.pallas import tpu_sc as plsc

F32 = jnp.float32
BF16 = jnp.bfloat16

N_DEV = 8
D = 2048
T = 2048
C = 256
TA = T + C
GRID_W = 64
HD = 128
NH = 8
NKV = 2
GROUP = NH // NKV
WINDOW = 128
QW = NH * HD
KVW = NKV * HD
SW = D // 4
SG = 16
NG = SW // SG
SP = 64
DFF = 4 * D
IN_COLS = QW + 2 * KVW + SW + 2 * D
ALPHA = 2.0 ** 0.25
LN_EPS = 1e-6
NEG_INF = -1e30
ROPE_BASE = 10000.0
ATT_SCALE = HD ** -0.5

NSEG = 8
GBLK = 8
NBLK = NG // GBLK
BW = GBLK * SP
UW = GBLK * SG

ADAM_LR = 0.001
ADAM_B1 = 0.9
ADAM_B2 = 0.999
ADAM_EPS = 1e-08
ADAM_WD = 0.01
ADAM_STEP = 10

VMEM_LIMIT_BYTES = 56 * 1024 * 1024
MESH = pl.DeviceIdType.MESH


def _cparams(sem=None):
    return pltpu.CompilerParams(dimension_semantics=sem, vmem_limit_bytes=VMEM_LIMIT_BYTES)


def _matmul(a, b, *, mode, name, out_dtypes=(F32,), tm=512, tn=512, tk=None, bias=None, extras=(), epilogue=None, after=()):
    if mode == "nn":
        (M, K), (K2, N) = a.shape, b.shape
    elif mode == "nt":
        (M, K), (N, K2) = a.shape, b.shape
    else:
        (K, M), (K2, N) = a.shape, b.shape
    assert K == K2, (name, a.shape, b.shape)
    tm, tn, tk = min(tm, M), min(tn, N), min(tk or K, K)
    assert M % tm == 0 and N % tn == 0 and K % tk == 0, (name, M, N, K, tm, tn, tk)
    nk = K // tk
    if mode == "tn":
        a_spec = pl.BlockSpec((tk, tm), lambda i, j, k: (k, i))
    else:
        a_spec = pl.BlockSpec((tm, tk), lambda i, j, k: (i, k))
    if mode == "nt":
        b_spec = pl.BlockSpec((tn, tk), lambda i, j, k: (j, k))
    else:
        b_spec = pl.BlockSpec((tk, tn), lambda i, j, k: (k, j))
    dims = {"nn": (((1,), (0,)), ((), ())), "nt": (((1,), (1,)), ((), ())), "tn": (((0,), (0,)), ((), ()))}[mode]
    in_specs = [a_spec, b_spec]
    operands = [a, b]
    if bias is not None:
        in_specs.append(pl.BlockSpec((1, tn), lambda i, j, k: (0, j)))
        operands.append(bias)
    for e in extras:
        in_specs.append(pl.BlockSpec((tm, tn), lambda i, j, k: (i, j)))
        operands.append(e)
    n_ex = len(extras)
    for t in after:
        in_specs.append(pl.BlockSpec(memory_space=pl.ANY))
        operands.append(t)
    n_after = len(after)
    n_out = len(out_dtypes)
    has_bias = bias is not None

    def kern(*refs):
        a_ref, b_ref = refs[0], refs[1]
        pos = 2
        bias_ref = None
        if has_bias:
            bias_ref = refs[pos]
            pos += 1
        ex_refs = refs[pos:pos + n_ex]
        pos += n_ex + n_after
        out_refs = refs[pos:pos + n_out]
        acc_ref = refs[pos + n_out] if nk > 1 else None

        def finish(r):
            if has_bias:
                r = r + bias_ref[...]
            outs = epilogue(r, *[e[...] for e in ex_refs]) if epilogue is not None else (r,)
            for o_ref, o in zip(out_refs, outs):
                o_ref[...] = o.astype(o_ref.dtype)

        part = lax.dot_general(a_ref[...].astype(BF16), b_ref[...].astype(BF16), dims, preferred_element_type=F32)
        if nk == 1:
            finish(part)
        else:
            k = pl.program_id(2)

            @pl.when(k == 0)
            def _():
                acc_ref[...] = part

            @pl.when(k > 0)
            def _():
                acc_ref[...] += part

            @pl.when(k == nk - 1)
            def _():
                finish(acc_ref[...])

    outs = pl.pallas_call(
        kern,
        name=name,
        grid=(M // tm, N // tn, nk),
        in_specs=in_specs,
        out_specs=[pl.BlockSpec((tm, tn), lambda i, j, k: (i, j)) for _ in out_dtypes],
        out_shape=[jax.ShapeDtypeStruct((M, N), dt) for dt in out_dtypes],
        scratch_shapes=[pltpu.VMEM((tm, tn), F32)] if nk > 1 else [],
        compiler_params=_cparams(("parallel", "parallel", "arbitrary")),
    )(*operands)
    return outs[0] if n_out == 1 else tuple(outs)


def _rowwise(fn, rows, vecs, outs, vec_outs, *, nrows, tr, name):
    n_rows, n_vecs, n_outs = len(rows), len(vecs), len(outs)
    in_specs = [pl.BlockSpec((tr, w), lambda i, cb=cb, ro=ro: (i + ro, cb)) for (_, w, cb, ro) in rows]
    in_specs += [pl.BlockSpec(v.shape, lambda i: (0, 0)) for v in vecs]
    out_specs = [pl.BlockSpec((tr, w), lambda i: (i, 0)) for (w, _) in outs]
    out_specs += [pl.BlockSpec(s, lambda i: (0, 0)) for s in vec_outs]
    out_shape = [jax.ShapeDtypeStruct((nrows, w), dt) for (w, dt) in outs]
    out_shape += [jax.ShapeDtypeStruct(s, F32) for s in vec_outs]

    def kern(*refs):
        rvals = [r[...] for r in refs[:n_rows]]
        vvals = [r[...] for r in refs[n_rows:n_rows + n_vecs]]
        o_refs = refs[n_rows + n_vecs:n_rows + n_vecs + n_outs]
        v_refs = refs[n_rows + n_vecs + n_outs:]
        ro, vo = fn(rvals, vvals)
        for r, val in zip(o_refs, ro):
            r[...] = val.astype(r.dtype)
        i = pl.program_id(0)
        for r, val in zip(v_refs, vo):
            @pl.when(i == 0)
            def _(r=r, val=val):
                r[...] = val.astype(F32)

            @pl.when(i > 0)
            def _(r=r, val=val):
                r[...] += val.astype(F32)

    res = pl.pallas_call(
        kern,
        name=name,
        grid=(nrows // tr,),
        in_specs=in_specs,
        out_specs=out_specs,
        out_shape=out_shape,
        compiler_params=_cparams(("arbitrary",)),
    )(*[r[0] for r in rows], *vecs)
    return list(res)


def _ln(x):
    mu = jnp.mean(x, axis=-1, keepdims=True)
    xc = x - mu
    var = jnp.mean(xc * xc, axis=-1, keepdims=True)
    return xc * lax.rsqrt(var + LN_EPS)


def _sigmoid(x):
    return 1.0 / (1.0 + jnp.exp(-x))


def _gelu(x):
    return 0.5 * x * (1.0 + jnp.tanh(math.sqrt(2.0 / math.pi) * (x + 0.044715 * (x * x * x))))


def _silu(x):
    return x * _sigmoid(x)


def _f_ln_mod(x, sc, sh):
    return _ln(x) * (1.0 + sc) + sh


def _f_glu(z):
    return z[:, :SW] * _sigmoid(z[:, SW:])


def _f_mix(ga, gs, attn_d, ssm_d):
    return _sigmoid(ga) * attn_d + _sigmoid(gs) * ssm_d


def _f_post1(x, y, g1, lg, lb, sc2, sh2):
    r1 = ALPHA * x + g1 * y
    x1 = _ln(r1) * lg + lb
    h2 = _ln(x1) * (1.0 + sc2) + sh2
    return x1, h2


def _f_loss(x1, mlp, tgt, g2, lg, lb, b2z):
    r2 = ALPHA * x1 + g2 * (mlp + b2z)
    out = _ln(r2) * lg + lb
    err = out - tgt
    return 0.5 * jnp.sum(err * err) * (1.0 / D)


def _rope_tables():
    rows = T // GRID_W
    row = jnp.repeat(jnp.arange(rows), GRID_W)
    col = jnp.tile(jnp.arange(GRID_W), rows)
    n_freq = HD // 4
    freqs = ROPE_BASE ** (-jnp.arange(n_freq, dtype=F32) / n_freq)
    ang_r = row.astype(F32)[:, None] * freqs
    ang_c = col.astype(F32)[:, None] * freqs
    ang = jnp.concatenate([ang_r, ang_r, ang_c, ang_c], -1)
    cos, sin = jnp.cos(ang), jnp.sin(ang)
    lo = (jnp.arange(HD) % (HD // 2)) < (HD // 4)
    sin_a = jnp.where(lo[None, :], -sin, 0.0)
    sin_b = jnp.where(lo[None, :], 0.0, sin)
    return cos, sin_a, sin_b


def _rope(x, cos, sa, sb):
    return x * cos + pltpu.roll(x, 96, 1) * sa + pltpu.roll(x, 32, 1) * sb


def _rope_t(dy, cos, sa, sb):
    return dy * cos + pltpu.roll(dy * sa, 32, 1) + pltpu.roll(dy * sb, 96, 1)


BAND = 3 * WINDOW
KPAD = T + 2 * WINDOW


def _attn_fill_kv(k_ref, v_ref, cos_ref, sa_ref, sb_ref, kp, vp, kc, vc):
    zeros = jnp.zeros((WINDOW, KVW), BF16)
    kp[0:WINDOW, :] = zeros
    kp[WINDOW + T:KPAD, :] = zeros
    vp[0:WINDOW, :] = zeros
    vp[WINDOW + T:KPAD, :] = zeros
    for hh in range(NKV):
        cs = slice(hh * HD, (hh + 1) * HD)
        for r0 in range(0, T, 512):
            rs = slice(r0, r0 + 512)
            kr = _rope(k_ref[rs, cs], cos_ref[rs, :], sa_ref[rs, :], sb_ref[rs, :])
            kp[WINDOW + r0:WINDOW + r0 + 512, cs] = kr.astype(BF16)
    vp[WINDOW:WINDOW + T, :] = v_ref[0:T, :].astype(BF16)
    kc[...] = k_ref[T:TA, :].astype(BF16)
    vc[...] = v_ref[T:TA, :].astype(BF16)


def _attn_scores(n, h, q_ref, cos_ref, sa_ref, sb_ref, sink_ref, kp, kc):
    kvh = h // GROUP
    r0 = pl.multiple_of(n * WINDOW, WINDOW)
    cos = cos_ref[pl.ds(r0, WINDOW), :]
    sa = sa_ref[pl.ds(r0, WINDOW), :]
    sb = sb_ref[pl.ds(r0, WINDOW), :]
    q_h = _rope(q_ref[:, h * HD:(h + 1) * HD], cos, sa, sb).astype(BF16)
    kb = kp[pl.ds(r0, BAND), kvh * HD:(kvh + 1) * HD]
    kcb = kc[:, kvh * HD:(kvh + 1) * HD]
    nt = (((1,), (1,)), ((), ()))
    s_loc = lax.dot_general(q_h, kb, nt, preferred_element_type=F32) * ATT_SCALE
    s_ctx = lax.dot_general(q_h, kcb, nt, preferred_element_type=F32) * ATT_SCALE
    row = lax.broadcasted_iota(jnp.int32, (WINDOW, BAND), 0)
    col = lax.broadcasted_iota(jnp.int32, (WINDOW, BAND), 1)
    rel = col - WINDOW - row
    kpos = r0 - WINDOW + col
    valid = (jnp.abs(rel) <= WINDOW) & (kpos >= 0) & (kpos < T)
    s_loc = jnp.where(valid, s_loc, NEG_INF)
    sk = sink_ref[0:1, h:h + 1]
    m = jnp.maximum(jnp.maximum(jnp.max(s_loc, -1, keepdims=True), jnp.max(s_ctx, -1, keepdims=True)), sk)
    e_loc = jnp.exp(s_loc - m)
    e_ctx = jnp.exp(s_ctx - m)
    e_sink = jnp.exp(sk - m)
    inv = 1.0 / (jnp.sum(e_loc, -1, keepdims=True) + jnp.sum(e_ctx, -1, keepdims=True) + e_sink)
    return q_h, r0, e_loc * inv, e_ctx * inv, e_sink * inv


def _attn_fwd(proj, sink, tabs):
    cos, sa, sb = tabs

    def kern(q_ref, k_ref, v_ref, cos_ref, sa_ref, sb_ref, sink_ref, o_ref, kp, vp, kc, vc):
        n = pl.program_id(0)

        @pl.when(n == 0)
        def _():
            _attn_fill_kv(k_ref, v_ref, cos_ref, sa_ref, sb_ref, kp, vp, kc, vc)

        for h in range(NH):
            kvh = h // GROUP
            _, r0, p_loc, p_ctx, _ = _attn_scores(n, h, q_ref, cos_ref, sa_ref, sb_ref, sink_ref, kp, kc)
            vb = vp[pl.ds(r0, BAND), kvh * HD:(kvh + 1) * HD]
            vcb = vc[:, kvh * HD:(kvh + 1) * HD]
            o = jnp.dot(p_loc.astype(BF16), vb, preferred_element_type=F32)
            o = o + jnp.dot(p_ctx.astype(BF16), vcb, preferred_element_type=F32)
            o_ref[:, h * HD:(h + 1) * HD] = o.astype(o_ref.dtype)

    full = lambda shape: pl.BlockSpec(shape, lambda n: (0, 0))
    return pl.pallas_call(
        kern,
        name="attn_fwd",
        grid=(T // WINDOW,),
        in_specs=[
            pl.BlockSpec((WINDOW, QW), lambda n: (n, 0)),
            pl.BlockSpec((TA, KVW), lambda n: (0, QW // KVW)),
            pl.BlockSpec((TA, KVW), lambda n: (0, QW // KVW + 1)),
            full((T, HD)), full((T, HD)), full((T, HD)), full((1, NH)),
        ],
        out_specs=pl.BlockSpec((WINDOW, QW), lambda n: (n, 0)),
        out_shape=jax.ShapeDtypeStruct((T, QW), BF16),
        scratch_shapes=[pltpu.VMEM((KPAD, KVW), BF16), pltpu.VMEM((KPAD, KVW), BF16),
                        pltpu.VMEM((C, KVW), BF16), pltpu.VMEM((C, KVW), BF16)],
        compiler_params=_cparams(("arbitrary",)),
    )(proj, proj, proj, cos, sa, sb, sink)


def _attn_bwd(proj, d_attn, sink, tabs):
    cos, sa, sb = tabs
    n_blocks = T // WINDOW

    def kern(q_ref, k_ref, v_ref, do_ref, cos_ref, sa_ref, sb_ref, sink_ref,
             dq_ref, dk_ref, dv_ref, dsink_ref, kp, vp, kc, vc, dkp, dvp, dkc, dvc):
        n = pl.program_id(0)

        @pl.when(n == 0)
        def _():
            _attn_fill_kv(k_ref, v_ref, cos_ref, sa_ref, sb_ref, kp, vp, kc, vc)
            dkp[...] = jnp.zeros_like(dkp)
            dvp[...] = jnp.zeros_like(dvp)
            dkc[...] = jnp.zeros_like(dkc)
            dvc[...] = jnp.zeros_like(dvc)
            dsink_ref[...] = jnp.zeros_like(dsink_ref)

        nt = (((1,), (1,)), ((), ()))
        tn = (((0,), (0,)), ((), ()))
        for h in range(NH):
            kvh = h // GROUP
            cs = slice(kvh * HD, (kvh + 1) * HD)
            q_h, r0, p_loc, p_ctx, p_sink = _attn_scores(n, h, q_ref, cos_ref, sa_ref, sb_ref, sink_ref, kp, kc)
            kb = kp[pl.ds(r0, BAND), cs]
            vb = vp[pl.ds(r0, BAND), cs]
            kcb = kc[:, cs]
            vcb = vc[:, cs]
            do_h = do_ref[:, h * HD:(h + 1) * HD]
            dp_loc = lax.dot_general(do_h, vb, nt, preferred_element_type=F32)
            dp_ctx = lax.dot_general(do_h, vcb, nt, preferred_element_type=F32)
            delta = jnp.sum(p_loc * dp_loc, -1, keepdims=True) + jnp.sum(p_ctx * dp_ctx, -1, keepdims=True)
            ds_loc = (p_loc * (dp_loc - delta) * ATT_SCALE).astype(BF16)
            ds_ctx = (p_ctx * (dp_ctx - delta) * ATT_SCALE).astype(BF16)
            dq = jnp.dot(ds_loc, kb, preferred_element_type=F32) + jnp.dot(ds_ctx, kcb, preferred_element_type=F32)
            cos = cos_ref[pl.ds(r0, WINDOW), :]
            sa_ = sa_ref[pl.ds(r0, WINDOW), :]
            sb_ = sb_ref[pl.ds(r0, WINDOW), :]
            dq_ref[:, h * HD:(h + 1) * HD] = _rope_t(dq, cos, sa_, sb_).astype(dq_ref.dtype)
            dkp[pl.ds(r0, BAND), cs] += lax.dot_general(ds_loc, q_h, tn, preferred_element_type=F32)
            dkc[:, cs] += lax.dot_general(ds_ctx, q_h, tn, preferred_element_type=F32)
            dvp[pl.ds(r0, BAND), cs] += lax.dot_general(p_loc.astype(BF16), do_h, tn, preferred_element_type=F32)
            dvc[:, cs] += lax.dot_general(p_ctx.astype(BF16), do_h, tn, preferred_element_type=F32)
            dsk = -jnp.sum(p_sink * delta, axis=0, keepdims=True)
            dsink_ref[h:h + 1, :] += jnp.broadcast_to(dsk, (1, HD))

        @pl.when(n == n_blocks - 1)
        def _():
            for hh in range(NKV):
                cs = slice(hh * HD, (hh + 1) * HD)
                for r0 in range(0, T, 512):
                    rs = slice(r0, r0 + 512)
                    g = dkp[WINDOW + r0:WINDOW + r0 + 512, cs]
                    dk_ref[rs, cs] = _rope_t(g, cos_ref[rs, :], sa_ref[rs, :], sb_ref[rs, :]).astype(dk_ref.dtype)
            dk_ref[T:TA, :] = dkc[...].astype(dk_ref.dtype)
            dv_ref[0:T, :] = dvp[WINDOW:WINDOW + T, :].astype(dv_ref.dtype)
            dv_ref[T:TA, :] = dvc[...].astype(dv_ref.dtype)

    full = lambda shape: pl.BlockSpec(shape, lambda n: (0, 0))
    return pl.pallas_call(
        kern,
        name="attn_bwd",
        grid=(n_blocks,),
        in_specs=[
            pl.BlockSpec((WINDOW, QW), lambda n: (n, 0)),
            pl.BlockSpec((TA, KVW), lambda n: (0, QW // KVW)),
            pl.BlockSpec((TA, KVW), lambda n: (0, QW // KVW + 1)),
            pl.BlockSpec((WINDOW, QW), lambda n: (n, 0)),
            full((T, HD)), full((T, HD)), full((T, HD)), full((1, NH)),
        ],
        out_specs=[pl.BlockSpec((WINDOW, QW), lambda n: (n, 0)), full((TA, KVW)), full((TA, KVW)), full((NH, HD))],
        out_shape=[jax.ShapeDtypeStruct((T, QW), BF16), jax.ShapeDtypeStruct((TA, KVW), BF16),
                   jax.ShapeDtypeStruct((TA, KVW), BF16), jax.ShapeDtypeStruct((NH, HD), F32)],
        scratch_shapes=[pltpu.VMEM((KPAD, KVW), BF16), pltpu.VMEM((KPAD, KVW), BF16),
                        pltpu.VMEM((C, KVW), BF16), pltpu.VMEM((C, KVW), BF16),
                        pltpu.VMEM((KPAD, KVW), F32), pltpu.VMEM((KPAD, KVW), F32),
                        pltpu.VMEM((C, KVW), F32), pltpu.VMEM((C, KVW), F32)],
        compiler_params=_cparams(("arbitrary",)),
    )(proj, proj, proj, d_attn, cos, sa, sb, sink)


def _s5_prep(a_re, a_im, log_dt, b_re, b_im, c_re, c_im):
    lam = lax.complex(a_re, a_im)
    dt = jnp.exp(log_dt)[..., None]
    lam_bar = jnp.exp(lam * dt)
    b_bar = ((lam_bar - 1.0) / lam)[..., None] * lax.complex(b_re, b_im)
    eye = jnp.eye(GBLK, dtype=F32)

    def lam_rows(v):
        return v.reshape(2, NBLK, 1, BW)

    lam_l = jnp.concatenate([lam_rows(jnp.real(lam_bar)), lam_rows(jnp.imag(lam_bar))], -1)
    lam_l = jnp.broadcast_to(lam_l, (2, NBLK, 8, 2 * BW))

    def b_blocks(v):
        v = v.reshape(2, NBLK, GBLK, SP, SG).transpose(0, 1, 2, 4, 3)
        return (v[:, :, :, :, None, :] * eye[None, None, :, None, :, None]).reshape(2, NBLK, UW, BW)

    bmat = jnp.concatenate([b_blocks(jnp.real(b_bar)), b_blocks(jnp.imag(b_bar))], -1)

    def c_blocks(v):
        v = v.reshape(2, NBLK, GBLK, SG, SP).transpose(0, 1, 2, 4, 3)
        return (v[:, :, :, :, None, :] * eye[None, None, :, None, :, None]).reshape(2, NBLK, BW, UW)

    cmat = jnp.concatenate([c_blocks(c_re), -c_blocks(c_im)], 2)
    return lam_l, bmat, cmat


def _cmul(ar, ai, br, bi):
    return ar * br - ai * bi, ar * bi + ai * br


def _shift_rows(x, rev, fill):
    r = lax.broadcasted_iota(jnp.int32, x.shape, 0)
    down = jnp.where(r == 0, fill, pltpu.roll(x, 1, 0))
    up = jnp.where(r == NSEG - 1, fill, pltpu.roll(x, NSEG - 1, 0))
    return jnp.where(rev == 0, down, up)


def _edge_row(x, rev):
    last = jnp.broadcast_to(x[NSEG - 1:NSEG, :], x.shape)
    first = jnp.broadcast_to(x[0:1, :], x.shape)
    return jnp.where(rev == 0, last, first)


def _seg_scan(get, put, base, seglen, lr, li, rev, cin):
    zero = jnp.zeros((NSEG, BW), F32)

    def rows(k):
        j = jnp.where(rev == 0, k, seglen - 1 - k)
        return pl.ds(base + j, NSEG, stride=seglen)

    def local(k, carry):
        sr, si, pr, pi = carry
        xr, xi = get(rows(k))
        tr, ti = _cmul(lr, li, sr, si)
        sr, si = tr + xr, ti + xi
        put(rows(k), sr, si)
        pr, pi = _cmul(lr, li, pr, pi)
        return sr, si, pr, pi

    er, ei, lpr, lpi = lax.fori_loop(0, seglen, local, (zero, zero, zero + 1.0, zero))
    cr, ci = _shift_rows(zero, rev, cin[0]), _shift_rows(zero, rev, cin[1])
    for _ in range(NSEG - 1):
        tr, ti = _cmul(lpr, lpi, cr, ci)
        cr, ci = _shift_rows(er + tr, rev, cin[0]), _shift_rows(ei + ti, rev, cin[1])

    def fix(k, carry):
        pr, pi = carry
        xr, xi = get(rows(k))
        tr, ti = _cmul(pr, pi, cr, ci)
        put(rows(k), xr + tr, xi + ti)
        return _cmul(lr, li, pr, pi)

    lax.fori_loop(0, seglen, fix, (lr, li))
    tr, ti = _cmul(lpr, lpi, cr, ci)
    return _edge_row(er + tr, rev), _edge_row(ei + ti, rev)


RCH = 256
CSEG = C // NSEG
TSEG = T // NSEG
UCOL0 = (QW + 2 * KVW) // UW


LANES_ = 128
NCG = 2 * BW // LANES_
HCG = NCG // 2


def _state_access(ref, lead=()):
    def get(rows):
        re = jnp.concatenate([ref[(*lead, g, rows, slice(None))] for g in range(HCG)], axis=-1)
        im = jnp.concatenate([ref[(*lead, g, rows, slice(None))] for g in range(HCG, NCG)], axis=-1)
        return re, im

    def put(rows, re, im):
        for g in range(HCG):
            ref[(*lead, g, rows, slice(None))] = re[:, g * LANES_:(g + 1) * LANES_]
            ref[(*lead, HCG + g, rows, slice(None))] = im[:, g * LANES_:(g + 1) * LANES_]

    return get, put


def _state_rows(ref, lead, rs):
    return jnp.concatenate([ref[(*lead, g, rs, slice(None))] for g in range(NCG)], axis=-1)


def _state_store(ref, lead, rs, val):
    for g in range(NCG):
        ref[(*lead, g, rs, slice(None))] = val[:, g * LANES_:(g + 1) * LANES_]


def _s5_fwd(proj, dskip, lam, bmat, cmat):
    def kern(u_ref, dk_ref, lam_ref, b_ref, c_ref, s_ref, ssm_ref, ge_ref):
        d = pl.program_id(1)
        bm = b_ref[0, 0].astype(BF16)
        for r0 in range(0, TA, RCH):
            _state_store(s_ref, (0, 0), slice(r0, r0 + RCH),
                         jnp.dot(u_ref[r0:r0 + RCH, :].astype(BF16), bm, preferred_element_type=F32))
        lr = lam_ref[0, 0, :, 0:BW]
        li = lam_ref[0, 0, :, BW:2 * BW]
        zero = jnp.zeros((NSEG, BW), F32)
        get, put = _state_access(s_ref, (0, 0))
        mid = _seg_scan(get, put, T, CSEG, lr, li, d, (zero, zero))
        _seg_scan(get, put, 0, TSEG, lr, li, d, mid)
        cm = c_ref[0, 0].astype(BF16)
        for r0 in range(0, T, RCH):
            y = jnp.dot(_state_rows(s_ref, (0, 0), slice(r0, r0 + RCH)).astype(BF16), cm, preferred_element_type=F32)

            @pl.when(d == 0)
            def _(y=y, r0=r0):
                ssm_ref[r0:r0 + RCH, :] = y + dk_ref[...] * u_ref[r0:r0 + RCH, :]

            @pl.when(d == 1)
            def _(y=y, r0=r0):
                s = ssm_ref[r0:r0 + RCH, :] + y
                ssm_ref[r0:r0 + RCH, :] = s
                ge_ref[r0:r0 + RCH, :] = _gelu(s).astype(ge_ref.dtype)

    blk4 = lambda shape: pl.BlockSpec((1, 1) + shape, lambda b, d: (d, b, 0, 0))
    return pl.pallas_call(
        kern,
        name="s5_fwd",
        grid=(NBLK, 2),
        in_specs=[pl.BlockSpec((TA, UW), lambda b, d: (0, UCOL0 + b)), pl.BlockSpec((1, UW), lambda b, d: (0, b)),
                  blk4((8, 2 * BW)), blk4((UW, 2 * BW)), blk4((2 * BW, UW))],
        out_specs=[pl.BlockSpec((1, 1, NCG, TA, LANES_), lambda b, d: (d, b, 0, 0, 0)),
                   pl.BlockSpec((T, UW), lambda b, d: (0, b)), pl.BlockSpec((T, UW), lambda b, d: (0, b))],
        out_shape=[jax.ShapeDtypeStruct((2, NBLK, NCG, TA, LANES_), F32), jax.ShapeDtypeStruct((T, SW), F32),
                   jax.ShapeDtypeStruct((T, SW), BF16)],
        compiler_params=_cparams(("parallel", "arbitrary")),
    )(proj, dskip, lam, bmat, cmat)


def _s5_bwd(d_ge, ssm, proj, dskip, states, lam, bmat, cmat):
    nt = (((1,), (1,)), ((), ()))
    tn = (((0,), (0,)), ((), ()))

    def kern(dge_ref, ssm_ref, u_ref, dk_ref, s_ref, lam_ref, b_ref, c_ref,
             du_ref, ddk_ref, dlam_ref, db_ref, dc_ref, g_ref, dua_ref, dssm_ref):
        d = pl.program_id(1)

        @pl.when(d == 0)
        def _():
            ddk = jnp.zeros((1, UW), F32)
            for r0 in range(0, T, RCH):
                rs = slice(r0, r0 + RCH)
                _, pull = jax.vjp(_gelu, ssm_ref[rs, :])
                dssm = pull(dge_ref[rs, :])[0]
                dssm_ref[rs, :] = dssm
                dua_ref[rs, :] = dssm * dk_ref[...]
                ddk = ddk + jnp.sum(dssm * u_ref[rs, :], axis=0, keepdims=True)
            dua_ref[T:TA, :] = jnp.zeros((C, UW), F32)
            ddk_ref[...] = ddk

        cm = c_ref[0, 0].astype(BF16)
        for r0 in range(0, T, RCH):
            _state_store(g_ref, (), slice(r0, r0 + RCH),
                         lax.dot_general(dssm_ref[r0:r0 + RCH, :].astype(BF16), cm, nt, preferred_element_type=F32))
        _state_store(g_ref, (), slice(T, TA), jnp.zeros((C, 2 * BW), F32))
        lr = lam_ref[0, 0, :, 0:BW]
        li = lam_ref[0, 0, :, BW:2 * BW]
        zero = jnp.zeros((NSEG, BW), F32)
        get_g, put_g = _state_access(g_ref)

        mid = _seg_scan(get_g, put_g, 0, TSEG, lr, -li, 1 - d, (zero, zero))
        _seg_scan(get_g, put_g, T, CSEG, lr, -li, 1 - d, mid)

        get_s, _ = _state_access(s_ref, (0, 0))

        def dlam_terms(g, s):
            return g[0] * s[0] + g[1] * s[1], g[1] * s[0] - g[0] * s[1]

        def dlam_region(base, seglen, s_in, acc):
            def rows(j):
                return pl.ds(base + j, NSEG, stride=seglen)

            def inner(k, acc):
                j = jnp.where(d == 0, k, seglen - 1 - k)
                jp = jnp.where(d == 0, k - 1, seglen - k)
                t = dlam_terms(get_g(rows(j)), get_s(rows(jp)))
                return acc[0] + t[0], acc[1] + t[1]

            acc = lax.fori_loop(1, seglen, inner, acc)
            jb = jnp.where(d == 0, 0, seglen - 1)
            jn = jnp.where(d == 0, seglen - 1, 0)
            sp = get_s(rows(jn))
            t = dlam_terms(get_g(rows(jb)), (_shift_rows(sp[0], d, s_in[0]), _shift_rows(sp[1], d, s_in[1])))
            return acc[0] + t[0], acc[1] + t[1]

        r_mid = jnp.where(d == 0, TA - 1, T)
        s_mid = tuple(jnp.broadcast_to(t, (NSEG, BW)) for t in get_s(pl.ds(r_mid, 1)))
        acc = dlam_region(T, CSEG, (zero, zero), (zero, zero))
        acc = dlam_region(0, TSEG, s_mid, acc)
        dlam_ref[0, 0, :, 0:BW] = acc[0]
        dlam_ref[0, 0, :, BW:2 * BW] = acc[1]

        bm = b_ref[0, 0].astype(BF16)
        db = jnp.zeros((UW, 2 * BW), F32)
        dc = jnp.zeros((2 * BW, UW), F32)
        for r0 in range(0, TA, RCH):
            rs = slice(r0, r0 + RCH)
            g = _state_rows(g_ref, (), rs).astype(BF16)
            dua_ref[rs, :] += lax.dot_general(g, bm, nt, preferred_element_type=F32)
            db = db + lax.dot_general(u_ref[rs, :].astype(BF16), g, tn, preferred_element_type=F32)
            if r0 < T:
                dc = dc + lax.dot_general(_state_rows(s_ref, (0, 0), rs).astype(BF16), dssm_ref[rs, :].astype(BF16), tn,
                                          preferred_element_type=F32)
        db_ref[0, 0] = db
        dc_ref[0, 0] = dc

        @pl.when(d == 1)
        def _():
            du_ref[...] = dua_ref[...].astype(du_ref.dtype)

    blk4 = lambda shape: pl.BlockSpec((1, 1) + shape, lambda b, d: (d, b, 0, 0))
    lat = pl.BlockSpec((T, UW), lambda b, d: (0, b))
    vec = pl.BlockSpec((1, UW), lambda b, d: (0, b))
    return pl.pallas_call(
        kern,
        name="s5_bwd",
        grid=(NBLK, 2),
        in_specs=[lat, lat, pl.BlockSpec((TA, UW), lambda b, d: (0, UCOL0 + b)), vec,
                  pl.BlockSpec((1, 1, NCG, TA, LANES_), lambda b, d: (d, b, 0, 0, 0)),
                  blk4((8, 2 * BW)), blk4((UW, 2 * BW)), blk4((2 * BW, UW))],
        out_specs=[pl.BlockSpec((TA, UW), lambda b, d: (0, b)), vec, blk4((8, 2 * BW)), blk4((UW, 2 * BW)), blk4((2 * BW, UW))],
        out_shape=[jax.ShapeDtypeStruct((TA, SW), BF16), jax.ShapeDtypeStruct((1, SW), F32),
                   jax.ShapeDtypeStruct((2, NBLK, 8, 2 * BW), F32),
                   jax.ShapeDtypeStruct((2, NBLK, UW, 2 * BW), F32), jax.ShapeDtypeStruct((2, NBLK, 2 * BW, UW), F32)],
        scratch_shapes=[pltpu.VMEM((NCG, TA, LANES_), F32), pltpu.VMEM((TA, UW), F32), pltpu.VMEM((T, UW), F32)],
        compiler_params=_cparams(("parallel", "arbitrary")),
    )(d_ge, ssm, proj, dskip, states, lam, bmat, cmat)


TR = 256


def _vjp_rows(f, primals, cots, n_row):
    _, pull = jax.vjp(f, *primals)
    g = pull(cots)
    return list(g[:n_row]), list(g[n_row:])


class _GradDict(dict):
    def __init__(self, on_set=None):
        super().__init__()
        self._on_set = on_set
        self.tokens = {}

    def __setitem__(self, key, value):
        super().__setitem__(key, value)
        if self._on_set is not None:
            self._on_set(self)

    def order(self, key):
        return self.tokens.get(key, self.get(key))


def _local_step(x, ctx, tgt, mod_lat, mod_ctx, wb, sp, on_grad=None):
    sh1, sc1, g1, sh2, sc2, g2 = [mod_lat[:, i * D:(i + 1) * D] for i in range(6)]
    csh1, csc1 = mod_ctx[:, 0:D], mod_ctx[:, D:2 * D]
    tabs = _rope_tables()
    sink = sp["attn_sink"].reshape(1, NH)
    dskip = sp["ssm_d"].reshape(1, SW)
    lg_mix, lb_mix = sp["ln_mix_g"].reshape(1, D), sp["ln_mix_b"].reshape(1, D)
    lg_mlp, lb_mlp = sp["ln_mlp_g"].reshape(1, D), sp["ln_mlp_b"].reshape(1, D)
    b1, b2 = sp["b_mlp1"].reshape(1, DFF), sp["b_mlp2"].reshape(1, D)
    s5_names = ("ssm_a_re", "ssm_a_im", "ssm_log_dt", "ssm_b_re", "ssm_b_im", "ssm_c_re", "ssm_c_im")
    (lam, bmat, cmat), s5_pull = jax.vjp(_s5_prep, *[sp[n] for n in s5_names])

    def ln_mod(rv, vv):
        return [_f_ln_mod(rv[0], vv[0], vv[1])], []

    h_lat, = _rowwise(ln_mod, [(x, D, 0, 0)], [sc1, sh1], [(D, BF16)], [], nrows=T, tr=TR, name="ln1_lat")
    h_ctx, = _rowwise(ln_mod, [(ctx, D, 0, 0)], [csc1, csh1], [(D, BF16)], [], nrows=C, tr=TR, name="ln1_ctx")
    h1 = jnp.concatenate([h_lat, h_ctx], 0)
    proj = _matmul(h1, wb["w_in"], mode="nn", name="proj", tm=768, tn=512)
    attn = _attn_fwd(proj, sink, tabs)
    states, ssm, ge = _s5_fwd(proj, dskip, lam, bmat, cmat)
    z = _matmul(ge, wb["w_glu"], mode="nn", name="glu_mm", tm=1024, tn=1024)

    def glu_act(rv, vv):
        return [_f_glu(rv[0])], []

    glu, = _rowwise(glu_act, [(z, 2 * SW, 0, 0)], [], [(SW, BF16)], [], nrows=T, tr=TR, name="glu_act")
    attn_d = _matmul(attn, wb["w_attn_up"], mode="nn", name="attn_up", tm=1024, tn=512)
    ssm_d = _matmul(glu, wb["w_ssm_up"], mode="nn", name="ssm_up", tm=1024, tn=512)
    ga_cb, gs_cb = (QW + 2 * KVW + SW) // D, (QW + 2 * KVW + SW) // D + 1

    def mix(rv, vv):
        return [_f_mix(*rv)], []

    mixv, = _rowwise(mix, [(proj, D, ga_cb, 0), (proj, D, gs_cb, 0), (attn_d, D, 0, 0), (ssm_d, D, 0, 0)], [],
                     [(D, BF16)], [], nrows=T, tr=TR, name="mix")
    y = _matmul(mixv, wb["w_out"], mode="nn", name="out_proj", tm=1024, tn=512)

    def post1(rv, vv):
        x1, h2 = _f_post1(rv[0], rv[1], *vv)
        return [x1, h2], []

    x1, h2 = _rowwise(post1, [(x, D, 0, 0), (y, D, 0, 0)], [g1, lg_mix, lb_mix, sc2, sh2],
                      [(D, F32), (D, BF16)], [], nrows=T, tr=TR, name="post1")

    def relu_sq(acc):
        r = jnp.maximum(acc, 0.0)
        return r, r * r

    r_act, act = _matmul(h2, wb["w_mlp1"], mode="nn", name="mlp1", tm=1024, tn=512, bias=b1,
                         out_dtypes=(BF16, BF16), epilogue=relu_sq)
    mlp = _matmul(act, wb["w_mlp2"], mode="nn", name="mlp2", tm=1024, tn=512, tk=2048)

    def loss_fb(rv, vv):
        x1_t, mlp_t, tgt_t = rv
        g2_v, lg_v, lb_v, b2_v = vv
        f = lambda a, m, g, p, q, b: _f_loss(a, m, tgt_t, g, p, q, b)
        val, grads = jax.value_and_grad(f, argnums=(0, 1, 2, 3, 4, 5))(x1_t, mlp_t, g2_v, lg_v, lb_v, b2_v)
        dx1, dmlp, dg2, dlg, dlb, db2 = grads
        return [dx1, dmlp], [jnp.reshape(val, (1, 1)), dg2, dlg, dlb, db2]

    dx1_a, d_mlp, loss_p, d_g2, d_lg_mlp, d_lb_mlp, d_b2 = _rowwise(
        loss_fb, [(x1, D, 0, 0), (mlp, D, 0, 0), (tgt, D, 0, 0)], [g2, lg_mlp, lb_mlp, b2],
        [(D, F32), (D, BF16)], [(1, 1), (1, D), (1, D), (1, D), (1, D)], nrows=T, tr=TR, name="loss_fb")

    gw = _GradDict(on_grad)
    gw["w_mlp2"] = _matmul(act, d_mlp, mode="tn", name="dw_mlp2", out_dtypes=(BF16,), tm=512, tn=1024, tk=1024)
    da, = (_matmul(d_mlp, wb["w_mlp2"], mode="nt", name="d_act", out_dtypes=(BF16,), tm=1024, tn=512,
                   extras=(r_act,), epilogue=lambda acc, r: (acc * (2.0 * r.astype(F32)),), after=(gw.order("w_mlp2"),)),)
    ones = jnp.ones((8, T), BF16)
    d_b1 = _matmul(ones, da, mode="nn", name="db_mlp1", tm=8, tn=2048)[0:1]
    gw["w_mlp1"] = _matmul(h2, da, mode="tn", name="dw_mlp1", out_dtypes=(BF16,), tm=512, tn=1024, tk=1024)
    dh2 = _matmul(da, wb["w_mlp1"], mode="nt", name="d_h2", tm=1024, tn=512, tk=2048, after=(gw.order("w_mlp1"),))

    def post1_b(rv, vv):
        x_t, y_t, dx1_t, dh2_t = rv
        gr, gv = _vjp_rows(_f_post1, (x_t, y_t, *vv), (dx1_t, dh2_t), 2)
        return [gr[0], gr[1]], gv

    dx_a, dy, d_g1, d_lg_mix, d_lb_mix, d_sc2, d_sh2 = _rowwise(
        post1_b, [(x, D, 0, 0), (y, D, 0, 0), (dx1_a, D, 0, 0), (dh2, D, 0, 0)], [g1, lg_mix, lb_mix, sc2, sh2],
        [(D, F32), (D, BF16)], [(1, D)] * 5, nrows=T, tr=TR, name="post1_bwd")
    gw["w_out"] = _matmul(mixv, dy, mode="tn", name="dw_out", out_dtypes=(BF16,), tm=512, tn=1024, tk=1024)
    dmix = _matmul(dy, wb["w_out"], mode="nt", name="d_mix", tm=1024, tn=512, after=(gw.order("w_out"),))

    def mix_b(rv, vv):
        gr, _ = _vjp_rows(_f_mix, tuple(rv[:4]), rv[4], 4)
        return gr, []

    d_ga, d_gs, d_attn_d, d_ssm_d = _rowwise(
        mix_b, [(proj, D, ga_cb, 0), (proj, D, gs_cb, 0), (attn_d, D, 0, 0), (ssm_d, D, 0, 0), (dmix, D, 0, 0)], [],
        [(D, BF16)] * 4, [], nrows=T, tr=TR, name="mix_bwd")
    gw["w_attn_up"] = _matmul(attn, d_attn_d, mode="tn", name="dw_attn_up", out_dtypes=(BF16,), tm=512, tn=1024, tk=1024)
    d_attn = _matmul(d_attn_d, wb["w_attn_up"], mode="nt", name="d_attn", out_dtypes=(BF16,), tm=1024, tn=512)
    gw["w_ssm_up"] = _matmul(glu, d_ssm_d, mode="tn", name="dw_ssm_up", out_dtypes=(BF16,), tm=512, tn=1024, tk=1024)
    d_glu = _matmul(d_ssm_d, wb["w_ssm_up"], mode="nt", name="d_glu", tm=1024, tn=512, after=(gw.order("w_attn_up"), gw.order("w_ssm_up")))

    def glu_b(rv, vv):
        gr, _ = _vjp_rows(_f_glu, (rv[0],), rv[1], 1)
        return gr, []

    dz, = _rowwise(glu_b, [(z, 2 * SW, 0, 0), (d_glu, SW, 0, 0)], [], [(2 * SW, BF16)], [], nrows=T, tr=TR, name="glu_bwd")
    gw["w_glu"] = _matmul(ge, dz, mode="tn", name="dw_glu", out_dtypes=(BF16,), tm=512, tn=1024, tk=1024)
    d_ge = _matmul(dz, wb["w_glu"], mode="nt", name="d_ge", tm=1024, tn=512, after=(gw.order("w_glu"),))

    du_all, d_dskip, dlam, dbmat, dcmat = _s5_bwd(d_ge, ssm, proj, dskip, states, lam, bmat, cmat)
    s5_grads = s5_pull((dlam, dbmat, dcmat))

    dq, dk, dv, dsink = _attn_bwd(proj, d_attn, sink, tabs)
    zc = lambda w: jnp.zeros((C, w), BF16)
    dproj = jnp.concatenate([
        jnp.concatenate([dq, zc(QW)], 0), dk, dv, du_all,
        jnp.concatenate([d_ga, zc(D)], 0), jnp.concatenate([d_gs, zc(D)], 0)], 1)
    gw["w_in"] = _matmul(h1, dproj, mode="tn", name="dw_in", out_dtypes=(BF16,), tm=512, tn=1536, tk=768)
    dh1 = _matmul(dproj, wb["w_in"], mode="nt", name="d_h1", tm=768, tn=512, tk=2048, after=(gw.order("w_in"),))

    def ln1_b(rv, vv):
        x_t, dh_t, dxa_t = rv
        gr, gv = _vjp_rows(_f_ln_mod, (x_t, vv[0], vv[1]), dh_t, 1)
        return [gr[0] + dxa_t], gv

    grad_x, d_sc1, d_sh1 = _rowwise(ln1_b, [(x, D, 0, 0), (dh1, D, 0, 0), (dx_a, D, 0, 0)], [sc1, sh1],
                                    [(D, F32)], [(1, D), (1, D)], nrows=T, tr=TR, name="ln1_lat_bwd")

    def ln1c_b(rv, vv):
        _, gv = _vjp_rows(_f_ln_mod, (rv[0], vv[0], vv[1]), rv[1], 1)
        return [], gv

    d_csc1, d_csh1 = _rowwise(ln1c_b, [(ctx, D, 0, 0), (dh1, D, 0, T // TR)], [csc1, csh1],
                              [], [(1, D), (1, D)], nrows=C, tr=TR, name="ln1_ctx_bwd")

    d_mod_lat = jnp.concatenate([d_sh1, d_sc1, d_g1, d_sh2, d_sc2, d_g2], 1)
    zv = jnp.zeros((1, D), F32)
    d_mod_ctx = jnp.concatenate([d_csh1, d_csc1, zv, zv, zv, zv], 1)
    gs = {n: g for n, g in zip(s5_names, s5_grads)}
    gs["attn_sink"] = dsink[:, 0]
    gs["ssm_d"] = d_dskip
    gs["ln_mix_g"], gs["ln_mix_b"] = d_lg_mix, d_lb_mix
    gs["ln_mlp_g"], gs["ln_mlp_b"] = d_lg_mlp, d_lb_mlp
    gs["b_mlp1"], gs["b_mlp2"] = d_b1, d_b2
    return loss_p, grad_x, d_mod_lat, d_mod_ctx, gw, gs


def _my_pos():
    return lax.axis_index("x"), lax.axis_index("y"), lax.axis_index("c")


def _flip(p, bit):
    return 1 - p if bit else p


def _peer(pos, k):
    x, y, c = pos
    return (_flip(x, (k >> 2) & 1), _flip(y, (k >> 1) & 1), _flip(c, k & 1))


def _lin(pos):
    return 4 * pos[0] + 2 * pos[1] + pos[2]


def _allgather_small(v, name):
    r, w = v.shape

    def body(v_ref, out_ref, send_sems, recv_sems, local_sem):
        me = _my_pos()
        mine = pltpu.make_async_copy(v_ref, out_ref.at[_lin(me)], local_sem)
        mine.start()
        sends = []
        for k in range(1, N_DEV):
            cp = pltpu.make_async_remote_copy(src_ref=v_ref, dst_ref=out_ref.at[_lin(me)], send_sem=send_sems.at[k - 1],
                                              recv_sem=recv_sems.at[k - 1], device_id=_peer(me, k), device_id_type=MESH)
            cp.start()
            sends.append(cp)
        for k in range(1, N_DEV):
            peer = _peer(me, k)
            pltpu.make_async_remote_copy(src_ref=v_ref, dst_ref=out_ref.at[_lin(peer)], send_sem=send_sems.at[k - 1],
                                         recv_sem=recv_sems.at[k - 1], device_id=peer, device_id_type=MESH).wait_recv()
        for cp in sends:
            cp.wait_send()
        mine.wait()

    return pl.pallas_call(
        body,
        name=name,
        out_shape=jax.ShapeDtypeStruct((N_DEV, r, w), v.dtype),
        in_specs=[pl.BlockSpec(memory_space=pltpu.VMEM)],
        out_specs=pl.BlockSpec(memory_space=pltpu.VMEM),
        scratch_shapes=[pltpu.SemaphoreType.DMA((N_DEV - 1,)), pltpu.SemaphoreType.DMA((N_DEV - 1,)), pltpu.SemaphoreType.DMA],
        compiler_params=pltpu.CompilerParams(vmem_limit_bytes=VMEM_LIMIT_BYTES),
    )(v)


def _block_of(ref, kind, idx, n):
    start = pl.multiple_of(idx * n, 128)
    if kind == "col":
        return ref.at[:, pl.ds(start, n)]
    return ref.at[pl.ds(start, n), :]


def _allgather_weights(shards, kinds):
    nt = len(shards)
    out_shape = []
    for s, kind in zip(shards, kinds):
        k, n = s.shape
        out_shape.append(jax.ShapeDtypeStruct((k, n * N_DEV) if kind == "col" else (k * N_DEV, n), s.dtype))

    def body(*refs):
        ins, outs = refs[:nt], refs[nt:2 * nt]
        send_sems, recv_sems, local_sems = refs[2 * nt:]
        x, y, c = _my_pos()
        me, sibling = (x, y, c), (x, y, 1 - c)
        chips = [(1 - x, y), (x, 1 - y), (1 - x, 1 - y)]

        def blk(t, pos):
            n = shards[t].shape[1] if kinds[t] == "col" else shards[t].shape[0]
            return _block_of(outs[t], kinds[t], _lin(pos), n)

        def copy(t, k, block, to, src=None):
            return pltpu.make_async_remote_copy(src_ref=blk(t, block) if src is None else src, dst_ref=blk(t, block),
                                                send_sem=send_sems.at[t, k], recv_sem=recv_sems.at[t, k],
                                                device_id=to, device_id_type=MESH)

        local, sends = [], []
        for t in range(nt):
            mine = pltpu.make_async_copy(ins[t], blk(t, me), local_sems.at[t])
            mine.start()
            local.append(mine)
            first = [copy(t, 0, me, sibling, src=ins[t])]
            first += [copy(t, 1 + j, me, (*chip, c), src=ins[t]) for j, chip in enumerate(chips)]
            for cp in first:
                cp.start()
            sends += first
        for t in range(nt):
            for j, chip in enumerate(chips):
                copy(t, 1 + j, (*chip, c), me).wait_recv()
                fwd = copy(t, 4 + j, (*chip, c), sibling)
                fwd.start()
                sends.append(fwd)
        for t in range(nt):
            copy(t, 0, sibling, me).wait_recv()
            for j, chip in enumerate(chips):
                copy(t, 4 + j, (*chip, 1 - c), me).wait_recv()
        for cp in sends:
            cp.wait_send()
        for cp in local:
            cp.wait()

    any_spec = pl.BlockSpec(memory_space=pl.ANY)
    return pl.pallas_call(
        body,
        name="allgather_weights",
        out_shape=out_shape,
        in_specs=[any_spec] * nt,
        out_specs=[any_spec] * nt,
        scratch_shapes=[pltpu.SemaphoreType.DMA((nt, N_DEV - 1)), pltpu.SemaphoreType.DMA((nt, N_DEV - 1)),
                        pltpu.SemaphoreType.DMA((nt,))],
    )(*shards)


def _handshake(peers):
    barrier = pltpu.get_barrier_semaphore()
    for peer in peers:
        pl.semaphore_signal(barrier, inc=1, device_id=peer, device_id_type=MESH)
    pl.semaphore_wait(barrier, len(peers))


def _allgather_weights_seq(shards, kinds, name, collective_id):
    nt = len(shards)
    hbm = pltpu.MemorySpace.HBM
    ins = [jax.new_ref(s, memory_space=hbm) for s in shards]
    outs = []
    for s, kind in zip(shards, kinds):
        k, n = s.shape
        shape = (k, n * N_DEV) if kind == "col" else (k * N_DEV, n)
        outs.append(jax.empty_ref(jax.ShapeDtypeStruct(shape, s.dtype), memory_space=hbm))

    @functools.partial(
        pl.kernel, mesh=plsc.ScalarSubcoreMesh(axis_name="seq", num_cores=1), name=name,
        scratch_types=(pltpu.SemaphoreType.DMA((nt, N_DEV - 1)), pltpu.SemaphoreType.DMA((nt, N_DEV - 1)),
                       pltpu.SemaphoreType.DMA((nt,))),
        compiler_params=pltpu.CompilerParams(collective_id=collective_id))
    def launch(send_sems, recv_sems, local_sems):
        x, y, c = _my_pos()
        me, sibling = (x, y, c), (x, y, 1 - c)
        chips = [(1 - x, y), (x, 1 - y), (1 - x, 1 - y)]
        _handshake([sibling] + [(*chip, c) for chip in chips])

        def blk(t, pos):
            n = shards[t].shape[1] if kinds[t] == "col" else shards[t].shape[0]
            return _block_of(outs[t], kinds[t], _lin(pos), n)

        def copy(t, k, block, to, src=None):
            return pltpu.make_async_remote_copy(src_ref=blk(t, block) if src is None else src, dst_ref=blk(t, block),
                                                send_sem=send_sems.at[t, k], recv_sem=recv_sems.at[t, k],
                                                device_id=to, device_id_type=MESH)

        local, sends = [], []
        for t in range(nt):
            mine = pltpu.make_async_copy(ins[t], blk(t, me), local_sems.at[t])
            mine.start()
            local.append(mine)
            first = [copy(t, 0, me, sibling, src=ins[t])]
            first += [copy(t, 1 + j, me, (*chip, c), src=ins[t]) for j, chip in enumerate(chips)]
            for cp in first:
                cp.start()
            sends += first
        for t in range(nt):
            for j, chip in enumerate(chips):
                copy(t, 1 + j, (*chip, c), me).wait_recv()
                fwd = copy(t, 4 + j, (*chip, c), sibling)
                fwd.start()
                sends.append(fwd)
        for t in range(nt):
            copy(t, 0, sibling, me).wait_recv()
            for j, chip in enumerate(chips):
                copy(t, 4 + j, (*chip, 1 - c), me).wait_recv()
        for cp in sends:
            cp.wait_send()
        for cp in local:
            cp.wait()

    launch()
    return [o[...] for o in outs]


def _allgather_small_seq(v, name, collective_id):
    hbm = pltpu.MemorySpace.HBM
    src = jax.new_ref(v, memory_space=hbm)
    out = jax.empty_ref(jax.ShapeDtypeStruct((N_DEV,) + v.shape, v.dtype), memory_space=hbm)

    @functools.partial(
        pl.kernel, mesh=plsc.ScalarSubcoreMesh(axis_name="seq", num_cores=1), name=name,
        scratch_types=(pltpu.SemaphoreType.DMA((N_DEV - 1,)), pltpu.SemaphoreType.DMA((N_DEV - 1,)), pltpu.SemaphoreType.DMA),
        compiler_params=pltpu.CompilerParams(collective_id=collective_id))
    def launch(send_sems, recv_sems, local_sem):
        me = _my_pos()
        _handshake([_peer(me, k) for k in range(1, N_DEV)])
        mine = pltpu.make_async_copy(src, out.at[_lin(me)], local_sem)
        mine.start()
        sends = []
        for k in range(1, N_DEV):
            cp = pltpu.make_async_remote_copy(src_ref=src, dst_ref=out.at[_lin(me)], send_sem=send_sems.at[k - 1],
                                              recv_sem=recv_sems.at[k - 1], device_id=_peer(me, k), device_id_type=MESH)
            cp.start()
            sends.append(cp)
        for k in range(1, N_DEV):
            peer = _peer(me, k)
            pltpu.make_async_remote_copy(src_ref=src, dst_ref=out.at[_lin(peer)], send_sem=send_sems.at[k - 1],
                                         recv_sem=recv_sems.at[k - 1], device_id=peer, device_id_type=MESH).wait_recv()
        for cp in sends:
            cp.wait_send()
        mine.wait()

    launch()
    return out[...]


def _scatter_grads_seq(grads, kinds, name, collective_id):
    nt = len(grads)
    hbm = pltpu.MemorySpace.HBM
    shard_shapes = []
    for g, kind in zip(grads, kinds):
        k, n = g.shape
        shard_shapes.append((k, n // N_DEV) if kind == "col" else (k // N_DEV, n))
    ins = [jax.new_ref(g, memory_space=hbm) for g in grads]
    outs = [jax.empty_ref(jax.ShapeDtypeStruct((N_DEV,) + s, g.dtype), memory_space=hbm) for s, g in zip(shard_shapes, grads)]

    @functools.partial(
        pl.kernel, mesh=plsc.ScalarSubcoreMesh(axis_name="seq", num_cores=1), name=name,
        scratch_types=(pltpu.SemaphoreType.DMA((nt, N_DEV - 1)), pltpu.SemaphoreType.DMA((nt, N_DEV - 1)),
                       pltpu.SemaphoreType.DMA((nt,))),
        compiler_params=pltpu.CompilerParams(collective_id=collective_id))
    def launch(send_sems, recv_sems, local_sems):
        me = _my_pos()
        _handshake([_peer(me, k) for k in range(1, N_DEV)])

        def blk(t, pos):
            n = shard_shapes[t][1] if kinds[t] == "col" else shard_shapes[t][0]
            return _block_of(ins[t], kinds[t], _lin(pos), n)

        local, sends = [], []
        for t in range(nt):
            cp = pltpu.make_async_copy(blk(t, me), outs[t].at[_lin(me)], local_sems.at[t])
            cp.start()
            local.append(cp)
            for k in range(1, N_DEV):
                peer = _peer(me, k)
                cp = pltpu.make_async_remote_copy(src_ref=blk(t, peer), dst_ref=outs[t].at[_lin(me)], send_sem=send_sems.at[t, k - 1],
                                                  recv_sem=recv_sems.at[t, k - 1], device_id=peer, device_id_type=MESH)
                cp.start()
                sends.append(cp)
        for t in range(nt):
            for k in range(1, N_DEV):
                peer = _peer(me, k)
                pltpu.make_async_remote_copy(src_ref=blk(t, me), dst_ref=outs[t].at[_lin(peer)], send_sem=send_sems.at[t, k - 1],
                                             recv_sem=recv_sems.at[t, k - 1], device_id=peer, device_id_type=MESH).wait_recv()
        for cp in sends:
            cp.wait_send()
        for cp in local:
            cp.wait()

    launch()
    return [o[...] for o in outs]


_HBM_SPEC = pl.BlockSpec(memory_space=pltpu.HBM)
_SEM_SPEC = pl.BlockSpec(memory_space=pltpu.SEMAPHORE)
_EFFECT = pltpu.SideEffectType.DATAFLOW_SIDE_EFFECTING


def _shard_shapes(grads, kinds):
    return [(g.shape[0], g.shape[1] // N_DEV) if kind == "col" else (g.shape[0] // N_DEV, g.shape[1]) for g, kind in zip(grads, kinds)]


def _scatter_copies(g_refs, land_refs, send_sems, recv_sems, kinds, shard_shapes):
    me = _my_pos()
    copies = []
    for t in range(len(g_refs)):
        n = shard_shapes[t][1] if kinds[t] == "col" else shard_shapes[t][0]
        for k in range(1, N_DEV):
            peer = _peer(me, k)
            copies.append(pltpu.make_async_remote_copy(
                src_ref=_block_of(g_refs[t], kinds[t], _lin(peer), n), dst_ref=land_refs[t].at[_lin(me)],
                send_sem=send_sems.at[t * (N_DEV - 1) + k - 1], recv_sem=recv_sems.at[t * (N_DEV - 1) + k - 1],
                device_id=peer, device_id_type=MESH))
    return copies


def _scatter_start(grads, kinds, name):
    nt = len(grads)
    shard_shapes = _shard_shapes(grads, kinds)

    def body(*refs):
        g_refs, land_refs = refs[:nt], refs[nt:2 * nt]
        send_sems, recv_sems = refs[2 * nt], refs[2 * nt + 1]
        token = refs[2 * nt + 2 + 2 * nt]
        local_sems = refs[-1]
        me = _my_pos()
        local = []
        for t in range(nt):
            n = shard_shapes[t][1] if kinds[t] == "col" else shard_shapes[t][0]
            cp = pltpu.make_async_copy(_block_of(g_refs[t], kinds[t], _lin(me), n), land_refs[t].at[_lin(me)], local_sems.at[t])
            cp.start()
            local.append(cp)
        token[...] = jnp.zeros_like(token)
        for cp in local:
            cp.wait()
        for cp in _scatter_copies(g_refs, land_refs, send_sems, recv_sems, kinds, shard_shapes):
            cp.start()

    lands = [pltpu.with_memory_space_constraint(lax.empty((N_DEV,) + s, g.dtype), pltpu.HBM) for s, g in zip(shard_shapes, grads)]
    sem_shape = pltpu.SemaphoreType.DMA((nt * (N_DEV - 1),))
    out = pl.pallas_call(
        body,
        name=name,
        out_shape=(sem_shape, sem_shape, *[pltpu.HBM(g.shape, g.dtype) for g in grads],
                   *[pltpu.HBM(l.shape, l.dtype) for l in lands], jax.ShapeDtypeStruct((8, 128), F32)),
        in_specs=[_HBM_SPEC] * (2 * nt),
        out_specs=(_SEM_SPEC, _SEM_SPEC, *[_HBM_SPEC] * (2 * nt), pl.BlockSpec(memory_space=pltpu.VMEM)),
        input_output_aliases={i: 2 + i for i in range(2 * nt)},
        scratch_shapes=[pltpu.SemaphoreType.DMA((nt,))],
        compiler_params=pltpu.CompilerParams(has_side_effects=_EFFECT),
    )(*[pltpu.with_memory_space_constraint(g, pltpu.HBM) for g in grads], *lands)
    return out[0], out[1], list(out[2:2 + nt]), list(out[2 + nt:2 + 2 * nt]), out[-1]


def _scatter_wait(send_sems, recv_sems, g_thru, land_thru, kinds, after, name):
    nt = len(g_thru)
    shard_shapes = _shard_shapes(g_thru, kinds)

    def body(*refs):
        g_refs, land_refs = refs[:nt], refs[nt:2 * nt]
        send_sems, recv_sems = refs[2 * nt], refs[2 * nt + 1]
        for cp in _scatter_copies(g_refs, land_refs, send_sems, recv_sems, kinds, shard_shapes):
            cp.wait_send()
            cp.wait_recv()

    out = pl.pallas_call(
        body,
        name=name,
        out_shape=tuple(pltpu.HBM(a.shape, a.dtype) for a in (*g_thru, *land_thru)),
        in_specs=[*[_HBM_SPEC] * (2 * nt), _SEM_SPEC, _SEM_SPEC, pl.BlockSpec(memory_space=pl.ANY)],
        out_specs=tuple([_HBM_SPEC] * (2 * nt)),
        input_output_aliases={i: i for i in range(2 * nt)},
        compiler_params=pltpu.CompilerParams(has_side_effects=_EFFECT),
    )(*g_thru, *land_thru, send_sems, recv_sems, after)
    return list(out[nt:])


def _scatter_grads(grads, kinds):
    nt = len(grads)
    shard_shapes = []
    for g, kind in zip(grads, kinds):
        k, n = g.shape
        shard_shapes.append((k, n // N_DEV) if kind == "col" else (k // N_DEV, n))

    def body(*refs):
        ins, outs = refs[:nt], refs[nt:2 * nt]
        send_sems, recv_sems, local_sems = refs[2 * nt:]
        me = _my_pos()

        def blk(t, pos):
            n = shard_shapes[t][1] if kinds[t] == "col" else shard_shapes[t][0]
            return _block_of(ins[t], kinds[t], _lin(pos), n)

        local, sends = [], []
        for t in range(nt):
            cp = pltpu.make_async_copy(blk(t, me), outs[t].at[_lin(me)], local_sems.at[t])
            cp.start()
            local.append(cp)
            for k in range(1, N_DEV):
                peer = _peer(me, k)
                cp = pltpu.make_async_remote_copy(src_ref=blk(t, peer), dst_ref=outs[t].at[_lin(me)], send_sem=send_sems.at[t, k - 1],
                                                  recv_sem=recv_sems.at[t, k - 1], device_id=peer, device_id_type=MESH)
                cp.start()
                sends.append(cp)
        for t in range(nt):
            for k in range(1, N_DEV):
                peer = _peer(me, k)
                pltpu.make_async_remote_copy(src_ref=blk(t, me), dst_ref=outs[t].at[_lin(peer)], send_sem=send_sems.at[t, k - 1],
                                             recv_sem=recv_sems.at[t, k - 1], device_id=peer, device_id_type=MESH).wait_recv()
        for cp in sends:
            cp.wait_send()
        for cp in local:
            cp.wait()

    any_spec = pl.BlockSpec(memory_space=pl.ANY)
    return pl.pallas_call(
        body,
        name="scatter_grads",
        out_shape=[jax.ShapeDtypeStruct((N_DEV,) + s, g.dtype) for s, g in zip(shard_shapes, grads)],
        in_specs=[any_spec] * nt,
        out_specs=[any_spec] * nt,
        scratch_shapes=[pltpu.SemaphoreType.DMA((nt, N_DEV - 1)), pltpu.SemaphoreType.DMA((nt, N_DEV - 1)),
                        pltpu.SemaphoreType.DMA((nt,))],
    )(*grads)


def _adam(g_slots, w, m, v, *, tr, name):
    ns, r, wd = g_slots.shape
    tr = min(tr, r)
    assert r % tr == 0, (name, r, tr)
    c1 = 1.0 - ADAM_B1 ** ADAM_STEP
    c2 = 1.0 - ADAM_B2 ** ADAM_STEP

    def kern(g_ref, w_ref, m_ref, v_ref, go_ref, d_ref, mo_ref, vo_ref):
        g = g_ref[0].astype(F32)
        for s in range(1, ns):
            g = g + g_ref[s].astype(F32)
        m_new = ADAM_B1 * m_ref[...] + (1.0 - ADAM_B1) * g
        v_new = ADAM_B2 * v_ref[...] + (1.0 - ADAM_B2) * (g * g)
        m_hat = m_new / c1
        v_hat = v_new / c2
        go_ref[...] = g
        d_ref[...] = -ADAM_LR * (m_hat / (jnp.sqrt(v_hat) + ADAM_EPS) + ADAM_WD * w_ref[...])
        mo_ref[...] = m_new
        vo_ref[...] = v_new

    tile = pl.BlockSpec((tr, wd), lambda i: (i, 0))
    return pl.pallas_call(
        kern,
        name=name,
        grid=(r // tr,),
        in_specs=[pl.BlockSpec((ns, tr, wd), lambda i: (0, i, 0)), tile, tile, tile],
        out_specs=[tile] * 4,
        out_shape=[jax.ShapeDtypeStruct((r, wd), F32)] * 4,
        compiler_params=_cparams(("parallel",)),
    )(g_slots, w, m, v)


SMALL = ("c_ctx", "b_ada", "attn_sink", "ssm_a_re", "ssm_a_im", "ssm_log_dt", "ssm_b_re", "ssm_b_im", "ssm_c_re", "ssm_c_im",
         "ssm_d", "ln_mix_g", "ln_mix_b", "b_mlp1", "b_mlp2", "ln_mlp_g", "ln_mlp_b")
BIG = ("w_in", "w_glu", "w_attn_up", "w_ssm_up", "w_out", "w_mlp1", "w_mlp2")
BIG_KIND = ("col", "col", "col", "col", "row", "col", "row")
AG_GROUPS = (("w_in",), ("w_glu", "w_attn_up", "w_ssm_up", "w_out"), ("w_mlp1",), ("w_mlp2",))
AG_COLLECTIVE_ID0 = 1
RS_GROUPS = (("w_mlp2",), ("w_mlp1",), ("w_out", "w_attn_up", "w_ssm_up", "w_glu"), ("w_in",))
RS_COLLECTIVE_ID0 = AG_COLLECTIVE_ID0 + len(AG_GROUPS)
SMALL_EARLY = ("ssm_a_re", "ssm_a_im", "ssm_log_dt", "ssm_b_re", "ssm_b_im", "ssm_c_re", "ssm_c_im", "ssm_d")
SMALL_LATE = tuple(n for n in SMALL if n not in SMALL_EARLY)
SMALL_COLLECTIVE_ID0 = RS_COLLECTIVE_ID0 + len(RS_GROUPS)
LANES = 128


def _pack(parts):
    rows = []
    for p in parts:
        flat = p.reshape(-1).astype(F32)
        pad = (-flat.shape[0]) % LANES
        rows.append(jnp.pad(flat, (0, pad)).reshape(-1, LANES))
    packed = jnp.concatenate(rows, 0)
    return jnp.pad(packed, ((0, (-packed.shape[0]) % 8), (0, 0)))


def _unpack(packed, shapes):
    out, r0 = [], 0
    for s in shapes:
        n = math.prod(s)
        nr = -(-n // LANES)
        out.append(packed[r0:r0 + nr].reshape(-1)[:n].reshape(s))
        r0 += nr
    return out


WEIGHTS = ("c_ctx", "w_ada", "b_ada", "w_in", "attn_sink", "ssm_a_re", "ssm_a_im", "ssm_log_dt", "ssm_b_re", "ssm_b_im",
           "ssm_c_re", "ssm_c_im", "ssm_d", "w_glu", "w_attn_up", "w_ssm_up", "w_out", "ln_mix_g", "ln_mix_b", "w_mlp1",
           "b_mlp1", "w_mlp2", "b_mlp2", "ln_mlp_g", "ln_mlp_b")
ADA_COLS = 6 * D // N_DEV


def _step(x, c, ctx, loss_target, p, m, v):
    me = _lin(_my_pos())
    x2, ctx2, tgt2 = x[0], ctx[0], loss_target[0]

    wb = {}
    for gi, group in enumerate(AG_GROUPS):
        full = _allgather_weights_seq([p[n][0].astype(BF16) for n in group], [BIG_KIND[BIG.index(n)] for n in group],
                                      "allgather_seq%d" % gi, AG_COLLECTIVE_ID0 + gi)
        wb.update(zip(group, full))

    c_all = _allgather_small(jnp.broadcast_to(c, (8, D)), "gather_c")[:, 0, :]
    cc = p["c_ctx"].reshape(1, D)
    s_in = jnp.concatenate([c_all, cc, jnp.zeros((7, D), F32)], 0)
    s_act, = _rowwise(lambda rv, vv: ([_silu(rv[0])], []), [(s_in, D, 0, 0)], [], [(D, F32)], [], nrows=16, tr=16, name="silu_c")
    b_mine = lax.dynamic_slice_in_dim(p["b_ada"], me * ADA_COLS, ADA_COLS, axis=1)
    mod_part = _matmul(s_act, p["w_ada"][0], mode="nn", name="ada_fwd", tm=16, tn=512, bias=b_mine)
    mod_all = _allgather_small(mod_part, "gather_mod")
    mod_lat = lax.dynamic_index_in_dim(mod_all, me, axis=1, keepdims=False).reshape(1, 6 * D)
    mod_ctx = mod_all[:, 8, :].reshape(1, 6 * D)

    sp = {n: p[n][0] for n in SMALL if n not in ("c_ctx", "b_ada")}
    started = {}

    def on_grad(gw):
        for gi, group in enumerate(RS_GROUPS):
            if gi not in started and all(n in gw for n in group):
                kinds = [BIG_KIND[BIG.index(n)] for n in group]
                send_sems, recv_sems, g_thru, land_thru, token = _scatter_start([gw[n] for n in group], kinds, "scatter_start%d" % gi)
                started[gi] = (send_sems, recv_sems, g_thru, land_thru, kinds)
                for n in group:
                    gw.tokens[n] = token

    loss_p, grad_x, d_mod_lat, d_mod_ctx, gw, gs = _local_step(x2, ctx2, tgt2, mod_lat, mod_ctx, wb, sp, on_grad)
    recv = {}
    for gi, group in enumerate(RS_GROUPS):
        send_sems, recv_sems, g_thru, land_thru, kinds = started[gi]
        recv.update(zip(group, _scatter_wait(send_sems, recv_sems, g_thru, land_thru, kinds, grad_x, "scatter_wait%d" % gi)))

    g_early = _allgather_small_seq(_pack([gs[n] for n in SMALL_EARLY]), "gather_small_early", SMALL_COLLECTIVE_ID0)

    dm = jnp.concatenate([d_mod_lat, d_mod_ctx, jnp.zeros((6, 6 * D), F32)], 0)
    dm_all = _allgather_small_seq(dm, "gather_dmod", SMALL_COLLECTIVE_ID0 + 1)
    dm2 = jnp.concatenate([dm_all[:, 0, :], dm_all[:, 1, :]], 0)
    dm2_mine = lax.dynamic_slice_in_dim(dm2, me * ADA_COLS, ADA_COLS, axis=1)
    s2 = jnp.concatenate([s_act[0:8], jnp.broadcast_to(s_act[8:9], (8, D))], 0)
    g_w_ada = _matmul(s2, dm2_mine, mode="tn", name="dw_ada", tm=512, tn=ADA_COLS)
    dsc_part = _matmul(dm2_mine[8:16], p["w_ada"][0], mode="nt", name="d_silu_cctx", tm=8, tn=512)

    def cctx_b(rv, vv):
        _, pull = jax.vjp(_silu, vv[0])
        return [], [pull(jnp.sum(rv[0], axis=0, keepdims=True))[0]]

    g_cctx, = _rowwise(cctx_b, [(dsc_part, D, 0, 0)], [cc], [], [(1, D)], nrows=8, tr=8, name="cctx_bwd")
    gs["c_ctx"] = g_cctx
    gs["b_ada"] = d_mod_lat + d_mod_ctx

    res = {}
    for n in BIG:
        res[n] = _adam(recv[n], p[n][0], m[n][0], v[n][0], tr=256, name="adam_" + n)
    res["w_ada"] = _adam(g_w_ada[None], p["w_ada"][0], m["w_ada"][0], v["w_ada"][0], tr=256, name="adam_w_ada")

    g_late = _allgather_small_seq(_pack([gs[n] for n in SMALL_LATE]), "gather_small_late", SMALL_COLLECTIVE_ID0 + 2)
    for names, g_pack, tag in ((SMALL_EARLY, g_early, "early"), (SMALL_LATE, g_late, "late")):
        sm = _adam(g_pack, _pack([p[n] for n in names]), _pack([m[n] for n in names]), _pack([v[n] for n in names]),
                   tr=g_pack.shape[1], name="adam_small_" + tag)
        shapes = [p[n].shape for n in names]
        for j, outs in enumerate(zip(*[_unpack(a, shapes) for a in sm])):
            res[names[j]] = outs

    loss = lax.psum(loss_p[0, 0], ("x", "y", "c"))
    outs = [loss, grad_x[None]]
    for j in range(4):
        outs += [res[n][j].reshape(p[n].shape) for n in WEIGHTS]
    return tuple(outs)


def kernel(x, c, ctx, c_ctx, w_ada, b_ada, w_in, attn_sink, ssm_a_re, ssm_a_im, ssm_log_dt, ssm_b_re, ssm_b_im, ssm_c_re, ssm_c_im, ssm_d, w_glu, w_attn_up, w_ssm_up, w_out, ln_mix_g, ln_mix_b, w_mlp1, b_mlp1, w_mlp2, b_mlp2, ln_mlp_g, ln_mlp_b, loss_target, m_c_ctx, m_w_ada, m_b_ada, m_w_in, m_attn_sink, m_ssm_a_re, m_ssm_a_im, m_ssm_log_dt, m_ssm_b_re, m_ssm_b_im, m_ssm_c_re, m_ssm_c_im, m_ssm_d, m_w_glu, m_w_attn_up, m_w_ssm_up, m_w_out, m_ln_mix_g, m_ln_mix_b, m_w_mlp1, m_b_mlp1, m_w_mlp2, m_b_mlp2, m_ln_mlp_g, m_ln_mlp_b, v_c_ctx, v_w_ada, v_b_ada, v_w_in, v_attn_sink, v_ssm_a_re, v_ssm_a_im, v_ssm_log_dt, v_ssm_b_re, v_ssm_b_im, v_ssm_c_re, v_ssm_c_im, v_ssm_d, v_w_glu, v_w_attn_up, v_w_ssm_up, v_w_out, v_ln_mix_g, v_ln_mix_b, v_w_mlp1, v_b_mlp1, v_w_mlp2, v_b_mlp2, v_ln_mlp_g, v_ln_mlp_b):
    given = dict(locals())
    p = {n: given[n] for n in WEIGHTS}
    m = {n: given["m_" + n] for n in WEIGHTS}
    v = {n: given["v_" + n] for n in WEIGHTS}
    return _step(x, c, ctx, loss_target, p, m, v)
```

```python
import functools
import math

import jax
import jax.numpy as jnp
from jax import lax
from jax.experimental import pallas as pl
from jax.experimental.pallas import tpu as pltpu
from jax.experimental.pallas import tpu_sc as plsc

F32 = jnp.float32
BF16 = jnp.bfloat16

N_DEV = 8
D = 2048
T = 2048
C = 256
TA = T + C
GRID_W = 64
HD = 128
NH = 8
NKV = 2
GROUP = NH // NKV
WINDOW = 128
QW = NH * HD
KVW = NKV * HD
SW = D // 4
SG = 16
NG = SW // SG
SP = 64
DFF = 4 * D
IN_COLS = QW + 2 * KVW + SW + 2 * D
ALPHA = 2.0 ** 0.25
LN_EPS = 1e-6
NEG_INF = -1e30
ROPE_BASE = 10000.0
ATT_SCALE = HD ** -0.5

NSEG = 8
GBLK = 8
NBLK = NG // GBLK
BW = GBLK * SP
UW = GBLK * SG

ADAM_LR = 0.001
ADAM_B1 = 0.9
ADAM_B2 = 0.999
ADAM_EPS = 1e-08
ADAM_WD = 0.01
ADAM_STEP = 10

VMEM_LIMIT_BYTES = 56 * 1024 * 1024
MESH = pl.DeviceIdType.MESH


def _cparams(sem=None):
    return pltpu.CompilerParams(dimension_semantics=sem, vmem_limit_bytes=VMEM_LIMIT_BYTES)


def _matmul(a, b, *, mode, name, out_dtypes=(F32,), tm=512, tn=512, tk=None, bias=None, extras=(), epilogue=None, after=()):
    if mode == "nn":
        (M, K), (K2, N) = a.shape, b.shape
    elif mode == "nt":
        (M, K), (N, K2) = a.shape, b.shape
    else:
        (K, M), (K2, N) = a.shape, b.shape
    assert K == K2, (name, a.shape, b.shape)
    tm, tn, tk = min(tm, M), min(tn, N), min(tk or K, K)
    assert M % tm == 0 and N % tn == 0 and K % tk == 0, (name, M, N, K, tm, tn, tk)
    nk = K // tk
    if mode == "tn":
        a_spec = pl.BlockSpec((tk, tm), lambda i, j, k: (k, i))
    else:
        a_spec = pl.BlockSpec((tm, tk), lambda i, j, k: (i, k))
    if mode == "nt":
        b_spec = pl.BlockSpec((tn, tk), lambda i, j, k: (j, k))
    else:
        b_spec = pl.BlockSpec((tk, tn), lambda i, j, k: (k, j))
    dims = {"nn": (((1,), (0,)), ((), ())), "nt": (((1,), (1,)), ((), ())), "tn": (((0,), (0,)), ((), ()))}[mode]
    in_specs = [a_spec, b_spec]
    operands = [a, b]
    if bias is not None:
        in_specs.append(pl.BlockSpec((1, tn), lambda i, j, k: (0, j)))
        operands.append(bias)
    for e in extras:
        in_specs.append(pl.BlockSpec((tm, tn), lambda i, j, k: (i, j)))
        operands.append(e)
    n_ex = len(extras)
    for t in after:
        in_specs.append(pl.BlockSpec(memory_space=pl.ANY))
        operands.append(t)
    n_after = len(after)
    n_out = len(out_dtypes)
    has_bias = bias is not None

    def kern(*refs):
        a_ref, b_ref = refs[0], refs[1]
        pos = 2
        bias_ref = None
        if has_bias:
            bias_ref = refs[pos]
            pos += 1
        ex_refs = refs[pos:pos + n_ex]
        pos += n_ex + n_after
        out_refs = refs[pos:pos + n_out]
        acc_ref = refs[pos + n_out] if nk > 1 else None

        def finish(r):
            if has_bias:
                r = r + bias_ref[...]
            outs = epilogue(r, *[e[...] for e in ex_refs]) if epilogue is not None else (r,)
            for o_ref, o in zip(out_refs, outs):
                o_ref[...] = o.astype(o_ref.dtype)

        part = lax.dot_general(a_ref[...].astype(BF16), b_ref[...].astype(BF16), dims, preferred_element_type=F32)
        if nk == 1:
            finish(part)
        else:
            k = pl.program_id(2)

            @pl.when(k == 0)
            def _():
                acc_ref[...] = part

            @pl.when(k > 0)
            def _():
                acc_ref[...] += part

            @pl.when(k == nk - 1)
            def _():
                finish(acc_ref[...])

    outs = pl.pallas_call(
        kern,
        name=name,
        grid=(M // tm, N // tn, nk),
        in_specs=in_specs,
        out_specs=[pl.BlockSpec((tm, tn), lambda i, j, k: (i, j)) for _ in out_dtypes],
        out_shape=[jax.ShapeDtypeStruct((M, N), dt) for dt in out_dtypes],
        scratch_shapes=[pltpu.VMEM((tm, tn), F32)] if nk > 1 else [],
        compiler_params=_cparams(("parallel", "parallel", "arbitrary")),
    )(*operands)
    return outs[0] if n_out == 1 else tuple(outs)


def _rowwise(fn, rows, vecs, outs, vec_outs, *, nrows, tr, name):
    n_rows, n_vecs, n_outs = len(rows), len(vecs), len(outs)
    in_specs = [pl.BlockSpec((tr, w), lambda i, cb=cb, ro=ro: (i + ro, cb)) for (_, w, cb, ro) in rows]
    in_specs += [pl.BlockSpec(v.shape, lambda i: (0, 0)) for v in vecs]
    out_specs = [pl.BlockSpec((tr, w), lambda i: (i, 0)) for (w, _) in outs]
    out_specs += [pl.BlockSpec(s, lambda i: (0, 0)) for s in vec_outs]
    out_shape = [jax.ShapeDtypeStruct((nrows, w), dt) for (w, dt) in outs]
    out_shape += [jax.ShapeDtypeStruct(s, F32) for s in vec_outs]

    def kern(*refs):
        rvals = [r[...] for r in refs[:n_rows]]
        vvals = [r[...] for r in refs[n_rows:n_rows + n_vecs]]
        o_refs = refs[n_rows + n_vecs:n_rows + n_vecs + n_outs]
        v_refs = refs[n_rows + n_vecs + n_outs:]
        ro, vo = fn(rvals, vvals)
        for r, val in zip(o_refs, ro):
            r[...] = val.astype(r.dtype)
        i = pl.program_id(0)
        for r, val in zip(v_refs, vo):
            @pl.when(i == 0)
            def _(r=r, val=val):
                r[...] = val.astype(F32)

            @pl.when(i > 0)
            def _(r=r, val=val):
                r[...] += val.astype(F32)

    res = pl.pallas_call(
        kern,
        name=name,
        grid=(nrows // tr,),
        in_specs=in_specs,
        out_specs=out_specs,
        out_shape=out_shape,
        compiler_params=_cparams(("arbitrary",)),
    )(*[r[0] for r in rows], *vecs)
    return list(res)


def _ln(x):
    mu = jnp.mean(x, axis=-1, keepdims=True)
    xc = x - mu
    var = jnp.mean(xc * xc, axis=-1, keepdims=True)
    return xc * lax.rsqrt(var + LN_EPS)


def _sigmoid(x):
    return 1.0 / (1.0 + jnp.exp(-x))


def _gelu(x):
    return 0.5 * x * (1.0 + jnp.tanh(math.sqrt(2.0 / math.pi) * (x + 0.044715 * (x * x * x))))


def _silu(x):
    return x * _sigmoid(x)


def _f_ln_mod(x, sc, sh):
    return _ln(x) * (1.0 + sc) + sh


def _f_glu(z):
    return z[:, :SW] * _sigmoid(z[:, SW:])


def _f_mix(ga, gs, attn_d, ssm_d):
    return _sigmoid(ga) * attn_d + _sigmoid(gs) * ssm_d


def _f_post1(x, y, g1, lg, lb, sc2, sh2):
    r1 = ALPHA * x + g1 * y
    x1 = _ln(r1) * lg + lb
    h2 = _ln(x1) * (1.0 + sc2) + sh2
    return x1, h2


def _f_loss(x1, mlp, tgt, g2, lg, lb, b2z):
    r2 = ALPHA * x1 + g2 * (mlp + b2z)
    out = _ln(r2) * lg + lb
    err = out - tgt
    return 0.5 * jnp.sum(err * err) * (1.0 / D)


def _rope_tables():
    rows = T // GRID_W
    row = jnp.repeat(jnp.arange(rows), GRID_W)
    col = jnp.tile(jnp.arange(GRID_W), rows)
    n_freq = HD // 4
    freqs = ROPE_BASE ** (-jnp.arange(n_freq, dtype=F32) / n_freq)
    ang_r = row.astype(F32)[:, None] * freqs
    ang_c = col.astype(F32)[:, None] * freqs
    ang = jnp.concatenate([ang_r, ang_r, ang_c, ang_c], -1)
    cos, sin = jnp.cos(ang), jnp.sin(ang)
    lo = (jnp.arange(HD) % (HD // 2)) < (HD // 4)
    sin_a = jnp.where(lo[None, :], -sin, 0.0)
    sin_b = jnp.where(lo[None, :], 0.0, sin)
    return cos, sin_a, sin_b


def _rope(x, cos, sa, sb):
    return x * cos + pltpu.roll(x, 96, 1) * sa + pltpu.roll(x, 32, 1) * sb


def _rope_t(dy, cos, sa, sb):
    return dy * cos + pltpu.roll(dy * sa, 32, 1) + pltpu.roll(dy * sb, 96, 1)


BAND = 3 * WINDOW
KPAD = T + 2 * WINDOW


def _attn_fill_kv(k_ref, v_ref, cos_ref, sa_ref, sb_ref, kp, vp, kc, vc):
    zeros = jnp.zeros((WINDOW, KVW), BF16)
    kp[0:WINDOW, :] = zeros
    kp[WINDOW + T:KPAD, :] = zeros
    vp[0:WINDOW, :] = zeros
    vp[WINDOW + T:KPAD, :] = zeros
    for hh in range(NKV):
        cs = slice(hh * HD, (hh + 1) * HD)
        for r0 in range(0, T, 512):
            rs = slice(r0, r0 + 512)
            kr = _rope(k_ref[rs, cs], cos_ref[rs, :], sa_ref[rs, :], sb_ref[rs, :])
            kp[WINDOW + r0:WINDOW + r0 + 512, cs] = kr.astype(BF16)
    vp[WINDOW:WINDOW + T, :] = v_ref[0:T, :].astype(BF16)
    kc[...] = k_ref[T:TA, :].astype(BF16)
    vc[...] = v_ref[T:TA, :].astype(BF16)


def _attn_scores(n, h, q_ref, cos_ref, sa_ref, sb_ref, sink_ref, kp, kc):
    kvh = h // GROUP
    r0 = pl.multiple_of(n * WINDOW, WINDOW)
    cos = cos_ref[pl.ds(r0, WINDOW), :]
    sa = sa_ref[pl.ds(r0, WINDOW), :]
    sb = sb_ref[pl.ds(r0, WINDOW), :]
    q_h = _rope(q_ref[:, h * HD:(h + 1) * HD], cos, sa, sb).astype(BF16)
    kb = kp[pl.ds(r0, BAND), kvh * HD:(kvh + 1) * HD]
    kcb = kc[:, kvh * HD:(kvh + 1) * HD]
    nt = (((1,), (1,)), ((), ()))
    s_loc = lax.dot_general(q_h, kb, nt, preferred_element_type=F32) * ATT_SCALE
    s_ctx = lax.dot_general(q_h, kcb, nt, preferred_element_type=F32) * ATT_SCALE
    row = lax.broadcasted_iota(jnp.int32, (WINDOW, BAND), 0)
    col = lax.broadcasted_iota(jnp.int32, (WINDOW, BAND), 1)
    rel = col - WINDOW - row
    kpos = r0 - WINDOW + col
    valid = (jnp.abs(rel) <= WINDOW) & (kpos >= 0) & (kpos < T)
    s_loc = jnp.where(valid, s_loc, NEG_INF)
    sk = sink_ref[0:1, h:h + 1]
    m = jnp.maximum(jnp.maximum(jnp.max(s_loc, -1, keepdims=True), jnp.max(s_ctx, -1, keepdims=True)), sk)
    e_loc = jnp.exp(s_loc - m)
    e_ctx = jnp.exp(s_ctx - m)
    e_sink = jnp.exp(sk - m)
    inv = 1.0 / (jnp.sum(e_loc, -1, keepdims=True) + jnp.sum(e_ctx, -1, keepdims=True) + e_sink)
    return q_h, r0, e_loc * inv, e_ctx * inv, e_sink * inv


def _attn_fwd(proj, sink, tabs):
    cos, sa, sb = tabs

    def kern(q_ref, k_ref, v_ref, cos_ref, sa_ref, sb_ref, sink_ref, o_ref, kp, vp, kc, vc):
        n = pl.program_id(0)

        @pl.when(n == 0)
        def _():
            _attn_fill_kv(k_ref, v_ref, cos_ref, sa_ref, sb_ref, kp, vp, kc, vc)

        for h in range(NH):
            kvh = h // GROUP
            _, r0, p_loc, p_ctx, _ = _attn_scores(n, h, q_ref, cos_ref, sa_ref, sb_ref, sink_ref, kp, kc)
            vb = vp[pl.ds(r0, BAND), kvh * HD:(kvh + 1) * HD]
            vcb = vc[:, kvh * HD:(kvh + 1) * HD]
            o = jnp.dot(p_loc.astype(BF16), vb, preferred_element_type=F32)
            o = o + jnp.dot(p_ctx.astype(BF16), vcb, preferred_element_type=F32)
            o_ref[:, h * HD:(h + 1) * HD] = o.astype(o_ref.dtype)

    full = lambda shape: pl.BlockSpec(shape, lambda n: (0, 0))
    return pl.pallas_call(
        kern,
        name="attn_fwd",
        grid=(T // WINDOW,),
        in_specs=[
            pl.BlockSpec((WINDOW, QW), lambda n: (n, 0)),
            pl.BlockSpec((TA, KVW), lambda n: (0, QW // KVW)),
            pl.BlockSpec((TA, KVW), lambda n: (0, QW // KVW + 1)),
            full((T, HD)), full((T, HD)), full((T, HD)), full((1, NH)),
        ],
        out_specs=pl.BlockSpec((WINDOW, QW), lambda n: (n, 0)),
        out_shape=jax.ShapeDtypeStruct((T, QW), BF16),
        scratch_shapes=[pltpu.VMEM((KPAD, KVW), BF16), pltpu.VMEM((KPAD, KVW), BF16),
                        pltpu.VMEM((C, KVW), BF16), pltpu.VMEM((C, KVW), BF16)],
        compiler_params=_cparams(("arbitrary",)),
    )(proj, proj, proj, cos, sa, sb, sink)


def _attn_bwd(proj, d_attn, sink, tabs):
    cos, sa, sb = tabs
    n_blocks = T // WINDOW

    def kern(q_ref, k_ref, v_ref, do_ref, cos_ref, sa_ref, sb_ref, sink_ref,
             dq_ref, dk_ref, dv_ref, dsink_ref, kp, vp, kc, vc, dkp, dvp, dkc, dvc):
        n = pl.program_id(0)

        @pl.when(n == 0)
        def _():
            _attn_fill_kv(k_ref, v_ref, cos_ref, sa_ref, sb_ref, kp, vp, kc, vc)
            dkp[...] = jnp.zeros_like(dkp)
            dvp[...] = jnp.zeros_like(dvp)
            dkc[...] = jnp.zeros_like(dkc)
            dvc[...] = jnp.zeros_like(dvc)
            dsink_ref[...] = jnp.zeros_like(dsink_ref)

        nt = (((1,), (1,)), ((), ()))
        tn = (((0,), (0,)), ((), ()))
        for h in range(NH):
            kvh = h // GROUP
            cs = slice(kvh * HD, (kvh + 1) * HD)
            q_h, r0, p_loc, p_ctx, p_sink = _attn_scores(n, h, q_ref, cos_ref, sa_ref, sb_ref, sink_ref, kp, kc)
            kb = kp[pl.ds(r0, BAND), cs]
            vb = vp[pl.ds(r0, BAND), cs]
            kcb = kc[:, cs]
            vcb = vc[:, cs]
            do_h = do_ref[:, h * HD:(h + 1) * HD]
            dp_loc = lax.dot_general(do_h, vb, nt, preferred_element_type=F32)
            dp_ctx = lax.dot_general(do_h, vcb, nt, preferred_element_type=F32)
            delta = jnp.sum(p_loc * dp_loc, -1, keepdims=True) + jnp.sum(p_ctx * dp_ctx, -1, keepdims=True)
            ds_loc = (p_loc * (dp_loc - delta) * ATT_SCALE).astype(BF16)
            ds_ctx = (p_ctx * (dp_ctx - delta) * ATT_SCALE).astype(BF16)
            dq = jnp.dot(ds_loc, kb, preferred_element_type=F32) + jnp.dot(ds_ctx, kcb, preferred_element_type=F32)
            cos = cos_ref[pl.ds(r0, WINDOW), :]
            sa_ = sa_ref[pl.ds(r0, WINDOW), :]
            sb_ = sb_ref[pl.ds(r0, WINDOW), :]
            dq_ref[:, h * HD:(h + 1) * HD] = _rope_t(dq, cos, sa_, sb_).astype(dq_ref.dtype)
            dkp[pl.ds(r0, BAND), cs] += lax.dot_general(ds_loc, q_h, tn, preferred_element_type=F32)
            dkc[:, cs] += lax.dot_general(ds_ctx, q_h, tn, preferred_element_type=F32)
            dvp[pl.ds(r0, BAND), cs] += lax.dot_general(p_loc.astype(BF16), do_h, tn, preferred_element_type=F32)
            dvc[:, cs] += lax.dot_general(p_ctx.astype(BF16), do_h, tn, preferred_element_type=F32)
            dsk = -jnp.sum(p_sink * delta, axis=0, keepdims=True)
            dsink_ref[h:h + 1, :] += jnp.broadcast_to(dsk, (1, HD))

        @pl.when(n == n_blocks - 1)
        def _():
            for hh in range(NKV):
                cs = slice(hh * HD, (hh + 1) * HD)
                for r0 in range(0, T, 512):
                    rs = slice(r0, r0 + 512)
                    g = dkp[WINDOW + r0:WINDOW + r0 + 512, cs]
                    dk_ref[rs, cs] = _rope_t(g, cos_ref[rs, :], sa_ref[rs, :], sb_ref[rs, :]).astype(dk_ref.dtype)
            dk_ref[T:TA, :] = dkc[...].astype(dk_ref.dtype)
            dv_ref[0:T, :] = dvp[WINDOW:WINDOW + T, :].astype(dv_ref.dtype)
            dv_ref[T:TA, :] = dvc[...].astype(dv_ref.dtype)

    full = lambda shape: pl.BlockSpec(shape, lambda n: (0, 0))
    return pl.pallas_call(
        kern,
        name="attn_bwd",
        grid=(n_blocks,),
        in_specs=[
            pl.BlockSpec((WINDOW, QW), lambda n: (n, 0)),
            pl.BlockSpec((TA, KVW), lambda n: (0, QW // KVW)),
            pl.BlockSpec((TA, KVW), lambda n: (0, QW // KVW + 1)),
            pl.BlockSpec((WINDOW, QW), lambda n: (n, 0)),
            full((T, HD)), full((T, HD)), full((T, HD)), full((1, NH)),
        ],
        out_specs=[pl.BlockSpec((WINDOW, QW), lambda n: (n, 0)), full((TA, KVW)), full((TA, KVW)), full((NH, HD))],
        out_shape=[jax.ShapeDtypeStruct((T, QW), BF16), jax.ShapeDtypeStruct((TA, KVW), BF16),
                   jax.ShapeDtypeStruct((TA, KVW), BF16), jax.ShapeDtypeStruct((NH, HD), F32)],
        scratch_shapes=[pltpu.VMEM((KPAD, KVW), BF16), pltpu.VMEM((KPAD, KVW), BF16),
                        pltpu.VMEM((C, KVW), BF16), pltpu.VMEM((C, KVW), BF16),
                        pltpu.VMEM((KPAD, KVW), F32), pltpu.VMEM((KPAD, KVW), F32),
                        pltpu.VMEM((C, KVW), F32), pltpu.VMEM((C, KVW), F32)],
        compiler_params=_cparams(("arbitrary",)),
    )(proj, proj, proj, d_attn, cos, sa, sb, sink)


def _s5_prep(a_re, a_im, log_dt, b_re, b_im, c_re, c_im):
    lam = lax.complex(a_re, a_im)
    dt = jnp.exp(log_dt)[..., None]
    lam_bar = jnp.exp(lam * dt)
    b_bar = ((lam_bar - 1.0) / lam)[..., None] * lax.complex(b_re, b_im)
    eye = jnp.eye(GBLK, dtype=F32)

    def lam_rows(v):
        return v.reshape(2, NBLK, 1, BW)

    lam_l = jnp.concatenate([lam_rows(jnp.real(lam_bar)), lam_rows(jnp.imag(lam_bar))], -1)
    lam_l = jnp.broadcast_to(lam_l, (2, NBLK, 8, 2 * BW))

    def b_blocks(v):
        v = v.reshape(2, NBLK, GBLK, SP, SG).transpose(0, 1, 2, 4, 3)
        return (v[:, :, :, :, None, :] * eye[None, None, :, None, :, None]).reshape(2, NBLK, UW, BW)

    bmat = jnp.concatenate([b_blocks(jnp.real(b_bar)), b_blocks(jnp.imag(b_bar))], -1)

    def c_blocks(v):
        v = v.reshape(2, NBLK, GBLK, SG, SP).transpose(0, 1, 2, 4, 3)
        return (v[:, :, :, :, None, :] * eye[None, None, :, None, :, None]).reshape(2, NBLK, BW, UW)

    cmat = jnp.concatenate([c_blocks(c_re), -c_blocks(c_im)], 2)
    return lam_l, bmat, cmat


def _cmul(ar, ai, br, bi):
    return ar * br - ai * bi, ar * bi + ai * br


def _shift_rows(x, rev, fill):
    r = lax.broadcasted_iota(jnp.int32, x.shape, 0)
    down = jnp.where(r == 0, fill, pltpu.roll(x, 1, 0))
    up = jnp.where(r == NSEG - 1, fill, pltpu.roll(x, NSEG - 1, 0))
    return jnp.where(rev == 0, down, up)


def _edge_row(x, rev):
    last = jnp.broadcast_to(x[NSEG - 1:NSEG, :], x.shape)
    first = jnp.broadcast_to(x[0:1, :], x.shape)
    return jnp.where(rev == 0, last, first)


def _seg_scan(get, put, base, seglen, lr, li, rev, cin):
    zero = jnp.zeros((NSEG, BW), F32)

    def rows(k):
        j = jnp.where(rev == 0, k, seglen - 1 - k)
        return pl.ds(pl.multiple_of(base + j * NSEG, NSEG), NSEG)

    def local(k, carry):
        sr, si, pr, pi = carry
        xr, xi = get(rows(k))
        tr, ti = _cmul(lr, li, sr, si)
        sr, si = tr + xr, ti + xi
        put(rows(k), sr, si)
        pr, pi = _cmul(lr, li, pr, pi)
        return sr, si, pr, pi

    er, ei, lpr, lpi = lax.fori_loop(0, seglen, local, (zero, zero, zero + 1.0, zero))
    cr, ci = _shift_rows(zero, rev, cin[0]), _shift_rows(zero, rev, cin[1])
    for _ in range(NSEG - 1):
        tr, ti = _cmul(lpr, lpi, cr, ci)
        cr, ci = _shift_rows(er + tr, rev, cin[0]), _shift_rows(ei + ti, rev, cin[1])

    def fix(k, carry):
        pr, pi = carry
        xr, xi = get(rows(k))
        tr, ti = _cmul(pr, pi, cr, ci)
        put(rows(k), xr + tr, xi + ti)
        return _cmul(lr, li, pr, pi)

    lax.fori_loop(0, seglen, fix, (lr, li))
    tr, ti = _cmul(lpr, lpi, cr, ci)
    return _edge_row(er + tr, rev), _edge_row(ei + ti, rev)


RCH = 256
CSEG = C // NSEG
TSEG = T // NSEG
UCOL0 = (QW + 2 * KVW) // UW


REGIONS = ((0, TSEG), (T, CSEG))


def _state_access(ref, lead=()):
    def get(rows):
        return ref[(*lead, rows, slice(0, BW))], ref[(*lead, rows, slice(BW, 2 * BW))]

    def put(rows, re, im):
        ref[(*lead, rows, slice(0, BW))] = re
        ref[(*lead, rows, slice(BW, 2 * BW))] = im

    return get, put


def _interleave_rows(src_ref, dst_ref, regions=REGIONS):
    for base, seglen in regions:
        def body(j, carry, base=base, seglen=seglen):
            dst_ref[pl.ds(pl.multiple_of(base + j * NSEG, NSEG), NSEG), :] = src_ref[pl.ds(base + j, NSEG, stride=seglen), :]
            return carry

        lax.fori_loop(0, seglen, body, 0, unroll=8)


def _deinterleave_rows(src_ref, dst_ref, regions=REGIONS):
    for base, seglen in regions:
        def body(j, carry, base=base, seglen=seglen):
            dst_ref[pl.ds(base + j, NSEG, stride=seglen), :] = src_ref[pl.ds(pl.multiple_of(base + j * NSEG, NSEG), NSEG), :]
            return carry

        lax.fori_loop(0, seglen, body, 0, unroll=8)


def _s5_fwd(proj, dskip, lam, bmat, cmat):
    def kern(u_ref, dk_ref, lam_ref, b_ref, c_ref, s_ref, ssm_ref, ge_ref, up_ref, yp_ref):
        d = pl.program_id(1)

        @pl.when(d == 0)
        def _():
            _interleave_rows(u_ref, up_ref)

        bm = b_ref[0, 0].astype(BF16)
        for r0 in range(0, TA, RCH):
            s_ref[0, 0, r0:r0 + RCH, :] = jnp.dot(up_ref[r0:r0 + RCH, :].astype(BF16), bm, preferred_element_type=F32)
        lr = lam_ref[0, 0, :, 0:BW]
        li = lam_ref[0, 0, :, BW:2 * BW]
        zero = jnp.zeros((NSEG, BW), F32)
        get, put = _state_access(s_ref, (0, 0))
        mid = _seg_scan(get, put, T, CSEG, lr, li, d, (zero, zero))
        _seg_scan(get, put, 0, TSEG, lr, li, d, mid)
        cm = c_ref[0, 0].astype(BF16)
        for r0 in range(0, T, RCH):
            y = jnp.dot(s_ref[0, 0, r0:r0 + RCH, :].astype(BF16), cm, preferred_element_type=F32)

            @pl.when(d == 0)
            def _(y=y, r0=r0):
                yp_ref[r0:r0 + RCH, :] = y + dk_ref[...] * up_ref[r0:r0 + RCH, :]

            @pl.when(d == 1)
            def _(y=y, r0=r0):
                yp_ref[r0:r0 + RCH, :] += y

        @pl.when(d == 1)
        def _():
            _deinterleave_rows(yp_ref, ssm_ref, REGIONS[:1])
            for r0 in range(0, T, RCH):
                ge_ref[r0:r0 + RCH, :] = _gelu(ssm_ref[r0:r0 + RCH, :]).astype(ge_ref.dtype)

    blk4 = lambda shape: pl.BlockSpec((1, 1) + shape, lambda b, d: (d, b, 0, 0))
    return pl.pallas_call(
        kern,
        name="s5_fwd",
        grid=(NBLK, 2),
        in_specs=[pl.BlockSpec((TA, UW), lambda b, d: (0, UCOL0 + b)), pl.BlockSpec((1, UW), lambda b, d: (0, b)),
                  blk4((8, 2 * BW)), blk4((UW, 2 * BW)), blk4((2 * BW, UW))],
        out_specs=[blk4((TA, 2 * BW)), pl.BlockSpec((T, UW), lambda b, d: (0, b)), pl.BlockSpec((T, UW), lambda b, d: (0, b))],
        out_shape=[jax.ShapeDtypeStruct((2, NBLK, TA, 2 * BW), F32), jax.ShapeDtypeStruct((T, SW), F32),
                   jax.ShapeDtypeStruct((T, SW), BF16)],
        scratch_shapes=[pltpu.VMEM((TA, UW), F32), pltpu.VMEM((T, UW), F32)],
        compiler_params=_cparams(("parallel", "arbitrary")),
    )(proj, dskip, lam, bmat, cmat)


def _s5_bwd(d_ge, ssm, proj, dskip, states, lam, bmat, cmat):
    nt = (((1,), (1,)), ((), ()))
    tn = (((0,), (0,)), ((), ()))

    def kern(dge_ref, ssm_ref, u_ref, dk_ref, s_ref, lam_ref, b_ref, c_ref,
             du_ref, ddk_ref, dlam_ref, db_ref, dc_ref, g_ref, dua_ref, dssm_ref, up_ref, nat_ref):
        d = pl.program_id(1)

        @pl.when(d == 0)
        def _():
            ddk = jnp.zeros((1, UW), F32)
            for r0 in range(0, T, RCH):
                rs = slice(r0, r0 + RCH)
                _, pull = jax.vjp(_gelu, ssm_ref[rs, :])
                dssm = pull(dge_ref[rs, :])[0]
                nat_ref[rs, :] = dssm
                ddk = ddk + jnp.sum(dssm * u_ref[rs, :], axis=0, keepdims=True)
            ddk_ref[...] = ddk
            _interleave_rows(nat_ref, dssm_ref, REGIONS[:1])
            _interleave_rows(u_ref, up_ref)
            for r0 in range(0, T, RCH):
                dua_ref[r0:r0 + RCH, :] = dssm_ref[r0:r0 + RCH, :] * dk_ref[...]
            dua_ref[T:TA, :] = jnp.zeros((C, UW), F32)

        cm = c_ref[0, 0].astype(BF16)
        for r0 in range(0, T, RCH):
            g_ref[r0:r0 + RCH, :] = lax.dot_general(dssm_ref[r0:r0 + RCH, :].astype(BF16), cm, nt, preferred_element_type=F32)
        g_ref[T:TA, :] = jnp.zeros((C, 2 * BW), F32)
        lr = lam_ref[0, 0, :, 0:BW]
        li = lam_ref[0, 0, :, BW:2 * BW]
        zero = jnp.zeros((NSEG, BW), F32)
        get_g, put_g = _state_access(g_ref)

        mid = _seg_scan(get_g, put_g, 0, TSEG, lr, -li, 1 - d, (zero, zero))
        _seg_scan(get_g, put_g, T, CSEG, lr, -li, 1 - d, mid)

        get_s, _ = _state_access(s_ref, (0, 0))

        def dlam_terms(g, s):
            return g[0] * s[0] + g[1] * s[1], g[1] * s[0] - g[0] * s[1]

        def dlam_region(base, seglen, s_in, acc):
            def rows(j):
                return pl.ds(pl.multiple_of(base + j * NSEG, NSEG), NSEG)

            def inner(k, acc):
                j = jnp.where(d == 0, k, seglen - 1 - k)
                jp = jnp.where(d == 0, k - 1, seglen - k)
                t = dlam_terms(get_g(rows(j)), get_s(rows(jp)))
                return acc[0] + t[0], acc[1] + t[1]

            acc = lax.fori_loop(1, seglen, inner, acc)
            jb = jnp.where(d == 0, 0, seglen - 1)
            jn = jnp.where(d == 0, seglen - 1, 0)
            sp = get_s(rows(jn))
            t = dlam_terms(get_g(rows(jb)), (_shift_rows(sp[0], d, s_in[0]), _shift_rows(sp[1], d, s_in[1])))
            return acc[0] + t[0], acc[1] + t[1]

        r_mid = jnp.where(d == 0, TA - 1, T)
        s_mid = tuple(jnp.broadcast_to(t, (NSEG, BW)) for t in get_s(pl.ds(r_mid, 1)))
        acc = dlam_region(T, CSEG, (zero, zero), (zero, zero))
        acc = dlam_region(0, TSEG, s_mid, acc)
        dlam_ref[0, 0, :, 0:BW] = acc[0]
        dlam_ref[0, 0, :, BW:2 * BW] = acc[1]

        bm = b_ref[0, 0].astype(BF16)
        db = jnp.zeros((UW, 2 * BW), F32)
        dc = jnp.zeros((2 * BW, UW), F32)
        for r0 in range(0, TA, RCH):
            rs = slice(r0, r0 + RCH)
            g = g_ref[rs, :].astype(BF16)
            dua_ref[rs, :] += lax.dot_general(g, bm, nt, preferred_element_type=F32)
            db = db + lax.dot_general(up_ref[rs, :].astype(BF16), g, tn, preferred_element_type=F32)
            if r0 < T:
                dc = dc + lax.dot_general(s_ref[0, 0, rs, :].astype(BF16), dssm_ref[rs, :].astype(BF16), tn,
                                          preferred_element_type=F32)
        db_ref[0, 0] = db
        dc_ref[0, 0] = dc

        @pl.when(d == 1)
        def _():
            _deinterleave_rows(dua_ref, nat_ref)
            du_ref[...] = nat_ref[...].astype(du_ref.dtype)

    blk4 = lambda shape: pl.BlockSpec((1, 1) + shape, lambda b, d: (d, b, 0, 0))
    lat = pl.BlockSpec((T, UW), lambda b, d: (0, b))
    vec = pl.BlockSpec((1, UW), lambda b, d: (0, b))
    return pl.pallas_call(
        kern,
        name="s5_bwd",
        grid=(NBLK, 2),
        in_specs=[lat, lat, pl.BlockSpec((TA, UW), lambda b, d: (0, UCOL0 + b)), vec,
                  blk4((TA, 2 * BW)), blk4((8, 2 * BW)), blk4((UW, 2 * BW)), blk4((2 * BW, UW))],
        out_specs=[pl.BlockSpec((TA, UW), lambda b, d: (0, b)), vec, blk4((8, 2 * BW)), blk4((UW, 2 * BW)), blk4((2 * BW, UW))],
        out_shape=[jax.ShapeDtypeStruct((TA, SW), BF16), jax.ShapeDtypeStruct((1, SW), F32),
                   jax.ShapeDtypeStruct((2, NBLK, 8, 2 * BW), F32),
                   jax.ShapeDtypeStruct((2, NBLK, UW, 2 * BW), F32), jax.ShapeDtypeStruct((2, NBLK, 2 * BW, UW), F32)],
        scratch_shapes=[pltpu.VMEM((TA, 2 * BW), F32), pltpu.VMEM((TA, UW), F32), pltpu.VMEM((T, UW), F32),
                        pltpu.VMEM((TA, UW), F32), pltpu.VMEM((TA, UW), F32)],
        compiler_params=_cparams(("parallel", "arbitrary")),
    )(d_ge, ssm, proj, dskip, states, lam, bmat, cmat)


TR = 256


def _vjp_rows(f, primals, cots, n_row):
    _, pull = jax.vjp(f, *primals)
    g = pull(cots)
    return list(g[:n_row]), list(g[n_row:])


class _GradDict(dict):
    def __init__(self, on_set=None):
        super().__init__()
        self._on_set = on_set
        self.tokens = {}

    def __setitem__(self, key, value):
        super().__setitem__(key, value)
        if self._on_set is not None:
            self._on_set(self)

    def order(self, key):
        return self.tokens.get(key, self.get(key))


def _local_step(x, ctx, tgt, mod_lat, mod_ctx, wb, sp, on_grad=None, on_loss=None):
    sh1, sc1, g1, sh2, sc2, g2 = [mod_lat[:, i * D:(i + 1) * D] for i in range(6)]
    csh1, csc1 = mod_ctx[:, 0:D], mod_ctx[:, D:2 * D]
    tabs = _rope_tables()
    sink = sp["attn_sink"].reshape(1, NH)
    dskip = sp["ssm_d"].reshape(1, SW)
    lg_mix, lb_mix = sp["ln_mix_g"].reshape(1, D), sp["ln_mix_b"].reshape(1, D)
    lg_mlp, lb_mlp = sp["ln_mlp_g"].reshape(1, D), sp["ln_mlp_b"].reshape(1, D)
    b1, b2 = sp["b_mlp1"].reshape(1, DFF), sp["b_mlp2"].reshape(1, D)
    s5_names = ("ssm_a_re", "ssm_a_im", "ssm_log_dt", "ssm_b_re", "ssm_b_im", "ssm_c_re", "ssm_c_im")
    (lam, bmat, cmat), s5_pull = jax.vjp(_s5_prep, *[sp[n] for n in s5_names])

    def ln_mod(rv, vv):
        return [_f_ln_mod(rv[0], vv[0], vv[1])], []

    h_lat, = _rowwise(ln_mod, [(x, D, 0, 0)], [sc1, sh1], [(D, BF16)], [], nrows=T, tr=TR, name="ln1_lat")
    h_ctx, = _rowwise(ln_mod, [(ctx, D, 0, 0)], [csc1, csh1], [(D, BF16)], [], nrows=C, tr=TR, name="ln1_ctx")
    h1 = jnp.concatenate([h_lat, h_ctx], 0)
    proj = _matmul(h1, wb["w_in"], mode="nn", name="proj", tm=768, tn=512)
    attn = _attn_fwd(proj, sink, tabs)
    states, ssm, ge = _s5_fwd(proj, dskip, lam, bmat, cmat)
    z = _matmul(ge, wb["w_glu"], mode="nn", name="glu_mm", tm=1024, tn=1024)

    def glu_act(rv, vv):
        return [_f_glu(rv[0])], []

    glu, = _rowwise(glu_act, [(z, 2 * SW, 0, 0)], [], [(SW, BF16)], [], nrows=T, tr=TR, name="glu_act")
    attn_d = _matmul(attn, wb["w_attn_up"], mode="nn", name="attn_up", tm=1024, tn=512)
    ssm_d = _matmul(glu, wb["w_ssm_up"], mode="nn", name="ssm_up", tm=1024, tn=512)
    ga_cb, gs_cb = (QW + 2 * KVW + SW) // D, (QW + 2 * KVW + SW) // D + 1

    def mix(rv, vv):
        return [_f_mix(*rv)], []

    mixv, = _rowwise(mix, [(proj, D, ga_cb, 0), (proj, D, gs_cb, 0), (attn_d, D, 0, 0), (ssm_d, D, 0, 0)], [],
                     [(D, BF16)], [], nrows=T, tr=TR, name="mix")
    y = _matmul(mixv, wb["w_out"], mode="nn", name="out_proj", tm=1024, tn=512)

    def post1(rv, vv):
        x1, h2 = _f_post1(rv[0], rv[1], *vv)
        return [x1, h2], []

    x1, h2 = _rowwise(post1, [(x, D, 0, 0), (y, D, 0, 0)], [g1, lg_mix, lb_mix, sc2, sh2],
                      [(D, F32), (D, BF16)], [], nrows=T, tr=TR, name="post1")

    def relu_sq(acc):
        r = jnp.maximum(acc, 0.0)
        return r, r * r

    r_act, act = _matmul(h2, wb["w_mlp1"], mode="nn", name="mlp1", tm=1024, tn=512, bias=b1,
                         out_dtypes=(BF16, BF16), epilogue=relu_sq)
    mlp = _matmul(act, wb["w_mlp2"], mode="nn", name="mlp2", tm=1024, tn=512, tk=2048)

    def loss_fb(rv, vv):
        x1_t, mlp_t, tgt_t = rv
        g2_v, lg_v, lb_v, b2_v = vv
        f = lambda a, m, g, p, q, b: _f_loss(a, m, tgt_t, g, p, q, b)
        val, grads = jax.value_and_grad(f, argnums=(0, 1, 2, 3, 4, 5))(x1_t, mlp_t, g2_v, lg_v, lb_v, b2_v)
        dx1, dmlp, dg2, dlg, dlb, db2 = grads
        return [dx1, dmlp], [jnp.reshape(val, (1, 1)), dg2, dlg, dlb, db2]

    dx1_a, d_mlp, loss_p, d_g2, d_lg_mlp, d_lb_mlp, d_b2 = _rowwise(
        loss_fb, [(x1, D, 0, 0), (mlp, D, 0, 0), (tgt, D, 0, 0)], [g2, lg_mlp, lb_mlp, b2],
        [(D, F32), (D, BF16)], [(1, 1), (1, D), (1, D), (1, D), (1, D)], nrows=T, tr=TR, name="loss_fb")

    gw = _GradDict(on_grad)
    loss_done = () if on_loss is None else (on_loss(loss_p),)
    gw["w_mlp2"] = _matmul(act, d_mlp, mode="tn", name="dw_mlp2", out_dtypes=(BF16,), tm=512, tn=1024, tk=1024, after=loss_done)
    da, = (_matmul(d_mlp, wb["w_mlp2"], mode="nt", name="d_act", out_dtypes=(BF16,), tm=1024, tn=512,
                   extras=(r_act,), epilogue=lambda acc, r: (acc * (2.0 * r.astype(F32)),), after=(gw.order("w_mlp2"),)),)
    ones = jnp.ones((8, T), BF16)
    d_b1 = _matmul(ones, da, mode="nn", name="db_mlp1", tm=8, tn=2048)[0:1]
    gw["w_mlp1"] = _matmul(h2, da, mode="tn", name="dw_mlp1", out_dtypes=(BF16,), tm=512, tn=1024, tk=1024)
    dh2 = _matmul(da, wb["w_mlp1"], mode="nt", name="d_h2", tm=1024, tn=512, tk=2048, after=(gw.order("w_mlp1"),))

    def post1_b(rv, vv):
        x_t, y_t, dx1_t, dh2_t = rv
        gr, gv = _vjp_rows(_f_post1, (x_t, y_t, *vv), (dx1_t, dh2_t), 2)
        return [gr[0], gr[1]], gv

    dx_a, dy, d_g1, d_lg_mix, d_lb_mix, d_sc2, d_sh2 = _rowwise(
        post1_b, [(x, D, 0, 0), (y, D, 0, 0), (dx1_a, D, 0, 0), (dh2, D, 0, 0)], [g1, lg_mix, lb_mix, sc2, sh2],
        [(D, F32), (D, BF16)], [(1, D)] * 5, nrows=T, tr=TR, name="post1_bwd")
    gw["w_out"] = _matmul(mixv, dy, mode="tn", name="dw_out", out_dtypes=(BF16,), tm=512, tn=1024, tk=1024)
    dmix = _matmul(dy, wb["w_out"], mode="nt", name="d_mix", tm=1024, tn=512, after=(gw.order("w_out"),))

    def mix_b(rv, vv):
        gr, _ = _vjp_rows(_f_mix, tuple(rv[:4]), rv[4], 4)
        return gr, []

    d_ga, d_gs, d_attn_d, d_ssm_d = _rowwise(
        mix_b, [(proj, D, ga_cb, 0), (proj, D, gs_cb, 0), (attn_d, D, 0, 0), (ssm_d, D, 0, 0), (dmix, D, 0, 0)], [],
        [(D, BF16)] * 4, [], nrows=T, tr=TR, name="mix_bwd")
    gw["w_attn_up"] = _matmul(attn, d_attn_d, mode="tn", name="dw_attn_up", out_dtypes=(BF16,), tm=512, tn=1024, tk=1024)
    d_attn = _matmul(d_attn_d, wb["w_attn_up"], mode="nt", name="d_attn", out_dtypes=(BF16,), tm=1024, tn=512)
    gw["w_ssm_up"] = _matmul(glu, d_ssm_d, mode="tn", name="dw_ssm_up", out_dtypes=(BF16,), tm=512, tn=1024, tk=1024)
    d_glu = _matmul(d_ssm_d, wb["w_ssm_up"], mode="nt", name="d_glu", tm=1024, tn=512, after=(gw.order("w_attn_up"), gw.order("w_ssm_up")))

    def glu_b(rv, vv):
        gr, _ = _vjp_rows(_f_glu, (rv[0],), rv[1], 1)
        return gr, []

    dz, = _rowwise(glu_b, [(z, 2 * SW, 0, 0), (d_glu, SW, 0, 0)], [], [(2 * SW, BF16)], [], nrows=T, tr=TR, name="glu_bwd")
    gw["w_glu"] = _matmul(ge, dz, mode="tn", name="dw_glu", out_dtypes=(BF16,), tm=512, tn=1024, tk=1024)
    d_ge = _matmul(dz, wb["w_glu"], mode="nt", name="d_ge", tm=1024, tn=512, after=(gw.order("w_glu"),))

    du_all, d_dskip, dlam, dbmat, dcmat = _s5_bwd(d_ge, ssm, proj, dskip, states, lam, bmat, cmat)
    s5_grads = s5_pull((dlam, dbmat, dcmat))

    dq, dk, dv, dsink = _attn_bwd(proj, d_attn, sink, tabs)
    zc = lambda w: jnp.zeros((C, w), BF16)
    dproj = jnp.concatenate([
        jnp.concatenate([dq, zc(QW)], 0), dk, dv, du_all,
        jnp.concatenate([d_ga, zc(D)], 0), jnp.concatenate([d_gs, zc(D)], 0)], 1)
    gw["w_in"] = _matmul(h1, dproj, mode="tn", name="dw_in", out_dtypes=(BF16,), tm=512, tn=1536, tk=768)
    dh1 = _matmul(dproj, wb["w_in"], mode="nt", name="d_h1", tm=768, tn=512, tk=2048, after=(gw.order("w_in"),))

    def ln1_b(rv, vv):
        x_t, dh_t, dxa_t = rv
        gr, gv = _vjp_rows(_f_ln_mod, (x_t, vv[0], vv[1]), dh_t, 1)
        return [gr[0] + dxa_t], gv

    grad_x, d_sc1, d_sh1 = _rowwise(ln1_b, [(x, D, 0, 0), (dh1, D, 0, 0), (dx_a, D, 0, 0)], [sc1, sh1],
                                    [(D, F32)], [(1, D), (1, D)], nrows=T, tr=TR, name="ln1_lat_bwd")

    def ln1c_b(rv, vv):
        _, gv = _vjp_rows(_f_ln_mod, (rv[0], vv[0], vv[1]), rv[1], 1)
        return [], gv

    d_csc1, d_csh1 = _rowwise(ln1c_b, [(ctx, D, 0, 0), (dh1, D, 0, T // TR)], [csc1, csh1],
                              [], [(1, D), (1, D)], nrows=C, tr=TR, name="ln1_ctx_bwd")

    d_mod_lat = jnp.concatenate([d_sh1, d_sc1, d_g1, d_sh2, d_sc2, d_g2], 1)
    zv = jnp.zeros((1, D), F32)
    d_mod_ctx = jnp.concatenate([d_csh1, d_csc1, zv, zv, zv, zv], 1)
    gs = {n: g for n, g in zip(s5_names, s5_grads)}
    gs["attn_sink"] = dsink[:, 0]
    gs["ssm_d"] = d_dskip
    gs["ln_mix_g"], gs["ln_mix_b"] = d_lg_mix, d_lb_mix
    gs["ln_mlp_g"], gs["ln_mlp_b"] = d_lg_mlp, d_lb_mlp
    gs["b_mlp1"], gs["b_mlp2"] = d_b1, d_b2
    return loss_p, grad_x, d_mod_lat, d_mod_ctx, gw, gs


def _my_pos():
    return lax.axis_index("x"), lax.axis_index("y"), lax.axis_index("c")


def _flip(p, bit):
    return 1 - p if bit else p


def _peer(pos, k):
    x, y, c = pos
    return (_flip(x, (k >> 2) & 1), _flip(y, (k >> 1) & 1), _flip(c, k & 1))


def _lin(pos):
    return 4 * pos[0] + 2 * pos[1] + pos[2]


def _allgather_small(v, name):
    r, w = v.shape

    def body(v_ref, out_ref, send_sems, recv_sems, local_sem):
        me = _my_pos()
        mine = pltpu.make_async_copy(v_ref, out_ref.at[_lin(me)], local_sem)
        mine.start()
        sends = []
        for k in range(1, N_DEV):
            cp = pltpu.make_async_remote_copy(src_ref=v_ref, dst_ref=out_ref.at[_lin(me)], send_sem=send_sems.at[k - 1],
                                              recv_sem=recv_sems.at[k - 1], device_id=_peer(me, k), device_id_type=MESH)
            cp.start()
            sends.append(cp)
        for k in range(1, N_DEV):
            peer = _peer(me, k)
            pltpu.make_async_remote_copy(src_ref=v_ref, dst_ref=out_ref.at[_lin(peer)], send_sem=send_sems.at[k - 1],
                                         recv_sem=recv_sems.at[k - 1], device_id=peer, device_id_type=MESH).wait_recv()
        for cp in sends:
            cp.wait_send()
        mine.wait()

    return pl.pallas_call(
        body,
        name=name,
        out_shape=jax.ShapeDtypeStruct((N_DEV, r, w), v.dtype),
        in_specs=[pl.BlockSpec(memory_space=pltpu.VMEM)],
        out_specs=pl.BlockSpec(memory_space=pltpu.VMEM),
        scratch_shapes=[pltpu.SemaphoreType.DMA((N_DEV - 1,)), pltpu.SemaphoreType.DMA((N_DEV - 1,)), pltpu.SemaphoreType.DMA],
        compiler_params=pltpu.CompilerParams(vmem_limit_bytes=VMEM_LIMIT_BYTES),
    )(v)


def _block_of(ref, kind, idx, n):
    start = pl.multiple_of(idx * n, 128)
    if kind == "col":
        return ref.at[:, pl.ds(start, n)]
    return ref.at[pl.ds(start, n), :]


def _allgather_weights(shards, kinds):
    nt = len(shards)
    out_shape = []
    for s, kind in zip(shards, kinds):
        k, n = s.shape
        out_shape.append(jax.ShapeDtypeStruct((k, n * N_DEV) if kind == "col" else (k * N_DEV, n), s.dtype))

    def body(*refs):
        ins, outs = refs[:nt], refs[nt:2 * nt]
        send_sems, recv_sems, local_sems = refs[2 * nt:]
        x, y, c = _my_pos()
        me, sibling = (x, y, c), (x, y, 1 - c)
        chips = [(1 - x, y), (x, 1 - y), (1 - x, 1 - y)]

        def blk(t, pos):
            n = shards[t].shape[1] if kinds[t] == "col" else shards[t].shape[0]
            return _block_of(outs[t], kinds[t], _lin(pos), n)

        def copy(t, k, block, to, src=None):
            return pltpu.make_async_remote_copy(src_ref=blk(t, block) if src is None else src, dst_ref=blk(t, block),
                                                send_sem=send_sems.at[t, k], recv_sem=recv_sems.at[t, k],
                                                device_id=to, device_id_type=MESH)

        local, sends = [], []
        for t in range(nt):
            mine = pltpu.make_async_copy(ins[t], blk(t, me), local_sems.at[t])
            mine.start()
            local.append(mine)
            first = [copy(t, 0, me, sibling, src=ins[t])]
            first += [copy(t, 1 + j, me, (*chip, c), src=ins[t]) for j, chip in enumerate(chips)]
            for cp in first:
                cp.start()
            sends += first
        for t in range(nt):
            for j, chip in enumerate(chips):
                copy(t, 1 + j, (*chip, c), me).wait_recv()
                fwd = copy(t, 4 + j, (*chip, c), sibling)
                fwd.start()
                sends.append(fwd)
        for t in range(nt):
            copy(t, 0, sibling, me).wait_recv()
            for j, chip in enumerate(chips):
                copy(t, 4 + j, (*chip, 1 - c), me).wait_recv()
        for cp in sends:
            cp.wait_send()
        for cp in local:
            cp.wait()

    any_spec = pl.BlockSpec(memory_space=pl.ANY)
    return pl.pallas_call(
        body,
        name="allgather_weights",
        out_shape=out_shape,
        in_specs=[any_spec] * nt,
        out_specs=[any_spec] * nt,
        scratch_shapes=[pltpu.SemaphoreType.DMA((nt, N_DEV - 1)), pltpu.SemaphoreType.DMA((nt, N_DEV - 1)),
                        pltpu.SemaphoreType.DMA((nt,))],
    )(*shards)


def _handshake(peers):
    barrier = pltpu.get_barrier_semaphore()
    for peer in peers:
        pl.semaphore_signal(barrier, inc=1, device_id=peer, device_id_type=MESH)
    pl.semaphore_wait(barrier, len(peers))


def _allgather_weights_seq(shards, kinds, name, collective_id):
    nt = len(shards)
    hbm = pltpu.MemorySpace.HBM
    ins = [jax.new_ref(s, memory_space=hbm) for s in shards]
    outs = []
    for s, kind in zip(shards, kinds):
        k, n = s.shape
        shape = (k, n * N_DEV) if kind == "col" else (k * N_DEV, n)
        outs.append(jax.empty_ref(jax.ShapeDtypeStruct(shape, s.dtype), memory_space=hbm))

    @functools.partial(
        pl.kernel, mesh=plsc.ScalarSubcoreMesh(axis_name="seq", num_cores=1), name=name,
        scratch_types=(pltpu.SemaphoreType.DMA((nt, N_DEV - 1)), pltpu.SemaphoreType.DMA((nt, N_DEV - 1)),
                       pltpu.SemaphoreType.DMA((nt,))),
        compiler_params=pltpu.CompilerParams(collective_id=collective_id))
    def launch(send_sems, recv_sems, local_sems):
        x, y, c = _my_pos()
        me, sibling = (x, y, c), (x, y, 1 - c)
        chips = [(1 - x, y), (x, 1 - y), (1 - x, 1 - y)]
        _handshake([sibling] + [(*chip, c) for chip in chips])

        def blk(t, pos):
            n = shards[t].shape[1] if kinds[t] == "col" else shards[t].shape[0]
            return _block_of(outs[t], kinds[t], _lin(pos), n)

        def copy(t, k, block, to, src=None):
            return pltpu.make_async_remote_copy(src_ref=blk(t, block) if src is None else src, dst_ref=blk(t, block),
                                                send_sem=send_sems.at[t, k], recv_sem=recv_sems.at[t, k],
                                                device_id=to, device_id_type=MESH)

        local, sends = [], []
        for t in range(nt):
            mine = pltpu.make_async_copy(ins[t], blk(t, me), local_sems.at[t])
            mine.start()
            local.append(mine)
            first = [copy(t, 0, me, sibling, src=ins[t])]
            first += [copy(t, 1 + j, me, (*chip, c), src=ins[t]) for j, chip in enumerate(chips)]
            for cp in first:
                cp.start()
            sends += first
        for t in range(nt):
            for j, chip in enumerate(chips):
                copy(t, 1 + j, (*chip, c), me).wait_recv()
                fwd = copy(t, 4 + j, (*chip, c), sibling)
                fwd.start()
                sends.append(fwd)
        for t in range(nt):
            copy(t, 0, sibling, me).wait_recv()
            for j, chip in enumerate(chips):
                copy(t, 4 + j, (*chip, 1 - c), me).wait_recv()
        for cp in sends:
            cp.wait_send()
        for cp in local:
            cp.wait()

    launch()
    return [o[...] for o in outs]


def _allgather_small_seq(v, name, collective_id):
    hbm = pltpu.MemorySpace.HBM
    src = jax.new_ref(v, memory_space=hbm)
    out = jax.empty_ref(jax.ShapeDtypeStruct((N_DEV,) + v.shape, v.dtype), memory_space=hbm)

    @functools.partial(
        pl.kernel, mesh=plsc.ScalarSubcoreMesh(axis_name="seq", num_cores=1), name=name,
        scratch_types=(pltpu.SemaphoreType.DMA((N_DEV - 1,)), pltpu.SemaphoreType.DMA((N_DEV - 1,)), pltpu.SemaphoreType.DMA),
        compiler_params=pltpu.CompilerParams(collective_id=collective_id))
    def launch(send_sems, recv_sems, local_sem):
        me = _my_pos()
        _handshake([_peer(me, k) for k in range(1, N_DEV)])
        mine = pltpu.make_async_copy(src, out.at[_lin(me)], local_sem)
        mine.start()
        sends = []
        for k in range(1, N_DEV):
            cp = pltpu.make_async_remote_copy(src_ref=src, dst_ref=out.at[_lin(me)], send_sem=send_sems.at[k - 1],
                                              recv_sem=recv_sems.at[k - 1], device_id=_peer(me, k), device_id_type=MESH)
            cp.start()
            sends.append(cp)
        for k in range(1, N_DEV):
            peer = _peer(me, k)
            pltpu.make_async_remote_copy(src_ref=src, dst_ref=out.at[_lin(peer)], send_sem=send_sems.at[k - 1],
                                         recv_sem=recv_sems.at[k - 1], device_id=peer, device_id_type=MESH).wait_recv()
        for cp in sends:
            cp.wait_send()
        mine.wait()

    launch()
    return out[...]


def _scatter_grads_seq(grads, kinds, name, collective_id):
    nt = len(grads)
    hbm = pltpu.MemorySpace.HBM
    shard_shapes = []
    for g, kind in zip(grads, kinds):
        k, n = g.shape
        shard_shapes.append((k, n // N_DEV) if kind == "col" else (k // N_DEV, n))
    ins = [jax.new_ref(g, memory_space=hbm) for g in grads]
    outs = [jax.empty_ref(jax.ShapeDtypeStruct((N_DEV,) + s, g.dtype), memory_space=hbm) for s, g in zip(shard_shapes, grads)]

    @functools.partial(
        pl.kernel, mesh=plsc.ScalarSubcoreMesh(axis_name="seq", num_cores=1), name=name,
        scratch_types=(pltpu.SemaphoreType.DMA((nt, N_DEV - 1)), pltpu.SemaphoreType.DMA((nt, N_DEV - 1)),
                       pltpu.SemaphoreType.DMA((nt,))),
        compiler_params=pltpu.CompilerParams(collective_id=collective_id))
    def launch(send_sems, recv_sems, local_sems):
        me = _my_pos()
        _handshake([_peer(me, k) for k in range(1, N_DEV)])

        def blk(t, pos):
            n = shard_shapes[t][1] if kinds[t] == "col" else shard_shapes[t][0]
            return _block_of(ins[t], kinds[t], _lin(pos), n)

        local, sends = [], []
        for t in range(nt):
            cp = pltpu.make_async_copy(blk(t, me), outs[t].at[_lin(me)], local_sems.at[t])
            cp.start()
            local.append(cp)
            for k in range(1, N_DEV):
                peer = _peer(me, k)
                cp = pltpu.make_async_remote_copy(src_ref=blk(t, peer), dst_ref=outs[t].at[_lin(me)], send_sem=send_sems.at[t, k - 1],
                                                  recv_sem=recv_sems.at[t, k - 1], device_id=peer, device_id_type=MESH)
                cp.start()
                sends.append(cp)
        for t in range(nt):
            for k in range(1, N_DEV):
                peer = _peer(me, k)
                pltpu.make_async_remote_copy(src_ref=blk(t, me), dst_ref=outs[t].at[_lin(peer)], send_sem=send_sems.at[t, k - 1],
                                             recv_sem=recv_sems.at[t, k - 1], device_id=peer, device_id_type=MESH).wait_recv()
        for cp in sends:
            cp.wait_send()
        for cp in local:
            cp.wait()

    launch()
    return [o[...] for o in outs]


_HBM_SPEC = pl.BlockSpec(memory_space=pltpu.HBM)
_SEM_SPEC = pl.BlockSpec(memory_space=pltpu.SEMAPHORE)
_EFFECT = pltpu.SideEffectType.DATAFLOW_SIDE_EFFECTING
LOCAL_CHUNKS = 16


def _shard_shapes(grads, kinds):
    return [(g.shape[0], g.shape[1] // N_DEV) if kind == "col" else (g.shape[0] // N_DEV, g.shape[1]) for g, kind in zip(grads, kinds)]


def _scatter_copies(g_refs, land_refs, send_sems, recv_sems, kinds, shard_shapes):
    me = _my_pos()
    copies = []
    for t in range(len(g_refs)):
        n = shard_shapes[t][1] if kinds[t] == "col" else shard_shapes[t][0]
        for k in range(1, N_DEV):
            peer = _peer(me, k)
            copies.append(pltpu.make_async_remote_copy(
                src_ref=_block_of(g_refs[t], kinds[t], _lin(peer), n), dst_ref=land_refs[t].at[_lin(me)],
                send_sem=send_sems.at[t * (N_DEV - 1) + k - 1], recv_sem=recv_sems.at[t * (N_DEV - 1) + k - 1],
                device_id=peer, device_id_type=MESH))
    return copies


def _scatter_start(grads, kinds, name):
    nt = len(grads)
    shard_shapes = _shard_shapes(grads, kinds)

    def body(*refs):
        g_refs, land_refs = refs[:nt], refs[nt:2 * nt]
        send_sems, recv_sems = refs[2 * nt], refs[2 * nt + 1]
        token = refs[2 * nt + 2 + 2 * nt]
        local_sems = refs[-1]
        me = _my_pos()
        local = []
        for t in range(nt):
            n = shard_shapes[t][1] if kinds[t] == "col" else shard_shapes[t][0]
            src, dst = _block_of(g_refs[t], kinds[t], _lin(me), n), land_refs[t].at[_lin(me)]
            rows = shard_shapes[t][0] // LOCAL_CHUNKS
            for ch in range(LOCAL_CHUNKS):
                rs = pl.ds(ch * rows, rows)
                cp = pltpu.make_async_copy(src.at[rs, :], dst.at[rs, :], local_sems.at[t * LOCAL_CHUNKS + ch])
                cp.start()
                local.append(cp)
        token[...] = jnp.zeros_like(token)
        for cp in local:
            cp.wait()
        for cp in _scatter_copies(g_refs, land_refs, send_sems, recv_sems, kinds, shard_shapes):
            cp.start()

    lands = [pltpu.with_memory_space_constraint(lax.empty((N_DEV,) + s, g.dtype), pltpu.HBM) for s, g in zip(shard_shapes, grads)]
    sem_shape = pltpu.SemaphoreType.DMA((nt * (N_DEV - 1),))
    out = pl.pallas_call(
        body,
        name=name,
        out_shape=(sem_shape, sem_shape, *[pltpu.HBM(g.shape, g.dtype) for g in grads],
                   *[pltpu.HBM(l.shape, l.dtype) for l in lands], jax.ShapeDtypeStruct((8, 128), F32)),
        in_specs=[_HBM_SPEC] * (2 * nt),
        out_specs=(_SEM_SPEC, _SEM_SPEC, *[_HBM_SPEC] * (2 * nt), pl.BlockSpec(memory_space=pltpu.VMEM)),
        input_output_aliases={i: 2 + i for i in range(2 * nt)},
        scratch_shapes=[pltpu.SemaphoreType.DMA((nt * LOCAL_CHUNKS,))],
        compiler_params=pltpu.CompilerParams(has_side_effects=_EFFECT),
    )(*[pltpu.with_memory_space_constraint(g, pltpu.HBM) for g in grads], *lands)
    return out[0], out[1], list(out[2:2 + nt]), list(out[2 + nt:2 + 2 * nt]), out[-1]


def _scatter_wait(send_sems, recv_sems, g_thru, land_thru, kinds, after, name):
    nt = len(g_thru)
    shard_shapes = _shard_shapes(g_thru, kinds)

    def body(*refs):
        g_refs, land_refs = refs[:nt], refs[nt:2 * nt]
        send_sems, recv_sems = refs[2 * nt], refs[2 * nt + 1]
        for cp in _scatter_copies(g_refs, land_refs, send_sems, recv_sems, kinds, shard_shapes):
            cp.wait_send()
            cp.wait_recv()

    out = pl.pallas_call(
        body,
        name=name,
        out_shape=tuple(pltpu.HBM(a.shape, a.dtype) for a in (*g_thru, *land_thru)),
        in_specs=[*[_HBM_SPEC] * (2 * nt), _SEM_SPEC, _SEM_SPEC, pl.BlockSpec(memory_space=pl.ANY)],
        out_specs=tuple([_HBM_SPEC] * (2 * nt)),
        input_output_aliases={i: i for i in range(2 * nt)},
        compiler_params=pltpu.CompilerParams(has_side_effects=_EFFECT),
    )(*g_thru, *land_thru, send_sems, recv_sems, after)
    return list(out[nt:])


def _scatter_grads(grads, kinds):
    nt = len(grads)
    shard_shapes = []
    for g, kind in zip(grads, kinds):
        k, n = g.shape
        shard_shapes.append((k, n // N_DEV) if kind == "col" else (k // N_DEV, n))

    def body(*refs):
        ins, outs = refs[:nt], refs[nt:2 * nt]
        send_sems, recv_sems, local_sems = refs[2 * nt:]
        me = _my_pos()

        def blk(t, pos):
            n = shard_shapes[t][1] if kinds[t] == "col" else shard_shapes[t][0]
            return _block_of(ins[t], kinds[t], _lin(pos), n)

        local, sends = [], []
        for t in range(nt):
            cp = pltpu.make_async_copy(blk(t, me), outs[t].at[_lin(me)], local_sems.at[t])
            cp.start()
            local.append(cp)
            for k in range(1, N_DEV):
                peer = _peer(me, k)
                cp = pltpu.make_async_remote_copy(src_ref=blk(t, peer), dst_ref=outs[t].at[_lin(me)], send_sem=send_sems.at[t, k - 1],
                                                  recv_sem=recv_sems.at[t, k - 1], device_id=peer, device_id_type=MESH)
                cp.start()
                sends.append(cp)
        for t in range(nt):
            for k in range(1, N_DEV):
                peer = _peer(me, k)
                pltpu.make_async_remote_copy(src_ref=blk(t, me), dst_ref=outs[t].at[_lin(peer)], send_sem=send_sems.at[t, k - 1],
                                             recv_sem=recv_sems.at[t, k - 1], device_id=peer, device_id_type=MESH).wait_recv()
        for cp in sends:
            cp.wait_send()
        for cp in local:
            cp.wait()

    any_spec = pl.BlockSpec(memory_space=pl.ANY)
    return pl.pallas_call(
        body,
        name="scatter_grads",
        out_shape=[jax.ShapeDtypeStruct((N_DEV,) + s, g.dtype) for s, g in zip(shard_shapes, grads)],
        in_specs=[any_spec] * nt,
        out_specs=[any_spec] * nt,
        scratch_shapes=[pltpu.SemaphoreType.DMA((nt, N_DEV - 1)), pltpu.SemaphoreType.DMA((nt, N_DEV - 1)),
                        pltpu.SemaphoreType.DMA((nt,))],
    )(*grads)


def _adam(g_slots, w, m, v, *, tr, name):
    ns, r, wd = g_slots.shape
    tr = min(tr, r)
    assert r % tr == 0, (name, r, tr)
    c1 = 1.0 - ADAM_B1 ** ADAM_STEP
    c2 = 1.0 - ADAM_B2 ** ADAM_STEP

    def kern(g_ref, w_ref, m_ref, v_ref, go_ref, d_ref, mo_ref, vo_ref):
        g = g_ref[0].astype(F32)
        for s in range(1, ns):
            g = g + g_ref[s].astype(F32)
        m_new = ADAM_B1 * m_ref[...] + (1.0 - ADAM_B1) * g
        v_new = ADAM_B2 * v_ref[...] + (1.0 - ADAM_B2) * (g * g)
        m_hat = m_new / c1
        v_hat = v_new / c2
        go_ref[...] = g
        d_ref[...] = -ADAM_LR * (m_hat / (jnp.sqrt(v_hat) + ADAM_EPS) + ADAM_WD * w_ref[...])
        mo_ref[...] = m_new
        vo_ref[...] = v_new

    tile = pl.BlockSpec((tr, wd), lambda i: (i, 0))
    return pl.pallas_call(
        kern,
        name=name,
        grid=(r // tr,),
        in_specs=[pl.BlockSpec((ns, tr, wd), lambda i: (0, i, 0)), tile, tile, tile],
        out_specs=[tile] * 4,
        out_shape=[jax.ShapeDtypeStruct((r, wd), F32)] * 4,
        compiler_params=_cparams(("parallel",)),
    )(g_slots, w, m, v)


SMALL = ("c_ctx", "b_ada", "attn_sink", "ssm_a_re", "ssm_a_im", "ssm_log_dt", "ssm_b_re", "ssm_b_im", "ssm_c_re", "ssm_c_im",
         "ssm_d", "ln_mix_g", "ln_mix_b", "b_mlp1", "b_mlp2", "ln_mlp_g", "ln_mlp_b")
BIG = ("w_in", "w_glu", "w_attn_up", "w_ssm_up", "w_out", "w_mlp1", "w_mlp2")
BIG_KIND = ("col", "col", "col", "col", "row", "col", "row")
AG_GROUPS = (("w_in",), ("w_glu", "w_attn_up", "w_ssm_up", "w_out"), ("w_mlp1",), ("w_mlp2",))
AG_COLLECTIVE_ID0 = 1
RS_GROUPS = (("w_mlp2",), ("w_mlp1",), ("w_out", "w_attn_up", "w_ssm_up", "w_glu"), ("w_in",))
RS_COLLECTIVE_ID0 = AG_COLLECTIVE_ID0 + len(AG_GROUPS)
SMALL_EARLY = ("ssm_a_re", "ssm_a_im", "ssm_log_dt", "ssm_b_re", "ssm_b_im", "ssm_c_re", "ssm_c_im", "ssm_d")
SMALL_LATE = tuple(n for n in SMALL if n not in SMALL_EARLY)
SMALL_COLLECTIVE_ID0 = RS_COLLECTIVE_ID0 + len(RS_GROUPS)
LANES = 128


def _pack(parts):
    rows = []
    for p in parts:
        flat = p.reshape(-1).astype(F32)
        pad = (-flat.shape[0]) % LANES
        rows.append(jnp.pad(flat, (0, pad)).reshape(-1, LANES))
    packed = jnp.concatenate(rows, 0)
    return jnp.pad(packed, ((0, (-packed.shape[0]) % 8), (0, 0)))


def _unpack(packed, shapes):
    out, r0 = [], 0
    for s in shapes:
        n = math.prod(s)
        nr = -(-n // LANES)
        out.append(packed[r0:r0 + nr].reshape(-1)[:n].reshape(s))
        r0 += nr
    return out


WEIGHTS = ("c_ctx", "w_ada", "b_ada", "w_in", "attn_sink", "ssm_a_re", "ssm_a_im", "ssm_log_dt", "ssm_b_re", "ssm_b_im",
           "ssm_c_re", "ssm_c_im", "ssm_d", "w_glu", "w_attn_up", "w_ssm_up", "w_out", "ln_mix_g", "ln_mix_b", "w_mlp1",
           "b_mlp1", "w_mlp2", "b_mlp2", "ln_mlp_g", "ln_mlp_b")
ADA_COLS = 6 * D // N_DEV


def _step(x, c, ctx, loss_target, p, m, v):
    me = _lin(_my_pos())
    x2, ctx2, tgt2 = x[0], ctx[0], loss_target[0]

    wb = {}
    for gi, group in enumerate(AG_GROUPS):
        full = _allgather_weights_seq([p[n][0].astype(BF16) for n in group], [BIG_KIND[BIG.index(n)] for n in group],
                                      "allgather_seq%d" % gi, AG_COLLECTIVE_ID0 + gi)
        wb.update(zip(group, full))

    c_all = _allgather_small(jnp.broadcast_to(c, (8, D)), "gather_c")[:, 0, :]
    cc = p["c_ctx"].reshape(1, D)
    s_in = jnp.concatenate([c_all, cc, jnp.zeros((7, D), F32)], 0)
    s_act, = _rowwise(lambda rv, vv: ([_silu(rv[0])], []), [(s_in, D, 0, 0)], [], [(D, F32)], [], nrows=16, tr=16, name="silu_c")
    b_mine = lax.dynamic_slice_in_dim(p["b_ada"], me * ADA_COLS, ADA_COLS, axis=1)
    mod_part = _matmul(s_act, p["w_ada"][0], mode="nn", name="ada_fwd", tm=16, tn=512, bias=b_mine)
    mod_all = _allgather_small(mod_part, "gather_mod")
    mod_lat = lax.dynamic_index_in_dim(mod_all, me, axis=1, keepdims=False).reshape(1, 6 * D)
    mod_ctx = mod_all[:, 8, :].reshape(1, 6 * D)

    sp = {n: p[n][0] for n in SMALL if n not in ("c_ctx", "b_ada")}
    started = {}

    def on_grad(gw):
        for gi, group in enumerate(RS_GROUPS):
            if gi not in started and all(n in gw for n in group):
                kinds = [BIG_KIND[BIG.index(n)] for n in group]
                send_sems, recv_sems, g_thru, land_thru, token = _scatter_start([gw[n] for n in group], kinds, "scatter_start%d" % gi)
                started[gi] = (send_sems, recv_sems, g_thru, land_thru, kinds)
                for n in group:
                    gw.tokens[n] = token

    total = {}

    def on_loss(loss_p):
        total["loss"] = lax.psum(loss_p[0, 0], ("x", "y", "c"))
        return total["loss"].reshape(1, 1)

    loss_p, grad_x, d_mod_lat, d_mod_ctx, gw, gs = _local_step(x2, ctx2, tgt2, mod_lat, mod_ctx, wb, sp, on_grad, on_loss)
    recv = {}
    for gi, group in enumerate(RS_GROUPS):
        send_sems, recv_sems, g_thru, land_thru, kinds = started[gi]
        recv.update(zip(group, _scatter_wait(send_sems, recv_sems, g_thru, land_thru, kinds, grad_x, "scatter_wait%d" % gi)))

    g_early = _allgather_small_seq(_pack([gs[n] for n in SMALL_EARLY]), "gather_small_early", SMALL_COLLECTIVE_ID0)

    dm = jnp.concatenate([d_mod_lat, d_mod_ctx, jnp.zeros((6, 6 * D), F32)], 0)
    dm_all = _allgather_small_seq(dm, "gather_dmod", SMALL_COLLECTIVE_ID0 + 1)
    dm2 = jnp.concatenate([dm_all[:, 0, :], dm_all[:, 1, :]], 0)
    dm2_mine = lax.dynamic_slice_in_dim(dm2, me * ADA_COLS, ADA_COLS, axis=1)
    s2 = jnp.concatenate([s_act[0:8], jnp.broadcast_to(s_act[8:9], (8, D))], 0)
    g_w_ada = _matmul(s2, dm2_mine, mode="tn", name="dw_ada", tm=512, tn=ADA_COLS)
    dsc_part = _matmul(dm2_mine[8:16], p["w_ada"][0], mode="nt", name="d_silu_cctx", tm=8, tn=512)

    def cctx_b(rv, vv):
        _, pull = jax.vjp(_silu, vv[0])
        return [], [pull(jnp.sum(rv[0], axis=0, keepdims=True))[0]]

    g_cctx, = _rowwise(cctx_b, [(dsc_part, D, 0, 0)], [cc], [], [(1, D)], nrows=8, tr=8, name="cctx_bwd")
    gs["c_ctx"] = g_cctx
    gs["b_ada"] = d_mod_lat + d_mod_ctx

    res = {}
    for n in BIG:
        res[n] = _adam(recv[n], p[n][0], m[n][0], v[n][0], tr=256, name="adam_" + n)
    res["w_ada"] = _adam(g_w_ada[None], p["w_ada"][0], m["w_ada"][0], v["w_ada"][0], tr=256, name="adam_w_ada")

    g_late = _allgather_small_seq(_pack([gs[n] for n in SMALL_LATE]), "gather_small_late", SMALL_COLLECTIVE_ID0 + 2)
    for names, g_pack, tag in ((SMALL_EARLY, g_early, "early"), (SMALL_LATE, g_late, "late")):
        sm = _adam(g_pack, _pack([p[n] for n in names]), _pack([m[n] for n in names]), _pack([v[n] for n in names]),
                   tr=g_pack.shape[1], name="adam_small_" + tag)
        shapes = [p[n].shape for n in names]
        for j, outs in enumerate(zip(*[_unpack(a, shapes) for a in sm])):
            res[names[j]] = outs

    outs = [total["loss"], grad_x[None]]
    for j in range(4):
        outs += [res[n][j].reshape(p[n].shape) for n in WEIGHTS]
    return tuple(outs)


def kernel(x, c, ctx, c_ctx, w_ada, b_ada, w_in, attn_sink, ssm_a_re, ssm_a_im, ssm_log_dt, ssm_b_re, ssm_b_im, ssm_c_re, ssm_c_im, ssm_d, w_glu, w_attn_up, w_ssm_up, w_out, ln_mix_g, ln_mix_b, w_mlp1, b_mlp1, w_mlp2, b_mlp2, ln_mlp_g, ln_mlp_b, loss_target, m_c_ctx, m_w_ada, m_b_ada, m_w_in, m_attn_sink, m_ssm_a_re, m_ssm_a_im, m_ssm_log_dt, m_ssm_b_re, m_ssm_b_im, m_ssm_c_re, m_ssm_c_im, m_ssm_d, m_w_glu, m_w_attn_up, m_w_ssm_up, m_w_out, m_ln_mix_g, m_ln_mix_b, m_w_mlp1, m_b_mlp1, m_w_mlp2, m_b_mlp2, m_ln_mlp_g, m_ln_mlp_b, v_c_ctx, v_w_ada, v_b_ada, v_w_in, v_attn_sink, v_ssm_a_re, v_ssm_a_im, v_ssm_log_dt, v_ssm_b_re, v_ssm_b_im, v_ssm_c_re, v_ssm_c_im, v_ssm_d, v_w_glu, v_w_attn_up, v_w_ssm_up, v_w_out, v_ln_mix_g, v_ln_mix_b, v_w_mlp1, v_b_mlp1, v_w_mlp2, v_b_mlp2, v_ln_mlp_g, v_ln_mlp_b):
    given = dict(locals())
    p = {n: given[n] for n in WEIGHTS}
    m = {n: given["m_" + n] for n in WEIGHTS}
    v = {n: given["v_" + n] for n in WEIGHTS}
    return _step(x, c, ctx, loss_target, p, m, v)
```

```python
import functools
import math

import jax
import jax.numpy as jnp
from jax import lax
from jax.experimental import pallas as pl
from jax.experimental.pallas import tpu as pltpu
from jax.experimental.pallas import tpu_sc as plsc

F32 = jnp.float32
BF16 = jnp.bfloat16

N_DEV = 8
D = 2048
T = 2048
C = 256
TA = T + C
GRID_W = 64
HD = 128
NH = 8
NKV = 2
GROUP = NH // NKV
WINDOW = 128
QW = NH * HD
KVW = NKV * HD
SW = D // 4
SG = 16
NG = SW // SG
SP = 64
DFF = 4 * D
IN_COLS = QW + 2 * KVW + SW + 2 * D
ALPHA = 2.0 ** 0.25
LN_EPS = 1e-6
NEG_INF = -1e30
ROPE_BASE = 10000.0
ATT_SCALE = HD ** -0.5

NSEG = 8
GBLK = 8
NBLK = NG // GBLK
BW = GBLK * SP
UW = GBLK * SG

ADAM_LR = 0.001
ADAM_B1 = 0.9
ADAM_B2 = 0.999
ADAM_EPS = 1e-08
ADAM_WD = 0.01
ADAM_STEP = 10

VMEM_LIMIT_BYTES = 56 * 1024 * 1024
MESH = pl.DeviceIdType.MESH


def _cparams(sem=None):
    return pltpu.CompilerParams(dimension_semantics=sem, vmem_limit_bytes=VMEM_LIMIT_BYTES)


def _matmul(a, b, *, mode, name, out_dtypes=(F32,), tm=512, tn=512, tk=None, bias=None, extras=(), epilogue=None, after=()):
    if mode == "nn":
        (M, K), (K2, N) = a.shape, b.shape
    elif mode == "nt":
        (M, K), (N, K2) = a.shape, b.shape
    else:
        (K, M), (K2, N) = a.shape, b.shape
    assert K == K2, (name, a.shape, b.shape)
    tm, tn, tk = min(tm, M), min(tn, N), min(tk or K, K)
    assert M % tm == 0 and N % tn == 0 and K % tk == 0, (name, M, N, K, tm, tn, tk)
    nk = K // tk
    if mode == "tn":
        a_spec = pl.BlockSpec((tk, tm), lambda i, j, k: (k, i))
    else:
        a_spec = pl.BlockSpec((tm, tk), lambda i, j, k: (i, k))
    if mode == "nt":
        b_spec = pl.BlockSpec((tn, tk), lambda i, j, k: (j, k))
    else:
        b_spec = pl.BlockSpec((tk, tn), lambda i, j, k: (k, j))
    dims = {"nn": (((1,), (0,)), ((), ())), "nt": (((1,), (1,)), ((), ())), "tn": (((0,), (0,)), ((), ()))}[mode]
    in_specs = [a_spec, b_spec]
    operands = [a, b]
    if bias is not None:
        in_specs.append(pl.BlockSpec((1, tn), lambda i, j, k: (0, j)))
        operands.append(bias)
    for e in extras:
        in_specs.append(pl.BlockSpec((tm, tn), lambda i, j, k: (i, j)))
        operands.append(e)
    n_ex = len(extras)
    for t in after:
        in_specs.append(pl.BlockSpec(memory_space=pl.ANY))
        operands.append(t)
    n_after = len(after)
    n_out = len(out_dtypes)
    has_bias = bias is not None

    def kern(*refs):
        a_ref, b_ref = refs[0], refs[1]
        pos = 2
        bias_ref = None
        if has_bias:
            bias_ref = refs[pos]
            pos += 1
        ex_refs = refs[pos:pos + n_ex]
        pos += n_ex + n_after
        out_refs = refs[pos:pos + n_out]
        acc_ref = refs[pos + n_out] if nk > 1 else None

        def finish(r):
            if has_bias:
                r = r + bias_ref[...]
            outs = epilogue(r, *[e[...] for e in ex_refs]) if epilogue is not None else (r,)
            for o_ref, o in zip(out_refs, outs):
                o_ref[...] = o.astype(o_ref.dtype)

        part = lax.dot_general(a_ref[...].astype(BF16), b_ref[...].astype(BF16), dims, preferred_element_type=F32)
        if nk == 1:
            finish(part)
        else:
            k = pl.program_id(2)

            @pl.when(k == 0)
            def _():
                acc_ref[...] = part

            @pl.when(k > 0)
            def _():
                acc_ref[...] += part

            @pl.when(k == nk - 1)
            def _():
                finish(acc_ref[...])

    outs = pl.pallas_call(
        kern,
        name=name,
        grid=(M // tm, N // tn, nk),
        in_specs=in_specs,
        out_specs=[pl.BlockSpec((tm, tn), lambda i, j, k: (i, j)) for _ in out_dtypes],
        out_shape=[jax.ShapeDtypeStruct((M, N), dt) for dt in out_dtypes],
        scratch_shapes=[pltpu.VMEM((tm, tn), F32)] if nk > 1 else [],
        compiler_params=_cparams(("parallel", "parallel", "arbitrary")),
    )(*operands)
    return outs[0] if n_out == 1 else tuple(outs)


def _rowwise(fn, rows, vecs, outs, vec_outs, *, nrows, tr, name):
    n_rows, n_vecs, n_outs = len(rows), len(vecs), len(outs)
    in_specs = [pl.BlockSpec((tr, w), lambda i, cb=cb, ro=ro: (i + ro, cb)) for (_, w, cb, ro) in rows]
    in_specs += [pl.BlockSpec(v.shape, lambda i: (0, 0)) for v in vecs]
    out_specs = [pl.BlockSpec((tr, w), lambda i: (i, 0)) for (w, _) in outs]
    out_specs += [pl.BlockSpec(s, lambda i: (0, 0)) for s in vec_outs]
    out_shape = [jax.ShapeDtypeStruct((nrows, w), dt) for (w, dt) in outs]
    out_shape += [jax.ShapeDtypeStruct(s, F32) for s in vec_outs]

    def kern(*refs):
        rvals = [r[...] for r in refs[:n_rows]]
        vvals = [r[...] for r in refs[n_rows:n_rows + n_vecs]]
        o_refs = refs[n_rows + n_vecs:n_rows + n_vecs + n_outs]
        v_refs = refs[n_rows + n_vecs + n_outs:]
        ro, vo = fn(rvals, vvals)
        for r, val in zip(o_refs, ro):
            r[...] = val.astype(r.dtype)
        i = pl.program_id(0)
        for r, val in zip(v_refs, vo):
            @pl.when(i == 0)
            def _(r=r, val=val):
                r[...] = val.astype(F32)

            @pl.when(i > 0)
            def _(r=r, val=val):
                r[...] += val.astype(F32)

    res = pl.pallas_call(
        kern,
        name=name,
        grid=(nrows // tr,),
        in_specs=in_specs,
        out_specs=out_specs,
        out_shape=out_shape,
        compiler_params=_cparams(("arbitrary",)),
    )(*[r[0] for r in rows], *vecs)
    return list(res)


def _ln(x):
    mu = jnp.mean(x, axis=-1, keepdims=True)
    xc = x - mu
    var = jnp.mean(xc * xc, axis=-1, keepdims=True)
    return xc * lax.rsqrt(var + LN_EPS)


def _sigmoid(x):
    return 1.0 / (1.0 + jnp.exp(-x))


def _gelu(x):
    return 0.5 * x * (1.0 + jnp.tanh(math.sqrt(2.0 / math.pi) * (x + 0.044715 * (x * x * x))))


def _silu(x):
    return x * _sigmoid(x)


def _f_ln_mod(x, sc, sh):
    return _ln(x) * (1.0 + sc) + sh


def _f_glu(z):
    return z[:, :SW] * _sigmoid(z[:, SW:])


def _f_mix(ga, gs, attn_d, ssm_d):
    return _sigmoid(ga) * attn_d + _sigmoid(gs) * ssm_d


def _f_post1(x, y, g1, lg, lb, sc2, sh2):
    r1 = ALPHA * x + g1 * y
    x1 = _ln(r1) * lg + lb
    h2 = _ln(x1) * (1.0 + sc2) + sh2
    return x1, h2


def _f_loss(x1, mlp, tgt, g2, lg, lb, b2z):
    r2 = ALPHA * x1 + g2 * (mlp + b2z)
    out = _ln(r2) * lg + lb
    err = out - tgt
    return 0.5 * jnp.sum(err * err) * (1.0 / D)


def _rope_tables():
    rows = T // GRID_W
    row = jnp.repeat(jnp.arange(rows), GRID_W)
    col = jnp.tile(jnp.arange(GRID_W), rows)
    n_freq = HD // 4
    freqs = ROPE_BASE ** (-jnp.arange(n_freq, dtype=F32) / n_freq)
    ang_r = row.astype(F32)[:, None] * freqs
    ang_c = col.astype(F32)[:, None] * freqs
    ang = jnp.concatenate([ang_r, ang_r, ang_c, ang_c], -1)
    cos, sin = jnp.cos(ang), jnp.sin(ang)
    lo = (jnp.arange(HD) % (HD // 2)) < (HD // 4)
    sin_a = jnp.where(lo[None, :], -sin, 0.0)
    sin_b = jnp.where(lo[None, :], 0.0, sin)
    return cos, sin_a, sin_b


def _rope(x, cos, sa, sb):
    return x * cos + pltpu.roll(x, 96, 1) * sa + pltpu.roll(x, 32, 1) * sb


def _rope_t(dy, cos, sa, sb):
    return dy * cos + pltpu.roll(dy * sa, 32, 1) + pltpu.roll(dy * sb, 96, 1)


BAND = 3 * WINDOW
KPAD = T + 2 * WINDOW


def _attn_fill_kv(k_ref, v_ref, cos_ref, sa_ref, sb_ref, kp, vp, kc, vc):
    zeros = jnp.zeros((WINDOW, KVW), BF16)
    kp[0:WINDOW, :] = zeros
    kp[WINDOW + T:KPAD, :] = zeros
    vp[0:WINDOW, :] = zeros
    vp[WINDOW + T:KPAD, :] = zeros
    for hh in range(NKV):
        cs = slice(hh * HD, (hh + 1) * HD)
        for r0 in range(0, T, 512):
            rs = slice(r0, r0 + 512)
            kr = _rope(k_ref[rs, cs], cos_ref[rs, :], sa_ref[rs, :], sb_ref[rs, :])
            kp[WINDOW + r0:WINDOW + r0 + 512, cs] = kr.astype(BF16)
    vp[WINDOW:WINDOW + T, :] = v_ref[0:T, :].astype(BF16)
    kc[...] = k_ref[T:TA, :].astype(BF16)
    vc[...] = v_ref[T:TA, :].astype(BF16)


def _attn_scores(n, h, q_ref, cos_ref, sa_ref, sb_ref, sink_ref, kp, kc):
    kvh = h // GROUP
    r0 = pl.multiple_of(n * WINDOW, WINDOW)
    cos = cos_ref[pl.ds(r0, WINDOW), :]
    sa = sa_ref[pl.ds(r0, WINDOW), :]
    sb = sb_ref[pl.ds(r0, WINDOW), :]
    q_h = _rope(q_ref[:, h * HD:(h + 1) * HD], cos, sa, sb).astype(BF16)
    kb = kp[pl.ds(r0, BAND), kvh * HD:(kvh + 1) * HD]
    kcb = kc[:, kvh * HD:(kvh + 1) * HD]
    nt = (((1,), (1,)), ((), ()))
    s_loc = lax.dot_general(q_h, kb, nt, preferred_element_type=F32) * ATT_SCALE
    s_ctx = lax.dot_general(q_h, kcb, nt, preferred_element_type=F32) * ATT_SCALE
    row = lax.broadcasted_iota(jnp.int32, (WINDOW, BAND), 0)
    col = lax.broadcasted_iota(jnp.int32, (WINDOW, BAND), 1)
    rel = col - WINDOW - row
    kpos = r0 - WINDOW + col
    valid = (jnp.abs(rel) <= WINDOW) & (kpos >= 0) & (kpos < T)
    s_loc = jnp.where(valid, s_loc, NEG_INF)
    sk = sink_ref[0:1, h:h + 1]
    m = jnp.maximum(jnp.maximum(jnp.max(s_loc, -1, keepdims=True), jnp.max(s_ctx, -1, keepdims=True)), sk)
    e_loc = jnp.exp(s_loc - m)
    e_ctx = jnp.exp(s_ctx - m)
    e_sink = jnp.exp(sk - m)
    inv = 1.0 / (jnp.sum(e_loc, -1, keepdims=True) + jnp.sum(e_ctx, -1, keepdims=True) + e_sink)
    return q_h, r0, e_loc * inv, e_ctx * inv, e_sink * inv


def _attn_fwd(proj, sink, tabs):
    cos, sa, sb = tabs

    def kern(q_ref, k_ref, v_ref, cos_ref, sa_ref, sb_ref, sink_ref, o_ref, kp, vp, kc, vc):
        n = pl.program_id(0)

        @pl.when(n == 0)
        def _():
            _attn_fill_kv(k_ref, v_ref, cos_ref, sa_ref, sb_ref, kp, vp, kc, vc)

        for h in range(NH):
            kvh = h // GROUP
            _, r0, p_loc, p_ctx, _ = _attn_scores(n, h, q_ref, cos_ref, sa_ref, sb_ref, sink_ref, kp, kc)
            vb = vp[pl.ds(r0, BAND), kvh * HD:(kvh + 1) * HD]
            vcb = vc[:, kvh * HD:(kvh + 1) * HD]
            o = jnp.dot(p_loc.astype(BF16), vb, preferred_element_type=F32)
            o = o + jnp.dot(p_ctx.astype(BF16), vcb, preferred_element_type=F32)
            o_ref[:, h * HD:(h + 1) * HD] = o.astype(o_ref.dtype)

    full = lambda shape: pl.BlockSpec(shape, lambda n: (0, 0))
    return pl.pallas_call(
        kern,
        name="attn_fwd",
        grid=(T // WINDOW,),
        in_specs=[
            pl.BlockSpec((WINDOW, QW), lambda n: (n, 0)),
            pl.BlockSpec((TA, KVW), lambda n: (0, QW // KVW)),
            pl.BlockSpec((TA, KVW), lambda n: (0, QW // KVW + 1)),
            full((T, HD)), full((T, HD)), full((T, HD)), full((1, NH)),
        ],
        out_specs=pl.BlockSpec((WINDOW, QW), lambda n: (n, 0)),
        out_shape=jax.ShapeDtypeStruct((T, QW), BF16),
        scratch_shapes=[pltpu.VMEM((KPAD, KVW), BF16), pltpu.VMEM((KPAD, KVW), BF16),
                        pltpu.VMEM((C, KVW), BF16), pltpu.VMEM((C, KVW), BF16)],
        compiler_params=_cparams(("arbitrary",)),
    )(proj, proj, proj, cos, sa, sb, sink)


def _attn_bwd(proj, d_attn, sink, tabs):
    cos, sa, sb = tabs
    n_blocks = T // WINDOW

    def kern(q_ref, k_ref, v_ref, do_ref, cos_ref, sa_ref, sb_ref, sink_ref,
             dq_ref, dk_ref, dv_ref, dsink_ref, kp, vp, kc, vc, dkp, dvp, dkc, dvc):
        n = pl.program_id(0)

        @pl.when(n == 0)
        def _():
            _attn_fill_kv(k_ref, v_ref, cos_ref, sa_ref, sb_ref, kp, vp, kc, vc)
            dkp[...] = jnp.zeros_like(dkp)
            dvp[...] = jnp.zeros_like(dvp)
            dkc[...] = jnp.zeros_like(dkc)
            dvc[...] = jnp.zeros_like(dvc)
            dsink_ref[...] = jnp.zeros_like(dsink_ref)

        nt = (((1,), (1,)), ((), ()))
        tn = (((0,), (0,)), ((), ()))
        for h in range(NH):
            kvh = h // GROUP
            cs = slice(kvh * HD, (kvh + 1) * HD)
            q_h, r0, p_loc, p_ctx, p_sink = _attn_scores(n, h, q_ref, cos_ref, sa_ref, sb_ref, sink_ref, kp, kc)
            kb = kp[pl.ds(r0, BAND), cs]
            vb = vp[pl.ds(r0, BAND), cs]
            kcb = kc[:, cs]
            vcb = vc[:, cs]
            do_h = do_ref[:, h * HD:(h + 1) * HD]
            dp_loc = lax.dot_general(do_h, vb, nt, preferred_element_type=F32)
            dp_ctx = lax.dot_general(do_h, vcb, nt, preferred_element_type=F32)
            delta = jnp.sum(p_loc * dp_loc, -1, keepdims=True) + jnp.sum(p_ctx * dp_ctx, -1, keepdims=True)
            ds_loc = (p_loc * (dp_loc - delta) * ATT_SCALE).astype(BF16)
            ds_ctx = (p_ctx * (dp_ctx - delta) * ATT_SCALE).astype(BF16)
            dq = jnp.dot(ds_loc, kb, preferred_element_type=F32) + jnp.dot(ds_ctx, kcb, preferred_element_type=F32)
            cos = cos_ref[pl.ds(r0, WINDOW), :]
            sa_ = sa_ref[pl.ds(r0, WINDOW), :]
            sb_ = sb_ref[pl.ds(r0, WINDOW), :]
            dq_ref[:, h * HD:(h + 1) * HD] = _rope_t(dq, cos, sa_, sb_).astype(dq_ref.dtype)
            dkp[pl.ds(r0, BAND), cs] += lax.dot_general(ds_loc, q_h, tn, preferred_element_type=F32)
            dkc[:, cs] += lax.dot_general(ds_ctx, q_h, tn, preferred_element_type=F32)
            dvp[pl.ds(r0, BAND), cs] += lax.dot_general(p_loc.astype(BF16), do_h, tn, preferred_element_type=F32)
            dvc[:, cs] += lax.dot_general(p_ctx.astype(BF16), do_h, tn, preferred_element_type=F32)
            dsk = -jnp.sum(p_sink * delta, axis=0, keepdims=True)
            dsink_ref[h:h + 1, :] += jnp.broadcast_to(dsk, (1, HD))

        @pl.when(n == n_blocks - 1)
        def _():
            for hh in range(NKV):
                cs = slice(hh * HD, (hh + 1) * HD)
                for r0 in range(0, T, 512):
                    rs = slice(r0, r0 + 512)
                    g = dkp[WINDOW + r0:WINDOW + r0 + 512, cs]
                    dk_ref[rs, cs] = _rope_t(g, cos_ref[rs, :], sa_ref[rs, :], sb_ref[rs, :]).astype(dk_ref.dtype)
            dk_ref[T:TA, :] = dkc[...].astype(dk_ref.dtype)
            dv_ref[0:T, :] = dvp[WINDOW:WINDOW + T, :].astype(dv_ref.dtype)
            dv_ref[T:TA, :] = dvc[...].astype(dv_ref.dtype)

    full = lambda shape: pl.BlockSpec(shape, lambda n: (0, 0))
    return pl.pallas_call(
        kern,
        name="attn_bwd",
        grid=(n_blocks,),
        in_specs=[
            pl.BlockSpec((WINDOW, QW), lambda n: (n, 0)),
            pl.BlockSpec((TA, KVW), lambda n: (0, QW // KVW)),
            pl.BlockSpec((TA, KVW), lambda n: (0, QW // KVW + 1)),
            pl.BlockSpec((WINDOW, QW), lambda n: (n, 0)),
            full((T, HD)), full((T, HD)), full((T, HD)), full((1, NH)),
        ],
        out_specs=[pl.BlockSpec((WINDOW, QW), lambda n: (n, 0)), full((TA, KVW)), full((TA, KVW)), full((NH, HD))],
        out_shape=[jax.ShapeDtypeStruct((T, QW), BF16), jax.ShapeDtypeStruct((TA, KVW), BF16),
                   jax.ShapeDtypeStruct((TA, KVW), BF16), jax.ShapeDtypeStruct((NH, HD), F32)],
        scratch_shapes=[pltpu.VMEM((KPAD, KVW), BF16), pltpu.VMEM((KPAD, KVW), BF16),
                        pltpu.VMEM((C, KVW), BF16), pltpu.VMEM((C, KVW), BF16),
                        pltpu.VMEM((KPAD, KVW), F32), pltpu.VMEM((KPAD, KVW), F32),
                        pltpu.VMEM((C, KVW), F32), pltpu.VMEM((C, KVW), F32)],
        compiler_params=_cparams(("arbitrary",)),
    )(proj, proj, proj, d_attn, cos, sa, sb, sink)


def _s5_prep(a_re, a_im, log_dt, b_re, b_im, c_re, c_im):
    lam = lax.complex(a_re, a_im)
    dt = jnp.exp(log_dt)[..., None]
    lam_bar = jnp.exp(lam * dt)
    b_bar = ((lam_bar - 1.0) / lam)[..., None] * lax.complex(b_re, b_im)
    eye = jnp.eye(GBLK, dtype=F32)

    def lam_rows(v):
        return v.reshape(2, NBLK, 1, BW)

    lam_l = jnp.concatenate([lam_rows(jnp.real(lam_bar)), lam_rows(jnp.imag(lam_bar))], -1)
    lam_l = jnp.broadcast_to(lam_l, (2, NBLK, 8, 2 * BW))

    def b_blocks(v):
        v = v.reshape(2, NBLK, GBLK, SP, SG).transpose(0, 1, 2, 4, 3)
        return (v[:, :, :, :, None, :] * eye[None, None, :, None, :, None]).reshape(2, NBLK, UW, BW)

    bmat = jnp.concatenate([b_blocks(jnp.real(b_bar)), b_blocks(jnp.imag(b_bar))], -1)

    def c_blocks(v):
        v = v.reshape(2, NBLK, GBLK, SG, SP).transpose(0, 1, 2, 4, 3)
        return (v[:, :, :, :, None, :] * eye[None, None, :, None, :, None]).reshape(2, NBLK, BW, UW)

    cmat = jnp.concatenate([c_blocks(c_re), -c_blocks(c_im)], 2)
    return lam_l, bmat, cmat


def _cmul(ar, ai, br, bi):
    return ar * br - ai * bi, ar * bi + ai * br


def _shift_rows(x, rev, fill):
    r = lax.broadcasted_iota(jnp.int32, x.shape, 0)
    down = jnp.where(r == 0, fill, pltpu.roll(x, 1, 0))
    up = jnp.where(r == NSEG - 1, fill, pltpu.roll(x, NSEG - 1, 0))
    return jnp.where(rev == 0, down, up)


def _edge_row(x, rev):
    last = jnp.broadcast_to(x[NSEG - 1:NSEG, :], x.shape)
    first = jnp.broadcast_to(x[0:1, :], x.shape)
    return jnp.where(rev == 0, last, first)


def _seg_scan(get, put, base, seglen, lr, li, rev, cin):
    zero = jnp.zeros((NSEG, BW), F32)

    def rows(k):
        j = jnp.where(rev == 0, k, seglen - 1 - k)
        return pl.ds(pl.multiple_of(base + j * NSEG, NSEG), NSEG)

    def local(k, carry):
        sr, si, pr, pi = carry
        xr, xi = get(rows(k))
        tr, ti = _cmul(lr, li, sr, si)
        sr, si = tr + xr, ti + xi
        put(rows(k), sr, si)
        pr, pi = _cmul(lr, li, pr, pi)
        return sr, si, pr, pi

    er, ei, lpr, lpi = lax.fori_loop(0, seglen, local, (zero, zero, zero + 1.0, zero))
    cr, ci = _shift_rows(zero, rev, cin[0]), _shift_rows(zero, rev, cin[1])
    for _ in range(NSEG - 1):
        tr, ti = _cmul(lpr, lpi, cr, ci)
        cr, ci = _shift_rows(er + tr, rev, cin[0]), _shift_rows(ei + ti, rev, cin[1])

    def fix(k, carry):
        pr, pi = carry
        xr, xi = get(rows(k))
        tr, ti = _cmul(pr, pi, cr, ci)
        put(rows(k), xr + tr, xi + ti)
        return _cmul(lr, li, pr, pi)

    lax.fori_loop(0, seglen, fix, (lr, li))
    tr, ti = _cmul(lpr, lpi, cr, ci)
    return _edge_row(er + tr, rev), _edge_row(ei + ti, rev)


RCH = 256
CSEG = C // NSEG
TSEG = T // NSEG
UCOL0 = (QW + 2 * KVW) // UW


REGIONS = ((0, TSEG), (T, CSEG))


def _state_access(ref, lead=()):
    def get(rows):
        return ref[(*lead, rows, slice(0, BW))], ref[(*lead, rows, slice(BW, 2 * BW))]

    def put(rows, re, im):
        ref[(*lead, rows, slice(0, BW))] = re
        ref[(*lead, rows, slice(BW, 2 * BW))] = im

    return get, put


def _interleave_rows(src_ref, dst_ref, regions=REGIONS):
    for base, seglen in regions:
        def body(j, carry, base=base, seglen=seglen):
            dst_ref[pl.ds(pl.multiple_of(base + j * NSEG, NSEG), NSEG), :] = src_ref[pl.ds(base + j, NSEG, stride=seglen), :]
            return carry

        lax.fori_loop(0, seglen, body, 0, unroll=8)


def _deinterleave_rows(src_ref, dst_ref, regions=REGIONS):
    for base, seglen in regions:
        def body(j, carry, base=base, seglen=seglen):
            dst_ref[pl.ds(base + j, NSEG, stride=seglen), :] = src_ref[pl.ds(pl.multiple_of(base + j * NSEG, NSEG), NSEG), :]
            return carry

        lax.fori_loop(0, seglen, body, 0, unroll=8)


def _s5_fwd(proj, dskip, lam, bmat, cmat):
    def kern(u_ref, dk_ref, lam_ref, b_ref, c_ref, s_ref, ssm_ref, ge_ref, up_ref, yp_ref):
        d = pl.program_id(1)

        @pl.when(d == 0)
        def _():
            _interleave_rows(u_ref, up_ref)

        bm = b_ref[0, 0].astype(BF16)
        for r0 in range(0, TA, RCH):
            s_ref[0, 0, r0:r0 + RCH, :] = jnp.dot(up_ref[r0:r0 + RCH, :].astype(BF16), bm, preferred_element_type=F32)
        lr = lam_ref[0, 0, :, 0:BW]
        li = lam_ref[0, 0, :, BW:2 * BW]
        zero = jnp.zeros((NSEG, BW), F32)
        get, put = _state_access(s_ref, (0, 0))
        mid = _seg_scan(get, put, T, CSEG, lr, li, d, (zero, zero))
        _seg_scan(get, put, 0, TSEG, lr, li, d, mid)
        cm = c_ref[0, 0].astype(BF16)
        for r0 in range(0, T, RCH):
            y = jnp.dot(s_ref[0, 0, r0:r0 + RCH, :].astype(BF16), cm, preferred_element_type=F32)

            @pl.when(d == 0)
            def _(y=y, r0=r0):
                yp_ref[r0:r0 + RCH, :] = y + dk_ref[...] * up_ref[r0:r0 + RCH, :]

            @pl.when(d == 1)
            def _(y=y, r0=r0):
                yp_ref[r0:r0 + RCH, :] += y

        @pl.when(d == 1)
        def _():
            _deinterleave_rows(yp_ref, ssm_ref, REGIONS[:1])
            for r0 in range(0, T, RCH):
                ge_ref[r0:r0 + RCH, :] = _gelu(ssm_ref[r0:r0 + RCH, :]).astype(ge_ref.dtype)

    blk4 = lambda shape: pl.BlockSpec((1, 1) + shape, lambda b, d: (d, b, 0, 0))
    return pl.pallas_call(
        kern,
        name="s5_fwd",
        grid=(NBLK, 2),
        in_specs=[pl.BlockSpec((TA, UW), lambda b, d: (0, UCOL0 + b)), pl.BlockSpec((1, UW), lambda b, d: (0, b)),
                  blk4((8, 2 * BW)), blk4((UW, 2 * BW)), blk4((2 * BW, UW))],
        out_specs=[blk4((TA, 2 * BW)), pl.BlockSpec((T, UW), lambda b, d: (0, b)), pl.BlockSpec((T, UW), lambda b, d: (0, b))],
        out_shape=[jax.ShapeDtypeStruct((2, NBLK, TA, 2 * BW), F32), jax.ShapeDtypeStruct((T, SW), F32),
                   jax.ShapeDtypeStruct((T, SW), BF16)],
        scratch_shapes=[pltpu.VMEM((TA, UW), F32), pltpu.VMEM((T, UW), F32)],
        compiler_params=_cparams(("parallel", "arbitrary")),
    )(proj, dskip, lam, bmat, cmat)


def _s5_bwd(d_ge, ssm, proj, dskip, states, lam, bmat, cmat):
    nt = (((1,), (1,)), ((), ()))
    tn = (((0,), (0,)), ((), ()))

    def kern(dge_ref, ssm_ref, u_ref, dk_ref, s_ref, lam_ref, b_ref, c_ref,
             du_ref, ddk_ref, dlam_ref, db_ref, dc_ref, g_ref, dua_ref, dssm_ref, up_ref, nat_ref):
        d = pl.program_id(1)

        @pl.when(d == 0)
        def _():
            ddk = jnp.zeros((1, UW), F32)
            for r0 in range(0, T, RCH):
                rs = slice(r0, r0 + RCH)
                _, pull = jax.vjp(_gelu, ssm_ref[rs, :])
                dssm = pull(dge_ref[rs, :])[0]
                nat_ref[rs, :] = dssm
                ddk = ddk + jnp.sum(dssm * u_ref[rs, :], axis=0, keepdims=True)
            ddk_ref[...] = ddk
            _interleave_rows(nat_ref, dssm_ref, REGIONS[:1])
            _interleave_rows(u_ref, up_ref)
            for r0 in range(0, T, RCH):
                dua_ref[r0:r0 + RCH, :] = dssm_ref[r0:r0 + RCH, :] * dk_ref[...]
            dua_ref[T:TA, :] = jnp.zeros((C, UW), F32)

        cm = c_ref[0, 0].astype(BF16)
        for r0 in range(0, T, RCH):
            g_ref[r0:r0 + RCH, :] = lax.dot_general(dssm_ref[r0:r0 + RCH, :].astype(BF16), cm, nt, preferred_element_type=F32)
        g_ref[T:TA, :] = jnp.zeros((C, 2 * BW), F32)
        lr = lam_ref[0, 0, :, 0:BW]
        li = lam_ref[0, 0, :, BW:2 * BW]
        zero = jnp.zeros((NSEG, BW), F32)
        get_g, put_g = _state_access(g_ref)

        mid = _seg_scan(get_g, put_g, 0, TSEG, lr, -li, 1 - d, (zero, zero))
        _seg_scan(get_g, put_g, T, CSEG, lr, -li, 1 - d, mid)

        get_s, _ = _state_access(s_ref, (0, 0))

        def dlam_terms(g, s):
            return g[0] * s[0] + g[1] * s[1], g[1] * s[0] - g[0] * s[1]

        def dlam_region(base, seglen, s_in, acc):
            def rows(j):
                return pl.ds(pl.multiple_of(base + j * NSEG, NSEG), NSEG)

            def inner(k, acc):
                j = jnp.where(d == 0, k, seglen - 1 - k)
                jp = jnp.where(d == 0, k - 1, seglen - k)
                t = dlam_terms(get_g(rows(j)), get_s(rows(jp)))
                return acc[0] + t[0], acc[1] + t[1]

            acc = lax.fori_loop(1, seglen, inner, acc)
            jb = jnp.where(d == 0, 0, seglen - 1)
            jn = jnp.where(d == 0, seglen - 1, 0)
            sp = get_s(rows(jn))
            t = dlam_terms(get_g(rows(jb)), (_shift_rows(sp[0], d, s_in[0]), _shift_rows(sp[1], d, s_in[1])))
            return acc[0] + t[0], acc[1] + t[1]

        r_mid = jnp.where(d == 0, TA - 1, T)
        s_mid = tuple(jnp.broadcast_to(t, (NSEG, BW)) for t in get_s(pl.ds(r_mid, 1)))
        acc = dlam_region(T, CSEG, (zero, zero), (zero, zero))
        acc = dlam_region(0, TSEG, s_mid, acc)
        dlam_ref[0, 0, :, 0:BW] = acc[0]
        dlam_ref[0, 0, :, BW:2 * BW] = acc[1]

        bm = b_ref[0, 0].astype(BF16)
        db = jnp.zeros((UW, 2 * BW), F32)
        dc = jnp.zeros((2 * BW, UW), F32)
        for r0 in range(0, TA, RCH):
            rs = slice(r0, r0 + RCH)
            g = g_ref[rs, :].astype(BF16)
            dua_ref[rs, :] += lax.dot_general(g, bm, nt, preferred_element_type=F32)
            db = db + lax.dot_general(up_ref[rs, :].astype(BF16), g, tn, preferred_element_type=F32)
            if r0 < T:
                dc = dc + lax.dot_general(s_ref[0, 0, rs, :].astype(BF16), dssm_ref[rs, :].astype(BF16), tn,
                                          preferred_element_type=F32)
        db_ref[0, 0] = db
        dc_ref[0, 0] = dc

        @pl.when(d == 1)
        def _():
            _deinterleave_rows(dua_ref, nat_ref)
            du_ref[...] = nat_ref[...].astype(du_ref.dtype)

    blk4 = lambda shape: pl.BlockSpec((1, 1) + shape, lambda b, d: (d, b, 0, 0))
    lat = pl.BlockSpec((T, UW), lambda b, d: (0, b))
    vec = pl.BlockSpec((1, UW), lambda b, d: (0, b))
    return pl.pallas_call(
        kern,
        name="s5_bwd",
        grid=(NBLK, 2),
        in_specs=[lat, lat, pl.BlockSpec((TA, UW), lambda b, d: (0, UCOL0 + b)), vec,
                  blk4((TA, 2 * BW)), blk4((8, 2 * BW)), blk4((UW, 2 * BW)), blk4((2 * BW, UW))],
        out_specs=[pl.BlockSpec((TA, UW), lambda b, d: (0, b)), vec, blk4((8, 2 * BW)), blk4((UW, 2 * BW)), blk4((2 * BW, UW))],
        out_shape=[jax.ShapeDtypeStruct((TA, SW), BF16), jax.ShapeDtypeStruct((1, SW), F32),
                   jax.ShapeDtypeStruct((2, NBLK, 8, 2 * BW), F32),
                   jax.ShapeDtypeStruct((2, NBLK, UW, 2 * BW), F32), jax.ShapeDtypeStruct((2, NBLK, 2 * BW, UW), F32)],
        scratch_shapes=[pltpu.VMEM((TA, 2 * BW), F32), pltpu.VMEM((TA, UW), F32), pltpu.VMEM((T, UW), F32),
                        pltpu.VMEM((TA, UW), F32), pltpu.VMEM((TA, UW), F32)],
        compiler_params=_cparams(("parallel", "arbitrary")),
    )(d_ge, ssm, proj, dskip, states, lam, bmat, cmat)


TR = 256


def _vjp_rows(f, primals, cots, n_row):
    _, pull = jax.vjp(f, *primals)
    g = pull(cots)
    return list(g[:n_row]), list(g[n_row:])


class _GradDict(dict):
    def __init__(self, on_set=None):
        super().__init__()
        self._on_set = on_set
        self.tokens = {}

    def __setitem__(self, key, value):
        super().__setitem__(key, value)
        if self._on_set is not None:
            self._on_set(self)

    def order(self, key):
        return self.tokens.get(key, self.get(key))


def _local_step(x, ctx, tgt, mod_lat, mod_ctx, wb, sp, on_grad=None, on_loss=None):
    sh1, sc1, g1, sh2, sc2, g2 = [mod_lat[:, i * D:(i + 1) * D] for i in range(6)]
    csh1, csc1 = mod_ctx[:, 0:D], mod_ctx[:, D:2 * D]
    tabs = _rope_tables()
    sink = sp["attn_sink"].reshape(1, NH)
    dskip = sp["ssm_d"].reshape(1, SW)
    lg_mix, lb_mix = sp["ln_mix_g"].reshape(1, D), sp["ln_mix_b"].reshape(1, D)
    lg_mlp, lb_mlp = sp["ln_mlp_g"].reshape(1, D), sp["ln_mlp_b"].reshape(1, D)
    b1, b2 = sp["b_mlp1"].reshape(1, DFF), sp["b_mlp2"].reshape(1, D)
    s5_names = ("ssm_a_re", "ssm_a_im", "ssm_log_dt", "ssm_b_re", "ssm_b_im", "ssm_c_re", "ssm_c_im")
    (lam, bmat, cmat), s5_pull = jax.vjp(_s5_prep, *[sp[n] for n in s5_names])

    def ln_mod(rv, vv):
        return [_f_ln_mod(rv[0], vv[0], vv[1])], []

    h_lat, = _rowwise(ln_mod, [(x, D, 0, 0)], [sc1, sh1], [(D, BF16)], [], nrows=T, tr=TR, name="ln1_lat")
    h_ctx, = _rowwise(ln_mod, [(ctx, D, 0, 0)], [csc1, csh1], [(D, BF16)], [], nrows=C, tr=TR, name="ln1_ctx")
    h1 = jnp.concatenate([h_lat, h_ctx], 0)
    proj = _matmul(h1, wb["w_in"], mode="nn", name="proj", tm=768, tn=512)
    attn = _attn_fwd(proj, sink, tabs)
    states, ssm, ge = _s5_fwd(proj, dskip, lam, bmat, cmat)
    z = _matmul(ge, wb["w_glu"], mode="nn", name="glu_mm", tm=1024, tn=1024)

    def glu_act(rv, vv):
        return [_f_glu(rv[0])], []

    glu, = _rowwise(glu_act, [(z, 2 * SW, 0, 0)], [], [(SW, BF16)], [], nrows=T, tr=TR, name="glu_act")
    attn_d = _matmul(attn, wb["w_attn_up"], mode="nn", name="attn_up", tm=1024, tn=512)
    ssm_d = _matmul(glu, wb["w_ssm_up"], mode="nn", name="ssm_up", tm=1024, tn=512)
    ga_cb, gs_cb = (QW + 2 * KVW + SW) // D, (QW + 2 * KVW + SW) // D + 1

    def mix(rv, vv):
        return [_f_mix(*rv)], []

    mixv, = _rowwise(mix, [(proj, D, ga_cb, 0), (proj, D, gs_cb, 0), (attn_d, D, 0, 0), (ssm_d, D, 0, 0)], [],
                     [(D, BF16)], [], nrows=T, tr=TR, name="mix")
    y = _matmul(mixv, wb["w_out"], mode="nn", name="out_proj", tm=1024, tn=512)

    def post1(rv, vv):
        x1, h2 = _f_post1(rv[0], rv[1], *vv)
        return [x1, h2], []

    x1, h2 = _rowwise(post1, [(x, D, 0, 0), (y, D, 0, 0)], [g1, lg_mix, lb_mix, sc2, sh2],
                      [(D, F32), (D, BF16)], [], nrows=T, tr=TR, name="post1")

    def relu_sq(acc):
        r = jnp.maximum(acc, 0.0)
        return r, r * r

    r_act, act = _matmul(h2, wb["w_mlp1"], mode="nn", name="mlp1", tm=1024, tn=512, bias=b1,
                         out_dtypes=(BF16, BF16), epilogue=relu_sq)
    mlp = _matmul(act, wb["w_mlp2"], mode="nn", name="mlp2", tm=1024, tn=512, tk=2048)

    def loss_fb(rv, vv):
        x1_t, mlp_t, tgt_t = rv
        g2_v, lg_v, lb_v, b2_v = vv
        f = lambda a, m, g, p, q, b: _f_loss(a, m, tgt_t, g, p, q, b)
        val, grads = jax.value_and_grad(f, argnums=(0, 1, 2, 3, 4, 5))(x1_t, mlp_t, g2_v, lg_v, lb_v, b2_v)
        dx1, dmlp, dg2, dlg, dlb, db2 = grads
        return [dx1, dmlp], [jnp.reshape(val, (1, 1)), dg2, dlg, dlb, db2]

    dx1_a, d_mlp, loss_p, d_g2, d_lg_mlp, d_lb_mlp, d_b2 = _rowwise(
        loss_fb, [(x1, D, 0, 0), (mlp, D, 0, 0), (tgt, D, 0, 0)], [g2, lg_mlp, lb_mlp, b2],
        [(D, F32), (D, BF16)], [(1, 1), (1, D), (1, D), (1, D), (1, D)], nrows=T, tr=TR, name="loss_fb")

    gw = _GradDict(on_grad)
    loss_done = () if on_loss is None else (on_loss(loss_p),)
    gw["w_mlp2"] = _matmul(act, d_mlp, mode="tn", name="dw_mlp2", out_dtypes=(BF16,), tm=512, tn=1024, tk=1024, after=loss_done)
    da, = (_matmul(d_mlp, wb["w_mlp2"], mode="nt", name="d_act", out_dtypes=(BF16,), tm=1024, tn=512,
                   extras=(r_act,), epilogue=lambda acc, r: (acc * (2.0 * r.astype(F32)),), after=(gw.order("w_mlp2"),)),)
    ones = jnp.ones((8, T), BF16)
    d_b1 = _matmul(ones, da, mode="nn", name="db_mlp1", tm=8, tn=2048)[0:1]
    gw["w_mlp1"] = _matmul(h2, da, mode="tn", name="dw_mlp1", out_dtypes=(BF16,), tm=512, tn=1024, tk=1024)
    dh2 = _matmul(da, wb["w_mlp1"], mode="nt", name="d_h2", tm=1024, tn=512, tk=2048, after=(gw.order("w_mlp1"),))

    def post1_b(rv, vv):
        x_t, y_t, dx1_t, dh2_t = rv
        gr, gv = _vjp_rows(_f_post1, (x_t, y_t, *vv), (dx1_t, dh2_t), 2)
        return [gr[0], gr[1]], gv

    dx_a, dy, d_g1, d_lg_mix, d_lb_mix, d_sc2, d_sh2 = _rowwise(
        post1_b, [(x, D, 0, 0), (y, D, 0, 0), (dx1_a, D, 0, 0), (dh2, D, 0, 0)], [g1, lg_mix, lb_mix, sc2, sh2],
        [(D, F32), (D, BF16)], [(1, D)] * 5, nrows=T, tr=TR, name="post1_bwd")
    gw["w_out"] = _matmul(mixv, dy, mode="tn", name="dw_out", out_dtypes=(BF16,), tm=512, tn=1024, tk=1024)
    dmix = _matmul(dy, wb["w_out"], mode="nt", name="d_mix", tm=1024, tn=512, after=(gw.order("w_out"),))

    def mix_b(rv, vv):
        gr, _ = _vjp_rows(_f_mix, tuple(rv[:4]), rv[4], 4)
        return gr, []

    d_ga, d_gs, d_attn_d, d_ssm_d = _rowwise(
        mix_b, [(proj, D, ga_cb, 0), (proj, D, gs_cb, 0), (attn_d, D, 0, 0), (ssm_d, D, 0, 0), (dmix, D, 0, 0)], [],
        [(D, BF16)] * 4, [], nrows=T, tr=TR, name="mix_bwd")
    gw["w_attn_up"] = _matmul(attn, d_attn_d, mode="tn", name="dw_attn_up", out_dtypes=(BF16,), tm=512, tn=1024, tk=1024)
    d_attn = _matmul(d_attn_d, wb["w_attn_up"], mode="nt", name="d_attn", out_dtypes=(BF16,), tm=1024, tn=512)
    gw["w_ssm_up"] = _matmul(glu, d_ssm_d, mode="tn", name="dw_ssm_up", out_dtypes=(BF16,), tm=512, tn=1024, tk=1024)
    d_glu = _matmul(d_ssm_d, wb["w_ssm_up"], mode="nt", name="d_glu", tm=1024, tn=512, after=(gw.order("w_attn_up"), gw.order("w_ssm_up")))

    def glu_b(rv, vv):
        gr, _ = _vjp_rows(_f_glu, (rv[0],), rv[1], 1)
        return gr, []

    dz, = _rowwise(glu_b, [(z, 2 * SW, 0, 0), (d_glu, SW, 0, 0)], [], [(2 * SW, BF16)], [], nrows=T, tr=TR, name="glu_bwd")
    gw["w_glu"] = _matmul(ge, dz, mode="tn", name="dw_glu", out_dtypes=(BF16,), tm=512, tn=1024, tk=1024)
    d_ge = _matmul(dz, wb["w_glu"], mode="nt", name="d_ge", tm=1024, tn=512, after=(gw.order("w_glu"),))

    du_all, d_dskip, dlam, dbmat, dcmat = _s5_bwd(d_ge, ssm, proj, dskip, states, lam, bmat, cmat)
    s5_grads = s5_pull((dlam, dbmat, dcmat))

    dq, dk, dv, dsink = _attn_bwd(proj, d_attn, sink, tabs)
    zc = lambda w: jnp.zeros((C, w), BF16)
    dproj = jnp.concatenate([
        jnp.concatenate([dq, zc(QW)], 0), dk, dv, du_all,
        jnp.concatenate([d_ga, zc(D)], 0), jnp.concatenate([d_gs, zc(D)], 0)], 1)
    gw["w_in"] = _matmul(h1, dproj, mode="tn", name="dw_in", out_dtypes=(BF16,), tm=512, tn=1536, tk=768)
    dh1 = _matmul(dproj, wb["w_in"], mode="nt", name="d_h1", tm=768, tn=512, tk=2048, after=(gw.order("w_in"),))

    def ln1_b(rv, vv):
        x_t, dh_t, dxa_t = rv
        gr, gv = _vjp_rows(_f_ln_mod, (x_t, vv[0], vv[1]), dh_t, 1)
        return [gr[0] + dxa_t], gv

    grad_x, d_sc1, d_sh1 = _rowwise(ln1_b, [(x, D, 0, 0), (dh1, D, 0, 0), (dx_a, D, 0, 0)], [sc1, sh1],
                                    [(D, F32)], [(1, D), (1, D)], nrows=T, tr=TR, name="ln1_lat_bwd")

    def ln1c_b(rv, vv):
        _, gv = _vjp_rows(_f_ln_mod, (rv[0], vv[0], vv[1]), rv[1], 1)
        return [], gv

    d_csc1, d_csh1 = _rowwise(ln1c_b, [(ctx, D, 0, 0), (dh1, D, 0, T // TR)], [csc1, csh1],
                              [], [(1, D), (1, D)], nrows=C, tr=TR, name="ln1_ctx_bwd")

    d_mod_lat = jnp.concatenate([d_sh1, d_sc1, d_g1, d_sh2, d_sc2, d_g2], 1)
    zv = jnp.zeros((1, D), F32)
    d_mod_ctx = jnp.concatenate([d_csh1, d_csc1, zv, zv, zv, zv], 1)
    gs = {n: g for n, g in zip(s5_names, s5_grads)}
    gs["attn_sink"] = dsink[:, 0]
    gs["ssm_d"] = d_dskip
    gs["ln_mix_g"], gs["ln_mix_b"] = d_lg_mix, d_lb_mix
    gs["ln_mlp_g"], gs["ln_mlp_b"] = d_lg_mlp, d_lb_mlp
    gs["b_mlp1"], gs["b_mlp2"] = d_b1, d_b2
    return loss_p, grad_x, d_mod_lat, d_mod_ctx, gw, gs


def _my_pos():
    return lax.axis_index("x"), lax.axis_index("y"), lax.axis_index("c")


def _flip(p, bit):
    return 1 - p if bit else p


def _peer(pos, k):
    x, y, c = pos
    return (_flip(x, (k >> 2) & 1), _flip(y, (k >> 1) & 1), _flip(c, k & 1))


def _lin(pos):
    return 4 * pos[0] + 2 * pos[1] + pos[2]


def _allgather_small(v, name):
    r, w = v.shape

    def body(v_ref, out_ref, send_sems, recv_sems, local_sem):
        me = _my_pos()
        mine = pltpu.make_async_copy(v_ref, out_ref.at[_lin(me)], local_sem)
        mine.start()
        sends = []
        for k in range(1, N_DEV):
            cp = pltpu.make_async_remote_copy(src_ref=v_ref, dst_ref=out_ref.at[_lin(me)], send_sem=send_sems.at[k - 1],
                                              recv_sem=recv_sems.at[k - 1], device_id=_peer(me, k), device_id_type=MESH)
            cp.start()
            sends.append(cp)
        for k in range(1, N_DEV):
            peer = _peer(me, k)
            pltpu.make_async_remote_copy(src_ref=v_ref, dst_ref=out_ref.at[_lin(peer)], send_sem=send_sems.at[k - 1],
                                         recv_sem=recv_sems.at[k - 1], device_id=peer, device_id_type=MESH).wait_recv()
        for cp in sends:
            cp.wait_send()
        mine.wait()

    return pl.pallas_call(
        body,
        name=name,
        out_shape=jax.ShapeDtypeStruct((N_DEV, r, w), v.dtype),
        in_specs=[pl.BlockSpec(memory_space=pltpu.VMEM)],
        out_specs=pl.BlockSpec(memory_space=pltpu.VMEM),
        scratch_shapes=[pltpu.SemaphoreType.DMA((N_DEV - 1,)), pltpu.SemaphoreType.DMA((N_DEV - 1,)), pltpu.SemaphoreType.DMA],
        compiler_params=pltpu.CompilerParams(vmem_limit_bytes=VMEM_LIMIT_BYTES),
    )(v)


def _block_of(ref, kind, idx, n):
    start = pl.multiple_of(idx * n, 128)
    if kind == "col":
        return ref.at[:, pl.ds(start, n)]
    return ref.at[pl.ds(start, n), :]


def _allgather_weights(shards, kinds):
    nt = len(shards)
    out_shape = []
    for s, kind in zip(shards, kinds):
        k, n = s.shape
        out_shape.append(jax.ShapeDtypeStruct((k, n * N_DEV) if kind == "col" else (k * N_DEV, n), s.dtype))

    def body(*refs):
        ins, outs = refs[:nt], refs[nt:2 * nt]
        send_sems, recv_sems, local_sems = refs[2 * nt:]
        x, y, c = _my_pos()
        me, sibling = (x, y, c), (x, y, 1 - c)
        chips = [(1 - x, y), (x, 1 - y), (1 - x, 1 - y)]

        def blk(t, pos):
            n = shards[t].shape[1] if kinds[t] == "col" else shards[t].shape[0]
            return _block_of(outs[t], kinds[t], _lin(pos), n)

        def copy(t, k, block, to, src=None):
            return pltpu.make_async_remote_copy(src_ref=blk(t, block) if src is None else src, dst_ref=blk(t, block),
                                                send_sem=send_sems.at[t, k], recv_sem=recv_sems.at[t, k],
                                                device_id=to, device_id_type=MESH)

        local, sends = [], []
        for t in range(nt):
            mine = pltpu.make_async_copy(ins[t], blk(t, me), local_sems.at[t])
            mine.start()
            local.append(mine)
            first = [copy(t, 0, me, sibling, src=ins[t])]
            first += [copy(t, 1 + j, me, (*chip, c), src=ins[t]) for j, chip in enumerate(chips)]
            for cp in first:
                cp.start()
            sends += first
        for t in range(nt):
            for j, chip in enumerate(chips):
                copy(t, 1 + j, (*chip, c), me).wait_recv()
                fwd = copy(t, 4 + j, (*chip, c), sibling)
                fwd.start()
                sends.append(fwd)
        for t in range(nt):
            copy(t, 0, sibling, me).wait_recv()
            for j, chip in enumerate(chips):
                copy(t, 4 + j, (*chip, 1 - c), me).wait_recv()
        for cp in sends:
            cp.wait_send()
        for cp in local:
            cp.wait()

    any_spec = pl.BlockSpec(memory_space=pl.ANY)
    return pl.pallas_call(
        body,
        name="allgather_weights",
        out_shape=out_shape,
        in_specs=[any_spec] * nt,
        out_specs=[any_spec] * nt,
        scratch_shapes=[pltpu.SemaphoreType.DMA((nt, N_DEV - 1)), pltpu.SemaphoreType.DMA((nt, N_DEV - 1)),
                        pltpu.SemaphoreType.DMA((nt,))],
    )(*shards)


def _handshake(peers):
    barrier = pltpu.get_barrier_semaphore()
    for peer in peers:
        pl.semaphore_signal(barrier, inc=1, device_id=peer, device_id_type=MESH)
    pl.semaphore_wait(barrier, len(peers))


def _allgather_weights_seq(shards, kinds, name, collective_id):
    nt = len(shards)
    hbm = pltpu.MemorySpace.HBM
    ins = [jax.new_ref(s, memory_space=hbm) for s in shards]
    outs = []
    for s, kind in zip(shards, kinds):
        k, n = s.shape
        shape = (k, n * N_DEV) if kind == "col" else (k * N_DEV, n)
        outs.append(jax.empty_ref(jax.ShapeDtypeStruct(shape, s.dtype), memory_space=hbm))

    @functools.partial(
        pl.kernel, mesh=plsc.ScalarSubcoreMesh(axis_name="seq", num_cores=1), name=name,
        scratch_types=(pltpu.SemaphoreType.DMA((nt, N_DEV - 1)), pltpu.SemaphoreType.DMA((nt, N_DEV - 1)),
                       pltpu.SemaphoreType.DMA((nt,))),
        compiler_params=pltpu.CompilerParams(collective_id=collective_id))
    def launch(send_sems, recv_sems, local_sems):
        x, y, c = _my_pos()
        me, sibling = (x, y, c), (x, y, 1 - c)
        chips = [(1 - x, y), (x, 1 - y), (1 - x, 1 - y)]
        _handshake([sibling] + [(*chip, c) for chip in chips])

        def blk(t, pos):
            n = shards[t].shape[1] if kinds[t] == "col" else shards[t].shape[0]
            return _block_of(outs[t], kinds[t], _lin(pos), n)

        def copy(t, k, block, to, src=None):
            return pltpu.make_async_remote_copy(src_ref=blk(t, block) if src is None else src, dst_ref=blk(t, block),
                                                send_sem=send_sems.at[t, k], recv_sem=recv_sems.at[t, k],
                                                device_id=to, device_id_type=MESH)

        local, sends = [], []
        for t in range(nt):
            mine = pltpu.make_async_copy(ins[t], blk(t, me), local_sems.at[t])
            mine.start()
            local.append(mine)
            first = [copy(t, 0, me, sibling, src=ins[t])]
            first += [copy(t, 1 + j, me, (*chip, c), src=ins[t]) for j, chip in enumerate(chips)]
            for cp in first:
                cp.start()
            sends += first
        for t in range(nt):
            for j, chip in enumerate(chips):
                copy(t, 1 + j, (*chip, c), me).wait_recv()
                fwd = copy(t, 4 + j, (*chip, c), sibling)
                fwd.start()
                sends.append(fwd)
        for t in range(nt):
            copy(t, 0, sibling, me).wait_recv()
            for j, chip in enumerate(chips):
                copy(t, 4 + j, (*chip, 1 - c), me).wait_recv()
        for cp in sends:
            cp.wait_send()
        for cp in local:
            cp.wait()

    launch()
    return [o[...] for o in outs]


def _allgather_small_seq(v, name, collective_id):
    hbm = pltpu.MemorySpace.HBM
    src = jax.new_ref(v, memory_space=hbm)
    out = jax.empty_ref(jax.ShapeDtypeStruct((N_DEV,) + v.shape, v.dtype), memory_space=hbm)

    @functools.partial(
        pl.kernel, mesh=plsc.ScalarSubcoreMesh(axis_name="seq", num_cores=1), name=name,
        scratch_types=(pltpu.SemaphoreType.DMA((N_DEV - 1,)), pltpu.SemaphoreType.DMA((N_DEV - 1,)), pltpu.SemaphoreType.DMA),
        compiler_params=pltpu.CompilerParams(collective_id=collective_id))
    def launch(send_sems, recv_sems, local_sem):
        me = _my_pos()
        _handshake([_peer(me, k) for k in range(1, N_DEV)])
        mine = pltpu.make_async_copy(src, out.at[_lin(me)], local_sem)
        mine.start()
        sends = []
        for k in range(1, N_DEV):
            cp = pltpu.make_async_remote_copy(src_ref=src, dst_ref=out.at[_lin(me)], send_sem=send_sems.at[k - 1],
                                              recv_sem=recv_sems.at[k - 1], device_id=_peer(me, k), device_id_type=MESH)
            cp.start()
            sends.append(cp)
        for k in range(1, N_DEV):
            peer = _peer(me, k)
            pltpu.make_async_remote_copy(src_ref=src, dst_ref=out.at[_lin(peer)], send_sem=send_sems.at[k - 1],
                                         recv_sem=recv_sems.at[k - 1], device_id=peer, device_id_type=MESH).wait_recv()
        for cp in sends:
            cp.wait_send()
        mine.wait()

    launch()
    return out[...]


def _scatter_grads_seq(grads, kinds, name, collective_id):
    nt = len(grads)
    hbm = pltpu.MemorySpace.HBM
    shard_shapes = []
    for g, kind in zip(grads, kinds):
        k, n = g.shape
        shard_shapes.append((k, n // N_DEV) if kind == "col" else (k // N_DEV, n))
    ins = [jax.new_ref(g, memory_space=hbm) for g in grads]
    outs = [jax.empty_ref(jax.ShapeDtypeStruct((N_DEV,) + s, g.dtype), memory_space=hbm) for s, g in zip(shard_shapes, grads)]

    @functools.partial(
        pl.kernel, mesh=plsc.ScalarSubcoreMesh(axis_name="seq", num_cores=1), name=name,
        scratch_types=(pltpu.SemaphoreType.DMA((nt, N_DEV - 1)), pltpu.SemaphoreType.DMA((nt, N_DEV - 1)),
                       pltpu.SemaphoreType.DMA((nt,))),
        compiler_params=pltpu.CompilerParams(collective_id=collective_id))
    def launch(send_sems, recv_sems, local_sems):
        me = _my_pos()
        _handshake([_peer(me, k) for k in range(1, N_DEV)])

        def blk(t, pos):
            n = shard_shapes[t][1] if kinds[t] == "col" else shard_shapes[t][0]
            return _block_of(ins[t], kinds[t], _lin(pos), n)

        local, sends = [], []
        for t in range(nt):
            cp = pltpu.make_async_copy(blk(t, me), outs[t].at[_lin(me)], local_sems.at[t])
            cp.start()
            local.append(cp)
            for k in range(1, N_DEV):
                peer = _peer(me, k)
                cp = pltpu.make_async_remote_copy(src_ref=blk(t, peer), dst_ref=outs[t].at[_lin(me)], send_sem=send_sems.at[t, k - 1],
                                                  recv_sem=recv_sems.at[t, k - 1], device_id=peer, device_id_type=MESH)
                cp.start()
                sends.append(cp)
        for t in range(nt):
            for k in range(1, N_DEV):
                peer = _peer(me, k)
                pltpu.make_async_remote_copy(src_ref=blk(t, me), dst_ref=outs[t].at[_lin(peer)], send_sem=send_sems.at[t, k - 1],
                                             recv_sem=recv_sems.at[t, k - 1], device_id=peer, device_id_type=MESH).wait_recv()
        for cp in sends:
            cp.wait_send()
        for cp in local:
            cp.wait()

    launch()
    return [o[...] for o in outs]


_HBM_SPEC = pl.BlockSpec(memory_space=pltpu.HBM)
_SEM_SPEC = pl.BlockSpec(memory_space=pltpu.SEMAPHORE)
_EFFECT = pltpu.SideEffectType.DATAFLOW_SIDE_EFFECTING
LOCAL_CHUNKS = 16


def _shard_shapes(grads, kinds):
    return [(g.shape[0], g.shape[1] // N_DEV) if kind == "col" else (g.shape[0] // N_DEV, g.shape[1]) for g, kind in zip(grads, kinds)]


def _scatter_copies(g_refs, land_refs, send_sems, recv_sems, kinds, shard_shapes):
    me = _my_pos()
    copies = []
    for t in range(len(g_refs)):
        n = shard_shapes[t][1] if kinds[t] == "col" else shard_shapes[t][0]
        for k in range(1, N_DEV):
            peer = _peer(me, k)
            copies.append(pltpu.make_async_remote_copy(
                src_ref=_block_of(g_refs[t], kinds[t], _lin(peer), n), dst_ref=land_refs[t].at[_lin(me)],
                send_sem=send_sems.at[t * (N_DEV - 1) + k - 1], recv_sem=recv_sems.at[t * (N_DEV - 1) + k - 1],
                device_id=peer, device_id_type=MESH))
    return copies


def _scatter_start(grads, kinds, name):
    nt = len(grads)
    shard_shapes = _shard_shapes(grads, kinds)

    def body(*refs):
        g_refs, land_refs = refs[:nt], refs[nt:2 * nt]
        send_sems, recv_sems = refs[2 * nt], refs[2 * nt + 1]
        token = refs[2 * nt + 2 + 2 * nt]
        local_sems = refs[-1]
        me = _my_pos()
        local = []
        for t in range(nt):
            n = shard_shapes[t][1] if kinds[t] == "col" else shard_shapes[t][0]
            src, dst = _block_of(g_refs[t], kinds[t], _lin(me), n), land_refs[t].at[_lin(me)]
            rows = shard_shapes[t][0] // LOCAL_CHUNKS
            for ch in range(LOCAL_CHUNKS):
                rs = pl.ds(ch * rows, rows)
                cp = pltpu.make_async_copy(src.at[rs, :], dst.at[rs, :], local_sems.at[t * LOCAL_CHUNKS + ch])
                cp.start()
                local.append(cp)
        token[...] = jnp.zeros_like(token)
        for cp in local:
            cp.wait()
        for cp in _scatter_copies(g_refs, land_refs, send_sems, recv_sems, kinds, shard_shapes):
            cp.start()

    lands = [pltpu.with_memory_space_constraint(lax.empty((N_DEV,) + s, g.dtype), pltpu.HBM) for s, g in zip(shard_shapes, grads)]
    sem_shape = pltpu.SemaphoreType.DMA((nt * (N_DEV - 1),))
    out = pl.pallas_call(
        body,
        name=name,
        out_shape=(sem_shape, sem_shape, *[pltpu.HBM(g.shape, g.dtype) for g in grads],
                   *[pltpu.HBM(l.shape, l.dtype) for l in lands], jax.ShapeDtypeStruct((8, 128), F32)),
        in_specs=[_HBM_SPEC] * (2 * nt),
        out_specs=(_SEM_SPEC, _SEM_SPEC, *[_HBM_SPEC] * (2 * nt), pl.BlockSpec(memory_space=pltpu.VMEM)),
        input_output_aliases={i: 2 + i for i in range(2 * nt)},
        scratch_shapes=[pltpu.SemaphoreType.DMA((nt * LOCAL_CHUNKS,))],
        compiler_params=pltpu.CompilerParams(has_side_effects=_EFFECT),
    )(*[pltpu.with_memory_space_constraint(g, pltpu.HBM) for g in grads], *lands)
    return out[0], out[1], list(out[2:2 + nt]), list(out[2 + nt:2 + 2 * nt]), out[-1]


def _scatter_wait(send_sems, recv_sems, g_thru, land_thru, kinds, after, name):
    nt = len(g_thru)
    shard_shapes = _shard_shapes(g_thru, kinds)

    def body(*refs):
        g_refs, land_refs = refs[:nt], refs[nt:2 * nt]
        send_sems, recv_sems = refs[2 * nt], refs[2 * nt + 1]
        for cp in _scatter_copies(g_refs, land_refs, send_sems, recv_sems, kinds, shard_shapes):
            cp.wait_send()
            cp.wait_recv()

    out = pl.pallas_call(
        body,
        name=name,
        out_shape=tuple(pltpu.HBM(a.shape, a.dtype) for a in (*g_thru, *land_thru)),
        in_specs=[*[_HBM_SPEC] * (2 * nt), _SEM_SPEC, _SEM_SPEC, pl.BlockSpec(memory_space=pl.ANY)],
        out_specs=tuple([_HBM_SPEC] * (2 * nt)),
        input_output_aliases={i: i for i in range(2 * nt)},
        compiler_params=pltpu.CompilerParams(has_side_effects=_EFFECT),
    )(*g_thru, *land_thru, send_sems, recv_sems, after)
    return list(out[nt:])


def _scatter_grads(grads, kinds):
    nt = len(grads)
    shard_shapes = []
    for g, kind in zip(grads, kinds):
        k, n = g.shape
        shard_shapes.append((k, n // N_DEV) if kind == "col" else (k // N_DEV, n))

    def body(*refs):
        ins, outs = refs[:nt], refs[nt:2 * nt]
        send_sems, recv_sems, local_sems = refs[2 * nt:]
        me = _my_pos()

        def blk(t, pos):
            n = shard_shapes[t][1] if kinds[t] == "col" else shard_shapes[t][0]
            return _block_of(ins[t], kinds[t], _lin(pos), n)

        local, sends = [], []
        for t in range(nt):
            cp = pltpu.make_async_copy(blk(t, me), outs[t].at[_lin(me)], local_sems.at[t])
            cp.start()
            local.append(cp)
            for k in range(1, N_DEV):
                peer = _peer(me, k)
                cp = pltpu.make_async_remote_copy(src_ref=blk(t, peer), dst_ref=outs[t].at[_lin(me)], send_sem=send_sems.at[t, k - 1],
                                                  recv_sem=recv_sems.at[t, k - 1], device_id=peer, device_id_type=MESH)
                cp.start()
                sends.append(cp)
        for t in range(nt):
            for k in range(1, N_DEV):
                peer = _peer(me, k)
                pltpu.make_async_remote_copy(src_ref=blk(t, me), dst_ref=outs[t].at[_lin(peer)], send_sem=send_sems.at[t, k - 1],
                                             recv_sem=recv_sems.at[t, k - 1], device_id=peer, device_id_type=MESH).wait_recv()
        for cp in sends:
            cp.wait_send()
        for cp in local:
            cp.wait()

    any_spec = pl.BlockSpec(memory_space=pl.ANY)
    return pl.pallas_call(
        body,
        name="scatter_grads",
        out_shape=[jax.ShapeDtypeStruct((N_DEV,) + s, g.dtype) for s, g in zip(shard_shapes, grads)],
        in_specs=[any_spec] * nt,
        out_specs=[any_spec] * nt,
        scratch_shapes=[pltpu.SemaphoreType.DMA((nt, N_DEV - 1)), pltpu.SemaphoreType.DMA((nt, N_DEV - 1)),
                        pltpu.SemaphoreType.DMA((nt,))],
    )(*grads)


def _adam(g_slots, w, m, v, *, tr, name, after=()):
    ns, r, wd = g_slots.shape
    tr = min(tr, r)
    assert r % tr == 0, (name, r, tr)
    c1 = 1.0 - ADAM_B1 ** ADAM_STEP
    c2 = 1.0 - ADAM_B2 ** ADAM_STEP
    n_after = len(after)

    def kern(g_ref, w_ref, m_ref, v_ref, *rest):
        go_ref, d_ref, mo_ref, vo_ref = rest[n_after:]
        g = g_ref[0].astype(F32)
        for s in range(1, ns):
            g = g + g_ref[s].astype(F32)
        m_new = ADAM_B1 * m_ref[...] + (1.0 - ADAM_B1) * g
        v_new = ADAM_B2 * v_ref[...] + (1.0 - ADAM_B2) * (g * g)
        m_hat = m_new / c1
        v_hat = v_new / c2
        go_ref[...] = g
        d_ref[...] = -ADAM_LR * (m_hat / (jnp.sqrt(v_hat) + ADAM_EPS) + ADAM_WD * w_ref[...])
        mo_ref[...] = m_new
        vo_ref[...] = v_new

    tile = pl.BlockSpec((tr, wd), lambda i: (i, 0))
    return pl.pallas_call(
        kern,
        name=name,
        grid=(r // tr,),
        in_specs=[pl.BlockSpec((ns, tr, wd), lambda i: (0, i, 0)), tile, tile, tile] + [pl.BlockSpec(memory_space=pl.ANY)] * n_after,
        out_specs=[tile] * 4,
        out_shape=[jax.ShapeDtypeStruct((r, wd), F32)] * 4,
        compiler_params=_cparams(("parallel",)),
    )(g_slots, w, m, v, *after)


SMALL = ("c_ctx", "b_ada", "attn_sink", "ssm_a_re", "ssm_a_im", "ssm_log_dt", "ssm_b_re", "ssm_b_im", "ssm_c_re", "ssm_c_im",
         "ssm_d", "ln_mix_g", "ln_mix_b", "b_mlp1", "b_mlp2", "ln_mlp_g", "ln_mlp_b")
BIG = ("w_in", "w_glu", "w_attn_up", "w_ssm_up", "w_out", "w_mlp1", "w_mlp2")
BIG_KIND = ("col", "col", "col", "col", "row", "col", "row")
AG_GROUPS = (("w_in",), ("w_glu", "w_attn_up", "w_ssm_up", "w_out"), ("w_mlp1",), ("w_mlp2",))
AG_COLLECTIVE_ID0 = 1
RS_GROUPS = (("w_mlp2",), ("w_mlp1",), ("w_out", "w_attn_up", "w_ssm_up", "w_glu"), ("w_in",))
RS_COLLECTIVE_ID0 = AG_COLLECTIVE_ID0 + len(AG_GROUPS)
SMALL_EARLY = ("ssm_a_re", "ssm_a_im", "ssm_log_dt", "ssm_b_re", "ssm_b_im", "ssm_c_re", "ssm_c_im", "ssm_d")
SMALL_LATE = tuple(n for n in SMALL if n not in SMALL_EARLY)
SMALL_COLLECTIVE_ID0 = RS_COLLECTIVE_ID0 + len(RS_GROUPS)
LANES = 128


def _pack(parts):
    rows = []
    for p in parts:
        flat = p.reshape(-1).astype(F32)
        pad = (-flat.shape[0]) % LANES
        rows.append(jnp.pad(flat, (0, pad)).reshape(-1, LANES))
    packed = jnp.concatenate(rows, 0)
    return jnp.pad(packed, ((0, (-packed.shape[0]) % 8), (0, 0)))


def _unpack(packed, shapes):
    out, r0 = [], 0
    for s in shapes:
        n = math.prod(s)
        nr = -(-n // LANES)
        out.append(packed[r0:r0 + nr].reshape(-1)[:n].reshape(s))
        r0 += nr
    return out


WEIGHTS = ("c_ctx", "w_ada", "b_ada", "w_in", "attn_sink", "ssm_a_re", "ssm_a_im", "ssm_log_dt", "ssm_b_re", "ssm_b_im",
           "ssm_c_re", "ssm_c_im", "ssm_d", "w_glu", "w_attn_up", "w_ssm_up", "w_out", "ln_mix_g", "ln_mix_b", "w_mlp1",
           "b_mlp1", "w_mlp2", "b_mlp2", "ln_mlp_g", "ln_mlp_b")
ADA_COLS = 6 * D // N_DEV


def _step(x, c, ctx, loss_target, p, m, v):
    me = _lin(_my_pos())
    x2, ctx2, tgt2 = x[0], ctx[0], loss_target[0]

    wb = {}
    for gi, group in enumerate(AG_GROUPS):
        full = _allgather_weights_seq([p[n][0].astype(BF16) for n in group], [BIG_KIND[BIG.index(n)] for n in group],
                                      "allgather_seq%d" % gi, AG_COLLECTIVE_ID0 + gi)
        wb.update(zip(group, full))

    c_all = _allgather_small(jnp.broadcast_to(c, (8, D)), "gather_c")[:, 0, :]
    cc = p["c_ctx"].reshape(1, D)
    s_in = jnp.concatenate([c_all, cc, jnp.zeros((7, D), F32)], 0)
    s_act, = _rowwise(lambda rv, vv: ([_silu(rv[0])], []), [(s_in, D, 0, 0)], [], [(D, F32)], [], nrows=16, tr=16, name="silu_c")
    b_mine = lax.dynamic_slice_in_dim(p["b_ada"], me * ADA_COLS, ADA_COLS, axis=1)
    mod_part = _matmul(s_act, p["w_ada"][0], mode="nn", name="ada_fwd", tm=16, tn=512, bias=b_mine)
    mod_all = _allgather_small(mod_part, "gather_mod")
    mod_lat = lax.dynamic_index_in_dim(mod_all, me, axis=1, keepdims=False).reshape(1, 6 * D)
    mod_ctx = mod_all[:, 8, :].reshape(1, 6 * D)

    sp = {n: p[n][0] for n in SMALL if n not in ("c_ctx", "b_ada")}
    recv = {}

    def on_grad(gw):
        for gi, group in enumerate(RS_GROUPS):
            if group[0] not in recv and all(n in gw for n in group):
                slots = _scatter_grads_seq([gw[n] for n in group], [BIG_KIND[BIG.index(n)] for n in group],
                                           "scatter_seq%d" % gi, RS_COLLECTIVE_ID0 + gi)
                recv.update(zip(group, slots))

    total = {}

    def on_loss(loss_p):
        total["loss"] = lax.psum(loss_p[0, 0], ("x", "y", "c"))
        return total["loss"].reshape(1, 1)

    loss_p, grad_x, d_mod_lat, d_mod_ctx, gw, gs = _local_step(x2, ctx2, tgt2, mod_lat, mod_ctx, wb, sp, on_grad, on_loss)

    res = {}
    last = ()
    for group in RS_GROUPS:
        for n in group:
            res[n] = _adam(recv[n], p[n][0], m[n][0], v[n][0], tr=256, name="adam_" + n, after=last)
            last = (res[n][0],)

    g_early = _allgather_small_seq(_pack([gs[n] for n in SMALL_EARLY]), "gather_small_early", SMALL_COLLECTIVE_ID0)

    dm = jnp.concatenate([d_mod_lat, d_mod_ctx, jnp.zeros((6, 6 * D), F32)], 0)
    dm_all = _allgather_small_seq(dm, "gather_dmod", SMALL_COLLECTIVE_ID0 + 1)
    dm_all = lax.optimization_barrier((dm_all,) + last)[0]
    dm2 = jnp.concatenate([dm_all[:, 0, :], dm_all[:, 1, :]], 0)
    dm2_mine = lax.dynamic_slice_in_dim(dm2, me * ADA_COLS, ADA_COLS, axis=1)
    s2 = jnp.concatenate([s_act[0:8], jnp.broadcast_to(s_act[8:9], (8, D))], 0)
    g_w_ada = _matmul(s2, dm2_mine, mode="tn", name="dw_ada", tm=512, tn=ADA_COLS, after=last)
    dsc_part = _matmul(dm2_mine[8:16], p["w_ada"][0], mode="nt", name="d_silu_cctx", tm=8, tn=512, after=last)

    def cctx_b(rv, vv):
        _, pull = jax.vjp(_silu, vv[0])
        return [], [pull(jnp.sum(rv[0], axis=0, keepdims=True))[0]]

    g_cctx, = _rowwise(cctx_b, [(dsc_part, D, 0, 0)], [cc], [], [(1, D)], nrows=8, tr=8, name="cctx_bwd")
    gs["c_ctx"] = g_cctx
    gs["b_ada"] = d_mod_lat + d_mod_ctx

    res["w_ada"] = _adam(g_w_ada[None], p["w_ada"][0], m["w_ada"][0], v["w_ada"][0], tr=256, name="adam_w_ada")

    g_late = _allgather_small_seq(_pack([gs[n] for n in SMALL_LATE]), "gather_small_late", SMALL_COLLECTIVE_ID0 + 2)
    for names, g_pack, tag in ((SMALL_EARLY, g_early, "early"), (SMALL_LATE, g_late, "late")):
        sm = _adam(g_pack, _pack([p[n] for n in names]), _pack([m[n] for n in names]), _pack([v[n] for n in names]),
                   tr=g_pack.shape[1], name="adam_small_" + tag, after=last)
        shapes = [p[n].shape for n in names]
        for j, outs in enumerate(zip(*[_unpack(a, shapes) for a in sm])):
            res[names[j]] = outs

    outs = [total["loss"], grad_x[None]]
    for j in range(4):
        outs += [res[n][j].reshape(p[n].shape) for n in WEIGHTS]
    return tuple(outs)


def kernel(x, c, ctx, c_ctx, w_ada, b_ada, w_in, attn_sink, ssm_a_re, ssm_a_im, ssm_log_dt, ssm_b_re, ssm_b_im, ssm_c_re, ssm_c_im, ssm_d, w_glu, w_attn_up, w_ssm_up, w_out, ln_mix_g, ln_mix_b, w_mlp1, b_mlp1, w_mlp2, b_mlp2, ln_mlp_g, ln_mlp_b, loss_target, m_c_ctx, m_w_ada, m_b_ada, m_w_in, m_attn_sink, m_ssm_a_re, m_ssm_a_im, m_ssm_log_dt, m_ssm_b_re, m_ssm_b_im, m_ssm_c_re, m_ssm_c_im, m_ssm_d, m_w_glu, m_w_attn_up, m_w_ssm_up, m_w_out, m_ln_mix_g, m_ln_mix_b, m_w_mlp1, m_b_mlp1, m_w_mlp2, m_b_mlp2, m_ln_mlp_g, m_ln_mlp_b, v_c_ctx, v_w_ada, v_b_ada, v_w_in, v_attn_sink, v_ssm_a_re, v_ssm_a_im, v_ssm_log_dt, v_ssm_b_re, v_ssm_b_im, v_ssm_c_re, v_ssm_c_im, v_ssm_d, v_w_glu, v_w_attn_up, v_w_ssm_up, v_w_out, v_ln_mix_g, v_ln_mix_b, v_w_mlp1, v_b_mlp1, v_w_mlp2, v_b_mlp2, v_ln_mlp_g, v_ln_mlp_b):
    given = dict(locals())
    p = {n: given[n] for n in WEIGHTS}
    m = {n: given["m_" + n] for n in WEIGHTS}
    v = {n: given["v_" + n] for n in WEIGHTS}
    return _step(x, c, ctx, loss_target, p, m, v)
```

```python
import functools
import math

import jax
import jax.numpy as jnp
from jax import lax
from jax.experimental import pallas as pl
from jax.experimental.pallas import tpu as pltpu
from jax.experimental.pallas import tpu_sc as plsc

F32 = jnp.float32
BF16 = jnp.bfloat16

N_DEV = 8
D = 2048
T = 2048
C = 256
TA = T + C
GRID_W = 64
HD = 128
NH = 8
NKV = 2
GROUP = NH // NKV
WINDOW = 128
QW = NH * HD
KVW = NKV * HD
SW = D // 4
SG = 16
NG = SW // SG
SP = 64
DFF = 4 * D
IN_COLS = QW + 2 * KVW + SW + 2 * D
ALPHA = 2.0 ** 0.25
LN_EPS = 1e-6
NEG_INF = -1e30
ROPE_BASE = 10000.0
ATT_SCALE = HD ** -0.5

NSEG = 8
GBLK = 8
NBLK = NG // GBLK
BW = GBLK * SP
UW = GBLK * SG

ADAM_LR = 0.001
ADAM_B1 = 0.9
ADAM_B2 = 0.999
ADAM_EPS = 1e-08
ADAM_WD = 0.01
ADAM_STEP = 10

VMEM_LIMIT_BYTES = 56 * 1024 * 1024
MESH = pl.DeviceIdType.MESH


def _cparams(sem=None):
    return pltpu.CompilerParams(dimension_semantics=sem, vmem_limit_bytes=VMEM_LIMIT_BYTES)


def _matmul(a, b, *, mode, name, out_dtypes=(F32,), tm=512, tn=512, tk=None, bias=None, extras=(), epilogue=None, after=(),
            out_t=None):
    if mode == "nn":
        (M, K), (K2, N) = a.shape, b.shape
    elif mode == "nt":
        (M, K), (N, K2) = a.shape, b.shape
    else:
        (K, M), (K2, N) = a.shape, b.shape
    assert K == K2, (name, a.shape, b.shape)
    tm, tn, tk = min(tm, M), min(tn, N), min(tk or K, K)
    assert M % tm == 0 and N % tn == 0 and K % tk == 0, (name, M, N, K, tm, tn, tk)
    nk = K // tk
    if mode == "tn":
        a_spec = pl.BlockSpec((tk, tm), lambda i, j, k: (k, i))
    else:
        a_spec = pl.BlockSpec((tm, tk), lambda i, j, k: (i, k))
    if mode == "nt":
        b_spec = pl.BlockSpec((tn, tk), lambda i, j, k: (j, k))
    else:
        b_spec = pl.BlockSpec((tk, tn), lambda i, j, k: (k, j))
    dims = {"nn": (((1,), (0,)), ((), ())), "nt": (((1,), (1,)), ((), ())), "tn": (((0,), (0,)), ((), ()))}[mode]
    in_specs = [a_spec, b_spec]
    operands = [a, b]
    if bias is not None:
        in_specs.append(pl.BlockSpec((1, tn), lambda i, j, k: (0, j)))
        operands.append(bias)
    for e in extras:
        in_specs.append(pl.BlockSpec((tm, tn), lambda i, j, k: (i, j)))
        operands.append(e)
    n_ex = len(extras)
    for t in after:
        in_specs.append(pl.BlockSpec(memory_space=pl.ANY))
        operands.append(t)
    n_after = len(after)
    n_out = len(out_dtypes)
    out_t = tuple(out_t) if out_t is not None else (False,) * n_out
    has_bias = bias is not None

    def kern(*refs):
        a_ref, b_ref = refs[0], refs[1]
        pos = 2
        bias_ref = None
        if has_bias:
            bias_ref = refs[pos]
            pos += 1
        ex_refs = refs[pos:pos + n_ex]
        pos += n_ex + n_after
        out_refs = refs[pos:pos + n_out]
        acc_ref = refs[pos + n_out] if nk > 1 else None

        def finish(r):
            if has_bias:
                r = r + bias_ref[...]
            outs = epilogue(r, *[e[...] for e in ex_refs]) if epilogue is not None else (r,)
            for o_ref, o, tr_ in zip(out_refs, outs, out_t):
                o_ref[...] = (o.T if tr_ else o).astype(o_ref.dtype)

        part = lax.dot_general(a_ref[...].astype(BF16), b_ref[...].astype(BF16), dims, preferred_element_type=F32)
        if nk == 1:
            finish(part)
        else:
            k = pl.program_id(2)

            @pl.when(k == 0)
            def _():
                acc_ref[...] = part

            @pl.when(k > 0)
            def _():
                acc_ref[...] += part

            @pl.when(k == nk - 1)
            def _():
                finish(acc_ref[...])

    outs = pl.pallas_call(
        kern,
        name=name,
        grid=(M // tm, N // tn, nk),
        in_specs=in_specs,
        out_specs=[pl.BlockSpec((tn, tm), lambda i, j, k: (j, i)) if tr_ else pl.BlockSpec((tm, tn), lambda i, j, k: (i, j))
                   for tr_ in out_t],
        out_shape=[jax.ShapeDtypeStruct((N, M) if tr_ else (M, N), dt) for dt, tr_ in zip(out_dtypes, out_t)],
        scratch_shapes=[pltpu.VMEM((tm, tn), F32)] if nk > 1 else [],
        compiler_params=_cparams(("parallel", "parallel", "arbitrary")),
    )(*operands)
    return outs[0] if n_out == 1 else tuple(outs)


def _rowwise(fn, rows, vecs, outs, vec_outs, *, nrows, tr, name):
    n_rows, n_vecs, n_outs = len(rows), len(vecs), len(outs)
    in_specs = [pl.BlockSpec((tr, w), lambda i, cb=cb, ro=ro: (i + ro, cb)) for (_, w, cb, ro) in rows]
    in_specs += [pl.BlockSpec(v.shape, lambda i: (0, 0)) for v in vecs]
    outs = [o if len(o) == 3 else (*o, False) for o in outs]
    out_specs = [pl.BlockSpec((w, tr), lambda i: (0, i)) if tr_ else pl.BlockSpec((tr, w), lambda i: (i, 0)) for (w, _, tr_) in outs]
    out_specs += [pl.BlockSpec(s, lambda i: (0, 0)) for s in vec_outs]
    out_shape = [jax.ShapeDtypeStruct((w, nrows) if tr_ else (nrows, w), dt) for (w, dt, tr_) in outs]
    out_tr = [tr_ for (_, _, tr_) in outs]
    out_shape += [jax.ShapeDtypeStruct(s, F32) for s in vec_outs]

    def kern(*refs):
        rvals = [r[...] for r in refs[:n_rows]]
        vvals = [r[...] for r in refs[n_rows:n_rows + n_vecs]]
        o_refs = refs[n_rows + n_vecs:n_rows + n_vecs + n_outs]
        v_refs = refs[n_rows + n_vecs + n_outs:]
        ro, vo = fn(rvals, vvals)
        for r, val, tr_ in zip(o_refs, ro, out_tr):
            r[...] = (val.astype(F32).T if tr_ else val).astype(r.dtype)
        i = pl.program_id(0)
        for r, val in zip(v_refs, vo):
            @pl.when(i == 0)
            def _(r=r, val=val):
                r[...] = val.astype(F32)

            @pl.when(i > 0)
            def _(r=r, val=val):
                r[...] += val.astype(F32)

    res = pl.pallas_call(
        kern,
        name=name,
        grid=(nrows // tr,),
        in_specs=in_specs,
        out_specs=out_specs,
        out_shape=out_shape,
        compiler_params=_cparams(("arbitrary",)),
    )(*[r[0] for r in rows], *vecs)
    return list(res)


def _ln(x):
    mu = jnp.mean(x, axis=-1, keepdims=True)
    xc = x - mu
    var = jnp.mean(xc * xc, axis=-1, keepdims=True)
    return xc * lax.rsqrt(var + LN_EPS)


def _sigmoid(x):
    return 1.0 / (1.0 + jnp.exp(-x))


def _gelu(x):
    return 0.5 * x * (1.0 + jnp.tanh(math.sqrt(2.0 / math.pi) * (x + 0.044715 * (x * x * x))))


def _silu(x):
    return x * _sigmoid(x)


def _f_ln_mod(x, sc, sh):
    return _ln(x) * (1.0 + sc) + sh


def _f_glu(z):
    return z[:, :SW] * _sigmoid(z[:, SW:])


def _f_mix(ga, gs, attn_d, ssm_d):
    return _sigmoid(ga) * attn_d + _sigmoid(gs) * ssm_d


def _f_post1(x, y, g1, lg, lb, sc2, sh2):
    r1 = ALPHA * x + g1 * y
    x1 = _ln(r1) * lg + lb
    h2 = _ln(x1) * (1.0 + sc2) + sh2
    return x1, h2


def _f_loss(x1, mlp, tgt, g2, lg, lb, b2z):
    r2 = ALPHA * x1 + g2 * (mlp + b2z)
    out = _ln(r2) * lg + lb
    err = out - tgt
    return 0.5 * jnp.sum(err * err) * (1.0 / D)


def _rope_tables():
    rows = T // GRID_W
    row = jnp.repeat(jnp.arange(rows), GRID_W)
    col = jnp.tile(jnp.arange(GRID_W), rows)
    n_freq = HD // 4
    freqs = ROPE_BASE ** (-jnp.arange(n_freq, dtype=F32) / n_freq)
    ang_r = row.astype(F32)[:, None] * freqs
    ang_c = col.astype(F32)[:, None] * freqs
    ang = jnp.concatenate([ang_r, ang_r, ang_c, ang_c], -1)
    cos, sin = jnp.cos(ang), jnp.sin(ang)
    lo = (jnp.arange(HD) % (HD // 2)) < (HD // 4)
    sin_a = jnp.where(lo[None, :], -sin, 0.0)
    sin_b = jnp.where(lo[None, :], 0.0, sin)
    return cos, sin_a, sin_b


def _rope(x, cos, sa, sb):
    return x * cos + pltpu.roll(x, 96, 1) * sa + pltpu.roll(x, 32, 1) * sb


def _rope_t(dy, cos, sa, sb):
    return dy * cos + pltpu.roll(dy * sa, 32, 1) + pltpu.roll(dy * sb, 96, 1)


BAND = 3 * WINDOW
KPAD = T + 2 * WINDOW


def _attn_fill_kv(k_ref, v_ref, cos_ref, sa_ref, sb_ref, kp, vp, kc, vc):
    zeros = jnp.zeros((WINDOW, KVW), BF16)
    kp[0:WINDOW, :] = zeros
    kp[WINDOW + T:KPAD, :] = zeros
    vp[0:WINDOW, :] = zeros
    vp[WINDOW + T:KPAD, :] = zeros
    for hh in range(NKV):
        cs = slice(hh * HD, (hh + 1) * HD)
        for r0 in range(0, T, 512):
            rs = slice(r0, r0 + 512)
            kr = _rope(k_ref[rs, cs], cos_ref[rs, :], sa_ref[rs, :], sb_ref[rs, :])
            kp[WINDOW + r0:WINDOW + r0 + 512, cs] = kr.astype(BF16)
    vp[WINDOW:WINDOW + T, :] = v_ref[0:T, :].astype(BF16)
    kc[...] = k_ref[T:TA, :].astype(BF16)
    vc[...] = v_ref[T:TA, :].astype(BF16)


def _attn_scores(n, h, q_ref, cos_ref, sa_ref, sb_ref, sink_ref, kp, kc):
    kvh = h // GROUP
    r0 = pl.multiple_of(n * WINDOW, WINDOW)
    cos = cos_ref[pl.ds(r0, WINDOW), :]
    sa = sa_ref[pl.ds(r0, WINDOW), :]
    sb = sb_ref[pl.ds(r0, WINDOW), :]
    q_h = _rope(q_ref[:, h * HD:(h + 1) * HD], cos, sa, sb).astype(BF16)
    kb = kp[pl.ds(r0, BAND), kvh * HD:(kvh + 1) * HD]
    kcb = kc[:, kvh * HD:(kvh + 1) * HD]
    nt = (((1,), (1,)), ((), ()))
    s_loc = lax.dot_general(q_h, kb, nt, preferred_element_type=F32) * ATT_SCALE
    s_ctx = lax.dot_general(q_h, kcb, nt, preferred_element_type=F32) * ATT_SCALE
    row = lax.broadcasted_iota(jnp.int32, (WINDOW, BAND), 0)
    col = lax.broadcasted_iota(jnp.int32, (WINDOW, BAND), 1)
    rel = col - WINDOW - row
    kpos = r0 - WINDOW + col
    valid = (jnp.abs(rel) <= WINDOW) & (kpos >= 0) & (kpos < T)
    s_loc = jnp.where(valid, s_loc, NEG_INF)
    sk = sink_ref[0:1, h:h + 1]
    m = jnp.maximum(jnp.maximum(jnp.max(s_loc, -1, keepdims=True), jnp.max(s_ctx, -1, keepdims=True)), sk)
    e_loc = jnp.exp(s_loc - m)
    e_ctx = jnp.exp(s_ctx - m)
    e_sink = jnp.exp(sk - m)
    inv = 1.0 / (jnp.sum(e_loc, -1, keepdims=True) + jnp.sum(e_ctx, -1, keepdims=True) + e_sink)
    return q_h, r0, e_loc * inv, e_ctx * inv, e_sink * inv


def _attn_fwd(proj, sink, tabs):
    cos, sa, sb = tabs

    def kern(q_ref, k_ref, v_ref, cos_ref, sa_ref, sb_ref, sink_ref, o_ref, kp, vp, kc, vc):
        n = pl.program_id(0)

        @pl.when(n == 0)
        def _():
            _attn_fill_kv(k_ref, v_ref, cos_ref, sa_ref, sb_ref, kp, vp, kc, vc)

        for h in range(NH):
            kvh = h // GROUP
            _, r0, p_loc, p_ctx, _ = _attn_scores(n, h, q_ref, cos_ref, sa_ref, sb_ref, sink_ref, kp, kc)
            vb = vp[pl.ds(r0, BAND), kvh * HD:(kvh + 1) * HD]
            vcb = vc[:, kvh * HD:(kvh + 1) * HD]
            o = jnp.dot(p_loc.astype(BF16), vb, preferred_element_type=F32)
            o = o + jnp.dot(p_ctx.astype(BF16), vcb, preferred_element_type=F32)
            o_ref[:, h * HD:(h + 1) * HD] = o.astype(o_ref.dtype)

    full = lambda shape: pl.BlockSpec(shape, lambda n: (0, 0))
    return pl.pallas_call(
        kern,
        name="attn_fwd",
        grid=(T // WINDOW,),
        in_specs=[
            pl.BlockSpec((WINDOW, QW), lambda n: (n, 0)),
            pl.BlockSpec((TA, KVW), lambda n: (0, QW // KVW)),
            pl.BlockSpec((TA, KVW), lambda n: (0, QW // KVW + 1)),
            full((T, HD)), full((T, HD)), full((T, HD)), full((1, NH)),
        ],
        out_specs=pl.BlockSpec((WINDOW, QW), lambda n: (n, 0)),
        out_shape=jax.ShapeDtypeStruct((T, QW), BF16),
        scratch_shapes=[pltpu.VMEM((KPAD, KVW), BF16), pltpu.VMEM((KPAD, KVW), BF16),
                        pltpu.VMEM((C, KVW), BF16), pltpu.VMEM((C, KVW), BF16)],
        compiler_params=_cparams(("arbitrary",)),
    )(proj, proj, proj, cos, sa, sb, sink)


def _attn_bwd(proj, d_attn, sink, tabs):
    cos, sa, sb = tabs
    n_blocks = T // WINDOW

    def kern(q_ref, k_ref, v_ref, do_ref, cos_ref, sa_ref, sb_ref, sink_ref,
             dq_ref, dk_ref, dv_ref, dsink_ref, kp, vp, kc, vc, dkp, dvp, dkc, dvc):
        n = pl.program_id(0)

        @pl.when(n == 0)
        def _():
            _attn_fill_kv(k_ref, v_ref, cos_ref, sa_ref, sb_ref, kp, vp, kc, vc)
            dkp[...] = jnp.zeros_like(dkp)
            dvp[...] = jnp.zeros_like(dvp)
            dkc[...] = jnp.zeros_like(dkc)
            dvc[...] = jnp.zeros_like(dvc)
            dsink_ref[...] = jnp.zeros_like(dsink_ref)

        nt = (((1,), (1,)), ((), ()))
        tn = (((0,), (0,)), ((), ()))
        for h in range(NH):
            kvh = h // GROUP
            cs = slice(kvh * HD, (kvh + 1) * HD)
            q_h, r0, p_loc, p_ctx, p_sink = _attn_scores(n, h, q_ref, cos_ref, sa_ref, sb_ref, sink_ref, kp, kc)
            kb = kp[pl.ds(r0, BAND), cs]
            vb = vp[pl.ds(r0, BAND), cs]
            kcb = kc[:, cs]
            vcb = vc[:, cs]
            do_h = do_ref[:, h * HD:(h + 1) * HD]
            dp_loc = lax.dot_general(do_h, vb, nt, preferred_element_type=F32)
            dp_ctx = lax.dot_general(do_h, vcb, nt, preferred_element_type=F32)
            delta = jnp.sum(p_loc * dp_loc, -1, keepdims=True) + jnp.sum(p_ctx * dp_ctx, -1, keepdims=True)
            ds_loc = (p_loc * (dp_loc - delta) * ATT_SCALE).astype(BF16)
            ds_ctx = (p_ctx * (dp_ctx - delta) * ATT_SCALE).astype(BF16)
            dq = jnp.dot(ds_loc, kb, preferred_element_type=F32) + jnp.dot(ds_ctx, kcb, preferred_element_type=F32)
            cos = cos_ref[pl.ds(r0, WINDOW), :]
            sa_ = sa_ref[pl.ds(r0, WINDOW), :]
            sb_ = sb_ref[pl.ds(r0, WINDOW), :]
            dq_ref[:, h * HD:(h + 1) * HD] = _rope_t(dq, cos, sa_, sb_).astype(dq_ref.dtype)
            dkp[pl.ds(r0, BAND), cs] += lax.dot_general(ds_loc, q_h, tn, preferred_element_type=F32)
            dkc[:, cs] += lax.dot_general(ds_ctx, q_h, tn, preferred_element_type=F32)
            dvp[pl.ds(r0, BAND), cs] += lax.dot_general(p_loc.astype(BF16), do_h, tn, preferred_element_type=F32)
            dvc[:, cs] += lax.dot_general(p_ctx.astype(BF16), do_h, tn, preferred_element_type=F32)
            dsk = -jnp.sum(p_sink * delta, axis=0, keepdims=True)
            dsink_ref[h:h + 1, :] += jnp.broadcast_to(dsk, (1, HD))

        @pl.when(n == n_blocks - 1)
        def _():
            for hh in range(NKV):
                cs = slice(hh * HD, (hh + 1) * HD)
                for r0 in range(0, T, 512):
                    rs = slice(r0, r0 + 512)
                    g = dkp[WINDOW + r0:WINDOW + r0 + 512, cs]
                    dk_ref[rs, cs] = _rope_t(g, cos_ref[rs, :], sa_ref[rs, :], sb_ref[rs, :]).astype(dk_ref.dtype)
            dk_ref[T:TA, :] = dkc[...].astype(dk_ref.dtype)
            dv_ref[0:T, :] = dvp[WINDOW:WINDOW + T, :].astype(dv_ref.dtype)
            dv_ref[T:TA, :] = dvc[...].astype(dv_ref.dtype)

    full = lambda shape: pl.BlockSpec(shape, lambda n: (0, 0))
    return pl.pallas_call(
        kern,
        name="attn_bwd",
        grid=(n_blocks,),
        in_specs=[
            pl.BlockSpec((WINDOW, QW), lambda n: (n, 0)),
            pl.BlockSpec((TA, KVW), lambda n: (0, QW // KVW)),
            pl.BlockSpec((TA, KVW), lambda n: (0, QW // KVW + 1)),
            pl.BlockSpec((WINDOW, QW), lambda n: (n, 0)),
            full((T, HD)), full((T, HD)), full((T, HD)), full((1, NH)),
        ],
        out_specs=[pl.BlockSpec((WINDOW, QW), lambda n: (n, 0)), full((TA, KVW)), full((TA, KVW)), full((NH, HD))],
        out_shape=[jax.ShapeDtypeStruct((T, QW), BF16), jax.ShapeDtypeStruct((TA, KVW), BF16),
                   jax.ShapeDtypeStruct((TA, KVW), BF16), jax.ShapeDtypeStruct((NH, HD), F32)],
        scratch_shapes=[pltpu.VMEM((KPAD, KVW), BF16), pltpu.VMEM((KPAD, KVW), BF16),
                        pltpu.VMEM((C, KVW), BF16), pltpu.VMEM((C, KVW), BF16),
                        pltpu.VMEM((KPAD, KVW), F32), pltpu.VMEM((KPAD, KVW), F32),
                        pltpu.VMEM((C, KVW), F32), pltpu.VMEM((C, KVW), F32)],
        compiler_params=_cparams(("arbitrary",)),
    )(proj, proj, proj, d_attn, cos, sa, sb, sink)


def _s5_prep(a_re, a_im, log_dt, b_re, b_im, c_re, c_im):
    lam = lax.complex(a_re, a_im)
    dt = jnp.exp(log_dt)[..., None]
    lam_bar = jnp.exp(lam * dt)
    b_bar = ((lam_bar - 1.0) / lam)[..., None] * lax.complex(b_re, b_im)
    eye = jnp.eye(GBLK, dtype=F32)

    def lam_rows(v):
        return v.reshape(2, NBLK, 1, BW)

    lam_l = jnp.concatenate([lam_rows(jnp.real(lam_bar)), lam_rows(jnp.imag(lam_bar))], -1)
    lam_l = jnp.broadcast_to(lam_l, (2, NBLK, 8, 2 * BW))

    def b_blocks(v):
        v = v.reshape(2, NBLK, GBLK, SP, SG).transpose(0, 1, 2, 4, 3)
        return (v[:, :, :, :, None, :] * eye[None, None, :, None, :, None]).reshape(2, NBLK, UW, BW)

    bmat = jnp.concatenate([b_blocks(jnp.real(b_bar)), b_blocks(jnp.imag(b_bar))], -1)

    def c_blocks(v):
        v = v.reshape(2, NBLK, GBLK, SG, SP).transpose(0, 1, 2, 4, 3)
        return (v[:, :, :, :, None, :] * eye[None, None, :, None, :, None]).reshape(2, NBLK, BW, UW)

    cmat = jnp.concatenate([c_blocks(c_re), -c_blocks(c_im)], 2)
    return lam_l, bmat, cmat


def _cmul(ar, ai, br, bi):
    return ar * br - ai * bi, ar * bi + ai * br


def _shift_rows(x, rev, fill):
    r = lax.broadcasted_iota(jnp.int32, x.shape, 0)
    down = jnp.where(r == 0, fill, pltpu.roll(x, 1, 0))
    up = jnp.where(r == NSEG - 1, fill, pltpu.roll(x, NSEG - 1, 0))
    return jnp.where(rev == 0, down, up)


def _edge_row(x, rev):
    last = jnp.broadcast_to(x[NSEG - 1:NSEG, :], x.shape)
    first = jnp.broadcast_to(x[0:1, :], x.shape)
    return jnp.where(rev == 0, last, first)


def _seg_scan(get, put, base, seglen, lr, li, rev, cin):
    zero = jnp.zeros((NSEG, BW), F32)

    def rows(k):
        j = jnp.where(rev == 0, k, seglen - 1 - k)
        return pl.ds(pl.multiple_of(base + j * NSEG, NSEG), NSEG)

    def local(k, carry):
        sr, si, pr, pi = carry
        xr, xi = get(rows(k))
        tr, ti = _cmul(lr, li, sr, si)
        sr, si = tr + xr, ti + xi
        put(rows(k), sr, si)
        pr, pi = _cmul(lr, li, pr, pi)
        return sr, si, pr, pi

    er, ei, lpr, lpi = lax.fori_loop(0, seglen, local, (zero, zero, zero + 1.0, zero))
    cr, ci = _shift_rows(zero, rev, cin[0]), _shift_rows(zero, rev, cin[1])
    for _ in range(NSEG - 1):
        tr, ti = _cmul(lpr, lpi, cr, ci)
        cr, ci = _shift_rows(er + tr, rev, cin[0]), _shift_rows(ei + ti, rev, cin[1])

    def fix(k, carry):
        pr, pi = carry
        xr, xi = get(rows(k))
        tr, ti = _cmul(pr, pi, cr, ci)
        put(rows(k), xr + tr, xi + ti)
        return _cmul(lr, li, pr, pi)

    lax.fori_loop(0, seglen, fix, (lr, li))
    tr, ti = _cmul(lpr, lpi, cr, ci)
    return _edge_row(er + tr, rev), _edge_row(ei + ti, rev)


RCH = 256
CSEG = C // NSEG
TSEG = T // NSEG
UCOL0 = (QW + 2 * KVW) // UW


REGIONS = ((0, TSEG), (T, CSEG))


def _state_access(ref, lead=()):
    def get(rows):
        return ref[(*lead, rows, slice(0, BW))], ref[(*lead, rows, slice(BW, 2 * BW))]

    def put(rows, re, im):
        ref[(*lead, rows, slice(0, BW))] = re
        ref[(*lead, rows, slice(BW, 2 * BW))] = im

    return get, put


def _interleave_rows(src_ref, dst_ref, regions=REGIONS):
    for base, seglen in regions:
        def body(j, carry, base=base, seglen=seglen):
            dst_ref[pl.ds(pl.multiple_of(base + j * NSEG, NSEG), NSEG), :] = src_ref[pl.ds(base + j, NSEG, stride=seglen), :]
            return carry

        lax.fori_loop(0, seglen, body, 0, unroll=8)


def _deinterleave_rows(src_ref, dst_ref, regions=REGIONS):
    for base, seglen in regions:
        def body(j, carry, base=base, seglen=seglen):
            dst_ref[pl.ds(base + j, NSEG, stride=seglen), :] = src_ref[pl.ds(pl.multiple_of(base + j * NSEG, NSEG), NSEG), :]
            return carry

        lax.fori_loop(0, seglen, body, 0, unroll=8)


def _s5_fwd(proj, dskip, lam, bmat, cmat):
    def kern(u_ref, dk_ref, lam_ref, b_ref, c_ref, s_ref, ssm_ref, ge_ref, up_ref, yp_ref):
        d = pl.program_id(1)

        @pl.when(d == 0)
        def _():
            _interleave_rows(u_ref, up_ref)

        bm = b_ref[0, 0].astype(BF16)
        for r0 in range(0, TA, RCH):
            s_ref[0, 0, r0:r0 + RCH, :] = jnp.dot(up_ref[r0:r0 + RCH, :].astype(BF16), bm, preferred_element_type=F32)
        lr = lam_ref[0, 0, :, 0:BW]
        li = lam_ref[0, 0, :, BW:2 * BW]
        zero = jnp.zeros((NSEG, BW), F32)
        get, put = _state_access(s_ref, (0, 0))
        mid = _seg_scan(get, put, T, CSEG, lr, li, d, (zero, zero))
        _seg_scan(get, put, 0, TSEG, lr, li, d, mid)
        cm = c_ref[0, 0].astype(BF16)
        for r0 in range(0, T, RCH):
            y = jnp.dot(s_ref[0, 0, r0:r0 + RCH, :].astype(BF16), cm, preferred_element_type=F32)

            @pl.when(d == 0)
            def _(y=y, r0=r0):
                yp_ref[r0:r0 + RCH, :] = y + dk_ref[...] * up_ref[r0:r0 + RCH, :]

            @pl.when(d == 1)
            def _(y=y, r0=r0):
                yp_ref[r0:r0 + RCH, :] += y

        @pl.when(d == 1)
        def _():
            _deinterleave_rows(yp_ref, ssm_ref, REGIONS[:1])
            for r0 in range(0, T, RCH):
                ge_ref[r0:r0 + RCH, :] = _gelu(ssm_ref[r0:r0 + RCH, :]).astype(ge_ref.dtype)

    blk4 = lambda shape: pl.BlockSpec((1, 1) + shape, lambda b, d: (d, b, 0, 0))
    return pl.pallas_call(
        kern,
        name="s5_fwd",
        grid=(NBLK, 2),
        in_specs=[pl.BlockSpec((TA, UW), lambda b, d: (0, UCOL0 + b)), pl.BlockSpec((1, UW), lambda b, d: (0, b)),
                  blk4((8, 2 * BW)), blk4((UW, 2 * BW)), blk4((2 * BW, UW))],
        out_specs=[blk4((TA, 2 * BW)), pl.BlockSpec((T, UW), lambda b, d: (0, b)), pl.BlockSpec((T, UW), lambda b, d: (0, b))],
        out_shape=[jax.ShapeDtypeStruct((2, NBLK, TA, 2 * BW), F32), jax.ShapeDtypeStruct((T, SW), F32),
                   jax.ShapeDtypeStruct((T, SW), BF16)],
        scratch_shapes=[pltpu.VMEM((TA, UW), F32), pltpu.VMEM((T, UW), F32)],
        compiler_params=_cparams(("parallel", "arbitrary")),
    )(proj, dskip, lam, bmat, cmat)


def _s5_bwd(d_ge, ssm, proj, dskip, states, lam, bmat, cmat):
    nt = (((1,), (1,)), ((), ()))
    tn = (((0,), (0,)), ((), ()))

    def kern(dge_ref, ssm_ref, u_ref, dk_ref, s_ref, lam_ref, b_ref, c_ref,
             du_ref, ddk_ref, dlam_ref, db_ref, dc_ref, g_ref, dua_ref, dssm_ref, up_ref, nat_ref):
        d = pl.program_id(1)

        @pl.when(d == 0)
        def _():
            ddk = jnp.zeros((1, UW), F32)
            for r0 in range(0, T, RCH):
                rs = slice(r0, r0 + RCH)
                _, pull = jax.vjp(_gelu, ssm_ref[rs, :])
                dssm = pull(dge_ref[rs, :])[0]
                nat_ref[rs, :] = dssm
                ddk = ddk + jnp.sum(dssm * u_ref[rs, :], axis=0, keepdims=True)
            ddk_ref[...] = ddk
            _interleave_rows(nat_ref, dssm_ref, REGIONS[:1])
            _interleave_rows(u_ref, up_ref)
            for r0 in range(0, T, RCH):
                dua_ref[r0:r0 + RCH, :] = dssm_ref[r0:r0 + RCH, :] * dk_ref[...]
            dua_ref[T:TA, :] = jnp.zeros((C, UW), F32)

        cm = c_ref[0, 0].astype(BF16)
        for r0 in range(0, T, RCH):
            g_ref[r0:r0 + RCH, :] = lax.dot_general(dssm_ref[r0:r0 + RCH, :].astype(BF16), cm, nt, preferred_element_type=F32)
        g_ref[T:TA, :] = jnp.zeros((C, 2 * BW), F32)
        lr = lam_ref[0, 0, :, 0:BW]
        li = lam_ref[0, 0, :, BW:2 * BW]
        zero = jnp.zeros((NSEG, BW), F32)
        get_g, put_g = _state_access(g_ref)

        mid = _seg_scan(get_g, put_g, 0, TSEG, lr, -li, 1 - d, (zero, zero))
        _seg_scan(get_g, put_g, T, CSEG, lr, -li, 1 - d, mid)

        get_s, _ = _state_access(s_ref, (0, 0))

        def dlam_terms(g, s):
            return g[0] * s[0] + g[1] * s[1], g[1] * s[0] - g[0] * s[1]

        def dlam_region(base, seglen, s_in, acc):
            def rows(j):
                return pl.ds(pl.multiple_of(base + j * NSEG, NSEG), NSEG)

            def inner(k, acc):
                j = jnp.where(d == 0, k, seglen - 1 - k)
                jp = jnp.where(d == 0, k - 1, seglen - k)
                t = dlam_terms(get_g(rows(j)), get_s(rows(jp)))
                return acc[0] + t[0], acc[1] + t[1]

            acc = lax.fori_loop(1, seglen, inner, acc)
            jb = jnp.where(d == 0, 0, seglen - 1)
            jn = jnp.where(d == 0, seglen - 1, 0)
            sp = get_s(rows(jn))
            t = dlam_terms(get_g(rows(jb)), (_shift_rows(sp[0], d, s_in[0]), _shift_rows(sp[1], d, s_in[1])))
            return acc[0] + t[0], acc[1] + t[1]

        r_mid = jnp.where(d == 0, TA - 1, T)
        s_mid = tuple(jnp.broadcast_to(t, (NSEG, BW)) for t in get_s(pl.ds(r_mid, 1)))
        acc = dlam_region(T, CSEG, (zero, zero), (zero, zero))
        acc = dlam_region(0, TSEG, s_mid, acc)
        dlam_ref[0, 0, :, 0:BW] = acc[0]
        dlam_ref[0, 0, :, BW:2 * BW] = acc[1]

        bm = b_ref[0, 0].astype(BF16)
        db = jnp.zeros((UW, 2 * BW), F32)
        dc = jnp.zeros((2 * BW, UW), F32)
        for r0 in range(0, TA, RCH):
            rs = slice(r0, r0 + RCH)
            g = g_ref[rs, :].astype(BF16)
            dua_ref[rs, :] += lax.dot_general(g, bm, nt, preferred_element_type=F32)
            db = db + lax.dot_general(up_ref[rs, :].astype(BF16), g, tn, preferred_element_type=F32)
            if r0 < T:
                dc = dc + lax.dot_general(s_ref[0, 0, rs, :].astype(BF16), dssm_ref[rs, :].astype(BF16), tn,
                                          preferred_element_type=F32)
        db_ref[0, 0] = db
        dc_ref[0, 0] = dc

        @pl.when(d == 1)
        def _():
            _deinterleave_rows(dua_ref, nat_ref)
            du_ref[...] = nat_ref[...].astype(du_ref.dtype)

    blk4 = lambda shape: pl.BlockSpec((1, 1) + shape, lambda b, d: (d, b, 0, 0))
    lat = pl.BlockSpec((T, UW), lambda b, d: (0, b))
    vec = pl.BlockSpec((1, UW), lambda b, d: (0, b))
    return pl.pallas_call(
        kern,
        name="s5_bwd",
        grid=(NBLK, 2),
        in_specs=[lat, lat, pl.BlockSpec((TA, UW), lambda b, d: (0, UCOL0 + b)), vec,
                  blk4((TA, 2 * BW)), blk4((8, 2 * BW)), blk4((UW, 2 * BW)), blk4((2 * BW, UW))],
        out_specs=[pl.BlockSpec((TA, UW), lambda b, d: (0, b)), vec, blk4((8, 2 * BW)), blk4((UW, 2 * BW)), blk4((2 * BW, UW))],
        out_shape=[jax.ShapeDtypeStruct((TA, SW), BF16), jax.ShapeDtypeStruct((1, SW), F32),
                   jax.ShapeDtypeStruct((2, NBLK, 8, 2 * BW), F32),
                   jax.ShapeDtypeStruct((2, NBLK, UW, 2 * BW), F32), jax.ShapeDtypeStruct((2, NBLK, 2 * BW, UW), F32)],
        scratch_shapes=[pltpu.VMEM((TA, 2 * BW), F32), pltpu.VMEM((TA, UW), F32), pltpu.VMEM((T, UW), F32),
                        pltpu.VMEM((TA, UW), F32), pltpu.VMEM((TA, UW), F32)],
        compiler_params=_cparams(("parallel", "arbitrary")),
    )(d_ge, ssm, proj, dskip, states, lam, bmat, cmat)


TR = 256


def _vjp_rows(f, primals, cots, n_row):
    _, pull = jax.vjp(f, *primals)
    g = pull(cots)
    return list(g[:n_row]), list(g[n_row:])


class _GradDict(dict):
    def __init__(self, on_set=None):
        super().__init__()
        self._on_set = on_set
        self.tokens = {}

    def __setitem__(self, key, value):
        super().__setitem__(key, value)
        if self._on_set is not None:
            self._on_set(self)

    def order(self, key):
        return self.tokens.get(key, self.get(key))


def _local_step(x, ctx, tgt, mod_lat, mod_ctx, wb, sp, on_grad=None, on_loss=None):
    sh1, sc1, g1, sh2, sc2, g2 = [mod_lat[:, i * D:(i + 1) * D] for i in range(6)]
    csh1, csc1 = mod_ctx[:, 0:D], mod_ctx[:, D:2 * D]
    tabs = _rope_tables()
    sink = sp["attn_sink"].reshape(1, NH)
    dskip = sp["ssm_d"].reshape(1, SW)
    lg_mix, lb_mix = sp["ln_mix_g"].reshape(1, D), sp["ln_mix_b"].reshape(1, D)
    lg_mlp, lb_mlp = sp["ln_mlp_g"].reshape(1, D), sp["ln_mlp_b"].reshape(1, D)
    b1, b2 = sp["b_mlp1"].reshape(1, DFF), sp["b_mlp2"].reshape(1, D)
    s5_names = ("ssm_a_re", "ssm_a_im", "ssm_log_dt", "ssm_b_re", "ssm_b_im", "ssm_c_re", "ssm_c_im")
    (lam, bmat, cmat), s5_pull = jax.vjp(_s5_prep, *[sp[n] for n in s5_names])

    def ln_mod2(rv, vv):
        h = _f_ln_mod(rv[0], vv[0], vv[1])
        return [h, h], []

    h_lat, h_lat_t = _rowwise(ln_mod2, [(x, D, 0, 0)], [sc1, sh1], [(D, BF16), (D, BF16, True)], [], nrows=T, tr=TR, name="ln1_lat")
    h_ctx, h_ctx_t = _rowwise(ln_mod2, [(ctx, D, 0, 0)], [csc1, csh1], [(D, BF16), (D, BF16, True)], [], nrows=C, tr=TR,
                              name="ln1_ctx")
    h1 = jnp.concatenate([h_lat, h_ctx], 0)
    h1_t = jnp.concatenate([h_lat_t, h_ctx_t], 1)
    proj = _matmul(h1, wb["w_in"], mode="nn", name="proj", tm=768, tn=512)
    attn = _attn_fwd(proj, sink, tabs)
    states, ssm, ge = _s5_fwd(proj, dskip, lam, bmat, cmat)
    z = _matmul(ge, wb["w_glu"], mode="nn", name="glu_mm", tm=1024, tn=1024)

    def glu_act(rv, vv):
        return [_f_glu(rv[0])], []

    glu, = _rowwise(glu_act, [(z, 2 * SW, 0, 0)], [], [(SW, BF16)], [], nrows=T, tr=TR, name="glu_act")
    attn_d = _matmul(attn, wb["w_attn_up"], mode="nn", name="attn_up", tm=1024, tn=512)
    ssm_d = _matmul(glu, wb["w_ssm_up"], mode="nn", name="ssm_up", tm=1024, tn=512)
    ga_cb, gs_cb = (QW + 2 * KVW + SW) // D, (QW + 2 * KVW + SW) // D + 1

    def mix(rv, vv):
        m_ = _f_mix(*rv)
        return [m_, m_], []

    mixv, mix_t = _rowwise(mix, [(proj, D, ga_cb, 0), (proj, D, gs_cb, 0), (attn_d, D, 0, 0), (ssm_d, D, 0, 0)], [],
                           [(D, BF16), (D, BF16, True)], [], nrows=T, tr=TR, name="mix")
    y = _matmul(mixv, wb["w_out"], mode="nn", name="out_proj", tm=1024, tn=512)

    def post1(rv, vv):
        x1, h2 = _f_post1(rv[0], rv[1], *vv)
        return [x1, h2, h2], []

    x1, h2, h2_t = _rowwise(post1, [(x, D, 0, 0), (y, D, 0, 0)], [g1, lg_mix, lb_mix, sc2, sh2],
                            [(D, F32), (D, BF16), (D, BF16, True)], [], nrows=T, tr=TR, name="post1")

    def relu_sq(acc):
        r = jnp.maximum(acc, 0.0)
        return r, r * r, r * r

    r_act, act, act_t = _matmul(h2, wb["w_mlp1"], mode="nn", name="mlp1", tm=1024, tn=512, bias=b1,
                                out_dtypes=(BF16, BF16, BF16), out_t=(False, False, True), epilogue=relu_sq)
    mlp = _matmul(act, wb["w_mlp2"], mode="nn", name="mlp2", tm=1024, tn=512, tk=2048)

    def loss_fb(rv, vv):
        x1_t, mlp_t, tgt_t = rv
        g2_v, lg_v, lb_v, b2_v = vv
        f = lambda a, m, g, p, q, b: _f_loss(a, m, tgt_t, g, p, q, b)
        val, grads = jax.value_and_grad(f, argnums=(0, 1, 2, 3, 4, 5))(x1_t, mlp_t, g2_v, lg_v, lb_v, b2_v)
        dx1, dmlp, dg2, dlg, dlb, db2 = grads
        return [dx1, dmlp], [jnp.reshape(val, (1, 1)), dg2, dlg, dlb, db2]

    dx1_a, d_mlp, loss_p, d_g2, d_lg_mlp, d_lb_mlp, d_b2 = _rowwise(
        loss_fb, [(x1, D, 0, 0), (mlp, D, 0, 0), (tgt, D, 0, 0)], [g2, lg_mlp, lb_mlp, b2],
        [(D, F32), (D, BF16)], [(1, 1), (1, D), (1, D), (1, D), (1, D)], nrows=T, tr=TR, name="loss_fb")

    gw = _GradDict(on_grad)
    loss_done = () if on_loss is None else (on_loss(loss_p),)
    gw["w_mlp2"] = _matmul(act_t, d_mlp, mode="nn", name="dw_mlp2", out_dtypes=(BF16,), tm=1024, tn=512, after=loss_done)
    da, = (_matmul(d_mlp, wb["w_mlp2"], mode="nt", name="d_act", out_dtypes=(BF16,), tm=1024, tn=512,
                   extras=(r_act,), epilogue=lambda acc, r: (acc * (2.0 * r.astype(F32)),), after=(gw.order("w_mlp2"),)),)
    ones = jnp.ones((8, T), BF16)
    d_b1 = _matmul(ones, da, mode="nn", name="db_mlp1", tm=8, tn=2048)[0:1]
    gw["w_mlp1"] = _matmul(h2_t, da, mode="nn", name="dw_mlp1", out_dtypes=(BF16,), tm=1024, tn=512)
    dh2 = _matmul(da, wb["w_mlp1"], mode="nt", name="d_h2", tm=1024, tn=512, tk=2048, after=(gw.order("w_mlp1"),))

    def post1_b(rv, vv):
        x_t, y_t, dx1_t, dh2_t = rv
        gr, gv = _vjp_rows(_f_post1, (x_t, y_t, *vv), (dx1_t, dh2_t), 2)
        return [gr[0], gr[1]], gv

    dx_a, dy, d_g1, d_lg_mix, d_lb_mix, d_sc2, d_sh2 = _rowwise(
        post1_b, [(x, D, 0, 0), (y, D, 0, 0), (dx1_a, D, 0, 0), (dh2, D, 0, 0)], [g1, lg_mix, lb_mix, sc2, sh2],
        [(D, F32), (D, BF16)], [(1, D)] * 5, nrows=T, tr=TR, name="post1_bwd")
    gw["w_out"] = _matmul(mix_t, dy, mode="nn", name="dw_out", out_dtypes=(BF16,), tm=1024, tn=512)
    dmix = _matmul(dy, wb["w_out"], mode="nt", name="d_mix", tm=1024, tn=512, after=(gw.order("w_out"),))

    def mix_b(rv, vv):
        gr, _ = _vjp_rows(_f_mix, tuple(rv[:4]), rv[4], 4)
        return gr, []

    d_ga, d_gs, d_attn_d, d_ssm_d = _rowwise(
        mix_b, [(proj, D, ga_cb, 0), (proj, D, gs_cb, 0), (attn_d, D, 0, 0), (ssm_d, D, 0, 0), (dmix, D, 0, 0)], [],
        [(D, BF16)] * 4, [], nrows=T, tr=TR, name="mix_bwd")
    gw["w_attn_up"] = _matmul(attn, d_attn_d, mode="tn", name="dw_attn_up", out_dtypes=(BF16,), tm=512, tn=1024, tk=1024)
    d_attn = _matmul(d_attn_d, wb["w_attn_up"], mode="nt", name="d_attn", out_dtypes=(BF16,), tm=1024, tn=512)
    gw["w_ssm_up"] = _matmul(glu, d_ssm_d, mode="tn", name="dw_ssm_up", out_dtypes=(BF16,), tm=512, tn=1024, tk=1024)
    d_glu = _matmul(d_ssm_d, wb["w_ssm_up"], mode="nt", name="d_glu", tm=1024, tn=512, after=(gw.order("w_attn_up"), gw.order("w_ssm_up")))

    def glu_b(rv, vv):
        gr, _ = _vjp_rows(_f_glu, (rv[0],), rv[1], 1)
        return gr, []

    dz, = _rowwise(glu_b, [(z, 2 * SW, 0, 0), (d_glu, SW, 0, 0)], [], [(2 * SW, BF16)], [], nrows=T, tr=TR, name="glu_bwd")
    gw["w_glu"] = _matmul(ge, dz, mode="tn", name="dw_glu", out_dtypes=(BF16,), tm=512, tn=1024, tk=1024)
    d_ge = _matmul(dz, wb["w_glu"], mode="nt", name="d_ge", tm=1024, tn=512, after=(gw.order("w_glu"),))

    du_all, d_dskip, dlam, dbmat, dcmat = _s5_bwd(d_ge, ssm, proj, dskip, states, lam, bmat, cmat)
    s5_grads = s5_pull((dlam, dbmat, dcmat))

    dq, dk, dv, dsink = _attn_bwd(proj, d_attn, sink, tabs)
    zc = lambda w: jnp.zeros((C, w), BF16)
    dproj = jnp.concatenate([
        jnp.concatenate([dq, zc(QW)], 0), dk, dv, du_all,
        jnp.concatenate([d_ga, zc(D)], 0), jnp.concatenate([d_gs, zc(D)], 0)], 1)
    gw["w_in"] = _matmul(h1_t, dproj, mode="nn", name="dw_in", out_dtypes=(BF16,), tm=1024, tn=512)
    dh1 = _matmul(dproj, wb["w_in"], mode="nt", name="d_h1", tm=768, tn=512, tk=2048, after=(gw.order("w_in"),))

    def ln1_b(rv, vv):
        x_t, dh_t, dxa_t = rv
        gr, gv = _vjp_rows(_f_ln_mod, (x_t, vv[0], vv[1]), dh_t, 1)
        return [gr[0] + dxa_t], gv

    grad_x, d_sc1, d_sh1 = _rowwise(ln1_b, [(x, D, 0, 0), (dh1, D, 0, 0), (dx_a, D, 0, 0)], [sc1, sh1],
                                    [(D, F32)], [(1, D), (1, D)], nrows=T, tr=TR, name="ln1_lat_bwd")

    def ln1c_b(rv, vv):
        _, gv = _vjp_rows(_f_ln_mod, (rv[0], vv[0], vv[1]), rv[1], 1)
        return [], gv

    d_csc1, d_csh1 = _rowwise(ln1c_b, [(ctx, D, 0, 0), (dh1, D, 0, T // TR)], [csc1, csh1],
                              [], [(1, D), (1, D)], nrows=C, tr=TR, name="ln1_ctx_bwd")

    d_mod_lat = jnp.concatenate([d_sh1, d_sc1, d_g1, d_sh2, d_sc2, d_g2], 1)
    zv = jnp.zeros((1, D), F32)
    d_mod_ctx = jnp.concatenate([d_csh1, d_csc1, zv, zv, zv, zv], 1)
    gs = {n: g for n, g in zip(s5_names, s5_grads)}
    gs["attn_sink"] = dsink[:, 0]
    gs["ssm_d"] = d_dskip
    gs["ln_mix_g"], gs["ln_mix_b"] = d_lg_mix, d_lb_mix
    gs["ln_mlp_g"], gs["ln_mlp_b"] = d_lg_mlp, d_lb_mlp
    gs["b_mlp1"], gs["b_mlp2"] = d_b1, d_b2
    return loss_p, grad_x, d_mod_lat, d_mod_ctx, gw, gs


def _my_pos():
    return lax.axis_index("x"), lax.axis_index("y"), lax.axis_index("c")


def _flip(p, bit):
    return 1 - p if bit else p


def _peer(pos, k):
    x, y, c = pos
    return (_flip(x, (k >> 2) & 1), _flip(y, (k >> 1) & 1), _flip(c, k & 1))


def _lin(pos):
    return 4 * pos[0] + 2 * pos[1] + pos[2]


def _allgather_small(v, name):
    r, w = v.shape

    def body(v_ref, out_ref, send_sems, recv_sems, local_sem):
        me = _my_pos()
        mine = pltpu.make_async_copy(v_ref, out_ref.at[_lin(me)], local_sem)
        mine.start()
        sends = []
        for k in range(1, N_DEV):
            cp = pltpu.make_async_remote_copy(src_ref=v_ref, dst_ref=out_ref.at[_lin(me)], send_sem=send_sems.at[k - 1],
                                              recv_sem=recv_sems.at[k - 1], device_id=_peer(me, k), device_id_type=MESH)
            cp.start()
            sends.append(cp)
        for k in range(1, N_DEV):
            peer = _peer(me, k)
            pltpu.make_async_remote_copy(src_ref=v_ref, dst_ref=out_ref.at[_lin(peer)], send_sem=send_sems.at[k - 1],
                                         recv_sem=recv_sems.at[k - 1], device_id=peer, device_id_type=MESH).wait_recv()
        for cp in sends:
            cp.wait_send()
        mine.wait()

    return pl.pallas_call(
        body,
        name=name,
        out_shape=jax.ShapeDtypeStruct((N_DEV, r, w), v.dtype),
        in_specs=[pl.BlockSpec(memory_space=pltpu.VMEM)],
        out_specs=pl.BlockSpec(memory_space=pltpu.VMEM),
        scratch_shapes=[pltpu.SemaphoreType.DMA((N_DEV - 1,)), pltpu.SemaphoreType.DMA((N_DEV - 1,)), pltpu.SemaphoreType.DMA],
        compiler_params=pltpu.CompilerParams(vmem_limit_bytes=VMEM_LIMIT_BYTES),
    )(v)


def _block_of(ref, kind, idx, n):
    start = pl.multiple_of(idx * n, 128)
    if kind == "col":
        return ref.at[:, pl.ds(start, n)]
    return ref.at[pl.ds(start, n), :]


def _allgather_weights(shards, kinds):
    nt = len(shards)
    out_shape = []
    for s, kind in zip(shards, kinds):
        k, n = s.shape
        out_shape.append(jax.ShapeDtypeStruct((k, n * N_DEV) if kind == "col" else (k * N_DEV, n), s.dtype))

    def body(*refs):
        ins, outs = refs[:nt], refs[nt:2 * nt]
        send_sems, recv_sems, local_sems = refs[2 * nt:]
        x, y, c = _my_pos()
        me, sibling = (x, y, c), (x, y, 1 - c)
        chips = [(1 - x, y), (x, 1 - y), (1 - x, 1 - y)]

        def blk(t, pos):
            n = shards[t].shape[1] if kinds[t] == "col" else shards[t].shape[0]
            return _block_of(outs[t], kinds[t], _lin(pos), n)

        def copy(t, k, block, to, src=None):
            return pltpu.make_async_remote_copy(src_ref=blk(t, block) if src is None else src, dst_ref=blk(t, block),
                                                send_sem=send_sems.at[t, k], recv_sem=recv_sems.at[t, k],
                                                device_id=to, device_id_type=MESH)

        local, sends = [], []
        for t in range(nt):
            mine = pltpu.make_async_copy(ins[t], blk(t, me), local_sems.at[t])
            mine.start()
            local.append(mine)
            first = [copy(t, 0, me, sibling, src=ins[t])]
            first += [copy(t, 1 + j, me, (*chip, c), src=ins[t]) for j, chip in enumerate(chips)]
            for cp in first:
                cp.start()
            sends += first
        for t in range(nt):
            for j, chip in enumerate(chips):
                copy(t, 1 + j, (*chip, c), me).wait_recv()
                fwd = copy(t, 4 + j, (*chip, c), sibling)
                fwd.start()
                sends.append(fwd)
        for t in range(nt):
            copy(t, 0, sibling, me).wait_recv()
            for j, chip in enumerate(chips):
                copy(t, 4 + j, (*chip, 1 - c), me).wait_recv()
        for cp in sends:
            cp.wait_send()
        for cp in local:
            cp.wait()

    any_spec = pl.BlockSpec(memory_space=pl.ANY)
    return pl.pallas_call(
        body,
        name="allgather_weights",
        out_shape=out_shape,
        in_specs=[any_spec] * nt,
        out_specs=[any_spec] * nt,
        scratch_shapes=[pltpu.SemaphoreType.DMA((nt, N_DEV - 1)), pltpu.SemaphoreType.DMA((nt, N_DEV - 1)),
                        pltpu.SemaphoreType.DMA((nt,))],
    )(*shards)


def _handshake(peers):
    barrier = pltpu.get_barrier_semaphore()
    for peer in peers:
        pl.semaphore_signal(barrier, inc=1, device_id=peer, device_id_type=MESH)
    pl.semaphore_wait(barrier, len(peers))


def _allgather_weights_seq(shards, kinds, name, collective_id):
    nt = len(shards)
    hbm = pltpu.MemorySpace.HBM
    ins = [jax.new_ref(s, memory_space=hbm) for s in shards]
    outs = []
    for s, kind in zip(shards, kinds):
        k, n = s.shape
        shape = (k, n * N_DEV) if kind == "col" else (k * N_DEV, n)
        outs.append(jax.empty_ref(jax.ShapeDtypeStruct(shape, s.dtype), memory_space=hbm))

    @functools.partial(
        pl.kernel, mesh=plsc.ScalarSubcoreMesh(axis_name="seq", num_cores=1), name=name,
        scratch_types=(pltpu.SemaphoreType.DMA((nt, N_DEV - 1)), pltpu.SemaphoreType.DMA((nt, N_DEV - 1)),
                       pltpu.SemaphoreType.DMA((nt,))),
        compiler_params=pltpu.CompilerParams(collective_id=collective_id))
    def launch(send_sems, recv_sems, local_sems):
        x, y, c = _my_pos()
        me, sibling = (x, y, c), (x, y, 1 - c)
        chips = [(1 - x, y), (x, 1 - y), (1 - x, 1 - y)]
        _handshake([sibling] + [(*chip, c) for chip in chips])

        def blk(t, pos):
            n = shards[t].shape[1] if kinds[t] == "col" else shards[t].shape[0]
            return _block_of(outs[t], kinds[t], _lin(pos), n)

        def copy(t, k, block, to, src=None):
            return pltpu.make_async_remote_copy(src_ref=blk(t, block) if src is None else src, dst_ref=blk(t, block),
                                                send_sem=send_sems.at[t, k], recv_sem=recv_sems.at[t, k],
                                                device_id=to, device_id_type=MESH)

        local, sends = [], []
        for t in range(nt):
            mine = pltpu.make_async_copy(ins[t], blk(t, me), local_sems.at[t])
            mine.start()
            local.append(mine)
            first = [copy(t, 0, me, sibling, src=ins[t])]
            first += [copy(t, 1 + j, me, (*chip, c), src=ins[t]) for j, chip in enumerate(chips)]
            for cp in first:
                cp.start()
            sends += first
        for t in range(nt):
            for j, chip in enumerate(chips):
                copy(t, 1 + j, (*chip, c), me).wait_recv()
                fwd = copy(t, 4 + j, (*chip, c), sibling)
                fwd.start()
                sends.append(fwd)
        for t in range(nt):
            copy(t, 0, sibling, me).wait_recv()
            for j, chip in enumerate(chips):
                copy(t, 4 + j, (*chip, 1 - c), me).wait_recv()
        for cp in sends:
            cp.wait_send()
        for cp in local:
            cp.wait()

    launch()
    return [o[...] for o in outs]


def _allgather_small_seq(v, name, collective_id):
    hbm = pltpu.MemorySpace.HBM
    src = jax.new_ref(v, memory_space=hbm)
    out = jax.empty_ref(jax.ShapeDtypeStruct((N_DEV,) + v.shape, v.dtype), memory_space=hbm)

    @functools.partial(
        pl.kernel, mesh=plsc.ScalarSubcoreMesh(axis_name="seq", num_cores=1), name=name,
        scratch_types=(pltpu.SemaphoreType.DMA((N_DEV - 1,)), pltpu.SemaphoreType.DMA((N_DEV - 1,)), pltpu.SemaphoreType.DMA),
        compiler_params=pltpu.CompilerParams(collective_id=collective_id))
    def launch(send_sems, recv_sems, local_sem):
        me = _my_pos()
        _handshake([_peer(me, k) for k in range(1, N_DEV)])
        mine = pltpu.make_async_copy(src, out.at[_lin(me)], local_sem)
        mine.start()
        sends = []
        for k in range(1, N_DEV):
            cp = pltpu.make_async_remote_copy(src_ref=src, dst_ref=out.at[_lin(me)], send_sem=send_sems.at[k - 1],
                                              recv_sem=recv_sems.at[k - 1], device_id=_peer(me, k), device_id_type=MESH)
            cp.start()
            sends.append(cp)
        for k in range(1, N_DEV):
            peer = _peer(me, k)
            pltpu.make_async_remote_copy(src_ref=src, dst_ref=out.at[_lin(peer)], send_sem=send_sems.at[k - 1],
                                         recv_sem=recv_sems.at[k - 1], device_id=peer, device_id_type=MESH).wait_recv()
        for cp in sends:
            cp.wait_send()
        mine.wait()

    launch()
    return out[...]


def _scatter_grads_seq(grads, kinds, name, collective_id):
    nt = len(grads)
    hbm = pltpu.MemorySpace.HBM
    shard_shapes = []
    for g, kind in zip(grads, kinds):
        k, n = g.shape
        shard_shapes.append((k, n // N_DEV) if kind == "col" else (k // N_DEV, n))
    ins = [jax.new_ref(g, memory_space=hbm) for g in grads]
    outs = [jax.empty_ref(jax.ShapeDtypeStruct((N_DEV,) + s, g.dtype), memory_space=hbm) for s, g in zip(shard_shapes, grads)]

    @functools.partial(
        pl.kernel, mesh=plsc.ScalarSubcoreMesh(axis_name="seq", num_cores=1), name=name,
        scratch_types=(pltpu.SemaphoreType.DMA((nt, N_DEV - 1)), pltpu.SemaphoreType.DMA((nt, N_DEV - 1)),
                       pltpu.SemaphoreType.DMA((nt,))),
        compiler_params=pltpu.CompilerParams(collective_id=collective_id))
    def launch(send_sems, recv_sems, local_sems):
        me = _my_pos()
        _handshake([_peer(me, k) for k in range(1, N_DEV)])

        def blk(t, pos):
            n = shard_shapes[t][1] if kinds[t] == "col" else shard_shapes[t][0]
            return _block_of(ins[t], kinds[t], _lin(pos), n)

        local, sends = [], []
        for t in range(nt):
            cp = pltpu.make_async_copy(blk(t, me), outs[t].at[_lin(me)], local_sems.at[t])
            cp.start()
            local.append(cp)
            for k in range(1, N_DEV):
                peer = _peer(me, k)
                cp = pltpu.make_async_remote_copy(src_ref=blk(t, peer), dst_ref=outs[t].at[_lin(me)], send_sem=send_sems.at[t, k - 1],
                                                  recv_sem=recv_sems.at[t, k - 1], device_id=peer, device_id_type=MESH)
                cp.start()
                sends.append(cp)
        for t in range(nt):
            for k in range(1, N_DEV):
                peer = _peer(me, k)
                pltpu.make_async_remote_copy(src_ref=blk(t, me), dst_ref=outs[t].at[_lin(peer)], send_sem=send_sems.at[t, k - 1],
                                             recv_sem=recv_sems.at[t, k - 1], device_id=peer, device_id_type=MESH).wait_recv()
        for cp in sends:
            cp.wait_send()
        for cp in local:
            cp.wait()

    launch()
    return [o[...] for o in outs]


_HBM_SPEC = pl.BlockSpec(memory_space=pltpu.HBM)
_SEM_SPEC = pl.BlockSpec(memory_space=pltpu.SEMAPHORE)
_EFFECT = pltpu.SideEffectType.DATAFLOW_SIDE_EFFECTING
LOCAL_CHUNKS = 16


def _shard_shapes(grads, kinds):
    return [(g.shape[0], g.shape[1] // N_DEV) if kind == "col" else (g.shape[0] // N_DEV, g.shape[1]) for g, kind in zip(grads, kinds)]


def _scatter_copies(g_refs, land_refs, send_sems, recv_sems, kinds, shard_shapes):
    me = _my_pos()
    copies = []
    for t in range(len(g_refs)):
        n = shard_shapes[t][1] if kinds[t] == "col" else shard_shapes[t][0]
        for k in range(1, N_DEV):
            peer = _peer(me, k)
            copies.append(pltpu.make_async_remote_copy(
                src_ref=_block_of(g_refs[t], kinds[t], _lin(peer), n), dst_ref=land_refs[t].at[_lin(me)],
                send_sem=send_sems.at[t * (N_DEV - 1) + k - 1], recv_sem=recv_sems.at[t * (N_DEV - 1) + k - 1],
                device_id=peer, device_id_type=MESH))
    return copies


def _scatter_start(grads, kinds, name):
    nt = len(grads)
    shard_shapes = _shard_shapes(grads, kinds)

    def body(*refs):
        g_refs, land_refs = refs[:nt], refs[nt:2 * nt]
        send_sems, recv_sems = refs[2 * nt], refs[2 * nt + 1]
        token = refs[2 * nt + 2 + 2 * nt]
        local_sems = refs[-1]
        me = _my_pos()
        local = []
        for t in range(nt):
            n = shard_shapes[t][1] if kinds[t] == "col" else shard_shapes[t][0]
            src, dst = _block_of(g_refs[t], kinds[t], _lin(me), n), land_refs[t].at[_lin(me)]
            rows = shard_shapes[t][0] // LOCAL_CHUNKS
            for ch in range(LOCAL_CHUNKS):
                rs = pl.ds(ch * rows, rows)
                cp = pltpu.make_async_copy(src.at[rs, :], dst.at[rs, :], local_sems.at[t * LOCAL_CHUNKS + ch])
                cp.start()
                local.append(cp)
        token[...] = jnp.zeros_like(token)
        for cp in local:
            cp.wait()
        for cp in _scatter_copies(g_refs, land_refs, send_sems, recv_sems, kinds, shard_shapes):
            cp.start()

    lands = [pltpu.with_memory_space_constraint(lax.empty((N_DEV,) + s, g.dtype), pltpu.HBM) for s, g in zip(shard_shapes, grads)]
    sem_shape = pltpu.SemaphoreType.DMA((nt * (N_DEV - 1),))
    out = pl.pallas_call(
        body,
        name=name,
        out_shape=(sem_shape, sem_shape, *[pltpu.HBM(g.shape, g.dtype) for g in grads],
                   *[pltpu.HBM(l.shape, l.dtype) for l in lands], jax.ShapeDtypeStruct((8, 128), F32)),
        in_specs=[_HBM_SPEC] * (2 * nt),
        out_specs=(_SEM_SPEC, _SEM_SPEC, *[_HBM_SPEC] * (2 * nt), pl.BlockSpec(memory_space=pltpu.VMEM)),
        input_output_aliases={i: 2 + i for i in range(2 * nt)},
        scratch_shapes=[pltpu.SemaphoreType.DMA((nt * LOCAL_CHUNKS,))],
        compiler_params=pltpu.CompilerParams(has_side_effects=_EFFECT),
    )(*[pltpu.with_memory_space_constraint(g, pltpu.HBM) for g in grads], *lands)
    return out[0], out[1], list(out[2:2 + nt]), list(out[2 + nt:2 + 2 * nt]), out[-1]


def _scatter_wait(send_sems, recv_sems, g_thru, land_thru, kinds, after, name):
    nt = len(g_thru)
    shard_shapes = _shard_shapes(g_thru, kinds)

    def body(*refs):
        g_refs, land_refs = refs[:nt], refs[nt:2 * nt]
        send_sems, recv_sems = refs[2 * nt], refs[2 * nt + 1]
        for cp in _scatter_copies(g_refs, land_refs, send_sems, recv_sems, kinds, shard_shapes):
            cp.wait_send()
            cp.wait_recv()

    out = pl.pallas_call(
        body,
        name=name,
        out_shape=tuple(pltpu.HBM(a.shape, a.dtype) for a in (*g_thru, *land_thru)),
        in_specs=[*[_HBM_SPEC] * (2 * nt), _SEM_SPEC, _SEM_SPEC, pl.BlockSpec(memory_space=pl.ANY)],
        out_specs=tuple([_HBM_SPEC] * (2 * nt)),
        input_output_aliases={i: i for i in range(2 * nt)},
        compiler_params=pltpu.CompilerParams(has_side_effects=_EFFECT),
    )(*g_thru, *land_thru, send_sems, recv_sems, after)
    return list(out[nt:])


def _scatter_grads(grads, kinds):
    nt = len(grads)
    shard_shapes = []
    for g, kind in zip(grads, kinds):
        k, n = g.shape
        shard_shapes.append((k, n // N_DEV) if kind == "col" else (k // N_DEV, n))

    def body(*refs):
        ins, outs = refs[:nt], refs[nt:2 * nt]
        send_sems, recv_sems, local_sems = refs[2 * nt:]
        me = _my_pos()

        def blk(t, pos):
            n = shard_shapes[t][1] if kinds[t] == "col" else shard_shapes[t][0]
            return _block_of(ins[t], kinds[t], _lin(pos), n)

        local, sends = [], []
        for t in range(nt):
            cp = pltpu.make_async_copy(blk(t, me), outs[t].at[_lin(me)], local_sems.at[t])
            cp.start()
            local.append(cp)
            for k in range(1, N_DEV):
                peer = _peer(me, k)
                cp = pltpu.make_async_remote_copy(src_ref=blk(t, peer), dst_ref=outs[t].at[_lin(me)], send_sem=send_sems.at[t, k - 1],
                                                  recv_sem=recv_sems.at[t, k - 1], device_id=peer, device_id_type=MESH)
                cp.start()
                sends.append(cp)
        for t in range(nt):
            for k in range(1, N_DEV):
                peer = _peer(me, k)
                pltpu.make_async_remote_copy(src_ref=blk(t, me), dst_ref=outs[t].at[_lin(peer)], send_sem=send_sems.at[t, k - 1],
                                             recv_sem=recv_sems.at[t, k - 1], device_id=peer, device_id_type=MESH).wait_recv()
        for cp in sends:
            cp.wait_send()
        for cp in local:
            cp.wait()

    any_spec = pl.BlockSpec(memory_space=pl.ANY)
    return pl.pallas_call(
        body,
        name="scatter_grads",
        out_shape=[jax.ShapeDtypeStruct((N_DEV,) + s, g.dtype) for s, g in zip(shard_shapes, grads)],
        in_specs=[any_spec] * nt,
        out_specs=[any_spec] * nt,
        scratch_shapes=[pltpu.SemaphoreType.DMA((nt, N_DEV - 1)), pltpu.SemaphoreType.DMA((nt, N_DEV - 1)),
                        pltpu.SemaphoreType.DMA((nt,))],
    )(*grads)


def _adam(g_slots, w, m, v, *, tr, name, after=()):
    ns, r, wd = g_slots.shape
    tr = min(tr, r)
    assert r % tr == 0, (name, r, tr)
    c1 = 1.0 - ADAM_B1 ** ADAM_STEP
    c2 = 1.0 - ADAM_B2 ** ADAM_STEP
    n_after = len(after)

    def kern(g_ref, w_ref, m_ref, v_ref, *rest):
        go_ref, d_ref, mo_ref, vo_ref = rest[n_after:]
        g = g_ref[0].astype(F32)
        for s in range(1, ns):
            g = g + g_ref[s].astype(F32)
        m_new = ADAM_B1 * m_ref[...] + (1.0 - ADAM_B1) * g
        v_new = ADAM_B2 * v_ref[...] + (1.0 - ADAM_B2) * (g * g)
        m_hat = m_new / c1
        v_hat = v_new / c2
        go_ref[...] = g
        d_ref[...] = -ADAM_LR * (m_hat / (jnp.sqrt(v_hat) + ADAM_EPS) + ADAM_WD * w_ref[...])
        mo_ref[...] = m_new
        vo_ref[...] = v_new

    tile = pl.BlockSpec((tr, wd), lambda i: (i, 0))
    return pl.pallas_call(
        kern,
        name=name,
        grid=(r // tr,),
        in_specs=[pl.BlockSpec((ns, tr, wd), lambda i: (0, i, 0)), tile, tile, tile] + [pl.BlockSpec(memory_space=pl.ANY)] * n_after,
        out_specs=[tile] * 4,
        out_shape=[jax.ShapeDtypeStruct((r, wd), F32)] * 4,
        compiler_params=_cparams(("parallel",)),
    )(g_slots, w, m, v, *after)


SMALL = ("c_ctx", "b_ada", "attn_sink", "ssm_a_re", "ssm_a_im", "ssm_log_dt", "ssm_b_re", "ssm_b_im", "ssm_c_re", "ssm_c_im",
         "ssm_d", "ln_mix_g", "ln_mix_b", "b_mlp1", "b_mlp2", "ln_mlp_g", "ln_mlp_b")
BIG = ("w_in", "w_glu", "w_attn_up", "w_ssm_up", "w_out", "w_mlp1", "w_mlp2")
BIG_KIND = ("col", "col", "col", "col", "row", "col", "row")
AG_GROUPS = (("w_in",), ("w_glu", "w_attn_up", "w_ssm_up", "w_out"), ("w_mlp1",), ("w_mlp2",))
AG_COLLECTIVE_ID0 = 1
RS_GROUPS = (("w_mlp2",), ("w_mlp1",), ("w_out", "w_attn_up", "w_ssm_up", "w_glu"), ("w_in",))
RS_COLLECTIVE_ID0 = AG_COLLECTIVE_ID0 + len(AG_GROUPS)
SMALL_EARLY = ("ssm_a_re", "ssm_a_im", "ssm_log_dt", "ssm_b_re", "ssm_b_im", "ssm_c_re", "ssm_c_im", "ssm_d")
SMALL_LATE = tuple(n for n in SMALL if n not in SMALL_EARLY)
SMALL_COLLECTIVE_ID0 = RS_COLLECTIVE_ID0 + len(RS_GROUPS)
LANES = 128


def _pack(parts):
    rows = []
    for p in parts:
        flat = p.reshape(-1).astype(F32)
        pad = (-flat.shape[0]) % LANES
        rows.append(jnp.pad(flat, (0, pad)).reshape(-1, LANES))
    packed = jnp.concatenate(rows, 0)
    return jnp.pad(packed, ((0, (-packed.shape[0]) % 8), (0, 0)))


def _unpack(packed, shapes):
    out, r0 = [], 0
    for s in shapes:
        n = math.prod(s)
        nr = -(-n // LANES)
        out.append(packed[r0:r0 + nr].reshape(-1)[:n].reshape(s))
        r0 += nr
    return out


WEIGHTS = ("c_ctx", "w_ada", "b_ada", "w_in", "attn_sink", "ssm_a_re", "ssm_a_im", "ssm_log_dt", "ssm_b_re", "ssm_b_im",
           "ssm_c_re", "ssm_c_im", "ssm_d", "w_glu", "w_attn_up", "w_ssm_up", "w_out", "ln_mix_g", "ln_mix_b", "w_mlp1",
           "b_mlp1", "w_mlp2", "b_mlp2", "ln_mlp_g", "ln_mlp_b")
ADA_COLS = 6 * D // N_DEV


def _step(x, c, ctx, loss_target, p, m, v):
    me = _lin(_my_pos())
    x2, ctx2, tgt2 = x[0], ctx[0], loss_target[0]

    wb = {}
    for gi, group in enumerate(AG_GROUPS):
        full = _allgather_weights_seq([p[n][0].astype(BF16) for n in group], [BIG_KIND[BIG.index(n)] for n in group],
                                      "allgather_seq%d" % gi, AG_COLLECTIVE_ID0 + gi)
        wb.update(zip(group, full))

    c_all = _allgather_small(jnp.broadcast_to(c, (8, D)), "gather_c")[:, 0, :]
    cc = p["c_ctx"].reshape(1, D)
    s_in = jnp.concatenate([c_all, cc, jnp.zeros((7, D), F32)], 0)
    s_act, = _rowwise(lambda rv, vv: ([_silu(rv[0])], []), [(s_in, D, 0, 0)], [], [(D, F32)], [], nrows=16, tr=16, name="silu_c")
    b_mine = lax.dynamic_slice_in_dim(p["b_ada"], me * ADA_COLS, ADA_COLS, axis=1)
    mod_part = _matmul(s_act, p["w_ada"][0], mode="nn", name="ada_fwd", tm=16, tn=512, bias=b_mine)
    mod_all = _allgather_small(mod_part, "gather_mod")
    mod_lat = lax.dynamic_index_in_dim(mod_all, me, axis=1, keepdims=False).reshape(1, 6 * D)
    mod_ctx = mod_all[:, 8, :].reshape(1, 6 * D)

    sp = {n: p[n][0] for n in SMALL if n not in ("c_ctx", "b_ada")}
    recv = {}

    def on_grad(gw):
        for gi, group in enumerate(RS_GROUPS):
            if group[0] not in recv and all(n in gw for n in group):
                slots = _scatter_grads_seq([gw[n] for n in group], [BIG_KIND[BIG.index(n)] for n in group],
                                           "scatter_seq%d" % gi, RS_COLLECTIVE_ID0 + gi)
                recv.update(zip(group, slots))

    total = {}

    def on_loss(loss_p):
        total["loss"] = lax.psum(loss_p[0, 0], ("x", "y", "c"))
        return total["loss"].reshape(1, 1)

    loss_p, grad_x, d_mod_lat, d_mod_ctx, gw, gs = _local_step(x2, ctx2, tgt2, mod_lat, mod_ctx, wb, sp, on_grad, on_loss)

    g_early = _allgather_small_seq(_pack([gs[n] for n in SMALL_EARLY]), "gather_small_early", SMALL_COLLECTIVE_ID0)
    res = {}
    last = ()

    def adam_small(names, g_pack, tag, after):
        sm = _adam(g_pack, _pack([p[n] for n in names]), _pack([m[n] for n in names]), _pack([v[n] for n in names]),
                   tr=g_pack.shape[1], name="adam_small_" + tag, after=after)
        shapes = [p[n].shape for n in names]
        for j, outs in enumerate(zip(*[_unpack(a, shapes) for a in sm])):
            res[names[j]] = outs
        return (sm[0],)

    for gi, group in enumerate(RS_GROUPS):
        if gi == len(RS_GROUPS) - 1:
            last = adam_small(SMALL_EARLY, g_early, "early", last)
        for n in group:
            res[n] = _adam(recv[n], p[n][0], m[n][0], v[n][0], tr=256, name="adam_" + n, after=last)
            last = (res[n][0],)

    dm = jnp.concatenate([d_mod_lat, d_mod_ctx, jnp.zeros((6, 6 * D), F32)], 0)
    dm_all = _allgather_small_seq(dm, "gather_dmod", SMALL_COLLECTIVE_ID0 + 1)
    dm_all = lax.optimization_barrier((dm_all,) + last)[0]
    dm2 = jnp.concatenate([dm_all[:, 0, :], dm_all[:, 1, :]], 0)
    dm2_mine = lax.dynamic_slice_in_dim(dm2, me * ADA_COLS, ADA_COLS, axis=1)
    s2 = jnp.concatenate([s_act[0:8], jnp.broadcast_to(s_act[8:9], (8, D))], 0)
    g_w_ada = _matmul(s2, dm2_mine, mode="tn", name="dw_ada", tm=512, tn=ADA_COLS, after=last)
    dsc_part = _matmul(dm2_mine[8:16], p["w_ada"][0], mode="nt", name="d_silu_cctx", tm=8, tn=512, after=last)

    def cctx_b(rv, vv):
        _, pull = jax.vjp(_silu, vv[0])
        return [], [pull(jnp.sum(rv[0], axis=0, keepdims=True))[0]]

    g_cctx, = _rowwise(cctx_b, [(dsc_part, D, 0, 0)], [cc], [], [(1, D)], nrows=8, tr=8, name="cctx_bwd")
    gs["c_ctx"] = g_cctx
    gs["b_ada"] = d_mod_lat + d_mod_ctx

    res["w_ada"] = _adam(g_w_ada[None], p["w_ada"][0], m["w_ada"][0], v["w_ada"][0], tr=256, name="adam_w_ada")

    g_late = _allgather_small_seq(_pack([gs[n] for n in SMALL_LATE]), "gather_small_late", SMALL_COLLECTIVE_ID0 + 2)
    adam_small(SMALL_LATE, g_late, "late", last)

    outs = [total["loss"], grad_x[None]]
    for j in range(4):
        outs += [res[n][j].reshape(p[n].shape) for n in WEIGHTS]
    return tuple(outs)


def kernel(x, c, ctx, c_ctx, w_ada, b_ada, w_in, attn_sink, ssm_a_re, ssm_a_im, ssm_log_dt, ssm_b_re, ssm_b_im, ssm_c_re, ssm_c_im, ssm_d, w_glu, w_attn_up, w_ssm_up, w_out, ln_mix_g, ln_mix_b, w_mlp1, b_mlp1, w_mlp2, b_mlp2, ln_mlp_g, ln_mlp_b, loss_target, m_c_ctx, m_w_ada, m_b_ada, m_w_in, m_attn_sink, m_ssm_a_re, m_ssm_a_im, m_ssm_log_dt, m_ssm_b_re, m_ssm_b_im, m_ssm_c_re, m_ssm_c_im, m_ssm_d, m_w_glu, m_w_attn_up, m_w_ssm_up, m_w_out, m_ln_mix_g, m_ln_mix_b, m_w_mlp1, m_b_mlp1, m_w_mlp2, m_b_mlp2, m_ln_mlp_g, m_ln_mlp_b, v_c_ctx, v_w_ada, v_b_ada, v_w_in, v_attn_sink, v_ssm_a_re, v_ssm_a_im, v_ssm_log_dt, v_ssm_b_re, v_ssm_b_im, v_ssm_c_re, v_ssm_c_im, v_ssm_d, v_w_glu, v_w_attn_up, v_w_ssm_up, v_w_out, v_ln_mix_g, v_ln_mix_b, v_w_mlp1, v_b_mlp1, v_w_mlp2, v_b_mlp2, v_ln_mlp_g, v_ln_mlp_b):
    given = dict(locals())
    p = {n: given[n] for n in WEIGHTS}
    m = {n: given["m_" + n] for n in WEIGHTS}
    v = {n: given["v_" + n] for n in WEIGHTS}
    return _step(x, c, ctx, loss_target, p, m, v)
```

```python
import functools
import math

import jax
import jax.numpy as jnp
from jax import lax
from jax.experimental import pallas as pl
from jax.experimental.pallas import tpu as pltpu
from jax.experimental.pallas import tpu_sc as plsc

F32 = jnp.float32
BF16 = jnp.bfloat16

N_DEV = 8
D = 2048
T = 2048
C = 256
TA = T + C
GRID_W = 64
HD = 128
NH = 8
NKV = 2
GROUP = NH // NKV
WINDOW = 128
QW = NH * HD
KVW = NKV * HD
SW = D // 4
SG = 16
NG = SW // SG
SP = 64
DFF = 4 * D
IN_COLS = QW + 2 * KVW + SW + 2 * D
ALPHA = 2.0 ** 0.25
LN_EPS = 1e-6
NEG_INF = -1e30
ROPE_BASE = 10000.0
ATT_SCALE = HD ** -0.5

NSEG = 8
GBLK = 8
NBLK = NG // GBLK
BW = GBLK * SP
UW = GBLK * SG

ADAM_LR = 0.001
ADAM_B1 = 0.9
ADAM_B2 = 0.999
ADAM_EPS = 1e-08
ADAM_WD = 0.01
ADAM_STEP = 10

VMEM_LIMIT_BYTES = 56 * 1024 * 1024
MESH = pl.DeviceIdType.MESH


def _cparams(sem=None):
    return pltpu.CompilerParams(dimension_semantics=sem, vmem_limit_bytes=VMEM_LIMIT_BYTES)


def _matmul(a, b, *, mode, name, out_dtypes=(F32,), tm=512, tn=512, tk=None, bias=None, extras=(), epilogue=None, after=(),
            out_t=None):
    if mode == "nn":
        (M, K), (K2, N) = a.shape, b.shape
    elif mode == "nt":
        (M, K), (N, K2) = a.shape, b.shape
    else:
        (K, M), (K2, N) = a.shape, b.shape
    assert K == K2, (name, a.shape, b.shape)
    tm, tn, tk = min(tm, M), min(tn, N), min(tk or K, K)
    assert M % tm == 0 and N % tn == 0 and K % tk == 0, (name, M, N, K, tm, tn, tk)
    nk = K // tk
    if mode == "tn":
        a_spec = pl.BlockSpec((tk, tm), lambda i, j, k: (k, i))
    else:
        a_spec = pl.BlockSpec((tm, tk), lambda i, j, k: (i, k))
    if mode == "nt":
        b_spec = pl.BlockSpec((tn, tk), lambda i, j, k: (j, k))
    else:
        b_spec = pl.BlockSpec((tk, tn), lambda i, j, k: (k, j))
    dims = {"nn": (((1,), (0,)), ((), ())), "nt": (((1,), (1,)), ((), ())), "tn": (((0,), (0,)), ((), ()))}[mode]
    in_specs = [a_spec, b_spec]
    operands = [a, b]
    if bias is not None:
        in_specs.append(pl.BlockSpec((1, tn), lambda i, j, k: (0, j)))
        operands.append(bias)
    for e in extras:
        in_specs.append(pl.BlockSpec((tm, tn), lambda i, j, k: (i, j)))
        operands.append(e)
    n_ex = len(extras)
    for t in after:
        in_specs.append(pl.BlockSpec(memory_space=pl.ANY))
        operands.append(t)
    n_after = len(after)
    n_out = len(out_dtypes)
    out_t = tuple(out_t) if out_t is not None else (False,) * n_out
    has_bias = bias is not None

    def kern(*refs):
        a_ref, b_ref = refs[0], refs[1]
        pos = 2
        bias_ref = None
        if has_bias:
            bias_ref = refs[pos]
            pos += 1
        ex_refs = refs[pos:pos + n_ex]
        pos += n_ex + n_after
        out_refs = refs[pos:pos + n_out]
        acc_ref = refs[pos + n_out] if nk > 1 else None

        def finish(r):
            if has_bias:
                r = r + bias_ref[...]
            outs = epilogue(r, *[e[...] for e in ex_refs]) if epilogue is not None else (r,)
            for o_ref, o, tr_ in zip(out_refs, outs, out_t):
                o_ref[...] = (o.T if tr_ else o).astype(o_ref.dtype)

        part = lax.dot_general(a_ref[...].astype(BF16), b_ref[...].astype(BF16), dims, preferred_element_type=F32)
        if nk == 1:
            finish(part)
        else:
            k = pl.program_id(2)

            @pl.when(k == 0)
            def _():
                acc_ref[...] = part

            @pl.when(k > 0)
            def _():
                acc_ref[...] += part

            @pl.when(k == nk - 1)
            def _():
                finish(acc_ref[...])

    outs = pl.pallas_call(
        kern,
        name=name,
        grid=(M // tm, N // tn, nk),
        in_specs=in_specs,
        out_specs=[pl.BlockSpec((tn, tm), lambda i, j, k: (j, i)) if tr_ else pl.BlockSpec((tm, tn), lambda i, j, k: (i, j))
                   for tr_ in out_t],
        out_shape=[jax.ShapeDtypeStruct((N, M) if tr_ else (M, N), dt) for dt, tr_ in zip(out_dtypes, out_t)],
        scratch_shapes=[pltpu.VMEM((tm, tn), F32)] if nk > 1 else [],
        compiler_params=_cparams(("parallel", "parallel", "arbitrary")),
    )(*operands)
    return outs[0] if n_out == 1 else tuple(outs)


def _rowwise(fn, rows, vecs, outs, vec_outs, *, nrows, tr, name, after=()):
    n_rows, n_vecs, n_outs, n_after = len(rows), len(vecs), len(outs), len(after)
    in_specs = [pl.BlockSpec((tr, w), lambda i, cb=cb, ro=ro: (i + ro, cb)) for (_, w, cb, ro) in rows]
    in_specs += [pl.BlockSpec(v.shape, lambda i: (0, 0)) for v in vecs]
    in_specs += [pl.BlockSpec(memory_space=pl.ANY)] * n_after
    outs = [o if len(o) == 3 else (*o, False) for o in outs]
    out_specs = [pl.BlockSpec((w, tr), lambda i: (0, i)) if tr_ else pl.BlockSpec((tr, w), lambda i: (i, 0)) for (w, _, tr_) in outs]
    out_specs += [pl.BlockSpec(s, lambda i: (0, 0)) for s in vec_outs]
    out_shape = [jax.ShapeDtypeStruct((w, nrows) if tr_ else (nrows, w), dt) for (w, dt, tr_) in outs]
    out_tr = [tr_ for (_, _, tr_) in outs]
    out_shape += [jax.ShapeDtypeStruct(s, F32) for s in vec_outs]

    def kern(*refs):
        rvals = [r[...] for r in refs[:n_rows]]
        vvals = [r[...] for r in refs[n_rows:n_rows + n_vecs]]
        first_out = n_rows + n_vecs + n_after
        o_refs = refs[first_out:first_out + n_outs]
        v_refs = refs[first_out + n_outs:]
        ro, vo = fn(rvals, vvals)
        for r, val, tr_ in zip(o_refs, ro, out_tr):
            r[...] = (val.astype(F32).T if tr_ else val).astype(r.dtype)
        i = pl.program_id(0)
        for r, val in zip(v_refs, vo):
            @pl.when(i == 0)
            def _(r=r, val=val):
                r[...] = val.astype(F32)

            @pl.when(i > 0)
            def _(r=r, val=val):
                r[...] += val.astype(F32)

    res = pl.pallas_call(
        kern,
        name=name,
        grid=(nrows // tr,),
        in_specs=in_specs,
        out_specs=out_specs,
        out_shape=out_shape,
        compiler_params=_cparams(("arbitrary",)),
    )(*[r[0] for r in rows], *vecs, *after)
    return list(res)


def _ln(x):
    mu = jnp.mean(x, axis=-1, keepdims=True)
    xc = x - mu
    var = jnp.mean(xc * xc, axis=-1, keepdims=True)
    return xc * lax.rsqrt(var + LN_EPS)


def _sigmoid(x):
    return 1.0 / (1.0 + jnp.exp(-x))


def _gelu(x):
    return 0.5 * x * (1.0 + jnp.tanh(math.sqrt(2.0 / math.pi) * (x + 0.044715 * (x * x * x))))


def _silu(x):
    return x * _sigmoid(x)


def _f_ln_mod(x, sc, sh):
    return _ln(x) * (1.0 + sc) + sh


def _f_glu(z):
    return z[:, :SW] * _sigmoid(z[:, SW:])


def _f_mix(ga, gs, attn_d, ssm_d):
    return _sigmoid(ga) * attn_d + _sigmoid(gs) * ssm_d


def _f_post1(x, y, g1, lg, lb, sc2, sh2):
    r1 = ALPHA * x + g1 * y
    x1 = _ln(r1) * lg + lb
    h2 = _ln(x1) * (1.0 + sc2) + sh2
    return x1, h2


def _f_loss(x1, mlp, tgt, g2, lg, lb, b2z):
    r2 = ALPHA * x1 + g2 * (mlp + b2z)
    out = _ln(r2) * lg + lb
    err = out - tgt
    return 0.5 * jnp.sum(err * err) * (1.0 / D)


def _rope_tables():
    rows = T // GRID_W
    row = jnp.repeat(jnp.arange(rows), GRID_W)
    col = jnp.tile(jnp.arange(GRID_W), rows)
    n_freq = HD // 4
    freqs = ROPE_BASE ** (-jnp.arange(n_freq, dtype=F32) / n_freq)
    ang_r = row.astype(F32)[:, None] * freqs
    ang_c = col.astype(F32)[:, None] * freqs
    ang = jnp.concatenate([ang_r, ang_r, ang_c, ang_c], -1)
    cos, sin = jnp.cos(ang), jnp.sin(ang)
    lo = (jnp.arange(HD) % (HD // 2)) < (HD // 4)
    sin_a = jnp.where(lo[None, :], -sin, 0.0)
    sin_b = jnp.where(lo[None, :], 0.0, sin)
    return cos, sin_a, sin_b


def _rope(x, cos, sa, sb):
    return x * cos + pltpu.roll(x, 96, 1) * sa + pltpu.roll(x, 32, 1) * sb


def _rope_t(dy, cos, sa, sb):
    return dy * cos + pltpu.roll(dy * sa, 32, 1) + pltpu.roll(dy * sb, 96, 1)


BAND = 3 * WINDOW
KPAD = T + 2 * WINDOW


def _attn_fill_kv(k_ref, v_ref, cos_ref, sa_ref, sb_ref, kp, vp, kc, vc):
    zeros = jnp.zeros((WINDOW, KVW), BF16)
    kp[0:WINDOW, :] = zeros
    kp[WINDOW + T:KPAD, :] = zeros
    vp[0:WINDOW, :] = zeros
    vp[WINDOW + T:KPAD, :] = zeros
    for hh in range(NKV):
        cs = slice(hh * HD, (hh + 1) * HD)
        for r0 in range(0, T, 512):
            rs = slice(r0, r0 + 512)
            kr = _rope(k_ref[rs, cs], cos_ref[rs, :], sa_ref[rs, :], sb_ref[rs, :])
            kp[WINDOW + r0:WINDOW + r0 + 512, cs] = kr.astype(BF16)
    vp[WINDOW:WINDOW + T, :] = v_ref[0:T, :].astype(BF16)
    kc[...] = k_ref[T:TA, :].astype(BF16)
    vc[...] = v_ref[T:TA, :].astype(BF16)


def _attn_scores(n, h, q_ref, cos_ref, sa_ref, sb_ref, sink_ref, kp, kc):
    kvh = h // GROUP
    r0 = pl.multiple_of(n * WINDOW, WINDOW)
    cos = cos_ref[pl.ds(r0, WINDOW), :]
    sa = sa_ref[pl.ds(r0, WINDOW), :]
    sb = sb_ref[pl.ds(r0, WINDOW), :]
    q_h = _rope(q_ref[:, h * HD:(h + 1) * HD], cos, sa, sb).astype(BF16)
    kb = kp[pl.ds(r0, BAND), kvh * HD:(kvh + 1) * HD]
    kcb = kc[:, kvh * HD:(kvh + 1) * HD]
    nt = (((1,), (1,)), ((), ()))
    s_loc = lax.dot_general(q_h, kb, nt, preferred_element_type=F32) * ATT_SCALE
    s_ctx = lax.dot_general(q_h, kcb, nt, preferred_element_type=F32) * ATT_SCALE
    row = lax.broadcasted_iota(jnp.int32, (WINDOW, BAND), 0)
    col = lax.broadcasted_iota(jnp.int32, (WINDOW, BAND), 1)
    rel = col - WINDOW - row
    kpos = r0 - WINDOW + col
    valid = (jnp.abs(rel) <= WINDOW) & (kpos >= 0) & (kpos < T)
    s_loc = jnp.where(valid, s_loc, NEG_INF)
    sk = sink_ref[0:1, h:h + 1]
    m = jnp.maximum(jnp.maximum(jnp.max(s_loc, -1, keepdims=True), jnp.max(s_ctx, -1, keepdims=True)), sk)
    e_loc = jnp.exp(s_loc - m)
    e_ctx = jnp.exp(s_ctx - m)
    e_sink = jnp.exp(sk - m)
    inv = 1.0 / (jnp.sum(e_loc, -1, keepdims=True) + jnp.sum(e_ctx, -1, keepdims=True) + e_sink)
    return q_h, r0, e_loc * inv, e_ctx * inv, e_sink * inv


def _attn_fwd(proj, sink, tabs):
    cos, sa, sb = tabs

    def kern(q_ref, k_ref, v_ref, cos_ref, sa_ref, sb_ref, sink_ref, o_ref, kp, vp, kc, vc):
        n = pl.program_id(0)

        @pl.when(n == 0)
        def _():
            _attn_fill_kv(k_ref, v_ref, cos_ref, sa_ref, sb_ref, kp, vp, kc, vc)

        for h in range(NH):
            kvh = h // GROUP
            _, r0, p_loc, p_ctx, _ = _attn_scores(n, h, q_ref, cos_ref, sa_ref, sb_ref, sink_ref, kp, kc)
            vb = vp[pl.ds(r0, BAND), kvh * HD:(kvh + 1) * HD]
            vcb = vc[:, kvh * HD:(kvh + 1) * HD]
            o = jnp.dot(p_loc.astype(BF16), vb, preferred_element_type=F32)
            o = o + jnp.dot(p_ctx.astype(BF16), vcb, preferred_element_type=F32)
            o_ref[:, h * HD:(h + 1) * HD] = o.astype(o_ref.dtype)

    full = lambda shape: pl.BlockSpec(shape, lambda n: (0, 0))
    return pl.pallas_call(
        kern,
        name="attn_fwd",
        grid=(T // WINDOW,),
        in_specs=[
            pl.BlockSpec((WINDOW, QW), lambda n: (n, 0)),
            pl.BlockSpec((TA, KVW), lambda n: (0, QW // KVW)),
            pl.BlockSpec((TA, KVW), lambda n: (0, QW // KVW + 1)),
            full((T, HD)), full((T, HD)), full((T, HD)), full((1, NH)),
        ],
        out_specs=pl.BlockSpec((WINDOW, QW), lambda n: (n, 0)),
        out_shape=jax.ShapeDtypeStruct((T, QW), BF16),
        scratch_shapes=[pltpu.VMEM((KPAD, KVW), BF16), pltpu.VMEM((KPAD, KVW), BF16),
                        pltpu.VMEM((C, KVW), BF16), pltpu.VMEM((C, KVW), BF16)],
        compiler_params=_cparams(("arbitrary",)),
    )(proj, proj, proj, cos, sa, sb, sink)


def _attn_bwd(proj, d_attn, sink, tabs):
    cos, sa, sb = tabs
    n_blocks = T // WINDOW

    def kern(q_ref, k_ref, v_ref, do_ref, cos_ref, sa_ref, sb_ref, sink_ref,
             dq_ref, dk_ref, dv_ref, dsink_ref, kp, vp, kc, vc, dkp, dvp, dkc, dvc):
        n = pl.program_id(0)

        @pl.when(n == 0)
        def _():
            _attn_fill_kv(k_ref, v_ref, cos_ref, sa_ref, sb_ref, kp, vp, kc, vc)
            dkp[...] = jnp.zeros_like(dkp)
            dvp[...] = jnp.zeros_like(dvp)
            dkc[...] = jnp.zeros_like(dkc)
            dvc[...] = jnp.zeros_like(dvc)
            dsink_ref[...] = jnp.zeros_like(dsink_ref)

        nt = (((1,), (1,)), ((), ()))
        tn = (((0,), (0,)), ((), ()))
        for h in range(NH):
            kvh = h // GROUP
            cs = slice(kvh * HD, (kvh + 1) * HD)
            q_h, r0, p_loc, p_ctx, p_sink = _attn_scores(n, h, q_ref, cos_ref, sa_ref, sb_ref, sink_ref, kp, kc)
            kb = kp[pl.ds(r0, BAND), cs]
            vb = vp[pl.ds(r0, BAND), cs]
            kcb = kc[:, cs]
            vcb = vc[:, cs]
            do_h = do_ref[:, h * HD:(h + 1) * HD]
            dp_loc = lax.dot_general(do_h, vb, nt, preferred_element_type=F32)
            dp_ctx = lax.dot_general(do_h, vcb, nt, preferred_element_type=F32)
            delta = jnp.sum(p_loc * dp_loc, -1, keepdims=True) + jnp.sum(p_ctx * dp_ctx, -1, keepdims=True)
            ds_loc = (p_loc * (dp_loc - delta) * ATT_SCALE).astype(BF16)
            ds_ctx = (p_ctx * (dp_ctx - delta) * ATT_SCALE).astype(BF16)
            dq = jnp.dot(ds_loc, kb, preferred_element_type=F32) + jnp.dot(ds_ctx, kcb, preferred_element_type=F32)
            cos = cos_ref[pl.ds(r0, WINDOW), :]
            sa_ = sa_ref[pl.ds(r0, WINDOW), :]
            sb_ = sb_ref[pl.ds(r0, WINDOW), :]
            dq_ref[:, h * HD:(h + 1) * HD] = _rope_t(dq, cos, sa_, sb_).astype(dq_ref.dtype)
            dkp[pl.ds(r0, BAND), cs] += lax.dot_general(ds_loc, q_h, tn, preferred_element_type=F32)
            dkc[:, cs] += lax.dot_general(ds_ctx, q_h, tn, preferred_element_type=F32)
            dvp[pl.ds(r0, BAND), cs] += lax.dot_general(p_loc.astype(BF16), do_h, tn, preferred_element_type=F32)
            dvc[:, cs] += lax.dot_general(p_ctx.astype(BF16), do_h, tn, preferred_element_type=F32)
            dsk = -jnp.sum(p_sink * delta, axis=0, keepdims=True)
            dsink_ref[h:h + 1, :] += jnp.broadcast_to(dsk, (1, HD))

        @pl.when(n == n_blocks - 1)
        def _():
            for hh in range(NKV):
                cs = slice(hh * HD, (hh + 1) * HD)
                for r0 in range(0, T, 512):
                    rs = slice(r0, r0 + 512)
                    g = dkp[WINDOW + r0:WINDOW + r0 + 512, cs]
                    dk_ref[rs, cs] = _rope_t(g, cos_ref[rs, :], sa_ref[rs, :], sb_ref[rs, :]).astype(dk_ref.dtype)
            dk_ref[T:TA, :] = dkc[...].astype(dk_ref.dtype)
            dv_ref[0:T, :] = dvp[WINDOW:WINDOW + T, :].astype(dv_ref.dtype)
            dv_ref[T:TA, :] = dvc[...].astype(dv_ref.dtype)

    full = lambda shape: pl.BlockSpec(shape, lambda n: (0, 0))
    return pl.pallas_call(
        kern,
        name="attn_bwd",
        grid=(n_blocks,),
        in_specs=[
            pl.BlockSpec((WINDOW, QW), lambda n: (n, 0)),
            pl.BlockSpec((TA, KVW), lambda n: (0, QW // KVW)),
            pl.BlockSpec((TA, KVW), lambda n: (0, QW // KVW + 1)),
            pl.BlockSpec((WINDOW, QW), lambda n: (n, 0)),
            full((T, HD)), full((T, HD)), full((T, HD)), full((1, NH)),
        ],
        out_specs=[pl.BlockSpec((WINDOW, QW), lambda n: (n, 0)), full((TA, KVW)), full((TA, KVW)), full((NH, HD))],
        out_shape=[jax.ShapeDtypeStruct((T, QW), BF16), jax.ShapeDtypeStruct((TA, KVW), BF16),
                   jax.ShapeDtypeStruct((TA, KVW), BF16), jax.ShapeDtypeStruct((NH, HD), F32)],
        scratch_shapes=[pltpu.VMEM((KPAD, KVW), BF16), pltpu.VMEM((KPAD, KVW), BF16),
                        pltpu.VMEM((C, KVW), BF16), pltpu.VMEM((C, KVW), BF16),
                        pltpu.VMEM((KPAD, KVW), F32), pltpu.VMEM((KPAD, KVW), F32),
                        pltpu.VMEM((C, KVW), F32), pltpu.VMEM((C, KVW), F32)],
        compiler_params=_cparams(("arbitrary",)),
    )(proj, proj, proj, d_attn, cos, sa, sb, sink)


def _s5_prep(a_re, a_im, log_dt, b_re, b_im, c_re, c_im):
    lam = lax.complex(a_re, a_im)
    dt = jnp.exp(log_dt)[..., None]
    lam_bar = jnp.exp(lam * dt)
    b_bar = ((lam_bar - 1.0) / lam)[..., None] * lax.complex(b_re, b_im)
    eye = jnp.eye(GBLK, dtype=F32)

    def lam_rows(v):
        return v.reshape(2, NBLK, 1, BW)

    lam_l = jnp.concatenate([lam_rows(jnp.real(lam_bar)), lam_rows(jnp.imag(lam_bar))], -1)
    lam_l = jnp.broadcast_to(lam_l, (2, NBLK, 8, 2 * BW))

    def b_blocks(v):
        v = v.reshape(2, NBLK, GBLK, SP, SG).transpose(0, 1, 2, 4, 3)
        return (v[:, :, :, :, None, :] * eye[None, None, :, None, :, None]).reshape(2, NBLK, UW, BW)

    bmat = jnp.concatenate([b_blocks(jnp.real(b_bar)), b_blocks(jnp.imag(b_bar))], -1)

    def c_blocks(v):
        v = v.reshape(2, NBLK, GBLK, SG, SP).transpose(0, 1, 2, 4, 3)
        return (v[:, :, :, :, None, :] * eye[None, None, :, None, :, None]).reshape(2, NBLK, BW, UW)

    cmat = jnp.concatenate([c_blocks(c_re), -c_blocks(c_im)], 2)
    return lam_l, bmat, cmat


def _cmul(ar, ai, br, bi):
    return ar * br - ai * bi, ar * bi + ai * br


def _shift_rows(x, rev, fill):
    r = lax.broadcasted_iota(jnp.int32, x.shape, 0)
    down = jnp.where(r == 0, fill, pltpu.roll(x, 1, 0))
    up = jnp.where(r == NSEG - 1, fill, pltpu.roll(x, NSEG - 1, 0))
    return jnp.where(rev == 0, down, up)


def _edge_row(x, rev):
    last = jnp.broadcast_to(x[NSEG - 1:NSEG, :], x.shape)
    first = jnp.broadcast_to(x[0:1, :], x.shape)
    return jnp.where(rev == 0, last, first)


def _seg_scan(get, put, base, seglen, lr, li, rev, cin):
    zero = jnp.zeros((NSEG, BW), F32)

    def rows(k):
        j = jnp.where(rev == 0, k, seglen - 1 - k)
        return pl.ds(pl.multiple_of(base + j * NSEG, NSEG), NSEG)

    def local(k, carry):
        sr, si, pr, pi = carry
        xr, xi = get(rows(k))
        tr, ti = _cmul(lr, li, sr, si)
        sr, si = tr + xr, ti + xi
        put(rows(k), sr, si)
        pr, pi = _cmul(lr, li, pr, pi)
        return sr, si, pr, pi

    er, ei, lpr, lpi = lax.fori_loop(0, seglen, local, (zero, zero, zero + 1.0, zero))
    cr, ci = _shift_rows(zero, rev, cin[0]), _shift_rows(zero, rev, cin[1])
    for _ in range(NSEG - 1):
        tr, ti = _cmul(lpr, lpi, cr, ci)
        cr, ci = _shift_rows(er + tr, rev, cin[0]), _shift_rows(ei + ti, rev, cin[1])

    def fix(k, carry):
        pr, pi = carry
        xr, xi = get(rows(k))
        tr, ti = _cmul(pr, pi, cr, ci)
        put(rows(k), xr + tr, xi + ti)
        return _cmul(lr, li, pr, pi)

    lax.fori_loop(0, seglen, fix, (lr, li))
    tr, ti = _cmul(lpr, lpi, cr, ci)
    return _edge_row(er + tr, rev), _edge_row(ei + ti, rev)


RCH = 256
CSEG = C // NSEG
TSEG = T // NSEG
UCOL0 = (QW + 2 * KVW) // UW


REGIONS = ((0, TSEG), (T, CSEG))


def _state_access(ref, lead=()):
    def get(rows):
        return ref[(*lead, rows, slice(0, BW))], ref[(*lead, rows, slice(BW, 2 * BW))]

    def put(rows, re, im):
        ref[(*lead, rows, slice(0, BW))] = re
        ref[(*lead, rows, slice(BW, 2 * BW))] = im

    return get, put


def _interleave_rows(src_ref, dst_ref, regions=REGIONS):
    for base, seglen in regions:
        def body(j, carry, base=base, seglen=seglen):
            dst_ref[pl.ds(pl.multiple_of(base + j * NSEG, NSEG), NSEG), :] = src_ref[pl.ds(base + j, NSEG, stride=seglen), :]
            return carry

        lax.fori_loop(0, seglen, body, 0, unroll=8)


def _deinterleave_rows(src_ref, dst_ref, regions=REGIONS):
    for base, seglen in regions:
        def body(j, carry, base=base, seglen=seglen):
            dst_ref[pl.ds(base + j, NSEG, stride=seglen), :] = src_ref[pl.ds(pl.multiple_of(base + j * NSEG, NSEG), NSEG), :]
            return carry

        lax.fori_loop(0, seglen, body, 0, unroll=8)


def _s5_fwd(proj, dskip, lam, bmat, cmat):
    def kern(u_ref, dk_ref, lam_ref, b_ref, c_ref, s_ref, ssm_ref, ge_ref, up_ref, yp_ref):
        d = pl.program_id(1)

        @pl.when(d == 0)
        def _():
            _interleave_rows(u_ref, up_ref)

        bm = b_ref[0, 0].astype(BF16)
        for r0 in range(0, TA, RCH):
            s_ref[0, 0, r0:r0 + RCH, :] = jnp.dot(up_ref[r0:r0 + RCH, :].astype(BF16), bm, preferred_element_type=F32)
        lr = lam_ref[0, 0, :, 0:BW]
        li = lam_ref[0, 0, :, BW:2 * BW]
        zero = jnp.zeros((NSEG, BW), F32)
        get, put = _state_access(s_ref, (0, 0))
        mid = _seg_scan(get, put, T, CSEG, lr, li, d, (zero, zero))
        _seg_scan(get, put, 0, TSEG, lr, li, d, mid)
        cm = c_ref[0, 0].astype(BF16)
        for r0 in range(0, T, RCH):
            y = jnp.dot(s_ref[0, 0, r0:r0 + RCH, :].astype(BF16), cm, preferred_element_type=F32)

            @pl.when(d == 0)
            def _(y=y, r0=r0):
                yp_ref[r0:r0 + RCH, :] = y + dk_ref[...] * up_ref[r0:r0 + RCH, :]

            @pl.when(d == 1)
            def _(y=y, r0=r0):
                yp_ref[r0:r0 + RCH, :] += y

        @pl.when(d == 1)
        def _():
            _deinterleave_rows(yp_ref, ssm_ref, REGIONS[:1])
            for r0 in range(0, T, RCH):
                ge_ref[r0:r0 + RCH, :] = _gelu(ssm_ref[r0:r0 + RCH, :]).astype(ge_ref.dtype)

    blk4 = lambda shape: pl.BlockSpec((1, 1) + shape, lambda b, d: (d, b, 0, 0))
    return pl.pallas_call(
        kern,
        name="s5_fwd",
        grid=(NBLK, 2),
        in_specs=[pl.BlockSpec((TA, UW), lambda b, d: (0, UCOL0 + b)), pl.BlockSpec((1, UW), lambda b, d: (0, b)),
                  blk4((8, 2 * BW)), blk4((UW, 2 * BW)), blk4((2 * BW, UW))],
        out_specs=[blk4((TA, 2 * BW)), pl.BlockSpec((T, UW), lambda b, d: (0, b)), pl.BlockSpec((T, UW), lambda b, d: (0, b))],
        out_shape=[jax.ShapeDtypeStruct((2, NBLK, TA, 2 * BW), F32), jax.ShapeDtypeStruct((T, SW), F32),
                   jax.ShapeDtypeStruct((T, SW), BF16)],
        scratch_shapes=[pltpu.VMEM((TA, UW), F32), pltpu.VMEM((T, UW), F32)],
        compiler_params=_cparams(("parallel", "arbitrary")),
    )(proj, dskip, lam, bmat, cmat)


def _s5_bwd(d_ge, ssm, proj, dskip, states, lam, bmat, cmat):
    nt = (((1,), (1,)), ((), ()))
    tn = (((0,), (0,)), ((), ()))

    def kern(dge_ref, ssm_ref, u_ref, dk_ref, s_ref, lam_ref, b_ref, c_ref,
             du_ref, ddk_ref, dlam_ref, db_ref, dc_ref, g_ref, dua_ref, dssm_ref, up_ref, nat_ref):
        d = pl.program_id(1)

        @pl.when(d == 0)
        def _():
            ddk = jnp.zeros((1, UW), F32)
            for r0 in range(0, T, RCH):
                rs = slice(r0, r0 + RCH)
                _, pull = jax.vjp(_gelu, ssm_ref[rs, :])
                dssm = pull(dge_ref[rs, :])[0]
                nat_ref[rs, :] = dssm
                ddk = ddk + jnp.sum(dssm * u_ref[rs, :], axis=0, keepdims=True)
            ddk_ref[...] = ddk
            _interleave_rows(nat_ref, dssm_ref, REGIONS[:1])
            _interleave_rows(u_ref, up_ref)
            for r0 in range(0, T, RCH):
                dua_ref[r0:r0 + RCH, :] = dssm_ref[r0:r0 + RCH, :] * dk_ref[...]
            dua_ref[T:TA, :] = jnp.zeros((C, UW), F32)

        cm = c_ref[0, 0].astype(BF16)
        for r0 in range(0, T, RCH):
            g_ref[r0:r0 + RCH, :] = lax.dot_general(dssm_ref[r0:r0 + RCH, :].astype(BF16), cm, nt, preferred_element_type=F32)
        g_ref[T:TA, :] = jnp.zeros((C, 2 * BW), F32)
        lr = lam_ref[0, 0, :, 0:BW]
        li = lam_ref[0, 0, :, BW:2 * BW]
        zero = jnp.zeros((NSEG, BW), F32)
        get_g, put_g = _state_access(g_ref)

        mid = _seg_scan(get_g, put_g, 0, TSEG, lr, -li, 1 - d, (zero, zero))
        _seg_scan(get_g, put_g, T, CSEG, lr, -li, 1 - d, mid)

        get_s, _ = _state_access(s_ref, (0, 0))

        def dlam_terms(g, s):
            return g[0] * s[0] + g[1] * s[1], g[1] * s[0] - g[0] * s[1]

        def dlam_region(base, seglen, s_in, acc):
            def rows(j):
                return pl.ds(pl.multiple_of(base + j * NSEG, NSEG), NSEG)

            def inner(k, acc):
                j = jnp.where(d == 0, k, seglen - 1 - k)
                jp = jnp.where(d == 0, k - 1, seglen - k)
                t = dlam_terms(get_g(rows(j)), get_s(rows(jp)))
                return acc[0] + t[0], acc[1] + t[1]

            acc = lax.fori_loop(1, seglen, inner, acc)
            jb = jnp.where(d == 0, 0, seglen - 1)
            jn = jnp.where(d == 0, seglen - 1, 0)
            sp = get_s(rows(jn))
            t = dlam_terms(get_g(rows(jb)), (_shift_rows(sp[0], d, s_in[0]), _shift_rows(sp[1], d, s_in[1])))
            return acc[0] + t[0], acc[1] + t[1]

        r_mid = jnp.where(d == 0, TA - 1, T)
        s_mid = tuple(jnp.broadcast_to(t, (NSEG, BW)) for t in get_s(pl.ds(r_mid, 1)))
        acc = dlam_region(T, CSEG, (zero, zero), (zero, zero))
        acc = dlam_region(0, TSEG, s_mid, acc)
        dlam_ref[0, 0, :, 0:BW] = acc[0]
        dlam_ref[0, 0, :, BW:2 * BW] = acc[1]

        bm = b_ref[0, 0].astype(BF16)
        db = jnp.zeros((UW, 2 * BW), F32)
        dc = jnp.zeros((2 * BW, UW), F32)
        for r0 in range(0, TA, RCH):
            rs = slice(r0, r0 + RCH)
            g = g_ref[rs, :].astype(BF16)
            dua_ref[rs, :] += lax.dot_general(g, bm, nt, preferred_element_type=F32)
            db = db + lax.dot_general(up_ref[rs, :].astype(BF16), g, tn, preferred_element_type=F32)
            if r0 < T:
                dc = dc + lax.dot_general(s_ref[0, 0, rs, :].astype(BF16), dssm_ref[rs, :].astype(BF16), tn,
                                          preferred_element_type=F32)
        db_ref[0, 0] = db
        dc_ref[0, 0] = dc

        @pl.when(d == 1)
        def _():
            _deinterleave_rows(dua_ref, nat_ref)
            du_ref[...] = nat_ref[...].astype(du_ref.dtype)

    blk4 = lambda shape: pl.BlockSpec((1, 1) + shape, lambda b, d: (d, b, 0, 0))
    lat = pl.BlockSpec((T, UW), lambda b, d: (0, b))
    vec = pl.BlockSpec((1, UW), lambda b, d: (0, b))
    return pl.pallas_call(
        kern,
        name="s5_bwd",
        grid=(NBLK, 2),
        in_specs=[lat, lat, pl.BlockSpec((TA, UW), lambda b, d: (0, UCOL0 + b)), vec,
                  blk4((TA, 2 * BW)), blk4((8, 2 * BW)), blk4((UW, 2 * BW)), blk4((2 * BW, UW))],
        out_specs=[pl.BlockSpec((TA, UW), lambda b, d: (0, b)), vec, blk4((8, 2 * BW)), blk4((UW, 2 * BW)), blk4((2 * BW, UW))],
        out_shape=[jax.ShapeDtypeStruct((TA, SW), BF16), jax.ShapeDtypeStruct((1, SW), F32),
                   jax.ShapeDtypeStruct((2, NBLK, 8, 2 * BW), F32),
                   jax.ShapeDtypeStruct((2, NBLK, UW, 2 * BW), F32), jax.ShapeDtypeStruct((2, NBLK, 2 * BW, UW), F32)],
        scratch_shapes=[pltpu.VMEM((TA, 2 * BW), F32), pltpu.VMEM((TA, UW), F32), pltpu.VMEM((T, UW), F32),
                        pltpu.VMEM((TA, UW), F32), pltpu.VMEM((TA, UW), F32)],
        compiler_params=_cparams(("parallel", "arbitrary")),
    )(d_ge, ssm, proj, dskip, states, lam, bmat, cmat)


TR = 256


def _vjp_rows(f, primals, cots, n_row):
    _, pull = jax.vjp(f, *primals)
    g = pull(cots)
    return list(g[:n_row]), list(g[n_row:])


class _GradDict(dict):
    def __init__(self, on_set=None):
        super().__init__()
        self._on_set = on_set
        self.tokens = {}

    def __setitem__(self, key, value):
        super().__setitem__(key, value)
        if self._on_set is not None:
            self._on_set(self)

    def order(self, key):
        return self.tokens.get(key, self.get(key))

    def finish(self, key, after):
        if self.on_finish is None:
            return ()
        return (self.on_finish(key, after),)

    on_finish = None


def _local_step(x, ctx, tgt, mod_lat, mod_ctx, wb, sp, on_grad=None, on_loss=None, on_finish=None):
    sh1, sc1, g1, sh2, sc2, g2 = [mod_lat[:, i * D:(i + 1) * D] for i in range(6)]
    csh1, csc1 = mod_ctx[:, 0:D], mod_ctx[:, D:2 * D]
    tabs = _rope_tables()
    sink = sp["attn_sink"].reshape(1, NH)
    dskip = sp["ssm_d"].reshape(1, SW)
    lg_mix, lb_mix = sp["ln_mix_g"].reshape(1, D), sp["ln_mix_b"].reshape(1, D)
    lg_mlp, lb_mlp = sp["ln_mlp_g"].reshape(1, D), sp["ln_mlp_b"].reshape(1, D)
    b1, b2 = sp["b_mlp1"].reshape(1, DFF), sp["b_mlp2"].reshape(1, D)
    s5_names = ("ssm_a_re", "ssm_a_im", "ssm_log_dt", "ssm_b_re", "ssm_b_im", "ssm_c_re", "ssm_c_im")
    (lam, bmat, cmat), s5_pull = jax.vjp(_s5_prep, *[sp[n] for n in s5_names])

    def ln_mod2(rv, vv):
        h = _f_ln_mod(rv[0], vv[0], vv[1])
        return [h, h], []

    h_lat, h_lat_t = _rowwise(ln_mod2, [(x, D, 0, 0)], [sc1, sh1], [(D, BF16), (D, BF16, True)], [], nrows=T, tr=TR, name="ln1_lat")
    h_ctx, h_ctx_t = _rowwise(ln_mod2, [(ctx, D, 0, 0)], [csc1, csh1], [(D, BF16), (D, BF16, True)], [], nrows=C, tr=TR,
                              name="ln1_ctx")
    h1 = jnp.concatenate([h_lat, h_ctx], 0)
    h1_t = jnp.concatenate([h_lat_t, h_ctx_t], 1)
    proj = _matmul(h1, wb["w_in"], mode="nn", name="proj", tm=768, tn=512)
    attn = _attn_fwd(proj, sink, tabs)
    states, ssm, ge = _s5_fwd(proj, dskip, lam, bmat, cmat)
    z = _matmul(ge, wb["w_glu"], mode="nn", name="glu_mm", tm=1024, tn=1024)

    def glu_act(rv, vv):
        return [_f_glu(rv[0])], []

    glu, = _rowwise(glu_act, [(z, 2 * SW, 0, 0)], [], [(SW, BF16)], [], nrows=T, tr=TR, name="glu_act")
    attn_d = _matmul(attn, wb["w_attn_up"], mode="nn", name="attn_up", tm=1024, tn=512)
    ssm_d = _matmul(glu, wb["w_ssm_up"], mode="nn", name="ssm_up", tm=1024, tn=512)
    ga_cb, gs_cb = (QW + 2 * KVW + SW) // D, (QW + 2 * KVW + SW) // D + 1

    def mix(rv, vv):
        m_ = _f_mix(*rv)
        return [m_, m_], []

    mixv, mix_t = _rowwise(mix, [(proj, D, ga_cb, 0), (proj, D, gs_cb, 0), (attn_d, D, 0, 0), (ssm_d, D, 0, 0)], [],
                           [(D, BF16), (D, BF16, True)], [], nrows=T, tr=TR, name="mix")
    y = _matmul(mixv, wb["w_out"], mode="nn", name="out_proj", tm=1024, tn=512)

    def post1(rv, vv):
        x1, h2 = _f_post1(rv[0], rv[1], *vv)
        return [x1, h2, h2], []

    x1, h2, h2_t = _rowwise(post1, [(x, D, 0, 0), (y, D, 0, 0)], [g1, lg_mix, lb_mix, sc2, sh2],
                            [(D, F32), (D, BF16), (D, BF16, True)], [], nrows=T, tr=TR, name="post1")

    def relu_sq(acc):
        r = jnp.maximum(acc, 0.0)
        return r, r * r, r * r

    r_act, act, act_t = _matmul(h2, wb["w_mlp1"], mode="nn", name="mlp1", tm=1024, tn=512, bias=b1,
                                out_dtypes=(BF16, BF16, BF16), out_t=(False, False, True), epilogue=relu_sq)
    mlp = _matmul(act, wb["w_mlp2"], mode="nn", name="mlp2", tm=1024, tn=512, tk=2048)

    def loss_fb(rv, vv):
        x1_t, mlp_t, tgt_t = rv
        g2_v, lg_v, lb_v, b2_v = vv
        f = lambda a, m, g, p, q, b: _f_loss(a, m, tgt_t, g, p, q, b)
        val, grads = jax.value_and_grad(f, argnums=(0, 1, 2, 3, 4, 5))(x1_t, mlp_t, g2_v, lg_v, lb_v, b2_v)
        dx1, dmlp, dg2, dlg, dlb, db2 = grads
        return [dx1, dmlp], [jnp.reshape(val, (1, 1)), dg2, dlg, dlb, db2]

    dx1_a, d_mlp, loss_p, d_g2, d_lg_mlp, d_lb_mlp, d_b2 = _rowwise(
        loss_fb, [(x1, D, 0, 0), (mlp, D, 0, 0), (tgt, D, 0, 0)], [g2, lg_mlp, lb_mlp, b2],
        [(D, F32), (D, BF16)], [(1, 1), (1, D), (1, D), (1, D), (1, D)], nrows=T, tr=TR, name="loss_fb")

    gw = _GradDict(on_grad)
    gw.on_finish = on_finish
    loss_done = () if on_loss is None else (on_loss(loss_p),)
    gw["w_mlp2"] = _matmul(act_t, d_mlp, mode="nn", name="dw_mlp2", out_dtypes=(BF16,), tm=1024, tn=512, after=loss_done)
    da, = (_matmul(d_mlp, wb["w_mlp2"], mode="nt", name="d_act", out_dtypes=(BF16,), tm=1024, tn=512,
                   extras=(r_act,), epilogue=lambda acc, r: (acc * (2.0 * r.astype(F32)),), after=(gw.order("w_mlp2"),)),)
    pin = gw.finish("w_mlp2", da)
    ones = jnp.ones((8, T), BF16)
    d_b1 = _matmul(ones, da, mode="nn", name="db_mlp1", tm=8, tn=2048)[0:1]
    gw["w_mlp1"] = _matmul(h2_t, da, mode="nn", name="dw_mlp1", out_dtypes=(BF16,), tm=1024, tn=512, after=pin)
    dh2 = _matmul(da, wb["w_mlp1"], mode="nt", name="d_h2", tm=1024, tn=512, tk=2048, after=(gw.order("w_mlp1"),))
    pin = gw.finish("w_mlp1", dh2)

    def post1_b(rv, vv):
        x_t, y_t, dx1_t, dh2_t = rv
        gr, gv = _vjp_rows(_f_post1, (x_t, y_t, *vv), (dx1_t, dh2_t), 2)
        return [gr[0], gr[1]], gv

    dx_a, dy, d_g1, d_lg_mix, d_lb_mix, d_sc2, d_sh2 = _rowwise(
        post1_b, [(x, D, 0, 0), (y, D, 0, 0), (dx1_a, D, 0, 0), (dh2, D, 0, 0)], [g1, lg_mix, lb_mix, sc2, sh2],
        [(D, F32), (D, BF16)], [(1, D)] * 5, nrows=T, tr=TR, name="post1_bwd")
    gw["w_out"] = _matmul(mix_t, dy, mode="nn", name="dw_out", out_dtypes=(BF16,), tm=1024, tn=512, after=pin)
    dmix = _matmul(dy, wb["w_out"], mode="nt", name="d_mix", tm=1024, tn=512, after=(gw.order("w_out"),))

    def mix_b(rv, vv):
        gr, _ = _vjp_rows(_f_mix, tuple(rv[:4]), rv[4], 4)
        return gr, []

    d_ga, d_gs, d_attn_d, d_ssm_d = _rowwise(
        mix_b, [(proj, D, ga_cb, 0), (proj, D, gs_cb, 0), (attn_d, D, 0, 0), (ssm_d, D, 0, 0), (dmix, D, 0, 0)], [],
        [(D, BF16)] * 4, [], nrows=T, tr=TR, name="mix_bwd")
    gw["w_attn_up"] = _matmul(attn, d_attn_d, mode="tn", name="dw_attn_up", out_dtypes=(BF16,), tm=512, tn=1024, tk=1024)
    d_attn = _matmul(d_attn_d, wb["w_attn_up"], mode="nt", name="d_attn", out_dtypes=(BF16,), tm=1024, tn=512)
    gw["w_ssm_up"] = _matmul(glu, d_ssm_d, mode="tn", name="dw_ssm_up", out_dtypes=(BF16,), tm=512, tn=1024, tk=1024)
    d_glu = _matmul(d_ssm_d, wb["w_ssm_up"], mode="nt", name="d_glu", tm=1024, tn=512, after=(gw.order("w_attn_up"), gw.order("w_ssm_up")))

    def glu_b(rv, vv):
        gr, _ = _vjp_rows(_f_glu, (rv[0],), rv[1], 1)
        return gr, []

    dz, = _rowwise(glu_b, [(z, 2 * SW, 0, 0), (d_glu, SW, 0, 0)], [], [(2 * SW, BF16)], [], nrows=T, tr=TR, name="glu_bwd")
    gw["w_glu"] = _matmul(ge, dz, mode="tn", name="dw_glu", out_dtypes=(BF16,), tm=512, tn=1024, tk=1024)
    d_ge = _matmul(dz, wb["w_glu"], mode="nt", name="d_ge", tm=1024, tn=512, after=(gw.order("w_glu"),))

    du_all, d_dskip, dlam, dbmat, dcmat = _s5_bwd(d_ge, ssm, proj, dskip, states, lam, bmat, cmat)
    s5_grads = s5_pull((dlam, dbmat, dcmat))
    pin = gw.finish("w_glu", du_all)

    dq, dk, dv, dsink = _attn_bwd(proj, d_attn, sink, tabs)
    zc = lambda w: jnp.zeros((C, w), BF16)
    dproj = jnp.concatenate([
        jnp.concatenate([dq, zc(QW)], 0), dk, dv, du_all,
        jnp.concatenate([d_ga, zc(D)], 0), jnp.concatenate([d_gs, zc(D)], 0)], 1)
    gw["w_in"] = _matmul(h1_t, dproj, mode="nn", name="dw_in", out_dtypes=(BF16,), tm=1024, tn=512, after=pin)
    dh1 = _matmul(dproj, wb["w_in"], mode="nt", name="d_h1", tm=768, tn=512, tk=2048, after=(gw.order("w_in"),))
    pin = gw.finish("w_in", dh1)

    def ln1_b(rv, vv):
        x_t, dh_t, dxa_t = rv
        gr, gv = _vjp_rows(_f_ln_mod, (x_t, vv[0], vv[1]), dh_t, 1)
        return [gr[0] + dxa_t], gv

    grad_x, d_sc1, d_sh1 = _rowwise(ln1_b, [(x, D, 0, 0), (dh1, D, 0, 0), (dx_a, D, 0, 0)], [sc1, sh1],
                                    [(D, F32)], [(1, D), (1, D)], nrows=T, tr=TR, name="ln1_lat_bwd", after=pin)

    def ln1c_b(rv, vv):
        _, gv = _vjp_rows(_f_ln_mod, (rv[0], vv[0], vv[1]), rv[1], 1)
        return [], gv

    d_csc1, d_csh1 = _rowwise(ln1c_b, [(ctx, D, 0, 0), (dh1, D, 0, T // TR)], [csc1, csh1],
                              [], [(1, D), (1, D)], nrows=C, tr=TR, name="ln1_ctx_bwd")

    d_mod_lat = jnp.concatenate([d_sh1, d_sc1, d_g1, d_sh2, d_sc2, d_g2], 1)
    zv = jnp.zeros((1, D), F32)
    d_mod_ctx = jnp.concatenate([d_csh1, d_csc1, zv, zv, zv, zv], 1)
    gs = {n: g for n, g in zip(s5_names, s5_grads)}
    gs["attn_sink"] = dsink[:, 0]
    gs["ssm_d"] = d_dskip
    gs["ln_mix_g"], gs["ln_mix_b"] = d_lg_mix, d_lb_mix
    gs["ln_mlp_g"], gs["ln_mlp_b"] = d_lg_mlp, d_lb_mlp
    gs["b_mlp1"], gs["b_mlp2"] = d_b1, d_b2
    return loss_p, grad_x, d_mod_lat, d_mod_ctx, gw, gs


def _my_pos():
    return lax.axis_index("x"), lax.axis_index("y"), lax.axis_index("c")


def _flip(p, bit):
    return 1 - p if bit else p


def _peer(pos, k):
    x, y, c = pos
    return (_flip(x, (k >> 2) & 1), _flip(y, (k >> 1) & 1), _flip(c, k & 1))


def _lin(pos):
    return 4 * pos[0] + 2 * pos[1] + pos[2]


def _allgather_small(v, name):
    r, w = v.shape

    def body(v_ref, out_ref, send_sems, recv_sems, local_sem):
        me = _my_pos()
        mine = pltpu.make_async_copy(v_ref, out_ref.at[_lin(me)], local_sem)
        mine.start()
        sends = []
        for k in range(1, N_DEV):
            cp = pltpu.make_async_remote_copy(src_ref=v_ref, dst_ref=out_ref.at[_lin(me)], send_sem=send_sems.at[k - 1],
                                              recv_sem=recv_sems.at[k - 1], device_id=_peer(me, k), device_id_type=MESH)
            cp.start()
            sends.append(cp)
        for k in range(1, N_DEV):
            peer = _peer(me, k)
            pltpu.make_async_remote_copy(src_ref=v_ref, dst_ref=out_ref.at[_lin(peer)], send_sem=send_sems.at[k - 1],
                                         recv_sem=recv_sems.at[k - 1], device_id=peer, device_id_type=MESH).wait_recv()
        for cp in sends:
            cp.wait_send()
        mine.wait()

    return pl.pallas_call(
        body,
        name=name,
        out_shape=jax.ShapeDtypeStruct((N_DEV, r, w), v.dtype),
        in_specs=[pl.BlockSpec(memory_space=pltpu.VMEM)],
        out_specs=pl.BlockSpec(memory_space=pltpu.VMEM),
        scratch_shapes=[pltpu.SemaphoreType.DMA((N_DEV - 1,)), pltpu.SemaphoreType.DMA((N_DEV - 1,)), pltpu.SemaphoreType.DMA],
        compiler_params=pltpu.CompilerParams(vmem_limit_bytes=VMEM_LIMIT_BYTES),
    )(v)


def _block_of(ref, kind, idx, n):
    start = pl.multiple_of(idx * n, 128)
    if kind == "col":
        return ref.at[:, pl.ds(start, n)]
    return ref.at[pl.ds(start, n), :]


def _allgather_weights(shards, kinds):
    nt = len(shards)
    out_shape = []
    for s, kind in zip(shards, kinds):
        k, n = s.shape
        out_shape.append(jax.ShapeDtypeStruct((k, n * N_DEV) if kind == "col" else (k * N_DEV, n), s.dtype))

    def body(*refs):
        ins, outs = refs[:nt], refs[nt:2 * nt]
        send_sems, recv_sems, local_sems = refs[2 * nt:]
        x, y, c = _my_pos()
        me, sibling = (x, y, c), (x, y, 1 - c)
        chips = [(1 - x, y), (x, 1 - y), (1 - x, 1 - y)]

        def blk(t, pos):
            n = shards[t].shape[1] if kinds[t] == "col" else shards[t].shape[0]
            return _block_of(outs[t], kinds[t], _lin(pos), n)

        def copy(t, k, block, to, src=None):
            return pltpu.make_async_remote_copy(src_ref=blk(t, block) if src is None else src, dst_ref=blk(t, block),
                                                send_sem=send_sems.at[t, k], recv_sem=recv_sems.at[t, k],
                                                device_id=to, device_id_type=MESH)

        local, sends = [], []
        for t in range(nt):
            mine = pltpu.make_async_copy(ins[t], blk(t, me), local_sems.at[t])
            mine.start()
            local.append(mine)
            first = [copy(t, 0, me, sibling, src=ins[t])]
            first += [copy(t, 1 + j, me, (*chip, c), src=ins[t]) for j, chip in enumerate(chips)]
            for cp in first:
                cp.start()
            sends += first
        for t in range(nt):
            for j, chip in enumerate(chips):
                copy(t, 1 + j, (*chip, c), me).wait_recv()
                fwd = copy(t, 4 + j, (*chip, c), sibling)
                fwd.start()
                sends.append(fwd)
        for t in range(nt):
            copy(t, 0, sibling, me).wait_recv()
            for j, chip in enumerate(chips):
                copy(t, 4 + j, (*chip, 1 - c), me).wait_recv()
        for cp in sends:
            cp.wait_send()
        for cp in local:
            cp.wait()

    any_spec = pl.BlockSpec(memory_space=pl.ANY)
    return pl.pallas_call(
        body,
        name="allgather_weights",
        out_shape=out_shape,
        in_specs=[any_spec] * nt,
        out_specs=[any_spec] * nt,
        scratch_shapes=[pltpu.SemaphoreType.DMA((nt, N_DEV - 1)), pltpu.SemaphoreType.DMA((nt, N_DEV - 1)),
                        pltpu.SemaphoreType.DMA((nt,))],
    )(*shards)


def _handshake(peers):
    barrier = pltpu.get_barrier_semaphore()
    for peer in peers:
        pl.semaphore_signal(barrier, inc=1, device_id=peer, device_id_type=MESH)
    pl.semaphore_wait(barrier, len(peers))


def _allgather_weights_seq(shards, kinds, name, collective_id):
    nt = len(shards)
    hbm = pltpu.MemorySpace.HBM
    ins = [jax.new_ref(s, memory_space=hbm) for s in shards]
    outs = []
    for s, kind in zip(shards, kinds):
        k, n = s.shape
        shape = (k, n * N_DEV) if kind == "col" else (k * N_DEV, n)
        outs.append(jax.empty_ref(jax.ShapeDtypeStruct(shape, s.dtype), memory_space=hbm))

    @functools.partial(
        pl.kernel, mesh=plsc.ScalarSubcoreMesh(axis_name="seq", num_cores=1), name=name,
        scratch_types=(pltpu.SemaphoreType.DMA((nt, N_DEV - 1)), pltpu.SemaphoreType.DMA((nt, N_DEV - 1)),
                       pltpu.SemaphoreType.DMA((nt,))),
        compiler_params=pltpu.CompilerParams(collective_id=collective_id))
    def launch(send_sems, recv_sems, local_sems):
        x, y, c = _my_pos()
        me, sibling = (x, y, c), (x, y, 1 - c)
        chips = [(1 - x, y), (x, 1 - y), (1 - x, 1 - y)]
        _handshake([sibling] + [(*chip, c) for chip in chips])

        def blk(t, pos):
            n = shards[t].shape[1] if kinds[t] == "col" else shards[t].shape[0]
            return _block_of(outs[t], kinds[t], _lin(pos), n)

        def copy(t, k, block, to, src=None):
            return pltpu.make_async_remote_copy(src_ref=blk(t, block) if src is None else src, dst_ref=blk(t, block),
                                                send_sem=send_sems.at[t, k], recv_sem=recv_sems.at[t, k],
                                                device_id=to, device_id_type=MESH)

        local, sends = [], []
        for t in range(nt):
            mine = pltpu.make_async_copy(ins[t], blk(t, me), local_sems.at[t])
            mine.start()
            local.append(mine)
            first = [copy(t, 0, me, sibling, src=ins[t])]
            first += [copy(t, 1 + j, me, (*chip, c), src=ins[t]) for j, chip in enumerate(chips)]
            for cp in first:
                cp.start()
            sends += first
        for t in range(nt):
            for j, chip in enumerate(chips):
                copy(t, 1 + j, (*chip, c), me).wait_recv()
                fwd = copy(t, 4 + j, (*chip, c), sibling)
                fwd.start()
                sends.append(fwd)
        for t in range(nt):
            copy(t, 0, sibling, me).wait_recv()
            for j, chip in enumerate(chips):
                copy(t, 4 + j, (*chip, 1 - c), me).wait_recv()
        for cp in sends:
            cp.wait_send()
        for cp in local:
            cp.wait()

    launch()
    return [o[...] for o in outs]


def _allgather_small_seq(v, name, collective_id):
    hbm = pltpu.MemorySpace.HBM
    src = jax.new_ref(v, memory_space=hbm)
    out = jax.empty_ref(jax.ShapeDtypeStruct((N_DEV,) + v.shape, v.dtype), memory_space=hbm)

    @functools.partial(
        pl.kernel, mesh=plsc.ScalarSubcoreMesh(axis_name="seq", num_cores=1), name=name,
        scratch_types=(pltpu.SemaphoreType.DMA((N_DEV - 1,)), pltpu.SemaphoreType.DMA((N_DEV - 1,)), pltpu.SemaphoreType.DMA),
        compiler_params=pltpu.CompilerParams(collective_id=collective_id))
    def launch(send_sems, recv_sems, local_sem):
        me = _my_pos()
        _handshake([_peer(me, k) for k in range(1, N_DEV)])
        mine = pltpu.make_async_copy(src, out.at[_lin(me)], local_sem)
        mine.start()
        sends = []
        for k in range(1, N_DEV):
            cp = pltpu.make_async_remote_copy(src_ref=src, dst_ref=out.at[_lin(me)], send_sem=send_sems.at[k - 1],
                                              recv_sem=recv_sems.at[k - 1], device_id=_peer(me, k), device_id_type=MESH)
            cp.start()
            sends.append(cp)
        for k in range(1, N_DEV):
            peer = _peer(me, k)
            pltpu.make_async_remote_copy(src_ref=src, dst_ref=out.at[_lin(peer)], send_sem=send_sems.at[k - 1],
                                         recv_sem=recv_sems.at[k - 1], device_id=peer, device_id_type=MESH).wait_recv()
        for cp in sends:
            cp.wait_send()
        mine.wait()

    launch()
    return out[...]


N_CHIP = N_DEV // 2


def _chip_of(pos):
    return 2 * pos[0] + pos[1]


def _pair_exchange_seq(grads, kinds, name, collective_id):
    nt = len(grads)
    hbm = pltpu.MemorySpace.HBM
    shard_shapes = _shard_shapes(grads, kinds)
    ins = [jax.new_ref(g, memory_space=hbm) for g in grads]
    outs = [jax.empty_ref(jax.ShapeDtypeStruct((N_CHIP,) + s, g.dtype), memory_space=hbm) for s, g in zip(shard_shapes, grads)]

    @functools.partial(
        pl.kernel, mesh=plsc.ScalarSubcoreMesh(axis_name="seq", num_cores=1), name=name,
        scratch_types=(pltpu.SemaphoreType.DMA((nt, N_CHIP)), pltpu.SemaphoreType.DMA((nt, N_CHIP))),
        compiler_params=pltpu.CompilerParams(collective_id=collective_id))
    def launch(send_sems, recv_sems):
        x, y, c = _my_pos()
        sibling = (x, y, 1 - c)
        _handshake([sibling])
        copies = []
        for t in range(nt):
            n = shard_shapes[t][1] if kinds[t] == "col" else shard_shapes[t][0]
            for q in range(N_CHIP):
                cp = pltpu.make_async_remote_copy(src_ref=_block_of(ins[t], kinds[t], 2 * q + (1 - c), n), dst_ref=outs[t].at[q],
                                                  send_sem=send_sems.at[t, q], recv_sem=recv_sems.at[t, q],
                                                  device_id=sibling, device_id_type=MESH)
                cp.start()
                copies.append(cp)
        for cp in copies:
            cp.wait_recv()
        for cp in copies:
            cp.wait_send()

    launch()
    return [o[...] for o in outs]


def _pair_add(g, half, kind, name, after=()):
    nq, k, ns = half.shape
    tr = min(k, 512)
    c_idx = lax.axis_index("c").astype(jnp.int32).reshape(1)
    if kind == "col":
        g_spec = pl.BlockSpec((tr, ns), lambda q, i, c_ref: (i, 2 * q + c_ref[0]))
    else:
        g_spec = pl.BlockSpec((tr, ns), lambda q, i, c_ref: ((2 * q + c_ref[0]) * (k // tr) + i, 0))
    n_after = len(after)

    def kern(c_ref, g_ref, h_ref, *rest):
        o_ref = rest[n_after]
        o_ref[0] = (g_ref[...].astype(F32) + h_ref[0].astype(F32)).astype(o_ref.dtype)

    return pl.pallas_call(
        kern,
        name=name,
        grid_spec=pltpu.PrefetchScalarGridSpec(
            num_scalar_prefetch=1,
            grid=(nq, k // tr),
            in_specs=[g_spec, pl.BlockSpec((1, tr, ns), lambda q, i, c_ref: (q, i, 0))] + [pl.BlockSpec(memory_space=pl.ANY)] * n_after,
            out_specs=pl.BlockSpec((1, tr, ns), lambda q, i, c_ref: (q, i, 0)),
        ),
        out_shape=jax.ShapeDtypeStruct(half.shape, half.dtype),
        compiler_params=_cparams(("parallel", "parallel")),
    )(c_idx, g, half, *after)


def _chip_exchange_seq(psums, name, collective_id):
    nt = len(psums)
    hbm = pltpu.MemorySpace.HBM
    ins = [jax.new_ref(s, memory_space=hbm) for s in psums]
    outs = [jax.empty_ref(jax.ShapeDtypeStruct(s.shape, s.dtype), memory_space=hbm) for s in psums]

    @functools.partial(
        pl.kernel, mesh=plsc.ScalarSubcoreMesh(axis_name="seq", num_cores=1), name=name,
        scratch_types=(pltpu.SemaphoreType.DMA((nt, N_CHIP - 1)), pltpu.SemaphoreType.DMA((nt, N_CHIP - 1)),
                       pltpu.SemaphoreType.DMA((nt,))),
        compiler_params=pltpu.CompilerParams(collective_id=collective_id))
    def launch(send_sems, recv_sems, local_sems):
        me = _my_pos()
        peers = [_peer(me, k) for k in (2, 4, 6)]
        _handshake(peers)
        mine = _chip_of(me)
        local, sends = [], []
        for t in range(nt):
            cp = pltpu.make_async_copy(ins[t].at[mine], outs[t].at[mine], local_sems.at[t])
            cp.start()
            local.append(cp)
            for j, peer in enumerate(peers):
                cp = pltpu.make_async_remote_copy(src_ref=ins[t].at[_chip_of(peer)], dst_ref=outs[t].at[mine],
                                                  send_sem=send_sems.at[t, j], recv_sem=recv_sems.at[t, j],
                                                  device_id=peer, device_id_type=MESH)
                cp.start()
                sends.append(cp)
        for t in range(nt):
            for j, peer in enumerate(peers):
                pltpu.make_async_remote_copy(src_ref=ins[t].at[mine], dst_ref=outs[t].at[_chip_of(peer)],
                                             send_sem=send_sems.at[t, j], recv_sem=recv_sems.at[t, j],
                                             device_id=peer, device_id_type=MESH).wait_recv()
        for cp in sends:
            cp.wait_send()
        for cp in local:
            cp.wait()

    launch()
    return [o[...] for o in outs]


def _scatter_grads_seq(grads, kinds, name, collective_id):
    nt = len(grads)
    hbm = pltpu.MemorySpace.HBM
    shard_shapes = []
    for g, kind in zip(grads, kinds):
        k, n = g.shape
        shard_shapes.append((k, n // N_DEV) if kind == "col" else (k // N_DEV, n))
    ins = [jax.new_ref(g, memory_space=hbm) for g in grads]
    outs = [jax.empty_ref(jax.ShapeDtypeStruct((N_DEV,) + s, g.dtype), memory_space=hbm) for s, g in zip(shard_shapes, grads)]

    @functools.partial(
        pl.kernel, mesh=plsc.ScalarSubcoreMesh(axis_name="seq", num_cores=1), name=name,
        scratch_types=(pltpu.SemaphoreType.DMA((nt, N_DEV - 1)), pltpu.SemaphoreType.DMA((nt, N_DEV - 1)),
                       pltpu.SemaphoreType.DMA((nt,))),
        compiler_params=pltpu.CompilerParams(collective_id=collective_id))
    def launch(send_sems, recv_sems, local_sems):
        me = _my_pos()
        _handshake([_peer(me, k) for k in range(1, N_DEV)])

        def blk(t, pos):
            n = shard_shapes[t][1] if kinds[t] == "col" else shard_shapes[t][0]
            return _block_of(ins[t], kinds[t], _lin(pos), n)

        local, sends = [], []
        for t in range(nt):
            cp = pltpu.make_async_copy(blk(t, me), outs[t].at[_lin(me)], local_sems.at[t])
            cp.start()
            local.append(cp)
            for k in range(1, N_DEV):
                peer = _peer(me, k)
                cp = pltpu.make_async_remote_copy(src_ref=blk(t, peer), dst_ref=outs[t].at[_lin(me)], send_sem=send_sems.at[t, k - 1],
                                                  recv_sem=recv_sems.at[t, k - 1], device_id=peer, device_id_type=MESH)
                cp.start()
                sends.append(cp)
        for t in range(nt):
            for k in range(1, N_DEV):
                peer = _peer(me, k)
                pltpu.make_async_remote_copy(src_ref=blk(t, me), dst_ref=outs[t].at[_lin(peer)], send_sem=send_sems.at[t, k - 1],
                                             recv_sem=recv_sems.at[t, k - 1], device_id=peer, device_id_type=MESH).wait_recv()
        for cp in sends:
            cp.wait_send()
        for cp in local:
            cp.wait()

    launch()
    return [o[...] for o in outs]


_HBM_SPEC = pl.BlockSpec(memory_space=pltpu.HBM)
_SEM_SPEC = pl.BlockSpec(memory_space=pltpu.SEMAPHORE)
_EFFECT = pltpu.SideEffectType.DATAFLOW_SIDE_EFFECTING
LOCAL_CHUNKS = 16


def _shard_shapes(grads, kinds):
    return [(g.shape[0], g.shape[1] // N_DEV) if kind == "col" else (g.shape[0] // N_DEV, g.shape[1]) for g, kind in zip(grads, kinds)]


def _scatter_copies(g_refs, land_refs, send_sems, recv_sems, kinds, shard_shapes):
    me = _my_pos()
    copies = []
    for t in range(len(g_refs)):
        n = shard_shapes[t][1] if kinds[t] == "col" else shard_shapes[t][0]
        for k in range(1, N_DEV):
            peer = _peer(me, k)
            copies.append(pltpu.make_async_remote_copy(
                src_ref=_block_of(g_refs[t], kinds[t], _lin(peer), n), dst_ref=land_refs[t].at[_lin(me)],
                send_sem=send_sems.at[t * (N_DEV - 1) + k - 1], recv_sem=recv_sems.at[t * (N_DEV - 1) + k - 1],
                device_id=peer, device_id_type=MESH))
    return copies


def _scatter_start(grads, kinds, name):
    nt = len(grads)
    shard_shapes = _shard_shapes(grads, kinds)

    def body(*refs):
        g_refs, land_refs = refs[:nt], refs[nt:2 * nt]
        send_sems, recv_sems = refs[2 * nt], refs[2 * nt + 1]
        token = refs[2 * nt + 2 + 2 * nt]
        local_sems = refs[-1]
        me = _my_pos()
        local = []
        for t in range(nt):
            n = shard_shapes[t][1] if kinds[t] == "col" else shard_shapes[t][0]
            src, dst = _block_of(g_refs[t], kinds[t], _lin(me), n), land_refs[t].at[_lin(me)]
            rows = shard_shapes[t][0] // LOCAL_CHUNKS
            for ch in range(LOCAL_CHUNKS):
                rs = pl.ds(ch * rows, rows)
                cp = pltpu.make_async_copy(src.at[rs, :], dst.at[rs, :], local_sems.at[t * LOCAL_CHUNKS + ch])
                cp.start()
                local.append(cp)
        token[...] = jnp.zeros_like(token)
        for cp in local:
            cp.wait()
        for cp in _scatter_copies(g_refs, land_refs, send_sems, recv_sems, kinds, shard_shapes):
            cp.start()

    lands = [pltpu.with_memory_space_constraint(lax.empty((N_DEV,) + s, g.dtype), pltpu.HBM) for s, g in zip(shard_shapes, grads)]
    sem_shape = pltpu.SemaphoreType.DMA((nt * (N_DEV - 1),))
    out = pl.pallas_call(
        body,
        name=name,
        out_shape=(sem_shape, sem_shape, *[pltpu.HBM(g.shape, g.dtype) for g in grads],
                   *[pltpu.HBM(l.shape, l.dtype) for l in lands], jax.ShapeDtypeStruct((8, 128), F32)),
        in_specs=[_HBM_SPEC] * (2 * nt),
        out_specs=(_SEM_SPEC, _SEM_SPEC, *[_HBM_SPEC] * (2 * nt), pl.BlockSpec(memory_space=pltpu.VMEM)),
        input_output_aliases={i: 2 + i for i in range(2 * nt)},
        scratch_shapes=[pltpu.SemaphoreType.DMA((nt * LOCAL_CHUNKS,))],
        compiler_params=pltpu.CompilerParams(has_side_effects=_EFFECT),
    )(*[pltpu.with_memory_space_constraint(g, pltpu.HBM) for g in grads], *lands)
    return out[0], out[1], list(out[2:2 + nt]), list(out[2 + nt:2 + 2 * nt]), out[-1]


def _scatter_wait(send_sems, recv_sems, g_thru, land_thru, kinds, after, name):
    nt = len(g_thru)
    shard_shapes = _shard_shapes(g_thru, kinds)

    def body(*refs):
        g_refs, land_refs = refs[:nt], refs[nt:2 * nt]
        send_sems, recv_sems = refs[2 * nt], refs[2 * nt + 1]
        for cp in _scatter_copies(g_refs, land_refs, send_sems, recv_sems, kinds, shard_shapes):
            cp.wait_send()
            cp.wait_recv()

    out = pl.pallas_call(
        body,
        name=name,
        out_shape=tuple(pltpu.HBM(a.shape, a.dtype) for a in (*g_thru, *land_thru)),
        in_specs=[*[_HBM_SPEC] * (2 * nt), _SEM_SPEC, _SEM_SPEC, pl.BlockSpec(memory_space=pl.ANY)],
        out_specs=tuple([_HBM_SPEC] * (2 * nt)),
        input_output_aliases={i: i for i in range(2 * nt)},
        compiler_params=pltpu.CompilerParams(has_side_effects=_EFFECT),
    )(*g_thru, *land_thru, send_sems, recv_sems, after)
    return list(out[nt:])


def _scatter_grads(grads, kinds):
    nt = len(grads)
    shard_shapes = []
    for g, kind in zip(grads, kinds):
        k, n = g.shape
        shard_shapes.append((k, n // N_DEV) if kind == "col" else (k // N_DEV, n))

    def body(*refs):
        ins, outs = refs[:nt], refs[nt:2 * nt]
        send_sems, recv_sems, local_sems = refs[2 * nt:]
        me = _my_pos()

        def blk(t, pos):
            n = shard_shapes[t][1] if kinds[t] == "col" else shard_shapes[t][0]
            return _block_of(ins[t], kinds[t], _lin(pos), n)

        local, sends = [], []
        for t in range(nt):
            cp = pltpu.make_async_copy(blk(t, me), outs[t].at[_lin(me)], local_sems.at[t])
            cp.start()
            local.append(cp)
            for k in range(1, N_DEV):
                peer = _peer(me, k)
                cp = pltpu.make_async_remote_copy(src_ref=blk(t, peer), dst_ref=outs[t].at[_lin(me)], send_sem=send_sems.at[t, k - 1],
                                                  recv_sem=recv_sems.at[t, k - 1], device_id=peer, device_id_type=MESH)
                cp.start()
                sends.append(cp)
        for t in range(nt):
            for k in range(1, N_DEV):
                peer = _peer(me, k)
                pltpu.make_async_remote_copy(src_ref=blk(t, me), dst_ref=outs[t].at[_lin(peer)], send_sem=send_sems.at[t, k - 1],
                                             recv_sem=recv_sems.at[t, k - 1], device_id=peer, device_id_type=MESH).wait_recv()
        for cp in sends:
            cp.wait_send()
        for cp in local:
            cp.wait()

    any_spec = pl.BlockSpec(memory_space=pl.ANY)
    return pl.pallas_call(
        body,
        name="scatter_grads",
        out_shape=[jax.ShapeDtypeStruct((N_DEV,) + s, g.dtype) for s, g in zip(shard_shapes, grads)],
        in_specs=[any_spec] * nt,
        out_specs=[any_spec] * nt,
        scratch_shapes=[pltpu.SemaphoreType.DMA((nt, N_DEV - 1)), pltpu.SemaphoreType.DMA((nt, N_DEV - 1)),
                        pltpu.SemaphoreType.DMA((nt,))],
    )(*grads)


def _adam(g_slots, w, m, v, *, tr, name, after=()):
    ns, r, wd = g_slots.shape
    tr = min(tr, r)
    assert r % tr == 0, (name, r, tr)
    c1 = 1.0 - ADAM_B1 ** ADAM_STEP
    c2 = 1.0 - ADAM_B2 ** ADAM_STEP
    n_after = len(after)

    def kern(g_ref, w_ref, m_ref, v_ref, *rest):
        go_ref, d_ref, mo_ref, vo_ref = rest[n_after:]
        g = g_ref[0].astype(F32)
        for s in range(1, ns):
            g = g + g_ref[s].astype(F32)
        m_new = ADAM_B1 * m_ref[...] + (1.0 - ADAM_B1) * g
        v_new = ADAM_B2 * v_ref[...] + (1.0 - ADAM_B2) * (g * g)
        m_hat = m_new / c1
        v_hat = v_new / c2
        go_ref[...] = g
        d_ref[...] = -ADAM_LR * (m_hat / (jnp.sqrt(v_hat) + ADAM_EPS) + ADAM_WD * w_ref[...])
        mo_ref[...] = m_new
        vo_ref[...] = v_new

    tile = pl.BlockSpec((tr, wd), lambda i: (i, 0))
    return pl.pallas_call(
        kern,
        name=name,
        grid=(r // tr,),
        in_specs=[pl.BlockSpec((ns, tr, wd), lambda i: (0, i, 0)), tile, tile, tile] + [pl.BlockSpec(memory_space=pl.ANY)] * n_after,
        out_specs=[tile] * 4,
        out_shape=[jax.ShapeDtypeStruct((r, wd), F32)] * 4,
        compiler_params=_cparams(("parallel",)),
    )(g_slots, w, m, v, *after)


SMALL = ("c_ctx", "b_ada", "attn_sink", "ssm_a_re", "ssm_a_im", "ssm_log_dt", "ssm_b_re", "ssm_b_im", "ssm_c_re", "ssm_c_im",
         "ssm_d", "ln_mix_g", "ln_mix_b", "b_mlp1", "b_mlp2", "ln_mlp_g", "ln_mlp_b")
BIG = ("w_in", "w_glu", "w_attn_up", "w_ssm_up", "w_out", "w_mlp1", "w_mlp2")
BIG_KIND = ("col", "col", "col", "col", "row", "col", "row")
AG_GROUPS = (("w_in",), ("w_glu", "w_attn_up", "w_ssm_up", "w_out"), ("w_mlp1",), ("w_mlp2",))
AG_COLLECTIVE_ID0 = 1
RS_GROUPS = (("w_mlp2",), ("w_mlp1",), ("w_out", "w_attn_up", "w_ssm_up", "w_glu"), ("w_in",))
RS_COLLECTIVE_ID0 = AG_COLLECTIVE_ID0 + len(AG_GROUPS)
SMALL_EARLY = ("ssm_a_re", "ssm_a_im", "ssm_log_dt", "ssm_b_re", "ssm_b_im", "ssm_c_re", "ssm_c_im", "ssm_d")
SMALL_LATE = tuple(n for n in SMALL if n not in SMALL_EARLY)
SMALL_COLLECTIVE_ID0 = RS_COLLECTIVE_ID0 + 2 * len(RS_GROUPS)
LANES = 128


def _pack(parts):
    rows = []
    for p in parts:
        flat = p.reshape(-1).astype(F32)
        pad = (-flat.shape[0]) % LANES
        rows.append(jnp.pad(flat, (0, pad)).reshape(-1, LANES))
    packed = jnp.concatenate(rows, 0)
    return jnp.pad(packed, ((0, (-packed.shape[0]) % 8), (0, 0)))


def _unpack(packed, shapes):
    out, r0 = [], 0
    for s in shapes:
        n = math.prod(s)
        nr = -(-n // LANES)
        out.append(packed[r0:r0 + nr].reshape(-1)[:n].reshape(s))
        r0 += nr
    return out


WEIGHTS = ("c_ctx", "w_ada", "b_ada", "w_in", "attn_sink", "ssm_a_re", "ssm_a_im", "ssm_log_dt", "ssm_b_re", "ssm_b_im",
           "ssm_c_re", "ssm_c_im", "ssm_d", "w_glu", "w_attn_up", "w_ssm_up", "w_out", "ln_mix_g", "ln_mix_b", "w_mlp1",
           "b_mlp1", "w_mlp2", "b_mlp2", "ln_mlp_g", "ln_mlp_b")
ADA_COLS = 6 * D // N_DEV


def _step(x, c, ctx, loss_target, p, m, v):
    me = _lin(_my_pos())
    x2, ctx2, tgt2 = x[0], ctx[0], loss_target[0]

    wb = {}
    for gi, group in enumerate(AG_GROUPS):
        full = _allgather_weights_seq([p[n][0].astype(BF16) for n in group], [BIG_KIND[BIG.index(n)] for n in group],
                                      "allgather_seq%d" % gi, AG_COLLECTIVE_ID0 + gi)
        wb.update(zip(group, full))

    c_all = _allgather_small(jnp.broadcast_to(c, (8, D)), "gather_c")[:, 0, :]
    cc = p["c_ctx"].reshape(1, D)
    s_in = jnp.concatenate([c_all, cc, jnp.zeros((7, D), F32)], 0)
    s_act, = _rowwise(lambda rv, vv: ([_silu(rv[0])], []), [(s_in, D, 0, 0)], [], [(D, F32)], [], nrows=16, tr=16, name="silu_c")
    b_mine = lax.dynamic_slice_in_dim(p["b_ada"], me * ADA_COLS, ADA_COLS, axis=1)
    mod_part = _matmul(s_act, p["w_ada"][0], mode="nn", name="ada_fwd", tm=16, tn=512, bias=b_mine)
    mod_all = _allgather_small(mod_part, "gather_mod")
    mod_lat = lax.dynamic_index_in_dim(mod_all, me, axis=1, keepdims=False).reshape(1, 6 * D)
    mod_ctx = mod_all[:, 8, :].reshape(1, 6 * D)

    sp = {n: p[n][0] for n in SMALL if n not in ("c_ctx", "b_ada")}
    recv, halves = {}, {}

    def on_grad(gw):
        for gi, group in enumerate(RS_GROUPS):
            if gi not in halves and all(n in gw for n in group):
                kinds = [BIG_KIND[BIG.index(n)] for n in group]
                halves[gi] = (dict(gw), _pair_exchange_seq([gw[n] for n in group], kinds, "pair_exchange%d" % gi, RS_COLLECTIVE_ID0 + 2 * gi))

    def on_finish(key, after):
        gi = [i for i, group in enumerate(RS_GROUPS) if key in group][0]
        group = RS_GROUPS[gi]
        grads, half = halves[gi]
        prev = tuple(recv[n] for n in RS_GROUPS[gi - 1][:1]) if gi else ()
        psums = [_pair_add(grads[n], h, BIG_KIND[BIG.index(n)], "pair_add_" + n, after=(after,) + prev) for n, h in zip(group, half)]
        recv.update(zip(group, _chip_exchange_seq(psums, "chip_exchange%d" % gi, RS_COLLECTIVE_ID0 + 2 * gi + 1)))
        return psums[-1]

    total = {}

    def on_loss(loss_p):
        total["loss"] = lax.psum(loss_p[0, 0], ("x", "y", "c"))
        return total["loss"].reshape(1, 1)

    loss_p, grad_x, d_mod_lat, d_mod_ctx, gw, gs = _local_step(x2, ctx2, tgt2, mod_lat, mod_ctx, wb, sp, on_grad, on_loss, on_finish)

    g_early = _allgather_small_seq(_pack([gs[n] for n in SMALL_EARLY]), "gather_small_early", SMALL_COLLECTIVE_ID0)
    res = {}
    last = ()

    def adam_small(names, g_pack, tag, after):
        sm = _adam(g_pack, _pack([p[n] for n in names]), _pack([m[n] for n in names]), _pack([v[n] for n in names]),
                   tr=g_pack.shape[1], name="adam_small_" + tag, after=after)
        shapes = [p[n].shape for n in names]
        for j, outs in enumerate(zip(*[_unpack(a, shapes) for a in sm])):
            res[names[j]] = outs
        return (sm[0],)

    for gi, group in enumerate(RS_GROUPS):
        if gi == len(RS_GROUPS) - 1:
            last = adam_small(SMALL_EARLY, g_early, "early", last)
        for n in group:
            res[n] = _adam(recv[n], p[n][0], m[n][0], v[n][0], tr=256, name="adam_" + n, after=last)
            last = (res[n][0],)

    dm = jnp.concatenate([d_mod_lat, d_mod_ctx, jnp.zeros((6, 6 * D), F32)], 0)
    dm_all = _allgather_small_seq(dm, "gather_dmod", SMALL_COLLECTIVE_ID0 + 1)
    dm_all = lax.optimization_barrier((dm_all,) + last)[0]
    dm2 = jnp.concatenate([dm_all[:, 0, :], dm_all[:, 1, :]], 0)
    dm2_mine = lax.dynamic_slice_in_dim(dm2, me * ADA_COLS, ADA_COLS, axis=1)
    s2 = jnp.concatenate([s_act[0:8], jnp.broadcast_to(s_act[8:9], (8, D))], 0)
    g_w_ada = _matmul(s2, dm2_mine, mode="tn", name="dw_ada", tm=512, tn=ADA_COLS, after=last)
    dsc_part = _matmul(dm2_mine[8:16], p["w_ada"][0], mode="nt", name="d_silu_cctx", tm=8, tn=512, after=last)

    def cctx_b(rv, vv):
        _, pull = jax.vjp(_silu, vv[0])
        return [], [pull(jnp.sum(rv[0], axis=0, keepdims=True))[0]]

    g_cctx, = _rowwise(cctx_b, [(dsc_part, D, 0, 0)], [cc], [], [(1, D)], nrows=8, tr=8, name="cctx_bwd")
    gs["c_ctx"] = g_cctx
    gs["b_ada"] = d_mod_lat + d_mod_ctx

    res["w_ada"] = _adam(g_w_ada[None], p["w_ada"][0], m["w_ada"][0], v["w_ada"][0], tr=256, name="adam_w_ada")

    g_late = _allgather_small_seq(_pack([gs[n] for n in SMALL_LATE]), "gather_small_late", SMALL_COLLECTIVE_ID0 + 2)
    adam_small(SMALL_LATE, g_late, "late", last)

    outs = [total["loss"], grad_x[None]]
    for j in range(4):
        outs += [res[n][j].reshape(p[n].shape) for n in WEIGHTS]
    return tuple(outs)


def kernel(x, c, ctx, c_ctx, w_ada, b_ada, w_in, attn_sink, ssm_a_re, ssm_a_im, ssm_log_dt, ssm_b_re, ssm_b_im, ssm_c_re, ssm_c_im, ssm_d, w_glu, w_attn_up, w_ssm_up, w_out, ln_mix_g, ln_mix_b, w_mlp1, b_mlp1, w_mlp2, b_mlp2, ln_mlp_g, ln_mlp_b, loss_target, m_c_ctx, m_w_ada, m_b_ada, m_w_in, m_attn_sink, m_ssm_a_re, m_ssm_a_im, m_ssm_log_dt, m_ssm_b_re, m_ssm_b_im, m_ssm_c_re, m_ssm_c_im, m_ssm_d, m_w_glu, m_w_attn_up, m_w_ssm_up, m_w_out, m_ln_mix_g, m_ln_mix_b, m_w_mlp1, m_b_mlp1, m_w_mlp2, m_b_mlp2, m_ln_mlp_g, m_ln_mlp_b, v_c_ctx, v_w_ada, v_b_ada, v_w_in, v_attn_sink, v_ssm_a_re, v_ssm_a_im, v_ssm_log_dt, v_ssm_b_re, v_ssm_b_im, v_ssm_c_re, v_ssm_c_im, v_ssm_d, v_w_glu, v_w_attn_up, v_w_ssm_up, v_w_out, v_ln_mix_g, v_ln_mix_b, v_w_mlp1, v_b_mlp1, v_w_mlp2, v_b_mlp2, v_ln_mlp_g, v_ln_mlp_b):
    given = dict(locals())
    p = {n: given[n] for n in WEIGHTS}
    m = {n: given["m_" + n] for n in WEIGHTS}
    v = {n: given["v_" + n] for n in WEIGHTS}
    return _step(x, c, ctx, loss_target, p, m, v)
```

```python
import functools
import math

import jax
import jax.numpy as jnp
from jax import lax
from jax.experimental import pallas as pl
from jax.experimental.pallas import tpu as pltpu
from jax.experimental.pallas import tpu_sc as plsc

F32 = jnp.float32
BF16 = jnp.bfloat16

N_DEV = 8
D = 2048
T = 2048
C = 256
TA = T + C
GRID_W = 64
HD = 128
NH = 8
NKV = 2
GROUP = NH // NKV
WINDOW = 128
QW = NH * HD
KVW = NKV * HD
SW = D // 4
SG = 16
NG = SW // SG
SP = 64
DFF = 4 * D
IN_COLS = QW + 2 * KVW + SW + 2 * D
ALPHA = 2.0 ** 0.25
LN_EPS = 1e-6
NEG_INF = -1e30
ROPE_BASE = 10000.0
ATT_SCALE = HD ** -0.5

NSEG = 8
GBLK = 8
NBLK = NG // GBLK
BW = GBLK * SP
UW = GBLK * SG

ADAM_LR = 0.001
ADAM_B1 = 0.9
ADAM_B2 = 0.999
ADAM_EPS = 1e-08
ADAM_WD = 0.01
ADAM_STEP = 10

VMEM_LIMIT_BYTES = 56 * 1024 * 1024
MESH = pl.DeviceIdType.MESH


def _cparams(sem=None):
    return pltpu.CompilerParams(dimension_semantics=sem, vmem_limit_bytes=VMEM_LIMIT_BYTES)


def _matmul(a, b, *, mode, name, out_dtypes=(F32,), tm=512, tn=512, tk=None, bias=None, extras=(), epilogue=None, after=(),
            out_t=None):
    if mode == "nn":
        (M, K), (K2, N) = a.shape, b.shape
    elif mode == "nt":
        (M, K), (N, K2) = a.shape, b.shape
    else:
        (K, M), (K2, N) = a.shape, b.shape
    assert K == K2, (name, a.shape, b.shape)
    tm, tn, tk = min(tm, M), min(tn, N), min(tk or K, K)
    assert M % tm == 0 and N % tn == 0 and K % tk == 0, (name, M, N, K, tm, tn, tk)
    nk = K // tk
    if mode == "tn":
        a_spec = pl.BlockSpec((tk, tm), lambda i, j, k: (k, i))
    else:
        a_spec = pl.BlockSpec((tm, tk), lambda i, j, k: (i, k))
    if mode == "nt":
        b_spec = pl.BlockSpec((tn, tk), lambda i, j, k: (j, k))
    else:
        b_spec = pl.BlockSpec((tk, tn), lambda i, j, k: (k, j))
    dims = {"nn": (((1,), (0,)), ((), ())), "nt": (((1,), (1,)), ((), ())), "tn": (((0,), (0,)), ((), ()))}[mode]
    in_specs = [a_spec, b_spec]
    operands = [a, b]
    if bias is not None:
        in_specs.append(pl.BlockSpec((1, tn), lambda i, j, k: (0, j)))
        operands.append(bias)
    for e in extras:
        in_specs.append(pl.BlockSpec((tm, tn), lambda i, j, k: (i, j)))
        operands.append(e)
    n_ex = len(extras)
    for t in after:
        in_specs.append(pl.BlockSpec(memory_space=pl.ANY))
        operands.append(t)
    n_after = len(after)
    n_out = len(out_dtypes)
    out_t = tuple(out_t) if out_t is not None else (False,) * n_out
    has_bias = bias is not None

    def kern(*refs):
        a_ref, b_ref = refs[0], refs[1]
        pos = 2
        bias_ref = None
        if has_bias:
            bias_ref = refs[pos]
            pos += 1
        ex_refs = refs[pos:pos + n_ex]
        pos += n_ex + n_after
        out_refs = refs[pos:pos + n_out]
        acc_ref = refs[pos + n_out] if nk > 1 else None

        def finish(r):
            if has_bias:
                r = r + bias_ref[...]
            outs = epilogue(r, *[e[...] for e in ex_refs]) if epilogue is not None else (r,)
            for o_ref, o, tr_ in zip(out_refs, outs, out_t):
                o_ref[...] = (o.T if tr_ else o).astype(o_ref.dtype)

        part = lax.dot_general(a_ref[...].astype(BF16), b_ref[...].astype(BF16), dims, preferred_element_type=F32)
        if nk == 1:
            finish(part)
        else:
            k = pl.program_id(2)

            @pl.when(k == 0)
            def _():
                acc_ref[...] = part

            @pl.when(k > 0)
            def _():
                acc_ref[...] += part

            @pl.when(k == nk - 1)
            def _():
                finish(acc_ref[...])

    outs = pl.pallas_call(
        kern,
        name=name,
        grid=(M // tm, N // tn, nk),
        in_specs=in_specs,
        out_specs=[pl.BlockSpec((tn, tm), lambda i, j, k: (j, i)) if tr_ else pl.BlockSpec((tm, tn), lambda i, j, k: (i, j))
                   for tr_ in out_t],
        out_shape=[jax.ShapeDtypeStruct((N, M) if tr_ else (M, N), dt) for dt, tr_ in zip(out_dtypes, out_t)],
        scratch_shapes=[pltpu.VMEM((tm, tn), F32)] if nk > 1 else [],
        compiler_params=_cparams(("parallel", "parallel", "arbitrary")),
    )(*operands)
    return outs[0] if n_out == 1 else tuple(outs)


def _rowwise(fn, rows, vecs, outs, vec_outs, *, nrows, tr, name, after=()):
    n_rows, n_vecs, n_outs, n_after = len(rows), len(vecs), len(outs), len(after)
    in_specs = [pl.BlockSpec((tr, w), lambda i, cb=cb, ro=ro: (i + ro, cb)) for (_, w, cb, ro) in rows]
    in_specs += [pl.BlockSpec(v.shape, lambda i: (0, 0)) for v in vecs]
    in_specs += [pl.BlockSpec(memory_space=pl.ANY)] * n_after
    outs = [o if len(o) == 3 else (*o, False) for o in outs]
    out_specs = [pl.BlockSpec((w, tr), lambda i: (0, i)) if tr_ else pl.BlockSpec((tr, w), lambda i: (i, 0)) for (w, _, tr_) in outs]
    out_specs += [pl.BlockSpec(s, lambda i: (0, 0)) for s in vec_outs]
    out_shape = [jax.ShapeDtypeStruct((w, nrows) if tr_ else (nrows, w), dt) for (w, dt, tr_) in outs]
    out_tr = [tr_ for (_, _, tr_) in outs]
    out_shape += [jax.ShapeDtypeStruct(s, F32) for s in vec_outs]

    def kern(*refs):
        rvals = [r[...] for r in refs[:n_rows]]
        vvals = [r[...] for r in refs[n_rows:n_rows + n_vecs]]
        first_out = n_rows + n_vecs + n_after
        o_refs = refs[first_out:first_out + n_outs]
        v_refs = refs[first_out + n_outs:]
        ro, vo = fn(rvals, vvals)
        for r, val, tr_ in zip(o_refs, ro, out_tr):
            r[...] = (val.astype(F32).T if tr_ else val).astype(r.dtype)
        i = pl.program_id(0)
        for r, val in zip(v_refs, vo):
            @pl.when(i == 0)
            def _(r=r, val=val):
                r[...] = val.astype(F32)

            @pl.when(i > 0)
            def _(r=r, val=val):
                r[...] += val.astype(F32)

    res = pl.pallas_call(
        kern,
        name=name,
        grid=(nrows // tr,),
        in_specs=in_specs,
        out_specs=out_specs,
        out_shape=out_shape,
        compiler_params=_cparams(("arbitrary",)),
    )(*[r[0] for r in rows], *vecs, *after)
    return list(res)


def _ln(x):
    mu = jnp.mean(x, axis=-1, keepdims=True)
    xc = x - mu
    var = jnp.mean(xc * xc, axis=-1, keepdims=True)
    return xc * lax.rsqrt(var + LN_EPS)


def _sigmoid(x):
    return 1.0 / (1.0 + jnp.exp(-x))


def _gelu(x):
    return 0.5 * x * (1.0 + jnp.tanh(math.sqrt(2.0 / math.pi) * (x + 0.044715 * (x * x * x))))


def _silu(x):
    return x * _sigmoid(x)


def _f_ln_mod(x, sc, sh):
    return _ln(x) * (1.0 + sc) + sh


def _f_glu(z):
    return z[:, :SW] * _sigmoid(z[:, SW:])


def _f_mix(ga, gs, attn_d, ssm_d):
    return _sigmoid(ga) * attn_d + _sigmoid(gs) * ssm_d


def _f_post1(x, y, g1, lg, lb, sc2, sh2):
    r1 = ALPHA * x + g1 * y
    x1 = _ln(r1) * lg + lb
    h2 = _ln(x1) * (1.0 + sc2) + sh2
    return x1, h2


def _f_loss(x1, mlp, tgt, g2, lg, lb, b2z):
    r2 = ALPHA * x1 + g2 * (mlp + b2z)
    out = _ln(r2) * lg + lb
    err = out - tgt
    return 0.5 * jnp.sum(err * err) * (1.0 / D)


def _rope_tables():
    rows = T // GRID_W
    row = jnp.repeat(jnp.arange(rows), GRID_W)
    col = jnp.tile(jnp.arange(GRID_W), rows)
    n_freq = HD // 4
    freqs = ROPE_BASE ** (-jnp.arange(n_freq, dtype=F32) / n_freq)
    ang_r = row.astype(F32)[:, None] * freqs
    ang_c = col.astype(F32)[:, None] * freqs
    ang = jnp.concatenate([ang_r, ang_r, ang_c, ang_c], -1)
    cos, sin = jnp.cos(ang), jnp.sin(ang)
    lo = (jnp.arange(HD) % (HD // 2)) < (HD // 4)
    sin_a = jnp.where(lo[None, :], -sin, 0.0)
    sin_b = jnp.where(lo[None, :], 0.0, sin)
    return cos, sin_a, sin_b


def _rope(x, cos, sa, sb):
    return x * cos + pltpu.roll(x, 96, 1) * sa + pltpu.roll(x, 32, 1) * sb


def _rope_t(dy, cos, sa, sb):
    return dy * cos + pltpu.roll(dy * sa, 32, 1) + pltpu.roll(dy * sb, 96, 1)


BAND = 3 * WINDOW
KPAD = T + 2 * WINDOW


def _attn_fill_kv(k_ref, v_ref, cos_ref, sa_ref, sb_ref, kp, vp, kc, vc):
    zeros = jnp.zeros((WINDOW, KVW), BF16)
    kp[0:WINDOW, :] = zeros
    kp[WINDOW + T:KPAD, :] = zeros
    vp[0:WINDOW, :] = zeros
    vp[WINDOW + T:KPAD, :] = zeros
    for hh in range(NKV):
        cs = slice(hh * HD, (hh + 1) * HD)
        for r0 in range(0, T, 512):
            rs = slice(r0, r0 + 512)
            kr = _rope(k_ref[rs, cs], cos_ref[rs, :], sa_ref[rs, :], sb_ref[rs, :])
            kp[WINDOW + r0:WINDOW + r0 + 512, cs] = kr.astype(BF16)
    vp[WINDOW:WINDOW + T, :] = v_ref[0:T, :].astype(BF16)
    kc[...] = k_ref[T:TA, :].astype(BF16)
    vc[...] = v_ref[T:TA, :].astype(BF16)


def _attn_scores(n, h, q_ref, cos_ref, sa_ref, sb_ref, sink_ref, kp, kc):
    kvh = h // GROUP
    r0 = pl.multiple_of(n * WINDOW, WINDOW)
    cos = cos_ref[pl.ds(r0, WINDOW), :]
    sa = sa_ref[pl.ds(r0, WINDOW), :]
    sb = sb_ref[pl.ds(r0, WINDOW), :]
    q_h = _rope(q_ref[:, h * HD:(h + 1) * HD], cos, sa, sb).astype(BF16)
    kb = kp[pl.ds(r0, BAND), kvh * HD:(kvh + 1) * HD]
    kcb = kc[:, kvh * HD:(kvh + 1) * HD]
    nt = (((1,), (1,)), ((), ()))
    s_loc = lax.dot_general(q_h, kb, nt, preferred_element_type=F32) * ATT_SCALE
    s_ctx = lax.dot_general(q_h, kcb, nt, preferred_element_type=F32) * ATT_SCALE
    row = lax.broadcasted_iota(jnp.int32, (WINDOW, BAND), 0)
    col = lax.broadcasted_iota(jnp.int32, (WINDOW, BAND), 1)
    rel = col - WINDOW - row
    kpos = r0 - WINDOW + col
    valid = (jnp.abs(rel) <= WINDOW) & (kpos >= 0) & (kpos < T)
    s_loc = jnp.where(valid, s_loc, NEG_INF)
    sk = sink_ref[0:1, h:h + 1]
    m = jnp.maximum(jnp.maximum(jnp.max(s_loc, -1, keepdims=True), jnp.max(s_ctx, -1, keepdims=True)), sk)
    e_loc = jnp.exp(s_loc - m)
    e_ctx = jnp.exp(s_ctx - m)
    e_sink = jnp.exp(sk - m)
    inv = 1.0 / (jnp.sum(e_loc, -1, keepdims=True) + jnp.sum(e_ctx, -1, keepdims=True) + e_sink)
    return q_h, r0, e_loc * inv, e_ctx * inv, e_sink * inv


def _attn_fwd(proj, sink, tabs):
    cos, sa, sb = tabs

    def kern(q_ref, k_ref, v_ref, cos_ref, sa_ref, sb_ref, sink_ref, o_ref, kp, vp, kc, vc):
        n = pl.program_id(0)

        @pl.when(n == 0)
        def _():
            _attn_fill_kv(k_ref, v_ref, cos_ref, sa_ref, sb_ref, kp, vp, kc, vc)

        for h in range(NH):
            kvh = h // GROUP
            _, r0, p_loc, p_ctx, _ = _attn_scores(n, h, q_ref, cos_ref, sa_ref, sb_ref, sink_ref, kp, kc)
            vb = vp[pl.ds(r0, BAND), kvh * HD:(kvh + 1) * HD]
            vcb = vc[:, kvh * HD:(kvh + 1) * HD]
            o = jnp.dot(p_loc.astype(BF16), vb, preferred_element_type=F32)
            o = o + jnp.dot(p_ctx.astype(BF16), vcb, preferred_element_type=F32)
            o_ref[:, h * HD:(h + 1) * HD] = o.astype(o_ref.dtype)

    full = lambda shape: pl.BlockSpec(shape, lambda n: (0, 0))
    return pl.pallas_call(
        kern,
        name="attn_fwd",
        grid=(T // WINDOW,),
        in_specs=[
            pl.BlockSpec((WINDOW, QW), lambda n: (n, 0)),
            pl.BlockSpec((TA, KVW), lambda n: (0, QW // KVW)),
            pl.BlockSpec((TA, KVW), lambda n: (0, QW // KVW + 1)),
            full((T, HD)), full((T, HD)), full((T, HD)), full((1, NH)),
        ],
        out_specs=pl.BlockSpec((WINDOW, QW), lambda n: (n, 0)),
        out_shape=jax.ShapeDtypeStruct((T, QW), BF16),
        scratch_shapes=[pltpu.VMEM((KPAD, KVW), BF16), pltpu.VMEM((KPAD, KVW), BF16),
                        pltpu.VMEM((C, KVW), BF16), pltpu.VMEM((C, KVW), BF16)],
        compiler_params=_cparams(("arbitrary",)),
    )(proj, proj, proj, cos, sa, sb, sink)


def _attn_bwd(proj, d_attn, sink, tabs):
    cos, sa, sb = tabs
    n_blocks = T // WINDOW

    def kern(q_ref, k_ref, v_ref, do_ref, cos_ref, sa_ref, sb_ref, sink_ref,
             dq_ref, dk_ref, dv_ref, dsink_ref, kp, vp, kc, vc, dkp, dvp, dkc, dvc):
        n = pl.program_id(0)

        @pl.when(n == 0)
        def _():
            _attn_fill_kv(k_ref, v_ref, cos_ref, sa_ref, sb_ref, kp, vp, kc, vc)
            dkp[...] = jnp.zeros_like(dkp)
            dvp[...] = jnp.zeros_like(dvp)
            dkc[...] = jnp.zeros_like(dkc)
            dvc[...] = jnp.zeros_like(dvc)
            dsink_ref[...] = jnp.zeros_like(dsink_ref)

        nt = (((1,), (1,)), ((), ()))
        tn = (((0,), (0,)), ((), ()))
        for h in range(NH):
            kvh = h // GROUP
            cs = slice(kvh * HD, (kvh + 1) * HD)
            q_h, r0, p_loc, p_ctx, p_sink = _attn_scores(n, h, q_ref, cos_ref, sa_ref, sb_ref, sink_ref, kp, kc)
            kb = kp[pl.ds(r0, BAND), cs]
            vb = vp[pl.ds(r0, BAND), cs]
            kcb = kc[:, cs]
            vcb = vc[:, cs]
            do_h = do_ref[:, h * HD:(h + 1) * HD]
            dp_loc = lax.dot_general(do_h, vb, nt, preferred_element_type=F32)
            dp_ctx = lax.dot_general(do_h, vcb, nt, preferred_element_type=F32)
            delta = jnp.sum(p_loc * dp_loc, -1, keepdims=True) + jnp.sum(p_ctx * dp_ctx, -1, keepdims=True)
            ds_loc = (p_loc * (dp_loc - delta) * ATT_SCALE).astype(BF16)
            ds_ctx = (p_ctx * (dp_ctx - delta) * ATT_SCALE).astype(BF16)
            dq = jnp.dot(ds_loc, kb, preferred_element_type=F32) + jnp.dot(ds_ctx, kcb, preferred_element_type=F32)
            cos = cos_ref[pl.ds(r0, WINDOW), :]
            sa_ = sa_ref[pl.ds(r0, WINDOW), :]
            sb_ = sb_ref[pl.ds(r0, WINDOW), :]
            dq_ref[:, h * HD:(h + 1) * HD] = _rope_t(dq, cos, sa_, sb_).astype(dq_ref.dtype)
            dkp[pl.ds(r0, BAND), cs] += lax.dot_general(ds_loc, q_h, tn, preferred_element_type=F32)
            dkc[:, cs] += lax.dot_general(ds_ctx, q_h, tn, preferred_element_type=F32)
            dvp[pl.ds(r0, BAND), cs] += lax.dot_general(p_loc.astype(BF16), do_h, tn, preferred_element_type=F32)
            dvc[:, cs] += lax.dot_general(p_ctx.astype(BF16), do_h, tn, preferred_element_type=F32)
            dsk = -jnp.sum(p_sink * delta, axis=0, keepdims=True)
            dsink_ref[h:h + 1, :] += jnp.broadcast_to(dsk, (1, HD))

        @pl.when(n == n_blocks - 1)
        def _():
            for hh in range(NKV):
                cs = slice(hh * HD, (hh + 1) * HD)
                for r0 in range(0, T, 512):
                    rs = slice(r0, r0 + 512)
                    g = dkp[WINDOW + r0:WINDOW + r0 + 512, cs]
                    dk_ref[rs, cs] = _rope_t(g, cos_ref[rs, :], sa_ref[rs, :], sb_ref[rs, :]).astype(dk_ref.dtype)
            dk_ref[T:TA, :] = dkc[...].astype(dk_ref.dtype)
            dv_ref[0:T, :] = dvp[WINDOW:WINDOW + T, :].astype(dv_ref.dtype)
            dv_ref[T:TA, :] = dvc[...].astype(dv_ref.dtype)

    full = lambda shape: pl.BlockSpec(shape, lambda n: (0, 0))
    return pl.pallas_call(
        kern,
        name="attn_bwd",
        grid=(n_blocks,),
        in_specs=[
            pl.BlockSpec((WINDOW, QW), lambda n: (n, 0)),
            pl.BlockSpec((TA, KVW), lambda n: (0, QW // KVW)),
            pl.BlockSpec((TA, KVW), lambda n: (0, QW // KVW + 1)),
            pl.BlockSpec((WINDOW, QW), lambda n: (n, 0)),
            full((T, HD)), full((T, HD)), full((T, HD)), full((1, NH)),
        ],
        out_specs=[pl.BlockSpec((WINDOW, QW), lambda n: (n, 0)), full((TA, KVW)), full((TA, KVW)), full((NH, HD))],
        out_shape=[jax.ShapeDtypeStruct((T, QW), BF16), jax.ShapeDtypeStruct((TA, KVW), BF16),
                   jax.ShapeDtypeStruct((TA, KVW), BF16), jax.ShapeDtypeStruct((NH, HD), F32)],
        scratch_shapes=[pltpu.VMEM((KPAD, KVW), BF16), pltpu.VMEM((KPAD, KVW), BF16),
                        pltpu.VMEM((C, KVW), BF16), pltpu.VMEM((C, KVW), BF16),
                        pltpu.VMEM((KPAD, KVW), F32), pltpu.VMEM((KPAD, KVW), F32),
                        pltpu.VMEM((C, KVW), F32), pltpu.VMEM((C, KVW), F32)],
        compiler_params=_cparams(("arbitrary",)),
    )(proj, proj, proj, d_attn, cos, sa, sb, sink)


def _s5_prep(a_re, a_im, log_dt, b_re, b_im, c_re, c_im):
    lam = lax.complex(a_re, a_im)
    dt = jnp.exp(log_dt)[..., None]
    lam_bar = jnp.exp(lam * dt)
    b_bar = ((lam_bar - 1.0) / lam)[..., None] * lax.complex(b_re, b_im)
    eye = jnp.eye(GBLK, dtype=F32)

    def lam_rows(v):
        return v.reshape(2, NBLK, 1, BW)

    lam_l = jnp.concatenate([lam_rows(jnp.real(lam_bar)), lam_rows(jnp.imag(lam_bar))], -1)
    lam_l = jnp.broadcast_to(lam_l, (2, NBLK, 8, 2 * BW))

    def b_blocks(v):
        v = v.reshape(2, NBLK, GBLK, SP, SG).transpose(0, 1, 2, 4, 3)
        return (v[:, :, :, :, None, :] * eye[None, None, :, None, :, None]).reshape(2, NBLK, UW, BW)

    bmat = jnp.concatenate([b_blocks(jnp.real(b_bar)), b_blocks(jnp.imag(b_bar))], -1)

    def c_blocks(v):
        v = v.reshape(2, NBLK, GBLK, SG, SP).transpose(0, 1, 2, 4, 3)
        return (v[:, :, :, :, None, :] * eye[None, None, :, None, :, None]).reshape(2, NBLK, BW, UW)

    cmat = jnp.concatenate([c_blocks(c_re), -c_blocks(c_im)], 2)
    return lam_l, bmat, cmat


def _cmul(ar, ai, br, bi):
    return ar * br - ai * bi, ar * bi + ai * br


def _shift_rows(x, rev, fill):
    r = lax.broadcasted_iota(jnp.int32, x.shape, 0)
    down = jnp.where(r == 0, fill, pltpu.roll(x, 1, 0))
    up = jnp.where(r == NSEG - 1, fill, pltpu.roll(x, NSEG - 1, 0))
    return jnp.where(rev == 0, down, up)


def _edge_row(x, rev):
    last = jnp.broadcast_to(x[NSEG - 1:NSEG, :], x.shape)
    first = jnp.broadcast_to(x[0:1, :], x.shape)
    return jnp.where(rev == 0, last, first)


def _seg_scan(get, put, base, seglen, lr, li, rev, cin):
    zero = jnp.zeros((NSEG, BW), F32)

    def rows(k):
        j = jnp.where(rev == 0, k, seglen - 1 - k)
        return pl.ds(pl.multiple_of(base + j * NSEG, NSEG), NSEG)

    def local(k, carry):
        sr, si, pr, pi = carry
        xr, xi = get(rows(k))
        tr, ti = _cmul(lr, li, sr, si)
        sr, si = tr + xr, ti + xi
        put(rows(k), sr, si)
        pr, pi = _cmul(lr, li, pr, pi)
        return sr, si, pr, pi

    er, ei, lpr, lpi = lax.fori_loop(0, seglen, local, (zero, zero, zero + 1.0, zero))
    cr, ci = _shift_rows(zero, rev, cin[0]), _shift_rows(zero, rev, cin[1])
    for _ in range(NSEG - 1):
        tr, ti = _cmul(lpr, lpi, cr, ci)
        cr, ci = _shift_rows(er + tr, rev, cin[0]), _shift_rows(ei + ti, rev, cin[1])

    def fix(k, carry):
        pr, pi = carry
        xr, xi = get(rows(k))
        tr, ti = _cmul(pr, pi, cr, ci)
        put(rows(k), xr + tr, xi + ti)
        return _cmul(lr, li, pr, pi)

    lax.fori_loop(0, seglen, fix, (lr, li))
    tr, ti = _cmul(lpr, lpi, cr, ci)
    return _edge_row(er + tr, rev), _edge_row(ei + ti, rev)


RCH = 256
CSEG = C // NSEG
TSEG = T // NSEG
UCOL0 = (QW + 2 * KVW) // UW


REGIONS = ((0, TSEG), (T, CSEG))


def _state_access(ref, lead=()):
    def get(rows):
        return ref[(*lead, rows, slice(0, BW))], ref[(*lead, rows, slice(BW, 2 * BW))]

    def put(rows, re, im):
        ref[(*lead, rows, slice(0, BW))] = re
        ref[(*lead, rows, slice(BW, 2 * BW))] = im

    return get, put


def _interleave_rows(src_ref, dst_ref, regions=REGIONS):
    for base, seglen in regions:
        def body(j, carry, base=base, seglen=seglen):
            dst_ref[pl.ds(pl.multiple_of(base + j * NSEG, NSEG), NSEG), :] = src_ref[pl.ds(base + j, NSEG, stride=seglen), :]
            return carry

        lax.fori_loop(0, seglen, body, 0, unroll=8)


def _deinterleave_rows(src_ref, dst_ref, regions=REGIONS):
    for base, seglen in regions:
        def body(j, carry, base=base, seglen=seglen):
            dst_ref[pl.ds(base + j, NSEG, stride=seglen), :] = src_ref[pl.ds(pl.multiple_of(base + j * NSEG, NSEG), NSEG), :]
            return carry

        lax.fori_loop(0, seglen, body, 0, unroll=8)


def _s5_fwd(proj, dskip, lam, bmat, cmat):
    def kern(u_ref, dk_ref, lam_ref, b_ref, c_ref, s_ref, ssm_ref, ge_ref, up_ref, yp_ref):
        d = pl.program_id(1)

        @pl.when(d == 0)
        def _():
            _interleave_rows(u_ref, up_ref)

        bm = b_ref[0, 0].astype(BF16)
        for r0 in range(0, TA, RCH):
            s_ref[0, 0, r0:r0 + RCH, :] = jnp.dot(up_ref[r0:r0 + RCH, :].astype(BF16), bm, preferred_element_type=F32)
        lr = lam_ref[0, 0, :, 0:BW]
        li = lam_ref[0, 0, :, BW:2 * BW]
        zero = jnp.zeros((NSEG, BW), F32)
        get, put = _state_access(s_ref, (0, 0))
        mid = _seg_scan(get, put, T, CSEG, lr, li, d, (zero, zero))
        _seg_scan(get, put, 0, TSEG, lr, li, d, mid)
        cm = c_ref[0, 0].astype(BF16)
        for r0 in range(0, T, RCH):
            y = jnp.dot(s_ref[0, 0, r0:r0 + RCH, :].astype(BF16), cm, preferred_element_type=F32)

            @pl.when(d == 0)
            def _(y=y, r0=r0):
                yp_ref[r0:r0 + RCH, :] = y + dk_ref[...] * up_ref[r0:r0 + RCH, :]

            @pl.when(d == 1)
            def _(y=y, r0=r0):
                yp_ref[r0:r0 + RCH, :] += y

        @pl.when(d == 1)
        def _():
            _deinterleave_rows(yp_ref, ssm_ref, REGIONS[:1])
            for r0 in range(0, T, RCH):
                ge_ref[r0:r0 + RCH, :] = _gelu(ssm_ref[r0:r0 + RCH, :]).astype(ge_ref.dtype)

    blk4 = lambda shape: pl.BlockSpec((1, 1) + shape, lambda b, d: (d, b, 0, 0))
    return pl.pallas_call(
        kern,
        name="s5_fwd",
        grid=(NBLK, 2),
        in_specs=[pl.BlockSpec((TA, UW), lambda b, d: (0, UCOL0 + b)), pl.BlockSpec((1, UW), lambda b, d: (0, b)),
                  blk4((8, 2 * BW)), blk4((UW, 2 * BW)), blk4((2 * BW, UW))],
        out_specs=[blk4((TA, 2 * BW)), pl.BlockSpec((T, UW), lambda b, d: (0, b)), pl.BlockSpec((T, UW), lambda b, d: (0, b))],
        out_shape=[jax.ShapeDtypeStruct((2, NBLK, TA, 2 * BW), F32), jax.ShapeDtypeStruct((T, SW), F32),
                   jax.ShapeDtypeStruct((T, SW), BF16)],
        scratch_shapes=[pltpu.VMEM((TA, UW), F32), pltpu.VMEM((T, UW), F32)],
        compiler_params=_cparams(("parallel", "arbitrary")),
    )(proj, dskip, lam, bmat, cmat)


def _s5_bwd(d_ge, ssm, proj, dskip, states, lam, bmat, cmat):
    nt = (((1,), (1,)), ((), ()))
    tn = (((0,), (0,)), ((), ()))

    def kern(dge_ref, ssm_ref, u_ref, dk_ref, s_ref, lam_ref, b_ref, c_ref,
             du_ref, ddk_ref, dlam_ref, db_ref, dc_ref, g_ref, dua_ref, dssm_ref, up_ref, nat_ref):
        d = pl.program_id(1)

        @pl.when(d == 0)
        def _():
            ddk = jnp.zeros((1, UW), F32)
            for r0 in range(0, T, RCH):
                rs = slice(r0, r0 + RCH)
                _, pull = jax.vjp(_gelu, ssm_ref[rs, :])
                dssm = pull(dge_ref[rs, :])[0]
                nat_ref[rs, :] = dssm
                ddk = ddk + jnp.sum(dssm * u_ref[rs, :], axis=0, keepdims=True)
            ddk_ref[...] = ddk
            _interleave_rows(nat_ref, dssm_ref, REGIONS[:1])
            _interleave_rows(u_ref, up_ref)
            for r0 in range(0, T, RCH):
                dua_ref[r0:r0 + RCH, :] = dssm_ref[r0:r0 + RCH, :] * dk_ref[...]
            dua_ref[T:TA, :] = jnp.zeros((C, UW), F32)

        cm = c_ref[0, 0].astype(BF16)
        for r0 in range(0, T, RCH):
            g_ref[r0:r0 + RCH, :] = lax.dot_general(dssm_ref[r0:r0 + RCH, :].astype(BF16), cm, nt, preferred_element_type=F32)
        g_ref[T:TA, :] = jnp.zeros((C, 2 * BW), F32)
        lr = lam_ref[0, 0, :, 0:BW]
        li = lam_ref[0, 0, :, BW:2 * BW]
        zero = jnp.zeros((NSEG, BW), F32)
        get_g, put_g = _state_access(g_ref)

        mid = _seg_scan(get_g, put_g, 0, TSEG, lr, -li, 1 - d, (zero, zero))
        _seg_scan(get_g, put_g, T, CSEG, lr, -li, 1 - d, mid)

        get_s, _ = _state_access(s_ref, (0, 0))

        def dlam_terms(g, s):
            return g[0] * s[0] + g[1] * s[1], g[1] * s[0] - g[0] * s[1]

        def dlam_region(base, seglen, s_in, acc):
            def rows(j):
                return pl.ds(pl.multiple_of(base + j * NSEG, NSEG), NSEG)

            def inner(k, acc):
                j = jnp.where(d == 0, k, seglen - 1 - k)
                jp = jnp.where(d == 0, k - 1, seglen - k)
                t = dlam_terms(get_g(rows(j)), get_s(rows(jp)))
                return acc[0] + t[0], acc[1] + t[1]

            acc = lax.fori_loop(1, seglen, inner, acc)
            jb = jnp.where(d == 0, 0, seglen - 1)
            jn = jnp.where(d == 0, seglen - 1, 0)
            sp = get_s(rows(jn))
            t = dlam_terms(get_g(rows(jb)), (_shift_rows(sp[0], d, s_in[0]), _shift_rows(sp[1], d, s_in[1])))
            return acc[0] + t[0], acc[1] + t[1]

        r_mid = jnp.where(d == 0, TA - 1, T)
        s_mid = tuple(jnp.broadcast_to(t, (NSEG, BW)) for t in get_s(pl.ds(r_mid, 1)))
        acc = dlam_region(T, CSEG, (zero, zero), (zero, zero))
        acc = dlam_region(0, TSEG, s_mid, acc)
        dlam_ref[0, 0, :, 0:BW] = acc[0]
        dlam_ref[0, 0, :, BW:2 * BW] = acc[1]

        bm = b_ref[0, 0].astype(BF16)
        db = jnp.zeros((UW, 2 * BW), F32)
        dc = jnp.zeros((2 * BW, UW), F32)
        for r0 in range(0, TA, RCH):
            rs = slice(r0, r0 + RCH)
            g = g_ref[rs, :].astype(BF16)
            dua_ref[rs, :] += lax.dot_general(g, bm, nt, preferred_element_type=F32)
            db = db + lax.dot_general(up_ref[rs, :].astype(BF16), g, tn, preferred_element_type=F32)
            if r0 < T:
                dc = dc + lax.dot_general(s_ref[0, 0, rs, :].astype(BF16), dssm_ref[rs, :].astype(BF16), tn,
                                          preferred_element_type=F32)
        db_ref[0, 0] = db
        dc_ref[0, 0] = dc

        @pl.when(d == 1)
        def _():
            _deinterleave_rows(dua_ref, nat_ref)
            du_ref[...] = nat_ref[...].astype(du_ref.dtype)

    blk4 = lambda shape: pl.BlockSpec((1, 1) + shape, lambda b, d: (d, b, 0, 0))
    lat = pl.BlockSpec((T, UW), lambda b, d: (0, b))
    vec = pl.BlockSpec((1, UW), lambda b, d: (0, b))
    return pl.pallas_call(
        kern,
        name="s5_bwd",
        grid=(NBLK, 2),
        in_specs=[lat, lat, pl.BlockSpec((TA, UW), lambda b, d: (0, UCOL0 + b)), vec,
                  blk4((TA, 2 * BW)), blk4((8, 2 * BW)), blk4((UW, 2 * BW)), blk4((2 * BW, UW))],
        out_specs=[pl.BlockSpec((TA, UW), lambda b, d: (0, b)), vec, blk4((8, 2 * BW)), blk4((UW, 2 * BW)), blk4((2 * BW, UW))],
        out_shape=[jax.ShapeDtypeStruct((TA, SW), BF16), jax.ShapeDtypeStruct((1, SW), F32),
                   jax.ShapeDtypeStruct((2, NBLK, 8, 2 * BW), F32),
                   jax.ShapeDtypeStruct((2, NBLK, UW, 2 * BW), F32), jax.ShapeDtypeStruct((2, NBLK, 2 * BW, UW), F32)],
        scratch_shapes=[pltpu.VMEM((TA, 2 * BW), F32), pltpu.VMEM((TA, UW), F32), pltpu.VMEM((T, UW), F32),
                        pltpu.VMEM((TA, UW), F32), pltpu.VMEM((TA, UW), F32)],
        compiler_params=_cparams(("parallel", "arbitrary")),
    )(d_ge, ssm, proj, dskip, states, lam, bmat, cmat)


TR = 256


def _vjp_rows(f, primals, cots, n_row):
    _, pull = jax.vjp(f, *primals)
    g = pull(cots)
    return list(g[:n_row]), list(g[n_row:])


class _GradDict(dict):
    def __init__(self, on_set=None):
        super().__init__()
        self._on_set = on_set
        self.tokens = {}

    def __setitem__(self, key, value):
        super().__setitem__(key, value)
        if self._on_set is not None:
            self._on_set(self)

    def order(self, key):
        return self.tokens.get(key, self.get(key))

    def finish(self, key, after):
        if self.on_finish is None:
            return ()
        return (self.on_finish(key, after),)

    on_finish = None


def _local_step(x, ctx, tgt, mod_lat, mod_ctx, wb, sp, on_grad=None, on_loss=None, on_finish=None, on_early=None):
    sh1, sc1, g1, sh2, sc2, g2 = [mod_lat[:, i * D:(i + 1) * D] for i in range(6)]
    csh1, csc1 = mod_ctx[:, 0:D], mod_ctx[:, D:2 * D]
    tabs = _rope_tables()
    sink = sp["attn_sink"].reshape(1, NH)
    dskip = sp["ssm_d"].reshape(1, SW)
    lg_mix, lb_mix = sp["ln_mix_g"].reshape(1, D), sp["ln_mix_b"].reshape(1, D)
    lg_mlp, lb_mlp = sp["ln_mlp_g"].reshape(1, D), sp["ln_mlp_b"].reshape(1, D)
    b1, b2 = sp["b_mlp1"].reshape(1, DFF), sp["b_mlp2"].reshape(1, D)
    s5_names = ("ssm_a_re", "ssm_a_im", "ssm_log_dt", "ssm_b_re", "ssm_b_im", "ssm_c_re", "ssm_c_im")
    (lam, bmat, cmat), s5_pull = jax.vjp(_s5_prep, *[sp[n] for n in s5_names])

    def ln_mod2(rv, vv):
        h = _f_ln_mod(rv[0], vv[0], vv[1])
        return [h, h], []

    h_lat, h_lat_t = _rowwise(ln_mod2, [(x, D, 0, 0)], [sc1, sh1], [(D, BF16), (D, BF16, True)], [], nrows=T, tr=TR, name="ln1_lat")
    h_ctx, h_ctx_t = _rowwise(ln_mod2, [(ctx, D, 0, 0)], [csc1, csh1], [(D, BF16), (D, BF16, True)], [], nrows=C, tr=TR,
                              name="ln1_ctx")
    h1 = jnp.concatenate([h_lat, h_ctx], 0)
    h1_t = jnp.concatenate([h_lat_t, h_ctx_t], 1)
    proj = _matmul(h1, wb["w_in"], mode="nn", name="proj", tm=768, tn=512)
    attn = _attn_fwd(proj, sink, tabs)
    states, ssm, ge = _s5_fwd(proj, dskip, lam, bmat, cmat)
    z = _matmul(ge, wb["w_glu"], mode="nn", name="glu_mm", tm=1024, tn=1024)

    def glu_act(rv, vv):
        return [_f_glu(rv[0])], []

    glu, = _rowwise(glu_act, [(z, 2 * SW, 0, 0)], [], [(SW, BF16)], [], nrows=T, tr=TR, name="glu_act")
    attn_d = _matmul(attn, wb["w_attn_up"], mode="nn", name="attn_up", tm=1024, tn=512)
    ssm_d = _matmul(glu, wb["w_ssm_up"], mode="nn", name="ssm_up", tm=1024, tn=512)
    ga_cb, gs_cb = (QW + 2 * KVW + SW) // D, (QW + 2 * KVW + SW) // D + 1

    def mix(rv, vv):
        m_ = _f_mix(*rv)
        return [m_, m_], []

    mixv, mix_t = _rowwise(mix, [(proj, D, ga_cb, 0), (proj, D, gs_cb, 0), (attn_d, D, 0, 0), (ssm_d, D, 0, 0)], [],
                           [(D, BF16), (D, BF16, True)], [], nrows=T, tr=TR, name="mix")
    y = _matmul(mixv, wb["w_out"], mode="nn", name="out_proj", tm=1024, tn=512)

    def post1(rv, vv):
        x1, h2 = _f_post1(rv[0], rv[1], *vv)
        return [x1, h2, h2], []

    x1, h2, h2_t = _rowwise(post1, [(x, D, 0, 0), (y, D, 0, 0)], [g1, lg_mix, lb_mix, sc2, sh2],
                            [(D, F32), (D, BF16), (D, BF16, True)], [], nrows=T, tr=TR, name="post1")

    def relu_sq(acc):
        r = jnp.maximum(acc, 0.0)
        return r, r * r, r * r

    r_act, act, act_t = _matmul(h2, wb["w_mlp1"], mode="nn", name="mlp1", tm=1024, tn=512, bias=b1,
                                out_dtypes=(BF16, BF16, BF16), out_t=(False, False, True), epilogue=relu_sq)
    mlp = _matmul(act, wb["w_mlp2"], mode="nn", name="mlp2", tm=1024, tn=512, tk=2048)

    def loss_fb(rv, vv):
        x1_t, mlp_t, tgt_t = rv
        g2_v, lg_v, lb_v, b2_v = vv
        f = lambda a, m, g, p, q, b: _f_loss(a, m, tgt_t, g, p, q, b)
        val, grads = jax.value_and_grad(f, argnums=(0, 1, 2, 3, 4, 5))(x1_t, mlp_t, g2_v, lg_v, lb_v, b2_v)
        dx1, dmlp, dg2, dlg, dlb, db2 = grads
        return [dx1, dmlp], [jnp.reshape(val, (1, 1)), dg2, dlg, dlb, db2]

    dx1_a, d_mlp, loss_p, d_g2, d_lg_mlp, d_lb_mlp, d_b2 = _rowwise(
        loss_fb, [(x1, D, 0, 0), (mlp, D, 0, 0), (tgt, D, 0, 0)], [g2, lg_mlp, lb_mlp, b2],
        [(D, F32), (D, BF16)], [(1, 1), (1, D), (1, D), (1, D), (1, D)], nrows=T, tr=TR, name="loss_fb")

    gw = _GradDict(on_grad)
    gw.on_finish = on_finish
    loss_done = () if on_loss is None else (on_loss(loss_p),)
    gw["w_mlp2"] = _matmul(act_t, d_mlp, mode="nn", name="dw_mlp2", out_dtypes=(BF16,), tm=1024, tn=512, after=loss_done)
    da, = (_matmul(d_mlp, wb["w_mlp2"], mode="nt", name="d_act", out_dtypes=(BF16,), tm=1024, tn=512,
                   extras=(r_act,), epilogue=lambda acc, r: (acc * (2.0 * r.astype(F32)),), after=(gw.order("w_mlp2"),)),)
    pin = gw.finish("w_mlp2", da)
    ones = jnp.ones((8, T), BF16)
    d_b1 = _matmul(ones, da, mode="nn", name="db_mlp1", tm=8, tn=2048)[0:1]
    gw["w_mlp1"] = _matmul(h2_t, da, mode="nn", name="dw_mlp1", out_dtypes=(BF16,), tm=1024, tn=512, after=pin)
    dh2 = _matmul(da, wb["w_mlp1"], mode="nt", name="d_h2", tm=1024, tn=512, tk=2048, after=(gw.order("w_mlp1"),))

    def post1_b(rv, vv):
        x_t, y_t, dx1_t, dh2_t = rv
        gr, gv = _vjp_rows(_f_post1, (x_t, y_t, *vv), (dx1_t, dh2_t), 2)
        return [gr[0], gr[1]], gv

    dx_a, dy, d_g1, d_lg_mix, d_lb_mix, d_sc2, d_sh2 = _rowwise(
        post1_b, [(x, D, 0, 0), (y, D, 0, 0), (dx1_a, D, 0, 0), (dh2, D, 0, 0)], [g1, lg_mix, lb_mix, sc2, sh2],
        [(D, F32), (D, BF16)], [(1, D)] * 5, nrows=T, tr=TR, name="post1_bwd")
    gw["w_out"] = _matmul(mix_t, dy, mode="nn", name="dw_out", out_dtypes=(BF16,), tm=1024, tn=512)
    dmix = _matmul(dy, wb["w_out"], mode="nt", name="d_mix", tm=1024, tn=512, after=(gw.order("w_out"),))

    def mix_b(rv, vv):
        gr, _ = _vjp_rows(_f_mix, tuple(rv[:4]), rv[4], 4)
        return gr, []

    d_ga, d_gs, d_attn_d, d_ssm_d = _rowwise(
        mix_b, [(proj, D, ga_cb, 0), (proj, D, gs_cb, 0), (attn_d, D, 0, 0), (ssm_d, D, 0, 0), (dmix, D, 0, 0)], [],
        [(D, BF16)] * 4, [], nrows=T, tr=TR, name="mix_bwd")
    pin = gw.finish("w_mlp1", d_ga)
    gw["w_attn_up"] = _matmul(attn, d_attn_d, mode="tn", name="dw_attn_up", out_dtypes=(BF16,), tm=512, tn=1024, tk=1024, after=pin)
    d_attn = _matmul(d_attn_d, wb["w_attn_up"], mode="nt", name="d_attn", out_dtypes=(BF16,), tm=1024, tn=512)
    gw["w_ssm_up"] = _matmul(glu, d_ssm_d, mode="tn", name="dw_ssm_up", out_dtypes=(BF16,), tm=512, tn=1024, tk=1024)
    d_glu = _matmul(d_ssm_d, wb["w_ssm_up"], mode="nt", name="d_glu", tm=1024, tn=512, after=(gw.order("w_attn_up"), gw.order("w_ssm_up")))

    def glu_b(rv, vv):
        gr, _ = _vjp_rows(_f_glu, (rv[0],), rv[1], 1)
        return gr, []

    dz, = _rowwise(glu_b, [(z, 2 * SW, 0, 0), (d_glu, SW, 0, 0)], [], [(2 * SW, BF16)], [], nrows=T, tr=TR, name="glu_bwd")
    gw["w_glu"] = _matmul(ge, dz, mode="tn", name="dw_glu", out_dtypes=(BF16,), tm=512, tn=1024, tk=1024)
    d_ge = _matmul(dz, wb["w_glu"], mode="nt", name="d_ge", tm=1024, tn=512, after=(gw.order("w_glu"),))

    du_all, d_dskip, dlam, dbmat, dcmat = _s5_bwd(d_ge, ssm, proj, dskip, states, lam, bmat, cmat)
    s5_grads = s5_pull((dlam, dbmat, dcmat))
    early = dict(zip(s5_names, s5_grads), ssm_d=d_dskip)
    if on_early is not None:
        on_early(early)
    pin = gw.finish("w_glu", du_all)

    dq, dk, dv, dsink = _attn_bwd(proj, d_attn, sink, tabs)
    zc = lambda w: jnp.zeros((C, w), BF16)
    dproj = jnp.concatenate([
        jnp.concatenate([dq, zc(QW)], 0), dk, dv, du_all,
        jnp.concatenate([d_ga, zc(D)], 0), jnp.concatenate([d_gs, zc(D)], 0)], 1)
    gw["w_in"] = _matmul(h1_t, dproj, mode="nn", name="dw_in", out_dtypes=(BF16,), tm=1024, tn=512, after=pin)
    pin = gw.finish("w_in", gw["w_in"])
    dh1 = _matmul(dproj, wb["w_in"], mode="nt", name="d_h1", tm=768, tn=512, tk=2048, after=pin)

    def ln1_b(rv, vv):
        x_t, dh_t, dxa_t = rv
        gr, gv = _vjp_rows(_f_ln_mod, (x_t, vv[0], vv[1]), dh_t, 1)
        return [gr[0] + dxa_t], gv

    grad_x, d_sc1, d_sh1 = _rowwise(ln1_b, [(x, D, 0, 0), (dh1, D, 0, 0), (dx_a, D, 0, 0)], [sc1, sh1],
                                    [(D, F32)], [(1, D), (1, D)], nrows=T, tr=TR, name="ln1_lat_bwd")

    def ln1c_b(rv, vv):
        _, gv = _vjp_rows(_f_ln_mod, (rv[0], vv[0], vv[1]), rv[1], 1)
        return [], gv

    d_csc1, d_csh1 = _rowwise(ln1c_b, [(ctx, D, 0, 0), (dh1, D, 0, T // TR)], [csc1, csh1],
                              [], [(1, D), (1, D)], nrows=C, tr=TR, name="ln1_ctx_bwd")

    d_mod_lat = jnp.concatenate([d_sh1, d_sc1, d_g1, d_sh2, d_sc2, d_g2], 1)
    zv = jnp.zeros((1, D), F32)
    d_mod_ctx = jnp.concatenate([d_csh1, d_csc1, zv, zv, zv, zv], 1)
    gs = {n: g for n, g in zip(s5_names, s5_grads)}
    gs["attn_sink"] = dsink[:, 0]
    gs["ssm_d"] = d_dskip
    gs["ln_mix_g"], gs["ln_mix_b"] = d_lg_mix, d_lb_mix
    gs["ln_mlp_g"], gs["ln_mlp_b"] = d_lg_mlp, d_lb_mlp
    gs["b_mlp1"], gs["b_mlp2"] = d_b1, d_b2
    return loss_p, grad_x, d_mod_lat, d_mod_ctx, gw, gs


def _my_pos():
    return lax.axis_index("x"), lax.axis_index("y"), lax.axis_index("c")


def _flip(p, bit):
    return 1 - p if bit else p


def _peer(pos, k):
    x, y, c = pos
    return (_flip(x, (k >> 2) & 1), _flip(y, (k >> 1) & 1), _flip(c, k & 1))


def _lin(pos):
    return 4 * pos[0] + 2 * pos[1] + pos[2]


def _allgather_small(v, name):
    r, w = v.shape

    def body(v_ref, out_ref, send_sems, recv_sems, local_sem):
        me = _my_pos()
        mine = pltpu.make_async_copy(v_ref, out_ref.at[_lin(me)], local_sem)
        mine.start()
        sends = []
        for k in range(1, N_DEV):
            cp = pltpu.make_async_remote_copy(src_ref=v_ref, dst_ref=out_ref.at[_lin(me)], send_sem=send_sems.at[k - 1],
                                              recv_sem=recv_sems.at[k - 1], device_id=_peer(me, k), device_id_type=MESH)
            cp.start()
            sends.append(cp)
        for k in range(1, N_DEV):
            peer = _peer(me, k)
            pltpu.make_async_remote_copy(src_ref=v_ref, dst_ref=out_ref.at[_lin(peer)], send_sem=send_sems.at[k - 1],
                                         recv_sem=recv_sems.at[k - 1], device_id=peer, device_id_type=MESH).wait_recv()
        for cp in sends:
            cp.wait_send()
        mine.wait()

    return pl.pallas_call(
        body,
        name=name,
        out_shape=jax.ShapeDtypeStruct((N_DEV, r, w), v.dtype),
        in_specs=[pl.BlockSpec(memory_space=pltpu.VMEM)],
        out_specs=pl.BlockSpec(memory_space=pltpu.VMEM),
        scratch_shapes=[pltpu.SemaphoreType.DMA((N_DEV - 1,)), pltpu.SemaphoreType.DMA((N_DEV - 1,)), pltpu.SemaphoreType.DMA],
        compiler_params=pltpu.CompilerParams(vmem_limit_bytes=VMEM_LIMIT_BYTES),
    )(v)


def _block_of(ref, kind, idx, n):
    start = pl.multiple_of(idx * n, 128)
    if kind == "col":
        return ref.at[:, pl.ds(start, n)]
    return ref.at[pl.ds(start, n), :]


def _allgather_weights(shards, kinds):
    nt = len(shards)
    out_shape = []
    for s, kind in zip(shards, kinds):
        k, n = s.shape
        out_shape.append(jax.ShapeDtypeStruct((k, n * N_DEV) if kind == "col" else (k * N_DEV, n), s.dtype))

    def body(*refs):
        ins, outs = refs[:nt], refs[nt:2 * nt]
        send_sems, recv_sems, local_sems = refs[2 * nt:]
        x, y, c = _my_pos()
        me, sibling = (x, y, c), (x, y, 1 - c)
        chips = [(1 - x, y), (x, 1 - y), (1 - x, 1 - y)]

        def blk(t, pos):
            n = shards[t].shape[1] if kinds[t] == "col" else shards[t].shape[0]
            return _block_of(outs[t], kinds[t], _lin(pos), n)

        def copy(t, k, block, to, src=None):
            return pltpu.make_async_remote_copy(src_ref=blk(t, block) if src is None else src, dst_ref=blk(t, block),
                                                send_sem=send_sems.at[t, k], recv_sem=recv_sems.at[t, k],
                                                device_id=to, device_id_type=MESH)

        local, sends = [], []
        for t in range(nt):
            mine = pltpu.make_async_copy(ins[t], blk(t, me), local_sems.at[t])
            mine.start()
            local.append(mine)
            first = [copy(t, 0, me, sibling, src=ins[t])]
            first += [copy(t, 1 + j, me, (*chip, c), src=ins[t]) for j, chip in enumerate(chips)]
            for cp in first:
                cp.start()
            sends += first
        for t in range(nt):
            for j, chip in enumerate(chips):
                copy(t, 1 + j, (*chip, c), me).wait_recv()
                fwd = copy(t, 4 + j, (*chip, c), sibling)
                fwd.start()
                sends.append(fwd)
        for t in range(nt):
            copy(t, 0, sibling, me).wait_recv()
            for j, chip in enumerate(chips):
                copy(t, 4 + j, (*chip, 1 - c), me).wait_recv()
        for cp in sends:
            cp.wait_send()
        for cp in local:
            cp.wait()

    any_spec = pl.BlockSpec(memory_space=pl.ANY)
    return pl.pallas_call(
        body,
        name="allgather_weights",
        out_shape=out_shape,
        in_specs=[any_spec] * nt,
        out_specs=[any_spec] * nt,
        scratch_shapes=[pltpu.SemaphoreType.DMA((nt, N_DEV - 1)), pltpu.SemaphoreType.DMA((nt, N_DEV - 1)),
                        pltpu.SemaphoreType.DMA((nt,))],
    )(*shards)


def _handshake(peers):
    barrier = pltpu.get_barrier_semaphore()
    for peer in peers:
        pl.semaphore_signal(barrier, inc=1, device_id=peer, device_id_type=MESH)
    pl.semaphore_wait(barrier, len(peers))


def _allgather_weights_seq(shards, kinds, name, collective_id):
    nt = len(shards)
    hbm = pltpu.MemorySpace.HBM
    ins = [jax.new_ref(s, memory_space=hbm) for s in shards]
    outs = []
    for s, kind in zip(shards, kinds):
        k, n = s.shape
        shape = (k, n * N_DEV) if kind == "col" else (k * N_DEV, n)
        outs.append(jax.empty_ref(jax.ShapeDtypeStruct(shape, s.dtype), memory_space=hbm))

    @functools.partial(
        pl.kernel, mesh=plsc.ScalarSubcoreMesh(axis_name="seq", num_cores=1), name=name,
        scratch_types=(pltpu.SemaphoreType.DMA((nt, N_DEV - 1)), pltpu.SemaphoreType.DMA((nt, N_DEV - 1)),
                       pltpu.SemaphoreType.DMA((nt,))),
        compiler_params=pltpu.CompilerParams(collective_id=collective_id))
    def launch(send_sems, recv_sems, local_sems):
        x, y, c = _my_pos()
        me, sibling = (x, y, c), (x, y, 1 - c)
        chips = [(1 - x, y), (x, 1 - y), (1 - x, 1 - y)]
        _handshake([sibling] + [(*chip, c) for chip in chips])

        def blk(t, pos):
            n = shards[t].shape[1] if kinds[t] == "col" else shards[t].shape[0]
            return _block_of(outs[t], kinds[t], _lin(pos), n)

        def copy(t, k, block, to, src=None):
            return pltpu.make_async_remote_copy(src_ref=blk(t, block) if src is None else src, dst_ref=blk(t, block),
                                                send_sem=send_sems.at[t, k], recv_sem=recv_sems.at[t, k],
                                                device_id=to, device_id_type=MESH)

        local, sends = [], []
        for t in range(nt):
            mine = pltpu.make_async_copy(ins[t], blk(t, me), local_sems.at[t])
            mine.start()
            local.append(mine)
            first = [copy(t, 0, me, sibling, src=ins[t])]
            first += [copy(t, 1 + j, me, (*chip, c), src=ins[t]) for j, chip in enumerate(chips)]
            for cp in first:
                cp.start()
            sends += first
        for t in range(nt):
            for j, chip in enumerate(chips):
                copy(t, 1 + j, (*chip, c), me).wait_recv()
                fwd = copy(t, 4 + j, (*chip, c), sibling)
                fwd.start()
                sends.append(fwd)
        for t in range(nt):
            copy(t, 0, sibling, me).wait_recv()
            for j, chip in enumerate(chips):
                copy(t, 4 + j, (*chip, 1 - c), me).wait_recv()
        for cp in sends:
            cp.wait_send()
        for cp in local:
            cp.wait()

    launch()
    return [o[...] for o in outs]


def _allgather_small_seq(v, name, collective_id):
    hbm = pltpu.MemorySpace.HBM
    src = jax.new_ref(v, memory_space=hbm)
    out = jax.empty_ref(jax.ShapeDtypeStruct((N_DEV,) + v.shape, v.dtype), memory_space=hbm)

    @functools.partial(
        pl.kernel, mesh=plsc.ScalarSubcoreMesh(axis_name="seq", num_cores=1), name=name,
        scratch_types=(pltpu.SemaphoreType.DMA((N_DEV - 1,)), pltpu.SemaphoreType.DMA((N_DEV - 1,)), pltpu.SemaphoreType.DMA),
        compiler_params=pltpu.CompilerParams(collective_id=collective_id))
    def launch(send_sems, recv_sems, local_sem):
        me = _my_pos()
        _handshake([_peer(me, k) for k in range(1, N_DEV)])
        mine = pltpu.make_async_copy(src, out.at[_lin(me)], local_sem)
        mine.start()
        sends = []
        for k in range(1, N_DEV):
            cp = pltpu.make_async_remote_copy(src_ref=src, dst_ref=out.at[_lin(me)], send_sem=send_sems.at[k - 1],
                                              recv_sem=recv_sems.at[k - 1], device_id=_peer(me, k), device_id_type=MESH)
            cp.start()
            sends.append(cp)
        for k in range(1, N_DEV):
            peer = _peer(me, k)
            pltpu.make_async_remote_copy(src_ref=src, dst_ref=out.at[_lin(peer)], send_sem=send_sems.at[k - 1],
                                         recv_sem=recv_sems.at[k - 1], device_id=peer, device_id_type=MESH).wait_recv()
        for cp in sends:
            cp.wait_send()
        mine.wait()

    launch()
    return out[...]


N_CHIP = N_DEV // 2


def _chip_of(pos):
    return 2 * pos[0] + pos[1]


def _pair_exchange_seq(grads, kinds, name, collective_id):
    nt = len(grads)
    hbm = pltpu.MemorySpace.HBM
    shard_shapes = _shard_shapes(grads, kinds)
    ins = [jax.new_ref(g, memory_space=hbm) for g in grads]
    outs = [jax.empty_ref(jax.ShapeDtypeStruct((N_CHIP,) + s, g.dtype), memory_space=hbm) for s, g in zip(shard_shapes, grads)]

    @functools.partial(
        pl.kernel, mesh=plsc.ScalarSubcoreMesh(axis_name="seq", num_cores=1), name=name,
        scratch_types=(pltpu.SemaphoreType.DMA((nt, N_CHIP)), pltpu.SemaphoreType.DMA((nt, N_CHIP))),
        compiler_params=pltpu.CompilerParams(collective_id=collective_id))
    def launch(send_sems, recv_sems):
        x, y, c = _my_pos()
        sibling = (x, y, 1 - c)
        _handshake([sibling])
        copies = []
        for t in range(nt):
            n = shard_shapes[t][1] if kinds[t] == "col" else shard_shapes[t][0]
            for q in range(N_CHIP):
                cp = pltpu.make_async_remote_copy(src_ref=_block_of(ins[t], kinds[t], 2 * q + (1 - c), n), dst_ref=outs[t].at[q],
                                                  send_sem=send_sems.at[t, q], recv_sem=recv_sems.at[t, q],
                                                  device_id=sibling, device_id_type=MESH)
                cp.start()
                copies.append(cp)
        for cp in copies:
            cp.wait_recv()
        for cp in copies:
            cp.wait_send()

    launch()
    return [o[...] for o in outs]


def _pair_add(g, half, kind, name, after=()):
    nq, k, ns = half.shape
    tr = min(k, 512)
    c_idx = lax.axis_index("c").astype(jnp.int32).reshape(1)
    if kind == "col":
        g_spec = pl.BlockSpec((tr, ns), lambda q, i, c_ref: (i, 2 * q + c_ref[0]))
    else:
        g_spec = pl.BlockSpec((tr, ns), lambda q, i, c_ref: ((2 * q + c_ref[0]) * (k // tr) + i, 0))
    n_after = len(after)

    def kern(c_ref, g_ref, h_ref, *rest):
        o_ref = rest[n_after]
        o_ref[0] = (g_ref[...].astype(F32) + h_ref[0].astype(F32)).astype(o_ref.dtype)

    return pl.pallas_call(
        kern,
        name=name,
        grid_spec=pltpu.PrefetchScalarGridSpec(
            num_scalar_prefetch=1,
            grid=(nq, k // tr),
            in_specs=[g_spec, pl.BlockSpec((1, tr, ns), lambda q, i, c_ref: (q, i, 0))] + [pl.BlockSpec(memory_space=pl.ANY)] * n_after,
            out_specs=pl.BlockSpec((1, tr, ns), lambda q, i, c_ref: (q, i, 0)),
        ),
        out_shape=jax.ShapeDtypeStruct(half.shape, half.dtype),
        compiler_params=_cparams(("parallel", "parallel")),
    )(c_idx, g, half, *after)


def _chip_exchange_seq(psums, name, collective_id):
    nt = len(psums)
    hbm = pltpu.MemorySpace.HBM
    ins = [jax.new_ref(s, memory_space=hbm) for s in psums]
    outs = [jax.empty_ref(jax.ShapeDtypeStruct(s.shape, s.dtype), memory_space=hbm) for s in psums]

    @functools.partial(
        pl.kernel, mesh=plsc.ScalarSubcoreMesh(axis_name="seq", num_cores=1), name=name,
        scratch_types=(pltpu.SemaphoreType.DMA((nt, N_CHIP - 1)), pltpu.SemaphoreType.DMA((nt, N_CHIP - 1)),
                       pltpu.SemaphoreType.DMA((nt,))),
        compiler_params=pltpu.CompilerParams(collective_id=collective_id))
    def launch(send_sems, recv_sems, local_sems):
        me = _my_pos()
        peers = [_peer(me, k) for k in (2, 4, 6)]
        _handshake(peers)
        mine = _chip_of(me)
        local, sends = [], []
        for t in range(nt):
            cp = pltpu.make_async_copy(ins[t].at[mine], outs[t].at[mine], local_sems.at[t])
            cp.start()
            local.append(cp)
            for j, peer in enumerate(peers):
                cp = pltpu.make_async_remote_copy(src_ref=ins[t].at[_chip_of(peer)], dst_ref=outs[t].at[mine],
                                                  send_sem=send_sems.at[t, j], recv_sem=recv_sems.at[t, j],
                                                  device_id=peer, device_id_type=MESH)
                cp.start()
                sends.append(cp)
        for t in range(nt):
            for j, peer in enumerate(peers):
                pltpu.make_async_remote_copy(src_ref=ins[t].at[mine], dst_ref=outs[t].at[_chip_of(peer)],
                                             send_sem=send_sems.at[t, j], recv_sem=recv_sems.at[t, j],
                                             device_id=peer, device_id_type=MESH).wait_recv()
        for cp in sends:
            cp.wait_send()
        for cp in local:
            cp.wait()

    launch()
    return [o[...] for o in outs]


def _scatter_grads_seq(grads, kinds, name, collective_id):
    nt = len(grads)
    hbm = pltpu.MemorySpace.HBM
    shard_shapes = []
    for g, kind in zip(grads, kinds):
        k, n = g.shape
        shard_shapes.append((k, n // N_DEV) if kind == "col" else (k // N_DEV, n))
    ins = [jax.new_ref(g, memory_space=hbm) for g in grads]
    outs = [jax.empty_ref(jax.ShapeDtypeStruct((N_DEV,) + s, g.dtype), memory_space=hbm) for s, g in zip(shard_shapes, grads)]

    @functools.partial(
        pl.kernel, mesh=plsc.ScalarSubcoreMesh(axis_name="seq", num_cores=1), name=name,
        scratch_types=(pltpu.SemaphoreType.DMA((nt, N_DEV - 1)), pltpu.SemaphoreType.DMA((nt, N_DEV - 1)),
                       pltpu.SemaphoreType.DMA((nt,))),
        compiler_params=pltpu.CompilerParams(collective_id=collective_id))
    def launch(send_sems, recv_sems, local_sems):
        me = _my_pos()
        _handshake([_peer(me, k) for k in range(1, N_DEV)])

        def blk(t, pos):
            n = shard_shapes[t][1] if kinds[t] == "col" else shard_shapes[t][0]
            return _block_of(ins[t], kinds[t], _lin(pos), n)

        local, sends = [], []
        for t in range(nt):
            cp = pltpu.make_async_copy(blk(t, me), outs[t].at[_lin(me)], local_sems.at[t])
            cp.start()
            local.append(cp)
            for k in range(1, N_DEV):
                peer = _peer(me, k)
                cp = pltpu.make_async_remote_copy(src_ref=blk(t, peer), dst_ref=outs[t].at[_lin(me)], send_sem=send_sems.at[t, k - 1],
                                                  recv_sem=recv_sems.at[t, k - 1], device_id=peer, device_id_type=MESH)
                cp.start()
                sends.append(cp)
        for t in range(nt):
            for k in range(1, N_DEV):
                peer = _peer(me, k)
                pltpu.make_async_remote_copy(src_ref=blk(t, me), dst_ref=outs[t].at[_lin(peer)], send_sem=send_sems.at[t, k - 1],
                                             recv_sem=recv_sems.at[t, k - 1], device_id=peer, device_id_type=MESH).wait_recv()
        for cp in sends:
            cp.wait_send()
        for cp in local:
            cp.wait()

    launch()
    return [o[...] for o in outs]


_HBM_SPEC = pl.BlockSpec(memory_space=pltpu.HBM)
_SEM_SPEC = pl.BlockSpec(memory_space=pltpu.SEMAPHORE)
_EFFECT = pltpu.SideEffectType.DATAFLOW_SIDE_EFFECTING
LOCAL_CHUNKS = 16


def _shard_shapes(grads, kinds):
    return [(g.shape[0], g.shape[1] // N_DEV) if kind == "col" else (g.shape[0] // N_DEV, g.shape[1]) for g, kind in zip(grads, kinds)]


def _scatter_copies(g_refs, land_refs, send_sems, recv_sems, kinds, shard_shapes):
    me = _my_pos()
    copies = []
    for t in range(len(g_refs)):
        n = shard_shapes[t][1] if kinds[t] == "col" else shard_shapes[t][0]
        for k in range(1, N_DEV):
            peer = _peer(me, k)
            copies.append(pltpu.make_async_remote_copy(
                src_ref=_block_of(g_refs[t], kinds[t], _lin(peer), n), dst_ref=land_refs[t].at[_lin(me)],
                send_sem=send_sems.at[t * (N_DEV - 1) + k - 1], recv_sem=recv_sems.at[t * (N_DEV - 1) + k - 1],
                device_id=peer, device_id_type=MESH))
    return copies


def _scatter_start(grads, kinds, name):
    nt = len(grads)
    shard_shapes = _shard_shapes(grads, kinds)

    def body(*refs):
        g_refs, land_refs = refs[:nt], refs[nt:2 * nt]
        send_sems, recv_sems = refs[2 * nt], refs[2 * nt + 1]
        token = refs[2 * nt + 2 + 2 * nt]
        local_sems = refs[-1]
        me = _my_pos()
        local = []
        for t in range(nt):
            n = shard_shapes[t][1] if kinds[t] == "col" else shard_shapes[t][0]
            src, dst = _block_of(g_refs[t], kinds[t], _lin(me), n), land_refs[t].at[_lin(me)]
            rows = shard_shapes[t][0] // LOCAL_CHUNKS
            for ch in range(LOCAL_CHUNKS):
                rs = pl.ds(ch * rows, rows)
                cp = pltpu.make_async_copy(src.at[rs, :], dst.at[rs, :], local_sems.at[t * LOCAL_CHUNKS + ch])
                cp.start()
                local.append(cp)
        token[...] = jnp.zeros_like(token)
        for cp in local:
            cp.wait()
        for cp in _scatter_copies(g_refs, land_refs, send_sems, recv_sems, kinds, shard_shapes):
            cp.start()

    lands = [pltpu.with_memory_space_constraint(lax.empty((N_DEV,) + s, g.dtype), pltpu.HBM) for s, g in zip(shard_shapes, grads)]
    sem_shape = pltpu.SemaphoreType.DMA((nt * (N_DEV - 1),))
    out = pl.pallas_call(
        body,
        name=name,
        out_shape=(sem_shape, sem_shape, *[pltpu.HBM(g.shape, g.dtype) for g in grads],
                   *[pltpu.HBM(l.shape, l.dtype) for l in lands], jax.ShapeDtypeStruct((8, 128), F32)),
        in_specs=[_HBM_SPEC] * (2 * nt),
        out_specs=(_SEM_SPEC, _SEM_SPEC, *[_HBM_SPEC] * (2 * nt), pl.BlockSpec(memory_space=pltpu.VMEM)),
        input_output_aliases={i: 2 + i for i in range(2 * nt)},
        scratch_shapes=[pltpu.SemaphoreType.DMA((nt * LOCAL_CHUNKS,))],
        compiler_params=pltpu.CompilerParams(has_side_effects=_EFFECT),
    )(*[pltpu.with_memory_space_constraint(g, pltpu.HBM) for g in grads], *lands)
    return out[0], out[1], list(out[2:2 + nt]), list(out[2 + nt:2 + 2 * nt]), out[-1]


def _scatter_wait(send_sems, recv_sems, g_thru, land_thru, kinds, after, name):
    nt = len(g_thru)
    shard_shapes = _shard_shapes(g_thru, kinds)

    def body(*refs):
        g_refs, land_refs = refs[:nt], refs[nt:2 * nt]
        send_sems, recv_sems = refs[2 * nt], refs[2 * nt + 1]
        for cp in _scatter_copies(g_refs, land_refs, send_sems, recv_sems, kinds, shard_shapes):
            cp.wait_send()
            cp.wait_recv()

    out = pl.pallas_call(
        body,
        name=name,
        out_shape=tuple(pltpu.HBM(a.shape, a.dtype) for a in (*g_thru, *land_thru)),
        in_specs=[*[_HBM_SPEC] * (2 * nt), _SEM_SPEC, _SEM_SPEC, pl.BlockSpec(memory_space=pl.ANY)],
        out_specs=tuple([_HBM_SPEC] * (2 * nt)),
        input_output_aliases={i: i for i in range(2 * nt)},
        compiler_params=pltpu.CompilerParams(has_side_effects=_EFFECT),
    )(*g_thru, *land_thru, send_sems, recv_sems, after)
    return list(out[nt:])


def _scatter_grads(grads, kinds):
    nt = len(grads)
    shard_shapes = []
    for g, kind in zip(grads, kinds):
        k, n = g.shape
        shard_shapes.append((k, n // N_DEV) if kind == "col" else (k // N_DEV, n))

    def body(*refs):
        ins, outs = refs[:nt], refs[nt:2 * nt]
        send_sems, recv_sems, local_sems = refs[2 * nt:]
        me = _my_pos()

        def blk(t, pos):
            n = shard_shapes[t][1] if kinds[t] == "col" else shard_shapes[t][0]
            return _block_of(ins[t], kinds[t], _lin(pos), n)

        local, sends = [], []
        for t in range(nt):
            cp = pltpu.make_async_copy(blk(t, me), outs[t].at[_lin(me)], local_sems.at[t])
            cp.start()
            local.append(cp)
            for k in range(1, N_DEV):
                peer = _peer(me, k)
                cp = pltpu.make_async_remote_copy(src_ref=blk(t, peer), dst_ref=outs[t].at[_lin(me)], send_sem=send_sems.at[t, k - 1],
                                                  recv_sem=recv_sems.at[t, k - 1], device_id=peer, device_id_type=MESH)
                cp.start()
                sends.append(cp)
        for t in range(nt):
            for k in range(1, N_DEV):
                peer = _peer(me, k)
                pltpu.make_async_remote_copy(src_ref=blk(t, me), dst_ref=outs[t].at[_lin(peer)], send_sem=send_sems.at[t, k - 1],
                                             recv_sem=recv_sems.at[t, k - 1], device_id=peer, device_id_type=MESH).wait_recv()
        for cp in sends:
            cp.wait_send()
        for cp in local:
            cp.wait()

    any_spec = pl.BlockSpec(memory_space=pl.ANY)
    return pl.pallas_call(
        body,
        name="scatter_grads",
        out_shape=[jax.ShapeDtypeStruct((N_DEV,) + s, g.dtype) for s, g in zip(shard_shapes, grads)],
        in_specs=[any_spec] * nt,
        out_specs=[any_spec] * nt,
        scratch_shapes=[pltpu.SemaphoreType.DMA((nt, N_DEV - 1)), pltpu.SemaphoreType.DMA((nt, N_DEV - 1)),
                        pltpu.SemaphoreType.DMA((nt,))],
    )(*grads)


def _adam(g_slots, w, m, v, *, tr, name, after=()):
    ns, r, wd = g_slots.shape
    tr = min(tr, r)
    assert r % tr == 0, (name, r, tr)
    c1 = 1.0 - ADAM_B1 ** ADAM_STEP
    c2 = 1.0 - ADAM_B2 ** ADAM_STEP
    n_after = len(after)

    def kern(g_ref, w_ref, m_ref, v_ref, *rest):
        go_ref, d_ref, mo_ref, vo_ref = rest[n_after:]
        g = g_ref[0].astype(F32)
        for s in range(1, ns):
            g = g + g_ref[s].astype(F32)
        m_new = ADAM_B1 * m_ref[...] + (1.0 - ADAM_B1) * g
        v_new = ADAM_B2 * v_ref[...] + (1.0 - ADAM_B2) * (g * g)
        m_hat = m_new / c1
        v_hat = v_new / c2
        go_ref[...] = g
        d_ref[...] = -ADAM_LR * (m_hat / (jnp.sqrt(v_hat) + ADAM_EPS) + ADAM_WD * w_ref[...])
        mo_ref[...] = m_new
        vo_ref[...] = v_new

    tile = pl.BlockSpec((tr, wd), lambda i: (i, 0))
    return pl.pallas_call(
        kern,
        name=name,
        grid=(r // tr,),
        in_specs=[pl.BlockSpec((ns, tr, wd), lambda i: (0, i, 0)), tile, tile, tile] + [pl.BlockSpec(memory_space=pl.ANY)] * n_after,
        out_specs=[tile] * 4,
        out_shape=[jax.ShapeDtypeStruct((r, wd), F32)] * 4,
        compiler_params=_cparams(("parallel",)),
    )(g_slots, w, m, v, *after)


SMALL = ("c_ctx", "b_ada", "attn_sink", "ssm_a_re", "ssm_a_im", "ssm_log_dt", "ssm_b_re", "ssm_b_im", "ssm_c_re", "ssm_c_im",
         "ssm_d", "ln_mix_g", "ln_mix_b", "b_mlp1", "b_mlp2", "ln_mlp_g", "ln_mlp_b")
BIG = ("w_in", "w_glu", "w_attn_up", "w_ssm_up", "w_out", "w_mlp1", "w_mlp2")
BIG_KIND = ("col", "col", "col", "col", "row", "col", "row")
AG_GROUPS = (("w_in",), ("w_glu", "w_attn_up", "w_ssm_up", "w_out"), ("w_mlp1",), ("w_mlp2",))
AG_COLLECTIVE_ID0 = 1
RS_GROUPS = (("w_mlp2",), ("w_mlp1",), ("w_out", "w_attn_up", "w_ssm_up", "w_glu"), ("w_in",))
RS_COLLECTIVE_ID0 = AG_COLLECTIVE_ID0 + len(AG_GROUPS)
SMALL_EARLY = ("ssm_a_re", "ssm_a_im", "ssm_log_dt", "ssm_b_re", "ssm_b_im", "ssm_c_re", "ssm_c_im", "ssm_d")
SMALL_LATE = tuple(n for n in SMALL if n not in SMALL_EARLY)
SMALL_COLLECTIVE_ID0 = RS_COLLECTIVE_ID0 + 2 * len(RS_GROUPS)
LANES = 128


def _pack(parts):
    rows = []
    for p in parts:
        flat = p.reshape(-1).astype(F32)
        pad = (-flat.shape[0]) % LANES
        rows.append(jnp.pad(flat, (0, pad)).reshape(-1, LANES))
    packed = jnp.concatenate(rows, 0)
    return jnp.pad(packed, ((0, (-packed.shape[0]) % 8), (0, 0)))


def _unpack(packed, shapes):
    out, r0 = [], 0
    for s in shapes:
        n = math.prod(s)
        nr = -(-n // LANES)
        out.append(packed[r0:r0 + nr].reshape(-1)[:n].reshape(s))
        r0 += nr
    return out


WEIGHTS = ("c_ctx", "w_ada", "b_ada", "w_in", "attn_sink", "ssm_a_re", "ssm_a_im", "ssm_log_dt", "ssm_b_re", "ssm_b_im",
           "ssm_c_re", "ssm_c_im", "ssm_d", "w_glu", "w_attn_up", "w_ssm_up", "w_out", "ln_mix_g", "ln_mix_b", "w_mlp1",
           "b_mlp1", "w_mlp2", "b_mlp2", "ln_mlp_g", "ln_mlp_b")
ADA_COLS = 6 * D // N_DEV


def _step(x, c, ctx, loss_target, p, m, v):
    me = _lin(_my_pos())
    x2, ctx2, tgt2 = x[0], ctx[0], loss_target[0]

    wb = {}
    for gi, group in enumerate(AG_GROUPS):
        full = _allgather_weights_seq([p[n][0].astype(BF16) for n in group], [BIG_KIND[BIG.index(n)] for n in group],
                                      "allgather_seq%d" % gi, AG_COLLECTIVE_ID0 + gi)
        wb.update(zip(group, full))

    c_all = _allgather_small(jnp.broadcast_to(c, (8, D)), "gather_c")[:, 0, :]
    cc = p["c_ctx"].reshape(1, D)
    s_in = jnp.concatenate([c_all, cc, jnp.zeros((7, D), F32)], 0)
    s_act, = _rowwise(lambda rv, vv: ([_silu(rv[0])], []), [(s_in, D, 0, 0)], [], [(D, F32)], [], nrows=16, tr=16, name="silu_c")
    b_mine = lax.dynamic_slice_in_dim(p["b_ada"], me * ADA_COLS, ADA_COLS, axis=1)
    mod_part = _matmul(s_act, p["w_ada"][0], mode="nn", name="ada_fwd", tm=16, tn=512, bias=b_mine)
    mod_all = _allgather_small(mod_part, "gather_mod")
    mod_lat = lax.dynamic_index_in_dim(mod_all, me, axis=1, keepdims=False).reshape(1, 6 * D)
    mod_ctx = mod_all[:, 8, :].reshape(1, 6 * D)

    sp = {n: p[n][0] for n in SMALL if n not in ("c_ctx", "b_ada")}
    recv, halves = {}, {}

    def on_grad(gw):
        for gi, group in enumerate(RS_GROUPS):
            if gi not in halves and all(n in gw for n in group):
                kinds = [BIG_KIND[BIG.index(n)] for n in group]
                halves[gi] = (dict(gw), _pair_exchange_seq([gw[n] for n in group], kinds, "pair_exchange%d" % gi, RS_COLLECTIVE_ID0 + 2 * gi))

    def on_finish(key, after):
        gi = [i for i, group in enumerate(RS_GROUPS) if key in group][0]
        group = RS_GROUPS[gi]
        grads, half = halves[gi]
        prev = tuple(recv[n] for n in RS_GROUPS[gi - 1][:1]) if gi else ()
        if gi == len(RS_GROUPS) - 1:
            prev += (small["early"],)
        psums =[_pair_add(grads[n], h, BIG_KIND[BIG.index(n)], "pair_add_" + n, after=(after,) + prev) for n, h in zip(group, half)]
        recv.update(zip(group, _chip_exchange_seq(psums, "chip_exchange%d" % gi, RS_COLLECTIVE_ID0 + 2 * gi + 1)))
        return psums[-1]

    small = {}

    def on_early(gs_early):
        small["early"] = _allgather_small_seq(_pack([gs_early[n] for n in SMALL_EARLY]), "gather_small_early", SMALL_COLLECTIVE_ID0)

    total = {}

    def on_loss(loss_p):
        total["loss"] = lax.psum(loss_p[0, 0], ("x", "y", "c"))
        return total["loss"].reshape(1, 1)

    loss_p, grad_x, d_mod_lat, d_mod_ctx, gw, gs = _local_step(x2, ctx2, tgt2, mod_lat, mod_ctx, wb, sp, on_grad, on_loss, on_finish, on_early)

    g_early = small["early"]
    res = {}
    last = ()

    def adam_small(names, g_pack, tag, after):
        sm = _adam(g_pack, _pack([p[n] for n in names]), _pack([m[n] for n in names]), _pack([v[n] for n in names]),
                   tr=g_pack.shape[1], name="adam_small_" + tag, after=after)
        shapes = [p[n].shape for n in names]
        for j, outs in enumerate(zip(*[_unpack(a, shapes) for a in sm])):
            res[names[j]] = outs
        return (sm[0],)

    for gi, group in enumerate(RS_GROUPS):
        if gi == len(RS_GROUPS) - 1:
            last = adam_small(SMALL_EARLY, g_early, "early", last)
        for n in group:
            res[n] = _adam(recv[n], p[n][0], m[n][0], v[n][0], tr=256, name="adam_" + n, after=last)
            last = (res[n][0],)

    dm = jnp.concatenate([d_mod_lat, d_mod_ctx, jnp.zeros((6, 6 * D), F32)], 0)
    dm_all = _allgather_small_seq(dm, "gather_dmod", SMALL_COLLECTIVE_ID0 + 1)
    dm_all = lax.optimization_barrier((dm_all,) + last)[0]
    dm2 = jnp.concatenate([dm_all[:, 0, :], dm_all[:, 1, :]], 0)
    dm2_mine = lax.dynamic_slice_in_dim(dm2, me * ADA_COLS, ADA_COLS, axis=1)
    s2 = jnp.concatenate([s_act[0:8], jnp.broadcast_to(s_act[8:9], (8, D))], 0)
    g_w_ada = _matmul(s2, dm2_mine, mode="tn", name="dw_ada", tm=512, tn=ADA_COLS, after=last)
    dsc_part = _matmul(dm2_mine[8:16], p["w_ada"][0], mode="nt", name="d_silu_cctx", tm=8, tn=512, after=last)

    def cctx_b(rv, vv):
        _, pull = jax.vjp(_silu, vv[0])
        return [], [pull(jnp.sum(rv[0], axis=0, keepdims=True))[0]]

    g_cctx, = _rowwise(cctx_b, [(dsc_part, D, 0, 0)], [cc], [], [(1, D)], nrows=8, tr=8, name="cctx_bwd")
    gs["c_ctx"] = g_cctx
    gs["b_ada"] = d_mod_lat + d_mod_ctx

    res["w_ada"] = _adam(g_w_ada[None], p["w_ada"][0], m["w_ada"][0], v["w_ada"][0], tr=256, name="adam_w_ada")

    g_late = _allgather_small_seq(_pack([gs[n] for n in SMALL_LATE]), "gather_small_late", SMALL_COLLECTIVE_ID0 + 2)
    adam_small(SMALL_LATE, g_late, "late", last)

    outs = [total["loss"], grad_x[None]]
    for j in range(4):
        outs += [res[n][j].reshape(p[n].shape) for n in WEIGHTS]
    return tuple(outs)


def kernel(x, c, ctx, c_ctx, w_ada, b_ada, w_in, attn_sink, ssm_a_re, ssm_a_im, ssm_log_dt, ssm_b_re, ssm_b_im, ssm_c_re, ssm_c_im, ssm_d, w_glu, w_attn_up, w_ssm_up, w_out, ln_mix_g, ln_mix_b, w_mlp1, b_mlp1, w_mlp2, b_mlp2, ln_mlp_g, ln_mlp_b, loss_target, m_c_ctx, m_w_ada, m_b_ada, m_w_in, m_attn_sink, m_ssm_a_re, m_ssm_a_im, m_ssm_log_dt, m_ssm_b_re, m_ssm_b_im, m_ssm_c_re, m_ssm_c_im, m_ssm_d, m_w_glu, m_w_attn_up, m_w_ssm_up, m_w_out, m_ln_mix_g, m_ln_mix_b, m_w_mlp1, m_b_mlp1, m_w_mlp2, m_b_mlp2, m_ln_mlp_g, m_ln_mlp_b, v_c_ctx, v_w_ada, v_b_ada, v_w_in, v_attn_sink, v_ssm_a_re, v_ssm_a_im, v_ssm_log_dt, v_ssm_b_re, v_ssm_b_im, v_ssm_c_re, v_ssm_c_im, v_ssm_d, v_w_glu, v_w_attn_up, v_w_ssm_up, v_w_out, v_ln_mix_g, v_ln_mix_b, v_w_mlp1, v_b_mlp1, v_w_mlp2, v_b_mlp2, v_ln_mlp_g, v_ln_mlp_b):
    given = dict(locals())
    p = {n: given[n] for n in WEIGHTS}
    m = {n: given["m_" + n] for n in WEIGHTS}
    v = {n: given["v_" + n] for n in WEIGHTS}
    return _step(x, c, ctx, loss_target, p, m, v)
```

```python
import functools
import math

import jax
import jax.numpy as jnp
from jax import lax
from jax.experimental import pallas as pl
from jax.experimental.pallas import tpu as pltpu
from jax.experimental.pallas import tpu_sc as plsc

F32 = jnp.float32
BF16 = jnp.bfloat16

N_DEV = 8
D = 2048
T = 2048
C = 256
TA = T + C
GRID_W = 64
HD = 128
NH = 8
NKV = 2
GROUP = NH // NKV
WINDOW = 128
QW = NH * HD
KVW = NKV * HD
SW = D // 4
SG = 16
NG = SW // SG
SP = 64
DFF = 4 * D
IN_COLS = QW + 2 * KVW + SW + 2 * D
ALPHA = 2.0 ** 0.25
LN_EPS = 1e-6
NEG_INF = -1e30
ROPE_BASE = 10000.0
ATT_SCALE = HD ** -0.5

NSEG = 8
GBLK = 8
NBLK = NG // GBLK
BW = GBLK * SP
UW = GBLK * SG

ADAM_LR = 0.001
ADAM_B1 = 0.9
ADAM_B2 = 0.999
ADAM_EPS = 1e-08
ADAM_WD = 0.01
ADAM_STEP = 10

VMEM_LIMIT_BYTES = 56 * 1024 * 1024
MESH = pl.DeviceIdType.MESH


def _cparams(sem=None):
    return pltpu.CompilerParams(dimension_semantics=sem, vmem_limit_bytes=VMEM_LIMIT_BYTES)


def _matmul(a, b, *, mode, name, out_dtypes=(F32,), tm=512, tn=512, tk=None, bias=None, extras=(), epilogue=None, after=(),
            out_t=None):
    if mode == "nn":
        (M, K), (K2, N) = a.shape, b.shape
    elif mode == "nt":
        (M, K), (N, K2) = a.shape, b.shape
    else:
        (K, M), (K2, N) = a.shape, b.shape
    assert K == K2, (name, a.shape, b.shape)
    tm, tn, tk = min(tm, M), min(tn, N), min(tk or K, K)
    assert M % tm == 0 and N % tn == 0 and K % tk == 0, (name, M, N, K, tm, tn, tk)
    nk = K // tk
    if mode == "tn":
        a_spec = pl.BlockSpec((tk, tm), lambda i, j, k: (k, i))
    else:
        a_spec = pl.BlockSpec((tm, tk), lambda i, j, k: (i, k))
    if mode == "nt":
        b_spec = pl.BlockSpec((tn, tk), lambda i, j, k: (j, k))
    else:
        b_spec = pl.BlockSpec((tk, tn), lambda i, j, k: (k, j))
    dims = {"nn": (((1,), (0,)), ((), ())), "nt": (((1,), (1,)), ((), ())), "tn": (((0,), (0,)), ((), ()))}[mode]
    in_specs = [a_spec, b_spec]
    operands = [a, b]
    if bias is not None:
        in_specs.append(pl.BlockSpec((1, tn), lambda i, j, k: (0, j)))
        operands.append(bias)
    for e in extras:
        in_specs.append(pl.BlockSpec((tm, tn), lambda i, j, k: (i, j)))
        operands.append(e)
    n_ex = len(extras)
    for t in after:
        in_specs.append(pl.BlockSpec(memory_space=pl.ANY))
        operands.append(t)
    n_after = len(after)
    n_out = len(out_dtypes)
    out_t = tuple(out_t) if out_t is not None else (False,) * n_out
    has_bias = bias is not None

    def kern(*refs):
        a_ref, b_ref = refs[0], refs[1]
        pos = 2
        bias_ref = None
        if has_bias:
            bias_ref = refs[pos]
            pos += 1
        ex_refs = refs[pos:pos + n_ex]
        pos += n_ex + n_after
        out_refs = refs[pos:pos + n_out]
        acc_ref = refs[pos + n_out] if nk > 1 else None

        def finish(r):
            if has_bias:
                r = r + bias_ref[...]
            outs = epilogue(r, *[e[...] for e in ex_refs]) if epilogue is not None else (r,)
            for o_ref, o, tr_ in zip(out_refs, outs, out_t):
                o_ref[...] = (o.T if tr_ else o).astype(o_ref.dtype)

        part = lax.dot_general(a_ref[...].astype(BF16), b_ref[...].astype(BF16), dims, preferred_element_type=F32)
        if nk == 1:
            finish(part)
        else:
            k = pl.program_id(2)

            @pl.when(k == 0)
            def _():
                acc_ref[...] = part

            @pl.when(k > 0)
            def _():
                acc_ref[...] += part

            @pl.when(k == nk - 1)
            def _():
                finish(acc_ref[...])

    outs = pl.pallas_call(
        kern,
        name=name,
        grid=(M // tm, N // tn, nk),
        in_specs=in_specs,
        out_specs=[pl.BlockSpec((tn, tm), lambda i, j, k: (j, i)) if tr_ else pl.BlockSpec((tm, tn), lambda i, j, k: (i, j))
                   for tr_ in out_t],
        out_shape=[jax.ShapeDtypeStruct((N, M) if tr_ else (M, N), dt) for dt, tr_ in zip(out_dtypes, out_t)],
        scratch_shapes=[pltpu.VMEM((tm, tn), F32)] if nk > 1 else [],
        compiler_params=_cparams(("parallel", "parallel", "arbitrary")),
    )(*operands)
    return outs[0] if n_out == 1 else tuple(outs)


def _rowwise(fn, rows, vecs, outs, vec_outs, *, nrows, tr, name, after=()):
    n_rows, n_vecs, n_outs, n_after = len(rows), len(vecs), len(outs), len(after)
    in_specs = [pl.BlockSpec((tr, w), lambda i, cb=cb, ro=ro: (i + ro, cb)) for (_, w, cb, ro) in rows]
    in_specs += [pl.BlockSpec(v.shape, lambda i: (0, 0)) for v in vecs]
    in_specs += [pl.BlockSpec(memory_space=pl.ANY)] * n_after
    outs = [o if len(o) == 3 else (*o, False) for o in outs]
    out_specs = [pl.BlockSpec((w, tr), lambda i: (0, i)) if tr_ else pl.BlockSpec((tr, w), lambda i: (i, 0)) for (w, _, tr_) in outs]
    out_specs += [pl.BlockSpec(s, lambda i: (0, 0)) for s in vec_outs]
    out_shape = [jax.ShapeDtypeStruct((w, nrows) if tr_ else (nrows, w), dt) for (w, dt, tr_) in outs]
    out_tr = [tr_ for (_, _, tr_) in outs]
    out_shape += [jax.ShapeDtypeStruct(s, F32) for s in vec_outs]

    def kern(*refs):
        rvals = [r[...] for r in refs[:n_rows]]
        vvals = [r[...] for r in refs[n_rows:n_rows + n_vecs]]
        first_out = n_rows + n_vecs + n_after
        o_refs = refs[first_out:first_out + n_outs]
        v_refs = refs[first_out + n_outs:]
        ro, vo = fn(rvals, vvals)
        for r, val, tr_ in zip(o_refs, ro, out_tr):
            r[...] = (val.astype(F32).T if tr_ else val).astype(r.dtype)
        i = pl.program_id(0)
        for r, val in zip(v_refs, vo):
            @pl.when(i == 0)
            def _(r=r, val=val):
                r[...] = val.astype(F32)

            @pl.when(i > 0)
            def _(r=r, val=val):
                r[...] += val.astype(F32)

    res = pl.pallas_call(
        kern,
        name=name,
        grid=(nrows // tr,),
        in_specs=in_specs,
        out_specs=out_specs,
        out_shape=out_shape,
        compiler_params=_cparams(("arbitrary",)),
    )(*[r[0] for r in rows], *vecs, *after)
    return list(res)


def _ln(x):
    mu = jnp.mean(x, axis=-1, keepdims=True)
    xc = x - mu
    var = jnp.mean(xc * xc, axis=-1, keepdims=True)
    return xc * lax.rsqrt(var + LN_EPS)


def _sigmoid(x):
    return 1.0 / (1.0 + jnp.exp(-x))


def _gelu(x):
    return 0.5 * x * (1.0 + jnp.tanh(math.sqrt(2.0 / math.pi) * (x + 0.044715 * (x * x * x))))


def _silu(x):
    return x * _sigmoid(x)


def _f_ln_mod(x, sc, sh):
    return _ln(x) * (1.0 + sc) + sh


def _f_glu(z):
    return z[:, :SW] * _sigmoid(z[:, SW:])


def _f_mix(ga, gs, attn_d, ssm_d):
    return _sigmoid(ga) * attn_d + _sigmoid(gs) * ssm_d


def _f_post1(x, y, g1, lg, lb, sc2, sh2):
    r1 = ALPHA * x + g1 * y
    x1 = _ln(r1) * lg + lb
    h2 = _ln(x1) * (1.0 + sc2) + sh2
    return x1, h2


def _f_loss(x1, mlp, tgt, g2, lg, lb, b2z):
    r2 = ALPHA * x1 + g2 * (mlp + b2z)
    out = _ln(r2) * lg + lb
    err = out - tgt
    return 0.5 * jnp.sum(err * err) * (1.0 / D)


def _rope_tables():
    rows = T // GRID_W
    row = jnp.repeat(jnp.arange(rows), GRID_W)
    col = jnp.tile(jnp.arange(GRID_W), rows)
    n_freq = HD // 4
    freqs = ROPE_BASE ** (-jnp.arange(n_freq, dtype=F32) / n_freq)
    ang_r = row.astype(F32)[:, None] * freqs
    ang_c = col.astype(F32)[:, None] * freqs
    ang = jnp.concatenate([ang_r, ang_r, ang_c, ang_c], -1)
    cos, sin = jnp.cos(ang), jnp.sin(ang)
    lo = (jnp.arange(HD) % (HD // 2)) < (HD // 4)
    sin_a = jnp.where(lo[None, :], -sin, 0.0)
    sin_b = jnp.where(lo[None, :], 0.0, sin)
    return cos, sin_a, sin_b


def _rope(x, cos, sa, sb):
    return x * cos + pltpu.roll(x, 96, 1) * sa + pltpu.roll(x, 32, 1) * sb


def _rope_t(dy, cos, sa, sb):
    return dy * cos + pltpu.roll(dy * sa, 32, 1) + pltpu.roll(dy * sb, 96, 1)


BAND = 3 * WINDOW
KPAD = T + 2 * WINDOW


def _attn_fill_kv(k_ref, v_ref, cos_ref, sa_ref, sb_ref, kp, vp, kc, vc):
    zeros = jnp.zeros((WINDOW, KVW), BF16)
    kp[0:WINDOW, :] = zeros
    kp[WINDOW + T:KPAD, :] = zeros
    vp[0:WINDOW, :] = zeros
    vp[WINDOW + T:KPAD, :] = zeros
    for hh in range(NKV):
        cs = slice(hh * HD, (hh + 1) * HD)
        for r0 in range(0, T, 512):
            rs = slice(r0, r0 + 512)
            kr = _rope(k_ref[rs, cs], cos_ref[rs, :], sa_ref[rs, :], sb_ref[rs, :])
            kp[WINDOW + r0:WINDOW + r0 + 512, cs] = kr.astype(BF16)
    vp[WINDOW:WINDOW + T, :] = v_ref[0:T, :].astype(BF16)
    kc[...] = k_ref[T:TA, :].astype(BF16)
    vc[...] = v_ref[T:TA, :].astype(BF16)


GROWS = GROUP * WINDOW


def _attn_scores(n, kvh, q_ref, cos_ref, sa_ref, sb_ref, sink_ref, kp, kc):
    r0 = pl.multiple_of(n * WINDOW, WINDOW)
    cos = cos_ref[pl.ds(r0, WINDOW), :]
    sa = sa_ref[pl.ds(r0, WINDOW), :]
    sb = sb_ref[pl.ds(r0, WINDOW), :]
    heads = range(kvh * GROUP, (kvh + 1) * GROUP)
    q_g = jnp.concatenate([_rope(q_ref[:, h * HD:(h + 1) * HD], cos, sa, sb).astype(BF16) for h in heads], axis=0)
    kb = kp[pl.ds(r0, BAND), kvh * HD:(kvh + 1) * HD]
    kcb = kc[:, kvh * HD:(kvh + 1) * HD]
    nt = (((1,), (1,)), ((), ()))
    s_loc = lax.dot_general(q_g, kb, nt, preferred_element_type=F32) * ATT_SCALE
    s_ctx = lax.dot_general(q_g, kcb, nt, preferred_element_type=F32) * ATT_SCALE
    row = lax.broadcasted_iota(jnp.int32, (GROWS, BAND), 0) & (WINDOW - 1)
    col = lax.broadcasted_iota(jnp.int32, (GROWS, BAND), 1)
    rel = col - WINDOW - row
    kpos = r0 - WINDOW + col
    valid = (jnp.abs(rel) <= WINDOW) & (kpos >= 0) & (kpos < T)
    s_loc = jnp.where(valid, s_loc, NEG_INF)
    sk = jnp.concatenate([jnp.broadcast_to(sink_ref[0:1, h:h + 1], (WINDOW, 1)) for h in heads], axis=0)
    m = jnp.maximum(jnp.maximum(jnp.max(s_loc, -1, keepdims=True), jnp.max(s_ctx, -1, keepdims=True)), sk)
    e_loc = jnp.exp(s_loc - m)
    e_ctx = jnp.exp(s_ctx - m)
    e_sink = jnp.exp(sk - m)
    inv = 1.0 / (jnp.sum(e_loc, -1, keepdims=True) + jnp.sum(e_ctx, -1, keepdims=True) + e_sink)
    return q_g, r0, e_loc * inv, e_ctx * inv, e_sink * inv


def _attn_fwd(proj, sink, tabs):
    cos, sa, sb = tabs

    def kern(q_ref, k_ref, v_ref, cos_ref, sa_ref, sb_ref, sink_ref, o_ref, kp, vp, kc, vc):
        n = pl.program_id(0)

        @pl.when(n == 0)
        def _():
            _attn_fill_kv(k_ref, v_ref, cos_ref, sa_ref, sb_ref, kp, vp, kc, vc)

        for kvh in range(NKV):
            _, r0, p_loc, p_ctx, _ = _attn_scores(n, kvh, q_ref, cos_ref, sa_ref, sb_ref, sink_ref, kp, kc)
            vb = vp[pl.ds(r0, BAND), kvh * HD:(kvh + 1) * HD]
            vcb = vc[:, kvh * HD:(kvh + 1) * HD]
            o = jnp.dot(p_loc.astype(BF16), vb, preferred_element_type=F32)
            o = o + jnp.dot(p_ctx.astype(BF16), vcb, preferred_element_type=F32)
            for g in range(GROUP):
                h = kvh * GROUP + g
                o_ref[:, h * HD:(h + 1) * HD] = o[g * WINDOW:(g + 1) * WINDOW, :].astype(o_ref.dtype)

    full = lambda shape: pl.BlockSpec(shape, lambda n: (0, 0))
    return pl.pallas_call(
        kern,
        name="attn_fwd",
        grid=(T // WINDOW,),
        in_specs=[
            pl.BlockSpec((WINDOW, QW), lambda n: (n, 0)),
            pl.BlockSpec((TA, KVW), lambda n: (0, QW // KVW)),
            pl.BlockSpec((TA, KVW), lambda n: (0, QW // KVW + 1)),
            full((T, HD)), full((T, HD)), full((T, HD)), full((1, NH)),
        ],
        out_specs=pl.BlockSpec((WINDOW, QW), lambda n: (n, 0)),
        out_shape=jax.ShapeDtypeStruct((T, QW), BF16),
        scratch_shapes=[pltpu.VMEM((KPAD, KVW), BF16), pltpu.VMEM((KPAD, KVW), BF16),
                        pltpu.VMEM((C, KVW), BF16), pltpu.VMEM((C, KVW), BF16)],
        compiler_params=_cparams(("arbitrary",)),
    )(proj, proj, proj, cos, sa, sb, sink)


def _attn_bwd(proj, d_attn, sink, tabs):
    cos, sa, sb = tabs
    n_blocks = T // WINDOW

    def kern(q_ref, k_ref, v_ref, do_ref, cos_ref, sa_ref, sb_ref, sink_ref,
             dq_ref, dk_ref, dv_ref, dsink_ref, kp, vp, kc, vc, dkp, dvp, dkc, dvc):
        n = pl.program_id(0)

        @pl.when(n == 0)
        def _():
            _attn_fill_kv(k_ref, v_ref, cos_ref, sa_ref, sb_ref, kp, vp, kc, vc)
            dkp[...] = jnp.zeros_like(dkp)
            dvp[...] = jnp.zeros_like(dvp)
            dkc[...] = jnp.zeros_like(dkc)
            dvc[...] = jnp.zeros_like(dvc)
            dsink_ref[...] = jnp.zeros_like(dsink_ref)

        nt = (((1,), (1,)), ((), ()))
        tn = (((0,), (0,)), ((), ()))
        for kvh in range(NKV):
            cs = slice(kvh * HD, (kvh + 1) * HD)
            heads = range(kvh * GROUP, (kvh + 1) * GROUP)
            q_g, r0, p_loc, p_ctx, p_sink = _attn_scores(n, kvh, q_ref, cos_ref, sa_ref, sb_ref, sink_ref, kp, kc)
            kb = kp[pl.ds(r0, BAND), cs]
            vb = vp[pl.ds(r0, BAND), cs]
            kcb = kc[:, cs]
            vcb = vc[:, cs]
            do_g = jnp.concatenate([do_ref[:, h * HD:(h + 1) * HD] for h in heads], axis=0)
            dp_loc = lax.dot_general(do_g, vb, nt, preferred_element_type=F32)
            dp_ctx = lax.dot_general(do_g, vcb, nt, preferred_element_type=F32)
            delta = jnp.sum(p_loc * dp_loc, -1, keepdims=True) + jnp.sum(p_ctx * dp_ctx, -1, keepdims=True)
            ds_loc = (p_loc * (dp_loc - delta) * ATT_SCALE).astype(BF16)
            ds_ctx = (p_ctx * (dp_ctx - delta) * ATT_SCALE).astype(BF16)
            dq = jnp.dot(ds_loc, kb, preferred_element_type=F32) + jnp.dot(ds_ctx, kcb, preferred_element_type=F32)
            cos = cos_ref[pl.ds(r0, WINDOW), :]
            sa_ = sa_ref[pl.ds(r0, WINDOW), :]
            sb_ = sb_ref[pl.ds(r0, WINDOW), :]
            dkp[pl.ds(r0, BAND), cs] += lax.dot_general(ds_loc, q_g, tn, preferred_element_type=F32)
            dkc[:, cs] += lax.dot_general(ds_ctx, q_g, tn, preferred_element_type=F32)
            dvp[pl.ds(r0, BAND), cs] += lax.dot_general(p_loc.astype(BF16), do_g, tn, preferred_element_type=F32)
            dvc[:, cs] += lax.dot_general(p_ctx.astype(BF16), do_g, tn, preferred_element_type=F32)
            dsk_rows = p_sink * delta
            for g, h in enumerate(heads):
                rs = slice(g * WINDOW, (g + 1) * WINDOW)
                dq_ref[:, h * HD:(h + 1) * HD] = _rope_t(dq[rs, :], cos, sa_, sb_).astype(dq_ref.dtype)
                dsk = -jnp.sum(dsk_rows[rs, :], axis=0, keepdims=True)
                dsink_ref[h:h + 1, :] += jnp.broadcast_to(dsk, (1, HD))

        @pl.when(n == n_blocks - 1)
        def _():
            for hh in range(NKV):
                cs = slice(hh * HD, (hh + 1) * HD)
                for r0 in range(0, T, 512):
                    rs = slice(r0, r0 + 512)
                    g = dkp[WINDOW + r0:WINDOW + r0 + 512, cs]
                    dk_ref[rs, cs] = _rope_t(g, cos_ref[rs, :], sa_ref[rs, :], sb_ref[rs, :]).astype(dk_ref.dtype)
            dk_ref[T:TA, :] = dkc[...].astype(dk_ref.dtype)
            dv_ref[0:T, :] = dvp[WINDOW:WINDOW + T, :].astype(dv_ref.dtype)
            dv_ref[T:TA, :] = dvc[...].astype(dv_ref.dtype)

    full = lambda shape: pl.BlockSpec(shape, lambda n: (0, 0))
    return pl.pallas_call(
        kern,
        name="attn_bwd",
        grid=(n_blocks,),
        in_specs=[
            pl.BlockSpec((WINDOW, QW), lambda n: (n, 0)),
            pl.BlockSpec((TA, KVW), lambda n: (0, QW // KVW)),
            pl.BlockSpec((TA, KVW), lambda n: (0, QW // KVW + 1)),
            pl.BlockSpec((WINDOW, QW), lambda n: (n, 0)),
            full((T, HD)), full((T, HD)), full((T, HD)), full((1, NH)),
        ],
        out_specs=[pl.BlockSpec((WINDOW, QW), lambda n: (n, 0)), full((TA, KVW)), full((TA, KVW)), full((NH, HD))],
        out_shape=[jax.ShapeDtypeStruct((T, QW), BF16), jax.ShapeDtypeStruct((TA, KVW), BF16),
                   jax.ShapeDtypeStruct((TA, KVW), BF16), jax.ShapeDtypeStruct((NH, HD), F32)],
        scratch_shapes=[pltpu.VMEM((KPAD, KVW), BF16), pltpu.VMEM((KPAD, KVW), BF16),
                        pltpu.VMEM((C, KVW), BF16), pltpu.VMEM((C, KVW), BF16),
                        pltpu.VMEM((KPAD, KVW), F32), pltpu.VMEM((KPAD, KVW), F32),
                        pltpu.VMEM((C, KVW), F32), pltpu.VMEM((C, KVW), F32)],
        compiler_params=_cparams(("arbitrary",)),
    )(proj, proj, proj, d_attn, cos, sa, sb, sink)


def _s5_prep(a_re, a_im, log_dt, b_re, b_im, c_re, c_im):
    lam = lax.complex(a_re, a_im)
    dt = jnp.exp(log_dt)[..., None]
    lam_bar = jnp.exp(lam * dt)
    b_bar = ((lam_bar - 1.0) / lam)[..., None] * lax.complex(b_re, b_im)
    eye = jnp.eye(GBLK, dtype=F32)

    def lam_rows(v):
        return v.reshape(2, NBLK, 1, BW)

    lam_l = jnp.concatenate([lam_rows(jnp.real(lam_bar)), lam_rows(jnp.imag(lam_bar))], -1)
    lam_l = jnp.broadcast_to(lam_l, (2, NBLK, 8, 2 * BW))

    def b_blocks(v):
        v = v.reshape(2, NBLK, GBLK, SP, SG).transpose(0, 1, 2, 4, 3)
        return (v[:, :, :, :, None, :] * eye[None, None, :, None, :, None]).reshape(2, NBLK, UW, BW)

    bmat = jnp.concatenate([b_blocks(jnp.real(b_bar)), b_blocks(jnp.imag(b_bar))], -1)

    def c_blocks(v):
        v = v.reshape(2, NBLK, GBLK, SG, SP).transpose(0, 1, 2, 4, 3)
        return (v[:, :, :, :, None, :] * eye[None, None, :, None, :, None]).reshape(2, NBLK, BW, UW)

    cmat = jnp.concatenate([c_blocks(c_re), -c_blocks(c_im)], 2)
    return lam_l, bmat, cmat


def _cmul(ar, ai, br, bi):
    return ar * br - ai * bi, ar * bi + ai * br


def _shift_rows(x, rev, fill):
    r = lax.broadcasted_iota(jnp.int32, x.shape, 0)
    down = jnp.where(r == 0, fill, pltpu.roll(x, 1, 0))
    up = jnp.where(r == NSEG - 1, fill, pltpu.roll(x, NSEG - 1, 0))
    return jnp.where(rev == 0, down, up)


def _edge_row(x, rev):
    last = jnp.broadcast_to(x[NSEG - 1:NSEG, :], x.shape)
    first = jnp.broadcast_to(x[0:1, :], x.shape)
    return jnp.where(rev == 0, last, first)


def _seg_scan(get, put, base, seglen, lr, li, rev, cin):
    zero = jnp.zeros((NSEG, BW), F32)

    def rows(k):
        j = jnp.where(rev == 0, k, seglen - 1 - k)
        return pl.ds(pl.multiple_of(base + j * NSEG, NSEG), NSEG)

    def local(k, carry):
        sr, si, pr, pi = carry
        xr, xi = get(rows(k))
        tr, ti = _cmul(lr, li, sr, si)
        sr, si = tr + xr, ti + xi
        put(rows(k), sr, si)
        pr, pi = _cmul(lr, li, pr, pi)
        return sr, si, pr, pi

    er, ei, lpr, lpi = lax.fori_loop(0, seglen, local, (zero, zero, zero + 1.0, zero))
    cr, ci = _shift_rows(zero, rev, cin[0]), _shift_rows(zero, rev, cin[1])
    for _ in range(NSEG - 1):
        tr, ti = _cmul(lpr, lpi, cr, ci)
        cr, ci = _shift_rows(er + tr, rev, cin[0]), _shift_rows(ei + ti, rev, cin[1])

    def fix(k, carry):
        pr, pi = carry
        xr, xi = get(rows(k))
        tr, ti = _cmul(pr, pi, cr, ci)
        put(rows(k), xr + tr, xi + ti)
        return _cmul(lr, li, pr, pi)

    lax.fori_loop(0, seglen, fix, (lr, li))
    tr, ti = _cmul(lpr, lpi, cr, ci)
    return _edge_row(er + tr, rev), _edge_row(ei + ti, rev)


RCH = 256
CSEG = C // NSEG
TSEG = T // NSEG
UCOL0 = (QW + 2 * KVW) // UW


REGIONS = ((0, TSEG), (T, CSEG))


def _state_access(ref, lead=()):
    def get(rows):
        return ref[(*lead, rows, slice(0, BW))], ref[(*lead, rows, slice(BW, 2 * BW))]

    def put(rows, re, im):
        ref[(*lead, rows, slice(0, BW))] = re
        ref[(*lead, rows, slice(BW, 2 * BW))] = im

    return get, put


def _interleave_rows(src_ref, dst_ref, regions=REGIONS):
    for base, seglen in regions:
        def body(j, carry, base=base, seglen=seglen):
            dst_ref[pl.ds(pl.multiple_of(base + j * NSEG, NSEG), NSEG), :] = src_ref[pl.ds(base + j, NSEG, stride=seglen), :]
            return carry

        lax.fori_loop(0, seglen, body, 0, unroll=8)


def _deinterleave_rows(src_ref, dst_ref, regions=REGIONS):
    for base, seglen in regions:
        def body(j, carry, base=base, seglen=seglen):
            dst_ref[pl.ds(base + j, NSEG, stride=seglen), :] = src_ref[pl.ds(pl.multiple_of(base + j * NSEG, NSEG), NSEG), :]
            return carry

        lax.fori_loop(0, seglen, body, 0, unroll=8)


def _s5_fwd(proj, dskip, lam, bmat, cmat):
    def kern(u_ref, dk_ref, lam_ref, b_ref, c_ref, s_ref, ssm_ref, ge_ref, up_ref, yp_ref):
        d = pl.program_id(1)

        @pl.when(d == 0)
        def _():
            _interleave_rows(u_ref, up_ref)

        bm = b_ref[0, 0].astype(BF16)
        for r0 in range(0, TA, RCH):
            s_ref[0, 0, r0:r0 + RCH, :] = jnp.dot(up_ref[r0:r0 + RCH, :].astype(BF16), bm, preferred_element_type=F32)
        lr = lam_ref[0, 0, :, 0:BW]
        li = lam_ref[0, 0, :, BW:2 * BW]
        zero = jnp.zeros((NSEG, BW), F32)
        get, put = _state_access(s_ref, (0, 0))
        mid = _seg_scan(get, put, T, CSEG, lr, li, d, (zero, zero))
        _seg_scan(get, put, 0, TSEG, lr, li, d, mid)
        cm = c_ref[0, 0].astype(BF16)
        for r0 in range(0, T, RCH):
            y = jnp.dot(s_ref[0, 0, r0:r0 + RCH, :].astype(BF16), cm, preferred_element_type=F32)

            @pl.when(d == 0)
            def _(y=y, r0=r0):
                yp_ref[r0:r0 + RCH, :] = y + dk_ref[...] * up_ref[r0:r0 + RCH, :]

            @pl.when(d == 1)
            def _(y=y, r0=r0):
                yp_ref[r0:r0 + RCH, :] += y

        @pl.when(d == 1)
        def _():
            _deinterleave_rows(yp_ref, ssm_ref, REGIONS[:1])
            for r0 in range(0, T, RCH):
                ge_ref[r0:r0 + RCH, :] = _gelu(ssm_ref[r0:r0 + RCH, :]).astype(ge_ref.dtype)

    blk4 = lambda shape: pl.BlockSpec((1, 1) + shape, lambda b, d: (d, b, 0, 0))
    return pl.pallas_call(
        kern,
        name="s5_fwd",
        grid=(NBLK, 2),
        in_specs=[pl.BlockSpec((TA, UW), lambda b, d: (0, UCOL0 + b)), pl.BlockSpec((1, UW), lambda b, d: (0, b)),
                  blk4((8, 2 * BW)), blk4((UW, 2 * BW)), blk4((2 * BW, UW))],
        out_specs=[blk4((TA, 2 * BW)), pl.BlockSpec((T, UW), lambda b, d: (0, b)), pl.BlockSpec((T, UW), lambda b, d: (0, b))],
        out_shape=[jax.ShapeDtypeStruct((2, NBLK, TA, 2 * BW), F32), jax.ShapeDtypeStruct((T, SW), F32),
                   jax.ShapeDtypeStruct((T, SW), BF16)],
        scratch_shapes=[pltpu.VMEM((TA, UW), F32), pltpu.VMEM((T, UW), F32)],
        compiler_params=_cparams(("parallel", "arbitrary")),
    )(proj, dskip, lam, bmat, cmat)


def _s5_bwd(d_ge, ssm, proj, dskip, states, lam, bmat, cmat):
    nt = (((1,), (1,)), ((), ()))
    tn = (((0,), (0,)), ((), ()))

    def kern(dge_ref, ssm_ref, u_ref, dk_ref, s_ref, lam_ref, b_ref, c_ref,
             du_ref, ddk_ref, dlam_ref, db_ref, dc_ref, g_ref, dua_ref, dssm_ref, up_ref, nat_ref):
        d = pl.program_id(1)

        @pl.when(d == 0)
        def _():
            ddk = jnp.zeros((1, UW), F32)
            for r0 in range(0, T, RCH):
                rs = slice(r0, r0 + RCH)
                _, pull = jax.vjp(_gelu, ssm_ref[rs, :])
                dssm = pull(dge_ref[rs, :])[0]
                nat_ref[rs, :] = dssm
                ddk = ddk + jnp.sum(dssm * u_ref[rs, :], axis=0, keepdims=True)
            ddk_ref[...] = ddk
            _interleave_rows(nat_ref, dssm_ref, REGIONS[:1])
            _interleave_rows(u_ref, up_ref)
            for r0 in range(0, T, RCH):
                dua_ref[r0:r0 + RCH, :] = dssm_ref[r0:r0 + RCH, :] * dk_ref[...]
            dua_ref[T:TA, :] = jnp.zeros((C, UW), F32)

        cm = c_ref[0, 0].astype(BF16)
        for r0 in range(0, T, RCH):
            g_ref[r0:r0 + RCH, :] = lax.dot_general(dssm_ref[r0:r0 + RCH, :].astype(BF16), cm, nt, preferred_element_type=F32)
        g_ref[T:TA, :] = jnp.zeros((C, 2 * BW), F32)
        lr = lam_ref[0, 0, :, 0:BW]
        li = lam_ref[0, 0, :, BW:2 * BW]
        zero = jnp.zeros((NSEG, BW), F32)
        get_g, put_g = _state_access(g_ref)

        mid = _seg_scan(get_g, put_g, 0, TSEG, lr, -li, 1 - d, (zero, zero))
        _seg_scan(get_g, put_g, T, CSEG, lr, -li, 1 - d, mid)

        get_s, _ = _state_access(s_ref, (0, 0))

        def dlam_terms(g, s):
            return g[0] * s[0] + g[1] * s[1], g[1] * s[0] - g[0] * s[1]

        def dlam_region(base, seglen, s_in, acc):
            def rows(j):
                return pl.ds(pl.multiple_of(base + j * NSEG, NSEG), NSEG)

            def inner(k, acc):
                j = jnp.where(d == 0, k, seglen - 1 - k)
                jp = jnp.where(d == 0, k - 1, seglen - k)
                t = dlam_terms(get_g(rows(j)), get_s(rows(jp)))
                return acc[0] + t[0], acc[1] + t[1]

            acc = lax.fori_loop(1, seglen, inner, acc)
            jb = jnp.where(d == 0, 0, seglen - 1)
            jn = jnp.where(d == 0, seglen - 1, 0)
            sp = get_s(rows(jn))
            t = dlam_terms(get_g(rows(jb)), (_shift_rows(sp[0], d, s_in[0]), _shift_rows(sp[1], d, s_in[1])))
            return acc[0] + t[0], acc[1] + t[1]

        r_mid = jnp.where(d == 0, TA - 1, T)
        s_mid = tuple(jnp.broadcast_to(t, (NSEG, BW)) for t in get_s(pl.ds(r_mid, 1)))
        acc = dlam_region(T, CSEG, (zero, zero), (zero, zero))
        acc = dlam_region(0, TSEG, s_mid, acc)
        dlam_ref[0, 0, :, 0:BW] = acc[0]
        dlam_ref[0, 0, :, BW:2 * BW] = acc[1]

        bm = b_ref[0, 0].astype(BF16)
        db = jnp.zeros((UW, 2 * BW), F32)
        dc = jnp.zeros((2 * BW, UW), F32)
        for r0 in range(0, TA, RCH):
            rs = slice(r0, r0 + RCH)
            g = g_ref[rs, :].astype(BF16)
            dua_ref[rs, :] += lax.dot_general(g, bm, nt, preferred_element_type=F32)
            db = db + lax.dot_general(up_ref[rs, :].astype(BF16), g, tn, preferred_element_type=F32)
            if r0 < T:
                dc = dc + lax.dot_general(s_ref[0, 0, rs, :].astype(BF16), dssm_ref[rs, :].astype(BF16), tn,
                                          preferred_element_type=F32)
        db_ref[0, 0] = db
        dc_ref[0, 0] = dc

        @pl.when(d == 1)
        def _():
            _deinterleave_rows(dua_ref, nat_ref)
            du_ref[...] = nat_ref[...].astype(du_ref.dtype)

    blk4 = lambda shape: pl.BlockSpec((1, 1) + shape, lambda b, d: (d, b, 0, 0))
    lat = pl.BlockSpec((T, UW), lambda b, d: (0, b))
    vec = pl.BlockSpec((1, UW), lambda b, d: (0, b))
    return pl.pallas_call(
        kern,
        name="s5_bwd",
        grid=(NBLK, 2),
        in_specs=[lat, lat, pl.BlockSpec((TA, UW), lambda b, d: (0, UCOL0 + b)), vec,
                  blk4((TA, 2 * BW)), blk4((8, 2 * BW)), blk4((UW, 2 * BW)), blk4((2 * BW, UW))],
        out_specs=[pl.BlockSpec((TA, UW), lambda b, d: (0, b)), vec, blk4((8, 2 * BW)), blk4((UW, 2 * BW)), blk4((2 * BW, UW))],
        out_shape=[jax.ShapeDtypeStruct((TA, SW), BF16), jax.ShapeDtypeStruct((1, SW), F32),
                   jax.ShapeDtypeStruct((2, NBLK, 8, 2 * BW), F32),
                   jax.ShapeDtypeStruct((2, NBLK, UW, 2 * BW), F32), jax.ShapeDtypeStruct((2, NBLK, 2 * BW, UW), F32)],
        scratch_shapes=[pltpu.VMEM((TA, 2 * BW), F32), pltpu.VMEM((TA, UW), F32), pltpu.VMEM((T, UW), F32),
                        pltpu.VMEM((TA, UW), F32), pltpu.VMEM((TA, UW), F32)],
        compiler_params=_cparams(("parallel", "arbitrary")),
    )(d_ge, ssm, proj, dskip, states, lam, bmat, cmat)


TR = 256


def _vjp_rows(f, primals, cots, n_row):
    _, pull = jax.vjp(f, *primals)
    g = pull(cots)
    return list(g[:n_row]), list(g[n_row:])


class _GradDict(dict):
    def __init__(self, on_set=None):
        super().__init__()
        self._on_set = on_set
        self.tokens = {}

    def __setitem__(self, key, value):
        super().__setitem__(key, value)
        if self._on_set is not None:
            self._on_set(self)

    def order(self, key):
        return self.tokens.get(key, self.get(key))

    def finish(self, key, after):
        if self.on_finish is None:
            return ()
        return (self.on_finish(key, after),)

    on_finish = None


def _local_step(x, ctx, tgt, mod_lat, mod_ctx, wb, sp, on_grad=None, on_loss=None, on_finish=None, on_early=None):
    sh1, sc1, g1, sh2, sc2, g2 = [mod_lat[:, i * D:(i + 1) * D] for i in range(6)]
    csh1, csc1 = mod_ctx[:, 0:D], mod_ctx[:, D:2 * D]
    tabs = _rope_tables()
    sink = sp["attn_sink"].reshape(1, NH)
    dskip = sp["ssm_d"].reshape(1, SW)
    lg_mix, lb_mix = sp["ln_mix_g"].reshape(1, D), sp["ln_mix_b"].reshape(1, D)
    lg_mlp, lb_mlp = sp["ln_mlp_g"].reshape(1, D), sp["ln_mlp_b"].reshape(1, D)
    b1, b2 = sp["b_mlp1"].reshape(1, DFF), sp["b_mlp2"].reshape(1, D)
    s5_names = ("ssm_a_re", "ssm_a_im", "ssm_log_dt", "ssm_b_re", "ssm_b_im", "ssm_c_re", "ssm_c_im")
    (lam, bmat, cmat), s5_pull = jax.vjp(_s5_prep, *[sp[n] for n in s5_names])

    def ln_mod2(rv, vv):
        h = _f_ln_mod(rv[0], vv[0], vv[1])
        return [h, h], []

    h_lat, h_lat_t = _rowwise(ln_mod2, [(x, D, 0, 0)], [sc1, sh1], [(D, BF16), (D, BF16, True)], [], nrows=T, tr=TR, name="ln1_lat")
    h_ctx, h_ctx_t = _rowwise(ln_mod2, [(ctx, D, 0, 0)], [csc1, csh1], [(D, BF16), (D, BF16, True)], [], nrows=C, tr=TR,
                              name="ln1_ctx")
    h1 = jnp.concatenate([h_lat, h_ctx], 0)
    h1_t = jnp.concatenate([h_lat_t, h_ctx_t], 1)
    proj = _matmul(h1, wb["w_in"], mode="nn", name="proj", tm=768, tn=512)
    attn = _attn_fwd(proj, sink, tabs)
    states, ssm, ge = _s5_fwd(proj, dskip, lam, bmat, cmat)
    z = _matmul(ge, wb["w_glu"], mode="nn", name="glu_mm", tm=1024, tn=1024)

    def glu_act(rv, vv):
        return [_f_glu(rv[0])], []

    glu, = _rowwise(glu_act, [(z, 2 * SW, 0, 0)], [], [(SW, BF16)], [], nrows=T, tr=TR, name="glu_act")
    attn_d = _matmul(attn, wb["w_attn_up"], mode="nn", name="attn_up", tm=1024, tn=512)
    ssm_d = _matmul(glu, wb["w_ssm_up"], mode="nn", name="ssm_up", tm=1024, tn=512)
    ga_cb, gs_cb = (QW + 2 * KVW + SW) // D, (QW + 2 * KVW + SW) // D + 1

    def mix(rv, vv):
        m_ = _f_mix(*rv)
        return [m_, m_], []

    mixv, mix_t = _rowwise(mix, [(proj, D, ga_cb, 0), (proj, D, gs_cb, 0), (attn_d, D, 0, 0), (ssm_d, D, 0, 0)], [],
                           [(D, BF16), (D, BF16, True)], [], nrows=T, tr=TR, name="mix")
    y = _matmul(mixv, wb["w_out"], mode="nn", name="out_proj", tm=1024, tn=512)

    def post1(rv, vv):
        x1, h2 = _f_post1(rv[0], rv[1], *vv)
        return [x1, h2, h2], []

    x1, h2, h2_t = _rowwise(post1, [(x, D, 0, 0), (y, D, 0, 0)], [g1, lg_mix, lb_mix, sc2, sh2],
                            [(D, F32), (D, BF16), (D, BF16, True)], [], nrows=T, tr=TR, name="post1")

    def relu_sq(acc):
        r = jnp.maximum(acc, 0.0)
        return r, r * r, r * r

    r_act, act, act_t = _matmul(h2, wb["w_mlp1"], mode="nn", name="mlp1", tm=1024, tn=512, bias=b1,
                                out_dtypes=(BF16, BF16, BF16), out_t=(False, False, True), epilogue=relu_sq)
    mlp = _matmul(act, wb["w_mlp2"], mode="nn", name="mlp2", tm=1024, tn=512, tk=2048)

    def loss_fb(rv, vv):
        x1_t, mlp_t, tgt_t = rv
        g2_v, lg_v, lb_v, b2_v = vv
        f = lambda a, m, g, p, q, b: _f_loss(a, m, tgt_t, g, p, q, b)
        val, grads = jax.value_and_grad(f, argnums=(0, 1, 2, 3, 4, 5))(x1_t, mlp_t, g2_v, lg_v, lb_v, b2_v)
        dx1, dmlp, dg2, dlg, dlb, db2 = grads
        return [dx1, dmlp], [jnp.reshape(val, (1, 1)), dg2, dlg, dlb, db2]

    dx1_a, d_mlp, loss_p, d_g2, d_lg_mlp, d_lb_mlp, d_b2 = _rowwise(
        loss_fb, [(x1, D, 0, 0), (mlp, D, 0, 0), (tgt, D, 0, 0)], [g2, lg_mlp, lb_mlp, b2],
        [(D, F32), (D, BF16)], [(1, 1), (1, D), (1, D), (1, D), (1, D)], nrows=T, tr=TR, name="loss_fb")

    gw = _GradDict(on_grad)
    gw.on_finish = on_finish
    loss_done = () if on_loss is None else (on_loss(loss_p),)
    gw["w_mlp2"] = _matmul(act_t, d_mlp, mode="nn", name="dw_mlp2", out_dtypes=(BF16,), tm=1024, tn=512, after=loss_done)
    da, = (_matmul(d_mlp, wb["w_mlp2"], mode="nt", name="d_act", out_dtypes=(BF16,), tm=1024, tn=512,
                   extras=(r_act,), epilogue=lambda acc, r: (acc * (2.0 * r.astype(F32)),), after=(gw.order("w_mlp2"),)),)
    pin = gw.finish("w_mlp2", da)
    ones = jnp.ones((8, T), BF16)
    d_b1 = _matmul(ones, da, mode="nn", name="db_mlp1", tm=8, tn=2048)[0:1]
    gw["w_mlp1"] = _matmul(h2_t, da, mode="nn", name="dw_mlp1", out_dtypes=(BF16,), tm=1024, tn=512, after=pin)
    dh2 = _matmul(da, wb["w_mlp1"], mode="nt", name="d_h2", tm=1024, tn=512, tk=2048, after=(gw.order("w_mlp1"),))

    def post1_b(rv, vv):
        x_t, y_t, dx1_t, dh2_t = rv
        gr, gv = _vjp_rows(_f_post1, (x_t, y_t, *vv), (dx1_t, dh2_t), 2)
        return [gr[0], gr[1]], gv

    dx_a, dy, d_g1, d_lg_mix, d_lb_mix, d_sc2, d_sh2 = _rowwise(
        post1_b, [(x, D, 0, 0), (y, D, 0, 0), (dx1_a, D, 0, 0), (dh2, D, 0, 0)], [g1, lg_mix, lb_mix, sc2, sh2],
        [(D, F32), (D, BF16)], [(1, D)] * 5, nrows=T, tr=TR, name="post1_bwd")
    gw["w_out"] = _matmul(mix_t, dy, mode="nn", name="dw_out", out_dtypes=(BF16,), tm=1024, tn=512)
    dmix = _matmul(dy, wb["w_out"], mode="nt", name="d_mix", tm=1024, tn=512, after=(gw.order("w_out"),))

    def mix_b(rv, vv):
        gr, _ = _vjp_rows(_f_mix, tuple(rv[:4]), rv[4], 4)
        return gr, []

    d_ga, d_gs, d_attn_d, d_ssm_d = _rowwise(
        mix_b, [(proj, D, ga_cb, 0), (proj, D, gs_cb, 0), (attn_d, D, 0, 0), (ssm_d, D, 0, 0), (dmix, D, 0, 0)], [],
        [(D, BF16)] * 4, [], nrows=T, tr=TR, name="mix_bwd")
    pin = gw.finish("w_mlp1", d_ga)
    gw["w_attn_up"] = _matmul(attn, d_attn_d, mode="tn", name="dw_attn_up", out_dtypes=(BF16,), tm=512, tn=1024, tk=1024, after=pin)
    d_attn = _matmul(d_attn_d, wb["w_attn_up"], mode="nt", name="d_attn", out_dtypes=(BF16,), tm=1024, tn=512)
    gw["w_ssm_up"] = _matmul(glu, d_ssm_d, mode="tn", name="dw_ssm_up", out_dtypes=(BF16,), tm=512, tn=1024, tk=1024)
    d_glu = _matmul(d_ssm_d, wb["w_ssm_up"], mode="nt", name="d_glu", tm=1024, tn=512, after=(gw.order("w_attn_up"), gw.order("w_ssm_up")))

    def glu_b(rv, vv):
        gr, _ = _vjp_rows(_f_glu, (rv[0],), rv[1], 1)
        return gr, []

    dz, = _rowwise(glu_b, [(z, 2 * SW, 0, 0), (d_glu, SW, 0, 0)], [], [(2 * SW, BF16)], [], nrows=T, tr=TR, name="glu_bwd")
    gw["w_glu"] = _matmul(ge, dz, mode="tn", name="dw_glu", out_dtypes=(BF16,), tm=512, tn=1024, tk=1024)
    d_ge = _matmul(dz, wb["w_glu"], mode="nt", name="d_ge", tm=1024, tn=512, after=(gw.order("w_glu"),))

    du_all, d_dskip, dlam, dbmat, dcmat = _s5_bwd(d_ge, ssm, proj, dskip, states, lam, bmat, cmat)
    s5_grads = s5_pull((dlam, dbmat, dcmat))
    early = dict(zip(s5_names, s5_grads), ssm_d=d_dskip)
    if on_early is not None:
        on_early(early)
    pin = gw.finish("w_glu", du_all)

    dq, dk, dv, dsink = _attn_bwd(proj, d_attn, sink, tabs)
    zc = lambda w: jnp.zeros((C, w), BF16)
    dproj = jnp.concatenate([
        jnp.concatenate([dq, zc(QW)], 0), dk, dv, du_all,
        jnp.concatenate([d_ga, zc(D)], 0), jnp.concatenate([d_gs, zc(D)], 0)], 1)
    gw["w_in"] = _matmul(h1_t, dproj, mode="nn", name="dw_in", out_dtypes=(BF16,), tm=1024, tn=512, after=pin)
    pin = gw.finish("w_in", gw["w_in"])
    dh1 = _matmul(dproj, wb["w_in"], mode="nt", name="d_h1", tm=768, tn=512, tk=2048, after=pin)

    def ln1_b(rv, vv):
        x_t, dh_t, dxa_t = rv
        gr, gv = _vjp_rows(_f_ln_mod, (x_t, vv[0], vv[1]), dh_t, 1)
        return [gr[0] + dxa_t], gv

    grad_x, d_sc1, d_sh1 = _rowwise(ln1_b, [(x, D, 0, 0), (dh1, D, 0, 0), (dx_a, D, 0, 0)], [sc1, sh1],
                                    [(D, F32)], [(1, D), (1, D)], nrows=T, tr=TR, name="ln1_lat_bwd")

    def ln1c_b(rv, vv):
        _, gv = _vjp_rows(_f_ln_mod, (rv[0], vv[0], vv[1]), rv[1], 1)
        return [], gv

    d_csc1, d_csh1 = _rowwise(ln1c_b, [(ctx, D, 0, 0), (dh1, D, 0, T // TR)], [csc1, csh1],
                              [], [(1, D), (1, D)], nrows=C, tr=TR, name="ln1_ctx_bwd")

    d_mod_lat = jnp.concatenate([d_sh1, d_sc1, d_g1, d_sh2, d_sc2, d_g2], 1)
    zv = jnp.zeros((1, D), F32)
    d_mod_ctx = jnp.concatenate([d_csh1, d_csc1, zv, zv, zv, zv], 1)
    gs = {n: g for n, g in zip(s5_names, s5_grads)}
    gs["attn_sink"] = dsink[:, 0]
    gs["ssm_d"] = d_dskip
    gs["ln_mix_g"], gs["ln_mix_b"] = d_lg_mix, d_lb_mix
    gs["ln_mlp_g"], gs["ln_mlp_b"] = d_lg_mlp, d_lb_mlp
    gs["b_mlp1"], gs["b_mlp2"] = d_b1, d_b2
    return loss_p, grad_x, d_mod_lat, d_mod_ctx, gw, gs


def _my_pos():
    return lax.axis_index("x"), lax.axis_index("y"), lax.axis_index("c")


def _flip(p, bit):
    return 1 - p if bit else p


def _peer(pos, k):
    x, y, c = pos
    return (_flip(x, (k >> 2) & 1), _flip(y, (k >> 1) & 1), _flip(c, k & 1))


def _lin(pos):
    return 4 * pos[0] + 2 * pos[1] + pos[2]


def _allgather_small(v, name):
    r, w = v.shape

    def body(v_ref, out_ref, send_sems, recv_sems, local_sem):
        me = _my_pos()
        mine = pltpu.make_async_copy(v_ref, out_ref.at[_lin(me)], local_sem)
        mine.start()
        sends = []
        for k in range(1, N_DEV):
            cp = pltpu.make_async_remote_copy(src_ref=v_ref, dst_ref=out_ref.at[_lin(me)], send_sem=send_sems.at[k - 1],
                                              recv_sem=recv_sems.at[k - 1], device_id=_peer(me, k), device_id_type=MESH)
            cp.start()
            sends.append(cp)
        for k in range(1, N_DEV):
            peer = _peer(me, k)
            pltpu.make_async_remote_copy(src_ref=v_ref, dst_ref=out_ref.at[_lin(peer)], send_sem=send_sems.at[k - 1],
                                         recv_sem=recv_sems.at[k - 1], device_id=peer, device_id_type=MESH).wait_recv()
        for cp in sends:
            cp.wait_send()
        mine.wait()

    return pl.pallas_call(
        body,
        name=name,
        out_shape=jax.ShapeDtypeStruct((N_DEV, r, w), v.dtype),
        in_specs=[pl.BlockSpec(memory_space=pltpu.VMEM)],
        out_specs=pl.BlockSpec(memory_space=pltpu.VMEM),
        scratch_shapes=[pltpu.SemaphoreType.DMA((N_DEV - 1,)), pltpu.SemaphoreType.DMA((N_DEV - 1,)), pltpu.SemaphoreType.DMA],
        compiler_params=pltpu.CompilerParams(vmem_limit_bytes=VMEM_LIMIT_BYTES),
    )(v)


def _block_of(ref, kind, idx, n):
    start = pl.multiple_of(idx * n, 128)
    if kind == "col":
        return ref.at[:, pl.ds(start, n)]
    return ref.at[pl.ds(start, n), :]


def _allgather_weights(shards, kinds):
    nt = len(shards)
    out_shape = []
    for s, kind in zip(shards, kinds):
        k, n = s.shape
        out_shape.append(jax.ShapeDtypeStruct((k, n * N_DEV) if kind == "col" else (k * N_DEV, n), s.dtype))

    def body(*refs):
        ins, outs = refs[:nt], refs[nt:2 * nt]
        send_sems, recv_sems, local_sems = refs[2 * nt:]
        x, y, c = _my_pos()
        me, sibling = (x, y, c), (x, y, 1 - c)
        chips = [(1 - x, y), (x, 1 - y), (1 - x, 1 - y)]

        def blk(t, pos):
            n = shards[t].shape[1] if kinds[t] == "col" else shards[t].shape[0]
            return _block_of(outs[t], kinds[t], _lin(pos), n)

        def copy(t, k, block, to, src=None):
            return pltpu.make_async_remote_copy(src_ref=blk(t, block) if src is None else src, dst_ref=blk(t, block),
                                                send_sem=send_sems.at[t, k], recv_sem=recv_sems.at[t, k],
                                                device_id=to, device_id_type=MESH)

        local, sends = [], []
        for t in range(nt):
            mine = pltpu.make_async_copy(ins[t], blk(t, me), local_sems.at[t])
            mine.start()
            local.append(mine)
            first = [copy(t, 0, me, sibling, src=ins[t])]
            first += [copy(t, 1 + j, me, (*chip, c), src=ins[t]) for j, chip in enumerate(chips)]
            for cp in first:
                cp.start()
            sends += first
        for t in range(nt):
            for j, chip in enumerate(chips):
                copy(t, 1 + j, (*chip, c), me).wait_recv()
                fwd = copy(t, 4 + j, (*chip, c), sibling)
                fwd.start()
                sends.append(fwd)
        for t in range(nt):
            copy(t, 0, sibling, me).wait_recv()
            for j, chip in enumerate(chips):
                copy(t, 4 + j, (*chip, 1 - c), me).wait_recv()
        for cp in sends:
            cp.wait_send()
        for cp in local:
            cp.wait()

    any_spec = pl.BlockSpec(memory_space=pl.ANY)
    return pl.pallas_call(
        body,
        name="allgather_weights",
        out_shape=out_shape,
        in_specs=[any_spec] * nt,
        out_specs=[any_spec] * nt,
        scratch_shapes=[pltpu.SemaphoreType.DMA((nt, N_DEV - 1)), pltpu.SemaphoreType.DMA((nt, N_DEV - 1)),
                        pltpu.SemaphoreType.DMA((nt,))],
    )(*shards)


def _handshake(peers):
    barrier = pltpu.get_barrier_semaphore()
    for peer in peers:
        pl.semaphore_signal(barrier, inc=1, device_id=peer, device_id_type=MESH)
    pl.semaphore_wait(barrier, len(peers))


def _allgather_weights_seq(shards, kinds, name, collective_id):
    nt = len(shards)
    hbm = pltpu.MemorySpace.HBM
    ins = [jax.new_ref(s, memory_space=hbm) for s in shards]
    outs = []
    for s, kind in zip(shards, kinds):
        k, n = s.shape
        shape = (k, n * N_DEV) if kind == "col" else (k * N_DEV, n)
        outs.append(jax.empty_ref(jax.ShapeDtypeStruct(shape, s.dtype), memory_space=hbm))

    @functools.partial(
        pl.kernel, mesh=plsc.ScalarSubcoreMesh(axis_name="seq", num_cores=1), name=name,
        scratch_types=(pltpu.SemaphoreType.DMA((nt, N_DEV - 1)), pltpu.SemaphoreType.DMA((nt, N_DEV - 1)),
                       pltpu.SemaphoreType.DMA((nt,))),
        compiler_params=pltpu.CompilerParams(collective_id=collective_id))
    def launch(send_sems, recv_sems, local_sems):
        x, y, c = _my_pos()
        me, sibling = (x, y, c), (x, y, 1 - c)
        chips = [(1 - x, y), (x, 1 - y), (1 - x, 1 - y)]
        _handshake([sibling] + [(*chip, c) for chip in chips])

        def blk(t, pos):
            n = shards[t].shape[1] if kinds[t] == "col" else shards[t].shape[0]
            return _block_of(outs[t], kinds[t], _lin(pos), n)

        def copy(t, k, block, to, src=None):
            return pltpu.make_async_remote_copy(src_ref=blk(t, block) if src is None else src, dst_ref=blk(t, block),
                                                send_sem=send_sems.at[t, k], recv_sem=recv_sems.at[t, k],
                                                device_id=to, device_id_type=MESH)

        local, sends = [], []
        for t in range(nt):
            mine = pltpu.make_async_copy(ins[t], blk(t, me), local_sems.at[t])
            mine.start()
            local.append(mine)
            first = [copy(t, 0, me, sibling, src=ins[t])]
            first += [copy(t, 1 + j, me, (*chip, c), src=ins[t]) for j, chip in enumerate(chips)]
            for cp in first:
                cp.start()
            sends += first
        for t in range(nt):
            for j, chip in enumerate(chips):
                copy(t, 1 + j, (*chip, c), me).wait_recv()
                fwd = copy(t, 4 + j, (*chip, c), sibling)
                fwd.start()
                sends.append(fwd)
        for t in range(nt):
            copy(t, 0, sibling, me).wait_recv()
            for j, chip in enumerate(chips):
                copy(t, 4 + j, (*chip, 1 - c), me).wait_recv()
        for cp in sends:
            cp.wait_send()
        for cp in local:
            cp.wait()

    launch()
    return [o[...] for o in outs]


def _allgather_small_seq(v, name, collective_id):
    hbm = pltpu.MemorySpace.HBM
    src = jax.new_ref(v, memory_space=hbm)
    out = jax.empty_ref(jax.ShapeDtypeStruct((N_DEV,) + v.shape, v.dtype), memory_space=hbm)

    @functools.partial(
        pl.kernel, mesh=plsc.ScalarSubcoreMesh(axis_name="seq", num_cores=1), name=name,
        scratch_types=(pltpu.SemaphoreType.DMA((N_DEV - 1,)), pltpu.SemaphoreType.DMA((N_DEV - 1,)), pltpu.SemaphoreType.DMA),
        compiler_params=pltpu.CompilerParams(collective_id=collective_id))
    def launch(send_sems, recv_sems, local_sem):
        me = _my_pos()
        _handshake([_peer(me, k) for k in range(1, N_DEV)])
        mine = pltpu.make_async_copy(src, out.at[_lin(me)], local_sem)
        mine.start()
        sends = []
        for k in range(1, N_DEV):
            cp = pltpu.make_async_remote_copy(src_ref=src, dst_ref=out.at[_lin(me)], send_sem=send_sems.at[k - 1],
                                              recv_sem=recv_sems.at[k - 1], device_id=_peer(me, k), device_id_type=MESH)
            cp.start()
            sends.append(cp)
        for k in range(1, N_DEV):
            peer = _peer(me, k)
            pltpu.make_async_remote_copy(src_ref=src, dst_ref=out.at[_lin(peer)], send_sem=send_sems.at[k - 1],
                                         recv_sem=recv_sems.at[k - 1], device_id=peer, device_id_type=MESH).wait_recv()
        for cp in sends:
            cp.wait_send()
        mine.wait()

    launch()
    return out[...]


N_CHIP = N_DEV // 2


def _chip_of(pos):
    return 2 * pos[0] + pos[1]


def _pair_exchange_seq(grads, kinds, name, collective_id):
    nt = len(grads)
    hbm = pltpu.MemorySpace.HBM
    shard_shapes = _shard_shapes(grads, kinds)
    ins = [jax.new_ref(g, memory_space=hbm) for g in grads]
    outs = [jax.empty_ref(jax.ShapeDtypeStruct((N_CHIP,) + s, g.dtype), memory_space=hbm) for s, g in zip(shard_shapes, grads)]

    @functools.partial(
        pl.kernel, mesh=plsc.ScalarSubcoreMesh(axis_name="seq", num_cores=1), name=name,
        scratch_types=(pltpu.SemaphoreType.DMA((nt, N_CHIP)), pltpu.SemaphoreType.DMA((nt, N_CHIP))),
        compiler_params=pltpu.CompilerParams(collective_id=collective_id))
    def launch(send_sems, recv_sems):
        x, y, c = _my_pos()
        sibling = (x, y, 1 - c)
        _handshake([sibling])
        copies = []
        for t in range(nt):
            n = shard_shapes[t][1] if kinds[t] == "col" else shard_shapes[t][0]
            for q in range(N_CHIP):
                cp = pltpu.make_async_remote_copy(src_ref=_block_of(ins[t], kinds[t], 2 * q + (1 - c), n), dst_ref=outs[t].at[q],
                                                  send_sem=send_sems.at[t, q], recv_sem=recv_sems.at[t, q],
                                                  device_id=sibling, device_id_type=MESH)
                cp.start()
                copies.append(cp)
        for cp in copies:
            cp.wait_recv()
        for cp in copies:
            cp.wait_send()

    launch()
    return [o[...] for o in outs]


def _pair_add(g, half, kind, name, after=()):
    nq, k, ns = half.shape
    tr = min(k, 512)
    c_idx = lax.axis_index("c").astype(jnp.int32).reshape(1)
    if kind == "col":
        g_spec = pl.BlockSpec((tr, ns), lambda q, i, c_ref: (i, 2 * q + c_ref[0]))
    else:
        g_spec = pl.BlockSpec((tr, ns), lambda q, i, c_ref: ((2 * q + c_ref[0]) * (k // tr) + i, 0))
    n_after = len(after)

    def kern(c_ref, g_ref, h_ref, *rest):
        o_ref = rest[n_after]
        o_ref[0] = (g_ref[...].astype(F32) + h_ref[0].astype(F32)).astype(o_ref.dtype)

    return pl.pallas_call(
        kern,
        name=name,
        grid_spec=pltpu.PrefetchScalarGridSpec(
            num_scalar_prefetch=1,
            grid=(nq, k // tr),
            in_specs=[g_spec, pl.BlockSpec((1, tr, ns), lambda q, i, c_ref: (q, i, 0))] + [pl.BlockSpec(memory_space=pl.ANY)] * n_after,
            out_specs=pl.BlockSpec((1, tr, ns), lambda q, i, c_ref: (q, i, 0)),
        ),
        out_shape=jax.ShapeDtypeStruct(half.shape, half.dtype),
        compiler_params=_cparams(("parallel", "parallel")),
    )(c_idx, g, half, *after)


def _chip_exchange_seq(psums, name, collective_id):
    nt = len(psums)
    hbm = pltpu.MemorySpace.HBM
    ins = [jax.new_ref(s, memory_space=hbm) for s in psums]
    outs = [jax.empty_ref(jax.ShapeDtypeStruct(s.shape, s.dtype), memory_space=hbm) for s in psums]

    @functools.partial(
        pl.kernel, mesh=plsc.ScalarSubcoreMesh(axis_name="seq", num_cores=1), name=name,
        scratch_types=(pltpu.SemaphoreType.DMA((nt, N_CHIP - 1)), pltpu.SemaphoreType.DMA((nt, N_CHIP - 1)),
                       pltpu.SemaphoreType.DMA((nt,))),
        compiler_params=pltpu.CompilerParams(collective_id=collective_id))
    def launch(send_sems, recv_sems, local_sems):
        me = _my_pos()
        peers = [_peer(me, k) for k in (2, 4, 6)]
        _handshake(peers)
        mine = _chip_of(me)
        local, sends = [], []
        for t in range(nt):
            cp = pltpu.make_async_copy(ins[t].at[mine], outs[t].at[mine], local_sems.at[t])
            cp.start()
            local.append(cp)
            for j, peer in enumerate(peers):
                cp = pltpu.make_async_remote_copy(src_ref=ins[t].at[_chip_of(peer)], dst_ref=outs[t].at[mine],
                                                  send_sem=send_sems.at[t, j], recv_sem=recv_sems.at[t, j],
                                                  device_id=peer, device_id_type=MESH)
                cp.start()
                sends.append(cp)
        for t in range(nt):
            for j, peer in enumerate(peers):
                pltpu.make_async_remote_copy(src_ref=ins[t].at[mine], dst_ref=outs[t].at[_chip_of(peer)],
                                             send_sem=send_sems.at[t, j], recv_sem=recv_sems.at[t, j],
                                             device_id=peer, device_id_type=MESH).wait_recv()
        for cp in sends:
            cp.wait_send()
        for cp in local:
            cp.wait()

    launch()
    return [o[...] for o in outs]


def _scatter_grads_seq(grads, kinds, name, collective_id):
    nt = len(grads)
    hbm = pltpu.MemorySpace.HBM
    shard_shapes = []
    for g, kind in zip(grads, kinds):
        k, n = g.shape
        shard_shapes.append((k, n // N_DEV) if kind == "col" else (k // N_DEV, n))
    ins = [jax.new_ref(g, memory_space=hbm) for g in grads]
    outs = [jax.empty_ref(jax.ShapeDtypeStruct((N_DEV,) + s, g.dtype), memory_space=hbm) for s, g in zip(shard_shapes, grads)]

    @functools.partial(
        pl.kernel, mesh=plsc.ScalarSubcoreMesh(axis_name="seq", num_cores=1), name=name,
        scratch_types=(pltpu.SemaphoreType.DMA((nt, N_DEV - 1)), pltpu.SemaphoreType.DMA((nt, N_DEV - 1)),
                       pltpu.SemaphoreType.DMA((nt,))),
        compiler_params=pltpu.CompilerParams(collective_id=collective_id))
    def launch(send_sems, recv_sems, local_sems):
        me = _my_pos()
        _handshake([_peer(me, k) for k in range(1, N_DEV)])

        def blk(t, pos):
            n = shard_shapes[t][1] if kinds[t] == "col" else shard_shapes[t][0]
            return _block_of(ins[t], kinds[t], _lin(pos), n)

        local, sends = [], []
        for t in range(nt):
            cp = pltpu.make_async_copy(blk(t, me), outs[t].at[_lin(me)], local_sems.at[t])
            cp.start()
            local.append(cp)
            for k in range(1, N_DEV):
                peer = _peer(me, k)
                cp = pltpu.make_async_remote_copy(src_ref=blk(t, peer), dst_ref=outs[t].at[_lin(me)], send_sem=send_sems.at[t, k - 1],
                                                  recv_sem=recv_sems.at[t, k - 1], device_id=peer, device_id_type=MESH)
                cp.start()
                sends.append(cp)
        for t in range(nt):
            for k in range(1, N_DEV):
                peer = _peer(me, k)
                pltpu.make_async_remote_copy(src_ref=blk(t, me), dst_ref=outs[t].at[_lin(peer)], send_sem=send_sems.at[t, k - 1],
                                             recv_sem=recv_sems.at[t, k - 1], device_id=peer, device_id_type=MESH).wait_recv()
        for cp in sends:
            cp.wait_send()
        for cp in local:
            cp.wait()

    launch()
    return [o[...] for o in outs]


_HBM_SPEC = pl.BlockSpec(memory_space=pltpu.HBM)
_SEM_SPEC = pl.BlockSpec(memory_space=pltpu.SEMAPHORE)
_EFFECT = pltpu.SideEffectType.DATAFLOW_SIDE_EFFECTING
LOCAL_CHUNKS = 16


def _shard_shapes(grads, kinds):
    return [(g.shape[0], g.shape[1] // N_DEV) if kind == "col" else (g.shape[0] // N_DEV, g.shape[1]) for g, kind in zip(grads, kinds)]


def _scatter_copies(g_refs, land_refs, send_sems, recv_sems, kinds, shard_shapes):
    me = _my_pos()
    copies = []
    for t in range(len(g_refs)):
        n = shard_shapes[t][1] if kinds[t] == "col" else shard_shapes[t][0]
        for k in range(1, N_DEV):
            peer = _peer(me, k)
            copies.append(pltpu.make_async_remote_copy(
                src_ref=_block_of(g_refs[t], kinds[t], _lin(peer), n), dst_ref=land_refs[t].at[_lin(me)],
                send_sem=send_sems.at[t * (N_DEV - 1) + k - 1], recv_sem=recv_sems.at[t * (N_DEV - 1) + k - 1],
                device_id=peer, device_id_type=MESH))
    return copies


def _scatter_start(grads, kinds, name):
    nt = len(grads)
    shard_shapes = _shard_shapes(grads, kinds)

    def body(*refs):
        g_refs, land_refs = refs[:nt], refs[nt:2 * nt]
        send_sems, recv_sems = refs[2 * nt], refs[2 * nt + 1]
        token = refs[2 * nt + 2 + 2 * nt]
        local_sems = refs[-1]
        me = _my_pos()
        local = []
        for t in range(nt):
            n = shard_shapes[t][1] if kinds[t] == "col" else shard_shapes[t][0]
            src, dst = _block_of(g_refs[t], kinds[t], _lin(me), n), land_refs[t].at[_lin(me)]
            rows = shard_shapes[t][0] // LOCAL_CHUNKS
            for ch in range(LOCAL_CHUNKS):
                rs = pl.ds(ch * rows, rows)
                cp = pltpu.make_async_copy(src.at[rs, :], dst.at[rs, :], local_sems.at[t * LOCAL_CHUNKS + ch])
                cp.start()
                local.append(cp)
        token[...] = jnp.zeros_like(token)
        for cp in local:
            cp.wait()
        for cp in _scatter_copies(g_refs, land_refs, send_sems, recv_sems, kinds, shard_shapes):
            cp.start()

    lands = [pltpu.with_memory_space_constraint(lax.empty((N_DEV,) + s, g.dtype), pltpu.HBM) for s, g in zip(shard_shapes, grads)]
    sem_shape = pltpu.SemaphoreType.DMA((nt * (N_DEV - 1),))
    out = pl.pallas_call(
        body,
        name=name,
        out_shape=(sem_shape, sem_shape, *[pltpu.HBM(g.shape, g.dtype) for g in grads],
                   *[pltpu.HBM(l.shape, l.dtype) for l in lands], jax.ShapeDtypeStruct((8, 128), F32)),
        in_specs=[_HBM_SPEC] * (2 * nt),
        out_specs=(_SEM_SPEC, _SEM_SPEC, *[_HBM_SPEC] * (2 * nt), pl.BlockSpec(memory_space=pltpu.VMEM)),
        input_output_aliases={i: 2 + i for i in range(2 * nt)},
        scratch_shapes=[pltpu.SemaphoreType.DMA((nt * LOCAL_CHUNKS,))],
        compiler_params=pltpu.CompilerParams(has_side_effects=_EFFECT),
    )(*[pltpu.with_memory_space_constraint(g, pltpu.HBM) for g in grads], *lands)
    return out[0], out[1], list(out[2:2 + nt]), list(out[2 + nt:2 + 2 * nt]), out[-1]


def _scatter_wait(send_sems, recv_sems, g_thru, land_thru, kinds, after, name):
    nt = len(g_thru)
    shard_shapes = _shard_shapes(g_thru, kinds)

    def body(*refs):
        g_refs, land_refs = refs[:nt], refs[nt:2 * nt]
        send_sems, recv_sems = refs[2 * nt], refs[2 * nt + 1]
        for cp in _scatter_copies(g_refs, land_refs, send_sems, recv_sems, kinds, shard_shapes):
            cp.wait_send()
            cp.wait_recv()

    out = pl.pallas_call(
        body,
        name=name,
        out_shape=tuple(pltpu.HBM(a.shape, a.dtype) for a in (*g_thru, *land_thru)),
        in_specs=[*[_HBM_SPEC] * (2 * nt), _SEM_SPEC, _SEM_SPEC, pl.BlockSpec(memory_space=pl.ANY)],
        out_specs=tuple([_HBM_SPEC] * (2 * nt)),
        input_output_aliases={i: i for i in range(2 * nt)},
        compiler_params=pltpu.CompilerParams(has_side_effects=_EFFECT),
    )(*g_thru, *land_thru, send_sems, recv_sems, after)
    return list(out[nt:])


def _scatter_grads(grads, kinds):
    nt = len(grads)
    shard_shapes = []
    for g, kind in zip(grads, kinds):
        k, n = g.shape
        shard_shapes.append((k, n // N_DEV) if kind == "col" else (k // N_DEV, n))

    def body(*refs):
        ins, outs = refs[:nt], refs[nt:2 * nt]
        send_sems, recv_sems, local_sems = refs[2 * nt:]
        me = _my_pos()

        def blk(t, pos):
            n = shard_shapes[t][1] if kinds[t] == "col" else shard_shapes[t][0]
            return _block_of(ins[t], kinds[t], _lin(pos), n)

        local, sends = [], []
        for t in range(nt):
            cp = pltpu.make_async_copy(blk(t, me), outs[t].at[_lin(me)], local_sems.at[t])
            cp.start()
            local.append(cp)
            for k in range(1, N_DEV):
                peer = _peer(me, k)
                cp = pltpu.make_async_remote_copy(src_ref=blk(t, peer), dst_ref=outs[t].at[_lin(me)], send_sem=send_sems.at[t, k - 1],
                                                  recv_sem=recv_sems.at[t, k - 1], device_id=peer, device_id_type=MESH)
                cp.start()
                sends.append(cp)
        for t in range(nt):
            for k in range(1, N_DEV):
                peer = _peer(me, k)
                pltpu.make_async_remote_copy(src_ref=blk(t, me), dst_ref=outs[t].at[_lin(peer)], send_sem=send_sems.at[t, k - 1],
                                             recv_sem=recv_sems.at[t, k - 1], device_id=peer, device_id_type=MESH).wait_recv()
        for cp in sends:
            cp.wait_send()
        for cp in local:
            cp.wait()

    any_spec = pl.BlockSpec(memory_space=pl.ANY)
    return pl.pallas_call(
        body,
        name="scatter_grads",
        out_shape=[jax.ShapeDtypeStruct((N_DEV,) + s, g.dtype) for s, g in zip(shard_shapes, grads)],
        in_specs=[any_spec] * nt,
        out_specs=[any_spec] * nt,
        scratch_shapes=[pltpu.SemaphoreType.DMA((nt, N_DEV - 1)), pltpu.SemaphoreType.DMA((nt, N_DEV - 1)),
                        pltpu.SemaphoreType.DMA((nt,))],
    )(*grads)


def _adam(g_slots, w, m, v, *, tr, name, after=()):
    ns, r, wd = g_slots.shape
    tr = min(tr, r)
    assert r % tr == 0, (name, r, tr)
    c1 = 1.0 - ADAM_B1 ** ADAM_STEP
    c2 = 1.0 - ADAM_B2 ** ADAM_STEP
    n_after = len(after)

    def kern(g_ref, w_ref, m_ref, v_ref, *rest):
        go_ref, d_ref, mo_ref, vo_ref = rest[n_after:]
        g = g_ref[0].astype(F32)
        for s in range(1, ns):
            g = g + g_ref[s].astype(F32)
        m_new = ADAM_B1 * m_ref[...] + (1.0 - ADAM_B1) * g
        v_new = ADAM_B2 * v_ref[...] + (1.0 - ADAM_B2) * (g * g)
        m_hat = m_new / c1
        v_hat = v_new / c2
        go_ref[...] = g
        d_ref[...] = -ADAM_LR * (m_hat / (jnp.sqrt(v_hat) + ADAM_EPS) + ADAM_WD * w_ref[...])
        mo_ref[...] = m_new
        vo_ref[...] = v_new

    tile = pl.BlockSpec((tr, wd), lambda i: (i, 0))
    return pl.pallas_call(
        kern,
        name=name,
        grid=(r // tr,),
        in_specs=[pl.BlockSpec((ns, tr, wd), lambda i: (0, i, 0)), tile, tile, tile] + [pl.BlockSpec(memory_space=pl.ANY)] * n_after,
        out_specs=[tile] * 4,
        out_shape=[jax.ShapeDtypeStruct((r, wd), F32)] * 4,
        compiler_params=_cparams(("parallel",)),
    )(g_slots, w, m, v, *after)


SMALL = ("c_ctx", "b_ada", "attn_sink", "ssm_a_re", "ssm_a_im", "ssm_log_dt", "ssm_b_re", "ssm_b_im", "ssm_c_re", "ssm_c_im",
         "ssm_d", "ln_mix_g", "ln_mix_b", "b_mlp1", "b_mlp2", "ln_mlp_g", "ln_mlp_b")
BIG = ("w_in", "w_glu", "w_attn_up", "w_ssm_up", "w_out", "w_mlp1", "w_mlp2")
BIG_KIND = ("col", "col", "col", "col", "row", "col", "row")
AG_GROUPS = (("w_in",), ("w_glu", "w_attn_up", "w_ssm_up", "w_out"), ("w_mlp1",), ("w_mlp2",))
AG_COLLECTIVE_ID0 = 1
RS_GROUPS = (("w_mlp2",), ("w_mlp1",), ("w_out", "w_attn_up", "w_ssm_up", "w_glu"), ("w_in",))
RS_COLLECTIVE_ID0 = AG_COLLECTIVE_ID0 + len(AG_GROUPS)
SMALL_EARLY = ("ssm_a_re", "ssm_a_im", "ssm_log_dt", "ssm_b_re", "ssm_b_im", "ssm_c_re", "ssm_c_im", "ssm_d")
SMALL_LATE = tuple(n for n in SMALL if n not in SMALL_EARLY)
SMALL_COLLECTIVE_ID0 = RS_COLLECTIVE_ID0 + 2 * len(RS_GROUPS)
LANES = 128


def _pack(parts):
    rows = []
    for p in parts:
        flat = p.reshape(-1).astype(F32)
        pad = (-flat.shape[0]) % LANES
        rows.append(jnp.pad(flat, (0, pad)).reshape(-1, LANES))
    packed = jnp.concatenate(rows, 0)
    return jnp.pad(packed, ((0, (-packed.shape[0]) % 8), (0, 0)))


def _unpack(packed, shapes):
    out, r0 = [], 0
    for s in shapes:
        n = math.prod(s)
        nr = -(-n // LANES)
        out.append(packed[r0:r0 + nr].reshape(-1)[:n].reshape(s))
        r0 += nr
    return out


WEIGHTS = ("c_ctx", "w_ada", "b_ada", "w_in", "attn_sink", "ssm_a_re", "ssm_a_im", "ssm_log_dt", "ssm_b_re", "ssm_b_im",
           "ssm_c_re", "ssm_c_im", "ssm_d", "w_glu", "w_attn_up", "w_ssm_up", "w_out", "ln_mix_g", "ln_mix_b", "w_mlp1",
           "b_mlp1", "w_mlp2", "b_mlp2", "ln_mlp_g", "ln_mlp_b")
ADA_COLS = 6 * D // N_DEV


def _step(x, c, ctx, loss_target, p, m, v):
    me = _lin(_my_pos())
    x2, ctx2, tgt2 = x[0], ctx[0], loss_target[0]

    wb = {}
    for gi, group in enumerate(AG_GROUPS):
        full = _allgather_weights_seq([p[n][0].astype(BF16) for n in group], [BIG_KIND[BIG.index(n)] for n in group],
                                      "allgather_seq%d" % gi, AG_COLLECTIVE_ID0 + gi)
        wb.update(zip(group, full))

    c_all = _allgather_small(jnp.broadcast_to(c, (8, D)), "gather_c")[:, 0, :]
    cc = p["c_ctx"].reshape(1, D)
    s_in = jnp.concatenate([c_all, cc, jnp.zeros((7, D), F32)], 0)
    s_act, = _rowwise(lambda rv, vv: ([_silu(rv[0])], []), [(s_in, D, 0, 0)], [], [(D, F32)], [], nrows=16, tr=16, name="silu_c")
    b_mine = lax.dynamic_slice_in_dim(p["b_ada"], me * ADA_COLS, ADA_COLS, axis=1)
    mod_part = _matmul(s_act, p["w_ada"][0], mode="nn", name="ada_fwd", tm=16, tn=512, bias=b_mine)
    mod_all = _allgather_small(mod_part, "gather_mod")
    mod_lat = lax.dynamic_index_in_dim(mod_all, me, axis=1, keepdims=False).reshape(1, 6 * D)
    mod_ctx = mod_all[:, 8, :].reshape(1, 6 * D)

    sp = {n: p[n][0] for n in SMALL if n not in ("c_ctx", "b_ada")}
    recv, halves = {}, {}

    def on_grad(gw):
        for gi, group in enumerate(RS_GROUPS):
            if gi not in halves and all(n in gw for n in group):
                kinds = [BIG_KIND[BIG.index(n)] for n in group]
                halves[gi] = (dict(gw), _pair_exchange_seq([gw[n] for n in group], kinds, "pair_exchange%d" % gi, RS_COLLECTIVE_ID0 + 2 * gi))

    def on_finish(key, after):
        gi = [i for i, group in enumerate(RS_GROUPS) if key in group][0]
        group = RS_GROUPS[gi]
        grads, half = halves[gi]
        prev = tuple(recv[n] for n in RS_GROUPS[gi - 1][:1]) if gi else ()
        if gi == len(RS_GROUPS) - 1:
            prev += (small["early"],)
        psums =[_pair_add(grads[n], h, BIG_KIND[BIG.index(n)], "pair_add_" + n, after=(after,) + prev) for n, h in zip(group, half)]
        recv.update(zip(group, _chip_exchange_seq(psums, "chip_exchange%d" % gi, RS_COLLECTIVE_ID0 + 2 * gi + 1)))
        return psums[-1]

    small = {}

    def on_early(gs_early):
        small["early"] = _allgather_small_seq(_pack([gs_early[n] for n in SMALL_EARLY]), "gather_small_early", SMALL_COLLECTIVE_ID0)

    total = {}

    def on_loss(loss_p):
        total["loss"] = lax.psum(loss_p[0, 0], ("x", "y", "c"))
        return total["loss"].reshape(1, 1)

    loss_p, grad_x, d_mod_lat, d_mod_ctx, gw, gs = _local_step(x2, ctx2, tgt2, mod_lat, mod_ctx, wb, sp, on_grad, on_loss, on_finish, on_early)

    g_early = small["early"]
    res = {}
    last = ()

    def adam_small(names, g_pack, tag, after):
        sm = _adam(g_pack, _pack([p[n] for n in names]), _pack([m[n] for n in names]), _pack([v[n] for n in names]),
                   tr=g_pack.shape[1], name="adam_small_" + tag, after=after)
        shapes = [p[n].shape for n in names]
        for j, outs in enumerate(zip(*[_unpack(a, shapes) for a in sm])):
            res[names[j]] = outs
        return (sm[0],)

    for gi, group in enumerate(RS_GROUPS):
        if gi == len(RS_GROUPS) - 1:
            last = adam_small(SMALL_EARLY, g_early, "early", last)
        for n in group:
            res[n] = _adam(recv[n], p[n][0], m[n][0], v[n][0], tr=256, name="adam_" + n, after=last)
            last = (res[n][0],)

    dm = jnp.concatenate([d_mod_lat, d_mod_ctx, jnp.zeros((6, 6 * D), F32)], 0)
    dm_all = _allgather_small_seq(dm, "gather_dmod", SMALL_COLLECTIVE_ID0 + 1)
    dm_all = lax.optimization_barrier((dm_all,) + last)[0]
    dm2 = jnp.concatenate([dm_all[:, 0, :], dm_all[:, 1, :]], 0)
    dm2_mine = lax.dynamic_slice_in_dim(dm2, me * ADA_COLS, ADA_COLS, axis=1)
    s2 = jnp.concatenate([s_act[0:8], jnp.broadcast_to(s_act[8:9], (8, D))], 0)
    g_w_ada = _matmul(s2, dm2_mine, mode="tn", name="dw_ada", tm=512, tn=ADA_COLS, after=last)
    dsc_part = _matmul(dm2_mine[8:16], p["w_ada"][0], mode="nt", name="d_silu_cctx", tm=8, tn=512, after=last)

    def cctx_b(rv, vv):
        _, pull = jax.vjp(_silu, vv[0])
        return [], [pull(jnp.sum(rv[0], axis=0, keepdims=True))[0]]

    g_cctx, = _rowwise(cctx_b, [(dsc_part, D, 0, 0)], [cc], [], [(1, D)], nrows=8, tr=8, name="cctx_bwd")
    gs["c_ctx"] = g_cctx
    gs["b_ada"] = d_mod_lat + d_mod_ctx

    res["w_ada"] = _adam(g_w_ada[None], p["w_ada"][0], m["w_ada"][0], v["w_ada"][0], tr=256, name="adam_w_ada")

    g_late = _allgather_small_seq(_pack([gs[n] for n in SMALL_LATE]), "gather_small_late", SMALL_COLLECTIVE_ID0 + 2)
    adam_small(SMALL_LATE, g_late, "late", (res["w_ada"][0],))

    outs = [total["loss"], grad_x[None]]
    for j in range(4):
        outs += [res[n][j].reshape(p[n].shape) for n in WEIGHTS]
    return tuple(outs)


def kernel(x, c, ctx, c_ctx, w_ada, b_ada, w_in, attn_sink, ssm_a_re, ssm_a_im, ssm_log_dt, ssm_b_re, ssm_b_im, ssm_c_re, ssm_c_im, ssm_d, w_glu, w_attn_up, w_ssm_up, w_out, ln_mix_g, ln_mix_b, w_mlp1, b_mlp1, w_mlp2, b_mlp2, ln_mlp_g, ln_mlp_b, loss_target, m_c_ctx, m_w_ada, m_b_ada, m_w_in, m_attn_sink, m_ssm_a_re, m_ssm_a_im, m_ssm_log_dt, m_ssm_b_re, m_ssm_b_im, m_ssm_c_re, m_ssm_c_im, m_ssm_d, m_w_glu, m_w_attn_up, m_w_ssm_up, m_w_out, m_ln_mix_g, m_ln_mix_b, m_w_mlp1, m_b_mlp1, m_w_mlp2, m_b_mlp2, m_ln_mlp_g, m_ln_mlp_b, v_c_ctx, v_w_ada, v_b_ada, v_w_in, v_attn_sink, v_ssm_a_re, v_ssm_a_im, v_ssm_log_dt, v_ssm_b_re, v_ssm_b_im, v_ssm_c_re, v_ssm_c_im, v_ssm_d, v_w_glu, v_w_attn_up, v_w_ssm_up, v_w_out, v_ln_mix_g, v_ln_mix_b, v_w_mlp1, v_b_mlp1, v_w_mlp2, v_b_mlp2, v_ln_mlp_g, v_ln_mlp_b):
    given = dict(locals())
    p = {n: given[n] for n in WEIGHTS}
    m = {n: given["m_" + n] for n in WEIGHTS}
    v = {n: given["v_" + n] for n in WEIGHTS}
    return _step(x, c, ctx, loss_target, p, m, v)
```

```python
import functools
import math

import jax
import jax.numpy as jnp
from jax import lax
from jax.experimental import pallas as pl
from jax.experimental.pallas import tpu as pltpu
from jax.experimental.pallas import tpu_sc as plsc

F32 = jnp.float32
BF16 = jnp.bfloat16

N_DEV = 8
D = 2048
T = 2048
C = 256
TA = T + C
GRID_W = 64
HD = 128
NH = 8
NKV = 2
GROUP = NH // NKV
WINDOW = 128
QW = NH * HD
KVW = NKV * HD
SW = D // 4
SG = 16
NG = SW // SG
SP = 64
DFF = 4 * D
IN_COLS = QW + 2 * KVW + SW + 2 * D
ALPHA = 2.0 ** 0.25
LN_EPS = 1e-6
NEG_INF = -1e30
ROPE_BASE = 10000.0
ATT_SCALE = HD ** -0.5

NSEG = 8
GBLK = 8
NBLK = NG // GBLK
BW = GBLK * SP
UW = GBLK * SG

ADAM_LR = 0.001
ADAM_B1 = 0.9
ADAM_B2 = 0.999
ADAM_EPS = 1e-08
ADAM_WD = 0.01
ADAM_STEP = 10

VMEM_LIMIT_BYTES = 56 * 1024 * 1024
MESH = pl.DeviceIdType.MESH


def _cparams(sem=None):
    return pltpu.CompilerParams(dimension_semantics=sem, vmem_limit_bytes=VMEM_LIMIT_BYTES)


def _matmul(a, b, *, mode, name, out_dtypes=(F32,), tm=512, tn=512, tk=None, bias=None, extras=(), epilogue=None, after=(),
            out_t=None):
    if mode == "nn":
        (M, K), (K2, N) = a.shape, b.shape
    elif mode == "nt":
        (M, K), (N, K2) = a.shape, b.shape
    else:
        (K, M), (K2, N) = a.shape, b.shape
    assert K == K2, (name, a.shape, b.shape)
    tm, tn, tk = min(tm, M), min(tn, N), min(tk or K, K)
    assert M % tm == 0 and N % tn == 0 and K % tk == 0, (name, M, N, K, tm, tn, tk)
    nk = K // tk
    if mode == "tn":
        a_spec = pl.BlockSpec((tk, tm), lambda i, j, k: (k, i))
    else:
        a_spec = pl.BlockSpec((tm, tk), lambda i, j, k: (i, k))
    if mode == "nt":
        b_spec = pl.BlockSpec((tn, tk), lambda i, j, k: (j, k))
    else:
        b_spec = pl.BlockSpec((tk, tn), lambda i, j, k: (k, j))
    dims = {"nn": (((1,), (0,)), ((), ())), "nt": (((1,), (1,)), ((), ())), "tn": (((0,), (0,)), ((), ()))}[mode]
    in_specs = [a_spec, b_spec]
    operands = [a, b]
    if bias is not None:
        in_specs.append(pl.BlockSpec((1, tn), lambda i, j, k: (0, j)))
        operands.append(bias)
    for e in extras:
        in_specs.append(pl.BlockSpec((tm, tn), lambda i, j, k: (i, j)))
        operands.append(e)
    n_ex = len(extras)
    for t in after:
        in_specs.append(pl.BlockSpec(memory_space=pl.ANY))
        operands.append(t)
    n_after = len(after)
    n_out = len(out_dtypes)
    out_t = tuple(out_t) if out_t is not None else (False,) * n_out
    has_bias = bias is not None

    def kern(*refs):
        a_ref, b_ref = refs[0], refs[1]
        pos = 2
        bias_ref = None
        if has_bias:
            bias_ref = refs[pos]
            pos += 1
        ex_refs = refs[pos:pos + n_ex]
        pos += n_ex + n_after
        out_refs = refs[pos:pos + n_out]
        acc_ref = refs[pos + n_out] if nk > 1 else None

        def finish(r):
            if has_bias:
                r = r + bias_ref[...]
            outs = epilogue(r, *[e[...] for e in ex_refs]) if epilogue is not None else (r,)
            for o_ref, o, tr_ in zip(out_refs, outs, out_t):
                o_ref[...] = (o.T if tr_ else o).astype(o_ref.dtype)

        part = lax.dot_general(a_ref[...].astype(BF16), b_ref[...].astype(BF16), dims, preferred_element_type=F32)
        if nk == 1:
            finish(part)
        else:
            k = pl.program_id(2)

            @pl.when(k == 0)
            def _():
                acc_ref[...] = part

            @pl.when(k > 0)
            def _():
                acc_ref[...] += part

            @pl.when(k == nk - 1)
            def _():
                finish(acc_ref[...])

    outs = pl.pallas_call(
        kern,
        name=name,
        grid=(M // tm, N // tn, nk),
        in_specs=in_specs,
        out_specs=[pl.BlockSpec((tn, tm), lambda i, j, k: (j, i)) if tr_ else pl.BlockSpec((tm, tn), lambda i, j, k: (i, j))
                   for tr_ in out_t],
        out_shape=[jax.ShapeDtypeStruct((N, M) if tr_ else (M, N), dt) for dt, tr_ in zip(out_dtypes, out_t)],
        scratch_shapes=[pltpu.VMEM((tm, tn), F32)] if nk > 1 else [],
        compiler_params=_cparams(("parallel", "parallel", "arbitrary")),
    )(*operands)
    return outs[0] if n_out == 1 else tuple(outs)


def _rowwise(fn, rows, vecs, outs, vec_outs, *, nrows, tr, name, after=()):
    n_rows, n_vecs, n_outs, n_after = len(rows), len(vecs), len(outs), len(after)
    in_specs = [pl.BlockSpec((tr, w), lambda i, cb=cb, ro=ro: (i + ro, cb)) for (_, w, cb, ro) in rows]
    in_specs += [pl.BlockSpec(v.shape, lambda i: (0, 0)) for v in vecs]
    in_specs += [pl.BlockSpec(memory_space=pl.ANY)] * n_after
    outs = [o if len(o) == 3 else (*o, False) for o in outs]
    out_specs = [pl.BlockSpec((w, tr), lambda i: (0, i)) if tr_ else pl.BlockSpec((tr, w), lambda i: (i, 0)) for (w, _, tr_) in outs]
    out_specs += [pl.BlockSpec(s, lambda i: (0, 0)) for s in vec_outs]
    out_shape = [jax.ShapeDtypeStruct((w, nrows) if tr_ else (nrows, w), dt) for (w, dt, tr_) in outs]
    out_tr = [tr_ for (_, _, tr_) in outs]
    out_shape += [jax.ShapeDtypeStruct(s, F32) for s in vec_outs]

    def kern(*refs):
        rvals = [r[...] for r in refs[:n_rows]]
        vvals = [r[...] for r in refs[n_rows:n_rows + n_vecs]]
        first_out = n_rows + n_vecs + n_after
        o_refs = refs[first_out:first_out + n_outs]
        v_refs = refs[first_out + n_outs:]
        ro, vo = fn(rvals, vvals)
        for r, val, tr_ in zip(o_refs, ro, out_tr):
            r[...] = (val.astype(F32).T if tr_ else val).astype(r.dtype)
        i = pl.program_id(0)
        for r, val in zip(v_refs, vo):
            @pl.when(i == 0)
            def _(r=r, val=val):
                r[...] = val.astype(F32)

            @pl.when(i > 0)
            def _(r=r, val=val):
                r[...] += val.astype(F32)

    res = pl.pallas_call(
        kern,
        name=name,
        grid=(nrows // tr,),
        in_specs=in_specs,
        out_specs=out_specs,
        out_shape=out_shape,
        compiler_params=_cparams(("arbitrary",)),
    )(*[r[0] for r in rows], *vecs, *after)
    return list(res)


def _ln(x):
    mu = jnp.mean(x, axis=-1, keepdims=True)
    xc = x - mu
    var = jnp.mean(xc * xc, axis=-1, keepdims=True)
    return xc * lax.rsqrt(var + LN_EPS)


def _sigmoid(x):
    return 1.0 / (1.0 + jnp.exp(-x))


def _gelu(x):
    return 0.5 * x * (1.0 + jnp.tanh(math.sqrt(2.0 / math.pi) * (x + 0.044715 * (x * x * x))))


def _silu(x):
    return x * _sigmoid(x)


def _f_ln_mod(x, sc, sh):
    return _ln(x) * (1.0 + sc) + sh


def _f_glu(z):
    return z[:, :SW] * _sigmoid(z[:, SW:])


def _f_mix(ga, gs, attn_d, ssm_d):
    return _sigmoid(ga) * attn_d + _sigmoid(gs) * ssm_d


def _f_post1(x, y, g1, lg, lb, sc2, sh2):
    r1 = ALPHA * x + g1 * y
    x1 = _ln(r1) * lg + lb
    h2 = _ln(x1) * (1.0 + sc2) + sh2
    return x1, h2


def _f_loss(x1, mlp, tgt, g2, lg, lb, b2z):
    r2 = ALPHA * x1 + g2 * (mlp + b2z)
    out = _ln(r2) * lg + lb
    err = out - tgt
    return 0.5 * jnp.sum(err * err) * (1.0 / D)


def _rope_tables():
    rows = T // GRID_W
    row = jnp.repeat(jnp.arange(rows), GRID_W)
    col = jnp.tile(jnp.arange(GRID_W), rows)
    n_freq = HD // 4
    freqs = ROPE_BASE ** (-jnp.arange(n_freq, dtype=F32) / n_freq)
    ang_r = row.astype(F32)[:, None] * freqs
    ang_c = col.astype(F32)[:, None] * freqs
    ang = jnp.concatenate([ang_r, ang_r, ang_c, ang_c], -1)
    cos, sin = jnp.cos(ang), jnp.sin(ang)
    lo = (jnp.arange(HD) % (HD // 2)) < (HD // 4)
    sin_a = jnp.where(lo[None, :], -sin, 0.0)
    sin_b = jnp.where(lo[None, :], 0.0, sin)
    return cos, sin_a, sin_b


def _rope(x, cos, sa, sb):
    return x * cos + pltpu.roll(x, 96, 1) * sa + pltpu.roll(x, 32, 1) * sb


def _rope_t(dy, cos, sa, sb):
    return dy * cos + pltpu.roll(dy * sa, 32, 1) + pltpu.roll(dy * sb, 96, 1)


BAND = 3 * WINDOW
KPAD = T + 2 * WINDOW


def _attn_fill_kv(k_ref, v_ref, cos_ref, sa_ref, sb_ref, kp, vp, kc, vc):
    zeros = jnp.zeros((WINDOW, KVW), BF16)
    kp[0:WINDOW, :] = zeros
    kp[WINDOW + T:KPAD, :] = zeros
    vp[0:WINDOW, :] = zeros
    vp[WINDOW + T:KPAD, :] = zeros
    for hh in range(NKV):
        cs = slice(hh * HD, (hh + 1) * HD)
        for r0 in range(0, T, 512):
            rs = slice(r0, r0 + 512)
            kr = _rope(k_ref[rs, cs], cos_ref[rs, :], sa_ref[rs, :], sb_ref[rs, :])
            kp[WINDOW + r0:WINDOW + r0 + 512, cs] = kr.astype(BF16)
    vp[WINDOW:WINDOW + T, :] = v_ref[0:T, :].astype(BF16)
    kc[...] = k_ref[T:TA, :].astype(BF16)
    vc[...] = v_ref[T:TA, :].astype(BF16)


GROWS = GROUP * WINDOW


def _attn_scores(n, kvh, q_ref, cos_ref, sa_ref, sb_ref, sink_ref, kp, kc):
    r0 = pl.multiple_of(n * WINDOW, WINDOW)
    cos = cos_ref[pl.ds(r0, WINDOW), :]
    sa = sa_ref[pl.ds(r0, WINDOW), :]
    sb = sb_ref[pl.ds(r0, WINDOW), :]
    heads = range(kvh * GROUP, (kvh + 1) * GROUP)
    q_g = jnp.concatenate([_rope(q_ref[:, h * HD:(h + 1) * HD], cos, sa, sb).astype(BF16) for h in heads], axis=0)
    kb = kp[pl.ds(r0, BAND), kvh * HD:(kvh + 1) * HD]
    kcb = kc[:, kvh * HD:(kvh + 1) * HD]
    nt = (((1,), (1,)), ((), ()))
    s_loc = lax.dot_general(q_g, kb, nt, preferred_element_type=F32) * ATT_SCALE
    s_ctx = lax.dot_general(q_g, kcb, nt, preferred_element_type=F32) * ATT_SCALE
    row = lax.broadcasted_iota(jnp.int32, (GROWS, BAND), 0) & (WINDOW - 1)
    col = lax.broadcasted_iota(jnp.int32, (GROWS, BAND), 1)
    rel = col - WINDOW - row
    kpos = r0 - WINDOW + col
    valid = (jnp.abs(rel) <= WINDOW) & (kpos >= 0) & (kpos < T)
    s_loc = jnp.where(valid, s_loc, NEG_INF)
    sk = jnp.concatenate([jnp.broadcast_to(sink_ref[0:1, h:h + 1], (WINDOW, 1)) for h in heads], axis=0)
    m = jnp.maximum(jnp.maximum(jnp.max(s_loc, -1, keepdims=True), jnp.max(s_ctx, -1, keepdims=True)), sk)
    e_loc = jnp.exp(s_loc - m)
    e_ctx = jnp.exp(s_ctx - m)
    e_sink = jnp.exp(sk - m)
    inv = 1.0 / (jnp.sum(e_loc, -1, keepdims=True) + jnp.sum(e_ctx, -1, keepdims=True) + e_sink)
    return q_g, r0, e_loc * inv, e_ctx * inv, e_sink * inv


def _attn_fwd(proj, sink, tabs):
    cos, sa, sb = tabs

    def kern(q_ref, k_ref, v_ref, cos_ref, sa_ref, sb_ref, sink_ref, o_ref, kp, vp, kc, vc):
        n = pl.program_id(0)

        @pl.when(n == 0)
        def _():
            _attn_fill_kv(k_ref, v_ref, cos_ref, sa_ref, sb_ref, kp, vp, kc, vc)

        for kvh in range(NKV):
            _, r0, p_loc, p_ctx, _ = _attn_scores(n, kvh, q_ref, cos_ref, sa_ref, sb_ref, sink_ref, kp, kc)
            vb = vp[pl.ds(r0, BAND), kvh * HD:(kvh + 1) * HD]
            vcb = vc[:, kvh * HD:(kvh + 1) * HD]
            o = jnp.dot(p_loc.astype(BF16), vb, preferred_element_type=F32)
            o = o + jnp.dot(p_ctx.astype(BF16), vcb, preferred_element_type=F32)
            for g in range(GROUP):
                h = kvh * GROUP + g
                o_ref[:, h * HD:(h + 1) * HD] = o[g * WINDOW:(g + 1) * WINDOW, :].astype(o_ref.dtype)

    full = lambda shape: pl.BlockSpec(shape, lambda n: (0, 0))
    return pl.pallas_call(
        kern,
        name="attn_fwd",
        grid=(T // WINDOW,),
        in_specs=[
            pl.BlockSpec((WINDOW, QW), lambda n: (n, 0)),
            pl.BlockSpec((TA, KVW), lambda n: (0, QW // KVW)),
            pl.BlockSpec((TA, KVW), lambda n: (0, QW // KVW + 1)),
            full((T, HD)), full((T, HD)), full((T, HD)), full((1, NH)),
        ],
        out_specs=pl.BlockSpec((WINDOW, QW), lambda n: (n, 0)),
        out_shape=jax.ShapeDtypeStruct((T, QW), BF16),
        scratch_shapes=[pltpu.VMEM((KPAD, KVW), BF16), pltpu.VMEM((KPAD, KVW), BF16),
                        pltpu.VMEM((C, KVW), BF16), pltpu.VMEM((C, KVW), BF16)],
        compiler_params=_cparams(("arbitrary",)),
    )(proj, proj, proj, cos, sa, sb, sink)


def _attn_bwd(proj, d_attn, sink, tabs):
    cos, sa, sb = tabs
    n_blocks = T // WINDOW

    def kern(q_ref, k_ref, v_ref, do_ref, cos_ref, sa_ref, sb_ref, sink_ref,
             dq_ref, dk_ref, dv_ref, dsink_ref, kp, vp, kc, vc, dkp, dvp, dkc, dvc):
        n = pl.program_id(0)

        @pl.when(n == 0)
        def _():
            _attn_fill_kv(k_ref, v_ref, cos_ref, sa_ref, sb_ref, kp, vp, kc, vc)
            dkp[...] = jnp.zeros_like(dkp)
            dvp[...] = jnp.zeros_like(dvp)
            dkc[...] = jnp.zeros_like(dkc)
            dvc[...] = jnp.zeros_like(dvc)
            dsink_ref[...] = jnp.zeros_like(dsink_ref)

        nt = (((1,), (1,)), ((), ()))
        tn = (((0,), (0,)), ((), ()))
        for kvh in range(NKV):
            cs = slice(kvh * HD, (kvh + 1) * HD)
            heads = range(kvh * GROUP, (kvh + 1) * GROUP)
            q_g, r0, p_loc, p_ctx, p_sink = _attn_scores(n, kvh, q_ref, cos_ref, sa_ref, sb_ref, sink_ref, kp, kc)
            kb = kp[pl.ds(r0, BAND), cs]
            vb = vp[pl.ds(r0, BAND), cs]
            kcb = kc[:, cs]
            vcb = vc[:, cs]
            do_g = jnp.concatenate([do_ref[:, h * HD:(h + 1) * HD] for h in heads], axis=0)
            dp_loc = lax.dot_general(do_g, vb, nt, preferred_element_type=F32)
            dp_ctx = lax.dot_general(do_g, vcb, nt, preferred_element_type=F32)
            delta = jnp.sum(p_loc * dp_loc, -1, keepdims=True) + jnp.sum(p_ctx * dp_ctx, -1, keepdims=True)
            ds_loc = (p_loc * (dp_loc - delta) * ATT_SCALE).astype(BF16)
            ds_ctx = (p_ctx * (dp_ctx - delta) * ATT_SCALE).astype(BF16)
            dq = jnp.dot(ds_loc, kb, preferred_element_type=F32) + jnp.dot(ds_ctx, kcb, preferred_element_type=F32)
            cos = cos_ref[pl.ds(r0, WINDOW), :]
            sa_ = sa_ref[pl.ds(r0, WINDOW), :]
            sb_ = sb_ref[pl.ds(r0, WINDOW), :]
            dkp[pl.ds(r0, BAND), cs] += lax.dot_general(ds_loc, q_g, tn, preferred_element_type=F32)
            dkc[:, cs] += lax.dot_general(ds_ctx, q_g, tn, preferred_element_type=F32)
            dvp[pl.ds(r0, BAND), cs] += lax.dot_general(p_loc.astype(BF16), do_g, tn, preferred_element_type=F32)
            dvc[:, cs] += lax.dot_general(p_ctx.astype(BF16), do_g, tn, preferred_element_type=F32)
            dsk_rows = p_sink * delta
            for g, h in enumerate(heads):
                rs = slice(g * WINDOW, (g + 1) * WINDOW)
                dq_ref[:, h * HD:(h + 1) * HD] = _rope_t(dq[rs, :], cos, sa_, sb_).astype(dq_ref.dtype)
                dsk = -jnp.sum(dsk_rows[rs, :], axis=0, keepdims=True)
                dsink_ref[h:h + 1, :] += jnp.broadcast_to(dsk, (1, HD))

        @pl.when(n == n_blocks - 1)
        def _():
            for hh in range(NKV):
                cs = slice(hh * HD, (hh + 1) * HD)
                for r0 in range(0, T, 512):
                    rs = slice(r0, r0 + 512)
                    g = dkp[WINDOW + r0:WINDOW + r0 + 512, cs]
                    dk_ref[rs, cs] = _rope_t(g, cos_ref[rs, :], sa_ref[rs, :], sb_ref[rs, :]).astype(dk_ref.dtype)
            dk_ref[T:TA, :] = dkc[...].astype(dk_ref.dtype)
            dv_ref[0:T, :] = dvp[WINDOW:WINDOW + T, :].astype(dv_ref.dtype)
            dv_ref[T:TA, :] = dvc[...].astype(dv_ref.dtype)

    full = lambda shape: pl.BlockSpec(shape, lambda n: (0, 0))
    return pl.pallas_call(
        kern,
        name="attn_bwd",
        grid=(n_blocks,),
        in_specs=[
            pl.BlockSpec((WINDOW, QW), lambda n: (n, 0)),
            pl.BlockSpec((TA, KVW), lambda n: (0, QW // KVW)),
            pl.BlockSpec((TA, KVW), lambda n: (0, QW // KVW + 1)),
            pl.BlockSpec((WINDOW, QW), lambda n: (n, 0)),
            full((T, HD)), full((T, HD)), full((T, HD)), full((1, NH)),
        ],
        out_specs=[pl.BlockSpec((WINDOW, QW), lambda n: (n, 0)), full((TA, KVW)), full((TA, KVW)), full((NH, HD))],
        out_shape=[jax.ShapeDtypeStruct((T, QW), BF16), jax.ShapeDtypeStruct((TA, KVW), BF16),
                   jax.ShapeDtypeStruct((TA, KVW), BF16), jax.ShapeDtypeStruct((NH, HD), F32)],
        scratch_shapes=[pltpu.VMEM((KPAD, KVW), BF16), pltpu.VMEM((KPAD, KVW), BF16),
                        pltpu.VMEM((C, KVW), BF16), pltpu.VMEM((C, KVW), BF16),
                        pltpu.VMEM((KPAD, KVW), F32), pltpu.VMEM((KPAD, KVW), F32),
                        pltpu.VMEM((C, KVW), F32), pltpu.VMEM((C, KVW), F32)],
        compiler_params=_cparams(("arbitrary",)),
    )(proj, proj, proj, d_attn, cos, sa, sb, sink)


def _s5_prep(a_re, a_im, log_dt, b_re, b_im, c_re, c_im):
    lam = lax.complex(a_re, a_im)
    dt = jnp.exp(log_dt)[..., None]
    lam_bar = jnp.exp(lam * dt)
    b_bar = ((lam_bar - 1.0) / lam)[..., None] * lax.complex(b_re, b_im)
    def lam_rows(v):
        return v.reshape(2, NBLK, 1, BW)

    lam_l = jnp.concatenate([lam_rows(jnp.real(lam_bar)), lam_rows(jnp.imag(lam_bar))], -1)
    lam_l = jnp.broadcast_to(lam_l, (2, NBLK, 8, 2 * BW))
    diag = (jnp.arange(UW)[:, None] // SG) == (jnp.arange(BW)[None, :] // SP)

    def blocks(v):
        return jnp.where(diag, jnp.tile(v.reshape(2, NBLK, UW, SP), (1, 1, 1, GBLK)), 0.0)

    b_t = jnp.swapaxes(b_bar, -1, -2)
    bmat = jnp.concatenate([blocks(jnp.real(b_t)), blocks(jnp.imag(b_t))], -1)
    cmat = jnp.concatenate([blocks(c_re), -blocks(c_im)], -1)
    return lam_l, bmat, cmat


def _cmul(ar, ai, br, bi):
    return ar * br - ai * bi, ar * bi + ai * br


def _shift_rows(x, rev, fill):
    r = lax.broadcasted_iota(jnp.int32, x.shape, 0)
    down = jnp.where(r == 0, fill, pltpu.roll(x, 1, 0))
    up = jnp.where(r == NSEG - 1, fill, pltpu.roll(x, NSEG - 1, 0))
    return jnp.where(rev == 0, down, up)


def _edge_row(x, rev):
    last = jnp.broadcast_to(x[NSEG - 1:NSEG, :], x.shape)
    first = jnp.broadcast_to(x[0:1, :], x.shape)
    return jnp.where(rev == 0, last, first)


def _seg_scan(get, put, base, seglen, lr, li, rev, cin):
    zero = jnp.zeros((NSEG, BW), F32)

    def rows(k):
        j = jnp.where(rev == 0, k, seglen - 1 - k)
        return pl.ds(pl.multiple_of(base + j * NSEG, NSEG), NSEG)

    def local(k, carry):
        sr, si = carry
        xr, xi = get(rows(k))
        tr, ti = _cmul(lr, li, sr, si)
        sr, si = tr + xr, ti + xi
        put(rows(k), sr, si)
        return sr, si

    er, ei = lax.fori_loop(0, seglen, local, (zero, zero))
    lpr, lpi = lr, li
    assert seglen & (seglen - 1) == 0, seglen
    for _ in range(seglen.bit_length() - 1):
        lpr, lpi = _cmul(lpr, lpi, lpr, lpi)
    cr, ci = _shift_rows(zero, rev, cin[0]), _shift_rows(zero, rev, cin[1])
    for _ in range(NSEG - 1):
        tr, ti = _cmul(lpr, lpi, cr, ci)
        cr, ci = _shift_rows(er + tr, rev, cin[0]), _shift_rows(ei + ti, rev, cin[1])

    def fix(k, carry):
        tr, ti = _cmul(lr, li, carry[0], carry[1])
        xr, xi = get(rows(k))
        put(rows(k), xr + tr, xi + ti)
        return tr, ti

    tr, ti = lax.fori_loop(0, seglen, fix, (cr, ci))
    return _edge_row(er + tr, rev), _edge_row(ei + ti, rev)


RCH = 256
CSEG = C // NSEG
TSEG = T // NSEG
UCOL0 = (QW + 2 * KVW) // UW


REGIONS = ((0, TSEG), (T, CSEG))


def _state_access(ref, lead=()):
    def get(rows):
        return ref[(*lead, rows, slice(0, BW))], ref[(*lead, rows, slice(BW, 2 * BW))]

    def put(rows, re, im):
        ref[(*lead, rows, slice(0, BW))] = re
        ref[(*lead, rows, slice(BW, 2 * BW))] = im

    return get, put


def _interleave_rows(src_ref, dst_ref, regions=REGIONS):
    for base, seglen in regions:
        def body(j, carry, base=base, seglen=seglen):
            dst_ref[pl.ds(pl.multiple_of(base + j * NSEG, NSEG), NSEG), :] = src_ref[pl.ds(base + j, NSEG, stride=seglen), :]
            return carry

        lax.fori_loop(0, seglen, body, 0, unroll=8)


def _deinterleave_rows(src_ref, dst_ref, regions=REGIONS):
    for base, seglen in regions:
        def body(j, carry, base=base, seglen=seglen):
            dst_ref[pl.ds(base + j, NSEG, stride=seglen), :] = src_ref[pl.ds(pl.multiple_of(base + j * NSEG, NSEG), NSEG), :]
            return carry

        lax.fori_loop(0, seglen, body, 0, unroll=8)


def _s5_fwd(proj, dskip, lam, bmat, cmat):
    def kern(u_ref, dk_ref, lam_ref, b_ref, c_ref, s_ref, ssm_ref, ge_ref, up_ref, yp_ref):
        d = pl.program_id(1)

        @pl.when(d == 0)
        def _():
            _interleave_rows(u_ref, up_ref)

        bm = b_ref[0, 0].astype(BF16)
        for r0 in range(0, TA, RCH):
            s_ref[0, 0, r0:r0 + RCH, :] = jnp.dot(up_ref[r0:r0 + RCH, :].astype(BF16), bm, preferred_element_type=F32)
        lr = lam_ref[0, 0, :, 0:BW]
        li = lam_ref[0, 0, :, BW:2 * BW]
        zero = jnp.zeros((NSEG, BW), F32)
        get, put = _state_access(s_ref, (0, 0))
        mid = _seg_scan(get, put, T, CSEG, lr, li, d, (zero, zero))
        _seg_scan(get, put, 0, TSEG, lr, li, d, mid)
        cm = c_ref[0, 0].astype(BF16)
        for r0 in range(0, T, RCH):
            y = lax.dot_general(s_ref[0, 0, r0:r0 + RCH, :].astype(BF16), cm, (((1,), (1,)), ((), ())), preferred_element_type=F32)

            @pl.when(d == 0)
            def _(y=y, r0=r0):
                yp_ref[r0:r0 + RCH, :] = y + dk_ref[...] * up_ref[r0:r0 + RCH, :]

            @pl.when(d == 1)
            def _(y=y, r0=r0):
                yp_ref[r0:r0 + RCH, :] += y

        @pl.when(d == 1)
        def _():
            _deinterleave_rows(yp_ref, ssm_ref, REGIONS[:1])
            for r0 in range(0, T, RCH):
                ge_ref[r0:r0 + RCH, :] = _gelu(ssm_ref[r0:r0 + RCH, :]).astype(ge_ref.dtype)

    blk4 = lambda shape: pl.BlockSpec((1, 1) + shape, lambda b, d: (d, b, 0, 0))
    return pl.pallas_call(
        kern,
        name="s5_fwd",
        grid=(NBLK, 2),
        in_specs=[pl.BlockSpec((TA, UW), lambda b, d: (0, UCOL0 + b)), pl.BlockSpec((1, UW), lambda b, d: (0, b)),
                  blk4((8, 2 * BW)), blk4((UW, 2 * BW)), blk4((UW, 2 * BW))],
        out_specs=[blk4((TA, 2 * BW)), pl.BlockSpec((T, UW), lambda b, d: (0, b)), pl.BlockSpec((T, UW), lambda b, d: (0, b))],
        out_shape=[jax.ShapeDtypeStruct((2, NBLK, TA, 2 * BW), F32), jax.ShapeDtypeStruct((T, SW), F32),
                   jax.ShapeDtypeStruct((T, SW), BF16)],
        scratch_shapes=[pltpu.VMEM((TA, UW), F32), pltpu.VMEM((T, UW), F32)],
        compiler_params=_cparams(("parallel", "arbitrary")),
    )(proj, dskip, lam, bmat, cmat)


def _s5_bwd(d_ge, ssm, proj, dskip, states, lam, bmat, cmat):
    nt = (((1,), (1,)), ((), ()))
    tn = (((0,), (0,)), ((), ()))

    def kern(dge_ref, ssm_ref, u_ref, dk_ref, s_ref, lam_ref, b_ref, c_ref,
             du_ref, ddk_ref, dlam_ref, db_ref, dc_ref, g_ref, dua_ref, dssm_ref, up_ref, nat_ref):
        d = pl.program_id(1)

        @pl.when(d == 0)
        def _():
            ddk = jnp.zeros((1, UW), F32)
            for r0 in range(0, T, RCH):
                rs = slice(r0, r0 + RCH)
                _, pull = jax.vjp(_gelu, ssm_ref[rs, :])
                dssm = pull(dge_ref[rs, :])[0]
                nat_ref[rs, :] = dssm
                ddk = ddk + jnp.sum(dssm * u_ref[rs, :], axis=0, keepdims=True)
            ddk_ref[...] = ddk
            _interleave_rows(nat_ref, dssm_ref, REGIONS[:1])
            _interleave_rows(u_ref, up_ref)
            for r0 in range(0, T, RCH):
                dua_ref[r0:r0 + RCH, :] = dssm_ref[r0:r0 + RCH, :] * dk_ref[...]
            dua_ref[T:TA, :] = jnp.zeros((C, UW), F32)

        cm = c_ref[0, 0].astype(BF16)
        for r0 in range(0, T, RCH):
            g_ref[r0:r0 + RCH, :] = jnp.dot(dssm_ref[r0:r0 + RCH, :].astype(BF16), cm, preferred_element_type=F32)
        g_ref[T:TA, :] = jnp.zeros((C, 2 * BW), F32)
        lr = lam_ref[0, 0, :, 0:BW]
        li = lam_ref[0, 0, :, BW:2 * BW]
        zero = jnp.zeros((NSEG, BW), F32)
        get_g, put_g = _state_access(g_ref)

        mid = _seg_scan(get_g, put_g, 0, TSEG, lr, -li, 1 - d, (zero, zero))
        _seg_scan(get_g, put_g, T, CSEG, lr, -li, 1 - d, mid)

        get_s, _ = _state_access(s_ref, (0, 0))

        def dlam_terms(g, s):
            return g[0] * s[0] + g[1] * s[1], g[1] * s[0] - g[0] * s[1]

        def dlam_region(base, seglen, s_in, acc):
            def rows(j):
                return pl.ds(pl.multiple_of(base + j * NSEG, NSEG), NSEG)

            def inner(k, acc):
                j = jnp.where(d == 0, k, seglen - 1 - k)
                jp = jnp.where(d == 0, k - 1, seglen - k)
                t = dlam_terms(get_g(rows(j)), get_s(rows(jp)))
                return acc[0] + t[0], acc[1] + t[1]

            acc = lax.fori_loop(1, seglen, inner, acc)
            jb = jnp.where(d == 0, 0, seglen - 1)
            jn = jnp.where(d == 0, seglen - 1, 0)
            sp = get_s(rows(jn))
            t = dlam_terms(get_g(rows(jb)), (_shift_rows(sp[0], d, s_in[0]), _shift_rows(sp[1], d, s_in[1])))
            return acc[0] + t[0], acc[1] + t[1]

        r_mid = jnp.where(d == 0, TA - 1, T)
        s_mid = tuple(jnp.broadcast_to(t, (NSEG, BW)) for t in get_s(pl.ds(r_mid, 1)))
        acc = dlam_region(T, CSEG, (zero, zero), (zero, zero))
        acc = dlam_region(0, TSEG, s_mid, acc)
        dlam_ref[0, 0, :, 0:BW] = acc[0]
        dlam_ref[0, 0, :, BW:2 * BW] = acc[1]

        bm = b_ref[0, 0].astype(BF16)
        db = jnp.zeros((UW, 2 * BW), F32)
        dc = jnp.zeros((UW, 2 * BW), F32)
        for r0 in range(0, TA, RCH):
            rs = slice(r0, r0 + RCH)
            g = g_ref[rs, :].astype(BF16)
            dua_ref[rs, :] += lax.dot_general(g, bm, nt, preferred_element_type=F32)
            db = db + lax.dot_general(up_ref[rs, :].astype(BF16), g, tn, preferred_element_type=F32)
            if r0 < T:
                dc = dc + lax.dot_general(dssm_ref[rs, :].astype(BF16), s_ref[0, 0, rs, :].astype(BF16), tn,
                                          preferred_element_type=F32)
        db_ref[0, 0] = db
        dc_ref[0, 0] = dc

        @pl.when(d == 1)
        def _():
            _deinterleave_rows(dua_ref, nat_ref)
            du_ref[...] = nat_ref[...].astype(du_ref.dtype)

    blk4 = lambda shape: pl.BlockSpec((1, 1) + shape, lambda b, d: (d, b, 0, 0))
    lat = pl.BlockSpec((T, UW), lambda b, d: (0, b))
    vec = pl.BlockSpec((1, UW), lambda b, d: (0, b))
    return pl.pallas_call(
        kern,
        name="s5_bwd",
        grid=(NBLK, 2),
        in_specs=[lat, lat, pl.BlockSpec((TA, UW), lambda b, d: (0, UCOL0 + b)), vec,
                  blk4((TA, 2 * BW)), blk4((8, 2 * BW)), blk4((UW, 2 * BW)), blk4((UW, 2 * BW))],
        out_specs=[pl.BlockSpec((TA, UW), lambda b, d: (0, b)), vec, blk4((8, 2 * BW)), blk4((UW, 2 * BW)), blk4((UW, 2 * BW))],
        out_shape=[jax.ShapeDtypeStruct((TA, SW), BF16), jax.ShapeDtypeStruct((1, SW), F32),
                   jax.ShapeDtypeStruct((2, NBLK, 8, 2 * BW), F32),
                   jax.ShapeDtypeStruct((2, NBLK, UW, 2 * BW), F32), jax.ShapeDtypeStruct((2, NBLK, UW, 2 * BW), F32)],
        scratch_shapes=[pltpu.VMEM((TA, 2 * BW), F32), pltpu.VMEM((TA, UW), F32), pltpu.VMEM((T, UW), F32),
                        pltpu.VMEM((TA, UW), F32), pltpu.VMEM((TA, UW), F32)],
        compiler_params=_cparams(("parallel", "arbitrary")),
    )(d_ge, ssm, proj, dskip, states, lam, bmat, cmat)


TR = 256


def _vjp_rows(f, primals, cots, n_row):
    _, pull = jax.vjp(f, *primals)
    g = pull(cots)
    return list(g[:n_row]), list(g[n_row:])


class _GradDict(dict):
    def __init__(self, on_set=None):
        super().__init__()
        self._on_set = on_set
        self.tokens = {}

    def __setitem__(self, key, value):
        super().__setitem__(key, value)
        if self._on_set is not None:
            self._on_set(self)

    def order(self, key):
        return self.tokens.get(key, self.get(key))

    def finish(self, key, after):
        if self.on_finish is None:
            return ()
        return (self.on_finish(key, after),)

    on_finish = None


def _local_step(x, ctx, tgt, mod_lat, mod_ctx, wb, sp, on_grad=None, on_loss=None, on_finish=None, on_early=None):
    sh1, sc1, g1, sh2, sc2, g2 = [mod_lat[:, i * D:(i + 1) * D] for i in range(6)]
    csh1, csc1 = mod_ctx[:, 0:D], mod_ctx[:, D:2 * D]
    tabs = _rope_tables()
    sink = sp["attn_sink"].reshape(1, NH)
    dskip = sp["ssm_d"].reshape(1, SW)
    lg_mix, lb_mix = sp["ln_mix_g"].reshape(1, D), sp["ln_mix_b"].reshape(1, D)
    lg_mlp, lb_mlp = sp["ln_mlp_g"].reshape(1, D), sp["ln_mlp_b"].reshape(1, D)
    b1, b2 = sp["b_mlp1"].reshape(1, DFF), sp["b_mlp2"].reshape(1, D)
    s5_names = ("ssm_a_re", "ssm_a_im", "ssm_log_dt", "ssm_b_re", "ssm_b_im", "ssm_c_re", "ssm_c_im")
    (lam, bmat, cmat), s5_pull = jax.vjp(_s5_prep, *[sp[n] for n in s5_names])

    def ln_mod2(rv, vv):
        h = _f_ln_mod(rv[0], vv[0], vv[1])
        return [h, h], []

    h_lat, h_lat_t = _rowwise(ln_mod2, [(x, D, 0, 0)], [sc1, sh1], [(D, BF16), (D, BF16, True)], [], nrows=T, tr=TR, name="ln1_lat")
    h_ctx, h_ctx_t = _rowwise(ln_mod2, [(ctx, D, 0, 0)], [csc1, csh1], [(D, BF16), (D, BF16, True)], [], nrows=C, tr=TR,
                              name="ln1_ctx")
    h1 = jnp.concatenate([h_lat, h_ctx], 0)
    h1_t = jnp.concatenate([h_lat_t, h_ctx_t], 1)
    proj = _matmul(h1, wb["w_in"], mode="nn", name="proj", tm=768, tn=512)
    attn = _attn_fwd(proj, sink, tabs)
    states, ssm, ge = _s5_fwd(proj, dskip, lam, bmat, cmat)
    z = _matmul(ge, wb["w_glu"], mode="nn", name="glu_mm", tm=1024, tn=1024)

    def glu_act(rv, vv):
        return [_f_glu(rv[0])], []

    glu, = _rowwise(glu_act, [(z, 2 * SW, 0, 0)], [], [(SW, BF16)], [], nrows=T, tr=TR, name="glu_act")
    attn_d = _matmul(attn, wb["w_attn_up"], mode="nn", name="attn_up", tm=1024, tn=512)
    ssm_d = _matmul(glu, wb["w_ssm_up"], mode="nn", name="ssm_up", tm=1024, tn=512)
    ga_cb, gs_cb = (QW + 2 * KVW + SW) // D, (QW + 2 * KVW + SW) // D + 1

    def mix(rv, vv):
        m_ = _f_mix(*rv)
        return [m_, m_], []

    mixv, mix_t = _rowwise(mix, [(proj, D, ga_cb, 0), (proj, D, gs_cb, 0), (attn_d, D, 0, 0), (ssm_d, D, 0, 0)], [],
                           [(D, BF16), (D, BF16, True)], [], nrows=T, tr=TR, name="mix")
    y = _matmul(mixv, wb["w_out"], mode="nn", name="out_proj", tm=1024, tn=512)

    def post1(rv, vv):
        x1, h2 = _f_post1(rv[0], rv[1], *vv)
        return [x1, h2, h2], []

    x1, h2, h2_t = _rowwise(post1, [(x, D, 0, 0), (y, D, 0, 0)], [g1, lg_mix, lb_mix, sc2, sh2],
                            [(D, F32), (D, BF16), (D, BF16, True)], [], nrows=T, tr=TR, name="post1")

    def relu_sq(acc):
        r = jnp.maximum(acc, 0.0)
        return r, r * r, r * r

    r_act, act, act_t = _matmul(h2, wb["w_mlp1"], mode="nn", name="mlp1", tm=1024, tn=512, bias=b1,
                                out_dtypes=(BF16, BF16, BF16), out_t=(False, False, True), epilogue=relu_sq)
    mlp = _matmul(act, wb["w_mlp2"], mode="nn", name="mlp2", tm=1024, tn=512, tk=2048)

    def loss_fb(rv, vv):
        x1_t, mlp_t, tgt_t = rv
        g2_v, lg_v, lb_v, b2_v = vv
        f = lambda a, m, g, p, q, b: _f_loss(a, m, tgt_t, g, p, q, b)
        val, grads = jax.value_and_grad(f, argnums=(0, 1, 2, 3, 4, 5))(x1_t, mlp_t, g2_v, lg_v, lb_v, b2_v)
        dx1, dmlp, dg2, dlg, dlb, db2 = grads
        return [dx1, dmlp], [jnp.reshape(val, (1, 1)), dg2, dlg, dlb, db2]

    dx1_a, d_mlp, loss_p, d_g2, d_lg_mlp, d_lb_mlp, d_b2 = _rowwise(
        loss_fb, [(x1, D, 0, 0), (mlp, D, 0, 0), (tgt, D, 0, 0)], [g2, lg_mlp, lb_mlp, b2],
        [(D, F32), (D, BF16)], [(1, 1), (1, D), (1, D), (1, D), (1, D)], nrows=T, tr=TR, name="loss_fb")

    gw = _GradDict(on_grad)
    gw.on_finish = on_finish
    loss_done = () if on_loss is None else (on_loss(loss_p),)
    gw["w_mlp2"] = _matmul(act_t, d_mlp, mode="nn", name="dw_mlp2", out_dtypes=(BF16,), tm=1024, tn=512, after=loss_done)
    da, = (_matmul(d_mlp, wb["w_mlp2"], mode="nt", name="d_act", out_dtypes=(BF16,), tm=1024, tn=512,
                   extras=(r_act,), epilogue=lambda acc, r: (acc * (2.0 * r.astype(F32)),), after=(gw.order("w_mlp2"),)),)
    pin = gw.finish("w_mlp2", da)
    ones = jnp.ones((8, T), BF16)
    d_b1 = _matmul(ones, da, mode="nn", name="db_mlp1", tm=8, tn=2048)[0:1]
    gw["w_mlp1"] = _matmul(h2_t, da, mode="nn", name="dw_mlp1", out_dtypes=(BF16,), tm=1024, tn=512, after=pin)
    dh2 = _matmul(da, wb["w_mlp1"], mode="nt", name="d_h2", tm=1024, tn=512, tk=2048, after=(gw.order("w_mlp1"),))

    def post1_b(rv, vv):
        x_t, y_t, dx1_t, dh2_t = rv
        gr, gv = _vjp_rows(_f_post1, (x_t, y_t, *vv), (dx1_t, dh2_t), 2)
        return [gr[0], gr[1]], gv

    dx_a, dy, d_g1, d_lg_mix, d_lb_mix, d_sc2, d_sh2 = _rowwise(
        post1_b, [(x, D, 0, 0), (y, D, 0, 0), (dx1_a, D, 0, 0), (dh2, D, 0, 0)], [g1, lg_mix, lb_mix, sc2, sh2],
        [(D, F32), (D, BF16)], [(1, D)] * 5, nrows=T, tr=TR, name="post1_bwd")
    gw["w_out"] = _matmul(mix_t, dy, mode="nn", name="dw_out", out_dtypes=(BF16,), tm=1024, tn=512)
    dmix = _matmul(dy, wb["w_out"], mode="nt", name="d_mix", tm=1024, tn=512, after=(gw.order("w_out"),))

    def mix_b(rv, vv):
        gr, _ = _vjp_rows(_f_mix, tuple(rv[:4]), rv[4], 4)
        return gr, []

    d_ga, d_gs, d_attn_d, d_ssm_d = _rowwise(
        mix_b, [(proj, D, ga_cb, 0), (proj, D, gs_cb, 0), (attn_d, D, 0, 0), (ssm_d, D, 0, 0), (dmix, D, 0, 0)], [],
        [(D, BF16)] * 4, [], nrows=T, tr=TR, name="mix_bwd")
    pin = gw.finish("w_mlp1", d_ga)
    gw["w_attn_up"] = _matmul(attn, d_attn_d, mode="tn", name="dw_attn_up", out_dtypes=(BF16,), tm=512, tn=1024, tk=1024, after=pin)
    d_attn = _matmul(d_attn_d, wb["w_attn_up"], mode="nt", name="d_attn", out_dtypes=(BF16,), tm=1024, tn=512)
    gw["w_ssm_up"] = _matmul(glu, d_ssm_d, mode="tn", name="dw_ssm_up", out_dtypes=(BF16,), tm=512, tn=1024, tk=1024)
    d_glu = _matmul(d_ssm_d, wb["w_ssm_up"], mode="nt", name="d_glu", tm=1024, tn=512, after=(gw.order("w_attn_up"), gw.order("w_ssm_up")))

    def glu_b(rv, vv):
        gr, _ = _vjp_rows(_f_glu, (rv[0],), rv[1], 1)
        return gr, []

    dz, = _rowwise(glu_b, [(z, 2 * SW, 0, 0), (d_glu, SW, 0, 0)], [], [(2 * SW, BF16)], [], nrows=T, tr=TR, name="glu_bwd")
    gw["w_glu"] = _matmul(ge, dz, mode="tn", name="dw_glu", out_dtypes=(BF16,), tm=512, tn=1024, tk=1024)
    d_ge = _matmul(dz, wb["w_glu"], mode="nt", name="d_ge", tm=1024, tn=512, after=(gw.order("w_glu"),))

    du_all, d_dskip, dlam, dbmat, dcmat = _s5_bwd(d_ge, ssm, proj, dskip, states, lam, bmat, cmat)
    s5_grads = s5_pull((dlam, dbmat, dcmat))
    early = dict(zip(s5_names, s5_grads), ssm_d=d_dskip)
    if on_early is not None:
        on_early(early)
    pin = gw.finish("w_glu", du_all)

    dq, dk, dv, dsink = _attn_bwd(proj, d_attn, sink, tabs)
    zc = lambda w: jnp.zeros((C, w), BF16)
    dproj = jnp.concatenate([
        jnp.concatenate([dq, zc(QW)], 0), dk, dv, du_all,
        jnp.concatenate([d_ga, zc(D)], 0), jnp.concatenate([d_gs, zc(D)], 0)], 1)
    gw["w_in"] = _matmul(h1_t, dproj, mode="nn", name="dw_in", out_dtypes=(BF16,), tm=1024, tn=512, after=pin)
    pin = gw.finish("w_in", gw["w_in"])
    dh1 = _matmul(dproj, wb["w_in"], mode="nt", name="d_h1", tm=768, tn=512, tk=2048, after=pin)

    def ln1_b(rv, vv):
        x_t, dh_t, dxa_t = rv
        gr, gv = _vjp_rows(_f_ln_mod, (x_t, vv[0], vv[1]), dh_t, 1)
        return [gr[0] + dxa_t], gv

    grad_x, d_sc1, d_sh1 = _rowwise(ln1_b, [(x, D, 0, 0), (dh1, D, 0, 0), (dx_a, D, 0, 0)], [sc1, sh1],
                                    [(D, F32)], [(1, D), (1, D)], nrows=T, tr=TR, name="ln1_lat_bwd")

    def ln1c_b(rv, vv):
        _, gv = _vjp_rows(_f_ln_mod, (rv[0], vv[0], vv[1]), rv[1], 1)
        return [], gv

    d_csc1, d_csh1 = _rowwise(ln1c_b, [(ctx, D, 0, 0), (dh1, D, 0, T // TR)], [csc1, csh1],
                              [], [(1, D), (1, D)], nrows=C, tr=TR, name="ln1_ctx_bwd")

    d_mod_lat = jnp.concatenate([d_sh1, d_sc1, d_g1, d_sh2, d_sc2, d_g2], 1)
    zv = jnp.zeros((1, D), F32)
    d_mod_ctx = jnp.concatenate([d_csh1, d_csc1, zv, zv, zv, zv], 1)
    gs = {n: g for n, g in zip(s5_names, s5_grads)}
    gs["attn_sink"] = dsink[:, 0]
    gs["ssm_d"] = d_dskip
    gs["ln_mix_g"], gs["ln_mix_b"] = d_lg_mix, d_lb_mix
    gs["ln_mlp_g"], gs["ln_mlp_b"] = d_lg_mlp, d_lb_mlp
    gs["b_mlp1"], gs["b_mlp2"] = d_b1, d_b2
    return loss_p, grad_x, d_mod_lat, d_mod_ctx, gw, gs


def _my_pos():
    return lax.axis_index("x"), lax.axis_index("y"), lax.axis_index("c")


def _flip(p, bit):
    return 1 - p if bit else p


def _peer(pos, k):
    x, y, c = pos
    return (_flip(x, (k >> 2) & 1), _flip(y, (k >> 1) & 1), _flip(c, k & 1))


def _lin(pos):
    return 4 * pos[0] + 2 * pos[1] + pos[2]


def _allgather_small(v, name):
    r, w = v.shape

    def body(v_ref, out_ref, send_sems, recv_sems, local_sem):
        me = _my_pos()
        mine = pltpu.make_async_copy(v_ref, out_ref.at[_lin(me)], local_sem)
        mine.start()
        sends = []
        for k in range(1, N_DEV):
            cp = pltpu.make_async_remote_copy(src_ref=v_ref, dst_ref=out_ref.at[_lin(me)], send_sem=send_sems.at[k - 1],
                                              recv_sem=recv_sems.at[k - 1], device_id=_peer(me, k), device_id_type=MESH)
            cp.start()
            sends.append(cp)
        for k in range(1, N_DEV):
            peer = _peer(me, k)
            pltpu.make_async_remote_copy(src_ref=v_ref, dst_ref=out_ref.at[_lin(peer)], send_sem=send_sems.at[k - 1],
                                         recv_sem=recv_sems.at[k - 1], device_id=peer, device_id_type=MESH).wait_recv()
        for cp in sends:
            cp.wait_send()
        mine.wait()

    return pl.pallas_call(
        body,
        name=name,
        out_shape=jax.ShapeDtypeStruct((N_DEV, r, w), v.dtype),
        in_specs=[pl.BlockSpec(memory_space=pltpu.VMEM)],
        out_specs=pl.BlockSpec(memory_space=pltpu.VMEM),
        scratch_shapes=[pltpu.SemaphoreType.DMA((N_DEV - 1,)), pltpu.SemaphoreType.DMA((N_DEV - 1,)), pltpu.SemaphoreType.DMA],
        compiler_params=pltpu.CompilerParams(vmem_limit_bytes=VMEM_LIMIT_BYTES),
    )(v)


def _block_of(ref, kind, idx, n):
    start = pl.multiple_of(idx * n, 128)
    if kind == "col":
        return ref.at[:, pl.ds(start, n)]
    return ref.at[pl.ds(start, n), :]


def _handshake(peers):
    barrier = pltpu.get_barrier_semaphore()
    for peer in peers:
        pl.semaphore_signal(barrier, inc=1, device_id=peer, device_id_type=MESH)
    pl.semaphore_wait(barrier, len(peers))


def _allgather_weights_seq(shards, kinds, name, collective_id):
    nt = len(shards)
    hbm = pltpu.MemorySpace.HBM
    ins = [jax.new_ref(s, memory_space=hbm) for s in shards]
    outs = []
    for s, kind in zip(shards, kinds):
        k, n = s.shape
        shape = (k, n * N_DEV) if kind == "col" else (k * N_DEV, n)
        outs.append(jax.empty_ref(jax.ShapeDtypeStruct(shape, s.dtype), memory_space=hbm))

    @functools.partial(
        pl.kernel, mesh=plsc.ScalarSubcoreMesh(axis_name="seq", num_cores=1), name=name,
        scratch_types=(pltpu.SemaphoreType.DMA((nt, N_DEV - 1)), pltpu.SemaphoreType.DMA((nt, N_DEV - 1)),
                       pltpu.SemaphoreType.DMA((nt,))),
        compiler_params=pltpu.CompilerParams(collective_id=collective_id))
    def launch(send_sems, recv_sems, local_sems):
        x, y, c = _my_pos()
        me, sibling = (x, y, c), (x, y, 1 - c)
        chips = [(1 - x, y), (x, 1 - y), (1 - x, 1 - y)]
        _handshake([sibling] + [(*chip, c) for chip in chips])

        def blk(t, pos):
            n = shards[t].shape[1] if kinds[t] == "col" else shards[t].shape[0]
            return _block_of(outs[t], kinds[t], _lin(pos), n)

        def copy(t, k, block, to, src=None):
            return pltpu.make_async_remote_copy(src_ref=blk(t, block) if src is None else src, dst_ref=blk(t, block),
                                                send_sem=send_sems.at[t, k], recv_sem=recv_sems.at[t, k],
                                                device_id=to, device_id_type=MESH)

        local, sends = [], []
        for t in range(nt):
            mine = pltpu.make_async_copy(ins[t], blk(t, me), local_sems.at[t])
            mine.start()
            local.append(mine)
            first = [copy(t, 0, me, sibling, src=ins[t])]
            first += [copy(t, 1 + j, me, (*chip, c), src=ins[t]) for j, chip in enumerate(chips)]
            for cp in first:
                cp.start()
            sends += first
        for t in range(nt):
            for j, chip in enumerate(chips):
                copy(t, 1 + j, (*chip, c), me).wait_recv()
                fwd = copy(t, 4 + j, (*chip, c), sibling)
                fwd.start()
                sends.append(fwd)
        for t in range(nt):
            copy(t, 0, sibling, me).wait_recv()
            for j, chip in enumerate(chips):
                copy(t, 4 + j, (*chip, 1 - c), me).wait_recv()
        for cp in sends:
            cp.wait_send()
        for cp in local:
            cp.wait()

    launch()
    return [o[...] for o in outs]


def _allgather_small_seq(v, name, collective_id):
    hbm = pltpu.MemorySpace.HBM
    src = jax.new_ref(v, memory_space=hbm)
    out = jax.empty_ref(jax.ShapeDtypeStruct((N_DEV,) + v.shape, v.dtype), memory_space=hbm)

    @functools.partial(
        pl.kernel, mesh=plsc.ScalarSubcoreMesh(axis_name="seq", num_cores=1), name=name,
        scratch_types=(pltpu.SemaphoreType.DMA((N_DEV - 1,)), pltpu.SemaphoreType.DMA((N_DEV - 1,)), pltpu.SemaphoreType.DMA),
        compiler_params=pltpu.CompilerParams(collective_id=collective_id))
    def launch(send_sems, recv_sems, local_sem):
        me = _my_pos()
        _handshake([_peer(me, k) for k in range(1, N_DEV)])
        mine = pltpu.make_async_copy(src, out.at[_lin(me)], local_sem)
        mine.start()
        sends = []
        for k in range(1, N_DEV):
            cp = pltpu.make_async_remote_copy(src_ref=src, dst_ref=out.at[_lin(me)], send_sem=send_sems.at[k - 1],
                                              recv_sem=recv_sems.at[k - 1], device_id=_peer(me, k), device_id_type=MESH)
            cp.start()
            sends.append(cp)
        for k in range(1, N_DEV):
            peer = _peer(me, k)
            pltpu.make_async_remote_copy(src_ref=src, dst_ref=out.at[_lin(peer)], send_sem=send_sems.at[k - 1],
                                         recv_sem=recv_sems.at[k - 1], device_id=peer, device_id_type=MESH).wait_recv()
        for cp in sends:
            cp.wait_send()
        mine.wait()

    launch()
    return out[...]


N_CHIP = N_DEV // 2


def _chip_of(pos):
    return 2 * pos[0] + pos[1]


def _pair_exchange_seq(grads, kinds, name, collective_id):
    nt = len(grads)
    hbm = pltpu.MemorySpace.HBM
    shard_shapes = _shard_shapes(grads, kinds)
    ins = [jax.new_ref(g, memory_space=hbm) for g in grads]
    outs = [jax.empty_ref(jax.ShapeDtypeStruct((N_CHIP,) + s, g.dtype), memory_space=hbm) for s, g in zip(shard_shapes, grads)]

    @functools.partial(
        pl.kernel, mesh=plsc.ScalarSubcoreMesh(axis_name="seq", num_cores=1), name=name,
        scratch_types=(pltpu.SemaphoreType.DMA((nt, N_CHIP)), pltpu.SemaphoreType.DMA((nt, N_CHIP))),
        compiler_params=pltpu.CompilerParams(collective_id=collective_id))
    def launch(send_sems, recv_sems):
        x, y, c = _my_pos()
        sibling = (x, y, 1 - c)
        _handshake([sibling])
        copies = []
        for t in range(nt):
            n = shard_shapes[t][1] if kinds[t] == "col" else shard_shapes[t][0]
            for q in range(N_CHIP):
                cp = pltpu.make_async_remote_copy(src_ref=_block_of(ins[t], kinds[t], 2 * q + (1 - c), n), dst_ref=outs[t].at[q],
                                                  send_sem=send_sems.at[t, q], recv_sem=recv_sems.at[t, q],
                                                  device_id=sibling, device_id_type=MESH)
                cp.start()
                copies.append(cp)
        for cp in copies:
            cp.wait_recv()
        for cp in copies:
            cp.wait_send()

    launch()
    return [o[...] for o in outs]


def _pair_add(g, half, kind, name, after=()):
    nq, k, ns = half.shape
    tr = min(k, 512)
    c_idx = lax.axis_index("c").astype(jnp.int32).reshape(1)
    if kind == "col":
        g_spec = pl.BlockSpec((tr, ns), lambda q, i, c_ref: (i, 2 * q + c_ref[0]))
    else:
        g_spec = pl.BlockSpec((tr, ns), lambda q, i, c_ref: ((2 * q + c_ref[0]) * (k // tr) + i, 0))
    n_after = len(after)

    def kern(c_ref, g_ref, h_ref, *rest):
        o_ref = rest[n_after]
        o_ref[0] = (g_ref[...].astype(F32) + h_ref[0].astype(F32)).astype(o_ref.dtype)

    return pl.pallas_call(
        kern,
        name=name,
        grid_spec=pltpu.PrefetchScalarGridSpec(
            num_scalar_prefetch=1,
            grid=(nq, k // tr),
            in_specs=[g_spec, pl.BlockSpec((1, tr, ns), lambda q, i, c_ref: (q, i, 0))] + [pl.BlockSpec(memory_space=pl.ANY)] * n_after,
            out_specs=pl.BlockSpec((1, tr, ns), lambda q, i, c_ref: (q, i, 0)),
        ),
        out_shape=jax.ShapeDtypeStruct(half.shape, half.dtype),
        compiler_params=_cparams(("parallel", "parallel")),
    )(c_idx, g, half, *after)


def _chip_exchange_seq(psums, name, collective_id):
    nt = len(psums)
    hbm = pltpu.MemorySpace.HBM
    ins = [jax.new_ref(s, memory_space=hbm) for s in psums]
    outs = [jax.empty_ref(jax.ShapeDtypeStruct(s.shape, s.dtype), memory_space=hbm) for s in psums]

    @functools.partial(
        pl.kernel, mesh=plsc.ScalarSubcoreMesh(axis_name="seq", num_cores=1), name=name,
        scratch_types=(pltpu.SemaphoreType.DMA((nt, N_CHIP - 1)), pltpu.SemaphoreType.DMA((nt, N_CHIP - 1)),
                       pltpu.SemaphoreType.DMA((nt,))),
        compiler_params=pltpu.CompilerParams(collective_id=collective_id))
    def launch(send_sems, recv_sems, local_sems):
        me = _my_pos()
        peers = [_peer(me, k) for k in (2, 4, 6)]
        _handshake(peers)
        mine = _chip_of(me)
        local, sends = [], []
        for t in range(nt):
            cp = pltpu.make_async_copy(ins[t].at[mine], outs[t].at[mine], local_sems.at[t])
            cp.start()
            local.append(cp)
            for j, peer in enumerate(peers):
                cp = pltpu.make_async_remote_copy(src_ref=ins[t].at[_chip_of(peer)], dst_ref=outs[t].at[mine],
                                                  send_sem=send_sems.at[t, j], recv_sem=recv_sems.at[t, j],
                                                  device_id=peer, device_id_type=MESH)
                cp.start()
                sends.append(cp)
        for t in range(nt):
            for j, peer in enumerate(peers):
                pltpu.make_async_remote_copy(src_ref=ins[t].at[mine], dst_ref=outs[t].at[_chip_of(peer)],
                                             send_sem=send_sems.at[t, j], recv_sem=recv_sems.at[t, j],
                                             device_id=peer, device_id_type=MESH).wait_recv()
        for cp in sends:
            cp.wait_send()
        for cp in local:
            cp.wait()

    launch()
    return [o[...] for o in outs]


def _shard_shapes(grads, kinds):
    return [(g.shape[0], g.shape[1] // N_DEV) if kind == "col" else (g.shape[0] // N_DEV, g.shape[1]) for g, kind in zip(grads, kinds)]


def _adam(g_slots, w, m, v, *, tr, name, after=()):
    ns, r, wd = g_slots.shape
    tr = min(tr, r)
    assert r % tr == 0, (name, r, tr)
    c1 = 1.0 - ADAM_B1 ** ADAM_STEP
    c2 = 1.0 - ADAM_B2 ** ADAM_STEP
    n_after = len(after)

    def kern(g_ref, w_ref, m_ref, v_ref, *rest):
        go_ref, d_ref, mo_ref, vo_ref = rest[n_after:]
        g = g_ref[0].astype(F32)
        for s in range(1, ns):
            g = g + g_ref[s].astype(F32)
        m_new = ADAM_B1 * m_ref[...] + (1.0 - ADAM_B1) * g
        v_new = ADAM_B2 * v_ref[...] + (1.0 - ADAM_B2) * (g * g)
        m_hat = m_new / c1
        v_hat = v_new / c2
        go_ref[...] = g
        d_ref[...] = -ADAM_LR * (m_hat / (jnp.sqrt(v_hat) + ADAM_EPS) + ADAM_WD * w_ref[...])
        mo_ref[...] = m_new
        vo_ref[...] = v_new

    tile = pl.BlockSpec((tr, wd), lambda i: (i, 0))
    return pl.pallas_call(
        kern,
        name=name,
        grid=(r // tr,),
        in_specs=[pl.BlockSpec((ns, tr, wd), lambda i: (0, i, 0)), tile, tile, tile] + [pl.BlockSpec(memory_space=pl.ANY)] * n_after,
        out_specs=[tile] * 4,
        out_shape=[jax.ShapeDtypeStruct((r, wd), F32)] * 4,
        compiler_params=_cparams(("parallel",)),
    )(g_slots, w, m, v, *after)


SMALL = ("c_ctx", "b_ada", "attn_sink", "ssm_a_re", "ssm_a_im", "ssm_log_dt", "ssm_b_re", "ssm_b_im", "ssm_c_re", "ssm_c_im",
         "ssm_d", "ln_mix_g", "ln_mix_b", "b_mlp1", "b_mlp2", "ln_mlp_g", "ln_mlp_b")
BIG = ("w_in", "w_glu", "w_attn_up", "w_ssm_up", "w_out", "w_mlp1", "w_mlp2")
BIG_KIND = ("col", "col", "col", "col", "row", "col", "row")
AG_GROUPS = (("w_in",), ("w_glu", "w_attn_up", "w_ssm_up", "w_out"), ("w_mlp1",), ("w_mlp2",))
AG_COLLECTIVE_ID0 = 1
RS_GROUPS = (("w_mlp2",), ("w_mlp1",), ("w_out", "w_attn_up", "w_ssm_up", "w_glu"), ("w_in",))
RS_COLLECTIVE_ID0 = AG_COLLECTIVE_ID0 + len(AG_GROUPS)
SMALL_EARLY = ("ssm_a_re", "ssm_a_im", "ssm_log_dt", "ssm_b_re", "ssm_b_im", "ssm_c_re", "ssm_c_im", "ssm_d")
SMALL_LATE = tuple(n for n in SMALL if n not in SMALL_EARLY)
SMALL_COLLECTIVE_ID0 = RS_COLLECTIVE_ID0 + 2 * len(RS_GROUPS)
LANES = 128


def _pack(parts):
    rows = []
    for p in parts:
        flat = p.reshape(-1).astype(F32)
        pad = (-flat.shape[0]) % LANES
        rows.append(jnp.pad(flat, (0, pad)).reshape(-1, LANES))
    packed = jnp.concatenate(rows, 0)
    return jnp.pad(packed, ((0, (-packed.shape[0]) % 8), (0, 0)))


def _unpack(packed, shapes):
    out, r0 = [], 0
    for s in shapes:
        n = math.prod(s)
        nr = -(-n // LANES)
        out.append(packed[r0:r0 + nr].reshape(-1)[:n].reshape(s))
        r0 += nr
    return out


WEIGHTS = ("c_ctx", "w_ada", "b_ada", "w_in", "attn_sink", "ssm_a_re", "ssm_a_im", "ssm_log_dt", "ssm_b_re", "ssm_b_im",
           "ssm_c_re", "ssm_c_im", "ssm_d", "w_glu", "w_attn_up", "w_ssm_up", "w_out", "ln_mix_g", "ln_mix_b", "w_mlp1",
           "b_mlp1", "w_mlp2", "b_mlp2", "ln_mlp_g", "ln_mlp_b")
ADA_COLS = 6 * D // N_DEV


def _step(x, c, ctx, loss_target, p, m, v):
    me = _lin(_my_pos())
    x2, ctx2, tgt2 = x[0], ctx[0], loss_target[0]

    wb = {}
    for gi, group in enumerate(AG_GROUPS):
        full = _allgather_weights_seq([p[n][0].astype(BF16) for n in group], [BIG_KIND[BIG.index(n)] for n in group],
                                      "allgather_seq%d" % gi, AG_COLLECTIVE_ID0 + gi)
        wb.update(zip(group, full))

    c_all = _allgather_small(jnp.broadcast_to(c, (8, D)), "gather_c")[:, 0, :]
    cc = p["c_ctx"].reshape(1, D)
    s_in = jnp.concatenate([c_all, cc, jnp.zeros((7, D), F32)], 0)
    s_act, = _rowwise(lambda rv, vv: ([_silu(rv[0])], []), [(s_in, D, 0, 0)], [], [(D, F32)], [], nrows=16, tr=16, name="silu_c")
    b_mine = lax.dynamic_slice_in_dim(p["b_ada"], me * ADA_COLS, ADA_COLS, axis=1)
    mod_part = _matmul(s_act, p["w_ada"][0], mode="nn", name="ada_fwd", tm=16, tn=512, bias=b_mine)
    mod_all = _allgather_small(mod_part, "gather_mod")
    mod_lat = lax.dynamic_index_in_dim(mod_all, me, axis=1, keepdims=False).reshape(1, 6 * D)
    mod_ctx = mod_all[:, 8, :].reshape(1, 6 * D)

    sp = {n: p[n][0] for n in SMALL if n not in ("c_ctx", "b_ada")}
    recv, halves = {}, {}

    def on_grad(gw):
        for gi, group in enumerate(RS_GROUPS):
            if gi not in halves and all(n in gw for n in group):
                kinds = [BIG_KIND[BIG.index(n)] for n in group]
                halves[gi] = (dict(gw), _pair_exchange_seq([gw[n] for n in group], kinds, "pair_exchange%d" % gi, RS_COLLECTIVE_ID0 + 2 * gi))

    def on_finish(key, after):
        gi = [i for i, group in enumerate(RS_GROUPS) if key in group][0]
        group = RS_GROUPS[gi]
        grads, half = halves[gi]
        prev = tuple(recv[n] for n in RS_GROUPS[gi - 1][:1]) if gi else ()
        if gi == len(RS_GROUPS) - 1:
            prev += (small["early"],)
        psums =[_pair_add(grads[n], h, BIG_KIND[BIG.index(n)], "pair_add_" + n, after=(after,) + prev) for n, h in zip(group, half)]
        recv.update(zip(group, _chip_exchange_seq(psums, "chip_exchange%d" % gi, RS_COLLECTIVE_ID0 + 2 * gi + 1)))
        return psums[-1]

    small = {}

    def on_early(gs_early):
        small["early"] = _allgather_small_seq(_pack([gs_early[n] for n in SMALL_EARLY]), "gather_small_early", SMALL_COLLECTIVE_ID0)

    total = {}

    def on_loss(loss_p):
        total["loss"] = lax.psum(loss_p[0, 0], ("x", "y", "c"))
        return total["loss"].reshape(1, 1)

    loss_p, grad_x, d_mod_lat, d_mod_ctx, gw, gs = _local_step(x2, ctx2, tgt2, mod_lat, mod_ctx, wb, sp, on_grad, on_loss, on_finish, on_early)

    g_early = small["early"]
    res = {}
    last = ()

    def adam_small(names, g_pack, tag, after):
        sm = _adam(g_pack, _pack([p[n] for n in names]), _pack([m[n] for n in names]), _pack([v[n] for n in names]),
                   tr=g_pack.shape[1], name="adam_small_" + tag, after=after)
        shapes = [p[n].shape for n in names]
        for j, outs in enumerate(zip(*[_unpack(a, shapes) for a in sm])):
            res[names[j]] = outs
        return (sm[0],)

    for gi, group in enumerate(RS_GROUPS):
        if gi == len(RS_GROUPS) - 1:
            last = adam_small(SMALL_EARLY, g_early, "early", last)
        for n in group:
            res[n] = _adam(recv[n], p[n][0], m[n][0], v[n][0], tr=256, name="adam_" + n, after=last)
            last = (res[n][0],)

    dm = jnp.concatenate([d_mod_lat, d_mod_ctx, jnp.zeros((6, 6 * D), F32)], 0)
    dm_all = _allgather_small_seq(dm, "gather_dmod", SMALL_COLLECTIVE_ID0 + 1)
    dm_all = lax.optimization_barrier((dm_all,) + last)[0]
    dm2 = jnp.concatenate([dm_all[:, 0, :], dm_all[:, 1, :]], 0)
    dm2_mine = lax.dynamic_slice_in_dim(dm2, me * ADA_COLS, ADA_COLS, axis=1)
    s2 = jnp.concatenate([s_act[0:8], jnp.broadcast_to(s_act[8:9], (8, D))], 0)
    g_w_ada = _matmul(s2, dm2_mine, mode="tn", name="dw_ada", tm=512, tn=ADA_COLS, after=last)
    dsc_part = _matmul(dm2_mine[8:16], p["w_ada"][0], mode="nt", name="d_silu_cctx", tm=8, tn=512, after=last)

    def cctx_b(rv, vv):
        _, pull = jax.vjp(_silu, vv[0])
        return [], [pull(jnp.sum(rv[0], axis=0, keepdims=True))[0]]

    g_cctx, = _rowwise(cctx_b, [(dsc_part, D, 0, 0)], [cc], [], [(1, D)], nrows=8, tr=8, name="cctx_bwd")
    gs["c_ctx"] = g_cctx
    gs["b_ada"] = d_mod_lat + d_mod_ctx

    res["w_ada"] = _adam(g_w_ada[None], p["w_ada"][0], m["w_ada"][0], v["w_ada"][0], tr=256, name="adam_w_ada")

    g_late = _allgather_small_seq(_pack([gs[n] for n in SMALL_LATE]), "gather_small_late", SMALL_COLLECTIVE_ID0 + 2)
    adam_small(SMALL_LATE, g_late, "late", (res["w_ada"][0],))

    outs = [total["loss"], grad_x[None]]
    for j in range(4):
        outs += [res[n][j].reshape(p[n].shape) for n in WEIGHTS]
    return tuple(outs)


def kernel(x, c, ctx, c_ctx, w_ada, b_ada, w_in, attn_sink, ssm_a_re, ssm_a_im, ssm_log_dt, ssm_b_re, ssm_b_im, ssm_c_re, ssm_c_im, ssm_d, w_glu, w_attn_up, w_ssm_up, w_out, ln_mix_g, ln_mix_b, w_mlp1, b_mlp1, w_mlp2, b_mlp2, ln_mlp_g, ln_mlp_b, loss_target, m_c_ctx, m_w_ada, m_b_ada, m_w_in, m_attn_sink, m_ssm_a_re, m_ssm_a_im, m_ssm_log_dt, m_ssm_b_re, m_ssm_b_im, m_ssm_c_re, m_ssm_c_im, m_ssm_d, m_w_glu, m_w_attn_up, m_w_ssm_up, m_w_out, m_ln_mix_g, m_ln_mix_b, m_w_mlp1, m_b_mlp1, m_w_mlp2, m_b_mlp2, m_ln_mlp_g, m_ln_mlp_b, v_c_ctx, v_w_ada, v_b_ada, v_w_in, v_attn_sink, v_ssm_a_re, v_ssm_a_im, v_ssm_log_dt, v_ssm_b_re, v_ssm_b_im, v_ssm_c_re, v_ssm_c_im, v_ssm_d, v_w_glu, v_w_attn_up, v_w_ssm_up, v_w_out, v_ln_mix_g, v_ln_mix_b, v_w_mlp1, v_b_mlp1, v_w_mlp2, v_b_mlp2, v_ln_mlp_g, v_ln_mlp_b):
    given = dict(locals())
    p = {n: given[n] for n in WEIGHTS}
    m = {n: given["m_" + n] for n in WEIGHTS}
    v = {n: given["v_" + n] for n in WEIGHTS}
    return _step(x, c, ctx, loss_target, p, m, v)
```

```python
import functools
import math

import jax
import jax.numpy as jnp
from jax import lax
from jax.experimental import pallas as pl
from jax.experimental.pallas import tpu as pltpu
from jax.experimental.pallas import tpu_sc as plsc

F32 = jnp.float32
BF16 = jnp.bfloat16

N_DEV = 8
D = 2048
T = 2048
C = 256
TA = T + C
GRID_W = 64
HD = 128
NH = 8
NKV = 2
GROUP = NH // NKV
WINDOW = 128
QW = NH * HD
KVW = NKV * HD
SW = D // 4
SG = 16
NG = SW // SG
SP = 64
DFF = 4 * D
IN_COLS = QW + 2 * KVW + SW + 2 * D
ALPHA = 2.0 ** 0.25
LN_EPS = 1e-6
NEG_INF = -1e30
ROPE_BASE = 10000.0
ATT_SCALE = HD ** -0.5

NSEG = 8
GBLK = 8
NBLK = NG // GBLK
BW = GBLK * SP
UW = GBLK * SG

ADAM_LR = 0.001
ADAM_B1 = 0.9
ADAM_B2 = 0.999
ADAM_EPS = 1e-08
ADAM_WD = 0.01
ADAM_STEP = 10

VMEM_LIMIT_BYTES = 56 * 1024 * 1024
MESH = pl.DeviceIdType.MESH


def _cparams(sem=None):
    return pltpu.CompilerParams(dimension_semantics=sem, vmem_limit_bytes=VMEM_LIMIT_BYTES)


def _matmul(a, b, *, mode, name, out_dtypes=(F32,), tm=512, tn=512, tk=None, bias=None, extras=(), epilogue=None, after=(),
            out_t=None):
    if mode == "nn":
        (M, K), (K2, N) = a.shape, b.shape
    elif mode == "nt":
        (M, K), (N, K2) = a.shape, b.shape
    else:
        (K, M), (K2, N) = a.shape, b.shape
    assert K == K2, (name, a.shape, b.shape)
    tm, tn, tk = min(tm, M), min(tn, N), min(tk or K, K)
    assert M % tm == 0 and N % tn == 0 and K % tk == 0, (name, M, N, K, tm, tn, tk)
    nk = K // tk
    if mode == "tn":
        a_spec = pl.BlockSpec((tk, tm), lambda i, j, k: (k, i))
    else:
        a_spec = pl.BlockSpec((tm, tk), lambda i, j, k: (i, k))
    if mode == "nt":
        b_spec = pl.BlockSpec((tn, tk), lambda i, j, k: (j, k))
    else:
        b_spec = pl.BlockSpec((tk, tn), lambda i, j, k: (k, j))
    dims = {"nn": (((1,), (0,)), ((), ())), "nt": (((1,), (1,)), ((), ())), "tn": (((0,), (0,)), ((), ()))}[mode]
    in_specs = [a_spec, b_spec]
    operands = [a, b]
    if bias is not None:
        in_specs.append(pl.BlockSpec((1, tn), lambda i, j, k: (0, j)))
        operands.append(bias)
    for e in extras:
        in_specs.append(pl.BlockSpec((tm, tn), lambda i, j, k: (i, j)))
        operands.append(e)
    n_ex = len(extras)
    for t in after:
        in_specs.append(pl.BlockSpec(memory_space=pl.ANY))
        operands.append(t)
    n_after = len(after)
    n_out = len(out_dtypes)
    out_t = tuple(out_t) if out_t is not None else (False,) * n_out
    has_bias = bias is not None

    def kern(*refs):
        a_ref, b_ref = refs[0], refs[1]
        pos = 2
        bias_ref = None
        if has_bias:
            bias_ref = refs[pos]
            pos += 1
        ex_refs = refs[pos:pos + n_ex]
        pos += n_ex + n_after
        out_refs = refs[pos:pos + n_out]
        acc_ref = refs[pos + n_out] if nk > 1 else None

        def finish(r):
            if has_bias:
                r = r + bias_ref[...]
            outs = epilogue(r, *[e[...] for e in ex_refs]) if epilogue is not None else (r,)
            for o_ref, o, tr_ in zip(out_refs, outs, out_t):
                o_ref[...] = (o.T if tr_ else o).astype(o_ref.dtype)

        part = lax.dot_general(a_ref[...].astype(BF16), b_ref[...].astype(BF16), dims, preferred_element_type=F32)
        if nk == 1:
            finish(part)
        else:
            k = pl.program_id(2)

            @pl.when(k == 0)
            def _():
                acc_ref[...] = part

            @pl.when(k > 0)
            def _():
                acc_ref[...] += part

            @pl.when(k == nk - 1)
            def _():
                finish(acc_ref[...])

    outs = pl.pallas_call(
        kern,
        name=name,
        grid=(M // tm, N // tn, nk),
        in_specs=in_specs,
        out_specs=[pl.BlockSpec((tn, tm), lambda i, j, k: (j, i)) if tr_ else pl.BlockSpec((tm, tn), lambda i, j, k: (i, j))
                   for tr_ in out_t],
        out_shape=[jax.ShapeDtypeStruct((N, M) if tr_ else (M, N), dt) for dt, tr_ in zip(out_dtypes, out_t)],
        scratch_shapes=[pltpu.VMEM((tm, tn), F32)] if nk > 1 else [],
        compiler_params=_cparams(("parallel", "parallel", "arbitrary")),
    )(*operands)
    return outs[0] if n_out == 1 else tuple(outs)


def _rowwise(fn, rows, vecs, outs, vec_outs, *, nrows, tr, name, after=()):
    n_rows, n_vecs, n_outs, n_after = len(rows), len(vecs), len(outs), len(after)
    in_specs = [pl.BlockSpec((tr, w), lambda i, cb=cb, ro=ro: (i + ro, cb)) for (_, w, cb, ro) in rows]
    in_specs += [pl.BlockSpec(v.shape, lambda i: (0, 0)) for v in vecs]
    in_specs += [pl.BlockSpec(memory_space=pl.ANY)] * n_after
    outs = [o if len(o) == 3 else (*o, False) for o in outs]
    out_specs = [pl.BlockSpec((w, tr), lambda i: (0, i)) if tr_ else pl.BlockSpec((tr, w), lambda i: (i, 0)) for (w, _, tr_) in outs]
    out_specs += [pl.BlockSpec(s, lambda i: (0, 0)) for s in vec_outs]
    out_shape = [jax.ShapeDtypeStruct((w, nrows) if tr_ else (nrows, w), dt) for (w, dt, tr_) in outs]
    out_tr = [tr_ for (_, _, tr_) in outs]
    out_shape += [jax.ShapeDtypeStruct(s, F32) for s in vec_outs]

    def kern(*refs):
        rvals = [r[...] for r in refs[:n_rows]]
        vvals = [r[...] for r in refs[n_rows:n_rows + n_vecs]]
        first_out = n_rows + n_vecs + n_after
        o_refs = refs[first_out:first_out + n_outs]
        v_refs = refs[first_out + n_outs:]
        ro, vo = fn(rvals, vvals)
        for r, val, tr_ in zip(o_refs, ro, out_tr):
            r[...] = (val.astype(F32).T if tr_ else val).astype(r.dtype)
        i = pl.program_id(0)
        for r, val in zip(v_refs, vo):
            @pl.when(i == 0)
            def _(r=r, val=val):
                r[...] = val.astype(F32)

            @pl.when(i > 0)
            def _(r=r, val=val):
                r[...] += val.astype(F32)

    res = pl.pallas_call(
        kern,
        name=name,
        grid=(nrows // tr,),
        in_specs=in_specs,
        out_specs=out_specs,
        out_shape=out_shape,
        compiler_params=_cparams(("arbitrary",)),
    )(*[r[0] for r in rows], *vecs, *after)
    return list(res)


def _ln(x):
    mu = jnp.mean(x, axis=-1, keepdims=True)
    xc = x - mu
    var = jnp.mean(xc * xc, axis=-1, keepdims=True)
    return xc * lax.rsqrt(var + LN_EPS)


def _sigmoid(x):
    return 1.0 / (1.0 + jnp.exp(-x))


def _gelu(x):
    return 0.5 * x * (1.0 + jnp.tanh(math.sqrt(2.0 / math.pi) * (x + 0.044715 * (x * x * x))))


def _silu(x):
    return x * _sigmoid(x)


def _f_ln_mod(x, sc, sh):
    return _ln(x) * (1.0 + sc) + sh


def _f_glu(z):
    return z[:, :SW] * _sigmoid(z[:, SW:])


def _f_mix(ga, gs, attn_d, ssm_d):
    return _sigmoid(ga) * attn_d + _sigmoid(gs) * ssm_d


def _f_post1(x, y, g1, lg, lb, sc2, sh2):
    r1 = ALPHA * x + g1 * y
    x1 = _ln(r1) * lg + lb
    h2 = _ln(x1) * (1.0 + sc2) + sh2
    return x1, h2


def _f_loss(x1, mlp, tgt, g2, lg, lb, b2z):
    r2 = ALPHA * x1 + g2 * (mlp + b2z)
    out = _ln(r2) * lg + lb
    err = out - tgt
    return 0.5 * jnp.sum(err * err) * (1.0 / D)


def _rope_tables():
    rows = T // GRID_W
    row = jnp.repeat(jnp.arange(rows), GRID_W)
    col = jnp.tile(jnp.arange(GRID_W), rows)
    n_freq = HD // 4
    freqs = ROPE_BASE ** (-jnp.arange(n_freq, dtype=F32) / n_freq)
    ang_r = row.astype(F32)[:, None] * freqs
    ang_c = col.astype(F32)[:, None] * freqs
    ang = jnp.concatenate([ang_r, ang_r, ang_c, ang_c], -1)
    cos, sin = jnp.cos(ang), jnp.sin(ang)
    lo = (jnp.arange(HD) % (HD // 2)) < (HD // 4)
    sin_a = jnp.where(lo[None, :], -sin, 0.0)
    sin_b = jnp.where(lo[None, :], 0.0, sin)
    return cos, sin_a, sin_b


def _rope(x, cos, sa, sb):
    return x * cos + pltpu.roll(x, 96, 1) * sa + pltpu.roll(x, 32, 1) * sb


def _rope_t(dy, cos, sa, sb):
    return dy * cos + pltpu.roll(dy * sa, 32, 1) + pltpu.roll(dy * sb, 96, 1)


BAND = 3 * WINDOW
KPAD = T + 2 * WINDOW


def _attn_fill_kv(k_ref, v_ref, cos_ref, sa_ref, sb_ref, kp, vp, kc, vc):
    zeros = jnp.zeros((WINDOW, KVW), BF16)
    kp[0:WINDOW, :] = zeros
    kp[WINDOW + T:KPAD, :] = zeros
    vp[0:WINDOW, :] = zeros
    vp[WINDOW + T:KPAD, :] = zeros
    for hh in range(NKV):
        cs = slice(hh * HD, (hh + 1) * HD)
        for r0 in range(0, T, 512):
            rs = slice(r0, r0 + 512)
            kr = _rope(k_ref[rs, cs], cos_ref[rs, :], sa_ref[rs, :], sb_ref[rs, :])
            kp[WINDOW + r0:WINDOW + r0 + 512, cs] = kr.astype(BF16)
    vp[WINDOW:WINDOW + T, :] = v_ref[0:T, :].astype(BF16)
    kc[...] = k_ref[T:TA, :].astype(BF16)
    vc[...] = v_ref[T:TA, :].astype(BF16)


GROWS = GROUP * WINDOW


def _attn_scores(n, kvh, q_ref, cos_ref, sa_ref, sb_ref, sink_ref, kp, kc):
    r0 = pl.multiple_of(n * WINDOW, WINDOW)
    cos = cos_ref[pl.ds(r0, WINDOW), :]
    sa = sa_ref[pl.ds(r0, WINDOW), :]
    sb = sb_ref[pl.ds(r0, WINDOW), :]
    heads = range(kvh * GROUP, (kvh + 1) * GROUP)
    q_g = jnp.concatenate([_rope(q_ref[:, h * HD:(h + 1) * HD], cos, sa, sb).astype(BF16) for h in heads], axis=0)
    kb = kp[pl.ds(r0, BAND), kvh * HD:(kvh + 1) * HD]
    kcb = kc[:, kvh * HD:(kvh + 1) * HD]
    nt = (((1,), (1,)), ((), ()))
    s_loc = lax.dot_general(q_g, kb, nt, preferred_element_type=F32) * ATT_SCALE
    s_ctx = lax.dot_general(q_g, kcb, nt, preferred_element_type=F32) * ATT_SCALE
    row = lax.broadcasted_iota(jnp.int32, (GROWS, BAND), 0) & (WINDOW - 1)
    col = lax.broadcasted_iota(jnp.int32, (GROWS, BAND), 1)
    rel = col - WINDOW - row
    kpos = r0 - WINDOW + col
    valid = (jnp.abs(rel) <= WINDOW) & (kpos >= 0) & (kpos < T)
    s_loc = jnp.where(valid, s_loc, NEG_INF)
    sk = jnp.concatenate([jnp.broadcast_to(sink_ref[0:1, h:h + 1], (WINDOW, 1)) for h in heads], axis=0)
    m = jnp.maximum(jnp.maximum(jnp.max(s_loc, -1, keepdims=True), jnp.max(s_ctx, -1, keepdims=True)), sk)
    e_loc = jnp.exp(s_loc - m)
    e_ctx = jnp.exp(s_ctx - m)
    e_sink = jnp.exp(sk - m)
    inv = 1.0 / (jnp.sum(e_loc, -1, keepdims=True) + jnp.sum(e_ctx, -1, keepdims=True) + e_sink)
    return q_g, r0, e_loc * inv, e_ctx * inv, e_sink * inv


def _attn_fwd(proj, sink, tabs):
    cos, sa, sb = tabs

    def kern(q_ref, k_ref, v_ref, cos_ref, sa_ref, sb_ref, sink_ref, o_ref, kp, vp, kc, vc):
        n = pl.program_id(0)

        @pl.when(n == 0)
        def _():
            _attn_fill_kv(k_ref, v_ref, cos_ref, sa_ref, sb_ref, kp, vp, kc, vc)

        for kvh in range(NKV):
            _, r0, p_loc, p_ctx, _ = _attn_scores(n, kvh, q_ref, cos_ref, sa_ref, sb_ref, sink_ref, kp, kc)
            vb = vp[pl.ds(r0, BAND), kvh * HD:(kvh + 1) * HD]
            vcb = vc[:, kvh * HD:(kvh + 1) * HD]
            o = jnp.dot(p_loc.astype(BF16), vb, preferred_element_type=F32)
            o = o + jnp.dot(p_ctx.astype(BF16), vcb, preferred_element_type=F32)
            for g in range(GROUP):
                h = kvh * GROUP + g
                o_ref[:, h * HD:(h + 1) * HD] = o[g * WINDOW:(g + 1) * WINDOW, :].astype(o_ref.dtype)

    full = lambda shape: pl.BlockSpec(shape, lambda n: (0, 0))
    return pl.pallas_call(
        kern,
        name="attn_fwd",
        grid=(T // WINDOW,),
        in_specs=[
            pl.BlockSpec((WINDOW, QW), lambda n: (n, 0)),
            pl.BlockSpec((TA, KVW), lambda n: (0, QW // KVW)),
            pl.BlockSpec((TA, KVW), lambda n: (0, QW // KVW + 1)),
            full((T, HD)), full((T, HD)), full((T, HD)), full((1, NH)),
        ],
        out_specs=pl.BlockSpec((WINDOW, QW), lambda n: (n, 0)),
        out_shape=jax.ShapeDtypeStruct((T, QW), BF16),
        scratch_shapes=[pltpu.VMEM((KPAD, KVW), BF16), pltpu.VMEM((KPAD, KVW), BF16),
                        pltpu.VMEM((C, KVW), BF16), pltpu.VMEM((C, KVW), BF16)],
        compiler_params=_cparams(("arbitrary",)),
    )(proj, proj, proj, cos, sa, sb, sink)


def _attn_bwd(proj, d_attn, sink, tabs):
    cos, sa, sb = tabs
    n_blocks = T // WINDOW

    def kern(q_ref, k_ref, v_ref, do_ref, cos_ref, sa_ref, sb_ref, sink_ref,
             dq_ref, dk_ref, dv_ref, dsink_ref, kp, vp, kc, vc, dkp, dvp, dkc, dvc):
        n = pl.program_id(0)

        @pl.when(n == 0)
        def _():
            _attn_fill_kv(k_ref, v_ref, cos_ref, sa_ref, sb_ref, kp, vp, kc, vc)
            dkp[...] = jnp.zeros_like(dkp)
            dvp[...] = jnp.zeros_like(dvp)
            dkc[...] = jnp.zeros_like(dkc)
            dvc[...] = jnp.zeros_like(dvc)
            dsink_ref[...] = jnp.zeros_like(dsink_ref)

        nt = (((1,), (1,)), ((), ()))
        tn = (((0,), (0,)), ((), ()))
        for kvh in range(NKV):
            cs = slice(kvh * HD, (kvh + 1) * HD)
            heads = range(kvh * GROUP, (kvh + 1) * GROUP)
            q_g, r0, p_loc, p_ctx, p_sink = _attn_scores(n, kvh, q_ref, cos_ref, sa_ref, sb_ref, sink_ref, kp, kc)
            kb = kp[pl.ds(r0, BAND), cs]
            vb = vp[pl.ds(r0, BAND), cs]
            kcb = kc[:, cs]
            vcb = vc[:, cs]
            do_g = jnp.concatenate([do_ref[:, h * HD:(h + 1) * HD] for h in heads], axis=0)
            dp_loc = lax.dot_general(do_g, vb, nt, preferred_element_type=F32)
            dp_ctx = lax.dot_general(do_g, vcb, nt, preferred_element_type=F32)
            delta = jnp.sum(p_loc * dp_loc, -1, keepdims=True) + jnp.sum(p_ctx * dp_ctx, -1, keepdims=True)
            ds_loc = (p_loc * (dp_loc - delta) * ATT_SCALE).astype(BF16)
            ds_ctx = (p_ctx * (dp_ctx - delta) * ATT_SCALE).astype(BF16)
            dq = jnp.dot(ds_loc, kb, preferred_element_type=F32) + jnp.dot(ds_ctx, kcb, preferred_element_type=F32)
            cos = cos_ref[pl.ds(r0, WINDOW), :]
            sa_ = sa_ref[pl.ds(r0, WINDOW), :]
            sb_ = sb_ref[pl.ds(r0, WINDOW), :]
            dkp[pl.ds(r0, BAND), cs] += lax.dot_general(ds_loc, q_g, tn, preferred_element_type=F32)
            dkc[:, cs] += lax.dot_general(ds_ctx, q_g, tn, preferred_element_type=F32)
            dvp[pl.ds(r0, BAND), cs] += lax.dot_general(p_loc.astype(BF16), do_g, tn, preferred_element_type=F32)
            dvc[:, cs] += lax.dot_general(p_ctx.astype(BF16), do_g, tn, preferred_element_type=F32)
            dsk_rows = p_sink * delta
            for g, h in enumerate(heads):
                rs = slice(g * WINDOW, (g + 1) * WINDOW)
                dq_ref[:, h * HD:(h + 1) * HD] = _rope_t(dq[rs, :], cos, sa_, sb_).astype(dq_ref.dtype)
                dsk = -jnp.sum(dsk_rows[rs, :], axis=0, keepdims=True)
                dsink_ref[h:h + 1, :] += jnp.broadcast_to(dsk, (1, HD))

        @pl.when(n == n_blocks - 1)
        def _():
            for hh in range(NKV):
                cs = slice(hh * HD, (hh + 1) * HD)
                for r0 in range(0, T, 512):
                    rs = slice(r0, r0 + 512)
                    g = dkp[WINDOW + r0:WINDOW + r0 + 512, cs]
                    dk_ref[rs, cs] = _rope_t(g, cos_ref[rs, :], sa_ref[rs, :], sb_ref[rs, :]).astype(dk_ref.dtype)
            dk_ref[T:TA, :] = dkc[...].astype(dk_ref.dtype)
            dv_ref[0:T, :] = dvp[WINDOW:WINDOW + T, :].astype(dv_ref.dtype)
            dv_ref[T:TA, :] = dvc[...].astype(dv_ref.dtype)

    full = lambda shape: pl.BlockSpec(shape, lambda n: (0, 0))
    return pl.pallas_call(
        kern,
        name="attn_bwd",
        grid=(n_blocks,),
        in_specs=[
            pl.BlockSpec((WINDOW, QW), lambda n: (n, 0)),
            pl.BlockSpec((TA, KVW), lambda n: (0, QW // KVW)),
            pl.BlockSpec((TA, KVW), lambda n: (0, QW // KVW + 1)),
            pl.BlockSpec((WINDOW, QW), lambda n: (n, 0)),
            full((T, HD)), full((T, HD)), full((T, HD)), full((1, NH)),
        ],
        out_specs=[pl.BlockSpec((WINDOW, QW), lambda n: (n, 0)), full((TA, KVW)), full((TA, KVW)), full((NH, HD))],
        out_shape=[jax.ShapeDtypeStruct((T, QW), BF16), jax.ShapeDtypeStruct((TA, KVW), BF16),
                   jax.ShapeDtypeStruct((TA, KVW), BF16), jax.ShapeDtypeStruct((NH, HD), F32)],
        scratch_shapes=[pltpu.VMEM((KPAD, KVW), BF16), pltpu.VMEM((KPAD, KVW), BF16),
                        pltpu.VMEM((C, KVW), BF16), pltpu.VMEM((C, KVW), BF16),
                        pltpu.VMEM((KPAD, KVW), F32), pltpu.VMEM((KPAD, KVW), F32),
                        pltpu.VMEM((C, KVW), F32), pltpu.VMEM((C, KVW), F32)],
        compiler_params=_cparams(("arbitrary",)),
    )(proj, proj, proj, d_attn, cos, sa, sb, sink)


def _s5_prep(a_re, a_im, log_dt, b_re, b_im, c_re, c_im):
    lam = lax.complex(a_re, a_im)
    dt = jnp.exp(log_dt)[..., None]
    lam_bar = jnp.exp(lam * dt)
    b_bar = ((lam_bar - 1.0) / lam)[..., None] * lax.complex(b_re, b_im)
    def lam_rows(v):
        return v.reshape(2, NBLK, 1, BW)

    lam_l = jnp.concatenate([lam_rows(jnp.real(lam_bar)), lam_rows(jnp.imag(lam_bar))], -1)
    lam_l = jnp.broadcast_to(lam_l, (2, NBLK, 8, 2 * BW))
    diag = (jnp.arange(UW)[:, None] // SG) == (jnp.arange(BW)[None, :] // SP)

    def blocks(v):
        return jnp.where(diag, jnp.tile(v.reshape(2, NBLK, UW, SP), (1, 1, 1, GBLK)), 0.0)

    b_t = jnp.swapaxes(b_bar, -1, -2)
    bmat = jnp.concatenate([blocks(jnp.real(b_t)), blocks(jnp.imag(b_t))], -1)
    cmat = jnp.concatenate([blocks(c_re), -blocks(c_im)], -1)
    return lam_l, bmat, cmat


def _cmul(ar, ai, br, bi):
    return ar * br - ai * bi, ar * bi + ai * br


def _shift_rows(x, rev, fill):
    r = lax.broadcasted_iota(jnp.int32, x.shape, 0)
    down = jnp.where(r == 0, fill, pltpu.roll(x, 1, 0))
    up = jnp.where(r == NSEG - 1, fill, pltpu.roll(x, NSEG - 1, 0))
    return jnp.where(rev == 0, down, up)


def _edge_row(x, rev):
    last = jnp.broadcast_to(x[NSEG - 1:NSEG, :], x.shape)
    first = jnp.broadcast_to(x[0:1, :], x.shape)
    return jnp.where(rev == 0, last, first)


def _seg_scan(get, put, base, seglen, lr, li, rev, cin):
    zero = jnp.zeros((NSEG, BW), F32)

    def rows(k):
        j = jnp.where(rev == 0, k, seglen - 1 - k)
        return pl.ds(pl.multiple_of(base + j * NSEG, NSEG), NSEG)

    def local(k, carry):
        sr, si = carry
        xr, xi = get(rows(k))
        tr, ti = _cmul(lr, li, sr, si)
        sr, si = tr + xr, ti + xi
        put(rows(k), sr, si)
        return sr, si

    er, ei = lax.fori_loop(0, seglen, local, (zero, zero))
    lpr, lpi = lr, li
    assert seglen & (seglen - 1) == 0, seglen
    for _ in range(seglen.bit_length() - 1):
        lpr, lpi = _cmul(lpr, lpi, lpr, lpi)
    cr, ci = _shift_rows(zero, rev, cin[0]), _shift_rows(zero, rev, cin[1])
    for _ in range(NSEG - 1):
        tr, ti = _cmul(lpr, lpi, cr, ci)
        cr, ci = _shift_rows(er + tr, rev, cin[0]), _shift_rows(ei + ti, rev, cin[1])

    def fix(k, carry):
        tr, ti = _cmul(lr, li, carry[0], carry[1])
        xr, xi = get(rows(k))
        put(rows(k), xr + tr, xi + ti)
        return tr, ti

    tr, ti = lax.fori_loop(0, seglen, fix, (cr, ci))
    return _edge_row(er + tr, rev), _edge_row(ei + ti, rev)


RCH = 256
CSEG = C // NSEG
TSEG = T // NSEG
UCOL0 = (QW + 2 * KVW) // UW


REGIONS = ((0, TSEG), (T, CSEG))


def _state_access(ref, lead=()):
    def get(rows):
        return ref[(*lead, rows, slice(0, BW))], ref[(*lead, rows, slice(BW, 2 * BW))]

    def put(rows, re, im):
        ref[(*lead, rows, slice(0, BW))] = re
        ref[(*lead, rows, slice(BW, 2 * BW))] = im

    return get, put


def _interleave_rows(src_ref, dst_ref, regions=REGIONS):
    for base, seglen in regions:
        def body(j, carry, base=base, seglen=seglen):
            dst_ref[pl.ds(pl.multiple_of(base + j * NSEG, NSEG), NSEG), :] = src_ref[pl.ds(base + j, NSEG, stride=seglen), :]
            return carry

        lax.fori_loop(0, seglen, body, 0, unroll=8)


def _deinterleave_rows(src_ref, dst_ref, regions=REGIONS):
    for base, seglen in regions:
        def body(j, carry, base=base, seglen=seglen):
            dst_ref[pl.ds(base + j, NSEG, stride=seglen), :] = src_ref[pl.ds(pl.multiple_of(base + j * NSEG, NSEG), NSEG), :]
            return carry

        lax.fori_loop(0, seglen, body, 0, unroll=8)


def _s5_fwd(proj, dskip, lam, bmat, cmat):
    def kern(u_ref, dk_ref, lam_ref, b_ref, c_ref, s_ref, ssm_ref, ge_ref, up_ref, yp_ref):
        d = pl.program_id(1)

        @pl.when(d == 0)
        def _():
            _interleave_rows(u_ref, up_ref)

        bm = b_ref[0, 0].astype(BF16)
        for r0 in range(0, TA, RCH):
            s_ref[0, 0, r0:r0 + RCH, :] = jnp.dot(up_ref[r0:r0 + RCH, :].astype(BF16), bm, preferred_element_type=F32)
        lr = lam_ref[0, 0, :, 0:BW]
        li = lam_ref[0, 0, :, BW:2 * BW]
        zero = jnp.zeros((NSEG, BW), F32)
        get, put = _state_access(s_ref, (0, 0))
        mid = _seg_scan(get, put, T, CSEG, lr, li, d, (zero, zero))
        _seg_scan(get, put, 0, TSEG, lr, li, d, mid)
        cm = c_ref[0, 0].astype(BF16)
        for r0 in range(0, T, RCH):
            y = lax.dot_general(s_ref[0, 0, r0:r0 + RCH, :].astype(BF16), cm, (((1,), (1,)), ((), ())), preferred_element_type=F32)

            @pl.when(d == 0)
            def _(y=y, r0=r0):
                yp_ref[r0:r0 + RCH, :] = y + dk_ref[...] * up_ref[r0:r0 + RCH, :]

            @pl.when(d == 1)
            def _(y=y, r0=r0):
                yp_ref[r0:r0 + RCH, :] += y

        @pl.when(d == 1)
        def _():
            _deinterleave_rows(yp_ref, ssm_ref, REGIONS[:1])
            for r0 in range(0, T, RCH):
                ge_ref[r0:r0 + RCH, :] = _gelu(ssm_ref[r0:r0 + RCH, :]).astype(ge_ref.dtype)

    blk4 = lambda shape: pl.BlockSpec((1, 1) + shape, lambda b, d: (d, b, 0, 0))
    return pl.pallas_call(
        kern,
        name="s5_fwd",
        grid=(NBLK, 2),
        in_specs=[pl.BlockSpec((TA, UW), lambda b, d: (0, UCOL0 + b)), pl.BlockSpec((1, UW), lambda b, d: (0, b)),
                  blk4((8, 2 * BW)), blk4((UW, 2 * BW)), blk4((UW, 2 * BW))],
        out_specs=[blk4((TA, 2 * BW)), pl.BlockSpec((T, UW), lambda b, d: (0, b)), pl.BlockSpec((T, UW), lambda b, d: (0, b))],
        out_shape=[jax.ShapeDtypeStruct((2, NBLK, TA, 2 * BW), F32), jax.ShapeDtypeStruct((T, SW), F32),
                   jax.ShapeDtypeStruct((T, SW), BF16)],
        scratch_shapes=[pltpu.VMEM((TA, UW), F32), pltpu.VMEM((T, UW), F32)],
        compiler_params=_cparams(("parallel", "arbitrary")),
    )(proj, dskip, lam, bmat, cmat)


def _s5_bwd(d_ge, ssm, proj, dskip, states, lam, bmat, cmat):
    nt = (((1,), (1,)), ((), ()))
    tn = (((0,), (0,)), ((), ()))

    def kern(dge_ref, ssm_ref, u_ref, dk_ref, s_ref, lam_ref, b_ref, c_ref,
             du_ref, ddk_ref, dlam_ref, db_ref, dc_ref, g_ref, dua_ref, dssm_ref, up_ref, nat_ref):
        d = pl.program_id(1)

        @pl.when(d == 0)
        def _():
            ddk = jnp.zeros((1, UW), F32)
            for r0 in range(0, T, RCH):
                rs = slice(r0, r0 + RCH)
                _, pull = jax.vjp(_gelu, ssm_ref[rs, :])
                dssm = pull(dge_ref[rs, :])[0]
                nat_ref[rs, :] = dssm
                ddk = ddk + jnp.sum(dssm * u_ref[rs, :], axis=0, keepdims=True)
            ddk_ref[...] = ddk
            _interleave_rows(nat_ref, dssm_ref, REGIONS[:1])
            _interleave_rows(u_ref, up_ref)
            for r0 in range(0, T, RCH):
                dua_ref[r0:r0 + RCH, :] = dssm_ref[r0:r0 + RCH, :] * dk_ref[...]
            dua_ref[T:TA, :] = jnp.zeros((C, UW), F32)

        cm = c_ref[0, 0].astype(BF16)
        for r0 in range(0, T, RCH):
            g_ref[r0:r0 + RCH, :] = jnp.dot(dssm_ref[r0:r0 + RCH, :].astype(BF16), cm, preferred_element_type=F32)
        g_ref[T:TA, :] = jnp.zeros((C, 2 * BW), F32)
        lr = lam_ref[0, 0, :, 0:BW]
        li = lam_ref[0, 0, :, BW:2 * BW]
        zero = jnp.zeros((NSEG, BW), F32)
        get_g, put_g = _state_access(g_ref)

        mid = _seg_scan(get_g, put_g, 0, TSEG, lr, -li, 1 - d, (zero, zero))
        _seg_scan(get_g, put_g, T, CSEG, lr, -li, 1 - d, mid)

        get_s, _ = _state_access(s_ref, (0, 0))

        def dlam_terms(g, s):
            return g[0] * s[0] + g[1] * s[1], g[1] * s[0] - g[0] * s[1]

        def dlam_region(base, seglen, s_in, acc):
            def rows(j):
                return pl.ds(pl.multiple_of(base + j * NSEG, NSEG), NSEG)

            def inner(k, acc):
                j = jnp.where(d == 0, k, seglen - 1 - k)
                jp = jnp.where(d == 0, k - 1, seglen - k)
                t = dlam_terms(get_g(rows(j)), get_s(rows(jp)))
                return acc[0] + t[0], acc[1] + t[1]

            acc = lax.fori_loop(1, seglen, inner, acc)
            jb = jnp.where(d == 0, 0, seglen - 1)
            jn = jnp.where(d == 0, seglen - 1, 0)
            sp = get_s(rows(jn))
            t = dlam_terms(get_g(rows(jb)), (_shift_rows(sp[0], d, s_in[0]), _shift_rows(sp[1], d, s_in[1])))
            return acc[0] + t[0], acc[1] + t[1]

        r_mid = jnp.where(d == 0, TA - 1, T)
        s_mid = tuple(jnp.broadcast_to(t, (NSEG, BW)) for t in get_s(pl.ds(r_mid, 1)))
        acc = dlam_region(T, CSEG, (zero, zero), (zero, zero))
        acc = dlam_region(0, TSEG, s_mid, acc)
        dlam_ref[0, 0, :, 0:BW] = acc[0]
        dlam_ref[0, 0, :, BW:2 * BW] = acc[1]

        bm = b_ref[0, 0].astype(BF16)
        db = jnp.zeros((UW, 2 * BW), F32)
        dc = jnp.zeros((UW, 2 * BW), F32)
        for r0 in range(0, TA, RCH):
            rs = slice(r0, r0 + RCH)
            g = g_ref[rs, :].astype(BF16)
            dua_ref[rs, :] += lax.dot_general(g, bm, nt, preferred_element_type=F32)
            db = db + lax.dot_general(up_ref[rs, :].astype(BF16), g, tn, preferred_element_type=F32)
            if r0 < T:
                dc = dc + lax.dot_general(dssm_ref[rs, :].astype(BF16), s_ref[0, 0, rs, :].astype(BF16), tn,
                                          preferred_element_type=F32)
        db_ref[0, 0] = db
        dc_ref[0, 0] = dc

        @pl.when(d == 1)
        def _():
            _deinterleave_rows(dua_ref, nat_ref)
            du_ref[...] = nat_ref[...].astype(du_ref.dtype)

    blk4 = lambda shape: pl.BlockSpec((1, 1) + shape, lambda b, d: (d, b, 0, 0))
    lat = pl.BlockSpec((T, UW), lambda b, d: (0, b))
    vec = pl.BlockSpec((1, UW), lambda b, d: (0, b))
    return pl.pallas_call(
        kern,
        name="s5_bwd",
        grid=(NBLK, 2),
        in_specs=[lat, lat, pl.BlockSpec((TA, UW), lambda b, d: (0, UCOL0 + b)), vec,
                  blk4((TA, 2 * BW)), blk4((8, 2 * BW)), blk4((UW, 2 * BW)), blk4((UW, 2 * BW))],
        out_specs=[pl.BlockSpec((TA, UW), lambda b, d: (0, b)), vec, blk4((8, 2 * BW)), blk4((UW, 2 * BW)), blk4((UW, 2 * BW))],
        out_shape=[jax.ShapeDtypeStruct((TA, SW), BF16), jax.ShapeDtypeStruct((1, SW), F32),
                   jax.ShapeDtypeStruct((2, NBLK, 8, 2 * BW), F32),
                   jax.ShapeDtypeStruct((2, NBLK, UW, 2 * BW), F32), jax.ShapeDtypeStruct((2, NBLK, UW, 2 * BW), F32)],
        scratch_shapes=[pltpu.VMEM((TA, 2 * BW), F32), pltpu.VMEM((TA, UW), F32), pltpu.VMEM((T, UW), F32),
                        pltpu.VMEM((TA, UW), F32), pltpu.VMEM((TA, UW), F32)],
        compiler_params=_cparams(("parallel", "arbitrary")),
    )(d_ge, ssm, proj, dskip, states, lam, bmat, cmat)


TR = 256


def _vjp_rows(f, primals, cots, n_row):
    _, pull = jax.vjp(f, *primals)
    g = pull(cots)
    return list(g[:n_row]), list(g[n_row:])


class _GradDict(dict):
    def __init__(self, on_set=None):
        super().__init__()
        self._on_set = on_set
        self.tokens = {}

    def __setitem__(self, key, value):
        super().__setitem__(key, value)
        if self._on_set is not None:
            self._on_set(self)

    def order(self, key):
        return self.tokens.get(key, self.get(key))

    def finish(self, key, after):
        if self.on_finish is None:
            return ()
        return (self.on_finish(key, after),)

    on_finish = None


def _local_step(x, ctx, tgt, mod_lat, mod_ctx, wb, sp, on_grad=None, on_loss=None, on_finish=None, on_early=None):
    sh1, sc1, g1, sh2, sc2, g2 = [mod_lat[:, i * D:(i + 1) * D] for i in range(6)]
    csh1, csc1 = mod_ctx[:, 0:D], mod_ctx[:, D:2 * D]
    tabs = _rope_tables()
    sink = sp["attn_sink"].reshape(1, NH)
    dskip = sp["ssm_d"].reshape(1, SW)
    lg_mix, lb_mix = sp["ln_mix_g"].reshape(1, D), sp["ln_mix_b"].reshape(1, D)
    lg_mlp, lb_mlp = sp["ln_mlp_g"].reshape(1, D), sp["ln_mlp_b"].reshape(1, D)
    b1, b2 = sp["b_mlp1"].reshape(1, DFF), sp["b_mlp2"].reshape(1, D)
    s5_names = ("ssm_a_re", "ssm_a_im", "ssm_log_dt", "ssm_b_re", "ssm_b_im", "ssm_c_re", "ssm_c_im")
    (lam, bmat, cmat), s5_pull = jax.vjp(_s5_prep, *[sp[n] for n in s5_names])

    def ln_mod2(rv, vv):
        h = _f_ln_mod(rv[0], vv[0], vv[1])
        return [h, h], []

    h_lat, h_lat_t = _rowwise(ln_mod2, [(x, D, 0, 0)], [sc1, sh1], [(D, BF16), (D, BF16, True)], [], nrows=T, tr=TR, name="ln1_lat")
    h_ctx, h_ctx_t = _rowwise(ln_mod2, [(ctx, D, 0, 0)], [csc1, csh1], [(D, BF16), (D, BF16, True)], [], nrows=C, tr=TR,
                              name="ln1_ctx")
    h1 = jnp.concatenate([h_lat, h_ctx], 0)
    h1_t = jnp.concatenate([h_lat_t, h_ctx_t], 1)
    proj = _matmul(h1, wb["w_in"], mode="nn", name="proj", tm=768, tn=512)
    attn = _attn_fwd(proj, sink, tabs)
    states, ssm, ge = _s5_fwd(proj, dskip, lam, bmat, cmat)
    z = _matmul(ge, wb["w_glu"], mode="nn", name="glu_mm", tm=1024, tn=1024)

    def glu_act(rv, vv):
        return [_f_glu(rv[0])], []

    glu, = _rowwise(glu_act, [(z, 2 * SW, 0, 0)], [], [(SW, BF16)], [], nrows=T, tr=TR, name="glu_act")
    attn_d = _matmul(attn, wb["w_attn_up"], mode="nn", name="attn_up", tm=1024, tn=512)
    ssm_d = _matmul(glu, wb["w_ssm_up"], mode="nn", name="ssm_up", tm=1024, tn=512)
    ga_cb, gs_cb = (QW + 2 * KVW + SW) // D, (QW + 2 * KVW + SW) // D + 1

    def mix(rv, vv):
        m_ = _f_mix(*rv)
        return [m_, m_], []

    mixv, mix_t = _rowwise(mix, [(proj, D, ga_cb, 0), (proj, D, gs_cb, 0), (attn_d, D, 0, 0), (ssm_d, D, 0, 0)], [],
                           [(D, BF16), (D, BF16, True)], [], nrows=T, tr=TR, name="mix")
    y = _matmul(mixv, wb["w_out"], mode="nn", name="out_proj", tm=1024, tn=512)

    def post1(rv, vv):
        x1, h2 = _f_post1(rv[0], rv[1], *vv)
        return [x1, h2, h2], []

    x1, h2, h2_t = _rowwise(post1, [(x, D, 0, 0), (y, D, 0, 0)], [g1, lg_mix, lb_mix, sc2, sh2],
                            [(D, F32), (D, BF16), (D, BF16, True)], [], nrows=T, tr=TR, name="post1")

    def relu_sq(acc):
        r = jnp.maximum(acc, 0.0)
        return r, r * r, r * r

    r_act, act, act_t = _matmul(h2, wb["w_mlp1"], mode="nn", name="mlp1", tm=1024, tn=512, bias=b1,
                                out_dtypes=(BF16, BF16, BF16), out_t=(False, False, True), epilogue=relu_sq)
    mlp = _matmul(act, wb["w_mlp2"], mode="nn", name="mlp2", tm=512, tn=512)

    def loss_fb(rv, vv):
        x1_t, mlp_t, tgt_t = rv
        g2_v, lg_v, lb_v, b2_v = vv
        f = lambda a, m, g, p, q, b: _f_loss(a, m, tgt_t, g, p, q, b)
        val, grads = jax.value_and_grad(f, argnums=(0, 1, 2, 3, 4, 5))(x1_t, mlp_t, g2_v, lg_v, lb_v, b2_v)
        dx1, dmlp, dg2, dlg, dlb, db2 = grads
        return [dx1, dmlp], [jnp.reshape(val, (1, 1)), dg2, dlg, dlb, db2]

    dx1_a, d_mlp, loss_p, d_g2, d_lg_mlp, d_lb_mlp, d_b2 = _rowwise(
        loss_fb, [(x1, D, 0, 0), (mlp, D, 0, 0), (tgt, D, 0, 0)], [g2, lg_mlp, lb_mlp, b2],
        [(D, F32), (D, BF16)], [(1, 1), (1, D), (1, D), (1, D), (1, D)], nrows=T, tr=TR, name="loss_fb")

    gw = _GradDict(on_grad)
    gw.on_finish = on_finish
    loss_done = () if on_loss is None else (on_loss(loss_p),)
    gw["w_mlp2"] = _matmul(act_t, d_mlp, mode="nn", name="dw_mlp2", out_dtypes=(BF16,), tm=1024, tn=512, after=loss_done)
    da, = (_matmul(d_mlp, wb["w_mlp2"], mode="nt", name="d_act", out_dtypes=(BF16,), tm=1024, tn=512,
                   extras=(r_act,), epilogue=lambda acc, r: (acc * (2.0 * r.astype(F32)),), after=(gw.order("w_mlp2"),)),)
    pin = gw.finish("w_mlp2", da)
    ones = jnp.ones((8, T), BF16)
    d_b1 = _matmul(ones, da, mode="nn", name="db_mlp1", tm=8, tn=2048)[0:1]
    gw["w_mlp1"] = _matmul(h2_t, da, mode="nn", name="dw_mlp1", out_dtypes=(BF16,), tm=1024, tn=512, after=pin)
    dh2 = _matmul(da, wb["w_mlp1"], mode="nt", name="d_h2", tm=512, tn=512, after=(gw.order("w_mlp1"),))

    def post1_b(rv, vv):
        x_t, y_t, dx1_t, dh2_t = rv
        gr, gv = _vjp_rows(_f_post1, (x_t, y_t, *vv), (dx1_t, dh2_t), 2)
        return [gr[0], gr[1]], gv

    dx_a, dy, d_g1, d_lg_mix, d_lb_mix, d_sc2, d_sh2 = _rowwise(
        post1_b, [(x, D, 0, 0), (y, D, 0, 0), (dx1_a, D, 0, 0), (dh2, D, 0, 0)], [g1, lg_mix, lb_mix, sc2, sh2],
        [(D, F32), (D, BF16)], [(1, D)] * 5, nrows=T, tr=TR, name="post1_bwd")
    gw["w_out"] = _matmul(mix_t, dy, mode="nn", name="dw_out", out_dtypes=(BF16,), tm=1024, tn=512)
    dmix = _matmul(dy, wb["w_out"], mode="nt", name="d_mix", tm=1024, tn=512, after=(gw.order("w_out"),))

    def mix_b(rv, vv):
        gr, _ = _vjp_rows(_f_mix, tuple(rv[:4]), rv[4], 4)
        return gr, []

    d_ga, d_gs, d_attn_d, d_ssm_d = _rowwise(
        mix_b, [(proj, D, ga_cb, 0), (proj, D, gs_cb, 0), (attn_d, D, 0, 0), (ssm_d, D, 0, 0), (dmix, D, 0, 0)], [],
        [(D, BF16)] * 4, [], nrows=T, tr=TR, name="mix_bwd")
    pin = gw.finish("w_mlp1", d_ga)
    gw["w_attn_up"] = _matmul(attn, d_attn_d, mode="tn", name="dw_attn_up", out_dtypes=(BF16,), tm=512, tn=1024, tk=1024, after=pin)
    d_attn = _matmul(d_attn_d, wb["w_attn_up"], mode="nt", name="d_attn", out_dtypes=(BF16,), tm=1024, tn=512)
    gw["w_ssm_up"] = _matmul(glu, d_ssm_d, mode="tn", name="dw_ssm_up", out_dtypes=(BF16,), tm=512, tn=1024, tk=1024)
    d_glu = _matmul(d_ssm_d, wb["w_ssm_up"], mode="nt", name="d_glu", tm=1024, tn=512, after=(gw.order("w_attn_up"), gw.order("w_ssm_up")))

    def glu_b(rv, vv):
        gr, _ = _vjp_rows(_f_glu, (rv[0],), rv[1], 1)
        return gr, []

    dz, = _rowwise(glu_b, [(z, 2 * SW, 0, 0), (d_glu, SW, 0, 0)], [], [(2 * SW, BF16)], [], nrows=T, tr=TR, name="glu_bwd")
    gw["w_glu"] = _matmul(ge, dz, mode="tn", name="dw_glu", out_dtypes=(BF16,), tm=512, tn=1024, tk=1024)
    d_ge = _matmul(dz, wb["w_glu"], mode="nt", name="d_ge", tm=1024, tn=512, after=(gw.order("w_glu"),))

    du_all, d_dskip, dlam, dbmat, dcmat = _s5_bwd(d_ge, ssm, proj, dskip, states, lam, bmat, cmat)
    s5_grads = s5_pull((dlam, dbmat, dcmat))
    early = dict(zip(s5_names, s5_grads), ssm_d=d_dskip)
    if on_early is not None:
        on_early(early)
    pin = gw.finish("w_glu", du_all)

    dq, dk, dv, dsink = _attn_bwd(proj, d_attn, sink, tabs)
    zc = lambda w: jnp.zeros((C, w), BF16)
    dproj = jnp.concatenate([
        jnp.concatenate([dq, zc(QW)], 0), dk, dv, du_all,
        jnp.concatenate([d_ga, zc(D)], 0), jnp.concatenate([d_gs, zc(D)], 0)], 1)
    gw["w_in"] = _matmul(h1_t, dproj, mode="nn", name="dw_in", out_dtypes=(BF16,), tm=1024, tn=512, after=pin)
    pin = gw.finish("w_in", gw["w_in"])
    dh1 = _matmul(dproj, wb["w_in"], mode="nt", name="d_h1", tm=768, tn=512, after=pin)

    def ln1_b(rv, vv):
        x_t, dh_t, dxa_t = rv
        gr, gv = _vjp_rows(_f_ln_mod, (x_t, vv[0], vv[1]), dh_t, 1)
        return [gr[0] + dxa_t], gv

    grad_x, d_sc1, d_sh1 = _rowwise(ln1_b, [(x, D, 0, 0), (dh1, D, 0, 0), (dx_a, D, 0, 0)], [sc1, sh1],
                                    [(D, F32)], [(1, D), (1, D)], nrows=T, tr=TR, name="ln1_lat_bwd")

    def ln1c_b(rv, vv):
        _, gv = _vjp_rows(_f_ln_mod, (rv[0], vv[0], vv[1]), rv[1], 1)
        return [], gv

    d_csc1, d_csh1 = _rowwise(ln1c_b, [(ctx, D, 0, 0), (dh1, D, 0, T // TR)], [csc1, csh1],
                              [], [(1, D), (1, D)], nrows=C, tr=TR, name="ln1_ctx_bwd")

    d_mod_lat = jnp.concatenate([d_sh1, d_sc1, d_g1, d_sh2, d_sc2, d_g2], 1)
    zv = jnp.zeros((1, D), F32)
    d_mod_ctx = jnp.concatenate([d_csh1, d_csc1, zv, zv, zv, zv], 1)
    gs = {n: g for n, g in zip(s5_names, s5_grads)}
    gs["attn_sink"] = dsink[:, 0]
    gs["ssm_d"] = d_dskip
    gs["ln_mix_g"], gs["ln_mix_b"] = d_lg_mix, d_lb_mix
    gs["ln_mlp_g"], gs["ln_mlp_b"] = d_lg_mlp, d_lb_mlp
    gs["b_mlp1"], gs["b_mlp2"] = d_b1, d_b2
    return loss_p, grad_x, d_mod_lat, d_mod_ctx, gw, gs


def _my_pos():
    return lax.axis_index("x"), lax.axis_index("y"), lax.axis_index("c")


def _flip(p, bit):
    return 1 - p if bit else p


def _peer(pos, k):
    x, y, c = pos
    return (_flip(x, (k >> 2) & 1), _flip(y, (k >> 1) & 1), _flip(c, k & 1))


def _lin(pos):
    return 4 * pos[0] + 2 * pos[1] + pos[2]


def _allgather_small(v, name):
    r, w = v.shape

    def body(v_ref, out_ref, send_sems, recv_sems, local_sem):
        me = _my_pos()
        mine = pltpu.make_async_copy(v_ref, out_ref.at[_lin(me)], local_sem)
        mine.start()
        sends = []
        for k in range(1, N_DEV):
            cp = pltpu.make_async_remote_copy(src_ref=v_ref, dst_ref=out_ref.at[_lin(me)], send_sem=send_sems.at[k - 1],
                                              recv_sem=recv_sems.at[k - 1], device_id=_peer(me, k), device_id_type=MESH)
            cp.start()
            sends.append(cp)
        for k in range(1, N_DEV):
            peer = _peer(me, k)
            pltpu.make_async_remote_copy(src_ref=v_ref, dst_ref=out_ref.at[_lin(peer)], send_sem=send_sems.at[k - 1],
                                         recv_sem=recv_sems.at[k - 1], device_id=peer, device_id_type=MESH).wait_recv()
        for cp in sends:
            cp.wait_send()
        mine.wait()

    return pl.pallas_call(
        body,
        name=name,
        out_shape=jax.ShapeDtypeStruct((N_DEV, r, w), v.dtype),
        in_specs=[pl.BlockSpec(memory_space=pltpu.VMEM)],
        out_specs=pl.BlockSpec(memory_space=pltpu.VMEM),
        scratch_shapes=[pltpu.SemaphoreType.DMA((N_DEV - 1,)), pltpu.SemaphoreType.DMA((N_DEV - 1,)), pltpu.SemaphoreType.DMA],
        compiler_params=pltpu.CompilerParams(vmem_limit_bytes=VMEM_LIMIT_BYTES),
    )(v)


def _block_of(ref, kind, idx, n):
    start = pl.multiple_of(idx * n, 128)
    if kind == "col":
        return ref.at[:, pl.ds(start, n)]
    return ref.at[pl.ds(start, n), :]


def _handshake(peers):
    barrier = pltpu.get_barrier_semaphore()
    for peer in peers:
        pl.semaphore_signal(barrier, inc=1, device_id=peer, device_id_type=MESH)
    pl.semaphore_wait(barrier, len(peers))


def _allgather_weights_seq(shards, kinds, name, collective_id):
    nt = len(shards)
    hbm = pltpu.MemorySpace.HBM
    ins = [jax.new_ref(s, memory_space=hbm) for s in shards]
    outs = []
    for s, kind in zip(shards, kinds):
        k, n = s.shape
        shape = (k, n * N_DEV) if kind == "col" else (k * N_DEV, n)
        outs.append(jax.empty_ref(jax.ShapeDtypeStruct(shape, s.dtype), memory_space=hbm))

    @functools.partial(
        pl.kernel, mesh=plsc.ScalarSubcoreMesh(axis_name="seq", num_cores=1), name=name,
        scratch_types=(pltpu.SemaphoreType.DMA((nt, N_DEV - 1)), pltpu.SemaphoreType.DMA((nt, N_DEV - 1)),
                       pltpu.SemaphoreType.DMA((nt,))),
        compiler_params=pltpu.CompilerParams(collective_id=collective_id))
    def launch(send_sems, recv_sems, local_sems):
        x, y, c = _my_pos()
        me, sibling = (x, y, c), (x, y, 1 - c)
        chips = [(1 - x, y), (x, 1 - y), (1 - x, 1 - y)]
        _handshake([sibling] + [(*chip, c) for chip in chips])

        def blk(t, pos):
            n = shards[t].shape[1] if kinds[t] == "col" else shards[t].shape[0]
            return _block_of(outs[t], kinds[t], _lin(pos), n)

        def copy(t, k, block, to, src=None):
            return pltpu.make_async_remote_copy(src_ref=blk(t, block) if src is None else src, dst_ref=blk(t, block),
                                                send_sem=send_sems.at[t, k], recv_sem=recv_sems.at[t, k],
                                                device_id=to, device_id_type=MESH)

        local, sends = [], []
        for t in range(nt):
            mine = pltpu.make_async_copy(ins[t], blk(t, me), local_sems.at[t])
            mine.start()
            local.append(mine)
            first = [copy(t, 0, me, sibling, src=ins[t])]
            first += [copy(t, 1 + j, me, (*chip, c), src=ins[t]) for j, chip in enumerate(chips)]
            for cp in first:
                cp.start()
            sends += first
        for t in range(nt):
            for j, chip in enumerate(chips):
                copy(t, 1 + j, (*chip, c), me).wait_recv()
                fwd = copy(t, 4 + j, (*chip, c), sibling)
                fwd.start()
                sends.append(fwd)
        for t in range(nt):
            copy(t, 0, sibling, me).wait_recv()
            for j, chip in enumerate(chips):
                copy(t, 4 + j, (*chip, 1 - c), me).wait_recv()
        for cp in sends:
            cp.wait_send()
        for cp in local:
            cp.wait()

    launch()
    return [o[...] for o in outs]


def _allgather_small_seq(v, name, collective_id):
    hbm = pltpu.MemorySpace.HBM
    src = jax.new_ref(v, memory_space=hbm)
    out = jax.empty_ref(jax.ShapeDtypeStruct((N_DEV,) + v.shape, v.dtype), memory_space=hbm)

    @functools.partial(
        pl.kernel, mesh=plsc.ScalarSubcoreMesh(axis_name="seq", num_cores=1), name=name,
        scratch_types=(pltpu.SemaphoreType.DMA((N_DEV - 1,)), pltpu.SemaphoreType.DMA((N_DEV - 1,)), pltpu.SemaphoreType.DMA),
        compiler_params=pltpu.CompilerParams(collective_id=collective_id))
    def launch(send_sems, recv_sems, local_sem):
        me = _my_pos()
        _handshake([_peer(me, k) for k in range(1, N_DEV)])
        mine = pltpu.make_async_copy(src, out.at[_lin(me)], local_sem)
        mine.start()
        sends = []
        for k in range(1, N_DEV):
            cp = pltpu.make_async_remote_copy(src_ref=src, dst_ref=out.at[_lin(me)], send_sem=send_sems.at[k - 1],
                                              recv_sem=recv_sems.at[k - 1], device_id=_peer(me, k), device_id_type=MESH)
            cp.start()
            sends.append(cp)
        for k in range(1, N_DEV):
            peer = _peer(me, k)
            pltpu.make_async_remote_copy(src_ref=src, dst_ref=out.at[_lin(peer)], send_sem=send_sems.at[k - 1],
                                         recv_sem=recv_sems.at[k - 1], device_id=peer, device_id_type=MESH).wait_recv()
        for cp in sends:
            cp.wait_send()
        mine.wait()

    launch()
    return out[...]


N_CHIP = N_DEV // 2


def _chip_of(pos):
    return 2 * pos[0] + pos[1]


def _pair_exchange_seq(grads, kinds, name, collective_id):
    nt = len(grads)
    hbm = pltpu.MemorySpace.HBM
    shard_shapes = _shard_shapes(grads, kinds)
    ins = [jax.new_ref(g, memory_space=hbm) for g in grads]
    outs = [jax.empty_ref(jax.ShapeDtypeStruct((N_CHIP,) + s, g.dtype), memory_space=hbm) for s, g in zip(shard_shapes, grads)]

    @functools.partial(
        pl.kernel, mesh=plsc.ScalarSubcoreMesh(axis_name="seq", num_cores=1), name=name,
        scratch_types=(pltpu.SemaphoreType.DMA((nt, N_CHIP)), pltpu.SemaphoreType.DMA((nt, N_CHIP))),
        compiler_params=pltpu.CompilerParams(collective_id=collective_id))
    def launch(send_sems, recv_sems):
        x, y, c = _my_pos()
        sibling = (x, y, 1 - c)
        _handshake([sibling])
        copies = []
        for t in range(nt):
            n = shard_shapes[t][1] if kinds[t] == "col" else shard_shapes[t][0]
            for q in range(N_CHIP):
                cp = pltpu.make_async_remote_copy(src_ref=_block_of(ins[t], kinds[t], 2 * q + (1 - c), n), dst_ref=outs[t].at[q],
                                                  send_sem=send_sems.at[t, q], recv_sem=recv_sems.at[t, q],
                                                  device_id=sibling, device_id_type=MESH)
                cp.start()
                copies.append(cp)
        for cp in copies:
            cp.wait_recv()
        for cp in copies:
            cp.wait_send()

    launch()
    return [o[...] for o in outs]


def _pair_add(g, half, kind, name, after=()):
    nq, k, ns = half.shape
    tr = min(k, 512)
    c_idx = lax.axis_index("c").astype(jnp.int32).reshape(1)
    if kind == "col":
        g_spec = pl.BlockSpec((tr, ns), lambda q, i, c_ref: (i, 2 * q + c_ref[0]))
    else:
        g_spec = pl.BlockSpec((tr, ns), lambda q, i, c_ref: ((2 * q + c_ref[0]) * (k // tr) + i, 0))
    n_after = len(after)

    def kern(c_ref, g_ref, h_ref, *rest):
        o_ref = rest[n_after]
        o_ref[0] = (g_ref[...].astype(F32) + h_ref[0].astype(F32)).astype(o_ref.dtype)

    return pl.pallas_call(
        kern,
        name=name,
        grid_spec=pltpu.PrefetchScalarGridSpec(
            num_scalar_prefetch=1,
            grid=(nq, k // tr),
            in_specs=[g_spec, pl.BlockSpec((1, tr, ns), lambda q, i, c_ref: (q, i, 0))] + [pl.BlockSpec(memory_space=pl.ANY)] * n_after,
            out_specs=pl.BlockSpec((1, tr, ns), lambda q, i, c_ref: (q, i, 0)),
        ),
        out_shape=jax.ShapeDtypeStruct(half.shape, half.dtype),
        compiler_params=_cparams(("parallel", "parallel")),
    )(c_idx, g, half, *after)


def _chip_exchange_seq(psums, name, collective_id):
    nt = len(psums)
    hbm = pltpu.MemorySpace.HBM
    ins = [jax.new_ref(s, memory_space=hbm) for s in psums]
    outs = [jax.empty_ref(jax.ShapeDtypeStruct(s.shape, s.dtype), memory_space=hbm) for s in psums]

    @functools.partial(
        pl.kernel, mesh=plsc.ScalarSubcoreMesh(axis_name="seq", num_cores=1), name=name,
        scratch_types=(pltpu.SemaphoreType.DMA((nt, N_CHIP - 1)), pltpu.SemaphoreType.DMA((nt, N_CHIP - 1)),
                       pltpu.SemaphoreType.DMA((nt,))),
        compiler_params=pltpu.CompilerParams(collective_id=collective_id))
    def launch(send_sems, recv_sems, local_sems):
        me = _my_pos()
        peers = [_peer(me, k) for k in (2, 4, 6)]
        _handshake(peers)
        mine = _chip_of(me)
        local, sends = [], []
        for t in range(nt):
            cp = pltpu.make_async_copy(ins[t].at[mine], outs[t].at[mine], local_sems.at[t])
            cp.start()
            local.append(cp)
            for j, peer in enumerate(peers):
                cp = pltpu.make_async_remote_copy(src_ref=ins[t].at[_chip_of(peer)], dst_ref=outs[t].at[mine],
                                                  send_sem=send_sems.at[t, j], recv_sem=recv_sems.at[t, j],
                                                  device_id=peer, device_id_type=MESH)
                cp.start()
                sends.append(cp)
        for t in range(nt):
            for j, peer in enumerate(peers):
                pltpu.make_async_remote_copy(src_ref=ins[t].at[mine], dst_ref=outs[t].at[_chip_of(peer)],
                                             send_sem=send_sems.at[t, j], recv_sem=recv_sems.at[t, j],
                                             device_id=peer, device_id_type=MESH).wait_recv()
        for cp in sends:
            cp.wait_send()
        for cp in local:
            cp.wait()

    launch()
    return [o[...] for o in outs]


def _shard_shapes(grads, kinds):
    return [(g.shape[0], g.shape[1] // N_DEV) if kind == "col" else (g.shape[0] // N_DEV, g.shape[1]) for g, kind in zip(grads, kinds)]


def _adam(g_slots, w, m, v, *, tr, name, after=()):
    ns, r, wd = g_slots.shape
    tr = min(tr, r)
    assert r % tr == 0, (name, r, tr)
    c1 = 1.0 - ADAM_B1 ** ADAM_STEP
    c2 = 1.0 - ADAM_B2 ** ADAM_STEP
    n_after = len(after)

    def kern(g_ref, w_ref, m_ref, v_ref, *rest):
        go_ref, d_ref, mo_ref, vo_ref = rest[n_after:]
        g = g_ref[0].astype(F32)
        for s in range(1, ns):
            g = g + g_ref[s].astype(F32)
        m_new = ADAM_B1 * m_ref[...] + (1.0 - ADAM_B1) * g
        v_new = ADAM_B2 * v_ref[...] + (1.0 - ADAM_B2) * (g * g)
        m_hat = m_new / c1
        v_hat = v_new / c2
        go_ref[...] = g
        d_ref[...] = -ADAM_LR * (m_hat / (jnp.sqrt(v_hat) + ADAM_EPS) + ADAM_WD * w_ref[...])
        mo_ref[...] = m_new
        vo_ref[...] = v_new

    tile = pl.BlockSpec((tr, wd), lambda i: (i, 0))
    return pl.pallas_call(
        kern,
        name=name,
        grid=(r // tr,),
        in_specs=[pl.BlockSpec((ns, tr, wd), lambda i: (0, i, 0)), tile, tile, tile] + [pl.BlockSpec(memory_space=pl.ANY)] * n_after,
        out_specs=[tile] * 4,
        out_shape=[jax.ShapeDtypeStruct((r, wd), F32)] * 4,
        compiler_params=_cparams(("parallel",)),
    )(g_slots, w, m, v, *after)


SMALL = ("c_ctx", "b_ada", "attn_sink", "ssm_a_re", "ssm_a_im", "ssm_log_dt", "ssm_b_re", "ssm_b_im", "ssm_c_re", "ssm_c_im",
         "ssm_d", "ln_mix_g", "ln_mix_b", "b_mlp1", "b_mlp2", "ln_mlp_g", "ln_mlp_b")
BIG = ("w_in", "w_glu", "w_attn_up", "w_ssm_up", "w_out", "w_mlp1", "w_mlp2")
BIG_KIND = ("col", "col", "col", "col", "row", "col", "row")
AG_GROUPS = (("w_in",), ("w_glu", "w_attn_up", "w_ssm_up", "w_out"), ("w_mlp1",), ("w_mlp2",))
AG_COLLECTIVE_ID0 = 1
RS_GROUPS = (("w_mlp2",), ("w_mlp1",), ("w_out", "w_attn_up", "w_ssm_up", "w_glu"), ("w_in",))
RS_COLLECTIVE_ID0 = AG_COLLECTIVE_ID0 + len(AG_GROUPS)
SMALL_EARLY = ("ssm_a_re", "ssm_a_im", "ssm_log_dt", "ssm_b_re", "ssm_b_im", "ssm_c_re", "ssm_c_im", "ssm_d")
SMALL_LATE = tuple(n for n in SMALL if n not in SMALL_EARLY)
SMALL_COLLECTIVE_ID0 = RS_COLLECTIVE_ID0 + 2 * len(RS_GROUPS)
LANES = 128


def _pack(parts):
    rows = []
    for p in parts:
        flat = p.reshape(-1).astype(F32)
        pad = (-flat.shape[0]) % LANES
        rows.append(jnp.pad(flat, (0, pad)).reshape(-1, LANES))
    packed = jnp.concatenate(rows, 0)
    return jnp.pad(packed, ((0, (-packed.shape[0]) % 8), (0, 0)))


def _unpack(packed, shapes):
    out, r0 = [], 0
    for s in shapes:
        n = math.prod(s)
        nr = -(-n // LANES)
        out.append(packed[r0:r0 + nr].reshape(-1)[:n].reshape(s))
        r0 += nr
    return out


WEIGHTS = ("c_ctx", "w_ada", "b_ada", "w_in", "attn_sink", "ssm_a_re", "ssm_a_im", "ssm_log_dt", "ssm_b_re", "ssm_b_im",
           "ssm_c_re", "ssm_c_im", "ssm_d", "w_glu", "w_attn_up", "w_ssm_up", "w_out", "ln_mix_g", "ln_mix_b", "w_mlp1",
           "b_mlp1", "w_mlp2", "b_mlp2", "ln_mlp_g", "ln_mlp_b")
ADA_COLS = 6 * D // N_DEV


def _step(x, c, ctx, loss_target, p, m, v):
    me = _lin(_my_pos())
    x2, ctx2, tgt2 = x[0], ctx[0], loss_target[0]

    wb = {}
    for gi, group in enumerate(AG_GROUPS):
        full = _allgather_weights_seq([p[n][0].astype(BF16) for n in group], [BIG_KIND[BIG.index(n)] for n in group],
                                      "allgather_seq%d" % gi, AG_COLLECTIVE_ID0 + gi)
        wb.update(zip(group, full))

    c_all = _allgather_small(jnp.broadcast_to(c, (8, D)), "gather_c")[:, 0, :]
    cc = p["c_ctx"].reshape(1, D)
    s_in = jnp.concatenate([c_all, cc, jnp.zeros((7, D), F32)], 0)
    s_act, = _rowwise(lambda rv, vv: ([_silu(rv[0])], []), [(s_in, D, 0, 0)], [], [(D, F32)], [], nrows=16, tr=16, name="silu_c")
    b_mine = lax.dynamic_slice_in_dim(p["b_ada"], me * ADA_COLS, ADA_COLS, axis=1)
    mod_part = _matmul(s_act, p["w_ada"][0], mode="nn", name="ada_fwd", tm=16, tn=512, bias=b_mine)
    mod_all = _allgather_small(mod_part, "gather_mod")
    mod_lat = lax.dynamic_index_in_dim(mod_all, me, axis=1, keepdims=False).reshape(1, 6 * D)
    mod_ctx = mod_all[:, 8, :].reshape(1, 6 * D)

    sp = {n: p[n][0] for n in SMALL if n not in ("c_ctx", "b_ada")}
    recv, halves = {}, {}

    def on_grad(gw):
        for gi, group in enumerate(RS_GROUPS):
            if gi not in halves and all(n in gw for n in group):
                kinds = [BIG_KIND[BIG.index(n)] for n in group]
                halves[gi] = (dict(gw), _pair_exchange_seq([gw[n] for n in group], kinds, "pair_exchange%d" % gi, RS_COLLECTIVE_ID0 + 2 * gi))

    def on_finish(key, after):
        gi = [i for i, group in enumerate(RS_GROUPS) if key in group][0]
        group = RS_GROUPS[gi]
        grads, half = halves[gi]
        prev = tuple(recv[n] for n in RS_GROUPS[gi - 1][:1]) if gi else ()
        if gi == len(RS_GROUPS) - 1:
            prev += (small["early"],)
        psums =[_pair_add(grads[n], h, BIG_KIND[BIG.index(n)], "pair_add_" + n, after=(after,) + prev) for n, h in zip(group, half)]
        recv.update(zip(group, _chip_exchange_seq(psums, "chip_exchange%d" % gi, RS_COLLECTIVE_ID0 + 2 * gi + 1)))
        return psums[-1]

    small = {}

    def on_early(gs_early):
        small["early"] = _allgather_small_seq(_pack([gs_early[n] for n in SMALL_EARLY]), "gather_small_early", SMALL_COLLECTIVE_ID0)

    total = {}

    def on_loss(loss_p):
        total["loss"] = lax.psum(loss_p[0, 0], ("x", "y", "c"))
        return total["loss"].reshape(1, 1)

    loss_p, grad_x, d_mod_lat, d_mod_ctx, gw, gs = _local_step(x2, ctx2, tgt2, mod_lat, mod_ctx, wb, sp, on_grad, on_loss, on_finish, on_early)

    g_early = small["early"]
    res = {}
    last = ()

    def adam_small(names, g_pack, tag, after):
        sm = _adam(g_pack, _pack([p[n] for n in names]), _pack([m[n] for n in names]), _pack([v[n] for n in names]),
                   tr=g_pack.shape[1], name="adam_small_" + tag, after=after)
        shapes = [p[n].shape for n in names]
        for j, outs in enumerate(zip(*[_unpack(a, shapes) for a in sm])):
            res[names[j]] = outs
        return (sm[0],)

    for gi, group in enumerate(RS_GROUPS):
        if gi == len(RS_GROUPS) - 1:
            last = adam_small(SMALL_EARLY, g_early, "early", last)
        for n in group:
            res[n] = _adam(recv[n], p[n][0], m[n][0], v[n][0], tr=256, name="adam_" + n, after=last)
            last = (res[n][0],)

    dm = jnp.concatenate([d_mod_lat, d_mod_ctx, jnp.zeros((6, 6 * D), F32)], 0)
    dm_all = _allgather_small_seq(dm, "gather_dmod", SMALL_COLLECTIVE_ID0 + 1)
    dm_all = lax.optimization_barrier((dm_all,) + last)[0]
    dm2 = jnp.concatenate([dm_all[:, 0, :], dm_all[:, 1, :]], 0)
    dm2_mine = lax.dynamic_slice_in_dim(dm2, me * ADA_COLS, ADA_COLS, axis=1)
    s2 = jnp.concatenate([s_act[0:8], jnp.broadcast_to(s_act[8:9], (8, D))], 0)
    g_w_ada = _matmul(s2, dm2_mine, mode="tn", name="dw_ada", tm=512, tn=ADA_COLS, after=last)
    dsc_part = _matmul(dm2_mine[8:16], p["w_ada"][0], mode="nt", name="d_silu_cctx", tm=8, tn=512, after=last)

    def cctx_b(rv, vv):
        _, pull = jax.vjp(_silu, vv[0])
        return [], [pull(jnp.sum(rv[0], axis=0, keepdims=True))[0]]

    g_cctx, = _rowwise(cctx_b, [(dsc_part, D, 0, 0)], [cc], [], [(1, D)], nrows=8, tr=8, name="cctx_bwd")
    gs["c_ctx"] = g_cctx
    gs["b_ada"] = d_mod_lat + d_mod_ctx

    res["w_ada"] = _adam(g_w_ada[None], p["w_ada"][0], m["w_ada"][0], v["w_ada"][0], tr=256, name="adam_w_ada")

    g_late = _allgather_small_seq(_pack([gs[n] for n in SMALL_LATE]), "gather_small_late", SMALL_COLLECTIVE_ID0 + 2)
    adam_small(SMALL_LATE, g_late, "late", (res["w_ada"][0],))

    outs = [total["loss"], grad_x[None]]
    for j in range(4):
        outs += [res[n][j].reshape(p[n].shape) for n in WEIGHTS]
    return tuple(outs)


def kernel(x, c, ctx, c_ctx, w_ada, b_ada, w_in, attn_sink, ssm_a_re, ssm_a_im, ssm_log_dt, ssm_b_re, ssm_b_im, ssm_c_re, ssm_c_im, ssm_d, w_glu, w_attn_up, w_ssm_up, w_out, ln_mix_g, ln_mix_b, w_mlp1, b_mlp1, w_mlp2, b_mlp2, ln_mlp_g, ln_mlp_b, loss_target, m_c_ctx, m_w_ada, m_b_ada, m_w_in, m_attn_sink, m_ssm_a_re, m_ssm_a_im, m_ssm_log_dt, m_ssm_b_re, m_ssm_b_im, m_ssm_c_re, m_ssm_c_im, m_ssm_d, m_w_glu, m_w_attn_up, m_w_ssm_up, m_w_out, m_ln_mix_g, m_ln_mix_b, m_w_mlp1, m_b_mlp1, m_w_mlp2, m_b_mlp2, m_ln_mlp_g, m_ln_mlp_b, v_c_ctx, v_w_ada, v_b_ada, v_w_in, v_attn_sink, v_ssm_a_re, v_ssm_a_im, v_ssm_log_dt, v_ssm_b_re, v_ssm_b_im, v_ssm_c_re, v_ssm_c_im, v_ssm_d, v_w_glu, v_w_attn_up, v_w_ssm_up, v_w_out, v_ln_mix_g, v_ln_mix_b, v_w_mlp1, v_b_mlp1, v_w_mlp2, v_b_mlp2, v_ln_mlp_g, v_ln_mlp_b):
    given = dict(locals())
    p = {n: given[n] for n in WEIGHTS}
    m = {n: given["m_" + n] for n in WEIGHTS}
    v = {n: given["v_" + n] for n in WEIGHTS}
    return _step(x, c, ctx, loss_target, p, m, v)
```

```python
import functools
import math

import jax
import jax.numpy as jnp
from jax import lax
from jax.experimental import pallas as pl
from jax.experimental.pallas import tpu as pltpu
from jax.experimental.pallas import tpu_sc as plsc

F32 = jnp.float32
BF16 = jnp.bfloat16

N_DEV = 8
D = 2048
T = 2048
C = 256
TA = T + C
GRID_W = 64
HD = 128
NH = 8
NKV = 2
GROUP = NH // NKV
WINDOW = 128
QW = NH * HD
KVW = NKV * HD
SW = D // 4
SG = 16
NG = SW // SG
SP = 64
DFF = 4 * D
IN_COLS = QW + 2 * KVW + SW + 2 * D
ALPHA = 2.0 ** 0.25
LN_EPS = 1e-6
NEG_INF = -1e30
ROPE_BASE = 10000.0
ATT_SCALE = HD ** -0.5

NSEG = 8
GBLK = 8
NBLK = NG // GBLK
BW = GBLK * SP
UW = GBLK * SG

ADAM_LR = 0.001
ADAM_B1 = 0.9
ADAM_B2 = 0.999
ADAM_EPS = 1e-08
ADAM_WD = 0.01
ADAM_STEP = 10

VMEM_LIMIT_BYTES = 56 * 1024 * 1024
MESH = pl.DeviceIdType.MESH


def _cparams(sem=None):
    return pltpu.CompilerParams(dimension_semantics=sem, vmem_limit_bytes=VMEM_LIMIT_BYTES)


def _matmul(a, b, *, mode, name, out_dtypes=(F32,), tm=512, tn=512, tk=None, bias=None, extras=(), epilogue=None, after=(),
            out_t=None):
    if mode == "nn":
        (M, K), (K2, N) = a.shape, b.shape
    elif mode == "nt":
        (M, K), (N, K2) = a.shape, b.shape
    else:
        (K, M), (K2, N) = a.shape, b.shape
    assert K == K2, (name, a.shape, b.shape)
    tm, tn, tk = min(tm, M), min(tn, N), min(tk or K, K)
    assert M % tm == 0 and N % tn == 0 and K % tk == 0, (name, M, N, K, tm, tn, tk)
    nk = K // tk
    if mode == "tn":
        a_spec = pl.BlockSpec((tk, tm), lambda i, j, k: (k, i))
    else:
        a_spec = pl.BlockSpec((tm, tk), lambda i, j, k: (i, k))
    if mode == "nt":
        b_spec = pl.BlockSpec((tn, tk), lambda i, j, k: (j, k))
    else:
        b_spec = pl.BlockSpec((tk, tn), lambda i, j, k: (k, j))
    dims = {"nn": (((1,), (0,)), ((), ())), "nt": (((1,), (1,)), ((), ())), "tn": (((0,), (0,)), ((), ()))}[mode]
    in_specs = [a_spec, b_spec]
    operands = [a, b]
    if bias is not None:
        in_specs.append(pl.BlockSpec((1, tn), lambda i, j, k: (0, j)))
        operands.append(bias)
    for e in extras:
        in_specs.append(pl.BlockSpec((tm, tn), lambda i, j, k: (i, j)))
        operands.append(e)
    n_ex = len(extras)
    for t in after:
        in_specs.append(pl.BlockSpec(memory_space=pl.ANY))
        operands.append(t)
    n_after = len(after)
    n_out = len(out_dtypes)
    out_t = tuple(out_t) if out_t is not None else (False,) * n_out
    has_bias = bias is not None

    def kern(*refs):
        a_ref, b_ref = refs[0], refs[1]
        pos = 2
        bias_ref = None
        if has_bias:
            bias_ref = refs[pos]
            pos += 1
        ex_refs = refs[pos:pos + n_ex]
        pos += n_ex + n_after
        out_refs = refs[pos:pos + n_out]
        acc_ref = refs[pos + n_out] if nk > 1 else None

        def finish(r):
            if has_bias:
                r = r + bias_ref[...]
            outs = epilogue(r, *[e[...] for e in ex_refs]) if epilogue is not None else (r,)
            for o_ref, o, tr_ in zip(out_refs, outs, out_t):
                o_ref[...] = (o.T if tr_ else o).astype(o_ref.dtype)

        part = lax.dot_general(a_ref[...].astype(BF16), b_ref[...].astype(BF16), dims, preferred_element_type=F32)
        if nk == 1:
            finish(part)
        else:
            k = pl.program_id(2)

            @pl.when(k == 0)
            def _():
                acc_ref[...] = part

            @pl.when(k > 0)
            def _():
                acc_ref[...] += part

            @pl.when(k == nk - 1)
            def _():
                finish(acc_ref[...])

    outs = pl.pallas_call(
        kern,
        name=name,
        grid=(M // tm, N // tn, nk),
        in_specs=in_specs,
        out_specs=[pl.BlockSpec((tn, tm), lambda i, j, k: (j, i)) if tr_ else pl.BlockSpec((tm, tn), lambda i, j, k: (i, j))
                   for tr_ in out_t],
        out_shape=[jax.ShapeDtypeStruct((N, M) if tr_ else (M, N), dt) for dt, tr_ in zip(out_dtypes, out_t)],
        scratch_shapes=[pltpu.VMEM((tm, tn), F32)] if nk > 1 else [],
        compiler_params=_cparams(("parallel", "parallel", "arbitrary")),
    )(*operands)
    return outs[0] if n_out == 1 else tuple(outs)


def _rowwise(fn, rows, vecs, outs, vec_outs, *, nrows, tr, name, after=()):
    n_rows, n_vecs, n_outs, n_after = len(rows), len(vecs), len(outs), len(after)
    in_specs = [pl.BlockSpec((tr, w), lambda i, cb=cb, ro=ro: (i + ro, cb)) for (_, w, cb, ro) in rows]
    in_specs += [pl.BlockSpec(v.shape, lambda i: (0, 0)) for v in vecs]
    in_specs += [pl.BlockSpec(memory_space=pl.ANY)] * n_after
    outs = [o if len(o) == 3 else (*o, False) for o in outs]
    out_specs = [pl.BlockSpec((w, tr), lambda i: (0, i)) if tr_ else pl.BlockSpec((tr, w), lambda i: (i, 0)) for (w, _, tr_) in outs]
    out_specs += [pl.BlockSpec(s, lambda i: (0, 0)) for s in vec_outs]
    out_shape = [jax.ShapeDtypeStruct((w, nrows) if tr_ else (nrows, w), dt) for (w, dt, tr_) in outs]
    out_tr = [tr_ for (_, _, tr_) in outs]
    out_shape += [jax.ShapeDtypeStruct(s, F32) for s in vec_outs]

    def kern(*refs):
        rvals = [r[...] for r in refs[:n_rows]]
        vvals = [r[...] for r in refs[n_rows:n_rows + n_vecs]]
        first_out = n_rows + n_vecs + n_after
        o_refs = refs[first_out:first_out + n_outs]
        v_refs = refs[first_out + n_outs:]
        ro, vo = fn(rvals, vvals)
        for r, val, tr_ in zip(o_refs, ro, out_tr):
            r[...] = (val.astype(F32).T if tr_ else val).astype(r.dtype)
        i = pl.program_id(0)
        for r, val in zip(v_refs, vo):
            @pl.when(i == 0)
            def _(r=r, val=val):
                r[...] = val.astype(F32)

            @pl.when(i > 0)
            def _(r=r, val=val):
                r[...] += val.astype(F32)

    res = pl.pallas_call(
        kern,
        name=name,
        grid=(nrows // tr,),
        in_specs=in_specs,
        out_specs=out_specs,
        out_shape=out_shape,
        compiler_params=_cparams(("arbitrary",)),
    )(*[r[0] for r in rows], *vecs, *after)
    return list(res)


def _ln(x):
    mu = jnp.mean(x, axis=-1, keepdims=True)
    xc = x - mu
    var = jnp.mean(xc * xc, axis=-1, keepdims=True)
    return xc * lax.rsqrt(var + LN_EPS)


def _sigmoid(x):
    return 1.0 / (1.0 + jnp.exp(-x))


def _gelu(x):
    return 0.5 * x * (1.0 + jnp.tanh(math.sqrt(2.0 / math.pi) * (x + 0.044715 * (x * x * x))))


def _silu(x):
    return x * _sigmoid(x)


def _f_ln_mod(x, sc, sh):
    return _ln(x) * (1.0 + sc) + sh


def _f_glu(z):
    return z[:, :SW] * _sigmoid(z[:, SW:])


def _f_mix(ga, gs, attn_d, ssm_d):
    return _sigmoid(ga) * attn_d + _sigmoid(gs) * ssm_d


def _f_post1(x, y, g1, lg, lb, sc2, sh2):
    r1 = ALPHA * x + g1 * y
    x1 = _ln(r1) * lg + lb
    h2 = _ln(x1) * (1.0 + sc2) + sh2
    return x1, h2


def _f_loss(x1, mlp, tgt, g2, lg, lb, b2z):
    r2 = ALPHA * x1 + g2 * (mlp + b2z)
    out = _ln(r2) * lg + lb
    err = out - tgt
    return 0.5 * jnp.sum(err * err) * (1.0 / D)


def _rope_tables():
    rows = T // GRID_W
    row = jnp.repeat(jnp.arange(rows), GRID_W)
    col = jnp.tile(jnp.arange(GRID_W), rows)
    n_freq = HD // 4
    freqs = ROPE_BASE ** (-jnp.arange(n_freq, dtype=F32) / n_freq)
    ang_r = row.astype(F32)[:, None] * freqs
    ang_c = col.astype(F32)[:, None] * freqs
    ang = jnp.concatenate([ang_r, ang_r, ang_c, ang_c], -1)
    cos, sin = jnp.cos(ang), jnp.sin(ang)
    lo = (jnp.arange(HD) % (HD // 2)) < (HD // 4)
    sin_a = jnp.where(lo[None, :], -sin, 0.0)
    sin_b = jnp.where(lo[None, :], 0.0, sin)
    return cos, sin_a, sin_b


def _rope(x, cos, sa, sb):
    return x * cos + pltpu.roll(x, 96, 1) * sa + pltpu.roll(x, 32, 1) * sb


def _rope_t(dy, cos, sa, sb):
    return dy * cos + pltpu.roll(dy * sa, 32, 1) + pltpu.roll(dy * sb, 96, 1)


BAND = 3 * WINDOW
KPAD = T + 2 * WINDOW


def _attn_fill_kv(k_ref, v_ref, cos_ref, sa_ref, sb_ref, kp, vp, kc, vc):
    zeros = jnp.zeros((WINDOW, KVW), BF16)
    kp[0:WINDOW, :] = zeros
    kp[WINDOW + T:KPAD, :] = zeros
    vp[0:WINDOW, :] = zeros
    vp[WINDOW + T:KPAD, :] = zeros
    for hh in range(NKV):
        cs = slice(hh * HD, (hh + 1) * HD)
        for r0 in range(0, T, 512):
            rs = slice(r0, r0 + 512)
            kr = _rope(k_ref[rs, cs], cos_ref[rs, :], sa_ref[rs, :], sb_ref[rs, :])
            kp[WINDOW + r0:WINDOW + r0 + 512, cs] = kr.astype(BF16)
    vp[WINDOW:WINDOW + T, :] = v_ref[0:T, :].astype(BF16)
    kc[...] = k_ref[T:TA, :].astype(BF16)
    vc[...] = v_ref[T:TA, :].astype(BF16)


GROWS = GROUP * WINDOW


def _attn_scores(n, kvh, q_ref, cos_ref, sa_ref, sb_ref, sink_ref, kp, kc):
    r0 = pl.multiple_of(n * WINDOW, WINDOW)
    cos = cos_ref[pl.ds(r0, WINDOW), :]
    sa = sa_ref[pl.ds(r0, WINDOW), :]
    sb = sb_ref[pl.ds(r0, WINDOW), :]
    heads = range(kvh * GROUP, (kvh + 1) * GROUP)
    q_g = jnp.concatenate([_rope(q_ref[:, h * HD:(h + 1) * HD], cos, sa, sb).astype(BF16) for h in heads], axis=0)
    kb = kp[pl.ds(r0, BAND), kvh * HD:(kvh + 1) * HD]
    kcb = kc[:, kvh * HD:(kvh + 1) * HD]
    nt = (((1,), (1,)), ((), ()))
    s_loc = lax.dot_general(q_g, kb, nt, preferred_element_type=F32) * ATT_SCALE
    s_ctx = lax.dot_general(q_g, kcb, nt, preferred_element_type=F32) * ATT_SCALE
    row = lax.broadcasted_iota(jnp.int32, (GROWS, BAND), 0) & (WINDOW - 1)
    col = lax.broadcasted_iota(jnp.int32, (GROWS, BAND), 1)
    rel = col - WINDOW - row
    kpos = r0 - WINDOW + col
    valid = (jnp.abs(rel) <= WINDOW) & (kpos >= 0) & (kpos < T)
    s_loc = jnp.where(valid, s_loc, NEG_INF)
    sk = jnp.concatenate([jnp.broadcast_to(sink_ref[0:1, h:h + 1], (WINDOW, 1)) for h in heads], axis=0)
    m = jnp.maximum(jnp.maximum(jnp.max(s_loc, -1, keepdims=True), jnp.max(s_ctx, -1, keepdims=True)), sk)
    e_loc = jnp.exp(s_loc - m)
    e_ctx = jnp.exp(s_ctx - m)
    e_sink = jnp.exp(sk - m)
    inv = 1.0 / (jnp.sum(e_loc, -1, keepdims=True) + jnp.sum(e_ctx, -1, keepdims=True) + e_sink)
    return q_g, r0, e_loc * inv, e_ctx * inv, e_sink * inv


def _attn_fwd(proj, sink, tabs):
    cos, sa, sb = tabs

    def kern(q_ref, k_ref, v_ref, cos_ref, sa_ref, sb_ref, sink_ref, o_ref, kp, vp, kc, vc):
        n = pl.program_id(0)

        @pl.when(n == 0)
        def _():
            _attn_fill_kv(k_ref, v_ref, cos_ref, sa_ref, sb_ref, kp, vp, kc, vc)

        for kvh in range(NKV):
            _, r0, p_loc, p_ctx, _ = _attn_scores(n, kvh, q_ref, cos_ref, sa_ref, sb_ref, sink_ref, kp, kc)
            vb = vp[pl.ds(r0, BAND), kvh * HD:(kvh + 1) * HD]
            vcb = vc[:, kvh * HD:(kvh + 1) * HD]
            o = jnp.dot(p_loc.astype(BF16), vb, preferred_element_type=F32)
            o = o + jnp.dot(p_ctx.astype(BF16), vcb, preferred_element_type=F32)
            for g in range(GROUP):
                h = kvh * GROUP + g
                o_ref[:, h * HD:(h + 1) * HD] = o[g * WINDOW:(g + 1) * WINDOW, :].astype(o_ref.dtype)

    full = lambda shape: pl.BlockSpec(shape, lambda n: (0, 0))
    return pl.pallas_call(
        kern,
        name="attn_fwd",
        grid=(T // WINDOW,),
        in_specs=[
            pl.BlockSpec((WINDOW, QW), lambda n: (n, 0)),
            pl.BlockSpec((TA, KVW), lambda n: (0, QW // KVW)),
            pl.BlockSpec((TA, KVW), lambda n: (0, QW // KVW + 1)),
            full((T, HD)), full((T, HD)), full((T, HD)), full((1, NH)),
        ],
        out_specs=pl.BlockSpec((WINDOW, QW), lambda n: (n, 0)),
        out_shape=jax.ShapeDtypeStruct((T, QW), BF16),
        scratch_shapes=[pltpu.VMEM((KPAD, KVW), BF16), pltpu.VMEM((KPAD, KVW), BF16),
                        pltpu.VMEM((C, KVW), BF16), pltpu.VMEM((C, KVW), BF16)],
        compiler_params=_cparams(("arbitrary",)),
    )(proj, proj, proj, cos, sa, sb, sink)


def _attn_bwd(proj, d_attn, sink, tabs):
    cos, sa, sb = tabs
    n_blocks = T // WINDOW

    def kern(q_ref, k_ref, v_ref, do_ref, cos_ref, sa_ref, sb_ref, sink_ref,
             dq_ref, dk_ref, dv_ref, dsink_ref, kp, vp, kc, vc, dkp, dvp, dkc, dvc):
        n = pl.program_id(0)

        @pl.when(n == 0)
        def _():
            _attn_fill_kv(k_ref, v_ref, cos_ref, sa_ref, sb_ref, kp, vp, kc, vc)
            dkp[...] = jnp.zeros_like(dkp)
            dvp[...] = jnp.zeros_like(dvp)
            dkc[...] = jnp.zeros_like(dkc)
            dvc[...] = jnp.zeros_like(dvc)
            dsink_ref[...] = jnp.zeros_like(dsink_ref)

        nt = (((1,), (1,)), ((), ()))
        tn = (((0,), (0,)), ((), ()))
        for kvh in range(NKV):
            cs = slice(kvh * HD, (kvh + 1) * HD)
            heads = range(kvh * GROUP, (kvh + 1) * GROUP)
            q_g, r0, p_loc, p_ctx, p_sink = _attn_scores(n, kvh, q_ref, cos_ref, sa_ref, sb_ref, sink_ref, kp, kc)
            kb = kp[pl.ds(r0, BAND), cs]
            vb = vp[pl.ds(r0, BAND), cs]
            kcb = kc[:, cs]
            vcb = vc[:, cs]
            do_g = jnp.concatenate([do_ref[:, h * HD:(h + 1) * HD] for h in heads], axis=0)
            dp_loc = lax.dot_general(do_g, vb, nt, preferred_element_type=F32)
            dp_ctx = lax.dot_general(do_g, vcb, nt, preferred_element_type=F32)
            delta = jnp.sum(p_loc * dp_loc, -1, keepdims=True) + jnp.sum(p_ctx * dp_ctx, -1, keepdims=True)
            ds_loc = (p_loc * (dp_loc - delta) * ATT_SCALE).astype(BF16)
            ds_ctx = (p_ctx * (dp_ctx - delta) * ATT_SCALE).astype(BF16)
            dq = jnp.dot(ds_loc, kb, preferred_element_type=F32) + jnp.dot(ds_ctx, kcb, preferred_element_type=F32)
            cos = cos_ref[pl.ds(r0, WINDOW), :]
            sa_ = sa_ref[pl.ds(r0, WINDOW), :]
            sb_ = sb_ref[pl.ds(r0, WINDOW), :]
            dkp[pl.ds(r0, BAND), cs] += lax.dot_general(ds_loc, q_g, tn, preferred_element_type=F32)
            dkc[:, cs] += lax.dot_general(ds_ctx, q_g, tn, preferred_element_type=F32)
            dvp[pl.ds(r0, BAND), cs] += lax.dot_general(p_loc.astype(BF16), do_g, tn, preferred_element_type=F32)
            dvc[:, cs] += lax.dot_general(p_ctx.astype(BF16), do_g, tn, preferred_element_type=F32)
            dsk_rows = p_sink * delta
            for g, h in enumerate(heads):
                rs = slice(g * WINDOW, (g + 1) * WINDOW)
                dq_ref[:, h * HD:(h + 1) * HD] = _rope_t(dq[rs, :], cos, sa_, sb_).astype(dq_ref.dtype)
                dsk = -jnp.sum(dsk_rows[rs, :], axis=0, keepdims=True)
                dsink_ref[h:h + 1, :] += jnp.broadcast_to(dsk, (1, HD))

        @pl.when(n == n_blocks - 1)
        def _():
            for hh in range(NKV):
                cs = slice(hh * HD, (hh + 1) * HD)
                for r0 in range(0, T, 512):
                    rs = slice(r0, r0 + 512)
                    g = dkp[WINDOW + r0:WINDOW + r0 + 512, cs]
                    dk_ref[rs, cs] = _rope_t(g, cos_ref[rs, :], sa_ref[rs, :], sb_ref[rs, :]).astype(dk_ref.dtype)
            dk_ref[T:TA, :] = dkc[...].astype(dk_ref.dtype)
            dv_ref[0:T, :] = dvp[WINDOW:WINDOW + T, :].astype(dv_ref.dtype)
            dv_ref[T:TA, :] = dvc[...].astype(dv_ref.dtype)

    full = lambda shape: pl.BlockSpec(shape, lambda n: (0, 0))
    return pl.pallas_call(
        kern,
        name="attn_bwd",
        grid=(n_blocks,),
        in_specs=[
            pl.BlockSpec((WINDOW, QW), lambda n: (n, 0)),
            pl.BlockSpec((TA, KVW), lambda n: (0, QW // KVW)),
            pl.BlockSpec((TA, KVW), lambda n: (0, QW // KVW + 1)),
            pl.BlockSpec((WINDOW, QW), lambda n: (n, 0)),
            full((T, HD)), full((T, HD)), full((T, HD)), full((1, NH)),
        ],
        out_specs=[pl.BlockSpec((WINDOW, QW), lambda n: (n, 0)), full((TA, KVW)), full((TA, KVW)), full((NH, HD))],
        out_shape=[jax.ShapeDtypeStruct((T, QW), BF16), jax.ShapeDtypeStruct((TA, KVW), BF16),
                   jax.ShapeDtypeStruct((TA, KVW), BF16), jax.ShapeDtypeStruct((NH, HD), F32)],
        scratch_shapes=[pltpu.VMEM((KPAD, KVW), BF16), pltpu.VMEM((KPAD, KVW), BF16),
                        pltpu.VMEM((C, KVW), BF16), pltpu.VMEM((C, KVW), BF16),
                        pltpu.VMEM((KPAD, KVW), F32), pltpu.VMEM((KPAD, KVW), F32),
                        pltpu.VMEM((C, KVW), F32), pltpu.VMEM((C, KVW), F32)],
        compiler_params=_cparams(("arbitrary",)),
    )(proj, proj, proj, d_attn, cos, sa, sb, sink)


def _s5_prep(a_re, a_im, log_dt, b_re, b_im, c_re, c_im):
    lam = lax.complex(a_re, a_im)
    dt = jnp.exp(log_dt)[..., None]
    lam_bar = jnp.exp(lam * dt)
    b_bar = ((lam_bar - 1.0) / lam)[..., None] * lax.complex(b_re, b_im)
    def lam_rows(v):
        return v.reshape(2, NBLK, 1, BW)

    lam_l = jnp.concatenate([lam_rows(jnp.real(lam_bar)), lam_rows(jnp.imag(lam_bar))], -1)
    lam_l = jnp.broadcast_to(lam_l, (2, NBLK, 8, 2 * BW))
    diag = (jnp.arange(UW)[:, None] // SG) == (jnp.arange(BW)[None, :] // SP)

    def blocks(v):
        return jnp.where(diag, jnp.tile(v.reshape(2, NBLK, UW, SP), (1, 1, 1, GBLK)), 0.0)

    b_t = jnp.swapaxes(b_bar, -1, -2)
    bmat = jnp.concatenate([blocks(jnp.real(b_t)), blocks(jnp.imag(b_t))], -1)
    cmat = jnp.concatenate([blocks(c_re), -blocks(c_im)], -1)
    return lam_l, bmat, cmat


def _cmul(ar, ai, br, bi):
    return ar * br - ai * bi, ar * bi + ai * br


def _shift_rows(x, rev, fill):
    r = lax.broadcasted_iota(jnp.int32, x.shape, 0)
    down = jnp.where(r == 0, fill, pltpu.roll(x, 1, 0))
    up = jnp.where(r == NSEG - 1, fill, pltpu.roll(x, NSEG - 1, 0))
    return jnp.where(rev == 0, down, up)


def _edge_row(x, rev):
    last = jnp.broadcast_to(x[NSEG - 1:NSEG, :], x.shape)
    first = jnp.broadcast_to(x[0:1, :], x.shape)
    return jnp.where(rev == 0, last, first)


def _seg_scan(get, put, base, seglen, lr, li, rev, cin, acc_fn=None, acc0=()):
    zero = jnp.zeros((NSEG, BW), F32)

    def rows(k):
        j = jnp.where(rev == 0, k, seglen - 1 - k)
        return pl.ds(pl.multiple_of(base + j * NSEG, NSEG), NSEG)

    def local(k, carry):
        sr, si = carry
        xr, xi = get(rows(k))
        tr, ti = _cmul(lr, li, sr, si)
        sr, si = tr + xr, ti + xi
        put(rows(k), sr, si)
        return sr, si

    er, ei = lax.fori_loop(0, seglen, local, (zero, zero))
    lpr, lpi = lr, li
    assert seglen & (seglen - 1) == 0, seglen
    for _ in range(seglen.bit_length() - 1):
        lpr, lpi = _cmul(lpr, lpi, lpr, lpi)
    cr, ci = _shift_rows(zero, rev, cin[0]), _shift_rows(zero, rev, cin[1])
    for _ in range(NSEG - 1):
        tr, ti = _cmul(lpr, lpi, cr, ci)
        cr, ci = _shift_rows(er + tr, rev, cin[0]), _shift_rows(ei + ti, rev, cin[1])

    def fix(k, carry):
        tr, ti = _cmul(lr, li, carry[0], carry[1])
        xr, xi = get(rows(k))
        fr, fi = xr + tr, xi + ti
        put(rows(k), fr, fi)
        if acc_fn is None:
            return tr, ti
        j = jnp.where(rev == 0, k, seglen - 1 - k)
        return (tr, ti) + tuple(acc_fn(j, fr, fi, carry[2:]))

    out = lax.fori_loop(0, seglen, fix, (cr, ci) + tuple(acc0))
    tr, ti = out[0], out[1]
    leaving = (_edge_row(er + tr, rev), _edge_row(ei + ti, rev))
    return leaving if acc_fn is None else (leaving, out[2:])


RCH = 256
CSEG = C // NSEG
TSEG = T // NSEG
UCOL0 = (QW + 2 * KVW) // UW


REGIONS = ((0, TSEG), (T, CSEG))


def _state_access(ref, lead=()):
    def get(rows):
        return ref[(*lead, rows, slice(0, BW))], ref[(*lead, rows, slice(BW, 2 * BW))]

    def put(rows, re, im):
        ref[(*lead, rows, slice(0, BW))] = re
        ref[(*lead, rows, slice(BW, 2 * BW))] = im

    return get, put


def _interleave_rows(src_ref, dst_ref, regions=REGIONS):
    for base, seglen in regions:
        def body(j, carry, base=base, seglen=seglen):
            dst_ref[pl.ds(pl.multiple_of(base + j * NSEG, NSEG), NSEG), :] = src_ref[pl.ds(base + j, NSEG, stride=seglen), :]
            return carry

        lax.fori_loop(0, seglen, body, 0, unroll=8)


def _deinterleave_rows(src_ref, dst_ref, regions=REGIONS):
    for base, seglen in regions:
        def body(j, carry, base=base, seglen=seglen):
            dst_ref[pl.ds(base + j, NSEG, stride=seglen), :] = src_ref[pl.ds(pl.multiple_of(base + j * NSEG, NSEG), NSEG), :]
            return carry

        lax.fori_loop(0, seglen, body, 0, unroll=8)


def _s5_fwd(proj, dskip, lam, bmat, cmat):
    def kern(u_ref, dk_ref, lam_ref, b_ref, c_ref, s_ref, ssm_ref, ge_ref, up_ref, yp_ref):
        d = pl.program_id(1)

        @pl.when(d == 0)
        def _():
            _interleave_rows(u_ref, up_ref)

        bm = b_ref[0, 0].astype(BF16)
        for r0 in range(0, TA, RCH):
            s_ref[0, 0, r0:r0 + RCH, :] = jnp.dot(up_ref[r0:r0 + RCH, :].astype(BF16), bm, preferred_element_type=F32)
        lr = lam_ref[0, 0, :, 0:BW]
        li = lam_ref[0, 0, :, BW:2 * BW]
        zero = jnp.zeros((NSEG, BW), F32)
        get, put = _state_access(s_ref, (0, 0))
        mid = _seg_scan(get, put, T, CSEG, lr, li, d, (zero, zero))
        _seg_scan(get, put, 0, TSEG, lr, li, d, mid)
        cm = c_ref[0, 0].astype(BF16)
        for r0 in range(0, T, RCH):
            y = lax.dot_general(s_ref[0, 0, r0:r0 + RCH, :].astype(BF16), cm, (((1,), (1,)), ((), ())), preferred_element_type=F32)

            @pl.when(d == 0)
            def _(y=y, r0=r0):
                yp_ref[r0:r0 + RCH, :] = y + dk_ref[...] * up_ref[r0:r0 + RCH, :]

            @pl.when(d == 1)
            def _(y=y, r0=r0):
                yp_ref[r0:r0 + RCH, :] += y

        @pl.when(d == 1)
        def _():
            _deinterleave_rows(yp_ref, ssm_ref, REGIONS[:1])
            for r0 in range(0, T, RCH):
                ge_ref[r0:r0 + RCH, :] = _gelu(ssm_ref[r0:r0 + RCH, :]).astype(ge_ref.dtype)

    blk4 = lambda shape: pl.BlockSpec((1, 1) + shape, lambda b, d: (d, b, 0, 0))
    return pl.pallas_call(
        kern,
        name="s5_fwd",
        grid=(NBLK, 2),
        in_specs=[pl.BlockSpec((TA, UW), lambda b, d: (0, UCOL0 + b)), pl.BlockSpec((1, UW), lambda b, d: (0, b)),
                  blk4((8, 2 * BW)), blk4((UW, 2 * BW)), blk4((UW, 2 * BW))],
        out_specs=[blk4((TA, 2 * BW)), pl.BlockSpec((T, UW), lambda b, d: (0, b)), pl.BlockSpec((T, UW), lambda b, d: (0, b))],
        out_shape=[jax.ShapeDtypeStruct((2, NBLK, TA, 2 * BW), F32), jax.ShapeDtypeStruct((T, SW), F32),
                   jax.ShapeDtypeStruct((T, SW), BF16)],
        scratch_shapes=[pltpu.VMEM((TA, UW), F32), pltpu.VMEM((T, UW), F32)],
        compiler_params=_cparams(("parallel", "arbitrary")),
    )(proj, dskip, lam, bmat, cmat)


def _s5_bwd(d_ge, ssm, proj, dskip, states, lam, bmat, cmat):
    nt = (((1,), (1,)), ((), ()))
    tn = (((0,), (0,)), ((), ()))

    def kern(dge_ref, ssm_ref, u_ref, dk_ref, s_ref, lam_ref, b_ref, c_ref,
             du_ref, ddk_ref, dlam_ref, db_ref, dc_ref, g_ref, dua_ref, dssm_ref, up_ref, nat_ref):
        d = pl.program_id(1)

        @pl.when(d == 0)
        def _():
            ddk = jnp.zeros((1, UW), F32)
            for r0 in range(0, T, RCH):
                rs = slice(r0, r0 + RCH)
                _, pull = jax.vjp(_gelu, ssm_ref[rs, :])
                dssm = pull(dge_ref[rs, :])[0]
                nat_ref[rs, :] = dssm
                ddk = ddk + jnp.sum(dssm * u_ref[rs, :], axis=0, keepdims=True)
            ddk_ref[...] = ddk
            _interleave_rows(nat_ref, dssm_ref, REGIONS[:1])
            _interleave_rows(u_ref, up_ref)
            for r0 in range(0, T, RCH):
                dua_ref[r0:r0 + RCH, :] = dssm_ref[r0:r0 + RCH, :] * dk_ref[...]
            dua_ref[T:TA, :] = jnp.zeros((C, UW), F32)

        cm = c_ref[0, 0].astype(BF16)
        for r0 in range(0, T, RCH):
            g_ref[r0:r0 + RCH, :] = jnp.dot(dssm_ref[r0:r0 + RCH, :].astype(BF16), cm, preferred_element_type=F32)
        g_ref[T:TA, :] = jnp.zeros((C, 2 * BW), F32)
        lr = lam_ref[0, 0, :, 0:BW]
        li = lam_ref[0, 0, :, BW:2 * BW]
        zero = jnp.zeros((NSEG, BW), F32)
        get_g, put_g = _state_access(g_ref)

        get_s, _ = _state_access(s_ref, (0, 0))

        def dlam_fold(base, seglen, s_in):
            def rows(j):
                return pl.ds(pl.multiple_of(base + j * NSEG, NSEG), NSEG)

            jb = jnp.where(d == 0, 0, seglen - 1)
            jn = jnp.where(d == 0, seglen - 1, 0)
            sp = get_s(rows(jn))
            edge = (_shift_rows(sp[0], d, s_in[0]), _shift_rows(sp[1], d, s_in[1]))

            def fold(j, gr, gi, acc):
                jp = jnp.clip(jnp.where(d == 0, j - 1, j + 1), 0, seglen - 1)
                sr, si = get_s(rows(jp))
                sr = jnp.where(j == jb, edge[0], sr)
                si = jnp.where(j == jb, edge[1], si)
                return acc[0] + (gr * sr + gi * si), acc[1] + (gi * sr - gr * si)

            return fold

        r_mid = jnp.where(d == 0, TA - 1, T)
        s_mid = tuple(jnp.broadcast_to(t, (NSEG, BW)) for t in get_s(pl.ds(r_mid, 1)))
        mid, acc = _seg_scan(get_g, put_g, 0, TSEG, lr, -li, 1 - d, (zero, zero), dlam_fold(0, TSEG, s_mid), (zero, zero))
        _, acc = _seg_scan(get_g, put_g, T, CSEG, lr, -li, 1 - d, mid, dlam_fold(T, CSEG, (zero, zero)), acc)
        dlam_ref[0, 0, :, 0:BW] = acc[0]
        dlam_ref[0, 0, :, BW:2 * BW] = acc[1]

        bm = b_ref[0, 0].astype(BF16)
        db = jnp.zeros((UW, 2 * BW), F32)
        dc = jnp.zeros((UW, 2 * BW), F32)
        for r0 in range(0, TA, RCH):
            rs = slice(r0, r0 + RCH)
            g = g_ref[rs, :].astype(BF16)
            dua_ref[rs, :] += lax.dot_general(g, bm, nt, preferred_element_type=F32)
            db = db + lax.dot_general(up_ref[rs, :].astype(BF16), g, tn, preferred_element_type=F32)
            if r0 < T:
                dc = dc + lax.dot_general(dssm_ref[rs, :].astype(BF16), s_ref[0, 0, rs, :].astype(BF16), tn,
                                          preferred_element_type=F32)
        db_ref[0, 0] = db
        dc_ref[0, 0] = dc

        @pl.when(d == 1)
        def _():
            _deinterleave_rows(dua_ref, nat_ref)
            du_ref[...] = nat_ref[...].astype(du_ref.dtype)

    blk4 = lambda shape: pl.BlockSpec((1, 1) + shape, lambda b, d: (d, b, 0, 0))
    lat = pl.BlockSpec((T, UW), lambda b, d: (0, b))
    vec = pl.BlockSpec((1, UW), lambda b, d: (0, b))
    return pl.pallas_call(
        kern,
        name="s5_bwd",
        grid=(NBLK, 2),
        in_specs=[lat, lat, pl.BlockSpec((TA, UW), lambda b, d: (0, UCOL0 + b)), vec,
                  blk4((TA, 2 * BW)), blk4((8, 2 * BW)), blk4((UW, 2 * BW)), blk4((UW, 2 * BW))],
        out_specs=[pl.BlockSpec((TA, UW), lambda b, d: (0, b)), vec, blk4((8, 2 * BW)), blk4((UW, 2 * BW)), blk4((UW, 2 * BW))],
        out_shape=[jax.ShapeDtypeStruct((TA, SW), BF16), jax.ShapeDtypeStruct((1, SW), F32),
                   jax.ShapeDtypeStruct((2, NBLK, 8, 2 * BW), F32),
                   jax.ShapeDtypeStruct((2, NBLK, UW, 2 * BW), F32), jax.ShapeDtypeStruct((2, NBLK, UW, 2 * BW), F32)],
        scratch_shapes=[pltpu.VMEM((TA, 2 * BW), F32), pltpu.VMEM((TA, UW), F32), pltpu.VMEM((T, UW), F32),
                        pltpu.VMEM((TA, UW), F32), pltpu.VMEM((TA, UW), F32)],
        compiler_params=_cparams(("parallel", "arbitrary")),
    )(d_ge, ssm, proj, dskip, states, lam, bmat, cmat)


TR = 256


def _vjp_rows(f, primals, cots, n_row):
    _, pull = jax.vjp(f, *primals)
    g = pull(cots)
    return list(g[:n_row]), list(g[n_row:])


class _GradDict(dict):
    def __init__(self, on_set=None):
        super().__init__()
        self._on_set = on_set
        self.tokens = {}

    def __setitem__(self, key, value):
        super().__setitem__(key, value)
        if self._on_set is not None:
            self._on_set(self)

    def order(self, key):
        return self.tokens.get(key, self.get(key))

    def finish(self, key, after):
        if self.on_finish is None:
            return ()
        return (self.on_finish(key, after),)

    on_finish = None


def _local_step(x, ctx, tgt, mod_lat, mod_ctx, wb, sp, on_grad=None, on_loss=None, on_finish=None, on_early=None):
    sh1, sc1, g1, sh2, sc2, g2 = [mod_lat[:, i * D:(i + 1) * D] for i in range(6)]
    csh1, csc1 = mod_ctx[:, 0:D], mod_ctx[:, D:2 * D]
    tabs = _rope_tables()
    sink = sp["attn_sink"].reshape(1, NH)
    dskip = sp["ssm_d"].reshape(1, SW)
    lg_mix, lb_mix = sp["ln_mix_g"].reshape(1, D), sp["ln_mix_b"].reshape(1, D)
    lg_mlp, lb_mlp = sp["ln_mlp_g"].reshape(1, D), sp["ln_mlp_b"].reshape(1, D)
    b1, b2 = sp["b_mlp1"].reshape(1, DFF), sp["b_mlp2"].reshape(1, D)
    s5_names = ("ssm_a_re", "ssm_a_im", "ssm_log_dt", "ssm_b_re", "ssm_b_im", "ssm_c_re", "ssm_c_im")
    (lam, bmat, cmat), s5_pull = jax.vjp(_s5_prep, *[sp[n] for n in s5_names])

    def ln_mod2(rv, vv):
        h = _f_ln_mod(rv[0], vv[0], vv[1])
        return [h, h], []

    h_lat, h_lat_t = _rowwise(ln_mod2, [(x, D, 0, 0)], [sc1, sh1], [(D, BF16), (D, BF16, True)], [], nrows=T, tr=TR, name="ln1_lat")
    h_ctx, h_ctx_t = _rowwise(ln_mod2, [(ctx, D, 0, 0)], [csc1, csh1], [(D, BF16), (D, BF16, True)], [], nrows=C, tr=TR,
                              name="ln1_ctx")
    h1 = jnp.concatenate([h_lat, h_ctx], 0)
    h1_t = jnp.concatenate([h_lat_t, h_ctx_t], 1)
    proj = _matmul(h1, wb["w_in"], mode="nn", name="proj", tm=768, tn=512)
    attn = _attn_fwd(proj, sink, tabs)
    states, ssm, ge = _s5_fwd(proj, dskip, lam, bmat, cmat)
    z = _matmul(ge, wb["w_glu"], mode="nn", name="glu_mm", tm=1024, tn=1024)

    def glu_act(rv, vv):
        return [_f_glu(rv[0])], []

    glu, = _rowwise(glu_act, [(z, 2 * SW, 0, 0)], [], [(SW, BF16)], [], nrows=T, tr=TR, name="glu_act")
    attn_d = _matmul(attn, wb["w_attn_up"], mode="nn", name="attn_up", tm=1024, tn=512)
    ssm_d = _matmul(glu, wb["w_ssm_up"], mode="nn", name="ssm_up", tm=1024, tn=512)
    ga_cb, gs_cb = (QW + 2 * KVW + SW) // D, (QW + 2 * KVW + SW) // D + 1

    def mix(rv, vv):
        m_ = _f_mix(*rv)
        return [m_, m_], []

    mixv, mix_t = _rowwise(mix, [(proj, D, ga_cb, 0), (proj, D, gs_cb, 0), (attn_d, D, 0, 0), (ssm_d, D, 0, 0)], [],
                           [(D, BF16), (D, BF16, True)], [], nrows=T, tr=TR, name="mix")
    y = _matmul(mixv, wb["w_out"], mode="nn", name="out_proj", tm=1024, tn=512)

    def post1(rv, vv):
        x1, h2 = _f_post1(rv[0], rv[1], *vv)
        return [x1, h2, h2], []

    x1, h2, h2_t = _rowwise(post1, [(x, D, 0, 0), (y, D, 0, 0)], [g1, lg_mix, lb_mix, sc2, sh2],
                            [(D, F32), (D, BF16), (D, BF16, True)], [], nrows=T, tr=TR, name="post1")

    def relu_sq(acc):
        r = jnp.maximum(acc, 0.0)
        return r, r * r, r * r

    r_act, act, act_t = _matmul(h2, wb["w_mlp1"], mode="nn", name="mlp1", tm=1024, tn=512, bias=b1,
                                out_dtypes=(BF16, BF16, BF16), out_t=(False, False, True), epilogue=relu_sq)
    mlp = _matmul(act, wb["w_mlp2"], mode="nn", name="mlp2", tm=512, tn=512)

    def loss_fb(rv, vv):
        x1_t, mlp_t, tgt_t = rv
        g2_v, lg_v, lb_v, b2_v = vv
        f = lambda a, m, g, p, q, b: _f_loss(a, m, tgt_t, g, p, q, b)
        val, grads = jax.value_and_grad(f, argnums=(0, 1, 2, 3, 4, 5))(x1_t, mlp_t, g2_v, lg_v, lb_v, b2_v)
        dx1, dmlp, dg2, dlg, dlb, db2 = grads
        return [dx1, dmlp], [jnp.reshape(val, (1, 1)), dg2, dlg, dlb, db2]

    dx1_a, d_mlp, loss_p, d_g2, d_lg_mlp, d_lb_mlp, d_b2 = _rowwise(
        loss_fb, [(x1, D, 0, 0), (mlp, D, 0, 0), (tgt, D, 0, 0)], [g2, lg_mlp, lb_mlp, b2],
        [(D, F32), (D, BF16)], [(1, 1), (1, D), (1, D), (1, D), (1, D)], nrows=T, tr=TR, name="loss_fb")

    gw = _GradDict(on_grad)
    gw.on_finish = on_finish
    loss_done = () if on_loss is None else (on_loss(loss_p),)
    gw["w_mlp2"] = _matmul(act_t, d_mlp, mode="nn", name="dw_mlp2", out_dtypes=(BF16,), tm=1024, tn=512, after=loss_done)
    da, = (_matmul(d_mlp, wb["w_mlp2"], mode="nt", name="d_act", out_dtypes=(BF16,), tm=1024, tn=512,
                   extras=(r_act,), epilogue=lambda acc, r: (acc * (2.0 * r.astype(F32)),), after=(gw.order("w_mlp2"),)),)
    pin = gw.finish("w_mlp2", da)
    ones = jnp.ones((8, T), BF16)
    d_b1 = _matmul(ones, da, mode="nn", name="db_mlp1", tm=8, tn=2048)[0:1]
    gw["w_mlp1"] = _matmul(h2_t, da, mode="nn", name="dw_mlp1", out_dtypes=(BF16,), tm=1024, tn=512, after=pin)
    dh2 = _matmul(da, wb["w_mlp1"], mode="nt", name="d_h2", tm=512, tn=512, after=(gw.order("w_mlp1"),))

    def post1_b(rv, vv):
        x_t, y_t, dx1_t, dh2_t = rv
        gr, gv = _vjp_rows(_f_post1, (x_t, y_t, *vv), (dx1_t, dh2_t), 2)
        return [gr[0], gr[1]], gv

    dx_a, dy, d_g1, d_lg_mix, d_lb_mix, d_sc2, d_sh2 = _rowwise(
        post1_b, [(x, D, 0, 0), (y, D, 0, 0), (dx1_a, D, 0, 0), (dh2, D, 0, 0)], [g1, lg_mix, lb_mix, sc2, sh2],
        [(D, F32), (D, BF16)], [(1, D)] * 5, nrows=T, tr=TR, name="post1_bwd")
    gw["w_out"] = _matmul(mix_t, dy, mode="nn", name="dw_out", out_dtypes=(BF16,), tm=1024, tn=512)
    dmix = _matmul(dy, wb["w_out"], mode="nt", name="d_mix", tm=1024, tn=512, after=(gw.order("w_out"),))

    def mix_b(rv, vv):
        gr, _ = _vjp_rows(_f_mix, tuple(rv[:4]), rv[4], 4)
        return gr, []

    d_ga, d_gs, d_attn_d, d_ssm_d = _rowwise(
        mix_b, [(proj, D, ga_cb, 0), (proj, D, gs_cb, 0), (attn_d, D, 0, 0), (ssm_d, D, 0, 0), (dmix, D, 0, 0)], [],
        [(D, BF16)] * 4, [], nrows=T, tr=TR, name="mix_bwd")
    pin = gw.finish("w_mlp1", d_ga)
    gw["w_attn_up"] = _matmul(attn, d_attn_d, mode="tn", name="dw_attn_up", out_dtypes=(BF16,), tm=512, tn=1024, tk=1024, after=pin)
    d_attn = _matmul(d_attn_d, wb["w_attn_up"], mode="nt", name="d_attn", out_dtypes=(BF16,), tm=1024, tn=512)
    gw["w_ssm_up"] = _matmul(glu, d_ssm_d, mode="tn", name="dw_ssm_up", out_dtypes=(BF16,), tm=512, tn=1024, tk=1024)
    d_glu = _matmul(d_ssm_d, wb["w_ssm_up"], mode="nt", name="d_glu", tm=1024, tn=512, after=(gw.order("w_attn_up"), gw.order("w_ssm_up")))

    def glu_b(rv, vv):
        gr, _ = _vjp_rows(_f_glu, (rv[0],), rv[1], 1)
        return gr, []

    dz, = _rowwise(glu_b, [(z, 2 * SW, 0, 0), (d_glu, SW, 0, 0)], [], [(2 * SW, BF16)], [], nrows=T, tr=TR, name="glu_bwd")
    gw["w_glu"] = _matmul(ge, dz, mode="tn", name="dw_glu", out_dtypes=(BF16,), tm=512, tn=1024, tk=1024)
    d_ge = _matmul(dz, wb["w_glu"], mode="nt", name="d_ge", tm=1024, tn=512, after=(gw.order("w_glu"),))

    du_all, d_dskip, dlam, dbmat, dcmat = _s5_bwd(d_ge, ssm, proj, dskip, states, lam, bmat, cmat)
    s5_grads = s5_pull((dlam, dbmat, dcmat))
    early = dict(zip(s5_names, s5_grads), ssm_d=d_dskip)
    if on_early is not None:
        on_early(early)
    pin = gw.finish("w_glu", du_all)

    dq, dk, dv, dsink = _attn_bwd(proj, d_attn, sink, tabs)
    zc = lambda w: jnp.zeros((C, w), BF16)
    dproj = jnp.concatenate([
        jnp.concatenate([dq, zc(QW)], 0), dk, dv, du_all,
        jnp.concatenate([d_ga, zc(D)], 0), jnp.concatenate([d_gs, zc(D)], 0)], 1)
    gw["w_in"] = _matmul(h1_t, dproj, mode="nn", name="dw_in", out_dtypes=(BF16,), tm=1024, tn=512, after=pin)
    pin = gw.finish("w_in", gw["w_in"])
    dh1 = _matmul(dproj, wb["w_in"], mode="nt", name="d_h1", tm=768, tn=512, after=pin)

    def ln1_b(rv, vv):
        x_t, dh_t, dxa_t = rv
        gr, gv = _vjp_rows(_f_ln_mod, (x_t, vv[0], vv[1]), dh_t, 1)
        return [gr[0] + dxa_t], gv

    grad_x, d_sc1, d_sh1 = _rowwise(ln1_b, [(x, D, 0, 0), (dh1, D, 0, 0), (dx_a, D, 0, 0)], [sc1, sh1],
                                    [(D, F32)], [(1, D), (1, D)], nrows=T, tr=TR, name="ln1_lat_bwd")

    def ln1c_b(rv, vv):
        _, gv = _vjp_rows(_f_ln_mod, (rv[0], vv[0], vv[1]), rv[1], 1)
        return [], gv

    d_csc1, d_csh1 = _rowwise(ln1c_b, [(ctx, D, 0, 0), (dh1, D, 0, T // TR)], [csc1, csh1],
                              [], [(1, D), (1, D)], nrows=C, tr=TR, name="ln1_ctx_bwd")

    d_mod_lat = jnp.concatenate([d_sh1, d_sc1, d_g1, d_sh2, d_sc2, d_g2], 1)
    zv = jnp.zeros((1, D), F32)
    d_mod_ctx = jnp.concatenate([d_csh1, d_csc1, zv, zv, zv, zv], 1)
    gs = {n: g for n, g in zip(s5_names, s5_grads)}
    gs["attn_sink"] = dsink[:, 0]
    gs["ssm_d"] = d_dskip
    gs["ln_mix_g"], gs["ln_mix_b"] = d_lg_mix, d_lb_mix
    gs["ln_mlp_g"], gs["ln_mlp_b"] = d_lg_mlp, d_lb_mlp
    gs["b_mlp1"], gs["b_mlp2"] = d_b1, d_b2
    return loss_p, grad_x, d_mod_lat, d_mod_ctx, gw, gs


def _my_pos():
    return lax.axis_index("x"), lax.axis_index("y"), lax.axis_index("c")


def _flip(p, bit):
    return 1 - p if bit else p


def _peer(pos, k):
    x, y, c = pos
    return (_flip(x, (k >> 2) & 1), _flip(y, (k >> 1) & 1), _flip(c, k & 1))


def _lin(pos):
    return 4 * pos[0] + 2 * pos[1] + pos[2]


def _allgather_small(v, name):
    r, w = v.shape

    def body(v_ref, out_ref, send_sems, recv_sems, local_sem):
        me = _my_pos()
        mine = pltpu.make_async_copy(v_ref, out_ref.at[_lin(me)], local_sem)
        mine.start()
        sends = []
        for k in range(1, N_DEV):
            cp = pltpu.make_async_remote_copy(src_ref=v_ref, dst_ref=out_ref.at[_lin(me)], send_sem=send_sems.at[k - 1],
                                              recv_sem=recv_sems.at[k - 1], device_id=_peer(me, k), device_id_type=MESH)
            cp.start()
            sends.append(cp)
        for k in range(1, N_DEV):
            peer = _peer(me, k)
            pltpu.make_async_remote_copy(src_ref=v_ref, dst_ref=out_ref.at[_lin(peer)], send_sem=send_sems.at[k - 1],
                                         recv_sem=recv_sems.at[k - 1], device_id=peer, device_id_type=MESH).wait_recv()
        for cp in sends:
            cp.wait_send()
        mine.wait()

    return pl.pallas_call(
        body,
        name=name,
        out_shape=jax.ShapeDtypeStruct((N_DEV, r, w), v.dtype),
        in_specs=[pl.BlockSpec(memory_space=pltpu.VMEM)],
        out_specs=pl.BlockSpec(memory_space=pltpu.VMEM),
        scratch_shapes=[pltpu.SemaphoreType.DMA((N_DEV - 1,)), pltpu.SemaphoreType.DMA((N_DEV - 1,)), pltpu.SemaphoreType.DMA],
        compiler_params=pltpu.CompilerParams(vmem_limit_bytes=VMEM_LIMIT_BYTES),
    )(v)


def _block_of(ref, kind, idx, n):
    start = pl.multiple_of(idx * n, 128)
    if kind == "col":
        return ref.at[:, pl.ds(start, n)]
    return ref.at[pl.ds(start, n), :]


def _handshake(peers):
    barrier = pltpu.get_barrier_semaphore()
    for peer in peers:
        pl.semaphore_signal(barrier, inc=1, device_id=peer, device_id_type=MESH)
    pl.semaphore_wait(barrier, len(peers))


def _allgather_weights_seq(shards, kinds, name, collective_id):
    nt = len(shards)
    hbm = pltpu.MemorySpace.HBM
    ins = [jax.new_ref(s, memory_space=hbm) for s in shards]
    outs = []
    for s, kind in zip(shards, kinds):
        k, n = s.shape
        shape = (k, n * N_DEV) if kind == "col" else (k * N_DEV, n)
        outs.append(jax.empty_ref(jax.ShapeDtypeStruct(shape, s.dtype), memory_space=hbm))

    @functools.partial(
        pl.kernel, mesh=plsc.ScalarSubcoreMesh(axis_name="seq", num_cores=1), name=name,
        scratch_types=(pltpu.SemaphoreType.DMA((nt, N_DEV - 1)), pltpu.SemaphoreType.DMA((nt, N_DEV - 1)),
                       pltpu.SemaphoreType.DMA((nt,))),
        compiler_params=pltpu.CompilerParams(collective_id=collective_id))
    def launch(send_sems, recv_sems, local_sems):
        x, y, c = _my_pos()
        me, sibling = (x, y, c), (x, y, 1 - c)
        chips = [(1 - x, y), (x, 1 - y), (1 - x, 1 - y)]
        _handshake([sibling] + [(*chip, c) for chip in chips])

        def blk(t, pos):
            n = shards[t].shape[1] if kinds[t] == "col" else shards[t].shape[0]
            return _block_of(outs[t], kinds[t], _lin(pos), n)

        def copy(t, k, block, to, src=None):
            return pltpu.make_async_remote_copy(src_ref=blk(t, block) if src is None else src, dst_ref=blk(t, block),
                                                send_sem=send_sems.at[t, k], recv_sem=recv_sems.at[t, k],
                                                device_id=to, device_id_type=MESH)

        local, sends = [], []
        for t in range(nt):
            mine = pltpu.make_async_copy(ins[t], blk(t, me), local_sems.at[t])
            mine.start()
            local.append(mine)
            first = [copy(t, 0, me, sibling, src=ins[t])]
            first += [copy(t, 1 + j, me, (*chip, c), src=ins[t]) for j, chip in enumerate(chips)]
            for cp in first:
                cp.start()
            sends += first
        for t in range(nt):
            for j, chip in enumerate(chips):
                copy(t, 1 + j, (*chip, c), me).wait_recv()
                fwd = copy(t, 4 + j, (*chip, c), sibling)
                fwd.start()
                sends.append(fwd)
        for t in range(nt):
            copy(t, 0, sibling, me).wait_recv()
            for j, chip in enumerate(chips):
                copy(t, 4 + j, (*chip, 1 - c), me).wait_recv()
        for cp in sends:
            cp.wait_send()
        for cp in local:
            cp.wait()

    launch()
    return [o[...] for o in outs]


def _allgather_small_seq(v, name, collective_id):
    hbm = pltpu.MemorySpace.HBM
    src = jax.new_ref(v, memory_space=hbm)
    out = jax.empty_ref(jax.ShapeDtypeStruct((N_DEV,) + v.shape, v.dtype), memory_space=hbm)

    @functools.partial(
        pl.kernel, mesh=plsc.ScalarSubcoreMesh(axis_name="seq", num_cores=1), name=name,
        scratch_types=(pltpu.SemaphoreType.DMA((N_DEV - 1,)), pltpu.SemaphoreType.DMA((N_DEV - 1,)), pltpu.SemaphoreType.DMA),
        compiler_params=pltpu.CompilerParams(collective_id=collective_id))
    def launch(send_sems, recv_sems, local_sem):
        me = _my_pos()
        _handshake([_peer(me, k) for k in range(1, N_DEV)])
        mine = pltpu.make_async_copy(src, out.at[_lin(me)], local_sem)
        mine.start()
        sends = []
        for k in range(1, N_DEV):
            cp = pltpu.make_async_remote_copy(src_ref=src, dst_ref=out.at[_lin(me)], send_sem=send_sems.at[k - 1],
                                              recv_sem=recv_sems.at[k - 1], device_id=_peer(me, k), device_id_type=MESH)
            cp.start()
            sends.append(cp)
        for k in range(1, N_DEV):
            peer = _peer(me, k)
            pltpu.make_async_remote_copy(src_ref=src, dst_ref=out.at[_lin(peer)], send_sem=send_sems.at[k - 1],
                                         recv_sem=recv_sems.at[k - 1], device_id=peer, device_id_type=MESH).wait_recv()
        for cp in sends:
            cp.wait_send()
        mine.wait()

    launch()
    return out[...]


N_CHIP = N_DEV // 2


def _chip_of(pos):
    return 2 * pos[0] + pos[1]


def _pair_exchange_seq(grads, kinds, name, collective_id):
    nt = len(grads)
    hbm = pltpu.MemorySpace.HBM
    shard_shapes = _shard_shapes(grads, kinds)
    ins = [jax.new_ref(g, memory_space=hbm) for g in grads]
    outs = [jax.empty_ref(jax.ShapeDtypeStruct((N_CHIP,) + s, g.dtype), memory_space=hbm) for s, g in zip(shard_shapes, grads)]

    @functools.partial(
        pl.kernel, mesh=plsc.ScalarSubcoreMesh(axis_name="seq", num_cores=1), name=name,
        scratch_types=(pltpu.SemaphoreType.DMA((nt, N_CHIP)), pltpu.SemaphoreType.DMA((nt, N_CHIP))),
        compiler_params=pltpu.CompilerParams(collective_id=collective_id))
    def launch(send_sems, recv_sems):
        x, y, c = _my_pos()
        sibling = (x, y, 1 - c)
        _handshake([sibling])
        copies = []
        for t in range(nt):
            n = shard_shapes[t][1] if kinds[t] == "col" else shard_shapes[t][0]
            for q in range(N_CHIP):
                cp = pltpu.make_async_remote_copy(src_ref=_block_of(ins[t], kinds[t], 2 * q + (1 - c), n), dst_ref=outs[t].at[q],
                                                  send_sem=send_sems.at[t, q], recv_sem=recv_sems.at[t, q],
                                                  device_id=sibling, device_id_type=MESH)
                cp.start()
                copies.append(cp)
        for cp in copies:
            cp.wait_recv()
        for cp in copies:
            cp.wait_send()

    launch()
    return [o[...] for o in outs]


def _pair_add(g, half, kind, name, after=()):
    nq, k, ns = half.shape
    tr = min(k, 512)
    c_idx = lax.axis_index("c").astype(jnp.int32).reshape(1)
    if kind == "col":
        g_spec = pl.BlockSpec((tr, ns), lambda q, i, c_ref: (i, 2 * q + c_ref[0]))
    else:
        g_spec = pl.BlockSpec((tr, ns), lambda q, i, c_ref: ((2 * q + c_ref[0]) * (k // tr) + i, 0))
    n_after = len(after)

    def kern(c_ref, g_ref, h_ref, *rest):
        o_ref = rest[n_after]
        o_ref[0] = (g_ref[...].astype(F32) + h_ref[0].astype(F32)).astype(o_ref.dtype)

    return pl.pallas_call(
        kern,
        name=name,
        grid_spec=pltpu.PrefetchScalarGridSpec(
            num_scalar_prefetch=1,
            grid=(nq, k // tr),
            in_specs=[g_spec, pl.BlockSpec((1, tr, ns), lambda q, i, c_ref: (q, i, 0))] + [pl.BlockSpec(memory_space=pl.ANY)] * n_after,
            out_specs=pl.BlockSpec((1, tr, ns), lambda q, i, c_ref: (q, i, 0)),
        ),
        out_shape=jax.ShapeDtypeStruct(half.shape, half.dtype),
        compiler_params=_cparams(("parallel", "parallel")),
    )(c_idx, g, half, *after)


def _chip_exchange_seq(psums, name, collective_id):
    nt = len(psums)
    hbm = pltpu.MemorySpace.HBM
    ins = [jax.new_ref(s, memory_space=hbm) for s in psums]
    outs = [jax.empty_ref(jax.ShapeDtypeStruct(s.shape, s.dtype), memory_space=hbm) for s in psums]

    @functools.partial(
        pl.kernel, mesh=plsc.ScalarSubcoreMesh(axis_name="seq", num_cores=1), name=name,
        scratch_types=(pltpu.SemaphoreType.DMA((nt, N_CHIP - 1)), pltpu.SemaphoreType.DMA((nt, N_CHIP - 1)),
                       pltpu.SemaphoreType.DMA((nt,))),
        compiler_params=pltpu.CompilerParams(collective_id=collective_id))
    def launch(send_sems, recv_sems, local_sems):
        me = _my_pos()
        peers = [_peer(me, k) for k in (2, 4, 6)]
        _handshake(peers)
        mine = _chip_of(me)
        local, sends = [], []
        for t in range(nt):
            cp = pltpu.make_async_copy(ins[t].at[mine], outs[t].at[mine], local_sems.at[t])
            cp.start()
            local.append(cp)
            for j, peer in enumerate(peers):
                cp = pltpu.make_async_remote_copy(src_ref=ins[t].at[_chip_of(peer)], dst_ref=outs[t].at[mine],
                                                  send_sem=send_sems.at[t, j], recv_sem=recv_sems.at[t, j],
                                                  device_id=peer, device_id_type=MESH)
                cp.start()
                sends.append(cp)
        for t in range(nt):
            for j, peer in enumerate(peers):
                pltpu.make_async_remote_copy(src_ref=ins[t].at[mine], dst_ref=outs[t].at[_chip_of(peer)],
                                             send_sem=send_sems.at[t, j], recv_sem=recv_sems.at[t, j],
                                             device_id=peer, device_id_type=MESH).wait_recv()
        for cp in sends:
            cp.wait_send()
        for cp in local:
            cp.wait()

    launch()
    return [o[...] for o in outs]


def _shard_shapes(grads, kinds):
    return [(g.shape[0], g.shape[1] // N_DEV) if kind == "col" else (g.shape[0] // N_DEV, g.shape[1]) for g, kind in zip(grads, kinds)]


def _adam(g_slots, w, m, v, *, tr, name, after=()):
    ns, r, wd = g_slots.shape
    tr = min(tr, r)
    assert r % tr == 0, (name, r, tr)
    c1 = 1.0 - ADAM_B1 ** ADAM_STEP
    c2 = 1.0 - ADAM_B2 ** ADAM_STEP
    n_after = len(after)

    def kern(g_ref, w_ref, m_ref, v_ref, *rest):
        go_ref, d_ref, mo_ref, vo_ref = rest[n_after:]
        g = g_ref[0].astype(F32)
        for s in range(1, ns):
            g = g + g_ref[s].astype(F32)
        m_new = ADAM_B1 * m_ref[...] + (1.0 - ADAM_B1) * g
        v_new = ADAM_B2 * v_ref[...] + (1.0 - ADAM_B2) * (g * g)
        m_hat = m_new / c1
        v_hat = v_new / c2
        go_ref[...] = g
        d_ref[...] = -ADAM_LR * (m_hat / (jnp.sqrt(v_hat) + ADAM_EPS) + ADAM_WD * w_ref[...])
        mo_ref[...] = m_new
        vo_ref[...] = v_new

    tile = pl.BlockSpec((tr, wd), lambda i: (i, 0))
    return pl.pallas_call(
        kern,
        name=name,
        grid=(r // tr,),
        in_specs=[pl.BlockSpec((ns, tr, wd), lambda i: (0, i, 0)), tile, tile, tile] + [pl.BlockSpec(memory_space=pl.ANY)] * n_after,
        out_specs=[tile] * 4,
        out_shape=[jax.ShapeDtypeStruct((r, wd), F32)] * 4,
        compiler_params=_cparams(("parallel",)),
    )(g_slots, w, m, v, *after)


SMALL = ("c_ctx", "b_ada", "attn_sink", "ssm_a_re", "ssm_a_im", "ssm_log_dt", "ssm_b_re", "ssm_b_im", "ssm_c_re", "ssm_c_im",
         "ssm_d", "ln_mix_g", "ln_mix_b", "b_mlp1", "b_mlp2", "ln_mlp_g", "ln_mlp_b")
BIG = ("w_in", "w_glu", "w_attn_up", "w_ssm_up", "w_out", "w_mlp1", "w_mlp2")
BIG_KIND = ("col", "col", "col", "col", "row", "col", "row")
AG_GROUPS = (("w_in",), ("w_glu", "w_attn_up", "w_ssm_up", "w_out"), ("w_mlp1",), ("w_mlp2",))
AG_COLLECTIVE_ID0 = 1
RS_GROUPS = (("w_mlp2",), ("w_mlp1",), ("w_out", "w_attn_up", "w_ssm_up", "w_glu"), ("w_in",))
RS_COLLECTIVE_ID0 = AG_COLLECTIVE_ID0 + len(AG_GROUPS)
SMALL_EARLY = ("ssm_a_re", "ssm_a_im", "ssm_log_dt", "ssm_b_re", "ssm_b_im", "ssm_c_re", "ssm_c_im", "ssm_d")
SMALL_LATE = tuple(n for n in SMALL if n not in SMALL_EARLY)
SMALL_COLLECTIVE_ID0 = RS_COLLECTIVE_ID0 + 2 * len(RS_GROUPS)
LANES = 128


def _pack(parts):
    rows = []
    for p in parts:
        flat = p.reshape(-1).astype(F32)
        pad = (-flat.shape[0]) % LANES
        rows.append(jnp.pad(flat, (0, pad)).reshape(-1, LANES))
    packed = jnp.concatenate(rows, 0)
    return jnp.pad(packed, ((0, (-packed.shape[0]) % 8), (0, 0)))


def _unpack(packed, shapes):
    out, r0 = [], 0
    for s in shapes:
        n = math.prod(s)
        nr = -(-n // LANES)
        out.append(packed[r0:r0 + nr].reshape(-1)[:n].reshape(s))
        r0 += nr
    return out


WEIGHTS = ("c_ctx", "w_ada", "b_ada", "w_in", "attn_sink", "ssm_a_re", "ssm_a_im", "ssm_log_dt", "ssm_b_re", "ssm_b_im",
           "ssm_c_re", "ssm_c_im", "ssm_d", "w_glu", "w_attn_up", "w_ssm_up", "w_out", "ln_mix_g", "ln_mix_b", "w_mlp1",
           "b_mlp1", "w_mlp2", "b_mlp2", "ln_mlp_g", "ln_mlp_b")
ADA_COLS = 6 * D // N_DEV


def _step(x, c, ctx, loss_target, p, m, v):
    me = _lin(_my_pos())
    x2, ctx2, tgt2 = x[0], ctx[0], loss_target[0]

    wb = {}
    for gi, group in enumerate(AG_GROUPS):
        full = _allgather_weights_seq([p[n][0].astype(BF16) for n in group], [BIG_KIND[BIG.index(n)] for n in group],
                                      "allgather_seq%d" % gi, AG_COLLECTIVE_ID0 + gi)
        wb.update(zip(group, full))

    c_all = _allgather_small(jnp.broadcast_to(c, (8, D)), "gather_c")[:, 0, :]
    cc = p["c_ctx"].reshape(1, D)
    s_in = jnp.concatenate([c_all, cc, jnp.zeros((7, D), F32)], 0)
    s_act, = _rowwise(lambda rv, vv: ([_silu(rv[0])], []), [(s_in, D, 0, 0)], [], [(D, F32)], [], nrows=16, tr=16, name="silu_c")
    b_mine = lax.dynamic_slice_in_dim(p["b_ada"], me * ADA_COLS, ADA_COLS, axis=1)
    mod_part = _matmul(s_act, p["w_ada"][0], mode="nn", name="ada_fwd", tm=16, tn=512, bias=b_mine)
    mod_all = _allgather_small(mod_part, "gather_mod")
    mod_lat = lax.dynamic_index_in_dim(mod_all, me, axis=1, keepdims=False).reshape(1, 6 * D)
    mod_ctx = mod_all[:, 8, :].reshape(1, 6 * D)

    sp = {n: p[n][0] for n in SMALL if n not in ("c_ctx", "b_ada")}
    recv, halves = {}, {}

    def on_grad(gw):
        for gi, group in enumerate(RS_GROUPS):
            if gi not in halves and all(n in gw for n in group):
                kinds = [BIG_KIND[BIG.index(n)] for n in group]
                halves[gi] = (dict(gw), _pair_exchange_seq([gw[n] for n in group], kinds, "pair_exchange%d" % gi, RS_COLLECTIVE_ID0 + 2 * gi))

    def on_finish(key, after):
        gi = [i for i, group in enumerate(RS_GROUPS) if key in group][0]
        group = RS_GROUPS[gi]
        grads, half = halves[gi]
        prev = tuple(recv[n] for n in RS_GROUPS[gi - 1][:1]) if gi else ()
        if gi == len(RS_GROUPS) - 1:
            prev += (small["early"],)
        psums =[_pair_add(grads[n], h, BIG_KIND[BIG.index(n)], "pair_add_" + n, after=(after,) + prev) for n, h in zip(group, half)]
        recv.update(zip(group, _chip_exchange_seq(psums, "chip_exchange%d" % gi, RS_COLLECTIVE_ID0 + 2 * gi + 1)))
        return psums[-1]

    small = {}

    def on_early(gs_early):
        small["early"] = _allgather_small_seq(_pack([gs_early[n] for n in SMALL_EARLY]), "gather_small_early", SMALL_COLLECTIVE_ID0)

    total = {}

    def on_loss(loss_p):
        total["loss"] = lax.psum(loss_p[0, 0], ("x", "y", "c"))
        return total["loss"].reshape(1, 1)

    loss_p, grad_x, d_mod_lat, d_mod_ctx, gw, gs = _local_step(x2, ctx2, tgt2, mod_lat, mod_ctx, wb, sp, on_grad, on_loss, on_finish, on_early)

    g_early = small["early"]
    res = {}
    last = ()

    def adam_small(names, g_pack, tag, after):
        sm = _adam(g_pack, _pack([p[n] for n in names]), _pack([m[n] for n in names]), _pack([v[n] for n in names]),
                   tr=g_pack.shape[1], name="adam_small_" + tag, after=after)
        shapes = [p[n].shape for n in names]
        for j, outs in enumerate(zip(*[_unpack(a, shapes) for a in sm])):
            res[names[j]] = outs
        return (sm[0],)

    for gi, group in enumerate(RS_GROUPS):
        if gi == len(RS_GROUPS) - 1:
            last = adam_small(SMALL_EARLY, g_early, "early", last)
        for n in group:
            res[n] = _adam(recv[n], p[n][0], m[n][0], v[n][0], tr=256, name="adam_" + n, after=last)
            last = (res[n][0],)

    dm = jnp.concatenate([d_mod_lat, d_mod_ctx, jnp.zeros((6, 6 * D), F32)], 0)
    dm_all = _allgather_small_seq(dm, "gather_dmod", SMALL_COLLECTIVE_ID0 + 1)
    dm_all = lax.optimization_barrier((dm_all,) + last)[0]
    dm2 = jnp.concatenate([dm_all[:, 0, :], dm_all[:, 1, :]], 0)
    dm2_mine = lax.dynamic_slice_in_dim(dm2, me * ADA_COLS, ADA_COLS, axis=1)
    s2 = jnp.concatenate([s_act[0:8], jnp.broadcast_to(s_act[8:9], (8, D))], 0)
    g_w_ada = _matmul(s2, dm2_mine, mode="tn", name="dw_ada", tm=512, tn=ADA_COLS, after=last)
    dsc_part = _matmul(dm2_mine[8:16], p["w_ada"][0], mode="nt", name="d_silu_cctx", tm=8, tn=512, after=last)

    def cctx_b(rv, vv):
        _, pull = jax.vjp(_silu, vv[0])
        return [], [pull(jnp.sum(rv[0], axis=0, keepdims=True))[0]]

    g_cctx, = _rowwise(cctx_b, [(dsc_part, D, 0, 0)], [cc], [], [(1, D)], nrows=8, tr=8, name="cctx_bwd")
    gs["c_ctx"] = g_cctx
    gs["b_ada"] = d_mod_lat + d_mod_ctx

    res["w_ada"] = _adam(g_w_ada[None], p["w_ada"][0], m["w_ada"][0], v["w_ada"][0], tr=256, name="adam_w_ada")

    g_late = _allgather_small_seq(_pack([gs[n] for n in SMALL_LATE]), "gather_small_late", SMALL_COLLECTIVE_ID0 + 2)
    adam_small(SMALL_LATE, g_late, "late", (res["w_ada"][0],))

    outs = [total["loss"], grad_x[None]]
    for j in range(4):
        outs += [res[n][j].reshape(p[n].shape) for n in WEIGHTS]
    return tuple(outs)


def kernel(x, c, ctx, c_ctx, w_ada, b_ada, w_in, attn_sink, ssm_a_re, ssm_a_im, ssm_log_dt, ssm_b_re, ssm_b_im, ssm_c_re, ssm_c_im, ssm_d, w_glu, w_attn_up, w_ssm_up, w_out, ln_mix_g, ln_mix_b, w_mlp1, b_mlp1, w_mlp2, b_mlp2, ln_mlp_g, ln_mlp_b, loss_target, m_c_ctx, m_w_ada, m_b_ada, m_w_in, m_attn_sink, m_ssm_a_re, m_ssm_a_im, m_ssm_log_dt, m_ssm_b_re, m_ssm_b_im, m_ssm_c_re, m_ssm_c_im, m_ssm_d, m_w_glu, m_w_attn_up, m_w_ssm_up, m_w_out, m_ln_mix_g, m_ln_mix_b, m_w_mlp1, m_b_mlp1, m_w_mlp2, m_b_mlp2, m_ln_mlp_g, m_ln_mlp_b, v_c_ctx, v_w_ada, v_b_ada, v_w_in, v_attn_sink, v_ssm_a_re, v_ssm_a_im, v_ssm_log_dt, v_ssm_b_re, v_ssm_b_im, v_ssm_c_re, v_ssm_c_im, v_ssm_d, v_w_glu, v_w_attn_up, v_w_ssm_up, v_w_out, v_ln_mix_g, v_ln_mix_b, v_w_mlp1, v_b_mlp1, v_w_mlp2, v_b_mlp2, v_ln_mlp_g, v_ln_mlp_b):
    given = dict(locals())
    p = {n: given[n] for n in WEIGHTS}
    m = {n: given["m_" + n] for n in WEIGHTS}
    v = {n: given["v_" + n] for n in WEIGHTS}
    return _step(x, c, ctx, loss_target, p, m, v)
```

```python
import functools
import math

import jax
import jax.numpy as jnp
from jax import lax
from jax.experimental import pallas as pl
from jax.experimental.pallas import tpu as pltpu
from jax.experimental.pallas import tpu_sc as plsc

F32 = jnp.float32
BF16 = jnp.bfloat16

N_DEV = 8
D = 2048
T = 2048
C = 256
TA = T + C
GRID_W = 64
HD = 128
NH = 8
NKV = 2
GROUP = NH // NKV
WINDOW = 128
QW = NH * HD
KVW = NKV * HD
SW = D // 4
SG = 16
NG = SW // SG
SP = 64
DFF = 4 * D
IN_COLS = QW + 2 * KVW + SW + 2 * D
ALPHA = 2.0 ** 0.25
LN_EPS = 1e-6
NEG_INF = -1e30
ROPE_BASE = 10000.0
ATT_SCALE = HD ** -0.5

NSEG = 8
GBLK = 8
NBLK = NG // GBLK
BW = GBLK * SP
UW = GBLK * SG

ADAM_LR = 0.001
ADAM_B1 = 0.9
ADAM_B2 = 0.999
ADAM_EPS = 1e-08
ADAM_WD = 0.01
ADAM_STEP = 10

VMEM_LIMIT_BYTES = 56 * 1024 * 1024
MESH = pl.DeviceIdType.MESH


def _cparams(sem=None):
    return pltpu.CompilerParams(dimension_semantics=sem, vmem_limit_bytes=VMEM_LIMIT_BYTES)


def _matmul(a, b, *, mode, name, out_dtypes=(F32,), tm=512, tn=512, tk=None, bias=None, extras=(), epilogue=None, after=(),
            out_t=None):
    if mode == "nn":
        (M, K), (K2, N) = a.shape, b.shape
    elif mode == "nt":
        (M, K), (N, K2) = a.shape, b.shape
    else:
        (K, M), (K2, N) = a.shape, b.shape
    assert K == K2, (name, a.shape, b.shape)
    tm, tn, tk = min(tm, M), min(tn, N), min(tk or K, K)
    assert M % tm == 0 and N % tn == 0 and K % tk == 0, (name, M, N, K, tm, tn, tk)
    nk = K // tk
    if mode == "tn":
        a_spec = pl.BlockSpec((tk, tm), lambda i, j, k: (k, i))
    else:
        a_spec = pl.BlockSpec((tm, tk), lambda i, j, k: (i, k))
    if mode == "nt":
        b_spec = pl.BlockSpec((tn, tk), lambda i, j, k: (j, k))
    else:
        b_spec = pl.BlockSpec((tk, tn), lambda i, j, k: (k, j))
    dims = {"nn": (((1,), (0,)), ((), ())), "nt": (((1,), (1,)), ((), ())), "tn": (((0,), (0,)), ((), ()))}[mode]
    in_specs = [a_spec, b_spec]
    operands = [a, b]
    if bias is not None:
        in_specs.append(pl.BlockSpec((1, tn), lambda i, j, k: (0, j)))
        operands.append(bias)
    for e in extras:
        in_specs.append(pl.BlockSpec((tm, tn), lambda i, j, k: (i, j)))
        operands.append(e)
    n_ex = len(extras)
    for t in after:
        in_specs.append(pl.BlockSpec(memory_space=pl.ANY))
        operands.append(t)
    n_after = len(after)
    n_out = len(out_dtypes)
    out_t = tuple(out_t) if out_t is not None else (False,) * n_out
    has_bias = bias is not None

    def kern(*refs):
        a_ref, b_ref = refs[0], refs[1]
        pos = 2
        bias_ref = None
        if has_bias:
            bias_ref = refs[pos]
            pos += 1
        ex_refs = refs[pos:pos + n_ex]
        pos += n_ex + n_after
        out_refs = refs[pos:pos + n_out]
        acc_ref = refs[pos + n_out] if nk > 1 else None

        def finish(r):
            if has_bias:
                r = r + bias_ref[...]
            outs = epilogue(r, *[e[...] for e in ex_refs]) if epilogue is not None else (r,)
            for o_ref, o, tr_ in zip(out_refs, outs, out_t):
                o_ref[...] = (o.T if tr_ else o).astype(o_ref.dtype)

        part = lax.dot_general(a_ref[...].astype(BF16), b_ref[...].astype(BF16), dims, preferred_element_type=F32)
        if nk == 1:
            finish(part)
        else:
            k = pl.program_id(2)

            @pl.when(k == 0)
            def _():
                acc_ref[...] = part

            @pl.when(k > 0)
            def _():
                acc_ref[...] += part

            @pl.when(k == nk - 1)
            def _():
                finish(acc_ref[...])

    outs = pl.pallas_call(
        kern,
        name=name,
        grid=(M // tm, N // tn, nk),
        in_specs=in_specs,
        out_specs=[pl.BlockSpec((tn, tm), lambda i, j, k: (j, i)) if tr_ else pl.BlockSpec((tm, tn), lambda i, j, k: (i, j))
                   for tr_ in out_t],
        out_shape=[jax.ShapeDtypeStruct((N, M) if tr_ else (M, N), dt) for dt, tr_ in zip(out_dtypes, out_t)],
        scratch_shapes=[pltpu.VMEM((tm, tn), F32)] if nk > 1 else [],
        compiler_params=_cparams(("parallel", "parallel", "arbitrary")),
    )(*operands)
    return outs[0] if n_out == 1 else tuple(outs)


def _rowwise(fn, rows, vecs, outs, vec_outs, *, nrows, tr, name, after=()):
    n_rows, n_vecs, n_outs, n_after = len(rows), len(vecs), len(outs), len(after)
    in_specs = [pl.BlockSpec((tr, w), lambda i, cb=cb, ro=ro: (i + ro, cb)) for (_, w, cb, ro) in rows]
    in_specs += [pl.BlockSpec(v.shape, lambda i: (0, 0)) for v in vecs]
    in_specs += [pl.BlockSpec(memory_space=pl.ANY)] * n_after
    outs = [o if len(o) == 3 else (*o, False) for o in outs]
    out_specs = [pl.BlockSpec((w, tr), lambda i: (0, i)) if tr_ else pl.BlockSpec((tr, w), lambda i: (i, 0)) for (w, _, tr_) in outs]
    out_specs += [pl.BlockSpec(s, lambda i: (0, 0)) for s in vec_outs]
    out_shape = [jax.ShapeDtypeStruct((w, nrows) if tr_ else (nrows, w), dt) for (w, dt, tr_) in outs]
    out_tr = [tr_ for (_, _, tr_) in outs]
    out_shape += [jax.ShapeDtypeStruct(s, F32) for s in vec_outs]

    def kern(*refs):
        rvals = [r[...] for r in refs[:n_rows]]
        vvals = [r[...] for r in refs[n_rows:n_rows + n_vecs]]
        first_out = n_rows + n_vecs + n_after
        o_refs = refs[first_out:first_out + n_outs]
        v_refs = refs[first_out + n_outs:]
        ro, vo = fn(rvals, vvals)
        for r, val, tr_ in zip(o_refs, ro, out_tr):
            r[...] = (val.astype(F32).T if tr_ else val).astype(r.dtype)
        i = pl.program_id(0)
        for r, val in zip(v_refs, vo):
            @pl.when(i == 0)
            def _(r=r, val=val):
                r[...] = val.astype(F32)

            @pl.when(i > 0)
            def _(r=r, val=val):
                r[...] += val.astype(F32)

    res = pl.pallas_call(
        kern,
        name=name,
        grid=(nrows // tr,),
        in_specs=in_specs,
        out_specs=out_specs,
        out_shape=out_shape,
        compiler_params=_cparams(("arbitrary",)),
    )(*[r[0] for r in rows], *vecs, *after)
    return list(res)


def _ln(x):
    mu = jnp.mean(x, axis=-1, keepdims=True)
    xc = x - mu
    var = jnp.mean(xc * xc, axis=-1, keepdims=True)
    return xc * lax.rsqrt(var + LN_EPS)


def _sigmoid(x):
    return 1.0 / (1.0 + jnp.exp(-x))


def _gelu(x):
    return 0.5 * x * (1.0 + jnp.tanh(math.sqrt(2.0 / math.pi) * (x + 0.044715 * (x * x * x))))


def _silu(x):
    return x * _sigmoid(x)


def _f_ln_mod(x, sc, sh):
    return _ln(x) * (1.0 + sc) + sh


def _f_glu(z):
    return z[:, :SW] * _sigmoid(z[:, SW:])


def _f_mix(ga, gs, attn_d, ssm_d):
    return _sigmoid(ga) * attn_d + _sigmoid(gs) * ssm_d


def _f_post1(x, y, g1, lg, lb, sc2, sh2):
    r1 = ALPHA * x + g1 * y
    x1 = _ln(r1) * lg + lb
    h2 = _ln(x1) * (1.0 + sc2) + sh2
    return x1, h2


def _f_loss(x1, mlp, tgt, g2, lg, lb, b2z):
    r2 = ALPHA * x1 + g2 * (mlp + b2z)
    out = _ln(r2) * lg + lb
    err = out - tgt
    return 0.5 * jnp.sum(err * err) * (1.0 / D)


def _rope_tables():
    rows = T // GRID_W
    row = jnp.repeat(jnp.arange(rows), GRID_W)
    col = jnp.tile(jnp.arange(GRID_W), rows)
    n_freq = HD // 4
    freqs = ROPE_BASE ** (-jnp.arange(n_freq, dtype=F32) / n_freq)
    ang_r = row.astype(F32)[:, None] * freqs
    ang_c = col.astype(F32)[:, None] * freqs
    ang = jnp.concatenate([ang_r, ang_r, ang_c, ang_c], -1)
    cos, sin = jnp.cos(ang), jnp.sin(ang)
    lo = (jnp.arange(HD) % (HD // 2)) < (HD // 4)
    sin_a = jnp.where(lo[None, :], -sin, 0.0)
    sin_b = jnp.where(lo[None, :], 0.0, sin)
    return cos, sin_a, sin_b


def _rope(x, cos, sa, sb):
    return x * cos + pltpu.roll(x, 96, 1) * sa + pltpu.roll(x, 32, 1) * sb


def _rope_t(dy, cos, sa, sb):
    return dy * cos + pltpu.roll(dy * sa, 32, 1) + pltpu.roll(dy * sb, 96, 1)


BAND = 3 * WINDOW
KPAD = T + 2 * WINDOW


def _attn_fill_kv(k_ref, v_ref, cos_ref, sa_ref, sb_ref, kp, vp, kc, vc):
    zeros = jnp.zeros((WINDOW, KVW), BF16)
    kp[0:WINDOW, :] = zeros
    kp[WINDOW + T:KPAD, :] = zeros
    vp[0:WINDOW, :] = zeros
    vp[WINDOW + T:KPAD, :] = zeros
    for hh in range(NKV):
        cs = slice(hh * HD, (hh + 1) * HD)
        for r0 in range(0, T, 512):
            rs = slice(r0, r0 + 512)
            kr = _rope(k_ref[rs, cs], cos_ref[rs, :], sa_ref[rs, :], sb_ref[rs, :])
            kp[WINDOW + r0:WINDOW + r0 + 512, cs] = kr.astype(BF16)
    vp[WINDOW:WINDOW + T, :] = v_ref[0:T, :].astype(BF16)
    kc[...] = k_ref[T:TA, :].astype(BF16)
    vc[...] = v_ref[T:TA, :].astype(BF16)


GROWS = GROUP * WINDOW


def _attn_scores(n, kvh, q_ref, cos_ref, sa_ref, sb_ref, sink_ref, kp, kc):
    r0 = pl.multiple_of(n * WINDOW, WINDOW)
    cos = cos_ref[pl.ds(r0, WINDOW), :]
    sa = sa_ref[pl.ds(r0, WINDOW), :]
    sb = sb_ref[pl.ds(r0, WINDOW), :]
    heads = range(kvh * GROUP, (kvh + 1) * GROUP)
    q_g = jnp.concatenate([_rope(q_ref[:, h * HD:(h + 1) * HD], cos, sa, sb).astype(BF16) for h in heads], axis=0)
    kb = kp[pl.ds(r0, BAND), kvh * HD:(kvh + 1) * HD]
    kcb = kc[:, kvh * HD:(kvh + 1) * HD]
    nt = (((1,), (1,)), ((), ()))
    s_loc = lax.dot_general(q_g, kb, nt, preferred_element_type=F32) * ATT_SCALE
    s_ctx = lax.dot_general(q_g, kcb, nt, preferred_element_type=F32) * ATT_SCALE
    row = lax.broadcasted_iota(jnp.int32, (GROWS, BAND), 0) & (WINDOW - 1)
    col = lax.broadcasted_iota(jnp.int32, (GROWS, BAND), 1)
    rel = col - WINDOW - row
    kpos = r0 - WINDOW + col
    valid = (jnp.abs(rel) <= WINDOW) & (kpos >= 0) & (kpos < T)
    s_loc = jnp.where(valid, s_loc, NEG_INF)
    sk = jnp.concatenate([jnp.broadcast_to(sink_ref[0:1, h:h + 1], (WINDOW, 1)) for h in heads], axis=0)
    m = jnp.maximum(jnp.maximum(jnp.max(s_loc, -1, keepdims=True), jnp.max(s_ctx, -1, keepdims=True)), sk)
    e_loc = jnp.exp(s_loc - m)
    e_ctx = jnp.exp(s_ctx - m)
    e_sink = jnp.exp(sk - m)
    inv = 1.0 / (jnp.sum(e_loc, -1, keepdims=True) + jnp.sum(e_ctx, -1, keepdims=True) + e_sink)
    return q_g, r0, e_loc * inv, e_ctx * inv, e_sink * inv


def _attn_fwd(proj, sink, tabs):
    cos, sa, sb = tabs

    def kern(q_ref, k_ref, v_ref, cos_ref, sa_ref, sb_ref, sink_ref, o_ref, kp, vp, kc, vc):
        n = pl.program_id(0)

        @pl.when(n == 0)
        def _():
            _attn_fill_kv(k_ref, v_ref, cos_ref, sa_ref, sb_ref, kp, vp, kc, vc)

        for kvh in range(NKV):
            _, r0, p_loc, p_ctx, _ = _attn_scores(n, kvh, q_ref, cos_ref, sa_ref, sb_ref, sink_ref, kp, kc)
            vb = vp[pl.ds(r0, BAND), kvh * HD:(kvh + 1) * HD]
            vcb = vc[:, kvh * HD:(kvh + 1) * HD]
            o = jnp.dot(p_loc.astype(BF16), vb, preferred_element_type=F32)
            o = o + jnp.dot(p_ctx.astype(BF16), vcb, preferred_element_type=F32)
            for g in range(GROUP):
                h = kvh * GROUP + g
                o_ref[:, h * HD:(h + 1) * HD] = o[g * WINDOW:(g + 1) * WINDOW, :].astype(o_ref.dtype)

    full = lambda shape: pl.BlockSpec(shape, lambda n: (0, 0))
    return pl.pallas_call(
        kern,
        name="attn_fwd",
        grid=(T // WINDOW,),
        in_specs=[
            pl.BlockSpec((WINDOW, QW), lambda n: (n, 0)),
            pl.BlockSpec((TA, KVW), lambda n: (0, QW // KVW)),
            pl.BlockSpec((TA, KVW), lambda n: (0, QW // KVW + 1)),
            full((T, HD)), full((T, HD)), full((T, HD)), full((1, NH)),
        ],
        out_specs=pl.BlockSpec((WINDOW, QW), lambda n: (n, 0)),
        out_shape=jax.ShapeDtypeStruct((T, QW), BF16),
        scratch_shapes=[pltpu.VMEM((KPAD, KVW), BF16), pltpu.VMEM((KPAD, KVW), BF16),
                        pltpu.VMEM((C, KVW), BF16), pltpu.VMEM((C, KVW), BF16)],
        compiler_params=_cparams(("arbitrary",)),
    )(proj, proj, proj, cos, sa, sb, sink)


def _attn_bwd(proj, d_attn, sink, tabs):
    cos, sa, sb = tabs
    n_blocks = T // WINDOW

    def kern(q_ref, k_ref, v_ref, do_ref, cos_ref, sa_ref, sb_ref, sink_ref,
             dq_ref, dk_ref, dv_ref, dsink_ref, kp, vp, kc, vc, dkp, dvp, dkc, dvc):
        n = pl.program_id(0)

        @pl.when(n == 0)
        def _():
            _attn_fill_kv(k_ref, v_ref, cos_ref, sa_ref, sb_ref, kp, vp, kc, vc)
            dkp[...] = jnp.zeros_like(dkp)
            dvp[...] = jnp.zeros_like(dvp)
            dkc[...] = jnp.zeros_like(dkc)
            dvc[...] = jnp.zeros_like(dvc)
            dsink_ref[...] = jnp.zeros_like(dsink_ref)

        nt = (((1,), (1,)), ((), ()))
        tn = (((0,), (0,)), ((), ()))
        for kvh in range(NKV):
            cs = slice(kvh * HD, (kvh + 1) * HD)
            heads = range(kvh * GROUP, (kvh + 1) * GROUP)
            q_g, r0, p_loc, p_ctx, p_sink = _attn_scores(n, kvh, q_ref, cos_ref, sa_ref, sb_ref, sink_ref, kp, kc)
            kb = kp[pl.ds(r0, BAND), cs]
            vb = vp[pl.ds(r0, BAND), cs]
            kcb = kc[:, cs]
            vcb = vc[:, cs]
            do_g = jnp.concatenate([do_ref[:, h * HD:(h + 1) * HD] for h in heads], axis=0)
            dp_loc = lax.dot_general(do_g, vb, nt, preferred_element_type=F32)
            dp_ctx = lax.dot_general(do_g, vcb, nt, preferred_element_type=F32)
            delta = jnp.sum(p_loc * dp_loc, -1, keepdims=True) + jnp.sum(p_ctx * dp_ctx, -1, keepdims=True)
            ds_loc = (p_loc * (dp_loc - delta) * ATT_SCALE).astype(BF16)
            ds_ctx = (p_ctx * (dp_ctx - delta) * ATT_SCALE).astype(BF16)
            dq = jnp.dot(ds_loc, kb, preferred_element_type=F32) + jnp.dot(ds_ctx, kcb, preferred_element_type=F32)
            cos = cos_ref[pl.ds(r0, WINDOW), :]
            sa_ = sa_ref[pl.ds(r0, WINDOW), :]
            sb_ = sb_ref[pl.ds(r0, WINDOW), :]
            dkp[pl.ds(r0, BAND), cs] += lax.dot_general(ds_loc, q_g, tn, preferred_element_type=F32)
            dkc[:, cs] += lax.dot_general(ds_ctx, q_g, tn, preferred_element_type=F32)
            dvp[pl.ds(r0, BAND), cs] += lax.dot_general(p_loc.astype(BF16), do_g, tn, preferred_element_type=F32)
            dvc[:, cs] += lax.dot_general(p_ctx.astype(BF16), do_g, tn, preferred_element_type=F32)
            dsk_rows = p_sink * delta
            for g, h in enumerate(heads):
                rs = slice(g * WINDOW, (g + 1) * WINDOW)
                dq_ref[:, h * HD:(h + 1) * HD] = _rope_t(dq[rs, :], cos, sa_, sb_).astype(dq_ref.dtype)
                dsk = -jnp.sum(dsk_rows[rs, :], axis=0, keepdims=True)
                dsink_ref[h:h + 1, :] += jnp.broadcast_to(dsk, (1, HD))

        @pl.when(n == n_blocks - 1)
        def _():
            for hh in range(NKV):
                cs = slice(hh * HD, (hh + 1) * HD)
                for r0 in range(0, T, 512):
                    rs = slice(r0, r0 + 512)
                    g = dkp[WINDOW + r0:WINDOW + r0 + 512, cs]
                    dk_ref[rs, cs] = _rope_t(g, cos_ref[rs, :], sa_ref[rs, :], sb_ref[rs, :]).astype(dk_ref.dtype)
            dk_ref[T:TA, :] = dkc[...].astype(dk_ref.dtype)
            dv_ref[0:T, :] = dvp[WINDOW:WINDOW + T, :].astype(dv_ref.dtype)
            dv_ref[T:TA, :] = dvc[...].astype(dv_ref.dtype)

    full = lambda shape: pl.BlockSpec(shape, lambda n: (0, 0))
    return pl.pallas_call(
        kern,
        name="attn_bwd",
        grid=(n_blocks,),
        in_specs=[
            pl.BlockSpec((WINDOW, QW), lambda n: (n, 0)),
            pl.BlockSpec((TA, KVW), lambda n: (0, QW // KVW)),
            pl.BlockSpec((TA, KVW), lambda n: (0, QW // KVW + 1)),
            pl.BlockSpec((WINDOW, QW), lambda n: (n, 0)),
            full((T, HD)), full((T, HD)), full((T, HD)), full((1, NH)),
        ],
        out_specs=[pl.BlockSpec((WINDOW, QW), lambda n: (n, 0)), full((TA, KVW)), full((TA, KVW)), full((NH, HD))],
        out_shape=[jax.ShapeDtypeStruct((T, QW), BF16), jax.ShapeDtypeStruct((TA, KVW), BF16),
                   jax.ShapeDtypeStruct((TA, KVW), BF16), jax.ShapeDtypeStruct((NH, HD), F32)],
        scratch_shapes=[pltpu.VMEM((KPAD, KVW), BF16), pltpu.VMEM((KPAD, KVW), BF16),
                        pltpu.VMEM((C, KVW), BF16), pltpu.VMEM((C, KVW), BF16),
                        pltpu.VMEM((KPAD, KVW), F32), pltpu.VMEM((KPAD, KVW), F32),
                        pltpu.VMEM((C, KVW), F32), pltpu.VMEM((C, KVW), F32)],
        compiler_params=_cparams(("arbitrary",)),
    )(proj, proj, proj, d_attn, cos, sa, sb, sink)


def _s5_prep(a_re, a_im, log_dt, b_re, b_im, c_re, c_im):
    lam = lax.complex(a_re, a_im)
    dt = jnp.exp(log_dt)[..., None]
    lam_bar = jnp.exp(lam * dt)
    b_bar = ((lam_bar - 1.0) / lam)[..., None] * lax.complex(b_re, b_im)
    def lam_rows(v):
        return v.reshape(2, NBLK, 1, BW)

    lam_l = jnp.concatenate([lam_rows(jnp.real(lam_bar)), lam_rows(jnp.imag(lam_bar))], -1)
    lam_l = jnp.broadcast_to(lam_l, (2, NBLK, 8, 2 * BW))
    diag = (jnp.arange(UW)[:, None] // SG) == (jnp.arange(BW)[None, :] // SP)

    def blocks(v):
        return jnp.where(diag, jnp.tile(v.reshape(2, NBLK, UW, SP), (1, 1, 1, GBLK)), 0.0)

    b_t = jnp.swapaxes(b_bar, -1, -2)
    bmat = jnp.concatenate([blocks(jnp.real(b_t)), blocks(jnp.imag(b_t))], -1)
    cmat = jnp.concatenate([blocks(c_re), -blocks(c_im)], -1)
    return lam_l, bmat, cmat


def _cmul(ar, ai, br, bi):
    return ar * br - ai * bi, ar * bi + ai * br


def _shift_rows(x, rev, fill):
    r = lax.broadcasted_iota(jnp.int32, x.shape, 0)
    down = jnp.where(r == 0, fill, pltpu.roll(x, 1, 0))
    up = jnp.where(r == NSEG - 1, fill, pltpu.roll(x, NSEG - 1, 0))
    return jnp.where(rev == 0, down, up)


def _edge_row(x, rev):
    last = jnp.broadcast_to(x[NSEG - 1:NSEG, :], x.shape)
    first = jnp.broadcast_to(x[0:1, :], x.shape)
    return jnp.where(rev == 0, last, first)


def _seg_scan(get, put, base, seglen, lr, li, rev, cin, acc_fn=None, acc0=()):
    zero = jnp.zeros((NSEG, BW), F32)

    def rows(k):
        j = jnp.where(rev == 0, k, seglen - 1 - k)
        return pl.ds(pl.multiple_of(base + j * NSEG, NSEG), NSEG)

    def local(k, carry):
        sr, si = carry
        xr, xi = get(rows(k))
        tr, ti = _cmul(lr, li, sr, si)
        sr, si = tr + xr, ti + xi
        put(rows(k), sr, si)
        return sr, si

    er, ei = lax.fori_loop(0, seglen, local, (zero, zero))
    lpr, lpi = lr, li
    assert seglen & (seglen - 1) == 0, seglen
    for _ in range(seglen.bit_length() - 1):
        lpr, lpi = _cmul(lpr, lpi, lpr, lpi)
    cr, ci = _shift_rows(zero, rev, cin[0]), _shift_rows(zero, rev, cin[1])
    for _ in range(NSEG - 1):
        tr, ti = _cmul(lpr, lpi, cr, ci)
        cr, ci = _shift_rows(er + tr, rev, cin[0]), _shift_rows(ei + ti, rev, cin[1])

    def fix(k, carry):
        tr, ti = _cmul(lr, li, carry[0], carry[1])
        xr, xi = get(rows(k))
        fr, fi = xr + tr, xi + ti
        put(rows(k), fr, fi)
        if acc_fn is None:
            return tr, ti
        j = jnp.where(rev == 0, k, seglen - 1 - k)
        return (tr, ti) + tuple(acc_fn(j, fr, fi, carry[2:]))

    out = lax.fori_loop(0, seglen, fix, (cr, ci) + tuple(acc0))
    tr, ti = out[0], out[1]
    leaving = (_edge_row(er + tr, rev), _edge_row(ei + ti, rev))
    return leaving if acc_fn is None else (leaving, out[2:])


RCH = 256
CSEG = C // NSEG
TSEG = T // NSEG
UCOL0 = (QW + 2 * KVW) // UW


REGIONS = ((0, TSEG), (T, CSEG))


def _state_access(ref, lead=()):
    def get(rows):
        return ref[(*lead, rows, slice(0, BW))], ref[(*lead, rows, slice(BW, 2 * BW))]

    def put(rows, re, im):
        ref[(*lead, rows, slice(0, BW))] = re
        ref[(*lead, rows, slice(BW, 2 * BW))] = im

    return get, put


def _interleave_rows(src_ref, dst_ref, regions=REGIONS):
    for base, seglen in regions:
        def body(j, carry, base=base, seglen=seglen):
            dst_ref[pl.ds(pl.multiple_of(base + j * NSEG, NSEG), NSEG), :] = src_ref[pl.ds(base + j, NSEG, stride=seglen), :]
            return carry

        lax.fori_loop(0, seglen, body, 0, unroll=8)


def _deinterleave_rows(src_ref, dst_ref, regions=REGIONS):
    for base, seglen in regions:
        def body(j, carry, base=base, seglen=seglen):
            dst_ref[pl.ds(base + j, NSEG, stride=seglen), :] = src_ref[pl.ds(pl.multiple_of(base + j * NSEG, NSEG), NSEG), :]
            return carry

        lax.fori_loop(0, seglen, body, 0, unroll=8)


def _s5_fwd(proj, dskip, lam, bmat, cmat):
    def kern(u_ref, dk_ref, lam_ref, b_ref, c_ref, s_ref, ssm_ref, ge_ref, up_ref, yp_ref):
        d = pl.program_id(1)

        @pl.when(d == 0)
        def _():
            _interleave_rows(u_ref, up_ref)

        bm = b_ref[0, 0].astype(BF16)
        for r0 in range(0, TA, RCH):
            s_ref[0, 0, r0:r0 + RCH, :] = jnp.dot(up_ref[r0:r0 + RCH, :].astype(BF16), bm, preferred_element_type=F32)
        lr = lam_ref[0, 0, :, 0:BW]
        li = lam_ref[0, 0, :, BW:2 * BW]
        zero = jnp.zeros((NSEG, BW), F32)
        get, put = _state_access(s_ref, (0, 0))
        mid = _seg_scan(get, put, T, CSEG, lr, li, d, (zero, zero))
        _seg_scan(get, put, 0, TSEG, lr, li, d, mid)
        cm = c_ref[0, 0].astype(BF16)
        for r0 in range(0, T, RCH):
            y = lax.dot_general(s_ref[0, 0, r0:r0 + RCH, :].astype(BF16), cm, (((1,), (1,)), ((), ())), preferred_element_type=F32)

            @pl.when(d == 0)
            def _(y=y, r0=r0):
                yp_ref[r0:r0 + RCH, :] = y + dk_ref[...] * up_ref[r0:r0 + RCH, :]

            @pl.when(d == 1)
            def _(y=y, r0=r0):
                yp_ref[r0:r0 + RCH, :] += y

        @pl.when(d == 1)
        def _():
            _deinterleave_rows(yp_ref, ssm_ref, REGIONS[:1])
            for r0 in range(0, T, RCH):
                ge_ref[r0:r0 + RCH, :] = _gelu(ssm_ref[r0:r0 + RCH, :]).astype(ge_ref.dtype)

    blk4 = lambda shape: pl.BlockSpec((1, 1) + shape, lambda b, d: (d, b, 0, 0))
    return pl.pallas_call(
        kern,
        name="s5_fwd",
        grid=(NBLK, 2),
        in_specs=[pl.BlockSpec((TA, UW), lambda b, d: (0, UCOL0 + b)), pl.BlockSpec((1, UW), lambda b, d: (0, b)),
                  blk4((8, 2 * BW)), blk4((UW, 2 * BW)), blk4((UW, 2 * BW))],
        out_specs=[blk4((TA, 2 * BW)), pl.BlockSpec((T, UW), lambda b, d: (0, b)), pl.BlockSpec((T, UW), lambda b, d: (0, b))],
        out_shape=[jax.ShapeDtypeStruct((2, NBLK, TA, 2 * BW), F32), jax.ShapeDtypeStruct((T, SW), F32),
                   jax.ShapeDtypeStruct((T, SW), BF16)],
        scratch_shapes=[pltpu.VMEM((TA, UW), F32), pltpu.VMEM((T, UW), F32)],
        compiler_params=_cparams(("parallel", "arbitrary")),
    )(proj, dskip, lam, bmat, cmat)


def _s5_bwd(d_ge, ssm, proj, dskip, states, lam, bmat, cmat):
    nt = (((1,), (1,)), ((), ()))
    tn = (((0,), (0,)), ((), ()))

    def kern(dge_ref, ssm_ref, u_ref, dk_ref, s_ref, lam_ref, b_ref, c_ref,
             du_ref, ddk_ref, dlam_ref, db_ref, dc_ref, g_ref, dua_ref, dssm_ref, up_ref, nat_ref):
        d = pl.program_id(1)

        @pl.when(d == 0)
        def _():
            ddk = jnp.zeros((1, UW), F32)
            for r0 in range(0, T, RCH):
                rs = slice(r0, r0 + RCH)
                _, pull = jax.vjp(_gelu, ssm_ref[rs, :])
                dssm = pull(dge_ref[rs, :])[0]
                nat_ref[rs, :] = dssm
                ddk = ddk + jnp.sum(dssm * u_ref[rs, :], axis=0, keepdims=True)
            ddk_ref[...] = ddk
            _interleave_rows(nat_ref, dssm_ref, REGIONS[:1])
            _interleave_rows(u_ref, up_ref)
            for r0 in range(0, T, RCH):
                dua_ref[r0:r0 + RCH, :] = dssm_ref[r0:r0 + RCH, :] * dk_ref[...]
            dua_ref[T:TA, :] = jnp.zeros((C, UW), F32)

        cm = c_ref[0, 0].astype(BF16)
        for r0 in range(0, T, RCH):
            g_ref[r0:r0 + RCH, :] = jnp.dot(dssm_ref[r0:r0 + RCH, :].astype(BF16), cm, preferred_element_type=F32)
        g_ref[T:TA, :] = jnp.zeros((C, 2 * BW), F32)
        lr = lam_ref[0, 0, :, 0:BW]
        li = lam_ref[0, 0, :, BW:2 * BW]
        zero = jnp.zeros((NSEG, BW), F32)
        get_g, put_g = _state_access(g_ref)

        get_s, _ = _state_access(s_ref, (0, 0))

        def dlam_fold(base, seglen, s_in):
            def rows(j):
                return pl.ds(pl.multiple_of(base + j * NSEG, NSEG), NSEG)

            jb = jnp.where(d == 0, 0, seglen - 1)
            jn = jnp.where(d == 0, seglen - 1, 0)
            sp = get_s(rows(jn))
            edge = (_shift_rows(sp[0], d, s_in[0]), _shift_rows(sp[1], d, s_in[1]))

            def fold(j, gr, gi, acc):
                jp = jnp.clip(jnp.where(d == 0, j - 1, j + 1), 0, seglen - 1)
                sr, si = get_s(rows(jp))
                sr = jnp.where(j == jb, edge[0], sr)
                si = jnp.where(j == jb, edge[1], si)
                return acc[0] + (gr * sr + gi * si), acc[1] + (gi * sr - gr * si)

            return fold

        r_mid = jnp.where(d == 0, TA - 1, T)
        s_mid = tuple(jnp.broadcast_to(t, (NSEG, BW)) for t in get_s(pl.ds(r_mid, 1)))
        mid, acc = _seg_scan(get_g, put_g, 0, TSEG, lr, -li, 1 - d, (zero, zero), dlam_fold(0, TSEG, s_mid), (zero, zero))
        _, acc = _seg_scan(get_g, put_g, T, CSEG, lr, -li, 1 - d, mid, dlam_fold(T, CSEG, (zero, zero)), acc)
        dlam_ref[0, 0, :, 0:BW] = acc[0]
        dlam_ref[0, 0, :, BW:2 * BW] = acc[1]

        bm = b_ref[0, 0].astype(BF16)
        db = jnp.zeros((UW, 2 * BW), F32)
        dc = jnp.zeros((UW, 2 * BW), F32)
        for r0 in range(0, TA, RCH):
            rs = slice(r0, r0 + RCH)
            g = g_ref[rs, :].astype(BF16)
            dua_ref[rs, :] += lax.dot_general(g, bm, nt, preferred_element_type=F32)
            db = db + lax.dot_general(up_ref[rs, :].astype(BF16), g, tn, preferred_element_type=F32)
            if r0 < T:
                dc = dc + lax.dot_general(dssm_ref[rs, :].astype(BF16), s_ref[0, 0, rs, :].astype(BF16), tn,
                                          preferred_element_type=F32)
        db_ref[0, 0] = db
        dc_ref[0, 0] = dc

        @pl.when(d == 1)
        def _():
            _deinterleave_rows(dua_ref, nat_ref)
            du_ref[...] = nat_ref[...].astype(du_ref.dtype)

    blk4 = lambda shape: pl.BlockSpec((1, 1) + shape, lambda b, d: (d, b, 0, 0))
    lat = pl.BlockSpec((T, UW), lambda b, d: (0, b))
    vec = pl.BlockSpec((1, UW), lambda b, d: (0, b))
    return pl.pallas_call(
        kern,
        name="s5_bwd",
        grid=(NBLK, 2),
        in_specs=[lat, lat, pl.BlockSpec((TA, UW), lambda b, d: (0, UCOL0 + b)), vec,
                  blk4((TA, 2 * BW)), blk4((8, 2 * BW)), blk4((UW, 2 * BW)), blk4((UW, 2 * BW))],
        out_specs=[pl.BlockSpec((TA, UW), lambda b, d: (0, b)), vec, blk4((8, 2 * BW)), blk4((UW, 2 * BW)), blk4((UW, 2 * BW))],
        out_shape=[jax.ShapeDtypeStruct((TA, SW), BF16), jax.ShapeDtypeStruct((1, SW), F32),
                   jax.ShapeDtypeStruct((2, NBLK, 8, 2 * BW), F32),
                   jax.ShapeDtypeStruct((2, NBLK, UW, 2 * BW), F32), jax.ShapeDtypeStruct((2, NBLK, UW, 2 * BW), F32)],
        scratch_shapes=[pltpu.VMEM((TA, 2 * BW), F32), pltpu.VMEM((TA, UW), F32), pltpu.VMEM((T, UW), F32),
                        pltpu.VMEM((TA, UW), F32), pltpu.VMEM((TA, UW), F32)],
        compiler_params=_cparams(("parallel", "arbitrary")),
    )(d_ge, ssm, proj, dskip, states, lam, bmat, cmat)


TR = 256
TN_WIDE = 1024


def _vjp_rows(f, primals, cots, n_row):
    _, pull = jax.vjp(f, *primals)
    g = pull(cots)
    return list(g[:n_row]), list(g[n_row:])


class _GradDict(dict):
    def __init__(self, on_set=None):
        super().__init__()
        self._on_set = on_set
        self.tokens = {}

    def __setitem__(self, key, value):
        super().__setitem__(key, value)
        if self._on_set is not None:
            self._on_set(self)

    def order(self, key):
        return self.tokens.get(key, self.get(key))

    def finish(self, key, after):
        if self.on_finish is None:
            return ()
        return (self.on_finish(key, after),)

    on_finish = None


def _local_step(x, ctx, tgt, mod_lat, mod_ctx, wb, sp, on_grad=None, on_loss=None, on_finish=None, on_early=None):
    sh1, sc1, g1, sh2, sc2, g2 = [mod_lat[:, i * D:(i + 1) * D] for i in range(6)]
    csh1, csc1 = mod_ctx[:, 0:D], mod_ctx[:, D:2 * D]
    tabs = _rope_tables()
    sink = sp["attn_sink"].reshape(1, NH)
    dskip = sp["ssm_d"].reshape(1, SW)
    lg_mix, lb_mix = sp["ln_mix_g"].reshape(1, D), sp["ln_mix_b"].reshape(1, D)
    lg_mlp, lb_mlp = sp["ln_mlp_g"].reshape(1, D), sp["ln_mlp_b"].reshape(1, D)
    b1, b2 = sp["b_mlp1"].reshape(1, DFF), sp["b_mlp2"].reshape(1, D)
    s5_names = ("ssm_a_re", "ssm_a_im", "ssm_log_dt", "ssm_b_re", "ssm_b_im", "ssm_c_re", "ssm_c_im")
    (lam, bmat, cmat), s5_pull = jax.vjp(_s5_prep, *[sp[n] for n in s5_names])

    def ln_mod2(rv, vv):
        h = _f_ln_mod(rv[0], vv[0], vv[1])
        return [h, h], []

    h_lat, h_lat_t = _rowwise(ln_mod2, [(x, D, 0, 0)], [sc1, sh1], [(D, BF16), (D, BF16, True)], [], nrows=T, tr=TR, name="ln1_lat")
    h_ctx, h_ctx_t = _rowwise(ln_mod2, [(ctx, D, 0, 0)], [csc1, csh1], [(D, BF16), (D, BF16, True)], [], nrows=C, tr=TR,
                              name="ln1_ctx")
    h1 = jnp.concatenate([h_lat, h_ctx], 0)
    h1_t = jnp.concatenate([h_lat_t, h_ctx_t], 1)
    proj = _matmul(h1, wb["w_in"], mode="nn", name="proj", tm=768, tn=TN_WIDE)
    attn = _attn_fwd(proj, sink, tabs)
    states, ssm, ge = _s5_fwd(proj, dskip, lam, bmat, cmat)
    z = _matmul(ge, wb["w_glu"], mode="nn", name="glu_mm", tm=1024, tn=1024)

    def glu_act(rv, vv):
        return [_f_glu(rv[0])], []

    glu, = _rowwise(glu_act, [(z, 2 * SW, 0, 0)], [], [(SW, BF16)], [], nrows=T, tr=TR, name="glu_act")
    attn_d = _matmul(attn, wb["w_attn_up"], mode="nn", name="attn_up", tm=1024, tn=512)
    ssm_d = _matmul(glu, wb["w_ssm_up"], mode="nn", name="ssm_up", tm=1024, tn=512)
    ga_cb, gs_cb = (QW + 2 * KVW + SW) // D, (QW + 2 * KVW + SW) // D + 1

    def mix(rv, vv):
        m_ = _f_mix(*rv)
        return [m_, m_], []

    mixv, mix_t = _rowwise(mix, [(proj, D, ga_cb, 0), (proj, D, gs_cb, 0), (attn_d, D, 0, 0), (ssm_d, D, 0, 0)], [],
                           [(D, BF16), (D, BF16, True)], [], nrows=T, tr=TR, name="mix")
    y = _matmul(mixv, wb["w_out"], mode="nn", name="out_proj", tm=1024, tn=512)

    def post1(rv, vv):
        x1, h2 = _f_post1(rv[0], rv[1], *vv)
        return [x1, h2, h2], []

    x1, h2, h2_t = _rowwise(post1, [(x, D, 0, 0), (y, D, 0, 0)], [g1, lg_mix, lb_mix, sc2, sh2],
                            [(D, F32), (D, BF16), (D, BF16, True)], [], nrows=T, tr=TR, name="post1")

    def relu_sq(acc):
        r = jnp.maximum(acc, 0.0)
        return r, r * r, r * r

    r_act, act, act_t = _matmul(h2, wb["w_mlp1"], mode="nn", name="mlp1", tm=1024, tn=TN_WIDE, bias=b1,
                                out_dtypes=(BF16, BF16, BF16), out_t=(False, False, True), epilogue=relu_sq)
    mlp = _matmul(act, wb["w_mlp2"], mode="nn", name="mlp2", tm=512, tn=512)

    def loss_fb(rv, vv):
        x1_t, mlp_t, tgt_t = rv
        g2_v, lg_v, lb_v, b2_v = vv
        f = lambda a, m, g, p, q, b: _f_loss(a, m, tgt_t, g, p, q, b)
        val, grads = jax.value_and_grad(f, argnums=(0, 1, 2, 3, 4, 5))(x1_t, mlp_t, g2_v, lg_v, lb_v, b2_v)
        dx1, dmlp, dg2, dlg, dlb, db2 = grads
        return [dx1, dmlp], [jnp.reshape(val, (1, 1)), dg2, dlg, dlb, db2]

    dx1_a, d_mlp, loss_p, d_g2, d_lg_mlp, d_lb_mlp, d_b2 = _rowwise(
        loss_fb, [(x1, D, 0, 0), (mlp, D, 0, 0), (tgt, D, 0, 0)], [g2, lg_mlp, lb_mlp, b2],
        [(D, F32), (D, BF16)], [(1, 1), (1, D), (1, D), (1, D), (1, D)], nrows=T, tr=TR, name="loss_fb")

    gw = _GradDict(on_grad)
    gw.on_finish = on_finish
    loss_done = () if on_loss is None else (on_loss(loss_p),)
    gw["w_mlp2"] = _matmul(act_t, d_mlp, mode="nn", name="dw_mlp2", out_dtypes=(BF16,), tm=1024, tn=TN_WIDE, after=loss_done)
    da, = (_matmul(d_mlp, wb["w_mlp2"], mode="nt", name="d_act", out_dtypes=(BF16,), tm=1024, tn=TN_WIDE,
                   extras=(r_act,), epilogue=lambda acc, r: (acc * (2.0 * r.astype(F32)),), after=(gw.order("w_mlp2"),)),)
    pin = gw.finish("w_mlp2", da)
    ones = jnp.ones((8, T), BF16)
    d_b1 = _matmul(ones, da, mode="nn", name="db_mlp1", tm=8, tn=2048)[0:1]
    gw["w_mlp1"] = _matmul(h2_t, da, mode="nn", name="dw_mlp1", out_dtypes=(BF16,), tm=1024, tn=TN_WIDE, after=pin)
    dh2 = _matmul(da, wb["w_mlp1"], mode="nt", name="d_h2", tm=512, tn=512, after=(gw.order("w_mlp1"),))

    def post1_b(rv, vv):
        x_t, y_t, dx1_t, dh2_t = rv
        gr, gv = _vjp_rows(_f_post1, (x_t, y_t, *vv), (dx1_t, dh2_t), 2)
        return [gr[0], gr[1]], gv

    dx_a, dy, d_g1, d_lg_mix, d_lb_mix, d_sc2, d_sh2 = _rowwise(
        post1_b, [(x, D, 0, 0), (y, D, 0, 0), (dx1_a, D, 0, 0), (dh2, D, 0, 0)], [g1, lg_mix, lb_mix, sc2, sh2],
        [(D, F32), (D, BF16)], [(1, D)] * 5, nrows=T, tr=TR, name="post1_bwd")
    gw["w_out"] = _matmul(mix_t, dy, mode="nn", name="dw_out", out_dtypes=(BF16,), tm=1024, tn=512)
    dmix = _matmul(dy, wb["w_out"], mode="nt", name="d_mix", tm=1024, tn=512, after=(gw.order("w_out"),))

    def mix_b(rv, vv):
        gr, _ = _vjp_rows(_f_mix, tuple(rv[:4]), rv[4], 4)
        return gr, []

    d_ga, d_gs, d_attn_d, d_ssm_d = _rowwise(
        mix_b, [(proj, D, ga_cb, 0), (proj, D, gs_cb, 0), (attn_d, D, 0, 0), (ssm_d, D, 0, 0), (dmix, D, 0, 0)], [],
        [(D, BF16)] * 4, [], nrows=T, tr=TR, name="mix_bwd")
    pin = gw.finish("w_mlp1", d_ga)
    gw["w_attn_up"] = _matmul(attn, d_attn_d, mode="tn", name="dw_attn_up", out_dtypes=(BF16,), tm=512, tn=1024, tk=1024, after=pin)
    d_attn = _matmul(d_attn_d, wb["w_attn_up"], mode="nt", name="d_attn", out_dtypes=(BF16,), tm=1024, tn=512)
    gw["w_ssm_up"] = _matmul(glu, d_ssm_d, mode="tn", name="dw_ssm_up", out_dtypes=(BF16,), tm=512, tn=1024, tk=1024)
    d_glu = _matmul(d_ssm_d, wb["w_ssm_up"], mode="nt", name="d_glu", tm=1024, tn=512, after=(gw.order("w_attn_up"), gw.order("w_ssm_up")))

    def glu_b(rv, vv):
        gr, _ = _vjp_rows(_f_glu, (rv[0],), rv[1], 1)
        return gr, []

    dz, = _rowwise(glu_b, [(z, 2 * SW, 0, 0), (d_glu, SW, 0, 0)], [], [(2 * SW, BF16)], [], nrows=T, tr=TR, name="glu_bwd")
    gw["w_glu"] = _matmul(ge, dz, mode="tn", name="dw_glu", out_dtypes=(BF16,), tm=512, tn=1024, tk=1024)
    d_ge = _matmul(dz, wb["w_glu"], mode="nt", name="d_ge", tm=1024, tn=512, after=(gw.order("w_glu"),))

    du_all, d_dskip, dlam, dbmat, dcmat = _s5_bwd(d_ge, ssm, proj, dskip, states, lam, bmat, cmat)
    s5_grads = s5_pull((dlam, dbmat, dcmat))
    early = dict(zip(s5_names, s5_grads), ssm_d=d_dskip)
    if on_early is not None:
        on_early(early)
    pin = gw.finish("w_glu", du_all)

    dq, dk, dv, dsink = _attn_bwd(proj, d_attn, sink, tabs)
    zc = lambda w: jnp.zeros((C, w), BF16)
    dproj = jnp.concatenate([
        jnp.concatenate([dq, zc(QW)], 0), dk, dv, du_all,
        jnp.concatenate([d_ga, zc(D)], 0), jnp.concatenate([d_gs, zc(D)], 0)], 1)
    gw["w_in"] = _matmul(h1_t, dproj, mode="nn", name="dw_in", out_dtypes=(BF16,), tm=1024, tn=TN_WIDE, after=pin)
    pin = gw.finish("w_in", gw["w_in"])
    dh1 = _matmul(dproj, wb["w_in"], mode="nt", name="d_h1", tm=768, tn=512, after=pin)

    def ln1_b(rv, vv):
        x_t, dh_t, dxa_t = rv
        gr, gv = _vjp_rows(_f_ln_mod, (x_t, vv[0], vv[1]), dh_t, 1)
        return [gr[0] + dxa_t], gv

    grad_x, d_sc1, d_sh1 = _rowwise(ln1_b, [(x, D, 0, 0), (dh1, D, 0, 0), (dx_a, D, 0, 0)], [sc1, sh1],
                                    [(D, F32)], [(1, D), (1, D)], nrows=T, tr=TR, name="ln1_lat_bwd")

    def ln1c_b(rv, vv):
        _, gv = _vjp_rows(_f_ln_mod, (rv[0], vv[0], vv[1]), rv[1], 1)
        return [], gv

    d_csc1, d_csh1 = _rowwise(ln1c_b, [(ctx, D, 0, 0), (dh1, D, 0, T // TR)], [csc1, csh1],
                              [], [(1, D), (1, D)], nrows=C, tr=TR, name="ln1_ctx_bwd")

    d_mod_lat = jnp.concatenate([d_sh1, d_sc1, d_g1, d_sh2, d_sc2, d_g2], 1)
    zv = jnp.zeros((1, D), F32)
    d_mod_ctx = jnp.concatenate([d_csh1, d_csc1, zv, zv, zv, zv], 1)
    gs = {n: g for n, g in zip(s5_names, s5_grads)}
    gs["attn_sink"] = dsink[:, 0]
    gs["ssm_d"] = d_dskip
    gs["ln_mix_g"], gs["ln_mix_b"] = d_lg_mix, d_lb_mix
    gs["ln_mlp_g"], gs["ln_mlp_b"] = d_lg_mlp, d_lb_mlp
    gs["b_mlp1"], gs["b_mlp2"] = d_b1, d_b2
    return loss_p, grad_x, d_mod_lat, d_mod_ctx, gw, gs


def _my_pos():
    return lax.axis_index("x"), lax.axis_index("y"), lax.axis_index("c")


def _flip(p, bit):
    return 1 - p if bit else p


def _peer(pos, k):
    x, y, c = pos
    return (_flip(x, (k >> 2) & 1), _flip(y, (k >> 1) & 1), _flip(c, k & 1))


def _lin(pos):
    return 4 * pos[0] + 2 * pos[1] + pos[2]


def _allgather_small(v, name):
    r, w = v.shape

    def body(v_ref, out_ref, send_sems, recv_sems, local_sem):
        me = _my_pos()
        mine = pltpu.make_async_copy(v_ref, out_ref.at[_lin(me)], local_sem)
        mine.start()
        sends = []
        for k in range(1, N_DEV):
            cp = pltpu.make_async_remote_copy(src_ref=v_ref, dst_ref=out_ref.at[_lin(me)], send_sem=send_sems.at[k - 1],
                                              recv_sem=recv_sems.at[k - 1], device_id=_peer(me, k), device_id_type=MESH)
            cp.start()
            sends.append(cp)
        for k in range(1, N_DEV):
            peer = _peer(me, k)
            pltpu.make_async_remote_copy(src_ref=v_ref, dst_ref=out_ref.at[_lin(peer)], send_sem=send_sems.at[k - 1],
                                         recv_sem=recv_sems.at[k - 1], device_id=peer, device_id_type=MESH).wait_recv()
        for cp in sends:
            cp.wait_send()
        mine.wait()

    return pl.pallas_call(
        body,
        name=name,
        out_shape=jax.ShapeDtypeStruct((N_DEV, r, w), v.dtype),
        in_specs=[pl.BlockSpec(memory_space=pltpu.VMEM)],
        out_specs=pl.BlockSpec(memory_space=pltpu.VMEM),
        scratch_shapes=[pltpu.SemaphoreType.DMA((N_DEV - 1,)), pltpu.SemaphoreType.DMA((N_DEV - 1,)), pltpu.SemaphoreType.DMA],
        compiler_params=pltpu.CompilerParams(vmem_limit_bytes=VMEM_LIMIT_BYTES),
    )(v)


def _block_of(ref, kind, idx, n):
    start = pl.multiple_of(idx * n, 128)
    if kind == "col":
        return ref.at[:, pl.ds(start, n)]
    return ref.at[pl.ds(start, n), :]


def _handshake(peers):
    barrier = pltpu.get_barrier_semaphore()
    for peer in peers:
        pl.semaphore_signal(barrier, inc=1, device_id=peer, device_id_type=MESH)
    pl.semaphore_wait(barrier, len(peers))


def _allgather_weights_seq(shards, kinds, name, collective_id):
    nt = len(shards)
    hbm = pltpu.MemorySpace.HBM
    ins = [jax.new_ref(s, memory_space=hbm) for s in shards]
    outs = []
    for s, kind in zip(shards, kinds):
        k, n = s.shape
        shape = (k, n * N_DEV) if kind == "col" else (k * N_DEV, n)
        outs.append(jax.empty_ref(jax.ShapeDtypeStruct(shape, s.dtype), memory_space=hbm))

    @functools.partial(
        pl.kernel, mesh=plsc.ScalarSubcoreMesh(axis_name="seq", num_cores=1), name=name,
        scratch_types=(pltpu.SemaphoreType.DMA((nt, N_DEV - 1)), pltpu.SemaphoreType.DMA((nt, N_DEV - 1)),
                       pltpu.SemaphoreType.DMA((nt,))),
        compiler_params=pltpu.CompilerParams(collective_id=collective_id))
    def launch(send_sems, recv_sems, local_sems):
        x, y, c = _my_pos()
        me, sibling = (x, y, c), (x, y, 1 - c)
        chips = [(1 - x, y), (x, 1 - y), (1 - x, 1 - y)]
        _handshake([sibling] + [(*chip, c) for chip in chips])

        def blk(t, pos):
            n = shards[t].shape[1] if kinds[t] == "col" else shards[t].shape[0]
            return _block_of(outs[t], kinds[t], _lin(pos), n)

        def copy(t, k, block, to, src=None):
            return pltpu.make_async_remote_copy(src_ref=blk(t, block) if src is None else src, dst_ref=blk(t, block),
                                                send_sem=send_sems.at[t, k], recv_sem=recv_sems.at[t, k],
                                                device_id=to, device_id_type=MESH)

        local, sends = [], []
        for t in range(nt):
            mine = pltpu.make_async_copy(ins[t], blk(t, me), local_sems.at[t])
            mine.start()
            local.append(mine)
            first = [copy(t, 0, me, sibling, src=ins[t])]
            first += [copy(t, 1 + j, me, (*chip, c), src=ins[t]) for j, chip in enumerate(chips)]
            for cp in first:
                cp.start()
            sends += first
        for t in range(nt):
            for j, chip in enumerate(chips):
                copy(t, 1 + j, (*chip, c), me).wait_recv()
                fwd = copy(t, 4 + j, (*chip, c), sibling)
                fwd.start()
                sends.append(fwd)
        for t in range(nt):
            copy(t, 0, sibling, me).wait_recv()
            for j, chip in enumerate(chips):
                copy(t, 4 + j, (*chip, 1 - c), me).wait_recv()
        for cp in sends:
            cp.wait_send()
        for cp in local:
            cp.wait()

    launch()
    return [o[...] for o in outs]


def _allgather_small_seq(v, name, collective_id):
    hbm = pltpu.MemorySpace.HBM
    src = jax.new_ref(v, memory_space=hbm)
    out = jax.empty_ref(jax.ShapeDtypeStruct((N_DEV,) + v.shape, v.dtype), memory_space=hbm)

    @functools.partial(
        pl.kernel, mesh=plsc.ScalarSubcoreMesh(axis_name="seq", num_cores=1), name=name,
        scratch_types=(pltpu.SemaphoreType.DMA((N_DEV - 1,)), pltpu.SemaphoreType.DMA((N_DEV - 1,)), pltpu.SemaphoreType.DMA),
        compiler_params=pltpu.CompilerParams(collective_id=collective_id))
    def launch(send_sems, recv_sems, local_sem):
        me = _my_pos()
        _handshake([_peer(me, k) for k in range(1, N_DEV)])
        mine = pltpu.make_async_copy(src, out.at[_lin(me)], local_sem)
        mine.start()
        sends = []
        for k in range(1, N_DEV):
            cp = pltpu.make_async_remote_copy(src_ref=src, dst_ref=out.at[_lin(me)], send_sem=send_sems.at[k - 1],
                                              recv_sem=recv_sems.at[k - 1], device_id=_peer(me, k), device_id_type=MESH)
            cp.start()
            sends.append(cp)
        for k in range(1, N_DEV):
            peer = _peer(me, k)
            pltpu.make_async_remote_copy(src_ref=src, dst_ref=out.at[_lin(peer)], send_sem=send_sems.at[k - 1],
                                         recv_sem=recv_sems.at[k - 1], device_id=peer, device_id_type=MESH).wait_recv()
        for cp in sends:
            cp.wait_send()
        mine.wait()

    launch()
    return out[...]


N_CHIP = N_DEV // 2


def _chip_of(pos):
    return 2 * pos[0] + pos[1]


def _pair_exchange_seq(grads, kinds, name, collective_id):
    nt = len(grads)
    hbm = pltpu.MemorySpace.HBM
    shard_shapes = _shard_shapes(grads, kinds)
    ins = [jax.new_ref(g, memory_space=hbm) for g in grads]
    outs = [jax.empty_ref(jax.ShapeDtypeStruct((N_CHIP,) + s, g.dtype), memory_space=hbm) for s, g in zip(shard_shapes, grads)]

    @functools.partial(
        pl.kernel, mesh=plsc.ScalarSubcoreMesh(axis_name="seq", num_cores=1), name=name,
        scratch_types=(pltpu.SemaphoreType.DMA((nt, N_CHIP)), pltpu.SemaphoreType.DMA((nt, N_CHIP))),
        compiler_params=pltpu.CompilerParams(collective_id=collective_id))
    def launch(send_sems, recv_sems):
        x, y, c = _my_pos()
        sibling = (x, y, 1 - c)
        _handshake([sibling])
        copies = []
        for t in range(nt):
            n = shard_shapes[t][1] if kinds[t] == "col" else shard_shapes[t][0]
            for q in range(N_CHIP):
                cp = pltpu.make_async_remote_copy(src_ref=_block_of(ins[t], kinds[t], 2 * q + (1 - c), n), dst_ref=outs[t].at[q],
                                                  send_sem=send_sems.at[t, q], recv_sem=recv_sems.at[t, q],
                                                  device_id=sibling, device_id_type=MESH)
                cp.start()
                copies.append(cp)
        for cp in copies:
            cp.wait_recv()
        for cp in copies:
            cp.wait_send()

    launch()
    return [o[...] for o in outs]


def _pair_add(g, half, kind, name, after=()):
    nq, k, ns = half.shape
    tr = min(k, 512)
    c_idx = lax.axis_index("c").astype(jnp.int32).reshape(1)
    if kind == "col":
        g_spec = pl.BlockSpec((tr, ns), lambda q, i, c_ref: (i, 2 * q + c_ref[0]))
    else:
        g_spec = pl.BlockSpec((tr, ns), lambda q, i, c_ref: ((2 * q + c_ref[0]) * (k // tr) + i, 0))
    n_after = len(after)

    def kern(c_ref, g_ref, h_ref, *rest):
        o_ref = rest[n_after]
        o_ref[0] = (g_ref[...].astype(F32) + h_ref[0].astype(F32)).astype(o_ref.dtype)

    return pl.pallas_call(
        kern,
        name=name,
        grid_spec=pltpu.PrefetchScalarGridSpec(
            num_scalar_prefetch=1,
            grid=(nq, k // tr),
            in_specs=[g_spec, pl.BlockSpec((1, tr, ns), lambda q, i, c_ref: (q, i, 0))] + [pl.BlockSpec(memory_space=pl.ANY)] * n_after,
            out_specs=pl.BlockSpec((1, tr, ns), lambda q, i, c_ref: (q, i, 0)),
        ),
        out_shape=jax.ShapeDtypeStruct(half.shape, half.dtype),
        compiler_params=_cparams(("parallel", "parallel")),
    )(c_idx, g, half, *after)


def _chip_exchange_seq(psums, name, collective_id):
    nt = len(psums)
    hbm = pltpu.MemorySpace.HBM
    ins = [jax.new_ref(s, memory_space=hbm) for s in psums]
    outs = [jax.empty_ref(jax.ShapeDtypeStruct(s.shape, s.dtype), memory_space=hbm) for s in psums]

    @functools.partial(
        pl.kernel, mesh=plsc.ScalarSubcoreMesh(axis_name="seq", num_cores=1), name=name,
        scratch_types=(pltpu.SemaphoreType.DMA((nt, N_CHIP - 1)), pltpu.SemaphoreType.DMA((nt, N_CHIP - 1)),
                       pltpu.SemaphoreType.DMA((nt,))),
        compiler_params=pltpu.CompilerParams(collective_id=collective_id))
    def launch(send_sems, recv_sems, local_sems):
        me = _my_pos()
        peers = [_peer(me, k) for k in (2, 4, 6)]
        _handshake(peers)
        mine = _chip_of(me)
        local, sends = [], []
        for t in range(nt):
            cp = pltpu.make_async_copy(ins[t].at[mine], outs[t].at[mine], local_sems.at[t])
            cp.start()
            local.append(cp)
            for j, peer in enumerate(peers):
                cp = pltpu.make_async_remote_copy(src_ref=ins[t].at[_chip_of(peer)], dst_ref=outs[t].at[mine],
                                                  send_sem=send_sems.at[t, j], recv_sem=recv_sems.at[t, j],
                                                  device_id=peer, device_id_type=MESH)
                cp.start()
                sends.append(cp)
        for t in range(nt):
            for j, peer in enumerate(peers):
                pltpu.make_async_remote_copy(src_ref=ins[t].at[mine], dst_ref=outs[t].at[_chip_of(peer)],
                                             send_sem=send_sems.at[t, j], recv_sem=recv_sems.at[t, j],
                                             device_id=peer, device_id_type=MESH).wait_recv()
        for cp in sends:
            cp.wait_send()
        for cp in local:
            cp.wait()

    launch()
    return [o[...] for o in outs]


def _shard_shapes(grads, kinds):
    return [(g.shape[0], g.shape[1] // N_DEV) if kind == "col" else (g.shape[0] // N_DEV, g.shape[1]) for g, kind in zip(grads, kinds)]


def _adam(g_slots, w, m, v, *, tr, name, after=()):
    ns, r, wd = g_slots.shape
    tr = min(tr, r)
    assert r % tr == 0, (name, r, tr)
    c1 = 1.0 - ADAM_B1 ** ADAM_STEP
    c2 = 1.0 - ADAM_B2 ** ADAM_STEP
    n_after = len(after)

    def kern(g_ref, w_ref, m_ref, v_ref, *rest):
        go_ref, d_ref, mo_ref, vo_ref = rest[n_after:]
        g = g_ref[0].astype(F32)
        for s in range(1, ns):
            g = g + g_ref[s].astype(F32)
        m_new = ADAM_B1 * m_ref[...] + (1.0 - ADAM_B1) * g
        v_new = ADAM_B2 * v_ref[...] + (1.0 - ADAM_B2) * (g * g)
        m_hat = m_new / c1
        v_hat = v_new / c2
        go_ref[...] = g
        d_ref[...] = -ADAM_LR * (m_hat / (jnp.sqrt(v_hat) + ADAM_EPS) + ADAM_WD * w_ref[...])
        mo_ref[...] = m_new
        vo_ref[...] = v_new

    tile = pl.BlockSpec((tr, wd), lambda i: (i, 0))
    return pl.pallas_call(
        kern,
        name=name,
        grid=(r // tr,),
        in_specs=[pl.BlockSpec((ns, tr, wd), lambda i: (0, i, 0)), tile, tile, tile] + [pl.BlockSpec(memory_space=pl.ANY)] * n_after,
        out_specs=[tile] * 4,
        out_shape=[jax.ShapeDtypeStruct((r, wd), F32)] * 4,
        compiler_params=_cparams(("parallel",)),
    )(g_slots, w, m, v, *after)


SMALL = ("c_ctx", "b_ada", "attn_sink", "ssm_a_re", "ssm_a_im", "ssm_log_dt", "ssm_b_re", "ssm_b_im", "ssm_c_re", "ssm_c_im",
         "ssm_d", "ln_mix_g", "ln_mix_b", "b_mlp1", "b_mlp2", "ln_mlp_g", "ln_mlp_b")
BIG = ("w_in", "w_glu", "w_attn_up", "w_ssm_up", "w_out", "w_mlp1", "w_mlp2")
BIG_KIND = ("col", "col", "col", "col", "row", "col", "row")
AG_GROUPS = (("w_in",), ("w_glu", "w_attn_up", "w_ssm_up", "w_out"), ("w_mlp1",), ("w_mlp2",))
AG_COLLECTIVE_ID0 = 1
RS_GROUPS = (("w_mlp2",), ("w_mlp1",), ("w_out", "w_attn_up", "w_ssm_up", "w_glu"), ("w_in",))
RS_COLLECTIVE_ID0 = AG_COLLECTIVE_ID0 + len(AG_GROUPS)
SMALL_EARLY = ("ssm_a_re", "ssm_a_im", "ssm_log_dt", "ssm_b_re", "ssm_b_im", "ssm_c_re", "ssm_c_im", "ssm_d")
SMALL_LATE = tuple(n for n in SMALL if n not in SMALL_EARLY)
SMALL_COLLECTIVE_ID0 = RS_COLLECTIVE_ID0 + 2 * len(RS_GROUPS)
LANES = 128


def _pack(parts):
    rows = []
    for p in parts:
        flat = p.reshape(-1).astype(F32)
        pad = (-flat.shape[0]) % LANES
        rows.append(jnp.pad(flat, (0, pad)).reshape(-1, LANES))
    packed = jnp.concatenate(rows, 0)
    return jnp.pad(packed, ((0, (-packed.shape[0]) % 8), (0, 0)))


def _unpack(packed, shapes):
    out, r0 = [], 0
    for s in shapes:
        n = math.prod(s)
        nr = -(-n // LANES)
        out.append(packed[r0:r0 + nr].reshape(-1)[:n].reshape(s))
        r0 += nr
    return out


WEIGHTS = ("c_ctx", "w_ada", "b_ada", "w_in", "attn_sink", "ssm_a_re", "ssm_a_im", "ssm_log_dt", "ssm_b_re", "ssm_b_im",
           "ssm_c_re", "ssm_c_im", "ssm_d", "w_glu", "w_attn_up", "w_ssm_up", "w_out", "ln_mix_g", "ln_mix_b", "w_mlp1",
           "b_mlp1", "w_mlp2", "b_mlp2", "ln_mlp_g", "ln_mlp_b")
ADA_COLS = 6 * D // N_DEV


def _step(x, c, ctx, loss_target, p, m, v):
    me = _lin(_my_pos())
    x2, ctx2, tgt2 = x[0], ctx[0], loss_target[0]

    wb = {}
    for gi, group in enumerate(AG_GROUPS):
        full = _allgather_weights_seq([p[n][0].astype(BF16) for n in group], [BIG_KIND[BIG.index(n)] for n in group],
                                      "allgather_seq%d" % gi, AG_COLLECTIVE_ID0 + gi)
        wb.update(zip(group, full))

    c_all = _allgather_small(jnp.broadcast_to(c, (8, D)), "gather_c")[:, 0, :]
    cc = p["c_ctx"].reshape(1, D)
    s_in = jnp.concatenate([c_all, cc, jnp.zeros((7, D), F32)], 0)
    s_act, = _rowwise(lambda rv, vv: ([_silu(rv[0])], []), [(s_in, D, 0, 0)], [], [(D, F32)], [], nrows=16, tr=16, name="silu_c")
    b_mine = lax.dynamic_slice_in_dim(p["b_ada"], me * ADA_COLS, ADA_COLS, axis=1)
    mod_part = _matmul(s_act, p["w_ada"][0], mode="nn", name="ada_fwd", tm=16, tn=512, bias=b_mine)
    mod_all = _allgather_small(mod_part, "gather_mod")
    mod_lat = lax.dynamic_index_in_dim(mod_all, me, axis=1, keepdims=False).reshape(1, 6 * D)
    mod_ctx = mod_all[:, 8, :].reshape(1, 6 * D)

    sp = {n: p[n][0] for n in SMALL if n not in ("c_ctx", "b_ada")}
    recv, halves = {}, {}

    def on_grad(gw):
        for gi, group in enumerate(RS_GROUPS):
            if gi not in halves and all(n in gw for n in group):
                kinds = [BIG_KIND[BIG.index(n)] for n in group]
                halves[gi] = (dict(gw), _pair_exchange_seq([gw[n] for n in group], kinds, "pair_exchange%d" % gi, RS_COLLECTIVE_ID0 + 2 * gi))

    def on_finish(key, after):
        gi = [i for i, group in enumerate(RS_GROUPS) if key in group][0]
        group = RS_GROUPS[gi]
        grads, half = halves[gi]
        prev = tuple(recv[n] for n in RS_GROUPS[gi - 1][:1]) if gi else ()
        if gi == len(RS_GROUPS) - 1:
            prev += (small["early"],)
        psums =[_pair_add(grads[n], h, BIG_KIND[BIG.index(n)], "pair_add_" + n, after=(after,) + prev) for n, h in zip(group, half)]
        recv.update(zip(group, _chip_exchange_seq(psums, "chip_exchange%d" % gi, RS_COLLECTIVE_ID0 + 2 * gi + 1)))
        return psums[-1]

    small = {}

    def on_early(gs_early):
        small["early"] = _allgather_small_seq(_pack([gs_early[n] for n in SMALL_EARLY]), "gather_small_early", SMALL_COLLECTIVE_ID0)

    total = {}

    def on_loss(loss_p):
        total["loss"] = lax.psum(loss_p[0, 0], ("x", "y", "c"))
        return total["loss"].reshape(1, 1)

    loss_p, grad_x, d_mod_lat, d_mod_ctx, gw, gs = _local_step(x2, ctx2, tgt2, mod_lat, mod_ctx, wb, sp, on_grad, on_loss, on_finish, on_early)

    g_early = small["early"]
    res = {}
    last = ()

    def adam_small(names, g_pack, tag, after):
        sm = _adam(g_pack, _pack([p[n] for n in names]), _pack([m[n] for n in names]), _pack([v[n] for n in names]),
                   tr=g_pack.shape[1], name="adam_small_" + tag, after=after)
        shapes = [p[n].shape for n in names]
        for j, outs in enumerate(zip(*[_unpack(a, shapes) for a in sm])):
            res[names[j]] = outs
        return (sm[0],)

    for gi, group in enumerate(RS_GROUPS):
        if gi == len(RS_GROUPS) - 1:
            last = adam_small(SMALL_EARLY, g_early, "early", last)
        for n in group:
            res[n] = _adam(recv[n], p[n][0], m[n][0], v[n][0], tr=256, name="adam_" + n, after=last)
            last = (res[n][0],)

    dm = jnp.concatenate([d_mod_lat, d_mod_ctx, jnp.zeros((6, 6 * D), F32)], 0)
    dm_all = _allgather_small_seq(dm, "gather_dmod", SMALL_COLLECTIVE_ID0 + 1)
    dm_all = lax.optimization_barrier((dm_all,) + last)[0]
    dm2 = jnp.concatenate([dm_all[:, 0, :], dm_all[:, 1, :]], 0)
    dm2_mine = lax.dynamic_slice_in_dim(dm2, me * ADA_COLS, ADA_COLS, axis=1)
    s2 = jnp.concatenate([s_act[0:8], jnp.broadcast_to(s_act[8:9], (8, D))], 0)
    g_w_ada = _matmul(s2, dm2_mine, mode="tn", name="dw_ada", tm=512, tn=ADA_COLS, after=last)
    dsc_part = _matmul(dm2_mine[8:16], p["w_ada"][0], mode="nt", name="d_silu_cctx", tm=8, tn=512, after=last)

    def cctx_b(rv, vv):
        _, pull = jax.vjp(_silu, vv[0])
        return [], [pull(jnp.sum(rv[0], axis=0, keepdims=True))[0]]

    g_cctx, = _rowwise(cctx_b, [(dsc_part, D, 0, 0)], [cc], [], [(1, D)], nrows=8, tr=8, name="cctx_bwd")
    gs["c_ctx"] = g_cctx
    gs["b_ada"] = d_mod_lat + d_mod_ctx

    res["w_ada"] = _adam(g_w_ada[None], p["w_ada"][0], m["w_ada"][0], v["w_ada"][0], tr=256, name="adam_w_ada")

    g_late = _allgather_small_seq(_pack([gs[n] for n in SMALL_LATE]), "gather_small_late", SMALL_COLLECTIVE_ID0 + 2)
    adam_small(SMALL_LATE, g_late, "late", (res["w_ada"][0],))

    outs = [total["loss"], grad_x[None]]
    for j in range(4):
        outs += [res[n][j].reshape(p[n].shape) for n in WEIGHTS]
    return tuple(outs)


def kernel(x, c, ctx, c_ctx, w_ada, b_ada, w_in, attn_sink, ssm_a_re, ssm_a_im, ssm_log_dt, ssm_b_re, ssm_b_im, ssm_c_re, ssm_c_im, ssm_d, w_glu, w_attn_up, w_ssm_up, w_out, ln_mix_g, ln_mix_b, w_mlp1, b_mlp1, w_mlp2, b_mlp2, ln_mlp_g, ln_mlp_b, loss_target, m_c_ctx, m_w_ada, m_b_ada, m_w_in, m_attn_sink, m_ssm_a_re, m_ssm_a_im, m_ssm_log_dt, m_ssm_b_re, m_ssm_b_im, m_ssm_c_re, m_ssm_c_im, m_ssm_d, m_w_glu, m_w_attn_up, m_w_ssm_up, m_w_out, m_ln_mix_g, m_ln_mix_b, m_w_mlp1, m_b_mlp1, m_w_mlp2, m_b_mlp2, m_ln_mlp_g, m_ln_mlp_b, v_c_ctx, v_w_ada, v_b_ada, v_w_in, v_attn_sink, v_ssm_a_re, v_ssm_a_im, v_ssm_log_dt, v_ssm_b_re, v_ssm_b_im, v_ssm_c_re, v_ssm_c_im, v_ssm_d, v_w_glu, v_w_attn_up, v_w_ssm_up, v_w_out, v_ln_mix_g, v_ln_mix_b, v_w_mlp1, v_b_mlp1, v_w_mlp2, v_b_mlp2, v_ln_mlp_g, v_ln_mlp_b):
    given = dict(locals())
    p = {n: given[n] for n in WEIGHTS}
    m = {n: given["m_" + n] for n in WEIGHTS}
    v = {n: given["v_" + n] for n in WEIGHTS}
    return _step(x, c, ctx, loss_target, p, m, v)
```

```python
import functools
import math

import jax
import jax.numpy as jnp
from jax import lax
from jax.experimental import pallas as pl
from jax.experimental.pallas import tpu as pltpu
from jax.experimental.pallas import tpu_sc as plsc

F32 = jnp.float32
BF16 = jnp.bfloat16

N_DEV = 8
D = 2048
T = 2048
C = 256
TA = T + C
GRID_W = 64
HD = 128
NH = 8
NKV = 2
GROUP = NH // NKV
WINDOW = 128
QW = NH * HD
KVW = NKV * HD
SW = D // 4
SG = 16
NG = SW // SG
SP = 64
DFF = 4 * D
IN_COLS = QW + 2 * KVW + SW + 2 * D
ALPHA = 2.0 ** 0.25
LN_EPS = 1e-6
NEG_INF = -1e30
ROPE_BASE = 10000.0
ATT_SCALE = HD ** -0.5

NSEG = 8
GBLK = 8
NBLK = NG // GBLK
BW = GBLK * SP
UW = GBLK * SG

ADAM_LR = 0.001
ADAM_B1 = 0.9
ADAM_B2 = 0.999
ADAM_EPS = 1e-08
ADAM_WD = 0.01
ADAM_STEP = 10

VMEM_LIMIT_BYTES = 56 * 1024 * 1024
MESH = pl.DeviceIdType.MESH


def _cparams(sem=None):
    return pltpu.CompilerParams(dimension_semantics=sem, vmem_limit_bytes=VMEM_LIMIT_BYTES)


def _matmul(a, b, *, mode, name, out_dtypes=(F32,), tm=512, tn=512, tk=None, bias=None, extras=(), epilogue=None, after=(),
            out_t=None):
    if mode == "nn":
        (M, K), (K2, N) = a.shape, b.shape
    elif mode == "nt":
        (M, K), (N, K2) = a.shape, b.shape
    else:
        (K, M), (K2, N) = a.shape, b.shape
    assert K == K2, (name, a.shape, b.shape)
    tm, tn, tk = min(tm, M), min(tn, N), min(tk or K, K)
    assert M % tm == 0 and N % tn == 0 and K % tk == 0, (name, M, N, K, tm, tn, tk)
    nk = K // tk
    if mode == "tn":
        a_spec = pl.BlockSpec((tk, tm), lambda i, j, k: (k, i))
    else:
        a_spec = pl.BlockSpec((tm, tk), lambda i, j, k: (i, k))
    if mode == "nt":
        b_spec = pl.BlockSpec((tn, tk), lambda i, j, k: (j, k))
    else:
        b_spec = pl.BlockSpec((tk, tn), lambda i, j, k: (k, j))
    dims = {"nn": (((1,), (0,)), ((), ())), "nt": (((1,), (1,)), ((), ())), "tn": (((0,), (0,)), ((), ()))}[mode]
    in_specs = [a_spec, b_spec]
    operands = [a, b]
    if bias is not None:
        in_specs.append(pl.BlockSpec((1, tn), lambda i, j, k: (0, j)))
        operands.append(bias)
    for e in extras:
        in_specs.append(pl.BlockSpec((tm, tn), lambda i, j, k: (i, j)))
        operands.append(e)
    n_ex = len(extras)
    for t in after:
        in_specs.append(pl.BlockSpec(memory_space=pl.ANY))
        operands.append(t)
    n_after = len(after)
    n_out = len(out_dtypes)
    out_t = tuple(out_t) if out_t is not None else (False,) * n_out
    has_bias = bias is not None

    def kern(*refs):
        a_ref, b_ref = refs[0], refs[1]
        pos = 2
        bias_ref = None
        if has_bias:
            bias_ref = refs[pos]
            pos += 1
        ex_refs = refs[pos:pos + n_ex]
        pos += n_ex + n_after
        out_refs = refs[pos:pos + n_out]
        acc_ref = refs[pos + n_out] if nk > 1 else None

        def finish(r):
            if has_bias:
                r = r + bias_ref[...]
            outs = epilogue(r, *[e[...] for e in ex_refs]) if epilogue is not None else (r,)
            for o_ref, o, tr_ in zip(out_refs, outs, out_t):
                o_ref[...] = (o.T if tr_ else o).astype(o_ref.dtype)

        part = lax.dot_general(a_ref[...].astype(BF16), b_ref[...].astype(BF16), dims, preferred_element_type=F32)
        if nk == 1:
            finish(part)
        else:
            k = pl.program_id(2)

            @pl.when(k == 0)
            def _():
                acc_ref[...] = part

            @pl.when(k > 0)
            def _():
                acc_ref[...] += part

            @pl.when(k == nk - 1)
            def _():
                finish(acc_ref[...])

    outs = pl.pallas_call(
        kern,
        name=name,
        grid=(M // tm, N // tn, nk),
        in_specs=in_specs,
        out_specs=[pl.BlockSpec((tn, tm), lambda i, j, k: (j, i)) if tr_ else pl.BlockSpec((tm, tn), lambda i, j, k: (i, j))
                   for tr_ in out_t],
        out_shape=[jax.ShapeDtypeStruct((N, M) if tr_ else (M, N), dt) for dt, tr_ in zip(out_dtypes, out_t)],
        scratch_shapes=[pltpu.VMEM((tm, tn), F32)] if nk > 1 else [],
        compiler_params=_cparams(("parallel", "parallel", "arbitrary")),
    )(*operands)
    return outs[0] if n_out == 1 else tuple(outs)


def _rowwise(fn, rows, vecs, outs, vec_outs, *, nrows, tr, name, after=()):
    n_rows, n_vecs, n_outs, n_after = len(rows), len(vecs), len(outs), len(after)
    in_specs = [pl.BlockSpec((tr, w), lambda i, cb=cb, ro=ro: (i + ro, cb)) for (_, w, cb, ro) in rows]
    in_specs += [pl.BlockSpec(v.shape, lambda i: (0, 0)) for v in vecs]
    in_specs += [pl.BlockSpec(memory_space=pl.ANY)] * n_after
    outs = [o if len(o) == 3 else (*o, False) for o in outs]
    out_specs = [pl.BlockSpec((w, tr), lambda i: (0, i)) if tr_ else pl.BlockSpec((tr, w), lambda i: (i, 0)) for (w, _, tr_) in outs]
    out_specs += [pl.BlockSpec(s, lambda i: (0, 0)) for s in vec_outs]
    out_shape = [jax.ShapeDtypeStruct((w, nrows) if tr_ else (nrows, w), dt) for (w, dt, tr_) in outs]
    out_tr = [tr_ for (_, _, tr_) in outs]
    out_shape += [jax.ShapeDtypeStruct(s, F32) for s in vec_outs]

    def kern(*refs):
        rvals = [r[...] for r in refs[:n_rows]]
        vvals = [r[...] for r in refs[n_rows:n_rows + n_vecs]]
        first_out = n_rows + n_vecs + n_after
        o_refs = refs[first_out:first_out + n_outs]
        v_refs = refs[first_out + n_outs:]
        ro, vo = fn(rvals, vvals)
        for r, val, tr_ in zip(o_refs, ro, out_tr):
            r[...] = (val.astype(F32).T if tr_ else val).astype(r.dtype)
        i = pl.program_id(0)
        for r, val in zip(v_refs, vo):
            @pl.when(i == 0)
            def _(r=r, val=val):
                r[...] = val.astype(F32)

            @pl.when(i > 0)
            def _(r=r, val=val):
                r[...] += val.astype(F32)

    res = pl.pallas_call(
        kern,
        name=name,
        grid=(nrows // tr,),
        in_specs=in_specs,
        out_specs=out_specs,
        out_shape=out_shape,
        compiler_params=_cparams(("arbitrary",)),
    )(*[r[0] for r in rows], *vecs, *after)
    return list(res)


def _ln(x):
    mu = jnp.mean(x, axis=-1, keepdims=True)
    xc = x - mu
    var = jnp.mean(xc * xc, axis=-1, keepdims=True)
    return xc * lax.rsqrt(var + LN_EPS)


def _sigmoid(x):
    return 1.0 / (1.0 + jnp.exp(-x))


def _gelu(x):
    return 0.5 * x * (1.0 + jnp.tanh(math.sqrt(2.0 / math.pi) * (x + 0.044715 * (x * x * x))))


def _silu(x):
    return x * _sigmoid(x)


def _f_ln_mod(x, sc, sh):
    return _ln(x) * (1.0 + sc) + sh


def _f_glu(z):
    return z[:, :SW] * _sigmoid(z[:, SW:])


def _f_mix(ga, gs, attn_d, ssm_d):
    return _sigmoid(ga) * attn_d + _sigmoid(gs) * ssm_d


def _f_post1(x, y, g1, lg, lb, sc2, sh2):
    r1 = ALPHA * x + g1 * y
    x1 = _ln(r1) * lg + lb
    h2 = _ln(x1) * (1.0 + sc2) + sh2
    return x1, h2


def _f_loss(x1, mlp, tgt, g2, lg, lb, b2z):
    r2 = ALPHA * x1 + g2 * (mlp + b2z)
    out = _ln(r2) * lg + lb
    err = out - tgt
    return 0.5 * jnp.sum(err * err) * (1.0 / D)


def _rope_tables():
    rows = T // GRID_W
    row = jnp.repeat(jnp.arange(rows), GRID_W)
    col = jnp.tile(jnp.arange(GRID_W), rows)
    n_freq = HD // 4
    freqs = ROPE_BASE ** (-jnp.arange(n_freq, dtype=F32) / n_freq)
    ang_r = row.astype(F32)[:, None] * freqs
    ang_c = col.astype(F32)[:, None] * freqs
    ang = jnp.concatenate([ang_r, ang_r, ang_c, ang_c], -1)
    cos, sin = jnp.cos(ang), jnp.sin(ang)
    lo = (jnp.arange(HD) % (HD // 2)) < (HD // 4)
    sin_a = jnp.where(lo[None, :], -sin, 0.0)
    sin_b = jnp.where(lo[None, :], 0.0, sin)
    return cos, sin_a, sin_b


def _rope(x, cos, sa, sb):
    return x * cos + pltpu.roll(x, 96, 1) * sa + pltpu.roll(x, 32, 1) * sb


def _rope_t(dy, cos, sa, sb):
    return dy * cos + pltpu.roll(dy * sa, 32, 1) + pltpu.roll(dy * sb, 96, 1)


BAND = 3 * WINDOW
KPAD = T + 2 * WINDOW


def _attn_fill_kv(k_ref, v_ref, cos_ref, sa_ref, sb_ref, kp, vp, kc, vc):
    zeros = jnp.zeros((WINDOW, KVW), BF16)
    kp[0:WINDOW, :] = zeros
    kp[WINDOW + T:KPAD, :] = zeros
    vp[0:WINDOW, :] = zeros
    vp[WINDOW + T:KPAD, :] = zeros
    for hh in range(NKV):
        cs = slice(hh * HD, (hh + 1) * HD)
        for r0 in range(0, T, 512):
            rs = slice(r0, r0 + 512)
            kr = _rope(k_ref[rs, cs], cos_ref[rs, :], sa_ref[rs, :], sb_ref[rs, :])
            kp[WINDOW + r0:WINDOW + r0 + 512, cs] = kr.astype(BF16)
    vp[WINDOW:WINDOW + T, :] = v_ref[0:T, :].astype(BF16)
    kc[...] = k_ref[T:TA, :].astype(BF16)
    vc[...] = v_ref[T:TA, :].astype(BF16)


GROWS = GROUP * WINDOW


def _attn_scores(n, kvh, q_ref, cos_ref, sa_ref, sb_ref, sink_ref, kp, kc):
    r0 = pl.multiple_of(n * WINDOW, WINDOW)
    cos = cos_ref[pl.ds(r0, WINDOW), :]
    sa = sa_ref[pl.ds(r0, WINDOW), :]
    sb = sb_ref[pl.ds(r0, WINDOW), :]
    heads = range(kvh * GROUP, (kvh + 1) * GROUP)
    q_g = jnp.concatenate([_rope(q_ref[:, h * HD:(h + 1) * HD], cos, sa, sb).astype(BF16) for h in heads], axis=0)
    kb = kp[pl.ds(r0, BAND), kvh * HD:(kvh + 1) * HD]
    kcb = kc[:, kvh * HD:(kvh + 1) * HD]
    nt = (((1,), (1,)), ((), ()))
    s_loc = lax.dot_general(q_g, kb, nt, preferred_element_type=F32) * ATT_SCALE
    s_ctx = lax.dot_general(q_g, kcb, nt, preferred_element_type=F32) * ATT_SCALE
    row = lax.broadcasted_iota(jnp.int32, (GROWS, BAND), 0) & (WINDOW - 1)
    col = lax.broadcasted_iota(jnp.int32, (GROWS, BAND), 1)
    rel = col - WINDOW - row
    kpos = r0 - WINDOW + col
    valid = (jnp.abs(rel) <= WINDOW) & (kpos >= 0) & (kpos < T)
    s_loc = jnp.where(valid, s_loc, NEG_INF)
    sk = jnp.concatenate([jnp.broadcast_to(sink_ref[0:1, h:h + 1], (WINDOW, 1)) for h in heads], axis=0)
    m = jnp.maximum(jnp.maximum(jnp.max(s_loc, -1, keepdims=True), jnp.max(s_ctx, -1, keepdims=True)), sk)
    e_loc = jnp.exp(s_loc - m)
    e_ctx = jnp.exp(s_ctx - m)
    e_sink = jnp.exp(sk - m)
    inv = 1.0 / (jnp.sum(e_loc, -1, keepdims=True) + jnp.sum(e_ctx, -1, keepdims=True) + e_sink)
    return q_g, r0, e_loc * inv, e_ctx * inv, e_sink * inv


def _attn_fwd(proj, sink, tabs):
    cos, sa, sb = tabs

    def kern(q_ref, k_ref, v_ref, cos_ref, sa_ref, sb_ref, sink_ref, o_ref, kp, vp, kc, vc):
        n = pl.program_id(0)

        @pl.when(n == 0)
        def _():
            _attn_fill_kv(k_ref, v_ref, cos_ref, sa_ref, sb_ref, kp, vp, kc, vc)

        for kvh in range(NKV):
            _, r0, p_loc, p_ctx, _ = _attn_scores(n, kvh, q_ref, cos_ref, sa_ref, sb_ref, sink_ref, kp, kc)
            vb = vp[pl.ds(r0, BAND), kvh * HD:(kvh + 1) * HD]
            vcb = vc[:, kvh * HD:(kvh + 1) * HD]
            o = jnp.dot(p_loc.astype(BF16), vb, preferred_element_type=F32)
            o = o + jnp.dot(p_ctx.astype(BF16), vcb, preferred_element_type=F32)
            for g in range(GROUP):
                h = kvh * GROUP + g
                o_ref[:, h * HD:(h + 1) * HD] = o[g * WINDOW:(g + 1) * WINDOW, :].astype(o_ref.dtype)

    full = lambda shape: pl.BlockSpec(shape, lambda n: (0, 0))
    return pl.pallas_call(
        kern,
        name="attn_fwd",
        grid=(T // WINDOW,),
        in_specs=[
            pl.BlockSpec((WINDOW, QW), lambda n: (n, 0)),
            pl.BlockSpec((TA, KVW), lambda n: (0, QW // KVW)),
            pl.BlockSpec((TA, KVW), lambda n: (0, QW // KVW + 1)),
            full((T, HD)), full((T, HD)), full((T, HD)), full((1, NH)),
        ],
        out_specs=pl.BlockSpec((WINDOW, QW), lambda n: (n, 0)),
        out_shape=jax.ShapeDtypeStruct((T, QW), BF16),
        scratch_shapes=[pltpu.VMEM((KPAD, KVW), BF16), pltpu.VMEM((KPAD, KVW), BF16),
                        pltpu.VMEM((C, KVW), BF16), pltpu.VMEM((C, KVW), BF16)],
        compiler_params=_cparams(("arbitrary",)),
    )(proj, proj, proj, cos, sa, sb, sink)


def _attn_bwd(proj, d_attn, sink, tabs):
    cos, sa, sb = tabs
    n_blocks = T // WINDOW

    def kern(q_ref, k_ref, v_ref, do_ref, cos_ref, sa_ref, sb_ref, sink_ref,
             dq_ref, dk_ref, dv_ref, dsink_ref, kp, vp, kc, vc, dkp, dvp, dkc, dvc):
        n = pl.program_id(0)

        @pl.when(n == 0)
        def _():
            _attn_fill_kv(k_ref, v_ref, cos_ref, sa_ref, sb_ref, kp, vp, kc, vc)
            dkp[...] = jnp.zeros_like(dkp)
            dvp[...] = jnp.zeros_like(dvp)
            dkc[...] = jnp.zeros_like(dkc)
            dvc[...] = jnp.zeros_like(dvc)
            dsink_ref[...] = jnp.zeros_like(dsink_ref)

        nt = (((1,), (1,)), ((), ()))
        tn = (((0,), (0,)), ((), ()))
        for kvh in range(NKV):
            cs = slice(kvh * HD, (kvh + 1) * HD)
            heads = range(kvh * GROUP, (kvh + 1) * GROUP)
            q_g, r0, p_loc, p_ctx, p_sink = _attn_scores(n, kvh, q_ref, cos_ref, sa_ref, sb_ref, sink_ref, kp, kc)
            kb = kp[pl.ds(r0, BAND), cs]
            vb = vp[pl.ds(r0, BAND), cs]
            kcb = kc[:, cs]
            vcb = vc[:, cs]
            do_g = jnp.concatenate([do_ref[:, h * HD:(h + 1) * HD] for h in heads], axis=0)
            dp_loc = lax.dot_general(do_g, vb, nt, preferred_element_type=F32)
            dp_ctx = lax.dot_general(do_g, vcb, nt, preferred_element_type=F32)
            delta = jnp.sum(p_loc * dp_loc, -1, keepdims=True) + jnp.sum(p_ctx * dp_ctx, -1, keepdims=True)
            ds_loc = (p_loc * (dp_loc - delta) * ATT_SCALE).astype(BF16)
            ds_ctx = (p_ctx * (dp_ctx - delta) * ATT_SCALE).astype(BF16)
            dq = jnp.dot(ds_loc, kb, preferred_element_type=F32) + jnp.dot(ds_ctx, kcb, preferred_element_type=F32)
            cos = cos_ref[pl.ds(r0, WINDOW), :]
            sa_ = sa_ref[pl.ds(r0, WINDOW), :]
            sb_ = sb_ref[pl.ds(r0, WINDOW), :]
            dkp[pl.ds(r0, BAND), cs] += lax.dot_general(ds_loc, q_g, tn, preferred_element_type=F32)
            dkc[:, cs] += lax.dot_general(ds_ctx, q_g, tn, preferred_element_type=F32)
            dvp[pl.ds(r0, BAND), cs] += lax.dot_general(p_loc.astype(BF16), do_g, tn, preferred_element_type=F32)
            dvc[:, cs] += lax.dot_general(p_ctx.astype(BF16), do_g, tn, preferred_element_type=F32)
            dsk_rows = p_sink * delta
            for g, h in enumerate(heads):
                rs = slice(g * WINDOW, (g + 1) * WINDOW)
                dq_ref[:, h * HD:(h + 1) * HD] = _rope_t(dq[rs, :], cos, sa_, sb_).astype(dq_ref.dtype)
                dsk = -jnp.sum(dsk_rows[rs, :], axis=0, keepdims=True)
                dsink_ref[h:h + 1, :] += jnp.broadcast_to(dsk, (1, HD))

        @pl.when(n == n_blocks - 1)
        def _():
            for hh in range(NKV):
                cs = slice(hh * HD, (hh + 1) * HD)
                for r0 in range(0, T, 512):
                    rs = slice(r0, r0 + 512)
                    g = dkp[WINDOW + r0:WINDOW + r0 + 512, cs]
                    dk_ref[rs, cs] = _rope_t(g, cos_ref[rs, :], sa_ref[rs, :], sb_ref[rs, :]).astype(dk_ref.dtype)
            dk_ref[T:TA, :] = dkc[...].astype(dk_ref.dtype)
            dv_ref[0:T, :] = dvp[WINDOW:WINDOW + T, :].astype(dv_ref.dtype)
            dv_ref[T:TA, :] = dvc[...].astype(dv_ref.dtype)

    full = lambda shape: pl.BlockSpec(shape, lambda n: (0, 0))
    return pl.pallas_call(
        kern,
        name="attn_bwd",
        grid=(n_blocks,),
        in_specs=[
            pl.BlockSpec((WINDOW, QW), lambda n: (n, 0)),
            pl.BlockSpec((TA, KVW), lambda n: (0, QW // KVW)),
            pl.BlockSpec((TA, KVW), lambda n: (0, QW // KVW + 1)),
            pl.BlockSpec((WINDOW, QW), lambda n: (n, 0)),
            full((T, HD)), full((T, HD)), full((T, HD)), full((1, NH)),
        ],
        out_specs=[pl.BlockSpec((WINDOW, QW), lambda n: (n, 0)), full((TA, KVW)), full((TA, KVW)), full((NH, HD))],
        out_shape=[jax.ShapeDtypeStruct((T, QW), BF16), jax.ShapeDtypeStruct((TA, KVW), BF16),
                   jax.ShapeDtypeStruct((TA, KVW), BF16), jax.ShapeDtypeStruct((NH, HD), F32)],
        scratch_shapes=[pltpu.VMEM((KPAD, KVW), BF16), pltpu.VMEM((KPAD, KVW), BF16),
                        pltpu.VMEM((C, KVW), BF16), pltpu.VMEM((C, KVW), BF16),
                        pltpu.VMEM((KPAD, KVW), F32), pltpu.VMEM((KPAD, KVW), F32),
                        pltpu.VMEM((C, KVW), F32), pltpu.VMEM((C, KVW), F32)],
        compiler_params=_cparams(("arbitrary",)),
    )(proj, proj, proj, d_attn, cos, sa, sb, sink)


def _s5_prep(a_re, a_im, log_dt, b_re, b_im, c_re, c_im):
    lam = lax.complex(a_re, a_im)
    dt = jnp.exp(log_dt)[..., None]
    lam_bar = jnp.exp(lam * dt)
    b_bar = ((lam_bar - 1.0) / lam)[..., None] * lax.complex(b_re, b_im)
    def lam_rows(v):
        return v.reshape(2, NBLK, 1, BW)

    lam_l = jnp.concatenate([lam_rows(jnp.real(lam_bar)), lam_rows(jnp.imag(lam_bar))], -1)
    lam_l = jnp.broadcast_to(lam_l, (2, NBLK, 8, 2 * BW))
    diag = (jnp.arange(UW)[:, None] // SG) == (jnp.arange(BW)[None, :] // SP)

    def blocks(v):
        return jnp.where(diag, jnp.tile(v.reshape(2, NBLK, UW, SP), (1, 1, 1, GBLK)), 0.0)

    b_t = jnp.swapaxes(b_bar, -1, -2)
    bmat = jnp.concatenate([blocks(jnp.real(b_t)), blocks(jnp.imag(b_t))], -1)
    cmat = jnp.concatenate([blocks(c_re), -blocks(c_im)], -1)
    return lam_l, bmat, cmat


def _cmul(ar, ai, br, bi):
    return ar * br - ai * bi, ar * bi + ai * br


def _shift_rows(x, rev, fill):
    r = lax.broadcasted_iota(jnp.int32, x.shape, 0)
    down = jnp.where(r == 0, fill, pltpu.roll(x, 1, 0))
    up = jnp.where(r == NSEG - 1, fill, pltpu.roll(x, NSEG - 1, 0))
    return jnp.where(rev == 0, down, up)


def _edge_row(x, rev):
    last = jnp.broadcast_to(x[NSEG - 1:NSEG, :], x.shape)
    first = jnp.broadcast_to(x[0:1, :], x.shape)
    return jnp.where(rev == 0, last, first)


def _seg_scan(get, put, base, seglen, lr, li, rev, cin, acc_fn=None, acc0=()):
    zero = jnp.zeros((NSEG, BW), F32)

    def rows(k):
        j = jnp.where(rev == 0, k, seglen - 1 - k)
        return pl.ds(pl.multiple_of(base + j * NSEG, NSEG), NSEG)

    def local(k, carry):
        sr, si = carry
        xr, xi = get(rows(k))
        tr, ti = _cmul(lr, li, sr, si)
        sr, si = tr + xr, ti + xi
        put(rows(k), sr, si)
        return sr, si

    er, ei = lax.fori_loop(0, seglen, local, (zero, zero))
    lpr, lpi = lr, li
    assert seglen & (seglen - 1) == 0, seglen
    for _ in range(seglen.bit_length() - 1):
        lpr, lpi = _cmul(lpr, lpi, lpr, lpi)
    cr, ci = _shift_rows(zero, rev, cin[0]), _shift_rows(zero, rev, cin[1])
    for _ in range(NSEG - 1):
        tr, ti = _cmul(lpr, lpi, cr, ci)
        cr, ci = _shift_rows(er + tr, rev, cin[0]), _shift_rows(ei + ti, rev, cin[1])

    def fix(k, carry):
        tr, ti = _cmul(lr, li, carry[0], carry[1])
        xr, xi = get(rows(k))
        fr, fi = xr + tr, xi + ti
        put(rows(k), fr, fi)
        if acc_fn is None:
            return tr, ti
        j = jnp.where(rev == 0, k, seglen - 1 - k)
        return (tr, ti) + tuple(acc_fn(j, fr, fi, carry[2:]))

    out = lax.fori_loop(0, seglen, fix, (cr, ci) + tuple(acc0))
    tr, ti = out[0], out[1]
    leaving = (_edge_row(er + tr, rev), _edge_row(ei + ti, rev))
    return leaving if acc_fn is None else (leaving, out[2:])


RCH = 256
CSEG = C // NSEG
TSEG = T // NSEG
UCOL0 = (QW + 2 * KVW) // UW


REGIONS = ((0, TSEG), (T, CSEG))


def _state_access(ref, lead=()):
    def get(rows):
        return ref[(*lead, rows, slice(0, BW))], ref[(*lead, rows, slice(BW, 2 * BW))]

    def put(rows, re, im):
        ref[(*lead, rows, slice(0, BW))] = re
        ref[(*lead, rows, slice(BW, 2 * BW))] = im

    return get, put


def _interleave_rows(src_ref, dst_ref, regions=REGIONS):
    for base, seglen in regions:
        def body(j, carry, base=base, seglen=seglen):
            dst_ref[pl.ds(pl.multiple_of(base + j * NSEG, NSEG), NSEG), :] = src_ref[pl.ds(base + j, NSEG, stride=seglen), :]
            return carry

        lax.fori_loop(0, seglen, body, 0, unroll=8)


def _deinterleave_rows(src_ref, dst_ref, regions=REGIONS):
    for base, seglen in regions:
        def body(j, carry, base=base, seglen=seglen):
            dst_ref[pl.ds(base + j, NSEG, stride=seglen), :] = src_ref[pl.ds(pl.multiple_of(base + j * NSEG, NSEG), NSEG), :]
            return carry

        lax.fori_loop(0, seglen, body, 0, unroll=8)


def _s5_fwd(proj, dskip, lam, bmat, cmat):
    def kern(u_ref, dk_ref, lam_ref, b_ref, c_ref, s_ref, ssm_ref, ge_ref, up_ref, yp_ref):
        d = pl.program_id(1)

        @pl.when(d == 0)
        def _():
            _interleave_rows(u_ref, up_ref)

        bm = b_ref[0, 0].astype(BF16)
        for r0 in range(0, TA, RCH):
            s_ref[0, 0, r0:r0 + RCH, :] = jnp.dot(up_ref[r0:r0 + RCH, :].astype(BF16), bm, preferred_element_type=F32)
        lr = lam_ref[0, 0, :, 0:BW]
        li = lam_ref[0, 0, :, BW:2 * BW]
        zero = jnp.zeros((NSEG, BW), F32)
        get, put = _state_access(s_ref, (0, 0))
        mid = _seg_scan(get, put, T, CSEG, lr, li, d, (zero, zero))
        _seg_scan(get, put, 0, TSEG, lr, li, d, mid)
        cm = c_ref[0, 0].astype(BF16)
        for r0 in range(0, T, RCH):
            y = lax.dot_general(s_ref[0, 0, r0:r0 + RCH, :].astype(BF16), cm, (((1,), (1,)), ((), ())), preferred_element_type=F32)

            @pl.when(d == 0)
            def _(y=y, r0=r0):
                yp_ref[r0:r0 + RCH, :] = y + dk_ref[...] * up_ref[r0:r0 + RCH, :]

            @pl.when(d == 1)
            def _(y=y, r0=r0):
                yp_ref[r0:r0 + RCH, :] += y

        @pl.when(d == 1)
        def _():
            _deinterleave_rows(yp_ref, ssm_ref, REGIONS[:1])
            for r0 in range(0, T, RCH):
                ge_ref[r0:r0 + RCH, :] = _gelu(ssm_ref[r0:r0 + RCH, :]).astype(ge_ref.dtype)

    blk4 = lambda shape: pl.BlockSpec((1, 1) + shape, lambda b, d: (d, b, 0, 0))
    return pl.pallas_call(
        kern,
        name="s5_fwd",
        grid=(NBLK, 2),
        in_specs=[pl.BlockSpec((TA, UW), lambda b, d: (0, UCOL0 + b)), pl.BlockSpec((1, UW), lambda b, d: (0, b)),
                  blk4((8, 2 * BW)), blk4((UW, 2 * BW)), blk4((UW, 2 * BW))],
        out_specs=[blk4((TA, 2 * BW)), pl.BlockSpec((T, UW), lambda b, d: (0, b)), pl.BlockSpec((T, UW), lambda b, d: (0, b))],
        out_shape=[jax.ShapeDtypeStruct((2, NBLK, TA, 2 * BW), F32), jax.ShapeDtypeStruct((T, SW), F32),
                   jax.ShapeDtypeStruct((T, SW), BF16)],
        scratch_shapes=[pltpu.VMEM((TA, UW), F32), pltpu.VMEM((T, UW), F32)],
        compiler_params=_cparams(("parallel", "arbitrary")),
    )(proj, dskip, lam, bmat, cmat)


def _s5_bwd(d_ge, ssm, proj, dskip, states, lam, bmat, cmat):
    nt = (((1,), (1,)), ((), ()))
    tn = (((0,), (0,)), ((), ()))

    def kern(dge_ref, ssm_ref, u_ref, dk_ref, s_ref, lam_ref, b_ref, c_ref,
             du_ref, ddk_ref, dlam_ref, db_ref, dc_ref, g_ref, dua_ref, dssm_ref, up_ref, nat_ref):
        d = pl.program_id(1)

        @pl.when(d == 0)
        def _():
            ddk = jnp.zeros((1, UW), F32)
            for r0 in range(0, T, RCH):
                rs = slice(r0, r0 + RCH)
                _, pull = jax.vjp(_gelu, ssm_ref[rs, :])
                dssm = pull(dge_ref[rs, :])[0]
                nat_ref[rs, :] = dssm
                ddk = ddk + jnp.sum(dssm * u_ref[rs, :], axis=0, keepdims=True)
            ddk_ref[...] = ddk
            _interleave_rows(nat_ref, dssm_ref, REGIONS[:1])
            _interleave_rows(u_ref, up_ref)
            for r0 in range(0, T, RCH):
                dua_ref[r0:r0 + RCH, :] = dssm_ref[r0:r0 + RCH, :] * dk_ref[...]
            dua_ref[T:TA, :] = jnp.zeros((C, UW), F32)

        cm = c_ref[0, 0].astype(BF16)
        for r0 in range(0, T, RCH):
            g_ref[r0:r0 + RCH, :] = jnp.dot(dssm_ref[r0:r0 + RCH, :].astype(BF16), cm, preferred_element_type=F32)
        g_ref[T:TA, :] = jnp.zeros((C, 2 * BW), F32)
        lr = lam_ref[0, 0, :, 0:BW]
        li = lam_ref[0, 0, :, BW:2 * BW]
        zero = jnp.zeros((NSEG, BW), F32)
        get_g, put_g = _state_access(g_ref)

        get_s, _ = _state_access(s_ref, (0, 0))

        def dlam_fold(base, seglen, s_in):
            def rows(j):
                return pl.ds(pl.multiple_of(base + j * NSEG, NSEG), NSEG)

            jb = jnp.where(d == 0, 0, seglen - 1)
            jn = jnp.where(d == 0, seglen - 1, 0)
            sp = get_s(rows(jn))
            edge = (_shift_rows(sp[0], d, s_in[0]), _shift_rows(sp[1], d, s_in[1]))

            def fold(j, gr, gi, acc):
                jp = jnp.clip(jnp.where(d == 0, j - 1, j + 1), 0, seglen - 1)
                sr, si = get_s(rows(jp))
                sr = jnp.where(j == jb, edge[0], sr)
                si = jnp.where(j == jb, edge[1], si)
                return acc[0] + (gr * sr + gi * si), acc[1] + (gi * sr - gr * si)

            return fold

        r_mid = jnp.where(d == 0, TA - 1, T)
        s_mid = tuple(jnp.broadcast_to(t, (NSEG, BW)) for t in get_s(pl.ds(r_mid, 1)))
        mid, acc = _seg_scan(get_g, put_g, 0, TSEG, lr, -li, 1 - d, (zero, zero), dlam_fold(0, TSEG, s_mid), (zero, zero))
        _, acc = _seg_scan(get_g, put_g, T, CSEG, lr, -li, 1 - d, mid, dlam_fold(T, CSEG, (zero, zero)), acc)
        dlam_ref[0, 0, :, 0:BW] = acc[0]
        dlam_ref[0, 0, :, BW:2 * BW] = acc[1]

        bm = b_ref[0, 0].astype(BF16)
        db = jnp.zeros((UW, 2 * BW), F32)
        dc = jnp.zeros((UW, 2 * BW), F32)
        for r0 in range(0, TA, RCH):
            rs = slice(r0, r0 + RCH)
            g = g_ref[rs, :].astype(BF16)
            dua_ref[rs, :] += lax.dot_general(g, bm, nt, preferred_element_type=F32)
            db = db + lax.dot_general(up_ref[rs, :].astype(BF16), g, tn, preferred_element_type=F32)
            if r0 < T:
                dc = dc + lax.dot_general(dssm_ref[rs, :].astype(BF16), s_ref[0, 0, rs, :].astype(BF16), tn,
                                          preferred_element_type=F32)
        db_ref[0, 0] = db
        dc_ref[0, 0] = dc

        @pl.when(d == 1)
        def _():
            _deinterleave_rows(dua_ref, nat_ref)
            du_ref[...] = nat_ref[...].astype(du_ref.dtype)

    blk4 = lambda shape: pl.BlockSpec((1, 1) + shape, lambda b, d: (d, b, 0, 0))
    lat = pl.BlockSpec((T, UW), lambda b, d: (0, b))
    vec = pl.BlockSpec((1, UW), lambda b, d: (0, b))
    return pl.pallas_call(
        kern,
        name="s5_bwd",
        grid=(NBLK, 2),
        in_specs=[lat, lat, pl.BlockSpec((TA, UW), lambda b, d: (0, UCOL0 + b)), vec,
                  blk4((TA, 2 * BW)), blk4((8, 2 * BW)), blk4((UW, 2 * BW)), blk4((UW, 2 * BW))],
        out_specs=[pl.BlockSpec((TA, UW), lambda b, d: (0, b)), vec, blk4((8, 2 * BW)), blk4((UW, 2 * BW)), blk4((UW, 2 * BW))],
        out_shape=[jax.ShapeDtypeStruct((TA, SW), BF16), jax.ShapeDtypeStruct((1, SW), F32),
                   jax.ShapeDtypeStruct((2, NBLK, 8, 2 * BW), F32),
                   jax.ShapeDtypeStruct((2, NBLK, UW, 2 * BW), F32), jax.ShapeDtypeStruct((2, NBLK, UW, 2 * BW), F32)],
        scratch_shapes=[pltpu.VMEM((TA, 2 * BW), F32), pltpu.VMEM((TA, UW), F32), pltpu.VMEM((T, UW), F32),
                        pltpu.VMEM((TA, UW), F32), pltpu.VMEM((TA, UW), F32)],
        compiler_params=_cparams(("parallel", "arbitrary")),
    )(d_ge, ssm, proj, dskip, states, lam, bmat, cmat)


TR = 256
TN_WIDE = 1024


def _vjp_rows(f, primals, cots, n_row):
    _, pull = jax.vjp(f, *primals)
    g = pull(cots)
    return list(g[:n_row]), list(g[n_row:])


class _GradDict(dict):
    def __init__(self, on_set=None):
        super().__init__()
        self._on_set = on_set
        self.tokens = {}

    def __setitem__(self, key, value):
        super().__setitem__(key, value)
        if self._on_set is not None:
            self._on_set(self)

    def order(self, key):
        return self.tokens.get(key, self.get(key))

    def finish(self, key, after):
        if self.on_finish is None:
            return ()
        return (self.on_finish(key, after),)

    on_finish = None


def _local_step(x, ctx, tgt, mod_lat, mod_ctx, wb, sp, on_grad=None, on_loss=None, on_finish=None, on_early=None):
    sh1, sc1, g1, sh2, sc2, g2 = [mod_lat[:, i * D:(i + 1) * D] for i in range(6)]
    csh1, csc1 = mod_ctx[:, 0:D], mod_ctx[:, D:2 * D]
    tabs = _rope_tables()
    sink = sp["attn_sink"].reshape(1, NH)
    dskip = sp["ssm_d"].reshape(1, SW)
    lg_mix, lb_mix = sp["ln_mix_g"].reshape(1, D), sp["ln_mix_b"].reshape(1, D)
    lg_mlp, lb_mlp = sp["ln_mlp_g"].reshape(1, D), sp["ln_mlp_b"].reshape(1, D)
    b1, b2 = sp["b_mlp1"].reshape(1, DFF), sp["b_mlp2"].reshape(1, D)
    s5_names = ("ssm_a_re", "ssm_a_im", "ssm_log_dt", "ssm_b_re", "ssm_b_im", "ssm_c_re", "ssm_c_im")
    (lam, bmat, cmat), s5_pull = jax.vjp(_s5_prep, *[sp[n] for n in s5_names])

    def ln_mod2(rv, vv):
        h = _f_ln_mod(rv[0], vv[0], vv[1])
        return [h, h], []

    h_lat, h_lat_t = _rowwise(ln_mod2, [(x, D, 0, 0)], [sc1, sh1], [(D, BF16), (D, BF16, True)], [], nrows=T, tr=TR, name="ln1_lat")
    h_ctx, h_ctx_t = _rowwise(ln_mod2, [(ctx, D, 0, 0)], [csc1, csh1], [(D, BF16), (D, BF16, True)], [], nrows=C, tr=TR,
                              name="ln1_ctx")
    h1 = jnp.concatenate([h_lat, h_ctx], 0)
    h1_t = jnp.concatenate([h_lat_t, h_ctx_t], 1)
    proj = _matmul(h1, wb["w_in"], mode="nn", name="proj", tm=768, tn=TN_WIDE)
    attn = _attn_fwd(proj, sink, tabs)
    states, ssm, ge = _s5_fwd(proj, dskip, lam, bmat, cmat)
    z = _matmul(ge, wb["w_glu"], mode="nn", name="glu_mm", tm=1024, tn=1024)

    def glu_act(rv, vv):
        return [_f_glu(rv[0])], []

    glu, = _rowwise(glu_act, [(z, 2 * SW, 0, 0)], [], [(SW, BF16)], [], nrows=T, tr=TR, name="glu_act")
    attn_d = _matmul(attn, wb["w_attn_up"], mode="nn", name="attn_up", tm=1024, tn=512)
    ssm_d = _matmul(glu, wb["w_ssm_up"], mode="nn", name="ssm_up", tm=1024, tn=512)
    ga_cb, gs_cb = (QW + 2 * KVW + SW) // D, (QW + 2 * KVW + SW) // D + 1

    def mix(rv, vv):
        m_ = _f_mix(*rv)
        return [m_, m_], []

    mixv, mix_t = _rowwise(mix, [(proj, D, ga_cb, 0), (proj, D, gs_cb, 0), (attn_d, D, 0, 0), (ssm_d, D, 0, 0)], [],
                           [(D, BF16), (D, BF16, True)], [], nrows=T, tr=TR, name="mix")
    y = _matmul(mixv, wb["w_out"], mode="nn", name="out_proj", tm=1024, tn=TN_WIDE)

    def post1(rv, vv):
        x1, h2 = _f_post1(rv[0], rv[1], *vv)
        return [x1, h2, h2], []

    x1, h2, h2_t = _rowwise(post1, [(x, D, 0, 0), (y, D, 0, 0)], [g1, lg_mix, lb_mix, sc2, sh2],
                            [(D, F32), (D, BF16), (D, BF16, True)], [], nrows=T, tr=TR, name="post1")

    def relu_sq(acc):
        r = jnp.maximum(acc, 0.0)
        return r, r * r, r * r

    r_act, act, act_t = _matmul(h2, wb["w_mlp1"], mode="nn", name="mlp1", tm=1024, tn=TN_WIDE, bias=b1,
                                out_dtypes=(BF16, BF16, BF16), out_t=(False, False, True), epilogue=relu_sq)
    mlp = _matmul(act, wb["w_mlp2"], mode="nn", name="mlp2", tm=1024, tn=512, tk=DFF // 2)

    def loss_fb(rv, vv):
        x1_t, mlp_t, tgt_t = rv
        g2_v, lg_v, lb_v, b2_v = vv
        f = lambda a, m, g, p, q, b: _f_loss(a, m, tgt_t, g, p, q, b)
        val, grads = jax.value_and_grad(f, argnums=(0, 1, 2, 3, 4, 5))(x1_t, mlp_t, g2_v, lg_v, lb_v, b2_v)
        dx1, dmlp, dg2, dlg, dlb, db2 = grads
        return [dx1, dmlp], [jnp.reshape(val, (1, 1)), dg2, dlg, dlb, db2]

    dx1_a, d_mlp, loss_p, d_g2, d_lg_mlp, d_lb_mlp, d_b2 = _rowwise(
        loss_fb, [(x1, D, 0, 0), (mlp, D, 0, 0), (tgt, D, 0, 0)], [g2, lg_mlp, lb_mlp, b2],
        [(D, F32), (D, BF16)], [(1, 1), (1, D), (1, D), (1, D), (1, D)], nrows=T, tr=TR, name="loss_fb")

    gw = _GradDict(on_grad)
    gw.on_finish = on_finish
    loss_done = () if on_loss is None else (on_loss(loss_p),)
    gw["w_mlp2"] = _matmul(act_t, d_mlp, mode="nn", name="dw_mlp2", out_dtypes=(BF16,), tm=1024, tn=TN_WIDE, after=loss_done)
    da, = (_matmul(d_mlp, wb["w_mlp2"], mode="nt", name="d_act", out_dtypes=(BF16,), tm=1024, tn=TN_WIDE,
                   extras=(r_act,), epilogue=lambda acc, r: (acc * (2.0 * r.astype(F32)),), after=(gw.order("w_mlp2"),)),)
    pin = gw.finish("w_mlp2", da)
    ones = jnp.ones((8, T), BF16)
    d_b1 = _matmul(ones, da, mode="nn", name="db_mlp1", tm=8, tn=2048)[0:1]
    gw["w_mlp1"] = _matmul(h2_t, da, mode="nn", name="dw_mlp1", out_dtypes=(BF16,), tm=1024, tn=TN_WIDE, after=pin)
    dh2 = _matmul(da, wb["w_mlp1"], mode="nt", name="d_h2", tm=1024, tn=512, tk=DFF // 2, after=(gw.order("w_mlp1"),))

    def post1_b(rv, vv):
        x_t, y_t, dx1_t, dh2_t = rv
        gr, gv = _vjp_rows(_f_post1, (x_t, y_t, *vv), (dx1_t, dh2_t), 2)
        return [gr[0], gr[1]], gv

    dx_a, dy, d_g1, d_lg_mix, d_lb_mix, d_sc2, d_sh2 = _rowwise(
        post1_b, [(x, D, 0, 0), (y, D, 0, 0), (dx1_a, D, 0, 0), (dh2, D, 0, 0)], [g1, lg_mix, lb_mix, sc2, sh2],
        [(D, F32), (D, BF16)], [(1, D)] * 5, nrows=T, tr=TR, name="post1_bwd")
    gw["w_out"] = _matmul(mix_t, dy, mode="nn", name="dw_out", out_dtypes=(BF16,), tm=1024, tn=TN_WIDE)
    dmix = _matmul(dy, wb["w_out"], mode="nt", name="d_mix", tm=1024, tn=TN_WIDE, after=(gw.order("w_out"),))

    def mix_b(rv, vv):
        gr, _ = _vjp_rows(_f_mix, tuple(rv[:4]), rv[4], 4)
        return gr, []

    d_ga, d_gs, d_attn_d, d_ssm_d = _rowwise(
        mix_b, [(proj, D, ga_cb, 0), (proj, D, gs_cb, 0), (attn_d, D, 0, 0), (ssm_d, D, 0, 0), (dmix, D, 0, 0)], [],
        [(D, BF16)] * 4, [], nrows=T, tr=TR, name="mix_bwd")
    pin = gw.finish("w_mlp1", d_ga)
    gw["w_attn_up"] = _matmul(attn, d_attn_d, mode="tn", name="dw_attn_up", out_dtypes=(BF16,), tm=512, tn=1024, tk=1024, after=pin)
    d_attn = _matmul(d_attn_d, wb["w_attn_up"], mode="nt", name="d_attn", out_dtypes=(BF16,), tm=1024, tn=512)
    gw["w_ssm_up"] = _matmul(glu, d_ssm_d, mode="tn", name="dw_ssm_up", out_dtypes=(BF16,), tm=512, tn=1024, tk=1024)
    d_glu = _matmul(d_ssm_d, wb["w_ssm_up"], mode="nt", name="d_glu", tm=1024, tn=512, after=(gw.order("w_attn_up"), gw.order("w_ssm_up")))

    def glu_b(rv, vv):
        gr, _ = _vjp_rows(_f_glu, (rv[0],), rv[1], 1)
        return gr, []

    dz, = _rowwise(glu_b, [(z, 2 * SW, 0, 0), (d_glu, SW, 0, 0)], [], [(2 * SW, BF16)], [], nrows=T, tr=TR, name="glu_bwd")
    gw["w_glu"] = _matmul(ge, dz, mode="tn", name="dw_glu", out_dtypes=(BF16,), tm=512, tn=1024, tk=1024)
    d_ge = _matmul(dz, wb["w_glu"], mode="nt", name="d_ge", tm=1024, tn=512, after=(gw.order("w_glu"),))

    du_all, d_dskip, dlam, dbmat, dcmat = _s5_bwd(d_ge, ssm, proj, dskip, states, lam, bmat, cmat)
    s5_grads = s5_pull((dlam, dbmat, dcmat))
    early = dict(zip(s5_names, s5_grads), ssm_d=d_dskip)
    if on_early is not None:
        on_early(early)
    pin = gw.finish("w_glu", du_all)

    dq, dk, dv, dsink = _attn_bwd(proj, d_attn, sink, tabs)
    zc = lambda w: jnp.zeros((C, w), BF16)
    dproj = jnp.concatenate([
        jnp.concatenate([dq, zc(QW)], 0), dk, dv, du_all,
        jnp.concatenate([d_ga, zc(D)], 0), jnp.concatenate([d_gs, zc(D)], 0)], 1)
    gw["w_in"] = _matmul(h1_t, dproj, mode="nn", name="dw_in", out_dtypes=(BF16,), tm=1024, tn=TN_WIDE, after=pin)
    pin = gw.finish("w_in", gw["w_in"])
    dh1 = _matmul(dproj, wb["w_in"], mode="nt", name="d_h1", tm=768, tn=512, after=pin)

    def ln1_b(rv, vv):
        x_t, dh_t, dxa_t = rv
        gr, gv = _vjp_rows(_f_ln_mod, (x_t, vv[0], vv[1]), dh_t, 1)
        return [gr[0] + dxa_t], gv

    grad_x, d_sc1, d_sh1 = _rowwise(ln1_b, [(x, D, 0, 0), (dh1, D, 0, 0), (dx_a, D, 0, 0)], [sc1, sh1],
                                    [(D, F32)], [(1, D), (1, D)], nrows=T, tr=TR, name="ln1_lat_bwd")

    def ln1c_b(rv, vv):
        _, gv = _vjp_rows(_f_ln_mod, (rv[0], vv[0], vv[1]), rv[1], 1)
        return [], gv

    d_csc1, d_csh1 = _rowwise(ln1c_b, [(ctx, D, 0, 0), (dh1, D, 0, T // TR)], [csc1, csh1],
                              [], [(1, D), (1, D)], nrows=C, tr=TR, name="ln1_ctx_bwd")

    d_mod_lat = jnp.concatenate([d_sh1, d_sc1, d_g1, d_sh2, d_sc2, d_g2], 1)
    zv = jnp.zeros((1, D), F32)
    d_mod_ctx = jnp.concatenate([d_csh1, d_csc1, zv, zv, zv, zv], 1)
    gs = {n: g for n, g in zip(s5_names, s5_grads)}
    gs["attn_sink"] = dsink[:, 0]
    gs["ssm_d"] = d_dskip
    gs["ln_mix_g"], gs["ln_mix_b"] = d_lg_mix, d_lb_mix
    gs["ln_mlp_g"], gs["ln_mlp_b"] = d_lg_mlp, d_lb_mlp
    gs["b_mlp1"], gs["b_mlp2"] = d_b1, d_b2
    return loss_p, grad_x, d_mod_lat, d_mod_ctx, gw, gs


def _my_pos():
    return lax.axis_index("x"), lax.axis_index("y"), lax.axis_index("c")


def _flip(p, bit):
    return 1 - p if bit else p


def _peer(pos, k):
    x, y, c = pos
    return (_flip(x, (k >> 2) & 1), _flip(y, (k >> 1) & 1), _flip(c, k & 1))


def _lin(pos):
    return 4 * pos[0] + 2 * pos[1] + pos[2]


def _allgather_small(v, name):
    r, w = v.shape

    def body(v_ref, out_ref, send_sems, recv_sems, local_sem):
        me = _my_pos()
        mine = pltpu.make_async_copy(v_ref, out_ref.at[_lin(me)], local_sem)
        mine.start()
        sends = []
        for k in range(1, N_DEV):
            cp = pltpu.make_async_remote_copy(src_ref=v_ref, dst_ref=out_ref.at[_lin(me)], send_sem=send_sems.at[k - 1],
                                              recv_sem=recv_sems.at[k - 1], device_id=_peer(me, k), device_id_type=MESH)
            cp.start()
            sends.append(cp)
        for k in range(1, N_DEV):
            peer = _peer(me, k)
            pltpu.make_async_remote_copy(src_ref=v_ref, dst_ref=out_ref.at[_lin(peer)], send_sem=send_sems.at[k - 1],
                                         recv_sem=recv_sems.at[k - 1], device_id=peer, device_id_type=MESH).wait_recv()
        for cp in sends:
            cp.wait_send()
        mine.wait()

    return pl.pallas_call(
        body,
        name=name,
        out_shape=jax.ShapeDtypeStruct((N_DEV, r, w), v.dtype),
        in_specs=[pl.BlockSpec(memory_space=pltpu.VMEM)],
        out_specs=pl.BlockSpec(memory_space=pltpu.VMEM),
        scratch_shapes=[pltpu.SemaphoreType.DMA((N_DEV - 1,)), pltpu.SemaphoreType.DMA((N_DEV - 1,)), pltpu.SemaphoreType.DMA],
        compiler_params=pltpu.CompilerParams(vmem_limit_bytes=VMEM_LIMIT_BYTES),
    )(v)


def _block_of(ref, kind, idx, n):
    start = pl.multiple_of(idx * n, 128)
    if kind == "col":
        return ref.at[:, pl.ds(start, n)]
    return ref.at[pl.ds(start, n), :]


def _handshake(peers):
    barrier = pltpu.get_barrier_semaphore()
    for peer in peers:
        pl.semaphore_signal(barrier, inc=1, device_id=peer, device_id_type=MESH)
    pl.semaphore_wait(barrier, len(peers))


def _allgather_weights_seq(shards, kinds, name, collective_id):
    nt = len(shards)
    hbm = pltpu.MemorySpace.HBM
    ins = [jax.new_ref(s, memory_space=hbm) for s in shards]
    outs = []
    for s, kind in zip(shards, kinds):
        k, n = s.shape
        shape = (k, n * N_DEV) if kind == "col" else (k * N_DEV, n)
        outs.append(jax.empty_ref(jax.ShapeDtypeStruct(shape, s.dtype), memory_space=hbm))

    @functools.partial(
        pl.kernel, mesh=plsc.ScalarSubcoreMesh(axis_name="seq", num_cores=1), name=name,
        scratch_types=(pltpu.SemaphoreType.DMA((nt, N_DEV - 1)), pltpu.SemaphoreType.DMA((nt, N_DEV - 1)),
                       pltpu.SemaphoreType.DMA((nt,))),
        compiler_params=pltpu.CompilerParams(collective_id=collective_id))
    def launch(send_sems, recv_sems, local_sems):
        x, y, c = _my_pos()
        me, sibling = (x, y, c), (x, y, 1 - c)
        chips = [(1 - x, y), (x, 1 - y), (1 - x, 1 - y)]
        _handshake([sibling] + [(*chip, c) for chip in chips])

        def blk(t, pos):
            n = shards[t].shape[1] if kinds[t] == "col" else shards[t].shape[0]
            return _block_of(outs[t], kinds[t], _lin(pos), n)

        def copy(t, k, block, to, src=None):
            return pltpu.make_async_remote_copy(src_ref=blk(t, block) if src is None else src, dst_ref=blk(t, block),
                                                send_sem=send_sems.at[t, k], recv_sem=recv_sems.at[t, k],
                                                device_id=to, device_id_type=MESH)

        local, sends = [], []
        for t in range(nt):
            mine = pltpu.make_async_copy(ins[t], blk(t, me), local_sems.at[t])
            mine.start()
            local.append(mine)
            first = [copy(t, 0, me, sibling, src=ins[t])]
            first += [copy(t, 1 + j, me, (*chip, c), src=ins[t]) for j, chip in enumerate(chips)]
            for cp in first:
                cp.start()
            sends += first
        for t in range(nt):
            for j, chip in enumerate(chips):
                copy(t, 1 + j, (*chip, c), me).wait_recv()
                fwd = copy(t, 4 + j, (*chip, c), sibling)
                fwd.start()
                sends.append(fwd)
        for t in range(nt):
            copy(t, 0, sibling, me).wait_recv()
            for j, chip in enumerate(chips):
                copy(t, 4 + j, (*chip, 1 - c), me).wait_recv()
        for cp in sends:
            cp.wait_send()
        for cp in local:
            cp.wait()

    launch()
    return [o[...] for o in outs]


def _allgather_small_seq(v, name, collective_id):
    hbm = pltpu.MemorySpace.HBM
    src = jax.new_ref(v, memory_space=hbm)
    out = jax.empty_ref(jax.ShapeDtypeStruct((N_DEV,) + v.shape, v.dtype), memory_space=hbm)

    @functools.partial(
        pl.kernel, mesh=plsc.ScalarSubcoreMesh(axis_name="seq", num_cores=1), name=name,
        scratch_types=(pltpu.SemaphoreType.DMA((N_DEV - 1,)), pltpu.SemaphoreType.DMA((N_DEV - 1,)), pltpu.SemaphoreType.DMA),
        compiler_params=pltpu.CompilerParams(collective_id=collective_id))
    def launch(send_sems, recv_sems, local_sem):
        me = _my_pos()
        _handshake([_peer(me, k) for k in range(1, N_DEV)])
        mine = pltpu.make_async_copy(src, out.at[_lin(me)], local_sem)
        mine.start()
        sends = []
        for k in range(1, N_DEV):
            cp = pltpu.make_async_remote_copy(src_ref=src, dst_ref=out.at[_lin(me)], send_sem=send_sems.at[k - 1],
                                              recv_sem=recv_sems.at[k - 1], device_id=_peer(me, k), device_id_type=MESH)
            cp.start()
            sends.append(cp)
        for k in range(1, N_DEV):
            peer = _peer(me, k)
            pltpu.make_async_remote_copy(src_ref=src, dst_ref=out.at[_lin(peer)], send_sem=send_sems.at[k - 1],
                                         recv_sem=recv_sems.at[k - 1], device_id=peer, device_id_type=MESH).wait_recv()
        for cp in sends:
            cp.wait_send()
        mine.wait()

    launch()
    return out[...]


N_CHIP = N_DEV // 2


def _chip_of(pos):
    return 2 * pos[0] + pos[1]


def _pair_exchange_seq(grads, kinds, name, collective_id):
    nt = len(grads)
    hbm = pltpu.MemorySpace.HBM
    shard_shapes = _shard_shapes(grads, kinds)
    ins = [jax.new_ref(g, memory_space=hbm) for g in grads]
    outs = [jax.empty_ref(jax.ShapeDtypeStruct((N_CHIP,) + s, g.dtype), memory_space=hbm) for s, g in zip(shard_shapes, grads)]

    @functools.partial(
        pl.kernel, mesh=plsc.ScalarSubcoreMesh(axis_name="seq", num_cores=1), name=name,
        scratch_types=(pltpu.SemaphoreType.DMA((nt, N_CHIP)), pltpu.SemaphoreType.DMA((nt, N_CHIP))),
        compiler_params=pltpu.CompilerParams(collective_id=collective_id))
    def launch(send_sems, recv_sems):
        x, y, c = _my_pos()
        sibling = (x, y, 1 - c)
        _handshake([sibling])
        copies = []
        for t in range(nt):
            n = shard_shapes[t][1] if kinds[t] == "col" else shard_shapes[t][0]
            for q in range(N_CHIP):
                cp = pltpu.make_async_remote_copy(src_ref=_block_of(ins[t], kinds[t], 2 * q + (1 - c), n), dst_ref=outs[t].at[q],
                                                  send_sem=send_sems.at[t, q], recv_sem=recv_sems.at[t, q],
                                                  device_id=sibling, device_id_type=MESH)
                cp.start()
                copies.append(cp)
        for cp in copies:
            cp.wait_recv()
        for cp in copies:
            cp.wait_send()

    launch()
    return [o[...] for o in outs]


def _pair_add(g, half, kind, name, after=()):
    nq, k, ns = half.shape
    tr = min(k, 512)
    c_idx = lax.axis_index("c").astype(jnp.int32).reshape(1)
    if kind == "col":
        g_spec = pl.BlockSpec((tr, ns), lambda q, i, c_ref: (i, 2 * q + c_ref[0]))
    else:
        g_spec = pl.BlockSpec((tr, ns), lambda q, i, c_ref: ((2 * q + c_ref[0]) * (k // tr) + i, 0))
    n_after = len(after)

    def kern(c_ref, g_ref, h_ref, *rest):
        o_ref = rest[n_after]
        o_ref[0] = (g_ref[...].astype(F32) + h_ref[0].astype(F32)).astype(o_ref.dtype)

    return pl.pallas_call(
        kern,
        name=name,
        grid_spec=pltpu.PrefetchScalarGridSpec(
            num_scalar_prefetch=1,
            grid=(nq, k // tr),
            in_specs=[g_spec, pl.BlockSpec((1, tr, ns), lambda q, i, c_ref: (q, i, 0))] + [pl.BlockSpec(memory_space=pl.ANY)] * n_after,
            out_specs=pl.BlockSpec((1, tr, ns), lambda q, i, c_ref: (q, i, 0)),
        ),
        out_shape=jax.ShapeDtypeStruct(half.shape, half.dtype),
        compiler_params=_cparams(("parallel", "parallel")),
    )(c_idx, g, half, *after)


def _chip_exchange_seq(psums, name, collective_id):
    nt = len(psums)
    hbm = pltpu.MemorySpace.HBM
    ins = [jax.new_ref(s, memory_space=hbm) for s in psums]
    outs = [jax.empty_ref(jax.ShapeDtypeStruct(s.shape, s.dtype), memory_space=hbm) for s in psums]

    @functools.partial(
        pl.kernel, mesh=plsc.ScalarSubcoreMesh(axis_name="seq", num_cores=1), name=name,
        scratch_types=(pltpu.SemaphoreType.DMA((nt, N_CHIP - 1)), pltpu.SemaphoreType.DMA((nt, N_CHIP - 1)),
                       pltpu.SemaphoreType.DMA((nt,))),
        compiler_params=pltpu.CompilerParams(collective_id=collective_id))
    def launch(send_sems, recv_sems, local_sems):
        me = _my_pos()
        peers = [_peer(me, k) for k in (2, 4, 6)]
        _handshake(peers)
        mine = _chip_of(me)
        local, sends = [], []
        for t in range(nt):
            cp = pltpu.make_async_copy(ins[t].at[mine], outs[t].at[mine], local_sems.at[t])
            cp.start()
            local.append(cp)
            for j, peer in enumerate(peers):
                cp = pltpu.make_async_remote_copy(src_ref=ins[t].at[_chip_of(peer)], dst_ref=outs[t].at[mine],
                                                  send_sem=send_sems.at[t, j], recv_sem=recv_sems.at[t, j],
                                                  device_id=peer, device_id_type=MESH)
                cp.start()
                sends.append(cp)
        for t in range(nt):
            for j, peer in enumerate(peers):
                pltpu.make_async_remote_copy(src_ref=ins[t].at[mine], dst_ref=outs[t].at[_chip_of(peer)],
                                             send_sem=send_sems.at[t, j], recv_sem=recv_sems.at[t, j],
                                             device_id=peer, device_id_type=MESH).wait_recv()
        for cp in sends:
            cp.wait_send()
        for cp in local:
            cp.wait()

    launch()
    return [o[...] for o in outs]


def _shard_shapes(grads, kinds):
    return [(g.shape[0], g.shape[1] // N_DEV) if kind == "col" else (g.shape[0] // N_DEV, g.shape[1]) for g, kind in zip(grads, kinds)]


def _adam(g_slots, w, m, v, *, tr, name, after=()):
    ns, r, wd = g_slots.shape
    tr = min(tr, r)
    assert r % tr == 0, (name, r, tr)
    c1 = 1.0 - ADAM_B1 ** ADAM_STEP
    c2 = 1.0 - ADAM_B2 ** ADAM_STEP
    n_after = len(after)

    def kern(g_ref, w_ref, m_ref, v_ref, *rest):
        go_ref, d_ref, mo_ref, vo_ref = rest[n_after:]
        g = g_ref[0].astype(F32)
        for s in range(1, ns):
            g = g + g_ref[s].astype(F32)
        m_new = ADAM_B1 * m_ref[...] + (1.0 - ADAM_B1) * g
        v_new = ADAM_B2 * v_ref[...] + (1.0 - ADAM_B2) * (g * g)
        m_hat = m_new / c1
        v_hat = v_new / c2
        go_ref[...] = g
        d_ref[...] = -ADAM_LR * (m_hat / (jnp.sqrt(v_hat) + ADAM_EPS) + ADAM_WD * w_ref[...])
        mo_ref[...] = m_new
        vo_ref[...] = v_new

    tile = pl.BlockSpec((tr, wd), lambda i: (i, 0))
    return pl.pallas_call(
        kern,
        name=name,
        grid=(r // tr,),
        in_specs=[pl.BlockSpec((ns, tr, wd), lambda i: (0, i, 0)), tile, tile, tile] + [pl.BlockSpec(memory_space=pl.ANY)] * n_after,
        out_specs=[tile] * 4,
        out_shape=[jax.ShapeDtypeStruct((r, wd), F32)] * 4,
        compiler_params=_cparams(("parallel",)),
    )(g_slots, w, m, v, *after)


SMALL = ("c_ctx", "b_ada", "attn_sink", "ssm_a_re", "ssm_a_im", "ssm_log_dt", "ssm_b_re", "ssm_b_im", "ssm_c_re", "ssm_c_im",
         "ssm_d", "ln_mix_g", "ln_mix_b", "b_mlp1", "b_mlp2", "ln_mlp_g", "ln_mlp_b")
BIG = ("w_in", "w_glu", "w_attn_up", "w_ssm_up", "w_out", "w_mlp1", "w_mlp2")
BIG_KIND = ("col", "col", "col", "col", "row", "col", "row")
AG_GROUPS = (("w_in",), ("w_glu", "w_attn_up", "w_ssm_up", "w_out"), ("w_mlp1",), ("w_mlp2",))
AG_COLLECTIVE_ID0 = 1
RS_GROUPS = (("w_mlp2",), ("w_mlp1",), ("w_out", "w_attn_up", "w_ssm_up", "w_glu"), ("w_in",))
RS_COLLECTIVE_ID0 = AG_COLLECTIVE_ID0 + len(AG_GROUPS)
SMALL_EARLY = ("ssm_a_re", "ssm_a_im", "ssm_log_dt", "ssm_b_re", "ssm_b_im", "ssm_c_re", "ssm_c_im", "ssm_d")
SMALL_LATE = tuple(n for n in SMALL if n not in SMALL_EARLY)
SMALL_COLLECTIVE_ID0 = RS_COLLECTIVE_ID0 + 2 * len(RS_GROUPS)
LANES = 128


def _pack(parts):
    rows = []
    for p in parts:
        flat = p.reshape(-1).astype(F32)
        pad = (-flat.shape[0]) % LANES
        rows.append(jnp.pad(flat, (0, pad)).reshape(-1, LANES))
    packed = jnp.concatenate(rows, 0)
    return jnp.pad(packed, ((0, (-packed.shape[0]) % 8), (0, 0)))


def _unpack(packed, shapes):
    out, r0 = [], 0
    for s in shapes:
        n = math.prod(s)
        nr = -(-n // LANES)
        out.append(packed[r0:r0 + nr].reshape(-1)[:n].reshape(s))
        r0 += nr
    return out


WEIGHTS = ("c_ctx", "w_ada", "b_ada", "w_in", "attn_sink", "ssm_a_re", "ssm_a_im", "ssm_log_dt", "ssm_b_re", "ssm_b_im",
           "ssm_c_re", "ssm_c_im", "ssm_d", "w_glu", "w_attn_up", "w_ssm_up", "w_out", "ln_mix_g", "ln_mix_b", "w_mlp1",
           "b_mlp1", "w_mlp2", "b_mlp2", "ln_mlp_g", "ln_mlp_b")
ADA_COLS = 6 * D // N_DEV


def _step(x, c, ctx, loss_target, p, m, v):
    me = _lin(_my_pos())
    x2, ctx2, tgt2 = x[0], ctx[0], loss_target[0]

    wb = {}
    for gi, group in enumerate(AG_GROUPS):
        full = _allgather_weights_seq([p[n][0].astype(BF16) for n in group], [BIG_KIND[BIG.index(n)] for n in group],
                                      "allgather_seq%d" % gi, AG_COLLECTIVE_ID0 + gi)
        wb.update(zip(group, full))

    c_all = _allgather_small(jnp.broadcast_to(c, (8, D)), "gather_c")[:, 0, :]
    cc = p["c_ctx"].reshape(1, D)
    s_in = jnp.concatenate([c_all, cc, jnp.zeros((7, D), F32)], 0)
    s_act, = _rowwise(lambda rv, vv: ([_silu(rv[0])], []), [(s_in, D, 0, 0)], [], [(D, F32)], [], nrows=16, tr=16, name="silu_c")
    b_mine = lax.dynamic_slice_in_dim(p["b_ada"], me * ADA_COLS, ADA_COLS, axis=1)
    mod_part = _matmul(s_act, p["w_ada"][0], mode="nn", name="ada_fwd", tm=16, tn=512, bias=b_mine)
    mod_all = _allgather_small(mod_part, "gather_mod")
    mod_lat = lax.dynamic_index_in_dim(mod_all, me, axis=1, keepdims=False).reshape(1, 6 * D)
    mod_ctx = mod_all[:, 8, :].reshape(1, 6 * D)

    sp = {n: p[n][0] for n in SMALL if n not in ("c_ctx", "b_ada")}
    recv, halves = {}, {}

    def on_grad(gw):
        for gi, group in enumerate(RS_GROUPS):
            if gi not in halves and all(n in gw for n in group):
                kinds = [BIG_KIND[BIG.index(n)] for n in group]
                halves[gi] = (dict(gw), _pair_exchange_seq([gw[n] for n in group], kinds, "pair_exchange%d" % gi, RS_COLLECTIVE_ID0 + 2 * gi))

    def on_finish(key, after):
        gi = [i for i, group in enumerate(RS_GROUPS) if key in group][0]
        group = RS_GROUPS[gi]
        grads, half = halves[gi]
        prev = tuple(recv[n] for n in RS_GROUPS[gi - 1][:1]) if gi else ()
        if gi == len(RS_GROUPS) - 1:
            prev += (small["early"],)
        psums =[_pair_add(grads[n], h, BIG_KIND[BIG.index(n)], "pair_add_" + n, after=(after,) + prev) for n, h in zip(group, half)]
        recv.update(zip(group, _chip_exchange_seq(psums, "chip_exchange%d" % gi, RS_COLLECTIVE_ID0 + 2 * gi + 1)))
        return psums[-1]

    small = {}

    def on_early(gs_early):
        small["early"] = _allgather_small_seq(_pack([gs_early[n] for n in SMALL_EARLY]), "gather_small_early", SMALL_COLLECTIVE_ID0)

    total = {}

    def on_loss(loss_p):
        total["loss"] = lax.psum(loss_p[0, 0], ("x", "y", "c"))
        return total["loss"].reshape(1, 1)

    loss_p, grad_x, d_mod_lat, d_mod_ctx, gw, gs = _local_step(x2, ctx2, tgt2, mod_lat, mod_ctx, wb, sp, on_grad, on_loss, on_finish, on_early)

    g_early = small["early"]
    res = {}
    last = ()

    def adam_small(names, g_pack, tag, after):
        sm = _adam(g_pack, _pack([p[n] for n in names]), _pack([m[n] for n in names]), _pack([v[n] for n in names]),
                   tr=g_pack.shape[1], name="adam_small_" + tag, after=after)
        shapes = [p[n].shape for n in names]
        for j, outs in enumerate(zip(*[_unpack(a, shapes) for a in sm])):
            res[names[j]] = outs
        return (sm[0],)

    for gi, group in enumerate(RS_GROUPS):
        if gi == len(RS_GROUPS) - 1:
            last = adam_small(SMALL_EARLY, g_early, "early", last)
        for n in group:
            res[n] = _adam(recv[n], p[n][0], m[n][0], v[n][0], tr=256, name="adam_" + n, after=last)
            last = (res[n][0],)

    dm = jnp.concatenate([d_mod_lat, d_mod_ctx, jnp.zeros((6, 6 * D), F32)], 0)
    dm_all = _allgather_small_seq(dm, "gather_dmod", SMALL_COLLECTIVE_ID0 + 1)
    dm_all = lax.optimization_barrier((dm_all,) + last)[0]
    dm2 = jnp.concatenate([dm_all[:, 0, :], dm_all[:, 1, :]], 0)
    dm2_mine = lax.dynamic_slice_in_dim(dm2, me * ADA_COLS, ADA_COLS, axis=1)
    s2 = jnp.concatenate([s_act[0:8], jnp.broadcast_to(s_act[8:9], (8, D))], 0)
    g_w_ada = _matmul(s2, dm2_mine, mode="tn", name="dw_ada", tm=512, tn=ADA_COLS, after=last)
    dsc_part = _matmul(dm2_mine[8:16], p["w_ada"][0], mode="nt", name="d_silu_cctx", tm=8, tn=512, after=last)

    def cctx_b(rv, vv):
        _, pull = jax.vjp(_silu, vv[0])
        return [], [pull(jnp.sum(rv[0], axis=0, keepdims=True))[0]]

    g_cctx, = _rowwise(cctx_b, [(dsc_part, D, 0, 0)], [cc], [], [(1, D)], nrows=8, tr=8, name="cctx_bwd")
    gs["c_ctx"] = g_cctx
    gs["b_ada"] = d_mod_lat + d_mod_ctx

    res["w_ada"] = _adam(g_w_ada[None], p["w_ada"][0], m["w_ada"][0], v["w_ada"][0], tr=256, name="adam_w_ada")

    g_late = _allgather_small_seq(_pack([gs[n] for n in SMALL_LATE]), "gather_small_late", SMALL_COLLECTIVE_ID0 + 2)
    adam_small(SMALL_LATE, g_late, "late", (res["w_ada"][0],))

    outs = [total["loss"], grad_x[None]]
    for j in range(4):
        outs += [res[n][j].reshape(p[n].shape) for n in WEIGHTS]
    return tuple(outs)


def kernel(x, c, ctx, c_ctx, w_ada, b_ada, w_in, attn_sink, ssm_a_re, ssm_a_im, ssm_log_dt, ssm_b_re, ssm_b_im, ssm_c_re, ssm_c_im, ssm_d, w_glu, w_attn_up, w_ssm_up, w_out, ln_mix_g, ln_mix_b, w_mlp1, b_mlp1, w_mlp2, b_mlp2, ln_mlp_g, ln_mlp_b, loss_target, m_c_ctx, m_w_ada, m_b_ada, m_w_in, m_attn_sink, m_ssm_a_re, m_ssm_a_im, m_ssm_log_dt, m_ssm_b_re, m_ssm_b_im, m_ssm_c_re, m_ssm_c_im, m_ssm_d, m_w_glu, m_w_attn_up, m_w_ssm_up, m_w_out, m_ln_mix_g, m_ln_mix_b, m_w_mlp1, m_b_mlp1, m_w_mlp2, m_b_mlp2, m_ln_mlp_g, m_ln_mlp_b, v_c_ctx, v_w_ada, v_b_ada, v_w_in, v_attn_sink, v_ssm_a_re, v_ssm_a_im, v_ssm_log_dt, v_ssm_b_re, v_ssm_b_im, v_ssm_c_re, v_ssm_c_im, v_ssm_d, v_w_glu, v_w_attn_up, v_w_ssm_up, v_w_out, v_ln_mix_g, v_ln_mix_b, v_w_mlp1, v_b_mlp1, v_w_mlp2, v_b_mlp2, v_ln_mlp_g, v_ln_mlp_b):
    given = dict(locals())
    p = {n: given[n] for n in WEIGHTS}
    m = {n: given["m_" + n] for n in WEIGHTS}
    v = {n: given["v_" + n] for n in WEIGHTS}
    return _step(x, c, ctx, loss_target, p, m, v)
```

```python
import functools
import math

import jax
import jax.numpy as jnp
from jax import lax
from jax.experimental import pallas as pl
from jax.experimental.pallas import tpu as pltpu
from jax.experimental.pallas import tpu_sc as plsc

F32 = jnp.float32
BF16 = jnp.bfloat16

N_DEV = 8
D = 2048
T = 2048
C = 256
TA = T + C
GRID_W = 64
HD = 128
NH = 8
NKV = 2
GROUP = NH // NKV
WINDOW = 128
QW = NH * HD
KVW = NKV * HD
SW = D // 4
SG = 16
NG = SW // SG
SP = 64
DFF = 4 * D
IN_COLS = QW + 2 * KVW + SW + 2 * D
ALPHA = 2.0 ** 0.25
LN_EPS = 1e-6
NEG_INF = -1e30
ROPE_BASE = 10000.0
ATT_SCALE = HD ** -0.5

NSEG = 8
GBLK = 8
NBLK = NG // GBLK
BW = GBLK * SP
UW = GBLK * SG

ADAM_LR = 0.001
ADAM_B1 = 0.9
ADAM_B2 = 0.999
ADAM_EPS = 1e-08
ADAM_WD = 0.01
ADAM_STEP = 10

VMEM_LIMIT_BYTES = 56 * 1024 * 1024
MESH = pl.DeviceIdType.MESH


def _cparams(sem=None):
    return pltpu.CompilerParams(dimension_semantics=sem, vmem_limit_bytes=VMEM_LIMIT_BYTES)


def _matmul(a, b, *, mode, name, out_dtypes=(F32,), tm=512, tn=512, tk=None, bias=None, extras=(), epilogue=None, after=(),
            out_t=None):
    if mode == "nn":
        (M, K), (K2, N) = a.shape, b.shape
    elif mode == "nt":
        (M, K), (N, K2) = a.shape, b.shape
    else:
        (K, M), (K2, N) = a.shape, b.shape
    assert K == K2, (name, a.shape, b.shape)
    tm, tn, tk = min(tm, M), min(tn, N), min(tk or K, K)
    assert M % tm == 0 and N % tn == 0 and K % tk == 0, (name, M, N, K, tm, tn, tk)
    nk = K // tk
    if mode == "tn":
        a_spec = pl.BlockSpec((tk, tm), lambda i, j, k: (k, i))
    else:
        a_spec = pl.BlockSpec((tm, tk), lambda i, j, k: (i, k))
    if mode == "nt":
        b_spec = pl.BlockSpec((tn, tk), lambda i, j, k: (j, k))
    else:
        b_spec = pl.BlockSpec((tk, tn), lambda i, j, k: (k, j))
    dims = {"nn": (((1,), (0,)), ((), ())), "nt": (((1,), (1,)), ((), ())), "tn": (((0,), (0,)), ((), ()))}[mode]
    in_specs = [a_spec, b_spec]
    operands = [a, b]
    if bias is not None:
        in_specs.append(pl.BlockSpec((1, tn), lambda i, j, k: (0, j)))
        operands.append(bias)
    for e in extras:
        in_specs.append(pl.BlockSpec((tm, tn), lambda i, j, k: (i, j)))
        operands.append(e)
    n_ex = len(extras)
    for t in after:
        in_specs.append(pl.BlockSpec(memory_space=pl.ANY))
        operands.append(t)
    n_after = len(after)
    n_out = len(out_dtypes)
    out_t = tuple(out_t) if out_t is not None else (False,) * n_out
    has_bias = bias is not None

    def kern(*refs):
        a_ref, b_ref = refs[0], refs[1]
        pos = 2
        bias_ref = None
        if has_bias:
            bias_ref = refs[pos]
            pos += 1
        ex_refs = refs[pos:pos + n_ex]
        pos += n_ex + n_after
        out_refs = refs[pos:pos + n_out]
        acc_ref = refs[pos + n_out] if nk > 1 else None

        def finish(r):
            if has_bias:
                r = r + bias_ref[...]
            outs = epilogue(r, *[e[...] for e in ex_refs]) if epilogue is not None else (r,)
            for o_ref, o, tr_ in zip(out_refs, outs, out_t):
                o_ref[...] = (o.T if tr_ else o).astype(o_ref.dtype)

        part = lax.dot_general(a_ref[...].astype(BF16), b_ref[...].astype(BF16), dims, preferred_element_type=F32)
        if nk == 1:
            finish(part)
        else:
            k = pl.program_id(2)

            @pl.when(k == 0)
            def _():
                acc_ref[...] = part

            @pl.when(k > 0)
            def _():
                acc_ref[...] += part

            @pl.when(k == nk - 1)
            def _():
                finish(acc_ref[...])

    outs = pl.pallas_call(
        kern,
        name=name,
        grid=(M // tm, N // tn, nk),
        in_specs=in_specs,
        out_specs=[pl.BlockSpec((tn, tm), lambda i, j, k: (j, i)) if tr_ else pl.BlockSpec((tm, tn), lambda i, j, k: (i, j))
                   for tr_ in out_t],
        out_shape=[jax.ShapeDtypeStruct((N, M) if tr_ else (M, N), dt) for dt, tr_ in zip(out_dtypes, out_t)],
        scratch_shapes=[pltpu.VMEM((tm, tn), F32)] if nk > 1 else [],
        compiler_params=_cparams(("parallel", "parallel", "arbitrary")),
    )(*operands)
    return outs[0] if n_out == 1 else tuple(outs)


def _rowwise(fn, rows, vecs, outs, vec_outs, *, nrows, tr, name, after=()):
    n_rows, n_vecs, n_outs, n_after = len(rows), len(vecs), len(outs), len(after)
    in_specs = [pl.BlockSpec((tr, w), lambda i, cb=cb, ro=ro: (i + ro, cb)) for (_, w, cb, ro) in rows]
    in_specs += [pl.BlockSpec(v.shape, lambda i: (0, 0)) for v in vecs]
    in_specs += [pl.BlockSpec(memory_space=pl.ANY)] * n_after
    outs = [o if len(o) == 3 else (*o, False) for o in outs]
    out_specs = [pl.BlockSpec((w, tr), lambda i: (0, i)) if tr_ else pl.BlockSpec((tr, w), lambda i: (i, 0)) for (w, _, tr_) in outs]
    out_specs += [pl.BlockSpec(s, lambda i: (0, 0)) for s in vec_outs]
    out_shape = [jax.ShapeDtypeStruct((w, nrows) if tr_ else (nrows, w), dt) for (w, dt, tr_) in outs]
    out_tr = [tr_ for (_, _, tr_) in outs]
    out_shape += [jax.ShapeDtypeStruct(s, F32) for s in vec_outs]

    def kern(*refs):
        rvals = [r[...] for r in refs[:n_rows]]
        vvals = [r[...] for r in refs[n_rows:n_rows + n_vecs]]
        first_out = n_rows + n_vecs + n_after
        o_refs = refs[first_out:first_out + n_outs]
        v_refs = refs[first_out + n_outs:]
        ro, vo = fn(rvals, vvals)
        for r, val, tr_ in zip(o_refs, ro, out_tr):
            r[...] = (val.astype(F32).T if tr_ else val).astype(r.dtype)
        i = pl.program_id(0)
        for r, val in zip(v_refs, vo):
            @pl.when(i == 0)
            def _(r=r, val=val):
                r[...] = val.astype(F32)

            @pl.when(i > 0)
            def _(r=r, val=val):
                r[...] += val.astype(F32)

    res = pl.pallas_call(
        kern,
        name=name,
        grid=(nrows // tr,),
        in_specs=in_specs,
        out_specs=out_specs,
        out_shape=out_shape,
        compiler_params=_cparams(("arbitrary",)),
    )(*[r[0] for r in rows], *vecs, *after)
    return list(res)


def _ln(x):
    mu = jnp.mean(x, axis=-1, keepdims=True)
    xc = x - mu
    var = jnp.mean(xc * xc, axis=-1, keepdims=True)
    return xc * lax.rsqrt(var + LN_EPS)


def _sigmoid(x):
    return 1.0 / (1.0 + jnp.exp(-x))


def _gelu(x):
    return 0.5 * x * (1.0 + jnp.tanh(math.sqrt(2.0 / math.pi) * (x + 0.044715 * (x * x * x))))


def _silu(x):
    return x * _sigmoid(x)


def _f_ln_mod(x, sc, sh):
    return _ln(x) * (1.0 + sc) + sh


def _f_glu(z):
    return z[:, :SW] * _sigmoid(z[:, SW:])


def _f_mix(ga, gs, attn_d, ssm_d):
    return _sigmoid(ga) * attn_d + _sigmoid(gs) * ssm_d


def _f_post1(x, y, g1, lg, lb, sc2, sh2):
    r1 = ALPHA * x + g1 * y
    x1 = _ln(r1) * lg + lb
    h2 = _ln(x1) * (1.0 + sc2) + sh2
    return x1, h2


def _f_loss(x1, mlp, tgt, g2, lg, lb, b2z):
    r2 = ALPHA * x1 + g2 * (mlp + b2z)
    out = _ln(r2) * lg + lb
    err = out - tgt
    return 0.5 * jnp.sum(err * err) * (1.0 / D)


def _rope_tables():
    rows = T // GRID_W
    row = jnp.repeat(jnp.arange(rows), GRID_W)
    col = jnp.tile(jnp.arange(GRID_W), rows)
    n_freq = HD // 4
    freqs = ROPE_BASE ** (-jnp.arange(n_freq, dtype=F32) / n_freq)
    ang_r = row.astype(F32)[:, None] * freqs
    ang_c = col.astype(F32)[:, None] * freqs
    ang = jnp.concatenate([ang_r, ang_r, ang_c, ang_c], -1)
    cos, sin = jnp.cos(ang), jnp.sin(ang)
    lo = (jnp.arange(HD) % (HD // 2)) < (HD // 4)
    sin_a = jnp.where(lo[None, :], -sin, 0.0)
    sin_b = jnp.where(lo[None, :], 0.0, sin)
    return cos, sin_a, sin_b


def _rope(x, cos, sa, sb):
    return x * cos + pltpu.roll(x, 96, 1) * sa + pltpu.roll(x, 32, 1) * sb


def _rope_t(dy, cos, sa, sb):
    return dy * cos + pltpu.roll(dy * sa, 32, 1) + pltpu.roll(dy * sb, 96, 1)


BAND = 3 * WINDOW
KPAD = T + 2 * WINDOW


def _attn_fill_kv(k_ref, v_ref, cos_ref, sa_ref, sb_ref, kp, vp, kc, vc):
    zeros = jnp.zeros((WINDOW, KVW), BF16)
    kp[0:WINDOW, :] = zeros
    kp[WINDOW + T:KPAD, :] = zeros
    vp[0:WINDOW, :] = zeros
    vp[WINDOW + T:KPAD, :] = zeros
    for hh in range(NKV):
        cs = slice(hh * HD, (hh + 1) * HD)
        for r0 in range(0, T, 512):
            rs = slice(r0, r0 + 512)
            kr = _rope(k_ref[rs, cs], cos_ref[rs, :], sa_ref[rs, :], sb_ref[rs, :])
            kp[WINDOW + r0:WINDOW + r0 + 512, cs] = kr.astype(BF16)
    vp[WINDOW:WINDOW + T, :] = v_ref[0:T, :].astype(BF16)
    kc[...] = k_ref[T:TA, :].astype(BF16)
    vc[...] = v_ref[T:TA, :].astype(BF16)


GROWS = GROUP * WINDOW


def _attn_scores(n, kvh, q_ref, cos_ref, sa_ref, sb_ref, sink_ref, kp, kc):
    r0 = pl.multiple_of(n * WINDOW, WINDOW)
    cos = cos_ref[pl.ds(r0, WINDOW), :]
    sa = sa_ref[pl.ds(r0, WINDOW), :]
    sb = sb_ref[pl.ds(r0, WINDOW), :]
    heads = range(kvh * GROUP, (kvh + 1) * GROUP)
    q_g = jnp.concatenate([_rope(q_ref[:, h * HD:(h + 1) * HD], cos, sa, sb).astype(BF16) for h in heads], axis=0)
    kb = kp[pl.ds(r0, BAND), kvh * HD:(kvh + 1) * HD]
    kcb = kc[:, kvh * HD:(kvh + 1) * HD]
    nt = (((1,), (1,)), ((), ()))
    s_loc = lax.dot_general(q_g, kb, nt, preferred_element_type=F32) * ATT_SCALE
    s_ctx = lax.dot_general(q_g, kcb, nt, preferred_element_type=F32) * ATT_SCALE
    row = lax.broadcasted_iota(jnp.int32, (GROWS, BAND), 0) & (WINDOW - 1)
    col = lax.broadcasted_iota(jnp.int32, (GROWS, BAND), 1)
    rel = col - WINDOW - row
    kpos = r0 - WINDOW + col
    valid = (jnp.abs(rel) <= WINDOW) & (kpos >= 0) & (kpos < T)
    s_loc = jnp.where(valid, s_loc, NEG_INF)
    sk = jnp.concatenate([jnp.broadcast_to(sink_ref[0:1, h:h + 1], (WINDOW, 1)) for h in heads], axis=0)
    m = jnp.maximum(jnp.maximum(jnp.max(s_loc, -1, keepdims=True), jnp.max(s_ctx, -1, keepdims=True)), sk)
    e_loc = jnp.exp(s_loc - m)
    e_ctx = jnp.exp(s_ctx - m)
    e_sink = jnp.exp(sk - m)
    inv = 1.0 / (jnp.sum(e_loc, -1, keepdims=True) + jnp.sum(e_ctx, -1, keepdims=True) + e_sink)
    return q_g, r0, e_loc * inv, e_ctx * inv, e_sink * inv


def _attn_fwd(proj, sink, tabs):
    cos, sa, sb = tabs

    def kern(q_ref, k_ref, v_ref, cos_ref, sa_ref, sb_ref, sink_ref, o_ref, kp, vp, kc, vc):
        n = pl.program_id(0)

        @pl.when(n == 0)
        def _():
            _attn_fill_kv(k_ref, v_ref, cos_ref, sa_ref, sb_ref, kp, vp, kc, vc)

        for kvh in range(NKV):
            _, r0, p_loc, p_ctx, _ = _attn_scores(n, kvh, q_ref, cos_ref, sa_ref, sb_ref, sink_ref, kp, kc)
            vb = vp[pl.ds(r0, BAND), kvh * HD:(kvh + 1) * HD]
            vcb = vc[:, kvh * HD:(kvh + 1) * HD]
            o = jnp.dot(p_loc.astype(BF16), vb, preferred_element_type=F32)
            o = o + jnp.dot(p_ctx.astype(BF16), vcb, preferred_element_type=F32)
            for g in range(GROUP):
                h = kvh * GROUP + g
                o_ref[:, h * HD:(h + 1) * HD] = o[g * WINDOW:(g + 1) * WINDOW, :].astype(o_ref.dtype)

    full = lambda shape: pl.BlockSpec(shape, lambda n: (0, 0))
    return pl.pallas_call(
        kern,
        name="attn_fwd",
        grid=(T // WINDOW,),
        in_specs=[
            pl.BlockSpec((WINDOW, QW), lambda n: (n, 0)),
            pl.BlockSpec((TA, KVW), lambda n: (0, QW // KVW)),
            pl.BlockSpec((TA, KVW), lambda n: (0, QW // KVW + 1)),
            full((T, HD)), full((T, HD)), full((T, HD)), full((1, NH)),
        ],
        out_specs=pl.BlockSpec((WINDOW, QW), lambda n: (n, 0)),
        out_shape=jax.ShapeDtypeStruct((T, QW), BF16),
        scratch_shapes=[pltpu.VMEM((KPAD, KVW), BF16), pltpu.VMEM((KPAD, KVW), BF16),
                        pltpu.VMEM((C, KVW), BF16), pltpu.VMEM((C, KVW), BF16)],
        compiler_params=_cparams(("arbitrary",)),
    )(proj, proj, proj, cos, sa, sb, sink)


def _attn_bwd(proj, d_attn, sink, tabs):
    cos, sa, sb = tabs
    n_blocks = T // WINDOW

    def kern(q_ref, k_ref, v_ref, do_ref, cos_ref, sa_ref, sb_ref, sink_ref,
             dq_ref, dk_ref, dv_ref, dsink_ref, kp, vp, kc, vc, dkp, dvp, dkc, dvc):
        n = pl.program_id(0)

        @pl.when(n == 0)
        def _():
            _attn_fill_kv(k_ref, v_ref, cos_ref, sa_ref, sb_ref, kp, vp, kc, vc)
            dkp[...] = jnp.zeros_like(dkp)
            dvp[...] = jnp.zeros_like(dvp)
            dkc[...] = jnp.zeros_like(dkc)
            dvc[...] = jnp.zeros_like(dvc)
            dsink_ref[...] = jnp.zeros_like(dsink_ref)

        nt = (((1,), (1,)), ((), ()))
        tn = (((0,), (0,)), ((), ()))
        for kvh in range(NKV):
            cs = slice(kvh * HD, (kvh + 1) * HD)
            heads = range(kvh * GROUP, (kvh + 1) * GROUP)
            q_g, r0, p_loc, p_ctx, p_sink = _attn_scores(n, kvh, q_ref, cos_ref, sa_ref, sb_ref, sink_ref, kp, kc)
            kb = kp[pl.ds(r0, BAND), cs]
            vb = vp[pl.ds(r0, BAND), cs]
            kcb = kc[:, cs]
            vcb = vc[:, cs]
            do_g = jnp.concatenate([do_ref[:, h * HD:(h + 1) * HD] for h in heads], axis=0)
            dp_loc = lax.dot_general(do_g, vb, nt, preferred_element_type=F32)
            dp_ctx = lax.dot_general(do_g, vcb, nt, preferred_element_type=F32)
            delta = jnp.sum(p_loc * dp_loc, -1, keepdims=True) + jnp.sum(p_ctx * dp_ctx, -1, keepdims=True)
            ds_loc = (p_loc * (dp_loc - delta) * ATT_SCALE).astype(BF16)
            ds_ctx = (p_ctx * (dp_ctx - delta) * ATT_SCALE).astype(BF16)
            dq = jnp.dot(ds_loc, kb, preferred_element_type=F32) + jnp.dot(ds_ctx, kcb, preferred_element_type=F32)
            cos = cos_ref[pl.ds(r0, WINDOW), :]
            sa_ = sa_ref[pl.ds(r0, WINDOW), :]
            sb_ = sb_ref[pl.ds(r0, WINDOW), :]
            dkp[pl.ds(r0, BAND), cs] += lax.dot_general(ds_loc, q_g, tn, preferred_element_type=F32)
            dkc[:, cs] += lax.dot_general(ds_ctx, q_g, tn, preferred_element_type=F32)
            dvp[pl.ds(r0, BAND), cs] += lax.dot_general(p_loc.astype(BF16), do_g, tn, preferred_element_type=F32)
            dvc[:, cs] += lax.dot_general(p_ctx.astype(BF16), do_g, tn, preferred_element_type=F32)
            dsk_rows = p_sink * delta
            for g, h in enumerate(heads):
                rs = slice(g * WINDOW, (g + 1) * WINDOW)
                dq_ref[:, h * HD:(h + 1) * HD] = _rope_t(dq[rs, :], cos, sa_, sb_).astype(dq_ref.dtype)
                dsk = -jnp.sum(dsk_rows[rs, :], axis=0, keepdims=True)
                dsink_ref[h:h + 1, :] += jnp.broadcast_to(dsk, (1, HD))

        @pl.when(n == n_blocks - 1)
        def _():
            for hh in range(NKV):
                cs = slice(hh * HD, (hh + 1) * HD)
                for r0 in range(0, T, 512):
                    rs = slice(r0, r0 + 512)
                    g = dkp[WINDOW + r0:WINDOW + r0 + 512, cs]
                    dk_ref[rs, cs] = _rope_t(g, cos_ref[rs, :], sa_ref[rs, :], sb_ref[rs, :]).astype(dk_ref.dtype)
            dk_ref[T:TA, :] = dkc[...].astype(dk_ref.dtype)
            dv_ref[0:T, :] = dvp[WINDOW:WINDOW + T, :].astype(dv_ref.dtype)
            dv_ref[T:TA, :] = dvc[...].astype(dv_ref.dtype)

    full = lambda shape: pl.BlockSpec(shape, lambda n: (0, 0))
    return pl.pallas_call(
        kern,
        name="attn_bwd",
        grid=(n_blocks,),
        in_specs=[
            pl.BlockSpec((WINDOW, QW), lambda n: (n, 0)),
            pl.BlockSpec((TA, KVW), lambda n: (0, QW // KVW)),
            pl.BlockSpec((TA, KVW), lambda n: (0, QW // KVW + 1)),
            pl.BlockSpec((WINDOW, QW), lambda n: (n, 0)),
            full((T, HD)), full((T, HD)), full((T, HD)), full((1, NH)),
        ],
        out_specs=[pl.BlockSpec((WINDOW, QW), lambda n: (n, 0)), full((TA, KVW)), full((TA, KVW)), full((NH, HD))],
        out_shape=[jax.ShapeDtypeStruct((T, QW), BF16), jax.ShapeDtypeStruct((TA, KVW), BF16),
                   jax.ShapeDtypeStruct((TA, KVW), BF16), jax.ShapeDtypeStruct((NH, HD), F32)],
        scratch_shapes=[pltpu.VMEM((KPAD, KVW), BF16), pltpu.VMEM((KPAD, KVW), BF16),
                        pltpu.VMEM((C, KVW), BF16), pltpu.VMEM((C, KVW), BF16),
                        pltpu.VMEM((KPAD, KVW), F32), pltpu.VMEM((KPAD, KVW), F32),
                        pltpu.VMEM((C, KVW), F32), pltpu.VMEM((C, KVW), F32)],
        compiler_params=_cparams(("arbitrary",)),
    )(proj, proj, proj, d_attn, cos, sa, sb, sink)


def _s5_prep(a_re, a_im, log_dt, b_re, b_im, c_re, c_im):
    lam = lax.complex(a_re, a_im)
    dt = jnp.exp(log_dt)[..., None]
    lam_bar = jnp.exp(lam * dt)
    b_bar = ((lam_bar - 1.0) / lam)[..., None] * lax.complex(b_re, b_im)
    def lam_rows(v):
        return v.reshape(2, NBLK, 1, BW)

    lam_l = jnp.concatenate([lam_rows(jnp.real(lam_bar)), lam_rows(jnp.imag(lam_bar))], -1)
    lam_l = jnp.broadcast_to(lam_l, (2, NBLK, 8, 2 * BW))
    diag = (jnp.arange(UW)[:, None] // SG) == (jnp.arange(BW)[None, :] // SP)

    def blocks(v):
        return jnp.where(diag, jnp.tile(v.reshape(2, NBLK, UW, SP), (1, 1, 1, GBLK)), 0.0)

    b_t = jnp.swapaxes(b_bar, -1, -2)
    bmat = jnp.concatenate([blocks(jnp.real(b_t)), blocks(jnp.imag(b_t))], -1)
    cmat = jnp.concatenate([blocks(c_re), -blocks(c_im)], -1)
    return lam_l, bmat, cmat


def _cmul(ar, ai, br, bi):
    return ar * br - ai * bi, ar * bi + ai * br


def _shift_rows(x, rev, fill):
    r = lax.broadcasted_iota(jnp.int32, x.shape, 0)
    down = jnp.where(r == 0, fill, pltpu.roll(x, 1, 0))
    up = jnp.where(r == NSEG - 1, fill, pltpu.roll(x, NSEG - 1, 0))
    return jnp.where(rev == 0, down, up)


def _edge_row(x, rev):
    last = jnp.broadcast_to(x[NSEG - 1:NSEG, :], x.shape)
    first = jnp.broadcast_to(x[0:1, :], x.shape)
    return jnp.where(rev == 0, last, first)


def _seg_scan(get, put, base, seglen, lr, li, rev, cin, acc_fn=None, acc0=()):
    zero = jnp.zeros((NSEG, BW), F32)

    def rows(k):
        j = jnp.where(rev == 0, k, seglen - 1 - k)
        return pl.ds(pl.multiple_of(base + j * NSEG, NSEG), NSEG)

    def local(k, carry):
        sr, si = carry
        xr, xi = get(rows(k))
        tr, ti = _cmul(lr, li, sr, si)
        sr, si = tr + xr, ti + xi
        put(rows(k), sr, si)
        return sr, si

    er, ei = lax.fori_loop(0, seglen, local, (zero, zero))
    lpr, lpi = lr, li
    assert seglen & (seglen - 1) == 0, seglen
    for _ in range(seglen.bit_length() - 1):
        lpr, lpi = _cmul(lpr, lpi, lpr, lpi)
    cr, ci = _shift_rows(zero, rev, cin[0]), _shift_rows(zero, rev, cin[1])
    for _ in range(NSEG - 1):
        tr, ti = _cmul(lpr, lpi, cr, ci)
        cr, ci = _shift_rows(er + tr, rev, cin[0]), _shift_rows(ei + ti, rev, cin[1])

    def fix(k, carry):
        tr, ti = _cmul(lr, li, carry[0], carry[1])
        xr, xi = get(rows(k))
        fr, fi = xr + tr, xi + ti
        put(rows(k), fr, fi)
        if acc_fn is None:
            return tr, ti
        j = jnp.where(rev == 0, k, seglen - 1 - k)
        return (tr, ti) + tuple(acc_fn(j, fr, fi, carry[2:]))

    out = lax.fori_loop(0, seglen, fix, (cr, ci) + tuple(acc0))
    tr, ti = out[0], out[1]
    leaving = (_edge_row(er + tr, rev), _edge_row(ei + ti, rev))
    return leaving if acc_fn is None else (leaving, out[2:])


RCH = 256
CSEG = C // NSEG
TSEG = T // NSEG
UCOL0 = (QW + 2 * KVW) // UW


REGIONS = ((0, TSEG), (T, CSEG))


def _state_access(ref, lead=()):
    def get(rows):
        return ref[(*lead, rows, slice(0, BW))], ref[(*lead, rows, slice(BW, 2 * BW))]

    def put(rows, re, im):
        ref[(*lead, rows, slice(0, BW))] = re
        ref[(*lead, rows, slice(BW, 2 * BW))] = im

    return get, put


def _interleave_rows(src_ref, dst_ref, regions=REGIONS):
    for base, seglen in regions:
        def body(j, carry, base=base, seglen=seglen):
            dst_ref[pl.ds(pl.multiple_of(base + j * NSEG, NSEG), NSEG), :] = src_ref[pl.ds(base + j, NSEG, stride=seglen), :]
            return carry

        lax.fori_loop(0, seglen, body, 0, unroll=8)


def _deinterleave_rows(src_ref, dst_ref, regions=REGIONS):
    for base, seglen in regions:
        def body(j, carry, base=base, seglen=seglen):
            dst_ref[pl.ds(base + j, NSEG, stride=seglen), :] = src_ref[pl.ds(pl.multiple_of(base + j * NSEG, NSEG), NSEG), :]
            return carry

        lax.fori_loop(0, seglen, body, 0, unroll=8)


def _s5_fwd(proj, dskip, lam, bmat, cmat):
    def kern(u_ref, dk_ref, lam_ref, b_ref, c_ref, s_ref, ssm_ref, ge_ref, up_ref, yp_ref):
        d = pl.program_id(1)

        @pl.when(d == 0)
        def _():
            _interleave_rows(u_ref, up_ref)

        bm = b_ref[0, 0].astype(BF16)
        for r0 in range(0, TA, RCH):
            s_ref[0, 0, r0:r0 + RCH, :] = jnp.dot(up_ref[r0:r0 + RCH, :].astype(BF16), bm, preferred_element_type=F32)
        lr = lam_ref[0, 0, :, 0:BW]
        li = lam_ref[0, 0, :, BW:2 * BW]
        zero = jnp.zeros((NSEG, BW), F32)
        get, put = _state_access(s_ref, (0, 0))
        mid = _seg_scan(get, put, T, CSEG, lr, li, d, (zero, zero))
        _seg_scan(get, put, 0, TSEG, lr, li, d, mid)
        cm = c_ref[0, 0].astype(BF16)
        for r0 in range(0, T, RCH):
            y = lax.dot_general(s_ref[0, 0, r0:r0 + RCH, :].astype(BF16), cm, (((1,), (1,)), ((), ())), preferred_element_type=F32)

            @pl.when(d == 0)
            def _(y=y, r0=r0):
                yp_ref[r0:r0 + RCH, :] = y + dk_ref[...] * up_ref[r0:r0 + RCH, :]

            @pl.when(d == 1)
            def _(y=y, r0=r0):
                yp_ref[r0:r0 + RCH, :] += y

        @pl.when(d == 1)
        def _():
            _deinterleave_rows(yp_ref, ssm_ref, REGIONS[:1])
            for r0 in range(0, T, RCH):
                ge_ref[r0:r0 + RCH, :] = _gelu(ssm_ref[r0:r0 + RCH, :]).astype(ge_ref.dtype)

    blk4 = lambda shape: pl.BlockSpec((1, 1) + shape, lambda b, d: (d, b, 0, 0))
    return pl.pallas_call(
        kern,
        name="s5_fwd",
        grid=(NBLK, 2),
        in_specs=[pl.BlockSpec((TA, UW), lambda b, d: (0, UCOL0 + b)), pl.BlockSpec((1, UW), lambda b, d: (0, b)),
                  blk4((8, 2 * BW)), blk4((UW, 2 * BW)), blk4((UW, 2 * BW))],
        out_specs=[blk4((TA, 2 * BW)), pl.BlockSpec((T, UW), lambda b, d: (0, b)), pl.BlockSpec((T, UW), lambda b, d: (0, b))],
        out_shape=[jax.ShapeDtypeStruct((2, NBLK, TA, 2 * BW), F32), jax.ShapeDtypeStruct((T, SW), F32),
                   jax.ShapeDtypeStruct((T, SW), BF16)],
        scratch_shapes=[pltpu.VMEM((TA, UW), F32), pltpu.VMEM((T, UW), F32)],
        compiler_params=_cparams(("parallel", "arbitrary")),
    )(proj, dskip, lam, bmat, cmat)


def _s5_bwd(d_ge, ssm, proj, dskip, states, lam, bmat, cmat):
    nt = (((1,), (1,)), ((), ()))
    tn = (((0,), (0,)), ((), ()))

    def kern(dge_ref, ssm_ref, u_ref, dk_ref, s_ref, lam_ref, b_ref, c_ref,
             du_ref, ddk_ref, dlam_ref, db_ref, dc_ref, g_ref, dua_ref, dssm_ref, up_ref, nat_ref):
        d = pl.program_id(1)

        @pl.when(d == 0)
        def _():
            ddk = jnp.zeros((1, UW), F32)
            for r0 in range(0, T, RCH):
                rs = slice(r0, r0 + RCH)
                _, pull = jax.vjp(_gelu, ssm_ref[rs, :])
                dssm = pull(dge_ref[rs, :])[0]
                nat_ref[rs, :] = dssm
                ddk = ddk + jnp.sum(dssm * u_ref[rs, :], axis=0, keepdims=True)
            ddk_ref[...] = ddk
            _interleave_rows(nat_ref, dssm_ref, REGIONS[:1])
            _interleave_rows(u_ref, up_ref)
            for r0 in range(0, T, RCH):
                dua_ref[r0:r0 + RCH, :] = dssm_ref[r0:r0 + RCH, :] * dk_ref[...]
            dua_ref[T:TA, :] = jnp.zeros((C, UW), F32)

        cm = c_ref[0, 0].astype(BF16)
        for r0 in range(0, T, RCH):
            g_ref[r0:r0 + RCH, :] = jnp.dot(dssm_ref[r0:r0 + RCH, :].astype(BF16), cm, preferred_element_type=F32)
        g_ref[T:TA, :] = jnp.zeros((C, 2 * BW), F32)
        lr = lam_ref[0, 0, :, 0:BW]
        li = lam_ref[0, 0, :, BW:2 * BW]
        zero = jnp.zeros((NSEG, BW), F32)
        get_g, put_g = _state_access(g_ref)

        get_s, _ = _state_access(s_ref, (0, 0))

        def dlam_fold(base, seglen, s_in):
            def rows(j):
                return pl.ds(pl.multiple_of(base + j * NSEG, NSEG), NSEG)

            jb = jnp.where(d == 0, 0, seglen - 1)
            jn = jnp.where(d == 0, seglen - 1, 0)
            sp = get_s(rows(jn))
            edge = (_shift_rows(sp[0], d, s_in[0]), _shift_rows(sp[1], d, s_in[1]))

            def fold(j, gr, gi, acc):
                jp = jnp.clip(jnp.where(d == 0, j - 1, j + 1), 0, seglen - 1)
                sr, si = get_s(rows(jp))
                sr = jnp.where(j == jb, edge[0], sr)
                si = jnp.where(j == jb, edge[1], si)
                return acc[0] + (gr * sr + gi * si), acc[1] + (gi * sr - gr * si)

            return fold

        r_mid = jnp.where(d == 0, TA - 1, T)
        s_mid = tuple(jnp.broadcast_to(t, (NSEG, BW)) for t in get_s(pl.ds(r_mid, 1)))
        mid, acc = _seg_scan(get_g, put_g, 0, TSEG, lr, -li, 1 - d, (zero, zero), dlam_fold(0, TSEG, s_mid), (zero, zero))
        _, acc = _seg_scan(get_g, put_g, T, CSEG, lr, -li, 1 - d, mid, dlam_fold(T, CSEG, (zero, zero)), acc)
        dlam_ref[0, 0, :, 0:BW] = acc[0]
        dlam_ref[0, 0, :, BW:2 * BW] = acc[1]

        bm = b_ref[0, 0].astype(BF16)
        db = jnp.zeros((UW, 2 * BW), F32)
        dc = jnp.zeros((UW, 2 * BW), F32)
        for r0 in range(0, TA, RCH):
            rs = slice(r0, r0 + RCH)
            g = g_ref[rs, :].astype(BF16)
            dua_ref[rs, :] += lax.dot_general(g, bm, nt, preferred_element_type=F32)
            db = db + lax.dot_general(up_ref[rs, :].astype(BF16), g, tn, preferred_element_type=F32)
            if r0 < T:
                dc = dc + lax.dot_general(dssm_ref[rs, :].astype(BF16), s_ref[0, 0, rs, :].astype(BF16), tn,
                                          preferred_element_type=F32)
        db_ref[0, 0] = db
        dc_ref[0, 0] = dc

        @pl.when(d == 1)
        def _():
            _deinterleave_rows(dua_ref, nat_ref)
            du_ref[...] = nat_ref[...].astype(du_ref.dtype)

    blk4 = lambda shape: pl.BlockSpec((1, 1) + shape, lambda b, d: (d, b, 0, 0))
    lat = pl.BlockSpec((T, UW), lambda b, d: (0, b))
    vec = pl.BlockSpec((1, UW), lambda b, d: (0, b))
    return pl.pallas_call(
        kern,
        name="s5_bwd",
        grid=(NBLK, 2),
        in_specs=[lat, lat, pl.BlockSpec((TA, UW), lambda b, d: (0, UCOL0 + b)), vec,
                  blk4((TA, 2 * BW)), blk4((8, 2 * BW)), blk4((UW, 2 * BW)), blk4((UW, 2 * BW))],
        out_specs=[pl.BlockSpec((TA, UW), lambda b, d: (0, b)), vec, blk4((8, 2 * BW)), blk4((UW, 2 * BW)), blk4((UW, 2 * BW))],
        out_shape=[jax.ShapeDtypeStruct((TA, SW), BF16), jax.ShapeDtypeStruct((1, SW), F32),
                   jax.ShapeDtypeStruct((2, NBLK, 8, 2 * BW), F32),
                   jax.ShapeDtypeStruct((2, NBLK, UW, 2 * BW), F32), jax.ShapeDtypeStruct((2, NBLK, UW, 2 * BW), F32)],
        scratch_shapes=[pltpu.VMEM((TA, 2 * BW), F32), pltpu.VMEM((TA, UW), F32), pltpu.VMEM((T, UW), F32),
                        pltpu.VMEM((TA, UW), F32), pltpu.VMEM((TA, UW), F32)],
        compiler_params=_cparams(("parallel", "arbitrary")),
    )(d_ge, ssm, proj, dskip, states, lam, bmat, cmat)


TR = 256
TN_WIDE = 1024


def _vjp_rows(f, primals, cots, n_row):
    _, pull = jax.vjp(f, *primals)
    g = pull(cots)
    return list(g[:n_row]), list(g[n_row:])


class _GradDict(dict):
    def __init__(self, on_set=None):
        super().__init__()
        self._on_set = on_set
        self.tokens = {}

    def __setitem__(self, key, value):
        super().__setitem__(key, value)
        if self._on_set is not None:
            self._on_set(self)

    def order(self, key):
        return self.tokens.get(key, self.get(key))

    def finish(self, key, after):
        if self.on_finish is None:
            return ()
        return (self.on_finish(key, after),)

    on_finish = None


def _local_step(x, ctx, tgt, mod_lat, mod_ctx, wb, sp, on_grad=None, on_loss=None, on_finish=None, on_early=None):
    sh1, sc1, g1, sh2, sc2, g2 = [mod_lat[:, i * D:(i + 1) * D] for i in range(6)]
    csh1, csc1 = mod_ctx[:, 0:D], mod_ctx[:, D:2 * D]
    tabs = _rope_tables()
    sink = sp["attn_sink"].reshape(1, NH)
    dskip = sp["ssm_d"].reshape(1, SW)
    lg_mix, lb_mix = sp["ln_mix_g"].reshape(1, D), sp["ln_mix_b"].reshape(1, D)
    lg_mlp, lb_mlp = sp["ln_mlp_g"].reshape(1, D), sp["ln_mlp_b"].reshape(1, D)
    b1, b2 = sp["b_mlp1"].reshape(1, DFF), sp["b_mlp2"].reshape(1, D)
    s5_names = ("ssm_a_re", "ssm_a_im", "ssm_log_dt", "ssm_b_re", "ssm_b_im", "ssm_c_re", "ssm_c_im")
    (lam, bmat, cmat), s5_pull = jax.vjp(_s5_prep, *[sp[n] for n in s5_names])

    def ln_mod2(rv, vv):
        h = _f_ln_mod(rv[0], vv[0], vv[1])
        return [h, h], []

    h_lat, h_lat_t = _rowwise(ln_mod2, [(x, D, 0, 0)], [sc1, sh1], [(D, BF16), (D, BF16, True)], [], nrows=T, tr=TR, name="ln1_lat")
    h_ctx, h_ctx_t = _rowwise(ln_mod2, [(ctx, D, 0, 0)], [csc1, csh1], [(D, BF16), (D, BF16, True)], [], nrows=C, tr=TR,
                              name="ln1_ctx")
    h1 = jnp.concatenate([h_lat, h_ctx], 0)
    h1_t = jnp.concatenate([h_lat_t, h_ctx_t], 1)
    proj = _matmul(h1, wb["w_in"], mode="nn", name="proj", tm=768, tn=TN_WIDE)
    attn = _attn_fwd(proj, sink, tabs)
    states, ssm, ge = _s5_fwd(proj, dskip, lam, bmat, cmat)
    z = _matmul(ge, wb["w_glu"], mode="nn", name="glu_mm", tm=1024, tn=1024)

    def glu_act(rv, vv):
        return [_f_glu(rv[0])], []

    glu, = _rowwise(glu_act, [(z, 2 * SW, 0, 0)], [], [(SW, BF16)], [], nrows=T, tr=TR, name="glu_act")
    attn_d = _matmul(attn, wb["w_attn_up"], mode="nn", name="attn_up", tm=1024, tn=512)
    ssm_d = _matmul(glu, wb["w_ssm_up"], mode="nn", name="ssm_up", tm=1024, tn=512)
    ga_cb, gs_cb = (QW + 2 * KVW + SW) // D, (QW + 2 * KVW + SW) // D + 1

    def mix(rv, vv):
        m_ = _f_mix(*rv)
        return [m_, m_], []

    mixv, mix_t = _rowwise(mix, [(proj, D, ga_cb, 0), (proj, D, gs_cb, 0), (attn_d, D, 0, 0), (ssm_d, D, 0, 0)], [],
                           [(D, BF16), (D, BF16, True)], [], nrows=T, tr=TR, name="mix")
    y = _matmul(mixv, wb["w_out"], mode="nn", name="out_proj", tm=1024, tn=TN_WIDE)

    def post1(rv, vv):
        x1, h2 = _f_post1(rv[0], rv[1], *vv)
        return [x1, h2, h2], []

    x1, h2, h2_t = _rowwise(post1, [(x, D, 0, 0), (y, D, 0, 0)], [g1, lg_mix, lb_mix, sc2, sh2],
                            [(D, F32), (D, BF16), (D, BF16, True)], [], nrows=T, tr=TR, name="post1")

    def relu_sq(acc):
        r = jnp.maximum(acc, 0.0)
        return r, r * r, r * r

    r_act, act, act_t = _matmul(h2, wb["w_mlp1"], mode="nn", name="mlp1", tm=1024, tn=TN_WIDE, bias=b1,
                                out_dtypes=(BF16, BF16, BF16), out_t=(False, False, True), epilogue=relu_sq)
    mlp = _matmul(act, wb["w_mlp2"], mode="nn", name="mlp2", tm=512, tn=512)

    def loss_fb(rv, vv):
        x1_t, mlp_t, tgt_t = rv
        g2_v, lg_v, lb_v, b2_v = vv
        f = lambda a, m, g, p, q, b: _f_loss(a, m, tgt_t, g, p, q, b)
        val, grads = jax.value_and_grad(f, argnums=(0, 1, 2, 3, 4, 5))(x1_t, mlp_t, g2_v, lg_v, lb_v, b2_v)
        dx1, dmlp, dg2, dlg, dlb, db2 = grads
        return [dx1, dmlp], [jnp.reshape(val, (1, 1)), dg2, dlg, dlb, db2]

    dx1_a, d_mlp, loss_p, d_g2, d_lg_mlp, d_lb_mlp, d_b2 = _rowwise(
        loss_fb, [(x1, D, 0, 0), (mlp, D, 0, 0), (tgt, D, 0, 0)], [g2, lg_mlp, lb_mlp, b2],
        [(D, F32), (D, BF16)], [(1, 1), (1, D), (1, D), (1, D), (1, D)], nrows=T, tr=TR, name="loss_fb")

    gw = _GradDict(on_grad)
    gw.on_finish = on_finish
    loss_done = () if on_loss is None else (on_loss(loss_p),)
    gw["w_mlp2"] = _matmul(act_t, d_mlp, mode="nn", name="dw_mlp2", out_dtypes=(BF16,), tm=1024, tn=TN_WIDE, after=loss_done)
    da, = (_matmul(d_mlp, wb["w_mlp2"], mode="nt", name="d_act", out_dtypes=(BF16,), tm=1024, tn=TN_WIDE,
                   extras=(r_act,), epilogue=lambda acc, r: (acc * (2.0 * r.astype(F32)),), after=(gw.order("w_mlp2"),)),)
    pin = gw.finish("w_mlp2", da)
    ones = jnp.ones((8, T), BF16)
    d_b1 = _matmul(ones, da, mode="nn", name="db_mlp1", tm=8, tn=2048)[0:1]
    gw["w_mlp1"] = _matmul(h2_t, da, mode="nn", name="dw_mlp1", out_dtypes=(BF16,), tm=1024, tn=TN_WIDE, after=pin)
    dh2 = _matmul(da, wb["w_mlp1"], mode="nt", name="d_h2", tm=512, tn=512, after=(gw.order("w_mlp1"),))

    def post1_b(rv, vv):
        x_t, y_t, dx1_t, dh2_t = rv
        gr, gv = _vjp_rows(_f_post1, (x_t, y_t, *vv), (dx1_t, dh2_t), 2)
        return [gr[0], gr[1]], gv

    dx_a, dy, d_g1, d_lg_mix, d_lb_mix, d_sc2, d_sh2 = _rowwise(
        post1_b, [(x, D, 0, 0), (y, D, 0, 0), (dx1_a, D, 0, 0), (dh2, D, 0, 0)], [g1, lg_mix, lb_mix, sc2, sh2],
        [(D, F32), (D, BF16)], [(1, D)] * 5, nrows=T, tr=TR, name="post1_bwd")
    gw["w_out"] = _matmul(mix_t, dy, mode="nn", name="dw_out", out_dtypes=(BF16,), tm=1024, tn=TN_WIDE)
    dmix = _matmul(dy, wb["w_out"], mode="nt", name="d_mix", tm=1024, tn=TN_WIDE, after=(gw.order("w_out"),))

    def mix_b(rv, vv):
        gr, _ = _vjp_rows(_f_mix, tuple(rv[:4]), rv[4], 4)
        return gr, []

    d_ga, d_gs, d_attn_d, d_ssm_d = _rowwise(
        mix_b, [(proj, D, ga_cb, 0), (proj, D, gs_cb, 0), (attn_d, D, 0, 0), (ssm_d, D, 0, 0), (dmix, D, 0, 0)], [],
        [(D, BF16)] * 4, [], nrows=T, tr=TR, name="mix_bwd")
    pin = gw.finish("w_mlp1", d_ga)
    gw["w_attn_up"] = _matmul(attn, d_attn_d, mode="tn", name="dw_attn_up", out_dtypes=(BF16,), tm=512, tn=1024, tk=1024, after=pin)
    d_attn = _matmul(d_attn_d, wb["w_attn_up"], mode="nt", name="d_attn", out_dtypes=(BF16,), tm=1024, tn=512)
    gw["w_ssm_up"] = _matmul(glu, d_ssm_d, mode="tn", name="dw_ssm_up", out_dtypes=(BF16,), tm=512, tn=1024, tk=1024)
    d_glu = _matmul(d_ssm_d, wb["w_ssm_up"], mode="nt", name="d_glu", tm=1024, tn=512, after=(gw.order("w_attn_up"), gw.order("w_ssm_up")))

    def glu_b(rv, vv):
        gr, _ = _vjp_rows(_f_glu, (rv[0],), rv[1], 1)
        return gr, []

    dz, = _rowwise(glu_b, [(z, 2 * SW, 0, 0), (d_glu, SW, 0, 0)], [], [(2 * SW, BF16)], [], nrows=T, tr=TR, name="glu_bwd")
    gw["w_glu"] = _matmul(ge, dz, mode="tn", name="dw_glu", out_dtypes=(BF16,), tm=512, tn=1024, tk=1024)
    d_ge = _matmul(dz, wb["w_glu"], mode="nt", name="d_ge", tm=1024, tn=512, after=(gw.order("w_glu"),))

    du_all, d_dskip, dlam, dbmat, dcmat = _s5_bwd(d_ge, ssm, proj, dskip, states, lam, bmat, cmat)
    s5_grads = s5_pull((dlam, dbmat, dcmat))
    early = dict(zip(s5_names, s5_grads), ssm_d=d_dskip)
    if on_early is not None:
        on_early(early)
    pin = gw.finish("w_glu", du_all)

    dq, dk, dv, dsink = _attn_bwd(proj, d_attn, sink, tabs)
    zc = lambda w: jnp.zeros((C, w), BF16)
    dproj = jnp.concatenate([
        jnp.concatenate([dq, zc(QW)], 0), dk, dv, du_all,
        jnp.concatenate([d_ga, zc(D)], 0), jnp.concatenate([d_gs, zc(D)], 0)], 1)
    gw["w_in"] = _matmul(h1_t, dproj, mode="nn", name="dw_in", out_dtypes=(BF16,), tm=1024, tn=TN_WIDE, after=pin)
    pin = gw.finish("w_in", gw["w_in"])
    dh1 = _matmul(dproj, wb["w_in"], mode="nt", name="d_h1", tm=768, tn=512, after=pin)

    def ln1_b(rv, vv):
        x_t, dh_t, dxa_t = rv
        gr, gv = _vjp_rows(_f_ln_mod, (x_t, vv[0], vv[1]), dh_t, 1)
        return [gr[0] + dxa_t], gv

    grad_x, d_sc1, d_sh1 = _rowwise(ln1_b, [(x, D, 0, 0), (dh1, D, 0, 0), (dx_a, D, 0, 0)], [sc1, sh1],
                                    [(D, F32)], [(1, D), (1, D)], nrows=T, tr=TR, name="ln1_lat_bwd")

    def ln1c_b(rv, vv):
        _, gv = _vjp_rows(_f_ln_mod, (rv[0], vv[0], vv[1]), rv[1], 1)
        return [], gv

    d_csc1, d_csh1 = _rowwise(ln1c_b, [(ctx, D, 0, 0), (dh1, D, 0, T // TR)], [csc1, csh1],
                              [], [(1, D), (1, D)], nrows=C, tr=TR, name="ln1_ctx_bwd")

    d_mod_lat = jnp.concatenate([d_sh1, d_sc1, d_g1, d_sh2, d_sc2, d_g2], 1)
    zv = jnp.zeros((1, D), F32)
    d_mod_ctx = jnp.concatenate([d_csh1, d_csc1, zv, zv, zv, zv], 1)
    gs = {n: g for n, g in zip(s5_names, s5_grads)}
    gs["attn_sink"] = dsink[:, 0]
    gs["ssm_d"] = d_dskip
    gs["ln_mix_g"], gs["ln_mix_b"] = d_lg_mix, d_lb_mix
    gs["ln_mlp_g"], gs["ln_mlp_b"] = d_lg_mlp, d_lb_mlp
    gs["b_mlp1"], gs["b_mlp2"] = d_b1, d_b2
    return loss_p, grad_x, d_mod_lat, d_mod_ctx, gw, gs


def _my_pos():
    return lax.axis_index("x"), lax.axis_index("y"), lax.axis_index("c")


def _flip(p, bit):
    return 1 - p if bit else p


def _peer(pos, k):
    x, y, c = pos
    return (_flip(x, (k >> 2) & 1), _flip(y, (k >> 1) & 1), _flip(c, k & 1))


def _lin(pos):
    return 4 * pos[0] + 2 * pos[1] + pos[2]


def _allgather_small(v, name):
    r, w = v.shape

    def body(v_ref, out_ref, send_sems, recv_sems, local_sem):
        me = _my_pos()
        mine = pltpu.make_async_copy(v_ref, out_ref.at[_lin(me)], local_sem)
        mine.start()
        sends = []
        for k in range(1, N_DEV):
            cp = pltpu.make_async_remote_copy(src_ref=v_ref, dst_ref=out_ref.at[_lin(me)], send_sem=send_sems.at[k - 1],
                                              recv_sem=recv_sems.at[k - 1], device_id=_peer(me, k), device_id_type=MESH)
            cp.start()
            sends.append(cp)
        for k in range(1, N_DEV):
            peer = _peer(me, k)
            pltpu.make_async_remote_copy(src_ref=v_ref, dst_ref=out_ref.at[_lin(peer)], send_sem=send_sems.at[k - 1],
                                         recv_sem=recv_sems.at[k - 1], device_id=peer, device_id_type=MESH).wait_recv()
        for cp in sends:
            cp.wait_send()
        mine.wait()

    return pl.pallas_call(
        body,
        name=name,
        out_shape=jax.ShapeDtypeStruct((N_DEV, r, w), v.dtype),
        in_specs=[pl.BlockSpec(memory_space=pltpu.VMEM)],
        out_specs=pl.BlockSpec(memory_space=pltpu.VMEM),
        scratch_shapes=[pltpu.SemaphoreType.DMA((N_DEV - 1,)), pltpu.SemaphoreType.DMA((N_DEV - 1,)), pltpu.SemaphoreType.DMA],
        compiler_params=pltpu.CompilerParams(vmem_limit_bytes=VMEM_LIMIT_BYTES),
    )(v)


def _block_of(ref, kind, idx, n):
    start = pl.multiple_of(idx * n, 128)
    if kind == "col":
        return ref.at[:, pl.ds(start, n)]
    return ref.at[pl.ds(start, n), :]


def _handshake(peers):
    barrier = pltpu.get_barrier_semaphore()
    for peer in peers:
        pl.semaphore_signal(barrier, inc=1, device_id=peer, device_id_type=MESH)
    pl.semaphore_wait(barrier, len(peers))


def _allgather_weights_seq(shards, kinds, name, collective_id):
    nt = len(shards)
    hbm = pltpu.MemorySpace.HBM
    ins = [jax.new_ref(s, memory_space=hbm) for s in shards]
    outs = []
    for s, kind in zip(shards, kinds):
        k, n = s.shape
        shape = (k, n * N_DEV) if kind == "col" else (k * N_DEV, n)
        outs.append(jax.empty_ref(jax.ShapeDtypeStruct(shape, s.dtype), memory_space=hbm))

    @functools.partial(
        pl.kernel, mesh=plsc.ScalarSubcoreMesh(axis_name="seq", num_cores=1), name=name,
        scratch_types=(pltpu.SemaphoreType.DMA((nt, N_DEV - 1)), pltpu.SemaphoreType.DMA((nt, N_DEV - 1)),
                       pltpu.SemaphoreType.DMA((nt,))),
        compiler_params=pltpu.CompilerParams(collective_id=collective_id))
    def launch(send_sems, recv_sems, local_sems):
        x, y, c = _my_pos()
        me, sibling = (x, y, c), (x, y, 1 - c)
        chips = [(1 - x, y), (x, 1 - y), (1 - x, 1 - y)]
        _handshake([sibling] + [(*chip, c) for chip in chips])

        def blk(t, pos):
            n = shards[t].shape[1] if kinds[t] == "col" else shards[t].shape[0]
            return _block_of(outs[t], kinds[t], _lin(pos), n)

        def copy(t, k, block, to, src=None):
            return pltpu.make_async_remote_copy(src_ref=blk(t, block) if src is None else src, dst_ref=blk(t, block),
                                                send_sem=send_sems.at[t, k], recv_sem=recv_sems.at[t, k],
                                                device_id=to, device_id_type=MESH)

        local, sends = [], []
        for t in range(nt):
            mine = pltpu.make_async_copy(ins[t], blk(t, me), local_sems.at[t])
            mine.start()
            local.append(mine)
            first = [copy(t, 0, me, sibling, src=ins[t])]
            first += [copy(t, 1 + j, me, (*chip, c), src=ins[t]) for j, chip in enumerate(chips)]
            for cp in first:
                cp.start()
            sends += first
        for t in range(nt):
            for j, chip in enumerate(chips):
                copy(t, 1 + j, (*chip, c), me).wait_recv()
                fwd = copy(t, 4 + j, (*chip, c), sibling)
                fwd.start()
                sends.append(fwd)
        for t in range(nt):
            copy(t, 0, sibling, me).wait_recv()
            for j, chip in enumerate(chips):
                copy(t, 4 + j, (*chip, 1 - c), me).wait_recv()
        for cp in sends:
            cp.wait_send()
        for cp in local:
            cp.wait()

    launch()
    return [o[...] for o in outs]


def _allgather_small_seq(v, name, collective_id):
    hbm = pltpu.MemorySpace.HBM
    src = jax.new_ref(v, memory_space=hbm)
    out = jax.empty_ref(jax.ShapeDtypeStruct((N_DEV,) + v.shape, v.dtype), memory_space=hbm)

    @functools.partial(
        pl.kernel, mesh=plsc.ScalarSubcoreMesh(axis_name="seq", num_cores=1), name=name,
        scratch_types=(pltpu.SemaphoreType.DMA((N_DEV - 1,)), pltpu.SemaphoreType.DMA((N_DEV - 1,)), pltpu.SemaphoreType.DMA),
        compiler_params=pltpu.CompilerParams(collective_id=collective_id))
    def launch(send_sems, recv_sems, local_sem):
        me = _my_pos()
        _handshake([_peer(me, k) for k in range(1, N_DEV)])
        mine = pltpu.make_async_copy(src, out.at[_lin(me)], local_sem)
        mine.start()
        sends = []
        for k in range(1, N_DEV):
            cp = pltpu.make_async_remote_copy(src_ref=src, dst_ref=out.at[_lin(me)], send_sem=send_sems.at[k - 1],
                                              recv_sem=recv_sems.at[k - 1], device_id=_peer(me, k), device_id_type=MESH)
            cp.start()
            sends.append(cp)
        for k in range(1, N_DEV):
            peer = _peer(me, k)
            pltpu.make_async_remote_copy(src_ref=src, dst_ref=out.at[_lin(peer)], send_sem=send_sems.at[k - 1],
                                         recv_sem=recv_sems.at[k - 1], device_id=peer, device_id_type=MESH).wait_recv()
        for cp in sends:
            cp.wait_send()
        mine.wait()

    launch()
    return out[...]


N_CHIP = N_DEV // 2


def _chip_of(pos):
    return 2 * pos[0] + pos[1]


def _pair_exchange_seq(grads, kinds, name, collective_id):
    nt = len(grads)
    hbm = pltpu.MemorySpace.HBM
    shard_shapes = _shard_shapes(grads, kinds)
    ins = [jax.new_ref(g, memory_space=hbm) for g in grads]
    outs = [jax.empty_ref(jax.ShapeDtypeStruct((N_CHIP,) + s, g.dtype), memory_space=hbm) for s, g in zip(shard_shapes, grads)]

    @functools.partial(
        pl.kernel, mesh=plsc.ScalarSubcoreMesh(axis_name="seq", num_cores=1), name=name,
        scratch_types=(pltpu.SemaphoreType.DMA((nt, N_CHIP)), pltpu.SemaphoreType.DMA((nt, N_CHIP))),
        compiler_params=pltpu.CompilerParams(collective_id=collective_id))
    def launch(send_sems, recv_sems):
        x, y, c = _my_pos()
        sibling = (x, y, 1 - c)
        _handshake([sibling])
        copies = []
        for t in range(nt):
            n = shard_shapes[t][1] if kinds[t] == "col" else shard_shapes[t][0]
            for q in range(N_CHIP):
                cp = pltpu.make_async_remote_copy(src_ref=_block_of(ins[t], kinds[t], 2 * q + (1 - c), n), dst_ref=outs[t].at[q],
                                                  send_sem=send_sems.at[t, q], recv_sem=recv_sems.at[t, q],
                                                  device_id=sibling, device_id_type=MESH)
                cp.start()
                copies.append(cp)
        for cp in copies:
            cp.wait_recv()
        for cp in copies:
            cp.wait_send()

    launch()
    return [o[...] for o in outs]


def _pair_add(g, half, kind, name, after=()):
    nq, k, ns = half.shape
    tr = min(k, 512)
    c_idx = lax.axis_index("c").astype(jnp.int32).reshape(1)
    if kind == "col":
        g_spec = pl.BlockSpec((tr, ns), lambda q, i, c_ref: (i, 2 * q + c_ref[0]))
    else:
        g_spec = pl.BlockSpec((tr, ns), lambda q, i, c_ref: ((2 * q + c_ref[0]) * (k // tr) + i, 0))
    n_after = len(after)

    def kern(c_ref, g_ref, h_ref, *rest):
        o_ref = rest[n_after]
        o_ref[0] = (g_ref[...].astype(F32) + h_ref[0].astype(F32)).astype(o_ref.dtype)

    return pl.pallas_call(
        kern,
        name=name,
        grid_spec=pltpu.PrefetchScalarGridSpec(
            num_scalar_prefetch=1,
            grid=(nq, k // tr),
            in_specs=[g_spec, pl.BlockSpec((1, tr, ns), lambda q, i, c_ref: (q, i, 0))] + [pl.BlockSpec(memory_space=pl.ANY)] * n_after,
            out_specs=pl.BlockSpec((1, tr, ns), lambda q, i, c_ref: (q, i, 0)),
        ),
        out_shape=jax.ShapeDtypeStruct(half.shape, half.dtype),
        compiler_params=_cparams(("parallel", "parallel")),
    )(c_idx, g, half, *after)


def _chip_exchange_seq(psums, name, collective_id):
    nt = len(psums)
    hbm = pltpu.MemorySpace.HBM
    ins = [jax.new_ref(s, memory_space=hbm) for s in psums]
    outs = [jax.empty_ref(jax.ShapeDtypeStruct(s.shape, s.dtype), memory_space=hbm) for s in psums]

    @functools.partial(
        pl.kernel, mesh=plsc.ScalarSubcoreMesh(axis_name="seq", num_cores=1), name=name,
        scratch_types=(pltpu.SemaphoreType.DMA((nt, N_CHIP - 1)), pltpu.SemaphoreType.DMA((nt, N_CHIP - 1)),
                       pltpu.SemaphoreType.DMA((nt,))),
        compiler_params=pltpu.CompilerParams(collective_id=collective_id))
    def launch(send_sems, recv_sems, local_sems):
        me = _my_pos()
        peers = [_peer(me, k) for k in (2, 4, 6)]
        _handshake(peers)
        mine = _chip_of(me)
        local, sends = [], []
        for t in range(nt):
            cp = pltpu.make_async_copy(ins[t].at[mine], outs[t].at[mine], local_sems.at[t])
            cp.start()
            local.append(cp)
            for j, peer in enumerate(peers):
                cp = pltpu.make_async_remote_copy(src_ref=ins[t].at[_chip_of(peer)], dst_ref=outs[t].at[mine],
                                                  send_sem=send_sems.at[t, j], recv_sem=recv_sems.at[t, j],
                                                  device_id=peer, device_id_type=MESH)
                cp.start()
                sends.append(cp)
        for t in range(nt):
            for j, peer in enumerate(peers):
                pltpu.make_async_remote_copy(src_ref=ins[t].at[mine], dst_ref=outs[t].at[_chip_of(peer)],
                                             send_sem=send_sems.at[t, j], recv_sem=recv_sems.at[t, j],
                                             device_id=peer, device_id_type=MESH).wait_recv()
        for cp in sends:
            cp.wait_send()
        for cp in local:
            cp.wait()

    launch()
    return [o[...] for o in outs]


def _shard_shapes(grads, kinds):
    return [(g.shape[0], g.shape[1] // N_DEV) if kind == "col" else (g.shape[0] // N_DEV, g.shape[1]) for g, kind in zip(grads, kinds)]


def _adam(g_slots, w, m, v, *, tr, name, after=()):
    ns, r, wd = g_slots.shape
    tr = min(tr, r)
    assert r % tr == 0, (name, r, tr)
    n_after = len(after)

    def kern(g_ref, w_ref, m_ref, v_ref, *rest):
        go_ref, d_ref, mo_ref, vo_ref = rest[n_after:]
        g = g_ref[0].astype(F32)
        for s in range(1, ns):
            g = g + g_ref[s].astype(F32)
        delta, m_new, v_new = _adam_update(g, w_ref[...], m_ref[...], v_ref[...])
        go_ref[...] = g
        d_ref[...] = delta
        mo_ref[...] = m_new
        vo_ref[...] = v_new

    tile = pl.BlockSpec((tr, wd), lambda i: (i, 0))
    return pl.pallas_call(
        kern,
        name=name,
        grid=(r // tr,),
        in_specs=[pl.BlockSpec((ns, tr, wd), lambda i: (0, i, 0)), tile, tile, tile] + [pl.BlockSpec(memory_space=pl.ANY)] * n_after,
        out_specs=[tile] * 4,
        out_shape=[jax.ShapeDtypeStruct((r, wd), F32)] * 4,
        compiler_params=_cparams(("parallel",)),
    )(g_slots, w, m, v, *after)


def _adam_update(g, w, m, v):
    m_new = ADAM_B1 * m + (1.0 - ADAM_B1) * g
    v_new = ADAM_B2 * v + (1.0 - ADAM_B2) * (g * g)
    m_hat = m_new / (1.0 - ADAM_B1 ** ADAM_STEP)
    v_hat = v_new / (1.0 - ADAM_B2 ** ADAM_STEP)
    return -ADAM_LR * (m_hat / (jnp.sqrt(v_hat) + ADAM_EPS) + ADAM_WD * w), m_new, v_new


def _lane_offsets(sizes):
    offs, o = [], 0
    for n in sizes:
        offs.append(o)
        o += -(-n // LANES) * LANES
    return offs, o


def _pack_lanes(parts):
    cols = []
    for p_ in parts:
        flat = p_.reshape(1, -1).astype(F32)
        cols.append(jnp.pad(flat, ((0, 0), (0, (-flat.shape[1]) % LANES))))
    return jnp.concatenate(cols, 1)


def _adam_lanes(g_slots, ws, ms, vs, *, name, after=()):
    ns = g_slots.shape[0]
    npar, n_after = len(ws), len(after)
    sizes = [w.shape[1] for w in ws]
    offs, _ = _lane_offsets(sizes)

    def kern(g_ref, *refs):
        w_refs, m_refs, v_refs = refs[:npar], refs[npar:2 * npar], refs[2 * npar:3 * npar]
        outs = refs[3 * npar + n_after:]
        g_all = g_ref[0]
        for s in range(1, ns):
            g_all = g_all + g_ref[s]
        for j in range(npar):
            g = g_all[:, offs[j]:offs[j] + sizes[j]]
            delta, m_new, v_new = _adam_update(g, w_refs[j][...], m_refs[j][...], v_refs[j][...])
            outs[4 * j][...] = g
            outs[4 * j + 1][...] = delta
            outs[4 * j + 2][...] = m_new
            outs[4 * j + 3][...] = v_new

    vmem = pl.BlockSpec(memory_space=pltpu.VMEM)
    res = pl.pallas_call(
        kern,
        name=name,
        in_specs=[vmem] * (1 + 3 * npar) + [pl.BlockSpec(memory_space=pl.ANY)] * n_after,
        out_specs=[vmem] * (4 * npar),
        out_shape=[jax.ShapeDtypeStruct((1, n), F32) for n in sizes for _ in range(4)],
        compiler_params=pltpu.CompilerParams(vmem_limit_bytes=VMEM_LIMIT_BYTES),
    )(g_slots, *ws, *ms, *vs, *after)
    return [tuple(res[4 * j:4 * j + 4]) for j in range(npar)]


SMALL = ("c_ctx", "b_ada", "attn_sink", "ssm_a_re", "ssm_a_im", "ssm_log_dt", "ssm_b_re", "ssm_b_im", "ssm_c_re", "ssm_c_im",
         "ssm_d", "ln_mix_g", "ln_mix_b", "b_mlp1", "b_mlp2", "ln_mlp_g", "ln_mlp_b")
BIG = ("w_in", "w_glu", "w_attn_up", "w_ssm_up", "w_out", "w_mlp1", "w_mlp2")
BIG_KIND = ("col", "col", "col", "col", "row", "col", "row")
AG_GROUPS = (("w_in",), ("w_glu", "w_attn_up", "w_ssm_up", "w_out"), ("w_mlp1",), ("w_mlp2",))
AG_COLLECTIVE_ID0 = 1
RS_GROUPS = (("w_mlp2",), ("w_mlp1",), ("w_out", "w_attn_up", "w_ssm_up", "w_glu"), ("w_in",))
RS_COLLECTIVE_ID0 = AG_COLLECTIVE_ID0 + len(AG_GROUPS)
SMALL_EARLY = ("ssm_a_re", "ssm_a_im", "ssm_log_dt", "ssm_b_re", "ssm_b_im", "ssm_c_re", "ssm_c_im", "ssm_d")
SMALL_LATE = tuple(n for n in SMALL if n not in SMALL_EARLY)
SMALL_COLLECTIVE_ID0 = RS_COLLECTIVE_ID0 + 2 * len(RS_GROUPS)
LANES = 128


def _pack(parts):
    rows = []
    for p in parts:
        flat = p.reshape(-1).astype(F32)
        pad = (-flat.shape[0]) % LANES
        rows.append(jnp.pad(flat, (0, pad)).reshape(-1, LANES))
    packed = jnp.concatenate(rows, 0)
    return jnp.pad(packed, ((0, (-packed.shape[0]) % 8), (0, 0)))


def _unpack(packed, shapes):
    out, r0 = [], 0
    for s in shapes:
        n = math.prod(s)
        nr = -(-n // LANES)
        out.append(packed[r0:r0 + nr].reshape(-1)[:n].reshape(s))
        r0 += nr
    return out


WEIGHTS = ("c_ctx", "w_ada", "b_ada", "w_in", "attn_sink", "ssm_a_re", "ssm_a_im", "ssm_log_dt", "ssm_b_re", "ssm_b_im",
           "ssm_c_re", "ssm_c_im", "ssm_d", "w_glu", "w_attn_up", "w_ssm_up", "w_out", "ln_mix_g", "ln_mix_b", "w_mlp1",
           "b_mlp1", "w_mlp2", "b_mlp2", "ln_mlp_g", "ln_mlp_b")
ADA_COLS = 6 * D // N_DEV


def _step(x, c, ctx, loss_target, p, m, v):
    me = _lin(_my_pos())
    x2, ctx2, tgt2 = x[0], ctx[0], loss_target[0]

    wb = {}
    for gi, group in enumerate(AG_GROUPS):
        full = _allgather_weights_seq([p[n][0].astype(BF16) for n in group], [BIG_KIND[BIG.index(n)] for n in group],
                                      "allgather_seq%d" % gi, AG_COLLECTIVE_ID0 + gi)
        wb.update(zip(group, full))

    c_all = _allgather_small(jnp.broadcast_to(c, (8, D)), "gather_c")[:, 0, :]
    cc = p["c_ctx"].reshape(1, D)
    s_in = jnp.concatenate([c_all, cc, jnp.zeros((7, D), F32)], 0)
    s_act, = _rowwise(lambda rv, vv: ([_silu(rv[0])], []), [(s_in, D, 0, 0)], [], [(D, F32)], [], nrows=16, tr=16, name="silu_c")
    b_mine = lax.dynamic_slice_in_dim(p["b_ada"], me * ADA_COLS, ADA_COLS, axis=1)
    mod_part = _matmul(s_act, p["w_ada"][0], mode="nn", name="ada_fwd", tm=16, tn=512, bias=b_mine)
    mod_all = _allgather_small(mod_part, "gather_mod")
    mod_lat = lax.dynamic_index_in_dim(mod_all, me, axis=1, keepdims=False).reshape(1, 6 * D)
    mod_ctx = mod_all[:, 8, :].reshape(1, 6 * D)

    sp = {n: p[n][0] for n in SMALL if n not in ("c_ctx", "b_ada")}
    recv, halves = {}, {}

    def on_grad(gw):
        for gi, group in enumerate(RS_GROUPS):
            if gi not in halves and all(n in gw for n in group):
                kinds = [BIG_KIND[BIG.index(n)] for n in group]
                halves[gi] = (dict(gw), _pair_exchange_seq([gw[n] for n in group], kinds, "pair_exchange%d" % gi, RS_COLLECTIVE_ID0 + 2 * gi))

    def on_finish(key, after):
        gi = [i for i, group in enumerate(RS_GROUPS) if key in group][0]
        group = RS_GROUPS[gi]
        grads, half = halves[gi]
        prev = tuple(recv[n] for n in RS_GROUPS[gi - 1][:1]) if gi else ()
        if gi == len(RS_GROUPS) - 1:
            prev += (small["early"],)
        psums =[_pair_add(grads[n], h, BIG_KIND[BIG.index(n)], "pair_add_" + n, after=(after,) + prev) for n, h in zip(group, half)]
        recv.update(zip(group, _chip_exchange_seq(psums, "chip_exchange%d" % gi, RS_COLLECTIVE_ID0 + 2 * gi + 1)))
        return psums[-1]

    small = {}

    def on_early(gs_early):
        small["early"] = _allgather_small_seq(_pack([gs_early[n] for n in SMALL_EARLY]), "gather_small_early", SMALL_COLLECTIVE_ID0)

    total = {}

    def on_loss(loss_p):
        total["loss"] = lax.psum(loss_p[0, 0], ("x", "y", "c"))
        return total["loss"].reshape(1, 1)

    loss_p, grad_x, d_mod_lat, d_mod_ctx, gw, gs = _local_step(x2, ctx2, tgt2, mod_lat, mod_ctx, wb, sp, on_grad, on_loss, on_finish, on_early)

    g_early = small["early"]
    res = {}
    last = ()

    def adam_small(names, g_pack, tag, after):
        sm = _adam(g_pack, _pack([p[n] for n in names]), _pack([m[n] for n in names]), _pack([v[n] for n in names]),
                   tr=g_pack.shape[1], name="adam_small_" + tag, after=after)
        shapes = [p[n].shape for n in names]
        for j, outs in enumerate(zip(*[_unpack(a, shapes) for a in sm])):
            res[names[j]] = outs
        return (sm[0],)

    for gi, group in enumerate(RS_GROUPS):
        if gi == len(RS_GROUPS) - 1:
            last = adam_small(SMALL_EARLY, g_early, "early", last)
        for n in group:
            res[n] = _adam(recv[n], p[n][0], m[n][0], v[n][0], tr=256, name="adam_" + n, after=last)
            last = (res[n][0],)

    dm = jnp.concatenate([d_mod_lat, d_mod_ctx, jnp.zeros((6, 6 * D), F32)], 0)
    dm_all = _allgather_small_seq(dm, "gather_dmod", SMALL_COLLECTIVE_ID0 + 1)
    dm_all = lax.optimization_barrier((dm_all,) + last)[0]
    dm2 = jnp.concatenate([dm_all[:, 0, :], dm_all[:, 1, :]], 0)
    dm2_mine = lax.dynamic_slice_in_dim(dm2, me * ADA_COLS, ADA_COLS, axis=1)
    s2 = jnp.concatenate([s_act[0:8], jnp.broadcast_to(s_act[8:9], (8, D))], 0)
    g_w_ada = _matmul(s2, dm2_mine, mode="tn", name="dw_ada", tm=512, tn=ADA_COLS, after=last)
    dsc_part = _matmul(dm2_mine[8:16], p["w_ada"][0], mode="nt", name="d_silu_cctx", tm=8, tn=512, after=last)

    def cctx_b(rv, vv):
        _, pull = jax.vjp(_silu, vv[0])
        return [], [pull(jnp.sum(rv[0], axis=0, keepdims=True))[0]]

    g_cctx, = _rowwise(cctx_b, [(dsc_part, D, 0, 0)], [cc], [], [(1, D)], nrows=8, tr=8, name="cctx_bwd")
    gs["c_ctx"] = g_cctx
    gs["b_ada"] = d_mod_lat + d_mod_ctx

    res["w_ada"] = _adam(g_w_ada[None], p["w_ada"][0], m["w_ada"][0], v["w_ada"][0], tr=256, name="adam_w_ada")

    g_late = _allgather_small_seq(_pack_lanes([gs[n] for n in SMALL_LATE]), "gather_small_late", SMALL_COLLECTIVE_ID0 + 2)
    row = lambda a: a.reshape(1, -1)
    late = _adam_lanes(g_late, [row(p[n]) for n in SMALL_LATE], [row(m[n]) for n in SMALL_LATE], [row(v[n]) for n in SMALL_LATE],
                       name="adam_small_late", after=(res["w_ada"][0],))
    res.update(zip(SMALL_LATE, late))

    outs = [total["loss"], grad_x[None]]
    for j in range(4):
        outs += [res[n][j].reshape(p[n].shape) for n in WEIGHTS]
    return tuple(outs)


def kernel(x, c, ctx, c_ctx, w_ada, b_ada, w_in, attn_sink, ssm_a_re, ssm_a_im, ssm_log_dt, ssm_b_re, ssm_b_im, ssm_c_re, ssm_c_im, ssm_d, w_glu, w_attn_up, w_ssm_up, w_out, ln_mix_g, ln_mix_b, w_mlp1, b_mlp1, w_mlp2, b_mlp2, ln_mlp_g, ln_mlp_b, loss_target, m_c_ctx, m_w_ada, m_b_ada, m_w_in, m_attn_sink, m_ssm_a_re, m_ssm_a_im, m_ssm_log_dt, m_ssm_b_re, m_ssm_b_im, m_ssm_c_re, m_ssm_c_im, m_ssm_d, m_w_glu, m_w_attn_up, m_w_ssm_up, m_w_out, m_ln_mix_g, m_ln_mix_b, m_w_mlp1, m_b_mlp1, m_w_mlp2, m_b_mlp2, m_ln_mlp_g, m_ln_mlp_b, v_c_ctx, v_w_ada, v_b_ada, v_w_in, v_attn_sink, v_ssm_a_re, v_ssm_a_im, v_ssm_log_dt, v_ssm_b_re, v_ssm_b_im, v_ssm_c_re, v_ssm_c_im, v_ssm_d, v_w_glu, v_w_attn_up, v_w_ssm_up, v_w_out, v_ln_mix_g, v_ln_mix_b, v_w_mlp1, v_b_mlp1, v_w_mlp2, v_b_mlp2, v_ln_mlp_g, v_ln_mlp_b):
    given = dict(locals())
    p = {n: given[n] for n in WEIGHTS}
    m = {n: given["m_" + n] for n in WEIGHTS}
    v = {n: given["v_" + n] for n in WEIGHTS}
    return _step(x, c, ctx, loss_target, p, m, v)
```

```python
import functools
import math

import jax
import jax.numpy as jnp
from jax import lax
from jax.experimental import pallas as pl
from jax.experimental.pallas import tpu as pltpu
from jax.experimental.pallas import tpu_sc as plsc

F32 = jnp.float32
BF16 = jnp.bfloat16

N_DEV = 8
D = 2048
T = 2048
C = 256
TA = T + C
GRID_W = 64
HD = 128
NH = 8
NKV = 2
GROUP = NH // NKV
WINDOW = 128
QW = NH * HD
KVW = NKV * HD
SW = D // 4
SG = 16
NG = SW // SG
SP = 64
DFF = 4 * D
IN_COLS = QW + 2 * KVW + SW + 2 * D
ALPHA = 2.0 ** 0.25
LN_EPS = 1e-6
NEG_INF = -1e30
ROPE_BASE = 10000.0
ATT_SCALE = HD ** -0.5

NSEG = 8
GBLK = 8
NBLK = NG // GBLK
BW = GBLK * SP
UW = GBLK * SG

ADAM_LR = 0.001
ADAM_B1 = 0.9
ADAM_B2 = 0.999
ADAM_EPS = 1e-08
ADAM_WD = 0.01
ADAM_STEP = 10

VMEM_LIMIT_BYTES = 56 * 1024 * 1024
MESH = pl.DeviceIdType.MESH


def _cparams(sem=None):
    return pltpu.CompilerParams(dimension_semantics=sem, vmem_limit_bytes=VMEM_LIMIT_BYTES)


def _matmul(a, b, *, mode, name, out_dtypes=(F32,), tm=512, tn=512, tk=None, bias=None, extras=(), epilogue=None, after=(),
            out_t=None):
    if mode == "nn":
        (M, K), (K2, N) = a.shape, b.shape
    elif mode == "nt":
        (M, K), (N, K2) = a.shape, b.shape
    else:
        (K, M), (K2, N) = a.shape, b.shape
    assert K == K2, (name, a.shape, b.shape)
    tm, tn, tk = min(tm, M), min(tn, N), min(tk or K, K)
    assert M % tm == 0 and N % tn == 0 and K % tk == 0, (name, M, N, K, tm, tn, tk)
    nk = K // tk
    if mode == "tn":
        a_spec = pl.BlockSpec((tk, tm), lambda i, j, k: (k, i))
    else:
        a_spec = pl.BlockSpec((tm, tk), lambda i, j, k: (i, k))
    if mode == "nt":
        b_spec = pl.BlockSpec((tn, tk), lambda i, j, k: (j, k))
    else:
        b_spec = pl.BlockSpec((tk, tn), lambda i, j, k: (k, j))
    dims = {"nn": (((1,), (0,)), ((), ())), "nt": (((1,), (1,)), ((), ())), "tn": (((0,), (0,)), ((), ()))}[mode]
    in_specs = [a_spec, b_spec]
    operands = [a, b]
    if bias is not None:
        in_specs.append(pl.BlockSpec((1, tn), lambda i, j, k: (0, j)))
        operands.append(bias)
    for e in extras:
        in_specs.append(pl.BlockSpec((tm, tn), lambda i, j, k: (i, j)))
        operands.append(e)
    n_ex = len(extras)
    for t in after:
        in_specs.append(pl.BlockSpec(memory_space=pl.ANY))
        operands.append(t)
    n_after = len(after)
    n_out = len(out_dtypes)
    out_t = tuple(out_t) if out_t is not None else (False,) * n_out
    has_bias = bias is not None

    def kern(*refs):
        a_ref, b_ref = refs[0], refs[1]
        pos = 2
        bias_ref = None
        if has_bias:
            bias_ref = refs[pos]
            pos += 1
        ex_refs = refs[pos:pos + n_ex]
        pos += n_ex + n_after
        out_refs = refs[pos:pos + n_out]
        acc_ref = refs[pos + n_out] if nk > 1 else None

        def finish(r):
            if has_bias:
                r = r + bias_ref[...]
            outs = epilogue(r, *[e[...] for e in ex_refs]) if epilogue is not None else (r,)
            for o_ref, o, tr_ in zip(out_refs, outs, out_t):
                o_ref[...] = (o.T if tr_ else o).astype(o_ref.dtype)

        part = lax.dot_general(a_ref[...].astype(BF16), b_ref[...].astype(BF16), dims, preferred_element_type=F32)
        if nk == 1:
            finish(part)
        else:
            k = pl.program_id(2)

            @pl.when(k == 0)
            def _():
                acc_ref[...] = part

            @pl.when(k > 0)
            def _():
                acc_ref[...] += part

            @pl.when(k == nk - 1)
            def _():
                finish(acc_ref[...])

    outs = pl.pallas_call(
        kern,
        name=name,
        grid=(M // tm, N // tn, nk),
        in_specs=in_specs,
        out_specs=[pl.BlockSpec((tn, tm), lambda i, j, k: (j, i)) if tr_ else pl.BlockSpec((tm, tn), lambda i, j, k: (i, j))
                   for tr_ in out_t],
        out_shape=[jax.ShapeDtypeStruct((N, M) if tr_ else (M, N), dt) for dt, tr_ in zip(out_dtypes, out_t)],
        scratch_shapes=[pltpu.VMEM((tm, tn), F32)] if nk > 1 else [],
        compiler_params=_cparams(("parallel", "parallel", "arbitrary")),
    )(*operands)
    return outs[0] if n_out == 1 else tuple(outs)


def _rowwise(fn, rows, vecs, outs, vec_outs, *, nrows, tr, name, after=(), pad_rows=0, pad_outs=()):
    n_rows, n_vecs, n_outs, n_after = len(rows), len(vecs), len(outs), len(after)
    nblk = nrows // tr
    assert not (pad_rows and vec_outs) and pad_rows % tr == 0
    last = (lambda i: jnp.minimum(i, nblk - 1)) if pad_rows else (lambda i: i)
    in_specs = [pl.BlockSpec((tr, w), lambda i, cb=cb, ro=ro: (last(i) + ro, cb)) for (_, w, cb, ro) in rows]
    in_specs += [pl.BlockSpec(v.shape, lambda i: (0, 0)) for v in vecs]
    in_specs += [pl.BlockSpec(memory_space=pl.ANY)] * n_after
    outs = [o if len(o) == 3 else (*o, False) for o in outs]
    padded = [pad_rows > 0 and j in pad_outs for j in range(n_outs)]
    assert not any(p_ and tr_ for p_, (_, _, tr_) in zip(padded, outs))
    out_specs = [pl.BlockSpec((w, tr), lambda i: (0, last(i))) if tr_ else
                 pl.BlockSpec((tr, w), (lambda i: (i, 0)) if p_ else (lambda i: (last(i), 0))) for (w, _, tr_), p_ in zip(outs, padded)]
    out_specs += [pl.BlockSpec(s, lambda i: (0, 0)) for s in vec_outs]
    out_shape = [jax.ShapeDtypeStruct((w, nrows) if tr_ else (nrows + (pad_rows if p_ else 0), w), dt)
                 for (w, dt, tr_), p_ in zip(outs, padded)]
    out_tr = [tr_ for (_, _, tr_) in outs]
    out_shape += [jax.ShapeDtypeStruct(s, F32) for s in vec_outs]

    def kern(*refs):
        rvals = [r[...] for r in refs[:n_rows]]
        vvals = [r[...] for r in refs[n_rows:n_rows + n_vecs]]
        first_out = n_rows + n_vecs + n_after
        o_refs = refs[first_out:first_out + n_outs]
        v_refs = refs[first_out + n_outs:]
        ro, vo = fn(rvals, vvals)
        i = pl.program_id(0)
        for r, val, tr_, p_ in zip(o_refs, ro, out_tr, padded):
            if p_:
                val = jnp.where(i < nblk, val, jnp.zeros_like(val))
            r[...] = (val.astype(F32).T if tr_ else val).astype(r.dtype)
        for r, val in zip(v_refs, vo):
            @pl.when(i == 0)
            def _(r=r, val=val):
                r[...] = val.astype(F32)

            @pl.when(i > 0)
            def _(r=r, val=val):
                r[...] += val.astype(F32)

    res = pl.pallas_call(
        kern,
        name=name,
        grid=((nrows + pad_rows) // tr,),
        in_specs=in_specs,
        out_specs=out_specs,
        out_shape=out_shape,
        compiler_params=_cparams(("arbitrary",)),
    )(*[r[0] for r in rows], *vecs, *after)
    return list(res)


def _ln(x):
    mu = jnp.mean(x, axis=-1, keepdims=True)
    xc = x - mu
    var = jnp.mean(xc * xc, axis=-1, keepdims=True)
    return xc * lax.rsqrt(var + LN_EPS)


def _sigmoid(x):
    return 1.0 / (1.0 + jnp.exp(-x))


def _gelu(x):
    return 0.5 * x * (1.0 + jnp.tanh(math.sqrt(2.0 / math.pi) * (x + 0.044715 * (x * x * x))))


def _silu(x):
    return x * _sigmoid(x)


def _f_ln_mod(x, sc, sh):
    return _ln(x) * (1.0 + sc) + sh


def _f_glu(z):
    return z[:, :SW] * _sigmoid(z[:, SW:])


def _f_mix(ga, gs, attn_d, ssm_d):
    return _sigmoid(ga) * attn_d + _sigmoid(gs) * ssm_d


def _f_post1(x, y, g1, lg, lb, sc2, sh2):
    r1 = ALPHA * x + g1 * y
    x1 = _ln(r1) * lg + lb
    h2 = _ln(x1) * (1.0 + sc2) + sh2
    return x1, h2


def _f_loss(x1, mlp, tgt, g2, lg, lb, b2z):
    r2 = ALPHA * x1 + g2 * (mlp + b2z)
    out = _ln(r2) * lg + lb
    err = out - tgt
    return 0.5 * jnp.sum(err * err) * (1.0 / D)


def _rope_tables():
    rows = T // GRID_W
    row = jnp.repeat(jnp.arange(rows), GRID_W)
    col = jnp.tile(jnp.arange(GRID_W), rows)
    n_freq = HD // 4
    freqs = ROPE_BASE ** (-jnp.arange(n_freq, dtype=F32) / n_freq)
    ang_r = row.astype(F32)[:, None] * freqs
    ang_c = col.astype(F32)[:, None] * freqs
    ang = jnp.concatenate([ang_r, ang_r, ang_c, ang_c], -1)
    cos, sin = jnp.cos(ang), jnp.sin(ang)
    lo = (jnp.arange(HD) % (HD // 2)) < (HD // 4)
    sin_a = jnp.where(lo[None, :], -sin, 0.0)
    sin_b = jnp.where(lo[None, :], 0.0, sin)
    return cos, sin_a, sin_b


def _rope(x, cos, sa, sb):
    return x * cos + pltpu.roll(x, 96, 1) * sa + pltpu.roll(x, 32, 1) * sb


def _rope_t(dy, cos, sa, sb):
    return dy * cos + pltpu.roll(dy * sa, 32, 1) + pltpu.roll(dy * sb, 96, 1)


BAND = 3 * WINDOW
KPAD = T + 2 * WINDOW


def _attn_fill_kv(k_ref, v_ref, cos_ref, sa_ref, sb_ref, kp, vp, kc, vc):
    zeros = jnp.zeros((WINDOW, KVW), BF16)
    kp[0:WINDOW, :] = zeros
    kp[WINDOW + T:KPAD, :] = zeros
    vp[0:WINDOW, :] = zeros
    vp[WINDOW + T:KPAD, :] = zeros
    for hh in range(NKV):
        cs = slice(hh * HD, (hh + 1) * HD)
        for r0 in range(0, T, 512):
            rs = slice(r0, r0 + 512)
            kr = _rope(k_ref[rs, cs], cos_ref[rs, :], sa_ref[rs, :], sb_ref[rs, :])
            kp[WINDOW + r0:WINDOW + r0 + 512, cs] = kr.astype(BF16)
    vp[WINDOW:WINDOW + T, :] = v_ref[0:T, :].astype(BF16)
    kc[...] = k_ref[T:TA, :].astype(BF16)
    vc[...] = v_ref[T:TA, :].astype(BF16)


GROWS = GROUP * WINDOW


def _attn_scores(n, kvh, q_ref, cos_ref, sa_ref, sb_ref, sink_ref, kp, kc):
    r0 = pl.multiple_of(n * WINDOW, WINDOW)
    cos = cos_ref[pl.ds(r0, WINDOW), :]
    sa = sa_ref[pl.ds(r0, WINDOW), :]
    sb = sb_ref[pl.ds(r0, WINDOW), :]
    heads = range(kvh * GROUP, (kvh + 1) * GROUP)
    q_g = jnp.concatenate([_rope(q_ref[:, h * HD:(h + 1) * HD], cos, sa, sb).astype(BF16) for h in heads], axis=0)
    kb = kp[pl.ds(r0, BAND), kvh * HD:(kvh + 1) * HD]
    kcb = kc[:, kvh * HD:(kvh + 1) * HD]
    nt = (((1,), (1,)), ((), ()))
    s_loc = lax.dot_general(q_g, kb, nt, preferred_element_type=F32) * ATT_SCALE
    s_ctx = lax.dot_general(q_g, kcb, nt, preferred_element_type=F32) * ATT_SCALE
    row = lax.broadcasted_iota(jnp.int32, (GROWS, BAND), 0) & (WINDOW - 1)
    col = lax.broadcasted_iota(jnp.int32, (GROWS, BAND), 1)
    rel = col - WINDOW - row
    kpos = r0 - WINDOW + col
    valid = (jnp.abs(rel) <= WINDOW) & (kpos >= 0) & (kpos < T)
    s_loc = jnp.where(valid, s_loc, NEG_INF)
    sk = jnp.concatenate([jnp.broadcast_to(sink_ref[0:1, h:h + 1], (WINDOW, 1)) for h in heads], axis=0)
    m = jnp.maximum(jnp.maximum(jnp.max(s_loc, -1, keepdims=True), jnp.max(s_ctx, -1, keepdims=True)), sk)
    e_loc = jnp.exp(s_loc - m)
    e_ctx = jnp.exp(s_ctx - m)
    e_sink = jnp.exp(sk - m)
    inv = 1.0 / (jnp.sum(e_loc, -1, keepdims=True) + jnp.sum(e_ctx, -1, keepdims=True) + e_sink)
    return q_g, r0, e_loc * inv, e_ctx * inv, e_sink * inv


def _attn_fwd(proj, sink, tabs):
    cos, sa, sb = tabs

    def kern(q_ref, k_ref, v_ref, cos_ref, sa_ref, sb_ref, sink_ref, o_ref, kp, vp, kc, vc):
        n = pl.program_id(0)

        @pl.when(n == 0)
        def _():
            _attn_fill_kv(k_ref, v_ref, cos_ref, sa_ref, sb_ref, kp, vp, kc, vc)

        for kvh in range(NKV):
            _, r0, p_loc, p_ctx, _ = _attn_scores(n, kvh, q_ref, cos_ref, sa_ref, sb_ref, sink_ref, kp, kc)
            vb = vp[pl.ds(r0, BAND), kvh * HD:(kvh + 1) * HD]
            vcb = vc[:, kvh * HD:(kvh + 1) * HD]
            o = jnp.dot(p_loc.astype(BF16), vb, preferred_element_type=F32)
            o = o + jnp.dot(p_ctx.astype(BF16), vcb, preferred_element_type=F32)
            for g in range(GROUP):
                h = kvh * GROUP + g
                o_ref[:, h * HD:(h + 1) * HD] = o[g * WINDOW:(g + 1) * WINDOW, :].astype(o_ref.dtype)

    full = lambda shape: pl.BlockSpec(shape, lambda n: (0, 0))
    return pl.pallas_call(
        kern,
        name="attn_fwd",
        grid=(T // WINDOW,),
        in_specs=[
            pl.BlockSpec((WINDOW, QW), lambda n: (n, 0)),
            pl.BlockSpec((TA, KVW), lambda n: (0, QW // KVW)),
            pl.BlockSpec((TA, KVW), lambda n: (0, QW // KVW + 1)),
            full((T, HD)), full((T, HD)), full((T, HD)), full((1, NH)),
        ],
        out_specs=pl.BlockSpec((WINDOW, QW), lambda n: (n, 0)),
        out_shape=jax.ShapeDtypeStruct((T, QW), BF16),
        scratch_shapes=[pltpu.VMEM((KPAD, KVW), BF16), pltpu.VMEM((KPAD, KVW), BF16),
                        pltpu.VMEM((C, KVW), BF16), pltpu.VMEM((C, KVW), BF16)],
        compiler_params=_cparams(("arbitrary",)),
    )(proj, proj, proj, cos, sa, sb, sink)


def _attn_bwd(proj, d_attn, sink, tabs):
    cos, sa, sb = tabs
    n_blocks = T // WINDOW

    def kern(q_ref, k_ref, v_ref, do_ref, cos_ref, sa_ref, sb_ref, sink_ref,
             dq_ref, dk_ref, dv_ref, dsink_ref, kp, vp, kc, vc, dkp, dvp, dkc, dvc):
        n = pl.program_id(0)

        @pl.when(n == 0)
        def _():
            _attn_fill_kv(k_ref, v_ref, cos_ref, sa_ref, sb_ref, kp, vp, kc, vc)
            dkp[...] = jnp.zeros_like(dkp)
            dvp[...] = jnp.zeros_like(dvp)
            dkc[...] = jnp.zeros_like(dkc)
            dvc[...] = jnp.zeros_like(dvc)
            dsink_ref[...] = jnp.zeros_like(dsink_ref)
            dq_ref[T:TA, :] = jnp.zeros((C, QW), dq_ref.dtype)

        nt = (((1,), (1,)), ((), ()))
        tn = (((0,), (0,)), ((), ()))
        for kvh in range(NKV):
            cs = slice(kvh * HD, (kvh + 1) * HD)
            heads = range(kvh * GROUP, (kvh + 1) * GROUP)
            q_g, r0, p_loc, p_ctx, p_sink = _attn_scores(n, kvh, q_ref, cos_ref, sa_ref, sb_ref, sink_ref, kp, kc)
            kb = kp[pl.ds(r0, BAND), cs]
            vb = vp[pl.ds(r0, BAND), cs]
            kcb = kc[:, cs]
            vcb = vc[:, cs]
            do_g = jnp.concatenate([do_ref[:, h * HD:(h + 1) * HD] for h in heads], axis=0)
            dp_loc = lax.dot_general(do_g, vb, nt, preferred_element_type=F32)
            dp_ctx = lax.dot_general(do_g, vcb, nt, preferred_element_type=F32)
            delta = jnp.sum(p_loc * dp_loc, -1, keepdims=True) + jnp.sum(p_ctx * dp_ctx, -1, keepdims=True)
            ds_loc = (p_loc * (dp_loc - delta) * ATT_SCALE).astype(BF16)
            ds_ctx = (p_ctx * (dp_ctx - delta) * ATT_SCALE).astype(BF16)
            dq = jnp.dot(ds_loc, kb, preferred_element_type=F32) + jnp.dot(ds_ctx, kcb, preferred_element_type=F32)
            cos = cos_ref[pl.ds(r0, WINDOW), :]
            sa_ = sa_ref[pl.ds(r0, WINDOW), :]
            sb_ = sb_ref[pl.ds(r0, WINDOW), :]
            dkp[pl.ds(r0, BAND), cs] += lax.dot_general(ds_loc, q_g, tn, preferred_element_type=F32)
            dkc[:, cs] += lax.dot_general(ds_ctx, q_g, tn, preferred_element_type=F32)
            dvp[pl.ds(r0, BAND), cs] += lax.dot_general(p_loc.astype(BF16), do_g, tn, preferred_element_type=F32)
            dvc[:, cs] += lax.dot_general(p_ctx.astype(BF16), do_g, tn, preferred_element_type=F32)
            dsk_rows = p_sink * delta
            for g, h in enumerate(heads):
                rs = slice(g * WINDOW, (g + 1) * WINDOW)
                dq_ref[pl.ds(r0, WINDOW), h * HD:(h + 1) * HD] = _rope_t(dq[rs, :], cos, sa_, sb_).astype(dq_ref.dtype)
                dsk = -jnp.sum(dsk_rows[rs, :], axis=0, keepdims=True)
                dsink_ref[h:h + 1, :] += jnp.broadcast_to(dsk, (1, HD))

        @pl.when(n == n_blocks - 1)
        def _():
            for hh in range(NKV):
                cs = slice(hh * HD, (hh + 1) * HD)
                for r0 in range(0, T, 512):
                    rs = slice(r0, r0 + 512)
                    g = dkp[WINDOW + r0:WINDOW + r0 + 512, cs]
                    dk_ref[rs, cs] = _rope_t(g, cos_ref[rs, :], sa_ref[rs, :], sb_ref[rs, :]).astype(dk_ref.dtype)
            dk_ref[T:TA, :] = dkc[...].astype(dk_ref.dtype)
            dv_ref[0:T, :] = dvp[WINDOW:WINDOW + T, :].astype(dv_ref.dtype)
            dv_ref[T:TA, :] = dvc[...].astype(dv_ref.dtype)

    full = lambda shape: pl.BlockSpec(shape, lambda n: (0, 0))
    return pl.pallas_call(
        kern,
        name="attn_bwd",
        grid=(n_blocks,),
        in_specs=[
            pl.BlockSpec((WINDOW, QW), lambda n: (n, 0)),
            pl.BlockSpec((TA, KVW), lambda n: (0, QW // KVW)),
            pl.BlockSpec((TA, KVW), lambda n: (0, QW // KVW + 1)),
            pl.BlockSpec((WINDOW, QW), lambda n: (n, 0)),
            full((T, HD)), full((T, HD)), full((T, HD)), full((1, NH)),
        ],
        out_specs=[full((TA, QW)), full((TA, KVW)), full((TA, KVW)), full((NH, HD))],
        out_shape=[jax.ShapeDtypeStruct((TA, QW), BF16), jax.ShapeDtypeStruct((TA, KVW), BF16),
                   jax.ShapeDtypeStruct((TA, KVW), BF16), jax.ShapeDtypeStruct((NH, HD), F32)],
        scratch_shapes=[pltpu.VMEM((KPAD, KVW), BF16), pltpu.VMEM((KPAD, KVW), BF16),
                        pltpu.VMEM((C, KVW), BF16), pltpu.VMEM((C, KVW), BF16),
                        pltpu.VMEM((KPAD, KVW), F32), pltpu.VMEM((KPAD, KVW), F32),
                        pltpu.VMEM((C, KVW), F32), pltpu.VMEM((C, KVW), F32)],
        compiler_params=_cparams(("arbitrary",)),
    )(proj, proj, proj, d_attn, cos, sa, sb, sink)


def _s5_prep(a_re, a_im, log_dt, b_re, b_im, c_re, c_im):
    lam = lax.complex(a_re, a_im)
    dt = jnp.exp(log_dt)[..., None]
    lam_bar = jnp.exp(lam * dt)
    b_bar = ((lam_bar - 1.0) / lam)[..., None] * lax.complex(b_re, b_im)
    def lam_rows(v):
        return v.reshape(2, NBLK, 1, BW)

    lam_l = jnp.concatenate([lam_rows(jnp.real(lam_bar)), lam_rows(jnp.imag(lam_bar))], -1)
    lam_l = jnp.broadcast_to(lam_l, (2, NBLK, 8, 2 * BW))
    diag = (jnp.arange(UW)[:, None] // SG) == (jnp.arange(BW)[None, :] // SP)

    def blocks(v):
        return jnp.where(diag, jnp.tile(v.reshape(2, NBLK, UW, SP), (1, 1, 1, GBLK)), 0.0)

    b_t = jnp.swapaxes(b_bar, -1, -2)
    bmat = jnp.concatenate([blocks(jnp.real(b_t)), blocks(jnp.imag(b_t))], -1)
    cmat = jnp.concatenate([blocks(c_re), -blocks(c_im)], -1)
    return lam_l, bmat, cmat


def _cmul(ar, ai, br, bi):
    return ar * br - ai * bi, ar * bi + ai * br


def _shift_rows(x, rev, fill):
    r = lax.broadcasted_iota(jnp.int32, x.shape, 0)
    down = jnp.where(r == 0, fill, pltpu.roll(x, 1, 0))
    up = jnp.where(r == NSEG - 1, fill, pltpu.roll(x, NSEG - 1, 0))
    return jnp.where(rev == 0, down, up)


def _edge_row(x, rev):
    last = jnp.broadcast_to(x[NSEG - 1:NSEG, :], x.shape)
    first = jnp.broadcast_to(x[0:1, :], x.shape)
    return jnp.where(rev == 0, last, first)


def _seg_scan(get, put, base, seglen, lr, li, rev, cin, acc_fn=None, acc0=()):
    zero = jnp.zeros((NSEG, BW), F32)

    def rows(k):
        j = jnp.where(rev == 0, k, seglen - 1 - k)
        return pl.ds(pl.multiple_of(base + j * NSEG, NSEG), NSEG)

    def local(k, carry):
        sr, si = carry
        xr, xi = get(rows(k))
        tr, ti = _cmul(lr, li, sr, si)
        sr, si = tr + xr, ti + xi
        put(rows(k), sr, si)
        return sr, si

    er, ei = lax.fori_loop(0, seglen, local, (zero, zero))
    lpr, lpi = lr, li
    assert seglen & (seglen - 1) == 0, seglen
    for _ in range(seglen.bit_length() - 1):
        lpr, lpi = _cmul(lpr, lpi, lpr, lpi)
    cr, ci = _shift_rows(zero, rev, cin[0]), _shift_rows(zero, rev, cin[1])
    for _ in range(NSEG - 1):
        tr, ti = _cmul(lpr, lpi, cr, ci)
        cr, ci = _shift_rows(er + tr, rev, cin[0]), _shift_rows(ei + ti, rev, cin[1])

    def fix(k, carry):
        tr, ti = _cmul(lr, li, carry[0], carry[1])
        xr, xi = get(rows(k))
        fr, fi = xr + tr, xi + ti
        put(rows(k), fr, fi)
        if acc_fn is None:
            return tr, ti
        j = jnp.where(rev == 0, k, seglen - 1 - k)
        return (tr, ti) + tuple(acc_fn(j, fr, fi, carry[2:]))

    out = lax.fori_loop(0, seglen, fix, (cr, ci) + tuple(acc0))
    tr, ti = out[0], out[1]
    leaving = (_edge_row(er + tr, rev), _edge_row(ei + ti, rev))
    return leaving if acc_fn is None else (leaving, out[2:])


RCH = 256
CSEG = C // NSEG
TSEG = T // NSEG
UCOL0 = (QW + 2 * KVW) // UW


REGIONS = ((0, TSEG), (T, CSEG))


def _state_access(ref, lead=()):
    def get(rows):
        return ref[(*lead, rows, slice(0, BW))], ref[(*lead, rows, slice(BW, 2 * BW))]

    def put(rows, re, im):
        ref[(*lead, rows, slice(0, BW))] = re
        ref[(*lead, rows, slice(BW, 2 * BW))] = im

    return get, put


def _interleave_rows(src_ref, dst_ref, regions=REGIONS):
    for base, seglen in regions:
        def body(j, carry, base=base, seglen=seglen):
            dst_ref[pl.ds(pl.multiple_of(base + j * NSEG, NSEG), NSEG), :] = src_ref[pl.ds(base + j, NSEG, stride=seglen), :]
            return carry

        lax.fori_loop(0, seglen, body, 0, unroll=8)


def _deinterleave_rows(src_ref, dst_ref, regions=REGIONS):
    for base, seglen in regions:
        def body(j, carry, base=base, seglen=seglen):
            dst_ref[pl.ds(base + j, NSEG, stride=seglen), :] = src_ref[pl.ds(pl.multiple_of(base + j * NSEG, NSEG), NSEG), :]
            return carry

        lax.fori_loop(0, seglen, body, 0, unroll=8)


def _s5_fwd(proj, dskip, lam, bmat, cmat):
    def kern(u_ref, dk_ref, lam_ref, b_ref, c_ref, s_ref, ssm_ref, ge_ref, up_ref, yp_ref):
        d = pl.program_id(1)

        @pl.when(d == 0)
        def _():
            _interleave_rows(u_ref, up_ref)

        bm = b_ref[0, 0].astype(BF16)
        for r0 in range(0, TA, RCH):
            s_ref[0, 0, r0:r0 + RCH, :] = jnp.dot(up_ref[r0:r0 + RCH, :].astype(BF16), bm, preferred_element_type=F32)
        lr = lam_ref[0, 0, :, 0:BW]
        li = lam_ref[0, 0, :, BW:2 * BW]
        zero = jnp.zeros((NSEG, BW), F32)
        get, put = _state_access(s_ref, (0, 0))
        mid = _seg_scan(get, put, T, CSEG, lr, li, d, (zero, zero))
        _seg_scan(get, put, 0, TSEG, lr, li, d, mid)
        cm = c_ref[0, 0].astype(BF16)
        for r0 in range(0, T, RCH):
            y = lax.dot_general(s_ref[0, 0, r0:r0 + RCH, :].astype(BF16), cm, (((1,), (1,)), ((), ())), preferred_element_type=F32)

            @pl.when(d == 0)
            def _(y=y, r0=r0):
                yp_ref[r0:r0 + RCH, :] = y + dk_ref[...] * up_ref[r0:r0 + RCH, :]

            @pl.when(d == 1)
            def _(y=y, r0=r0):
                yp_ref[r0:r0 + RCH, :] += y

        @pl.when(d == 1)
        def _():
            _deinterleave_rows(yp_ref, ssm_ref, REGIONS[:1])
            for r0 in range(0, T, RCH):
                ge_ref[r0:r0 + RCH, :] = _gelu(ssm_ref[r0:r0 + RCH, :]).astype(ge_ref.dtype)

    blk4 = lambda shape: pl.BlockSpec((1, 1) + shape, lambda b, d: (d, b, 0, 0))
    return pl.pallas_call(
        kern,
        name="s5_fwd",
        grid=(NBLK, 2),
        in_specs=[pl.BlockSpec((TA, UW), lambda b, d: (0, UCOL0 + b)), pl.BlockSpec((1, UW), lambda b, d: (0, b)),
                  blk4((8, 2 * BW)), blk4((UW, 2 * BW)), blk4((UW, 2 * BW))],
        out_specs=[blk4((TA, 2 * BW)), pl.BlockSpec((T, UW), lambda b, d: (0, b)), pl.BlockSpec((T, UW), lambda b, d: (0, b))],
        out_shape=[jax.ShapeDtypeStruct((2, NBLK, TA, 2 * BW), F32), jax.ShapeDtypeStruct((T, SW), F32),
                   jax.ShapeDtypeStruct((T, SW), BF16)],
        scratch_shapes=[pltpu.VMEM((TA, UW), F32), pltpu.VMEM((T, UW), F32)],
        compiler_params=_cparams(("parallel", "arbitrary")),
    )(proj, dskip, lam, bmat, cmat)


def _s5_bwd(d_ge, ssm, proj, dskip, states, lam, bmat, cmat):
    nt = (((1,), (1,)), ((), ()))
    tn = (((0,), (0,)), ((), ()))

    def kern(dge_ref, ssm_ref, u_ref, dk_ref, s_ref, lam_ref, b_ref, c_ref,
             du_ref, ddk_ref, dlam_ref, db_ref, dc_ref, g_ref, dua_ref, dssm_ref, up_ref, nat_ref):
        d = pl.program_id(1)

        @pl.when(d == 0)
        def _():
            ddk = jnp.zeros((1, UW), F32)
            for r0 in range(0, T, RCH):
                rs = slice(r0, r0 + RCH)
                _, pull = jax.vjp(_gelu, ssm_ref[rs, :])
                dssm = pull(dge_ref[rs, :])[0]
                nat_ref[rs, :] = dssm
                ddk = ddk + jnp.sum(dssm * u_ref[rs, :], axis=0, keepdims=True)
            ddk_ref[...] = ddk
            _interleave_rows(nat_ref, dssm_ref, REGIONS[:1])
            _interleave_rows(u_ref, up_ref)
            for r0 in range(0, T, RCH):
                dua_ref[r0:r0 + RCH, :] = dssm_ref[r0:r0 + RCH, :] * dk_ref[...]
            dua_ref[T:TA, :] = jnp.zeros((C, UW), F32)

        cm = c_ref[0, 0].astype(BF16)
        for r0 in range(0, T, RCH):
            g_ref[r0:r0 + RCH, :] = jnp.dot(dssm_ref[r0:r0 + RCH, :].astype(BF16), cm, preferred_element_type=F32)
        g_ref[T:TA, :] = jnp.zeros((C, 2 * BW), F32)
        lr = lam_ref[0, 0, :, 0:BW]
        li = lam_ref[0, 0, :, BW:2 * BW]
        zero = jnp.zeros((NSEG, BW), F32)
        get_g, put_g = _state_access(g_ref)

        get_s, _ = _state_access(s_ref, (0, 0))

        def dlam_fold(base, seglen, s_in):
            def rows(j):
                return pl.ds(pl.multiple_of(base + j * NSEG, NSEG), NSEG)

            jb = jnp.where(d == 0, 0, seglen - 1)
            jn = jnp.where(d == 0, seglen - 1, 0)
            sp = get_s(rows(jn))
            edge = (_shift_rows(sp[0], d, s_in[0]), _shift_rows(sp[1], d, s_in[1]))

            def fold(j, gr, gi, acc):
                jp = jnp.clip(jnp.where(d == 0, j - 1, j + 1), 0, seglen - 1)
                sr, si = get_s(rows(jp))
                sr = jnp.where(j == jb, edge[0], sr)
                si = jnp.where(j == jb, edge[1], si)
                return acc[0] + (gr * sr + gi * si), acc[1] + (gi * sr - gr * si)

            return fold

        r_mid = jnp.where(d == 0, TA - 1, T)
        s_mid = tuple(jnp.broadcast_to(t, (NSEG, BW)) for t in get_s(pl.ds(r_mid, 1)))
        mid, acc = _seg_scan(get_g, put_g, 0, TSEG, lr, -li, 1 - d, (zero, zero), dlam_fold(0, TSEG, s_mid), (zero, zero))
        _, acc = _seg_scan(get_g, put_g, T, CSEG, lr, -li, 1 - d, mid, dlam_fold(T, CSEG, (zero, zero)), acc)
        dlam_ref[0, 0, :, 0:BW] = acc[0]
        dlam_ref[0, 0, :, BW:2 * BW] = acc[1]

        bm = b_ref[0, 0].astype(BF16)
        db = jnp.zeros((UW, 2 * BW), F32)
        dc = jnp.zeros((UW, 2 * BW), F32)
        for r0 in range(0, TA, RCH):
            rs = slice(r0, r0 + RCH)
            g = g_ref[rs, :].astype(BF16)
            dua_ref[rs, :] += lax.dot_general(g, bm, nt, preferred_element_type=F32)
            db = db + lax.dot_general(up_ref[rs, :].astype(BF16), g, tn, preferred_element_type=F32)
            if r0 < T:
                dc = dc + lax.dot_general(dssm_ref[rs, :].astype(BF16), s_ref[0, 0, rs, :].astype(BF16), tn,
                                          preferred_element_type=F32)
        db_ref[0, 0] = db
        dc_ref[0, 0] = dc

        @pl.when(d == 1)
        def _():
            _deinterleave_rows(dua_ref, nat_ref)
            du_ref[...] = nat_ref[...].astype(du_ref.dtype)

    blk4 = lambda shape: pl.BlockSpec((1, 1) + shape, lambda b, d: (d, b, 0, 0))
    lat = pl.BlockSpec((T, UW), lambda b, d: (0, b))
    vec = pl.BlockSpec((1, UW), lambda b, d: (0, b))
    return pl.pallas_call(
        kern,
        name="s5_bwd",
        grid=(NBLK, 2),
        in_specs=[lat, lat, pl.BlockSpec((TA, UW), lambda b, d: (0, UCOL0 + b)), vec,
                  blk4((TA, 2 * BW)), blk4((8, 2 * BW)), blk4((UW, 2 * BW)), blk4((UW, 2 * BW))],
        out_specs=[pl.BlockSpec((TA, UW), lambda b, d: (0, b)), vec, blk4((8, 2 * BW)), blk4((UW, 2 * BW)), blk4((UW, 2 * BW))],
        out_shape=[jax.ShapeDtypeStruct((TA, SW), BF16), jax.ShapeDtypeStruct((1, SW), F32),
                   jax.ShapeDtypeStruct((2, NBLK, 8, 2 * BW), F32),
                   jax.ShapeDtypeStruct((2, NBLK, UW, 2 * BW), F32), jax.ShapeDtypeStruct((2, NBLK, UW, 2 * BW), F32)],
        scratch_shapes=[pltpu.VMEM((TA, 2 * BW), F32), pltpu.VMEM((TA, UW), F32), pltpu.VMEM((T, UW), F32),
                        pltpu.VMEM((TA, UW), F32), pltpu.VMEM((TA, UW), F32)],
        compiler_params=_cparams(("parallel", "arbitrary")),
    )(d_ge, ssm, proj, dskip, states, lam, bmat, cmat)


TR = 256
TN_WIDE = 1024


def _vjp_rows(f, primals, cots, n_row):
    _, pull = jax.vjp(f, *primals)
    g = pull(cots)
    return list(g[:n_row]), list(g[n_row:])


class _GradDict(dict):
    def __init__(self, on_set=None):
        super().__init__()
        self._on_set = on_set
        self.tokens = {}

    def __setitem__(self, key, value):
        super().__setitem__(key, value)
        if self._on_set is not None:
            self._on_set(self)

    def order(self, key):
        return self.tokens.get(key, self.get(key))

    def finish(self, key, after):
        if self.on_finish is None:
            return ()
        return (self.on_finish(key, after),)

    on_finish = None


def _local_step(x, ctx, tgt, mod_lat, mod_ctx, wb, sp, on_grad=None, on_loss=None, on_finish=None, on_early=None):
    sh1, sc1, g1, sh2, sc2, g2 = [mod_lat[:, i * D:(i + 1) * D] for i in range(6)]
    csh1, csc1 = mod_ctx[:, 0:D], mod_ctx[:, D:2 * D]
    tabs = _rope_tables()
    sink = sp["attn_sink"].reshape(1, NH)
    dskip = sp["ssm_d"].reshape(1, SW)
    lg_mix, lb_mix = sp["ln_mix_g"].reshape(1, D), sp["ln_mix_b"].reshape(1, D)
    lg_mlp, lb_mlp = sp["ln_mlp_g"].reshape(1, D), sp["ln_mlp_b"].reshape(1, D)
    b1, b2 = sp["b_mlp1"].reshape(1, DFF), sp["b_mlp2"].reshape(1, D)
    s5_names = ("ssm_a_re", "ssm_a_im", "ssm_log_dt", "ssm_b_re", "ssm_b_im", "ssm_c_re", "ssm_c_im")
    (lam, bmat, cmat), s5_pull = jax.vjp(_s5_prep, *[sp[n] for n in s5_names])

    def ln_mod2(rv, vv):
        h = _f_ln_mod(rv[0], vv[0], vv[1])
        return [h, h], []

    h_lat, h_lat_t = _rowwise(ln_mod2, [(x, D, 0, 0)], [sc1, sh1], [(D, BF16), (D, BF16, True)], [], nrows=T, tr=TR, name="ln1_lat")
    h_ctx, h_ctx_t = _rowwise(ln_mod2, [(ctx, D, 0, 0)], [csc1, csh1], [(D, BF16), (D, BF16, True)], [], nrows=C, tr=TR,
                              name="ln1_ctx")
    h1 = jnp.concatenate([h_lat, h_ctx], 0)
    h1_t = jnp.concatenate([h_lat_t, h_ctx_t], 1)
    proj = _matmul(h1, wb["w_in"], mode="nn", name="proj", tm=768, tn=TN_WIDE)
    attn = _attn_fwd(proj, sink, tabs)
    states, ssm, ge = _s5_fwd(proj, dskip, lam, bmat, cmat)
    z = _matmul(ge, wb["w_glu"], mode="nn", name="glu_mm", tm=1024, tn=1024)

    def glu_act(rv, vv):
        return [_f_glu(rv[0])], []

    glu, = _rowwise(glu_act, [(z, 2 * SW, 0, 0)], [], [(SW, BF16)], [], nrows=T, tr=TR, name="glu_act")
    attn_d = _matmul(attn, wb["w_attn_up"], mode="nn", name="attn_up", tm=1024, tn=512)
    ssm_d = _matmul(glu, wb["w_ssm_up"], mode="nn", name="ssm_up", tm=1024, tn=512)
    ga_cb, gs_cb = (QW + 2 * KVW + SW) // D, (QW + 2 * KVW + SW) // D + 1

    def mix(rv, vv):
        m_ = _f_mix(*rv)
        return [m_, m_], []

    mixv, mix_t = _rowwise(mix, [(proj, D, ga_cb, 0), (proj, D, gs_cb, 0), (attn_d, D, 0, 0), (ssm_d, D, 0, 0)], [],
                           [(D, BF16), (D, BF16, True)], [], nrows=T, tr=TR, name="mix")
    y = _matmul(mixv, wb["w_out"], mode="nn", name="out_proj", tm=1024, tn=TN_WIDE)

    def post1(rv, vv):
        x1, h2 = _f_post1(rv[0], rv[1], *vv)
        return [x1, h2, h2], []

    x1, h2, h2_t = _rowwise(post1, [(x, D, 0, 0), (y, D, 0, 0)], [g1, lg_mix, lb_mix, sc2, sh2],
                            [(D, F32), (D, BF16), (D, BF16, True)], [], nrows=T, tr=TR, name="post1")

    def relu_sq(acc):
        r = jnp.maximum(acc, 0.0)
        return r, r * r, r * r

    r_act, act, act_t = _matmul(h2, wb["w_mlp1"], mode="nn", name="mlp1", tm=1024, tn=TN_WIDE, bias=b1,
                                out_dtypes=(BF16, BF16, BF16), out_t=(False, False, True), epilogue=relu_sq)
    mlp = _matmul(act, wb["w_mlp2"], mode="nn", name="mlp2", tm=512, tn=512)

    def loss_fb(rv, vv):
        x1_t, mlp_t, tgt_t = rv
        g2_v, lg_v, lb_v, b2_v = vv
        f = lambda a, m, g, p, q, b: _f_loss(a, m, tgt_t, g, p, q, b)
        val, grads = jax.value_and_grad(f, argnums=(0, 1, 2, 3, 4, 5))(x1_t, mlp_t, g2_v, lg_v, lb_v, b2_v)
        dx1, dmlp, dg2, dlg, dlb, db2 = grads
        return [dx1, dmlp], [jnp.reshape(val, (1, 1)), dg2, dlg, dlb, db2]

    dx1_a, d_mlp, loss_p, d_g2, d_lg_mlp, d_lb_mlp, d_b2 = _rowwise(
        loss_fb, [(x1, D, 0, 0), (mlp, D, 0, 0), (tgt, D, 0, 0)], [g2, lg_mlp, lb_mlp, b2],
        [(D, F32), (D, BF16)], [(1, 1), (1, D), (1, D), (1, D), (1, D)], nrows=T, tr=TR, name="loss_fb")

    gw = _GradDict(on_grad)
    gw.on_finish = on_finish
    loss_done = () if on_loss is None else (on_loss(loss_p),)
    gw["w_mlp2"] = _matmul(act_t, d_mlp, mode="nn", name="dw_mlp2", out_dtypes=(BF16,), tm=1024, tn=TN_WIDE, after=loss_done)
    da, = (_matmul(d_mlp, wb["w_mlp2"], mode="nt", name="d_act", out_dtypes=(BF16,), tm=1024, tn=TN_WIDE,
                   extras=(r_act,), epilogue=lambda acc, r: (acc * (2.0 * r.astype(F32)),), after=(gw.order("w_mlp2"),)),)
    pin = gw.finish("w_mlp2", da)
    ones = jnp.ones((8, T), BF16)
    d_b1 = _matmul(ones, da, mode="nn", name="db_mlp1", tm=8, tn=2048)[0:1]
    gw["w_mlp1"] = _matmul(h2_t, da, mode="nn", name="dw_mlp1", out_dtypes=(BF16,), tm=1024, tn=TN_WIDE, after=pin)
    dh2 = _matmul(da, wb["w_mlp1"], mode="nt", name="d_h2", tm=512, tn=512, after=(gw.order("w_mlp1"),))

    def post1_b(rv, vv):
        x_t, y_t, dx1_t, dh2_t = rv
        gr, gv = _vjp_rows(_f_post1, (x_t, y_t, *vv), (dx1_t, dh2_t), 2)
        return [gr[0], gr[1]], gv

    dx_a, dy, d_g1, d_lg_mix, d_lb_mix, d_sc2, d_sh2 = _rowwise(
        post1_b, [(x, D, 0, 0), (y, D, 0, 0), (dx1_a, D, 0, 0), (dh2, D, 0, 0)], [g1, lg_mix, lb_mix, sc2, sh2],
        [(D, F32), (D, BF16)], [(1, D)] * 5, nrows=T, tr=TR, name="post1_bwd")
    gw["w_out"] = _matmul(mix_t, dy, mode="nn", name="dw_out", out_dtypes=(BF16,), tm=1024, tn=TN_WIDE)
    dmix = _matmul(dy, wb["w_out"], mode="nt", name="d_mix", tm=1024, tn=TN_WIDE, after=(gw.order("w_out"),))

    def mix_b(rv, vv):
        gr, _ = _vjp_rows(_f_mix, tuple(rv[:4]), rv[4], 4)
        return gr, []

    d_ga, d_gs, d_attn_d, d_ssm_d = _rowwise(
        mix_b, [(proj, D, ga_cb, 0), (proj, D, gs_cb, 0), (attn_d, D, 0, 0), (ssm_d, D, 0, 0), (dmix, D, 0, 0)], [],
        [(D, BF16)] * 4, [], nrows=T, tr=TR, name="mix_bwd", pad_rows=C, pad_outs=(0, 1))
    pin = gw.finish("w_mlp1", d_ga)
    gw["w_attn_up"] = _matmul(attn, d_attn_d, mode="tn", name="dw_attn_up", out_dtypes=(BF16,), tm=512, tn=1024, tk=1024, after=pin)
    d_attn = _matmul(d_attn_d, wb["w_attn_up"], mode="nt", name="d_attn", out_dtypes=(BF16,), tm=1024, tn=512)
    gw["w_ssm_up"] = _matmul(glu, d_ssm_d, mode="tn", name="dw_ssm_up", out_dtypes=(BF16,), tm=512, tn=1024, tk=1024)
    d_glu = _matmul(d_ssm_d, wb["w_ssm_up"], mode="nt", name="d_glu", tm=1024, tn=512, after=(gw.order("w_attn_up"), gw.order("w_ssm_up")))

    def glu_b(rv, vv):
        gr, _ = _vjp_rows(_f_glu, (rv[0],), rv[1], 1)
        return gr, []

    dz, = _rowwise(glu_b, [(z, 2 * SW, 0, 0), (d_glu, SW, 0, 0)], [], [(2 * SW, BF16)], [], nrows=T, tr=TR, name="glu_bwd")
    gw["w_glu"] = _matmul(ge, dz, mode="tn", name="dw_glu", out_dtypes=(BF16,), tm=512, tn=1024, tk=1024)
    d_ge = _matmul(dz, wb["w_glu"], mode="nt", name="d_ge", tm=1024, tn=512, after=(gw.order("w_glu"),))

    du_all, d_dskip, dlam, dbmat, dcmat = _s5_bwd(d_ge, ssm, proj, dskip, states, lam, bmat, cmat)
    s5_grads = s5_pull((dlam, dbmat, dcmat))
    early = dict(zip(s5_names, s5_grads), ssm_d=d_dskip)
    if on_early is not None:
        on_early(early)
    pin = gw.finish("w_glu", du_all)

    dq, dk, dv, dsink = _attn_bwd(proj, d_attn, sink, tabs)
    dproj = jnp.concatenate([dq, dk, dv, du_all, d_ga, d_gs], 1)
    gw["w_in"] = _matmul(h1_t, dproj, mode="nn", name="dw_in", out_dtypes=(BF16,), tm=1024, tn=TN_WIDE, after=pin)
    pin = gw.finish("w_in", gw["w_in"])
    dh1 = _matmul(dproj, wb["w_in"], mode="nt", name="d_h1", tm=768, tn=512, after=pin)

    def ln1_b(rv, vv):
        x_t, dh_t, dxa_t = rv
        gr, gv = _vjp_rows(_f_ln_mod, (x_t, vv[0], vv[1]), dh_t, 1)
        return [gr[0] + dxa_t], gv

    grad_x, d_sc1, d_sh1 = _rowwise(ln1_b, [(x, D, 0, 0), (dh1, D, 0, 0), (dx_a, D, 0, 0)], [sc1, sh1],
                                    [(D, F32)], [(1, D), (1, D)], nrows=T, tr=TR, name="ln1_lat_bwd")

    def ln1c_b(rv, vv):
        _, gv = _vjp_rows(_f_ln_mod, (rv[0], vv[0], vv[1]), rv[1], 1)
        return [], gv

    d_csc1, d_csh1 = _rowwise(ln1c_b, [(ctx, D, 0, 0), (dh1, D, 0, T // TR)], [csc1, csh1],
                              [], [(1, D), (1, D)], nrows=C, tr=TR, name="ln1_ctx_bwd")

    d_mod_lat = jnp.concatenate([d_sh1, d_sc1, d_g1, d_sh2, d_sc2, d_g2], 1)
    zv = jnp.zeros((1, D), F32)
    d_mod_ctx = jnp.concatenate([d_csh1, d_csc1, zv, zv, zv, zv], 1)
    gs = {n: g for n, g in zip(s5_names, s5_grads)}
    gs["attn_sink"] = dsink[:, 0]
    gs["ssm_d"] = d_dskip
    gs["ln_mix_g"], gs["ln_mix_b"] = d_lg_mix, d_lb_mix
    gs["ln_mlp_g"], gs["ln_mlp_b"] = d_lg_mlp, d_lb_mlp
    gs["b_mlp1"], gs["b_mlp2"] = d_b1, d_b2
    return loss_p, grad_x, d_mod_lat, d_mod_ctx, gw, gs


def _my_pos():
    return lax.axis_index("x"), lax.axis_index("y"), lax.axis_index("c")


def _flip(p, bit):
    return 1 - p if bit else p


def _peer(pos, k):
    x, y, c = pos
    return (_flip(x, (k >> 2) & 1), _flip(y, (k >> 1) & 1), _flip(c, k & 1))


def _lin(pos):
    return 4 * pos[0] + 2 * pos[1] + pos[2]


def _allgather_small(v, name):
    r, w = v.shape

    def body(v_ref, out_ref, send_sems, recv_sems, local_sem):
        me = _my_pos()
        mine = pltpu.make_async_copy(v_ref, out_ref.at[_lin(me)], local_sem)
        mine.start()
        sends = []
        for k in range(1, N_DEV):
            cp = pltpu.make_async_remote_copy(src_ref=v_ref, dst_ref=out_ref.at[_lin(me)], send_sem=send_sems.at[k - 1],
                                              recv_sem=recv_sems.at[k - 1], device_id=_peer(me, k), device_id_type=MESH)
            cp.start()
            sends.append(cp)
        for k in range(1, N_DEV):
            peer = _peer(me, k)
            pltpu.make_async_remote_copy(src_ref=v_ref, dst_ref=out_ref.at[_lin(peer)], send_sem=send_sems.at[k - 1],
                                         recv_sem=recv_sems.at[k - 1], device_id=peer, device_id_type=MESH).wait_recv()
        for cp in sends:
            cp.wait_send()
        mine.wait()

    return pl.pallas_call(
        body,
        name=name,
        out_shape=jax.ShapeDtypeStruct((N_DEV, r, w), v.dtype),
        in_specs=[pl.BlockSpec(memory_space=pltpu.VMEM)],
        out_specs=pl.BlockSpec(memory_space=pltpu.VMEM),
        scratch_shapes=[pltpu.SemaphoreType.DMA((N_DEV - 1,)), pltpu.SemaphoreType.DMA((N_DEV - 1,)), pltpu.SemaphoreType.DMA],
        compiler_params=pltpu.CompilerParams(vmem_limit_bytes=VMEM_LIMIT_BYTES),
    )(v)


def _block_of(ref, kind, idx, n):
    start = pl.multiple_of(idx * n, 128)
    if kind == "col":
        return ref.at[:, pl.ds(start, n)]
    return ref.at[pl.ds(start, n), :]


def _handshake(peers):
    barrier = pltpu.get_barrier_semaphore()
    for peer in peers:
        pl.semaphore_signal(barrier, inc=1, device_id=peer, device_id_type=MESH)
    pl.semaphore_wait(barrier, len(peers))


def _allgather_weights_seq(shards, kinds, name, collective_id):
    nt = len(shards)
    hbm = pltpu.MemorySpace.HBM
    ins = [jax.new_ref(s, memory_space=hbm) for s in shards]
    outs = []
    for s, kind in zip(shards, kinds):
        k, n = s.shape
        shape = (k, n * N_DEV) if kind == "col" else (k * N_DEV, n)
        outs.append(jax.empty_ref(jax.ShapeDtypeStruct(shape, s.dtype), memory_space=hbm))

    @functools.partial(
        pl.kernel, mesh=plsc.ScalarSubcoreMesh(axis_name="seq", num_cores=1), name=name,
        scratch_types=(pltpu.SemaphoreType.DMA((nt, N_DEV - 1)), pltpu.SemaphoreType.DMA((nt, N_DEV - 1)),
                       pltpu.SemaphoreType.DMA((nt,))),
        compiler_params=pltpu.CompilerParams(collective_id=collective_id))
    def launch(send_sems, recv_sems, local_sems):
        x, y, c = _my_pos()
        me, sibling = (x, y, c), (x, y, 1 - c)
        chips = [(1 - x, y), (x, 1 - y), (1 - x, 1 - y)]
        _handshake([sibling] + [(*chip, c) for chip in chips])

        def blk(t, pos):
            n = shards[t].shape[1] if kinds[t] == "col" else shards[t].shape[0]
            return _block_of(outs[t], kinds[t], _lin(pos), n)

        def copy(t, k, block, to, src=None):
            return pltpu.make_async_remote_copy(src_ref=blk(t, block) if src is None else src, dst_ref=blk(t, block),
                                                send_sem=send_sems.at[t, k], recv_sem=recv_sems.at[t, k],
                                                device_id=to, device_id_type=MESH)

        local, sends = [], []
        for t in range(nt):
            mine = pltpu.make_async_copy(ins[t], blk(t, me), local_sems.at[t])
            mine.start()
            local.append(mine)
            first = [copy(t, 0, me, sibling, src=ins[t])]
            first += [copy(t, 1 + j, me, (*chip, c), src=ins[t]) for j, chip in enumerate(chips)]
            for cp in first:
                cp.start()
            sends += first
        for t in range(nt):
            for j, chip in enumerate(chips):
                copy(t, 1 + j, (*chip, c), me).wait_recv()
                fwd = copy(t, 4 + j, (*chip, c), sibling)
                fwd.start()
                sends.append(fwd)
        for t in range(nt):
            copy(t, 0, sibling, me).wait_recv()
            for j, chip in enumerate(chips):
                copy(t, 4 + j, (*chip, 1 - c), me).wait_recv()
        for cp in sends:
            cp.wait_send()
        for cp in local:
            cp.wait()

    launch()
    return [o[...] for o in outs]


def _allgather_small_seq(v, name, collective_id):
    hbm = pltpu.MemorySpace.HBM
    src = jax.new_ref(v, memory_space=hbm)
    out = jax.empty_ref(jax.ShapeDtypeStruct((N_DEV,) + v.shape, v.dtype), memory_space=hbm)

    @functools.partial(
        pl.kernel, mesh=plsc.ScalarSubcoreMesh(axis_name="seq", num_cores=1), name=name,
        scratch_types=(pltpu.SemaphoreType.DMA((N_DEV - 1,)), pltpu.SemaphoreType.DMA((N_DEV - 1,)), pltpu.SemaphoreType.DMA),
        compiler_params=pltpu.CompilerParams(collective_id=collective_id))
    def launch(send_sems, recv_sems, local_sem):
        me = _my_pos()
        _handshake([_peer(me, k) for k in range(1, N_DEV)])
        mine = pltpu.make_async_copy(src, out.at[_lin(me)], local_sem)
        mine.start()
        sends = []
        for k in range(1, N_DEV):
            cp = pltpu.make_async_remote_copy(src_ref=src, dst_ref=out.at[_lin(me)], send_sem=send_sems.at[k - 1],
                                              recv_sem=recv_sems.at[k - 1], device_id=_peer(me, k), device_id_type=MESH)
            cp.start()
            sends.append(cp)
        for k in range(1, N_DEV):
            peer = _peer(me, k)
            pltpu.make_async_remote_copy(src_ref=src, dst_ref=out.at[_lin(peer)], send_sem=send_sems.at[k - 1],
                                         recv_sem=recv_sems.at[k - 1], device_id=peer, device_id_type=MESH).wait_recv()
        for cp in sends:
            cp.wait_send()
        mine.wait()

    launch()
    return out[...]


N_CHIP = N_DEV // 2


def _chip_of(pos):
    return 2 * pos[0] + pos[1]


def _pair_exchange_seq(grads, kinds, name, collective_id):
    nt = len(grads)
    hbm = pltpu.MemorySpace.HBM
    shard_shapes = _shard_shapes(grads, kinds)
    ins = [jax.new_ref(g, memory_space=hbm) for g in grads]
    outs = [jax.empty_ref(jax.ShapeDtypeStruct((N_CHIP,) + s, g.dtype), memory_space=hbm) for s, g in zip(shard_shapes, grads)]

    @functools.partial(
        pl.kernel, mesh=plsc.ScalarSubcoreMesh(axis_name="seq", num_cores=1), name=name,
        scratch_types=(pltpu.SemaphoreType.DMA((nt, N_CHIP)), pltpu.SemaphoreType.DMA((nt, N_CHIP))),
        compiler_params=pltpu.CompilerParams(collective_id=collective_id))
    def launch(send_sems, recv_sems):
        x, y, c = _my_pos()
        sibling = (x, y, 1 - c)
        _handshake([sibling])
        copies = []
        for t in range(nt):
            n = shard_shapes[t][1] if kinds[t] == "col" else shard_shapes[t][0]
            for q in range(N_CHIP):
                cp = pltpu.make_async_remote_copy(src_ref=_block_of(ins[t], kinds[t], 2 * q + (1 - c), n), dst_ref=outs[t].at[q],
                                                  send_sem=send_sems.at[t, q], recv_sem=recv_sems.at[t, q],
                                                  device_id=sibling, device_id_type=MESH)
                cp.start()
                copies.append(cp)
        for cp in copies:
            cp.wait_recv()
        for cp in copies:
            cp.wait_send()

    launch()
    return [o[...] for o in outs]


def _pair_add(g, half, kind, name, after=()):
    nq, k, ns = half.shape
    tr = min(k, 512)
    c_idx = lax.axis_index("c").astype(jnp.int32).reshape(1)
    if kind == "col":
        g_spec = pl.BlockSpec((tr, ns), lambda q, i, c_ref: (i, 2 * q + c_ref[0]))
    else:
        g_spec = pl.BlockSpec((tr, ns), lambda q, i, c_ref: ((2 * q + c_ref[0]) * (k // tr) + i, 0))
    n_after = len(after)

    def kern(c_ref, g_ref, h_ref, *rest):
        o_ref = rest[n_after]
        o_ref[0] = (g_ref[...].astype(F32) + h_ref[0].astype(F32)).astype(o_ref.dtype)

    return pl.pallas_call(
        kern,
        name=name,
        grid_spec=pltpu.PrefetchScalarGridSpec(
            num_scalar_prefetch=1,
            grid=(nq, k // tr),
            in_specs=[g_spec, pl.BlockSpec((1, tr, ns), lambda q, i, c_ref: (q, i, 0))] + [pl.BlockSpec(memory_space=pl.ANY)] * n_after,
            out_specs=pl.BlockSpec((1, tr, ns), lambda q, i, c_ref: (q, i, 0)),
        ),
        out_shape=jax.ShapeDtypeStruct(half.shape, half.dtype),
        compiler_params=_cparams(("parallel", "parallel")),
    )(c_idx, g, half, *after)


def _chip_exchange_seq(psums, name, collective_id):
    nt = len(psums)
    hbm = pltpu.MemorySpace.HBM
    ins = [jax.new_ref(s, memory_space=hbm) for s in psums]
    outs = [jax.empty_ref(jax.ShapeDtypeStruct(s.shape, s.dtype), memory_space=hbm) for s in psums]

    @functools.partial(
        pl.kernel, mesh=plsc.ScalarSubcoreMesh(axis_name="seq", num_cores=1), name=name,
        scratch_types=(pltpu.SemaphoreType.DMA((nt, N_CHIP - 1)), pltpu.SemaphoreType.DMA((nt, N_CHIP - 1)),
                       pltpu.SemaphoreType.DMA((nt,))),
        compiler_params=pltpu.CompilerParams(collective_id=collective_id))
    def launch(send_sems, recv_sems, local_sems):
        me = _my_pos()
        peers = [_peer(me, k) for k in (2, 4, 6)]
        _handshake(peers)
        mine = _chip_of(me)
        local, sends = [], []
        for t in range(nt):
            cp = pltpu.make_async_copy(ins[t].at[mine], outs[t].at[mine], local_sems.at[t])
            cp.start()
            local.append(cp)
            for j, peer in enumerate(peers):
                cp = pltpu.make_async_remote_copy(src_ref=ins[t].at[_chip_of(peer)], dst_ref=outs[t].at[mine],
                                                  send_sem=send_sems.at[t, j], recv_sem=recv_sems.at[t, j],
                                                  device_id=peer, device_id_type=MESH)
                cp.start()
                sends.append(cp)
        for t in range(nt):
            for j, peer in enumerate(peers):
                pltpu.make_async_remote_copy(src_ref=ins[t].at[mine], dst_ref=outs[t].at[_chip_of(peer)],
                                             send_sem=send_sems.at[t, j], recv_sem=recv_sems.at[t, j],
                                             device_id=peer, device_id_type=MESH).wait_recv()
        for cp in sends:
            cp.wait_send()
        for cp in local:
            cp.wait()

    launch()
    return [o[...] for o in outs]


def _shard_shapes(grads, kinds):
    return [(g.shape[0], g.shape[1] // N_DEV) if kind == "col" else (g.shape[0] // N_DEV, g.shape[1]) for g, kind in zip(grads, kinds)]


def _adam(g_slots, w, m, v, *, tr, name, after=()):
    ns, r, wd = g_slots.shape
    tr = min(tr, r)
    assert r % tr == 0, (name, r, tr)
    n_after = len(after)

    def kern(g_ref, w_ref, m_ref, v_ref, *rest):
        go_ref, d_ref, mo_ref, vo_ref = rest[n_after:]
        g = g_ref[0].astype(F32)
        for s in range(1, ns):
            g = g + g_ref[s].astype(F32)
        delta, m_new, v_new = _adam_update(g, w_ref[...], m_ref[...], v_ref[...])
        go_ref[...] = g
        d_ref[...] = delta
        mo_ref[...] = m_new
        vo_ref[...] = v_new

    tile = pl.BlockSpec((tr, wd), lambda i: (i, 0))
    return pl.pallas_call(
        kern,
        name=name,
        grid=(r // tr,),
        in_specs=[pl.BlockSpec((ns, tr, wd), lambda i: (0, i, 0)), tile, tile, tile] + [pl.BlockSpec(memory_space=pl.ANY)] * n_after,
        out_specs=[tile] * 4,
        out_shape=[jax.ShapeDtypeStruct((r, wd), F32)] * 4,
        compiler_params=_cparams(("parallel",)),
    )(g_slots, w, m, v, *after)


def _adam_update(g, w, m, v):
    m_new = ADAM_B1 * m + (1.0 - ADAM_B1) * g
    v_new = ADAM_B2 * v + (1.0 - ADAM_B2) * (g * g)
    m_hat = m_new / (1.0 - ADAM_B1 ** ADAM_STEP)
    v_hat = v_new / (1.0 - ADAM_B2 ** ADAM_STEP)
    return -ADAM_LR * (m_hat / (jnp.sqrt(v_hat) + ADAM_EPS) + ADAM_WD * w), m_new, v_new


def _lane_offsets(sizes):
    offs, o = [], 0
    for n in sizes:
        offs.append(o)
        o += -(-n // LANES) * LANES
    return offs, o


def _pack_lanes(parts):
    cols = []
    for p_ in parts:
        flat = p_.reshape(1, -1).astype(F32)
        cols.append(jnp.pad(flat, ((0, 0), (0, (-flat.shape[1]) % LANES))))
    return jnp.concatenate(cols, 1)


def _adam_lanes(g_slots, ws, ms, vs, *, name, after=()):
    ns = g_slots.shape[0]
    npar, n_after = len(ws), len(after)
    sizes = [w.shape[1] for w in ws]
    offs, _ = _lane_offsets(sizes)

    def kern(g_ref, *refs):
        w_refs, m_refs, v_refs = refs[:npar], refs[npar:2 * npar], refs[2 * npar:3 * npar]
        outs = refs[3 * npar + n_after:]
        g_all = g_ref[0]
        for s in range(1, ns):
            g_all = g_all + g_ref[s]
        for j in range(npar):
            g = g_all[:, offs[j]:offs[j] + sizes[j]]
            delta, m_new, v_new = _adam_update(g, w_refs[j][...], m_refs[j][...], v_refs[j][...])
            outs[4 * j][...] = g
            outs[4 * j + 1][...] = delta
            outs[4 * j + 2][...] = m_new
            outs[4 * j + 3][...] = v_new

    vmem = pl.BlockSpec(memory_space=pltpu.VMEM)
    res = pl.pallas_call(
        kern,
        name=name,
        in_specs=[vmem] * (1 + 3 * npar) + [pl.BlockSpec(memory_space=pl.ANY)] * n_after,
        out_specs=[vmem] * (4 * npar),
        out_shape=[jax.ShapeDtypeStruct((1, n), F32) for n in sizes for _ in range(4)],
        compiler_params=pltpu.CompilerParams(vmem_limit_bytes=VMEM_LIMIT_BYTES),
    )(g_slots, *ws, *ms, *vs, *after)
    return [tuple(res[4 * j:4 * j + 4]) for j in range(npar)]


SMALL = ("c_ctx", "b_ada", "attn_sink", "ssm_a_re", "ssm_a_im", "ssm_log_dt", "ssm_b_re", "ssm_b_im", "ssm_c_re", "ssm_c_im",
         "ssm_d", "ln_mix_g", "ln_mix_b", "b_mlp1", "b_mlp2", "ln_mlp_g", "ln_mlp_b")
BIG = ("w_in", "w_glu", "w_attn_up", "w_ssm_up", "w_out", "w_mlp1", "w_mlp2")
BIG_KIND = ("col", "col", "col", "col", "row", "col", "row")
AG_GROUPS = (("w_in",), ("w_glu", "w_attn_up", "w_ssm_up", "w_out"), ("w_mlp1",), ("w_mlp2",))
AG_COLLECTIVE_ID0 = 1
RS_GROUPS = (("w_mlp2",), ("w_mlp1",), ("w_out", "w_attn_up", "w_ssm_up", "w_glu"), ("w_in",))
RS_COLLECTIVE_ID0 = AG_COLLECTIVE_ID0 + len(AG_GROUPS)
SMALL_EARLY = ("ssm_a_re", "ssm_a_im", "ssm_log_dt", "ssm_b_re", "ssm_b_im", "ssm_c_re", "ssm_c_im", "ssm_d")
SMALL_LATE = tuple(n for n in SMALL if n not in SMALL_EARLY)
SMALL_COLLECTIVE_ID0 = RS_COLLECTIVE_ID0 + 2 * len(RS_GROUPS)
LANES = 128


def _pack(parts):
    rows = []
    for p in parts:
        flat = p.reshape(-1).astype(F32)
        pad = (-flat.shape[0]) % LANES
        rows.append(jnp.pad(flat, (0, pad)).reshape(-1, LANES))
    packed = jnp.concatenate(rows, 0)
    return jnp.pad(packed, ((0, (-packed.shape[0]) % 8), (0, 0)))


def _unpack(packed, shapes):
    out, r0 = [], 0
    for s in shapes:
        n = math.prod(s)
        nr = -(-n // LANES)
        out.append(packed[r0:r0 + nr].reshape(-1)[:n].reshape(s))
        r0 += nr
    return out


WEIGHTS = ("c_ctx", "w_ada", "b_ada", "w_in", "attn_sink", "ssm_a_re", "ssm_a_im", "ssm_log_dt", "ssm_b_re", "ssm_b_im",
           "ssm_c_re", "ssm_c_im", "ssm_d", "w_glu", "w_attn_up", "w_ssm_up", "w_out", "ln_mix_g", "ln_mix_b", "w_mlp1",
           "b_mlp1", "w_mlp2", "b_mlp2", "ln_mlp_g", "ln_mlp_b")
ADA_COLS = 6 * D // N_DEV


def _step(x, c, ctx, loss_target, p, m, v):
    me = _lin(_my_pos())
    x2, ctx2, tgt2 = x[0], ctx[0], loss_target[0]

    wb = {}
    for gi, group in enumerate(AG_GROUPS):
        full = _allgather_weights_seq([p[n][0].astype(BF16) for n in group], [BIG_KIND[BIG.index(n)] for n in group],
                                      "allgather_seq%d" % gi, AG_COLLECTIVE_ID0 + gi)
        wb.update(zip(group, full))

    c_all = _allgather_small(jnp.broadcast_to(c, (8, D)), "gather_c")[:, 0, :]
    cc = p["c_ctx"].reshape(1, D)
    s_in = jnp.concatenate([c_all, cc, jnp.zeros((7, D), F32)], 0)
    s_act, = _rowwise(lambda rv, vv: ([_silu(rv[0])], []), [(s_in, D, 0, 0)], [], [(D, F32)], [], nrows=16, tr=16, name="silu_c")
    b_mine = lax.dynamic_slice_in_dim(p["b_ada"], me * ADA_COLS, ADA_COLS, axis=1)
    mod_part = _matmul(s_act, p["w_ada"][0], mode="nn", name="ada_fwd", tm=16, tn=512, bias=b_mine)
    mod_all = _allgather_small(mod_part, "gather_mod")
    mod_lat = lax.dynamic_index_in_dim(mod_all, me, axis=1, keepdims=False).reshape(1, 6 * D)
    mod_ctx = mod_all[:, 8, :].reshape(1, 6 * D)

    sp = {n: p[n][0] for n in SMALL if n not in ("c_ctx", "b_ada")}
    recv, halves = {}, {}

    def on_grad(gw):
        for gi, group in enumerate(RS_GROUPS):
            if gi not in halves and all(n in gw for n in group):
                kinds = [BIG_KIND[BIG.index(n)] for n in group]
                halves[gi] = (dict(gw), _pair_exchange_seq([gw[n] for n in group], kinds, "pair_exchange%d" % gi, RS_COLLECTIVE_ID0 + 2 * gi))

    def on_finish(key, after):
        gi = [i for i, group in enumerate(RS_GROUPS) if key in group][0]
        group = RS_GROUPS[gi]
        grads, half = halves[gi]
        prev = tuple(recv[n] for n in RS_GROUPS[gi - 1][:1]) if gi else ()
        if gi == len(RS_GROUPS) - 1:
            prev += (small["early"],)
        psums =[_pair_add(grads[n], h, BIG_KIND[BIG.index(n)], "pair_add_" + n, after=(after,) + prev) for n, h in zip(group, half)]
        recv.update(zip(group, _chip_exchange_seq(psums, "chip_exchange%d" % gi, RS_COLLECTIVE_ID0 + 2 * gi + 1)))
        return psums[-1]

    small = {}

    def on_early(gs_early):
        small["early"] = _allgather_small_seq(_pack([gs_early[n] for n in SMALL_EARLY]), "gather_small_early", SMALL_COLLECTIVE_ID0)

    total = {}

    def on_loss(loss_p):
        total["loss"] = lax.psum(loss_p[0, 0], ("x", "y", "c"))
        return total["loss"].reshape(1, 1)

    loss_p, grad_x, d_mod_lat, d_mod_ctx, gw, gs = _local_step(x2, ctx2, tgt2, mod_lat, mod_ctx, wb, sp, on_grad, on_loss, on_finish, on_early)

    g_early = small["early"]
    res = {}
    last = ()

    def adam_small(names, g_pack, tag, after):
        sm = _adam(g_pack, _pack([p[n] for n in names]), _pack([m[n] for n in names]), _pack([v[n] for n in names]),
                   tr=g_pack.shape[1], name="adam_small_" + tag, after=after)
        shapes = [p[n].shape for n in names]
        for j, outs in enumerate(zip(*[_unpack(a, shapes) for a in sm])):
            res[names[j]] = outs
        return (sm[0],)

    for gi, group in enumerate(RS_GROUPS):
        if gi == len(RS_GROUPS) - 1:
            last = adam_small(SMALL_EARLY, g_early, "early", last)
        for n in group:
            res[n] = _adam(recv[n], p[n][0], m[n][0], v[n][0], tr=256, name="adam_" + n, after=last)
            last = (res[n][0],)

    dm = jnp.concatenate([d_mod_lat, d_mod_ctx, jnp.zeros((6, 6 * D), F32)], 0)
    dm_all = _allgather_small_seq(dm, "gather_dmod", SMALL_COLLECTIVE_ID0 + 1)
    dm_all = lax.optimization_barrier((dm_all,) + last)[0]
    dm2 = jnp.concatenate([dm_all[:, 0, :], dm_all[:, 1, :]], 0)
    dm2_mine = lax.dynamic_slice_in_dim(dm2, me * ADA_COLS, ADA_COLS, axis=1)
    s2 = jnp.concatenate([s_act[0:8], jnp.broadcast_to(s_act[8:9], (8, D))], 0)
    g_w_ada = _matmul(s2, dm2_mine, mode="tn", name="dw_ada", tm=512, tn=ADA_COLS, after=last)
    dsc_part = _matmul(dm2_mine[8:16], p["w_ada"][0], mode="nt", name="d_silu_cctx", tm=8, tn=512, after=last)

    def cctx_b(rv, vv):
        _, pull = jax.vjp(_silu, vv[0])
        return [], [pull(jnp.sum(rv[0], axis=0, keepdims=True))[0]]

    g_cctx, = _rowwise(cctx_b, [(dsc_part, D, 0, 0)], [cc], [], [(1, D)], nrows=8, tr=8, name="cctx_bwd")
    gs["c_ctx"] = g_cctx
    gs["b_ada"] = d_mod_lat + d_mod_ctx

    res["w_ada"] = _adam(g_w_ada[None], p["w_ada"][0], m["w_ada"][0], v["w_ada"][0], tr=256, name="adam_w_ada")

    g_late = _allgather_small_seq(_pack_lanes([gs[n] for n in SMALL_LATE]), "gather_small_late", SMALL_COLLECTIVE_ID0 + 2)
    row = lambda a: a.reshape(1, -1)
    late = _adam_lanes(g_late, [row(p[n]) for n in SMALL_LATE], [row(m[n]) for n in SMALL_LATE], [row(v[n]) for n in SMALL_LATE],
                       name="adam_small_late", after=(res["w_ada"][0],))
    res.update(zip(SMALL_LATE, late))

    outs = [total["loss"], grad_x[None]]
    for j in range(4):
        outs += [res[n][j].reshape(p[n].shape) for n in WEIGHTS]
    return tuple(outs)


def kernel(x, c, ctx, c_ctx, w_ada, b_ada, w_in, attn_sink, ssm_a_re, ssm_a_im, ssm_log_dt, ssm_b_re, ssm_b_im, ssm_c_re, ssm_c_im, ssm_d, w_glu, w_attn_up, w_ssm_up, w_out, ln_mix_g, ln_mix_b, w_mlp1, b_mlp1, w_mlp2, b_mlp2, ln_mlp_g, ln_mlp_b, loss_target, m_c_ctx, m_w_ada, m_b_ada, m_w_in, m_attn_sink, m_ssm_a_re, m_ssm_a_im, m_ssm_log_dt, m_ssm_b_re, m_ssm_b_im, m_ssm_c_re, m_ssm_c_im, m_ssm_d, m_w_glu, m_w_attn_up, m_w_ssm_up, m_w_out, m_ln_mix_g, m_ln_mix_b, m_w_mlp1, m_b_mlp1, m_w_mlp2, m_b_mlp2, m_ln_mlp_g, m_ln_mlp_b, v_c_ctx, v_w_ada, v_b_ada, v_w_in, v_attn_sink, v_ssm_a_re, v_ssm_a_im, v_ssm_log_dt, v_ssm_b_re, v_ssm_b_im, v_ssm_c_re, v_ssm_c_im, v_ssm_d, v_w_glu, v_w_attn_up, v_w_ssm_up, v_w_out, v_ln_mix_g, v_ln_mix_b, v_w_mlp1, v_b_mlp1, v_w_mlp2, v_b_mlp2, v_ln_mlp_g, v_ln_mlp_b):
    given = dict(locals())
    p = {n: given[n] for n in WEIGHTS}
    m = {n: given["m_" + n] for n in WEIGHTS}
    v = {n: given["v_" + n] for n in WEIGHTS}
    return _step(x, c, ctx, loss_target, p, m, v)
```

```python
import functools
import math

import jax
import jax.numpy as jnp
from jax import lax
from jax.experimental import pallas as pl
from jax.experimental.pallas import tpu as pltpu
from jax.experimental.pallas import tpu_sc as plsc

F32 = jnp.float32
BF16 = jnp.bfloat16

N_DEV = 8
D = 2048
T = 2048
C = 256
TA = T + C
GRID_W = 64
HD = 128
NH = 8
NKV = 2
GROUP = NH // NKV
WINDOW = 128
QW = NH * HD
KVW = NKV * HD
SW = D // 4
SG = 16
NG = SW // SG
SP = 64
DFF = 4 * D
IN_COLS = QW + 2 * KVW + SW + 2 * D
ALPHA = 2.0 ** 0.25
LN_EPS = 1e-6
NEG_INF = -1e30
ROPE_BASE = 10000.0
ATT_SCALE = HD ** -0.5

NSEG = 8
GBLK = 8
NBLK = NG // GBLK
BW = GBLK * SP
UW = GBLK * SG

ADAM_LR = 0.001
ADAM_B1 = 0.9
ADAM_B2 = 0.999
ADAM_EPS = 1e-08
ADAM_WD = 0.01
ADAM_STEP = 10

VMEM_LIMIT_BYTES = 56 * 1024 * 1024
MESH = pl.DeviceIdType.MESH


def _cparams(sem=None):
    return pltpu.CompilerParams(dimension_semantics=sem, vmem_limit_bytes=VMEM_LIMIT_BYTES)


def _matmul(a, b, *, mode, name, out_dtypes=(F32,), tm=512, tn=512, tk=None, bias=None, extras=(), epilogue=None, after=(),
            out_t=None):
    if mode == "nn":
        (M, K), (K2, N) = a.shape, b.shape
    elif mode == "nt":
        (M, K), (N, K2) = a.shape, b.shape
    else:
        (K, M), (K2, N) = a.shape, b.shape
    assert K == K2, (name, a.shape, b.shape)
    tm, tn, tk = min(tm, M), min(tn, N), min(tk or K, K)
    assert M % tm == 0 and N % tn == 0 and K % tk == 0, (name, M, N, K, tm, tn, tk)
    nk = K // tk
    if mode == "tn":
        a_spec = pl.BlockSpec((tk, tm), lambda i, j, k: (k, i))
    else:
        a_spec = pl.BlockSpec((tm, tk), lambda i, j, k: (i, k))
    if mode == "nt":
        b_spec = pl.BlockSpec((tn, tk), lambda i, j, k: (j, k))
    else:
        b_spec = pl.BlockSpec((tk, tn), lambda i, j, k: (k, j))
    dims = {"nn": (((1,), (0,)), ((), ())), "nt": (((1,), (1,)), ((), ())), "tn": (((0,), (0,)), ((), ()))}[mode]
    in_specs = [a_spec, b_spec]
    operands = [a, b]
    if bias is not None:
        in_specs.append(pl.BlockSpec((1, tn), lambda i, j, k: (0, j)))
        operands.append(bias)
    for e in extras:
        in_specs.append(pl.BlockSpec((tm, tn), lambda i, j, k: (i, j)))
        operands.append(e)
    n_ex = len(extras)
    for t in after:
        in_specs.append(pl.BlockSpec(memory_space=pl.ANY))
        operands.append(t)
    n_after = len(after)
    n_out = len(out_dtypes)
    out_t = tuple(out_t) if out_t is not None else (False,) * n_out
    has_bias = bias is not None

    def kern(*refs):
        a_ref, b_ref = refs[0], refs[1]
        pos = 2
        bias_ref = None
        if has_bias:
            bias_ref = refs[pos]
            pos += 1
        ex_refs = refs[pos:pos + n_ex]
        pos += n_ex + n_after
        out_refs = refs[pos:pos + n_out]
        acc_ref = refs[pos + n_out] if nk > 1 else None

        def finish(r):
            if has_bias:
                r = r + bias_ref[...]
            outs = epilogue(r, *[e[...] for e in ex_refs]) if epilogue is not None else (r,)
            for o_ref, o, tr_ in zip(out_refs, outs, out_t):
                o_ref[...] = (o.T if tr_ else o).astype(o_ref.dtype)

        part = lax.dot_general(a_ref[...].astype(BF16), b_ref[...].astype(BF16), dims, preferred_element_type=F32)
        if nk == 1:
            finish(part)
        else:
            k = pl.program_id(2)

            @pl.when(k == 0)
            def _():
                acc_ref[...] = part

            @pl.when(k > 0)
            def _():
                acc_ref[...] += part

            @pl.when(k == nk - 1)
            def _():
                finish(acc_ref[...])

    outs = pl.pallas_call(
        kern,
        name=name,
        grid=(M // tm, N // tn, nk),
        in_specs=in_specs,
        out_specs=[pl.BlockSpec((tn, tm), lambda i, j, k: (j, i)) if tr_ else pl.BlockSpec((tm, tn), lambda i, j, k: (i, j))
                   for tr_ in out_t],
        out_shape=[jax.ShapeDtypeStruct((N, M) if tr_ else (M, N), dt) for dt, tr_ in zip(out_dtypes, out_t)],
        scratch_shapes=[pltpu.VMEM((tm, tn), F32)] if nk > 1 else [],
        compiler_params=_cparams(("parallel", "parallel", "arbitrary")),
    )(*operands)
    return outs[0] if n_out == 1 else tuple(outs)


def _rowwise(fn, rows, vecs, outs, vec_outs, *, nrows, tr, name, after=(), pad_rows=0, pad_outs=()):
    n_rows, n_vecs, n_outs, n_after = len(rows), len(vecs), len(outs), len(after)
    nblk = nrows // tr
    assert not (pad_rows and vec_outs) and pad_rows % tr == 0
    last = (lambda i: jnp.minimum(i, nblk - 1)) if pad_rows else (lambda i: i)
    in_specs = [pl.BlockSpec((tr, w), lambda i, cb=cb, ro=ro: (last(i) + ro, cb)) for (_, w, cb, ro) in rows]
    in_specs += [pl.BlockSpec(v.shape, lambda i: (0, 0)) for v in vecs]
    in_specs += [pl.BlockSpec(memory_space=pl.ANY)] * n_after
    outs = [o if len(o) == 3 else (*o, False) for o in outs]
    padded = [pad_rows > 0 and j in pad_outs for j in range(n_outs)]
    assert not any(p_ and tr_ for p_, (_, _, tr_) in zip(padded, outs))
    out_specs = [pl.BlockSpec((w, tr), lambda i: (0, last(i))) if tr_ else
                 pl.BlockSpec((tr, w), (lambda i: (i, 0)) if p_ else (lambda i: (last(i), 0))) for (w, _, tr_), p_ in zip(outs, padded)]
    out_specs += [pl.BlockSpec(s, lambda i: (0, 0)) for s in vec_outs]
    out_shape = [jax.ShapeDtypeStruct((w, nrows) if tr_ else (nrows + (pad_rows if p_ else 0), w), dt)
                 for (w, dt, tr_), p_ in zip(outs, padded)]
    out_tr = [tr_ for (_, _, tr_) in outs]
    out_shape += [jax.ShapeDtypeStruct(s, F32) for s in vec_outs]

    def kern(*refs):
        rvals = [r[...] for r in refs[:n_rows]]
        vvals = [r[...] for r in refs[n_rows:n_rows + n_vecs]]
        first_out = n_rows + n_vecs + n_after
        o_refs = refs[first_out:first_out + n_outs]
        v_refs = refs[first_out + n_outs:]
        ro, vo = fn(rvals, vvals)
        i = pl.program_id(0)
        for r, val, tr_, p_ in zip(o_refs, ro, out_tr, padded):
            if p_:
                val = jnp.where(i < nblk, val, jnp.zeros_like(val))
            r[...] = (val.astype(F32).T if tr_ else val).astype(r.dtype)
        for r, val in zip(v_refs, vo):
            @pl.when(i == 0)
            def _(r=r, val=val):
                r[...] = val.astype(F32)

            @pl.when(i > 0)
            def _(r=r, val=val):
                r[...] += val.astype(F32)

    res = pl.pallas_call(
        kern,
        name=name,
        grid=((nrows + pad_rows) // tr,),
        in_specs=in_specs,
        out_specs=out_specs,
        out_shape=out_shape,
        compiler_params=_cparams(("arbitrary",)),
    )(*[r[0] for r in rows], *vecs, *after)
    return list(res)


def _ln(x):
    mu = jnp.mean(x, axis=-1, keepdims=True)
    xc = x - mu
    var = jnp.mean(xc * xc, axis=-1, keepdims=True)
    return xc * lax.rsqrt(var + LN_EPS)


def _sigmoid(x):
    return 1.0 / (1.0 + jnp.exp(-x))


def _gelu(x):
    return 0.5 * x * (1.0 + jnp.tanh(math.sqrt(2.0 / math.pi) * (x + 0.044715 * (x * x * x))))


def _silu(x):
    return x * _sigmoid(x)


def _f_ln_mod(x, sc, sh):
    return _ln(x) * (1.0 + sc) + sh


def _f_glu(z):
    return z[:, :SW] * _sigmoid(z[:, SW:])


def _f_mix(ga, gs, attn_d, ssm_d):
    return _sigmoid(ga) * attn_d + _sigmoid(gs) * ssm_d


def _f_post1(x, y, g1, lg, lb, sc2, sh2):
    r1 = ALPHA * x + g1 * y
    x1 = _ln(r1) * lg + lb
    h2 = _ln(x1) * (1.0 + sc2) + sh2
    return x1, h2


def _f_loss(x1, mlp, tgt, g2, lg, lb, b2z):
    r2 = ALPHA * x1 + g2 * (mlp + b2z)
    out = _ln(r2) * lg + lb
    err = out - tgt
    return 0.5 * jnp.sum(err * err) * (1.0 / D)


def _rope_tables():
    rows = T // GRID_W
    row = jnp.repeat(jnp.arange(rows), GRID_W)
    col = jnp.tile(jnp.arange(GRID_W), rows)
    n_freq = HD // 4
    freqs = ROPE_BASE ** (-jnp.arange(n_freq, dtype=F32) / n_freq)
    ang_r = row.astype(F32)[:, None] * freqs
    ang_c = col.astype(F32)[:, None] * freqs
    ang = jnp.concatenate([ang_r, ang_r, ang_c, ang_c], -1)
    cos, sin = jnp.cos(ang), jnp.sin(ang)
    lo = (jnp.arange(HD) % (HD // 2)) < (HD // 4)
    sin_a = jnp.where(lo[None, :], -sin, 0.0)
    sin_b = jnp.where(lo[None, :], 0.0, sin)
    return cos, sin_a, sin_b


def _rope(x, cos, sa, sb):
    return x * cos + pltpu.roll(x, 96, 1) * sa + pltpu.roll(x, 32, 1) * sb


def _rope_t(dy, cos, sa, sb):
    return dy * cos + pltpu.roll(dy * sa, 32, 1) + pltpu.roll(dy * sb, 96, 1)


BAND = 3 * WINDOW
KPAD = T + 2 * WINDOW


def _attn_fill_kv(k_ref, v_ref, cos_ref, sa_ref, sb_ref, kp, vp, kc, vc):
    zeros = jnp.zeros((WINDOW, KVW), BF16)
    kp[0:WINDOW, :] = zeros
    kp[WINDOW + T:KPAD, :] = zeros
    vp[0:WINDOW, :] = zeros
    vp[WINDOW + T:KPAD, :] = zeros
    for hh in range(NKV):
        cs = slice(hh * HD, (hh + 1) * HD)
        for r0 in range(0, T, 512):
            rs = slice(r0, r0 + 512)
            kr = _rope(k_ref[rs, cs], cos_ref[rs, :], sa_ref[rs, :], sb_ref[rs, :])
            kp[WINDOW + r0:WINDOW + r0 + 512, cs] = kr.astype(BF16)
    vp[WINDOW:WINDOW + T, :] = v_ref[0:T, :].astype(BF16)
    kc[...] = k_ref[T:TA, :].astype(BF16)
    vc[...] = v_ref[T:TA, :].astype(BF16)


GROWS = GROUP * WINDOW


def _attn_scores(n, kvh, q_ref, cos_ref, sa_ref, sb_ref, sink_ref, kp, kc):
    r0 = pl.multiple_of(n * WINDOW, WINDOW)
    cos = cos_ref[pl.ds(r0, WINDOW), :]
    sa = sa_ref[pl.ds(r0, WINDOW), :]
    sb = sb_ref[pl.ds(r0, WINDOW), :]
    heads = range(kvh * GROUP, (kvh + 1) * GROUP)
    q_g = jnp.concatenate([_rope(q_ref[:, h * HD:(h + 1) * HD], cos, sa, sb).astype(BF16) for h in heads], axis=0)
    kb = kp[pl.ds(r0, BAND), kvh * HD:(kvh + 1) * HD]
    kcb = kc[:, kvh * HD:(kvh + 1) * HD]
    nt = (((1,), (1,)), ((), ()))
    s_loc = lax.dot_general(q_g, kb, nt, preferred_element_type=F32) * ATT_SCALE
    s_ctx = lax.dot_general(q_g, kcb, nt, preferred_element_type=F32) * ATT_SCALE
    row = lax.broadcasted_iota(jnp.int32, (GROWS, BAND), 0) & (WINDOW - 1)
    col = lax.broadcasted_iota(jnp.int32, (GROWS, BAND), 1)
    rel = col - WINDOW - row
    kpos = r0 - WINDOW + col
    valid = (jnp.abs(rel) <= WINDOW) & (kpos >= 0) & (kpos < T)
    s_loc = jnp.where(valid, s_loc, NEG_INF)
    sk = jnp.concatenate([jnp.broadcast_to(sink_ref[0:1, h:h + 1], (WINDOW, 1)) for h in heads], axis=0)
    m = jnp.maximum(jnp.maximum(jnp.max(s_loc, -1, keepdims=True), jnp.max(s_ctx, -1, keepdims=True)), sk)
    e_loc = jnp.exp(s_loc - m)
    e_ctx = jnp.exp(s_ctx - m)
    e_sink = jnp.exp(sk - m)
    inv = 1.0 / (jnp.sum(e_loc, -1, keepdims=True) + jnp.sum(e_ctx, -1, keepdims=True) + e_sink)
    return q_g, r0, e_loc * inv, e_ctx * inv, e_sink * inv


def _attn_fwd(proj, sink, tabs):
    cos, sa, sb = tabs

    def kern(q_ref, k_ref, v_ref, cos_ref, sa_ref, sb_ref, sink_ref, o_ref, ot_ref, kp, vp, kc, vc):
        n = pl.program_id(0)

        @pl.when(n == 0)
        def _():
            _attn_fill_kv(k_ref, v_ref, cos_ref, sa_ref, sb_ref, kp, vp, kc, vc)

        for kvh in range(NKV):
            _, r0, p_loc, p_ctx, _ = _attn_scores(n, kvh, q_ref, cos_ref, sa_ref, sb_ref, sink_ref, kp, kc)
            vb = vp[pl.ds(r0, BAND), kvh * HD:(kvh + 1) * HD]
            vcb = vc[:, kvh * HD:(kvh + 1) * HD]
            o = jnp.dot(p_loc.astype(BF16), vb, preferred_element_type=F32)
            o = o + jnp.dot(p_ctx.astype(BF16), vcb, preferred_element_type=F32)
            for g in range(GROUP):
                h = kvh * GROUP + g
                o_h = o[g * WINDOW:(g + 1) * WINDOW, :]
                o_ref[:, h * HD:(h + 1) * HD] = o_h.astype(o_ref.dtype)
                ot_ref[h * HD:(h + 1) * HD, :] = o_h.T.astype(ot_ref.dtype)

    full = lambda shape: pl.BlockSpec(shape, lambda n: (0, 0))
    return pl.pallas_call(
        kern,
        name="attn_fwd",
        grid=(T // WINDOW,),
        in_specs=[
            pl.BlockSpec((WINDOW, QW), lambda n: (n, 0)),
            pl.BlockSpec((TA, KVW), lambda n: (0, QW // KVW)),
            pl.BlockSpec((TA, KVW), lambda n: (0, QW // KVW + 1)),
            full((T, HD)), full((T, HD)), full((T, HD)), full((1, NH)),
        ],
        out_specs=[pl.BlockSpec((WINDOW, QW), lambda n: (n, 0)), pl.BlockSpec((QW, WINDOW), lambda n: (0, n))],
        out_shape=[jax.ShapeDtypeStruct((T, QW), BF16), jax.ShapeDtypeStruct((QW, T), BF16)],
        scratch_shapes=[pltpu.VMEM((KPAD, KVW), BF16), pltpu.VMEM((KPAD, KVW), BF16),
                        pltpu.VMEM((C, KVW), BF16), pltpu.VMEM((C, KVW), BF16)],
        compiler_params=_cparams(("arbitrary",)),
    )(proj, proj, proj, cos, sa, sb, sink)


def _attn_bwd(proj, d_attn, sink, tabs):
    cos, sa, sb = tabs
    n_blocks = T // WINDOW

    def kern(q_ref, k_ref, v_ref, do_ref, cos_ref, sa_ref, sb_ref, sink_ref,
             dq_ref, dk_ref, dv_ref, dsink_ref, kp, vp, kc, vc, dkp, dvp, dkc, dvc):
        n = pl.program_id(0)

        @pl.when(n == 0)
        def _():
            _attn_fill_kv(k_ref, v_ref, cos_ref, sa_ref, sb_ref, kp, vp, kc, vc)
            dkp[...] = jnp.zeros_like(dkp)
            dvp[...] = jnp.zeros_like(dvp)
            dkc[...] = jnp.zeros_like(dkc)
            dvc[...] = jnp.zeros_like(dvc)
            dsink_ref[...] = jnp.zeros_like(dsink_ref)
            dq_ref[T:TA, :] = jnp.zeros((C, QW), dq_ref.dtype)

        nt = (((1,), (1,)), ((), ()))
        tn = (((0,), (0,)), ((), ()))
        for kvh in range(NKV):
            cs = slice(kvh * HD, (kvh + 1) * HD)
            heads = range(kvh * GROUP, (kvh + 1) * GROUP)
            q_g, r0, p_loc, p_ctx, p_sink = _attn_scores(n, kvh, q_ref, cos_ref, sa_ref, sb_ref, sink_ref, kp, kc)
            kb = kp[pl.ds(r0, BAND), cs]
            vb = vp[pl.ds(r0, BAND), cs]
            kcb = kc[:, cs]
            vcb = vc[:, cs]
            do_g = jnp.concatenate([do_ref[:, h * HD:(h + 1) * HD] for h in heads], axis=0)
            dp_loc = lax.dot_general(do_g, vb, nt, preferred_element_type=F32)
            dp_ctx = lax.dot_general(do_g, vcb, nt, preferred_element_type=F32)
            delta = jnp.sum(p_loc * dp_loc, -1, keepdims=True) + jnp.sum(p_ctx * dp_ctx, -1, keepdims=True)
            ds_loc = (p_loc * (dp_loc - delta) * ATT_SCALE).astype(BF16)
            ds_ctx = (p_ctx * (dp_ctx - delta) * ATT_SCALE).astype(BF16)
            dq = jnp.dot(ds_loc, kb, preferred_element_type=F32) + jnp.dot(ds_ctx, kcb, preferred_element_type=F32)
            cos = cos_ref[pl.ds(r0, WINDOW), :]
            sa_ = sa_ref[pl.ds(r0, WINDOW), :]
            sb_ = sb_ref[pl.ds(r0, WINDOW), :]
            dkp[pl.ds(r0, BAND), cs] += lax.dot_general(ds_loc, q_g, tn, preferred_element_type=F32)
            dkc[:, cs] += lax.dot_general(ds_ctx, q_g, tn, preferred_element_type=F32)
            dvp[pl.ds(r0, BAND), cs] += lax.dot_general(p_loc.astype(BF16), do_g, tn, preferred_element_type=F32)
            dvc[:, cs] += lax.dot_general(p_ctx.astype(BF16), do_g, tn, preferred_element_type=F32)
            dsk_rows = p_sink * delta
            for g, h in enumerate(heads):
                rs = slice(g * WINDOW, (g + 1) * WINDOW)
                dq_ref[pl.ds(r0, WINDOW), h * HD:(h + 1) * HD] = _rope_t(dq[rs, :], cos, sa_, sb_).astype(dq_ref.dtype)
                dsk = -jnp.sum(dsk_rows[rs, :], axis=0, keepdims=True)
                dsink_ref[h:h + 1, :] += jnp.broadcast_to(dsk, (1, HD))

        @pl.when(n == n_blocks - 1)
        def _():
            for hh in range(NKV):
                cs = slice(hh * HD, (hh + 1) * HD)
                for r0 in range(0, T, 512):
                    rs = slice(r0, r0 + 512)
                    g = dkp[WINDOW + r0:WINDOW + r0 + 512, cs]
                    dk_ref[rs, cs] = _rope_t(g, cos_ref[rs, :], sa_ref[rs, :], sb_ref[rs, :]).astype(dk_ref.dtype)
            dk_ref[T:TA, :] = dkc[...].astype(dk_ref.dtype)
            dv_ref[0:T, :] = dvp[WINDOW:WINDOW + T, :].astype(dv_ref.dtype)
            dv_ref[T:TA, :] = dvc[...].astype(dv_ref.dtype)

    full = lambda shape: pl.BlockSpec(shape, lambda n: (0, 0))
    return pl.pallas_call(
        kern,
        name="attn_bwd",
        grid=(n_blocks,),
        in_specs=[
            pl.BlockSpec((WINDOW, QW), lambda n: (n, 0)),
            pl.BlockSpec((TA, KVW), lambda n: (0, QW // KVW)),
            pl.BlockSpec((TA, KVW), lambda n: (0, QW // KVW + 1)),
            pl.BlockSpec((WINDOW, QW), lambda n: (n, 0)),
            full((T, HD)), full((T, HD)), full((T, HD)), full((1, NH)),
        ],
        out_specs=[full((TA, QW)), full((TA, KVW)), full((TA, KVW)), full((NH, HD))],
        out_shape=[jax.ShapeDtypeStruct((TA, QW), BF16), jax.ShapeDtypeStruct((TA, KVW), BF16),
                   jax.ShapeDtypeStruct((TA, KVW), BF16), jax.ShapeDtypeStruct((NH, HD), F32)],
        scratch_shapes=[pltpu.VMEM((KPAD, KVW), BF16), pltpu.VMEM((KPAD, KVW), BF16),
                        pltpu.VMEM((C, KVW), BF16), pltpu.VMEM((C, KVW), BF16),
                        pltpu.VMEM((KPAD, KVW), F32), pltpu.VMEM((KPAD, KVW), F32),
                        pltpu.VMEM((C, KVW), F32), pltpu.VMEM((C, KVW), F32)],
        compiler_params=_cparams(("arbitrary",)),
    )(proj, proj, proj, d_attn, cos, sa, sb, sink)


def _s5_prep(a_re, a_im, log_dt, b_re, b_im, c_re, c_im):
    lam = lax.complex(a_re, a_im)
    dt = jnp.exp(log_dt)[..., None]
    lam_bar = jnp.exp(lam * dt)
    b_bar = ((lam_bar - 1.0) / lam)[..., None] * lax.complex(b_re, b_im)
    def lam_rows(v):
        return v.reshape(2, NBLK, 1, BW)

    lam_l = jnp.concatenate([lam_rows(jnp.real(lam_bar)), lam_rows(jnp.imag(lam_bar))], -1)
    lam_l = jnp.broadcast_to(lam_l, (2, NBLK, 8, 2 * BW))
    diag = (jnp.arange(UW)[:, None] // SG) == (jnp.arange(BW)[None, :] // SP)

    def blocks(v):
        return jnp.where(diag, jnp.tile(v.reshape(2, NBLK, UW, SP), (1, 1, 1, GBLK)), 0.0)

    b_t = jnp.swapaxes(b_bar, -1, -2)
    bmat = jnp.concatenate([blocks(jnp.real(b_t)), blocks(jnp.imag(b_t))], -1)
    cmat = jnp.concatenate([blocks(c_re), -blocks(c_im)], -1)
    return lam_l, bmat, cmat


def _cmul(ar, ai, br, bi):
    return ar * br - ai * bi, ar * bi + ai * br


def _shift_rows(x, rev, fill):
    r = lax.broadcasted_iota(jnp.int32, x.shape, 0)
    down = jnp.where(r == 0, fill, pltpu.roll(x, 1, 0))
    up = jnp.where(r == NSEG - 1, fill, pltpu.roll(x, NSEG - 1, 0))
    return jnp.where(rev == 0, down, up)


def _edge_row(x, rev):
    last = jnp.broadcast_to(x[NSEG - 1:NSEG, :], x.shape)
    first = jnp.broadcast_to(x[0:1, :], x.shape)
    return jnp.where(rev == 0, last, first)


def _seg_scan(get, put, base, seglen, lr, li, rev, cin, acc_fn=None, acc0=()):
    zero = jnp.zeros((NSEG, BW), F32)

    def rows(k):
        j = jnp.where(rev == 0, k, seglen - 1 - k)
        return pl.ds(pl.multiple_of(base + j * NSEG, NSEG), NSEG)

    def local(k, carry):
        sr, si = carry
        xr, xi = get(rows(k))
        tr, ti = _cmul(lr, li, sr, si)
        sr, si = tr + xr, ti + xi
        put(rows(k), sr, si)
        return sr, si

    er, ei = lax.fori_loop(0, seglen, local, (zero, zero))
    lpr, lpi = lr, li
    assert seglen & (seglen - 1) == 0, seglen
    for _ in range(seglen.bit_length() - 1):
        lpr, lpi = _cmul(lpr, lpi, lpr, lpi)
    cr, ci = _shift_rows(zero, rev, cin[0]), _shift_rows(zero, rev, cin[1])
    for _ in range(NSEG - 1):
        tr, ti = _cmul(lpr, lpi, cr, ci)
        cr, ci = _shift_rows(er + tr, rev, cin[0]), _shift_rows(ei + ti, rev, cin[1])

    def fix(k, carry):
        tr, ti = _cmul(lr, li, carry[0], carry[1])
        xr, xi = get(rows(k))
        fr, fi = xr + tr, xi + ti
        put(rows(k), fr, fi)
        if acc_fn is None:
            return tr, ti
        j = jnp.where(rev == 0, k, seglen - 1 - k)
        return (tr, ti) + tuple(acc_fn(j, fr, fi, carry[2:]))

    out = lax.fori_loop(0, seglen, fix, (cr, ci) + tuple(acc0))
    tr, ti = out[0], out[1]
    leaving = (_edge_row(er + tr, rev), _edge_row(ei + ti, rev))
    return leaving if acc_fn is None else (leaving, out[2:])


RCH = 256
CSEG = C // NSEG
TSEG = T // NSEG
UCOL0 = (QW + 2 * KVW) // UW


REGIONS = ((0, TSEG), (T, CSEG))


def _state_access(ref, lead=()):
    def get(rows):
        return ref[(*lead, rows, slice(0, BW))], ref[(*lead, rows, slice(BW, 2 * BW))]

    def put(rows, re, im):
        ref[(*lead, rows, slice(0, BW))] = re
        ref[(*lead, rows, slice(BW, 2 * BW))] = im

    return get, put


def _interleave_rows(src_ref, dst_ref, regions=REGIONS):
    for base, seglen in regions:
        def body(j, carry, base=base, seglen=seglen):
            dst_ref[pl.ds(pl.multiple_of(base + j * NSEG, NSEG), NSEG), :] = src_ref[pl.ds(base + j, NSEG, stride=seglen), :]
            return carry

        lax.fori_loop(0, seglen, body, 0, unroll=8)


def _deinterleave_rows(src_ref, dst_ref, regions=REGIONS):
    for base, seglen in regions:
        def body(j, carry, base=base, seglen=seglen):
            dst_ref[pl.ds(base + j, NSEG, stride=seglen), :] = src_ref[pl.ds(pl.multiple_of(base + j * NSEG, NSEG), NSEG), :]
            return carry

        lax.fori_loop(0, seglen, body, 0, unroll=8)


def _s5_fwd(proj, dskip, lam, bmat, cmat):
    def kern(u_ref, dk_ref, lam_ref, b_ref, c_ref, s_ref, ssm_ref, ge_ref, get_ref, up_ref, yp_ref):
        d = pl.program_id(1)

        @pl.when(d == 0)
        def _():
            _interleave_rows(u_ref, up_ref)

        bm = b_ref[0, 0].astype(BF16)
        for r0 in range(0, TA, RCH):
            s_ref[0, 0, r0:r0 + RCH, :] = jnp.dot(up_ref[r0:r0 + RCH, :].astype(BF16), bm, preferred_element_type=F32)
        lr = lam_ref[0, 0, :, 0:BW]
        li = lam_ref[0, 0, :, BW:2 * BW]
        zero = jnp.zeros((NSEG, BW), F32)
        get, put = _state_access(s_ref, (0, 0))
        mid = _seg_scan(get, put, T, CSEG, lr, li, d, (zero, zero))
        _seg_scan(get, put, 0, TSEG, lr, li, d, mid)
        cm = c_ref[0, 0].astype(BF16)
        for r0 in range(0, T, RCH):
            y = lax.dot_general(s_ref[0, 0, r0:r0 + RCH, :].astype(BF16), cm, (((1,), (1,)), ((), ())), preferred_element_type=F32)

            @pl.when(d == 0)
            def _(y=y, r0=r0):
                yp_ref[r0:r0 + RCH, :] = y + dk_ref[...] * up_ref[r0:r0 + RCH, :]

            @pl.when(d == 1)
            def _(y=y, r0=r0):
                yp_ref[r0:r0 + RCH, :] += y

        @pl.when(d == 1)
        def _():
            _deinterleave_rows(yp_ref, ssm_ref, REGIONS[:1])
            for r0 in range(0, T, RCH):
                ge = _gelu(ssm_ref[r0:r0 + RCH, :])
                ge_ref[r0:r0 + RCH, :] = ge.astype(ge_ref.dtype)
                get_ref[:, r0:r0 + RCH] = ge.T.astype(get_ref.dtype)

    blk4 = lambda shape: pl.BlockSpec((1, 1) + shape, lambda b, d: (d, b, 0, 0))
    return pl.pallas_call(
        kern,
        name="s5_fwd",
        grid=(NBLK, 2),
        in_specs=[pl.BlockSpec((TA, UW), lambda b, d: (0, UCOL0 + b)), pl.BlockSpec((1, UW), lambda b, d: (0, b)),
                  blk4((8, 2 * BW)), blk4((UW, 2 * BW)), blk4((UW, 2 * BW))],
        out_specs=[blk4((TA, 2 * BW)), pl.BlockSpec((T, UW), lambda b, d: (0, b)), pl.BlockSpec((T, UW), lambda b, d: (0, b)),
                   pl.BlockSpec((UW, T), lambda b, d: (b, 0))],
        out_shape=[jax.ShapeDtypeStruct((2, NBLK, TA, 2 * BW), F32), jax.ShapeDtypeStruct((T, SW), F32),
                   jax.ShapeDtypeStruct((T, SW), BF16), jax.ShapeDtypeStruct((SW, T), BF16)],
        scratch_shapes=[pltpu.VMEM((TA, UW), F32), pltpu.VMEM((T, UW), F32)],
        compiler_params=_cparams(("parallel", "arbitrary")),
    )(proj, dskip, lam, bmat, cmat)


def _s5_bwd(d_ge, ssm, proj, dskip, states, lam, bmat, cmat):
    nt = (((1,), (1,)), ((), ()))
    tn = (((0,), (0,)), ((), ()))

    def kern(dge_ref, ssm_ref, u_ref, dk_ref, s_ref, lam_ref, b_ref, c_ref,
             du_ref, ddk_ref, dlam_ref, db_ref, dc_ref, g_ref, dua_ref, dssm_ref, up_ref, nat_ref):
        d = pl.program_id(1)

        @pl.when(d == 0)
        def _():
            ddk = jnp.zeros((1, UW), F32)
            for r0 in range(0, T, RCH):
                rs = slice(r0, r0 + RCH)
                _, pull = jax.vjp(_gelu, ssm_ref[rs, :])
                dssm = pull(dge_ref[rs, :])[0]
                nat_ref[rs, :] = dssm
                ddk = ddk + jnp.sum(dssm * u_ref[rs, :], axis=0, keepdims=True)
            ddk_ref[...] = ddk
            _interleave_rows(nat_ref, dssm_ref, REGIONS[:1])
            _interleave_rows(u_ref, up_ref)
            for r0 in range(0, T, RCH):
                dua_ref[r0:r0 + RCH, :] = dssm_ref[r0:r0 + RCH, :] * dk_ref[...]
            dua_ref[T:TA, :] = jnp.zeros((C, UW), F32)

        cm = c_ref[0, 0].astype(BF16)
        for r0 in range(0, T, RCH):
            g_ref[r0:r0 + RCH, :] = jnp.dot(dssm_ref[r0:r0 + RCH, :].astype(BF16), cm, preferred_element_type=F32)
        g_ref[T:TA, :] = jnp.zeros((C, 2 * BW), F32)
        lr = lam_ref[0, 0, :, 0:BW]
        li = lam_ref[0, 0, :, BW:2 * BW]
        zero = jnp.zeros((NSEG, BW), F32)
        get_g, put_g = _state_access(g_ref)

        get_s, _ = _state_access(s_ref, (0, 0))

        def dlam_fold(base, seglen, s_in):
            def rows(j):
                return pl.ds(pl.multiple_of(base + j * NSEG, NSEG), NSEG)

            jb = jnp.where(d == 0, 0, seglen - 1)
            jn = jnp.where(d == 0, seglen - 1, 0)
            sp = get_s(rows(jn))
            edge = (_shift_rows(sp[0], d, s_in[0]), _shift_rows(sp[1], d, s_in[1]))

            def fold(j, gr, gi, acc):
                jp = jnp.clip(jnp.where(d == 0, j - 1, j + 1), 0, seglen - 1)
                sr, si = get_s(rows(jp))
                sr = jnp.where(j == jb, edge[0], sr)
                si = jnp.where(j == jb, edge[1], si)
                return acc[0] + (gr * sr + gi * si), acc[1] + (gi * sr - gr * si)

            return fold

        r_mid = jnp.where(d == 0, TA - 1, T)
        s_mid = tuple(jnp.broadcast_to(t, (NSEG, BW)) for t in get_s(pl.ds(r_mid, 1)))
        mid, acc = _seg_scan(get_g, put_g, 0, TSEG, lr, -li, 1 - d, (zero, zero), dlam_fold(0, TSEG, s_mid), (zero, zero))
        _, acc = _seg_scan(get_g, put_g, T, CSEG, lr, -li, 1 - d, mid, dlam_fold(T, CSEG, (zero, zero)), acc)
        dlam_ref[0, 0, :, 0:BW] = acc[0]
        dlam_ref[0, 0, :, BW:2 * BW] = acc[1]

        bm = b_ref[0, 0].astype(BF16)
        db = jnp.zeros((UW, 2 * BW), F32)
        dc = jnp.zeros((UW, 2 * BW), F32)
        for r0 in range(0, TA, RCH):
            rs = slice(r0, r0 + RCH)
            g = g_ref[rs, :].astype(BF16)
            dua_ref[rs, :] += lax.dot_general(g, bm, nt, preferred_element_type=F32)
            db = db + lax.dot_general(up_ref[rs, :].astype(BF16), g, tn, preferred_element_type=F32)
            if r0 < T:
                dc = dc + lax.dot_general(dssm_ref[rs, :].astype(BF16), s_ref[0, 0, rs, :].astype(BF16), tn,
                                          preferred_element_type=F32)
        db_ref[0, 0] = db
        dc_ref[0, 0] = dc

        @pl.when(d == 1)
        def _():
            _deinterleave_rows(dua_ref, nat_ref)
            du_ref[...] = nat_ref[...].astype(du_ref.dtype)

    blk4 = lambda shape: pl.BlockSpec((1, 1) + shape, lambda b, d: (d, b, 0, 0))
    lat = pl.BlockSpec((T, UW), lambda b, d: (0, b))
    vec = pl.BlockSpec((1, UW), lambda b, d: (0, b))
    return pl.pallas_call(
        kern,
        name="s5_bwd",
        grid=(NBLK, 2),
        in_specs=[lat, lat, pl.BlockSpec((TA, UW), lambda b, d: (0, UCOL0 + b)), vec,
                  blk4((TA, 2 * BW)), blk4((8, 2 * BW)), blk4((UW, 2 * BW)), blk4((UW, 2 * BW))],
        out_specs=[pl.BlockSpec((TA, UW), lambda b, d: (0, b)), vec, blk4((8, 2 * BW)), blk4((UW, 2 * BW)), blk4((UW, 2 * BW))],
        out_shape=[jax.ShapeDtypeStruct((TA, SW), BF16), jax.ShapeDtypeStruct((1, SW), F32),
                   jax.ShapeDtypeStruct((2, NBLK, 8, 2 * BW), F32),
                   jax.ShapeDtypeStruct((2, NBLK, UW, 2 * BW), F32), jax.ShapeDtypeStruct((2, NBLK, UW, 2 * BW), F32)],
        scratch_shapes=[pltpu.VMEM((TA, 2 * BW), F32), pltpu.VMEM((TA, UW), F32), pltpu.VMEM((T, UW), F32),
                        pltpu.VMEM((TA, UW), F32), pltpu.VMEM((TA, UW), F32)],
        compiler_params=_cparams(("parallel", "arbitrary")),
    )(d_ge, ssm, proj, dskip, states, lam, bmat, cmat)


TR = 256
TN_WIDE = 1024


def _vjp_rows(f, primals, cots, n_row):
    _, pull = jax.vjp(f, *primals)
    g = pull(cots)
    return list(g[:n_row]), list(g[n_row:])


class _GradDict(dict):
    def __init__(self, on_set=None):
        super().__init__()
        self._on_set = on_set
        self.tokens = {}

    def __setitem__(self, key, value):
        super().__setitem__(key, value)
        if self._on_set is not None:
            self._on_set(self)

    def order(self, key):
        return self.tokens.get(key, self.get(key))

    def finish(self, key, after):
        if self.on_finish is None:
            return ()
        return (self.on_finish(key, after),)

    on_finish = None


def _local_step(x, ctx, tgt, mod_lat, mod_ctx, wb, sp, on_grad=None, on_loss=None, on_finish=None, on_early=None):
    sh1, sc1, g1, sh2, sc2, g2 = [mod_lat[:, i * D:(i + 1) * D] for i in range(6)]
    csh1, csc1 = mod_ctx[:, 0:D], mod_ctx[:, D:2 * D]
    tabs = _rope_tables()
    sink = sp["attn_sink"].reshape(1, NH)
    dskip = sp["ssm_d"].reshape(1, SW)
    lg_mix, lb_mix = sp["ln_mix_g"].reshape(1, D), sp["ln_mix_b"].reshape(1, D)
    lg_mlp, lb_mlp = sp["ln_mlp_g"].reshape(1, D), sp["ln_mlp_b"].reshape(1, D)
    b1, b2 = sp["b_mlp1"].reshape(1, DFF), sp["b_mlp2"].reshape(1, D)
    s5_names = ("ssm_a_re", "ssm_a_im", "ssm_log_dt", "ssm_b_re", "ssm_b_im", "ssm_c_re", "ssm_c_im")
    (lam, bmat, cmat), s5_pull = jax.vjp(_s5_prep, *[sp[n] for n in s5_names])

    def ln_mod2(rv, vv):
        h = _f_ln_mod(rv[0], vv[0], vv[1])
        return [h, h], []

    h_lat, h_lat_t = _rowwise(ln_mod2, [(x, D, 0, 0)], [sc1, sh1], [(D, BF16), (D, BF16, True)], [], nrows=T, tr=TR, name="ln1_lat")
    h_ctx, h_ctx_t = _rowwise(ln_mod2, [(ctx, D, 0, 0)], [csc1, csh1], [(D, BF16), (D, BF16, True)], [], nrows=C, tr=TR,
                              name="ln1_ctx")
    h1 = jnp.concatenate([h_lat, h_ctx], 0)
    h1_t = jnp.concatenate([h_lat_t, h_ctx_t], 1)
    proj = _matmul(h1, wb["w_in"], mode="nn", name="proj", tm=768, tn=TN_WIDE)
    attn, attn_t = _attn_fwd(proj, sink, tabs)
    states, ssm, ge, ge_t = _s5_fwd(proj, dskip, lam, bmat, cmat)
    z = _matmul(ge, wb["w_glu"], mode="nn", name="glu_mm", tm=1024, tn=1024)

    def glu_act(rv, vv):
        g_ = _f_glu(rv[0])
        return [g_, g_], []

    glu, glu_t = _rowwise(glu_act, [(z, 2 * SW, 0, 0)], [], [(SW, BF16), (SW, BF16, True)], [], nrows=T, tr=TR, name="glu_act")
    attn_d = _matmul(attn, wb["w_attn_up"], mode="nn", name="attn_up", tm=1024, tn=512)
    ssm_d = _matmul(glu, wb["w_ssm_up"], mode="nn", name="ssm_up", tm=1024, tn=512)
    ga_cb, gs_cb = (QW + 2 * KVW + SW) // D, (QW + 2 * KVW + SW) // D + 1

    def mix(rv, vv):
        m_ = _f_mix(*rv)
        return [m_, m_], []

    mixv, mix_t = _rowwise(mix, [(proj, D, ga_cb, 0), (proj, D, gs_cb, 0), (attn_d, D, 0, 0), (ssm_d, D, 0, 0)], [],
                           [(D, BF16), (D, BF16, True)], [], nrows=T, tr=TR, name="mix")
    y = _matmul(mixv, wb["w_out"], mode="nn", name="out_proj", tm=1024, tn=TN_WIDE)

    def post1(rv, vv):
        x1, h2 = _f_post1(rv[0], rv[1], *vv)
        return [x1, h2, h2], []

    x1, h2, h2_t = _rowwise(post1, [(x, D, 0, 0), (y, D, 0, 0)], [g1, lg_mix, lb_mix, sc2, sh2],
                            [(D, F32), (D, BF16), (D, BF16, True)], [], nrows=T, tr=TR, name="post1")

    def relu_sq(acc):
        r = jnp.maximum(acc, 0.0)
        return r, r * r, r * r

    r_act, act, act_t = _matmul(h2, wb["w_mlp1"], mode="nn", name="mlp1", tm=1024, tn=TN_WIDE, bias=b1,
                                out_dtypes=(BF16, BF16, BF16), out_t=(False, False, True), epilogue=relu_sq)
    mlp = _matmul(act, wb["w_mlp2"], mode="nn", name="mlp2", tm=512, tn=512)

    def loss_fb(rv, vv):
        x1_t, mlp_t, tgt_t = rv
        g2_v, lg_v, lb_v, b2_v = vv
        f = lambda a, m, g, p, q, b: _f_loss(a, m, tgt_t, g, p, q, b)
        val, grads = jax.value_and_grad(f, argnums=(0, 1, 2, 3, 4, 5))(x1_t, mlp_t, g2_v, lg_v, lb_v, b2_v)
        dx1, dmlp, dg2, dlg, dlb, db2 = grads
        return [dx1, dmlp], [jnp.reshape(val, (1, 1)), dg2, dlg, dlb, db2]

    dx1_a, d_mlp, loss_p, d_g2, d_lg_mlp, d_lb_mlp, d_b2 = _rowwise(
        loss_fb, [(x1, D, 0, 0), (mlp, D, 0, 0), (tgt, D, 0, 0)], [g2, lg_mlp, lb_mlp, b2],
        [(D, F32), (D, BF16)], [(1, 1), (1, D), (1, D), (1, D), (1, D)], nrows=T, tr=TR, name="loss_fb")

    gw = _GradDict(on_grad)
    gw.on_finish = on_finish
    loss_done = () if on_loss is None else (on_loss(loss_p),)
    gw["w_mlp2"] = _matmul(act_t, d_mlp, mode="nn", name="dw_mlp2", out_dtypes=(BF16,), tm=1024, tn=TN_WIDE, after=loss_done)
    da, = (_matmul(d_mlp, wb["w_mlp2"], mode="nt", name="d_act", out_dtypes=(BF16,), tm=1024, tn=TN_WIDE,
                   extras=(r_act,), epilogue=lambda acc, r: (acc * (2.0 * r.astype(F32)),), after=(gw.order("w_mlp2"),)),)
    pin = gw.finish("w_mlp2", da)
    ones = jnp.ones((8, T), BF16)
    d_b1 = _matmul(ones, da, mode="nn", name="db_mlp1", tm=8, tn=2048)[0:1]
    gw["w_mlp1"] = _matmul(h2_t, da, mode="nn", name="dw_mlp1", out_dtypes=(BF16,), tm=1024, tn=TN_WIDE, after=pin)
    dh2 = _matmul(da, wb["w_mlp1"], mode="nt", name="d_h2", tm=512, tn=512, after=(gw.order("w_mlp1"),))

    def post1_b(rv, vv):
        x_t, y_t, dx1_t, dh2_t = rv
        gr, gv = _vjp_rows(_f_post1, (x_t, y_t, *vv), (dx1_t, dh2_t), 2)
        return [gr[0], gr[1]], gv

    dx_a, dy, d_g1, d_lg_mix, d_lb_mix, d_sc2, d_sh2 = _rowwise(
        post1_b, [(x, D, 0, 0), (y, D, 0, 0), (dx1_a, D, 0, 0), (dh2, D, 0, 0)], [g1, lg_mix, lb_mix, sc2, sh2],
        [(D, F32), (D, BF16)], [(1, D)] * 5, nrows=T, tr=TR, name="post1_bwd")
    gw["w_out"] = _matmul(mix_t, dy, mode="nn", name="dw_out", out_dtypes=(BF16,), tm=1024, tn=TN_WIDE)
    dmix = _matmul(dy, wb["w_out"], mode="nt", name="d_mix", tm=1024, tn=TN_WIDE, after=(gw.order("w_out"),))

    def mix_b(rv, vv):
        gr, _ = _vjp_rows(_f_mix, tuple(rv[:4]), rv[4], 4)
        return gr, []

    d_ga, d_gs, d_attn_d, d_ssm_d = _rowwise(
        mix_b, [(proj, D, ga_cb, 0), (proj, D, gs_cb, 0), (attn_d, D, 0, 0), (ssm_d, D, 0, 0), (dmix, D, 0, 0)], [],
        [(D, BF16)] * 4, [], nrows=T, tr=TR, name="mix_bwd", pad_rows=C, pad_outs=(0, 1))
    pin = gw.finish("w_mlp1", d_ga)
    gw["w_attn_up"] = _matmul(attn_t, d_attn_d, mode="nn", name="dw_attn_up", out_dtypes=(BF16,), tm=1024, tn=TN_WIDE, after=pin)
    d_attn = _matmul(d_attn_d, wb["w_attn_up"], mode="nt", name="d_attn", out_dtypes=(BF16,), tm=1024, tn=512)
    gw["w_ssm_up"] = _matmul(glu_t, d_ssm_d, mode="nn", name="dw_ssm_up", out_dtypes=(BF16,), tm=512, tn=TN_WIDE)
    d_glu = _matmul(d_ssm_d, wb["w_ssm_up"], mode="nt", name="d_glu", tm=1024, tn=512, after=(gw.order("w_attn_up"), gw.order("w_ssm_up")))

    def glu_b(rv, vv):
        gr, _ = _vjp_rows(_f_glu, (rv[0],), rv[1], 1)
        return gr, []

    dz, = _rowwise(glu_b, [(z, 2 * SW, 0, 0), (d_glu, SW, 0, 0)], [], [(2 * SW, BF16)], [], nrows=T, tr=TR, name="glu_bwd")
    gw["w_glu"] = _matmul(ge_t, dz, mode="nn", name="dw_glu", out_dtypes=(BF16,), tm=512, tn=TN_WIDE)
    d_ge = _matmul(dz, wb["w_glu"], mode="nt", name="d_ge", tm=1024, tn=512, after=(gw.order("w_glu"),))

    du_all, d_dskip, dlam, dbmat, dcmat = _s5_bwd(d_ge, ssm, proj, dskip, states, lam, bmat, cmat)
    s5_grads = s5_pull((dlam, dbmat, dcmat))
    early = dict(zip(s5_names, s5_grads), ssm_d=d_dskip)
    if on_early is not None:
        on_early(early)
    pin = gw.finish("w_glu", du_all)

    dq, dk, dv, dsink = _attn_bwd(proj, d_attn, sink, tabs)
    dproj = jnp.concatenate([dq, dk, dv, du_all, d_ga, d_gs], 1)
    gw["w_in"] = _matmul(h1_t, dproj, mode="nn", name="dw_in", out_dtypes=(BF16,), tm=1024, tn=TN_WIDE, after=pin)
    pin = gw.finish("w_in", gw["w_in"])
    dh1 = _matmul(dproj, wb["w_in"], mode="nt", name="d_h1", tm=768, tn=512, after=pin)

    def ln1_b(rv, vv):
        x_t, dh_t, dxa_t = rv
        gr, gv = _vjp_rows(_f_ln_mod, (x_t, vv[0], vv[1]), dh_t, 1)
        return [gr[0] + dxa_t], gv

    grad_x, d_sc1, d_sh1 = _rowwise(ln1_b, [(x, D, 0, 0), (dh1, D, 0, 0), (dx_a, D, 0, 0)], [sc1, sh1],
                                    [(D, F32)], [(1, D), (1, D)], nrows=T, tr=TR, name="ln1_lat_bwd")

    def ln1c_b(rv, vv):
        _, gv = _vjp_rows(_f_ln_mod, (rv[0], vv[0], vv[1]), rv[1], 1)
        return [], gv

    d_csc1, d_csh1 = _rowwise(ln1c_b, [(ctx, D, 0, 0), (dh1, D, 0, T // TR)], [csc1, csh1],
                              [], [(1, D), (1, D)], nrows=C, tr=TR, name="ln1_ctx_bwd")

    d_mod_lat = jnp.concatenate([d_sh1, d_sc1, d_g1, d_sh2, d_sc2, d_g2], 1)
    zv = jnp.zeros((1, D), F32)
    d_mod_ctx = jnp.concatenate([d_csh1, d_csc1, zv, zv, zv, zv], 1)
    gs = {n: g for n, g in zip(s5_names, s5_grads)}
    gs["attn_sink"] = dsink[:, 0]
    gs["ssm_d"] = d_dskip
    gs["ln_mix_g"], gs["ln_mix_b"] = d_lg_mix, d_lb_mix
    gs["ln_mlp_g"], gs["ln_mlp_b"] = d_lg_mlp, d_lb_mlp
    gs["b_mlp1"], gs["b_mlp2"] = d_b1, d_b2
    return loss_p, grad_x, d_mod_lat, d_mod_ctx, gw, gs


def _my_pos():
    return lax.axis_index("x"), lax.axis_index("y"), lax.axis_index("c")


def _flip(p, bit):
    return 1 - p if bit else p


def _peer(pos, k):
    x, y, c = pos
    return (_flip(x, (k >> 2) & 1), _flip(y, (k >> 1) & 1), _flip(c, k & 1))


def _lin(pos):
    return 4 * pos[0] + 2 * pos[1] + pos[2]


def _allgather_small(v, name):
    r, w = v.shape

    def body(v_ref, out_ref, send_sems, recv_sems, local_sem):
        me = _my_pos()
        mine = pltpu.make_async_copy(v_ref, out_ref.at[_lin(me)], local_sem)
        mine.start()
        sends = []
        for k in range(1, N_DEV):
            cp = pltpu.make_async_remote_copy(src_ref=v_ref, dst_ref=out_ref.at[_lin(me)], send_sem=send_sems.at[k - 1],
                                              recv_sem=recv_sems.at[k - 1], device_id=_peer(me, k), device_id_type=MESH)
            cp.start()
            sends.append(cp)
        for k in range(1, N_DEV):
            peer = _peer(me, k)
            pltpu.make_async_remote_copy(src_ref=v_ref, dst_ref=out_ref.at[_lin(peer)], send_sem=send_sems.at[k - 1],
                                         recv_sem=recv_sems.at[k - 1], device_id=peer, device_id_type=MESH).wait_recv()
        for cp in sends:
            cp.wait_send()
        mine.wait()

    return pl.pallas_call(
        body,
        name=name,
        out_shape=jax.ShapeDtypeStruct((N_DEV, r, w), v.dtype),
        in_specs=[pl.BlockSpec(memory_space=pltpu.VMEM)],
        out_specs=pl.BlockSpec(memory_space=pltpu.VMEM),
        scratch_shapes=[pltpu.SemaphoreType.DMA((N_DEV - 1,)), pltpu.SemaphoreType.DMA((N_DEV - 1,)), pltpu.SemaphoreType.DMA],
        compiler_params=pltpu.CompilerParams(vmem_limit_bytes=VMEM_LIMIT_BYTES),
    )(v)


def _block_of(ref, kind, idx, n):
    start = pl.multiple_of(idx * n, 128)
    if kind == "col":
        return ref.at[:, pl.ds(start, n)]
    return ref.at[pl.ds(start, n), :]


def _handshake(peers):
    barrier = pltpu.get_barrier_semaphore()
    for peer in peers:
        pl.semaphore_signal(barrier, inc=1, device_id=peer, device_id_type=MESH)
    pl.semaphore_wait(barrier, len(peers))


def _allgather_weights_seq(shards, kinds, name, collective_id):
    nt = len(shards)
    hbm = pltpu.MemorySpace.HBM
    ins = [jax.new_ref(s, memory_space=hbm) for s in shards]
    outs = []
    for s, kind in zip(shards, kinds):
        k, n = s.shape
        shape = (k, n * N_DEV) if kind == "col" else (k * N_DEV, n)
        outs.append(jax.empty_ref(jax.ShapeDtypeStruct(shape, s.dtype), memory_space=hbm))

    @functools.partial(
        pl.kernel, mesh=plsc.ScalarSubcoreMesh(axis_name="seq", num_cores=1), name=name,
        scratch_types=(pltpu.SemaphoreType.DMA((nt, N_DEV - 1)), pltpu.SemaphoreType.DMA((nt, N_DEV - 1)),
                       pltpu.SemaphoreType.DMA((nt,))),
        compiler_params=pltpu.CompilerParams(collective_id=collective_id))
    def launch(send_sems, recv_sems, local_sems):
        x, y, c = _my_pos()
        me, sibling = (x, y, c), (x, y, 1 - c)
        chips = [(1 - x, y), (x, 1 - y), (1 - x, 1 - y)]
        _handshake([sibling] + [(*chip, c) for chip in chips])

        def blk(t, pos):
            n = shards[t].shape[1] if kinds[t] == "col" else shards[t].shape[0]
            return _block_of(outs[t], kinds[t], _lin(pos), n)

        def copy(t, k, block, to, src=None):
            return pltpu.make_async_remote_copy(src_ref=blk(t, block) if src is None else src, dst_ref=blk(t, block),
                                                send_sem=send_sems.at[t, k], recv_sem=recv_sems.at[t, k],
                                                device_id=to, device_id_type=MESH)

        local, sends = [], []
        for t in range(nt):
            mine = pltpu.make_async_copy(ins[t], blk(t, me), local_sems.at[t])
            mine.start()
            local.append(mine)
            first = [copy(t, 0, me, sibling, src=ins[t])]
            first += [copy(t, 1 + j, me, (*chip, c), src=ins[t]) for j, chip in enumerate(chips)]
            for cp in first:
                cp.start()
            sends += first
        for t in range(nt):
            for j, chip in enumerate(chips):
                copy(t, 1 + j, (*chip, c), me).wait_recv()
                fwd = copy(t, 4 + j, (*chip, c), sibling)
                fwd.start()
                sends.append(fwd)
        for t in range(nt):
            copy(t, 0, sibling, me).wait_recv()
            for j, chip in enumerate(chips):
                copy(t, 4 + j, (*chip, 1 - c), me).wait_recv()
        for cp in sends:
            cp.wait_send()
        for cp in local:
            cp.wait()

    launch()
    return [o[...] for o in outs]


def _allgather_small_seq(v, name, collective_id):
    hbm = pltpu.MemorySpace.HBM
    src = jax.new_ref(v, memory_space=hbm)
    out = jax.empty_ref(jax.ShapeDtypeStruct((N_DEV,) + v.shape, v.dtype), memory_space=hbm)

    @functools.partial(
        pl.kernel, mesh=plsc.ScalarSubcoreMesh(axis_name="seq", num_cores=1), name=name,
        scratch_types=(pltpu.SemaphoreType.DMA((N_DEV - 1,)), pltpu.SemaphoreType.DMA((N_DEV - 1,)), pltpu.SemaphoreType.DMA),
        compiler_params=pltpu.CompilerParams(collective_id=collective_id))
    def launch(send_sems, recv_sems, local_sem):
        me = _my_pos()
        _handshake([_peer(me, k) for k in range(1, N_DEV)])
        mine = pltpu.make_async_copy(src, out.at[_lin(me)], local_sem)
        mine.start()
        sends = []
        for k in range(1, N_DEV):
            cp = pltpu.make_async_remote_copy(src_ref=src, dst_ref=out.at[_lin(me)], send_sem=send_sems.at[k - 1],
                                              recv_sem=recv_sems.at[k - 1], device_id=_peer(me, k), device_id_type=MESH)
            cp.start()
            sends.append(cp)
        for k in range(1, N_DEV):
            peer = _peer(me, k)
            pltpu.make_async_remote_copy(src_ref=src, dst_ref=out.at[_lin(peer)], send_sem=send_sems.at[k - 1],
                                         recv_sem=recv_sems.at[k - 1], device_id=peer, device_id_type=MESH).wait_recv()
        for cp in sends:
            cp.wait_send()
        mine.wait()

    launch()
    return out[...]


N_CHIP = N_DEV // 2


def _chip_of(pos):
    return 2 * pos[0] + pos[1]


def _pair_exchange_seq(grads, kinds, name, collective_id):
    nt = len(grads)
    hbm = pltpu.MemorySpace.HBM
    shard_shapes = _shard_shapes(grads, kinds)
    ins = [jax.new_ref(g, memory_space=hbm) for g in grads]
    outs = [jax.empty_ref(jax.ShapeDtypeStruct((N_CHIP,) + s, g.dtype), memory_space=hbm) for s, g in zip(shard_shapes, grads)]

    @functools.partial(
        pl.kernel, mesh=plsc.ScalarSubcoreMesh(axis_name="seq", num_cores=1), name=name,
        scratch_types=(pltpu.SemaphoreType.DMA((nt, N_CHIP)), pltpu.SemaphoreType.DMA((nt, N_CHIP))),
        compiler_params=pltpu.CompilerParams(collective_id=collective_id))
    def launch(send_sems, recv_sems):
        x, y, c = _my_pos()
        sibling = (x, y, 1 - c)
        _handshake([sibling])
        copies = []
        for t in range(nt):
            n = shard_shapes[t][1] if kinds[t] == "col" else shard_shapes[t][0]
            for q in range(N_CHIP):
                cp = pltpu.make_async_remote_copy(src_ref=_block_of(ins[t], kinds[t], 2 * q + (1 - c), n), dst_ref=outs[t].at[q],
                                                  send_sem=send_sems.at[t, q], recv_sem=recv_sems.at[t, q],
                                                  device_id=sibling, device_id_type=MESH)
                cp.start()
                copies.append(cp)
        for cp in copies:
            cp.wait_recv()
        for cp in copies:
            cp.wait_send()

    launch()
    return [o[...] for o in outs]


def _pair_add(g, half, kind, name, after=()):
    nq, k, ns = half.shape
    tr = min(k, 512)
    c_idx = lax.axis_index("c").astype(jnp.int32).reshape(1)
    if kind == "col":
        g_spec = pl.BlockSpec((tr, ns), lambda q, i, c_ref: (i, 2 * q + c_ref[0]))
    else:
        g_spec = pl.BlockSpec((tr, ns), lambda q, i, c_ref: ((2 * q + c_ref[0]) * (k // tr) + i, 0))
    n_after = len(after)

    def kern(c_ref, g_ref, h_ref, *rest):
        o_ref = rest[n_after]
        o_ref[0] = (g_ref[...].astype(F32) + h_ref[0].astype(F32)).astype(o_ref.dtype)

    return pl.pallas_call(
        kern,
        name=name,
        grid_spec=pltpu.PrefetchScalarGridSpec(
            num_scalar_prefetch=1,
            grid=(nq, k // tr),
            in_specs=[g_spec, pl.BlockSpec((1, tr, ns), lambda q, i, c_ref: (q, i, 0))] + [pl.BlockSpec(memory_space=pl.ANY)] * n_after,
            out_specs=pl.BlockSpec((1, tr, ns), lambda q, i, c_ref: (q, i, 0)),
        ),
        out_shape=jax.ShapeDtypeStruct(half.shape, half.dtype),
        compiler_params=_cparams(("parallel", "parallel")),
    )(c_idx, g, half, *after)


def _chip_exchange_seq(psums, name, collective_id):
    nt = len(psums)
    hbm = pltpu.MemorySpace.HBM
    ins = [jax.new_ref(s, memory_space=hbm) for s in psums]
    outs = [jax.empty_ref(jax.ShapeDtypeStruct(s.shape, s.dtype), memory_space=hbm) for s in psums]

    @functools.partial(
        pl.kernel, mesh=plsc.ScalarSubcoreMesh(axis_name="seq", num_cores=1), name=name,
        scratch_types=(pltpu.SemaphoreType.DMA((nt, N_CHIP - 1)), pltpu.SemaphoreType.DMA((nt, N_CHIP - 1)),
                       pltpu.SemaphoreType.DMA((nt,))),
        compiler_params=pltpu.CompilerParams(collective_id=collective_id))
    def launch(send_sems, recv_sems, local_sems):
        me = _my_pos()
        peers = [_peer(me, k) for k in (2, 4, 6)]
        _handshake(peers)
        mine = _chip_of(me)
        local, sends = [], []
        for t in range(nt):
            cp = pltpu.make_async_copy(ins[t].at[mine], outs[t].at[mine], local_sems.at[t])
            cp.start()
            local.append(cp)
            for j, peer in enumerate(peers):
                cp = pltpu.make_async_remote_copy(src_ref=ins[t].at[_chip_of(peer)], dst_ref=outs[t].at[mine],
                                                  send_sem=send_sems.at[t, j], recv_sem=recv_sems.at[t, j],
                                                  device_id=peer, device_id_type=MESH)
                cp.start()
                sends.append(cp)
        for t in range(nt):
            for j, peer in enumerate(peers):
                pltpu.make_async_remote_copy(src_ref=ins[t].at[mine], dst_ref=outs[t].at[_chip_of(peer)],
                                             send_sem=send_sems.at[t, j], recv_sem=recv_sems.at[t, j],
                                             device_id=peer, device_id_type=MESH).wait_recv()
        for cp in sends:
            cp.wait_send()
        for cp in local:
            cp.wait()

    launch()
    return [o[...] for o in outs]


def _shard_shapes(grads, kinds):
    return [(g.shape[0], g.shape[1] // N_DEV) if kind == "col" else (g.shape[0] // N_DEV, g.shape[1]) for g, kind in zip(grads, kinds)]


def _adam(g_slots, w, m, v, *, tr, name, after=()):
    ns, r, wd = g_slots.shape
    tr = min(tr, r)
    assert r % tr == 0, (name, r, tr)
    n_after = len(after)

    def kern(g_ref, w_ref, m_ref, v_ref, *rest):
        go_ref, d_ref, mo_ref, vo_ref = rest[n_after:]
        g = g_ref[0].astype(F32)
        for s in range(1, ns):
            g = g + g_ref[s].astype(F32)
        delta, m_new, v_new = _adam_update(g, w_ref[...], m_ref[...], v_ref[...])
        go_ref[...] = g
        d_ref[...] = delta
        mo_ref[...] = m_new
        vo_ref[...] = v_new

    tile = pl.BlockSpec((tr, wd), lambda i: (i, 0))
    return pl.pallas_call(
        kern,
        name=name,
        grid=(r // tr,),
        in_specs=[pl.BlockSpec((ns, tr, wd), lambda i: (0, i, 0)), tile, tile, tile] + [pl.BlockSpec(memory_space=pl.ANY)] * n_after,
        out_specs=[tile] * 4,
        out_shape=[jax.ShapeDtypeStruct((r, wd), F32)] * 4,
        compiler_params=_cparams(("parallel",)),
    )(g_slots, w, m, v, *after)


def _adam_update(g, w, m, v):
    m_new = ADAM_B1 * m + (1.0 - ADAM_B1) * g
    v_new = ADAM_B2 * v + (1.0 - ADAM_B2) * (g * g)
    m_hat = m_new / (1.0 - ADAM_B1 ** ADAM_STEP)
    v_hat = v_new / (1.0 - ADAM_B2 ** ADAM_STEP)
    return -ADAM_LR * (m_hat / (jnp.sqrt(v_hat) + ADAM_EPS) + ADAM_WD * w), m_new, v_new


def _lane_offsets(sizes):
    offs, o = [], 0
    for n in sizes:
        offs.append(o)
        o += -(-n // LANES) * LANES
    return offs, o


def _pack_lanes(parts):
    cols = []
    for p_ in parts:
        flat = p_.reshape(1, -1).astype(F32)
        cols.append(jnp.pad(flat, ((0, 0), (0, (-flat.shape[1]) % LANES))))
    return jnp.concatenate(cols, 1)


def _adam_lanes(g_slots, ws, ms, vs, *, name, after=()):
    ns = g_slots.shape[0]
    npar, n_after = len(ws), len(after)
    sizes = [w.shape[1] for w in ws]
    offs, _ = _lane_offsets(sizes)

    def kern(g_ref, *refs):
        w_refs, m_refs, v_refs = refs[:npar], refs[npar:2 * npar], refs[2 * npar:3 * npar]
        outs = refs[3 * npar + n_after:]
        g_all = g_ref[0]
        for s in range(1, ns):
            g_all = g_all + g_ref[s]
        for j in range(npar):
            g = g_all[:, offs[j]:offs[j] + sizes[j]]
            delta, m_new, v_new = _adam_update(g, w_refs[j][...], m_refs[j][...], v_refs[j][...])
            outs[4 * j][...] = g
            outs[4 * j + 1][...] = delta
            outs[4 * j + 2][...] = m_new
            outs[4 * j + 3][...] = v_new

    vmem = pl.BlockSpec(memory_space=pltpu.VMEM)
    res = pl.pallas_call(
        kern,
        name=name,
        in_specs=[vmem] * (1 + 3 * npar) + [pl.BlockSpec(memory_space=pl.ANY)] * n_after,
        out_specs=[vmem] * (4 * npar),
        out_shape=[jax.ShapeDtypeStruct((1, n), F32) for n in sizes for _ in range(4)],
        compiler_params=pltpu.CompilerParams(vmem_limit_bytes=VMEM_LIMIT_BYTES),
    )(g_slots, *ws, *ms, *vs, *after)
    return [tuple(res[4 * j:4 * j + 4]) for j in range(npar)]


SMALL = ("c_ctx", "b_ada", "attn_sink", "ssm_a_re", "ssm_a_im", "ssm_log_dt", "ssm_b_re", "ssm_b_im", "ssm_c_re", "ssm_c_im",
         "ssm_d", "ln_mix_g", "ln_mix_b", "b_mlp1", "b_mlp2", "ln_mlp_g", "ln_mlp_b")
BIG = ("w_in", "w_glu", "w_attn_up", "w_ssm_up", "w_out", "w_mlp1", "w_mlp2")
BIG_KIND = ("col", "col", "col", "col", "row", "col", "row")
AG_GROUPS = (("w_in",), ("w_glu", "w_attn_up", "w_ssm_up", "w_out"), ("w_mlp1",), ("w_mlp2",))
AG_COLLECTIVE_ID0 = 1
RS_GROUPS = (("w_mlp2",), ("w_mlp1",), ("w_out", "w_attn_up", "w_ssm_up", "w_glu"), ("w_in",))
RS_COLLECTIVE_ID0 = AG_COLLECTIVE_ID0 + len(AG_GROUPS)
SMALL_EARLY = ("ssm_a_re", "ssm_a_im", "ssm_log_dt", "ssm_b_re", "ssm_b_im", "ssm_c_re", "ssm_c_im", "ssm_d")
SMALL_LATE = tuple(n for n in SMALL if n not in SMALL_EARLY)
SMALL_COLLECTIVE_ID0 = RS_COLLECTIVE_ID0 + 2 * len(RS_GROUPS)
LANES = 128


def _pack(parts):
    rows = []
    for p in parts:
        flat = p.reshape(-1).astype(F32)
        pad = (-flat.shape[0]) % LANES
        rows.append(jnp.pad(flat, (0, pad)).reshape(-1, LANES))
    packed = jnp.concatenate(rows, 0)
    return jnp.pad(packed, ((0, (-packed.shape[0]) % 8), (0, 0)))


def _unpack(packed, shapes):
    out, r0 = [], 0
    for s in shapes:
        n = math.prod(s)
        nr = -(-n // LANES)
        out.append(packed[r0:r0 + nr].reshape(-1)[:n].reshape(s))
        r0 += nr
    return out


WEIGHTS = ("c_ctx", "w_ada", "b_ada", "w_in", "attn_sink", "ssm_a_re", "ssm_a_im", "ssm_log_dt", "ssm_b_re", "ssm_b_im",
           "ssm_c_re", "ssm_c_im", "ssm_d", "w_glu", "w_attn_up", "w_ssm_up", "w_out", "ln_mix_g", "ln_mix_b", "w_mlp1",
           "b_mlp1", "w_mlp2", "b_mlp2", "ln_mlp_g", "ln_mlp_b")
ADA_COLS = 6 * D // N_DEV


def _step(x, c, ctx, loss_target, p, m, v):
    me = _lin(_my_pos())
    x2, ctx2, tgt2 = x[0], ctx[0], loss_target[0]

    wb = {}
    for gi, group in enumerate(AG_GROUPS):
        full = _allgather_weights_seq([p[n][0].astype(BF16) for n in group], [BIG_KIND[BIG.index(n)] for n in group],
                                      "allgather_seq%d" % gi, AG_COLLECTIVE_ID0 + gi)
        wb.update(zip(group, full))

    c_all = _allgather_small(jnp.broadcast_to(c, (8, D)), "gather_c")[:, 0, :]
    cc = p["c_ctx"].reshape(1, D)
    s_in = jnp.concatenate([c_all, cc, jnp.zeros((7, D), F32)], 0)
    s_act, = _rowwise(lambda rv, vv: ([_silu(rv[0])], []), [(s_in, D, 0, 0)], [], [(D, F32)], [], nrows=16, tr=16, name="silu_c")
    b_mine = lax.dynamic_slice_in_dim(p["b_ada"], me * ADA_COLS, ADA_COLS, axis=1)
    mod_part = _matmul(s_act, p["w_ada"][0], mode="nn", name="ada_fwd", tm=16, tn=512, bias=b_mine)
    mod_all = _allgather_small(mod_part, "gather_mod")
    mod_lat = lax.dynamic_index_in_dim(mod_all, me, axis=1, keepdims=False).reshape(1, 6 * D)
    mod_ctx = mod_all[:, 8, :].reshape(1, 6 * D)

    sp = {n: p[n][0] for n in SMALL if n not in ("c_ctx", "b_ada")}
    recv, halves = {}, {}

    def on_grad(gw):
        for gi, group in enumerate(RS_GROUPS):
            if gi not in halves and all(n in gw for n in group):
                kinds = [BIG_KIND[BIG.index(n)] for n in group]
                halves[gi] = (dict(gw), _pair_exchange_seq([gw[n] for n in group], kinds, "pair_exchange%d" % gi, RS_COLLECTIVE_ID0 + 2 * gi))

    def on_finish(key, after):
        gi = [i for i, group in enumerate(RS_GROUPS) if key in group][0]
        group = RS_GROUPS[gi]
        grads, half = halves[gi]
        prev = tuple(recv[n] for n in RS_GROUPS[gi - 1][:1]) if gi else ()
        if gi == len(RS_GROUPS) - 1:
            prev += (small["early"],)
        psums =[_pair_add(grads[n], h, BIG_KIND[BIG.index(n)], "pair_add_" + n, after=(after,) + prev) for n, h in zip(group, half)]
        recv.update(zip(group, _chip_exchange_seq(psums, "chip_exchange%d" % gi, RS_COLLECTIVE_ID0 + 2 * gi + 1)))
        return psums[-1]

    small = {}

    def on_early(gs_early):
        small["early"] = _allgather_small_seq(_pack([gs_early[n] for n in SMALL_EARLY]), "gather_small_early", SMALL_COLLECTIVE_ID0)

    total = {}

    def on_loss(loss_p):
        total["loss"] = lax.psum(loss_p[0, 0], ("x", "y", "c"))
        return total["loss"].reshape(1, 1)

    loss_p, grad_x, d_mod_lat, d_mod_ctx, gw, gs = _local_step(x2, ctx2, tgt2, mod_lat, mod_ctx, wb, sp, on_grad, on_loss, on_finish, on_early)

    g_early = small["early"]
    res = {}
    last = ()

    def adam_small(names, g_pack, tag, after):
        sm = _adam(g_pack, _pack([p[n] for n in names]), _pack([m[n] for n in names]), _pack([v[n] for n in names]),
                   tr=g_pack.shape[1], name="adam_small_" + tag, after=after)
        shapes = [p[n].shape for n in names]
        for j, outs in enumerate(zip(*[_unpack(a, shapes) for a in sm])):
            res[names[j]] = outs
        return (sm[0],)

    for gi, group in enumerate(RS_GROUPS):
        if gi == len(RS_GROUPS) - 1:
            last = adam_small(SMALL_EARLY, g_early, "early", last)
        for n in group:
            res[n] = _adam(recv[n], p[n][0], m[n][0], v[n][0], tr=256, name="adam_" + n, after=last)
            last = (res[n][0],)

    dm = jnp.concatenate([d_mod_lat, d_mod_ctx, jnp.zeros((6, 6 * D), F32)], 0)
    dm_all = _allgather_small_seq(dm, "gather_dmod", SMALL_COLLECTIVE_ID0 + 1)
    dm_all = lax.optimization_barrier((dm_all,) + last)[0]
    dm2 = jnp.concatenate([dm_all[:, 0, :], dm_all[:, 1, :]], 0)
    dm2_mine = lax.dynamic_slice_in_dim(dm2, me * ADA_COLS, ADA_COLS, axis=1)
    s2 = jnp.concatenate([s_act[0:8], jnp.broadcast_to(s_act[8:9], (8, D))], 0)
    g_w_ada = _matmul(s2, dm2_mine, mode="tn", name="dw_ada", tm=512, tn=ADA_COLS, after=last)
    dsc_part = _matmul(dm2_mine[8:16], p["w_ada"][0], mode="nt", name="d_silu_cctx", tm=8, tn=512, after=last)

    def cctx_b(rv, vv):
        _, pull = jax.vjp(_silu, vv[0])
        return [], [pull(jnp.sum(rv[0], axis=0, keepdims=True))[0]]

    g_cctx, = _rowwise(cctx_b, [(dsc_part, D, 0, 0)], [cc], [], [(1, D)], nrows=8, tr=8, name="cctx_bwd")
    gs["c_ctx"] = g_cctx
    gs["b_ada"] = d_mod_lat + d_mod_ctx

    res["w_ada"] = _adam(g_w_ada[None], p["w_ada"][0], m["w_ada"][0], v["w_ada"][0], tr=256, name="adam_w_ada")

    g_late = _allgather_small_seq(_pack_lanes([gs[n] for n in SMALL_LATE]), "gather_small_late", SMALL_COLLECTIVE_ID0 + 2)
    row = lambda a: a.reshape(1, -1)
    late = _adam_lanes(g_late, [row(p[n]) for n in SMALL_LATE], [row(m[n]) for n in SMALL_LATE], [row(v[n]) for n in SMALL_LATE],
                       name="adam_small_late", after=(res["w_ada"][0],))
    res.update(zip(SMALL_LATE, late))

    outs = [total["loss"], grad_x[None]]
    for j in range(4):
        outs += [res[n][j].reshape(p[n].shape) for n in WEIGHTS]
    return tuple(outs)


def kernel(x, c, ctx, c_ctx, w_ada, b_ada, w_in, attn_sink, ssm_a_re, ssm_a_im, ssm_log_dt, ssm_b_re, ssm_b_im, ssm_c_re, ssm_c_im, ssm_d, w_glu, w_attn_up, w_ssm_up, w_out, ln_mix_g, ln_mix_b, w_mlp1, b_mlp1, w_mlp2, b_mlp2, ln_mlp_g, ln_mlp_b, loss_target, m_c_ctx, m_w_ada, m_b_ada, m_w_in, m_attn_sink, m_ssm_a_re, m_ssm_a_im, m_ssm_log_dt, m_ssm_b_re, m_ssm_b_im, m_ssm_c_re, m_ssm_c_im, m_ssm_d, m_w_glu, m_w_attn_up, m_w_ssm_up, m_w_out, m_ln_mix_g, m_ln_mix_b, m_w_mlp1, m_b_mlp1, m_w_mlp2, m_b_mlp2, m_ln_mlp_g, m_ln_mlp_b, v_c_ctx, v_w_ada, v_b_ada, v_w_in, v_attn_sink, v_ssm_a_re, v_ssm_a_im, v_ssm_log_dt, v_ssm_b_re, v_ssm_b_im, v_ssm_c_re, v_ssm_c_im, v_ssm_d, v_w_glu, v_w_attn_up, v_w_ssm_up, v_w_out, v_ln_mix_g, v_ln_mix_b, v_w_mlp1, v_b_mlp1, v_w_mlp2, v_b_mlp2, v_ln_mlp_g, v_ln_mlp_b):
    given = dict(locals())
    p = {n: given[n] for n in WEIGHTS}
    m = {n: given["m_" + n] for n in WEIGHTS}
    v = {n: given["v_" + n] for n in WEIGHTS}
    return _step(x, c, ctx, loss_target, p, m, v)
```

```python
import functools
import math

import jax
import jax.numpy as jnp
from jax import lax
from jax.experimental import pallas as pl
from jax.experimental.pallas import tpu as pltpu
from jax.experimental.pallas import tpu_sc as plsc

F32 = jnp.float32
BF16 = jnp.bfloat16

N_DEV = 8
D = 2048
T = 2048
C = 256
TA = T + C
GRID_W = 64
HD = 128
NH = 8
NKV = 2
GROUP = NH // NKV
WINDOW = 128
QW = NH * HD
KVW = NKV * HD
SW = D // 4
SG = 16
NG = SW // SG
SP = 64
DFF = 4 * D
IN_COLS = QW + 2 * KVW + SW + 2 * D
ALPHA = 2.0 ** 0.25
LN_EPS = 1e-6
NEG_INF = -1e30
ROPE_BASE = 10000.0
ATT_SCALE = HD ** -0.5

NSEG = 8
GBLK = 8
NBLK = NG // GBLK
BW = GBLK * SP
UW = GBLK * SG

ADAM_LR = 0.001
ADAM_B1 = 0.9
ADAM_B2 = 0.999
ADAM_EPS = 1e-08
ADAM_WD = 0.01
ADAM_STEP = 10

VMEM_LIMIT_BYTES = 56 * 1024 * 1024
MESH = pl.DeviceIdType.MESH


def _cparams(sem=None):
    return pltpu.CompilerParams(dimension_semantics=sem, vmem_limit_bytes=VMEM_LIMIT_BYTES)


def _matmul(a, b, *, mode, name, out_dtypes=(F32,), tm=512, tn=512, tk=None, bias=None, extras=(), epilogue=None, after=(),
            out_t=None):
    if mode == "nn":
        (M, K), (K2, N) = a.shape, b.shape
    elif mode == "nt":
        (M, K), (N, K2) = a.shape, b.shape
    else:
        (K, M), (K2, N) = a.shape, b.shape
    assert K == K2, (name, a.shape, b.shape)
    tm, tn, tk = min(tm, M), min(tn, N), min(tk or K, K)
    assert M % tm == 0 and N % tn == 0 and K % tk == 0, (name, M, N, K, tm, tn, tk)
    nk = K // tk
    if mode == "tn":
        a_spec = pl.BlockSpec((tk, tm), lambda i, j, k: (k, i))
    else:
        a_spec = pl.BlockSpec((tm, tk), lambda i, j, k: (i, k))
    if mode == "nt":
        b_spec = pl.BlockSpec((tn, tk), lambda i, j, k: (j, k))
    else:
        b_spec = pl.BlockSpec((tk, tn), lambda i, j, k: (k, j))
    dims = {"nn": (((1,), (0,)), ((), ())), "nt": (((1,), (1,)), ((), ())), "tn": (((0,), (0,)), ((), ()))}[mode]
    in_specs = [a_spec, b_spec]
    operands = [a, b]
    if bias is not None:
        in_specs.append(pl.BlockSpec((1, tn), lambda i, j, k: (0, j)))
        operands.append(bias)
    for e in extras:
        in_specs.append(pl.BlockSpec((tm, tn), lambda i, j, k: (i, j)))
        operands.append(e)
    n_ex = len(extras)
    for t in after:
        in_specs.append(pl.BlockSpec(memory_space=pl.ANY))
        operands.append(t)
    n_after = len(after)
    n_out = len(out_dtypes)
    out_t = tuple(out_t) if out_t is not None else (False,) * n_out
    has_bias = bias is not None

    def kern(*refs):
        a_ref, b_ref = refs[0], refs[1]
        pos = 2
        bias_ref = None
        if has_bias:
            bias_ref = refs[pos]
            pos += 1
        ex_refs = refs[pos:pos + n_ex]
        pos += n_ex + n_after
        out_refs = refs[pos:pos + n_out]
        acc_ref = refs[pos + n_out] if nk > 1 else None

        def finish(r):
            if has_bias:
                r = r + bias_ref[...]
            outs = epilogue(r, *[e[...] for e in ex_refs]) if epilogue is not None else (r,)
            for o_ref, o, tr_ in zip(out_refs, outs, out_t):
                o_ref[...] = (o.T if tr_ else o).astype(o_ref.dtype)

        part = lax.dot_general(a_ref[...].astype(BF16), b_ref[...].astype(BF16), dims, preferred_element_type=F32)
        if nk == 1:
            finish(part)
        else:
            k = pl.program_id(2)

            @pl.when(k == 0)
            def _():
                acc_ref[...] = part

            @pl.when(k > 0)
            def _():
                acc_ref[...] += part

            @pl.when(k == nk - 1)
            def _():
                finish(acc_ref[...])

    outs = pl.pallas_call(
        kern,
        name=name,
        grid=(M // tm, N // tn, nk),
        in_specs=in_specs,
        out_specs=[pl.BlockSpec((tn, tm), lambda i, j, k: (j, i)) if tr_ else pl.BlockSpec((tm, tn), lambda i, j, k: (i, j))
                   for tr_ in out_t],
        out_shape=[jax.ShapeDtypeStruct((N, M) if tr_ else (M, N), dt) for dt, tr_ in zip(out_dtypes, out_t)],
        scratch_shapes=[pltpu.VMEM((tm, tn), F32)] if nk > 1 else [],
        compiler_params=_cparams(("parallel", "parallel", "arbitrary")),
    )(*operands)
    return outs[0] if n_out == 1 else tuple(outs)


def _rowwise(fn, rows, vecs, outs, vec_outs, *, nrows, tr, name, after=(), pad_rows=0, pad_outs=()):
    n_rows, n_vecs, n_outs, n_after = len(rows), len(vecs), len(outs), len(after)
    nblk = nrows // tr
    assert not (pad_rows and vec_outs) and pad_rows % tr == 0
    last = (lambda i: jnp.minimum(i, nblk - 1)) if pad_rows else (lambda i: i)
    in_specs = [pl.BlockSpec((tr, w), lambda i, cb=cb, ro=ro: (last(i) + ro, cb)) for (_, w, cb, ro) in rows]
    in_specs += [pl.BlockSpec(v.shape, lambda i: (0, 0)) for v in vecs]
    in_specs += [pl.BlockSpec(memory_space=pl.ANY)] * n_after
    outs = [o if len(o) == 3 else (*o, False) for o in outs]
    padded = [pad_rows > 0 and j in pad_outs for j in range(n_outs)]
    assert not any(p_ and tr_ for p_, (_, _, tr_) in zip(padded, outs))
    out_specs = [pl.BlockSpec((w, tr), lambda i: (0, last(i))) if tr_ else
                 pl.BlockSpec((tr, w), (lambda i: (i, 0)) if p_ else (lambda i: (last(i), 0))) for (w, _, tr_), p_ in zip(outs, padded)]
    out_specs += [pl.BlockSpec(s, lambda i: (0, 0)) for s in vec_outs]
    out_shape = [jax.ShapeDtypeStruct((w, nrows) if tr_ else (nrows + (pad_rows if p_ else 0), w), dt)
                 for (w, dt, tr_), p_ in zip(outs, padded)]
    out_tr = [tr_ for (_, _, tr_) in outs]
    out_shape += [jax.ShapeDtypeStruct(s, F32) for s in vec_outs]

    def kern(*refs):
        rvals = [r[...] for r in refs[:n_rows]]
        vvals = [r[...] for r in refs[n_rows:n_rows + n_vecs]]
        first_out = n_rows + n_vecs + n_after
        o_refs = refs[first_out:first_out + n_outs]
        v_refs = refs[first_out + n_outs:]
        ro, vo = fn(rvals, vvals)
        i = pl.program_id(0)
        for r, val, tr_, p_ in zip(o_refs, ro, out_tr, padded):
            if p_:
                val = jnp.where(i < nblk, val, jnp.zeros_like(val))
            r[...] = (val.astype(F32).T if tr_ else val).astype(r.dtype)
        for r, val in zip(v_refs, vo):
            @pl.when(i == 0)
            def _(r=r, val=val):
                r[...] = val.astype(F32)

            @pl.when(i > 0)
            def _(r=r, val=val):
                r[...] += val.astype(F32)

    res = pl.pallas_call(
        kern,
        name=name,
        grid=((nrows + pad_rows) // tr,),
        in_specs=in_specs,
        out_specs=out_specs,
        out_shape=out_shape,
        compiler_params=_cparams(("arbitrary",)),
    )(*[r[0] for r in rows], *vecs, *after)
    return list(res)


def _ln(x):
    mu = jnp.mean(x, axis=-1, keepdims=True)
    xc = x - mu
    var = jnp.mean(xc * xc, axis=-1, keepdims=True)
    return xc * lax.rsqrt(var + LN_EPS)


def _sigmoid(x):
    return 1.0 / (1.0 + jnp.exp(-x))


def _gelu(x):
    return 0.5 * x * (1.0 + jnp.tanh(math.sqrt(2.0 / math.pi) * (x + 0.044715 * (x * x * x))))


def _silu(x):
    return x * _sigmoid(x)


def _f_ln_mod(x, sc, sh):
    return _ln(x) * (1.0 + sc) + sh


def _f_glu(z):
    return z[:, :SW] * _sigmoid(z[:, SW:])


def _f_mix(ga, gs, attn_d, ssm_d):
    return _sigmoid(ga) * attn_d + _sigmoid(gs) * ssm_d


def _f_post1(x, y, g1, lg, lb, sc2, sh2):
    r1 = ALPHA * x + g1 * y
    x1 = _ln(r1) * lg + lb
    h2 = _ln(x1) * (1.0 + sc2) + sh2
    return x1, h2


def _f_loss(x1, mlp, tgt, g2, lg, lb, b2z):
    r2 = ALPHA * x1 + g2 * (mlp + b2z)
    out = _ln(r2) * lg + lb
    err = out - tgt
    return 0.5 * jnp.sum(err * err) * (1.0 / D)


def _rope_tables():
    rows = T // GRID_W
    row = jnp.repeat(jnp.arange(rows), GRID_W)
    col = jnp.tile(jnp.arange(GRID_W), rows)
    n_freq = HD // 4
    freqs = ROPE_BASE ** (-jnp.arange(n_freq, dtype=F32) / n_freq)
    ang_r = row.astype(F32)[:, None] * freqs
    ang_c = col.astype(F32)[:, None] * freqs
    ang = jnp.concatenate([ang_r, ang_r, ang_c, ang_c], -1)
    cos, sin = jnp.cos(ang), jnp.sin(ang)
    lo = (jnp.arange(HD) % (HD // 2)) < (HD // 4)
    sin_a = jnp.where(lo[None, :], -sin, 0.0)
    sin_b = jnp.where(lo[None, :], 0.0, sin)
    return cos, sin_a, sin_b


def _rope(x, cos, sa, sb):
    return x * cos + pltpu.roll(x, 96, 1) * sa + pltpu.roll(x, 32, 1) * sb


def _rope_t(dy, cos, sa, sb):
    return dy * cos + pltpu.roll(dy * sa, 32, 1) + pltpu.roll(dy * sb, 96, 1)


BAND = 3 * WINDOW
KPAD = T + 2 * WINDOW


def _attn_fill_kv(k_ref, v_ref, cos_ref, sa_ref, sb_ref, kp, vp, kc, vc):
    zeros = jnp.zeros((WINDOW, KVW), BF16)
    kp[0:WINDOW, :] = zeros
    kp[WINDOW + T:KPAD, :] = zeros
    vp[0:WINDOW, :] = zeros
    vp[WINDOW + T:KPAD, :] = zeros
    for hh in range(NKV):
        cs = slice(hh * HD, (hh + 1) * HD)
        for r0 in range(0, T, 512):
            rs = slice(r0, r0 + 512)
            kr = _rope(k_ref[rs, cs], cos_ref[rs, :], sa_ref[rs, :], sb_ref[rs, :])
            kp[WINDOW + r0:WINDOW + r0 + 512, cs] = kr.astype(BF16)
    vp[WINDOW:WINDOW + T, :] = v_ref[0:T, :].astype(BF16)
    kc[...] = k_ref[T:TA, :].astype(BF16)
    vc[...] = v_ref[T:TA, :].astype(BF16)


GROWS = GROUP * WINDOW


def _attn_scores(n, kvh, q_ref, cos_ref, sa_ref, sb_ref, sink_ref, kp, kc):
    r0 = pl.multiple_of(n * WINDOW, WINDOW)
    cos = cos_ref[pl.ds(r0, WINDOW), :]
    sa = sa_ref[pl.ds(r0, WINDOW), :]
    sb = sb_ref[pl.ds(r0, WINDOW), :]
    heads = range(kvh * GROUP, (kvh + 1) * GROUP)
    q_g = jnp.concatenate([_rope(q_ref[:, h * HD:(h + 1) * HD], cos, sa, sb).astype(BF16) for h in heads], axis=0)
    kb = kp[pl.ds(r0, BAND), kvh * HD:(kvh + 1) * HD]
    kcb = kc[:, kvh * HD:(kvh + 1) * HD]
    nt = (((1,), (1,)), ((), ()))
    s_loc = lax.dot_general(q_g, kb, nt, preferred_element_type=F32) * ATT_SCALE
    s_ctx = lax.dot_general(q_g, kcb, nt, preferred_element_type=F32) * ATT_SCALE
    row = lax.broadcasted_iota(jnp.int32, (GROWS, BAND), 0) & (WINDOW - 1)
    col = lax.broadcasted_iota(jnp.int32, (GROWS, BAND), 1)
    rel = col - WINDOW - row
    kpos = r0 - WINDOW + col
    valid = (jnp.abs(rel) <= WINDOW) & (kpos >= 0) & (kpos < T)
    s_loc = jnp.where(valid, s_loc, NEG_INF)
    sk = jnp.concatenate([jnp.broadcast_to(sink_ref[0:1, h:h + 1], (WINDOW, 1)) for h in heads], axis=0)
    m = jnp.maximum(jnp.maximum(jnp.max(s_loc, -1, keepdims=True), jnp.max(s_ctx, -1, keepdims=True)), sk)
    e_loc = jnp.exp(s_loc - m)
    e_ctx = jnp.exp(s_ctx - m)
    e_sink = jnp.exp(sk - m)
    inv = 1.0 / (jnp.sum(e_loc, -1, keepdims=True) + jnp.sum(e_ctx, -1, keepdims=True) + e_sink)
    return q_g, r0, e_loc * inv, e_ctx * inv, e_sink * inv


def _attn_fwd(proj, sink, tabs):
    cos, sa, sb = tabs

    def kern(q_ref, k_ref, v_ref, cos_ref, sa_ref, sb_ref, sink_ref, o_ref, kp, vp, kc, vc):
        n = pl.program_id(0)

        @pl.when(n == 0)
        def _():
            _attn_fill_kv(k_ref, v_ref, cos_ref, sa_ref, sb_ref, kp, vp, kc, vc)

        for kvh in range(NKV):
            _, r0, p_loc, p_ctx, _ = _attn_scores(n, kvh, q_ref, cos_ref, sa_ref, sb_ref, sink_ref, kp, kc)
            vb = vp[pl.ds(r0, BAND), kvh * HD:(kvh + 1) * HD]
            vcb = vc[:, kvh * HD:(kvh + 1) * HD]
            o = jnp.dot(p_loc.astype(BF16), vb, preferred_element_type=F32)
            o = o + jnp.dot(p_ctx.astype(BF16), vcb, preferred_element_type=F32)
            for g in range(GROUP):
                h = kvh * GROUP + g
                o_ref[:, h * HD:(h + 1) * HD] = o[g * WINDOW:(g + 1) * WINDOW, :].astype(o_ref.dtype)

    full = lambda shape: pl.BlockSpec(shape, lambda n: (0, 0))
    return pl.pallas_call(
        kern,
        name="attn_fwd",
        grid=(T // WINDOW,),
        in_specs=[
            pl.BlockSpec((WINDOW, QW), lambda n: (n, 0)),
            pl.BlockSpec((TA, KVW), lambda n: (0, QW // KVW)),
            pl.BlockSpec((TA, KVW), lambda n: (0, QW // KVW + 1)),
            full((T, HD)), full((T, HD)), full((T, HD)), full((1, NH)),
        ],
        out_specs=pl.BlockSpec((WINDOW, QW), lambda n: (n, 0)),
        out_shape=jax.ShapeDtypeStruct((T, QW), BF16),
        scratch_shapes=[pltpu.VMEM((KPAD, KVW), BF16), pltpu.VMEM((KPAD, KVW), BF16),
                        pltpu.VMEM((C, KVW), BF16), pltpu.VMEM((C, KVW), BF16)],
        compiler_params=_cparams(("arbitrary",)),
    )(proj, proj, proj, cos, sa, sb, sink)


def _attn_bwd(proj, d_attn, sink, tabs):
    cos, sa, sb = tabs
    n_blocks = T // WINDOW

    def kern(q_ref, k_ref, v_ref, do_ref, cos_ref, sa_ref, sb_ref, sink_ref,
             dq_ref, dk_ref, dv_ref, dsink_ref, kp, vp, kc, vc, dkp, dvp, dkc, dvc):
        n = pl.program_id(0)

        @pl.when(n == 0)
        def _():
            _attn_fill_kv(k_ref, v_ref, cos_ref, sa_ref, sb_ref, kp, vp, kc, vc)
            dkp[...] = jnp.zeros_like(dkp)
            dvp[...] = jnp.zeros_like(dvp)
            dkc[...] = jnp.zeros_like(dkc)
            dvc[...] = jnp.zeros_like(dvc)
            dsink_ref[...] = jnp.zeros_like(dsink_ref)
            dq_ref[T:TA, :] = jnp.zeros((C, QW), dq_ref.dtype)

        nt = (((1,), (1,)), ((), ()))
        tn = (((0,), (0,)), ((), ()))
        for kvh in range(NKV):
            cs = slice(kvh * HD, (kvh + 1) * HD)
            heads = range(kvh * GROUP, (kvh + 1) * GROUP)
            q_g, r0, p_loc, p_ctx, p_sink = _attn_scores(n, kvh, q_ref, cos_ref, sa_ref, sb_ref, sink_ref, kp, kc)
            kb = kp[pl.ds(r0, BAND), cs]
            vb = vp[pl.ds(r0, BAND), cs]
            kcb = kc[:, cs]
            vcb = vc[:, cs]
            do_g = jnp.concatenate([do_ref[:, h * HD:(h + 1) * HD] for h in heads], axis=0)
            dp_loc = lax.dot_general(do_g, vb, nt, preferred_element_type=F32)
            dp_ctx = lax.dot_general(do_g, vcb, nt, preferred_element_type=F32)
            delta = jnp.sum(p_loc * dp_loc, -1, keepdims=True) + jnp.sum(p_ctx * dp_ctx, -1, keepdims=True)
            ds_loc = (p_loc * (dp_loc - delta) * ATT_SCALE).astype(BF16)
            ds_ctx = (p_ctx * (dp_ctx - delta) * ATT_SCALE).astype(BF16)
            dq = jnp.dot(ds_loc, kb, preferred_element_type=F32) + jnp.dot(ds_ctx, kcb, preferred_element_type=F32)
            cos = cos_ref[pl.ds(r0, WINDOW), :]
            sa_ = sa_ref[pl.ds(r0, WINDOW), :]
            sb_ = sb_ref[pl.ds(r0, WINDOW), :]
            dkp[pl.ds(r0, BAND), cs] += lax.dot_general(ds_loc, q_g, tn, preferred_element_type=F32)
            dkc[:, cs] += lax.dot_general(ds_ctx, q_g, tn, preferred_element_type=F32)
            dvp[pl.ds(r0, BAND), cs] += lax.dot_general(p_loc.astype(BF16), do_g, tn, preferred_element_type=F32)
            dvc[:, cs] += lax.dot_general(p_ctx.astype(BF16), do_g, tn, preferred_element_type=F32)
            dsk_rows = p_sink * delta
            for g, h in enumerate(heads):
                rs = slice(g * WINDOW, (g + 1) * WINDOW)
                dq_ref[pl.ds(r0, WINDOW), h * HD:(h + 1) * HD] = _rope_t(dq[rs, :], cos, sa_, sb_).astype(dq_ref.dtype)
                dsk = -jnp.sum(dsk_rows[rs, :], axis=0, keepdims=True)
                dsink_ref[h:h + 1, :] += jnp.broadcast_to(dsk, (1, HD))

        @pl.when(n == n_blocks - 1)
        def _():
            for hh in range(NKV):
                cs = slice(hh * HD, (hh + 1) * HD)
                for r0 in range(0, T, 512):
                    rs = slice(r0, r0 + 512)
                    g = dkp[WINDOW + r0:WINDOW + r0 + 512, cs]
                    dk_ref[rs, cs] = _rope_t(g, cos_ref[rs, :], sa_ref[rs, :], sb_ref[rs, :]).astype(dk_ref.dtype)
            dk_ref[T:TA, :] = dkc[...].astype(dk_ref.dtype)
            dv_ref[0:T, :] = dvp[WINDOW:WINDOW + T, :].astype(dv_ref.dtype)
            dv_ref[T:TA, :] = dvc[...].astype(dv_ref.dtype)

    full = lambda shape: pl.BlockSpec(shape, lambda n: (0, 0))
    return pl.pallas_call(
        kern,
        name="attn_bwd",
        grid=(n_blocks,),
        in_specs=[
            pl.BlockSpec((WINDOW, QW), lambda n: (n, 0)),
            pl.BlockSpec((TA, KVW), lambda n: (0, QW // KVW)),
            pl.BlockSpec((TA, KVW), lambda n: (0, QW // KVW + 1)),
            pl.BlockSpec((WINDOW, QW), lambda n: (n, 0)),
            full((T, HD)), full((T, HD)), full((T, HD)), full((1, NH)),
        ],
        out_specs=[full((TA, QW)), full((TA, KVW)), full((TA, KVW)), full((NH, HD))],
        out_shape=[jax.ShapeDtypeStruct((TA, QW), BF16), jax.ShapeDtypeStruct((TA, KVW), BF16),
                   jax.ShapeDtypeStruct((TA, KVW), BF16), jax.ShapeDtypeStruct((NH, HD), F32)],
        scratch_shapes=[pltpu.VMEM((KPAD, KVW), BF16), pltpu.VMEM((KPAD, KVW), BF16),
                        pltpu.VMEM((C, KVW), BF16), pltpu.VMEM((C, KVW), BF16),
                        pltpu.VMEM((KPAD, KVW), F32), pltpu.VMEM((KPAD, KVW), F32),
                        pltpu.VMEM((C, KVW), F32), pltpu.VMEM((C, KVW), F32)],
        compiler_params=_cparams(("arbitrary",)),
    )(proj, proj, proj, d_attn, cos, sa, sb, sink)


def _s5_prep(a_re, a_im, log_dt, b_re, b_im, c_re, c_im):
    lam = lax.complex(a_re, a_im)
    dt = jnp.exp(log_dt)[..., None]
    lam_bar = jnp.exp(lam * dt)
    b_bar = ((lam_bar - 1.0) / lam)[..., None] * lax.complex(b_re, b_im)
    def lam_rows(v):
        return v.reshape(2, NBLK, 1, BW)

    lam_l = jnp.concatenate([lam_rows(jnp.real(lam_bar)), lam_rows(jnp.imag(lam_bar))], -1)
    lam_l = jnp.broadcast_to(lam_l, (2, NBLK, 8, 2 * BW))
    diag = (jnp.arange(UW)[:, None] // SG) == (jnp.arange(BW)[None, :] // SP)

    def blocks(v):
        return jnp.where(diag, jnp.tile(v.reshape(2, NBLK, UW, SP), (1, 1, 1, GBLK)), 0.0)

    b_t = jnp.swapaxes(b_bar, -1, -2)
    bmat = jnp.concatenate([blocks(jnp.real(b_t)), blocks(jnp.imag(b_t))], -1)
    cmat = jnp.concatenate([blocks(c_re), -blocks(c_im)], -1)
    return lam_l, bmat, cmat


def _cmul(ar, ai, br, bi):
    return ar * br - ai * bi, ar * bi + ai * br


def _shift_rows(x, rev, fill):
    r = lax.broadcasted_iota(jnp.int32, x.shape, 0)
    down = jnp.where(r == 0, fill, pltpu.roll(x, 1, 0))
    up = jnp.where(r == NSEG - 1, fill, pltpu.roll(x, NSEG - 1, 0))
    return jnp.where(rev == 0, down, up)


def _edge_row(x, rev):
    last = jnp.broadcast_to(x[NSEG - 1:NSEG, :], x.shape)
    first = jnp.broadcast_to(x[0:1, :], x.shape)
    return jnp.where(rev == 0, last, first)


def _seg_scan(get, put, base, seglen, lr, li, rev, cin, acc_fn=None, acc0=()):
    zero = jnp.zeros((NSEG, BW), F32)

    def rows(k):
        j = jnp.where(rev == 0, k, seglen - 1 - k)
        return pl.ds(pl.multiple_of(base + j * NSEG, NSEG), NSEG)

    def local(k, carry):
        sr, si = carry
        xr, xi = get(rows(k))
        tr, ti = _cmul(lr, li, sr, si)
        sr, si = tr + xr, ti + xi
        put(rows(k), sr, si)
        return sr, si

    er, ei = lax.fori_loop(0, seglen, local, (zero, zero))
    lpr, lpi = lr, li
    assert seglen & (seglen - 1) == 0, seglen
    for _ in range(seglen.bit_length() - 1):
        lpr, lpi = _cmul(lpr, lpi, lpr, lpi)
    cr, ci = _shift_rows(zero, rev, cin[0]), _shift_rows(zero, rev, cin[1])
    for _ in range(NSEG - 1):
        tr, ti = _cmul(lpr, lpi, cr, ci)
        cr, ci = _shift_rows(er + tr, rev, cin[0]), _shift_rows(ei + ti, rev, cin[1])

    def fix(k, carry):
        tr, ti = _cmul(lr, li, carry[0], carry[1])
        xr, xi = get(rows(k))
        fr, fi = xr + tr, xi + ti
        put(rows(k), fr, fi)
        if acc_fn is None:
            return tr, ti
        j = jnp.where(rev == 0, k, seglen - 1 - k)
        return (tr, ti) + tuple(acc_fn(j, fr, fi, carry[2:]))

    out = lax.fori_loop(0, seglen, fix, (cr, ci) + tuple(acc0))
    tr, ti = out[0], out[1]
    leaving = (_edge_row(er + tr, rev), _edge_row(ei + ti, rev))
    return leaving if acc_fn is None else (leaving, out[2:])


RCH = 256
CSEG = C // NSEG
TSEG = T // NSEG
UCOL0 = (QW + 2 * KVW) // UW


REGIONS = ((0, TSEG), (T, CSEG))


def _state_access(ref, lead=()):
    def get(rows):
        return ref[(*lead, rows, slice(0, BW))], ref[(*lead, rows, slice(BW, 2 * BW))]

    def put(rows, re, im):
        ref[(*lead, rows, slice(0, BW))] = re
        ref[(*lead, rows, slice(BW, 2 * BW))] = im

    return get, put


def _interleave_rows(src_ref, dst_ref, regions=REGIONS):
    for base, seglen in regions:
        def body(j, carry, base=base, seglen=seglen):
            dst_ref[pl.ds(pl.multiple_of(base + j * NSEG, NSEG), NSEG), :] = src_ref[pl.ds(base + j, NSEG, stride=seglen), :]
            return carry

        lax.fori_loop(0, seglen, body, 0, unroll=8)


def _deinterleave_rows(src_ref, dst_ref, regions=REGIONS):
    for base, seglen in regions:
        def body(j, carry, base=base, seglen=seglen):
            dst_ref[pl.ds(base + j, NSEG, stride=seglen), :] = src_ref[pl.ds(pl.multiple_of(base + j * NSEG, NSEG), NSEG), :]
            return carry

        lax.fori_loop(0, seglen, body, 0, unroll=8)


def _s5_fwd(proj, dskip, lam, bmat, cmat):
    def kern(u_ref, dk_ref, lam_ref, b_ref, c_ref, s_ref, ssm_ref, ge_ref, get_ref, up_ref, yp_ref):
        d = pl.program_id(1)

        @pl.when(d == 0)
        def _():
            _interleave_rows(u_ref, up_ref)

        bm = b_ref[0, 0].astype(BF16)
        for r0 in range(0, TA, RCH):
            s_ref[0, 0, r0:r0 + RCH, :] = jnp.dot(up_ref[r0:r0 + RCH, :].astype(BF16), bm, preferred_element_type=F32)
        lr = lam_ref[0, 0, :, 0:BW]
        li = lam_ref[0, 0, :, BW:2 * BW]
        zero = jnp.zeros((NSEG, BW), F32)
        get, put = _state_access(s_ref, (0, 0))
        mid = _seg_scan(get, put, T, CSEG, lr, li, d, (zero, zero))
        _seg_scan(get, put, 0, TSEG, lr, li, d, mid)
        cm = c_ref[0, 0].astype(BF16)
        for r0 in range(0, T, RCH):
            y = lax.dot_general(s_ref[0, 0, r0:r0 + RCH, :].astype(BF16), cm, (((1,), (1,)), ((), ())), preferred_element_type=F32)

            @pl.when(d == 0)
            def _(y=y, r0=r0):
                yp_ref[r0:r0 + RCH, :] = y + dk_ref[...] * up_ref[r0:r0 + RCH, :]

            @pl.when(d == 1)
            def _(y=y, r0=r0):
                yp_ref[r0:r0 + RCH, :] += y

        @pl.when(d == 1)
        def _():
            _deinterleave_rows(yp_ref, ssm_ref, REGIONS[:1])
            for r0 in range(0, T, RCH):
                ge = _gelu(ssm_ref[r0:r0 + RCH, :])
                ge_ref[r0:r0 + RCH, :] = ge.astype(ge_ref.dtype)
                get_ref[:, r0:r0 + RCH] = ge.T.astype(get_ref.dtype)

    blk4 = lambda shape: pl.BlockSpec((1, 1) + shape, lambda b, d: (d, b, 0, 0))
    return pl.pallas_call(
        kern,
        name="s5_fwd",
        grid=(NBLK, 2),
        in_specs=[pl.BlockSpec((TA, UW), lambda b, d: (0, UCOL0 + b)), pl.BlockSpec((1, UW), lambda b, d: (0, b)),
                  blk4((8, 2 * BW)), blk4((UW, 2 * BW)), blk4((UW, 2 * BW))],
        out_specs=[blk4((TA, 2 * BW)), pl.BlockSpec((T, UW), lambda b, d: (0, b)), pl.BlockSpec((T, UW), lambda b, d: (0, b)),
                   pl.BlockSpec((UW, T), lambda b, d: (b, 0))],
        out_shape=[jax.ShapeDtypeStruct((2, NBLK, TA, 2 * BW), F32), jax.ShapeDtypeStruct((T, SW), F32),
                   jax.ShapeDtypeStruct((T, SW), BF16), jax.ShapeDtypeStruct((SW, T), BF16)],
        scratch_shapes=[pltpu.VMEM((TA, UW), F32), pltpu.VMEM((T, UW), F32)],
        compiler_params=_cparams(("parallel", "arbitrary")),
    )(proj, dskip, lam, bmat, cmat)


def _s5_bwd(d_ge, ssm, proj, dskip, states, lam, bmat, cmat):
    nt = (((1,), (1,)), ((), ()))
    tn = (((0,), (0,)), ((), ()))

    def kern(dge_ref, ssm_ref, u_ref, dk_ref, s_ref, lam_ref, b_ref, c_ref,
             du_ref, ddk_ref, dlam_ref, db_ref, dc_ref, g_ref, dua_ref, dssm_ref, up_ref, nat_ref):
        d = pl.program_id(1)

        @pl.when(d == 0)
        def _():
            ddk = jnp.zeros((1, UW), F32)
            for r0 in range(0, T, RCH):
                rs = slice(r0, r0 + RCH)
                _, pull = jax.vjp(_gelu, ssm_ref[rs, :])
                dssm = pull(dge_ref[rs, :])[0]
                nat_ref[rs, :] = dssm
                ddk = ddk + jnp.sum(dssm * u_ref[rs, :], axis=0, keepdims=True)
            ddk_ref[...] = ddk
            _interleave_rows(nat_ref, dssm_ref, REGIONS[:1])
            _interleave_rows(u_ref, up_ref)
            for r0 in range(0, T, RCH):
                dua_ref[r0:r0 + RCH, :] = dssm_ref[r0:r0 + RCH, :] * dk_ref[...]
            dua_ref[T:TA, :] = jnp.zeros((C, UW), F32)

        cm = c_ref[0, 0].astype(BF16)
        for r0 in range(0, T, RCH):
            g_ref[r0:r0 + RCH, :] = jnp.dot(dssm_ref[r0:r0 + RCH, :].astype(BF16), cm, preferred_element_type=F32)
        g_ref[T:TA, :] = jnp.zeros((C, 2 * BW), F32)
        lr = lam_ref[0, 0, :, 0:BW]
        li = lam_ref[0, 0, :, BW:2 * BW]
        zero = jnp.zeros((NSEG, BW), F32)
        get_g, put_g = _state_access(g_ref)

        get_s, _ = _state_access(s_ref, (0, 0))

        def dlam_fold(base, seglen, s_in):
            def rows(j):
                return pl.ds(pl.multiple_of(base + j * NSEG, NSEG), NSEG)

            jb = jnp.where(d == 0, 0, seglen - 1)
            jn = jnp.where(d == 0, seglen - 1, 0)
            sp = get_s(rows(jn))
            edge = (_shift_rows(sp[0], d, s_in[0]), _shift_rows(sp[1], d, s_in[1]))

            def fold(j, gr, gi, acc):
                jp = jnp.clip(jnp.where(d == 0, j - 1, j + 1), 0, seglen - 1)
                sr, si = get_s(rows(jp))
                sr = jnp.where(j == jb, edge[0], sr)
                si = jnp.where(j == jb, edge[1], si)
                return acc[0] + (gr * sr + gi * si), acc[1] + (gi * sr - gr * si)

            return fold

        r_mid = jnp.where(d == 0, TA - 1, T)
        s_mid = tuple(jnp.broadcast_to(t, (NSEG, BW)) for t in get_s(pl.ds(r_mid, 1)))
        mid, acc = _seg_scan(get_g, put_g, 0, TSEG, lr, -li, 1 - d, (zero, zero), dlam_fold(0, TSEG, s_mid), (zero, zero))
        _, acc = _seg_scan(get_g, put_g, T, CSEG, lr, -li, 1 - d, mid, dlam_fold(T, CSEG, (zero, zero)), acc)
        dlam_ref[0, 0, :, 0:BW] = acc[0]
        dlam_ref[0, 0, :, BW:2 * BW] = acc[1]

        bm = b_ref[0, 0].astype(BF16)
        db = jnp.zeros((UW, 2 * BW), F32)
        dc = jnp.zeros((UW, 2 * BW), F32)
        for r0 in range(0, TA, RCH):
            rs = slice(r0, r0 + RCH)
            g = g_ref[rs, :].astype(BF16)
            dua_ref[rs, :] += lax.dot_general(g, bm, nt, preferred_element_type=F32)
            db = db + lax.dot_general(up_ref[rs, :].astype(BF16), g, tn, preferred_element_type=F32)
            if r0 < T:
                dc = dc + lax.dot_general(dssm_ref[rs, :].astype(BF16), s_ref[0, 0, rs, :].astype(BF16), tn,
                                          preferred_element_type=F32)
        db_ref[0, 0] = db
        dc_ref[0, 0] = dc

        @pl.when(d == 1)
        def _():
            _deinterleave_rows(dua_ref, nat_ref)
            du_ref[...] = nat_ref[...].astype(du_ref.dtype)

    blk4 = lambda shape: pl.BlockSpec((1, 1) + shape, lambda b, d: (d, b, 0, 0))
    lat = pl.BlockSpec((T, UW), lambda b, d: (0, b))
    vec = pl.BlockSpec((1, UW), lambda b, d: (0, b))
    return pl.pallas_call(
        kern,
        name="s5_bwd",
        grid=(NBLK, 2),
        in_specs=[lat, lat, pl.BlockSpec((TA, UW), lambda b, d: (0, UCOL0 + b)), vec,
                  blk4((TA, 2 * BW)), blk4((8, 2 * BW)), blk4((UW, 2 * BW)), blk4((UW, 2 * BW))],
        out_specs=[pl.BlockSpec((TA, UW), lambda b, d: (0, b)), vec, blk4((8, 2 * BW)), blk4((UW, 2 * BW)), blk4((UW, 2 * BW))],
        out_shape=[jax.ShapeDtypeStruct((TA, SW), BF16), jax.ShapeDtypeStruct((1, SW), F32),
                   jax.ShapeDtypeStruct((2, NBLK, 8, 2 * BW), F32),
                   jax.ShapeDtypeStruct((2, NBLK, UW, 2 * BW), F32), jax.ShapeDtypeStruct((2, NBLK, UW, 2 * BW), F32)],
        scratch_shapes=[pltpu.VMEM((TA, 2 * BW), F32), pltpu.VMEM((TA, UW), F32), pltpu.VMEM((T, UW), F32),
                        pltpu.VMEM((TA, UW), F32), pltpu.VMEM((TA, UW), F32)],
        compiler_params=_cparams(("parallel", "arbitrary")),
    )(d_ge, ssm, proj, dskip, states, lam, bmat, cmat)


TR = 256
TN_WIDE = 1024


def _vjp_rows(f, primals, cots, n_row):
    _, pull = jax.vjp(f, *primals)
    g = pull(cots)
    return list(g[:n_row]), list(g[n_row:])


class _GradDict(dict):
    def __init__(self, on_set=None):
        super().__init__()
        self._on_set = on_set
        self.tokens = {}

    def __setitem__(self, key, value):
        super().__setitem__(key, value)
        if self._on_set is not None:
            self._on_set(self)

    def order(self, key):
        return self.tokens.get(key, self.get(key))

    def finish(self, key, after):
        if self.on_finish is None:
            return ()
        return (self.on_finish(key, after),)

    on_finish = None


def _local_step(x, ctx, tgt, mod_lat, mod_ctx, wb, sp, on_grad=None, on_loss=None, on_finish=None, on_early=None):
    sh1, sc1, g1, sh2, sc2, g2 = [mod_lat[:, i * D:(i + 1) * D] for i in range(6)]
    csh1, csc1 = mod_ctx[:, 0:D], mod_ctx[:, D:2 * D]
    tabs = _rope_tables()
    sink = sp["attn_sink"].reshape(1, NH)
    dskip = sp["ssm_d"].reshape(1, SW)
    lg_mix, lb_mix = sp["ln_mix_g"].reshape(1, D), sp["ln_mix_b"].reshape(1, D)
    lg_mlp, lb_mlp = sp["ln_mlp_g"].reshape(1, D), sp["ln_mlp_b"].reshape(1, D)
    b1, b2 = sp["b_mlp1"].reshape(1, DFF), sp["b_mlp2"].reshape(1, D)
    s5_names = ("ssm_a_re", "ssm_a_im", "ssm_log_dt", "ssm_b_re", "ssm_b_im", "ssm_c_re", "ssm_c_im")
    (lam, bmat, cmat), s5_pull = jax.vjp(_s5_prep, *[sp[n] for n in s5_names])

    def ln_mod2(rv, vv):
        h = _f_ln_mod(rv[0], vv[0], vv[1])
        return [h, h], []

    h_lat, h_lat_t = _rowwise(ln_mod2, [(x, D, 0, 0)], [sc1, sh1], [(D, BF16), (D, BF16, True)], [], nrows=T, tr=TR, name="ln1_lat")
    h_ctx, h_ctx_t = _rowwise(ln_mod2, [(ctx, D, 0, 0)], [csc1, csh1], [(D, BF16), (D, BF16, True)], [], nrows=C, tr=TR,
                              name="ln1_ctx")
    h1 = jnp.concatenate([h_lat, h_ctx], 0)
    h1_t = jnp.concatenate([h_lat_t, h_ctx_t], 1)
    proj = _matmul(h1, wb["w_in"], mode="nn", name="proj", tm=768, tn=TN_WIDE)
    attn = _attn_fwd(proj, sink, tabs)
    states, ssm, ge, ge_t = _s5_fwd(proj, dskip, lam, bmat, cmat)
    z = _matmul(ge, wb["w_glu"], mode="nn", name="glu_mm", tm=1024, tn=1024)

    def glu_act(rv, vv):
        g_ = _f_glu(rv[0])
        return [g_, g_], []

    glu, glu_t = _rowwise(glu_act, [(z, 2 * SW, 0, 0)], [], [(SW, BF16), (SW, BF16, True)], [], nrows=T, tr=TR, name="glu_act")
    attn_d = _matmul(attn, wb["w_attn_up"], mode="nn", name="attn_up", tm=1024, tn=512)
    ssm_d = _matmul(glu, wb["w_ssm_up"], mode="nn", name="ssm_up", tm=1024, tn=512)
    ga_cb, gs_cb = (QW + 2 * KVW + SW) // D, (QW + 2 * KVW + SW) // D + 1

    def mix(rv, vv):
        m_ = _f_mix(*rv)
        return [m_, m_], []

    mixv, mix_t = _rowwise(mix, [(proj, D, ga_cb, 0), (proj, D, gs_cb, 0), (attn_d, D, 0, 0), (ssm_d, D, 0, 0)], [],
                           [(D, BF16), (D, BF16, True)], [], nrows=T, tr=TR, name="mix")
    y = _matmul(mixv, wb["w_out"], mode="nn", name="out_proj", tm=1024, tn=TN_WIDE)

    def post1(rv, vv):
        x1, h2 = _f_post1(rv[0], rv[1], *vv)
        return [x1, h2, h2], []

    x1, h2, h2_t = _rowwise(post1, [(x, D, 0, 0), (y, D, 0, 0)], [g1, lg_mix, lb_mix, sc2, sh2],
                            [(D, F32), (D, BF16), (D, BF16, True)], [], nrows=T, tr=TR, name="post1")

    def relu_sq(acc):
        r = jnp.maximum(acc, 0.0)
        return r, r * r, r * r

    r_act, act, act_t = _matmul(h2, wb["w_mlp1"], mode="nn", name="mlp1", tm=1024, tn=TN_WIDE, bias=b1,
                                out_dtypes=(BF16, BF16, BF16), out_t=(False, False, True), epilogue=relu_sq)
    mlp = _matmul(act, wb["w_mlp2"], mode="nn", name="mlp2", tm=512, tn=512)

    def loss_fb(rv, vv):
        x1_t, mlp_t, tgt_t = rv
        g2_v, lg_v, lb_v, b2_v = vv
        f = lambda a, m, g, p, q, b: _f_loss(a, m, tgt_t, g, p, q, b)
        val, grads = jax.value_and_grad(f, argnums=(0, 1, 2, 3, 4, 5))(x1_t, mlp_t, g2_v, lg_v, lb_v, b2_v)
        dx1, dmlp, dg2, dlg, dlb, db2 = grads
        return [dx1, dmlp], [jnp.reshape(val, (1, 1)), dg2, dlg, dlb, db2]

    dx1_a, d_mlp, loss_p, d_g2, d_lg_mlp, d_lb_mlp, d_b2 = _rowwise(
        loss_fb, [(x1, D, 0, 0), (mlp, D, 0, 0), (tgt, D, 0, 0)], [g2, lg_mlp, lb_mlp, b2],
        [(D, F32), (D, BF16)], [(1, 1), (1, D), (1, D), (1, D), (1, D)], nrows=T, tr=TR, name="loss_fb")

    gw = _GradDict(on_grad)
    gw.on_finish = on_finish
    loss_done = () if on_loss is None else (on_loss(loss_p),)
    gw["w_mlp2"] = _matmul(act_t, d_mlp, mode="nn", name="dw_mlp2", out_dtypes=(BF16,), tm=1024, tn=TN_WIDE, after=loss_done)
    da, = (_matmul(d_mlp, wb["w_mlp2"], mode="nt", name="d_act", out_dtypes=(BF16,), tm=1024, tn=TN_WIDE,
                   extras=(r_act,), epilogue=lambda acc, r: (acc * (2.0 * r.astype(F32)),), after=(gw.order("w_mlp2"),)),)
    pin = gw.finish("w_mlp2", da)
    ones = jnp.ones((8, T), BF16)
    d_b1 = _matmul(ones, da, mode="nn", name="db_mlp1", tm=8, tn=2048)[0:1]
    gw["w_mlp1"] = _matmul(h2_t, da, mode="nn", name="dw_mlp1", out_dtypes=(BF16,), tm=1024, tn=TN_WIDE, after=pin)
    dh2 = _matmul(da, wb["w_mlp1"], mode="nt", name="d_h2", tm=512, tn=512, after=(gw.order("w_mlp1"),))

    def post1_b(rv, vv):
        x_t, y_t, dx1_t, dh2_t = rv
        gr, gv = _vjp_rows(_f_post1, (x_t, y_t, *vv), (dx1_t, dh2_t), 2)
        return [gr[0], gr[1]], gv

    dx_a, dy, d_g1, d_lg_mix, d_lb_mix, d_sc2, d_sh2 = _rowwise(
        post1_b, [(x, D, 0, 0), (y, D, 0, 0), (dx1_a, D, 0, 0), (dh2, D, 0, 0)], [g1, lg_mix, lb_mix, sc2, sh2],
        [(D, F32), (D, BF16)], [(1, D)] * 5, nrows=T, tr=TR, name="post1_bwd")
    gw["w_out"] = _matmul(mix_t, dy, mode="nn", name="dw_out", out_dtypes=(BF16,), tm=1024, tn=TN_WIDE)
    dmix = _matmul(dy, wb["w_out"], mode="nt", name="d_mix", tm=1024, tn=TN_WIDE, after=(gw.order("w_out"),))

    def mix_b(rv, vv):
        gr, _ = _vjp_rows(_f_mix, tuple(rv[:4]), rv[4], 4)
        return gr, []

    d_ga, d_gs, d_attn_d, d_ssm_d = _rowwise(
        mix_b, [(proj, D, ga_cb, 0), (proj, D, gs_cb, 0), (attn_d, D, 0, 0), (ssm_d, D, 0, 0), (dmix, D, 0, 0)], [],
        [(D, BF16)] * 4, [], nrows=T, tr=TR, name="mix_bwd", pad_rows=C, pad_outs=(0, 1))
    pin = gw.finish("w_mlp1", d_ga)
    gw["w_attn_up"] = _matmul(attn, d_attn_d, mode="tn", name="dw_attn_up", out_dtypes=(BF16,), tm=512, tn=1024, tk=1024, after=pin)
    d_attn = _matmul(d_attn_d, wb["w_attn_up"], mode="nt", name="d_attn", out_dtypes=(BF16,), tm=1024, tn=512)
    gw["w_ssm_up"] = _matmul(glu_t, d_ssm_d, mode="nn", name="dw_ssm_up", out_dtypes=(BF16,), tm=512, tn=TN_WIDE)
    d_glu = _matmul(d_ssm_d, wb["w_ssm_up"], mode="nt", name="d_glu", tm=1024, tn=512, after=(gw.order("w_attn_up"), gw.order("w_ssm_up")))

    def glu_b(rv, vv):
        gr, _ = _vjp_rows(_f_glu, (rv[0],), rv[1], 1)
        return gr, []

    dz, = _rowwise(glu_b, [(z, 2 * SW, 0, 0), (d_glu, SW, 0, 0)], [], [(2 * SW, BF16)], [], nrows=T, tr=TR, name="glu_bwd")
    gw["w_glu"] = _matmul(ge_t, dz, mode="nn", name="dw_glu", out_dtypes=(BF16,), tm=512, tn=TN_WIDE)
    d_ge = _matmul(dz, wb["w_glu"], mode="nt", name="d_ge", tm=1024, tn=512, after=(gw.order("w_glu"),))

    du_all, d_dskip, dlam, dbmat, dcmat = _s5_bwd(d_ge, ssm, proj, dskip, states, lam, bmat, cmat)
    s5_grads = s5_pull((dlam, dbmat, dcmat))
    early = dict(zip(s5_names, s5_grads), ssm_d=d_dskip)
    if on_early is not None:
        on_early(early)
    pin = gw.finish("w_glu", du_all)

    dq, dk, dv, dsink = _attn_bwd(proj, d_attn, sink, tabs)
    dproj = jnp.concatenate([dq, dk, dv, du_all, d_ga, d_gs], 1)
    gw["w_in"] = _matmul(h1_t, dproj, mode="nn", name="dw_in", out_dtypes=(BF16,), tm=1024, tn=TN_WIDE, after=pin)
    pin = gw.finish("w_in", gw["w_in"])
    dh1 = _matmul(dproj, wb["w_in"], mode="nt", name="d_h1", tm=768, tn=512, after=pin)

    def ln1_b(rv, vv):
        x_t, dh_t, dxa_t = rv
        gr, gv = _vjp_rows(_f_ln_mod, (x_t, vv[0], vv[1]), dh_t, 1)
        return [gr[0] + dxa_t], gv

    grad_x, d_sc1, d_sh1 = _rowwise(ln1_b, [(x, D, 0, 0), (dh1, D, 0, 0), (dx_a, D, 0, 0)], [sc1, sh1],
                                    [(D, F32)], [(1, D), (1, D)], nrows=T, tr=TR, name="ln1_lat_bwd")

    def ln1c_b(rv, vv):
        _, gv = _vjp_rows(_f_ln_mod, (rv[0], vv[0], vv[1]), rv[1], 1)
        return [], gv

    d_csc1, d_csh1 = _rowwise(ln1c_b, [(ctx, D, 0, 0), (dh1, D, 0, T // TR)], [csc1, csh1],
                              [], [(1, D), (1, D)], nrows=C, tr=TR, name="ln1_ctx_bwd")

    d_mod_lat = jnp.concatenate([d_sh1, d_sc1, d_g1, d_sh2, d_sc2, d_g2], 1)
    zv = jnp.zeros((1, D), F32)
    d_mod_ctx = jnp.concatenate([d_csh1, d_csc1, zv, zv, zv, zv], 1)
    gs = {n: g for n, g in zip(s5_names, s5_grads)}
    gs["attn_sink"] = dsink[:, 0]
    gs["ssm_d"] = d_dskip
    gs["ln_mix_g"], gs["ln_mix_b"] = d_lg_mix, d_lb_mix
    gs["ln_mlp_g"], gs["ln_mlp_b"] = d_lg_mlp, d_lb_mlp
    gs["b_mlp1"], gs["b_mlp2"] = d_b1, d_b2
    return loss_p, grad_x, d_mod_lat, d_mod_ctx, gw, gs


def _my_pos():
    return lax.axis_index("x"), lax.axis_index("y"), lax.axis_index("c")


def _flip(p, bit):
    return 1 - p if bit else p


def _peer(pos, k):
    x, y, c = pos
    return (_flip(x, (k >> 2) & 1), _flip(y, (k >> 1) & 1), _flip(c, k & 1))


def _lin(pos):
    return 4 * pos[0] + 2 * pos[1] + pos[2]


def _allgather_small(v, name):
    r, w = v.shape

    def body(v_ref, out_ref, send_sems, recv_sems, local_sem):
        me = _my_pos()
        mine = pltpu.make_async_copy(v_ref, out_ref.at[_lin(me)], local_sem)
        mine.start()
        sends = []
        for k in range(1, N_DEV):
            cp = pltpu.make_async_remote_copy(src_ref=v_ref, dst_ref=out_ref.at[_lin(me)], send_sem=send_sems.at[k - 1],
                                              recv_sem=recv_sems.at[k - 1], device_id=_peer(me, k), device_id_type=MESH)
            cp.start()
            sends.append(cp)
        for k in range(1, N_DEV):
            peer = _peer(me, k)
            pltpu.make_async_remote_copy(src_ref=v_ref, dst_ref=out_ref.at[_lin(peer)], send_sem=send_sems.at[k - 1],
                                         recv_sem=recv_sems.at[k - 1], device_id=peer, device_id_type=MESH).wait_recv()
        for cp in sends:
            cp.wait_send()
        mine.wait()

    return pl.pallas_call(
        body,
        name=name,
        out_shape=jax.ShapeDtypeStruct((N_DEV, r, w), v.dtype),
        in_specs=[pl.BlockSpec(memory_space=pltpu.VMEM)],
        out_specs=pl.BlockSpec(memory_space=pltpu.VMEM),
        scratch_shapes=[pltpu.SemaphoreType.DMA((N_DEV - 1,)), pltpu.SemaphoreType.DMA((N_DEV - 1,)), pltpu.SemaphoreType.DMA],
        compiler_params=pltpu.CompilerParams(vmem_limit_bytes=VMEM_LIMIT_BYTES),
    )(v)


def _block_of(ref, kind, idx, n):
    start = pl.multiple_of(idx * n, 128)
    if kind == "col":
        return ref.at[:, pl.ds(start, n)]
    return ref.at[pl.ds(start, n), :]


def _handshake(peers):
    barrier = pltpu.get_barrier_semaphore()
    for peer in peers:
        pl.semaphore_signal(barrier, inc=1, device_id=peer, device_id_type=MESH)
    pl.semaphore_wait(barrier, len(peers))


def _allgather_weights_seq(shards, kinds, name, collective_id):
    nt = len(shards)
    hbm = pltpu.MemorySpace.HBM
    ins = [jax.new_ref(s, memory_space=hbm) for s in shards]
    outs = []
    for s, kind in zip(shards, kinds):
        k, n = s.shape
        shape = (k, n * N_DEV) if kind == "col" else (k * N_DEV, n)
        outs.append(jax.empty_ref(jax.ShapeDtypeStruct(shape, s.dtype), memory_space=hbm))

    @functools.partial(
        pl.kernel, mesh=plsc.ScalarSubcoreMesh(axis_name="seq", num_cores=1), name=name,
        scratch_types=(pltpu.SemaphoreType.DMA((nt, N_DEV - 1)), pltpu.SemaphoreType.DMA((nt, N_DEV - 1)),
                       pltpu.SemaphoreType.DMA((nt,))),
        compiler_params=pltpu.CompilerParams(collective_id=collective_id))
    def launch(send_sems, recv_sems, local_sems):
        x, y, c = _my_pos()
        me, sibling = (x, y, c), (x, y, 1 - c)
        chips = [(1 - x, y), (x, 1 - y), (1 - x, 1 - y)]
        _handshake([sibling] + [(*chip, c) for chip in chips])

        def blk(t, pos):
            n = shards[t].shape[1] if kinds[t] == "col" else shards[t].shape[0]
            return _block_of(outs[t], kinds[t], _lin(pos), n)

        def copy(t, k, block, to, src=None):
            return pltpu.make_async_remote_copy(src_ref=blk(t, block) if src is None else src, dst_ref=blk(t, block),
                                                send_sem=send_sems.at[t, k], recv_sem=recv_sems.at[t, k],
                                                device_id=to, device_id_type=MESH)

        local, sends = [], []
        for t in range(nt):
            mine = pltpu.make_async_copy(ins[t], blk(t, me), local_sems.at[t])
            mine.start()
            local.append(mine)
            first = [copy(t, 0, me, sibling, src=ins[t])]
            first += [copy(t, 1 + j, me, (*chip, c), src=ins[t]) for j, chip in enumerate(chips)]
            for cp in first:
                cp.start()
            sends += first
        for t in range(nt):
            for j, chip in enumerate(chips):
                copy(t, 1 + j, (*chip, c), me).wait_recv()
                fwd = copy(t, 4 + j, (*chip, c), sibling)
                fwd.start()
                sends.append(fwd)
        for t in range(nt):
            copy(t, 0, sibling, me).wait_recv()
            for j, chip in enumerate(chips):
                copy(t, 4 + j, (*chip, 1 - c), me).wait_recv()
        for cp in sends:
            cp.wait_send()
        for cp in local:
            cp.wait()

    launch()
    return [o[...] for o in outs]


def _allgather_small_seq(v, name, collective_id):
    hbm = pltpu.MemorySpace.HBM
    src = jax.new_ref(v, memory_space=hbm)
    out = jax.empty_ref(jax.ShapeDtypeStruct((N_DEV,) + v.shape, v.dtype), memory_space=hbm)

    @functools.partial(
        pl.kernel, mesh=plsc.ScalarSubcoreMesh(axis_name="seq", num_cores=1), name=name,
        scratch_types=(pltpu.SemaphoreType.DMA((N_DEV - 1,)), pltpu.SemaphoreType.DMA((N_DEV - 1,)), pltpu.SemaphoreType.DMA),
        compiler_params=pltpu.CompilerParams(collective_id=collective_id))
    def launch(send_sems, recv_sems, local_sem):
        me = _my_pos()
        _handshake([_peer(me, k) for k in range(1, N_DEV)])
        mine = pltpu.make_async_copy(src, out.at[_lin(me)], local_sem)
        mine.start()
        sends = []
        for k in range(1, N_DEV):
            cp = pltpu.make_async_remote_copy(src_ref=src, dst_ref=out.at[_lin(me)], send_sem=send_sems.at[k - 1],
                                              recv_sem=recv_sems.at[k - 1], device_id=_peer(me, k), device_id_type=MESH)
            cp.start()
            sends.append(cp)
        for k in range(1, N_DEV):
            peer = _peer(me, k)
            pltpu.make_async_remote_copy(src_ref=src, dst_ref=out.at[_lin(peer)], send_sem=send_sems.at[k - 1],
                                         recv_sem=recv_sems.at[k - 1], device_id=peer, device_id_type=MESH).wait_recv()
        for cp in sends:
            cp.wait_send()
        mine.wait()

    launch()
    return out[...]


N_CHIP = N_DEV // 2


def _chip_of(pos):
    return 2 * pos[0] + pos[1]


def _pair_exchange_seq(grads, kinds, name, collective_id):
    nt = len(grads)
    hbm = pltpu.MemorySpace.HBM
    shard_shapes = _shard_shapes(grads, kinds)
    ins = [jax.new_ref(g, memory_space=hbm) for g in grads]
    outs = [jax.empty_ref(jax.ShapeDtypeStruct((N_CHIP,) + s, g.dtype), memory_space=hbm) for s, g in zip(shard_shapes, grads)]

    @functools.partial(
        pl.kernel, mesh=plsc.ScalarSubcoreMesh(axis_name="seq", num_cores=1), name=name,
        scratch_types=(pltpu.SemaphoreType.DMA((nt, N_CHIP)), pltpu.SemaphoreType.DMA((nt, N_CHIP))),
        compiler_params=pltpu.CompilerParams(collective_id=collective_id))
    def launch(send_sems, recv_sems):
        x, y, c = _my_pos()
        sibling = (x, y, 1 - c)
        _handshake([sibling])
        copies = []
        for t in range(nt):
            n = shard_shapes[t][1] if kinds[t] == "col" else shard_shapes[t][0]
            for q in range(N_CHIP):
                cp = pltpu.make_async_remote_copy(src_ref=_block_of(ins[t], kinds[t], 2 * q + (1 - c), n), dst_ref=outs[t].at[q],
                                                  send_sem=send_sems.at[t, q], recv_sem=recv_sems.at[t, q],
                                                  device_id=sibling, device_id_type=MESH)
                cp.start()
                copies.append(cp)
        for cp in copies:
            cp.wait_recv()
        for cp in copies:
            cp.wait_send()

    launch()
    return [o[...] for o in outs]


def _pair_add(g, half, kind, name, after=()):
    nq, k, ns = half.shape
    tr = min(k, 512)
    c_idx = lax.axis_index("c").astype(jnp.int32).reshape(1)
    if kind == "col":
        g_spec = pl.BlockSpec((tr, ns), lambda q, i, c_ref: (i, 2 * q + c_ref[0]))
    else:
        g_spec = pl.BlockSpec((tr, ns), lambda q, i, c_ref: ((2 * q + c_ref[0]) * (k // tr) + i, 0))
    n_after = len(after)

    def kern(c_ref, g_ref, h_ref, *rest):
        o_ref = rest[n_after]
        o_ref[0] = (g_ref[...].astype(F32) + h_ref[0].astype(F32)).astype(o_ref.dtype)

    return pl.pallas_call(
        kern,
        name=name,
        grid_spec=pltpu.PrefetchScalarGridSpec(
            num_scalar_prefetch=1,
            grid=(nq, k // tr),
            in_specs=[g_spec, pl.BlockSpec((1, tr, ns), lambda q, i, c_ref: (q, i, 0))] + [pl.BlockSpec(memory_space=pl.ANY)] * n_after,
            out_specs=pl.BlockSpec((1, tr, ns), lambda q, i, c_ref: (q, i, 0)),
        ),
        out_shape=jax.ShapeDtypeStruct(half.shape, half.dtype),
        compiler_params=_cparams(("parallel", "parallel")),
    )(c_idx, g, half, *after)


def _chip_exchange_seq(psums, name, collective_id):
    nt = len(psums)
    hbm = pltpu.MemorySpace.HBM
    ins = [jax.new_ref(s, memory_space=hbm) for s in psums]
    outs = [jax.empty_ref(jax.ShapeDtypeStruct(s.shape, s.dtype), memory_space=hbm) for s in psums]

    @functools.partial(
        pl.kernel, mesh=plsc.ScalarSubcoreMesh(axis_name="seq", num_cores=1), name=name,
        scratch_types=(pltpu.SemaphoreType.DMA((nt, N_CHIP - 1)), pltpu.SemaphoreType.DMA((nt, N_CHIP - 1)),
                       pltpu.SemaphoreType.DMA((nt,))),
        compiler_params=pltpu.CompilerParams(collective_id=collective_id))
    def launch(send_sems, recv_sems, local_sems):
        me = _my_pos()
        peers = [_peer(me, k) for k in (2, 4, 6)]
        _handshake(peers)
        mine = _chip_of(me)
        local, sends = [], []
        for t in range(nt):
            cp = pltpu.make_async_copy(ins[t].at[mine], outs[t].at[mine], local_sems.at[t])
            cp.start()
            local.append(cp)
            for j, peer in enumerate(peers):
                cp = pltpu.make_async_remote_copy(src_ref=ins[t].at[_chip_of(peer)], dst_ref=outs[t].at[mine],
                                                  send_sem=send_sems.at[t, j], recv_sem=recv_sems.at[t, j],
                                                  device_id=peer, device_id_type=MESH)
                cp.start()
                sends.append(cp)
        for t in range(nt):
            for j, peer in enumerate(peers):
                pltpu.make_async_remote_copy(src_ref=ins[t].at[mine], dst_ref=outs[t].at[_chip_of(peer)],
                                             send_sem=send_sems.at[t, j], recv_sem=recv_sems.at[t, j],
                                             device_id=peer, device_id_type=MESH).wait_recv()
        for cp in sends:
            cp.wait_send()
        for cp in local:
            cp.wait()

    launch()
    return [o[...] for o in outs]


def _shard_shapes(grads, kinds):
    return [(g.shape[0], g.shape[1] // N_DEV) if kind == "col" else (g.shape[0] // N_DEV, g.shape[1]) for g, kind in zip(grads, kinds)]


def _adam(g_slots, w, m, v, *, tr, name, after=()):
    ns, r, wd = g_slots.shape
    tr = min(tr, r)
    assert r % tr == 0, (name, r, tr)
    n_after = len(after)

    def kern(g_ref, w_ref, m_ref, v_ref, *rest):
        go_ref, d_ref, mo_ref, vo_ref = rest[n_after:]
        g = g_ref[0].astype(F32)
        for s in range(1, ns):
            g = g + g_ref[s].astype(F32)
        delta, m_new, v_new = _adam_update(g, w_ref[...], m_ref[...], v_ref[...])
        go_ref[...] = g
        d_ref[...] = delta
        mo_ref[...] = m_new
        vo_ref[...] = v_new

    tile = pl.BlockSpec((tr, wd), lambda i: (i, 0))
    return pl.pallas_call(
        kern,
        name=name,
        grid=(r // tr,),
        in_specs=[pl.BlockSpec((ns, tr, wd), lambda i: (0, i, 0)), tile, tile, tile] + [pl.BlockSpec(memory_space=pl.ANY)] * n_after,
        out_specs=[tile] * 4,
        out_shape=[jax.ShapeDtypeStruct((r, wd), F32)] * 4,
        compiler_params=_cparams(("parallel",)),
    )(g_slots, w, m, v, *after)


def _adam_update(g, w, m, v):
    m_new = ADAM_B1 * m + (1.0 - ADAM_B1) * g
    v_new = ADAM_B2 * v + (1.0 - ADAM_B2) * (g * g)
    m_hat = m_new / (1.0 - ADAM_B1 ** ADAM_STEP)
    v_hat = v_new / (1.0 - ADAM_B2 ** ADAM_STEP)
    return -ADAM_LR * (m_hat / (jnp.sqrt(v_hat) + ADAM_EPS) + ADAM_WD * w), m_new, v_new


def _lane_offsets(sizes):
    offs, o = [], 0
    for n in sizes:
        offs.append(o)
        o += -(-n // LANES) * LANES
    return offs, o


def _pack_lanes(parts):
    cols = []
    for p_ in parts:
        flat = p_.reshape(1, -1).astype(F32)
        cols.append(jnp.pad(flat, ((0, 0), (0, (-flat.shape[1]) % LANES))))
    return jnp.concatenate(cols, 1)


def _adam_lanes(g_slots, ws, ms, vs, *, name, after=()):
    ns = g_slots.shape[0]
    npar, n_after = len(ws), len(after)
    sizes = [w.shape[1] for w in ws]
    offs, _ = _lane_offsets(sizes)

    def kern(g_ref, *refs):
        w_refs, m_refs, v_refs = refs[:npar], refs[npar:2 * npar], refs[2 * npar:3 * npar]
        outs = refs[3 * npar + n_after:]
        g_all = g_ref[0]
        for s in range(1, ns):
            g_all = g_all + g_ref[s]
        for j in range(npar):
            g = g_all[:, offs[j]:offs[j] + sizes[j]]
            delta, m_new, v_new = _adam_update(g, w_refs[j][...], m_refs[j][...], v_refs[j][...])
            outs[4 * j][...] = g
            outs[4 * j + 1][...] = delta
            outs[4 * j + 2][...] = m_new
            outs[4 * j + 3][...] = v_new

    vmem = pl.BlockSpec(memory_space=pltpu.VMEM)
    res = pl.pallas_call(
        kern,
        name=name,
        in_specs=[vmem] * (1 + 3 * npar) + [pl.BlockSpec(memory_space=pl.ANY)] * n_after,
        out_specs=[vmem] * (4 * npar),
        out_shape=[jax.ShapeDtypeStruct((1, n), F32) for n in sizes for _ in range(4)],
        compiler_params=pltpu.CompilerParams(vmem_limit_bytes=VMEM_LIMIT_BYTES),
    )(g_slots, *ws, *ms, *vs, *after)
    return [tuple(res[4 * j:4 * j + 4]) for j in range(npar)]


SMALL = ("c_ctx", "b_ada", "attn_sink", "ssm_a_re", "ssm_a_im", "ssm_log_dt", "ssm_b_re", "ssm_b_im", "ssm_c_re", "ssm_c_im",
         "ssm_d", "ln_mix_g", "ln_mix_b", "b_mlp1", "b_mlp2", "ln_mlp_g", "ln_mlp_b")
BIG = ("w_in", "w_glu", "w_attn_up", "w_ssm_up", "w_out", "w_mlp1", "w_mlp2")
BIG_KIND = ("col", "col", "col", "col", "row", "col", "row")
AG_GROUPS = (("w_in",), ("w_glu", "w_attn_up", "w_ssm_up", "w_out"), ("w_mlp1",), ("w_mlp2",))
AG_COLLECTIVE_ID0 = 1
RS_GROUPS = (("w_mlp2",), ("w_mlp1",), ("w_out", "w_attn_up", "w_ssm_up", "w_glu"), ("w_in",))
RS_COLLECTIVE_ID0 = AG_COLLECTIVE_ID0 + len(AG_GROUPS)
SMALL_EARLY = ("ssm_a_re", "ssm_a_im", "ssm_log_dt", "ssm_b_re", "ssm_b_im", "ssm_c_re", "ssm_c_im", "ssm_d")
SMALL_LATE = tuple(n for n in SMALL if n not in SMALL_EARLY)
SMALL_COLLECTIVE_ID0 = RS_COLLECTIVE_ID0 + 2 * len(RS_GROUPS)
LANES = 128


def _pack(parts):
    rows = []
    for p in parts:
        flat = p.reshape(-1).astype(F32)
        pad = (-flat.shape[0]) % LANES
        rows.append(jnp.pad(flat, (0, pad)).reshape(-1, LANES))
    packed = jnp.concatenate(rows, 0)
    return jnp.pad(packed, ((0, (-packed.shape[0]) % 8), (0, 0)))


def _unpack(packed, shapes):
    out, r0 = [], 0
    for s in shapes:
        n = math.prod(s)
        nr = -(-n // LANES)
        out.append(packed[r0:r0 + nr].reshape(-1)[:n].reshape(s))
        r0 += nr
    return out


WEIGHTS = ("c_ctx", "w_ada", "b_ada", "w_in", "attn_sink", "ssm_a_re", "ssm_a_im", "ssm_log_dt", "ssm_b_re", "ssm_b_im",
           "ssm_c_re", "ssm_c_im", "ssm_d", "w_glu", "w_attn_up", "w_ssm_up", "w_out", "ln_mix_g", "ln_mix_b", "w_mlp1",
           "b_mlp1", "w_mlp2", "b_mlp2", "ln_mlp_g", "ln_mlp_b")
ADA_COLS = 6 * D // N_DEV


def _step(x, c, ctx, loss_target, p, m, v):
    me = _lin(_my_pos())
    x2, ctx2, tgt2 = x[0], ctx[0], loss_target[0]

    wb = {}
    for gi, group in enumerate(AG_GROUPS):
        full = _allgather_weights_seq([p[n][0].astype(BF16) for n in group], [BIG_KIND[BIG.index(n)] for n in group],
                                      "allgather_seq%d" % gi, AG_COLLECTIVE_ID0 + gi)
        wb.update(zip(group, full))

    c_all = _allgather_small(jnp.broadcast_to(c, (8, D)), "gather_c")[:, 0, :]
    cc = p["c_ctx"].reshape(1, D)
    s_in = jnp.concatenate([c_all, cc, jnp.zeros((7, D), F32)], 0)
    s_act, = _rowwise(lambda rv, vv: ([_silu(rv[0])], []), [(s_in, D, 0, 0)], [], [(D, F32)], [], nrows=16, tr=16, name="silu_c")
    b_mine = lax.dynamic_slice_in_dim(p["b_ada"], me * ADA_COLS, ADA_COLS, axis=1)
    mod_part = _matmul(s_act, p["w_ada"][0], mode="nn", name="ada_fwd", tm=16, tn=512, bias=b_mine)
    mod_all = _allgather_small(mod_part, "gather_mod")
    mod_lat = lax.dynamic_index_in_dim(mod_all, me, axis=1, keepdims=False).reshape(1, 6 * D)
    mod_ctx = mod_all[:, 8, :].reshape(1, 6 * D)

    sp = {n: p[n][0] for n in SMALL if n not in ("c_ctx", "b_ada")}
    recv, halves = {}, {}

    def on_grad(gw):
        for gi, group in enumerate(RS_GROUPS):
            if gi not in halves and all(n in gw for n in group):
                kinds = [BIG_KIND[BIG.index(n)] for n in group]
                halves[gi] = (dict(gw), _pair_exchange_seq([gw[n] for n in group], kinds, "pair_exchange%d" % gi, RS_COLLECTIVE_ID0 + 2 * gi))

    def on_finish(key, after):
        gi = [i for i, group in enumerate(RS_GROUPS) if key in group][0]
        group = RS_GROUPS[gi]
        grads, half = halves[gi]
        prev = tuple(recv[n] for n in RS_GROUPS[gi - 1][:1]) if gi else ()
        if gi == len(RS_GROUPS) - 1:
            prev += (small["early"],)
        psums =[_pair_add(grads[n], h, BIG_KIND[BIG.index(n)], "pair_add_" + n, after=(after,) + prev) for n, h in zip(group, half)]
        recv.update(zip(group, _chip_exchange_seq(psums, "chip_exchange%d" % gi, RS_COLLECTIVE_ID0 + 2 * gi + 1)))
        return psums[-1]

    small = {}

    def on_early(gs_early):
        small["early"] = _allgather_small_seq(_pack([gs_early[n] for n in SMALL_EARLY]), "gather_small_early", SMALL_COLLECTIVE_ID0)

    total = {}

    def on_loss(loss_p):
        total["loss"] = lax.psum(loss_p[0, 0], ("x", "y", "c"))
        return total["loss"].reshape(1, 1)

    loss_p, grad_x, d_mod_lat, d_mod_ctx, gw, gs = _local_step(x2, ctx2, tgt2, mod_lat, mod_ctx, wb, sp, on_grad, on_loss, on_finish, on_early)

    g_early = small["early"]
    res = {}
    last = ()

    def adam_small(names, g_pack, tag, after):
        sm = _adam(g_pack, _pack([p[n] for n in names]), _pack([m[n] for n in names]), _pack([v[n] for n in names]),
                   tr=g_pack.shape[1], name="adam_small_" + tag, after=after)
        shapes = [p[n].shape for n in names]
        for j, outs in enumerate(zip(*[_unpack(a, shapes) for a in sm])):
            res[names[j]] = outs
        return (sm[0],)

    for gi, group in enumerate(RS_GROUPS):
        if gi == len(RS_GROUPS) - 1:
            last = adam_small(SMALL_EARLY, g_early, "early", last)
        for n in group:
            res[n] = _adam(recv[n], p[n][0], m[n][0], v[n][0], tr=256, name="adam_" + n, after=last)
            last = (res[n][0],)

    dm = jnp.concatenate([d_mod_lat, d_mod_ctx, jnp.zeros((6, 6 * D), F32)], 0)
    dm_all = _allgather_small_seq(dm, "gather_dmod", SMALL_COLLECTIVE_ID0 + 1)
    dm_all = lax.optimization_barrier((dm_all,) + last)[0]
    dm2 = jnp.concatenate([dm_all[:, 0, :], dm_all[:, 1, :]], 0)
    dm2_mine = lax.dynamic_slice_in_dim(dm2, me * ADA_COLS, ADA_COLS, axis=1)
    s2 = jnp.concatenate([s_act[0:8], jnp.broadcast_to(s_act[8:9], (8, D))], 0)
    g_w_ada = _matmul(s2, dm2_mine, mode="tn", name="dw_ada", tm=512, tn=ADA_COLS, after=last)
    dsc_part = _matmul(dm2_mine[8:16], p["w_ada"][0], mode="nt", name="d_silu_cctx", tm=8, tn=512, after=last)

    def cctx_b(rv, vv):
        _, pull = jax.vjp(_silu, vv[0])
        return [], [pull(jnp.sum(rv[0], axis=0, keepdims=True))[0]]

    g_cctx, = _rowwise(cctx_b, [(dsc_part, D, 0, 0)], [cc], [], [(1, D)], nrows=8, tr=8, name="cctx_bwd")
    gs["c_ctx"] = g_cctx
    gs["b_ada"] = d_mod_lat + d_mod_ctx

    res["w_ada"] = _adam(g_w_ada[None], p["w_ada"][0], m["w_ada"][0], v["w_ada"][0], tr=256, name="adam_w_ada")

    g_late = _allgather_small_seq(_pack_lanes([gs[n] for n in SMALL_LATE]), "gather_small_late", SMALL_COLLECTIVE_ID0 + 2)
    row = lambda a: a.reshape(1, -1)
    late = _adam_lanes(g_late, [row(p[n]) for n in SMALL_LATE], [row(m[n]) for n in SMALL_LATE], [row(v[n]) for n in SMALL_LATE],
                       name="adam_small_late", after=(res["w_ada"][0],))
    res.update(zip(SMALL_LATE, late))

    outs = [total["loss"], grad_x[None]]
    for j in range(4):
        outs += [res[n][j].reshape(p[n].shape) for n in WEIGHTS]
    return tuple(outs)


def kernel(x, c, ctx, c_ctx, w_ada, b_ada, w_in, attn_sink, ssm_a_re, ssm_a_im, ssm_log_dt, ssm_b_re, ssm_b_im, ssm_c_re, ssm_c_im, ssm_d, w_glu, w_attn_up, w_ssm_up, w_out, ln_mix_g, ln_mix_b, w_mlp1, b_mlp1, w_mlp2, b_mlp2, ln_mlp_g, ln_mlp_b, loss_target, m_c_ctx, m_w_ada, m_b_ada, m_w_in, m_attn_sink, m_ssm_a_re, m_ssm_a_im, m_ssm_log_dt, m_ssm_b_re, m_ssm_b_im, m_ssm_c_re, m_ssm_c_im, m_ssm_d, m_w_glu, m_w_attn_up, m_w_ssm_up, m_w_out, m_ln_mix_g, m_ln_mix_b, m_w_mlp1, m_b_mlp1, m_w_mlp2, m_b_mlp2, m_ln_mlp_g, m_ln_mlp_b, v_c_ctx, v_w_ada, v_b_ada, v_w_in, v_attn_sink, v_ssm_a_re, v_ssm_a_im, v_ssm_log_dt, v_ssm_b_re, v_ssm_b_im, v_ssm_c_re, v_ssm_c_im, v_ssm_d, v_w_glu, v_w_attn_up, v_w_ssm_up, v_w_out, v_ln_mix_g, v_ln_mix_b, v_w_mlp1, v_b_mlp1, v_w_mlp2, v_b_mlp2, v_ln_mlp_g, v_ln_mlp_b):
    given = dict(locals())
    p = {n: given[n] for n in WEIGHTS}
    m = {n: given["m_" + n] for n in WEIGHTS}
    v = {n: given["v_" + n] for n in WEIGHTS}
    return _step(x, c, ctx, loss_target, p, m, v)
```

```python
import functools
import math

import jax
import jax.numpy as jnp
from jax import lax
from jax.experimental import pallas as pl
from jax.experimental.pallas import tpu as pltpu
from jax.experimental.pallas import tpu_sc as plsc

F32 = jnp.float32
BF16 = jnp.bfloat16

N_DEV = 8
D = 2048
T = 2048
C = 256
TA = T + C
GRID_W = 64
HD = 128
NH = 8
NKV = 2
GROUP = NH // NKV
WINDOW = 128
QW = NH * HD
KVW = NKV * HD
SW = D // 4
SG = 16
NG = SW // SG
SP = 64
DFF = 4 * D
IN_COLS = QW + 2 * KVW + SW + 2 * D
ALPHA = 2.0 ** 0.25
LN_EPS = 1e-6
NEG_INF = -1e30
ROPE_BASE = 10000.0
ATT_SCALE = HD ** -0.5

NSEG = 8
GBLK = 8
NBLK = NG // GBLK
BW = GBLK * SP
UW = GBLK * SG

ADAM_LR = 0.001
ADAM_B1 = 0.9
ADAM_B2 = 0.999
ADAM_EPS = 1e-08
ADAM_WD = 0.01
ADAM_STEP = 10

VMEM_LIMIT_BYTES = 56 * 1024 * 1024
MESH = pl.DeviceIdType.MESH


def _cparams(sem=None):
    return pltpu.CompilerParams(dimension_semantics=sem, vmem_limit_bytes=VMEM_LIMIT_BYTES)


def _matmul(a, b, *, mode, name, out_dtypes=(F32,), tm=512, tn=512, tk=None, bias=None, extras=(), epilogue=None, after=(),
            out_t=None):
    if mode == "nn":
        (M, K), (K2, N) = a.shape, b.shape
    elif mode == "nt":
        (M, K), (N, K2) = a.shape, b.shape
    else:
        (K, M), (K2, N) = a.shape, b.shape
    assert K == K2, (name, a.shape, b.shape)
    tm, tn, tk = min(tm, M), min(tn, N), min(tk or K, K)
    assert M % tm == 0 and N % tn == 0 and K % tk == 0, (name, M, N, K, tm, tn, tk)
    nk = K // tk
    if mode == "tn":
        a_spec = pl.BlockSpec((tk, tm), lambda i, j, k: (k, i))
    else:
        a_spec = pl.BlockSpec((tm, tk), lambda i, j, k: (i, k))
    if mode == "nt":
        b_spec = pl.BlockSpec((tn, tk), lambda i, j, k: (j, k))
    else:
        b_spec = pl.BlockSpec((tk, tn), lambda i, j, k: (k, j))
    dims = {"nn": (((1,), (0,)), ((), ())), "nt": (((1,), (1,)), ((), ())), "tn": (((0,), (0,)), ((), ()))}[mode]
    in_specs = [a_spec, b_spec]
    operands = [a, b]
    if bias is not None:
        in_specs.append(pl.BlockSpec((1, tn), lambda i, j, k: (0, j)))
        operands.append(bias)
    for e in extras:
        in_specs.append(pl.BlockSpec((tm, tn), lambda i, j, k: (i, j)))
        operands.append(e)
    n_ex = len(extras)
    for t in after:
        in_specs.append(pl.BlockSpec(memory_space=pl.ANY))
        operands.append(t)
    n_after = len(after)
    n_out = len(out_dtypes)
    out_t = tuple(out_t) if out_t is not None else (False,) * n_out
    has_bias = bias is not None

    def kern(*refs):
        a_ref, b_ref = refs[0], refs[1]
        pos = 2
        bias_ref = None
        if has_bias:
            bias_ref = refs[pos]
            pos += 1
        ex_refs = refs[pos:pos + n_ex]
        pos += n_ex + n_after
        out_refs = refs[pos:pos + n_out]
        acc_ref = refs[pos + n_out] if nk > 1 else None

        def finish(r):
            if has_bias:
                r = r + bias_ref[...]
            outs = epilogue(r, *[e[...] for e in ex_refs]) if epilogue is not None else (r,)
            for o_ref, o, tr_ in zip(out_refs, outs, out_t):
                o_ref[...] = (o.T if tr_ else o).astype(o_ref.dtype)

        part = lax.dot_general(a_ref[...].astype(BF16), b_ref[...].astype(BF16), dims, preferred_element_type=F32)
        if nk == 1:
            finish(part)
        else:
            k = pl.program_id(2)

            @pl.when(k == 0)
            def _():
                acc_ref[...] = part

            @pl.when(k > 0)
            def _():
                acc_ref[...] += part

            @pl.when(k == nk - 1)
            def _():
                finish(acc_ref[...])

    outs = pl.pallas_call(
        kern,
        name=name,
        grid=(M // tm, N // tn, nk),
        in_specs=in_specs,
        out_specs=[pl.BlockSpec((tn, tm), lambda i, j, k: (j, i)) if tr_ else pl.BlockSpec((tm, tn), lambda i, j, k: (i, j))
                   for tr_ in out_t],
        out_shape=[jax.ShapeDtypeStruct((N, M) if tr_ else (M, N), dt) for dt, tr_ in zip(out_dtypes, out_t)],
        scratch_shapes=[pltpu.VMEM((tm, tn), F32)] if nk > 1 else [],
        compiler_params=_cparams(("parallel", "parallel", "arbitrary")),
    )(*operands)
    return outs[0] if n_out == 1 else tuple(outs)


def _rowwise(fn, rows, vecs, outs, vec_outs, *, nrows, tr, name, after=()):
    n_rows, n_vecs, n_outs, n_after = len(rows), len(vecs), len(outs), len(after)
    in_specs = [pl.BlockSpec((tr, w), lambda i, cb=cb, ro=ro: (i + ro, cb)) for (_, w, cb, ro) in rows]
    in_specs += [pl.BlockSpec(v.shape, lambda i: (0, 0)) for v in vecs]
    in_specs += [pl.BlockSpec(memory_space=pl.ANY)] * n_after
    outs = [o if len(o) == 3 else (*o, False) for o in outs]
    out_specs = [pl.BlockSpec((w, tr), lambda i: (0, i)) if tr_ else pl.BlockSpec((tr, w), lambda i: (i, 0)) for (w, _, tr_) in outs]
    out_specs += [pl.BlockSpec(s, lambda i: (0, 0)) for s in vec_outs]
    out_shape = [jax.ShapeDtypeStruct((w, nrows) if tr_ else (nrows, w), dt) for (w, dt, tr_) in outs]
    out_tr = [tr_ for (_, _, tr_) in outs]
    out_shape += [jax.ShapeDtypeStruct(s, F32) for s in vec_outs]

    def kern(*refs):
        rvals = [r[...].astype(F32) for r in refs[:n_rows]]
        vvals = [r[...] for r in refs[n_rows:n_rows + n_vecs]]
        first_out = n_rows + n_vecs + n_after
        o_refs = refs[first_out:first_out + n_outs]
        v_refs = refs[first_out + n_outs:]
        ro, vo = fn(rvals, vvals)
        for r, val, tr_ in zip(o_refs, ro, out_tr):
            r[...] = (val.astype(F32).T if tr_ else val).astype(r.dtype)
        i = pl.program_id(0)
        for r, val in zip(v_refs, vo):
            @pl.when(i == 0)
            def _(r=r, val=val):
                r[...] = val.astype(F32)

            @pl.when(i > 0)
            def _(r=r, val=val):
                r[...] += val.astype(F32)

    res = pl.pallas_call(
        kern,
        name=name,
        grid=(nrows // tr,),
        in_specs=in_specs,
        out_specs=out_specs,
        out_shape=out_shape,
        compiler_params=_cparams(("arbitrary",)),
    )(*[r[0] for r in rows], *vecs, *after)
    return list(res)


def _ln(x):
    mu = jnp.mean(x, axis=-1, keepdims=True)
    xc = x - mu
    var = jnp.mean(xc * xc, axis=-1, keepdims=True)
    return xc * lax.rsqrt(var + LN_EPS)


def _sigmoid(x):
    return 1.0 / (1.0 + jnp.exp(-x))


def _gelu(x):
    return 0.5 * x * (1.0 + jnp.tanh(math.sqrt(2.0 / math.pi) * (x + 0.044715 * (x * x * x))))


def _silu(x):
    return x * _sigmoid(x)


def _f_ln_mod(x, sc, sh):
    return _ln(x) * (1.0 + sc) + sh


def _f_glu(z):
    return z[:, :SW] * _sigmoid(z[:, SW:])


def _f_mix(ga, gs, attn_d, ssm_d):
    return _sigmoid(ga) * attn_d + _sigmoid(gs) * ssm_d


def _f_post1(x, y, g1, lg, lb, sc2, sh2):
    r1 = ALPHA * x + g1 * y
    x1 = _ln(r1) * lg + lb
    h2 = _ln(x1) * (1.0 + sc2) + sh2
    return x1, h2


def _f_loss(x1, mlp, tgt, g2, lg, lb, b2z):
    r2 = ALPHA * x1 + g2 * (mlp + b2z)
    out = _ln(r2) * lg + lb
    err = out - tgt
    return 0.5 * jnp.sum(err * err) * (1.0 / D)


def _rope_tables():
    rows = T // GRID_W
    row = jnp.repeat(jnp.arange(rows), GRID_W)
    col = jnp.tile(jnp.arange(GRID_W), rows)
    n_freq = HD // 4
    freqs = ROPE_BASE ** (-jnp.arange(n_freq, dtype=F32) / n_freq)
    ang_r = row.astype(F32)[:, None] * freqs
    ang_c = col.astype(F32)[:, None] * freqs
    ang = jnp.concatenate([ang_r, ang_r, ang_c, ang_c], -1)
    cos, sin = jnp.cos(ang), jnp.sin(ang)
    lo = (jnp.arange(HD) % (HD // 2)) < (HD // 4)
    sin_a = jnp.where(lo[None, :], -sin, 0.0)
    sin_b = jnp.where(lo[None, :], 0.0, sin)
    return cos, sin_a, sin_b


def _rope(x, cos, sa, sb):
    return x * cos + pltpu.roll(x, 96, 1) * sa + pltpu.roll(x, 32, 1) * sb


def _rope_t(dy, cos, sa, sb):
    return dy * cos + pltpu.roll(dy * sa, 32, 1) + pltpu.roll(dy * sb, 96, 1)


BAND = 3 * WINDOW
KPAD = T + 2 * WINDOW


def _attn_fill_kv(k_ref, v_ref, cos_ref, sa_ref, sb_ref, kp, vp, kc, vc):
    zeros = jnp.zeros((WINDOW, KVW), BF16)
    kp[0:WINDOW, :] = zeros
    kp[WINDOW + T:KPAD, :] = zeros
    vp[0:WINDOW, :] = zeros
    vp[WINDOW + T:KPAD, :] = zeros
    for hh in range(NKV):
        cs = slice(hh * HD, (hh + 1) * HD)
        for r0 in range(0, T, 512):
            rs = slice(r0, r0 + 512)
            kr = _rope(k_ref[rs, cs], cos_ref[rs, :], sa_ref[rs, :], sb_ref[rs, :])
            kp[WINDOW + r0:WINDOW + r0 + 512, cs] = kr.astype(BF16)
    vp[WINDOW:WINDOW + T, :] = v_ref[0:T, :].astype(BF16)
    kc[...] = k_ref[T:TA, :].astype(BF16)
    vc[...] = v_ref[T:TA, :].astype(BF16)


GROWS = GROUP * WINDOW


def _attn_scores(n, kvh, q_ref, cos_ref, sa_ref, sb_ref, sink_ref, kp, kc):
    r0 = pl.multiple_of(n * WINDOW, WINDOW)
    cos = cos_ref[pl.ds(r0, WINDOW), :]
    sa = sa_ref[pl.ds(r0, WINDOW), :]
    sb = sb_ref[pl.ds(r0, WINDOW), :]
    heads = range(kvh * GROUP, (kvh + 1) * GROUP)
    q_g = jnp.concatenate([_rope(q_ref[:, h * HD:(h + 1) * HD], cos, sa, sb).astype(BF16) for h in heads], axis=0)
    kb = kp[pl.ds(r0, BAND), kvh * HD:(kvh + 1) * HD]
    kcb = kc[:, kvh * HD:(kvh + 1) * HD]
    nt = (((1,), (1,)), ((), ()))
    s_loc = lax.dot_general(q_g, kb, nt, preferred_element_type=F32) * ATT_SCALE
    s_ctx = lax.dot_general(q_g, kcb, nt, preferred_element_type=F32) * ATT_SCALE
    row = lax.broadcasted_iota(jnp.int32, (GROWS, BAND), 0) & (WINDOW - 1)
    col = lax.broadcasted_iota(jnp.int32, (GROWS, BAND), 1)
    rel = col - WINDOW - row
    kpos = r0 - WINDOW + col
    valid = (jnp.abs(rel) <= WINDOW) & (kpos >= 0) & (kpos < T)
    s_loc = jnp.where(valid, s_loc, NEG_INF)
    sk = jnp.concatenate([jnp.broadcast_to(sink_ref[0:1, h:h + 1], (WINDOW, 1)) for h in heads], axis=0)
    m = jnp.maximum(jnp.maximum(jnp.max(s_loc, -1, keepdims=True), jnp.max(s_ctx, -1, keepdims=True)), sk)
    e_loc = jnp.exp(s_loc - m)
    e_ctx = jnp.exp(s_ctx - m)
    e_sink = jnp.exp(sk - m)
    inv = 1.0 / (jnp.sum(e_loc, -1, keepdims=True) + jnp.sum(e_ctx, -1, keepdims=True) + e_sink)
    return q_g, r0, e_loc * inv, e_ctx * inv, e_sink * inv


def _attn_fwd(proj, sink, tabs):
    cos, sa, sb = tabs

    def kern(q_ref, k_ref, v_ref, cos_ref, sa_ref, sb_ref, sink_ref, o_ref, kp, vp, kc, vc):
        n = pl.program_id(0)

        @pl.when(n == 0)
        def _():
            _attn_fill_kv(k_ref, v_ref, cos_ref, sa_ref, sb_ref, kp, vp, kc, vc)

        for kvh in range(NKV):
            _, r0, p_loc, p_ctx, _ = _attn_scores(n, kvh, q_ref, cos_ref, sa_ref, sb_ref, sink_ref, kp, kc)
            vb = vp[pl.ds(r0, BAND), kvh * HD:(kvh + 1) * HD]
            vcb = vc[:, kvh * HD:(kvh + 1) * HD]
            o = jnp.dot(p_loc.astype(BF16), vb, preferred_element_type=F32)
            o = o + jnp.dot(p_ctx.astype(BF16), vcb, preferred_element_type=F32)
            for g in range(GROUP):
                h = kvh * GROUP + g
                o_ref[:, h * HD:(h + 1) * HD] = o[g * WINDOW:(g + 1) * WINDOW, :].astype(o_ref.dtype)

    full = lambda shape: pl.BlockSpec(shape, lambda n: (0, 0))
    return pl.pallas_call(
        kern,
        name="attn_fwd",
        grid=(T // WINDOW,),
        in_specs=[
            pl.BlockSpec((WINDOW, QW), lambda n: (n, 0)),
            pl.BlockSpec((TA, KVW), lambda n: (0, QW // KVW)),
            pl.BlockSpec((TA, KVW), lambda n: (0, QW // KVW + 1)),
            full((T, HD)), full((T, HD)), full((T, HD)), full((1, NH)),
        ],
        out_specs=pl.BlockSpec((WINDOW, QW), lambda n: (n, 0)),
        out_shape=jax.ShapeDtypeStruct((T, QW), BF16),
        scratch_shapes=[pltpu.VMEM((KPAD, KVW), BF16), pltpu.VMEM((KPAD, KVW), BF16),
                        pltpu.VMEM((C, KVW), BF16), pltpu.VMEM((C, KVW), BF16)],
        compiler_params=_cparams(("arbitrary",)),
    )(proj, proj, proj, cos, sa, sb, sink)


def _attn_bwd(proj, d_attn, sink, tabs):
    cos, sa, sb = tabs
    n_blocks = T // WINDOW

    def kern(q_ref, k_ref, v_ref, do_ref, cos_ref, sa_ref, sb_ref, sink_ref,
             dq_ref, dk_ref, dv_ref, dsink_ref, kp, vp, kc, vc, dkp, dvp, dkc, dvc):
        n = pl.program_id(0)

        @pl.when(n == 0)
        def _():
            _attn_fill_kv(k_ref, v_ref, cos_ref, sa_ref, sb_ref, kp, vp, kc, vc)
            dkp[...] = jnp.zeros_like(dkp)
            dvp[...] = jnp.zeros_like(dvp)
            dkc[...] = jnp.zeros_like(dkc)
            dvc[...] = jnp.zeros_like(dvc)
            dsink_ref[...] = jnp.zeros_like(dsink_ref)

        nt = (((1,), (1,)), ((), ()))
        tn = (((0,), (0,)), ((), ()))
        for kvh in range(NKV):
            cs = slice(kvh * HD, (kvh + 1) * HD)
            heads = range(kvh * GROUP, (kvh + 1) * GROUP)
            q_g, r0, p_loc, p_ctx, p_sink = _attn_scores(n, kvh, q_ref, cos_ref, sa_ref, sb_ref, sink_ref, kp, kc)
            kb = kp[pl.ds(r0, BAND), cs]
            vb = vp[pl.ds(r0, BAND), cs]
            kcb = kc[:, cs]
            vcb = vc[:, cs]
            do_g = jnp.concatenate([do_ref[:, h * HD:(h + 1) * HD] for h in heads], axis=0)
            dp_loc = lax.dot_general(do_g, vb, nt, preferred_element_type=F32)
            dp_ctx = lax.dot_general(do_g, vcb, nt, preferred_element_type=F32)
            delta = jnp.sum(p_loc * dp_loc, -1, keepdims=True) + jnp.sum(p_ctx * dp_ctx, -1, keepdims=True)
            ds_loc = (p_loc * (dp_loc - delta) * ATT_SCALE).astype(BF16)
            ds_ctx = (p_ctx * (dp_ctx - delta) * ATT_SCALE).astype(BF16)
            dq = jnp.dot(ds_loc, kb, preferred_element_type=F32) + jnp.dot(ds_ctx, kcb, preferred_element_type=F32)
            cos = cos_ref[pl.ds(r0, WINDOW), :]
            sa_ = sa_ref[pl.ds(r0, WINDOW), :]
            sb_ = sb_ref[pl.ds(r0, WINDOW), :]
            dkp[pl.ds(r0, BAND), cs] += lax.dot_general(ds_loc, q_g, tn, preferred_element_type=F32)
            dkc[:, cs] += lax.dot_general(ds_ctx, q_g, tn, preferred_element_type=F32)
            dvp[pl.ds(r0, BAND), cs] += lax.dot_general(p_loc.astype(BF16), do_g, tn, preferred_element_type=F32)
            dvc[:, cs] += lax.dot_general(p_ctx.astype(BF16), do_g, tn, preferred_element_type=F32)
            dsk_rows = p_sink * delta
            for g, h in enumerate(heads):
                rs = slice(g * WINDOW, (g + 1) * WINDOW)
                dq_ref[:, h * HD:(h + 1) * HD] = _rope_t(dq[rs, :], cos, sa_, sb_).astype(dq_ref.dtype)
                dsk = -jnp.sum(dsk_rows[rs, :], axis=0, keepdims=True)
                dsink_ref[h:h + 1, :] += jnp.broadcast_to(dsk, (1, HD))

        @pl.when(n == n_blocks - 1)
        def _():
            for hh in range(NKV):
                cs = slice(hh * HD, (hh + 1) * HD)
                for r0 in range(0, T, 512):
                    rs = slice(r0, r0 + 512)
                    g = dkp[WINDOW + r0:WINDOW + r0 + 512, cs]
                    dk_ref[rs, cs] = _rope_t(g, cos_ref[rs, :], sa_ref[rs, :], sb_ref[rs, :]).astype(dk_ref.dtype)
            dk_ref[T:TA, :] = dkc[...].astype(dk_ref.dtype)
            dv_ref[0:T, :] = dvp[WINDOW:WINDOW + T, :].astype(dv_ref.dtype)
            dv_ref[T:TA, :] = dvc[...].astype(dv_ref.dtype)

    full = lambda shape: pl.BlockSpec(shape, lambda n: (0, 0))
    return pl.pallas_call(
        kern,
        name="attn_bwd",
        grid=(n_blocks,),
        in_specs=[
            pl.BlockSpec((WINDOW, QW), lambda n: (n, 0)),
            pl.BlockSpec((TA, KVW), lambda n: (0, QW // KVW)),
            pl.BlockSpec((TA, KVW), lambda n: (0, QW // KVW + 1)),
            pl.BlockSpec((WINDOW, QW), lambda n: (n, 0)),
            full((T, HD)), full((T, HD)), full((T, HD)), full((1, NH)),
        ],
        out_specs=[pl.BlockSpec((WINDOW, QW), lambda n: (n, 0)), full((TA, KVW)), full((TA, KVW)), full((NH, HD))],
        out_shape=[jax.ShapeDtypeStruct((T, QW), BF16), jax.ShapeDtypeStruct((TA, KVW), BF16),
                   jax.ShapeDtypeStruct((TA, KVW), BF16), jax.ShapeDtypeStruct((NH, HD), F32)],
        scratch_shapes=[pltpu.VMEM((KPAD, KVW), BF16), pltpu.VMEM((KPAD, KVW), BF16),
                        pltpu.VMEM((C, KVW), BF16), pltpu.VMEM((C, KVW), BF16),
                        pltpu.VMEM((KPAD, KVW), F32), pltpu.VMEM((KPAD, KVW), F32),
                        pltpu.VMEM((C, KVW), F32), pltpu.VMEM((C, KVW), F32)],
        compiler_params=_cparams(("arbitrary",)),
    )(proj, proj, proj, d_attn, cos, sa, sb, sink)


def _s5_prep(a_re, a_im, log_dt, b_re, b_im, c_re, c_im):
    lam = lax.complex(a_re, a_im)
    dt = jnp.exp(log_dt)[..., None]
    lam_bar = jnp.exp(lam * dt)
    b_bar = ((lam_bar - 1.0) / lam)[..., None] * lax.complex(b_re, b_im)
    def lam_rows(v):
        return v.reshape(2, NBLK, 1, BW)

    lam_l = jnp.concatenate([lam_rows(jnp.real(lam_bar)), lam_rows(jnp.imag(lam_bar))], -1)
    lam_l = jnp.broadcast_to(lam_l, (2, NBLK, 8, 2 * BW))
    diag = (jnp.arange(UW)[:, None] // SG) == (jnp.arange(BW)[None, :] // SP)

    def blocks(v):
        return jnp.where(diag, jnp.tile(v.reshape(2, NBLK, UW, SP), (1, 1, 1, GBLK)), 0.0)

    b_t = jnp.swapaxes(b_bar, -1, -2)
    bmat = jnp.concatenate([blocks(jnp.real(b_t)), blocks(jnp.imag(b_t))], -1)
    cmat = jnp.concatenate([blocks(c_re), -blocks(c_im)], -1)
    return lam_l, bmat, cmat


def _cmul(ar, ai, br, bi):
    return ar * br - ai * bi, ar * bi + ai * br


def _shift_rows(x, rev, fill):
    r = lax.broadcasted_iota(jnp.int32, x.shape, 0)
    down = jnp.where(r == 0, fill, pltpu.roll(x, 1, 0))
    up = jnp.where(r == NSEG - 1, fill, pltpu.roll(x, NSEG - 1, 0))
    return jnp.where(rev == 0, down, up)


def _edge_row(x, rev):
    last = jnp.broadcast_to(x[NSEG - 1:NSEG, :], x.shape)
    first = jnp.broadcast_to(x[0:1, :], x.shape)
    return jnp.where(rev == 0, last, first)


def _seg_scan(get, put, base, seglen, lr, li, rev, cin, acc_fn=None, acc0=()):
    zero = jnp.zeros((NSEG, BW), F32)

    def rows(k):
        j = jnp.where(rev == 0, k, seglen - 1 - k)
        return pl.ds(pl.multiple_of(base + j * NSEG, NSEG), NSEG)

    def local(k, carry):
        sr, si = carry
        xr, xi = get(rows(k))
        tr, ti = _cmul(lr, li, sr, si)
        sr, si = tr + xr, ti + xi
        put(rows(k), sr, si)
        return sr, si

    er, ei = lax.fori_loop(0, seglen, local, (zero, zero))
    lpr, lpi = lr, li
    assert seglen & (seglen - 1) == 0, seglen
    for _ in range(seglen.bit_length() - 1):
        lpr, lpi = _cmul(lpr, lpi, lpr, lpi)
    cr, ci = _shift_rows(zero, rev, cin[0]), _shift_rows(zero, rev, cin[1])
    for _ in range(NSEG - 1):
        tr, ti = _cmul(lpr, lpi, cr, ci)
        cr, ci = _shift_rows(er + tr, rev, cin[0]), _shift_rows(ei + ti, rev, cin[1])

    def fix(k, carry):
        tr, ti = _cmul(lr, li, carry[0], carry[1])
        xr, xi = get(rows(k))
        fr, fi = xr + tr, xi + ti
        put(rows(k), fr, fi)
        if acc_fn is None:
            return tr, ti
        j = jnp.where(rev == 0, k, seglen - 1 - k)
        return (tr, ti) + tuple(acc_fn(j, fr, fi, carry[2:]))

    out = lax.fori_loop(0, seglen, fix, (cr, ci) + tuple(acc0))
    tr, ti = out[0], out[1]
    leaving = (_edge_row(er + tr, rev), _edge_row(ei + ti, rev))
    return leaving if acc_fn is None else (leaving, out[2:])


RCH = 256
CSEG = C // NSEG
TSEG = T // NSEG
UCOL0 = (QW + 2 * KVW) // UW


REGIONS = ((0, TSEG), (T, CSEG))


def _state_access(ref, lead=()):
    def get(rows):
        return ref[(*lead, rows, slice(0, BW))], ref[(*lead, rows, slice(BW, 2 * BW))]

    def put(rows, re, im):
        ref[(*lead, rows, slice(0, BW))] = re
        ref[(*lead, rows, slice(BW, 2 * BW))] = im

    return get, put


def _interleave_rows(src_ref, dst_ref, regions=REGIONS):
    for base, seglen in regions:
        def body(j, carry, base=base, seglen=seglen):
            dst_ref[pl.ds(pl.multiple_of(base + j * NSEG, NSEG), NSEG), :] = src_ref[pl.ds(base + j, NSEG, stride=seglen), :]
            return carry

        lax.fori_loop(0, seglen, body, 0, unroll=8)


def _deinterleave_rows(src_ref, dst_ref, regions=REGIONS):
    for base, seglen in regions:
        def body(j, carry, base=base, seglen=seglen):
            dst_ref[pl.ds(base + j, NSEG, stride=seglen), :] = src_ref[pl.ds(pl.multiple_of(base + j * NSEG, NSEG), NSEG), :]
            return carry

        lax.fori_loop(0, seglen, body, 0, unroll=8)


def _s5_fwd(proj, dskip, lam, bmat, cmat):
    def kern(u_ref, dk_ref, lam_ref, b_ref, c_ref, s_ref, ssm_ref, ge_ref, up_ref, yp_ref):
        d = pl.program_id(1)

        @pl.when(d == 0)
        def _():
            _interleave_rows(u_ref, up_ref)

        bm = b_ref[0, 0].astype(BF16)
        for r0 in range(0, TA, RCH):
            s_ref[0, 0, r0:r0 + RCH, :] = jnp.dot(up_ref[r0:r0 + RCH, :].astype(BF16), bm, preferred_element_type=F32)
        lr = lam_ref[0, 0, :, 0:BW]
        li = lam_ref[0, 0, :, BW:2 * BW]
        zero = jnp.zeros((NSEG, BW), F32)
        get, put = _state_access(s_ref, (0, 0))
        mid = _seg_scan(get, put, T, CSEG, lr, li, d, (zero, zero))
        _seg_scan(get, put, 0, TSEG, lr, li, d, mid)
        cm = c_ref[0, 0].astype(BF16)
        for r0 in range(0, T, RCH):
            y = lax.dot_general(s_ref[0, 0, r0:r0 + RCH, :].astype(BF16), cm, (((1,), (1,)), ((), ())), preferred_element_type=F32)

            @pl.when(d == 0)
            def _(y=y, r0=r0):
                yp_ref[r0:r0 + RCH, :] = y + dk_ref[...] * up_ref[r0:r0 + RCH, :]

            @pl.when(d == 1)
            def _(y=y, r0=r0):
                yp_ref[r0:r0 + RCH, :] += y

        @pl.when(d == 1)
        def _():
            _deinterleave_rows(yp_ref, ssm_ref, REGIONS[:1])
            for r0 in range(0, T, RCH):
                ge_ref[r0:r0 + RCH, :] = _gelu(ssm_ref[r0:r0 + RCH, :]).astype(ge_ref.dtype)

    blk4 = lambda shape: pl.BlockSpec((1, 1) + shape, lambda b, d: (d, b, 0, 0))
    return pl.pallas_call(
        kern,
        name="s5_fwd",
        grid=(NBLK, 2),
        in_specs=[pl.BlockSpec((TA, UW), lambda b, d: (0, UCOL0 + b)), pl.BlockSpec((1, UW), lambda b, d: (0, b)),
                  blk4((8, 2 * BW)), blk4((UW, 2 * BW)), blk4((UW, 2 * BW))],
        out_specs=[blk4((TA, 2 * BW)), pl.BlockSpec((T, UW), lambda b, d: (0, b)), pl.BlockSpec((T, UW), lambda b, d: (0, b))],
        out_shape=[jax.ShapeDtypeStruct((2, NBLK, TA, 2 * BW), F32), jax.ShapeDtypeStruct((T, SW), F32),
                   jax.ShapeDtypeStruct((T, SW), BF16)],
        scratch_shapes=[pltpu.VMEM((TA, UW), F32), pltpu.VMEM((T, UW), F32)],
        compiler_params=_cparams(("parallel", "arbitrary")),
    )(proj, dskip, lam, bmat, cmat)


def _s5_bwd(d_ge, ssm, proj, dskip, states, lam, bmat, cmat):
    nt = (((1,), (1,)), ((), ()))
    tn = (((0,), (0,)), ((), ()))

    def kern(dge_ref, ssm_ref, u_ref, dk_ref, s_ref, lam_ref, b_ref, c_ref,
             du_ref, ddk_ref, dlam_ref, db_ref, dc_ref, g_ref, dua_ref, dssm_ref, up_ref, nat_ref):
        d = pl.program_id(1)

        @pl.when(d == 0)
        def _():
            ddk = jnp.zeros((1, UW), F32)
            for r0 in range(0, T, RCH):
                rs = slice(r0, r0 + RCH)
                _, pull = jax.vjp(_gelu, ssm_ref[rs, :])
                dssm = pull(dge_ref[rs, :])[0]
                nat_ref[rs, :] = dssm
                ddk = ddk + jnp.sum(dssm * u_ref[rs, :], axis=0, keepdims=True)
            ddk_ref[...] = ddk
            _interleave_rows(nat_ref, dssm_ref, REGIONS[:1])
            _interleave_rows(u_ref, up_ref)
            for r0 in range(0, T, RCH):
                dua_ref[r0:r0 + RCH, :] = dssm_ref[r0:r0 + RCH, :] * dk_ref[...]
            dua_ref[T:TA, :] = jnp.zeros((C, UW), F32)

        cm = c_ref[0, 0].astype(BF16)
        for r0 in range(0, T, RCH):
            g_ref[r0:r0 + RCH, :] = jnp.dot(dssm_ref[r0:r0 + RCH, :].astype(BF16), cm, preferred_element_type=F32)
        g_ref[T:TA, :] = jnp.zeros((C, 2 * BW), F32)
        lr = lam_ref[0, 0, :, 0:BW]
        li = lam_ref[0, 0, :, BW:2 * BW]
        zero = jnp.zeros((NSEG, BW), F32)
        get_g, put_g = _state_access(g_ref)

        get_s, _ = _state_access(s_ref, (0, 0))

        def dlam_fold(base, seglen, s_in):
            def rows(j):
                return pl.ds(pl.multiple_of(base + j * NSEG, NSEG), NSEG)

            jb = jnp.where(d == 0, 0, seglen - 1)
            jn = jnp.where(d == 0, seglen - 1, 0)
            sp = get_s(rows(jn))
            edge = (_shift_rows(sp[0], d, s_in[0]), _shift_rows(sp[1], d, s_in[1]))

            def fold(j, gr, gi, acc):
                jp = jnp.clip(jnp.where(d == 0, j - 1, j + 1), 0, seglen - 1)
                sr, si = get_s(rows(jp))
                sr = jnp.where(j == jb, edge[0], sr)
                si = jnp.where(j == jb, edge[1], si)
                return acc[0] + (gr * sr + gi * si), acc[1] + (gi * sr - gr * si)

            return fold

        r_mid = jnp.where(d == 0, TA - 1, T)
        s_mid = tuple(jnp.broadcast_to(t, (NSEG, BW)) for t in get_s(pl.ds(r_mid, 1)))
        mid, acc = _seg_scan(get_g, put_g, 0, TSEG, lr, -li, 1 - d, (zero, zero), dlam_fold(0, TSEG, s_mid), (zero, zero))
        _, acc = _seg_scan(get_g, put_g, T, CSEG, lr, -li, 1 - d, mid, dlam_fold(T, CSEG, (zero, zero)), acc)
        dlam_ref[0, 0, :, 0:BW] = acc[0]
        dlam_ref[0, 0, :, BW:2 * BW] = acc[1]

        bm = b_ref[0, 0].astype(BF16)
        db = jnp.zeros((UW, 2 * BW), F32)
        dc = jnp.zeros((UW, 2 * BW), F32)
        for r0 in range(0, TA, RCH):
            rs = slice(r0, r0 + RCH)
            g = g_ref[rs, :].astype(BF16)
            dua_ref[rs, :] += lax.dot_general(g, bm, nt, preferred_element_type=F32)
            db = db + lax.dot_general(up_ref[rs, :].astype(BF16), g, tn, preferred_element_type=F32)
            if r0 < T:
                dc = dc + lax.dot_general(dssm_ref[rs, :].astype(BF16), s_ref[0, 0, rs, :].astype(BF16), tn,
                                          preferred_element_type=F32)
        db_ref[0, 0] = db
        dc_ref[0, 0] = dc

        @pl.when(d == 1)
        def _():
            _deinterleave_rows(dua_ref, nat_ref)
            du_ref[...] = nat_ref[...].astype(du_ref.dtype)

    blk4 = lambda shape: pl.BlockSpec((1, 1) + shape, lambda b, d: (d, b, 0, 0))
    lat = pl.BlockSpec((T, UW), lambda b, d: (0, b))
    vec = pl.BlockSpec((1, UW), lambda b, d: (0, b))
    return pl.pallas_call(
        kern,
        name="s5_bwd",
        grid=(NBLK, 2),
        in_specs=[lat, lat, pl.BlockSpec((TA, UW), lambda b, d: (0, UCOL0 + b)), vec,
                  blk4((TA, 2 * BW)), blk4((8, 2 * BW)), blk4((UW, 2 * BW)), blk4((UW, 2 * BW))],
        out_specs=[pl.BlockSpec((TA, UW), lambda b, d: (0, b)), vec, blk4((8, 2 * BW)), blk4((UW, 2 * BW)), blk4((UW, 2 * BW))],
        out_shape=[jax.ShapeDtypeStruct((TA, SW), BF16), jax.ShapeDtypeStruct((1, SW), F32),
                   jax.ShapeDtypeStruct((2, NBLK, 8, 2 * BW), F32),
                   jax.ShapeDtypeStruct((2, NBLK, UW, 2 * BW), F32), jax.ShapeDtypeStruct((2, NBLK, UW, 2 * BW), F32)],
        scratch_shapes=[pltpu.VMEM((TA, 2 * BW), F32), pltpu.VMEM((TA, UW), F32), pltpu.VMEM((T, UW), F32),
                        pltpu.VMEM((TA, UW), F32), pltpu.VMEM((TA, UW), F32)],
        compiler_params=_cparams(("parallel", "arbitrary")),
    )(d_ge, ssm, proj, dskip, states, lam, bmat, cmat)


TR = 256
TN_WIDE = 1024


def _vjp_rows(f, primals, cots, n_row):
    _, pull = jax.vjp(f, *primals)
    g = pull(cots)
    return list(g[:n_row]), list(g[n_row:])


class _GradDict(dict):
    def __init__(self, on_set=None):
        super().__init__()
        self._on_set = on_set
        self.tokens = {}

    def __setitem__(self, key, value):
        super().__setitem__(key, value)
        if self._on_set is not None:
            self._on_set(self)

    def order(self, key):
        return self.tokens.get(key, self.get(key))

    def finish(self, key, after):
        if self.on_finish is None:
            return ()
        return (self.on_finish(key, after),)

    on_finish = None


def _local_step(x, ctx, tgt, mod_lat, mod_ctx, wb, sp, on_grad=None, on_loss=None, on_finish=None, on_early=None):
    sh1, sc1, g1, sh2, sc2, g2 = [mod_lat[:, i * D:(i + 1) * D] for i in range(6)]
    csh1, csc1 = mod_ctx[:, 0:D], mod_ctx[:, D:2 * D]
    tabs = _rope_tables()
    sink = sp["attn_sink"].reshape(1, NH)
    dskip = sp["ssm_d"].reshape(1, SW)
    lg_mix, lb_mix = sp["ln_mix_g"].reshape(1, D), sp["ln_mix_b"].reshape(1, D)
    lg_mlp, lb_mlp = sp["ln_mlp_g"].reshape(1, D), sp["ln_mlp_b"].reshape(1, D)
    b1, b2 = sp["b_mlp1"].reshape(1, DFF), sp["b_mlp2"].reshape(1, D)
    s5_names = ("ssm_a_re", "ssm_a_im", "ssm_log_dt", "ssm_b_re", "ssm_b_im", "ssm_c_re", "ssm_c_im")
    (lam, bmat, cmat), s5_pull = jax.vjp(_s5_prep, *[sp[n] for n in s5_names])

    def ln_mod2(rv, vv):
        h = _f_ln_mod(rv[0], vv[0], vv[1])
        return [h, h], []

    h_lat, h_lat_t = _rowwise(ln_mod2, [(x, D, 0, 0)], [sc1, sh1], [(D, BF16), (D, BF16, True)], [], nrows=T, tr=TR, name="ln1_lat")
    h_ctx, h_ctx_t = _rowwise(ln_mod2, [(ctx, D, 0, 0)], [csc1, csh1], [(D, BF16), (D, BF16, True)], [], nrows=C, tr=TR,
                              name="ln1_ctx")
    h1 = jnp.concatenate([h_lat, h_ctx], 0)
    h1_t = jnp.concatenate([h_lat_t, h_ctx_t], 1)
    proj = _matmul(h1, wb["w_in"], mode="nn", name="proj", tm=768, tn=TN_WIDE)
    attn = _attn_fwd(proj, sink, tabs)
    states, ssm, ge = _s5_fwd(proj, dskip, lam, bmat, cmat)
    z = _matmul(ge, wb["w_glu"], mode="nn", name="glu_mm", tm=1024, tn=1024)

    def glu_act(rv, vv):
        return [_f_glu(rv[0])], []

    glu, = _rowwise(glu_act, [(z, 2 * SW, 0, 0)], [], [(SW, BF16)], [], nrows=T, tr=TR, name="glu_act")
    attn_d = _matmul(attn, wb["w_attn_up"], mode="nn", name="attn_up", tm=1024, tn=512, out_dtypes=(BF16,))
    ssm_d = _matmul(glu, wb["w_ssm_up"], mode="nn", name="ssm_up", tm=1024, tn=512, out_dtypes=(BF16,))
    ga_cb, gs_cb = (QW + 2 * KVW + SW) // D, (QW + 2 * KVW + SW) // D + 1

    def mix(rv, vv):
        m_ = _f_mix(*rv)
        return [m_, m_], []

    mixv, mix_t = _rowwise(mix, [(proj, D, ga_cb, 0), (proj, D, gs_cb, 0), (attn_d, D, 0, 0), (ssm_d, D, 0, 0)], [],
                           [(D, BF16), (D, BF16, True)], [], nrows=T, tr=TR, name="mix")
    y = _matmul(mixv, wb["w_out"], mode="nn", name="out_proj", tm=1024, tn=TN_WIDE)

    def post1(rv, vv):
        x1, h2 = _f_post1(rv[0], rv[1], *vv)
        return [x1, h2, h2], []

    x1, h2, h2_t = _rowwise(post1, [(x, D, 0, 0), (y, D, 0, 0)], [g1, lg_mix, lb_mix, sc2, sh2],
                            [(D, F32), (D, BF16), (D, BF16, True)], [], nrows=T, tr=TR, name="post1")

    def relu_sq(acc):
        r = jnp.maximum(acc, 0.0)
        return r, r * r, r * r

    r_act, act, act_t = _matmul(h2, wb["w_mlp1"], mode="nn", name="mlp1", tm=1024, tn=TN_WIDE, bias=b1,
                                out_dtypes=(BF16, BF16, BF16), out_t=(False, False, True), epilogue=relu_sq)
    mlp = _matmul(act, wb["w_mlp2"], mode="nn", name="mlp2", tm=512, tn=512)

    def loss_fb(rv, vv):
        x1_t, mlp_t, tgt_t = rv
        g2_v, lg_v, lb_v, b2_v = vv
        f = lambda a, m, g, p, q, b: _f_loss(a, m, tgt_t, g, p, q, b)
        val, grads = jax.value_and_grad(f, argnums=(0, 1, 2, 3, 4, 5))(x1_t, mlp_t, g2_v, lg_v, lb_v, b2_v)
        dx1, dmlp, dg2, dlg, dlb, db2 = grads
        return [dx1, dmlp], [jnp.reshape(val, (1, 1)), dg2, dlg, dlb, db2]

    dx1_a, d_mlp, loss_p, d_g2, d_lg_mlp, d_lb_mlp, d_b2 = _rowwise(
        loss_fb, [(x1, D, 0, 0), (mlp, D, 0, 0), (tgt, D, 0, 0)], [g2, lg_mlp, lb_mlp, b2],
        [(D, F32), (D, BF16)], [(1, 1), (1, D), (1, D), (1, D), (1, D)], nrows=T, tr=TR, name="loss_fb")

    gw = _GradDict(on_grad)
    gw.on_finish = on_finish
    loss_done = () if on_loss is None else (on_loss(loss_p),)
    gw["w_mlp2"] = _matmul(act_t, d_mlp, mode="nn", name="dw_mlp2", out_dtypes=(BF16,), tm=1024, tn=TN_WIDE, after=loss_done)
    da, = (_matmul(d_mlp, wb["w_mlp2"], mode="nt", name="d_act", out_dtypes=(BF16,), tm=1024, tn=TN_WIDE,
                   extras=(r_act,), epilogue=lambda acc, r: (acc * (2.0 * r.astype(F32)),), after=(gw.order("w_mlp2"),)),)
    pin = gw.finish("w_mlp2", da)
    ones = jnp.ones((8, T), BF16)
    d_b1 = _matmul(ones, da, mode="nn", name="db_mlp1", tm=8, tn=2048)[0:1]
    gw["w_mlp1"] = _matmul(h2_t, da, mode="nn", name="dw_mlp1", out_dtypes=(BF16,), tm=1024, tn=TN_WIDE, after=pin)
    dh2 = _matmul(da, wb["w_mlp1"], mode="nt", name="d_h2", tm=512, tn=512, out_dtypes=(BF16,), after=(gw.order("w_mlp1"),))

    def post1_b(rv, vv):
        x_t, y_t, dx1_t, dh2_t = rv
        gr, gv = _vjp_rows(_f_post1, (x_t, y_t, *vv), (dx1_t, dh2_t), 2)
        return [gr[0], gr[1]], gv

    dx_a, dy, d_g1, d_lg_mix, d_lb_mix, d_sc2, d_sh2 = _rowwise(
        post1_b, [(x, D, 0, 0), (y, D, 0, 0), (dx1_a, D, 0, 0), (dh2, D, 0, 0)], [g1, lg_mix, lb_mix, sc2, sh2],
        [(D, F32), (D, BF16)], [(1, D)] * 5, nrows=T, tr=TR, name="post1_bwd")
    gw["w_out"] = _matmul(mix_t, dy, mode="nn", name="dw_out", out_dtypes=(BF16,), tm=1024, tn=TN_WIDE)
    dmix = _matmul(dy, wb["w_out"], mode="nt", name="d_mix", tm=1024, tn=TN_WIDE, out_dtypes=(BF16,), after=(gw.order("w_out"),))

    def mix_b(rv, vv):
        gr, _ = _vjp_rows(_f_mix, tuple(rv[:4]), rv[4], 4)
        return gr, []

    d_ga, d_gs, d_attn_d, d_ssm_d = _rowwise(
        mix_b, [(proj, D, ga_cb, 0), (proj, D, gs_cb, 0), (attn_d, D, 0, 0), (ssm_d, D, 0, 0), (dmix, D, 0, 0)], [],
        [(D, BF16)] * 4, [], nrows=T, tr=TR, name="mix_bwd")
    pin = gw.finish("w_mlp1", d_ga)
    gw["w_attn_up"] = _matmul(attn, d_attn_d, mode="tn", name="dw_attn_up", out_dtypes=(BF16,), tm=512, tn=1024, tk=1024, after=pin)
    d_attn = _matmul(d_attn_d, wb["w_attn_up"], mode="nt", name="d_attn", out_dtypes=(BF16,), tm=1024, tn=512)
    gw["w_ssm_up"] = _matmul(glu, d_ssm_d, mode="tn", name="dw_ssm_up", out_dtypes=(BF16,), tm=512, tn=1024, tk=1024)
    d_glu = _matmul(d_ssm_d, wb["w_ssm_up"], mode="nt", name="d_glu", tm=1024, tn=512, after=(gw.order("w_attn_up"), gw.order("w_ssm_up")))

    def glu_b(rv, vv):
        gr, _ = _vjp_rows(_f_glu, (rv[0],), rv[1], 1)
        return gr, []

    dz, = _rowwise(glu_b, [(z, 2 * SW, 0, 0), (d_glu, SW, 0, 0)], [], [(2 * SW, BF16)], [], nrows=T, tr=TR, name="glu_bwd")
    gw["w_glu"] = _matmul(ge, dz, mode="tn", name="dw_glu", out_dtypes=(BF16,), tm=512, tn=1024, tk=1024)
    d_ge = _matmul(dz, wb["w_glu"], mode="nt", name="d_ge", tm=1024, tn=512, after=(gw.order("w_glu"),))

    du_all, d_dskip, dlam, dbmat, dcmat = _s5_bwd(d_ge, ssm, proj, dskip, states, lam, bmat, cmat)
    s5_grads = s5_pull((dlam, dbmat, dcmat))
    early = dict(zip(s5_names, s5_grads), ssm_d=d_dskip)
    if on_early is not None:
        on_early(early)
    pin = gw.finish("w_glu", du_all)

    dq, dk, dv, dsink = _attn_bwd(proj, d_attn, sink, tabs)
    zc = lambda w: jnp.zeros((C, w), BF16)
    dproj = jnp.concatenate([
        jnp.concatenate([dq, zc(QW)], 0), dk, dv, du_all,
        jnp.concatenate([d_ga, zc(D)], 0), jnp.concatenate([d_gs, zc(D)], 0)], 1)
    gw["w_in"] = _matmul(h1_t, dproj, mode="nn", name="dw_in", out_dtypes=(BF16,), tm=1024, tn=TN_WIDE, after=pin)
    pin = gw.finish("w_in", gw["w_in"])
    dh1 = _matmul(dproj, wb["w_in"], mode="nt", name="d_h1", tm=768, tn=512, out_dtypes=(BF16,), after=pin)

    def ln1_b(rv, vv):
        x_t, dh_t, dxa_t = rv
        gr, gv = _vjp_rows(_f_ln_mod, (x_t, vv[0], vv[1]), dh_t, 1)
        return [gr[0] + dxa_t], gv

    grad_x, d_sc1, d_sh1 = _rowwise(ln1_b, [(x, D, 0, 0), (dh1, D, 0, 0), (dx_a, D, 0, 0)], [sc1, sh1],
                                    [(D, F32)], [(1, D), (1, D)], nrows=T, tr=TR, name="ln1_lat_bwd")

    def ln1c_b(rv, vv):
        _, gv = _vjp_rows(_f_ln_mod, (rv[0], vv[0], vv[1]), rv[1], 1)
        return [], gv

    d_csc1, d_csh1 = _rowwise(ln1c_b, [(ctx, D, 0, 0), (dh1, D, 0, T // TR)], [csc1, csh1],
                              [], [(1, D), (1, D)], nrows=C, tr=TR, name="ln1_ctx_bwd")

    d_mod_lat = jnp.concatenate([d_sh1, d_sc1, d_g1, d_sh2, d_sc2, d_g2], 1)
    zv = jnp.zeros((1, D), F32)
    d_mod_ctx = jnp.concatenate([d_csh1, d_csc1, zv, zv, zv, zv], 1)
    gs = {n: g for n, g in zip(s5_names, s5_grads)}
    gs["attn_sink"] = dsink[:, 0]
    gs["ssm_d"] = d_dskip
    gs["ln_mix_g"], gs["ln_mix_b"] = d_lg_mix, d_lb_mix
    gs["ln_mlp_g"], gs["ln_mlp_b"] = d_lg_mlp, d_lb_mlp
    gs["b_mlp1"], gs["b_mlp2"] = d_b1, d_b2
    return loss_p, grad_x, d_mod_lat, d_mod_ctx, gw, gs


def _my_pos():
    return lax.axis_index("x"), lax.axis_index("y"), lax.axis_index("c")


def _flip(p, bit):
    return 1 - p if bit else p


def _peer(pos, k):
    x, y, c = pos
    return (_flip(x, (k >> 2) & 1), _flip(y, (k >> 1) & 1), _flip(c, k & 1))


def _lin(pos):
    return 4 * pos[0] + 2 * pos[1] + pos[2]


def _allgather_small(v, name):
    r, w = v.shape

    def body(v_ref, out_ref, send_sems, recv_sems, local_sem):
        me = _my_pos()
        mine = pltpu.make_async_copy(v_ref, out_ref.at[_lin(me)], local_sem)
        mine.start()
        sends = []
        for k in range(1, N_DEV):
            cp = pltpu.make_async_remote_copy(src_ref=v_ref, dst_ref=out_ref.at[_lin(me)], send_sem=send_sems.at[k - 1],
                                              recv_sem=recv_sems.at[k - 1], device_id=_peer(me, k), device_id_type=MESH)
            cp.start()
            sends.append(cp)
        for k in range(1, N_DEV):
            peer = _peer(me, k)
            pltpu.make_async_remote_copy(src_ref=v_ref, dst_ref=out_ref.at[_lin(peer)], send_sem=send_sems.at[k - 1],
                                         recv_sem=recv_sems.at[k - 1], device_id=peer, device_id_type=MESH).wait_recv()
        for cp in sends:
            cp.wait_send()
        mine.wait()

    return pl.pallas_call(
        body,
        name=name,
        out_shape=jax.ShapeDtypeStruct((N_DEV, r, w), v.dtype),
        in_specs=[pl.BlockSpec(memory_space=pltpu.VMEM)],
        out_specs=pl.BlockSpec(memory_space=pltpu.VMEM),
        scratch_shapes=[pltpu.SemaphoreType.DMA((N_DEV - 1,)), pltpu.SemaphoreType.DMA((N_DEV - 1,)), pltpu.SemaphoreType.DMA],
        compiler_params=pltpu.CompilerParams(vmem_limit_bytes=VMEM_LIMIT_BYTES),
    )(v)


def _block_of(ref, kind, idx, n):
    start = pl.multiple_of(idx * n, 128)
    if kind == "col":
        return ref.at[:, pl.ds(start, n)]
    return ref.at[pl.ds(start, n), :]


def _handshake(peers):
    barrier = pltpu.get_barrier_semaphore()
    for peer in peers:
        pl.semaphore_signal(barrier, inc=1, device_id=peer, device_id_type=MESH)
    pl.semaphore_wait(barrier, len(peers))


def _allgather_weights_seq(shards, kinds, name, collective_id):
    nt = len(shards)
    hbm = pltpu.MemorySpace.HBM
    ins = [jax.new_ref(s, memory_space=hbm) for s in shards]
    outs = []
    for s, kind in zip(shards, kinds):
        k, n = s.shape
        shape = (k, n * N_DEV) if kind == "col" else (k * N_DEV, n)
        outs.append(jax.empty_ref(jax.ShapeDtypeStruct(shape, s.dtype), memory_space=hbm))

    @functools.partial(
        pl.kernel, mesh=plsc.ScalarSubcoreMesh(axis_name="seq", num_cores=1), name=name,
        scratch_types=(pltpu.SemaphoreType.DMA((nt, N_DEV - 1)), pltpu.SemaphoreType.DMA((nt, N_DEV - 1)),
                       pltpu.SemaphoreType.DMA((nt,))),
        compiler_params=pltpu.CompilerParams(collective_id=collective_id))
    def launch(send_sems, recv_sems, local_sems):
        x, y, c = _my_pos()
        me, sibling = (x, y, c), (x, y, 1 - c)
        chips = [(1 - x, y), (x, 1 - y), (1 - x, 1 - y)]
        _handshake([sibling] + [(*chip, c) for chip in chips])

        def blk(t, pos):
            n = shards[t].shape[1] if kinds[t] == "col" else shards[t].shape[0]
            return _block_of(outs[t], kinds[t], _lin(pos), n)

        def copy(t, k, block, to, src=None):
            return pltpu.make_async_remote_copy(src_ref=blk(t, block) if src is None else src, dst_ref=blk(t, block),
                                                send_sem=send_sems.at[t, k], recv_sem=recv_sems.at[t, k],
                                                device_id=to, device_id_type=MESH)

        local, sends = [], []
        for t in range(nt):
            mine = pltpu.make_async_copy(ins[t], blk(t, me), local_sems.at[t])
            mine.start()
            local.append(mine)
            first = [copy(t, 0, me, sibling, src=ins[t])]
            first += [copy(t, 1 + j, me, (*chip, c), src=ins[t]) for j, chip in enumerate(chips)]
            for cp in first:
                cp.start()
            sends += first
        for t in range(nt):
            for j, chip in enumerate(chips):
                copy(t, 1 + j, (*chip, c), me).wait_recv()
                fwd = copy(t, 4 + j, (*chip, c), sibling)
                fwd.start()
                sends.append(fwd)
        for t in range(nt):
            copy(t, 0, sibling, me).wait_recv()
            for j, chip in enumerate(chips):
                copy(t, 4 + j, (*chip, 1 - c), me).wait_recv()
        for cp in sends:
            cp.wait_send()
        for cp in local:
            cp.wait()

    launch()
    return [o[...] for o in outs]


def _allgather_small_seq(v, name, collective_id):
    hbm = pltpu.MemorySpace.HBM
    src = jax.new_ref(v, memory_space=hbm)
    out = jax.empty_ref(jax.ShapeDtypeStruct((N_DEV,) + v.shape, v.dtype), memory_space=hbm)

    @functools.partial(
        pl.kernel, mesh=plsc.ScalarSubcoreMesh(axis_name="seq", num_cores=1), name=name,
        scratch_types=(pltpu.SemaphoreType.DMA((N_DEV - 1,)), pltpu.SemaphoreType.DMA((N_DEV - 1,)), pltpu.SemaphoreType.DMA),
        compiler_params=pltpu.CompilerParams(collective_id=collective_id))
    def launch(send_sems, recv_sems, local_sem):
        me = _my_pos()
        _handshake([_peer(me, k) for k in range(1, N_DEV)])
        mine = pltpu.make_async_copy(src, out.at[_lin(me)], local_sem)
        mine.start()
        sends = []
        for k in range(1, N_DEV):
            cp = pltpu.make_async_remote_copy(src_ref=src, dst_ref=out.at[_lin(me)], send_sem=send_sems.at[k - 1],
                                              recv_sem=recv_sems.at[k - 1], device_id=_peer(me, k), device_id_type=MESH)
            cp.start()
            sends.append(cp)
        for k in range(1, N_DEV):
            peer = _peer(me, k)
            pltpu.make_async_remote_copy(src_ref=src, dst_ref=out.at[_lin(peer)], send_sem=send_sems.at[k - 1],
                                         recv_sem=recv_sems.at[k - 1], device_id=peer, device_id_type=MESH).wait_recv()
        for cp in sends:
            cp.wait_send()
        mine.wait()

    launch()
    return out[...]


N_CHIP = N_DEV // 2


def _chip_of(pos):
    return 2 * pos[0] + pos[1]


def _pair_exchange_seq(grads, kinds, name, collective_id):
    nt = len(grads)
    hbm = pltpu.MemorySpace.HBM
    shard_shapes = _shard_shapes(grads, kinds)
    ins = [jax.new_ref(g, memory_space=hbm) for g in grads]
    outs = [jax.empty_ref(jax.ShapeDtypeStruct((N_CHIP,) + s, g.dtype), memory_space=hbm) for s, g in zip(shard_shapes, grads)]

    @functools.partial(
        pl.kernel, mesh=plsc.ScalarSubcoreMesh(axis_name="seq", num_cores=1), name=name,
        scratch_types=(pltpu.SemaphoreType.DMA((nt, N_CHIP)), pltpu.SemaphoreType.DMA((nt, N_CHIP))),
        compiler_params=pltpu.CompilerParams(collective_id=collective_id))
    def launch(send_sems, recv_sems):
        x, y, c = _my_pos()
        sibling = (x, y, 1 - c)
        _handshake([sibling])
        copies = []
        for t in range(nt):
            n = shard_shapes[t][1] if kinds[t] == "col" else shard_shapes[t][0]
            for q in range(N_CHIP):
                cp = pltpu.make_async_remote_copy(src_ref=_block_of(ins[t], kinds[t], 2 * q + (1 - c), n), dst_ref=outs[t].at[q],
                                                  send_sem=send_sems.at[t, q], recv_sem=recv_sems.at[t, q],
                                                  device_id=sibling, device_id_type=MESH)
                cp.start()
                copies.append(cp)
        for cp in copies:
            cp.wait_recv()
        for cp in copies:
            cp.wait_send()

    launch()
    return [o[...] for o in outs]


def _pair_add(g, half, kind, name, after=()):
    nq, k, ns = half.shape
    tr = min(k, 512)
    c_idx = lax.axis_index("c").astype(jnp.int32).reshape(1)
    if kind == "col":
        g_spec = pl.BlockSpec((tr, ns), lambda q, i, c_ref: (i, 2 * q + c_ref[0]))
    else:
        g_spec = pl.BlockSpec((tr, ns), lambda q, i, c_ref: ((2 * q + c_ref[0]) * (k // tr) + i, 0))
    n_after = len(after)

    def kern(c_ref, g_ref, h_ref, *rest):
        o_ref = rest[n_after]
        o_ref[0] = (g_ref[...].astype(F32) + h_ref[0].astype(F32)).astype(o_ref.dtype)

    return pl.pallas_call(
        kern,
        name=name,
        grid_spec=pltpu.PrefetchScalarGridSpec(
            num_scalar_prefetch=1,
            grid=(nq, k // tr),
            in_specs=[g_spec, pl.BlockSpec((1, tr, ns), lambda q, i, c_ref: (q, i, 0))] + [pl.BlockSpec(memory_space=pl.ANY)] * n_after,
            out_specs=pl.BlockSpec((1, tr, ns), lambda q, i, c_ref: (q, i, 0)),
        ),
        out_shape=jax.ShapeDtypeStruct(half.shape, half.dtype),
        compiler_params=_cparams(("parallel", "parallel")),
    )(c_idx, g, half, *after)


def _chip_exchange_seq(psums, name, collective_id):
    nt = len(psums)
    hbm = pltpu.MemorySpace.HBM
    ins = [jax.new_ref(s, memory_space=hbm) for s in psums]
    outs = [jax.empty_ref(jax.ShapeDtypeStruct(s.shape, s.dtype), memory_space=hbm) for s in psums]

    @functools.partial(
        pl.kernel, mesh=plsc.ScalarSubcoreMesh(axis_name="seq", num_cores=1), name=name,
        scratch_types=(pltpu.SemaphoreType.DMA((nt, N_CHIP - 1)), pltpu.SemaphoreType.DMA((nt, N_CHIP - 1)),
                       pltpu.SemaphoreType.DMA((nt,))),
        compiler_params=pltpu.CompilerParams(collective_id=collective_id))
    def launch(send_sems, recv_sems, local_sems):
        me = _my_pos()
        peers = [_peer(me, k) for k in (2, 4, 6)]
        _handshake(peers)
        mine = _chip_of(me)
        local, sends = [], []
        for t in range(nt):
            cp = pltpu.make_async_copy(ins[t].at[mine], outs[t].at[mine], local_sems.at[t])
            cp.start()
            local.append(cp)
            for j, peer in enumerate(peers):
                cp = pltpu.make_async_remote_copy(src_ref=ins[t].at[_chip_of(peer)], dst_ref=outs[t].at[mine],
                                                  send_sem=send_sems.at[t, j], recv_sem=recv_sems.at[t, j],
                                                  device_id=peer, device_id_type=MESH)
                cp.start()
                sends.append(cp)
        for t in range(nt):
            for j, peer in enumerate(peers):
                pltpu.make_async_remote_copy(src_ref=ins[t].at[mine], dst_ref=outs[t].at[_chip_of(peer)],
                                             send_sem=send_sems.at[t, j], recv_sem=recv_sems.at[t, j],
                                             device_id=peer, device_id_type=MESH).wait_recv()
        for cp in sends:
            cp.wait_send()
        for cp in local:
            cp.wait()

    launch()
    return [o[...] for o in outs]


def _shard_shapes(grads, kinds):
    return [(g.shape[0], g.shape[1] // N_DEV) if kind == "col" else (g.shape[0] // N_DEV, g.shape[1]) for g, kind in zip(grads, kinds)]


def _adam(g_slots, w, m, v, *, tr, name, after=()):
    ns, r, wd = g_slots.shape
    tr = min(tr, r)
    assert r % tr == 0, (name, r, tr)
    n_after = len(after)

    def kern(g_ref, w_ref, m_ref, v_ref, *rest):
        go_ref, d_ref, mo_ref, vo_ref = rest[n_after:]
        g = g_ref[0].astype(F32)
        for s in range(1, ns):
            g = g + g_ref[s].astype(F32)
        delta, m_new, v_new = _adam_update(g, w_ref[...], m_ref[...], v_ref[...])
        go_ref[...] = g
        d_ref[...] = delta
        mo_ref[...] = m_new
        vo_ref[...] = v_new

    tile = pl.BlockSpec((tr, wd), lambda i: (i, 0))
    return pl.pallas_call(
        kern,
        name=name,
        grid=(r // tr,),
        in_specs=[pl.BlockSpec((ns, tr, wd), lambda i: (0, i, 0)), tile, tile, tile] + [pl.BlockSpec(memory_space=pl.ANY)] * n_after,
        out_specs=[tile] * 4,
        out_shape=[jax.ShapeDtypeStruct((r, wd), F32)] * 4,
        compiler_params=_cparams(("parallel",)),
    )(g_slots, w, m, v, *after)


def _adam_update(g, w, m, v):
    m_new = ADAM_B1 * m + (1.0 - ADAM_B1) * g
    v_new = ADAM_B2 * v + (1.0 - ADAM_B2) * (g * g)
    m_hat = m_new / (1.0 - ADAM_B1 ** ADAM_STEP)
    v_hat = v_new / (1.0 - ADAM_B2 ** ADAM_STEP)
    return -ADAM_LR * (m_hat / (jnp.sqrt(v_hat) + ADAM_EPS) + ADAM_WD * w), m_new, v_new


def _lane_offsets(sizes):
    offs, o = [], 0
    for n in sizes:
        offs.append(o)
        o += -(-n // LANES) * LANES
    return offs, o


def _pack_lanes(parts):
    cols = []
    for p_ in parts:
        flat = p_.reshape(1, -1).astype(F32)
        cols.append(jnp.pad(flat, ((0, 0), (0, (-flat.shape[1]) % LANES))))
    return jnp.concatenate(cols, 1)


def _adam_lanes(g_slots, ws, ms, vs, *, name, after=()):
    ns = g_slots.shape[0]
    npar, n_after = len(ws), len(after)
    sizes = [w.shape[1] for w in ws]
    offs, _ = _lane_offsets(sizes)

    def kern(g_ref, *refs):
        w_refs, m_refs, v_refs = refs[:npar], refs[npar:2 * npar], refs[2 * npar:3 * npar]
        outs = refs[3 * npar + n_after:]
        g_all = g_ref[0]
        for s in range(1, ns):
            g_all = g_all + g_ref[s]
        for j in range(npar):
            g = g_all[:, offs[j]:offs[j] + sizes[j]]
            delta, m_new, v_new = _adam_update(g, w_refs[j][...], m_refs[j][...], v_refs[j][...])
            outs[4 * j][...] = g
            outs[4 * j + 1][...] = delta
            outs[4 * j + 2][...] = m_new
            outs[4 * j + 3][...] = v_new

    vmem = pl.BlockSpec(memory_space=pltpu.VMEM)
    res = pl.pallas_call(
        kern,
        name=name,
        in_specs=[vmem] * (1 + 3 * npar) + [pl.BlockSpec(memory_space=pl.ANY)] * n_after,
        out_specs=[vmem] * (4 * npar),
        out_shape=[jax.ShapeDtypeStruct((1, n), F32) for n in sizes for _ in range(4)],
        compiler_params=pltpu.CompilerParams(vmem_limit_bytes=VMEM_LIMIT_BYTES),
    )(g_slots, *ws, *ms, *vs, *after)
    return [tuple(res[4 * j:4 * j + 4]) for j in range(npar)]


SMALL = ("c_ctx", "b_ada", "attn_sink", "ssm_a_re", "ssm_a_im", "ssm_log_dt", "ssm_b_re", "ssm_b_im", "ssm_c_re", "ssm_c_im",
         "ssm_d", "ln_mix_g", "ln_mix_b", "b_mlp1", "b_mlp2", "ln_mlp_g", "ln_mlp_b")
BIG = ("w_in", "w_glu", "w_attn_up", "w_ssm_up", "w_out", "w_mlp1", "w_mlp2")
BIG_KIND = ("col", "col", "col", "col", "row", "col", "row")
AG_GROUPS = (("w_in",), ("w_glu", "w_attn_up", "w_ssm_up", "w_out"), ("w_mlp1",), ("w_mlp2",))
AG_COLLECTIVE_ID0 = 1
RS_GROUPS = (("w_mlp2",), ("w_mlp1",), ("w_out", "w_attn_up", "w_ssm_up", "w_glu"), ("w_in",))
RS_COLLECTIVE_ID0 = AG_COLLECTIVE_ID0 + len(AG_GROUPS)
SMALL_EARLY = ("ssm_a_re", "ssm_a_im", "ssm_log_dt", "ssm_b_re", "ssm_b_im", "ssm_c_re", "ssm_c_im", "ssm_d")
SMALL_LATE = tuple(n for n in SMALL if n not in SMALL_EARLY)
SMALL_COLLECTIVE_ID0 = RS_COLLECTIVE_ID0 + 2 * len(RS_GROUPS)
LANES = 128


def _pack(parts):
    rows = []
    for p in parts:
        flat = p.reshape(-1).astype(F32)
        pad = (-flat.shape[0]) % LANES
        rows.append(jnp.pad(flat, (0, pad)).reshape(-1, LANES))
    packed = jnp.concatenate(rows, 0)
    return jnp.pad(packed, ((0, (-packed.shape[0]) % 8), (0, 0)))


def _unpack(packed, shapes):
    out, r0 = [], 0
    for s in shapes:
        n = math.prod(s)
        nr = -(-n // LANES)
        out.append(packed[r0:r0 + nr].reshape(-1)[:n].reshape(s))
        r0 += nr
    return out


WEIGHTS = ("c_ctx", "w_ada", "b_ada", "w_in", "attn_sink", "ssm_a_re", "ssm_a_im", "ssm_log_dt", "ssm_b_re", "ssm_b_im",
           "ssm_c_re", "ssm_c_im", "ssm_d", "w_glu", "w_attn_up", "w_ssm_up", "w_out", "ln_mix_g", "ln_mix_b", "w_mlp1",
           "b_mlp1", "w_mlp2", "b_mlp2", "ln_mlp_g", "ln_mlp_b")
ADA_COLS = 6 * D // N_DEV


def _step(x, c, ctx, loss_target, p, m, v):
    me = _lin(_my_pos())
    x2, ctx2, tgt2 = x[0], ctx[0], loss_target[0]

    wb = {}
    for gi, group in enumerate(AG_GROUPS):
        full = _allgather_weights_seq([p[n][0].astype(BF16) for n in group], [BIG_KIND[BIG.index(n)] for n in group],
                                      "allgather_seq%d" % gi, AG_COLLECTIVE_ID0 + gi)
        wb.update(zip(group, full))

    c_all = _allgather_small(jnp.broadcast_to(c, (8, D)), "gather_c")[:, 0, :]
    cc = p["c_ctx"].reshape(1, D)
    s_in = jnp.concatenate([c_all, cc, jnp.zeros((7, D), F32)], 0)
    s_act, = _rowwise(lambda rv, vv: ([_silu(rv[0])], []), [(s_in, D, 0, 0)], [], [(D, F32)], [], nrows=16, tr=16, name="silu_c")
    b_mine = lax.dynamic_slice_in_dim(p["b_ada"], me * ADA_COLS, ADA_COLS, axis=1)
    mod_part = _matmul(s_act, p["w_ada"][0], mode="nn", name="ada_fwd", tm=16, tn=512, bias=b_mine)
    mod_all = _allgather_small(mod_part, "gather_mod")
    mod_lat = lax.dynamic_index_in_dim(mod_all, me, axis=1, keepdims=False).reshape(1, 6 * D)
    mod_ctx = mod_all[:, 8, :].reshape(1, 6 * D)

    sp = {n: p[n][0] for n in SMALL if n not in ("c_ctx", "b_ada")}
    recv, halves = {}, {}

    def on_grad(gw):
        for gi, group in enumerate(RS_GROUPS):
            if gi not in halves and all(n in gw for n in group):
                kinds = [BIG_KIND[BIG.index(n)] for n in group]
                halves[gi] = (dict(gw), _pair_exchange_seq([gw[n] for n in group], kinds, "pair_exchange%d" % gi, RS_COLLECTIVE_ID0 + 2 * gi))

    def on_finish(key, after):
        gi = [i for i, group in enumerate(RS_GROUPS) if key in group][0]
        group = RS_GROUPS[gi]
        grads, half = halves[gi]
        prev = tuple(recv[n] for n in RS_GROUPS[gi - 1][:1]) if gi else ()
        if gi == len(RS_GROUPS) - 1:
            prev += (small["early"],)
        psums =[_pair_add(grads[n], h, BIG_KIND[BIG.index(n)], "pair_add_" + n, after=(after,) + prev) for n, h in zip(group, half)]
        recv.update(zip(group, _chip_exchange_seq(psums, "chip_exchange%d" % gi, RS_COLLECTIVE_ID0 + 2 * gi + 1)))
        return psums[-1]

    small = {}

    def on_early(gs_early):
        small["early"] = _allgather_small_seq(_pack([gs_early[n] for n in SMALL_EARLY]), "gather_small_early", SMALL_COLLECTIVE_ID0)

    total = {}

    def on_loss(loss_p):
        total["loss"] = lax.psum(loss_p[0, 0], ("x", "y", "c"))
        return total["loss"].reshape(1, 1)

    loss_p, grad_x, d_mod_lat, d_mod_ctx, gw, gs = _local_step(x2, ctx2, tgt2, mod_lat, mod_ctx, wb, sp, on_grad, on_loss, on_finish, on_early)

    g_early = small["early"]
    res = {}
    last = ()

    def adam_small(names, g_pack, tag, after):
        sm = _adam(g_pack, _pack([p[n] for n in names]), _pack([m[n] for n in names]), _pack([v[n] for n in names]),
                   tr=g_pack.shape[1], name="adam_small_" + tag, after=after)
        shapes = [p[n].shape for n in names]
        for j, outs in enumerate(zip(*[_unpack(a, shapes) for a in sm])):
            res[names[j]] = outs
        return (sm[0],)

    for gi, group in enumerate(RS_GROUPS):
        if gi == len(RS_GROUPS) - 1:
            last = adam_small(SMALL_EARLY, g_early, "early", last)
        for n in group:
            res[n] = _adam(recv[n], p[n][0], m[n][0], v[n][0], tr=256, name="adam_" + n, after=last)
            last = (res[n][0],)

    dm = jnp.concatenate([d_mod_lat, d_mod_ctx, jnp.zeros((6, 6 * D), F32)], 0)
    dm_all = _allgather_small_seq(dm, "gather_dmod", SMALL_COLLECTIVE_ID0 + 1)
    dm_all = lax.optimization_barrier((dm_all,) + last)[0]
    dm2 = jnp.concatenate([dm_all[:, 0, :], dm_all[:, 1, :]], 0)
    dm2_mine = lax.dynamic_slice_in_dim(dm2, me * ADA_COLS, ADA_COLS, axis=1)
    s2 = jnp.concatenate([s_act[0:8], jnp.broadcast_to(s_act[8:9], (8, D))], 0)
    g_w_ada = _matmul(s2, dm2_mine, mode="tn", name="dw_ada", tm=512, tn=ADA_COLS, after=last)
    dsc_part = _matmul(dm2_mine[8:16], p["w_ada"][0], mode="nt", name="d_silu_cctx", tm=8, tn=512, after=last)

    def cctx_b(rv, vv):
        _, pull = jax.vjp(_silu, vv[0])
        return [], [pull(jnp.sum(rv[0], axis=0, keepdims=True))[0]]

    g_cctx, = _rowwise(cctx_b, [(dsc_part, D, 0, 0)], [cc], [], [(1, D)], nrows=8, tr=8, name="cctx_bwd")
    gs["c_ctx"] = g_cctx
    gs["b_ada"] = d_mod_lat + d_mod_ctx

    res["w_ada"] = _adam(g_w_ada[None], p["w_ada"][0], m["w_ada"][0], v["w_ada"][0], tr=256, name="adam_w_ada")

    g_late = _allgather_small_seq(_pack_lanes([gs[n] for n in SMALL_LATE]), "gather_small_late", SMALL_COLLECTIVE_ID0 + 2)
    row = lambda a: a.reshape(1, -1)
    late = _adam_lanes(g_late, [row(p[n]) for n in SMALL_LATE], [row(m[n]) for n in SMALL_LATE], [row(v[n]) for n in SMALL_LATE],
                       name="adam_small_late", after=(res["w_ada"][0],))
    res.update(zip(SMALL_LATE, late))

    outs = [total["loss"], grad_x[None]]
    for j in range(4):
        outs += [res[n][j].reshape(p[n].shape) for n in WEIGHTS]
    return tuple(outs)


def kernel(x, c, ctx, c_ctx, w_ada, b_ada, w_in, attn_sink, ssm_a_re, ssm_a_im, ssm_log_dt, ssm_b_re, ssm_b_im, ssm_c_re, ssm_c_im, ssm_d, w_glu, w_attn_up, w_ssm_up, w_out, ln_mix_g, ln_mix_b, w_mlp1, b_mlp1, w_mlp2, b_mlp2, ln_mlp_g, ln_mlp_b, loss_target, m_c_ctx, m_w_ada, m_b_ada, m_w_in, m_attn_sink, m_ssm_a_re, m_ssm_a_im, m_ssm_log_dt, m_ssm_b_re, m_ssm_b_im, m_ssm_c_re, m_ssm_c_im, m_ssm_d, m_w_glu, m_w_attn_up, m_w_ssm_up, m_w_out, m_ln_mix_g, m_ln_mix_b, m_w_mlp1, m_b_mlp1, m_w_mlp2, m_b_mlp2, m_ln_mlp_g, m_ln_mlp_b, v_c_ctx, v_w_ada, v_b_ada, v_w_in, v_attn_sink, v_ssm_a_re, v_ssm_a_im, v_ssm_log_dt, v_ssm_b_re, v_ssm_b_im, v_ssm_c_re, v_ssm_c_im, v_ssm_d, v_w_glu, v_w_attn_up, v_w_ssm_up, v_w_out, v_ln_mix_g, v_ln_mix_b, v_w_mlp1, v_b_mlp1, v_w_mlp2, v_b_mlp2, v_ln_mlp_g, v_ln_mlp_b):
    given = dict(locals())
    p = {n: given[n] for n in WEIGHTS}
    m = {n: given["m_" + n] for n in WEIGHTS}
    v = {n: given["v_" + n] for n in WEIGHTS}
    return _step(x, c, ctx, loss_target, p, m, v)
```

```python
import functools
import math

import jax
import jax.numpy as jnp
from jax import lax
from jax.experimental import pallas as pl
from jax.experimental.pallas import tpu as pltpu
from jax.experimental.pallas import tpu_sc as plsc

F32 = jnp.float32
BF16 = jnp.bfloat16

N_DEV = 8
D = 2048
T = 2048
C = 256
TA = T + C
GRID_W = 64
HD = 128
NH = 8
NKV = 2
GROUP = NH // NKV
WINDOW = 128
QW = NH * HD
KVW = NKV * HD
SW = D // 4
SG = 16
NG = SW // SG
SP = 64
DFF = 4 * D
IN_COLS = QW + 2 * KVW + SW + 2 * D
ALPHA = 2.0 ** 0.25
LN_EPS = 1e-6
NEG_INF = -1e30
ROPE_BASE = 10000.0
ATT_SCALE = HD ** -0.5

NSEG = 8
GBLK = 8
NBLK = NG // GBLK
BW = GBLK * SP
UW = GBLK * SG

ADAM_LR = 0.001
ADAM_B1 = 0.9
ADAM_B2 = 0.999
ADAM_EPS = 1e-08
ADAM_WD = 0.01
ADAM_STEP = 10

VMEM_LIMIT_BYTES = 56 * 1024 * 1024
MESH = pl.DeviceIdType.MESH


def _cparams(sem=None):
    return pltpu.CompilerParams(dimension_semantics=sem, vmem_limit_bytes=VMEM_LIMIT_BYTES)


def _matmul(a, b, *, mode, name, out_dtypes=(F32,), tm=512, tn=512, tk=None, bias=None, extras=(), epilogue=None, after=(),
            out_t=None):
    if mode == "nn":
        (M, K), (K2, N) = a.shape, b.shape
    elif mode == "nt":
        (M, K), (N, K2) = a.shape, b.shape
    else:
        (K, M), (K2, N) = a.shape, b.shape
    assert K == K2, (name, a.shape, b.shape)
    tm, tn, tk = min(tm, M), min(tn, N), min(tk or K, K)
    assert M % tm == 0 and N % tn == 0 and K % tk == 0, (name, M, N, K, tm, tn, tk)
    nk = K // tk
    if mode == "tn":
        a_spec = pl.BlockSpec((tk, tm), lambda i, j, k: (k, i))
    else:
        a_spec = pl.BlockSpec((tm, tk), lambda i, j, k: (i, k))
    if mode == "nt":
        b_spec = pl.BlockSpec((tn, tk), lambda i, j, k: (j, k))
    else:
        b_spec = pl.BlockSpec((tk, tn), lambda i, j, k: (k, j))
    dims = {"nn": (((1,), (0,)), ((), ())), "nt": (((1,), (1,)), ((), ())), "tn": (((0,), (0,)), ((), ()))}[mode]
    in_specs = [a_spec, b_spec]
    operands = [a, b]
    if bias is not None:
        in_specs.append(pl.BlockSpec((1, tn), lambda i, j, k: (0, j)))
        operands.append(bias)
    for e in extras:
        in_specs.append(pl.BlockSpec((tm, tn), lambda i, j, k: (i, j)))
        operands.append(e)
    n_ex = len(extras)
    for t in after:
        in_specs.append(pl.BlockSpec(memory_space=pl.ANY))
        operands.append(t)
    n_after = len(after)
    n_out = len(out_dtypes)
    out_t = tuple(out_t) if out_t is not None else (False,) * n_out
    has_bias = bias is not None

    def kern(*refs):
        a_ref, b_ref = refs[0], refs[1]
        pos = 2
        bias_ref = None
        if has_bias:
            bias_ref = refs[pos]
            pos += 1
        ex_refs = refs[pos:pos + n_ex]
        pos += n_ex + n_after
        out_refs = refs[pos:pos + n_out]
        acc_ref = refs[pos + n_out] if nk > 1 else None

        def finish(r):
            if has_bias:
                r = r + bias_ref[...]
            outs = epilogue(r, *[e[...] for e in ex_refs]) if epilogue is not None else (r,)
            for o_ref, o, tr_ in zip(out_refs, outs, out_t):
                o_ref[...] = (o.T if tr_ else o).astype(o_ref.dtype)

        part = lax.dot_general(a_ref[...].astype(BF16), b_ref[...].astype(BF16), dims, preferred_element_type=F32)
        if nk == 1:
            finish(part)
        else:
            k = pl.program_id(2)

            @pl.when(k == 0)
            def _():
                acc_ref[...] = part

            @pl.when(k > 0)
            def _():
                acc_ref[...] += part

            @pl.when(k == nk - 1)
            def _():
                finish(acc_ref[...])

    outs = pl.pallas_call(
        kern,
        name=name,
        grid=(M // tm, N // tn, nk),
        in_specs=in_specs,
        out_specs=[pl.BlockSpec((tn, tm), lambda i, j, k: (j, i)) if tr_ else pl.BlockSpec((tm, tn), lambda i, j, k: (i, j))
                   for tr_ in out_t],
        out_shape=[jax.ShapeDtypeStruct((N, M) if tr_ else (M, N), dt) for dt, tr_ in zip(out_dtypes, out_t)],
        scratch_shapes=[pltpu.VMEM((tm, tn), F32)] if nk > 1 else [],
        compiler_params=_cparams(("parallel", "parallel", "arbitrary")),
    )(*operands)
    return outs[0] if n_out == 1 else tuple(outs)


def _rowwise(fn, rows, vecs, outs, vec_outs, *, nrows, tr, name, after=()):
    n_rows, n_vecs, n_outs, n_after = len(rows), len(vecs), len(outs), len(after)
    in_specs = [pl.BlockSpec((tr, w), lambda i, cb=cb, ro=ro: (i + ro, cb)) for (_, w, cb, ro) in rows]
    in_specs += [pl.BlockSpec(v.shape, lambda i: (0, 0)) for v in vecs]
    in_specs += [pl.BlockSpec(memory_space=pl.ANY)] * n_after
    outs = [o if len(o) == 3 else (*o, False) for o in outs]
    out_specs = [pl.BlockSpec((w, tr), lambda i: (0, i)) if tr_ else pl.BlockSpec((tr, w), lambda i: (i, 0)) for (w, _, tr_) in outs]
    out_specs += [pl.BlockSpec(s, lambda i: (0, 0)) for s in vec_outs]
    out_shape = [jax.ShapeDtypeStruct((w, nrows) if tr_ else (nrows, w), dt) for (w, dt, tr_) in outs]
    out_tr = [tr_ for (_, _, tr_) in outs]
    out_shape += [jax.ShapeDtypeStruct(s, F32) for s in vec_outs]

    def kern(*refs):
        rvals = [r[...].astype(F32) for r in refs[:n_rows]]
        vvals = [r[...] for r in refs[n_rows:n_rows + n_vecs]]
        first_out = n_rows + n_vecs + n_after
        o_refs = refs[first_out:first_out + n_outs]
        v_refs = refs[first_out + n_outs:]
        ro, vo = fn(rvals, vvals)
        for r, val, tr_ in zip(o_refs, ro, out_tr):
            r[...] = (val.astype(F32).T if tr_ else val).astype(r.dtype)
        i = pl.program_id(0)
        for r, val in zip(v_refs, vo):
            @pl.when(i == 0)
            def _(r=r, val=val):
                r[...] = val.astype(F32)

            @pl.when(i > 0)
            def _(r=r, val=val):
                r[...] += val.astype(F32)

    res = pl.pallas_call(
        kern,
        name=name,
        grid=(nrows // tr,),
        in_specs=in_specs,
        out_specs=out_specs,
        out_shape=out_shape,
        compiler_params=_cparams(("arbitrary",)),
    )(*[r[0] for r in rows], *vecs, *after)
    return list(res)


def _ln(x):
    mu = jnp.mean(x, axis=-1, keepdims=True)
    xc = x - mu
    var = jnp.mean(xc * xc, axis=-1, keepdims=True)
    return xc * lax.rsqrt(var + LN_EPS)


def _sigmoid(x):
    return 1.0 / (1.0 + jnp.exp(-x))


def _gelu(x):
    return 0.5 * x * (1.0 + jnp.tanh(math.sqrt(2.0 / math.pi) * (x + 0.044715 * (x * x * x))))


def _silu(x):
    return x * _sigmoid(x)


def _f_ln_mod(x, sc, sh):
    return _ln(x) * (1.0 + sc) + sh


def _f_glu(z):
    return z[:, :SW] * _sigmoid(z[:, SW:])


def _f_mix(ga, gs, attn_d, ssm_d):
    return _sigmoid(ga) * attn_d + _sigmoid(gs) * ssm_d


def _f_post1(x, y, g1, lg, lb, sc2, sh2):
    r1 = ALPHA * x + g1 * y
    x1 = _ln(r1) * lg + lb
    h2 = _ln(x1) * (1.0 + sc2) + sh2
    return x1, h2


def _f_loss(x1, mlp, tgt, g2, lg, lb, b2z):
    r2 = ALPHA * x1 + g2 * (mlp + b2z)
    out = _ln(r2) * lg + lb
    err = out - tgt
    return 0.5 * jnp.sum(err * err) * (1.0 / D)


def _rope_tables():
    rows = T // GRID_W
    row = jnp.repeat(jnp.arange(rows), GRID_W)
    col = jnp.tile(jnp.arange(GRID_W), rows)
    n_freq = HD // 4
    freqs = ROPE_BASE ** (-jnp.arange(n_freq, dtype=F32) / n_freq)
    ang_r = row.astype(F32)[:, None] * freqs
    ang_c = col.astype(F32)[:, None] * freqs
    ang = jnp.concatenate([ang_r, ang_r, ang_c, ang_c], -1)
    cos, sin = jnp.cos(ang), jnp.sin(ang)
    lo = (jnp.arange(HD) % (HD // 2)) < (HD // 4)
    sin_a = jnp.where(lo[None, :], -sin, 0.0)
    sin_b = jnp.where(lo[None, :], 0.0, sin)
    return cos, sin_a, sin_b


def _rope(x, cos, sa, sb):
    return x * cos + pltpu.roll(x, 96, 1) * sa + pltpu.roll(x, 32, 1) * sb


def _rope_t(dy, cos, sa, sb):
    return dy * cos + pltpu.roll(dy * sa, 32, 1) + pltpu.roll(dy * sb, 96, 1)


BAND = 3 * WINDOW
KPAD = T + 2 * WINDOW


def _attn_fill_kv(k_ref, v_ref, cos_ref, sa_ref, sb_ref, kp, vp, kc, vc):
    zeros = jnp.zeros((WINDOW, KVW), BF16)
    kp[0:WINDOW, :] = zeros
    kp[WINDOW + T:KPAD, :] = zeros
    vp[0:WINDOW, :] = zeros
    vp[WINDOW + T:KPAD, :] = zeros
    for hh in range(NKV):
        cs = slice(hh * HD, (hh + 1) * HD)
        for r0 in range(0, T, 512):
            rs = slice(r0, r0 + 512)
            kr = _rope(k_ref[rs, cs], cos_ref[rs, :], sa_ref[rs, :], sb_ref[rs, :])
            kp[WINDOW + r0:WINDOW + r0 + 512, cs] = kr.astype(BF16)
    vp[WINDOW:WINDOW + T, :] = v_ref[0:T, :].astype(BF16)
    kc[...] = k_ref[T:TA, :].astype(BF16)
    vc[...] = v_ref[T:TA, :].astype(BF16)


GROWS = GROUP * WINDOW


def _attn_scores(n, kvh, q_ref, cos_ref, sa_ref, sb_ref, sink_ref, kp, kc):
    r0 = pl.multiple_of(n * WINDOW, WINDOW)
    cos = cos_ref[pl.ds(r0, WINDOW), :]
    sa = sa_ref[pl.ds(r0, WINDOW), :]
    sb = sb_ref[pl.ds(r0, WINDOW), :]
    heads = range(kvh * GROUP, (kvh + 1) * GROUP)
    q_g = jnp.concatenate([_rope(q_ref[:, h * HD:(h + 1) * HD], cos, sa, sb).astype(BF16) for h in heads], axis=0)
    kb = kp[pl.ds(r0, BAND), kvh * HD:(kvh + 1) * HD]
    kcb = kc[:, kvh * HD:(kvh + 1) * HD]
    nt = (((1,), (1,)), ((), ()))
    s_loc = lax.dot_general(q_g, kb, nt, preferred_element_type=F32) * ATT_SCALE
    s_ctx = lax.dot_general(q_g, kcb, nt, preferred_element_type=F32) * ATT_SCALE
    row = lax.broadcasted_iota(jnp.int32, (GROWS, BAND), 0) & (WINDOW - 1)
    col = lax.broadcasted_iota(jnp.int32, (GROWS, BAND), 1)
    rel = col - WINDOW - row
    kpos = r0 - WINDOW + col
    valid = (jnp.abs(rel) <= WINDOW) & (kpos >= 0) & (kpos < T)
    s_loc = jnp.where(valid, s_loc, NEG_INF)
    sk = jnp.concatenate([jnp.broadcast_to(sink_ref[0:1, h:h + 1], (WINDOW, 1)) for h in heads], axis=0)
    m = jnp.maximum(jnp.maximum(jnp.max(s_loc, -1, keepdims=True), jnp.max(s_ctx, -1, keepdims=True)), sk)
    e_loc = jnp.exp(s_loc - m)
    e_ctx = jnp.exp(s_ctx - m)
    e_sink = jnp.exp(sk - m)
    inv = 1.0 / (jnp.sum(e_loc, -1, keepdims=True) + jnp.sum(e_ctx, -1, keepdims=True) + e_sink)
    return q_g, r0, e_loc * inv, e_ctx * inv, e_sink * inv


def _attn_fwd(proj, sink, tabs):
    cos, sa, sb = tabs

    def kern(q_ref, k_ref, v_ref, cos_ref, sa_ref, sb_ref, sink_ref, o_ref, kp, vp, kc, vc):
        n = pl.program_id(0)

        @pl.when(n == 0)
        def _():
            _attn_fill_kv(k_ref, v_ref, cos_ref, sa_ref, sb_ref, kp, vp, kc, vc)

        for kvh in range(NKV):
            _, r0, p_loc, p_ctx, _ = _attn_scores(n, kvh, q_ref, cos_ref, sa_ref, sb_ref, sink_ref, kp, kc)
            vb = vp[pl.ds(r0, BAND), kvh * HD:(kvh + 1) * HD]
            vcb = vc[:, kvh * HD:(kvh + 1) * HD]
            o = jnp.dot(p_loc.astype(BF16), vb, preferred_element_type=F32)
            o = o + jnp.dot(p_ctx.astype(BF16), vcb, preferred_element_type=F32)
            for g in range(GROUP):
                h = kvh * GROUP + g
                o_ref[:, h * HD:(h + 1) * HD] = o[g * WINDOW:(g + 1) * WINDOW, :].astype(o_ref.dtype)

    full = lambda shape: pl.BlockSpec(shape, lambda n: (0, 0))
    return pl.pallas_call(
        kern,
        name="attn_fwd",
        grid=(T // WINDOW,),
        in_specs=[
            pl.BlockSpec((WINDOW, QW), lambda n: (n, 0)),
            pl.BlockSpec((TA, KVW), lambda n: (0, QW // KVW)),
            pl.BlockSpec((TA, KVW), lambda n: (0, QW // KVW + 1)),
            full((T, HD)), full((T, HD)), full((T, HD)), full((1, NH)),
        ],
        out_specs=pl.BlockSpec((WINDOW, QW), lambda n: (n, 0)),
        out_shape=jax.ShapeDtypeStruct((T, QW), BF16),
        scratch_shapes=[pltpu.VMEM((KPAD, KVW), BF16), pltpu.VMEM((KPAD, KVW), BF16),
                        pltpu.VMEM((C, KVW), BF16), pltpu.VMEM((C, KVW), BF16)],
        compiler_params=_cparams(("arbitrary",)),
    )(proj, proj, proj, cos, sa, sb, sink)


def _attn_bwd(proj, d_attn, sink, tabs):
    cos, sa, sb = tabs
    n_blocks = T // WINDOW

    def kern(q_ref, k_ref, v_ref, do_ref, cos_ref, sa_ref, sb_ref, sink_ref,
             dq_ref, dk_ref, dv_ref, dsink_ref, kp, vp, kc, vc, dkp, dvp, dkc, dvc):
        n = pl.program_id(0)

        @pl.when(n == 0)
        def _():
            _attn_fill_kv(k_ref, v_ref, cos_ref, sa_ref, sb_ref, kp, vp, kc, vc)
            dkp[...] = jnp.zeros_like(dkp)
            dvp[...] = jnp.zeros_like(dvp)
            dkc[...] = jnp.zeros_like(dkc)
            dvc[...] = jnp.zeros_like(dvc)
            dsink_ref[...] = jnp.zeros_like(dsink_ref)

        nt = (((1,), (1,)), ((), ()))
        tn = (((0,), (0,)), ((), ()))
        for kvh in range(NKV):
            cs = slice(kvh * HD, (kvh + 1) * HD)
            heads = range(kvh * GROUP, (kvh + 1) * GROUP)
            q_g, r0, p_loc, p_ctx, p_sink = _attn_scores(n, kvh, q_ref, cos_ref, sa_ref, sb_ref, sink_ref, kp, kc)
            kb = kp[pl.ds(r0, BAND), cs]
            vb = vp[pl.ds(r0, BAND), cs]
            kcb = kc[:, cs]
            vcb = vc[:, cs]
            do_g = jnp.concatenate([do_ref[:, h * HD:(h + 1) * HD] for h in heads], axis=0)
            dp_loc = lax.dot_general(do_g, vb, nt, preferred_element_type=F32)
            dp_ctx = lax.dot_general(do_g, vcb, nt, preferred_element_type=F32)
            delta = jnp.sum(p_loc * dp_loc, -1, keepdims=True) + jnp.sum(p_ctx * dp_ctx, -1, keepdims=True)
            ds_loc = (p_loc * (dp_loc - delta) * ATT_SCALE).astype(BF16)
            ds_ctx = (p_ctx * (dp_ctx - delta) * ATT_SCALE).astype(BF16)
            dq = jnp.dot(ds_loc, kb, preferred_element_type=F32) + jnp.dot(ds_ctx, kcb, preferred_element_type=F32)
            cos = cos_ref[pl.ds(r0, WINDOW), :]
            sa_ = sa_ref[pl.ds(r0, WINDOW), :]
            sb_ = sb_ref[pl.ds(r0, WINDOW), :]
            dkp[pl.ds(r0, BAND), cs] += lax.dot_general(ds_loc, q_g, tn, preferred_element_type=F32)
            dkc[:, cs] += lax.dot_general(ds_ctx, q_g, tn, preferred_element_type=F32)
            dvp[pl.ds(r0, BAND), cs] += lax.dot_general(p_loc.astype(BF16), do_g, tn, preferred_element_type=F32)
            dvc[:, cs] += lax.dot_general(p_ctx.astype(BF16), do_g, tn, preferred_element_type=F32)
            dsk_rows = p_sink * delta
            for g, h in enumerate(heads):
                rs = slice(g * WINDOW, (g + 1) * WINDOW)
                dq_ref[:, h * HD:(h + 1) * HD] = _rope_t(dq[rs, :], cos, sa_, sb_).astype(dq_ref.dtype)
                dsk = -jnp.sum(dsk_rows[rs, :], axis=0, keepdims=True)
                dsink_ref[h:h + 1, :] += jnp.broadcast_to(dsk, (1, HD))

        @pl.when(n == n_blocks - 1)
        def _():
            for hh in range(NKV):
                cs = slice(hh * HD, (hh + 1) * HD)
                for r0 in range(0, T, 512):
                    rs = slice(r0, r0 + 512)
                    g = dkp[WINDOW + r0:WINDOW + r0 + 512, cs]
                    dk_ref[rs, cs] = _rope_t(g, cos_ref[rs, :], sa_ref[rs, :], sb_ref[rs, :]).astype(dk_ref.dtype)
            dk_ref[T:TA, :] = dkc[...].astype(dk_ref.dtype)
            dv_ref[0:T, :] = dvp[WINDOW:WINDOW + T, :].astype(dv_ref.dtype)
            dv_ref[T:TA, :] = dvc[...].astype(dv_ref.dtype)

    full = lambda shape: pl.BlockSpec(shape, lambda n: (0, 0))
    return pl.pallas_call(
        kern,
        name="attn_bwd",
        grid=(n_blocks,),
        in_specs=[
            pl.BlockSpec((WINDOW, QW), lambda n: (n, 0)),
            pl.BlockSpec((TA, KVW), lambda n: (0, QW // KVW)),
            pl.BlockSpec((TA, KVW), lambda n: (0, QW // KVW + 1)),
            pl.BlockSpec((WINDOW, QW), lambda n: (n, 0)),
            full((T, HD)), full((T, HD)), full((T, HD)), full((1, NH)),
        ],
        out_specs=[pl.BlockSpec((WINDOW, QW), lambda n: (n, 0)), full((TA, KVW)), full((TA, KVW)), full((NH, HD))],
        out_shape=[jax.ShapeDtypeStruct((T, QW), BF16), jax.ShapeDtypeStruct((TA, KVW), BF16),
                   jax.ShapeDtypeStruct((TA, KVW), BF16), jax.ShapeDtypeStruct((NH, HD), F32)],
        scratch_shapes=[pltpu.VMEM((KPAD, KVW), BF16), pltpu.VMEM((KPAD, KVW), BF16),
                        pltpu.VMEM((C, KVW), BF16), pltpu.VMEM((C, KVW), BF16),
                        pltpu.VMEM((KPAD, KVW), F32), pltpu.VMEM((KPAD, KVW), F32),
                        pltpu.VMEM((C, KVW), F32), pltpu.VMEM((C, KVW), F32)],
        compiler_params=_cparams(("arbitrary",)),
    )(proj, proj, proj, d_attn, cos, sa, sb, sink)


def _s5_prep(a_re, a_im, log_dt, b_re, b_im, c_re, c_im):
    lam = lax.complex(a_re, a_im)
    dt = jnp.exp(log_dt)[..., None]
    lam_bar = jnp.exp(lam * dt)
    b_bar = ((lam_bar - 1.0) / lam)[..., None] * lax.complex(b_re, b_im)
    def lam_rows(v):
        return v.reshape(2, NBLK, 1, BW)

    lam_l = jnp.concatenate([lam_rows(jnp.real(lam_bar)), lam_rows(jnp.imag(lam_bar))], -1)
    lam_l = jnp.broadcast_to(lam_l, (2, NBLK, 8, 2 * BW))
    diag = (jnp.arange(UW)[:, None] // SG) == (jnp.arange(BW)[None, :] // SP)

    def blocks(v):
        return jnp.where(diag, jnp.tile(v.reshape(2, NBLK, UW, SP), (1, 1, 1, GBLK)), 0.0)

    b_t = jnp.swapaxes(b_bar, -1, -2)
    bmat = jnp.concatenate([blocks(jnp.real(b_t)), blocks(jnp.imag(b_t))], -1)
    cmat = jnp.concatenate([blocks(c_re), -blocks(c_im)], -1)
    return lam_l, bmat, cmat


def _cmul(ar, ai, br, bi):
    return ar * br - ai * bi, ar * bi + ai * br


def _shift_rows(x, rev, fill):
    r = lax.broadcasted_iota(jnp.int32, x.shape, 0)
    down = jnp.where(r == 0, fill, pltpu.roll(x, 1, 0))
    up = jnp.where(r == NSEG - 1, fill, pltpu.roll(x, NSEG - 1, 0))
    return jnp.where(rev == 0, down, up)


def _edge_row(x, rev):
    last = jnp.broadcast_to(x[NSEG - 1:NSEG, :], x.shape)
    first = jnp.broadcast_to(x[0:1, :], x.shape)
    return jnp.where(rev == 0, last, first)


def _seg_scan(get, put, base, seglen, lr, li, rev, cin, acc_fn=None, acc0=()):
    zero = jnp.zeros((NSEG, BW), F32)

    def rows(k):
        j = jnp.where(rev == 0, k, seglen - 1 - k)
        return pl.ds(pl.multiple_of(base + j * NSEG, NSEG), NSEG)

    def local(k, carry):
        sr, si = carry
        xr, xi = get(rows(k))
        tr, ti = _cmul(lr, li, sr, si)
        sr, si = tr + xr, ti + xi
        put(rows(k), sr, si)
        return sr, si

    er, ei = lax.fori_loop(0, seglen, local, (zero, zero))
    lpr, lpi = lr, li
    assert seglen & (seglen - 1) == 0, seglen
    for _ in range(seglen.bit_length() - 1):
        lpr, lpi = _cmul(lpr, lpi, lpr, lpi)
    cr, ci = _shift_rows(zero, rev, cin[0]), _shift_rows(zero, rev, cin[1])
    for _ in range(NSEG - 1):
        tr, ti = _cmul(lpr, lpi, cr, ci)
        cr, ci = _shift_rows(er + tr, rev, cin[0]), _shift_rows(ei + ti, rev, cin[1])

    def fix(k, carry):
        tr, ti = _cmul(lr, li, carry[0], carry[1])
        xr, xi = get(rows(k))
        fr, fi = xr + tr, xi + ti
        put(rows(k), fr, fi)
        if acc_fn is None:
            return tr, ti
        j = jnp.where(rev == 0, k, seglen - 1 - k)
        return (tr, ti) + tuple(acc_fn(j, fr, fi, carry[2:]))

    out = lax.fori_loop(0, seglen, fix, (cr, ci) + tuple(acc0))
    tr, ti = out[0], out[1]
    leaving = (_edge_row(er + tr, rev), _edge_row(ei + ti, rev))
    return leaving if acc_fn is None else (leaving, out[2:])


RCH = 256
CSEG = C // NSEG
TSEG = T // NSEG
UCOL0 = (QW + 2 * KVW) // UW


REGIONS = ((0, TSEG), (T, CSEG))


def _state_access(ref, lead=()):
    def get(rows):
        return ref[(*lead, rows, slice(0, BW))], ref[(*lead, rows, slice(BW, 2 * BW))]

    def put(rows, re, im):
        ref[(*lead, rows, slice(0, BW))] = re
        ref[(*lead, rows, slice(BW, 2 * BW))] = im

    return get, put


def _interleave_rows(src_ref, dst_ref, regions=REGIONS):
    for base, seglen in regions:
        def body(j, carry, base=base, seglen=seglen):
            dst_ref[pl.ds(pl.multiple_of(base + j * NSEG, NSEG), NSEG), :] = src_ref[pl.ds(base + j, NSEG, stride=seglen), :]
            return carry

        lax.fori_loop(0, seglen, body, 0, unroll=8)


def _deinterleave_rows(src_ref, dst_ref, regions=REGIONS):
    for base, seglen in regions:
        def body(j, carry, base=base, seglen=seglen):
            dst_ref[pl.ds(base + j, NSEG, stride=seglen), :] = src_ref[pl.ds(pl.multiple_of(base + j * NSEG, NSEG), NSEG), :]
            return carry

        lax.fori_loop(0, seglen, body, 0, unroll=8)


def _s5_fwd(proj, dskip, lam, bmat, cmat):
    def kern(u_ref, dk_ref, lam_ref, b_ref, c_ref, s_ref, ssm_ref, ge_ref, up_ref, yp_ref):
        d = pl.program_id(1)

        @pl.when(d == 0)
        def _():
            _interleave_rows(u_ref, up_ref)

        bm = b_ref[0, 0].astype(BF16)
        for r0 in range(0, TA, RCH):
            s_ref[0, 0, r0:r0 + RCH, :] = jnp.dot(up_ref[r0:r0 + RCH, :].astype(BF16), bm, preferred_element_type=F32)
        lr = lam_ref[0, 0, :, 0:BW]
        li = lam_ref[0, 0, :, BW:2 * BW]
        zero = jnp.zeros((NSEG, BW), F32)
        get, put = _state_access(s_ref, (0, 0))
        mid = _seg_scan(get, put, T, CSEG, lr, li, d, (zero, zero))
        _seg_scan(get, put, 0, TSEG, lr, li, d, mid)
        cm = c_ref[0, 0].astype(BF16)
        for r0 in range(0, T, RCH):
            y = lax.dot_general(s_ref[0, 0, r0:r0 + RCH, :].astype(BF16), cm, (((1,), (1,)), ((), ())), preferred_element_type=F32)

            @pl.when(d == 0)
            def _(y=y, r0=r0):
                yp_ref[r0:r0 + RCH, :] = y + dk_ref[...] * up_ref[r0:r0 + RCH, :]

            @pl.when(d == 1)
            def _(y=y, r0=r0):
                yp_ref[r0:r0 + RCH, :] += y

        @pl.when(d == 1)
        def _():
            _deinterleave_rows(yp_ref, ssm_ref, REGIONS[:1])
            for r0 in range(0, T, RCH):
                ge_ref[r0:r0 + RCH, :] = _gelu(ssm_ref[r0:r0 + RCH, :]).astype(ge_ref.dtype)

    blk4 = lambda shape: pl.BlockSpec((1, 1) + shape, lambda b, d: (d, b, 0, 0))
    return pl.pallas_call(
        kern,
        name="s5_fwd",
        grid=(NBLK, 2),
        in_specs=[pl.BlockSpec((TA, UW), lambda b, d: (0, UCOL0 + b)), pl.BlockSpec((1, UW), lambda b, d: (0, b)),
                  blk4((8, 2 * BW)), blk4((UW, 2 * BW)), blk4((UW, 2 * BW))],
        out_specs=[blk4((TA, 2 * BW)), pl.BlockSpec((T, UW), lambda b, d: (0, b)), pl.BlockSpec((T, UW), lambda b, d: (0, b))],
        out_shape=[jax.ShapeDtypeStruct((2, NBLK, TA, 2 * BW), F32), jax.ShapeDtypeStruct((T, SW), F32),
                   jax.ShapeDtypeStruct((T, SW), BF16)],
        scratch_shapes=[pltpu.VMEM((TA, UW), F32), pltpu.VMEM((T, UW), F32)],
        compiler_params=_cparams(("parallel", "arbitrary")),
    )(proj, dskip, lam, bmat, cmat)


def _s5_bwd(d_ge, ssm, proj, dskip, states, lam, bmat, cmat):
    nt = (((1,), (1,)), ((), ()))
    tn = (((0,), (0,)), ((), ()))

    def kern(dge_ref, ssm_ref, u_ref, dk_ref, s_ref, lam_ref, b_ref, c_ref,
             du_ref, ddk_ref, dlam_ref, db_ref, dc_ref, g_ref, dua_ref, dssm_ref, up_ref, nat_ref):
        d = pl.program_id(1)

        @pl.when(d == 0)
        def _():
            ddk = jnp.zeros((1, UW), F32)
            for r0 in range(0, T, RCH):
                rs = slice(r0, r0 + RCH)
                _, pull = jax.vjp(_gelu, ssm_ref[rs, :])
                dssm = pull(dge_ref[rs, :])[0]
                nat_ref[rs, :] = dssm
                ddk = ddk + jnp.sum(dssm * u_ref[rs, :], axis=0, keepdims=True)
            ddk_ref[...] = ddk
            _interleave_rows(nat_ref, dssm_ref, REGIONS[:1])
            _interleave_rows(u_ref, up_ref)
            for r0 in range(0, T, RCH):
                dua_ref[r0:r0 + RCH, :] = dssm_ref[r0:r0 + RCH, :] * dk_ref[...]
            dua_ref[T:TA, :] = jnp.zeros((C, UW), F32)

        cm = c_ref[0, 0].astype(BF16)
        for r0 in range(0, T, RCH):
            g_ref[r0:r0 + RCH, :] = jnp.dot(dssm_ref[r0:r0 + RCH, :].astype(BF16), cm, preferred_element_type=F32)
        g_ref[T:TA, :] = jnp.zeros((C, 2 * BW), F32)
        lr = lam_ref[0, 0, :, 0:BW]
        li = lam_ref[0, 0, :, BW:2 * BW]
        zero = jnp.zeros((NSEG, BW), F32)
        get_g, put_g = _state_access(g_ref)

        get_s, _ = _state_access(s_ref, (0, 0))

        def dlam_fold(base, seglen, s_in):
            def rows(j):
                return pl.ds(pl.multiple_of(base + j * NSEG, NSEG), NSEG)

            jb = jnp.where(d == 0, 0, seglen - 1)
            jn = jnp.where(d == 0, seglen - 1, 0)
            sp = get_s(rows(jn))
            edge = (_shift_rows(sp[0], d, s_in[0]), _shift_rows(sp[1], d, s_in[1]))

            def fold(j, gr, gi, acc):
                jp = jnp.clip(jnp.where(d == 0, j - 1, j + 1), 0, seglen - 1)
                sr, si = get_s(rows(jp))
                sr = jnp.where(j == jb, edge[0], sr)
                si = jnp.where(j == jb, edge[1], si)
                return acc[0] + (gr * sr + gi * si), acc[1] + (gi * sr - gr * si)

            return fold

        r_mid = jnp.where(d == 0, TA - 1, T)
        s_mid = tuple(jnp.broadcast_to(t, (NSEG, BW)) for t in get_s(pl.ds(r_mid, 1)))
        mid, acc = _seg_scan(get_g, put_g, 0, TSEG, lr, -li, 1 - d, (zero, zero), dlam_fold(0, TSEG, s_mid), (zero, zero))
        _, acc = _seg_scan(get_g, put_g, T, CSEG, lr, -li, 1 - d, mid, dlam_fold(T, CSEG, (zero, zero)), acc)
        dlam_ref[0, 0, :, 0:BW] = acc[0]
        dlam_ref[0, 0, :, BW:2 * BW] = acc[1]

        bm = b_ref[0, 0].astype(BF16)
        db = jnp.zeros((UW, 2 * BW), F32)
        dc = jnp.zeros((UW, 2 * BW), F32)
        for r0 in range(0, TA, RCH):
            rs = slice(r0, r0 + RCH)
            g = g_ref[rs, :].astype(BF16)
            dua_ref[rs, :] += lax.dot_general(g, bm, nt, preferred_element_type=F32)
            db = db + lax.dot_general(up_ref[rs, :].astype(BF16), g, tn, preferred_element_type=F32)
            if r0 < T:
                dc = dc + lax.dot_general(dssm_ref[rs, :].astype(BF16), s_ref[0, 0, rs, :].astype(BF16), tn,
                                          preferred_element_type=F32)
        db_ref[0, 0] = db
        dc_ref[0, 0] = dc

        @pl.when(d == 1)
        def _():
            _deinterleave_rows(dua_ref, nat_ref)
            du_ref[...] = nat_ref[...].astype(du_ref.dtype)

    blk4 = lambda shape: pl.BlockSpec((1, 1) + shape, lambda b, d: (d, b, 0, 0))
    lat = pl.BlockSpec((T, UW), lambda b, d: (0, b))
    vec = pl.BlockSpec((1, UW), lambda b, d: (0, b))
    return pl.pallas_call(
        kern,
        name="s5_bwd",
        grid=(NBLK, 2),
        in_specs=[lat, lat, pl.BlockSpec((TA, UW), lambda b, d: (0, UCOL0 + b)), vec,
                  blk4((TA, 2 * BW)), blk4((8, 2 * BW)), blk4((UW, 2 * BW)), blk4((UW, 2 * BW))],
        out_specs=[pl.BlockSpec((TA, UW), lambda b, d: (0, b)), vec, blk4((8, 2 * BW)), blk4((UW, 2 * BW)), blk4((UW, 2 * BW))],
        out_shape=[jax.ShapeDtypeStruct((TA, SW), BF16), jax.ShapeDtypeStruct((1, SW), F32),
                   jax.ShapeDtypeStruct((2, NBLK, 8, 2 * BW), F32),
                   jax.ShapeDtypeStruct((2, NBLK, UW, 2 * BW), F32), jax.ShapeDtypeStruct((2, NBLK, UW, 2 * BW), F32)],
        scratch_shapes=[pltpu.VMEM((TA, 2 * BW), F32), pltpu.VMEM((TA, UW), F32), pltpu.VMEM((T, UW), F32),
                        pltpu.VMEM((TA, UW), F32), pltpu.VMEM((TA, UW), F32)],
        compiler_params=_cparams(("parallel", "arbitrary")),
    )(d_ge, ssm, proj, dskip, states, lam, bmat, cmat)


TR = 256
TN_WIDE = 1024


def _vjp_rows(f, primals, cots, n_row):
    _, pull = jax.vjp(f, *primals)
    g = pull(cots)
    return list(g[:n_row]), list(g[n_row:])


class _GradDict(dict):
    def __init__(self, on_set=None):
        super().__init__()
        self._on_set = on_set
        self.tokens = {}

    def __setitem__(self, key, value):
        super().__setitem__(key, value)
        if self._on_set is not None:
            self._on_set(self)

    def order(self, key):
        return self.tokens.get(key, self.get(key))

    def finish(self, key, after):
        if self.on_finish is None:
            return ()
        return (self.on_finish(key, after),)

    on_finish = None


def _local_step(x, ctx, tgt, mod_lat, mod_ctx, wb, sp, on_grad=None, on_loss=None, on_finish=None, on_early=None):
    sh1, sc1, g1, sh2, sc2, g2 = [mod_lat[:, i * D:(i + 1) * D] for i in range(6)]
    csh1, csc1 = mod_ctx[:, 0:D], mod_ctx[:, D:2 * D]
    tabs = _rope_tables()
    sink = sp["attn_sink"].reshape(1, NH)
    dskip = sp["ssm_d"].reshape(1, SW)
    lg_mix, lb_mix = sp["ln_mix_g"].reshape(1, D), sp["ln_mix_b"].reshape(1, D)
    lg_mlp, lb_mlp = sp["ln_mlp_g"].reshape(1, D), sp["ln_mlp_b"].reshape(1, D)
    b1, b2 = sp["b_mlp1"].reshape(1, DFF), sp["b_mlp2"].reshape(1, D)
    s5_names = ("ssm_a_re", "ssm_a_im", "ssm_log_dt", "ssm_b_re", "ssm_b_im", "ssm_c_re", "ssm_c_im")
    (lam, bmat, cmat), s5_pull = jax.vjp(_s5_prep, *[sp[n] for n in s5_names])

    def ln_mod2(rv, vv):
        h = _f_ln_mod(rv[0], vv[0], vv[1])
        return [h, h], []

    h_lat, h_lat_t = _rowwise(ln_mod2, [(x, D, 0, 0)], [sc1, sh1], [(D, BF16), (D, BF16, True)], [], nrows=T, tr=TR, name="ln1_lat")
    h_ctx, h_ctx_t = _rowwise(ln_mod2, [(ctx, D, 0, 0)], [csc1, csh1], [(D, BF16), (D, BF16, True)], [], nrows=C, tr=TR,
                              name="ln1_ctx")
    h1 = jnp.concatenate([h_lat, h_ctx], 0)
    h1_t = jnp.concatenate([h_lat_t, h_ctx_t], 1)
    proj = _matmul(h1, wb["w_in"], mode="nn", name="proj", tm=768, tn=TN_WIDE)
    attn = _attn_fwd(proj, sink, tabs)
    states, ssm, ge = _s5_fwd(proj, dskip, lam, bmat, cmat)
    z = _matmul(ge, wb["w_glu"], mode="nn", name="glu_mm", tm=1024, tn=1024, out_dtypes=(BF16,))

    def glu_act(rv, vv):
        return [_f_glu(rv[0])], []

    glu, = _rowwise(glu_act, [(z, 2 * SW, 0, 0)], [], [(SW, BF16)], [], nrows=T, tr=TR, name="glu_act")
    attn_d = _matmul(attn, wb["w_attn_up"], mode="nn", name="attn_up", tm=1024, tn=512, out_dtypes=(BF16,))
    ssm_d = _matmul(glu, wb["w_ssm_up"], mode="nn", name="ssm_up", tm=1024, tn=512, out_dtypes=(BF16,))
    ga_cb, gs_cb = (QW + 2 * KVW + SW) // D, (QW + 2 * KVW + SW) // D + 1

    def mix(rv, vv):
        m_ = _f_mix(*rv)
        return [m_, m_], []

    mixv, mix_t = _rowwise(mix, [(proj, D, ga_cb, 0), (proj, D, gs_cb, 0), (attn_d, D, 0, 0), (ssm_d, D, 0, 0)], [],
                           [(D, BF16), (D, BF16, True)], [], nrows=T, tr=TR, name="mix")
    y = _matmul(mixv, wb["w_out"], mode="nn", name="out_proj", tm=1024, tn=TN_WIDE, out_dtypes=(BF16,))

    def post1(rv, vv):
        x1, h2 = _f_post1(rv[0], rv[1], *vv)
        return [x1, h2, h2], []

    x1, h2, h2_t = _rowwise(post1, [(x, D, 0, 0), (y, D, 0, 0)], [g1, lg_mix, lb_mix, sc2, sh2],
                            [(D, F32), (D, BF16), (D, BF16, True)], [], nrows=T, tr=TR, name="post1")

    def relu_sq(acc):
        r = jnp.maximum(acc, 0.0)
        return r, r * r, r * r

    r_act, act, act_t = _matmul(h2, wb["w_mlp1"], mode="nn", name="mlp1", tm=1024, tn=TN_WIDE, bias=b1,
                                out_dtypes=(BF16, BF16, BF16), out_t=(False, False, True), epilogue=relu_sq)
    mlp = _matmul(act, wb["w_mlp2"], mode="nn", name="mlp2", tm=512, tn=512, out_dtypes=(BF16,))

    def loss_fb(rv, vv):
        x1_t, mlp_t, tgt_t = rv
        g2_v, lg_v, lb_v, b2_v = vv
        f = lambda a, m, g, p, q, b: _f_loss(a, m, tgt_t, g, p, q, b)
        val, grads = jax.value_and_grad(f, argnums=(0, 1, 2, 3, 4, 5))(x1_t, mlp_t, g2_v, lg_v, lb_v, b2_v)
        dx1, dmlp, dg2, dlg, dlb, db2 = grads
        return [dx1, dmlp], [jnp.reshape(val, (1, 1)), dg2, dlg, dlb, db2]

    dx1_a, d_mlp, loss_p, d_g2, d_lg_mlp, d_lb_mlp, d_b2 = _rowwise(
        loss_fb, [(x1, D, 0, 0), (mlp, D, 0, 0), (tgt, D, 0, 0)], [g2, lg_mlp, lb_mlp, b2],
        [(D, F32), (D, BF16)], [(1, 1), (1, D), (1, D), (1, D), (1, D)], nrows=T, tr=TR, name="loss_fb")

    gw = _GradDict(on_grad)
    gw.on_finish = on_finish
    loss_done = () if on_loss is None else (on_loss(loss_p),)
    gw["w_mlp2"] = _matmul(act_t, d_mlp, mode="nn", name="dw_mlp2", out_dtypes=(BF16,), tm=1024, tn=TN_WIDE, after=loss_done)
    da, = (_matmul(d_mlp, wb["w_mlp2"], mode="nt", name="d_act", out_dtypes=(BF16,), tm=1024, tn=TN_WIDE,
                   extras=(r_act,), epilogue=lambda acc, r: (acc * (2.0 * r.astype(F32)),), after=(gw.order("w_mlp2"),)),)
    pin = gw.finish("w_mlp2", da)
    ones = jnp.ones((8, T), BF16)
    d_b1 = _matmul(ones, da, mode="nn", name="db_mlp1", tm=8, tn=2048)[0:1]
    gw["w_mlp1"] = _matmul(h2_t, da, mode="nn", name="dw_mlp1", out_dtypes=(BF16,), tm=1024, tn=TN_WIDE, after=pin)
    dh2 = _matmul(da, wb["w_mlp1"], mode="nt", name="d_h2", tm=512, tn=512, out_dtypes=(BF16,), after=(gw.order("w_mlp1"),))

    def post1_b(rv, vv):
        x_t, y_t, dx1_t, dh2_t = rv
        gr, gv = _vjp_rows(_f_post1, (x_t, y_t, *vv), (dx1_t, dh2_t), 2)
        return [gr[0], gr[1]], gv

    dx_a, dy, d_g1, d_lg_mix, d_lb_mix, d_sc2, d_sh2 = _rowwise(
        post1_b, [(x, D, 0, 0), (y, D, 0, 0), (dx1_a, D, 0, 0), (dh2, D, 0, 0)], [g1, lg_mix, lb_mix, sc2, sh2],
        [(D, F32), (D, BF16)], [(1, D)] * 5, nrows=T, tr=TR, name="post1_bwd")
    gw["w_out"] = _matmul(mix_t, dy, mode="nn", name="dw_out", out_dtypes=(BF16,), tm=1024, tn=TN_WIDE)
    dmix = _matmul(dy, wb["w_out"], mode="nt", name="d_mix", tm=1024, tn=TN_WIDE, out_dtypes=(BF16,), after=(gw.order("w_out"),))

    def mix_b(rv, vv):
        gr, _ = _vjp_rows(_f_mix, tuple(rv[:4]), rv[4], 4)
        return gr, []

    d_ga, d_gs, d_attn_d, d_ssm_d = _rowwise(
        mix_b, [(proj, D, ga_cb, 0), (proj, D, gs_cb, 0), (attn_d, D, 0, 0), (ssm_d, D, 0, 0), (dmix, D, 0, 0)], [],
        [(D, BF16)] * 4, [], nrows=T, tr=TR, name="mix_bwd")
    pin = gw.finish("w_mlp1", d_ga)
    gw["w_attn_up"] = _matmul(attn, d_attn_d, mode="tn", name="dw_attn_up", out_dtypes=(BF16,), tm=512, tn=1024, tk=1024, after=pin)
    d_attn = _matmul(d_attn_d, wb["w_attn_up"], mode="nt", name="d_attn", out_dtypes=(BF16,), tm=1024, tn=512)
    gw["w_ssm_up"] = _matmul(glu, d_ssm_d, mode="tn", name="dw_ssm_up", out_dtypes=(BF16,), tm=512, tn=1024, tk=1024)
    d_glu = _matmul(d_ssm_d, wb["w_ssm_up"], mode="nt", name="d_glu", tm=1024, tn=512, after=(gw.order("w_attn_up"), gw.order("w_ssm_up")))

    def glu_b(rv, vv):
        gr, _ = _vjp_rows(_f_glu, (rv[0],), rv[1], 1)
        return gr, []

    dz, = _rowwise(glu_b, [(z, 2 * SW, 0, 0), (d_glu, SW, 0, 0)], [], [(2 * SW, BF16)], [], nrows=T, tr=TR, name="glu_bwd")
    gw["w_glu"] = _matmul(ge, dz, mode="tn", name="dw_glu", out_dtypes=(BF16,), tm=512, tn=1024, tk=1024)
    d_ge = _matmul(dz, wb["w_glu"], mode="nt", name="d_ge", tm=1024, tn=512, after=(gw.order("w_glu"),))

    du_all, d_dskip, dlam, dbmat, dcmat = _s5_bwd(d_ge, ssm, proj, dskip, states, lam, bmat, cmat)
    s5_grads = s5_pull((dlam, dbmat, dcmat))
    early = dict(zip(s5_names, s5_grads), ssm_d=d_dskip)
    if on_early is not None:
        on_early(early)
    pin = gw.finish("w_glu", du_all)

    dq, dk, dv, dsink = _attn_bwd(proj, d_attn, sink, tabs)
    zc = lambda w: jnp.zeros((C, w), BF16)
    dproj = jnp.concatenate([
        jnp.concatenate([dq, zc(QW)], 0), dk, dv, du_all,
        jnp.concatenate([d_ga, zc(D)], 0), jnp.concatenate([d_gs, zc(D)], 0)], 1)
    gw["w_in"] = _matmul(h1_t, dproj, mode="nn", name="dw_in", out_dtypes=(BF16,), tm=1024, tn=TN_WIDE, after=pin)
    pin = gw.finish("w_in", gw["w_in"])
    dh1 = _matmul(dproj, wb["w_in"], mode="nt", name="d_h1", tm=768, tn=512, out_dtypes=(BF16,), after=pin)

    def ln1_b(rv, vv):
        x_t, dh_t, dxa_t = rv
        gr, gv = _vjp_rows(_f_ln_mod, (x_t, vv[0], vv[1]), dh_t, 1)
        return [gr[0] + dxa_t], gv

    grad_x, d_sc1, d_sh1 = _rowwise(ln1_b, [(x, D, 0, 0), (dh1, D, 0, 0), (dx_a, D, 0, 0)], [sc1, sh1],
                                    [(D, F32)], [(1, D), (1, D)], nrows=T, tr=TR, name="ln1_lat_bwd")

    def ln1c_b(rv, vv):
        _, gv = _vjp_rows(_f_ln_mod, (rv[0], vv[0], vv[1]), rv[1], 1)
        return [], gv

    d_csc1, d_csh1 = _rowwise(ln1c_b, [(ctx, D, 0, 0), (dh1, D, 0, T // TR)], [csc1, csh1],
                              [], [(1, D), (1, D)], nrows=C, tr=TR, name="ln1_ctx_bwd")

    d_mod_lat = jnp.concatenate([d_sh1, d_sc1, d_g1, d_sh2, d_sc2, d_g2], 1)
    zv = jnp.zeros((1, D), F32)
    d_mod_ctx = jnp.concatenate([d_csh1, d_csc1, zv, zv, zv, zv], 1)
    gs = {n: g for n, g in zip(s5_names, s5_grads)}
    gs["attn_sink"] = dsink[:, 0]
    gs["ssm_d"] = d_dskip
    gs["ln_mix_g"], gs["ln_mix_b"] = d_lg_mix, d_lb_mix
    gs["ln_mlp_g"], gs["ln_mlp_b"] = d_lg_mlp, d_lb_mlp
    gs["b_mlp1"], gs["b_mlp2"] = d_b1, d_b2
    return loss_p, grad_x, d_mod_lat, d_mod_ctx, gw, gs


def _my_pos():
    return lax.axis_index("x"), lax.axis_index("y"), lax.axis_index("c")


def _flip(p, bit):
    return 1 - p if bit else p


def _peer(pos, k):
    x, y, c = pos
    return (_flip(x, (k >> 2) & 1), _flip(y, (k >> 1) & 1), _flip(c, k & 1))


def _lin(pos):
    return 4 * pos[0] + 2 * pos[1] + pos[2]


def _allgather_small(v, name):
    r, w = v.shape

    def body(v_ref, out_ref, send_sems, recv_sems, local_sem):
        me = _my_pos()
        mine = pltpu.make_async_copy(v_ref, out_ref.at[_lin(me)], local_sem)
        mine.start()
        sends = []
        for k in range(1, N_DEV):
            cp = pltpu.make_async_remote_copy(src_ref=v_ref, dst_ref=out_ref.at[_lin(me)], send_sem=send_sems.at[k - 1],
                                              recv_sem=recv_sems.at[k - 1], device_id=_peer(me, k), device_id_type=MESH)
            cp.start()
            sends.append(cp)
        for k in range(1, N_DEV):
            peer = _peer(me, k)
            pltpu.make_async_remote_copy(src_ref=v_ref, dst_ref=out_ref.at[_lin(peer)], send_sem=send_sems.at[k - 1],
                                         recv_sem=recv_sems.at[k - 1], device_id=peer, device_id_type=MESH).wait_recv()
        for cp in sends:
            cp.wait_send()
        mine.wait()

    return pl.pallas_call(
        body,
        name=name,
        out_shape=jax.ShapeDtypeStruct((N_DEV, r, w), v.dtype),
        in_specs=[pl.BlockSpec(memory_space=pltpu.VMEM)],
        out_specs=pl.BlockSpec(memory_space=pltpu.VMEM),
        scratch_shapes=[pltpu.SemaphoreType.DMA((N_DEV - 1,)), pltpu.SemaphoreType.DMA((N_DEV - 1,)), pltpu.SemaphoreType.DMA],
        compiler_params=pltpu.CompilerParams(vmem_limit_bytes=VMEM_LIMIT_BYTES),
    )(v)


def _block_of(ref, kind, idx, n):
    start = pl.multiple_of(idx * n, 128)
    if kind == "col":
        return ref.at[:, pl.ds(start, n)]
    return ref.at[pl.ds(start, n), :]


def _handshake(peers):
    barrier = pltpu.get_barrier_semaphore()
    for peer in peers:
        pl.semaphore_signal(barrier, inc=1, device_id=peer, device_id_type=MESH)
    pl.semaphore_wait(barrier, len(peers))


def _allgather_weights_seq(shards, kinds, name, collective_id):
    nt = len(shards)
    hbm = pltpu.MemorySpace.HBM
    ins = [jax.new_ref(s, memory_space=hbm) for s in shards]
    outs = []
    for s, kind in zip(shards, kinds):
        k, n = s.shape
        shape = (k, n * N_DEV) if kind == "col" else (k * N_DEV, n)
        outs.append(jax.empty_ref(jax.ShapeDtypeStruct(shape, s.dtype), memory_space=hbm))

    @functools.partial(
        pl.kernel, mesh=plsc.ScalarSubcoreMesh(axis_name="seq", num_cores=1), name=name,
        scratch_types=(pltpu.SemaphoreType.DMA((nt, N_DEV - 1)), pltpu.SemaphoreType.DMA((nt, N_DEV - 1)),
                       pltpu.SemaphoreType.DMA((nt,))),
        compiler_params=pltpu.CompilerParams(collective_id=collective_id))
    def launch(send_sems, recv_sems, local_sems):
        x, y, c = _my_pos()
        me, sibling = (x, y, c), (x, y, 1 - c)
        chips = [(1 - x, y), (x, 1 - y), (1 - x, 1 - y)]
        _handshake([sibling] + [(*chip, c) for chip in chips])

        def blk(t, pos):
            n = shards[t].shape[1] if kinds[t] == "col" else shards[t].shape[0]
            return _block_of(outs[t], kinds[t], _lin(pos), n)

        def copy(t, k, block, to, src=None):
            return pltpu.make_async_remote_copy(src_ref=blk(t, block) if src is None else src, dst_ref=blk(t, block),
                                                send_sem=send_sems.at[t, k], recv_sem=recv_sems.at[t, k],
                                                device_id=to, device_id_type=MESH)

        local, sends = [], []
        for t in range(nt):
            mine = pltpu.make_async_copy(ins[t], blk(t, me), local_sems.at[t])
            mine.start()
            local.append(mine)
            first = [copy(t, 0, me, sibling, src=ins[t])]
            first += [copy(t, 1 + j, me, (*chip, c), src=ins[t]) for j, chip in enumerate(chips)]
            for cp in first:
                cp.start()
            sends += first
        for t in range(nt):
            for j, chip in enumerate(chips):
                copy(t, 1 + j, (*chip, c), me).wait_recv()
                fwd = copy(t, 4 + j, (*chip, c), sibling)
                fwd.start()
                sends.append(fwd)
        for t in range(nt):
            copy(t, 0, sibling, me).wait_recv()
            for j, chip in enumerate(chips):
                copy(t, 4 + j, (*chip, 1 - c), me).wait_recv()
        for cp in sends:
            cp.wait_send()
        for cp in local:
            cp.wait()

    launch()
    return [o[...] for o in outs]


def _allgather_small_seq(v, name, collective_id):
    hbm = pltpu.MemorySpace.HBM
    src = jax.new_ref(v, memory_space=hbm)
    out = jax.empty_ref(jax.ShapeDtypeStruct((N_DEV,) + v.shape, v.dtype), memory_space=hbm)

    @functools.partial(
        pl.kernel, mesh=plsc.ScalarSubcoreMesh(axis_name="seq", num_cores=1), name=name,
        scratch_types=(pltpu.SemaphoreType.DMA((N_DEV - 1,)), pltpu.SemaphoreType.DMA((N_DEV - 1,)), pltpu.SemaphoreType.DMA),
        compiler_params=pltpu.CompilerParams(collective_id=collective_id))
    def launch(send_sems, recv_sems, local_sem):
        me = _my_pos()
        _handshake([_peer(me, k) for k in range(1, N_DEV)])
        mine = pltpu.make_async_copy(src, out.at[_lin(me)], local_sem)
        mine.start()
        sends = []
        for k in range(1, N_DEV):
            cp = pltpu.make_async_remote_copy(src_ref=src, dst_ref=out.at[_lin(me)], send_sem=send_sems.at[k - 1],
                                              recv_sem=recv_sems.at[k - 1], device_id=_peer(me, k), device_id_type=MESH)
            cp.start()
            sends.append(cp)
        for k in range(1, N_DEV):
            peer = _peer(me, k)
            pltpu.make_async_remote_copy(src_ref=src, dst_ref=out.at[_lin(peer)], send_sem=send_sems.at[k - 1],
                                         recv_sem=recv_sems.at[k - 1], device_id=peer, device_id_type=MESH).wait_recv()
        for cp in sends:
            cp.wait_send()
        mine.wait()

    launch()
    return out[...]


N_CHIP = N_DEV // 2


def _chip_of(pos):
    return 2 * pos[0] + pos[1]


def _pair_exchange_seq(grads, kinds, name, collective_id):
    nt = len(grads)
    hbm = pltpu.MemorySpace.HBM
    shard_shapes = _shard_shapes(grads, kinds)
    ins = [jax.new_ref(g, memory_space=hbm) for g in grads]
    outs = [jax.empty_ref(jax.ShapeDtypeStruct((N_CHIP,) + s, g.dtype), memory_space=hbm) for s, g in zip(shard_shapes, grads)]

    @functools.partial(
        pl.kernel, mesh=plsc.ScalarSubcoreMesh(axis_name="seq", num_cores=1), name=name,
        scratch_types=(pltpu.SemaphoreType.DMA((nt, N_CHIP)), pltpu.SemaphoreType.DMA((nt, N_CHIP))),
        compiler_params=pltpu.CompilerParams(collective_id=collective_id))
    def launch(send_sems, recv_sems):
        x, y, c = _my_pos()
        sibling = (x, y, 1 - c)
        _handshake([sibling])
        copies = []
        for t in range(nt):
            n = shard_shapes[t][1] if kinds[t] == "col" else shard_shapes[t][0]
            for q in range(N_CHIP):
                cp = pltpu.make_async_remote_copy(src_ref=_block_of(ins[t], kinds[t], 2 * q + (1 - c), n), dst_ref=outs[t].at[q],
                                                  send_sem=send_sems.at[t, q], recv_sem=recv_sems.at[t, q],
                                                  device_id=sibling, device_id_type=MESH)
                cp.start()
                copies.append(cp)
        for cp in copies:
            cp.wait_recv()
        for cp in copies:
            cp.wait_send()

    launch()
    return [o[...] for o in outs]


def _pair_add(g, half, kind, name, after=()):
    nq, k, ns = half.shape
    tr = min(k, 512)
    c_idx = lax.axis_index("c").astype(jnp.int32).reshape(1)
    if kind == "col":
        g_spec = pl.BlockSpec((tr, ns), lambda q, i, c_ref: (i, 2 * q + c_ref[0]))
    else:
        g_spec = pl.BlockSpec((tr, ns), lambda q, i, c_ref: ((2 * q + c_ref[0]) * (k // tr) + i, 0))
    n_after = len(after)

    def kern(c_ref, g_ref, h_ref, *rest):
        o_ref = rest[n_after]
        o_ref[0] = (g_ref[...].astype(F32) + h_ref[0].astype(F32)).astype(o_ref.dtype)

    return pl.pallas_call(
        kern,
        name=name,
        grid_spec=pltpu.PrefetchScalarGridSpec(
            num_scalar_prefetch=1,
            grid=(nq, k // tr),
            in_specs=[g_spec, pl.BlockSpec((1, tr, ns), lambda q, i, c_ref: (q, i, 0))] + [pl.BlockSpec(memory_space=pl.ANY)] * n_after,
            out_specs=pl.BlockSpec((1, tr, ns), lambda q, i, c_ref: (q, i, 0)),
        ),
        out_shape=jax.ShapeDtypeStruct(half.shape, half.dtype),
        compiler_params=_cparams(("parallel", "parallel")),
    )(c_idx, g, half, *after)


def _chip_exchange_seq(psums, name, collective_id):
    nt = len(psums)
    hbm = pltpu.MemorySpace.HBM
    ins = [jax.new_ref(s, memory_space=hbm) for s in psums]
    outs = [jax.empty_ref(jax.ShapeDtypeStruct(s.shape, s.dtype), memory_space=hbm) for s in psums]

    @functools.partial(
        pl.kernel, mesh=plsc.ScalarSubcoreMesh(axis_name="seq", num_cores=1), name=name,
        scratch_types=(pltpu.SemaphoreType.DMA((nt, N_CHIP - 1)), pltpu.SemaphoreType.DMA((nt, N_CHIP - 1)),
                       pltpu.SemaphoreType.DMA((nt,))),
        compiler_params=pltpu.CompilerParams(collective_id=collective_id))
    def launch(send_sems, recv_sems, local_sems):
        me = _my_pos()
        peers = [_peer(me, k) for k in (2, 4, 6)]
        _handshake(peers)
        mine = _chip_of(me)
        local, sends = [], []
        for t in range(nt):
            cp = pltpu.make_async_copy(ins[t].at[mine], outs[t].at[mine], local_sems.at[t])
            cp.start()
            local.append(cp)
            for j, peer in enumerate(peers):
                cp = pltpu.make_async_remote_copy(src_ref=ins[t].at[_chip_of(peer)], dst_ref=outs[t].at[mine],
                                                  send_sem=send_sems.at[t, j], recv_sem=recv_sems.at[t, j],
                                                  device_id=peer, device_id_type=MESH)
                cp.start()
                sends.append(cp)
        for t in range(nt):
            for j, peer in enumerate(peers):
                pltpu.make_async_remote_copy(src_ref=ins[t].at[mine], dst_ref=outs[t].at[_chip_of(peer)],
                                             send_sem=send_sems.at[t, j], recv_sem=recv_sems.at[t, j],
                                             device_id=peer, device_id_type=MESH).wait_recv()
        for cp in sends:
            cp.wait_send()
        for cp in local:
            cp.wait()

    launch()
    return [o[...] for o in outs]


def _shard_shapes(grads, kinds):
    return [(g.shape[0], g.shape[1] // N_DEV) if kind == "col" else (g.shape[0] // N_DEV, g.shape[1]) for g, kind in zip(grads, kinds)]


def _adam(g_slots, w, m, v, *, tr, name, after=()):
    ns, r, wd = g_slots.shape
    tr = min(tr, r)
    assert r % tr == 0, (name, r, tr)
    n_after = len(after)

    def kern(g_ref, w_ref, m_ref, v_ref, *rest):
        go_ref, d_ref, mo_ref, vo_ref = rest[n_after:]
        g = g_ref[0].astype(F32)
        for s in range(1, ns):
            g = g + g_ref[s].astype(F32)
        delta, m_new, v_new = _adam_update(g, w_ref[...], m_ref[...], v_ref[...])
        go_ref[...] = g
        d_ref[...] = delta
        mo_ref[...] = m_new
        vo_ref[...] = v_new

    tile = pl.BlockSpec((tr, wd), lambda i: (i, 0))
    return pl.pallas_call(
        kern,
        name=name,
        grid=(r // tr,),
        in_specs=[pl.BlockSpec((ns, tr, wd), lambda i: (0, i, 0)), tile, tile, tile] + [pl.BlockSpec(memory_space=pl.ANY)] * n_after,
        out_specs=[tile] * 4,
        out_shape=[jax.ShapeDtypeStruct((r, wd), F32)] * 4,
        compiler_params=_cparams(("parallel",)),
    )(g_slots, w, m, v, *after)


def _adam_update(g, w, m, v):
    m_new = ADAM_B1 * m + (1.0 - ADAM_B1) * g
    v_new = ADAM_B2 * v + (1.0 - ADAM_B2) * (g * g)
    m_hat = m_new / (1.0 - ADAM_B1 ** ADAM_STEP)
    v_hat = v_new / (1.0 - ADAM_B2 ** ADAM_STEP)
    return -ADAM_LR * (m_hat / (jnp.sqrt(v_hat) + ADAM_EPS) + ADAM_WD * w), m_new, v_new


def _lane_offsets(sizes):
    offs, o = [], 0
    for n in sizes:
        offs.append(o)
        o += -(-n // LANES) * LANES
    return offs, o


def _pack_lanes(parts):
    cols = []
    for p_ in parts:
        flat = p_.reshape(1, -1).astype(F32)
        cols.append(jnp.pad(flat, ((0, 0), (0, (-flat.shape[1]) % LANES))))
    return jnp.concatenate(cols, 1)


def _adam_lanes(g_slots, ws, ms, vs, *, name, after=()):
    ns = g_slots.shape[0]
    npar, n_after = len(ws), len(after)
    sizes = [w.shape[1] for w in ws]
    offs, _ = _lane_offsets(sizes)

    def kern(g_ref, *refs):
        w_refs, m_refs, v_refs = refs[:npar], refs[npar:2 * npar], refs[2 * npar:3 * npar]
        outs = refs[3 * npar + n_after:]
        g_all = g_ref[0]
        for s in range(1, ns):
            g_all = g_all + g_ref[s]
        for j in range(npar):
            g = g_all[:, offs[j]:offs[j] + sizes[j]]
            delta, m_new, v_new = _adam_update(g, w_refs[j][...], m_refs[j][...], v_refs[j][...])
            outs[4 * j][...] = g
            outs[4 * j + 1][...] = delta
            outs[4 * j + 2][...] = m_new
            outs[4 * j + 3][...] = v_new

    vmem = pl.BlockSpec(memory_space=pltpu.VMEM)
    res = pl.pallas_call(
        kern,
        name=name,
        in_specs=[vmem] * (1 + 3 * npar) + [pl.BlockSpec(memory_space=pl.ANY)] * n_after,
        out_specs=[vmem] * (4 * npar),
        out_shape=[jax.ShapeDtypeStruct((1, n), F32) for n in sizes for _ in range(4)],
        compiler_params=pltpu.CompilerParams(vmem_limit_bytes=VMEM_LIMIT_BYTES),
    )(g_slots, *ws, *ms, *vs, *after)
    return [tuple(res[4 * j:4 * j + 4]) for j in range(npar)]


SMALL = ("c_ctx", "b_ada", "attn_sink", "ssm_a_re", "ssm_a_im", "ssm_log_dt", "ssm_b_re", "ssm_b_im", "ssm_c_re", "ssm_c_im",
         "ssm_d", "ln_mix_g", "ln_mix_b", "b_mlp1", "b_mlp2", "ln_mlp_g", "ln_mlp_b")
BIG = ("w_in", "w_glu", "w_attn_up", "w_ssm_up", "w_out", "w_mlp1", "w_mlp2")
BIG_KIND = ("col", "col", "col", "col", "row", "col", "row")
AG_GROUPS = (("w_in",), ("w_glu", "w_attn_up", "w_ssm_up", "w_out"), ("w_mlp1",), ("w_mlp2",))
AG_COLLECTIVE_ID0 = 1
RS_GROUPS = (("w_mlp2",), ("w_mlp1",), ("w_out", "w_attn_up", "w_ssm_up", "w_glu"), ("w_in",))
RS_COLLECTIVE_ID0 = AG_COLLECTIVE_ID0 + len(AG_GROUPS)
SMALL_EARLY = ("ssm_a_re", "ssm_a_im", "ssm_log_dt", "ssm_b_re", "ssm_b_im", "ssm_c_re", "ssm_c_im", "ssm_d")
SMALL_LATE = tuple(n for n in SMALL if n not in SMALL_EARLY)
SMALL_COLLECTIVE_ID0 = RS_COLLECTIVE_ID0 + 2 * len(RS_GROUPS)
LANES = 128


def _pack(parts):
    rows = []
    for p in parts:
        flat = p.reshape(-1).astype(F32)
        pad = (-flat.shape[0]) % LANES
        rows.append(jnp.pad(flat, (0, pad)).reshape(-1, LANES))
    packed = jnp.concatenate(rows, 0)
    return jnp.pad(packed, ((0, (-packed.shape[0]) % 8), (0, 0)))


def _unpack(packed, shapes):
    out, r0 = [], 0
    for s in shapes:
        n = math.prod(s)
        nr = -(-n // LANES)
        out.append(packed[r0:r0 + nr].reshape(-1)[:n].reshape(s))
        r0 += nr
    return out


WEIGHTS = ("c_ctx", "w_ada", "b_ada", "w_in", "attn_sink", "ssm_a_re", "ssm_a_im", "ssm_log_dt", "ssm_b_re", "ssm_b_im",
           "ssm_c_re", "ssm_c_im", "ssm_d", "w_glu", "w_attn_up", "w_ssm_up", "w_out", "ln_mix_g", "ln_mix_b", "w_mlp1",
           "b_mlp1", "w_mlp2", "b_mlp2", "ln_mlp_g", "ln_mlp_b")
ADA_COLS = 6 * D // N_DEV


def _step(x, c, ctx, loss_target, p, m, v):
    me = _lin(_my_pos())
    x2, ctx2, tgt2 = x[0], ctx[0], loss_target[0]

    wb = {}
    for gi, group in enumerate(AG_GROUPS):
        full = _allgather_weights_seq([p[n][0].astype(BF16) for n in group], [BIG_KIND[BIG.index(n)] for n in group],
                                      "allgather_seq%d" % gi, AG_COLLECTIVE_ID0 + gi)
        wb.update(zip(group, full))

    c_all = _allgather_small(jnp.broadcast_to(c, (8, D)), "gather_c")[:, 0, :]
    cc = p["c_ctx"].reshape(1, D)
    s_in = jnp.concatenate([c_all, cc, jnp.zeros((7, D), F32)], 0)
    s_act, = _rowwise(lambda rv, vv: ([_silu(rv[0])], []), [(s_in, D, 0, 0)], [], [(D, F32)], [], nrows=16, tr=16, name="silu_c")
    b_mine = lax.dynamic_slice_in_dim(p["b_ada"], me * ADA_COLS, ADA_COLS, axis=1)
    mod_part = _matmul(s_act, p["w_ada"][0], mode="nn", name="ada_fwd", tm=16, tn=512, bias=b_mine)
    mod_all = _allgather_small(mod_part, "gather_mod")
    mod_lat = lax.dynamic_index_in_dim(mod_all, me, axis=1, keepdims=False).reshape(1, 6 * D)
    mod_ctx = mod_all[:, 8, :].reshape(1, 6 * D)

    sp = {n: p[n][0] for n in SMALL if n not in ("c_ctx", "b_ada")}
    recv, halves = {}, {}

    def on_grad(gw):
        for gi, group in enumerate(RS_GROUPS):
            if gi not in halves and all(n in gw for n in group):
                kinds = [BIG_KIND[BIG.index(n)] for n in group]
                halves[gi] = (dict(gw), _pair_exchange_seq([gw[n] for n in group], kinds, "pair_exchange%d" % gi, RS_COLLECTIVE_ID0 + 2 * gi))

    def on_finish(key, after):
        gi = [i for i, group in enumerate(RS_GROUPS) if key in group][0]
        group = RS_GROUPS[gi]
        grads, half = halves[gi]
        prev = tuple(recv[n] for n in RS_GROUPS[gi - 1][:1]) if gi else ()
        if gi == len(RS_GROUPS) - 1:
            prev += (small["early"],)
        psums =[_pair_add(grads[n], h, BIG_KIND[BIG.index(n)], "pair_add_" + n, after=(after,) + prev) for n, h in zip(group, half)]
        recv.update(zip(group, _chip_exchange_seq(psums, "chip_exchange%d" % gi, RS_COLLECTIVE_ID0 + 2 * gi + 1)))
        return psums[-1]

    small = {}

    def on_early(gs_early):
        small["early"] = _allgather_small_seq(_pack([gs_early[n] for n in SMALL_EARLY]), "gather_small_early", SMALL_COLLECTIVE_ID0)

    total = {}

    def on_loss(loss_p):
        total["loss"] = lax.psum(loss_p[0, 0], ("x", "y", "c"))
        return total["loss"].reshape(1, 1)

    loss_p, grad_x, d_mod_lat, d_mod_ctx, gw, gs = _local_step(x2, ctx2, tgt2, mod_lat, mod_ctx, wb, sp, on_grad, on_loss, on_finish, on_early)

    g_early = small["early"]
    res = {}
    last = ()

    def adam_small(names, g_pack, tag, after):
        sm = _adam(g_pack, _pack([p[n] for n in names]), _pack([m[n] for n in names]), _pack([v[n] for n in names]),
                   tr=g_pack.shape[1], name="adam_small_" + tag, after=after)
        shapes = [p[n].shape for n in names]
        for j, outs in enumerate(zip(*[_unpack(a, shapes) for a in sm])):
            res[names[j]] = outs
        return (sm[0],)

    for gi, group in enumerate(RS_GROUPS):
        if gi == len(RS_GROUPS) - 1:
            last = adam_small(SMALL_EARLY, g_early, "early", last)
        for n in group:
            res[n] = _adam(recv[n], p[n][0], m[n][0], v[n][0], tr=256, name="adam_" + n, after=last)
            last = (res[n][0],)

    dm = jnp.concatenate([d_mod_lat, d_mod_ctx, jnp.zeros((6, 6 * D), F32)], 0)
    dm_all = _allgather_small_seq(dm, "gather_dmod", SMALL_COLLECTIVE_ID0 + 1)
    dm_all = lax.optimization_barrier((dm_all,) + last)[0]
    dm2 = jnp.concatenate([dm_all[:, 0, :], dm_all[:, 1, :]], 0)
    dm2_mine = lax.dynamic_slice_in_dim(dm2, me * ADA_COLS, ADA_COLS, axis=1)
    s2 = jnp.concatenate([s_act[0:8], jnp.broadcast_to(s_act[8:9], (8, D))], 0)
    g_w_ada = _matmul(s2, dm2_mine, mode="tn", name="dw_ada", tm=512, tn=ADA_COLS, after=last)
    dsc_part = _matmul(dm2_mine[8:16], p["w_ada"][0], mode="nt", name="d_silu_cctx", tm=8, tn=512, after=last)

    def cctx_b(rv, vv):
        _, pull = jax.vjp(_silu, vv[0])
        return [], [pull(jnp.sum(rv[0], axis=0, keepdims=True))[0]]

    g_cctx, = _rowwise(cctx_b, [(dsc_part, D, 0, 0)], [cc], [], [(1, D)], nrows=8, tr=8, name="cctx_bwd")
    gs["c_ctx"] = g_cctx
    gs["b_ada"] = d_mod_lat + d_mod_ctx

    res["w_ada"] = _adam(g_w_ada[None], p["w_ada"][0], m["w_ada"][0], v["w_ada"][0], tr=256, name="adam_w_ada")

    g_late = _allgather_small_seq(_pack_lanes([gs[n] for n in SMALL_LATE]), "gather_small_late", SMALL_COLLECTIVE_ID0 + 2)
    row = lambda a: a.reshape(1, -1)
    late = _adam_lanes(g_late, [row(p[n]) for n in SMALL_LATE], [row(m[n]) for n in SMALL_LATE], [row(v[n]) for n in SMALL_LATE],
                       name="adam_small_late", after=(res["w_ada"][0],))
    res.update(zip(SMALL_LATE, late))

    outs = [total["loss"], grad_x[None]]
    for j in range(4):
        outs += [res[n][j].reshape(p[n].shape) for n in WEIGHTS]
    return tuple(outs)


def kernel(x, c, ctx, c_ctx, w_ada, b_ada, w_in, attn_sink, ssm_a_re, ssm_a_im, ssm_log_dt, ssm_b_re, ssm_b_im, ssm_c_re, ssm_c_im, ssm_d, w_glu, w_attn_up, w_ssm_up, w_out, ln_mix_g, ln_mix_b, w_mlp1, b_mlp1, w_mlp2, b_mlp2, ln_mlp_g, ln_mlp_b, loss_target, m_c_ctx, m_w_ada, m_b_ada, m_w_in, m_attn_sink, m_ssm_a_re, m_ssm_a_im, m_ssm_log_dt, m_ssm_b_re, m_ssm_b_im, m_ssm_c_re, m_ssm_c_im, m_ssm_d, m_w_glu, m_w_attn_up, m_w_ssm_up, m_w_out, m_ln_mix_g, m_ln_mix_b, m_w_mlp1, m_b_mlp1, m_w_mlp2, m_b_mlp2, m_ln_mlp_g, m_ln_mlp_b, v_c_ctx, v_w_ada, v_b_ada, v_w_in, v_attn_sink, v_ssm_a_re, v_ssm_a_im, v_ssm_log_dt, v_ssm_b_re, v_ssm_b_im, v_ssm_c_re, v_ssm_c_im, v_ssm_d, v_w_glu, v_w_attn_up, v_w_ssm_up, v_w_out, v_ln_mix_g, v_ln_mix_b, v_w_mlp1, v_b_mlp1, v_w_mlp2, v_b_mlp2, v_ln_mlp_g, v_ln_mlp_b):
    given = dict(locals())
    p = {n: given[n] for n in WEIGHTS}
    m = {n: given["m_" + n] for n in WEIGHTS}
    v = {n: given["v_" + n] for n in WEIGHTS}
    return _step(x, c, ctx, loss_target, p, m, v)
```

```python
import functools
import math

import jax
import jax.numpy as jnp
from jax import lax
from jax.experimental import pallas as pl
from jax.experimental.pallas import tpu as pltpu
from jax.experimental.pallas import tpu_sc as plsc

F32 = jnp.float32
BF16 = jnp.bfloat16

N_DEV = 8
D = 2048
T = 2048
C = 256
TA = T + C
GRID_W = 64
HD = 128
NH = 8
NKV = 2
GROUP = NH // NKV
WINDOW = 128
QW = NH * HD
KVW = NKV * HD
SW = D // 4
SG = 16
NG = SW // SG
SP = 64
DFF = 4 * D
IN_COLS = QW + 2 * KVW + SW + 2 * D
ALPHA = 2.0 ** 0.25
LN_EPS = 1e-6
NEG_INF = -1e30
ROPE_BASE = 10000.0
ATT_SCALE = HD ** -0.5

NSEG = 8
GBLK = 8
NBLK = NG // GBLK
BW = GBLK * SP
UW = GBLK * SG

ADAM_LR = 0.001
ADAM_B1 = 0.9
ADAM_B2 = 0.999
ADAM_EPS = 1e-08
ADAM_WD = 0.01
ADAM_STEP = 10

VMEM_LIMIT_BYTES = 56 * 1024 * 1024
MESH = pl.DeviceIdType.MESH


def _cparams(sem=None):
    return pltpu.CompilerParams(dimension_semantics=sem, vmem_limit_bytes=VMEM_LIMIT_BYTES)


def _matmul(a, b, *, mode, name, out_dtypes=(F32,), tm=512, tn=512, tk=None, bias=None, extras=(), epilogue=None, after=(),
            out_t=None):
    if mode == "nn":
        (M, K), (K2, N) = a.shape, b.shape
    elif mode == "nt":
        (M, K), (N, K2) = a.shape, b.shape
    else:
        (K, M), (K2, N) = a.shape, b.shape
    assert K == K2, (name, a.shape, b.shape)
    tm, tn, tk = min(tm, M), min(tn, N), min(tk or K, K)
    assert M % tm == 0 and N % tn == 0 and K % tk == 0, (name, M, N, K, tm, tn, tk)
    nk = K // tk
    if mode == "tn":
        a_spec = pl.BlockSpec((tk, tm), lambda i, j, k: (k, i))
    else:
        a_spec = pl.BlockSpec((tm, tk), lambda i, j, k: (i, k))
    if mode == "nt":
        b_spec = pl.BlockSpec((tn, tk), lambda i, j, k: (j, k))
    else:
        b_spec = pl.BlockSpec((tk, tn), lambda i, j, k: (k, j))
    dims = {"nn": (((1,), (0,)), ((), ())), "nt": (((1,), (1,)), ((), ())), "tn": (((0,), (0,)), ((), ()))}[mode]
    in_specs = [a_spec, b_spec]
    operands = [a, b]
    if bias is not None:
        in_specs.append(pl.BlockSpec((1, tn), lambda i, j, k: (0, j)))
        operands.append(bias)
    for e in extras:
        in_specs.append(pl.BlockSpec((tm, tn), lambda i, j, k: (i, j)))
        operands.append(e)
    n_ex = len(extras)
    for t in after:
        in_specs.append(pl.BlockSpec(memory_space=pl.ANY))
        operands.append(t)
    n_after = len(after)
    n_out = len(out_dtypes)
    out_t = tuple(out_t) if out_t is not None else (False,) * n_out
    has_bias = bias is not None

    def kern(*refs):
        a_ref, b_ref = refs[0], refs[1]
        pos = 2
        bias_ref = None
        if has_bias:
            bias_ref = refs[pos]
            pos += 1
        ex_refs = refs[pos:pos + n_ex]
        pos += n_ex + n_after
        out_refs = refs[pos:pos + n_out]
        acc_ref = refs[pos + n_out] if nk > 1 else None

        def finish(r):
            if has_bias:
                r = r + bias_ref[...]
            outs = epilogue(r, *[e[...] for e in ex_refs]) if epilogue is not None else (r,)
            for o_ref, o, tr_ in zip(out_refs, outs, out_t):
                o_ref[...] = (o.T if tr_ else o).astype(o_ref.dtype)

        part = lax.dot_general(a_ref[...].astype(BF16), b_ref[...].astype(BF16), dims, preferred_element_type=F32)
        if nk == 1:
            finish(part)
        else:
            k = pl.program_id(2)

            @pl.when(k == 0)
            def _():
                acc_ref[...] = part

            @pl.when(k > 0)
            def _():
                acc_ref[...] += part

            @pl.when(k == nk - 1)
            def _():
                finish(acc_ref[...])

    outs = pl.pallas_call(
        kern,
        name=name,
        grid=(M // tm, N // tn, nk),
        in_specs=in_specs,
        out_specs=[pl.BlockSpec((tn, tm), lambda i, j, k: (j, i)) if tr_ else pl.BlockSpec((tm, tn), lambda i, j, k: (i, j))
                   for tr_ in out_t],
        out_shape=[jax.ShapeDtypeStruct((N, M) if tr_ else (M, N), dt) for dt, tr_ in zip(out_dtypes, out_t)],
        scratch_shapes=[pltpu.VMEM((tm, tn), F32)] if nk > 1 else [],
        compiler_params=_cparams(("parallel", "parallel", "arbitrary")),
    )(*operands)
    return outs[0] if n_out == 1 else tuple(outs)


def _rowwise(fn, rows, vecs, outs, vec_outs, *, nrows, tr, name, after=()):
    n_rows, n_vecs, n_outs, n_after = len(rows), len(vecs), len(outs), len(after)
    in_specs = [pl.BlockSpec((tr, w), lambda i, cb=cb, ro=ro: (i + ro, cb)) for (_, w, cb, ro) in rows]
    in_specs += [pl.BlockSpec(v.shape, lambda i: (0, 0)) for v in vecs]
    in_specs += [pl.BlockSpec(memory_space=pl.ANY)] * n_after
    outs = [o if len(o) == 3 else (*o, False) for o in outs]
    out_specs = [pl.BlockSpec((w, tr), lambda i: (0, i)) if tr_ else pl.BlockSpec((tr, w), lambda i: (i, 0)) for (w, _, tr_) in outs]
    out_specs += [pl.BlockSpec(s, lambda i: (0, 0)) for s in vec_outs]
    out_shape = [jax.ShapeDtypeStruct((w, nrows) if tr_ else (nrows, w), dt) for (w, dt, tr_) in outs]
    out_tr = [tr_ for (_, _, tr_) in outs]
    out_shape += [jax.ShapeDtypeStruct(s, F32) for s in vec_outs]

    def kern(*refs):
        rvals = [r[...].astype(F32) for r in refs[:n_rows]]
        vvals = [r[...] for r in refs[n_rows:n_rows + n_vecs]]
        first_out = n_rows + n_vecs + n_after
        o_refs = refs[first_out:first_out + n_outs]
        v_refs = refs[first_out + n_outs:]
        ro, vo = fn(rvals, vvals)
        for r, val, tr_ in zip(o_refs, ro, out_tr):
            r[...] = (val.astype(F32).T if tr_ else val).astype(r.dtype)
        i = pl.program_id(0)
        for r, val in zip(v_refs, vo):
            @pl.when(i == 0)
            def _(r=r, val=val):
                r[...] = val.astype(F32)

            @pl.when(i > 0)
            def _(r=r, val=val):
                r[...] += val.astype(F32)

    res = pl.pallas_call(
        kern,
        name=name,
        grid=(nrows // tr,),
        in_specs=in_specs,
        out_specs=out_specs,
        out_shape=out_shape,
        compiler_params=_cparams(("arbitrary",)),
    )(*[r[0] for r in rows], *vecs, *after)
    return list(res)


def _ln(x):
    mu = jnp.mean(x, axis=-1, keepdims=True)
    xc = x - mu
    var = jnp.mean(xc * xc, axis=-1, keepdims=True)
    return xc * lax.rsqrt(var + LN_EPS)


def _sigmoid(x):
    return 1.0 / (1.0 + jnp.exp(-x))


def _gelu(x):
    return 0.5 * x * (1.0 + jnp.tanh(math.sqrt(2.0 / math.pi) * (x + 0.044715 * (x * x * x))))


def _silu(x):
    return x * _sigmoid(x)


def _f_ln_mod(x, sc, sh):
    return _ln(x) * (1.0 + sc) + sh


def _f_glu(z):
    return z[:, :SW] * _sigmoid(z[:, SW:])


def _f_mix(ga, gs, attn_d, ssm_d):
    return _sigmoid(ga) * attn_d + _sigmoid(gs) * ssm_d


def _f_post1(x, y, g1, lg, lb, sc2, sh2):
    r1 = ALPHA * x + g1 * y
    x1 = _ln(r1) * lg + lb
    h2 = _ln(x1) * (1.0 + sc2) + sh2
    return x1, h2


def _f_loss(x1, mlp, tgt, g2, lg, lb, b2z):
    r2 = ALPHA * x1 + g2 * (mlp + b2z)
    out = _ln(r2) * lg + lb
    err = out - tgt
    return 0.5 * jnp.sum(err * err) * (1.0 / D)


def _rope_tables():
    rows = T // GRID_W
    row = jnp.repeat(jnp.arange(rows), GRID_W)
    col = jnp.tile(jnp.arange(GRID_W), rows)
    n_freq = HD // 4
    freqs = ROPE_BASE ** (-jnp.arange(n_freq, dtype=F32) / n_freq)
    ang_r = row.astype(F32)[:, None] * freqs
    ang_c = col.astype(F32)[:, None] * freqs
    ang = jnp.concatenate([ang_r, ang_r, ang_c, ang_c], -1)
    cos, sin = jnp.cos(ang), jnp.sin(ang)
    lo = (jnp.arange(HD) % (HD // 2)) < (HD // 4)
    sin_a = jnp.where(lo[None, :], -sin, 0.0)
    sin_b = jnp.where(lo[None, :], 0.0, sin)
    return cos, sin_a, sin_b


def _rope(x, cos, sa, sb):
    return x * cos + pltpu.roll(x, 96, 1) * sa + pltpu.roll(x, 32, 1) * sb


def _rope_t(dy, cos, sa, sb):
    return dy * cos + pltpu.roll(dy * sa, 32, 1) + pltpu.roll(dy * sb, 96, 1)


BAND = 3 * WINDOW
KPAD = T + 2 * WINDOW


def _attn_fill_kv(k_ref, v_ref, cos_ref, sa_ref, sb_ref, kp, vp, kc, vc):
    zeros = jnp.zeros((WINDOW, KVW), BF16)
    kp[0:WINDOW, :] = zeros
    kp[WINDOW + T:KPAD, :] = zeros
    vp[0:WINDOW, :] = zeros
    vp[WINDOW + T:KPAD, :] = zeros
    for hh in range(NKV):
        cs = slice(hh * HD, (hh + 1) * HD)
        for r0 in range(0, T, 512):
            rs = slice(r0, r0 + 512)
            kr = _rope(k_ref[rs, cs], cos_ref[rs, :], sa_ref[rs, :], sb_ref[rs, :])
            kp[WINDOW + r0:WINDOW + r0 + 512, cs] = kr.astype(BF16)
    vp[WINDOW:WINDOW + T, :] = v_ref[0:T, :].astype(BF16)
    kc[...] = k_ref[T:TA, :].astype(BF16)
    vc[...] = v_ref[T:TA, :].astype(BF16)


GROWS = GROUP * WINDOW


def _attn_scores(n, kvh, q_ref, cos_ref, sa_ref, sb_ref, sink_ref, kp, kc):
    r0 = pl.multiple_of(n * WINDOW, WINDOW)
    cos = cos_ref[pl.ds(r0, WINDOW), :]
    sa = sa_ref[pl.ds(r0, WINDOW), :]
    sb = sb_ref[pl.ds(r0, WINDOW), :]
    heads = range(kvh * GROUP, (kvh + 1) * GROUP)
    q_g = jnp.concatenate([_rope(q_ref[:, h * HD:(h + 1) * HD], cos, sa, sb).astype(BF16) for h in heads], axis=0)
    kb = kp[pl.ds(r0, BAND), kvh * HD:(kvh + 1) * HD]
    kcb = kc[:, kvh * HD:(kvh + 1) * HD]
    nt = (((1,), (1,)), ((), ()))
    s_loc = lax.dot_general(q_g, kb, nt, preferred_element_type=F32) * ATT_SCALE
    s_ctx = lax.dot_general(q_g, kcb, nt, preferred_element_type=F32) * ATT_SCALE
    row = lax.broadcasted_iota(jnp.int32, (GROWS, BAND), 0) & (WINDOW - 1)
    col = lax.broadcasted_iota(jnp.int32, (GROWS, BAND), 1)
    rel = col - WINDOW - row
    kpos = r0 - WINDOW + col
    valid = (jnp.abs(rel) <= WINDOW) & (kpos >= 0) & (kpos < T)
    s_loc = jnp.where(valid, s_loc, NEG_INF)
    sk = jnp.concatenate([jnp.broadcast_to(sink_ref[0:1, h:h + 1], (WINDOW, 1)) for h in heads], axis=0)
    m = jnp.maximum(jnp.maximum(jnp.max(s_loc, -1, keepdims=True), jnp.max(s_ctx, -1, keepdims=True)), sk)
    e_loc = jnp.exp(s_loc - m)
    e_ctx = jnp.exp(s_ctx - m)
    e_sink = jnp.exp(sk - m)
    inv = 1.0 / (jnp.sum(e_loc, -1, keepdims=True) + jnp.sum(e_ctx, -1, keepdims=True) + e_sink)
    return q_g, r0, e_loc * inv, e_ctx * inv, e_sink * inv


def _attn_fwd(proj, sink, tabs):
    cos, sa, sb = tabs

    def kern(q_ref, k_ref, v_ref, cos_ref, sa_ref, sb_ref, sink_ref, o_ref, kp, vp, kc, vc):
        n = pl.program_id(0)

        @pl.when(n == 0)
        def _():
            _attn_fill_kv(k_ref, v_ref, cos_ref, sa_ref, sb_ref, kp, vp, kc, vc)

        for kvh in range(NKV):
            _, r0, p_loc, p_ctx, _ = _attn_scores(n, kvh, q_ref, cos_ref, sa_ref, sb_ref, sink_ref, kp, kc)
            vb = vp[pl.ds(r0, BAND), kvh * HD:(kvh + 1) * HD]
            vcb = vc[:, kvh * HD:(kvh + 1) * HD]
            o = jnp.dot(p_loc.astype(BF16), vb, preferred_element_type=F32)
            o = o + jnp.dot(p_ctx.astype(BF16), vcb, preferred_element_type=F32)
            for g in range(GROUP):
                h = kvh * GROUP + g
                o_ref[:, h * HD:(h + 1) * HD] = o[g * WINDOW:(g + 1) * WINDOW, :].astype(o_ref.dtype)

    full = lambda shape: pl.BlockSpec(shape, lambda n: (0, 0))
    return pl.pallas_call(
        kern,
        name="attn_fwd",
        grid=(T // WINDOW,),
        in_specs=[
            pl.BlockSpec((WINDOW, QW), lambda n: (n, 0)),
            pl.BlockSpec((TA, KVW), lambda n: (0, QW // KVW)),
            pl.BlockSpec((TA, KVW), lambda n: (0, QW // KVW + 1)),
            full((T, HD)), full((T, HD)), full((T, HD)), full((1, NH)),
        ],
        out_specs=pl.BlockSpec((WINDOW, QW), lambda n: (n, 0)),
        out_shape=jax.ShapeDtypeStruct((T, QW), BF16),
        scratch_shapes=[pltpu.VMEM((KPAD, KVW), BF16), pltpu.VMEM((KPAD, KVW), BF16),
                        pltpu.VMEM((C, KVW), BF16), pltpu.VMEM((C, KVW), BF16)],
        compiler_params=_cparams(("arbitrary",)),
    )(proj, proj, proj, cos, sa, sb, sink)


def _attn_bwd(proj, d_attn, sink, tabs):
    cos, sa, sb = tabs
    n_blocks = T // WINDOW

    def kern(q_ref, k_ref, v_ref, do_ref, cos_ref, sa_ref, sb_ref, sink_ref,
             dq_ref, dk_ref, dv_ref, dsink_ref, kp, vp, kc, vc, dkp, dvp, dkc, dvc):
        n = pl.program_id(0)

        @pl.when(n == 0)
        def _():
            _attn_fill_kv(k_ref, v_ref, cos_ref, sa_ref, sb_ref, kp, vp, kc, vc)
            dkp[...] = jnp.zeros_like(dkp)
            dvp[...] = jnp.zeros_like(dvp)
            dkc[...] = jnp.zeros_like(dkc)
            dvc[...] = jnp.zeros_like(dvc)
            dsink_ref[...] = jnp.zeros_like(dsink_ref)

        nt = (((1,), (1,)), ((), ()))
        tn = (((0,), (0,)), ((), ()))
        for kvh in range(NKV):
            cs = slice(kvh * HD, (kvh + 1) * HD)
            heads = range(kvh * GROUP, (kvh + 1) * GROUP)
            q_g, r0, p_loc, p_ctx, p_sink = _attn_scores(n, kvh, q_ref, cos_ref, sa_ref, sb_ref, sink_ref, kp, kc)
            kb = kp[pl.ds(r0, BAND), cs]
            vb = vp[pl.ds(r0, BAND), cs]
            kcb = kc[:, cs]
            vcb = vc[:, cs]
            do_g = jnp.concatenate([do_ref[:, h * HD:(h + 1) * HD] for h in heads], axis=0)
            dp_loc = lax.dot_general(do_g, vb, nt, preferred_element_type=F32)
            dp_ctx = lax.dot_general(do_g, vcb, nt, preferred_element_type=F32)
            delta = jnp.sum(p_loc * dp_loc, -1, keepdims=True) + jnp.sum(p_ctx * dp_ctx, -1, keepdims=True)
            ds_loc = (p_loc * (dp_loc - delta) * ATT_SCALE).astype(BF16)
            ds_ctx = (p_ctx * (dp_ctx - delta) * ATT_SCALE).astype(BF16)
            dq = jnp.dot(ds_loc, kb, preferred_element_type=F32) + jnp.dot(ds_ctx, kcb, preferred_element_type=F32)
            cos = cos_ref[pl.ds(r0, WINDOW), :]
            sa_ = sa_ref[pl.ds(r0, WINDOW), :]
            sb_ = sb_ref[pl.ds(r0, WINDOW), :]
            dkp[pl.ds(r0, BAND), cs] += lax.dot_general(ds_loc, q_g, tn, preferred_element_type=F32)
            dkc[:, cs] += lax.dot_general(ds_ctx, q_g, tn, preferred_element_type=F32)
            dvp[pl.ds(r0, BAND), cs] += lax.dot_general(p_loc.astype(BF16), do_g, tn, preferred_element_type=F32)
            dvc[:, cs] += lax.dot_general(p_ctx.astype(BF16), do_g, tn, preferred_element_type=F32)
            dsk_rows = p_sink * delta
            for g, h in enumerate(heads):
                rs = slice(g * WINDOW, (g + 1) * WINDOW)
                dq_ref[:, h * HD:(h + 1) * HD] = _rope_t(dq[rs, :], cos, sa_, sb_).astype(dq_ref.dtype)
                dsk = -jnp.sum(dsk_rows[rs, :], axis=0, keepdims=True)
                dsink_ref[h:h + 1, :] += jnp.broadcast_to(dsk, (1, HD))

        @pl.when(n == n_blocks - 1)
        def _():
            for hh in range(NKV):
                cs = slice(hh * HD, (hh + 1) * HD)
                for r0 in range(0, T, 512):
                    rs = slice(r0, r0 + 512)
                    g = dkp[WINDOW + r0:WINDOW + r0 + 512, cs]
                    dk_ref[rs, cs] = _rope_t(g, cos_ref[rs, :], sa_ref[rs, :], sb_ref[rs, :]).astype(dk_ref.dtype)
            dk_ref[T:TA, :] = dkc[...].astype(dk_ref.dtype)
            dv_ref[0:T, :] = dvp[WINDOW:WINDOW + T, :].astype(dv_ref.dtype)
            dv_ref[T:TA, :] = dvc[...].astype(dv_ref.dtype)

    full = lambda shape: pl.BlockSpec(shape, lambda n: (0, 0))
    return pl.pallas_call(
        kern,
        name="attn_bwd",
        grid=(n_blocks,),
        in_specs=[
            pl.BlockSpec((WINDOW, QW), lambda n: (n, 0)),
            pl.BlockSpec((TA, KVW), lambda n: (0, QW // KVW)),
            pl.BlockSpec((TA, KVW), lambda n: (0, QW // KVW + 1)),
            pl.BlockSpec((WINDOW, QW), lambda n: (n, 0)),
            full((T, HD)), full((T, HD)), full((T, HD)), full((1, NH)),
        ],
        out_specs=[pl.BlockSpec((WINDOW, QW), lambda n: (n, 0)), full((TA, KVW)), full((TA, KVW)), full((NH, HD))],
        out_shape=[jax.ShapeDtypeStruct((T, QW), BF16), jax.ShapeDtypeStruct((TA, KVW), BF16),
                   jax.ShapeDtypeStruct((TA, KVW), BF16), jax.ShapeDtypeStruct((NH, HD), F32)],
        scratch_shapes=[pltpu.VMEM((KPAD, KVW), BF16), pltpu.VMEM((KPAD, KVW), BF16),
                        pltpu.VMEM((C, KVW), BF16), pltpu.VMEM((C, KVW), BF16),
                        pltpu.VMEM((KPAD, KVW), F32), pltpu.VMEM((KPAD, KVW), F32),
                        pltpu.VMEM((C, KVW), F32), pltpu.VMEM((C, KVW), F32)],
        compiler_params=_cparams(("arbitrary",)),
    )(proj, proj, proj, d_attn, cos, sa, sb, sink)


def _s5_prep(a_re, a_im, log_dt, b_re, b_im, c_re, c_im):
    lam = lax.complex(a_re, a_im)
    dt = jnp.exp(log_dt)[..., None]
    lam_bar = jnp.exp(lam * dt)
    b_bar = ((lam_bar - 1.0) / lam)[..., None] * lax.complex(b_re, b_im)
    def lam_rows(v):
        return v.reshape(2, NBLK, 1, BW)

    lam_l = jnp.concatenate([lam_rows(jnp.real(lam_bar)), lam_rows(jnp.imag(lam_bar))], -1)
    lam_l = jnp.broadcast_to(lam_l, (2, NBLK, 8, 2 * BW))
    diag = (jnp.arange(UW)[:, None] // SG) == (jnp.arange(BW)[None, :] // SP)

    def blocks(v):
        return jnp.where(diag, jnp.tile(v.reshape(2, NBLK, UW, SP), (1, 1, 1, GBLK)), 0.0)

    b_t = jnp.swapaxes(b_bar, -1, -2)
    bmat = jnp.concatenate([blocks(jnp.real(b_t)), blocks(jnp.imag(b_t))], -1)
    cmat = jnp.concatenate([blocks(c_re), -blocks(c_im)], -1)
    return lam_l, bmat, cmat


def _cmul(ar, ai, br, bi):
    return ar * br - ai * bi, ar * bi + ai * br


def _shift_rows(x, rev, fill):
    r = lax.broadcasted_iota(jnp.int32, x.shape, 0)
    down = jnp.where(r == 0, fill, pltpu.roll(x, 1, 0))
    up = jnp.where(r == NSEG - 1, fill, pltpu.roll(x, NSEG - 1, 0))
    return jnp.where(rev == 0, down, up)


def _edge_row(x, rev):
    last = jnp.broadcast_to(x[NSEG - 1:NSEG, :], x.shape)
    first = jnp.broadcast_to(x[0:1, :], x.shape)
    return jnp.where(rev == 0, last, first)


def _seg_scan(get, put, base, seglen, lr, li, rev, cin, acc_fn=None, acc0=()):
    zero = jnp.zeros((NSEG, BW), F32)

    def rows(k):
        j = jnp.where(rev == 0, k, seglen - 1 - k)
        return pl.ds(pl.multiple_of(base + j * NSEG, NSEG), NSEG)

    def local(k, carry):
        sr, si = carry
        xr, xi = get(rows(k))
        tr, ti = _cmul(lr, li, sr, si)
        sr, si = tr + xr, ti + xi
        put(rows(k), sr, si)
        return sr, si

    er, ei = lax.fori_loop(0, seglen, local, (zero, zero))
    lpr, lpi = lr, li
    assert seglen & (seglen - 1) == 0, seglen
    for _ in range(seglen.bit_length() - 1):
        lpr, lpi = _cmul(lpr, lpi, lpr, lpi)
    cr, ci = _shift_rows(zero, rev, cin[0]), _shift_rows(zero, rev, cin[1])
    for _ in range(NSEG - 1):
        tr, ti = _cmul(lpr, lpi, cr, ci)
        cr, ci = _shift_rows(er + tr, rev, cin[0]), _shift_rows(ei + ti, rev, cin[1])

    def fix(k, carry):
        tr, ti = _cmul(lr, li, carry[0], carry[1])
        xr, xi = get(rows(k))
        fr, fi = xr + tr, xi + ti
        put(rows(k), fr, fi)
        if acc_fn is None:
            return tr, ti
        j = jnp.where(rev == 0, k, seglen - 1 - k)
        return (tr, ti) + tuple(acc_fn(j, fr, fi, carry[2:]))

    out = lax.fori_loop(0, seglen, fix, (cr, ci) + tuple(acc0))
    tr, ti = out[0], out[1]
    leaving = (_edge_row(er + tr, rev), _edge_row(ei + ti, rev))
    return leaving if acc_fn is None else (leaving, out[2:])


RCH = 256
CSEG = C // NSEG
TSEG = T // NSEG
UCOL0 = (QW + 2 * KVW) // UW


REGIONS = ((0, TSEG), (T, CSEG))


def _state_access(ref, lead=()):
    def get(rows):
        return ref[(*lead, rows, slice(0, BW))], ref[(*lead, rows, slice(BW, 2 * BW))]

    def put(rows, re, im):
        ref[(*lead, rows, slice(0, BW))] = re
        ref[(*lead, rows, slice(BW, 2 * BW))] = im

    return get, put


def _interleave_rows(src_ref, dst_ref, regions=REGIONS):
    for base, seglen in regions:
        def body(j, carry, base=base, seglen=seglen):
            dst_ref[pl.ds(pl.multiple_of(base + j * NSEG, NSEG), NSEG), :] = src_ref[pl.ds(base + j, NSEG, stride=seglen), :]
            return carry

        lax.fori_loop(0, seglen, body, 0, unroll=8)


def _deinterleave_rows(src_ref, dst_ref, regions=REGIONS):
    for base, seglen in regions:
        def body(j, carry, base=base, seglen=seglen):
            dst_ref[pl.ds(base + j, NSEG, stride=seglen), :] = src_ref[pl.ds(pl.multiple_of(base + j * NSEG, NSEG), NSEG), :]
            return carry

        lax.fori_loop(0, seglen, body, 0, unroll=8)


def _s5_fwd(proj, dskip, lam, bmat, cmat):
    def kern(u_ref, dk_ref, lam_ref, b_ref, c_ref, s_ref, ssm_ref, ge_ref, up_ref, yp_ref):
        d = pl.program_id(1)

        @pl.when(d == 0)
        def _():
            _interleave_rows(u_ref, up_ref)

        bm = b_ref[0, 0].astype(BF16)
        for r0 in range(0, TA, RCH):
            s_ref[0, 0, r0:r0 + RCH, :] = jnp.dot(up_ref[r0:r0 + RCH, :].astype(BF16), bm, preferred_element_type=F32)
        lr = lam_ref[0, 0, :, 0:BW]
        li = lam_ref[0, 0, :, BW:2 * BW]
        zero = jnp.zeros((NSEG, BW), F32)
        get, put = _state_access(s_ref, (0, 0))
        mid = _seg_scan(get, put, T, CSEG, lr, li, d, (zero, zero))
        _seg_scan(get, put, 0, TSEG, lr, li, d, mid)
        cm = c_ref[0, 0].astype(BF16)
        for r0 in range(0, T, RCH):
            y = lax.dot_general(s_ref[0, 0, r0:r0 + RCH, :].astype(BF16), cm, (((1,), (1,)), ((), ())), preferred_element_type=F32)

            @pl.when(d == 0)
            def _(y=y, r0=r0):
                yp_ref[r0:r0 + RCH, :] = y + dk_ref[...] * up_ref[r0:r0 + RCH, :]

            @pl.when(d == 1)
            def _(y=y, r0=r0):
                yp_ref[r0:r0 + RCH, :] += y

        @pl.when(d == 1)
        def _():
            _deinterleave_rows(yp_ref, ssm_ref, REGIONS[:1])
            for r0 in range(0, T, RCH):
                ge_ref[r0:r0 + RCH, :] = _gelu(ssm_ref[r0:r0 + RCH, :]).astype(ge_ref.dtype)

    blk4 = lambda shape: pl.BlockSpec((1, 1) + shape, lambda b, d: (d, b, 0, 0))
    return pl.pallas_call(
        kern,
        name="s5_fwd",
        grid=(NBLK, 2),
        in_specs=[pl.BlockSpec((TA, UW), lambda b, d: (0, UCOL0 + b)), pl.BlockSpec((1, UW), lambda b, d: (0, b)),
                  blk4((8, 2 * BW)), blk4((UW, 2 * BW)), blk4((UW, 2 * BW))],
        out_specs=[blk4((TA, 2 * BW)), pl.BlockSpec((T, UW), lambda b, d: (0, b)), pl.BlockSpec((T, UW), lambda b, d: (0, b))],
        out_shape=[jax.ShapeDtypeStruct((2, NBLK, TA, 2 * BW), F32), jax.ShapeDtypeStruct((T, SW), F32),
                   jax.ShapeDtypeStruct((T, SW), BF16)],
        scratch_shapes=[pltpu.VMEM((TA, UW), F32), pltpu.VMEM((T, UW), F32)],
        compiler_params=_cparams(("parallel", "arbitrary")),
    )(proj, dskip, lam, bmat, cmat)


def _s5_bwd(d_ge, ssm, proj, dskip, states, lam, bmat, cmat):
    nt = (((1,), (1,)), ((), ()))
    tn = (((0,), (0,)), ((), ()))

    def kern(dge_ref, ssm_ref, u_ref, dk_ref, s_ref, lam_ref, b_ref, c_ref,
             du_ref, ddk_ref, dlam_ref, db_ref, dc_ref, g_ref, dua_ref, dssm_ref, up_ref, nat_ref):
        d = pl.program_id(1)

        @pl.when(d == 0)
        def _():
            ddk = jnp.zeros((1, UW), F32)
            for r0 in range(0, T, RCH):
                rs = slice(r0, r0 + RCH)
                _, pull = jax.vjp(_gelu, ssm_ref[rs, :])
                dssm = pull(dge_ref[rs, :])[0]
                nat_ref[rs, :] = dssm
                ddk = ddk + jnp.sum(dssm * u_ref[rs, :], axis=0, keepdims=True)
            ddk_ref[...] = ddk
            _interleave_rows(nat_ref, dssm_ref, REGIONS[:1])
            _interleave_rows(u_ref, up_ref)
            for r0 in range(0, T, RCH):
                dua_ref[r0:r0 + RCH, :] = dssm_ref[r0:r0 + RCH, :] * dk_ref[...]
            dua_ref[T:TA, :] = jnp.zeros((C, UW), F32)

        cm = c_ref[0, 0].astype(BF16)
        for r0 in range(0, T, RCH):
            g_ref[r0:r0 + RCH, :] = jnp.dot(dssm_ref[r0:r0 + RCH, :].astype(BF16), cm, preferred_element_type=F32)
        g_ref[T:TA, :] = jnp.zeros((C, 2 * BW), F32)
        lr = lam_ref[0, 0, :, 0:BW]
        li = lam_ref[0, 0, :, BW:2 * BW]
        zero = jnp.zeros((NSEG, BW), F32)
        get_g, put_g = _state_access(g_ref)

        get_s, _ = _state_access(s_ref, (0, 0))

        def dlam_fold(base, seglen, s_in):
            def rows(j):
                return pl.ds(pl.multiple_of(base + j * NSEG, NSEG), NSEG)

            jb = jnp.where(d == 0, 0, seglen - 1)
            jn = jnp.where(d == 0, seglen - 1, 0)
            sp = get_s(rows(jn))
            edge = (_shift_rows(sp[0], d, s_in[0]), _shift_rows(sp[1], d, s_in[1]))

            def fold(j, gr, gi, acc):
                jp = jnp.clip(jnp.where(d == 0, j - 1, j + 1), 0, seglen - 1)
                sr, si = get_s(rows(jp))
                sr = jnp.where(j == jb, edge[0], sr)
                si = jnp.where(j == jb, edge[1], si)
                return acc[0] + (gr * sr + gi * si), acc[1] + (gi * sr - gr * si)

            return fold

        r_mid = jnp.where(d == 0, TA - 1, T)
        s_mid = tuple(jnp.broadcast_to(t, (NSEG, BW)) for t in get_s(pl.ds(r_mid, 1)))
        mid, acc = _seg_scan(get_g, put_g, 0, TSEG, lr, -li, 1 - d, (zero, zero), dlam_fold(0, TSEG, s_mid), (zero, zero))
        _, acc = _seg_scan(get_g, put_g, T, CSEG, lr, -li, 1 - d, mid, dlam_fold(T, CSEG, (zero, zero)), acc)
        dlam_ref[0, 0, :, 0:BW] = acc[0]
        dlam_ref[0, 0, :, BW:2 * BW] = acc[1]

        bm = b_ref[0, 0].astype(BF16)
        db = jnp.zeros((UW, 2 * BW), F32)
        dc = jnp.zeros((UW, 2 * BW), F32)
        for r0 in range(0, TA, RCH):
            rs = slice(r0, r0 + RCH)
            g = g_ref[rs, :].astype(BF16)
            dua_ref[rs, :] += lax.dot_general(g, bm, nt, preferred_element_type=F32)
            db = db + lax.dot_general(up_ref[rs, :].astype(BF16), g, tn, preferred_element_type=F32)
            if r0 < T:
                dc = dc + lax.dot_general(dssm_ref[rs, :].astype(BF16), s_ref[0, 0, rs, :].astype(BF16), tn,
                                          preferred_element_type=F32)
        db_ref[0, 0] = db
        dc_ref[0, 0] = dc

        @pl.when(d == 1)
        def _():
            _deinterleave_rows(dua_ref, nat_ref)
            du_ref[...] = nat_ref[...].astype(du_ref.dtype)

    blk4 = lambda shape: pl.BlockSpec((1, 1) + shape, lambda b, d: (d, b, 0, 0))
    lat = pl.BlockSpec((T, UW), lambda b, d: (0, b))
    vec = pl.BlockSpec((1, UW), lambda b, d: (0, b))
    return pl.pallas_call(
        kern,
        name="s5_bwd",
        grid=(NBLK, 2),
        in_specs=[lat, lat, pl.BlockSpec((TA, UW), lambda b, d: (0, UCOL0 + b)), vec,
                  blk4((TA, 2 * BW)), blk4((8, 2 * BW)), blk4((UW, 2 * BW)), blk4((UW, 2 * BW))],
        out_specs=[pl.BlockSpec((TA, UW), lambda b, d: (0, b)), vec, blk4((8, 2 * BW)), blk4((UW, 2 * BW)), blk4((UW, 2 * BW))],
        out_shape=[jax.ShapeDtypeStruct((TA, SW), BF16), jax.ShapeDtypeStruct((1, SW), F32),
                   jax.ShapeDtypeStruct((2, NBLK, 8, 2 * BW), F32),
                   jax.ShapeDtypeStruct((2, NBLK, UW, 2 * BW), F32), jax.ShapeDtypeStruct((2, NBLK, UW, 2 * BW), F32)],
        scratch_shapes=[pltpu.VMEM((TA, 2 * BW), F32), pltpu.VMEM((TA, UW), F32), pltpu.VMEM((T, UW), F32),
                        pltpu.VMEM((TA, UW), F32), pltpu.VMEM((TA, UW), F32)],
        compiler_params=_cparams(("parallel", "arbitrary")),
    )(d_ge, ssm, proj, dskip, states, lam, bmat, cmat)


TR = 256
TN_WIDE = 1024


def _vjp_rows(f, primals, cots, n_row):
    _, pull = jax.vjp(f, *primals)
    g = pull(cots)
    return list(g[:n_row]), list(g[n_row:])


class _GradDict(dict):
    def __init__(self, on_set=None):
        super().__init__()
        self._on_set = on_set
        self.tokens = {}

    def __setitem__(self, key, value):
        super().__setitem__(key, value)
        if self._on_set is not None:
            self._on_set(self)

    def order(self, key):
        return self.tokens.get(key, self.get(key))

    def finish(self, key, after):
        if self.on_finish is None:
            return ()
        return (self.on_finish(key, after),)

    on_finish = None


def _local_step(x, ctx, tgt, mod_lat, mod_ctx, wb, sp, on_grad=None, on_loss=None, on_finish=None, on_early=None):
    sh1, sc1, g1, sh2, sc2, g2 = [mod_lat[:, i * D:(i + 1) * D] for i in range(6)]
    csh1, csc1 = mod_ctx[:, 0:D], mod_ctx[:, D:2 * D]
    tabs = _rope_tables()
    sink = sp["attn_sink"].reshape(1, NH)
    dskip = sp["ssm_d"].reshape(1, SW)
    lg_mix, lb_mix = sp["ln_mix_g"].reshape(1, D), sp["ln_mix_b"].reshape(1, D)
    lg_mlp, lb_mlp = sp["ln_mlp_g"].reshape(1, D), sp["ln_mlp_b"].reshape(1, D)
    b1, b2 = sp["b_mlp1"].reshape(1, DFF), sp["b_mlp2"].reshape(1, D)
    s5_names = ("ssm_a_re", "ssm_a_im", "ssm_log_dt", "ssm_b_re", "ssm_b_im", "ssm_c_re", "ssm_c_im")
    (lam, bmat, cmat), s5_pull = jax.vjp(_s5_prep, *[sp[n] for n in s5_names])

    def ln_mod2(rv, vv):
        h = _f_ln_mod(rv[0], vv[0], vv[1])
        return [h, h], []

    h_lat, h_lat_t = _rowwise(ln_mod2, [(x, D, 0, 0)], [sc1, sh1], [(D, BF16), (D, BF16, True)], [], nrows=T, tr=TR, name="ln1_lat")
    h_ctx, h_ctx_t = _rowwise(ln_mod2, [(ctx, D, 0, 0)], [csc1, csh1], [(D, BF16), (D, BF16, True)], [], nrows=C, tr=TR,
                              name="ln1_ctx")
    h1 = jnp.concatenate([h_lat, h_ctx], 0)
    h1_t = jnp.concatenate([h_lat_t, h_ctx_t], 1)
    proj = _matmul(h1, wb["w_in"], mode="nn", name="proj", tm=768, tn=TN_WIDE)
    attn = _attn_fwd(proj, sink, tabs)
    states, ssm, ge = _s5_fwd(proj, dskip, lam, bmat, cmat)
    z = _matmul(ge, wb["w_glu"], mode="nn", name="glu_mm", tm=1024, tn=1024, out_dtypes=(BF16,))

    def glu_act(rv, vv):
        return [_f_glu(rv[0])], []

    glu, = _rowwise(glu_act, [(z, 2 * SW, 0, 0)], [], [(SW, BF16)], [], nrows=T, tr=TR, name="glu_act")
    attn_d = _matmul(attn, wb["w_attn_up"], mode="nn", name="attn_up", tm=1024, tn=512, out_dtypes=(BF16,))
    ssm_d = _matmul(glu, wb["w_ssm_up"], mode="nn", name="ssm_up", tm=1024, tn=512, out_dtypes=(BF16,))
    ga_cb, gs_cb = (QW + 2 * KVW + SW) // D, (QW + 2 * KVW + SW) // D + 1

    def mix(rv, vv):
        m_ = _f_mix(*rv)
        return [m_, m_], []

    mixv, mix_t = _rowwise(mix, [(proj, D, ga_cb, 0), (proj, D, gs_cb, 0), (attn_d, D, 0, 0), (ssm_d, D, 0, 0)], [],
                           [(D, BF16), (D, BF16, True)], [], nrows=T, tr=TR, name="mix")
    y = _matmul(mixv, wb["w_out"], mode="nn", name="out_proj", tm=1024, tn=TN_WIDE, out_dtypes=(BF16,))

    def post1(rv, vv):
        x1, h2 = _f_post1(rv[0], rv[1], *vv)
        return [x1, h2, h2], []

    x1, h2, h2_t = _rowwise(post1, [(x, D, 0, 0), (y, D, 0, 0)], [g1, lg_mix, lb_mix, sc2, sh2],
                            [(D, F32), (D, BF16), (D, BF16, True)], [], nrows=T, tr=TR, name="post1")

    def relu_sq(acc):
        r = jnp.maximum(acc, 0.0)
        return r, r * r, r * r

    r_act, act, act_t = _matmul(h2, wb["w_mlp1"], mode="nn", name="mlp1", tm=1024, tn=TN_WIDE, bias=b1,
                                out_dtypes=(BF16, BF16, BF16), out_t=(False, False, True), epilogue=relu_sq)
    mlp = _matmul(act, wb["w_mlp2"], mode="nn", name="mlp2", tm=512, tn=512, out_dtypes=(BF16,))

    def loss_fb(rv, vv):
        x1_t, mlp_t, tgt_t = rv
        g2_v, lg_v, lb_v, b2_v = vv
        f = lambda a, m, g, p, q, b: _f_loss(a, m, tgt_t, g, p, q, b)
        val, grads = jax.value_and_grad(f, argnums=(0, 1, 2, 3, 4, 5))(x1_t, mlp_t, g2_v, lg_v, lb_v, b2_v)
        dx1, dmlp, dg2, dlg, dlb, db2 = grads
        return [dx1, dmlp], [jnp.reshape(val, (1, 1)), dg2, dlg, dlb, db2]

    dx1_a, d_mlp, loss_p, d_g2, d_lg_mlp, d_lb_mlp, d_b2 = _rowwise(
        loss_fb, [(x1, D, 0, 0), (mlp, D, 0, 0), (tgt, D, 0, 0)], [g2, lg_mlp, lb_mlp, b2],
        [(D, BF16), (D, BF16)], [(1, 1), (1, D), (1, D), (1, D), (1, D)], nrows=T, tr=TR, name="loss_fb")

    gw = _GradDict(on_grad)
    gw.on_finish = on_finish
    loss_done = () if on_loss is None else (on_loss(loss_p),)
    gw["w_mlp2"] = _matmul(act_t, d_mlp, mode="nn", name="dw_mlp2", out_dtypes=(BF16,), tm=1024, tn=TN_WIDE, after=loss_done)
    da, = (_matmul(d_mlp, wb["w_mlp2"], mode="nt", name="d_act", out_dtypes=(BF16,), tm=1024, tn=TN_WIDE,
                   extras=(r_act,), epilogue=lambda acc, r: (acc * (2.0 * r.astype(F32)),), after=(gw.order("w_mlp2"),)),)
    pin = gw.finish("w_mlp2", da)
    ones = jnp.ones((8, T), BF16)
    d_b1 = _matmul(ones, da, mode="nn", name="db_mlp1", tm=8, tn=2048)[0:1]
    gw["w_mlp1"] = _matmul(h2_t, da, mode="nn", name="dw_mlp1", out_dtypes=(BF16,), tm=1024, tn=TN_WIDE, after=pin)
    dh2 = _matmul(da, wb["w_mlp1"], mode="nt", name="d_h2", tm=512, tn=512, out_dtypes=(BF16,), after=(gw.order("w_mlp1"),))

    def post1_b(rv, vv):
        x_t, y_t, dx1_t, dh2_t = rv
        gr, gv = _vjp_rows(_f_post1, (x_t, y_t, *vv), (dx1_t, dh2_t), 2)
        return [gr[0], gr[1]], gv

    dx_a, dy, d_g1, d_lg_mix, d_lb_mix, d_sc2, d_sh2 = _rowwise(
        post1_b, [(x, D, 0, 0), (y, D, 0, 0), (dx1_a, D, 0, 0), (dh2, D, 0, 0)], [g1, lg_mix, lb_mix, sc2, sh2],
        [(D, BF16), (D, BF16)], [(1, D)] * 5, nrows=T, tr=TR, name="post1_bwd")
    gw["w_out"] = _matmul(mix_t, dy, mode="nn", name="dw_out", out_dtypes=(BF16,), tm=1024, tn=TN_WIDE)
    dmix = _matmul(dy, wb["w_out"], mode="nt", name="d_mix", tm=1024, tn=TN_WIDE, out_dtypes=(BF16,), after=(gw.order("w_out"),))

    def mix_b(rv, vv):
        gr, _ = _vjp_rows(_f_mix, tuple(rv[:4]), rv[4], 4)
        return gr, []

    d_ga, d_gs, d_attn_d, d_ssm_d = _rowwise(
        mix_b, [(proj, D, ga_cb, 0), (proj, D, gs_cb, 0), (attn_d, D, 0, 0), (ssm_d, D, 0, 0), (dmix, D, 0, 0)], [],
        [(D, BF16)] * 4, [], nrows=T, tr=TR, name="mix_bwd")
    pin = gw.finish("w_mlp1", d_ga)
    gw["w_attn_up"] = _matmul(attn, d_attn_d, mode="tn", name="dw_attn_up", out_dtypes=(BF16,), tm=512, tn=1024, tk=1024, after=pin)
    d_attn = _matmul(d_attn_d, wb["w_attn_up"], mode="nt", name="d_attn", out_dtypes=(BF16,), tm=1024, tn=512)
    gw["w_ssm_up"] = _matmul(glu, d_ssm_d, mode="tn", name="dw_ssm_up", out_dtypes=(BF16,), tm=512, tn=1024, tk=1024)
    d_glu = _matmul(d_ssm_d, wb["w_ssm_up"], mode="nt", name="d_glu", tm=1024, tn=512, after=(gw.order("w_attn_up"), gw.order("w_ssm_up")))

    def glu_b(rv, vv):
        gr, _ = _vjp_rows(_f_glu, (rv[0],), rv[1], 1)
        return gr, []

    dz, = _rowwise(glu_b, [(z, 2 * SW, 0, 0), (d_glu, SW, 0, 0)], [], [(2 * SW, BF16)], [], nrows=T, tr=TR, name="glu_bwd")
    gw["w_glu"] = _matmul(ge, dz, mode="tn", name="dw_glu", out_dtypes=(BF16,), tm=512, tn=1024, tk=1024)
    d_ge = _matmul(dz, wb["w_glu"], mode="nt", name="d_ge", tm=1024, tn=512, after=(gw.order("w_glu"),))

    du_all, d_dskip, dlam, dbmat, dcmat = _s5_bwd(d_ge, ssm, proj, dskip, states, lam, bmat, cmat)
    s5_grads = s5_pull((dlam, dbmat, dcmat))
    early = dict(zip(s5_names, s5_grads), ssm_d=d_dskip)
    if on_early is not None:
        on_early(early)
    pin = gw.finish("w_glu", du_all)

    dq, dk, dv, dsink = _attn_bwd(proj, d_attn, sink, tabs)
    zc = lambda w: jnp.zeros((C, w), BF16)
    dproj = jnp.concatenate([
        jnp.concatenate([dq, zc(QW)], 0), dk, dv, du_all,
        jnp.concatenate([d_ga, zc(D)], 0), jnp.concatenate([d_gs, zc(D)], 0)], 1)
    gw["w_in"] = _matmul(h1_t, dproj, mode="nn", name="dw_in", out_dtypes=(BF16,), tm=1024, tn=TN_WIDE, after=pin)
    pin = gw.finish("w_in", gw["w_in"])
    dh1 = _matmul(dproj, wb["w_in"], mode="nt", name="d_h1", tm=768, tn=512, out_dtypes=(BF16,), after=pin)

    def ln1_b(rv, vv):
        x_t, dh_t, dxa_t = rv
        gr, gv = _vjp_rows(_f_ln_mod, (x_t, vv[0], vv[1]), dh_t, 1)
        return [gr[0] + dxa_t], gv

    grad_x, d_sc1, d_sh1 = _rowwise(ln1_b, [(x, D, 0, 0), (dh1, D, 0, 0), (dx_a, D, 0, 0)], [sc1, sh1],
                                    [(D, F32)], [(1, D), (1, D)], nrows=T, tr=TR, name="ln1_lat_bwd")

    def ln1c_b(rv, vv):
        _, gv = _vjp_rows(_f_ln_mod, (rv[0], vv[0], vv[1]), rv[1], 1)
        return [], gv

    d_csc1, d_csh1 = _rowwise(ln1c_b, [(ctx, D, 0, 0), (dh1, D, 0, T // TR)], [csc1, csh1],
                              [], [(1, D), (1, D)], nrows=C, tr=TR, name="ln1_ctx_bwd")

    d_mod_lat = jnp.concatenate([d_sh1, d_sc1, d_g1, d_sh2, d_sc2, d_g2], 1)
    zv = jnp.zeros((1, D), F32)
    d_mod_ctx = jnp.concatenate([d_csh1, d_csc1, zv, zv, zv, zv], 1)
    gs = {n: g for n, g in zip(s5_names, s5_grads)}
    gs["attn_sink"] = dsink[:, 0]
    gs["ssm_d"] = d_dskip
    gs["ln_mix_g"], gs["ln_mix_b"] = d_lg_mix, d_lb_mix
    gs["ln_mlp_g"], gs["ln_mlp_b"] = d_lg_mlp, d_lb_mlp
    gs["b_mlp1"], gs["b_mlp2"] = d_b1, d_b2
    return loss_p, grad_x, d_mod_lat, d_mod_ctx, gw, gs


def _my_pos():
    return lax.axis_index("x"), lax.axis_index("y"), lax.axis_index("c")


def _flip(p, bit):
    return 1 - p if bit else p


def _peer(pos, k):
    x, y, c = pos
    return (_flip(x, (k >> 2) & 1), _flip(y, (k >> 1) & 1), _flip(c, k & 1))


def _lin(pos):
    return 4 * pos[0] + 2 * pos[1] + pos[2]


def _allgather_small(v, name):
    r, w = v.shape

    def body(v_ref, out_ref, send_sems, recv_sems, local_sem):
        me = _my_pos()
        mine = pltpu.make_async_copy(v_ref, out_ref.at[_lin(me)], local_sem)
        mine.start()
        sends = []
        for k in range(1, N_DEV):
            cp = pltpu.make_async_remote_copy(src_ref=v_ref, dst_ref=out_ref.at[_lin(me)], send_sem=send_sems.at[k - 1],
                                              recv_sem=recv_sems.at[k - 1], device_id=_peer(me, k), device_id_type=MESH)
            cp.start()
            sends.append(cp)
        for k in range(1, N_DEV):
            peer = _peer(me, k)
            pltpu.make_async_remote_copy(src_ref=v_ref, dst_ref=out_ref.at[_lin(peer)], send_sem=send_sems.at[k - 1],
                                         recv_sem=recv_sems.at[k - 1], device_id=peer, device_id_type=MESH).wait_recv()
        for cp in sends:
            cp.wait_send()
        mine.wait()

    return pl.pallas_call(
        body,
        name=name,
        out_shape=jax.ShapeDtypeStruct((N_DEV, r, w), v.dtype),
        in_specs=[pl.BlockSpec(memory_space=pltpu.VMEM)],
        out_specs=pl.BlockSpec(memory_space=pltpu.VMEM),
        scratch_shapes=[pltpu.SemaphoreType.DMA((N_DEV - 1,)), pltpu.SemaphoreType.DMA((N_DEV - 1,)), pltpu.SemaphoreType.DMA],
        compiler_params=pltpu.CompilerParams(vmem_limit_bytes=VMEM_LIMIT_BYTES),
    )(v)


def _block_of(ref, kind, idx, n):
    start = pl.multiple_of(idx * n, 128)
    if kind == "col":
        return ref.at[:, pl.ds(start, n)]
    return ref.at[pl.ds(start, n), :]


def _handshake(peers):
    barrier = pltpu.get_barrier_semaphore()
    for peer in peers:
        pl.semaphore_signal(barrier, inc=1, device_id=peer, device_id_type=MESH)
    pl.semaphore_wait(barrier, len(peers))


def _allgather_weights_seq(shards, kinds, name, collective_id):
    nt = len(shards)
    hbm = pltpu.MemorySpace.HBM
    ins = [jax.new_ref(s, memory_space=hbm) for s in shards]
    outs = []
    for s, kind in zip(shards, kinds):
        k, n = s.shape
        shape = (k, n * N_DEV) if kind == "col" else (k * N_DEV, n)
        outs.append(jax.empty_ref(jax.ShapeDtypeStruct(shape, s.dtype), memory_space=hbm))

    @functools.partial(
        pl.kernel, mesh=plsc.ScalarSubcoreMesh(axis_name="seq", num_cores=1), name=name,
        scratch_types=(pltpu.SemaphoreType.DMA((nt, N_DEV - 1)), pltpu.SemaphoreType.DMA((nt, N_DEV - 1)),
                       pltpu.SemaphoreType.DMA((nt,))),
        compiler_params=pltpu.CompilerParams(collective_id=collective_id))
    def launch(send_sems, recv_sems, local_sems):
        x, y, c = _my_pos()
        me, sibling = (x, y, c), (x, y, 1 - c)
        chips = [(1 - x, y), (x, 1 - y), (1 - x, 1 - y)]
        _handshake([sibling] + [(*chip, c) for chip in chips])

        def blk(t, pos):
            n = shards[t].shape[1] if kinds[t] == "col" else shards[t].shape[0]
            return _block_of(outs[t], kinds[t], _lin(pos), n)

        def copy(t, k, block, to, src=None):
            return pltpu.make_async_remote_copy(src_ref=blk(t, block) if src is None else src, dst_ref=blk(t, block),
                                                send_sem=send_sems.at[t, k], recv_sem=recv_sems.at[t, k],
                                                device_id=to, device_id_type=MESH)

        local, sends = [], []
        for t in range(nt):
            mine = pltpu.make_async_copy(ins[t], blk(t, me), local_sems.at[t])
            mine.start()
            local.append(mine)
            first = [copy(t, 0, me, sibling, src=ins[t])]
            first += [copy(t, 1 + j, me, (*chip, c), src=ins[t]) for j, chip in enumerate(chips)]
            for cp in first:
                cp.start()
            sends += first
        for t in range(nt):
            for j, chip in enumerate(chips):
                copy(t, 1 + j, (*chip, c), me).wait_recv()
                fwd = copy(t, 4 + j, (*chip, c), sibling)
                fwd.start()
                sends.append(fwd)
        for t in range(nt):
            copy(t, 0, sibling, me).wait_recv()
            for j, chip in enumerate(chips):
                copy(t, 4 + j, (*chip, 1 - c), me).wait_recv()
        for cp in sends:
            cp.wait_send()
        for cp in local:
            cp.wait()

    launch()
    return [o[...] for o in outs]


def _allgather_small_seq(v, name, collective_id):
    hbm = pltpu.MemorySpace.HBM
    src = jax.new_ref(v, memory_space=hbm)
    out = jax.empty_ref(jax.ShapeDtypeStruct((N_DEV,) + v.shape, v.dtype), memory_space=hbm)

    @functools.partial(
        pl.kernel, mesh=plsc.ScalarSubcoreMesh(axis_name="seq", num_cores=1), name=name,
        scratch_types=(pltpu.SemaphoreType.DMA((N_DEV - 1,)), pltpu.SemaphoreType.DMA((N_DEV - 1,)), pltpu.SemaphoreType.DMA),
        compiler_params=pltpu.CompilerParams(collective_id=collective_id))
    def launch(send_sems, recv_sems, local_sem):
        me = _my_pos()
        _handshake([_peer(me, k) for k in range(1, N_DEV)])
        mine = pltpu.make_async_copy(src, out.at[_lin(me)], local_sem)
        mine.start()
        sends = []
        for k in range(1, N_DEV):
            cp = pltpu.make_async_remote_copy(src_ref=src, dst_ref=out.at[_lin(me)], send_sem=send_sems.at[k - 1],
                                              recv_sem=recv_sems.at[k - 1], device_id=_peer(me, k), device_id_type=MESH)
            cp.start()
            sends.append(cp)
        for k in range(1, N_DEV):
            peer = _peer(me, k)
            pltpu.make_async_remote_copy(src_ref=src, dst_ref=out.at[_lin(peer)], send_sem=send_sems.at[k - 1],
                                         recv_sem=recv_sems.at[k - 1], device_id=peer, device_id_type=MESH).wait_recv()
        for cp in sends:
            cp.wait_send()
        mine.wait()

    launch()
    return out[...]


N_CHIP = N_DEV // 2


def _chip_of(pos):
    return 2 * pos[0] + pos[1]


def _pair_exchange_seq(grads, kinds, name, collective_id):
    nt = len(grads)
    hbm = pltpu.MemorySpace.HBM
    shard_shapes = _shard_shapes(grads, kinds)
    ins = [jax.new_ref(g, memory_space=hbm) for g in grads]
    outs = [jax.empty_ref(jax.ShapeDtypeStruct((N_CHIP,) + s, g.dtype), memory_space=hbm) for s, g in zip(shard_shapes, grads)]

    @functools.partial(
        pl.kernel, mesh=plsc.ScalarSubcoreMesh(axis_name="seq", num_cores=1), name=name,
        scratch_types=(pltpu.SemaphoreType.DMA((nt, N_CHIP)), pltpu.SemaphoreType.DMA((nt, N_CHIP))),
        compiler_params=pltpu.CompilerParams(collective_id=collective_id))
    def launch(send_sems, recv_sems):
        x, y, c = _my_pos()
        sibling = (x, y, 1 - c)
        _handshake([sibling])
        copies = []
        for t in range(nt):
            n = shard_shapes[t][1] if kinds[t] == "col" else shard_shapes[t][0]
            for q in range(N_CHIP):
                cp = pltpu.make_async_remote_copy(src_ref=_block_of(ins[t], kinds[t], 2 * q + (1 - c), n), dst_ref=outs[t].at[q],
                                                  send_sem=send_sems.at[t, q], recv_sem=recv_sems.at[t, q],
                                                  device_id=sibling, device_id_type=MESH)
                cp.start()
                copies.append(cp)
        for cp in copies:
            cp.wait_recv()
        for cp in copies:
            cp.wait_send()

    launch()
    return [o[...] for o in outs]


def _pair_add(g, half, kind, name, after=()):
    nq, k, ns = half.shape
    tr = min(k, 512)
    c_idx = lax.axis_index("c").astype(jnp.int32).reshape(1)
    if kind == "col":
        g_spec = pl.BlockSpec((tr, ns), lambda q, i, c_ref: (i, 2 * q + c_ref[0]))
    else:
        g_spec = pl.BlockSpec((tr, ns), lambda q, i, c_ref: ((2 * q + c_ref[0]) * (k // tr) + i, 0))
    n_after = len(after)

    def kern(c_ref, g_ref, h_ref, *rest):
        o_ref = rest[n_after]
        o_ref[0] = (g_ref[...].astype(F32) + h_ref[0].astype(F32)).astype(o_ref.dtype)

    return pl.pallas_call(
        kern,
        name=name,
        grid_spec=pltpu.PrefetchScalarGridSpec(
            num_scalar_prefetch=1,
            grid=(nq, k // tr),
            in_specs=[g_spec, pl.BlockSpec((1, tr, ns), lambda q, i, c_ref: (q, i, 0))] + [pl.BlockSpec(memory_space=pl.ANY)] * n_after,
            out_specs=pl.BlockSpec((1, tr, ns), lambda q, i, c_ref: (q, i, 0)),
        ),
        out_shape=jax.ShapeDtypeStruct(half.shape, half.dtype),
        compiler_params=_cparams(("parallel", "parallel")),
    )(c_idx, g, half, *after)


def _chip_exchange_seq(psums, name, collective_id):
    nt = len(psums)
    hbm = pltpu.MemorySpace.HBM
    ins = [jax.new_ref(s, memory_space=hbm) for s in psums]
    outs = [jax.empty_ref(jax.ShapeDtypeStruct(s.shape, s.dtype), memory_space=hbm) for s in psums]

    @functools.partial(
        pl.kernel, mesh=plsc.ScalarSubcoreMesh(axis_name="seq", num_cores=1), name=name,
        scratch_types=(pltpu.SemaphoreType.DMA((nt, N_CHIP - 1)), pltpu.SemaphoreType.DMA((nt, N_CHIP - 1)),
                       pltpu.SemaphoreType.DMA((nt,))),
        compiler_params=pltpu.CompilerParams(collective_id=collective_id))
    def launch(send_sems, recv_sems, local_sems):
        me = _my_pos()
        peers = [_peer(me, k) for k in (2, 4, 6)]
        _handshake(peers)
        mine = _chip_of(me)
        local, sends = [], []
        for t in range(nt):
            cp = pltpu.make_async_copy(ins[t].at[mine], outs[t].at[mine], local_sems.at[t])
            cp.start()
            local.append(cp)
            for j, peer in enumerate(peers):
                cp = pltpu.make_async_remote_copy(src_ref=ins[t].at[_chip_of(peer)], dst_ref=outs[t].at[mine],
                                                  send_sem=send_sems.at[t, j], recv_sem=recv_sems.at[t, j],
                                                  device_id=peer, device_id_type=MESH)
                cp.start()
                sends.append(cp)
        for t in range(nt):
            for j, peer in enumerate(peers):
                pltpu.make_async_remote_copy(src_ref=ins[t].at[mine], dst_ref=outs[t].at[_chip_of(peer)],
                                             send_sem=send_sems.at[t, j], recv_sem=recv_sems.at[t, j],
                                             device_id=peer, device_id_type=MESH).wait_recv()
        for cp in sends:
            cp.wait_send()
        for cp in local:
            cp.wait()

    launch()
    return [o[...] for o in outs]


def _shard_shapes(grads, kinds):
    return [(g.shape[0], g.shape[1] // N_DEV) if kind == "col" else (g.shape[0] // N_DEV, g.shape[1]) for g, kind in zip(grads, kinds)]


def _adam(g_slots, w, m, v, *, tr, name, after=()):
    ns, r, wd = g_slots.shape
    tr = min(tr, r)
    assert r % tr == 0, (name, r, tr)
    n_after = len(after)

    def kern(g_ref, w_ref, m_ref, v_ref, *rest):
        go_ref, d_ref, mo_ref, vo_ref = rest[n_after:]
        g = g_ref[0].astype(F32)
        for s in range(1, ns):
            g = g + g_ref[s].astype(F32)
        delta, m_new, v_new = _adam_update(g, w_ref[...], m_ref[...], v_ref[...])
        go_ref[...] = g
        d_ref[...] = delta
        mo_ref[...] = m_new
        vo_ref[...] = v_new

    tile = pl.BlockSpec((tr, wd), lambda i: (i, 0))
    return pl.pallas_call(
        kern,
        name=name,
        grid=(r // tr,),
        in_specs=[pl.BlockSpec((ns, tr, wd), lambda i: (0, i, 0)), tile, tile, tile] + [pl.BlockSpec(memory_space=pl.ANY)] * n_after,
        out_specs=[tile] * 4,
        out_shape=[jax.ShapeDtypeStruct((r, wd), F32)] * 4,
        compiler_params=_cparams(("parallel",)),
    )(g_slots, w, m, v, *after)


def _adam_update(g, w, m, v):
    m_new = ADAM_B1 * m + (1.0 - ADAM_B1) * g
    v_new = ADAM_B2 * v + (1.0 - ADAM_B2) * (g * g)
    m_hat = m_new / (1.0 - ADAM_B1 ** ADAM_STEP)
    v_hat = v_new / (1.0 - ADAM_B2 ** ADAM_STEP)
    return -ADAM_LR * (m_hat / (jnp.sqrt(v_hat) + ADAM_EPS) + ADAM_WD * w), m_new, v_new


def _lane_offsets(sizes):
    offs, o = [], 0
    for n in sizes:
        offs.append(o)
        o += -(-n // LANES) * LANES
    return offs, o


def _pack_lanes(parts):
    cols = []
    for p_ in parts:
        flat = p_.reshape(1, -1).astype(F32)
        cols.append(jnp.pad(flat, ((0, 0), (0, (-flat.shape[1]) % LANES))))
    return jnp.concatenate(cols, 1)


def _adam_lanes(g_slots, ws, ms, vs, *, name, after=()):
    ns = g_slots.shape[0]
    npar, n_after = len(ws), len(after)
    sizes = [w.shape[1] for w in ws]
    offs, _ = _lane_offsets(sizes)

    def kern(g_ref, *refs):
        w_refs, m_refs, v_refs = refs[:npar], refs[npar:2 * npar], refs[2 * npar:3 * npar]
        outs = refs[3 * npar + n_after:]
        g_all = g_ref[0]
        for s in range(1, ns):
            g_all = g_all + g_ref[s]
        for j in range(npar):
            g = g_all[:, offs[j]:offs[j] + sizes[j]]
            delta, m_new, v_new = _adam_update(g, w_refs[j][...], m_refs[j][...], v_refs[j][...])
            outs[4 * j][...] = g
            outs[4 * j + 1][...] = delta
            outs[4 * j + 2][...] = m_new
            outs[4 * j + 3][...] = v_new

    vmem = pl.BlockSpec(memory_space=pltpu.VMEM)
    res = pl.pallas_call(
        kern,
        name=name,
        in_specs=[vmem] * (1 + 3 * npar) + [pl.BlockSpec(memory_space=pl.ANY)] * n_after,
        out_specs=[vmem] * (4 * npar),
        out_shape=[jax.ShapeDtypeStruct((1, n), F32) for n in sizes for _ in range(4)],
        compiler_params=pltpu.CompilerParams(vmem_limit_bytes=VMEM_LIMIT_BYTES),
    )(g_slots, *ws, *ms, *vs, *after)
    return [tuple(res[4 * j:4 * j + 4]) for j in range(npar)]


SMALL = ("c_ctx", "b_ada", "attn_sink", "ssm_a_re", "ssm_a_im", "ssm_log_dt", "ssm_b_re", "ssm_b_im", "ssm_c_re", "ssm_c_im",
         "ssm_d", "ln_mix_g", "ln_mix_b", "b_mlp1", "b_mlp2", "ln_mlp_g", "ln_mlp_b")
BIG = ("w_in", "w_glu", "w_attn_up", "w_ssm_up", "w_out", "w_mlp1", "w_mlp2")
BIG_KIND = ("col", "col", "col", "col", "row", "col", "row")
AG_GROUPS = (("w_in",), ("w_glu", "w_attn_up", "w_ssm_up", "w_out"), ("w_mlp1",), ("w_mlp2",))
AG_COLLECTIVE_ID0 = 1
RS_GROUPS = (("w_mlp2",), ("w_mlp1",), ("w_out", "w_attn_up", "w_ssm_up", "w_glu"), ("w_in",))
RS_COLLECTIVE_ID0 = AG_COLLECTIVE_ID0 + len(AG_GROUPS)
SMALL_EARLY = ("ssm_a_re", "ssm_a_im", "ssm_log_dt", "ssm_b_re", "ssm_b_im", "ssm_c_re", "ssm_c_im", "ssm_d")
SMALL_LATE = tuple(n for n in SMALL if n not in SMALL_EARLY)
SMALL_COLLECTIVE_ID0 = RS_COLLECTIVE_ID0 + 2 * len(RS_GROUPS)
LANES = 128


def _pack(parts):
    rows = []
    for p in parts:
        flat = p.reshape(-1).astype(F32)
        pad = (-flat.shape[0]) % LANES
        rows.append(jnp.pad(flat, (0, pad)).reshape(-1, LANES))
    packed = jnp.concatenate(rows, 0)
    return jnp.pad(packed, ((0, (-packed.shape[0]) % 8), (0, 0)))


def _unpack(packed, shapes):
    out, r0 = [], 0
    for s in shapes:
        n = math.prod(s)
        nr = -(-n // LANES)
        out.append(packed[r0:r0 + nr].reshape(-1)[:n].reshape(s))
        r0 += nr
    return out


WEIGHTS = ("c_ctx", "w_ada", "b_ada", "w_in", "attn_sink", "ssm_a_re", "ssm_a_im", "ssm_log_dt", "ssm_b_re", "ssm_b_im",
           "ssm_c_re", "ssm_c_im", "ssm_d", "w_glu", "w_attn_up", "w_ssm_up", "w_out", "ln_mix_g", "ln_mix_b", "w_mlp1",
           "b_mlp1", "w_mlp2", "b_mlp2", "ln_mlp_g", "ln_mlp_b")
ADA_COLS = 6 * D // N_DEV


def _step(x, c, ctx, loss_target, p, m, v):
    me = _lin(_my_pos())
    x2, ctx2, tgt2 = x[0], ctx[0], loss_target[0]

    wb = {}
    for gi, group in enumerate(AG_GROUPS):
        full = _allgather_weights_seq([p[n][0].astype(BF16) for n in group], [BIG_KIND[BIG.index(n)] for n in group],
                                      "allgather_seq%d" % gi, AG_COLLECTIVE_ID0 + gi)
        wb.update(zip(group, full))

    c_all = _allgather_small(jnp.broadcast_to(c, (8, D)), "gather_c")[:, 0, :]
    cc = p["c_ctx"].reshape(1, D)
    s_in = jnp.concatenate([c_all, cc, jnp.zeros((7, D), F32)], 0)
    s_act, = _rowwise(lambda rv, vv: ([_silu(rv[0])], []), [(s_in, D, 0, 0)], [], [(D, F32)], [], nrows=16, tr=16, name="silu_c")
    b_mine = lax.dynamic_slice_in_dim(p["b_ada"], me * ADA_COLS, ADA_COLS, axis=1)
    mod_part = _matmul(s_act, p["w_ada"][0], mode="nn", name="ada_fwd", tm=16, tn=512, bias=b_mine)
    mod_all = _allgather_small(mod_part, "gather_mod")
    mod_lat = lax.dynamic_index_in_dim(mod_all, me, axis=1, keepdims=False).reshape(1, 6 * D)
    mod_ctx = mod_all[:, 8, :].reshape(1, 6 * D)

    sp = {n: p[n][0] for n in SMALL if n not in ("c_ctx", "b_ada")}
    recv, halves = {}, {}

    def on_grad(gw):
        for gi, group in enumerate(RS_GROUPS):
            if gi not in halves and all(n in gw for n in group):
                kinds = [BIG_KIND[BIG.index(n)] for n in group]
                halves[gi] = (dict(gw), _pair_exchange_seq([gw[n] for n in group], kinds, "pair_exchange%d" % gi, RS_COLLECTIVE_ID0 + 2 * gi))

    def on_finish(key, after):
        gi = [i for i, group in enumerate(RS_GROUPS) if key in group][0]
        group = RS_GROUPS[gi]
        grads, half = halves[gi]
        prev = tuple(recv[n] for n in RS_GROUPS[gi - 1][:1]) if gi else ()
        if gi == len(RS_GROUPS) - 1:
            prev += (small["early"],)
        psums =[_pair_add(grads[n], h, BIG_KIND[BIG.index(n)], "pair_add_" + n, after=(after,) + prev) for n, h in zip(group, half)]
        recv.update(zip(group, _chip_exchange_seq(psums, "chip_exchange%d" % gi, RS_COLLECTIVE_ID0 + 2 * gi + 1)))
        return psums[-1]

    small = {}

    def on_early(gs_early):
        small["early"] = _allgather_small_seq(_pack([gs_early[n] for n in SMALL_EARLY]), "gather_small_early", SMALL_COLLECTIVE_ID0)

    total = {}

    def on_loss(loss_p):
        total["loss"] = lax.psum(loss_p[0, 0], ("x", "y", "c"))
        return total["loss"].reshape(1, 1)

    loss_p, grad_x, d_mod_lat, d_mod_ctx, gw, gs = _local_step(x2, ctx2, tgt2, mod_lat, mod_ctx, wb, sp, on_grad, on_loss, on_finish, on_early)

    g_early = small["early"]
    res = {}
    last = ()

    def adam_small(names, g_pack, tag, after):
        sm = _adam(g_pack, _pack([p[n] for n in names]), _pack([m[n] for n in names]), _pack([v[n] for n in names]),
                   tr=g_pack.shape[1], name="adam_small_" + tag, after=after)
        shapes = [p[n].shape for n in names]
        for j, outs in enumerate(zip(*[_unpack(a, shapes) for a in sm])):
            res[names[j]] = outs
        return (sm[0],)

    for gi, group in enumerate(RS_GROUPS):
        if gi == len(RS_GROUPS) - 1:
            last = adam_small(SMALL_EARLY, g_early, "early", last)
        for n in group:
            res[n] = _adam(recv[n], p[n][0], m[n][0], v[n][0], tr=256, name="adam_" + n, after=last)
            last = (res[n][0],)

    dm = jnp.concatenate([d_mod_lat, d_mod_ctx, jnp.zeros((6, 6 * D), F32)], 0)
    dm_all = _allgather_small_seq(dm, "gather_dmod", SMALL_COLLECTIVE_ID0 + 1)
    dm_all = lax.optimization_barrier((dm_all,) + last)[0]
    dm2 = jnp.concatenate([dm_all[:, 0, :], dm_all[:, 1, :]], 0)
    dm2_mine = lax.dynamic_slice_in_dim(dm2, me * ADA_COLS, ADA_COLS, axis=1)
    s2 = jnp.concatenate([s_act[0:8], jnp.broadcast_to(s_act[8:9], (8, D))], 0)
    g_w_ada = _matmul(s2, dm2_mine, mode="tn", name="dw_ada", tm=512, tn=ADA_COLS, after=last)
    dsc_part = _matmul(dm2_mine[8:16], p["w_ada"][0], mode="nt", name="d_silu_cctx", tm=8, tn=512, after=last)

    def cctx_b(rv, vv):
        _, pull = jax.vjp(_silu, vv[0])
        return [], [pull(jnp.sum(rv[0], axis=0, keepdims=True))[0]]

    g_cctx, = _rowwise(cctx_b, [(dsc_part, D, 0, 0)], [cc], [], [(1, D)], nrows=8, tr=8, name="cctx_bwd")
    gs["c_ctx"] = g_cctx
    gs["b_ada"] = d_mod_lat + d_mod_ctx

    res["w_ada"] = _adam(g_w_ada[None], p["w_ada"][0], m["w_ada"][0], v["w_ada"][0], tr=256, name="adam_w_ada")

    g_late = _allgather_small_seq(_pack_lanes([gs[n] for n in SMALL_LATE]), "gather_small_late", SMALL_COLLECTIVE_ID0 + 2)
    row = lambda a: a.reshape(1, -1)
    late = _adam_lanes(g_late, [row(p[n]) for n in SMALL_LATE], [row(m[n]) for n in SMALL_LATE], [row(v[n]) for n in SMALL_LATE],
                       name="adam_small_late", after=(res["w_ada"][0],))
    res.update(zip(SMALL_LATE, late))

    outs = [total["loss"], grad_x[None]]
    for j in range(4):
        outs += [res[n][j].reshape(p[n].shape) for n in WEIGHTS]
    return tuple(outs)


def kernel(x, c, ctx, c_ctx, w_ada, b_ada, w_in, attn_sink, ssm_a_re, ssm_a_im, ssm_log_dt, ssm_b_re, ssm_b_im, ssm_c_re, ssm_c_im, ssm_d, w_glu, w_attn_up, w_ssm_up, w_out, ln_mix_g, ln_mix_b, w_mlp1, b_mlp1, w_mlp2, b_mlp2, ln_mlp_g, ln_mlp_b, loss_target, m_c_ctx, m_w_ada, m_b_ada, m_w_in, m_attn_sink, m_ssm_a_re, m_ssm_a_im, m_ssm_log_dt, m_ssm_b_re, m_ssm_b_im, m_ssm_c_re, m_ssm_c_im, m_ssm_d, m_w_glu, m_w_attn_up, m_w_ssm_up, m_w_out, m_ln_mix_g, m_ln_mix_b, m_w_mlp1, m_b_mlp1, m_w_mlp2, m_b_mlp2, m_ln_mlp_g, m_ln_mlp_b, v_c_ctx, v_w_ada, v_b_ada, v_w_in, v_attn_sink, v_ssm_a_re, v_ssm_a_im, v_ssm_log_dt, v_ssm_b_re, v_ssm_b_im, v_ssm_c_re, v_ssm_c_im, v_ssm_d, v_w_glu, v_w_attn_up, v_w_ssm_up, v_w_out, v_ln_mix_g, v_ln_mix_b, v_w_mlp1, v_b_mlp1, v_w_mlp2, v_b_mlp2, v_ln_mlp_g, v_ln_mlp_b):
    given = dict(locals())
    p = {n: given[n] for n in WEIGHTS}
    m = {n: given["m_" + n] for n in WEIGHTS}
    v = {n: given["v_" + n] for n in WEIGHTS}
    return _step(x, c, ctx, loss_target, p, m, v)
```

```python
import functools
import math

import jax
import jax.numpy as jnp
from jax import lax
from jax.experimental import pallas as pl
from jax.experimental.pallas import tpu as pltpu
from jax.experimental.pallas import tpu_sc as plsc

F32 = jnp.float32
BF16 = jnp.bfloat16

N_DEV = 8
D = 2048
T = 2048
C = 256
TA = T + C
GRID_W = 64
HD = 128
NH = 8
NKV = 2
GROUP = NH // NKV
WINDOW = 128
QW = NH * HD
KVW = NKV * HD
SW = D // 4
SG = 16
NG = SW // SG
SP = 64
DFF = 4 * D
IN_COLS = QW + 2 * KVW + SW + 2 * D
ALPHA = 2.0 ** 0.25
LN_EPS = 1e-6
NEG_INF = -1e30
ROPE_BASE = 10000.0
ATT_SCALE = HD ** -0.5

NSEG = 8
GBLK = 8
NBLK = NG // GBLK
BW = GBLK * SP
UW = GBLK * SG

ADAM_LR = 0.001
ADAM_B1 = 0.9
ADAM_B2 = 0.999
ADAM_EPS = 1e-08
ADAM_WD = 0.01
ADAM_STEP = 10

VMEM_LIMIT_BYTES = 56 * 1024 * 1024
MESH = pl.DeviceIdType.MESH


def _cparams(sem=None):
    return pltpu.CompilerParams(dimension_semantics=sem, vmem_limit_bytes=VMEM_LIMIT_BYTES)


def _matmul(a, b, *, mode, name, out_dtypes=(F32,), tm=512, tn=512, tk=None, bias=None, extras=(), epilogue=None, after=(),
            out_t=None, colsum=False):
    if mode == "nn":
        (M, K), (K2, N) = a.shape, b.shape
    elif mode == "nt":
        (M, K), (N, K2) = a.shape, b.shape
    else:
        (K, M), (K2, N) = a.shape, b.shape
    assert K == K2, (name, a.shape, b.shape)
    tm, tn, tk = min(tm, M), min(tn, N), min(tk or K, K)
    assert M % tm == 0 and N % tn == 0 and K % tk == 0, (name, M, N, K, tm, tn, tk)
    nk = K // tk
    if mode == "tn":
        a_spec = pl.BlockSpec((tk, tm), lambda i, j, k: (k, i))
    else:
        a_spec = pl.BlockSpec((tm, tk), lambda i, j, k: (i, k))
    if mode == "nt":
        b_spec = pl.BlockSpec((tn, tk), lambda i, j, k: (j, k))
    else:
        b_spec = pl.BlockSpec((tk, tn), lambda i, j, k: (k, j))
    dims = {"nn": (((1,), (0,)), ((), ())), "nt": (((1,), (1,)), ((), ())), "tn": (((0,), (0,)), ((), ()))}[mode]
    in_specs = [a_spec, b_spec]
    operands = [a, b]
    if bias is not None:
        in_specs.append(pl.BlockSpec((1, tn), lambda i, j, k: (0, j)))
        operands.append(bias)
    for e in extras:
        in_specs.append(pl.BlockSpec((tm, tn), lambda i, j, k: (i, j)))
        operands.append(e)
    n_ex = len(extras)
    for t in after:
        in_specs.append(pl.BlockSpec(memory_space=pl.ANY))
        operands.append(t)
    n_after = len(after)
    n_out = len(out_dtypes)
    out_t = tuple(out_t) if out_t is not None else (False,) * n_out
    has_bias = bias is not None

    def kern(*refs):
        a_ref, b_ref = refs[0], refs[1]
        pos = 2
        bias_ref = None
        if has_bias:
            bias_ref = refs[pos]
            pos += 1
        ex_refs = refs[pos:pos + n_ex]
        pos += n_ex + n_after
        out_refs = refs[pos:pos + n_out]
        cs_ref = refs[pos + n_out] if colsum else None
        acc_ref = refs[pos + n_out + int(colsum)] if nk > 1 else None

        def finish(r):
            if has_bias:
                r = r + bias_ref[...]
            outs = epilogue(r, *[e[...] for e in ex_refs]) if epilogue is not None else (r,)
            for o_ref, o, tr_ in zip(out_refs, outs, out_t):
                o_ref[...] = (o.T if tr_ else o).astype(o_ref.dtype)
            if colsum:
                cs_ref[0] = jnp.sum(outs[0].astype(F32), axis=0, keepdims=True)

        part = lax.dot_general(a_ref[...].astype(BF16), b_ref[...].astype(BF16), dims, preferred_element_type=F32)
        if nk == 1:
            finish(part)
        else:
            k = pl.program_id(2)

            @pl.when(k == 0)
            def _():
                acc_ref[...] = part

            @pl.when(k > 0)
            def _():
                acc_ref[...] += part

            @pl.when(k == nk - 1)
            def _():
                finish(acc_ref[...])

    outs = pl.pallas_call(
        kern,
        name=name,
        grid=(M // tm, N // tn, nk),
        in_specs=in_specs,
        out_specs=[pl.BlockSpec((tn, tm), lambda i, j, k: (j, i)) if tr_ else pl.BlockSpec((tm, tn), lambda i, j, k: (i, j))
                   for tr_ in out_t] + ([pl.BlockSpec((1, 1, tn), lambda i, j, k: (i, 0, j))] if colsum else []),
        out_shape=[jax.ShapeDtypeStruct((N, M) if tr_ else (M, N), dt) for dt, tr_ in zip(out_dtypes, out_t)]
        + ([jax.ShapeDtypeStruct((M // tm, 1, N), F32)] if colsum else []),
        scratch_shapes=[pltpu.VMEM((tm, tn), F32)] if nk > 1 else [],
        compiler_params=_cparams(("parallel", "parallel", "arbitrary")),
    )(*operands)
    return outs[0] if len(outs) == 1 else tuple(outs)


def _rowwise(fn, rows, vecs, outs, vec_outs, *, nrows, tr, name, after=()):
    n_rows, n_vecs, n_outs, n_after = len(rows), len(vecs), len(outs), len(after)
    in_specs = [pl.BlockSpec((tr, w), lambda i, cb=cb, ro=ro: (i + ro, cb)) for (_, w, cb, ro) in rows]
    in_specs += [pl.BlockSpec(v.shape, lambda i: (0, 0)) for v in vecs]
    in_specs += [pl.BlockSpec(memory_space=pl.ANY)] * n_after
    outs = [o if len(o) == 3 else (*o, False) for o in outs]
    out_specs = [pl.BlockSpec((w, tr), lambda i: (0, i)) if tr_ else pl.BlockSpec((tr, w), lambda i: (i, 0)) for (w, _, tr_) in outs]
    out_specs += [pl.BlockSpec(s, lambda i: (0, 0)) for s in vec_outs]
    out_shape = [jax.ShapeDtypeStruct((w, nrows) if tr_ else (nrows, w), dt) for (w, dt, tr_) in outs]
    out_tr = [tr_ for (_, _, tr_) in outs]
    out_shape += [jax.ShapeDtypeStruct(s, F32) for s in vec_outs]

    def kern(*refs):
        rvals = [r[...].astype(F32) for r in refs[:n_rows]]
        vvals = [r[...] for r in refs[n_rows:n_rows + n_vecs]]
        first_out = n_rows + n_vecs + n_after
        o_refs = refs[first_out:first_out + n_outs]
        v_refs = refs[first_out + n_outs:]
        ro, vo = fn(rvals, vvals)
        for r, val, tr_ in zip(o_refs, ro, out_tr):
            r[...] = (val.astype(F32).T if tr_ else val).astype(r.dtype)
        i = pl.program_id(0)
        for r, val in zip(v_refs, vo):
            @pl.when(i == 0)
            def _(r=r, val=val):
                r[...] = val.astype(F32)

            @pl.when(i > 0)
            def _(r=r, val=val):
                r[...] += val.astype(F32)

    res = pl.pallas_call(
        kern,
        name=name,
        grid=(nrows // tr,),
        in_specs=in_specs,
        out_specs=out_specs,
        out_shape=out_shape,
        compiler_params=_cparams(("arbitrary",)),
    )(*[r[0] for r in rows], *vecs, *after)
    return list(res)


def _ln(x):
    mu = jnp.mean(x, axis=-1, keepdims=True)
    xc = x - mu
    var = jnp.mean(xc * xc, axis=-1, keepdims=True)
    return xc * lax.rsqrt(var + LN_EPS)


def _sigmoid(x):
    return 1.0 / (1.0 + jnp.exp(-x))


def _gelu(x):
    return 0.5 * x * (1.0 + jnp.tanh(math.sqrt(2.0 / math.pi) * (x + 0.044715 * (x * x * x))))


def _silu(x):
    return x * _sigmoid(x)


def _f_ln_mod(x, sc, sh):
    return _ln(x) * (1.0 + sc) + sh


def _f_glu(z):
    return z[:, :SW] * _sigmoid(z[:, SW:])


def _f_mix(ga, gs, attn_d, ssm_d):
    return _sigmoid(ga) * attn_d + _sigmoid(gs) * ssm_d


def _f_post1(x, y, g1, lg, lb, sc2, sh2):
    r1 = ALPHA * x + g1 * y
    x1 = _ln(r1) * lg + lb
    h2 = _ln(x1) * (1.0 + sc2) + sh2
    return x1, h2


def _f_loss(x1, mlp, tgt, g2, lg, lb, b2z):
    r2 = ALPHA * x1 + g2 * (mlp + b2z)
    out = _ln(r2) * lg + lb
    err = out - tgt
    return 0.5 * jnp.sum(err * err) * (1.0 / D)


def _rope_tables():
    rows = T // GRID_W
    row = jnp.repeat(jnp.arange(rows), GRID_W)
    col = jnp.tile(jnp.arange(GRID_W), rows)
    n_freq = HD // 4
    freqs = ROPE_BASE ** (-jnp.arange(n_freq, dtype=F32) / n_freq)
    ang_r = row.astype(F32)[:, None] * freqs
    ang_c = col.astype(F32)[:, None] * freqs
    ang = jnp.concatenate([ang_r, ang_r, ang_c, ang_c], -1)
    cos, sin = jnp.cos(ang), jnp.sin(ang)
    lo = (jnp.arange(HD) % (HD // 2)) < (HD // 4)
    sin_a = jnp.where(lo[None, :], -sin, 0.0)
    sin_b = jnp.where(lo[None, :], 0.0, sin)
    return cos, sin_a, sin_b


def _rope(x, cos, sa, sb):
    return x * cos + pltpu.roll(x, 96, 1) * sa + pltpu.roll(x, 32, 1) * sb


def _rope_t(dy, cos, sa, sb):
    return dy * cos + pltpu.roll(dy * sa, 32, 1) + pltpu.roll(dy * sb, 96, 1)


BAND = 3 * WINDOW
KPAD = T + 2 * WINDOW


def _attn_fill_kv(k_ref, v_ref, cos_ref, sa_ref, sb_ref, kp, vp, kc, vc):
    zeros = jnp.zeros((WINDOW, KVW), BF16)
    kp[0:WINDOW, :] = zeros
    kp[WINDOW + T:KPAD, :] = zeros
    vp[0:WINDOW, :] = zeros
    vp[WINDOW + T:KPAD, :] = zeros
    for hh in range(NKV):
        cs = slice(hh * HD, (hh + 1) * HD)
        for r0 in range(0, T, 512):
            rs = slice(r0, r0 + 512)
            kr = _rope(k_ref[rs, cs], cos_ref[rs, :], sa_ref[rs, :], sb_ref[rs, :])
            kp[WINDOW + r0:WINDOW + r0 + 512, cs] = kr.astype(BF16)
    vp[WINDOW:WINDOW + T, :] = v_ref[0:T, :].astype(BF16)
    kc[...] = k_ref[T:TA, :].astype(BF16)
    vc[...] = v_ref[T:TA, :].astype(BF16)


GROWS = GROUP * WINDOW


def _attn_scores(n, kvh, q_ref, cos_ref, sa_ref, sb_ref, sink_ref, kp, kc):
    r0 = pl.multiple_of(n * WINDOW, WINDOW)
    cos = cos_ref[pl.ds(r0, WINDOW), :]
    sa = sa_ref[pl.ds(r0, WINDOW), :]
    sb = sb_ref[pl.ds(r0, WINDOW), :]
    heads = range(kvh * GROUP, (kvh + 1) * GROUP)
    q_g = jnp.concatenate([_rope(q_ref[:, h * HD:(h + 1) * HD], cos, sa, sb).astype(BF16) for h in heads], axis=0)
    kb = kp[pl.ds(r0, BAND), kvh * HD:(kvh + 1) * HD]
    kcb = kc[:, kvh * HD:(kvh + 1) * HD]
    nt = (((1,), (1,)), ((), ()))
    s_loc = lax.dot_general(q_g, kb, nt, preferred_element_type=F32) * ATT_SCALE
    s_ctx = lax.dot_general(q_g, kcb, nt, preferred_element_type=F32) * ATT_SCALE
    row = lax.broadcasted_iota(jnp.int32, (GROWS, BAND), 0) & (WINDOW - 1)
    col = lax.broadcasted_iota(jnp.int32, (GROWS, BAND), 1)
    rel = col - WINDOW - row
    kpos = r0 - WINDOW + col
    valid = (jnp.abs(rel) <= WINDOW) & (kpos >= 0) & (kpos < T)
    s_loc = jnp.where(valid, s_loc, NEG_INF)
    sk = jnp.concatenate([jnp.broadcast_to(sink_ref[0:1, h:h + 1], (WINDOW, 1)) for h in heads], axis=0)
    m = jnp.maximum(jnp.maximum(jnp.max(s_loc, -1, keepdims=True), jnp.max(s_ctx, -1, keepdims=True)), sk)
    e_loc = jnp.exp(s_loc - m)
    e_ctx = jnp.exp(s_ctx - m)
    e_sink = jnp.exp(sk - m)
    inv = 1.0 / (jnp.sum(e_loc, -1, keepdims=True) + jnp.sum(e_ctx, -1, keepdims=True) + e_sink)
    return q_g, r0, e_loc * inv, e_ctx * inv, e_sink * inv


def _attn_fwd(proj, sink, tabs):
    cos, sa, sb = tabs

    def kern(q_ref, k_ref, v_ref, cos_ref, sa_ref, sb_ref, sink_ref, o_ref, kp, vp, kc, vc):
        n = pl.program_id(0)

        @pl.when(n == 0)
        def _():
            _attn_fill_kv(k_ref, v_ref, cos_ref, sa_ref, sb_ref, kp, vp, kc, vc)

        for kvh in range(NKV):
            _, r0, p_loc, p_ctx, _ = _attn_scores(n, kvh, q_ref, cos_ref, sa_ref, sb_ref, sink_ref, kp, kc)
            vb = vp[pl.ds(r0, BAND), kvh * HD:(kvh + 1) * HD]
            vcb = vc[:, kvh * HD:(kvh + 1) * HD]
            o = jnp.dot(p_loc.astype(BF16), vb, preferred_element_type=F32)
            o = o + jnp.dot(p_ctx.astype(BF16), vcb, preferred_element_type=F32)
            for g in range(GROUP):
                h = kvh * GROUP + g
                o_ref[:, h * HD:(h + 1) * HD] = o[g * WINDOW:(g + 1) * WINDOW, :].astype(o_ref.dtype)

    full = lambda shape: pl.BlockSpec(shape, lambda n: (0, 0))
    return pl.pallas_call(
        kern,
        name="attn_fwd",
        grid=(T // WINDOW,),
        in_specs=[
            pl.BlockSpec((WINDOW, QW), lambda n: (n, 0)),
            pl.BlockSpec((TA, KVW), lambda n: (0, QW // KVW)),
            pl.BlockSpec((TA, KVW), lambda n: (0, QW // KVW + 1)),
            full((T, HD)), full((T, HD)), full((T, HD)), full((1, NH)),
        ],
        out_specs=pl.BlockSpec((WINDOW, QW), lambda n: (n, 0)),
        out_shape=jax.ShapeDtypeStruct((T, QW), BF16),
        scratch_shapes=[pltpu.VMEM((KPAD, KVW), BF16), pltpu.VMEM((KPAD, KVW), BF16),
                        pltpu.VMEM((C, KVW), BF16), pltpu.VMEM((C, KVW), BF16)],
        compiler_params=_cparams(("arbitrary",)),
    )(proj, proj, proj, cos, sa, sb, sink)


def _attn_bwd(proj, d_attn, sink, tabs):
    cos, sa, sb = tabs
    n_blocks = T // WINDOW

    def kern(q_ref, k_ref, v_ref, do_ref, cos_ref, sa_ref, sb_ref, sink_ref,
             dq_ref, dk_ref, dv_ref, dsink_ref, kp, vp, kc, vc, dkp, dvp, dkc, dvc):
        n = pl.program_id(0)

        @pl.when(n == 0)
        def _():
            _attn_fill_kv(k_ref, v_ref, cos_ref, sa_ref, sb_ref, kp, vp, kc, vc)
            dkp[...] = jnp.zeros_like(dkp)
            dvp[...] = jnp.zeros_like(dvp)
            dkc[...] = jnp.zeros_like(dkc)
            dvc[...] = jnp.zeros_like(dvc)
            dsink_ref[...] = jnp.zeros_like(dsink_ref)

        nt = (((1,), (1,)), ((), ()))
        tn = (((0,), (0,)), ((), ()))
        for kvh in range(NKV):
            cs = slice(kvh * HD, (kvh + 1) * HD)
            heads = range(kvh * GROUP, (kvh + 1) * GROUP)
            q_g, r0, p_loc, p_ctx, p_sink = _attn_scores(n, kvh, q_ref, cos_ref, sa_ref, sb_ref, sink_ref, kp, kc)
            kb = kp[pl.ds(r0, BAND), cs]
            vb = vp[pl.ds(r0, BAND), cs]
            kcb = kc[:, cs]
            vcb = vc[:, cs]
            do_g = jnp.concatenate([do_ref[:, h * HD:(h + 1) * HD] for h in heads], axis=0)
            dp_loc = lax.dot_general(do_g, vb, nt, preferred_element_type=F32)
            dp_ctx = lax.dot_general(do_g, vcb, nt, preferred_element_type=F32)
            delta = jnp.sum(p_loc * dp_loc, -1, keepdims=True) + jnp.sum(p_ctx * dp_ctx, -1, keepdims=True)
            ds_loc = (p_loc * (dp_loc - delta) * ATT_SCALE).astype(BF16)
            ds_ctx = (p_ctx * (dp_ctx - delta) * ATT_SCALE).astype(BF16)
            dq = jnp.dot(ds_loc, kb, preferred_element_type=F32) + jnp.dot(ds_ctx, kcb, preferred_element_type=F32)
            cos = cos_ref[pl.ds(r0, WINDOW), :]
            sa_ = sa_ref[pl.ds(r0, WINDOW), :]
            sb_ = sb_ref[pl.ds(r0, WINDOW), :]
            dkp[pl.ds(r0, BAND), cs] += lax.dot_general(ds_loc, q_g, tn, preferred_element_type=F32)
            dkc[:, cs] += lax.dot_general(ds_ctx, q_g, tn, preferred_element_type=F32)
            dvp[pl.ds(r0, BAND), cs] += lax.dot_general(p_loc.astype(BF16), do_g, tn, preferred_element_type=F32)
            dvc[:, cs] += lax.dot_general(p_ctx.astype(BF16), do_g, tn, preferred_element_type=F32)
            dsk_rows = p_sink * delta
            for g, h in enumerate(heads):
                rs = slice(g * WINDOW, (g + 1) * WINDOW)
                dq_ref[:, h * HD:(h + 1) * HD] = _rope_t(dq[rs, :], cos, sa_, sb_).astype(dq_ref.dtype)
                dsk = -jnp.sum(dsk_rows[rs, :], axis=0, keepdims=True)
                dsink_ref[h:h + 1, :] += jnp.broadcast_to(dsk, (1, HD))

        @pl.when(n == n_blocks - 1)
        def _():
            for hh in range(NKV):
                cs = slice(hh * HD, (hh + 1) * HD)
                for r0 in range(0, T, 512):
                    rs = slice(r0, r0 + 512)
                    g = dkp[WINDOW + r0:WINDOW + r0 + 512, cs]
                    dk_ref[rs, cs] = _rope_t(g, cos_ref[rs, :], sa_ref[rs, :], sb_ref[rs, :]).astype(dk_ref.dtype)
            dk_ref[T:TA, :] = dkc[...].astype(dk_ref.dtype)
            dv_ref[0:T, :] = dvp[WINDOW:WINDOW + T, :].astype(dv_ref.dtype)
            dv_ref[T:TA, :] = dvc[...].astype(dv_ref.dtype)

    full = lambda shape: pl.BlockSpec(shape, lambda n: (0, 0))
    return pl.pallas_call(
        kern,
        name="attn_bwd",
        grid=(n_blocks,),
        in_specs=[
            pl.BlockSpec((WINDOW, QW), lambda n: (n, 0)),
            pl.BlockSpec((TA, KVW), lambda n: (0, QW // KVW)),
            pl.BlockSpec((TA, KVW), lambda n: (0, QW // KVW + 1)),
            pl.BlockSpec((WINDOW, QW), lambda n: (n, 0)),
            full((T, HD)), full((T, HD)), full((T, HD)), full((1, NH)),
        ],
        out_specs=[pl.BlockSpec((WINDOW, QW), lambda n: (n, 0)), full((TA, KVW)), full((TA, KVW)), full((NH, HD))],
        out_shape=[jax.ShapeDtypeStruct((T, QW), BF16), jax.ShapeDtypeStruct((TA, KVW), BF16),
                   jax.ShapeDtypeStruct((TA, KVW), BF16), jax.ShapeDtypeStruct((NH, HD), F32)],
        scratch_shapes=[pltpu.VMEM((KPAD, KVW), BF16), pltpu.VMEM((KPAD, KVW), BF16),
                        pltpu.VMEM((C, KVW), BF16), pltpu.VMEM((C, KVW), BF16),
                        pltpu.VMEM((KPAD, KVW), F32), pltpu.VMEM((KPAD, KVW), F32),
                        pltpu.VMEM((C, KVW), F32), pltpu.VMEM((C, KVW), F32)],
        compiler_params=_cparams(("arbitrary",)),
    )(proj, proj, proj, d_attn, cos, sa, sb, sink)


def _s5_prep(a_re, a_im, log_dt, b_re, b_im, c_re, c_im):
    lam = lax.complex(a_re, a_im)
    dt = jnp.exp(log_dt)[..., None]
    lam_bar = jnp.exp(lam * dt)
    b_bar = ((lam_bar - 1.0) / lam)[..., None] * lax.complex(b_re, b_im)
    def lam_rows(v):
        return v.reshape(2, NBLK, 1, BW)

    lam_l = jnp.concatenate([lam_rows(jnp.real(lam_bar)), lam_rows(jnp.imag(lam_bar))], -1)
    lam_l = jnp.broadcast_to(lam_l, (2, NBLK, 8, 2 * BW))
    diag = (jnp.arange(UW)[:, None] // SG) == (jnp.arange(BW)[None, :] // SP)

    def blocks(v):
        return jnp.where(diag, jnp.tile(v.reshape(2, NBLK, UW, SP), (1, 1, 1, GBLK)), 0.0)

    b_t = jnp.swapaxes(b_bar, -1, -2)
    bmat = jnp.concatenate([blocks(jnp.real(b_t)), blocks(jnp.imag(b_t))], -1)
    cmat = jnp.concatenate([blocks(c_re), -blocks(c_im)], -1)
    return lam_l, bmat, cmat


def _cmul(ar, ai, br, bi):
    return ar * br - ai * bi, ar * bi + ai * br


def _shift_rows(x, rev, fill):
    r = lax.broadcasted_iota(jnp.int32, x.shape, 0)
    down = jnp.where(r == 0, fill, pltpu.roll(x, 1, 0))
    up = jnp.where(r == NSEG - 1, fill, pltpu.roll(x, NSEG - 1, 0))
    return jnp.where(rev == 0, down, up)


def _edge_row(x, rev):
    last = jnp.broadcast_to(x[NSEG - 1:NSEG, :], x.shape)
    first = jnp.broadcast_to(x[0:1, :], x.shape)
    return jnp.where(rev == 0, last, first)


def _seg_scan(get, put, base, seglen, lr, li, rev, cin, acc_fn=None, acc0=()):
    zero = jnp.zeros((NSEG, BW), F32)

    def rows(k):
        j = jnp.where(rev == 0, k, seglen - 1 - k)
        return pl.ds(pl.multiple_of(base + j * NSEG, NSEG), NSEG)

    def local(k, carry):
        sr, si = carry
        xr, xi = get(rows(k))
        tr, ti = _cmul(lr, li, sr, si)
        sr, si = tr + xr, ti + xi
        put(rows(k), sr, si)
        return sr, si

    er, ei = lax.fori_loop(0, seglen, local, (zero, zero))
    lpr, lpi = lr, li
    assert seglen & (seglen - 1) == 0, seglen
    for _ in range(seglen.bit_length() - 1):
        lpr, lpi = _cmul(lpr, lpi, lpr, lpi)
    cr, ci = _shift_rows(zero, rev, cin[0]), _shift_rows(zero, rev, cin[1])
    for _ in range(NSEG - 1):
        tr, ti = _cmul(lpr, lpi, cr, ci)
        cr, ci = _shift_rows(er + tr, rev, cin[0]), _shift_rows(ei + ti, rev, cin[1])

    def fix(k, carry):
        tr, ti = _cmul(lr, li, carry[0], carry[1])
        xr, xi = get(rows(k))
        fr, fi = xr + tr, xi + ti
        put(rows(k), fr, fi)
        if acc_fn is None:
            return tr, ti
        j = jnp.where(rev == 0, k, seglen - 1 - k)
        return (tr, ti) + tuple(acc_fn(j, fr, fi, carry[2:]))

    out = lax.fori_loop(0, seglen, fix, (cr, ci) + tuple(acc0))
    tr, ti = out[0], out[1]
    leaving = (_edge_row(er + tr, rev), _edge_row(ei + ti, rev))
    return leaving if acc_fn is None else (leaving, out[2:])


RCH = 256
CSEG = C // NSEG
TSEG = T // NSEG
UCOL0 = (QW + 2 * KVW) // UW


REGIONS = ((0, TSEG), (T, CSEG))


def _state_access(ref, lead=()):
    def get(rows):
        return ref[(*lead, rows, slice(0, BW))], ref[(*lead, rows, slice(BW, 2 * BW))]

    def put(rows, re, im):
        ref[(*lead, rows, slice(0, BW))] = re
        ref[(*lead, rows, slice(BW, 2 * BW))] = im

    return get, put


def _interleave_rows(src_ref, dst_ref, regions=REGIONS):
    for base, seglen in regions:
        def body(j, carry, base=base, seglen=seglen):
            dst_ref[pl.ds(pl.multiple_of(base + j * NSEG, NSEG), NSEG), :] = src_ref[pl.ds(base + j, NSEG, stride=seglen), :]
            return carry

        lax.fori_loop(0, seglen, body, 0, unroll=8)


def _deinterleave_rows(src_ref, dst_ref, regions=REGIONS):
    for base, seglen in regions:
        def body(j, carry, base=base, seglen=seglen):
            dst_ref[pl.ds(base + j, NSEG, stride=seglen), :] = src_ref[pl.ds(pl.multiple_of(base + j * NSEG, NSEG), NSEG), :]
            return carry

        lax.fori_loop(0, seglen, body, 0, unroll=8)


def _s5_fwd(proj, dskip, lam, bmat, cmat):
    def kern(u_ref, dk_ref, lam_ref, b_ref, c_ref, s_ref, ssm_ref, ge_ref, up_ref, yp_ref):
        d = pl.program_id(1)

        @pl.when(d == 0)
        def _():
            _interleave_rows(u_ref, up_ref)

        bm = b_ref[0, 0].astype(BF16)
        for r0 in range(0, TA, RCH):
            s_ref[0, 0, r0:r0 + RCH, :] = jnp.dot(up_ref[r0:r0 + RCH, :].astype(BF16), bm, preferred_element_type=F32)
        lr = lam_ref[0, 0, :, 0:BW]
        li = lam_ref[0, 0, :, BW:2 * BW]
        zero = jnp.zeros((NSEG, BW), F32)
        get, put = _state_access(s_ref, (0, 0))
        mid = _seg_scan(get, put, T, CSEG, lr, li, d, (zero, zero))
        _seg_scan(get, put, 0, TSEG, lr, li, d, mid)
        cm = c_ref[0, 0].astype(BF16)
        for r0 in range(0, T, RCH):
            y = lax.dot_general(s_ref[0, 0, r0:r0 + RCH, :].astype(BF16), cm, (((1,), (1,)), ((), ())), preferred_element_type=F32)

            @pl.when(d == 0)
            def _(y=y, r0=r0):
                yp_ref[r0:r0 + RCH, :] = y + dk_ref[...] * up_ref[r0:r0 + RCH, :]

            @pl.when(d == 1)
            def _(y=y, r0=r0):
                yp_ref[r0:r0 + RCH, :] += y

        @pl.when(d == 1)
        def _():
            _deinterleave_rows(yp_ref, ssm_ref, REGIONS[:1])
            for r0 in range(0, T, RCH):
                ge_ref[r0:r0 + RCH, :] = _gelu(ssm_ref[r0:r0 + RCH, :]).astype(ge_ref.dtype)

    blk4 = lambda shape: pl.BlockSpec((1, 1) + shape, lambda b, d: (d, b, 0, 0))
    return pl.pallas_call(
        kern,
        name="s5_fwd",
        grid=(NBLK, 2),
        in_specs=[pl.BlockSpec((TA, UW), lambda b, d: (0, UCOL0 + b)), pl.BlockSpec((1, UW), lambda b, d: (0, b)),
                  blk4((8, 2 * BW)), blk4((UW, 2 * BW)), blk4((UW, 2 * BW))],
        out_specs=[blk4((TA, 2 * BW)), pl.BlockSpec((T, UW), lambda b, d: (0, b)), pl.BlockSpec((T, UW), lambda b, d: (0, b))],
        out_shape=[jax.ShapeDtypeStruct((2, NBLK, TA, 2 * BW), F32), jax.ShapeDtypeStruct((T, SW), F32),
                   jax.ShapeDtypeStruct((T, SW), BF16)],
        scratch_shapes=[pltpu.VMEM((TA, UW), F32), pltpu.VMEM((T, UW), F32)],
        compiler_params=_cparams(("parallel", "arbitrary")),
    )(proj, dskip, lam, bmat, cmat)


def _s5_bwd(d_ge, ssm, proj, dskip, states, lam, bmat, cmat):
    nt = (((1,), (1,)), ((), ()))
    tn = (((0,), (0,)), ((), ()))

    def kern(dge_ref, ssm_ref, u_ref, dk_ref, s_ref, lam_ref, b_ref, c_ref,
             du_ref, ddk_ref, dlam_ref, db_ref, dc_ref, g_ref, dua_ref, dssm_ref, up_ref, nat_ref):
        d = pl.program_id(1)

        @pl.when(d == 0)
        def _():
            ddk = jnp.zeros((1, UW), F32)
            for r0 in range(0, T, RCH):
                rs = slice(r0, r0 + RCH)
                _, pull = jax.vjp(_gelu, ssm_ref[rs, :])
                dssm = pull(dge_ref[rs, :])[0]
                nat_ref[rs, :] = dssm
                ddk = ddk + jnp.sum(dssm * u_ref[rs, :], axis=0, keepdims=True)
            ddk_ref[...] = ddk
            _interleave_rows(nat_ref, dssm_ref, REGIONS[:1])
            _interleave_rows(u_ref, up_ref)
            for r0 in range(0, T, RCH):
                dua_ref[r0:r0 + RCH, :] = dssm_ref[r0:r0 + RCH, :] * dk_ref[...]
            dua_ref[T:TA, :] = jnp.zeros((C, UW), F32)

        cm = c_ref[0, 0].astype(BF16)
        for r0 in range(0, T, RCH):
            g_ref[r0:r0 + RCH, :] = jnp.dot(dssm_ref[r0:r0 + RCH, :].astype(BF16), cm, preferred_element_type=F32)
        g_ref[T:TA, :] = jnp.zeros((C, 2 * BW), F32)
        lr = lam_ref[0, 0, :, 0:BW]
        li = lam_ref[0, 0, :, BW:2 * BW]
        zero = jnp.zeros((NSEG, BW), F32)
        get_g, put_g = _state_access(g_ref)

        get_s, _ = _state_access(s_ref, (0, 0))

        def dlam_fold(base, seglen, s_in):
            def rows(j):
                return pl.ds(pl.multiple_of(base + j * NSEG, NSEG), NSEG)

            jb = jnp.where(d == 0, 0, seglen - 1)
            jn = jnp.where(d == 0, seglen - 1, 0)
            sp = get_s(rows(jn))
            edge = (_shift_rows(sp[0], d, s_in[0]), _shift_rows(sp[1], d, s_in[1]))

            def fold(j, gr, gi, acc):
                jp = jnp.clip(jnp.where(d == 0, j - 1, j + 1), 0, seglen - 1)
                sr, si = get_s(rows(jp))
                sr = jnp.where(j == jb, edge[0], sr)
                si = jnp.where(j == jb, edge[1], si)
                return acc[0] + (gr * sr + gi * si), acc[1] + (gi * sr - gr * si)

            return fold

        r_mid = jnp.where(d == 0, TA - 1, T)
        s_mid = tuple(jnp.broadcast_to(t, (NSEG, BW)) for t in get_s(pl.ds(r_mid, 1)))
        mid, acc = _seg_scan(get_g, put_g, 0, TSEG, lr, -li, 1 - d, (zero, zero), dlam_fold(0, TSEG, s_mid), (zero, zero))
        _, acc = _seg_scan(get_g, put_g, T, CSEG, lr, -li, 1 - d, mid, dlam_fold(T, CSEG, (zero, zero)), acc)
        dlam_ref[0, 0, :, 0:BW] = acc[0]
        dlam_ref[0, 0, :, BW:2 * BW] = acc[1]

        bm = b_ref[0, 0].astype(BF16)
        db = jnp.zeros((UW, 2 * BW), F32)
        dc = jnp.zeros((UW, 2 * BW), F32)
        for r0 in range(0, TA, RCH):
            rs = slice(r0, r0 + RCH)
            g = g_ref[rs, :].astype(BF16)
            dua_ref[rs, :] += lax.dot_general(g, bm, nt, preferred_element_type=F32)
            db = db + lax.dot_general(up_ref[rs, :].astype(BF16), g, tn, preferred_element_type=F32)
            if r0 < T:
                dc = dc + lax.dot_general(dssm_ref[rs, :].astype(BF16), s_ref[0, 0, rs, :].astype(BF16), tn,
                                          preferred_element_type=F32)
        db_ref[0, 0] = db
        dc_ref[0, 0] = dc

        @pl.when(d == 1)
        def _():
            _deinterleave_rows(dua_ref, nat_ref)
            du_ref[...] = nat_ref[...].astype(du_ref.dtype)

    blk4 = lambda shape: pl.BlockSpec((1, 1) + shape, lambda b, d: (d, b, 0, 0))
    lat = pl.BlockSpec((T, UW), lambda b, d: (0, b))
    vec = pl.BlockSpec((1, UW), lambda b, d: (0, b))
    return pl.pallas_call(
        kern,
        name="s5_bwd",
        grid=(NBLK, 2),
        in_specs=[lat, lat, pl.BlockSpec((TA, UW), lambda b, d: (0, UCOL0 + b)), vec,
                  blk4((TA, 2 * BW)), blk4((8, 2 * BW)), blk4((UW, 2 * BW)), blk4((UW, 2 * BW))],
        out_specs=[pl.BlockSpec((TA, UW), lambda b, d: (0, b)), vec, blk4((8, 2 * BW)), blk4((UW, 2 * BW)), blk4((UW, 2 * BW))],
        out_shape=[jax.ShapeDtypeStruct((TA, SW), BF16), jax.ShapeDtypeStruct((1, SW), F32),
                   jax.ShapeDtypeStruct((2, NBLK, 8, 2 * BW), F32),
                   jax.ShapeDtypeStruct((2, NBLK, UW, 2 * BW), F32), jax.ShapeDtypeStruct((2, NBLK, UW, 2 * BW), F32)],
        scratch_shapes=[pltpu.VMEM((TA, 2 * BW), F32), pltpu.VMEM((TA, UW), F32), pltpu.VMEM((T, UW), F32),
                        pltpu.VMEM((TA, UW), F32), pltpu.VMEM((TA, UW), F32)],
        compiler_params=_cparams(("parallel", "arbitrary")),
    )(d_ge, ssm, proj, dskip, states, lam, bmat, cmat)


TR = 256
TN_WIDE = 1024


def _vjp_rows(f, primals, cots, n_row):
    _, pull = jax.vjp(f, *primals)
    g = pull(cots)
    return list(g[:n_row]), list(g[n_row:])


class _GradDict(dict):
    def __init__(self, on_set=None):
        super().__init__()
        self._on_set = on_set
        self.tokens = {}

    def __setitem__(self, key, value):
        super().__setitem__(key, value)
        if self._on_set is not None:
            self._on_set(self)

    def order(self, key):
        return self.tokens.get(key, self.get(key))

    def finish(self, key, after):
        if self.on_finish is None:
            return ()
        return (self.on_finish(key, after),)

    on_finish = None


def _local_step(x, ctx, tgt, mod_lat, mod_ctx, wb, sp, on_grad=None, on_loss=None, on_finish=None, on_early=None):
    sh1, sc1, g1, sh2, sc2, g2 = [mod_lat[:, i * D:(i + 1) * D] for i in range(6)]
    csh1, csc1 = mod_ctx[:, 0:D], mod_ctx[:, D:2 * D]
    tabs = _rope_tables()
    sink = sp["attn_sink"].reshape(1, NH)
    dskip = sp["ssm_d"].reshape(1, SW)
    lg_mix, lb_mix = sp["ln_mix_g"].reshape(1, D), sp["ln_mix_b"].reshape(1, D)
    lg_mlp, lb_mlp = sp["ln_mlp_g"].reshape(1, D), sp["ln_mlp_b"].reshape(1, D)
    b1, b2 = sp["b_mlp1"].reshape(1, DFF), sp["b_mlp2"].reshape(1, D)
    s5_names = ("ssm_a_re", "ssm_a_im", "ssm_log_dt", "ssm_b_re", "ssm_b_im", "ssm_c_re", "ssm_c_im")
    (lam, bmat, cmat), s5_pull = jax.vjp(_s5_prep, *[sp[n] for n in s5_names])

    def ln_mod2(rv, vv):
        h = _f_ln_mod(rv[0], vv[0], vv[1])
        return [h, h], []

    h_lat, h_lat_t = _rowwise(ln_mod2, [(x, D, 0, 0)], [sc1, sh1], [(D, BF16), (D, BF16, True)], [], nrows=T, tr=TR, name="ln1_lat")
    h_ctx, h_ctx_t = _rowwise(ln_mod2, [(ctx, D, 0, 0)], [csc1, csh1], [(D, BF16), (D, BF16, True)], [], nrows=C, tr=TR,
                              name="ln1_ctx")
    h1 = jnp.concatenate([h_lat, h_ctx], 0)
    h1_t = jnp.concatenate([h_lat_t, h_ctx_t], 1)
    proj = _matmul(h1, wb["w_in"], mode="nn", name="proj", tm=768, tn=TN_WIDE)
    attn = _attn_fwd(proj, sink, tabs)
    states, ssm, ge = _s5_fwd(proj, dskip, lam, bmat, cmat)
    z = _matmul(ge, wb["w_glu"], mode="nn", name="glu_mm", tm=1024, tn=1024, out_dtypes=(BF16,))

    def glu_act(rv, vv):
        return [_f_glu(rv[0])], []

    glu, = _rowwise(glu_act, [(z, 2 * SW, 0, 0)], [], [(SW, BF16)], [], nrows=T, tr=TR, name="glu_act")
    attn_d = _matmul(attn, wb["w_attn_up"], mode="nn", name="attn_up", tm=1024, tn=512, out_dtypes=(BF16,))
    ssm_d = _matmul(glu, wb["w_ssm_up"], mode="nn", name="ssm_up", tm=1024, tn=512, out_dtypes=(BF16,))
    ga_cb, gs_cb = (QW + 2 * KVW + SW) // D, (QW + 2 * KVW + SW) // D + 1

    def mix(rv, vv):
        m_ = _f_mix(*rv)
        return [m_, m_], []

    mixv, mix_t = _rowwise(mix, [(proj, D, ga_cb, 0), (proj, D, gs_cb, 0), (attn_d, D, 0, 0), (ssm_d, D, 0, 0)], [],
                           [(D, BF16), (D, BF16, True)], [], nrows=T, tr=TR, name="mix")
    y = _matmul(mixv, wb["w_out"], mode="nn", name="out_proj", tm=1024, tn=TN_WIDE, out_dtypes=(BF16,))

    def post1(rv, vv):
        x1, h2 = _f_post1(rv[0], rv[1], *vv)
        return [x1, h2, h2], []

    x1, h2, h2_t = _rowwise(post1, [(x, D, 0, 0), (y, D, 0, 0)], [g1, lg_mix, lb_mix, sc2, sh2],
                            [(D, F32), (D, BF16), (D, BF16, True)], [], nrows=T, tr=TR, name="post1")

    def relu_sq(acc):
        r = jnp.maximum(acc, 0.0)
        return r, r * r, r * r

    r_act, act, act_t = _matmul(h2, wb["w_mlp1"], mode="nn", name="mlp1", tm=1024, tn=TN_WIDE, bias=b1,
                                out_dtypes=(BF16, BF16, BF16), out_t=(False, False, True), epilogue=relu_sq)
    mlp = _matmul(act, wb["w_mlp2"], mode="nn", name="mlp2", tm=512, tn=512, out_dtypes=(BF16,))

    def loss_fb(rv, vv):
        x1_t, mlp_t, tgt_t = rv
        g2_v, lg_v, lb_v, b2_v = vv
        f = lambda a, m, g, p, q, b: _f_loss(a, m, tgt_t, g, p, q, b)
        val, grads = jax.value_and_grad(f, argnums=(0, 1, 2, 3, 4, 5))(x1_t, mlp_t, g2_v, lg_v, lb_v, b2_v)
        dx1, dmlp, dg2, dlg, dlb, db2 = grads
        return [dx1, dmlp], [jnp.reshape(val, (1, 1)), dg2, dlg, dlb, db2]

    dx1_a, d_mlp, loss_p, d_g2, d_lg_mlp, d_lb_mlp, d_b2 = _rowwise(
        loss_fb, [(x1, D, 0, 0), (mlp, D, 0, 0), (tgt, D, 0, 0)], [g2, lg_mlp, lb_mlp, b2],
        [(D, BF16), (D, BF16)], [(1, 1), (1, D), (1, D), (1, D), (1, D)], nrows=T, tr=TR, name="loss_fb")

    gw = _GradDict(on_grad)
    gw.on_finish = on_finish
    loss_done = () if on_loss is None else (on_loss(loss_p),)
    gw["w_mlp2"] = _matmul(act_t, d_mlp, mode="nn", name="dw_mlp2", out_dtypes=(BF16,), tm=1024, tn=TN_WIDE, after=loss_done)
    da, d_b1_parts = _matmul(d_mlp, wb["w_mlp2"], mode="nt", name="d_act", out_dtypes=(BF16,), tm=1024, tn=TN_WIDE,
                             extras=(r_act,), epilogue=lambda acc, r: (acc * (2.0 * r.astype(F32)),),
                             after=(gw.order("w_mlp2"),), colsum=True)
    pin = gw.finish("w_mlp2", da)
    d_b1 = d_b1_parts[0] + d_b1_parts[1]
    gw["w_mlp1"] = _matmul(h2_t, da, mode="nn", name="dw_mlp1", out_dtypes=(BF16,), tm=1024, tn=TN_WIDE, after=pin)
    dh2 = _matmul(da, wb["w_mlp1"], mode="nt", name="d_h2", tm=512, tn=512, out_dtypes=(BF16,), after=(gw.order("w_mlp1"),))

    def post1_b(rv, vv):
        x_t, y_t, dx1_t, dh2_t = rv
        gr, gv = _vjp_rows(_f_post1, (x_t, y_t, *vv), (dx1_t, dh2_t), 2)
        return [gr[0], gr[1]], gv

    dx_a, dy, d_g1, d_lg_mix, d_lb_mix, d_sc2, d_sh2 = _rowwise(
        post1_b, [(x, D, 0, 0), (y, D, 0, 0), (dx1_a, D, 0, 0), (dh2, D, 0, 0)], [g1, lg_mix, lb_mix, sc2, sh2],
        [(D, BF16), (D, BF16)], [(1, D)] * 5, nrows=T, tr=TR, name="post1_bwd")
    gw["w_out"] = _matmul(mix_t, dy, mode="nn", name="dw_out", out_dtypes=(BF16,), tm=1024, tn=TN_WIDE)
    dmix = _matmul(dy, wb["w_out"], mode="nt", name="d_mix", tm=1024, tn=TN_WIDE, out_dtypes=(BF16,), after=(gw.order("w_out"),))

    def mix_b(rv, vv):
        gr, _ = _vjp_rows(_f_mix, tuple(rv[:4]), rv[4], 4)
        return gr, []

    d_ga, d_gs, d_attn_d, d_ssm_d = _rowwise(
        mix_b, [(proj, D, ga_cb, 0), (proj, D, gs_cb, 0), (attn_d, D, 0, 0), (ssm_d, D, 0, 0), (dmix, D, 0, 0)], [],
        [(D, BF16)] * 4, [], nrows=T, tr=TR, name="mix_bwd")
    pin = gw.finish("w_mlp1", d_ga)
    gw["w_attn_up"] = _matmul(attn, d_attn_d, mode="tn", name="dw_attn_up", out_dtypes=(BF16,), tm=512, tn=1024, tk=1024, after=pin)
    d_attn = _matmul(d_attn_d, wb["w_attn_up"], mode="nt", name="d_attn", out_dtypes=(BF16,), tm=1024, tn=512)
    gw["w_ssm_up"] = _matmul(glu, d_ssm_d, mode="tn", name="dw_ssm_up", out_dtypes=(BF16,), tm=512, tn=1024, tk=1024)
    d_glu = _matmul(d_ssm_d, wb["w_ssm_up"], mode="nt", name="d_glu", tm=1024, tn=512, after=(gw.order("w_attn_up"), gw.order("w_ssm_up")))

    def glu_b(rv, vv):
        gr, _ = _vjp_rows(_f_glu, (rv[0],), rv[1], 1)
        return gr, []

    dz, = _rowwise(glu_b, [(z, 2 * SW, 0, 0), (d_glu, SW, 0, 0)], [], [(2 * SW, BF16)], [], nrows=T, tr=TR, name="glu_bwd")
    gw["w_glu"] = _matmul(ge, dz, mode="tn", name="dw_glu", out_dtypes=(BF16,), tm=512, tn=1024, tk=1024)
    d_ge = _matmul(dz, wb["w_glu"], mode="nt", name="d_ge", tm=1024, tn=512, after=(gw.order("w_glu"),))

    du_all, d_dskip, dlam, dbmat, dcmat = _s5_bwd(d_ge, ssm, proj, dskip, states, lam, bmat, cmat)
    s5_grads = s5_pull((dlam, dbmat, dcmat))
    early = dict(zip(s5_names, s5_grads), ssm_d=d_dskip)
    if on_early is not None:
        on_early(early)
    pin = gw.finish("w_glu", du_all)

    dq, dk, dv, dsink = _attn_bwd(proj, d_attn, sink, tabs)
    zc = lambda w: jnp.zeros((C, w), BF16)
    dproj = jnp.concatenate([
        jnp.concatenate([dq, zc(QW)], 0), dk, dv, du_all,
        jnp.concatenate([d_ga, zc(D)], 0), jnp.concatenate([d_gs, zc(D)], 0)], 1)
    gw["w_in"] = _matmul(h1_t, dproj, mode="nn", name="dw_in", out_dtypes=(BF16,), tm=1024, tn=TN_WIDE, after=pin)
    pin = gw.finish("w_in", gw["w_in"])
    dh1 = _matmul(dproj, wb["w_in"], mode="nt", name="d_h1", tm=768, tn=512, out_dtypes=(BF16,), after=pin)

    def ln1_b(rv, vv):
        x_t, dh_t, dxa_t = rv
        gr, gv = _vjp_rows(_f_ln_mod, (x_t, vv[0], vv[1]), dh_t, 1)
        return [gr[0] + dxa_t], gv

    grad_x, d_sc1, d_sh1 = _rowwise(ln1_b, [(x, D, 0, 0), (dh1, D, 0, 0), (dx_a, D, 0, 0)], [sc1, sh1],
                                    [(D, F32)], [(1, D), (1, D)], nrows=T, tr=TR, name="ln1_lat_bwd")

    def ln1c_b(rv, vv):
        _, gv = _vjp_rows(_f_ln_mod, (rv[0], vv[0], vv[1]), rv[1], 1)
        return [], gv

    d_csc1, d_csh1 = _rowwise(ln1c_b, [(ctx, D, 0, 0), (dh1, D, 0, T // TR)], [csc1, csh1],
                              [], [(1, D), (1, D)], nrows=C, tr=TR, name="ln1_ctx_bwd")

    d_mod_lat = jnp.concatenate([d_sh1, d_sc1, d_g1, d_sh2, d_sc2, d_g2], 1)
    zv = jnp.zeros((1, D), F32)
    d_mod_ctx = jnp.concatenate([d_csh1, d_csc1, zv, zv, zv, zv], 1)
    gs = {n: g for n, g in zip(s5_names, s5_grads)}
    gs["attn_sink"] = dsink[:, 0]
    gs["ssm_d"] = d_dskip
    gs["ln_mix_g"], gs["ln_mix_b"] = d_lg_mix, d_lb_mix
    gs["ln_mlp_g"], gs["ln_mlp_b"] = d_lg_mlp, d_lb_mlp
    gs["b_mlp1"], gs["b_mlp2"] = d_b1, d_b2
    return loss_p, grad_x, d_mod_lat, d_mod_ctx, gw, gs


def _my_pos():
    return lax.axis_index("x"), lax.axis_index("y"), lax.axis_index("c")


def _flip(p, bit):
    return 1 - p if bit else p


def _peer(pos, k):
    x, y, c = pos
    return (_flip(x, (k >> 2) & 1), _flip(y, (k >> 1) & 1), _flip(c, k & 1))


def _lin(pos):
    return 4 * pos[0] + 2 * pos[1] + pos[2]


def _allgather_small(v, name):
    r, w = v.shape

    def body(v_ref, out_ref, send_sems, recv_sems, local_sem):
        me = _my_pos()
        mine = pltpu.make_async_copy(v_ref, out_ref.at[_lin(me)], local_sem)
        mine.start()
        sends = []
        for k in range(1, N_DEV):
            cp = pltpu.make_async_remote_copy(src_ref=v_ref, dst_ref=out_ref.at[_lin(me)], send_sem=send_sems.at[k - 1],
                                              recv_sem=recv_sems.at[k - 1], device_id=_peer(me, k), device_id_type=MESH)
            cp.start()
            sends.append(cp)
        for k in range(1, N_DEV):
            peer = _peer(me, k)
            pltpu.make_async_remote_copy(src_ref=v_ref, dst_ref=out_ref.at[_lin(peer)], send_sem=send_sems.at[k - 1],
                                         recv_sem=recv_sems.at[k - 1], device_id=peer, device_id_type=MESH).wait_recv()
        for cp in sends:
            cp.wait_send()
        mine.wait()

    return pl.pallas_call(
        body,
        name=name,
        out_shape=jax.ShapeDtypeStruct((N_DEV, r, w), v.dtype),
        in_specs=[pl.BlockSpec(memory_space=pltpu.VMEM)],
        out_specs=pl.BlockSpec(memory_space=pltpu.VMEM),
        scratch_shapes=[pltpu.SemaphoreType.DMA((N_DEV - 1,)), pltpu.SemaphoreType.DMA((N_DEV - 1,)), pltpu.SemaphoreType.DMA],
        compiler_params=pltpu.CompilerParams(vmem_limit_bytes=VMEM_LIMIT_BYTES),
    )(v)


def _block_of(ref, kind, idx, n):
    start = pl.multiple_of(idx * n, 128)
    if kind == "col":
        return ref.at[:, pl.ds(start, n)]
    return ref.at[pl.ds(start, n), :]


def _handshake(peers):
    barrier = pltpu.get_barrier_semaphore()
    for peer in peers:
        pl.semaphore_signal(barrier, inc=1, device_id=peer, device_id_type=MESH)
    pl.semaphore_wait(barrier, len(peers))


def _allgather_weights_seq(shards, kinds, name, collective_id):
    nt = len(shards)
    hbm = pltpu.MemorySpace.HBM
    ins = [jax.new_ref(s, memory_space=hbm) for s in shards]
    outs = []
    for s, kind in zip(shards, kinds):
        k, n = s.shape
        shape = (k, n * N_DEV) if kind == "col" else (k * N_DEV, n)
        outs.append(jax.empty_ref(jax.ShapeDtypeStruct(shape, s.dtype), memory_space=hbm))

    @functools.partial(
        pl.kernel, mesh=plsc.ScalarSubcoreMesh(axis_name="seq", num_cores=1), name=name,
        scratch_types=(pltpu.SemaphoreType.DMA((nt, N_DEV - 1)), pltpu.SemaphoreType.DMA((nt, N_DEV - 1)),
                       pltpu.SemaphoreType.DMA((nt,))),
        compiler_params=pltpu.CompilerParams(collective_id=collective_id))
    def launch(send_sems, recv_sems, local_sems):
        x, y, c = _my_pos()
        me, sibling = (x, y, c), (x, y, 1 - c)
        chips = [(1 - x, y), (x, 1 - y), (1 - x, 1 - y)]
        _handshake([sibling] + [(*chip, c) for chip in chips])

        def blk(t, pos):
            n = shards[t].shape[1] if kinds[t] == "col" else shards[t].shape[0]
            return _block_of(outs[t], kinds[t], _lin(pos), n)

        def copy(t, k, block, to, src=None):
            return pltpu.make_async_remote_copy(src_ref=blk(t, block) if src is None else src, dst_ref=blk(t, block),
                                                send_sem=send_sems.at[t, k], recv_sem=recv_sems.at[t, k],
                                                device_id=to, device_id_type=MESH)

        local, sends = [], []
        for t in range(nt):
            mine = pltpu.make_async_copy(ins[t], blk(t, me), local_sems.at[t])
            mine.start()
            local.append(mine)
            first = [copy(t, 0, me, sibling, src=ins[t])]
            first += [copy(t, 1 + j, me, (*chip, c), src=ins[t]) for j, chip in enumerate(chips)]
            for cp in first:
                cp.start()
            sends += first
        for t in range(nt):
            for j, chip in enumerate(chips):
                copy(t, 1 + j, (*chip, c), me).wait_recv()
                fwd = copy(t, 4 + j, (*chip, c), sibling)
                fwd.start()
                sends.append(fwd)
        for t in range(nt):
            copy(t, 0, sibling, me).wait_recv()
            for j, chip in enumerate(chips):
                copy(t, 4 + j, (*chip, 1 - c), me).wait_recv()
        for cp in sends:
            cp.wait_send()
        for cp in local:
            cp.wait()

    launch()
    return [o[...] for o in outs]


def _allgather_small_seq(v, name, collective_id):
    hbm = pltpu.MemorySpace.HBM
    src = jax.new_ref(v, memory_space=hbm)
    out = jax.empty_ref(jax.ShapeDtypeStruct((N_DEV,) + v.shape, v.dtype), memory_space=hbm)

    @functools.partial(
        pl.kernel, mesh=plsc.ScalarSubcoreMesh(axis_name="seq", num_cores=1), name=name,
        scratch_types=(pltpu.SemaphoreType.DMA((N_DEV - 1,)), pltpu.SemaphoreType.DMA((N_DEV - 1,)), pltpu.SemaphoreType.DMA),
        compiler_params=pltpu.CompilerParams(collective_id=collective_id))
    def launch(send_sems, recv_sems, local_sem):
        me = _my_pos()
        _handshake([_peer(me, k) for k in range(1, N_DEV)])
        mine = pltpu.make_async_copy(src, out.at[_lin(me)], local_sem)
        mine.start()
        sends = []
        for k in range(1, N_DEV):
            cp = pltpu.make_async_remote_copy(src_ref=src, dst_ref=out.at[_lin(me)], send_sem=send_sems.at[k - 1],
                                              recv_sem=recv_sems.at[k - 1], device_id=_peer(me, k), device_id_type=MESH)
            cp.start()
            sends.append(cp)
        for k in range(1, N_DEV):
            peer = _peer(me, k)
            pltpu.make_async_remote_copy(src_ref=src, dst_ref=out.at[_lin(peer)], send_sem=send_sems.at[k - 1],
                                         recv_sem=recv_sems.at[k - 1], device_id=peer, device_id_type=MESH).wait_recv()
        for cp in sends:
            cp.wait_send()
        mine.wait()

    launch()
    return out[...]


N_CHIP = N_DEV // 2


def _chip_of(pos):
    return 2 * pos[0] + pos[1]


def _pair_exchange_seq(grads, kinds, name, collective_id):
    nt = len(grads)
    hbm = pltpu.MemorySpace.HBM
    shard_shapes = _shard_shapes(grads, kinds)
    ins = [jax.new_ref(g, memory_space=hbm) for g in grads]
    outs = [jax.empty_ref(jax.ShapeDtypeStruct((N_CHIP,) + s, g.dtype), memory_space=hbm) for s, g in zip(shard_shapes, grads)]

    @functools.partial(
        pl.kernel, mesh=plsc.ScalarSubcoreMesh(axis_name="seq", num_cores=1), name=name,
        scratch_types=(pltpu.SemaphoreType.DMA((nt, N_CHIP)), pltpu.SemaphoreType.DMA((nt, N_CHIP))),
        compiler_params=pltpu.CompilerParams(collective_id=collective_id))
    def launch(send_sems, recv_sems):
        x, y, c = _my_pos()
        sibling = (x, y, 1 - c)
        _handshake([sibling])
        copies = []
        for t in range(nt):
            n = shard_shapes[t][1] if kinds[t] == "col" else shard_shapes[t][0]
            for q in range(N_CHIP):
                cp = pltpu.make_async_remote_copy(src_ref=_block_of(ins[t], kinds[t], 2 * q + (1 - c), n), dst_ref=outs[t].at[q],
                                                  send_sem=send_sems.at[t, q], recv_sem=recv_sems.at[t, q],
                                                  device_id=sibling, device_id_type=MESH)
                cp.start()
                copies.append(cp)
        for cp in copies:
            cp.wait_recv()
        for cp in copies:
            cp.wait_send()

    launch()
    return [o[...] for o in outs]


def _pair_add(g, half, kind, name, after=()):
    nq, k, ns = half.shape
    tr = min(k, 512)
    c_idx = lax.axis_index("c").astype(jnp.int32).reshape(1)
    if kind == "col":
        g_spec = pl.BlockSpec((tr, ns), lambda q, i, c_ref: (i, 2 * q + c_ref[0]))
    else:
        g_spec = pl.BlockSpec((tr, ns), lambda q, i, c_ref: ((2 * q + c_ref[0]) * (k // tr) + i, 0))
    n_after = len(after)

    def kern(c_ref, g_ref, h_ref, *rest):
        o_ref = rest[n_after]
        o_ref[0] = (g_ref[...].astype(F32) + h_ref[0].astype(F32)).astype(o_ref.dtype)

    return pl.pallas_call(
        kern,
        name=name,
        grid_spec=pltpu.PrefetchScalarGridSpec(
            num_scalar_prefetch=1,
            grid=(nq, k // tr),
            in_specs=[g_spec, pl.BlockSpec((1, tr, ns), lambda q, i, c_ref: (q, i, 0))] + [pl.BlockSpec(memory_space=pl.ANY)] * n_after,
            out_specs=pl.BlockSpec((1, tr, ns), lambda q, i, c_ref: (q, i, 0)),
        ),
        out_shape=jax.ShapeDtypeStruct(half.shape, half.dtype),
        compiler_params=_cparams(("parallel", "parallel")),
    )(c_idx, g, half, *after)


def _chip_exchange_seq(psums, name, collective_id):
    nt = len(psums)
    hbm = pltpu.MemorySpace.HBM
    ins = [jax.new_ref(s, memory_space=hbm) for s in psums]
    outs = [jax.empty_ref(jax.ShapeDtypeStruct(s.shape, s.dtype), memory_space=hbm) for s in psums]

    @functools.partial(
        pl.kernel, mesh=plsc.ScalarSubcoreMesh(axis_name="seq", num_cores=1), name=name,
        scratch_types=(pltpu.SemaphoreType.DMA((nt, N_CHIP - 1)), pltpu.SemaphoreType.DMA((nt, N_CHIP - 1)),
                       pltpu.SemaphoreType.DMA((nt,))),
        compiler_params=pltpu.CompilerParams(collective_id=collective_id))
    def launch(send_sems, recv_sems, local_sems):
        me = _my_pos()
        peers = [_peer(me, k) for k in (2, 4, 6)]
        _handshake(peers)
        mine = _chip_of(me)
        local, sends = [], []
        for t in range(nt):
            cp = pltpu.make_async_copy(ins[t].at[mine], outs[t].at[mine], local_sems.at[t])
            cp.start()
            local.append(cp)
            for j, peer in enumerate(peers):
                cp = pltpu.make_async_remote_copy(src_ref=ins[t].at[_chip_of(peer)], dst_ref=outs[t].at[mine],
                                                  send_sem=send_sems.at[t, j], recv_sem=recv_sems.at[t, j],
                                                  device_id=peer, device_id_type=MESH)
                cp.start()
                sends.append(cp)
        for t in range(nt):
            for j, peer in enumerate(peers):
                pltpu.make_async_remote_copy(src_ref=ins[t].at[mine], dst_ref=outs[t].at[_chip_of(peer)],
                                             send_sem=send_sems.at[t, j], recv_sem=recv_sems.at[t, j],
                                             device_id=peer, device_id_type=MESH).wait_recv()
        for cp in sends:
            cp.wait_send()
        for cp in local:
            cp.wait()

    launch()
    return [o[...] for o in outs]


def _shard_shapes(grads, kinds):
    return [(g.shape[0], g.shape[1] // N_DEV) if kind == "col" else (g.shape[0] // N_DEV, g.shape[1]) for g, kind in zip(grads, kinds)]


def _adam(g_slots, w, m, v, *, tr, name, after=()):
    ns, r, wd = g_slots.shape
    tr = min(tr, r)
    assert r % tr == 0, (name, r, tr)
    n_after = len(after)

    def kern(g_ref, w_ref, m_ref, v_ref, *rest):
        go_ref, d_ref, mo_ref, vo_ref = rest[n_after:]
        g = g_ref[0].astype(F32)
        for s in range(1, ns):
            g = g + g_ref[s].astype(F32)
        delta, m_new, v_new = _adam_update(g, w_ref[...], m_ref[...], v_ref[...])
        go_ref[...] = g
        d_ref[...] = delta
        mo_ref[...] = m_new
        vo_ref[...] = v_new

    tile = pl.BlockSpec((tr, wd), lambda i: (i, 0))
    return pl.pallas_call(
        kern,
        name=name,
        grid=(r // tr,),
        in_specs=[pl.BlockSpec((ns, tr, wd), lambda i: (0, i, 0)), tile, tile, tile] + [pl.BlockSpec(memory_space=pl.ANY)] * n_after,
        out_specs=[tile] * 4,
        out_shape=[jax.ShapeDtypeStruct((r, wd), F32)] * 4,
        compiler_params=_cparams(("parallel",)),
    )(g_slots, w, m, v, *after)


def _adam_update(g, w, m, v):
    m_new = ADAM_B1 * m + (1.0 - ADAM_B1) * g
    v_new = ADAM_B2 * v + (1.0 - ADAM_B2) * (g * g)
    m_hat = m_new / (1.0 - ADAM_B1 ** ADAM_STEP)
    v_hat = v_new / (1.0 - ADAM_B2 ** ADAM_STEP)
    return -ADAM_LR * (m_hat / (jnp.sqrt(v_hat) + ADAM_EPS) + ADAM_WD * w), m_new, v_new


def _lane_offsets(sizes):
    offs, o = [], 0
    for n in sizes:
        offs.append(o)
        o += -(-n // LANES) * LANES
    return offs, o


def _pack_lanes(parts):
    cols = []
    for p_ in parts:
        flat = p_.reshape(1, -1).astype(F32)
        cols.append(jnp.pad(flat, ((0, 0), (0, (-flat.shape[1]) % LANES))))
    return jnp.concatenate(cols, 1)


def _adam_lanes(g_slots, ws, ms, vs, *, name, after=()):
    ns = g_slots.shape[0]
    npar, n_after = len(ws), len(after)
    sizes = [w.shape[1] for w in ws]
    offs, _ = _lane_offsets(sizes)

    def kern(g_ref, *refs):
        w_refs, m_refs, v_refs = refs[:npar], refs[npar:2 * npar], refs[2 * npar:3 * npar]
        outs = refs[3 * npar + n_after:]
        g_all = g_ref[0]
        for s in range(1, ns):
            g_all = g_all + g_ref[s]
        for j in range(npar):
            g = g_all[:, offs[j]:offs[j] + sizes[j]]
            delta, m_new, v_new = _adam_update(g, w_refs[j][...], m_refs[j][...], v_refs[j][...])
            outs[4 * j][...] = g
            outs[4 * j + 1][...] = delta
            outs[4 * j + 2][...] = m_new
            outs[4 * j + 3][...] = v_new

    vmem = pl.BlockSpec(memory_space=pltpu.VMEM)
    res = pl.pallas_call(
        kern,
        name=name,
        in_specs=[vmem] * (1 + 3 * npar) + [pl.BlockSpec(memory_space=pl.ANY)] * n_after,
        out_specs=[vmem] * (4 * npar),
        out_shape=[jax.ShapeDtypeStruct((1, n), F32) for n in sizes for _ in range(4)],
        compiler_params=pltpu.CompilerParams(vmem_limit_bytes=VMEM_LIMIT_BYTES),
    )(g_slots, *ws, *ms, *vs, *after)
    return [tuple(res[4 * j:4 * j + 4]) for j in range(npar)]


SMALL = ("c_ctx", "b_ada", "attn_sink", "ssm_a_re", "ssm_a_im", "ssm_log_dt", "ssm_b_re", "ssm_b_im", "ssm_c_re", "ssm_c_im",
         "ssm_d", "ln_mix_g", "ln_mix_b", "b_mlp1", "b_mlp2", "ln_mlp_g", "ln_mlp_b")
BIG = ("w_in", "w_glu", "w_attn_up", "w_ssm_up", "w_out", "w_mlp1", "w_mlp2")
BIG_KIND = ("col", "col", "col", "col", "row", "col", "row")
AG_GROUPS = (("w_in",), ("w_glu", "w_attn_up", "w_ssm_up", "w_out"), ("w_mlp1",), ("w_mlp2",))
AG_COLLECTIVE_ID0 = 1
RS_GROUPS = (("w_mlp2",), ("w_mlp1",), ("w_out", "w_attn_up", "w_ssm_up", "w_glu"), ("w_in",))
RS_COLLECTIVE_ID0 = AG_COLLECTIVE_ID0 + len(AG_GROUPS)
SMALL_EARLY = ("ssm_a_re", "ssm_a_im", "ssm_log_dt", "ssm_b_re", "ssm_b_im", "ssm_c_re", "ssm_c_im", "ssm_d")
SMALL_LATE = tuple(n for n in SMALL if n not in SMALL_EARLY)
SMALL_COLLECTIVE_ID0 = RS_COLLECTIVE_ID0 + 2 * len(RS_GROUPS)
LANES = 128


def _pack(parts):
    rows = []
    for p in parts:
        flat = p.reshape(-1).astype(F32)
        pad = (-flat.shape[0]) % LANES
        rows.append(jnp.pad(flat, (0, pad)).reshape(-1, LANES))
    packed = jnp.concatenate(rows, 0)
    return jnp.pad(packed, ((0, (-packed.shape[0]) % 8), (0, 0)))


def _unpack(packed, shapes):
    out, r0 = [], 0
    for s in shapes:
        n = math.prod(s)
        nr = -(-n // LANES)
        out.append(packed[r0:r0 + nr].reshape(-1)[:n].reshape(s))
        r0 += nr
    return out


WEIGHTS = ("c_ctx", "w_ada", "b_ada", "w_in", "attn_sink", "ssm_a_re", "ssm_a_im", "ssm_log_dt", "ssm_b_re", "ssm_b_im",
           "ssm_c_re", "ssm_c_im", "ssm_d", "w_glu", "w_attn_up", "w_ssm_up", "w_out", "ln_mix_g", "ln_mix_b", "w_mlp1",
           "b_mlp1", "w_mlp2", "b_mlp2", "ln_mlp_g", "ln_mlp_b")
ADA_COLS = 6 * D // N_DEV


def _step(x, c, ctx, loss_target, p, m, v):
    me = _lin(_my_pos())
    x2, ctx2, tgt2 = x[0], ctx[0], loss_target[0]

    wb = {}
    for gi, group in enumerate(AG_GROUPS):
        full = _allgather_weights_seq([p[n][0].astype(BF16) for n in group], [BIG_KIND[BIG.index(n)] for n in group],
                                      "allgather_seq%d" % gi, AG_COLLECTIVE_ID0 + gi)
        wb.update(zip(group, full))

    c_all = _allgather_small(jnp.broadcast_to(c, (8, D)), "gather_c")[:, 0, :]
    cc = p["c_ctx"].reshape(1, D)
    s_in = jnp.concatenate([c_all, cc, jnp.zeros((7, D), F32)], 0)
    s_act, = _rowwise(lambda rv, vv: ([_silu(rv[0])], []), [(s_in, D, 0, 0)], [], [(D, F32)], [], nrows=16, tr=16, name="silu_c")
    b_mine = lax.dynamic_slice_in_dim(p["b_ada"], me * ADA_COLS, ADA_COLS, axis=1)
    mod_part = _matmul(s_act, p["w_ada"][0], mode="nn", name="ada_fwd", tm=16, tn=512, bias=b_mine)
    mod_all = _allgather_small(mod_part, "gather_mod")
    mod_lat = lax.dynamic_index_in_dim(mod_all, me, axis=1, keepdims=False).reshape(1, 6 * D)
    mod_ctx = mod_all[:, 8, :].reshape(1, 6 * D)

    sp = {n: p[n][0] for n in SMALL if n not in ("c_ctx", "b_ada")}
    recv, halves = {}, {}

    def on_grad(gw):
        for gi, group in enumerate(RS_GROUPS):
            if gi not in halves and all(n in gw for n in group):
                kinds = [BIG_KIND[BIG.index(n)] for n in group]
                halves[gi] = (dict(gw), _pair_exchange_seq([gw[n] for n in group], kinds, "pair_exchange%d" % gi, RS_COLLECTIVE_ID0 + 2 * gi))

    def on_finish(key, after):
        gi = [i for i, group in enumerate(RS_GROUPS) if key in group][0]
        group = RS_GROUPS[gi]
        grads, half = halves[gi]
        prev = tuple(recv[n] for n in RS_GROUPS[gi - 1][:1]) if gi else ()
        if gi == len(RS_GROUPS) - 1:
            prev += (small["early"],)
        psums =[_pair_add(grads[n], h, BIG_KIND[BIG.index(n)], "pair_add_" + n, after=(after,) + prev) for n, h in zip(group, half)]
        recv.update(zip(group, _chip_exchange_seq(psums, "chip_exchange%d" % gi, RS_COLLECTIVE_ID0 + 2 * gi + 1)))
        return psums[-1]

    small = {}

    def on_early(gs_early):
        small["early"] = _allgather_small_seq(_pack([gs_early[n] for n in SMALL_EARLY]), "gather_small_early", SMALL_COLLECTIVE_ID0)

    total = {}

    def on_loss(loss_p):
        total["loss"] = lax.psum(loss_p[0, 0], ("x", "y", "c"))
        return total["loss"].reshape(1, 1)

    loss_p, grad_x, d_mod_lat, d_mod_ctx, gw, gs = _local_step(x2, ctx2, tgt2, mod_lat, mod_ctx, wb, sp, on_grad, on_loss, on_finish, on_early)

    g_early = small["early"]
    res = {}
    last = ()

    def adam_small(names, g_pack, tag, after):
        sm = _adam(g_pack, _pack([p[n] for n in names]), _pack([m[n] for n in names]), _pack([v[n] for n in names]),
                   tr=g_pack.shape[1], name="adam_small_" + tag, after=after)
        shapes = [p[n].shape for n in names]
        for j, outs in enumerate(zip(*[_unpack(a, shapes) for a in sm])):
            res[names[j]] = outs
        return (sm[0],)

    for gi, group in enumerate(RS_GROUPS):
        if gi == len(RS_GROUPS) - 1:
            last = adam_small(SMALL_EARLY, g_early, "early", last)
        for n in group:
            res[n] = _adam(recv[n], p[n][0], m[n][0], v[n][0], tr=256, name="adam_" + n, after=last)
            last = (res[n][0],)

    dm = jnp.concatenate([d_mod_lat, d_mod_ctx, jnp.zeros((6, 6 * D), F32)], 0)
    dm_all = _allgather_small_seq(dm, "gather_dmod", SMALL_COLLECTIVE_ID0 + 1)
    dm_all = lax.optimization_barrier((dm_all,) + last)[0]
    dm2 = jnp.concatenate([dm_all[:, 0, :], dm_all[:, 1, :]], 0)
    dm2_mine = lax.dynamic_slice_in_dim(dm2, me * ADA_COLS, ADA_COLS, axis=1)
    s2 = jnp.concatenate([s_act[0:8], jnp.broadcast_to(s_act[8:9], (8, D))], 0)
    g_w_ada = _matmul(s2, dm2_mine, mode="tn", name="dw_ada", tm=512, tn=ADA_COLS, after=last)
    dsc_part = _matmul(dm2_mine[8:16], p["w_ada"][0], mode="nt", name="d_silu_cctx", tm=8, tn=512, after=last)

    def cctx_b(rv, vv):
        _, pull = jax.vjp(_silu, vv[0])
        return [], [pull(jnp.sum(rv[0], axis=0, keepdims=True))[0]]

    g_cctx, = _rowwise(cctx_b, [(dsc_part, D, 0, 0)], [cc], [], [(1, D)], nrows=8, tr=8, name="cctx_bwd")
    gs["c_ctx"] = g_cctx
    gs["b_ada"] = d_mod_lat + d_mod_ctx

    res["w_ada"] = _adam(g_w_ada[None], p["w_ada"][0], m["w_ada"][0], v["w_ada"][0], tr=256, name="adam_w_ada")

    g_late = _allgather_small_seq(_pack_lanes([gs[n] for n in SMALL_LATE]), "gather_small_late", SMALL_COLLECTIVE_ID0 + 2)
    row = lambda a: a.reshape(1, -1)
    late = _adam_lanes(g_late, [row(p[n]) for n in SMALL_LATE], [row(m[n]) for n in SMALL_LATE], [row(v[n]) for n in SMALL_LATE],
                       name="adam_small_late", after=(res["w_ada"][0],))
    res.update(zip(SMALL_LATE, late))

    outs = [total["loss"], grad_x[None]]
    for j in range(4):
        outs += [res[n][j].reshape(p[n].shape) for n in WEIGHTS]
    return tuple(outs)


def kernel(x, c, ctx, c_ctx, w_ada, b_ada, w_in, attn_sink, ssm_a_re, ssm_a_im, ssm_log_dt, ssm_b_re, ssm_b_im, ssm_c_re, ssm_c_im, ssm_d, w_glu, w_attn_up, w_ssm_up, w_out, ln_mix_g, ln_mix_b, w_mlp1, b_mlp1, w_mlp2, b_mlp2, ln_mlp_g, ln_mlp_b, loss_target, m_c_ctx, m_w_ada, m_b_ada, m_w_in, m_attn_sink, m_ssm_a_re, m_ssm_a_im, m_ssm_log_dt, m_ssm_b_re, m_ssm_b_im, m_ssm_c_re, m_ssm_c_im, m_ssm_d, m_w_glu, m_w_attn_up, m_w_ssm_up, m_w_out, m_ln_mix_g, m_ln_mix_b, m_w_mlp1, m_b_mlp1, m_w_mlp2, m_b_mlp2, m_ln_mlp_g, m_ln_mlp_b, v_c_ctx, v_w_ada, v_b_ada, v_w_in, v_attn_sink, v_ssm_a_re, v_ssm_a_im, v_ssm_log_dt, v_ssm_b_re, v_ssm_b_im, v_ssm_c_re, v_ssm_c_im, v_ssm_d, v_w_glu, v_w_attn_up, v_w_ssm_up, v_w_out, v_ln_mix_g, v_ln_mix_b, v_w_mlp1, v_b_mlp1, v_w_mlp2, v_b_mlp2, v_ln_mlp_g, v_ln_mlp_b):
    given = dict(locals())
    p = {n: given[n] for n in WEIGHTS}
    m = {n: given["m_" + n] for n in WEIGHTS}
    v = {n: given["v_" + n] for n in WEIGHTS}
    return _step(x, c, ctx, loss_target, p, m, v)
```
